```python
import math
import jax, jax.numpy as jnp
from jax import lax
import numpy as np

D_MODEL = 1024
BATCH = 32
SEQ = 2048
DEPTH = 1

RWKV_WIDTH = 512
RWKV_HEAD = 64
RWKV_HEADS = RWKV_WIDTH // RWKV_HEAD
LORA_W = 64
LORA_A = 64
LORA_G = 128
S5_WIDTH = 512
S5_GROUP = 16
S5_GROUPS = S5_WIDTH // S5_GROUP
S5_STATE = 64
N_BRANCH = 2
D_FF = 2816
CONV_W = 3
RMS_EPS = 1e-6
GN_EPS = 64e-5
L2_EPS = 1e-12
DT_MIN = 1e-3
DT_MAX = 1e-1
SHIFT_COLS = 3 * RWKV_WIDTH + LORA_W + LORA_A + LORA_G
IN_COLS = SHIFT_COLS + S5_WIDTH + N_BRANCH * D_MODEL

kernel_name = "hybrid_rwkv7_s5_convffn_adaln"


def _rms_norm(x, g):
    xf = x.astype(jnp.float32)
    y = xf * lax.rsqrt(jnp.mean(xf * xf, axis=-1, keepdims=True) + RMS_EPS)
    return (y * g.astype(jnp.float32)).astype(x.dtype)


def _token_shift(p, mu):
    prev = jnp.pad(p, ((0, 0), (1, 0), (0, 0)))[:, :-1]
    return p + (prev - p) * mu


def _causal_dwconv(u, w, b):
    up = jnp.pad(u, ((0, 0), (CONV_W - 1, 0), (0, 0)))
    s = u.shape[1]
    y = b
    for j in range(CONV_W):
        y = y + w[j] * up[:, j:j + s]
    return y


def _wkv7(r, decay, k, v, a, b):
    bsz, _, h, n = r.shape

    def step(state, inp):
        r_t, w_t, k_t, v_t, a_t, b_t = inp
        sa = jnp.einsum('bhvk,bhk->bhv', state, a_t)
        state = (state * w_t[:, :, None, :] + sa[..., None] * b_t[:, :, None, :]
                 + v_t[..., None] * k_t[:, :, None, :])
        y_t = jnp.einsum('bhvk,bhk->bhv', state, r_t)
        return state, y_t

    xs = (jnp.moveaxis(r, 1, 0), jnp.moveaxis(decay, 1, 0), jnp.moveaxis(k, 1, 0),
          jnp.moveaxis(v, 1, 0), jnp.moveaxis(a, 1, 0), jnp.moveaxis(b, 1, 0))
    state0 = jnp.zeros((bsz, h, n, n), r.dtype)
    _, ys = lax.scan(step, state0, xs)
    return jnp.moveaxis(ys, 0, 1)


def _rwkv7_branch(p, w0, w_up, a0, a_up, g_up, k_k, k_a, r_k, ln_g, ln_b):
    bsz, s, _ = p.shape
    W, H, N = RWKV_WIDTH, RWKV_HEADS, RWKV_HEAD
    r, k, v, wd, ad, gd = jnp.split(
        p, [W, 2 * W, 3 * W, 3 * W + LORA_W, 3 * W + LORA_W + LORA_A], axis=-1)
    w_raw = w0 + jnp.tanh(wd) @ w_up
    log_w = -jax.nn.softplus(-w_raw) - 0.5
    decay = jnp.exp(-jnp.exp(log_w))
    a = jax.nn.sigmoid(a0 + ad @ a_up)
    g = jax.nn.sigmoid(gd) @ g_up
    heads = lambda t: t.reshape(bsz, s, H, N)
    kk = heads(k * k_k).astype(jnp.float32)
    kk = (kk * lax.rsqrt(jnp.sum(kk * kk, axis=-1, keepdims=True) + L2_EPS)).astype(k.dtype)
    k = k * (1.0 + (a - 1.0) * k_a)
    r_h, k_h, v_h, a_h = heads(r), heads(k), heads(v), heads(a)
    y = _wkv7(r_h, heads(decay), k_h, v_h, -kk, kk * a_h)
    yf = y.astype(jnp.float32)
    mean = jnp.mean(yf, axis=-1, keepdims=True)
    var = jnp.mean(jnp.square(yf - mean), axis=-1, keepdims=True)
    y = ((yf - mean) * lax.rsqrt(var + GN_EPS)).astype(y.dtype)
    y = y * ln_g.reshape(H, N) + ln_b.reshape(H, N)
    bonus = jnp.sum(r_h * k_h * r_k, axis=-1, keepdims=True) * v_h
    return (y + bonus).reshape(bsz, s, W) * g


def _complex_linear_combine(e_i, e_j):
    a_re_i, a_im_i, b_re_i, b_im_i = e_i
    a_re_j, a_im_j, b_re_j, b_im_j = e_j
    a_re = a_re_j * a_re_i - a_im_j * a_im_i
    a_im = a_re_j * a_im_i + a_im_j * a_re_i
    b_re = a_re_j * b_re_i - a_im_j * b_im_i + b_re_j
    b_im = a_re_j * b_im_i + a_im_j * b_re_i + b_im_j
    return (a_re, a_im, b_re, b_im)


def _s5_branch(u, a_re, a_im, log_dt, b_re, b_im, c_re, c_im, d, w_glu):
    bsz, s, _ = u.shape
    G, P, C = S5_GROUPS, S5_STATE, S5_GROUP
    dt = jnp.exp(log_dt)[:, None]
    z_re, z_im = a_re * dt, a_im * dt
    mag = jnp.exp(z_re)
    ab_re, ab_im = mag * jnp.cos(z_im), mag * jnp.sin(z_im)
    den = a_re * a_re + a_im * a_im
    q_re = ((ab_re - 1.0) * a_re + ab_im * a_im) / den
    q_im = (ab_im * a_re - (ab_re - 1.0) * a_im) / den
    bb_re = q_re[..., None] * b_re - q_im[..., None] * b_im
    bb_im = q_re[..., None] * b_im + q_im[..., None] * b_re
    ug = u.reshape(bsz, s, G, C)
    bu_re = jnp.einsum('bsgc,gpc->bsgp', ug, bb_re)
    bu_im = jnp.einsum('bsgc,gpc->bsgp', ug, bb_im)
    a_seq_re = jnp.broadcast_to(ab_re[None, None], (1, s, G, P))
    a_seq_im = jnp.broadcast_to(ab_im[None, None], (1, s, G, P))
    _, _, x_re, x_im = lax.associative_scan(
        _complex_linear_combine, (a_seq_re, a_seq_im, bu_re, bu_im), axis=1)
    y = jnp.einsum('bsgp,gcp->bsgc', x_re, c_re) - jnp.einsum('bsgp,gcp->bsgc', x_im, c_im)
    y = y.reshape(bsz, s, S5_WIDTH) + d * u
    y = jax.nn.gelu(y)
    glu_a, glu_b = jnp.split(y @ w_glu, 2, axis=-1)
    return glu_a * jax.nn.sigmoid(glu_b)


def _fwd_setup_inputs(seed: int = 0) -> dict:
    key = jax.random.key(seed)
    ks = iter(jax.random.split(key, 40))
    L = DEPTH

    def nrm(shape, scale):
        return scale * jax.random.normal(next(ks), shape, jnp.float32)

    def uni(shape, lo, hi):
        return jax.random.uniform(next(ks), shape, jnp.float32, lo, hi)

    n_idx = jnp.arange(S5_STATE, dtype=jnp.float32)
    return {
        "x": nrm((BATCH, SEQ, D_MODEL), 1.0),
        "c": nrm((BATCH, D_MODEL), 1.0),
        "w_ada": nrm((L, D_MODEL, 6 * D_MODEL), 0.02),
        "b_ada": nrm((L, 6 * D_MODEL), 0.02),
        "norm1_g": 1.0 + nrm((L, D_MODEL), 0.02),
        "w_in": nrm((L, D_MODEL, IN_COLS), D_MODEL ** -0.5),
        "mu_shift": uni((L, SHIFT_COLS), 0.0, 1.0),
        "rwkv_w0": uni((L, RWKV_WIDTH), -6.0, -0.5),
        "rwkv_w_up": nrm((L, LORA_W, RWKV_WIDTH), LORA_W ** -0.5),
        "rwkv_a0": nrm((L, RWKV_WIDTH), 0.1),
        "rwkv_a_up": nrm((L, LORA_A, RWKV_WIDTH), LORA_A ** -0.5),
        "rwkv_g_up": nrm((L, LORA_G, RWKV_WIDTH), LORA_G ** -0.5),
        "rwkv_k_k": 0.85 + nrm((L, RWKV_WIDTH), 0.02),
        "rwkv_k_a": 1.0 + nrm((L, RWKV_WIDTH), 0.02),
        "rwkv_r_k": nrm((L, RWKV_HEADS, RWKV_HEAD), 0.1),
        "rwkv_ln_g": 1.0 + nrm((L, RWKV_WIDTH), 0.02),
        "rwkv_ln_b": nrm((L, RWKV_WIDTH), 0.02),
        "w_out_rwkv": nrm((L, RWKV_WIDTH, D_MODEL), RWKV_WIDTH ** -0.5),
        "s5_a_re": -0.5 + nrm((L, S5_GROUPS, S5_STATE), 0.01),
        "s5_a_im": math.pi * n_idx + nrm((L, S5_GROUPS, S5_STATE), 0.01),
        "s5_log_dt": uni((L, S5_GROUPS), math.log(DT_MIN), math.log(DT_MAX)),
        "s5_b_re": nrm((L, S5_GROUPS, S5_STATE, S5_GROUP), (2 * S5_GROUP) ** -0.5),
        "s5_b_im": nrm((L, S5_GROUPS, S5_STATE, S5_GROUP), (2 * S5_GROUP) ** -0.5),
        "s5_c_re": nrm((L, S5_GROUPS, S5_GROUP, S5_STATE), (2 * S5_STATE) ** -0.5),
        "s5_c_im": nrm((L, S5_GROUPS, S5_GROUP, S5_STATE), (2 * S5_STATE) ** -0.5),
        "s5_d": nrm((L, S5_WIDTH), 1.0),
        "w_glu": nrm((L, S5_WIDTH, 2 * D_MODEL), S5_WIDTH ** -0.5),
        "w_out": nrm((L, D_MODEL, D_MODEL), D_MODEL ** -0.5),
        "norm2_g": 1.0 + nrm((L, D_MODEL), 0.02),
        "w_ffn_up": nrm((L, D_MODEL, 2 * D_FF), D_MODEL ** -0.5),
        "ffn_conv_w": nrm((L, CONV_W, 2 * D_FF), CONV_W ** -0.5),
        "ffn_conv_b": nrm((L, 2 * D_FF), 0.02),
        "w_ffn_down": nrm((L, D_FF, D_MODEL), D_FF ** -0.5),
        "norm_f_g": 1.0 + nrm((D_MODEL,), 0.02),
    }


def _fwd_reference(x, c, w_ada, b_ada, norm1_g, w_in, mu_shift, rwkv_w0, rwkv_w_up, rwkv_a0,
              rwkv_a_up, rwkv_g_up, rwkv_k_k, rwkv_k_a, rwkv_r_k, rwkv_ln_g, rwkv_ln_b,
              w_out_rwkv, s5_a_re, s5_a_im, s5_log_dt, s5_b_re, s5_b_im, s5_c_re, s5_c_im,
              s5_d, w_glu, w_out, norm2_g, w_ffn_up, ffn_conv_w, ffn_conv_b, w_ffn_down,
              norm_f_g):
    for l in range(DEPTH):
        mod = (jax.nn.silu(c) @ w_ada[l] + b_ada[l])[:, None, :]
        sh1, sc1, gt1, sh2, sc2, gt2 = jnp.split(mod, 6, axis=-1)

        h = _rms_norm(x, norm1_g[l]) * (1.0 + sc1) + sh1
        proj = h @ w_in[l]
        p_rwkv, u_s5, gates = jnp.split(proj, [SHIFT_COLS, SHIFT_COLS + S5_WIDTH], axis=-1)
        p_rwkv = _token_shift(p_rwkv, mu_shift[l])
        y_a = _rwkv7_branch(p_rwkv, rwkv_w0[l], rwkv_w_up[l], rwkv_a0[l], rwkv_a_up[l],
                            rwkv_g_up[l], rwkv_k_k[l], rwkv_k_a[l], rwkv_r_k[l],
                            rwkv_ln_g[l], rwkv_ln_b[l]) @ w_out_rwkv[l]
        y_b = _s5_branch(u_s5, s5_a_re[l], s5_a_im[l], s5_log_dt[l], s5_b_re[l], s5_b_im[l],
                         s5_c_re[l], s5_c_im[l], s5_d[l], w_glu[l])
        g_a, g_b = jnp.split(jax.nn.sigmoid(gates), 2, axis=-1)
        mixed = (g_a * y_a + g_b * y_b) @ w_out[l]
        x = x + gt1 * mixed

        h = _rms_norm(x, norm2_g[l]) * (1.0 + sc2) + sh2
        hid = _causal_dwconv(h @ w_ffn_up[l], ffn_conv_w[l], ffn_conv_b[l])
        gate, up = jnp.split(hid, 2, axis=-1)
        x = x + gt2 * ((jax.nn.silu(gate) * up) @ w_ffn_down[l])
    return _rms_norm(x, norm_f_g)


import jax as _jax
import jax.numpy as _jnp

TWIN_FORMAT = 'train_step'
FWD_PARAMS = ['x', 'c', 'w_ada', 'b_ada', 'norm1_g', 'w_in', 'mu_shift', 'rwkv_w0', 'rwkv_w_up', 'rwkv_a0', 'rwkv_a_up', 'rwkv_g_up', 'rwkv_k_k', 'rwkv_k_a', 'rwkv_r_k', 'rwkv_ln_g', 'rwkv_ln_b', 'w_out_rwkv', 's5_a_re', 's5_a_im', 's5_log_dt', 's5_b_re', 's5_b_im', 's5_c_re', 's5_c_im', 's5_d', 'w_glu', 'w_out', 'norm2_g', 'w_ffn_up', 'ffn_conv_w', 'ffn_conv_b', 'w_ffn_down', 'norm_f_g']
TWIN_WEIGHTS = ['w_ada', 'b_ada', 'norm1_g', 'w_in', 'mu_shift', 'rwkv_w0', 'rwkv_w_up', 'rwkv_a0', 'rwkv_a_up', 'rwkv_g_up', 'rwkv_k_k', 'rwkv_k_a', 'rwkv_r_k', 'rwkv_ln_g', 'rwkv_ln_b', 'w_out_rwkv', 's5_a_re', 's5_a_im', 's5_log_dt', 's5_b_re', 's5_b_im', 's5_c_re', 's5_c_im', 's5_d', 'w_glu', 'w_out', 'norm2_g', 'w_ffn_up', 'ffn_conv_w', 'ffn_conv_b', 'w_ffn_down', 'norm_f_g']
TWIN_DIFF_INPUT = 'x'
TWIN_INPUTS = ['x', 'c', 'w_ada', 'b_ada', 'norm1_g', 'w_in', 'mu_shift', 'rwkv_w0', 'rwkv_w_up', 'rwkv_a0', 'rwkv_a_up', 'rwkv_g_up', 'rwkv_k_k', 'rwkv_k_a', 'rwkv_r_k', 'rwkv_ln_g', 'rwkv_ln_b', 'w_out_rwkv', 's5_a_re', 's5_a_im', 's5_log_dt', 's5_b_re', 's5_b_im', 's5_c_re', 's5_c_im', 's5_d', 'w_glu', 'w_out', 'norm2_g', 'w_ffn_up', 'ffn_conv_w', 'ffn_conv_b', 'w_ffn_down', 'norm_f_g', 'loss_target', 'm_w_ada', 'm_b_ada', 'm_norm1_g', 'm_w_in', 'm_mu_shift', 'm_rwkv_w0', 'm_rwkv_w_up', 'm_rwkv_a0', 'm_rwkv_a_up', 'm_rwkv_g_up', 'm_rwkv_k_k', 'm_rwkv_k_a', 'm_rwkv_r_k', 'm_rwkv_ln_g', 'm_rwkv_ln_b', 'm_w_out_rwkv', 'm_s5_a_re', 'm_s5_a_im', 'm_s5_log_dt', 'm_s5_b_re', 'm_s5_b_im', 'm_s5_c_re', 'm_s5_c_im', 'm_s5_d', 'm_w_glu', 'm_w_out', 'm_norm2_g', 'm_w_ffn_up', 'm_ffn_conv_w', 'm_ffn_conv_b', 'm_w_ffn_down', 'm_norm_f_g', 'v_w_ada', 'v_b_ada', 'v_norm1_g', 'v_w_in', 'v_mu_shift', 'v_rwkv_w0', 'v_rwkv_w_up', 'v_rwkv_a0', 'v_rwkv_a_up', 'v_rwkv_g_up', 'v_rwkv_k_k', 'v_rwkv_k_a', 'v_rwkv_r_k', 'v_rwkv_ln_g', 'v_rwkv_ln_b', 'v_w_out_rwkv', 'v_s5_a_re', 'v_s5_a_im', 'v_s5_log_dt', 'v_s5_b_re', 'v_s5_b_im', 'v_s5_c_re', 'v_s5_c_im', 'v_s5_d', 'v_w_glu', 'v_w_out', 'v_norm2_g', 'v_w_ffn_up', 'v_ffn_conv_w', 'v_ffn_conv_b', 'v_w_ffn_down', 'v_norm_f_g']
TWIN_OUTPUTS = ['loss', 'grad_x', 'grad_w_ada', 'grad_b_ada', 'grad_norm1_g', 'grad_w_in', 'grad_mu_shift', 'grad_rwkv_w0', 'grad_rwkv_w_up', 'grad_rwkv_a0', 'grad_rwkv_a_up', 'grad_rwkv_g_up', 'grad_rwkv_k_k', 'grad_rwkv_k_a', 'grad_rwkv_r_k', 'grad_rwkv_ln_g', 'grad_rwkv_ln_b', 'grad_w_out_rwkv', 'grad_s5_a_re', 'grad_s5_a_im', 'grad_s5_log_dt', 'grad_s5_b_re', 'grad_s5_b_im', 'grad_s5_c_re', 'grad_s5_c_im', 'grad_s5_d', 'grad_w_glu', 'grad_w_out', 'grad_norm2_g', 'grad_w_ffn_up', 'grad_ffn_conv_w', 'grad_ffn_conv_b', 'grad_w_ffn_down', 'grad_norm_f_g', 'delta_w_ada', 'delta_b_ada', 'delta_norm1_g', 'delta_w_in', 'delta_mu_shift', 'delta_rwkv_w0', 'delta_rwkv_w_up', 'delta_rwkv_a0', 'delta_rwkv_a_up', 'delta_rwkv_g_up', 'delta_rwkv_k_k', 'delta_rwkv_k_a', 'delta_rwkv_r_k', 'delta_rwkv_ln_g', 'delta_rwkv_ln_b', 'delta_w_out_rwkv', 'delta_s5_a_re', 'delta_s5_a_im', 'delta_s5_log_dt', 'delta_s5_b_re', 'delta_s5_b_im', 'delta_s5_c_re', 'delta_s5_c_im', 'delta_s5_d', 'delta_w_glu', 'delta_w_out', 'delta_norm2_g', 'delta_w_ffn_up', 'delta_ffn_conv_w', 'delta_ffn_conv_b', 'delta_w_ffn_down', 'delta_norm_f_g', 'new_m_w_ada', 'new_m_b_ada', 'new_m_norm1_g', 'new_m_w_in', 'new_m_mu_shift', 'new_m_rwkv_w0', 'new_m_rwkv_w_up', 'new_m_rwkv_a0', 'new_m_rwkv_a_up', 'new_m_rwkv_g_up', 'new_m_rwkv_k_k', 'new_m_rwkv_k_a', 'new_m_rwkv_r_k', 'new_m_rwkv_ln_g', 'new_m_rwkv_ln_b', 'new_m_w_out_rwkv', 'new_m_s5_a_re', 'new_m_s5_a_im', 'new_m_s5_log_dt', 'new_m_s5_b_re', 'new_m_s5_b_im', 'new_m_s5_c_re', 'new_m_s5_c_im', 'new_m_s5_d', 'new_m_w_glu', 'new_m_w_out', 'new_m_norm2_g', 'new_m_w_ffn_up', 'new_m_ffn_conv_w', 'new_m_ffn_conv_b', 'new_m_w_ffn_down', 'new_m_norm_f_g', 'new_v_w_ada', 'new_v_b_ada', 'new_v_norm1_g', 'new_v_w_in', 'new_v_mu_shift', 'new_v_rwkv_w0', 'new_v_rwkv_w_up', 'new_v_rwkv_a0', 'new_v_rwkv_a_up', 'new_v_rwkv_g_up', 'new_v_rwkv_k_k', 'new_v_rwkv_k_a', 'new_v_rwkv_r_k', 'new_v_rwkv_ln_g', 'new_v_rwkv_ln_b', 'new_v_w_out_rwkv', 'new_v_s5_a_re', 'new_v_s5_a_im', 'new_v_s5_log_dt', 'new_v_s5_b_re', 'new_v_s5_b_im', 'new_v_s5_c_re', 'new_v_s5_c_im', 'new_v_s5_d', 'new_v_w_glu', 'new_v_w_out', 'new_v_norm2_g', 'new_v_w_ffn_up', 'new_v_ffn_conv_w', 'new_v_ffn_conv_b', 'new_v_w_ffn_down', 'new_v_norm_f_g']
TWIN_LEAF_KINDS = {'loss': 'loss', 'grad_x': 'grad_x', 'grad_w_ada': 'grad_w', 'grad_b_ada': 'grad_w', 'grad_norm1_g': 'grad_w', 'grad_w_in': 'grad_w', 'grad_mu_shift': 'grad_w', 'grad_rwkv_w0': 'grad_w', 'grad_rwkv_w_up': 'grad_w', 'grad_rwkv_a0': 'grad_w', 'grad_rwkv_a_up': 'grad_w', 'grad_rwkv_g_up': 'grad_w', 'grad_rwkv_k_k': 'grad_w', 'grad_rwkv_k_a': 'grad_w', 'grad_rwkv_r_k': 'grad_w', 'grad_rwkv_ln_g': 'grad_w', 'grad_rwkv_ln_b': 'grad_w', 'grad_w_out_rwkv': 'grad_w', 'grad_s5_a_re': 'grad_w', 'grad_s5_a_im': 'grad_w', 'grad_s5_log_dt': 'grad_w', 'grad_s5_b_re': 'grad_w', 'grad_s5_b_im': 'grad_w', 'grad_s5_c_re': 'grad_w', 'grad_s5_c_im': 'grad_w', 'grad_s5_d': 'grad_w', 'grad_w_glu': 'grad_w', 'grad_w_out': 'grad_w', 'grad_norm2_g': 'grad_w', 'grad_w_ffn_up': 'grad_w', 'grad_ffn_conv_w': 'grad_w', 'grad_ffn_conv_b': 'grad_w', 'grad_w_ffn_down': 'grad_w', 'grad_norm_f_g': 'grad_w', 'delta_w_ada': 'delta_w', 'delta_b_ada': 'delta_w', 'delta_norm1_g': 'delta_w', 'delta_w_in': 'delta_w', 'delta_mu_shift': 'delta_w', 'delta_rwkv_w0': 'delta_w', 'delta_rwkv_w_up': 'delta_w', 'delta_rwkv_a0': 'delta_w', 'delta_rwkv_a_up': 'delta_w', 'delta_rwkv_g_up': 'delta_w', 'delta_rwkv_k_k': 'delta_w', 'delta_rwkv_k_a': 'delta_w', 'delta_rwkv_r_k': 'delta_w', 'delta_rwkv_ln_g': 'delta_w', 'delta_rwkv_ln_b': 'delta_w', 'delta_w_out_rwkv': 'delta_w', 'delta_s5_a_re': 'delta_w', 'delta_s5_a_im': 'delta_w', 'delta_s5_log_dt': 'delta_w', 'delta_s5_b_re': 'delta_w', 'delta_s5_b_im': 'delta_w', 'delta_s5_c_re': 'delta_w', 'delta_s5_c_im': 'delta_w', 'delta_s5_d': 'delta_w', 'delta_w_glu': 'delta_w', 'delta_w_out': 'delta_w', 'delta_norm2_g': 'delta_w', 'delta_w_ffn_up': 'delta_w', 'delta_ffn_conv_w': 'delta_w', 'delta_ffn_conv_b': 'delta_w', 'delta_w_ffn_down': 'delta_w', 'delta_norm_f_g': 'delta_w', 'new_m_w_ada': 'new_m', 'new_m_b_ada': 'new_m', 'new_m_norm1_g': 'new_m', 'new_m_w_in': 'new_m', 'new_m_mu_shift': 'new_m', 'new_m_rwkv_w0': 'new_m', 'new_m_rwkv_w_up': 'new_m', 'new_m_rwkv_a0': 'new_m', 'new_m_rwkv_a_up': 'new_m', 'new_m_rwkv_g_up': 'new_m', 'new_m_rwkv_k_k': 'new_m', 'new_m_rwkv_k_a': 'new_m', 'new_m_rwkv_r_k': 'new_m', 'new_m_rwkv_ln_g': 'new_m', 'new_m_rwkv_ln_b': 'new_m', 'new_m_w_out_rwkv': 'new_m', 'new_m_s5_a_re': 'new_m', 'new_m_s5_a_im': 'new_m', 'new_m_s5_log_dt': 'new_m', 'new_m_s5_b_re': 'new_m', 'new_m_s5_b_im': 'new_m', 'new_m_s5_c_re': 'new_m', 'new_m_s5_c_im': 'new_m', 'new_m_s5_d': 'new_m', 'new_m_w_glu': 'new_m', 'new_m_w_out': 'new_m', 'new_m_norm2_g': 'new_m', 'new_m_w_ffn_up': 'new_m', 'new_m_ffn_conv_w': 'new_m', 'new_m_ffn_conv_b': 'new_m', 'new_m_w_ffn_down': 'new_m', 'new_m_norm_f_g': 'new_m', 'new_v_w_ada': 'new_v', 'new_v_b_ada': 'new_v', 'new_v_norm1_g': 'new_v', 'new_v_w_in': 'new_v', 'new_v_mu_shift': 'new_v', 'new_v_rwkv_w0': 'new_v', 'new_v_rwkv_w_up': 'new_v', 'new_v_rwkv_a0': 'new_v', 'new_v_rwkv_a_up': 'new_v', 'new_v_rwkv_g_up': 'new_v', 'new_v_rwkv_k_k': 'new_v', 'new_v_rwkv_k_a': 'new_v', 'new_v_rwkv_r_k': 'new_v', 'new_v_rwkv_ln_g': 'new_v', 'new_v_rwkv_ln_b': 'new_v', 'new_v_w_out_rwkv': 'new_v', 'new_v_s5_a_re': 'new_v', 'new_v_s5_a_im': 'new_v', 'new_v_s5_log_dt': 'new_v', 'new_v_s5_b_re': 'new_v', 'new_v_s5_b_im': 'new_v', 'new_v_s5_c_re': 'new_v', 'new_v_s5_c_im': 'new_v', 'new_v_s5_d': 'new_v', 'new_v_w_glu': 'new_v', 'new_v_w_out': 'new_v', 'new_v_norm2_g': 'new_v', 'new_v_w_ffn_up': 'new_v', 'new_v_ffn_conv_w': 'new_v', 'new_v_ffn_conv_b': 'new_v', 'new_v_w_ffn_down': 'new_v', 'new_v_norm_f_g': 'new_v'}


def _forward(args):
    return _fwd_reference(*[args[k] for k in FWD_PARAMS])


def _output_shape():
    out = _jax.eval_shape(lambda: _forward(_fwd_setup_inputs(0)))
    return out.shape, out.dtype

N_MICROBATCH = 1
ADAM_LR = 0.001
ADAM_B1 = 0.9
ADAM_B2 = 0.999
ADAM_EPS = 1e-08
ADAM_WD = 0.01
ADAM_STEP = 10
PER_EXAMPLE_BATCH_AXIS = {'x': 0, 'c': 0, 'loss_target': 0}
SHARED_INPUTS = []
_WEIGHT_DTYPES = {'w_ada': _jnp.float32, 'b_ada': _jnp.float32, 'norm1_g': _jnp.float32, 'w_in': _jnp.float32, 'mu_shift': _jnp.float32, 'rwkv_w0': _jnp.float32, 'rwkv_w_up': _jnp.float32, 'rwkv_a0': _jnp.float32, 'rwkv_a_up': _jnp.float32, 'rwkv_g_up': _jnp.float32, 'rwkv_k_k': _jnp.float32, 'rwkv_k_a': _jnp.float32, 'rwkv_r_k': _jnp.float32, 'rwkv_ln_g': _jnp.float32, 'rwkv_ln_b': _jnp.float32, 'w_out_rwkv': _jnp.float32, 's5_a_re': _jnp.float32, 's5_a_im': _jnp.float32, 's5_log_dt': _jnp.float32, 's5_b_re': _jnp.float32, 's5_b_im': _jnp.float32, 's5_c_re': _jnp.float32, 's5_c_im': _jnp.float32, 's5_d': _jnp.float32, 'w_glu': _jnp.float32, 'w_out': _jnp.float32, 'norm2_g': _jnp.float32, 'w_ffn_up': _jnp.float32, 'ffn_conv_w': _jnp.float32, 'ffn_conv_b': _jnp.float32, 'w_ffn_down': _jnp.float32, 'norm_f_g': _jnp.float32}
MOMENT_SCALE = {'w_ada': 7.446395e-02, 'b_ada': 1.262693e-01, 'norm1_g': 6.496655e-02, 'w_in': 3.307988e-02, 'mu_shift': 6.785928e-02, 'rwkv_w0': 2.363528e-02, 'rwkv_w_up': 5.977944e-03, 'rwkv_a0': 1.669926e-02, 'rwkv_a_up': 1.747397e-02, 'rwkv_g_up': 4.642731e-02, 'rwkv_k_k': 1.876433e-01, 'rwkv_k_a': 1.273225e-01, 'rwkv_r_k': 1.779568e-01, 'rwkv_ln_g': 4.024128e-02, 'rwkv_ln_b': 5.540089e-02, 'w_out_rwkv': 3.234874e-02, 's5_a_re': 2.877400e-03, 's5_a_im': 3.225202e-03, 's5_log_dt': 1.051397e+00, 's5_b_re': 1.612012e-03, 's5_b_im': 1.536128e-03, 's5_c_re': 3.348958e-03, 's5_c_im': 3.790925e-03, 's5_d': 3.621077e-02, 'w_glu': 1.739970e-02, 'w_out': 3.982029e-02, 'norm2_g': 1.050325e-01, 'w_ffn_up': 4.602663e-02, 'ffn_conv_w': 4.627135e-02, 'ffn_conv_b': 3.994134e-02, 'w_ffn_down': 7.496379e-02, 'norm_f_g': 6.415509e+01}


def _to_microbatches(a, axis):
    t = _jnp.moveaxis(a, axis, 0)
    t = t.reshape((N_MICROBATCH, t.shape[0] // N_MICROBATCH) + t.shape[1:])
    return _jnp.moveaxis(t, 1, axis + 1)


def setup_inputs(seed: int = 0) -> dict:
    inp = _fwd_setup_inputs(seed)
    key = _jax.random.fold_in(_jax.random.key(seed), 7919)
    shape, _ = _output_shape()
    out = dict(inp)
    out["loss_target"] = _jax.random.normal(_jax.random.fold_in(key, 0), shape, _jnp.float32)
    for i, name in enumerate(TWIN_WEIGHTS):
        w = inp[name].astype(_jnp.float32)
        if MOMENT_SCALE is None:
            s = _jnp.sqrt(_jnp.mean(_jnp.square(w)) + 1e-30)
        else:
            s = MOMENT_SCALE[name]
        km, kv = _jax.random.split(_jax.random.fold_in(key, i + 1))
        out[name] = w
        out["m_" + name] = s * _jax.random.normal(km, w.shape, _jnp.float32)
        out["v_" + name] = (s * s) * _jax.random.uniform(kv, w.shape, _jnp.float32, 0.5, 1.5)
    if N_MICROBATCH > 1:
        for name, axis in PER_EXAMPLE_BATCH_AXIS.items():
            out[name] = _to_microbatches(out[name], axis)
    return {'x': out['x'], 'c': out['c'], 'w_ada': out['w_ada'], 'b_ada': out['b_ada'], 'norm1_g': out['norm1_g'], 'w_in': out['w_in'], 'mu_shift': out['mu_shift'], 'rwkv_w0': out['rwkv_w0'], 'rwkv_w_up': out['rwkv_w_up'], 'rwkv_a0': out['rwkv_a0'], 'rwkv_a_up': out['rwkv_a_up'], 'rwkv_g_up': out['rwkv_g_up'], 'rwkv_k_k': out['rwkv_k_k'], 'rwkv_k_a': out['rwkv_k_a'], 'rwkv_r_k': out['rwkv_r_k'], 'rwkv_ln_g': out['rwkv_ln_g'], 'rwkv_ln_b': out['rwkv_ln_b'], 'w_out_rwkv': out['w_out_rwkv'], 's5_a_re': out['s5_a_re'], 's5_a_im': out['s5_a_im'], 's5_log_dt': out['s5_log_dt'], 's5_b_re': out['s5_b_re'], 's5_b_im': out['s5_b_im'], 's5_c_re': out['s5_c_re'], 's5_c_im': out['s5_c_im'], 's5_d': out['s5_d'], 'w_glu': out['w_glu'], 'w_out': out['w_out'], 'norm2_g': out['norm2_g'], 'w_ffn_up': out['w_ffn_up'], 'ffn_conv_w': out['ffn_conv_w'], 'ffn_conv_b': out['ffn_conv_b'], 'w_ffn_down': out['w_ffn_down'], 'norm_f_g': out['norm_f_g'], 'loss_target': out['loss_target'], 'm_w_ada': out['m_w_ada'], 'm_b_ada': out['m_b_ada'], 'm_norm1_g': out['m_norm1_g'], 'm_w_in': out['m_w_in'], 'm_mu_shift': out['m_mu_shift'], 'm_rwkv_w0': out['m_rwkv_w0'], 'm_rwkv_w_up': out['m_rwkv_w_up'], 'm_rwkv_a0': out['m_rwkv_a0'], 'm_rwkv_a_up': out['m_rwkv_a_up'], 'm_rwkv_g_up': out['m_rwkv_g_up'], 'm_rwkv_k_k': out['m_rwkv_k_k'], 'm_rwkv_k_a': out['m_rwkv_k_a'], 'm_rwkv_r_k': out['m_rwkv_r_k'], 'm_rwkv_ln_g': out['m_rwkv_ln_g'], 'm_rwkv_ln_b': out['m_rwkv_ln_b'], 'm_w_out_rwkv': out['m_w_out_rwkv'], 'm_s5_a_re': out['m_s5_a_re'], 'm_s5_a_im': out['m_s5_a_im'], 'm_s5_log_dt': out['m_s5_log_dt'], 'm_s5_b_re': out['m_s5_b_re'], 'm_s5_b_im': out['m_s5_b_im'], 'm_s5_c_re': out['m_s5_c_re'], 'm_s5_c_im': out['m_s5_c_im'], 'm_s5_d': out['m_s5_d'], 'm_w_glu': out['m_w_glu'], 'm_w_out': out['m_w_out'], 'm_norm2_g': out['m_norm2_g'], 'm_w_ffn_up': out['m_w_ffn_up'], 'm_ffn_conv_w': out['m_ffn_conv_w'], 'm_ffn_conv_b': out['m_ffn_conv_b'], 'm_w_ffn_down': out['m_w_ffn_down'], 'm_norm_f_g': out['m_norm_f_g'], 'v_w_ada': out['v_w_ada'], 'v_b_ada': out['v_b_ada'], 'v_norm1_g': out['v_norm1_g'], 'v_w_in': out['v_w_in'], 'v_mu_shift': out['v_mu_shift'], 'v_rwkv_w0': out['v_rwkv_w0'], 'v_rwkv_w_up': out['v_rwkv_w_up'], 'v_rwkv_a0': out['v_rwkv_a0'], 'v_rwkv_a_up': out['v_rwkv_a_up'], 'v_rwkv_g_up': out['v_rwkv_g_up'], 'v_rwkv_k_k': out['v_rwkv_k_k'], 'v_rwkv_k_a': out['v_rwkv_k_a'], 'v_rwkv_r_k': out['v_rwkv_r_k'], 'v_rwkv_ln_g': out['v_rwkv_ln_g'], 'v_rwkv_ln_b': out['v_rwkv_ln_b'], 'v_w_out_rwkv': out['v_w_out_rwkv'], 'v_s5_a_re': out['v_s5_a_re'], 'v_s5_a_im': out['v_s5_a_im'], 'v_s5_log_dt': out['v_s5_log_dt'], 'v_s5_b_re': out['v_s5_b_re'], 'v_s5_b_im': out['v_s5_b_im'], 'v_s5_c_re': out['v_s5_c_re'], 'v_s5_c_im': out['v_s5_c_im'], 'v_s5_d': out['v_s5_d'], 'v_w_glu': out['v_w_glu'], 'v_w_out': out['v_w_out'], 'v_norm2_g': out['v_norm2_g'], 'v_w_ffn_up': out['v_w_ffn_up'], 'v_ffn_conv_w': out['v_ffn_conv_w'], 'v_ffn_conv_b': out['v_ffn_conv_b'], 'v_w_ffn_down': out['v_w_ffn_down'], 'v_norm_f_g': out['v_norm_f_g']}


def _loss(weights, diff, rest, loss_target):
    with _jax.named_scope("forward"):
        args = {**rest, TWIN_DIFF_INPUT: diff, **{k: w.astype(_WEIGHT_DTYPES[k]) for k, w in weights.items()}}
        y = _forward(args)
    with _jax.named_scope("loss_head"):
        err = _jnp.square(y.astype(_jnp.float32) - loss_target)
        return 0.5 * _jnp.sum(_jnp.mean(err, axis=-1)) if err.ndim else 0.5 * err


def _adamw(w, g, m, v):
    m = ADAM_B1 * m + (1.0 - ADAM_B1) * g
    v = ADAM_B2 * v + (1.0 - ADAM_B2) * _jnp.square(g)
    m_hat = m / (1.0 - ADAM_B1 ** ADAM_STEP)
    v_hat = v / (1.0 - ADAM_B2 ** ADAM_STEP)
    delta = -ADAM_LR * (m_hat / (_jnp.sqrt(v_hat) + ADAM_EPS) + ADAM_WD * w)
    return delta, m, v


def reference(x, c, w_ada, b_ada, norm1_g, w_in, mu_shift, rwkv_w0, rwkv_w_up, rwkv_a0, rwkv_a_up, rwkv_g_up, rwkv_k_k, rwkv_k_a, rwkv_r_k, rwkv_ln_g, rwkv_ln_b, w_out_rwkv, s5_a_re, s5_a_im, s5_log_dt, s5_b_re, s5_b_im, s5_c_re, s5_c_im, s5_d, w_glu, w_out, norm2_g, w_ffn_up, ffn_conv_w, ffn_conv_b, w_ffn_down, norm_f_g, loss_target, m_w_ada, m_b_ada, m_norm1_g, m_w_in, m_mu_shift, m_rwkv_w0, m_rwkv_w_up, m_rwkv_a0, m_rwkv_a_up, m_rwkv_g_up, m_rwkv_k_k, m_rwkv_k_a, m_rwkv_r_k, m_rwkv_ln_g, m_rwkv_ln_b, m_w_out_rwkv, m_s5_a_re, m_s5_a_im, m_s5_log_dt, m_s5_b_re, m_s5_b_im, m_s5_c_re, m_s5_c_im, m_s5_d, m_w_glu, m_w_out, m_norm2_g, m_w_ffn_up, m_ffn_conv_w, m_ffn_conv_b, m_w_ffn_down, m_norm_f_g, v_w_ada, v_b_ada, v_norm1_g, v_w_in, v_mu_shift, v_rwkv_w0, v_rwkv_w_up, v_rwkv_a0, v_rwkv_a_up, v_rwkv_g_up, v_rwkv_k_k, v_rwkv_k_a, v_rwkv_r_k, v_rwkv_ln_g, v_rwkv_ln_b, v_w_out_rwkv, v_s5_a_re, v_s5_a_im, v_s5_log_dt, v_s5_b_re, v_s5_b_im, v_s5_c_re, v_s5_c_im, v_s5_d, v_w_glu, v_w_out, v_norm2_g, v_w_ffn_up, v_ffn_conv_w, v_ffn_conv_b, v_w_ffn_down, v_norm_f_g):
    given = dict(x=x, c=c, w_ada=w_ada, b_ada=b_ada, norm1_g=norm1_g, w_in=w_in, mu_shift=mu_shift, rwkv_w0=rwkv_w0, rwkv_w_up=rwkv_w_up, rwkv_a0=rwkv_a0, rwkv_a_up=rwkv_a_up, rwkv_g_up=rwkv_g_up, rwkv_k_k=rwkv_k_k, rwkv_k_a=rwkv_k_a, rwkv_r_k=rwkv_r_k, rwkv_ln_g=rwkv_ln_g, rwkv_ln_b=rwkv_ln_b, w_out_rwkv=w_out_rwkv, s5_a_re=s5_a_re, s5_a_im=s5_a_im, s5_log_dt=s5_log_dt, s5_b_re=s5_b_re, s5_b_im=s5_b_im, s5_c_re=s5_c_re, s5_c_im=s5_c_im, s5_d=s5_d, w_glu=w_glu, w_out=w_out, norm2_g=norm2_g, w_ffn_up=w_ffn_up, ffn_conv_w=ffn_conv_w, ffn_conv_b=ffn_conv_b, w_ffn_down=w_ffn_down, norm_f_g=norm_f_g, loss_target=loss_target, m_w_ada=m_w_ada, m_b_ada=m_b_ada, m_norm1_g=m_norm1_g, m_w_in=m_w_in, m_mu_shift=m_mu_shift, m_rwkv_w0=m_rwkv_w0, m_rwkv_w_up=m_rwkv_w_up, m_rwkv_a0=m_rwkv_a0, m_rwkv_a_up=m_rwkv_a_up, m_rwkv_g_up=m_rwkv_g_up, m_rwkv_k_k=m_rwkv_k_k, m_rwkv_k_a=m_rwkv_k_a, m_rwkv_r_k=m_rwkv_r_k, m_rwkv_ln_g=m_rwkv_ln_g, m_rwkv_ln_b=m_rwkv_ln_b, m_w_out_rwkv=m_w_out_rwkv, m_s5_a_re=m_s5_a_re, m_s5_a_im=m_s5_a_im, m_s5_log_dt=m_s5_log_dt, m_s5_b_re=m_s5_b_re, m_s5_b_im=m_s5_b_im, m_s5_c_re=m_s5_c_re, m_s5_c_im=m_s5_c_im, m_s5_d=m_s5_d, m_w_glu=m_w_glu, m_w_out=m_w_out, m_norm2_g=m_norm2_g, m_w_ffn_up=m_w_ffn_up, m_ffn_conv_w=m_ffn_conv_w, m_ffn_conv_b=m_ffn_conv_b, m_w_ffn_down=m_w_ffn_down, m_norm_f_g=m_norm_f_g, v_w_ada=v_w_ada, v_b_ada=v_b_ada, v_norm1_g=v_norm1_g, v_w_in=v_w_in, v_mu_shift=v_mu_shift, v_rwkv_w0=v_rwkv_w0, v_rwkv_w_up=v_rwkv_w_up, v_rwkv_a0=v_rwkv_a0, v_rwkv_a_up=v_rwkv_a_up, v_rwkv_g_up=v_rwkv_g_up, v_rwkv_k_k=v_rwkv_k_k, v_rwkv_k_a=v_rwkv_k_a, v_rwkv_r_k=v_rwkv_r_k, v_rwkv_ln_g=v_rwkv_ln_g, v_rwkv_ln_b=v_rwkv_ln_b, v_w_out_rwkv=v_w_out_rwkv, v_s5_a_re=v_s5_a_re, v_s5_a_im=v_s5_a_im, v_s5_log_dt=v_s5_log_dt, v_s5_b_re=v_s5_b_re, v_s5_b_im=v_s5_b_im, v_s5_c_re=v_s5_c_re, v_s5_c_im=v_s5_c_im, v_s5_d=v_s5_d, v_w_glu=v_w_glu, v_w_out=v_w_out, v_norm2_g=v_norm2_g, v_w_ffn_up=v_w_ffn_up, v_ffn_conv_w=v_ffn_conv_w, v_ffn_conv_b=v_ffn_conv_b, v_w_ffn_down=v_w_ffn_down, v_norm_f_g=v_norm_f_g)
    weights = {n: given[n] for n in TWIN_WEIGHTS}
    shared = {n: given[n] for n in SHARED_INPUTS}
    per_example = {n: given[n] for n in ['x', 'c']}
    grad_fn = _jax.value_and_grad(_loss, argnums=(0, 1))

    def one_microbatch(ex, loss_target):
        ex = dict(ex)
        diff = ex.pop(TWIN_DIFF_INPUT)
        return grad_fn(weights, diff, {**shared, **ex}, loss_target)

    if N_MICROBATCH == 1:
        loss, (grad_w, grad_x) = one_microbatch(per_example, given["loss_target"])
    else:
        def body(carry, xs):
            loss_sum, grad_sum = carry
            l_k, (gw_k, gx_k) = one_microbatch(xs[0], xs[1])
            with _jax.named_scope("update"):
                return (loss_sum + l_k, _jax.tree.map(_jnp.add, grad_sum, gw_k)), gx_k

        init = (_jnp.zeros((), _jnp.float32), _jax.tree.map(_jnp.zeros_like, weights))
        (loss, grad_w), grad_x = _jax.lax.scan(body, init, (per_example, given["loss_target"]))
    with _jax.named_scope("update"):
        delta_w, new_m, new_v = {}, {}, {}
        for n in TWIN_WEIGHTS:
            delta_w[n], new_m[n], new_v[n] = _adamw(weights[n], grad_w[n], given["m_" + n], given["v_" + n])
    return (loss, grad_x, *[grad_w[n] for n in TWIN_WEIGHTS], *[delta_w[n] for n in TWIN_WEIGHTS],
            *[new_m[n] for n in TWIN_WEIGHTS], *[new_v[n] for n in TWIN_WEIGHTS])
```

```python
import functools
import math

import jax
import jax.numpy as jnp
from jax import lax
from jax.experimental import pallas as pl
from jax.experimental.pallas import tpu as pltpu

F32 = jnp.float32
BF16 = jnp.bfloat16
MXU_DTYPE = jnp.bfloat16
MESH_IDS = pl.DeviceIdType.MESH
HIGHEST = lax.Precision.HIGHEST

D = 1024
RW, NH, HD = 512, 8, 64
LW, LA, LG = 64, 64, 128
SW, SGC, NG, SP = 512, 16, 32, 64
NSG = 4
SHIFT = 3 * RW + LW + LA + LG
DFF = 2816
RMS_EPS, GN_EPS, L2_EPS = 1e-6, 64e-5, 1e-12
LR, B1, B2, ADAM_EPS, WD, STEP = 0.001, 0.9, 0.999, 1e-8, 0.01, 10
DECAY_SCALE = math.exp(-0.5)
GELU_C = math.sqrt(2.0 / math.pi)

VMEM_LIMIT = 52 * 1024 * 1024
SUBLANES, LANES = 8, 128
WKV_CHUNK = 128
WKV_CKPT = 64


def _pick(n, cap):
    if n <= cap:
        return n
    best = None
    for t in range(LANES, cap + 1, LANES):
        if n % t == 0:
            best = t
    assert best is not None, (n, cap)
    return best


def _params(sem=None, vmem=VMEM_LIMIT):
    return pltpu.CompilerParams(dimension_semantics=sem, vmem_limit_bytes=vmem)


def _chip_of(p):
    return 2 * p[0] + p[1]


def _exchange(src, flips, n_out, slot_src, slot_dst, name):
    blk = src.shape[1:]
    nf = len(flips)

    def body(src_ref, out_ref, send_sems, recv_sems, loc_sem):
        me = (lax.axis_index("x"), lax.axis_index("y"), lax.axis_index("c"))
        copies = []
        for k, f in enumerate(flips):
            peer = tuple(1 - v if b else v for v, b in zip(me, f))
            cp = pltpu.make_async_remote_copy(
                src_ref=src_ref.at[slot_src(me, peer)], dst_ref=out_ref.at[slot_dst(me)],
                send_sem=send_sems.at[k], recv_sem=recv_sems.at[k],
                device_id=peer, device_id_type=MESH_IDS)
            cp.start()
            copies.append(cp)
        loc = pltpu.make_async_copy(src_ref.at[slot_src(me, me)], out_ref.at[slot_dst(me)], loc_sem)
        loc.start()
        for cp in copies:
            cp.wait_recv()
        for cp in copies:
            cp.wait_send()
        loc.wait()

    return pl.pallas_call(
        body, name=name,
        out_shape=jax.ShapeDtypeStruct((n_out,) + blk, src.dtype),
        in_specs=[pl.BlockSpec(memory_space=pl.ANY)],
        out_specs=pl.BlockSpec(memory_space=pl.ANY),
        scratch_shapes=[pltpu.SemaphoreType.DMA((nf,)), pltpu.SemaphoreType.DMA((nf,)), pltpu.SemaphoreType.DMA(())],
    )(src)


CHIP_FLIPS = ((1, 0, 0), (0, 1, 0), (1, 1, 0))
PAIR_FLIPS = ((0, 0, 1),)


def _gather_chips(blk, name):
    return _exchange(blk[None], CHIP_FLIPS, 4, lambda me, peer: 0, _chip_of, name)


def _gather_pair(blk, name):
    return _exchange(blk[None], PAIR_FLIPS, 2, lambda me, peer: 0, lambda me: me[2], name)


def _gather8(blk, name):
    g4 = _gather_chips(blk, name + "_chips")
    g2 = _gather_pair(g4, name + "_pair")
    return jnp.swapaxes(g2, 0, 1).reshape((8,) + blk.shape)


def _alltoall_chips(src4, name):
    return _exchange(src4, CHIP_FLIPS, 4, lambda me, peer: _chip_of(peer), _chip_of, name)


def _mm(As, Bs, out_dtype, name, tm=512, cap=1408):
    n = len(As)
    M, N = As[0].shape[0], Bs[0].shape[1]
    tm = min(tm, M)
    tn = _pick(N, cap)

    def body(*refs):
        o = refs[2 * n]
        acc = None
        for a, b in zip(refs[:n], refs[n:2 * n]):
            d = jnp.dot(a[...].astype(MXU_DTYPE), b[...].astype(MXU_DTYPE), preferred_element_type=F32)
            acc = d if acc is None else acc + d
        o[...] = acc.astype(o.dtype)

    in_specs = [pl.BlockSpec((tm, a.shape[1]), lambda i, j: (i, 0)) for a in As]
    in_specs += [pl.BlockSpec((b.shape[0], tn), lambda i, j: (0, j)) for b in Bs]
    return pl.pallas_call(
        body, name=name, grid=(M // tm, N // tn), in_specs=in_specs,
        out_specs=pl.BlockSpec((tm, tn), lambda i, j: (i, j)),
        out_shape=jax.ShapeDtypeStruct((M, N), out_dtype),
        compiler_params=_params(("parallel", "parallel")),
    )(*As, *Bs)


def _mm_tn(A, G, name, tt=1024, cap=1024):
    T, Ka = A.shape
    N = G.shape[1]
    tt = min(tt, T)
    tk = _pick(Ka, cap)
    tn = _pick(N, cap)

    def body(a, g, o):
        @pl.when(pl.program_id(2) == 0)
        def _():
            o[...] = jnp.zeros(o.shape, F32)
        o[...] += lax.dot_general(a[...].astype(MXU_DTYPE), g[...].astype(MXU_DTYPE),
                                  (((0,), (0,)), ((), ())), preferred_element_type=F32)

    return pl.pallas_call(
        body, name=name, grid=(Ka // tk, N // tn, T // tt),
        in_specs=[pl.BlockSpec((tt, tk), lambda i, j, t: (t, i)), pl.BlockSpec((tt, tn), lambda i, j, t: (t, j))],
        out_specs=pl.BlockSpec((tk, tn), lambda i, j, t: (i, j)),
        out_shape=jax.ShapeDtypeStruct((Ka, N), F32),
        compiler_params=_params(("parallel", "parallel", "arbitrary")),
    )(A, G)


def _rowwise(name, fn, *, Bl, S, R, tiled=(), prev=(), nxt=(), batch=(), full=(),
             out_tiled=(), out_batch=(), out_acc=()):
    R = min(R, S)
    nS = S // R
    T = Bl * S
    hb = R // SUBLANES
    n_in = len(tiled) + len(prev) + len(nxt) + len(batch) + len(full)

    in_specs, args = [], []
    for a, wd, cb in tiled:
        in_specs.append(pl.BlockSpec((R, wd), lambda b, i, cb=cb: (b * nS + i, cb)))
        args.append(a)
    for a, wd, cb in prev:
        in_specs.append(pl.BlockSpec((SUBLANES, wd), lambda b, i, cb=cb: (jnp.maximum((b * nS + i) * hb - 1, 0), cb)))
        args.append(a)
    for a, wd, cb in nxt:
        in_specs.append(pl.BlockSpec((SUBLANES, wd), lambda b, i, cb=cb: (jnp.minimum((b * nS + i + 1) * hb, T // SUBLANES - 1), cb)))
        args.append(a)
    for a, wd, cb in batch:
        in_specs.append(pl.BlockSpec((1, 1, wd), lambda b, i, cb=cb: (b, 0, cb)))
        args.append(a)
    for a in full:
        in_specs.append(pl.BlockSpec(a.shape, lambda b, i, nd=a.ndim: (0,) * nd))
        args.append(a)

    out_specs, out_shape = [], []
    for C, dt in out_tiled:
        out_specs.append(pl.BlockSpec((R, C), lambda b, i: (b * nS + i, 0)))
        out_shape.append(jax.ShapeDtypeStruct((T, C), dt))
    for C in out_batch:
        out_specs.append(pl.BlockSpec((1, 1, C), lambda b, i: (b, 0, 0)))
        out_shape.append(jax.ShapeDtypeStruct((Bl, 1, C), F32))
    for shp in out_acc:
        out_specs.append(pl.BlockSpec(shp, lambda b, i, nd=len(shp): (0,) * nd))
        out_shape.append(jax.ShapeDtypeStruct(shp, F32))

    nt, npv, nnx, nbt = len(tiled), len(prev), len(nxt), len(batch)

    def body(*refs):
        b, i = pl.program_id(0), pl.program_id(1)
        ins, outs = refs[:n_in], refs[n_in:]
        vals = [r[...] for r in ins[:nt]]
        vals += [jnp.where(i > 0, r[...], jnp.zeros(r.shape, r.dtype)) for r in ins[nt:nt + npv]]
        vals += [jnp.where(i < nS - 1, r[...], jnp.zeros(r.shape, r.dtype)) for r in ins[nt + npv:nt + npv + nnx]]
        vals += [r[0] for r in ins[nt + npv + nnx:nt + npv + nnx + nbt]]
        vals += [r[...] for r in ins[nt + npv + nnx + nbt:]]
        res = fn(*vals)
        if not isinstance(res, (tuple, list)):
            res = (res,)
        k = 0
        for _ in out_tiled:
            outs[k][...] = res[k].astype(outs[k].dtype)
            k += 1
        for _ in out_batch:
            o = outs[k]

            @pl.when(i == 0)
            def _(o=o):
                o[...] = jnp.zeros(o.shape, F32)
            o[0] += res[k]
            k += 1
        for _ in out_acc:
            o = outs[k]

            @pl.when((i == 0) & (b == 0))
            def _(o=o):
                o[...] = jnp.zeros(o.shape, F32)
            o[...] += res[k]
            k += 1

    out = pl.pallas_call(
        body, name=name, grid=(Bl, nS), in_specs=in_specs, out_specs=out_specs, out_shape=out_shape,
        compiler_params=_params(("arbitrary", "arbitrary")),
    )(*args)
    return out


def _shift_down(x, halo, k):
    row = lax.broadcasted_iota(jnp.int32, x.shape, 0)
    out = pltpu.roll(x, k, 0)
    for j in range(k):
        out = jnp.where(row == j, halo[SUBLANES - k + j:SUBLANES - k + j + 1, :], out)
    return out


def _shift_up(x, halo, k):
    n = x.shape[0]
    row = lax.broadcasted_iota(jnp.int32, x.shape, 0)
    out = pltpu.roll(x, n - k, 0)
    for j in range(k):
        out = jnp.where(row == n - k + j, halo[j:j + 1, :], out)
    return out


def _dotm(a, b):
    return jnp.dot(a.astype(MXU_DTYPE), b.astype(MXU_DTYPE), preferred_element_type=F32)


def _headsum(x, hm):
    return jnp.dot(x, hm, preferred_element_type=F32, precision=HIGHEST)


def _sigmoid(x):
    return 1.0 / (1.0 + jnp.exp(-x))


def _rms(x, g):
    return x * lax.rsqrt(jnp.mean(x * x, axis=-1, keepdims=True) + RMS_EPS) * g


def _norm_mod(x, g, sc, sh):
    return _rms(x, g) * (1.0 + sc) + sh


def _split_ps(ps):
    return (ps[:, 0:RW], ps[:, RW:2 * RW], ps[:, 2 * RW:3 * RW], ps[:, 3 * RW:3 * RW + LW + LA],
            ps[:, 3 * RW + LW + LA:SHIFT])


def _rwkv_prep(r, k, v, wa, gd, w0, w_up_p, a0, a_up_p, g_up, k_k, k_a, hm):
    w_raw = w0 + _dotm(jnp.tanh(wa), w_up_p)
    decay = jnp.exp(-DECAY_SCALE * _sigmoid(w_raw))
    a = _sigmoid(a0 + _dotm(wa, a_up_p))
    g = _dotm(_sigmoid(gd), g_up)
    kk = k * k_k
    kk = kk * lax.rsqrt(_headsum(kk * kk, hm) + L2_EPS)
    k2 = k * (1.0 + (a - 1.0) * k_a)
    return r, decay, k2, v, -kk, kk * a, g


def _rwkv_post(y, r, k2, v, g, ln_g, ln_b, r_k, hm):
    mean = _headsum(y, hm) * (1.0 / HD)
    yc = y - mean
    var = _headsum(yc * yc, hm) * (1.0 / HD)
    yn = yc * lax.rsqrt(var + GN_EPS) * ln_g + ln_b
    bonus = _headsum(r * k2 * r_k, hm) * v
    return (yn + bonus) * g


def _gelu(x):
    return 0.5 * x * (1.0 + jnp.tanh(GELU_C * (x + 0.044715 * (x * x * x))))


def _s5_post(yssm, u, d):
    return _gelu(yssm + d * u)


def _mix(ga, gb, ya, za, zb):
    return _sigmoid(ga) * ya + _sigmoid(gb) * (za * _sigmoid(zb))


def _conv_act(up_g, up_u, hg, hu, w_g, w_u, b_g, b_u):
    def conv(x, h, w, b):
        return b + w[0:1] * _shift_down(x, h, 2) + w[1:2] * _shift_down(x, h, 1) + w[2:3] * x
    gate = conv(up_g, hg, w_g, b_g)
    upv = conv(up_u, hu, w_u, b_u)
    return gate, upv


def _silu_gate(gate, upv):
    return gate * _sigmoid(gate) * upv


def _pair_sum(x, lo):
    s0 = jnp.sum(jnp.where(lo, x, 0.0), axis=1, keepdims=True)
    s1 = jnp.sum(jnp.where(lo, 0.0, x), axis=1, keepdims=True)
    return jnp.where(lo, s0, s1)


def _pair_col(xT_ref, bi, p, m, lo):
    c0 = jnp.sum(jnp.where(m, xT_ref[bi, p * 128:p * 128 + 64, :], 0.0), axis=1, keepdims=True)
    c1 = jnp.sum(jnp.where(m, xT_ref[bi, p * 128 + 64:p * 128 + 128, :], 0.0), axis=1, keepdims=True)
    return jnp.where(lo, c0, c1)


def _get_row(ref, bi, t, cols):
    t8 = pl.multiple_of((t // SUBLANES) * SUBLANES, SUBLANES)
    grp = ref[bi, pl.ds(t8, SUBLANES), cols]
    return pltpu.roll(grp, (SUBLANES - t % SUBLANES) % SUBLANES, 0)[0:1]


def _put_row(ref, bi, t, cols, row):
    t8 = pl.multiple_of((t // SUBLANES) * SUBLANES, SUBLANES)
    sub = lax.broadcasted_iota(jnp.int32, (SUBLANES, 128), 0)
    idx = (bi, pl.ds(t8, SUBLANES), cols)
    ref[idx] = jnp.where(sub == t % SUBLANES, row, ref[idx])


def _wkv_nb(Bl):
    return 2 if Bl % 2 == 0 else 1


def _wkv_fwd(r, w, k, a, b, vT):
    Bl, S, _ = r.shape
    nb = _wkv_nb(Bl)
    TC, CK = WKV_CHUNK, WKV_CKPT
    nsub = TC // CK

    def body(r_ref, w_ref, k_ref, a_ref, b_ref, vT_ref, yT_ref, ck_ref, S_ref):
        @pl.when(pl.program_id(1) == 0)
        def _():
            S_ref[...] = jnp.zeros(S_ref.shape, F32)
        yT_ref[...] = jnp.zeros(yT_ref.shape, F32)
        lane = lax.broadcasted_iota(jnp.int32, (HD, 128), 1)
        lo = lane < HD

        def step(t, carry):
            m = lane == t
            for bi in range(nb):
                for p in range(4):
                    cols = slice(p * 128, (p + 1) * 128)
                    rr, ww, kk, aa, bb = (_get_row(z, bi, t, cols) for z in (r_ref, w_ref, k_ref, a_ref, b_ref))
                    st = S_ref[bi, p]
                    sa = _pair_sum(st * aa, lo)
                    vc = _pair_col(vT_ref, bi, p, m, lo)
                    st = st * ww + sa * bb + vc * kk
                    S_ref[bi, p] = st
                    pr = st * rr
                    y0 = jnp.sum(jnp.where(lo, pr, 0.0), axis=1, keepdims=True)
                    y1 = jnp.sum(jnp.where(lo, 0.0, pr), axis=1, keepdims=True)
                    h0 = (bi, slice(p * 128, p * 128 + 64), slice(None))
                    h1 = (bi, slice(p * 128 + 64, p * 128 + 128), slice(None))
                    yT_ref[h0] = jnp.where(m, y0, yT_ref[h0])
                    yT_ref[h1] = jnp.where(m, y1, yT_ref[h1])
            return carry

        for sub in range(nsub):
            ck_ref[:, sub] = S_ref[...]
            lax.fori_loop(sub * CK, (sub + 1) * CK, step, 0)

    row_spec = pl.BlockSpec((nb, TC, RW), lambda g, c: (g, c, 0))
    col_spec = pl.BlockSpec((nb, RW, TC), lambda g, c: (g, 0, c))
    return pl.pallas_call(
        body, name="wkv_fwd", grid=(Bl // nb, S // TC),
        in_specs=[row_spec] * 5 + [col_spec],
        out_specs=[col_spec, pl.BlockSpec((nb, nsub, 4, HD, 128), lambda g, c: (g, c, 0, 0, 0))],
        out_shape=[jax.ShapeDtypeStruct((Bl, RW, S), F32), jax.ShapeDtypeStruct((Bl, S // CK, 4, HD, 128), F32)],
        scratch_shapes=[pltpu.VMEM((nb, 4, HD, 128), F32)],
        compiler_params=_params(("arbitrary", "arbitrary")),
    )(r, w, k, a, b, vT)


def _wkv_bwd(r, w, k, a, b, vT, dyT, ck):
    Bl, S, _ = r.shape
    nb = _wkv_nb(Bl)
    TC, CK = WKV_CHUNK, WKV_CKPT
    nsub = TC // CK
    nC = S // TC

    def body(r_ref, w_ref, k_ref, a_ref, b_ref, vT_ref, dyT_ref, ck_ref,
             dr_ref, dw_ref, dk_ref, da_ref, db_ref, dvT_ref, dS_ref, S_ref, H_ref):
        @pl.when(pl.program_id(1) == 0)
        def _():
            dS_ref[...] = jnp.zeros(dS_ref.shape, F32)
        for o in (dr_ref, dw_ref, dk_ref, da_ref, db_ref, dvT_ref):
            o[...] = jnp.zeros(o.shape, F32)
        lane = lax.broadcasted_iota(jnp.int32, (HD, 128), 1)
        lo = lane < HD

        for sub in reversed(range(nsub)):
            base = sub * CK
            S_ref[...] = ck_ref[:, sub]

            def fwd(j, carry, base=base):
                t = base + j
                m = lane == t
                for bi in range(nb):
                    for p in range(4):
                        cols = slice(p * 128, (p + 1) * 128)
                        ww, kk, aa, bb = (_get_row(z, bi, t, cols) for z in (w_ref, k_ref, a_ref, b_ref))
                        st = S_ref[bi, p]
                        H_ref[j, bi * 4 + p] = st
                        sa = _pair_sum(st * aa, lo)
                        vc = _pair_col(vT_ref, bi, p, m, lo)
                        S_ref[bi, p] = st * ww + sa * bb + vc * kk
                return carry

            lax.fori_loop(0, CK, fwd, 0)
            for bi in range(nb):
                for p in range(4):
                    H_ref[CK, bi * 4 + p] = S_ref[bi, p]

            def bwd(i, carry, base=base):
                j = CK - 1 - i
                t = base + j
                m = lane == t
                for bi in range(nb):
                    for p in range(4):
                        cols = slice(p * 128, (p + 1) * 128)
                        put = lambda ref, val: _put_row(ref, bi, t, cols, jnp.sum(val, axis=0, keepdims=True))
                        s_prev = H_ref[j, bi * 4 + p]
                        s_cur = H_ref[j + 1, bi * 4 + p]
                        rr, ww, kk, aa, bb = (_get_row(z, bi, t, cols) for z in (r_ref, w_ref, k_ref, a_ref, b_ref))
                        dyc = _pair_col(dyT_ref, bi, p, m, lo)
                        vc = _pair_col(vT_ref, bi, p, m, lo)
                        ds = dS_ref[bi, p] + dyc * rr
                        put(dr_ref, s_cur * dyc)
                        put(dw_ref, ds * s_prev)
                        put(dk_ref, ds * vc)
                        pv = ds * kk
                        dv0 = jnp.sum(jnp.where(lo, pv, 0.0), axis=1, keepdims=True)
                        dv1 = jnp.sum(jnp.where(lo, 0.0, pv), axis=1, keepdims=True)
                        h0 = (bi, slice(p * 128, p * 128 + 64), slice(None))
                        h1 = (bi, slice(p * 128 + 64, p * 128 + 128), slice(None))
                        dvT_ref[h0] = jnp.where(m, dv0, dvT_ref[h0])
                        dvT_ref[h1] = jnp.where(m, dv1, dvT_ref[h1])
                        dsa = _pair_sum(ds * bb, lo)
                        sa = _pair_sum(s_prev * aa, lo)
                        put(db_ref, ds * sa)
                        put(da_ref, s_prev * dsa)
                        dS_ref[bi, p] = ds * ww + dsa * aa
                return carry

            lax.fori_loop(0, CK, bwd, 0)

    row_spec = pl.BlockSpec((nb, TC, RW), lambda g, c: (g, nC - 1 - c, 0))
    col_spec = pl.BlockSpec((nb, RW, TC), lambda g, c: (g, 0, nC - 1 - c))
    rows = jax.ShapeDtypeStruct((Bl, S, RW), F32)
    return pl.pallas_call(
        body, name="wkv_bwd", grid=(Bl // nb, nC),
        in_specs=[row_spec] * 5 + [col_spec, col_spec,
                                   pl.BlockSpec((nb, nsub, 4, HD, 128), lambda g, c: (g, nC - 1 - c, 0, 0, 0))],
        out_specs=[row_spec] * 5 + [col_spec],
        out_shape=[rows] * 5 + [jax.ShapeDtypeStruct((Bl, RW, S), F32)],
        scratch_shapes=[pltpu.VMEM((nb, 4, HD, 128), F32), pltpu.VMEM((nb, 4, HD, 128), F32),
                        pltpu.VMEM((CK + 1, nb * 4, HD, 128), F32)],
        compiler_params=_params(("arbitrary", "arbitrary")),
    )(r, w, k, a, b, vT, dyT, ck)


NST = NG * SP


def _cmul(ar, ai, br, bi):
    return ar * br - ai * bi, ar * bi + ai * br


def _s5_tiles(are, aim, reverse):
    if reverse:
        aim = -aim
    row = lax.broadcasted_iota(jnp.int32, (SUBLANES, NST), 0)
    pw = [(are, aim)]
    for _ in range(SUBLANES - 1):
        pw.append(_cmul(pw[-1][0], pw[-1][1], are, aim))
    bc = lambda z: jnp.broadcast_to(z, (SUBLANES, NST))
    ms = []
    for kk in (1, 2, 4):
        cond = (row < SUBLANES - kk) if reverse else (row >= kk)
        ms.append((jnp.where(cond, bc(pw[kk - 1][0]), 0.0), jnp.where(cond, bc(pw[kk - 1][1]), 0.0)))
    pr = jnp.zeros((SUBLANES, NST), F32)
    pi = jnp.zeros((SUBLANES, NST), F32)
    for i in range(SUBLANES):
        n = SUBLANES - i if reverse else i + 1
        pr = jnp.where(row == i, bc(pw[n - 1][0]), pr)
        pi = jnp.where(row == i, bc(pw[n - 1][1]), pi)
    return ms, (pr, pi)


def _s5_block(re, im, ms, pc, cre, cim, sg, reverse):
    ln = slice(sg * 512, (sg + 1) * 512)
    for (mr, mi), kk in zip(ms, (1, 2, 4)):
        sh = SUBLANES - kk if reverse else kk
        sre, sim = pltpu.roll(re, sh, 0), pltpu.roll(im, sh, 0)
        tr, ti = _cmul(mr[:, ln], mi[:, ln], sre, sim)
        re, im = re + tr, im + ti
    tr, ti = _cmul(pc[0][:, ln], pc[1][:, ln], cre[:, ln], cim[:, ln])
    return re + tr, im + ti


def _s5_scan(X_ref, n_rows, ms, pc, cre, cim, reverse, visit=None, acc0=None):
    nblk = n_rows // SUBLANES

    def it(i, carry):
        cre, cim, acc = carry
        j = nblk - 1 - i if reverse else i
        rows = pl.ds(pl.multiple_of(j * SUBLANES, SUBLANES), SUBLANES)
        edge = 0 if reverse else SUBLANES - 1
        blocks, ncre, ncim = [], [], []
        for sg in range(NSG):
            lr = slice(sg * 1024, sg * 1024 + 512)
            li = slice(sg * 1024 + 512, (sg + 1) * 1024)
            re, im = _s5_block(X_ref[rows, lr], X_ref[rows, li], ms, pc, cre, cim, sg, reverse)
            X_ref[rows, lr] = re
            X_ref[rows, li] = im
            blocks.append((re, im))
            ncre.append(re[edge:edge + 1])
            ncim.append(im[edge:edge + 1])
        if visit is not None:
            acc = visit(j, blocks, acc)
        return jnp.concatenate(ncre, axis=1), jnp.concatenate(ncim, axis=1), acc

    return lax.fori_loop(0, nblk, it, (cre, cim, acc0 if acc0 is not None else 0))


def _s5_fwd(u, wb, wc, ab, Bl, S, R=256):
    R = min(R, S)
    nC = S // R

    def body(u_ref, wb_ref, wc_ref, ab_ref, y_ref, st_ref, X_ref, car_ref):
        @pl.when(pl.program_id(1) == 0)
        def _():
            car_ref[...] = jnp.zeros(car_ref.shape, F32)
        st_ref[0, 0] = car_ref[...]
        ms, pc = _s5_tiles(ab_ref[0:1], ab_ref[1:2], False)
        for sg in range(NSG):
            X_ref[:, sg * 1024:(sg + 1) * 1024] = _dotm(u_ref[:, sg * 128:(sg + 1) * 128], wb_ref[sg])
        cre, cim, _ = _s5_scan(X_ref, R, ms, pc, car_ref[0:1], car_ref[1:2], False)
        car_ref[0:1] = cre
        car_ref[1:2] = cim
        for sg in range(NSG):
            y_ref[:, sg * 128:(sg + 1) * 128] = _dotm(X_ref[:, sg * 1024:(sg + 1) * 1024], wc_ref[sg])

    return pl.pallas_call(
        body, name="s5_fwd", grid=(Bl, nC),
        in_specs=[pl.BlockSpec((R, SW), lambda b, c: (b * nC + c, 0)),
                  pl.BlockSpec(wb.shape, lambda b, c: (0, 0, 0)), pl.BlockSpec(wc.shape, lambda b, c: (0, 0, 0)),
                  pl.BlockSpec(ab.shape, lambda b, c: (0, 0))],
        out_specs=[pl.BlockSpec((R, SW), lambda b, c: (b * nC + c, 0)),
                   pl.BlockSpec((1, 1, 2, NST), lambda b, c: (b, c, 0, 0))],
        out_shape=[jax.ShapeDtypeStruct((Bl * S, SW), F32), jax.ShapeDtypeStruct((Bl, nC, 2, NST), F32)],
        scratch_shapes=[pltpu.VMEM((R, 2 * NST), F32), pltpu.VMEM((2, NST), F32)],
        compiler_params=_params(("arbitrary", "arbitrary")),
    )(u, wb, wc, ab)


def _s5_bwd(u, dy, wb, wc, ab, st, Bl, S, R=256):
    R = min(R, S)
    nC = S // R

    def body(u_ref, dy_ref, wb_ref, wc_ref, ab_ref, st_ref, du_ref, dwb_ref, dwc_ref, dab_ref,
             X_ref, G_ref, car_ref):
        first = (pl.program_id(0) == 0) & (pl.program_id(1) == 0)

        @pl.when(first)
        def _():
            dwb_ref[...] = jnp.zeros(dwb_ref.shape, F32)
            dwc_ref[...] = jnp.zeros(dwc_ref.shape, F32)
            dab_ref[...] = jnp.zeros(dab_ref.shape, F32)

        @pl.when(pl.program_id(1) == 0)
        def _():
            car_ref[...] = jnp.zeros(car_ref.shape, F32)

        are, aim = ab_ref[0:1], ab_ref[1:2]
        ms, pc = _s5_tiles(are, aim, False)
        for sg in range(NSG):
            X_ref[:, sg * 1024:(sg + 1) * 1024] = _dotm(u_ref[:, sg * 128:(sg + 1) * 128], wb_ref[sg])
        _s5_scan(X_ref, R, ms, pc, st_ref[0, 0, 0:1], st_ref[0, 0, 1:2], False)
        dyv = dy_ref[...].astype(MXU_DTYPE)
        for sg in range(NSG):
            G_ref[:, sg * 1024:(sg + 1) * 1024] = lax.dot_general(
                dyv[:, sg * 128:(sg + 1) * 128], wc_ref[sg].astype(MXU_DTYPE), (((1,), (1,)), ((), ())),
                preferred_element_type=F32)
        rms_, rpc = _s5_tiles(are, aim, True)
        row = lax.broadcasted_iota(jnp.int32, (SUBLANES, 512), 0)

        def visit(j, blocks, acc):
            before = pl.multiple_of(jnp.maximum(j - 1, 0) * SUBLANES, SUBLANES)
            prow = X_ref[pl.ds(before, SUBLANES), :][SUBLANES - 1:SUBLANES]
            rows = pl.ds(pl.multiple_of(j * SUBLANES, SUBLANES), SUBLANES)
            are_acc, aim_acc = [], []
            for sg in range(NSG):
                lr = slice(sg * 1024, sg * 1024 + 512)
                li = slice(sg * 1024 + 512, (sg + 1) * 1024)
                ln = slice(sg * 512, (sg + 1) * 512)
                pre = jnp.where(j > 0, prow[:, lr], st_ref[0, 0, 0:1, ln])
                pim = jnp.where(j > 0, prow[:, li], st_ref[0, 0, 1:2, ln])
                xre = jnp.where(row == 0, pre, pltpu.roll(X_ref[rows, lr], 1, 0))
                xim = jnp.where(row == 0, pim, pltpu.roll(X_ref[rows, li], 1, 0))
                dre, dim = blocks[sg]
                are_acc.append(dre * xre + dim * xim)
                aim_acc.append(dim * xre - dre * xim)
            return acc[0] + jnp.concatenate(are_acc, axis=1), acc[1] + jnp.concatenate(aim_acc, axis=1)

        zero = jnp.zeros((SUBLANES, NST), F32)
        cre, cim, acc = _s5_scan(G_ref, R, rms_, rpc, car_ref[0:1], car_ref[1:2], True, visit, (zero, zero))
        car_ref[0:1] = cre
        car_ref[1:2] = cim
        dab_ref[0:1] += jnp.sum(acc[0], axis=0, keepdims=True)
        dab_ref[1:2] += jnp.sum(acc[1], axis=0, keepdims=True)
        uv = u_ref[...].astype(MXU_DTYPE)
        for sg in range(NSG):
            cs = slice(sg * 1024, (sg + 1) * 1024)
            us = slice(sg * 128, (sg + 1) * 128)
            gx = G_ref[:, cs].astype(MXU_DTYPE)
            dwb_ref[sg] += lax.dot_general(uv[:, us], gx, (((0,), (0,)), ((), ())), preferred_element_type=F32)
            dwc_ref[sg] += lax.dot_general(X_ref[:, cs].astype(MXU_DTYPE), dyv[:, us], (((0,), (0,)), ((), ())),
                                           preferred_element_type=F32)
            du_ref[:, us] = lax.dot_general(gx, wb_ref[sg].astype(MXU_DTYPE), (((1,), (1,)), ((), ())),
                                            preferred_element_type=F32)

    rmap = lambda b, c: (b * nC + nC - 1 - c, 0)
    return pl.pallas_call(
        body, name="s5_bwd", grid=(Bl, nC),
        in_specs=[pl.BlockSpec((R, SW), rmap), pl.BlockSpec((R, SW), rmap),
                  pl.BlockSpec(wb.shape, lambda b, c: (0, 0, 0)), pl.BlockSpec(wc.shape, lambda b, c: (0, 0, 0)),
                  pl.BlockSpec(ab.shape, lambda b, c: (0, 0)),
                  pl.BlockSpec((1, 1, 2, NST), lambda b, c: (b, nC - 1 - c, 0, 0))],
        out_specs=[pl.BlockSpec((R, SW), rmap), pl.BlockSpec(wb.shape, lambda b, c: (0, 0, 0)),
                   pl.BlockSpec(wc.shape, lambda b, c: (0, 0, 0)), pl.BlockSpec((2, NST), lambda b, c: (0, 0))],
        out_shape=[jax.ShapeDtypeStruct((Bl * S, SW), F32), jax.ShapeDtypeStruct(wb.shape, F32),
                   jax.ShapeDtypeStruct(wc.shape, F32), jax.ShapeDtypeStruct((2, NST), F32)],
        scratch_shapes=[pltpu.VMEM((R, 2 * NST), F32), pltpu.VMEM((R, 2 * NST), F32), pltpu.VMEM((2, NST), F32)],
        compiler_params=_params(("arbitrary", "arbitrary")),
    )(u, dy, wb, wc, ab, st)


def _s5_disc_math(a_re, a_im, log_dt, b_re, b_im, expand):
    dt = jnp.exp(log_dt)
    z_re, z_im = a_re * dt, a_im * dt
    mag = jnp.exp(z_re)
    ab_re, ab_im = mag * jnp.cos(z_im), mag * jnp.sin(z_im)
    den = a_re * a_re + a_im * a_im
    q_re = ((ab_re - 1.0) * a_re + ab_im * a_im) / den
    q_im = (ab_im * a_re - (ab_re - 1.0) * a_im) / den
    qe_re = jnp.dot(q_re, expand, preferred_element_type=F32, precision=HIGHEST)
    qe_im = jnp.dot(q_im, expand, preferred_element_type=F32, precision=HIGHEST)
    return ab_re, ab_im, qe_re * b_re - qe_im * b_im, qe_re * b_im + qe_im * b_re


def _whole(shape):
    return pl.BlockSpec(shape, lambda nd=len(shape): (0,) * nd)


def _s5_disc(a_re, a_im, log_dt, b_re, b_im, expand):
    def body(a, b, c, d, e, f, o0, o1, o2, o3):
        res = _s5_disc_math(a[...], b[...], c[...], d[...], e[...], f[...])
        for o, v in zip((o0, o1, o2, o3), res):
            o[...] = v
    ins = (a_re, a_im, log_dt, b_re, b_im, expand)
    outs = [jax.ShapeDtypeStruct(a_re.shape, F32)] * 2 + [jax.ShapeDtypeStruct(b_re.shape, F32)] * 2
    return pl.pallas_call(body, name="s5_disc", in_specs=[_whole(x.shape) for x in ins],
                          out_specs=[_whole(o.shape) for o in outs], out_shape=outs)(*ins)


def _s5_disc_bwd(a_re, a_im, log_dt, b_re, b_im, expand, cts):
    def body(a, b, c, d, e, f, g0, g1, g2, g3, o0, o1, o2, o3, o4):
        fn = lambda *p: _s5_disc_math(*p, f[...])
        _, vjp = jax.vjp(fn, a[...], b[...], c[...], d[...], e[...])
        for o, v in zip((o0, o1, o2, o3, o4), vjp((g0[...], g1[...], g2[...], g3[...]))):
            o[...] = v
    ins = (a_re, a_im, log_dt, b_re, b_im, expand) + tuple(cts)
    outs = [jax.ShapeDtypeStruct(x.shape, F32) for x in (a_re, a_im, log_dt, b_re, b_im)]
    return pl.pallas_call(body, name="s5_disc_bwd", in_specs=[_whole(x.shape) for x in ins],
                          out_specs=[_whole(o.shape) for o in outs], out_shape=outs)(*ins)


def _ada_fwd(c_all, w_shard, b_shard):
    def body(c_ref, w_ref, b_ref, o_ref):
        cv = c_ref[...]
        o_ref[...] = _dotm(cv * _sigmoid(cv), w_ref[...]) + b_ref[...]
    n = w_shard.shape[1]
    return pl.pallas_call(
        body, name="ada_fwd", in_specs=[_whole(c_all.shape), _whole(w_shard.shape), _whole(b_shard.shape)],
        out_specs=_whole((c_all.shape[0], n)), out_shape=jax.ShapeDtypeStruct((c_all.shape[0], n), F32),
        compiler_params=_params(),
    )(c_all, w_shard, b_shard)


def _ada_bwd(c_all, dmod_cols, dmod_all):
    def body(c_ref, dc_ref, da_ref, gw_ref, gb_ref):
        cv = c_ref[...]
        gw_ref[...] = lax.dot_general((cv * _sigmoid(cv)).astype(MXU_DTYPE), dc_ref[...].astype(MXU_DTYPE),
                                      (((0,), (0,)), ((), ())), preferred_element_type=F32)
        gb_ref[...] = jnp.sum(da_ref[...], axis=0, keepdims=True)
    n = dmod_cols.shape[1]
    return pl.pallas_call(
        body, name="ada_bwd", in_specs=[_whole(c_all.shape), _whole(dmod_cols.shape), _whole(dmod_all.shape)],
        out_specs=[_whole((D, n)), _whole((1, dmod_all.shape[1]))],
        out_shape=[jax.ShapeDtypeStruct((D, n), F32), jax.ShapeDtypeStruct((1, dmod_all.shape[1]), F32)],
        compiler_params=_params(),
    )(c_all, dmod_cols, dmod_all)


def _rows_block(n_rows, cap=512):
    if n_rows <= cap:
        return n_rows
    for t in range(cap - cap % SUBLANES, 0, -SUBLANES):
        if n_rows % t == 0:
            return t
    return n_rows


def _adamw(w, g, m, v, name):
    rows, cols = w.shape
    tr = _rows_block(rows, max(SUBLANES, (1 << 19) // max(cols, 1) // SUBLANES * SUBLANES))

    def body(w_ref, g_ref, m_ref, v_ref, d_ref, nm_ref, nv_ref):
        gv = g_ref[...]
        nm = B1 * m_ref[...] + (1.0 - B1) * gv
        nv = B2 * v_ref[...] + (1.0 - B2) * (gv * gv)
        m_hat = nm / (1.0 - B1 ** STEP)
        v_hat = nv / (1.0 - B2 ** STEP)
        d_ref[...] = -LR * (m_hat / (jnp.sqrt(v_hat) + ADAM_EPS) + WD * w_ref[...])
        nm_ref[...] = nm
        nv_ref[...] = nv

    spec = pl.BlockSpec((tr, cols), lambda i: (i, 0))
    sd = jax.ShapeDtypeStruct((rows, cols), F32)
    return pl.pallas_call(body, name=name, grid=(rows // tr,), in_specs=[spec] * 4, out_specs=[spec] * 3,
                          out_shape=[sd] * 3, compiler_params=_params(("parallel",)))(w, g, m, v)


def _sum_slots(x, out_dtype, name):
    n, rows, cols = x.shape
    tr = _rows_block(rows)

    def body(x_ref, o_ref):
        acc = x_ref[0].astype(F32)
        for j in range(1, n):
            acc = acc + x_ref[j].astype(F32)
        o_ref[...] = acc.astype(o_ref.dtype)

    return pl.pallas_call(
        body, name=name, grid=(rows // tr,), in_specs=[pl.BlockSpec((n, tr, cols), lambda i: (0, i, 0))],
        out_specs=pl.BlockSpec((tr, cols), lambda i: (i, 0)), out_shape=jax.ShapeDtypeStruct((rows, cols), out_dtype),
        compiler_params=_params(("parallel",)))(x)


PACK_COLS = 1024


def _pack_rows(parts, dtype, row_mult):
    flat = jnp.concatenate([p.reshape(-1).astype(dtype) for p in parts])
    per = PACK_COLS * row_mult
    n = -(-flat.shape[0] // per) * per
    flat = jnp.pad(flat, (0, n - flat.shape[0]))
    return flat.reshape(n // PACK_COLS, PACK_COLS)


def _unpack(flat, shapes):
    out, off = [], 0
    for s in shapes:
        n = math.prod(s)
        out.append(flat[off:off + n].reshape(s))
        off += n
    return out


def _col_shards(g):
    r, C = g.shape
    return g.reshape(r, 4, C // 4).transpose(1, 0, 2).reshape(4, r * (C // 4))


def _row_shards(g):
    r, C = g.shape
    return g.reshape(4, (r // 4) * C)


def _from_col_shards(x, r, C):
    return x.reshape(4, r, C // 4).transpose(1, 0, 2).reshape(r, C)


BIG = (("w_in", (D, SHIFT + SW + 2 * D), 1), ("w_out_rwkv", (RW, D), 1), ("w_glu", (SW, 2 * D), 1),
       ("w_out", (D, D), 0), ("w_ffn_up", (D, 2 * DFF), 1), ("w_ffn_down", (DFF, D), 0))
BIG_SMALL = (("rwkv_w_up", (LW, RW), 1), ("rwkv_a_up", (LA, RW), 1), ("rwkv_g_up", (LG, RW), 1),
             ("ffn_conv_w", (3, 2 * DFF), 1))


def _shard_shape(shape, axis):
    return (shape[0] // 4, shape[1]) if axis == 0 else (shape[0], shape[1] // 4)


def _to_shards(g, axis):
    return _row_shards(g) if axis == 0 else _col_shards(g)


def _from_shards(x, shape, axis):
    return x.reshape(shape) if axis == 0 else _from_col_shards(x, *shape)


def kernel(x, c, w_ada, b_ada, norm1_g, w_in, mu_shift, rwkv_w0, rwkv_w_up, rwkv_a0, rwkv_a_up, rwkv_g_up, rwkv_k_k, rwkv_k_a, rwkv_r_k, rwkv_ln_g, rwkv_ln_b, w_out_rwkv, s5_a_re, s5_a_im, s5_log_dt, s5_b_re, s5_b_im, s5_c_re, s5_c_im, s5_d, w_glu, w_out, norm2_g, w_ffn_up, ffn_conv_w, ffn_conv_b, w_ffn_down, norm_f_g, loss_target, m_w_ada, m_b_ada, m_norm1_g, m_w_in, m_mu_shift, m_rwkv_w0, m_rwkv_w_up, m_rwkv_a0, m_rwkv_a_up, m_rwkv_g_up, m_rwkv_k_k, m_rwkv_k_a, m_rwkv_r_k, m_rwkv_ln_g, m_rwkv_ln_b, m_w_out_rwkv, m_s5_a_re, m_s5_a_im, m_s5_log_dt, m_s5_b_re, m_s5_b_im, m_s5_c_re, m_s5_c_im, m_s5_d, m_w_glu, m_w_out, m_norm2_g, m_w_ffn_up, m_ffn_conv_w, m_ffn_conv_b, m_w_ffn_down, m_norm_f_g, v_w_ada, v_b_ada, v_norm1_g, v_w_in, v_mu_shift, v_rwkv_w0, v_rwkv_w_up, v_rwkv_a0, v_rwkv_a_up, v_rwkv_g_up, v_rwkv_k_k, v_rwkv_k_a, v_rwkv_r_k, v_rwkv_ln_g, v_rwkv_ln_b, v_w_out_rwkv, v_s5_a_re, v_s5_a_im, v_s5_log_dt, v_s5_b_re, v_s5_b_im, v_s5_c_re, v_s5_c_im, v_s5_d, v_w_glu, v_w_out, v_norm2_g, v_w_ffn_up, v_ffn_conv_w, v_ffn_conv_b, v_w_ffn_down, v_norm_f_g):
    names = ["w_ada", "b_ada", "norm1_g", "w_in", "mu_shift", "rwkv_w0", "rwkv_w_up", "rwkv_a0", "rwkv_a_up",
             "rwkv_g_up", "rwkv_k_k", "rwkv_k_a", "rwkv_r_k", "rwkv_ln_g", "rwkv_ln_b", "w_out_rwkv", "s5_a_re",
             "s5_a_im", "s5_log_dt", "s5_b_re", "s5_b_im", "s5_c_re", "s5_c_im", "s5_d", "w_glu", "w_out", "norm2_g",
             "w_ffn_up", "ffn_conv_w", "ffn_conv_b", "w_ffn_down", "norm_f_g"]
    env = dict(locals())
    W = {n: env[n] for n in names}
    M = {n: env["m_" + n] for n in names}
    V = {n: env["v_" + n] for n in names}

    Bl, S, _ = x.shape
    T = Bl * S
    ix, iy, ic = lax.axis_index("x"), lax.axis_index("y"), lax.axis_index("c")
    chip = 2 * ix + iy
    dev = 2 * chip + ic
    rw = functools.partial(_rowwise, Bl=Bl, S=S)

    big_pack = _pack_rows([W[n][0] for n, _, _ in BIG], MXU_DTYPE, 2 * SUBLANES * 2)
    half_rows = big_pack.shape[0] // 2
    my_half = lax.dynamic_slice_in_dim(big_pack, ic * half_rows, half_rows, 0)
    big_all = _gather8(my_half, "gather_w").reshape(4, -1)
    small_pack = _pack_rows([c] + [W[n][0] for n, _, _ in BIG_SMALL], F32, SUBLANES)
    small_all = _gather8(small_pack, "gather_small").reshape(8, -1)
    c_all = small_all[:, :Bl * D].reshape(8 * Bl, D)
    small_chip = small_all[0::2, Bl * D:]

    full = {}
    off = 0
    for n, shape, axis in BIG:
        ss = _shard_shape(shape, axis)
        full[n] = _from_shards(big_all[:, off:off + math.prod(ss)], shape, axis)
        off += math.prod(ss)
    off = 0
    for n, shape, axis in BIG_SMALL:
        ss = _shard_shape(shape, axis)
        full[n] = _from_shards(small_chip[:, off:off + math.prod(ss)], shape, axis)
        off += math.prod(ss)
    w_p, w_u, w_g = full["w_in"][:, :SHIFT], full["w_in"][:, SHIFT:SHIFT + SW], full["w_in"][:, SHIFT + SW:]
    zeros_l = jnp.zeros((LW, RW), F32)
    w_up_p = jnp.concatenate([full["rwkv_w_up"], zeros_l], axis=0)
    a_up_p = jnp.concatenate([zeros_l, full["rwkv_a_up"]], axis=0)
    g_up = full["rwkv_g_up"]
    conv_w = full["ffn_conv_w"]
    conv_wg, conv_wu = conv_w[:, :DFF], conv_w[:, DFF:]
    conv_bg, conv_bu = ffn_conv_b[:, :DFF], ffn_conv_b[:, DFF:]
    hm = jnp.kron(jnp.eye(NH, dtype=F32), jnp.ones((HD, HD), F32))

    ncol = 6 * D // 4
    b_ada_cols = lax.dynamic_slice_in_dim(b_ada, chip * ncol, ncol, 1)
    mod_part = _ada_fwd(c_all, w_ada[0], b_ada_cols)
    mod4 = _gather_chips(mod_part, "gather_mod")
    mod = lax.dynamic_slice_in_dim(mod4, dev * Bl, Bl, 1).transpose(1, 0, 2).reshape(Bl, 1, 6 * D)
    SH1, SC1, GT1, SH2, SC2, GT2 = range(6)

    x2d = x.reshape(T, D)
    tgt = loss_target.reshape(T, D)

    (h1,) = rw("norm1", lambda xv, sc, sh, g: _norm_mod(xv, g, sc, sh), R=256, tiled=[(x2d, D, 0)],
               batch=[(mod, D, SC1), (mod, D, SH1)], full=[norm1_g], out_tiled=[(D, MXU_DTYPE)])
    p = _mm([h1], [w_p], F32, "proj_p")
    u = _mm([h1], [w_u], F32, "proj_u")
    gates = _mm([h1], [w_g], F32, "proj_g")

    prep_params = [rwkv_w0, w_up_p, rwkv_a0, a_up_p, g_up, rwkv_k_k, rwkv_k_a, hm]

    def prep_fwd(pv, ph, mu, *pp):
        ps = pv + (_shift_down(pv, ph, 1) - pv) * mu
        return _rwkv_prep(*_split_ps(ps), *pp)

    r_, w_, k_, v_, a_, b_, g_ = rw("rwkv_prep", prep_fwd, R=256, tiled=[(p, SHIFT, 0)], prev=[(p, SHIFT, 0)],
                                    full=[mu_shift] + prep_params, out_tiled=[(RW, F32)] * 7)
    to3 = lambda z: z.reshape(Bl, S, RW)
    toT = lambda z: jnp.swapaxes(z.reshape(Bl, S, RW), 1, 2)
    fromT = lambda zT: jnp.swapaxes(zT, 1, 2).reshape(T, RW)
    vT = toT(v_)
    yT, ck = _wkv_fwd(to3(r_), to3(w_), to3(k_), to3(a_), to3(b_), vT)
    y_wkv = fromT(yT)
    r_k_row = rwkv_r_k.reshape(1, RW)
    post_params = [rwkv_ln_g, rwkv_ln_b, r_k_row, hm]
    (o_rwkv,) = rw("rwkv_post", _rwkv_post, R=256,
                   tiled=[(y_wkv, RW, 0), (r_, RW, 0), (k_, RW, 0), (v_, RW, 0), (g_, RW, 0)],
                   full=post_params, out_tiled=[(RW, MXU_DTYPE)])
    y_a = _mm([o_rwkv], [full["w_out_rwkv"]], F32, "out_rwkv")

    expand = jnp.kron(jnp.eye(SP, dtype=F32), jnp.ones((1, SGC), F32))
    s5_in = (s5_a_re[0], s5_a_im[0], s5_log_dt[0].reshape(NG, 1), s5_b_re[0].reshape(NG, SP * SGC),
             s5_b_im[0].reshape(NG, SP * SGC), expand)
    ab_re, ab_im, bb_re, bb_im = _s5_disc(*s5_in)
    eye8 = jnp.eye(8, dtype=F32)

    def blockdiag_in(bb):
        t = bb.reshape(NSG, 8, SP, SGC)
        return jnp.einsum("ab,sapc->sacbp", eye8, t).reshape(NSG, 128, 512)

    def blockdiag_out(cc):
        t = cc.reshape(NSG, 8, SGC, SP)
        return jnp.einsum("ab,sacp->sapbc", eye8, t).reshape(NSG, 512, 128)

    wb = jnp.concatenate([blockdiag_in(bb_re), blockdiag_in(bb_im)], axis=2).astype(MXU_DTYPE)
    wc = jnp.concatenate([blockdiag_out(s5_c_re[0]), -blockdiag_out(s5_c_im[0])], axis=1).astype(MXU_DTYPE)
    ab = jnp.stack([ab_re.reshape(NST), ab_im.reshape(NST)])
    y_ssm, s5_st = _s5_fwd(u, wb, wc, ab, Bl, S)
    (s5o,) = rw("s5_post", _s5_post, R=256, tiled=[(y_ssm, SW, 0), (u, SW, 0)], full=[s5_d],
                out_tiled=[(SW, MXU_DTYPE)])
    z = _mm([s5o], [full["w_glu"]], F32, "glu")
    mix_tiled = [(gates, D, 0), (gates, D, 1), (y_a, D, 0), (z, D, 0), (z, D, 1)]
    (mixed_in,) = rw("mix", _mix, R=256, tiled=mix_tiled, out_tiled=[(D, MXU_DTYPE)])
    mixed = _mm([mixed_in], [full["w_out"]], F32, "out_proj")

    def norm2_fwd(xv, mx, gt, sc, sh, g):
        x1 = xv + gt * mx
        return x1, _norm_mod(x1, g, sc, sh)

    x1, h2 = rw("norm2", norm2_fwd, R=256, tiled=[(x2d, D, 0), (mixed, D, 0)],
                batch=[(mod, D, GT1), (mod, D, SC2), (mod, D, SH2)], full=[norm2_g],
                out_tiled=[(D, F32), (D, MXU_DTYPE)])
    up = _mm([h2], [full["w_ffn_up"]], F32, "ffn_up")
    conv_tiled = [(up, DFF, 0), (up, DFF, 1)]
    conv_full = [conv_wg, conv_wu, conv_bg, conv_bu]

    def act_fwd(*a):
        return _silu_gate(*_conv_act(*a))

    (act,) = rw("ffn_act", act_fwd, R=128, tiled=conv_tiled, prev=conv_tiled, full=conv_full,
                out_tiled=[(DFF, MXU_DTYPE)])
    ffn = _mm([act], [full["w_ffn_down"]], F32, "ffn_down")

    def head(x1v, fv, tv, gt, g):
        x2 = x1v + gt * fv
        y, vjp = jax.vjp(_rms, x2, g)
        e = y - tv
        dx2, dg = vjp(e * (1.0 / D))
        loss = jnp.sum(e * e, keepdims=True) * jnp.ones((1, LANES), F32)
        return dx2, dx2 * gt, jnp.sum(dx2 * fv, axis=0, keepdims=True), dg.reshape(1, D), loss

    dx2, d_ffn, d_gt2, g_norm_f, loss_acc = rw(
        "head", head, R=256, tiled=[(x1, D, 0), (ffn, D, 0), (tgt, D, 0)], batch=[(mod, D, GT2)],
        full=[norm_f_g.reshape(1, D)], out_tiled=[(D, F32), (D, MXU_DTYPE)], out_batch=[D],
        out_acc=[(1, D), (1, LANES)])
    loss = lax.psum(0.5 / D * loss_acc[0, 0], ("x", "y", "c"))

    tr = lambda wmat: wmat.T
    d_act = _mm([d_ffn], [tr(full["w_ffn_down"])], F32, "d_act")
    g_w_ffn_down = _mm_tn(act, d_ffn, "g_ffn_down")

    def act_bwd(ug, uu, dact, hg, hu, wg, wu, bg, bu):
        gate, upv = _conv_act(ug, uu, hg, hu, wg, wu, bg, bu)
        _, vjp_s = jax.vjp(_silu_gate, gate, upv)
        d_gate, d_upv = vjp_s(dact)
        def taps(dh, xv, h):
            return [jnp.sum(dh * _shift_down(xv, h, 2), axis=0, keepdims=True),
                    jnp.sum(dh * _shift_down(xv, h, 1), axis=0, keepdims=True),
                    jnp.sum(dh * xv, axis=0, keepdims=True), jnp.sum(dh, axis=0, keepdims=True)]
        return (d_gate, d_upv, *taps(d_gate, ug, hg), *taps(d_upv, uu, hu))

    dh_g, dh_u, *tapg = rw(
        "ffn_act_bwd", act_bwd, R=128, tiled=conv_tiled + [(d_act, DFF, 0)], prev=conv_tiled, full=conv_full,
        out_tiled=[(DFF, F32), (DFF, F32)], out_acc=[(1, DFF)] * 8)
    g_cw_g, g_cb_g = jnp.concatenate(tapg[0:3], axis=0), tapg[3]
    g_cw_u, g_cb_u = jnp.concatenate(tapg[4:7], axis=0), tapg[7]

    def conv_t(dg, du_, ng, nu, wg, wu):
        def ct(d, n, w):
            return w[2:3] * d + w[1:2] * _shift_up(d, n, 1) + w[0:1] * _shift_up(d, n, 2)
        return jnp.concatenate([ct(dg, ng, wg), ct(du_, nu, wu)], axis=1)

    (d_up,) = rw("conv_bwd", conv_t, R=128, tiled=[(dh_g, DFF, 0), (dh_u, DFF, 0)],
                 nxt=[(dh_g, DFF, 0), (dh_u, DFF, 0)], full=[conv_wg, conv_wu], out_tiled=[(2 * DFF, MXU_DTYPE)])
    d_h2 = _mm([d_up], [tr(full["w_ffn_up"])], F32, "d_h2")
    g_w_ffn_up = _mm_tn(h2, d_up, "g_ffn_up")

    def norm2_bwd(x1v, dh2, dx2v, mx, gt, sc, sh, g):
        _, vjp = jax.vjp(_norm_mod, x1v, g, sc, sh)
        dxn, dg, dsc, dsh = vjp(dh2)
        dx1 = dx2v + dxn
        return dx1, dx1 * gt, jnp.sum(dx1 * mx, axis=0, keepdims=True), dsc, dsh, dg

    dx1, d_mixed, d_gt1, d_sc2, d_sh2, g_norm2 = rw(
        "norm2_bwd", norm2_bwd, R=256, tiled=[(x1, D, 0), (d_h2, D, 0), (dx2, D, 0), (mixed, D, 0)],
        batch=[(mod, D, GT1), (mod, D, SC2), (mod, D, SH2)], full=[norm2_g],
        out_tiled=[(D, F32), (D, MXU_DTYPE)], out_batch=[D, D, D], out_acc=[(1, D)])

    d_mixed_in = _mm([d_mixed], [tr(full["w_out"])], F32, "d_mixed_in")
    g_w_out = _mm_tn(mixed_in, d_mixed, "g_w_out")

    def mix_bwd(ga, gb, ya, za, zb, dm):
        _, vjp = jax.vjp(_mix, ga, gb, ya, za, zb)
        dga, dgb, dya, dza, dzb = vjp(dm)
        return jnp.concatenate([dga, dgb], axis=1), dya, jnp.concatenate([dza, dzb], axis=1)

    d_gates, d_ya, d_z = rw("mix_bwd", mix_bwd, R=256, tiled=mix_tiled + [(d_mixed_in, D, 0)],
                            out_tiled=[(2 * D, MXU_DTYPE), (D, MXU_DTYPE), (2 * D, MXU_DTYPE)])
    d_o_rwkv = _mm([d_ya], [tr(full["w_out_rwkv"])], F32, "d_o_rwkv")
    g_w_out_rwkv = _mm_tn(o_rwkv, d_ya, "g_out_rwkv")
    d_s5o = _mm([d_z], [tr(full["w_glu"])], F32, "d_s5o")
    g_w_glu = _mm_tn(s5o, d_z, "g_glu")

    def s5_post_bwd(ys, uv, ds, dd):
        _, vjp = jax.vjp(_s5_post, ys, uv, dd)
        return vjp(ds)

    d_yssm, d_u_direct, g_s5_d = rw("s5_post_bwd", s5_post_bwd, R=256,
                                    tiled=[(y_ssm, SW, 0), (u, SW, 0), (d_s5o, SW, 0)], full=[s5_d],
                                    out_tiled=[(SW, F32), (SW, F32)], out_acc=[(1, SW)])
    d_u_ssm, d_wb, d_wc, d_ab = _s5_bwd(u, d_yssm, wb, wc, ab, s5_st, Bl, S)

    def diag_in(dw):
        t = dw.reshape(NSG, 8, SGC, 8, SP)
        return jnp.einsum("ab,sacbp->sapc", eye8, t).reshape(NG, SP * SGC)

    def diag_out(dw):
        t = dw.reshape(NSG, 8, SP, 8, SGC)
        return jnp.einsum("ab,sapbc->sacp", eye8, t).reshape(NG, SGC, SP)

    g_s5_c_re = diag_out(d_wc[:, :512])
    g_s5_c_im = -diag_out(d_wc[:, 512:])
    disc_cts = (d_ab[0].reshape(NG, SP), d_ab[1].reshape(NG, SP), diag_in(d_wb[:, :, :512]), diag_in(d_wb[:, :, 512:]))
    g_a_re, g_a_im, g_log_dt, g_b_re, g_b_im = _s5_disc_bwd(*s5_in, disc_cts)

    def post_bwd(yv, rv, kv, vv, gv, do, *pp):
        _, vjp = jax.vjp(lambda *a: _rwkv_post(*a, pp[3]), yv, rv, kv, vv, gv, *pp[:3])
        return vjp(do)

    dy_wkv, dr_b, dk_b, dv_b, dg_, g_ln_g, g_ln_b, g_r_k = rw(
        "rwkv_post_bwd", post_bwd, R=256,
        tiled=[(y_wkv, RW, 0), (r_, RW, 0), (k_, RW, 0), (v_, RW, 0), (g_, RW, 0), (d_o_rwkv, RW, 0)],
        full=post_params, out_tiled=[(RW, F32)] * 5, out_acc=[(1, RW)] * 3)
    dr3, dw3, dk3, da3, db3, dvT = _wkv_bwd(to3(r_), to3(w_), to3(k_), to3(a_), to3(b_), vT, toT(dy_wkv), ck)
    flat = lambda z: z.reshape(T, RW)

    def prep_bwd(pv, dr1, dr2, dwv, dk1, dk2, dv1, dv2, dav, dbv, dgv, ph, mu, *pp):
        prev = _shift_down(pv, ph, 1)
        ps = pv + (prev - pv) * mu
        _, vjp = jax.vjp(lambda *q: _rwkv_prep(*q, pp[7]), *_split_ps(ps), *pp[:7])
        grads = vjp((dr1 + dr2, dwv, dk1 + dk2, dv1 + dv2, dav, dbv, dgv))
        dps = jnp.concatenate(grads[:5], axis=1)
        return (dps,) + tuple(grads[5:]) + (jnp.sum(dps * (prev - pv), axis=0, keepdims=True),)

    prep_outs = rw(
        "rwkv_prep_bwd", prep_bwd, R=256,
        tiled=[(p, SHIFT, 0), (flat(dr3), RW, 0), (dr_b, RW, 0), (flat(dw3), RW, 0), (flat(dk3), RW, 0), (dk_b, RW, 0),
               (fromT(dvT), RW, 0), (dv_b, RW, 0), (flat(da3), RW, 0), (flat(db3), RW, 0), (dg_, RW, 0)],
        prev=[(p, SHIFT, 0)], full=[mu_shift] + prep_params,
        out_tiled=[(SHIFT, F32)],
        out_acc=[(1, RW), (LW + LA, RW), (1, RW), (LW + LA, RW), (LG, RW), (1, RW), (1, RW), (1, SHIFT)])
    d_ps, g_w0, g_w_up_p, g_a0, g_a_up_p, g_g_up, g_k_k, g_k_a, g_mu = prep_outs

    def shift_bwd(dps, nx, mu):
        return dps * (1.0 - mu) + _shift_up(dps * mu, nx * mu, 1)

    (d_p,) = rw("shift_bwd", shift_bwd, R=256, tiled=[(d_ps, SHIFT, 0)], nxt=[(d_ps, SHIFT, 0)], full=[mu_shift],
                out_tiled=[(SHIFT, MXU_DTYPE)])
    (d_u,) = rw("d_u", lambda a1, a2: a1 + a2, R=256, tiled=[(d_u_direct, SW, 0), (d_u_ssm, SW, 0)],
                out_tiled=[(SW, MXU_DTYPE)])
    d_h1 = _mm([d_p, d_u, d_gates], [tr(w_p), tr(w_u), tr(w_g)], F32, "d_h1")
    g_w_in = jnp.concatenate([_mm_tn(h1, d_p, "g_w_p"), _mm_tn(h1, d_u, "g_w_u"), _mm_tn(h1, d_gates, "g_w_g")], axis=1)

    def norm1_bwd(xv, dh1, dx1v, sc, sh, g):
        _, vjp = jax.vjp(_norm_mod, xv, g, sc, sh)
        dxn, dg, dsc, dsh = vjp(dh1)
        return dx1v + dxn, dsc, dsh, dg

    grad_x, d_sc1, d_sh1, g_norm1 = rw(
        "norm1_bwd", norm1_bwd, R=256, tiled=[(x2d, D, 0), (d_h1, D, 0), (dx1, D, 0)],
        batch=[(mod, D, SC1), (mod, D, SH1)], full=[norm1_g], out_tiled=[(D, F32)], out_batch=[D, D], out_acc=[(1, D)])

    dmod = jnp.concatenate([d_sh1, d_sc1, d_gt1, d_sh2, d_sc2, d_gt2], axis=2).reshape(Bl, 6 * D)
    dmod_all = _gather8(dmod, "gather_dmod").reshape(8 * Bl, 6 * D)
    dmod_cols = lax.dynamic_slice_in_dim(dmod_all, chip * ncol, ncol, 1)
    g_w_ada, g_b_ada = _ada_bwd(c_all, dmod_cols, dmod_all)

    small = {"norm1_g": g_norm1, "mu_shift": g_mu, "rwkv_w0": g_w0, "rwkv_a0": g_a0, "rwkv_k_k": g_k_k,
             "rwkv_k_a": g_k_a, "rwkv_r_k": g_r_k, "rwkv_ln_g": g_ln_g, "rwkv_ln_b": g_ln_b, "s5_a_re": g_a_re,
             "s5_a_im": g_a_im, "s5_log_dt": g_log_dt, "s5_b_re": g_b_re, "s5_b_im": g_b_im, "s5_c_re": g_s5_c_re,
             "s5_c_im": g_s5_c_im, "s5_d": g_s5_d, "norm2_g": g_norm2,
             "ffn_conv_b": jnp.concatenate([g_cb_g, g_cb_u], axis=1), "norm_f_g": g_norm_f}
    small_names = list(small)
    g_conv_w = jnp.concatenate([g_cw_g, g_cw_u], axis=1)
    shard_small = {"rwkv_w_up": g_w_up_p[:LW], "rwkv_a_up": g_a_up_p[LW:], "rwkv_g_up": g_g_up, "ffn_conv_w": g_conv_w}
    parts = [small[n] for n in small_names] + [_to_shards(shard_small[n], ax) for n, _, ax in BIG_SMALL]
    spack = _pack_rows(parts, F32, SUBLANES)
    s_all = _gather8(spack, "gather_gsmall")
    s_sum = _sum_slots(s_all, F32, "sum_gsmall").reshape(-1)
    grads = {}
    off = 0
    for n in small_names:
        grads[n] = s_sum[off:off + W[n].size].reshape(W[n].shape)
        off += W[n].size
    for n, shape, axis in BIG_SMALL:
        ss = _shard_shape(shape, axis)
        k4 = 4 * math.prod(ss)
        sh4 = s_sum[off:off + k4].reshape(4, math.prod(ss))
        grads[n] = lax.dynamic_index_in_dim(sh4, chip, 0, keepdims=False).reshape((1,) + ss)
        off += k4

    big_g = {"w_in": g_w_in, "w_out_rwkv": g_w_out_rwkv, "w_glu": g_w_glu, "w_out": g_w_out,
             "w_ffn_up": g_w_ffn_up, "w_ffn_down": g_w_ffn_down}
    gsh = jnp.concatenate([_to_shards(big_g[n], ax) for n, _, ax in BIG], axis=1)
    n_shard = gsh.shape[1]
    n_pad = big_pack.shape[0] * PACK_COLS
    gsh = jnp.pad(gsh, ((0, 0), (0, n_pad - n_shard))).astype(MXU_DTYPE).reshape(4, 2, half_rows, PACK_COLS)
    gsh = jnp.swapaxes(gsh, 0, 1)
    pair = _exchange(gsh, PAIR_FLIPS, 2, lambda me, peer: peer[2], lambda me: me[2], "rs_pair")
    pair = pair.reshape(2, 4 * half_rows, PACK_COLS)
    chip_part = _sum_slots(pair, MXU_DTYPE, "rs_pair_sum").reshape(4, half_rows, PACK_COLS)
    recv = _alltoall_chips(chip_part, "rs_chips")
    g_half = _sum_slots(recv, F32, "rs_chip_sum")
    g_both = _gather_pair(g_half, "rs_share").reshape(-1)
    off = 0
    for n, shape, axis in BIG:
        ss = _shard_shape(shape, axis)
        grads[n] = g_both[off:off + math.prod(ss)].reshape((1,) + ss)
        off += math.prod(ss)
    grads["w_ada"] = g_w_ada[None]
    grads["b_ada"] = g_b_ada

    delta, new_m, new_v = {}, {}, {}
    to2 = lambda z: z.reshape(-1, z.shape[-1])
    for n in ["w_ada"] + [b[0] for b in BIG]:
        d_, m_, v2_ = _adamw(to2(W[n]), to2(grads[n]), to2(M[n]), to2(V[n]), "adamw_" + n)
        delta[n], new_m[n], new_v[n] = (z.reshape(W[n].shape) for z in (d_, m_, v2_))
    rest = [n for n in names if n not in delta]
    packs = [_pack_rows([src[n] for n in rest], F32, SUBLANES) for src in (W, grads, M, V)]
    d_, m_, v2_ = _adamw(*packs, "adamw_small")
    shapes = [W[n].shape for n in rest]
    for dst, z in ((delta, d_), (new_m, m_), (new_v, v2_)):
        for n, val in zip(rest, _unpack(z.reshape(-1), shapes)):
            dst[n] = val

    return (loss, grad_x.reshape(Bl, S, D), *[grads[n] for n in names], *[delta[n] for n in names],
            *[new_m[n] for n in names], *[new_v[n] for n in names])
```

```python
import functools
import math

import jax
import jax.numpy as jnp
from jax import lax
from jax.experimental import pallas as pl
from jax.experimental.pallas import tpu as pltpu

F32 = jnp.float32
BF16 = jnp.bfloat16
MXU_DTYPE = jnp.bfloat16
MESH_IDS = pl.DeviceIdType.MESH
HIGHEST = lax.Precision.HIGHEST

D = 1024
RW, NH, HD = 512, 8, 64
LW, LA, LG = 64, 64, 128
SW, SGC, NG, SP = 512, 16, 32, 64
NSG = 4
SHIFT = 3 * RW + LW + LA + LG
DFF = 2816
RMS_EPS, GN_EPS, L2_EPS = 1e-6, 64e-5, 1e-12
LR, B1, B2, ADAM_EPS, WD, STEP = 0.001, 0.9, 0.999, 1e-8, 0.01, 10
DECAY_SCALE = math.exp(-0.5)
GELU_C = math.sqrt(2.0 / math.pi)

VMEM_LIMIT = 52 * 1024 * 1024
SUBLANES, LANES = 8, 128


def _pick(n, cap):
    if n <= cap:
        return n
    best = None
    for t in range(LANES, cap + 1, LANES):
        if n % t == 0:
            best = t
    assert best is not None, (n, cap)
    return best


def _params(sem=None, vmem=VMEM_LIMIT):
    return pltpu.CompilerParams(dimension_semantics=sem, vmem_limit_bytes=vmem)


def _chip_of(p):
    return 2 * p[0] + p[1]


def _exchange(src, flips, n_out, slot_src, slot_dst, name):
    blk = src.shape[1:]
    nf = len(flips)

    def body(src_ref, out_ref, send_sems, recv_sems, loc_sem):
        me = (lax.axis_index("x"), lax.axis_index("y"), lax.axis_index("c"))
        copies = []
        for k, f in enumerate(flips):
            peer = tuple(1 - v if b else v for v, b in zip(me, f))
            cp = pltpu.make_async_remote_copy(
                src_ref=src_ref.at[slot_src(me, peer)], dst_ref=out_ref.at[slot_dst(me)],
                send_sem=send_sems.at[k], recv_sem=recv_sems.at[k],
                device_id=peer, device_id_type=MESH_IDS)
            cp.start()
            copies.append(cp)
        loc = pltpu.make_async_copy(src_ref.at[slot_src(me, me)], out_ref.at[slot_dst(me)], loc_sem)
        loc.start()
        for cp in copies:
            cp.wait_recv()
        for cp in copies:
            cp.wait_send()
        loc.wait()

    return pl.pallas_call(
        body, name=name,
        out_shape=jax.ShapeDtypeStruct((n_out,) + blk, src.dtype),
        in_specs=[pl.BlockSpec(memory_space=pl.ANY)],
        out_specs=pl.BlockSpec(memory_space=pl.ANY),
        scratch_shapes=[pltpu.SemaphoreType.DMA((nf,)), pltpu.SemaphoreType.DMA((nf,)), pltpu.SemaphoreType.DMA(())],
    )(src)


CHIP_FLIPS = ((1, 0, 0), (0, 1, 0), (1, 1, 0))
PAIR_FLIPS = ((0, 0, 1),)


def _gather_chips(blk, name):
    return _exchange(blk[None], CHIP_FLIPS, 4, lambda me, peer: 0, _chip_of, name)


def _gather_pair(blk, name):
    return _exchange(blk[None], PAIR_FLIPS, 2, lambda me, peer: 0, lambda me: me[2], name)


def _gather8(blk, name):
    g4 = _gather_chips(blk, name + "_chips")
    g2 = _gather_pair(g4, name + "_pair")
    return jnp.swapaxes(g2, 0, 1).reshape((8,) + blk.shape)


def _alltoall_chips(src4, name):
    return _exchange(src4, CHIP_FLIPS, 4, lambda me, peer: _chip_of(peer), _chip_of, name)


def _mm(As, Bs, out_dtype, name, tm=512, cap=1408):
    n = len(As)
    M, N = As[0].shape[0], Bs[0].shape[1]
    tm = min(tm, M)
    tn = _pick(N, cap)

    def body(*refs):
        o = refs[2 * n]
        acc = None
        for a, b in zip(refs[:n], refs[n:2 * n]):
            d = jnp.dot(a[...].astype(MXU_DTYPE), b[...].astype(MXU_DTYPE), preferred_element_type=F32)
            acc = d if acc is None else acc + d
        o[...] = acc.astype(o.dtype)

    in_specs = [pl.BlockSpec((tm, a.shape[1]), lambda i, j: (i, 0)) for a in As]
    in_specs += [pl.BlockSpec((b.shape[0], tn), lambda i, j: (0, j)) for b in Bs]
    return pl.pallas_call(
        body, name=name, grid=(M // tm, N // tn), in_specs=in_specs,
        out_specs=pl.BlockSpec((tm, tn), lambda i, j: (i, j)),
        out_shape=jax.ShapeDtypeStruct((M, N), out_dtype),
        compiler_params=_params(("parallel", "parallel")),
    )(*As, *Bs)


def _mm_tn(A, G, name, tt=1024, cap=1024):
    T, Ka = A.shape
    N = G.shape[1]
    tt = min(tt, T)
    tk = _pick(Ka, cap)
    tn = _pick(N, cap)

    def body(a, g, o):
        @pl.when(pl.program_id(2) == 0)
        def _():
            o[...] = jnp.zeros(o.shape, F32)
        o[...] += lax.dot_general(a[...].astype(MXU_DTYPE), g[...].astype(MXU_DTYPE),
                                  (((0,), (0,)), ((), ())), preferred_element_type=F32)

    return pl.pallas_call(
        body, name=name, grid=(Ka // tk, N // tn, T // tt),
        in_specs=[pl.BlockSpec((tt, tk), lambda i, j, t: (t, i)), pl.BlockSpec((tt, tn), lambda i, j, t: (t, j))],
        out_specs=pl.BlockSpec((tk, tn), lambda i, j, t: (i, j)),
        out_shape=jax.ShapeDtypeStruct((Ka, N), F32),
        compiler_params=_params(("parallel", "parallel", "arbitrary")),
    )(A, G)


def _rowwise(name, fn, *, Bl, S, R, tiled=(), prev=(), nxt=(), batch=(), full=(),
             out_tiled=(), out_batch=(), out_acc=()):
    R = min(R, S)
    nS = S // R
    T = Bl * S
    hb = R // SUBLANES
    n_in = len(tiled) + len(prev) + len(nxt) + len(batch) + len(full)

    in_specs, args = [], []
    for a, wd, cb in tiled:
        in_specs.append(pl.BlockSpec((R, wd), lambda b, i, cb=cb: (b * nS + i, cb)))
        args.append(a)
    for a, wd, cb in prev:
        in_specs.append(pl.BlockSpec((SUBLANES, wd), lambda b, i, cb=cb: (jnp.maximum((b * nS + i) * hb - 1, 0), cb)))
        args.append(a)
    for a, wd, cb in nxt:
        in_specs.append(pl.BlockSpec((SUBLANES, wd), lambda b, i, cb=cb: (jnp.minimum((b * nS + i + 1) * hb, T // SUBLANES - 1), cb)))
        args.append(a)
    for a, wd, cb in batch:
        in_specs.append(pl.BlockSpec((1, 1, wd), lambda b, i, cb=cb: (b, 0, cb)))
        args.append(a)
    for a in full:
        in_specs.append(pl.BlockSpec(a.shape, lambda b, i, nd=a.ndim: (0,) * nd))
        args.append(a)

    out_specs, out_shape = [], []
    for C, dt in out_tiled:
        out_specs.append(pl.BlockSpec((R, C), lambda b, i: (b * nS + i, 0)))
        out_shape.append(jax.ShapeDtypeStruct((T, C), dt))
    for C in out_batch:
        out_specs.append(pl.BlockSpec((1, 1, C), lambda b, i: (b, 0, 0)))
        out_shape.append(jax.ShapeDtypeStruct((Bl, 1, C), F32))
    for shp in out_acc:
        out_specs.append(pl.BlockSpec(shp, lambda b, i, nd=len(shp): (0,) * nd))
        out_shape.append(jax.ShapeDtypeStruct(shp, F32))

    nt, npv, nnx, nbt = len(tiled), len(prev), len(nxt), len(batch)

    def body(*refs):
        b, i = pl.program_id(0), pl.program_id(1)
        ins, outs = refs[:n_in], refs[n_in:]
        vals = [r[...] for r in ins[:nt]]
        vals += [jnp.where(i > 0, r[...], jnp.zeros(r.shape, r.dtype)) for r in ins[nt:nt + npv]]
        vals += [jnp.where(i < nS - 1, r[...], jnp.zeros(r.shape, r.dtype)) for r in ins[nt + npv:nt + npv + nnx]]
        vals += [r[0] for r in ins[nt + npv + nnx:nt + npv + nnx + nbt]]
        vals += [r[...] for r in ins[nt + npv + nnx + nbt:]]
        res = fn(*vals)
        if not isinstance(res, (tuple, list)):
            res = (res,)
        k = 0
        for _ in out_tiled:
            outs[k][...] = res[k].astype(outs[k].dtype)
            k += 1
        for _ in out_batch:
            o = outs[k]

            @pl.when(i == 0)
            def _(o=o):
                o[...] = jnp.zeros(o.shape, F32)
            o[0] += res[k]
            k += 1
        for _ in out_acc:
            o = outs[k]

            @pl.when((i == 0) & (b == 0))
            def _(o=o):
                o[...] = jnp.zeros(o.shape, F32)
            o[...] += res[k]
            k += 1

    out = pl.pallas_call(
        body, name=name, grid=(Bl, nS), in_specs=in_specs, out_specs=out_specs, out_shape=out_shape,
        compiler_params=_params(("arbitrary", "arbitrary")),
    )(*args)
    return out


def _shift_down(x, halo, k):
    row = lax.broadcasted_iota(jnp.int32, x.shape, 0)
    out = pltpu.roll(x, k, 0)
    for j in range(k):
        out = jnp.where(row == j, halo[SUBLANES - k + j:SUBLANES - k + j + 1, :], out)
    return out


def _shift_up(x, halo, k):
    n = x.shape[0]
    row = lax.broadcasted_iota(jnp.int32, x.shape, 0)
    out = pltpu.roll(x, n - k, 0)
    for j in range(k):
        out = jnp.where(row == n - k + j, halo[j:j + 1, :], out)
    return out


def _dotm(a, b):
    return jnp.dot(a.astype(MXU_DTYPE), b.astype(MXU_DTYPE), preferred_element_type=F32)


def _headsum(x, hm):
    return jnp.dot(x, hm, preferred_element_type=F32, precision=HIGHEST)


def _sigmoid(x):
    return 1.0 / (1.0 + jnp.exp(-x))


def _rms(x, g):
    return x * lax.rsqrt(jnp.mean(x * x, axis=-1, keepdims=True) + RMS_EPS) * g


def _norm_mod(x, g, sc, sh):
    return _rms(x, g) * (1.0 + sc) + sh


def _split_ps(ps):
    return (ps[:, 0:RW], ps[:, RW:2 * RW], ps[:, 2 * RW:3 * RW], ps[:, 3 * RW:3 * RW + LW + LA],
            ps[:, 3 * RW + LW + LA:SHIFT])


def _rwkv_prep(r, k, v, wa, gd, w0, w_up_p, a0, a_up_p, g_up, k_k, k_a, hm):
    w_raw = w0 + _dotm(jnp.tanh(wa), w_up_p)
    decay = jnp.exp(-DECAY_SCALE * _sigmoid(w_raw))
    a = _sigmoid(a0 + _dotm(wa, a_up_p))
    g = _dotm(_sigmoid(gd), g_up)
    kk = k * k_k
    kk = kk * lax.rsqrt(_headsum(kk * kk, hm) + L2_EPS)
    k2 = k * (1.0 + (a - 1.0) * k_a)
    return r, decay, k2, v, -kk, kk * a, g


def _rwkv_post(y, r, k2, v, g, ln_g, ln_b, r_k, hm):
    mean = _headsum(y, hm) * (1.0 / HD)
    yc = y - mean
    var = _headsum(yc * yc, hm) * (1.0 / HD)
    yn = yc * lax.rsqrt(var + GN_EPS) * ln_g + ln_b
    bonus = _headsum(r * k2 * r_k, hm) * v
    return (yn + bonus) * g


def _gelu(x):
    return 0.5 * x * (1.0 + jnp.tanh(GELU_C * (x + 0.044715 * (x * x * x))))


def _s5_post(yssm, u, d):
    return _gelu(yssm + d * u)


def _mix(ga, gb, ya, za, zb):
    return _sigmoid(ga) * ya + _sigmoid(gb) * (za * _sigmoid(zb))


def _conv_act(up_g, up_u, hg, hu, w_g, w_u, b_g, b_u):
    def conv(x, h, w, b):
        return b + w[0:1] * _shift_down(x, h, 2) + w[1:2] * _shift_down(x, h, 1) + w[2:3] * x
    gate = conv(up_g, hg, w_g, b_g)
    upv = conv(up_u, hu, w_u, b_u)
    return gate, upv


def _silu_gate(gate, upv):
    return gate * _sigmoid(gate) * upv


WKV_L = 64
_NT, _NN, _TN = ((1,), (1,)), ((1,), (0,)), ((0,), (0,))


def _dotw(x, y, dims):
    return lax.dot_general(x.astype(MXU_DTYPE), y.astype(MXU_DTYPE), (dims, ((), ())), preferred_element_type=F32)


def _wkv_chunk(s0, r, w, k, v, a, b):
    L = r.shape[0]
    n2 = 2 * L
    lane_head = lax.broadcasted_iota(jnp.int32, (2, 1, 2 * HD), 2) // HD
    head_mask = (lane_head == lax.broadcasted_iota(jnp.int32, (2, 1, 2 * HD), 0)).astype(F32)
    ri = lax.broadcasted_iota(jnp.int32, (n2, n2), 0)
    ci = lax.broadcasted_iota(jnp.int32, (n2, n2), 1)
    same = (ri // L) == (ci // L)
    strict = same & ((ci % L) < (ri % L))
    incl = same & ((ci % L) <= (ri % L))
    si = lax.broadcasted_iota(jnp.int32, (2 * HD, 2 * HD), 0) // HD
    sj = lax.broadcasted_iota(jnp.int32, (2 * HD, 2 * HD), 1) // HD
    tri = (lax.broadcasted_iota(jnp.int32, (L, L), 0) >= lax.broadcasted_iota(jnp.int32, (L, L), 1)).astype(F32)

    lw = jnp.log(w)
    cum = jnp.dot(tri, lw, preferred_element_type=F32, precision=HIGHEST)
    tot = jnp.sum(lw, axis=0, keepdims=True)
    e_neg = jnp.exp(-cum)
    rem = jnp.exp(tot - cum)
    stack = lambda z: (z[None] * head_mask).reshape(n2, 2 * HD)
    dup = lambda z: jnp.broadcast_to(z[None], (2, L, 2 * HD)).reshape(n2, 2 * HD)
    a2, r2, v2 = stack(a * jnp.exp(cum - lw)), stack(r * jnp.exp(cum)), stack(v)
    b2, k2 = dup(b * e_neg), dup(k * e_neg)
    n_ab = jnp.where(strict, _dotw(a2, b2, _NT), 0.0)
    n_ak = jnp.where(strict, _dotw(a2, k2, _NT), 0.0)
    m_rb = jnp.where(incl, _dotw(r2, b2, _NT), 0.0)
    m_rk = jnp.where(incl, _dotw(r2, k2, _NT), 0.0)
    u = _dotw(a2, s0, _NT) + _dotw(n_ak, v2, _NN)
    q = n_ab
    steps = L.bit_length() - 1
    for i in range(steps):
        u = u + _dotw(q, u, _NN)
        if i < steps - 1:
            q = _dotw(q, q, _NN)
    y2 = _dotw(r2, s0, _NT) + _dotw(m_rb, u, _NN) + _dotw(m_rk, v2, _NN)
    y = jnp.sum(y2.reshape(2, L, 2 * HD), axis=0)
    upd = _dotw(u, dup(b * rem), _TN) + _dotw(v2, dup(k * rem), _TN)
    s1 = s0 * jnp.exp(tot) + jnp.where(si == sj, upd, 0.0)
    return y, s1


NPAIR = NH // 2


def _wkv_fwd(r, w, k, v, a, b, Bl, S):
    L = WKV_L
    nC = S // L

    def body(r_ref, w_ref, k_ref, v_ref, a_ref, b_ref, y_ref, ck_ref, s_ref):
        @pl.when(pl.program_id(1) == 0)
        def _():
            s_ref[...] = jnp.zeros(s_ref.shape, F32)
        for p in range(NPAIR):
            cols = slice(p * 2 * HD, (p + 1) * 2 * HD)
            s0 = s_ref[p]
            ck_ref[0, 0, p] = s0
            y, s1 = _wkv_chunk(s0, *(z[:, cols] for z in (r_ref, w_ref, k_ref, v_ref, a_ref, b_ref)))
            y_ref[:, cols] = y
            s_ref[p] = s1

    row_spec = pl.BlockSpec((L, RW), lambda bb, c: (bb * nC + c, 0))
    return pl.pallas_call(
        body, name="wkv_fwd", grid=(Bl, nC), in_specs=[row_spec] * 6,
        out_specs=[row_spec, pl.BlockSpec((1, 1, NPAIR, 2 * HD, 2 * HD), lambda bb, c: (bb, c, 0, 0, 0))],
        out_shape=[jax.ShapeDtypeStruct((Bl * S, RW), F32), jax.ShapeDtypeStruct((Bl, nC, NPAIR, 2 * HD, 2 * HD), F32)],
        scratch_shapes=[pltpu.VMEM((NPAIR, 2 * HD, 2 * HD), F32)],
        compiler_params=_params(("arbitrary", "arbitrary")),
    )(r, w, k, v, a, b)


def _wkv_bwd(r, w, k, v, a, b, dy, ck, Bl, S):
    L = WKV_L
    nC = S // L

    def body(r_ref, w_ref, k_ref, v_ref, a_ref, b_ref, dy_ref, ck_ref,
             dr_ref, dw_ref, dk_ref, dv_ref, da_ref, db_ref, ds_ref):
        @pl.when(pl.program_id(1) == 0)
        def _():
            ds_ref[...] = jnp.zeros(ds_ref.shape, F32)
        for p in range(NPAIR):
            cols = slice(p * 2 * HD, (p + 1) * 2 * HD)
            _, vjp = jax.vjp(_wkv_chunk, ck_ref[0, 0, p],
                             *(z[:, cols] for z in (r_ref, w_ref, k_ref, v_ref, a_ref, b_ref)))
            ds0, *grads = vjp((dy_ref[:, cols], ds_ref[p]))
            ds_ref[p] = ds0
            for o, g in zip((dr_ref, dw_ref, dk_ref, dv_ref, da_ref, db_ref), grads):
                o[:, cols] = g

    row_spec = pl.BlockSpec((L, RW), lambda bb, c: (bb * nC + nC - 1 - c, 0))
    rows = jax.ShapeDtypeStruct((Bl * S, RW), F32)
    return pl.pallas_call(
        body, name="wkv_bwd", grid=(Bl, nC),
        in_specs=[row_spec] * 7 + [pl.BlockSpec((1, 1, NPAIR, 2 * HD, 2 * HD), lambda bb, c: (bb, nC - 1 - c, 0, 0, 0))],
        out_specs=[row_spec] * 6, out_shape=[rows] * 6,
        scratch_shapes=[pltpu.VMEM((NPAIR, 2 * HD, 2 * HD), F32)],
        compiler_params=_params(("arbitrary", "arbitrary")),
    )(r, w, k, v, a, b, dy, ck)


NST = NG * SP


def _cmul(ar, ai, br, bi):
    return ar * br - ai * bi, ar * bi + ai * br


def _s5_tiles(are, aim, reverse):
    if reverse:
        aim = -aim
    row = lax.broadcasted_iota(jnp.int32, (SUBLANES, NST), 0)
    pw = [(are, aim)]
    for _ in range(SUBLANES - 1):
        pw.append(_cmul(pw[-1][0], pw[-1][1], are, aim))
    bc = lambda z: jnp.broadcast_to(z, (SUBLANES, NST))
    ms = []
    for kk in (1, 2, 4):
        cond = (row < SUBLANES - kk) if reverse else (row >= kk)
        ms.append((jnp.where(cond, bc(pw[kk - 1][0]), 0.0), jnp.where(cond, bc(pw[kk - 1][1]), 0.0)))
    pr = jnp.zeros((SUBLANES, NST), F32)
    pi = jnp.zeros((SUBLANES, NST), F32)
    for i in range(SUBLANES):
        n = SUBLANES - i if reverse else i + 1
        pr = jnp.where(row == i, bc(pw[n - 1][0]), pr)
        pi = jnp.where(row == i, bc(pw[n - 1][1]), pi)
    return ms, (pr, pi)


def _s5_block(re, im, ms, pc, cre, cim, sg, reverse):
    ln = slice(sg * 512, (sg + 1) * 512)
    for (mr, mi), kk in zip(ms, (1, 2, 4)):
        sh = SUBLANES - kk if reverse else kk
        sre, sim = pltpu.roll(re, sh, 0), pltpu.roll(im, sh, 0)
        tr, ti = _cmul(mr[:, ln], mi[:, ln], sre, sim)
        re, im = re + tr, im + ti
    tr, ti = _cmul(pc[0][:, ln], pc[1][:, ln], cre[:, ln], cim[:, ln])
    return re + tr, im + ti


def _s5_scan(X_ref, n_rows, ms, pc, cre, cim, reverse, visit=None, acc0=None):
    nblk = n_rows // SUBLANES

    def it(i, carry):
        cre, cim, acc = carry
        j = nblk - 1 - i if reverse else i
        rows = pl.ds(pl.multiple_of(j * SUBLANES, SUBLANES), SUBLANES)
        edge = 0 if reverse else SUBLANES - 1
        blocks, ncre, ncim = [], [], []
        for sg in range(NSG):
            lr = slice(sg * 1024, sg * 1024 + 512)
            li = slice(sg * 1024 + 512, (sg + 1) * 1024)
            re, im = _s5_block(X_ref[rows, lr], X_ref[rows, li], ms, pc, cre, cim, sg, reverse)
            X_ref[rows, lr] = re
            X_ref[rows, li] = im
            blocks.append((re, im))
            ncre.append(re[edge:edge + 1])
            ncim.append(im[edge:edge + 1])
        if visit is not None:
            acc = visit(j, blocks, acc)
        return jnp.concatenate(ncre, axis=1), jnp.concatenate(ncim, axis=1), acc

    return lax.fori_loop(0, nblk, it, (cre, cim, acc0 if acc0 is not None else 0))


def _s5_fwd(u, wb, wc, ab, Bl, S, R=256):
    R = min(R, S)
    nC = S // R

    def body(u_ref, wb_ref, wc_ref, ab_ref, y_ref, st_ref, X_ref, car_ref):
        @pl.when(pl.program_id(1) == 0)
        def _():
            car_ref[...] = jnp.zeros(car_ref.shape, F32)
        st_ref[0, 0] = car_ref[...]
        ms, pc = _s5_tiles(ab_ref[0:1], ab_ref[1:2], False)
        for sg in range(NSG):
            X_ref[:, sg * 1024:(sg + 1) * 1024] = _dotm(u_ref[:, sg * 128:(sg + 1) * 128], wb_ref[sg])
        cre, cim, _ = _s5_scan(X_ref, R, ms, pc, car_ref[0:1], car_ref[1:2], False)
        car_ref[0:1] = cre
        car_ref[1:2] = cim
        for sg in range(NSG):
            y_ref[:, sg * 128:(sg + 1) * 128] = _dotm(X_ref[:, sg * 1024:(sg + 1) * 1024], wc_ref[sg])

    return pl.pallas_call(
        body, name="s5_fwd", grid=(Bl, nC),
        in_specs=[pl.BlockSpec((R, SW), lambda b, c: (b * nC + c, 0)),
                  pl.BlockSpec(wb.shape, lambda b, c: (0, 0, 0)), pl.BlockSpec(wc.shape, lambda b, c: (0, 0, 0)),
                  pl.BlockSpec(ab.shape, lambda b, c: (0, 0))],
        out_specs=[pl.BlockSpec((R, SW), lambda b, c: (b * nC + c, 0)),
                   pl.BlockSpec((1, 1, 2, NST), lambda b, c: (b, c, 0, 0))],
        out_shape=[jax.ShapeDtypeStruct((Bl * S, SW), F32), jax.ShapeDtypeStruct((Bl, nC, 2, NST), F32)],
        scratch_shapes=[pltpu.VMEM((R, 2 * NST), F32), pltpu.VMEM((2, NST), F32)],
        compiler_params=_params(("arbitrary", "arbitrary")),
    )(u, wb, wc, ab)


def _s5_bwd(u, dy, wb, wc, ab, st, Bl, S, R=256):
    R = min(R, S)
    nC = S // R

    def body(u_ref, dy_ref, wb_ref, wc_ref, ab_ref, st_ref, du_ref, dwb_ref, dwc_ref, dab_ref,
             X_ref, G_ref, car_ref):
        first = (pl.program_id(0) == 0) & (pl.program_id(1) == 0)

        @pl.when(first)
        def _():
            dwb_ref[...] = jnp.zeros(dwb_ref.shape, F32)
            dwc_ref[...] = jnp.zeros(dwc_ref.shape, F32)
            dab_ref[...] = jnp.zeros(dab_ref.shape, F32)

        @pl.when(pl.program_id(1) == 0)
        def _():
            car_ref[...] = jnp.zeros(car_ref.shape, F32)

        are, aim = ab_ref[0:1], ab_ref[1:2]
        ms, pc = _s5_tiles(are, aim, False)
        for sg in range(NSG):
            X_ref[:, sg * 1024:(sg + 1) * 1024] = _dotm(u_ref[:, sg * 128:(sg + 1) * 128], wb_ref[sg])
        _s5_scan(X_ref, R, ms, pc, st_ref[0, 0, 0:1], st_ref[0, 0, 1:2], False)
        dyv = dy_ref[...].astype(MXU_DTYPE)
        for sg in range(NSG):
            G_ref[:, sg * 1024:(sg + 1) * 1024] = lax.dot_general(
                dyv[:, sg * 128:(sg + 1) * 128], wc_ref[sg].astype(MXU_DTYPE), (((1,), (1,)), ((), ())),
                preferred_element_type=F32)
        rms_, rpc = _s5_tiles(are, aim, True)
        row = lax.broadcasted_iota(jnp.int32, (SUBLANES, 512), 0)

        def visit(j, blocks, acc):
            before = pl.multiple_of(jnp.maximum(j - 1, 0) * SUBLANES, SUBLANES)
            prow = X_ref[pl.ds(before, SUBLANES), :][SUBLANES - 1:SUBLANES]
            rows = pl.ds(pl.multiple_of(j * SUBLANES, SUBLANES), SUBLANES)
            are_acc, aim_acc = [], []
            for sg in range(NSG):
                lr = slice(sg * 1024, sg * 1024 + 512)
                li = slice(sg * 1024 + 512, (sg + 1) * 1024)
                ln = slice(sg * 512, (sg + 1) * 512)
                pre = jnp.where(j > 0, prow[:, lr], st_ref[0, 0, 0:1, ln])
                pim = jnp.where(j > 0, prow[:, li], st_ref[0, 0, 1:2, ln])
                xre = jnp.where(row == 0, pre, pltpu.roll(X_ref[rows, lr], 1, 0))
                xim = jnp.where(row == 0, pim, pltpu.roll(X_ref[rows, li], 1, 0))
                dre, dim = blocks[sg]
                are_acc.append(dre * xre + dim * xim)
                aim_acc.append(dim * xre - dre * xim)
            return acc[0] + jnp.concatenate(are_acc, axis=1), acc[1] + jnp.concatenate(aim_acc, axis=1)

        zero = jnp.zeros((SUBLANES, NST), F32)
        cre, cim, acc = _s5_scan(G_ref, R, rms_, rpc, car_ref[0:1], car_ref[1:2], True, visit, (zero, zero))
        car_ref[0:1] = cre
        car_ref[1:2] = cim
        dab_ref[0:1] += jnp.sum(acc[0], axis=0, keepdims=True)
        dab_ref[1:2] += jnp.sum(acc[1], axis=0, keepdims=True)
        uv = u_ref[...].astype(MXU_DTYPE)
        for sg in range(NSG):
            cs = slice(sg * 1024, (sg + 1) * 1024)
            us = slice(sg * 128, (sg + 1) * 128)
            gx = G_ref[:, cs].astype(MXU_DTYPE)
            dwb_ref[sg] += lax.dot_general(uv[:, us], gx, (((0,), (0,)), ((), ())), preferred_element_type=F32)
            dwc_ref[sg] += lax.dot_general(X_ref[:, cs].astype(MXU_DTYPE), dyv[:, us], (((0,), (0,)), ((), ())),
                                           preferred_element_type=F32)
            du_ref[:, us] = lax.dot_general(gx, wb_ref[sg].astype(MXU_DTYPE), (((1,), (1,)), ((), ())),
                                            preferred_element_type=F32)

    rmap = lambda b, c: (b * nC + nC - 1 - c, 0)
    return pl.pallas_call(
        body, name="s5_bwd", grid=(Bl, nC),
        in_specs=[pl.BlockSpec((R, SW), rmap), pl.BlockSpec((R, SW), rmap),
                  pl.BlockSpec(wb.shape, lambda b, c: (0, 0, 0)), pl.BlockSpec(wc.shape, lambda b, c: (0, 0, 0)),
                  pl.BlockSpec(ab.shape, lambda b, c: (0, 0)),
                  pl.BlockSpec((1, 1, 2, NST), lambda b, c: (b, nC - 1 - c, 0, 0))],
        out_specs=[pl.BlockSpec((R, SW), rmap), pl.BlockSpec(wb.shape, lambda b, c: (0, 0, 0)),
                   pl.BlockSpec(wc.shape, lambda b, c: (0, 0, 0)), pl.BlockSpec((2, NST), lambda b, c: (0, 0))],
        out_shape=[jax.ShapeDtypeStruct((Bl * S, SW), F32), jax.ShapeDtypeStruct(wb.shape, F32),
                   jax.ShapeDtypeStruct(wc.shape, F32), jax.ShapeDtypeStruct((2, NST), F32)],
        scratch_shapes=[pltpu.VMEM((R, 2 * NST), F32), pltpu.VMEM((R, 2 * NST), F32), pltpu.VMEM((2, NST), F32)],
        compiler_params=_params(("arbitrary", "arbitrary")),
    )(u, dy, wb, wc, ab, st)


def _s5_disc_math(a_re, a_im, log_dt, b_re, b_im, expand):
    dt = jnp.exp(log_dt)
    z_re, z_im = a_re * dt, a_im * dt
    mag = jnp.exp(z_re)
    ab_re, ab_im = mag * jnp.cos(z_im), mag * jnp.sin(z_im)
    den = a_re * a_re + a_im * a_im
    q_re = ((ab_re - 1.0) * a_re + ab_im * a_im) / den
    q_im = (ab_im * a_re - (ab_re - 1.0) * a_im) / den
    qe_re = jnp.dot(q_re, expand, preferred_element_type=F32, precision=HIGHEST)
    qe_im = jnp.dot(q_im, expand, preferred_element_type=F32, precision=HIGHEST)
    return ab_re, ab_im, qe_re * b_re - qe_im * b_im, qe_re * b_im + qe_im * b_re


def _whole(shape):
    return pl.BlockSpec(shape, lambda nd=len(shape): (0,) * nd)


def _s5_disc(a_re, a_im, log_dt, b_re, b_im, expand):
    def body(a, b, c, d, e, f, o0, o1, o2, o3):
        res = _s5_disc_math(a[...], b[...], c[...], d[...], e[...], f[...])
        for o, v in zip((o0, o1, o2, o3), res):
            o[...] = v
    ins = (a_re, a_im, log_dt, b_re, b_im, expand)
    outs = [jax.ShapeDtypeStruct(a_re.shape, F32)] * 2 + [jax.ShapeDtypeStruct(b_re.shape, F32)] * 2
    return pl.pallas_call(body, name="s5_disc", in_specs=[_whole(x.shape) for x in ins],
                          out_specs=[_whole(o.shape) for o in outs], out_shape=outs)(*ins)


def _s5_disc_bwd(a_re, a_im, log_dt, b_re, b_im, expand, cts):
    def body(a, b, c, d, e, f, g0, g1, g2, g3, o0, o1, o2, o3, o4):
        fn = lambda *p: _s5_disc_math(*p, f[...])
        _, vjp = jax.vjp(fn, a[...], b[...], c[...], d[...], e[...])
        for o, v in zip((o0, o1, o2, o3, o4), vjp((g0[...], g1[...], g2[...], g3[...]))):
            o[...] = v
    ins = (a_re, a_im, log_dt, b_re, b_im, expand) + tuple(cts)
    outs = [jax.ShapeDtypeStruct(x.shape, F32) for x in (a_re, a_im, log_dt, b_re, b_im)]
    return pl.pallas_call(body, name="s5_disc_bwd", in_specs=[_whole(x.shape) for x in ins],
                          out_specs=[_whole(o.shape) for o in outs], out_shape=outs)(*ins)


def _ada_fwd(c_all, w_shard, b_shard):
    def body(c_ref, w_ref, b_ref, o_ref):
        cv = c_ref[...]
        o_ref[...] = _dotm(cv * _sigmoid(cv), w_ref[...]) + b_ref[...]
    n = w_shard.shape[1]
    return pl.pallas_call(
        body, name="ada_fwd", in_specs=[_whole(c_all.shape), _whole(w_shard.shape), _whole(b_shard.shape)],
        out_specs=_whole((c_all.shape[0], n)), out_shape=jax.ShapeDtypeStruct((c_all.shape[0], n), F32),
        compiler_params=_params(),
    )(c_all, w_shard, b_shard)


def _ada_bwd(c_all, dmod_cols, dmod_all):
    def body(c_ref, dc_ref, da_ref, gw_ref, gb_ref):
        cv = c_ref[...]
        gw_ref[...] = lax.dot_general((cv * _sigmoid(cv)).astype(MXU_DTYPE), dc_ref[...].astype(MXU_DTYPE),
                                      (((0,), (0,)), ((), ())), preferred_element_type=F32)
        gb_ref[...] = jnp.sum(da_ref[...], axis=0, keepdims=True)
    n = dmod_cols.shape[1]
    return pl.pallas_call(
        body, name="ada_bwd", in_specs=[_whole(c_all.shape), _whole(dmod_cols.shape), _whole(dmod_all.shape)],
        out_specs=[_whole((D, n)), _whole((1, dmod_all.shape[1]))],
        out_shape=[jax.ShapeDtypeStruct((D, n), F32), jax.ShapeDtypeStruct((1, dmod_all.shape[1]), F32)],
        compiler_params=_params(),
    )(c_all, dmod_cols, dmod_all)


def _rows_block(n_rows, cap=512):
    if n_rows <= cap:
        return n_rows
    for t in range(cap - cap % SUBLANES, 0, -SUBLANES):
        if n_rows % t == 0:
            return t
    return n_rows


def _adamw(w, g, m, v, name):
    rows, cols = w.shape
    tr = _rows_block(rows, max(SUBLANES, (1 << 19) // max(cols, 1) // SUBLANES * SUBLANES))

    def body(w_ref, g_ref, m_ref, v_ref, d_ref, nm_ref, nv_ref):
        gv = g_ref[...]
        nm = B1 * m_ref[...] + (1.0 - B1) * gv
        nv = B2 * v_ref[...] + (1.0 - B2) * (gv * gv)
        m_hat = nm / (1.0 - B1 ** STEP)
        v_hat = nv / (1.0 - B2 ** STEP)
        d_ref[...] = -LR * (m_hat / (jnp.sqrt(v_hat) + ADAM_EPS) + WD * w_ref[...])
        nm_ref[...] = nm
        nv_ref[...] = nv

    spec = pl.BlockSpec((tr, cols), lambda i: (i, 0))
    sd = jax.ShapeDtypeStruct((rows, cols), F32)
    return pl.pallas_call(body, name=name, grid=(rows // tr,), in_specs=[spec] * 4, out_specs=[spec] * 3,
                          out_shape=[sd] * 3, compiler_params=_params(("parallel",)))(w, g, m, v)


def _sum_slots(x, out_dtype, name):
    n, rows, cols = x.shape
    tr = _rows_block(rows)

    def body(x_ref, o_ref):
        acc = x_ref[0].astype(F32)
        for j in range(1, n):
            acc = acc + x_ref[j].astype(F32)
        o_ref[...] = acc.astype(o_ref.dtype)

    return pl.pallas_call(
        body, name=name, grid=(rows // tr,), in_specs=[pl.BlockSpec((n, tr, cols), lambda i: (0, i, 0))],
        out_specs=pl.BlockSpec((tr, cols), lambda i: (i, 0)), out_shape=jax.ShapeDtypeStruct((rows, cols), out_dtype),
        compiler_params=_params(("parallel",)))(x)


PACK_COLS = 1024


def _pack_rows(parts, dtype, row_mult):
    flat = jnp.concatenate([p.reshape(-1).astype(dtype) for p in parts])
    per = PACK_COLS * row_mult
    n = -(-flat.shape[0] // per) * per
    flat = jnp.pad(flat, (0, n - flat.shape[0]))
    return flat.reshape(n // PACK_COLS, PACK_COLS)


def _unpack(flat, shapes):
    out, off = [], 0
    for s in shapes:
        n = math.prod(s)
        out.append(flat[off:off + n].reshape(s))
        off += n
    return out


def _col_shards(g):
    r, C = g.shape
    return g.reshape(r, 4, C // 4).transpose(1, 0, 2).reshape(4, r * (C // 4))


def _row_shards(g):
    r, C = g.shape
    return g.reshape(4, (r // 4) * C)


def _from_col_shards(x, r, C):
    return x.reshape(4, r, C // 4).transpose(1, 0, 2).reshape(r, C)


BIG = (("w_in", (D, SHIFT + SW + 2 * D), 1), ("w_out_rwkv", (RW, D), 1), ("w_glu", (SW, 2 * D), 1),
       ("w_out", (D, D), 0), ("w_ffn_up", (D, 2 * DFF), 1), ("w_ffn_down", (DFF, D), 0))
BIG_SMALL = (("rwkv_w_up", (LW, RW), 1), ("rwkv_a_up", (LA, RW), 1), ("rwkv_g_up", (LG, RW), 1),
             ("ffn_conv_w", (3, 2 * DFF), 1))


def _shard_shape(shape, axis):
    return (shape[0] // 4, shape[1]) if axis == 0 else (shape[0], shape[1] // 4)


def _to_shards(g, axis):
    return _row_shards(g) if axis == 0 else _col_shards(g)


def _from_shards(x, shape, axis):
    return x.reshape(shape) if axis == 0 else _from_col_shards(x, *shape)


def kernel(x, c, w_ada, b_ada, norm1_g, w_in, mu_shift, rwkv_w0, rwkv_w_up, rwkv_a0, rwkv_a_up, rwkv_g_up, rwkv_k_k, rwkv_k_a, rwkv_r_k, rwkv_ln_g, rwkv_ln_b, w_out_rwkv, s5_a_re, s5_a_im, s5_log_dt, s5_b_re, s5_b_im, s5_c_re, s5_c_im, s5_d, w_glu, w_out, norm2_g, w_ffn_up, ffn_conv_w, ffn_conv_b, w_ffn_down, norm_f_g, loss_target, m_w_ada, m_b_ada, m_norm1_g, m_w_in, m_mu_shift, m_rwkv_w0, m_rwkv_w_up, m_rwkv_a0, m_rwkv_a_up, m_rwkv_g_up, m_rwkv_k_k, m_rwkv_k_a, m_rwkv_r_k, m_rwkv_ln_g, m_rwkv_ln_b, m_w_out_rwkv, m_s5_a_re, m_s5_a_im, m_s5_log_dt, m_s5_b_re, m_s5_b_im, m_s5_c_re, m_s5_c_im, m_s5_d, m_w_glu, m_w_out, m_norm2_g, m_w_ffn_up, m_ffn_conv_w, m_ffn_conv_b, m_w_ffn_down, m_norm_f_g, v_w_ada, v_b_ada, v_norm1_g, v_w_in, v_mu_shift, v_rwkv_w0, v_rwkv_w_up, v_rwkv_a0, v_rwkv_a_up, v_rwkv_g_up, v_rwkv_k_k, v_rwkv_k_a, v_rwkv_r_k, v_rwkv_ln_g, v_rwkv_ln_b, v_w_out_rwkv, v_s5_a_re, v_s5_a_im, v_s5_log_dt, v_s5_b_re, v_s5_b_im, v_s5_c_re, v_s5_c_im, v_s5_d, v_w_glu, v_w_out, v_norm2_g, v_w_ffn_up, v_ffn_conv_w, v_ffn_conv_b, v_w_ffn_down, v_norm_f_g):
    names = ["w_ada", "b_ada", "norm1_g", "w_in", "mu_shift", "rwkv_w0", "rwkv_w_up", "rwkv_a0", "rwkv_a_up",
             "rwkv_g_up", "rwkv_k_k", "rwkv_k_a", "rwkv_r_k", "rwkv_ln_g", "rwkv_ln_b", "w_out_rwkv", "s5_a_re",
             "s5_a_im", "s5_log_dt", "s5_b_re", "s5_b_im", "s5_c_re", "s5_c_im", "s5_d", "w_glu", "w_out", "norm2_g",
             "w_ffn_up", "ffn_conv_w", "ffn_conv_b", "w_ffn_down", "norm_f_g"]
    env = dict(locals())
    W = {n: env[n] for n in names}
    M = {n: env["m_" + n] for n in names}
    V = {n: env["v_" + n] for n in names}

    Bl, S, _ = x.shape
    T = Bl * S
    ix, iy, ic = lax.axis_index("x"), lax.axis_index("y"), lax.axis_index("c")
    chip = 2 * ix + iy
    dev = 2 * chip + ic
    rw = functools.partial(_rowwise, Bl=Bl, S=S)

    big_pack = _pack_rows([W[n][0] for n, _, _ in BIG], MXU_DTYPE, 2 * SUBLANES * 2)
    half_rows = big_pack.shape[0] // 2
    my_half = lax.dynamic_slice_in_dim(big_pack, ic * half_rows, half_rows, 0)
    big_all = _gather8(my_half, "gather_w").reshape(4, -1)
    small_pack = _pack_rows([c] + [W[n][0] for n, _, _ in BIG_SMALL], F32, SUBLANES)
    small_all = _gather8(small_pack, "gather_small").reshape(8, -1)
    c_all = small_all[:, :Bl * D].reshape(8 * Bl, D)
    small_chip = small_all[0::2, Bl * D:]

    full = {}
    off = 0
    for n, shape, axis in BIG:
        ss = _shard_shape(shape, axis)
        full[n] = _from_shards(big_all[:, off:off + math.prod(ss)], shape, axis)
        off += math.prod(ss)
    off = 0
    for n, shape, axis in BIG_SMALL:
        ss = _shard_shape(shape, axis)
        full[n] = _from_shards(small_chip[:, off:off + math.prod(ss)], shape, axis)
        off += math.prod(ss)
    w_p, w_u, w_g = full["w_in"][:, :SHIFT], full["w_in"][:, SHIFT:SHIFT + SW], full["w_in"][:, SHIFT + SW:]
    zeros_l = jnp.zeros((LW, RW), F32)
    w_up_p = jnp.concatenate([full["rwkv_w_up"], zeros_l], axis=0)
    a_up_p = jnp.concatenate([zeros_l, full["rwkv_a_up"]], axis=0)
    g_up = full["rwkv_g_up"]
    conv_w = full["ffn_conv_w"]
    conv_wg, conv_wu = conv_w[:, :DFF], conv_w[:, DFF:]
    conv_bg, conv_bu = ffn_conv_b[:, :DFF], ffn_conv_b[:, DFF:]
    hm = jnp.kron(jnp.eye(NH, dtype=F32), jnp.ones((HD, HD), F32))

    ncol = 6 * D // 4
    b_ada_cols = lax.dynamic_slice_in_dim(b_ada, chip * ncol, ncol, 1)
    mod_part = _ada_fwd(c_all, w_ada[0], b_ada_cols)
    mod4 = _gather_chips(mod_part, "gather_mod")
    mod = lax.dynamic_slice_in_dim(mod4, dev * Bl, Bl, 1).transpose(1, 0, 2).reshape(Bl, 1, 6 * D)
    SH1, SC1, GT1, SH2, SC2, GT2 = range(6)

    x2d = x.reshape(T, D)
    tgt = loss_target.reshape(T, D)

    (h1,) = rw("norm1", lambda xv, sc, sh, g: _norm_mod(xv, g, sc, sh), R=256, tiled=[(x2d, D, 0)],
               batch=[(mod, D, SC1), (mod, D, SH1)], full=[norm1_g], out_tiled=[(D, MXU_DTYPE)])
    p = _mm([h1], [w_p], F32, "proj_p")
    u = _mm([h1], [w_u], F32, "proj_u")
    gates = _mm([h1], [w_g], F32, "proj_g")

    prep_params = [rwkv_w0, w_up_p, rwkv_a0, a_up_p, g_up, rwkv_k_k, rwkv_k_a, hm]

    def prep_fwd(pv, ph, mu, *pp):
        ps = pv + (_shift_down(pv, ph, 1) - pv) * mu
        return _rwkv_prep(*_split_ps(ps), *pp)

    r_, w_, k_, v_, a_, b_, g_ = rw("rwkv_prep", prep_fwd, R=256, tiled=[(p, SHIFT, 0)], prev=[(p, SHIFT, 0)],
                                    full=[mu_shift] + prep_params, out_tiled=[(RW, F32)] * 7)
    y_wkv, ck = _wkv_fwd(r_, w_, k_, v_, a_, b_, Bl, S)
    r_k_row = rwkv_r_k.reshape(1, RW)
    post_params = [rwkv_ln_g, rwkv_ln_b, r_k_row, hm]
    (o_rwkv,) = rw("rwkv_post", _rwkv_post, R=256,
                   tiled=[(y_wkv, RW, 0), (r_, RW, 0), (k_, RW, 0), (v_, RW, 0), (g_, RW, 0)],
                   full=post_params, out_tiled=[(RW, MXU_DTYPE)])
    y_a = _mm([o_rwkv], [full["w_out_rwkv"]], F32, "out_rwkv")

    expand = jnp.kron(jnp.eye(SP, dtype=F32), jnp.ones((1, SGC), F32))
    s5_in = (s5_a_re[0], s5_a_im[0], s5_log_dt[0].reshape(NG, 1), s5_b_re[0].reshape(NG, SP * SGC),
             s5_b_im[0].reshape(NG, SP * SGC), expand)
    ab_re, ab_im, bb_re, bb_im = _s5_disc(*s5_in)
    eye8 = jnp.eye(8, dtype=F32)

    def blockdiag_in(bb):
        t = bb.reshape(NSG, 8, SP, SGC)
        return jnp.einsum("ab,sapc->sacbp", eye8, t).reshape(NSG, 128, 512)

    def blockdiag_out(cc):
        t = cc.reshape(NSG, 8, SGC, SP)
        return jnp.einsum("ab,sacp->sapbc", eye8, t).reshape(NSG, 512, 128)

    wb = jnp.concatenate([blockdiag_in(bb_re), blockdiag_in(bb_im)], axis=2).astype(MXU_DTYPE)
    wc = jnp.concatenate([blockdiag_out(s5_c_re[0]), -blockdiag_out(s5_c_im[0])], axis=1).astype(MXU_DTYPE)
    ab = jnp.stack([ab_re.reshape(NST), ab_im.reshape(NST)])
    y_ssm, s5_st = _s5_fwd(u, wb, wc, ab, Bl, S)
    (s5o,) = rw("s5_post", _s5_post, R=256, tiled=[(y_ssm, SW, 0), (u, SW, 0)], full=[s5_d],
                out_tiled=[(SW, MXU_DTYPE)])
    z = _mm([s5o], [full["w_glu"]], F32, "glu")
    mix_tiled = [(gates, D, 0), (gates, D, 1), (y_a, D, 0), (z, D, 0), (z, D, 1)]
    (mixed_in,) = rw("mix", _mix, R=256, tiled=mix_tiled, out_tiled=[(D, MXU_DTYPE)])
    mixed = _mm([mixed_in], [full["w_out"]], F32, "out_proj")

    def norm2_fwd(xv, mx, gt, sc, sh, g):
        x1 = xv + gt * mx
        return x1, _norm_mod(x1, g, sc, sh)

    x1, h2 = rw("norm2", norm2_fwd, R=256, tiled=[(x2d, D, 0), (mixed, D, 0)],
                batch=[(mod, D, GT1), (mod, D, SC2), (mod, D, SH2)], full=[norm2_g],
                out_tiled=[(D, F32), (D, MXU_DTYPE)])
    up = _mm([h2], [full["w_ffn_up"]], F32, "ffn_up")
    conv_tiled = [(up, DFF, 0), (up, DFF, 1)]
    conv_full = [conv_wg, conv_wu, conv_bg, conv_bu]

    def act_fwd(*a):
        return _silu_gate(*_conv_act(*a))

    (act,) = rw("ffn_act", act_fwd, R=128, tiled=conv_tiled, prev=conv_tiled, full=conv_full,
                out_tiled=[(DFF, MXU_DTYPE)])
    ffn = _mm([act], [full["w_ffn_down"]], F32, "ffn_down")

    def head(x1v, fv, tv, gt, g):
        x2 = x1v + gt * fv
        y, vjp = jax.vjp(_rms, x2, g)
        e = y - tv
        dx2, dg = vjp(e * (1.0 / D))
        loss = jnp.sum(e * e, keepdims=True) * jnp.ones((1, LANES), F32)
        return dx2, dx2 * gt, jnp.sum(dx2 * fv, axis=0, keepdims=True), dg.reshape(1, D), loss

    dx2, d_ffn, d_gt2, g_norm_f, loss_acc = rw(
        "head", head, R=256, tiled=[(x1, D, 0), (ffn, D, 0), (tgt, D, 0)], batch=[(mod, D, GT2)],
        full=[norm_f_g.reshape(1, D)], out_tiled=[(D, F32), (D, MXU_DTYPE)], out_batch=[D],
        out_acc=[(1, D), (1, LANES)])
    loss = lax.psum(0.5 / D * loss_acc[0, 0], ("x", "y", "c"))

    tr = lambda wmat: wmat.T
    d_act = _mm([d_ffn], [tr(full["w_ffn_down"])], F32, "d_act")
    g_w_ffn_down = _mm_tn(act, d_ffn, "g_ffn_down")

    def act_bwd(ug, uu, dact, hg, hu, wg, wu, bg, bu):
        gate, upv = _conv_act(ug, uu, hg, hu, wg, wu, bg, bu)
        _, vjp_s = jax.vjp(_silu_gate, gate, upv)
        d_gate, d_upv = vjp_s(dact)
        def taps(dh, xv, h):
            return [jnp.sum(dh * _shift_down(xv, h, 2), axis=0, keepdims=True),
                    jnp.sum(dh * _shift_down(xv, h, 1), axis=0, keepdims=True),
                    jnp.sum(dh * xv, axis=0, keepdims=True), jnp.sum(dh, axis=0, keepdims=True)]
        return (d_gate, d_upv, *taps(d_gate, ug, hg), *taps(d_upv, uu, hu))

    dh_g, dh_u, *tapg = rw(
        "ffn_act_bwd", act_bwd, R=128, tiled=conv_tiled + [(d_act, DFF, 0)], prev=conv_tiled, full=conv_full,
        out_tiled=[(DFF, F32), (DFF, F32)], out_acc=[(1, DFF)] * 8)
    g_cw_g, g_cb_g = jnp.concatenate(tapg[0:3], axis=0), tapg[3]
    g_cw_u, g_cb_u = jnp.concatenate(tapg[4:7], axis=0), tapg[7]

    def conv_t(dg, du_, ng, nu, wg, wu):
        def ct(d, n, w):
            return w[2:3] * d + w[1:2] * _shift_up(d, n, 1) + w[0:1] * _shift_up(d, n, 2)
        return jnp.concatenate([ct(dg, ng, wg), ct(du_, nu, wu)], axis=1)

    (d_up,) = rw("conv_bwd", conv_t, R=128, tiled=[(dh_g, DFF, 0), (dh_u, DFF, 0)],
                 nxt=[(dh_g, DFF, 0), (dh_u, DFF, 0)], full=[conv_wg, conv_wu], out_tiled=[(2 * DFF, MXU_DTYPE)])
    d_h2 = _mm([d_up], [tr(full["w_ffn_up"])], F32, "d_h2")
    g_w_ffn_up = _mm_tn(h2, d_up, "g_ffn_up")

    def norm2_bwd(x1v, dh2, dx2v, mx, gt, sc, sh, g):
        _, vjp = jax.vjp(_norm_mod, x1v, g, sc, sh)
        dxn, dg, dsc, dsh = vjp(dh2)
        dx1 = dx2v + dxn
        return dx1, dx1 * gt, jnp.sum(dx1 * mx, axis=0, keepdims=True), dsc, dsh, dg

    dx1, d_mixed, d_gt1, d_sc2, d_sh2, g_norm2 = rw(
        "norm2_bwd", norm2_bwd, R=256, tiled=[(x1, D, 0), (d_h2, D, 0), (dx2, D, 0), (mixed, D, 0)],
        batch=[(mod, D, GT1), (mod, D, SC2), (mod, D, SH2)], full=[norm2_g],
        out_tiled=[(D, F32), (D, MXU_DTYPE)], out_batch=[D, D, D], out_acc=[(1, D)])

    d_mixed_in = _mm([d_mixed], [tr(full["w_out"])], F32, "d_mixed_in")
    g_w_out = _mm_tn(mixed_in, d_mixed, "g_w_out")

    def mix_bwd(ga, gb, ya, za, zb, dm):
        _, vjp = jax.vjp(_mix, ga, gb, ya, za, zb)
        dga, dgb, dya, dza, dzb = vjp(dm)
        return jnp.concatenate([dga, dgb], axis=1), dya, jnp.concatenate([dza, dzb], axis=1)

    d_gates, d_ya, d_z = rw("mix_bwd", mix_bwd, R=256, tiled=mix_tiled + [(d_mixed_in, D, 0)],
                            out_tiled=[(2 * D, MXU_DTYPE), (D, MXU_DTYPE), (2 * D, MXU_DTYPE)])
    d_o_rwkv = _mm([d_ya], [tr(full["w_out_rwkv"])], F32, "d_o_rwkv")
    g_w_out_rwkv = _mm_tn(o_rwkv, d_ya, "g_out_rwkv")
    d_s5o = _mm([d_z], [tr(full["w_glu"])], F32, "d_s5o")
    g_w_glu = _mm_tn(s5o, d_z, "g_glu")

    def s5_post_bwd(ys, uv, ds, dd):
        _, vjp = jax.vjp(_s5_post, ys, uv, dd)
        return vjp(ds)

    d_yssm, d_u_direct, g_s5_d = rw("s5_post_bwd", s5_post_bwd, R=256,
                                    tiled=[(y_ssm, SW, 0), (u, SW, 0), (d_s5o, SW, 0)], full=[s5_d],
                                    out_tiled=[(SW, F32), (SW, F32)], out_acc=[(1, SW)])
    d_u_ssm, d_wb, d_wc, d_ab = _s5_bwd(u, d_yssm, wb, wc, ab, s5_st, Bl, S)

    def diag_in(dw):
        t = dw.reshape(NSG, 8, SGC, 8, SP)
        return jnp.einsum("ab,sacbp->sapc", eye8, t).reshape(NG, SP * SGC)

    def diag_out(dw):
        t = dw.reshape(NSG, 8, SP, 8, SGC)
        return jnp.einsum("ab,sapbc->sacp", eye8, t).reshape(NG, SGC, SP)

    g_s5_c_re = diag_out(d_wc[:, :512])
    g_s5_c_im = -diag_out(d_wc[:, 512:])
    disc_cts = (d_ab[0].reshape(NG, SP), d_ab[1].reshape(NG, SP), diag_in(d_wb[:, :, :512]), diag_in(d_wb[:, :, 512:]))
    g_a_re, g_a_im, g_log_dt, g_b_re, g_b_im = _s5_disc_bwd(*s5_in, disc_cts)

    def post_bwd(yv, rv, kv, vv, gv, do, *pp):
        _, vjp = jax.vjp(lambda *a: _rwkv_post(*a, pp[3]), yv, rv, kv, vv, gv, *pp[:3])
        return vjp(do)

    dy_wkv, dr_b, dk_b, dv_b, dg_, g_ln_g, g_ln_b, g_r_k = rw(
        "rwkv_post_bwd", post_bwd, R=256,
        tiled=[(y_wkv, RW, 0), (r_, RW, 0), (k_, RW, 0), (v_, RW, 0), (g_, RW, 0), (d_o_rwkv, RW, 0)],
        full=post_params, out_tiled=[(RW, F32)] * 5, out_acc=[(1, RW)] * 3)
    dr3, dw3, dk3, dv3, da3, db3 = _wkv_bwd(r_, w_, k_, v_, a_, b_, dy_wkv, ck, Bl, S)

    def prep_bwd(pv, dr1, dr2, dwv, dk1, dk2, dv1, dv2, dav, dbv, dgv, ph, mu, *pp):
        prev = _shift_down(pv, ph, 1)
        ps = pv + (prev - pv) * mu
        _, vjp = jax.vjp(lambda *q: _rwkv_prep(*q, pp[7]), *_split_ps(ps), *pp[:7])
        grads = vjp((dr1 + dr2, dwv, dk1 + dk2, dv1 + dv2, dav, dbv, dgv))
        dps = jnp.concatenate(grads[:5], axis=1)
        return (dps,) + tuple(grads[5:]) + (jnp.sum(dps * (prev - pv), axis=0, keepdims=True),)

    prep_outs = rw(
        "rwkv_prep_bwd", prep_bwd, R=256,
        tiled=[(p, SHIFT, 0), (dr3, RW, 0), (dr_b, RW, 0), (dw3, RW, 0), (dk3, RW, 0), (dk_b, RW, 0),
               (dv3, RW, 0), (dv_b, RW, 0), (da3, RW, 0), (db3, RW, 0), (dg_, RW, 0)],
        prev=[(p, SHIFT, 0)], full=[mu_shift] + prep_params,
        out_tiled=[(SHIFT, F32)],
        out_acc=[(1, RW), (LW + LA, RW), (1, RW), (LW + LA, RW), (LG, RW), (1, RW), (1, RW), (1, SHIFT)])
    d_ps, g_w0, g_w_up_p, g_a0, g_a_up_p, g_g_up, g_k_k, g_k_a, g_mu = prep_outs

    def shift_bwd(dps, nx, mu):
        return dps * (1.0 - mu) + _shift_up(dps * mu, nx * mu, 1)

    (d_p,) = rw("shift_bwd", shift_bwd, R=256, tiled=[(d_ps, SHIFT, 0)], nxt=[(d_ps, SHIFT, 0)], full=[mu_shift],
                out_tiled=[(SHIFT, MXU_DTYPE)])
    (d_u,) = rw("d_u", lambda a1, a2: a1 + a2, R=256, tiled=[(d_u_direct, SW, 0), (d_u_ssm, SW, 0)],
                out_tiled=[(SW, MXU_DTYPE)])
    d_h1 = _mm([d_p, d_u, d_gates], [tr(w_p), tr(w_u), tr(w_g)], F32, "d_h1")
    g_w_in = jnp.concatenate([_mm_tn(h1, d_p, "g_w_p"), _mm_tn(h1, d_u, "g_w_u"), _mm_tn(h1, d_gates, "g_w_g")], axis=1)

    def norm1_bwd(xv, dh1, dx1v, sc, sh, g):
        _, vjp = jax.vjp(_norm_mod, xv, g, sc, sh)
        dxn, dg, dsc, dsh = vjp(dh1)
        return dx1v + dxn, dsc, dsh, dg

    grad_x, d_sc1, d_sh1, g_norm1 = rw(
        "norm1_bwd", norm1_bwd, R=256, tiled=[(x2d, D, 0), (d_h1, D, 0), (dx1, D, 0)],
        batch=[(mod, D, SC1), (mod, D, SH1)], full=[norm1_g], out_tiled=[(D, F32)], out_batch=[D, D], out_acc=[(1, D)])

    dmod = jnp.concatenate([d_sh1, d_sc1, d_gt1, d_sh2, d_sc2, d_gt2], axis=2).reshape(Bl, 6 * D)
    dmod_all = _gather8(dmod, "gather_dmod").reshape(8 * Bl, 6 * D)
    dmod_cols = lax.dynamic_slice_in_dim(dmod_all, chip * ncol, ncol, 1)
    g_w_ada, g_b_ada = _ada_bwd(c_all, dmod_cols, dmod_all)

    small = {"norm1_g": g_norm1, "mu_shift": g_mu, "rwkv_w0": g_w0, "rwkv_a0": g_a0, "rwkv_k_k": g_k_k,
             "rwkv_k_a": g_k_a, "rwkv_r_k": g_r_k, "rwkv_ln_g": g_ln_g, "rwkv_ln_b": g_ln_b, "s5_a_re": g_a_re,
             "s5_a_im": g_a_im, "s5_log_dt": g_log_dt, "s5_b_re": g_b_re, "s5_b_im": g_b_im, "s5_c_re": g_s5_c_re,
             "s5_c_im": g_s5_c_im, "s5_d": g_s5_d, "norm2_g": g_norm2,
             "ffn_conv_b": jnp.concatenate([g_cb_g, g_cb_u], axis=1), "norm_f_g": g_norm_f}
    small_names = list(small)
    g_conv_w = jnp.concatenate([g_cw_g, g_cw_u], axis=1)
    shard_small = {"rwkv_w_up": g_w_up_p[:LW], "rwkv_a_up": g_a_up_p[LW:], "rwkv_g_up": g_g_up, "ffn_conv_w": g_conv_w}
    parts = [small[n] for n in small_names] + [_to_shards(shard_small[n], ax) for n, _, ax in BIG_SMALL]
    spack = _pack_rows(parts, F32, SUBLANES)
    s_all = _gather8(spack, "gather_gsmall")
    s_sum = _sum_slots(s_all, F32, "sum_gsmall").reshape(-1)
    grads = {}
    off = 0
    for n in small_names:
        grads[n] = s_sum[off:off + W[n].size].reshape(W[n].shape)
        off += W[n].size
    for n, shape, axis in BIG_SMALL:
        ss = _shard_shape(shape, axis)
        k4 = 4 * math.prod(ss)
        sh4 = s_sum[off:off + k4].reshape(4, math.prod(ss))
        grads[n] = lax.dynamic_index_in_dim(sh4, chip, 0, keepdims=False).reshape((1,) + ss)
        off += k4

    big_g = {"w_in": g_w_in, "w_out_rwkv": g_w_out_rwkv, "w_glu": g_w_glu, "w_out": g_w_out,
             "w_ffn_up": g_w_ffn_up, "w_ffn_down": g_w_ffn_down}
    gsh = jnp.concatenate([_to_shards(big_g[n], ax) for n, _, ax in BIG], axis=1)
    n_shard = gsh.shape[1]
    n_pad = big_pack.shape[0] * PACK_COLS
    gsh = jnp.pad(gsh, ((0, 0), (0, n_pad - n_shard))).astype(MXU_DTYPE).reshape(4, 2, half_rows, PACK_COLS)
    gsh = jnp.swapaxes(gsh, 0, 1)
    pair = _exchange(gsh, PAIR_FLIPS, 2, lambda me, peer: peer[2], lambda me: me[2], "rs_pair")
    pair = pair.reshape(2, 4 * half_rows, PACK_COLS)
    chip_part = _sum_slots(pair, MXU_DTYPE, "rs_pair_sum").reshape(4, half_rows, PACK_COLS)
    recv = _alltoall_chips(chip_part, "rs_chips")
    g_half = _sum_slots(recv, F32, "rs_chip_sum")
    g_both = _gather_pair(g_half, "rs_share").reshape(-1)
    off = 0
    for n, shape, axis in BIG:
        ss = _shard_shape(shape, axis)
        grads[n] = g_both[off:off + math.prod(ss)].reshape((1,) + ss)
        off += math.prod(ss)
    grads["w_ada"] = g_w_ada[None]
    grads["b_ada"] = g_b_ada

    delta, new_m, new_v = {}, {}, {}
    to2 = lambda z: z.reshape(-1, z.shape[-1])
    for n in ["w_ada"] + [b[0] for b in BIG]:
        d_, m_, v2_ = _adamw(to2(W[n]), to2(grads[n]), to2(M[n]), to2(V[n]), "adamw_" + n)
        delta[n], new_m[n], new_v[n] = (z.reshape(W[n].shape) for z in (d_, m_, v2_))
    rest = [n for n in names if n not in delta]
    packs = [_pack_rows([src[n] for n in rest], F32, SUBLANES) for src in (W, grads, M, V)]
    d_, m_, v2_ = _adamw(*packs, "adamw_small")
    shapes = [W[n].shape for n in rest]
    for dst, z in ((delta, d_), (new_m, m_), (new_v, v2_)):
        for n, val in zip(rest, _unpack(z.reshape(-1), shapes)):
            dst[n] = val

    return (loss, grad_x.reshape(Bl, S, D), *[grads[n] for n in names], *[delta[n] for n in names],
            *[new_m[n] for n in names], *[new_v[n] for n in names])
```

```python
import functools
import math

import jax
import jax.numpy as jnp
from jax import lax
from jax.experimental import pallas as pl
from jax.experimental.pallas import tpu as pltpu

F32 = jnp.float32
BF16 = jnp.bfloat16
MXU_DTYPE = jnp.bfloat16
MESH_IDS = pl.DeviceIdType.MESH
HIGHEST = lax.Precision.HIGHEST

D = 1024
RW, NH, HD = 512, 8, 64
LW, LA, LG = 64, 64, 128
SW, SGC, NG, SP = 512, 16, 32, 64
NSG = 4
SHIFT = 3 * RW + LW + LA + LG
DFF = 2816
RMS_EPS, GN_EPS, L2_EPS = 1e-6, 64e-5, 1e-12
LR, B1, B2, ADAM_EPS, WD, STEP = 0.001, 0.9, 0.999, 1e-8, 0.01, 10
DECAY_SCALE = math.exp(-0.5)
GELU_C = math.sqrt(2.0 / math.pi)

VMEM_LIMIT = 52 * 1024 * 1024
SUBLANES, LANES = 8, 128


def _pick(n, cap):
    if n <= cap:
        return n
    best = None
    for t in range(LANES, cap + 1, LANES):
        if n % t == 0:
            best = t
    assert best is not None, (n, cap)
    return best


def _params(sem=None, vmem=VMEM_LIMIT):
    return pltpu.CompilerParams(dimension_semantics=sem, vmem_limit_bytes=vmem)


def _chip_of(p):
    return 2 * p[0] + p[1]


def _me():
    return (lax.axis_index("x"), lax.axis_index("y"), lax.axis_index("c"))


def _half(rows, core):
    h = rows // 2
    return pl.ds(pl.multiple_of(core * h, 16 if h % 16 == 0 else SUBLANES), h)


def _exchange(name, flips, srcs, outs, moves):
    ns, nf, nm = len(srcs), len(flips), len(moves)

    def body(*refs):
        src_refs, out_refs = refs[:ns], refs[ns:ns + len(outs)]
        send_sems, recv_sems, loc_sems = refs[ns + len(outs):]
        me = _me()
        copies, locs = [], []
        for m, (si, oi, src_sel, dst_sel) in enumerate(moves):
            for k, f in enumerate(flips):
                peer = tuple(1 - v if b else v for v, b in zip(me, f))
                cp = pltpu.make_async_remote_copy(
                    src_ref=src_sel(src_refs[si], me, peer), dst_ref=dst_sel(out_refs[oi], me),
                    send_sem=send_sems.at[m * nf + k], recv_sem=recv_sems.at[m * nf + k],
                    device_id=peer, device_id_type=MESH_IDS)
                cp.start()
                copies.append(cp)
            loc = pltpu.make_async_copy(src_sel(src_refs[si], me, me), dst_sel(out_refs[oi], me), loc_sems.at[m])
            loc.start()
            locs.append(loc)
        for cp in copies:
            cp.wait_recv()
        for cp in copies:
            cp.wait_send()
        for loc in locs:
            loc.wait()

    return pl.pallas_call(
        body, name=name, out_shape=list(outs),
        in_specs=[pl.BlockSpec(memory_space=pl.ANY)] * ns,
        out_specs=[pl.BlockSpec(memory_space=pl.ANY)] * len(outs),
        scratch_shapes=[pltpu.SemaphoreType.DMA((nm * nf,)), pltpu.SemaphoreType.DMA((nm * nf,)),
                        pltpu.SemaphoreType.DMA((nm,))],
    )(*srcs)


CHIP_FLIPS = ((1, 0, 0), (0, 1, 0), (1, 1, 0))
PAIR_FLIPS = ((0, 0, 1),)


def _gather_two_level(chip_arrs, dev_arrs, name):
    arrs = list(chip_arrs) + list(dev_arrs)
    n, nchip = len(arrs), len(chip_arrs)
    NS = 7

    def body(*refs):
        srcs, outs = refs[:n], refs[n:2 * n]
        send_sems, recv_sems, loc_sems = refs[2 * n:]
        x, y, c = _me()
        sib = (x, y, 1 - c)
        chips = [(1 - x, y), (x, 1 - y), (1 - x, 1 - y)]
        mine = 2 * x + y
        ids = [2 * cx + cy for cx, cy in chips]

        def part(i, slot, core):
            if i < nchip:
                return outs[i].at[slot, _half(arrs[i].shape[0], core)]
            return outs[i].at[slot, core]

        def rcopy(i, k, src, dst, to):
            return pltpu.make_async_remote_copy(src_ref=src, dst_ref=dst, send_sem=send_sems.at[i * NS + k],
                                                recv_sem=recv_sems.at[i * NS + k], device_id=to, device_id_type=MESH_IDS)

        started, locs = [], []
        for i in range(n):
            own = srcs[i].at[_half(arrs[i].shape[0], c)] if i < nchip else srcs[i]
            loc = pltpu.make_async_copy(srcs[i], outs[i].at[mine] if i < nchip else outs[i].at[mine, c], loc_sems.at[i])
            loc.start()
            locs.append(loc)
            for f, chip in enumerate(chips):
                cp = rcopy(i, f, own, part(i, mine, c), (*chip, c))
                cp.start()
                started.append(cp)
            if i >= nchip:
                cp = rcopy(i, 6, own, part(i, mine, c), sib)
                cp.start()
                started.append(cp)
        for i in range(n):
            for f in range(3):
                land = part(i, ids[f], c)
                rcopy(i, f, land, land, sib).wait_recv()
                fw = rcopy(i, 3 + f, land, land, sib)
                fw.start()
                started.append(fw)
        for i in range(n):
            for f in range(3):
                land = part(i, ids[f], 1 - c)
                rcopy(i, 3 + f, land, land, sib).wait_recv()
            if i >= nchip:
                land = part(i, mine, 1 - c)
                rcopy(i, 6, land, land, sib).wait_recv()
        for cp in started:
            cp.wait_send()
        for loc in locs:
            loc.wait()

    outs = [jax.ShapeDtypeStruct((4,) + a.shape, a.dtype) for a in chip_arrs]
    outs += [jax.ShapeDtypeStruct((4, 2) + a.shape, a.dtype) for a in dev_arrs]
    res = pl.pallas_call(
        body, name=name, out_shape=outs,
        in_specs=[pl.BlockSpec(memory_space=pl.ANY)] * n, out_specs=[pl.BlockSpec(memory_space=pl.ANY)] * n,
        scratch_shapes=[pltpu.SemaphoreType.DMA((n * NS,)), pltpu.SemaphoreType.DMA((n * NS,)),
                        pltpu.SemaphoreType.DMA((n,))],
    )(*arrs)
    return res[:nchip], res[nchip:]


def _mm(As, Bs, out_dtype, name, tm=512, cap=1408, bt=False):
    n = len(As)
    M, N = As[0].shape[0], Bs[0].shape[0 if bt else 1]
    tm = min(tm, M)
    tn = _pick(N, cap)
    dims = (((1,), (1,)), ((), ())) if bt else (((1,), (0,)), ((), ()))

    def body(*refs):
        o = refs[2 * n]
        acc = None
        for a, b in zip(refs[:n], refs[n:2 * n]):
            d = lax.dot_general(a[...].astype(MXU_DTYPE), b[...].astype(MXU_DTYPE), dims, preferred_element_type=F32)
            acc = d if acc is None else acc + d
        o[...] = acc.astype(o.dtype)

    in_specs = [pl.BlockSpec((tm, a.shape[1]), lambda i, j: (i, 0)) for a in As]
    if bt:
        in_specs += [pl.BlockSpec((tn, b.shape[1]), lambda i, j: (j, 0)) for b in Bs]
    else:
        in_specs += [pl.BlockSpec((b.shape[0], tn), lambda i, j: (0, j)) for b in Bs]
    return pl.pallas_call(
        body, name=name, grid=(M // tm, N // tn), in_specs=in_specs,
        out_specs=pl.BlockSpec((tm, tn), lambda i, j: (i, j)),
        out_shape=jax.ShapeDtypeStruct((M, N), out_dtype),
        compiler_params=_params(("parallel", "parallel")),
    )(*As, *Bs)


def _mm_tn(A, G, name, tt=1024, cap=1024):
    T, Ka = A.shape
    N = G.shape[1]
    tt = min(tt, T)
    tk = _pick(Ka, cap)
    tn = _pick(N, cap)

    def body(a, g, o):
        @pl.when(pl.program_id(2) == 0)
        def _():
            o[...] = jnp.zeros(o.shape, F32)
        o[...] += lax.dot_general(a[...].astype(MXU_DTYPE), g[...].astype(MXU_DTYPE),
                                  (((0,), (0,)), ((), ())), preferred_element_type=F32)

    return pl.pallas_call(
        body, name=name, grid=(Ka // tk, N // tn, T // tt),
        in_specs=[pl.BlockSpec((tt, tk), lambda i, j, t: (t, i)), pl.BlockSpec((tt, tn), lambda i, j, t: (t, j))],
        out_specs=pl.BlockSpec((tk, tn), lambda i, j, t: (i, j)),
        out_shape=jax.ShapeDtypeStruct((Ka, N), F32),
        compiler_params=_params(("parallel", "parallel", "arbitrary")),
    )(A, G)


def _rowwise(name, fn, *, Bl, S, R, tiled=(), prev=(), nxt=(), batch=(), full=(),
             out_tiled=(), out_batch=(), out_acc=()):
    R = min(R, S)
    nS = S // R
    T = Bl * S
    hb = R // SUBLANES
    n_in = len(tiled) + len(prev) + len(nxt) + len(batch) + len(full)

    in_specs, args = [], []
    for a, wd, cb in tiled:
        in_specs.append(pl.BlockSpec((R, wd), lambda b, i, cb=cb: (b * nS + i, cb)))
        args.append(a)
    for a, wd, cb in prev:
        in_specs.append(pl.BlockSpec((SUBLANES, wd), lambda b, i, cb=cb: (jnp.maximum((b * nS + i) * hb - 1, 0), cb)))
        args.append(a)
    for a, wd, cb in nxt:
        in_specs.append(pl.BlockSpec((SUBLANES, wd), lambda b, i, cb=cb: (jnp.minimum((b * nS + i + 1) * hb, T // SUBLANES - 1), cb)))
        args.append(a)
    for a, wd, cb in batch:
        in_specs.append(pl.BlockSpec((1, 1, wd), lambda b, i, cb=cb: (b, 0, cb)))
        args.append(a)
    for a in full:
        in_specs.append(pl.BlockSpec(a.shape, lambda b, i, nd=a.ndim: (0,) * nd))
        args.append(a)

    out_specs, out_shape = [], []
    for C, dt in out_tiled:
        out_specs.append(pl.BlockSpec((R, C), lambda b, i: (b * nS + i, 0)))
        out_shape.append(jax.ShapeDtypeStruct((T, C), dt))
    for C in out_batch:
        out_specs.append(pl.BlockSpec((1, 1, C), lambda b, i: (b, 0, 0)))
        out_shape.append(jax.ShapeDtypeStruct((Bl, 1, C), F32))
    for shp in out_acc:
        out_specs.append(pl.BlockSpec(shp, lambda b, i, nd=len(shp): (0,) * nd))
        out_shape.append(jax.ShapeDtypeStruct(shp, F32))

    nt, npv, nnx, nbt = len(tiled), len(prev), len(nxt), len(batch)

    def body(*refs):
        b, i = pl.program_id(0), pl.program_id(1)
        ins, outs = refs[:n_in], refs[n_in:]
        vals = [r[...] for r in ins[:nt]]
        vals += [jnp.where(i > 0, r[...], jnp.zeros(r.shape, r.dtype)) for r in ins[nt:nt + npv]]
        vals += [jnp.where(i < nS - 1, r[...], jnp.zeros(r.shape, r.dtype)) for r in ins[nt + npv:nt + npv + nnx]]
        vals += [r[0] for r in ins[nt + npv + nnx:nt + npv + nnx + nbt]]
        vals += [r[...] for r in ins[nt + npv + nnx + nbt:]]
        res = fn(*vals)
        if not isinstance(res, (tuple, list)):
            res = (res,)
        k = 0
        for _ in out_tiled:
            outs[k][...] = res[k].astype(outs[k].dtype)
            k += 1
        for _ in out_batch:
            o = outs[k]

            @pl.when(i == 0)
            def _(o=o):
                o[...] = jnp.zeros(o.shape, F32)
            o[0] += res[k]
            k += 1
        for _ in out_acc:
            o = outs[k]

            @pl.when((i == 0) & (b == 0))
            def _(o=o):
                o[...] = jnp.zeros(o.shape, F32)
            o[...] += res[k]
            k += 1

    out = pl.pallas_call(
        body, name=name, grid=(Bl, nS), in_specs=in_specs, out_specs=out_specs, out_shape=out_shape,
        compiler_params=_params(("arbitrary", "arbitrary")),
    )(*args)
    return out


def _shift_down(x, halo, k):
    row = lax.broadcasted_iota(jnp.int32, x.shape, 0)
    out = pltpu.roll(x, k, 0)
    for j in range(k):
        out = jnp.where(row == j, halo[SUBLANES - k + j:SUBLANES - k + j + 1, :], out)
    return out


def _shift_up(x, halo, k):
    n = x.shape[0]
    row = lax.broadcasted_iota(jnp.int32, x.shape, 0)
    out = pltpu.roll(x, n - k, 0)
    for j in range(k):
        out = jnp.where(row == n - k + j, halo[j:j + 1, :], out)
    return out


def _dotm(a, b):
    return jnp.dot(a.astype(MXU_DTYPE), b.astype(MXU_DTYPE), preferred_element_type=F32)


def _headsum(x, hm):
    return jnp.dot(x, hm, preferred_element_type=F32, precision=HIGHEST)


def _sigmoid(x):
    return 1.0 / (1.0 + jnp.exp(-x))


def _rms(x, g):
    return x * lax.rsqrt(jnp.mean(x * x, axis=-1, keepdims=True) + RMS_EPS) * g


def _norm_mod(x, g, sc, sh):
    return _rms(x, g) * (1.0 + sc) + sh


def _split_ps(ps):
    return (ps[:, 0:RW], ps[:, RW:2 * RW], ps[:, 2 * RW:3 * RW], ps[:, 3 * RW:3 * RW + LW + LA],
            ps[:, 3 * RW + LW + LA:SHIFT])


def _rwkv_prep(r, k, v, wa, gd, w0, w_up_p, a0, a_up_p, g_up, k_k, k_a, hm):
    w_raw = w0 + _dotm(jnp.tanh(wa), w_up_p)
    decay = jnp.exp(-DECAY_SCALE * _sigmoid(w_raw))
    a = _sigmoid(a0 + _dotm(wa, a_up_p))
    g = _dotm(_sigmoid(gd), g_up)
    kk = k * k_k
    kk = kk * lax.rsqrt(_headsum(kk * kk, hm) + L2_EPS)
    k2 = k * (1.0 + (a - 1.0) * k_a)
    return r, decay, k2, v, -kk, kk * a, g


def _rwkv_post(y, r, k2, v, g, ln_g, ln_b, r_k, hm):
    mean = _headsum(y, hm) * (1.0 / HD)
    yc = y - mean
    var = _headsum(yc * yc, hm) * (1.0 / HD)
    yn = yc * lax.rsqrt(var + GN_EPS) * ln_g + ln_b
    bonus = _headsum(r * k2 * r_k, hm) * v
    return (yn + bonus) * g


def _gelu(x):
    return 0.5 * x * (1.0 + jnp.tanh(GELU_C * (x + 0.044715 * (x * x * x))))


def _s5_post(yssm, u, d):
    return _gelu(yssm + d * u)


def _mix(ga, gb, ya, za, zb):
    return _sigmoid(ga) * ya + _sigmoid(gb) * (za * _sigmoid(zb))


def _conv_act(up_g, up_u, hg, hu, w_g, w_u, b_g, b_u):
    def conv(x, h, w, b):
        return b + w[0:1] * _shift_down(x, h, 2) + w[1:2] * _shift_down(x, h, 1) + w[2:3] * x
    gate = conv(up_g, hg, w_g, b_g)
    upv = conv(up_u, hu, w_u, b_u)
    return gate, upv


def _silu_gate(gate, upv):
    return gate * _sigmoid(gate) * upv


WKV_L = 64
_NT, _NN, _TN = ((1,), (1,)), ((1,), (0,)), ((0,), (0,))


def _dotw(x, y, dims):
    return lax.dot_general(x.astype(MXU_DTYPE), y.astype(MXU_DTYPE), (dims, ((), ())), preferred_element_type=F32)


def _wkv_chunk(s0, r, w, k, v, a, b):
    y, s1 = _wkv_chunks((s0,), (r,), (w,), (k,), (v,), (a,), (b,))
    return y[0], s1[0]


def _wkv_chunks(s0, r, w, k, v, a, b):
    each = lambda f, *ls: tuple(f(*xs) for xs in zip(*ls))
    L = r[0].shape[0]
    n2 = 2 * L
    lane_head = lax.broadcasted_iota(jnp.int32, (2, 1, 2 * HD), 2) // HD
    head_mask = (lane_head == lax.broadcasted_iota(jnp.int32, (2, 1, 2 * HD), 0)).astype(F32)
    ri = lax.broadcasted_iota(jnp.int32, (n2, n2), 0)
    ci = lax.broadcasted_iota(jnp.int32, (n2, n2), 1)
    same = (ri // L) == (ci // L)
    strict = same & ((ci % L) < (ri % L))
    incl = same & ((ci % L) <= (ri % L))
    si = lax.broadcasted_iota(jnp.int32, (2 * HD, 2 * HD), 0) // HD
    sj = lax.broadcasted_iota(jnp.int32, (2 * HD, 2 * HD), 1) // HD
    tri = (lax.broadcasted_iota(jnp.int32, (L, L), 0) >= lax.broadcasted_iota(jnp.int32, (L, L), 1)).astype(F32)

    stack = lambda z: (z[None] * head_mask).reshape(n2, 2 * HD)
    dup = lambda z: jnp.broadcast_to(z[None], (2, L, 2 * HD)).reshape(n2, 2 * HD)
    gram = lambda x, y: lax.dot_general(x, y, (_NT, ((), ())), preferred_element_type=F32, precision=HIGHEST)
    nt, nn, tn = (lambda x, y, d=d: _dotw(x, y, d) for d in (_NT, _NN, _TN))
    add = lambda x, y: x + y

    lw = each(jnp.log, w)
    cum = each(lambda z: jnp.dot(tri, z, preferred_element_type=F32, precision=HIGHEST), lw)
    tot = each(lambda z: jnp.sum(z, axis=0, keepdims=True), lw)
    a2 = each(lambda av, cv, lv: stack(av * jnp.exp(cv - lv)), a, cum, lw)
    r2 = each(lambda rv, cv: stack(rv * jnp.exp(cv)), r, cum)
    v2 = each(stack, v)
    b2 = each(lambda bv, cv: dup(bv * jnp.exp(-cv)), b, cum)
    k2 = each(lambda kv, cv: dup(kv * jnp.exp(-cv)), k, cum)
    n_ab = each(lambda x, y: jnp.where(strict, gram(x, y), 0.0), a2, b2)
    n_ak = each(lambda x, y: jnp.where(strict, gram(x, y), 0.0), a2, k2)
    m_rb = each(lambda x, y: jnp.where(incl, gram(x, y), 0.0), r2, b2)
    m_rk = each(lambda x, y: jnp.where(incl, gram(x, y), 0.0), r2, k2)
    u = each(add, each(nt, a2, s0), each(nn, n_ak, v2))
    q = n_ab
    steps = L.bit_length() - 1
    for i in range(steps):
        u = each(add, u, each(nn, q, u))
        if i < steps - 1:
            q = each(nn, q, q)
    y2 = each(lambda x, y, z: x + y + z, each(nt, r2, s0), each(nn, m_rb, u), each(nn, m_rk, v2))
    y = each(lambda z: jnp.sum(z.reshape(2, L, 2 * HD), axis=0), y2)
    b3 = each(lambda bv, tv, cv: dup(bv * jnp.exp(tv - cv)), b, tot, cum)
    k3 = each(lambda kv, tv, cv: dup(kv * jnp.exp(tv - cv)), k, tot, cum)
    upd = each(add, each(tn, u, b3), each(tn, v2, k3))
    s1 = each(lambda sv, tv, uv: sv * jnp.exp(tv) + jnp.where(si == sj, uv, 0.0), s0, tot, upd)
    return y, s1


NPAIR = NH // 2


def _wkv_fwd(r, w, k, v, a, b, Bl, S):
    L = WKV_L
    nC = S // L

    def body(r_ref, w_ref, k_ref, v_ref, a_ref, b_ref, y_ref, ck_ref, s_ref):
        @pl.when(pl.program_id(1) == 0)
        def _():
            s_ref[...] = jnp.zeros(s_ref.shape, F32)
        cols = [slice(p * 2 * HD, (p + 1) * 2 * HD) for p in range(NPAIR)]
        s0 = tuple(s_ref[p] for p in range(NPAIR))
        ops = [tuple(z[:, cols[p]] for p in range(NPAIR)) for z in (r_ref, w_ref, k_ref, v_ref, a_ref, b_ref)]
        y, s1 = _wkv_chunks(s0, *ops)
        for p in range(NPAIR):
            ck_ref[0, 0, p] = s0[p]
            y_ref[:, cols[p]] = y[p]
            s_ref[p] = s1[p]

    row_spec = pl.BlockSpec((L, RW), lambda bb, c: (bb * nC + c, 0))
    return pl.pallas_call(
        body, name="wkv_fwd", grid=(Bl, nC), in_specs=[row_spec] * 6,
        out_specs=[row_spec, pl.BlockSpec((1, 1, NPAIR, 2 * HD, 2 * HD), lambda bb, c: (bb, c, 0, 0, 0))],
        out_shape=[jax.ShapeDtypeStruct((Bl * S, RW), F32), jax.ShapeDtypeStruct((Bl, nC, NPAIR, 2 * HD, 2 * HD), F32)],
        scratch_shapes=[pltpu.VMEM((NPAIR, 2 * HD, 2 * HD), F32)],
        compiler_params=_params(("arbitrary", "arbitrary")),
    )(r, w, k, v, a, b)


def _wkv_bwd(r, w, k, v, a, b, dy, ck, Bl, S):
    L = WKV_L
    nC = S // L

    def body(r_ref, w_ref, k_ref, v_ref, a_ref, b_ref, dy_ref, ck_ref,
             dr_ref, dw_ref, dk_ref, dv_ref, da_ref, db_ref, ds_ref):
        @pl.when(pl.program_id(1) == 0)
        def _():
            ds_ref[...] = jnp.zeros(ds_ref.shape, F32)
        cols = [slice(p * 2 * HD, (p + 1) * 2 * HD) for p in range(NPAIR)]
        s0 = tuple(ck_ref[0, 0, p] for p in range(NPAIR))
        ops = [tuple(z[:, cols[p]] for p in range(NPAIR)) for z in (r_ref, w_ref, k_ref, v_ref, a_ref, b_ref)]
        cts = (tuple(dy_ref[:, cols[p]] for p in range(NPAIR)), tuple(ds_ref[p] for p in range(NPAIR)))
        ds0, *grads = jax.vjp(_wkv_chunks, s0, *ops)[1](cts)
        for p in range(NPAIR):
            ds_ref[p] = ds0[p]
            for o, g in zip((dr_ref, dw_ref, dk_ref, dv_ref, da_ref, db_ref), grads):
                o[:, cols[p]] = g[p]

    row_spec = pl.BlockSpec((L, RW), lambda bb, c: (bb * nC + nC - 1 - c, 0))
    rows = jax.ShapeDtypeStruct((Bl * S, RW), F32)
    return pl.pallas_call(
        body, name="wkv_bwd", grid=(Bl, nC),
        in_specs=[row_spec] * 7 + [pl.BlockSpec((1, 1, NPAIR, 2 * HD, 2 * HD), lambda bb, c: (bb, nC - 1 - c, 0, 0, 0))],
        out_specs=[row_spec] * 6, out_shape=[rows] * 6,
        scratch_shapes=[pltpu.VMEM((NPAIR, 2 * HD, 2 * HD), F32)],
        compiler_params=_params(("arbitrary", "arbitrary")),
    )(r, w, k, v, a, b, dy, ck)


NST = NG * SP


def _cmul(ar, ai, br, bi):
    return ar * br - ai * bi, ar * bi + ai * br


def _s5_tiles(are, aim, reverse):
    if reverse:
        aim = -aim
    row = lax.broadcasted_iota(jnp.int32, (SUBLANES, NST), 0)
    pw = [(are, aim)]
    for _ in range(SUBLANES - 1):
        pw.append(_cmul(pw[-1][0], pw[-1][1], are, aim))
    bc = lambda z: jnp.broadcast_to(z, (SUBLANES, NST))
    ms = []
    for kk in (1, 2, 4):
        cond = (row < SUBLANES - kk) if reverse else (row >= kk)
        ms.append((jnp.where(cond, bc(pw[kk - 1][0]), 0.0), jnp.where(cond, bc(pw[kk - 1][1]), 0.0)))
    pr = jnp.zeros((SUBLANES, NST), F32)
    pi = jnp.zeros((SUBLANES, NST), F32)
    for i in range(SUBLANES):
        n = SUBLANES - i if reverse else i + 1
        pr = jnp.where(row == i, bc(pw[n - 1][0]), pr)
        pi = jnp.where(row == i, bc(pw[n - 1][1]), pi)
    return ms, (pr, pi)


def _s5_block(re, im, ms, pc, cre, cim, sg, reverse):
    ln = slice(sg * 512, (sg + 1) * 512)
    for (mr, mi), kk in zip(ms, (1, 2, 4)):
        sh = SUBLANES - kk if reverse else kk
        sre, sim = pltpu.roll(re, sh, 0), pltpu.roll(im, sh, 0)
        tr, ti = _cmul(mr[:, ln], mi[:, ln], sre, sim)
        re, im = re + tr, im + ti
    tr, ti = _cmul(pc[0][:, ln], pc[1][:, ln], cre[:, ln], cim[:, ln])
    return re + tr, im + ti


def _s5_scan(X_ref, n_rows, ms, pc, cre, cim, reverse, visit=None, acc0=None):
    nblk = n_rows // SUBLANES

    def it(i, carry):
        cre, cim, acc = carry
        j = nblk - 1 - i if reverse else i
        rows = pl.ds(pl.multiple_of(j * SUBLANES, SUBLANES), SUBLANES)
        edge = 0 if reverse else SUBLANES - 1
        blocks, ncre, ncim = [], [], []
        for sg in range(NSG):
            lr = slice(sg * 1024, sg * 1024 + 512)
            li = slice(sg * 1024 + 512, (sg + 1) * 1024)
            re, im = _s5_block(X_ref[rows, lr], X_ref[rows, li], ms, pc, cre, cim, sg, reverse)
            X_ref[rows, lr] = re
            X_ref[rows, li] = im
            blocks.append((re, im))
            ncre.append(re[edge:edge + 1])
            ncim.append(im[edge:edge + 1])
        if visit is not None:
            acc = visit(j, blocks, acc)
        return jnp.concatenate(ncre, axis=1), jnp.concatenate(ncim, axis=1), acc

    return lax.fori_loop(0, nblk, it, (cre, cim, acc0 if acc0 is not None else 0))


def _s5_fwd(u, wb, wc, ab, Bl, S, R=256):
    R = min(R, S)
    nC = S // R

    def body(u_ref, wb_ref, wc_ref, ab_ref, y_ref, st_ref, X_ref, car_ref):
        @pl.when(pl.program_id(1) == 0)
        def _():
            car_ref[...] = jnp.zeros(car_ref.shape, F32)
        st_ref[0, 0] = car_ref[...]
        ms, pc = _s5_tiles(ab_ref[0:1], ab_ref[1:2], False)
        for sg in range(NSG):
            X_ref[:, sg * 1024:(sg + 1) * 1024] = _dotm(u_ref[:, sg * 128:(sg + 1) * 128], wb_ref[sg])
        cre, cim, _ = _s5_scan(X_ref, R, ms, pc, car_ref[0:1], car_ref[1:2], False)
        car_ref[0:1] = cre
        car_ref[1:2] = cim
        for sg in range(NSG):
            y_ref[:, sg * 128:(sg + 1) * 128] = _dotm(X_ref[:, sg * 1024:(sg + 1) * 1024], wc_ref[sg])

    return pl.pallas_call(
        body, name="s5_fwd", grid=(Bl, nC),
        in_specs=[pl.BlockSpec((R, SW), lambda b, c: (b * nC + c, 0)),
                  pl.BlockSpec(wb.shape, lambda b, c: (0, 0, 0)), pl.BlockSpec(wc.shape, lambda b, c: (0, 0, 0)),
                  pl.BlockSpec(ab.shape, lambda b, c: (0, 0))],
        out_specs=[pl.BlockSpec((R, SW), lambda b, c: (b * nC + c, 0)),
                   pl.BlockSpec((1, 1, 2, NST), lambda b, c: (b, c, 0, 0))],
        out_shape=[jax.ShapeDtypeStruct((Bl * S, SW), F32), jax.ShapeDtypeStruct((Bl, nC, 2, NST), F32)],
        scratch_shapes=[pltpu.VMEM((R, 2 * NST), F32), pltpu.VMEM((2, NST), F32)],
        compiler_params=_params(("arbitrary", "arbitrary")),
    )(u, wb, wc, ab)


def _s5_bwd(u, dy, wb, wc, ab, st, Bl, S, R=256):
    R = min(R, S)
    nC = S // R

    def body(u_ref, dy_ref, wb_ref, wc_ref, ab_ref, st_ref, du_ref, dwb_ref, dwc_ref, dab_ref,
             X_ref, G_ref, car_ref):
        first = (pl.program_id(0) == 0) & (pl.program_id(1) == 0)

        @pl.when(first)
        def _():
            dwb_ref[...] = jnp.zeros(dwb_ref.shape, F32)
            dwc_ref[...] = jnp.zeros(dwc_ref.shape, F32)
            dab_ref[...] = jnp.zeros(dab_ref.shape, F32)

        @pl.when(pl.program_id(1) == 0)
        def _():
            car_ref[...] = jnp.zeros(car_ref.shape, F32)

        are, aim = ab_ref[0:1], ab_ref[1:2]
        ms, pc = _s5_tiles(are, aim, False)
        for sg in range(NSG):
            X_ref[:, sg * 1024:(sg + 1) * 1024] = _dotm(u_ref[:, sg * 128:(sg + 1) * 128], wb_ref[sg])
        _s5_scan(X_ref, R, ms, pc, st_ref[0, 0, 0:1], st_ref[0, 0, 1:2], False)
        dyv = dy_ref[...].astype(MXU_DTYPE)
        for sg in range(NSG):
            G_ref[:, sg * 1024:(sg + 1) * 1024] = lax.dot_general(
                dyv[:, sg * 128:(sg + 1) * 128], wc_ref[sg].astype(MXU_DTYPE), (((1,), (1,)), ((), ())),
                preferred_element_type=F32)
        rms_, rpc = _s5_tiles(are, aim, True)
        row = lax.broadcasted_iota(jnp.int32, (SUBLANES, 512), 0)

        def visit(j, blocks, acc):
            before = pl.multiple_of(jnp.maximum(j - 1, 0) * SUBLANES, SUBLANES)
            prow = X_ref[pl.ds(before, SUBLANES), :][SUBLANES - 1:SUBLANES]
            rows = pl.ds(pl.multiple_of(j * SUBLANES, SUBLANES), SUBLANES)
            are_acc, aim_acc = [], []
            for sg in range(NSG):
                lr = slice(sg * 1024, sg * 1024 + 512)
                li = slice(sg * 1024 + 512, (sg + 1) * 1024)
                ln = slice(sg * 512, (sg + 1) * 512)
                pre = jnp.where(j > 0, prow[:, lr], st_ref[0, 0, 0:1, ln])
                pim = jnp.where(j > 0, prow[:, li], st_ref[0, 0, 1:2, ln])
                xre = jnp.where(row == 0, pre, pltpu.roll(X_ref[rows, lr], 1, 0))
                xim = jnp.where(row == 0, pim, pltpu.roll(X_ref[rows, li], 1, 0))
                dre, dim = blocks[sg]
                are_acc.append(dre * xre + dim * xim)
                aim_acc.append(dim * xre - dre * xim)
            return acc[0] + jnp.concatenate(are_acc, axis=1), acc[1] + jnp.concatenate(aim_acc, axis=1)

        zero = jnp.zeros((SUBLANES, NST), F32)
        cre, cim, acc = _s5_scan(G_ref, R, rms_, rpc, car_ref[0:1], car_ref[1:2], True, visit, (zero, zero))
        car_ref[0:1] = cre
        car_ref[1:2] = cim
        dab_ref[0:1] += jnp.sum(acc[0], axis=0, keepdims=True)
        dab_ref[1:2] += jnp.sum(acc[1], axis=0, keepdims=True)
        uv = u_ref[...].astype(MXU_DTYPE)
        for sg in range(NSG):
            cs = slice(sg * 1024, (sg + 1) * 1024)
            us = slice(sg * 128, (sg + 1) * 128)
            gx = G_ref[:, cs].astype(MXU_DTYPE)
            dwb_ref[sg] += lax.dot_general(uv[:, us], gx, (((0,), (0,)), ((), ())), preferred_element_type=F32)
            dwc_ref[sg] += lax.dot_general(X_ref[:, cs].astype(MXU_DTYPE), dyv[:, us], (((0,), (0,)), ((), ())),
                                           preferred_element_type=F32)
            du_ref[:, us] = lax.dot_general(gx, wb_ref[sg].astype(MXU_DTYPE), (((1,), (1,)), ((), ())),
                                            preferred_element_type=F32)

    rmap = lambda b, c: (b * nC + nC - 1 - c, 0)
    return pl.pallas_call(
        body, name="s5_bwd", grid=(Bl, nC),
        in_specs=[pl.BlockSpec((R, SW), rmap), pl.BlockSpec((R, SW), rmap),
                  pl.BlockSpec(wb.shape, lambda b, c: (0, 0, 0)), pl.BlockSpec(wc.shape, lambda b, c: (0, 0, 0)),
                  pl.BlockSpec(ab.shape, lambda b, c: (0, 0)),
                  pl.BlockSpec((1, 1, 2, NST), lambda b, c: (b, nC - 1 - c, 0, 0))],
        out_specs=[pl.BlockSpec((R, SW), rmap), pl.BlockSpec(wb.shape, lambda b, c: (0, 0, 0)),
                   pl.BlockSpec(wc.shape, lambda b, c: (0, 0, 0)), pl.BlockSpec((2, NST), lambda b, c: (0, 0))],
        out_shape=[jax.ShapeDtypeStruct((Bl * S, SW), F32), jax.ShapeDtypeStruct(wb.shape, F32),
                   jax.ShapeDtypeStruct(wc.shape, F32), jax.ShapeDtypeStruct((2, NST), F32)],
        scratch_shapes=[pltpu.VMEM((R, 2 * NST), F32), pltpu.VMEM((R, 2 * NST), F32), pltpu.VMEM((2, NST), F32)],
        compiler_params=_params(("arbitrary", "arbitrary")),
    )(u, dy, wb, wc, ab, st)


def _s5_disc_math(a_re, a_im, log_dt, b_re, b_im, expand):
    dt = jnp.exp(log_dt)
    z_re, z_im = a_re * dt, a_im * dt
    mag = jnp.exp(z_re)
    ab_re, ab_im = mag * jnp.cos(z_im), mag * jnp.sin(z_im)
    den = a_re * a_re + a_im * a_im
    q_re = ((ab_re - 1.0) * a_re + ab_im * a_im) / den
    q_im = (ab_im * a_re - (ab_re - 1.0) * a_im) / den
    qe_re = jnp.dot(q_re, expand, preferred_element_type=F32, precision=HIGHEST)
    qe_im = jnp.dot(q_im, expand, preferred_element_type=F32, precision=HIGHEST)
    return ab_re, ab_im, qe_re * b_re - qe_im * b_im, qe_re * b_im + qe_im * b_re


def _whole(shape):
    return pl.BlockSpec(shape, lambda nd=len(shape): (0,) * nd)


def _s5_disc(a_re, a_im, log_dt, b_re, b_im, expand):
    def body(a, b, c, d, e, f, o0, o1, o2, o3):
        res = _s5_disc_math(a[...], b[...], c[...], d[...], e[...], f[...])
        for o, v in zip((o0, o1, o2, o3), res):
            o[...] = v
    ins = (a_re, a_im, log_dt, b_re, b_im, expand)
    outs = [jax.ShapeDtypeStruct(a_re.shape, F32)] * 2 + [jax.ShapeDtypeStruct(b_re.shape, F32)] * 2
    return pl.pallas_call(body, name="s5_disc", in_specs=[_whole(x.shape) for x in ins],
                          out_specs=[_whole(o.shape) for o in outs], out_shape=outs)(*ins)


def _s5_disc_bwd(a_re, a_im, log_dt, b_re, b_im, expand, cts):
    def body(a, b, c, d, e, f, g0, g1, g2, g3, o0, o1, o2, o3, o4):
        fn = lambda *p: _s5_disc_math(*p, f[...])
        _, vjp = jax.vjp(fn, a[...], b[...], c[...], d[...], e[...])
        for o, v in zip((o0, o1, o2, o3, o4), vjp((g0[...], g1[...], g2[...], g3[...]))):
            o[...] = v
    ins = (a_re, a_im, log_dt, b_re, b_im, expand) + tuple(cts)
    outs = [jax.ShapeDtypeStruct(x.shape, F32) for x in (a_re, a_im, log_dt, b_re, b_im)]
    return pl.pallas_call(body, name="s5_disc_bwd", in_specs=[_whole(x.shape) for x in ins],
                          out_specs=[_whole(o.shape) for o in outs], out_shape=outs)(*ins)


def _ada_fwd(c_all, w_shard, b_shard):
    def body(c_ref, w_ref, b_ref, o_ref):
        cv = c_ref[...]
        o_ref[...] = _dotm(cv * _sigmoid(cv), w_ref[...]) + b_ref[...]
    n = w_shard.shape[1]
    return pl.pallas_call(
        body, name="ada_fwd", in_specs=[_whole(c_all.shape), _whole(w_shard.shape), _whole(b_shard.shape)],
        out_specs=_whole((c_all.shape[0], n)), out_shape=jax.ShapeDtypeStruct((c_all.shape[0], n), F32),
        compiler_params=_params(),
    )(c_all, w_shard, b_shard)


def _ada_bwd(c_all, dmod_cols, dmod_all):
    def body(c_ref, dc_ref, da_ref, gw_ref, gb_ref):
        cv = c_ref[...]
        gw_ref[...] = lax.dot_general((cv * _sigmoid(cv)).astype(MXU_DTYPE), dc_ref[...].astype(MXU_DTYPE),
                                      (((0,), (0,)), ((), ())), preferred_element_type=F32)
        gb_ref[...] = jnp.sum(da_ref[...], axis=0, keepdims=True)
    n = dmod_cols.shape[1]
    return pl.pallas_call(
        body, name="ada_bwd", in_specs=[_whole(c_all.shape), _whole(dmod_cols.shape), _whole(dmod_all.shape)],
        out_specs=[_whole((D, n)), _whole((1, dmod_all.shape[1]))],
        out_shape=[jax.ShapeDtypeStruct((D, n), F32), jax.ShapeDtypeStruct((1, dmod_all.shape[1]), F32)],
        compiler_params=_params(),
    )(c_all, dmod_cols, dmod_all)


def _rows_block(n_rows, cap=512):
    if n_rows <= cap:
        return n_rows
    for t in range(cap - cap % SUBLANES, 0, -SUBLANES):
        if n_rows % t == 0:
            return t
    return n_rows


def _adamw(w, g, m, v, name):
    rows, cols = w.shape
    tr = _rows_block(rows, max(SUBLANES, (1 << 19) // max(cols, 1) // SUBLANES * SUBLANES))

    def body(w_ref, g_ref, m_ref, v_ref, d_ref, nm_ref, nv_ref):
        gv = g_ref[...]
        nm = B1 * m_ref[...] + (1.0 - B1) * gv
        nv = B2 * v_ref[...] + (1.0 - B2) * (gv * gv)
        m_hat = nm / (1.0 - B1 ** STEP)
        v_hat = nv / (1.0 - B2 ** STEP)
        d_ref[...] = -LR * (m_hat / (jnp.sqrt(v_hat) + ADAM_EPS) + WD * w_ref[...])
        nm_ref[...] = nm
        nv_ref[...] = nv

    spec = pl.BlockSpec((tr, cols), lambda i: (i, 0))
    sd = jax.ShapeDtypeStruct((rows, cols), F32)
    return pl.pallas_call(body, name=name, grid=(rows // tr,), in_specs=[spec] * 4, out_specs=[spec] * 3,
                          out_shape=[sd] * 3, compiler_params=_params(("parallel",)))(w, g, m, v)


def _sum_slots(x, out_dtype, name):
    n, rows, cols = x.shape
    tr = _rows_block(rows)

    def body(x_ref, o_ref):
        acc = x_ref[0].astype(F32)
        for j in range(1, n):
            acc = acc + x_ref[j].astype(F32)
        o_ref[...] = acc.astype(o_ref.dtype)

    return pl.pallas_call(
        body, name=name, grid=(rows // tr,), in_specs=[pl.BlockSpec((n, tr, cols), lambda i: (0, i, 0))],
        out_specs=pl.BlockSpec((tr, cols), lambda i: (i, 0)), out_shape=jax.ShapeDtypeStruct((rows, cols), out_dtype),
        compiler_params=_params(("parallel",)))(x)


PACK_COLS = 1024


def _pack_rows(parts, dtype, row_mult):
    flat = jnp.concatenate([p.reshape(-1).astype(dtype) for p in parts])
    per = PACK_COLS * row_mult
    n = -(-flat.shape[0] // per) * per
    flat = jnp.pad(flat, (0, n - flat.shape[0]))
    return flat.reshape(n // PACK_COLS, PACK_COLS)


def _unpack(flat, shapes):
    out, off = [], 0
    for s in shapes:
        n = math.prod(s)
        out.append(flat[off:off + n].reshape(s))
        off += n
    return out


BIG = (("w_in", (D, SHIFT + SW + 2 * D), 1), ("w_out_rwkv", (RW, D), 1), ("w_glu", (SW, 2 * D), 1),
       ("w_out", (D, D), 0), ("w_ffn_up", (D, 2 * DFF), 1), ("w_ffn_down", (DFF, D), 0))
BIG_SMALL = (("rwkv_w_up", (LW, RW), 1), ("rwkv_a_up", (LA, RW), 1), ("rwkv_g_up", (LG, RW), 1),
             ("ffn_conv_w", (3, 2 * DFF), 1))


def _shard_shape(shape, axis):
    return (shape[0] // 4, shape[1]) if axis == 0 else (shape[0], shape[1] // 4)


def _to_shards(g, axis):
    r, C = g.shape
    return g.reshape(4, r // 4, C) if axis == 0 else g.reshape(r, 4, C // 4).transpose(1, 0, 2)


def _from_shards(x, axis):
    _, r, C = x.shape
    return x.reshape(4 * r, C) if axis == 0 else x.transpose(1, 0, 2).reshape(r, 4 * C)


def kernel(x, c, w_ada, b_ada, norm1_g, w_in, mu_shift, rwkv_w0, rwkv_w_up, rwkv_a0, rwkv_a_up, rwkv_g_up, rwkv_k_k, rwkv_k_a, rwkv_r_k, rwkv_ln_g, rwkv_ln_b, w_out_rwkv, s5_a_re, s5_a_im, s5_log_dt, s5_b_re, s5_b_im, s5_c_re, s5_c_im, s5_d, w_glu, w_out, norm2_g, w_ffn_up, ffn_conv_w, ffn_conv_b, w_ffn_down, norm_f_g, loss_target, m_w_ada, m_b_ada, m_norm1_g, m_w_in, m_mu_shift, m_rwkv_w0, m_rwkv_w_up, m_rwkv_a0, m_rwkv_a_up, m_rwkv_g_up, m_rwkv_k_k, m_rwkv_k_a, m_rwkv_r_k, m_rwkv_ln_g, m_rwkv_ln_b, m_w_out_rwkv, m_s5_a_re, m_s5_a_im, m_s5_log_dt, m_s5_b_re, m_s5_b_im, m_s5_c_re, m_s5_c_im, m_s5_d, m_w_glu, m_w_out, m_norm2_g, m_w_ffn_up, m_ffn_conv_w, m_ffn_conv_b, m_w_ffn_down, m_norm_f_g, v_w_ada, v_b_ada, v_norm1_g, v_w_in, v_mu_shift, v_rwkv_w0, v_rwkv_w_up, v_rwkv_a0, v_rwkv_a_up, v_rwkv_g_up, v_rwkv_k_k, v_rwkv_k_a, v_rwkv_r_k, v_rwkv_ln_g, v_rwkv_ln_b, v_w_out_rwkv, v_s5_a_re, v_s5_a_im, v_s5_log_dt, v_s5_b_re, v_s5_b_im, v_s5_c_re, v_s5_c_im, v_s5_d, v_w_glu, v_w_out, v_norm2_g, v_w_ffn_up, v_ffn_conv_w, v_ffn_conv_b, v_w_ffn_down, v_norm_f_g):
    names = ["w_ada", "b_ada", "norm1_g", "w_in", "mu_shift", "rwkv_w0", "rwkv_w_up", "rwkv_a0", "rwkv_a_up",
             "rwkv_g_up", "rwkv_k_k", "rwkv_k_a", "rwkv_r_k", "rwkv_ln_g", "rwkv_ln_b", "w_out_rwkv", "s5_a_re",
             "s5_a_im", "s5_log_dt", "s5_b_re", "s5_b_im", "s5_c_re", "s5_c_im", "s5_d", "w_glu", "w_out", "norm2_g",
             "w_ffn_up", "ffn_conv_w", "ffn_conv_b", "w_ffn_down", "norm_f_g"]
    env = dict(locals())
    W = {n: env[n] for n in names}
    M = {n: env["m_" + n] for n in names}
    V = {n: env["v_" + n] for n in names}

    Bl, S, _ = x.shape
    T = Bl * S
    ix, iy, ic = lax.axis_index("x"), lax.axis_index("y"), lax.axis_index("c")
    chip = 2 * ix + iy
    dev = 2 * chip + ic
    rw = functools.partial(_rowwise, Bl=Bl, S=S)

    chip_arrs = [W[n][0].astype(MXU_DTYPE) for n, _, _ in BIG] + [W[n][0] for n, _, _ in BIG_SMALL[:3]]
    got_chip, got_dev = _gather_two_level(chip_arrs, [W["ffn_conv_w"][0], c], "gather_w")
    full = {n: _from_shards(g, axis) for (n, _, axis), g in zip(BIG + BIG_SMALL[:3], got_chip)}
    full["ffn_conv_w"] = _from_shards(got_dev[0][:, 0], 1)
    c_all = got_dev[1].reshape(8 * Bl, D)
    w_p, w_u, w_g = full["w_in"][:, :SHIFT], full["w_in"][:, SHIFT:SHIFT + SW], full["w_in"][:, SHIFT + SW:]
    zeros_l = jnp.zeros((LW, RW), F32)
    w_up_p = jnp.concatenate([full["rwkv_w_up"], zeros_l], axis=0)
    a_up_p = jnp.concatenate([zeros_l, full["rwkv_a_up"]], axis=0)
    g_up = full["rwkv_g_up"]
    conv_w = full["ffn_conv_w"]
    conv_wg, conv_wu = conv_w[:, :DFF], conv_w[:, DFF:]
    conv_bg, conv_bu = ffn_conv_b[:, :DFF], ffn_conv_b[:, DFF:]
    hm = jnp.kron(jnp.eye(NH, dtype=F32), jnp.ones((HD, HD), F32))

    ncol = 6 * D // 4
    b_ada_cols = lax.dynamic_slice_in_dim(b_ada, chip * ncol, ncol, 1)
    mod_part = _ada_fwd(c_all, w_ada[0], b_ada_cols)
    mod4 = _gather_two_level([], [mod_part], "gather_mod")[1][0][:, 0]
    mod = lax.dynamic_slice_in_dim(mod4, dev * Bl, Bl, 1).transpose(1, 0, 2).reshape(Bl, 1, 6 * D)
    SH1, SC1, GT1, SH2, SC2, GT2 = range(6)

    x2d = x.reshape(T, D)
    tgt = loss_target.reshape(T, D)

    (h1,) = rw("norm1", lambda xv, sc, sh, g: _norm_mod(xv, g, sc, sh), R=256, tiled=[(x2d, D, 0)],
               batch=[(mod, D, SC1), (mod, D, SH1)], full=[norm1_g], out_tiled=[(D, MXU_DTYPE)])
    p = _mm([h1], [w_p], F32, "proj_p")
    u = _mm([h1], [w_u], F32, "proj_u")
    gates = _mm([h1], [w_g], F32, "proj_g")

    prep_params = [rwkv_w0, w_up_p, rwkv_a0, a_up_p, g_up, rwkv_k_k, rwkv_k_a, hm]

    def prep_fwd(pv, ph, mu, *pp):
        ps = pv + (_shift_down(pv, ph, 1) - pv) * mu
        return _rwkv_prep(*_split_ps(ps), *pp)

    r_, w_, k_, v_, a_, b_, g_ = rw("rwkv_prep", prep_fwd, R=256, tiled=[(p, SHIFT, 0)], prev=[(p, SHIFT, 0)],
                                    full=[mu_shift] + prep_params, out_tiled=[(RW, F32)] * 7)
    y_wkv, ck = _wkv_fwd(r_, w_, k_, v_, a_, b_, Bl, S)
    r_k_row = rwkv_r_k.reshape(1, RW)
    post_params = [rwkv_ln_g, rwkv_ln_b, r_k_row, hm]
    (o_rwkv,) = rw("rwkv_post", _rwkv_post, R=256,
                   tiled=[(y_wkv, RW, 0), (r_, RW, 0), (k_, RW, 0), (v_, RW, 0), (g_, RW, 0)],
                   full=post_params, out_tiled=[(RW, MXU_DTYPE)])
    y_a = _mm([o_rwkv], [full["w_out_rwkv"]], F32, "out_rwkv")

    expand = jnp.kron(jnp.eye(SP, dtype=F32), jnp.ones((1, SGC), F32))
    s5_in = (s5_a_re[0], s5_a_im[0], s5_log_dt[0].reshape(NG, 1), s5_b_re[0].reshape(NG, SP * SGC),
             s5_b_im[0].reshape(NG, SP * SGC), expand)
    ab_re, ab_im, bb_re, bb_im = _s5_disc(*s5_in)
    eye8 = jnp.eye(8, dtype=F32)

    def blockdiag_in(bb):
        t = bb.reshape(NSG, 8, SP, SGC)
        return jnp.einsum("ab,sapc->sacbp", eye8, t).reshape(NSG, 128, 512)

    def blockdiag_out(cc):
        t = cc.reshape(NSG, 8, SGC, SP)
        return jnp.einsum("ab,sacp->sapbc", eye8, t).reshape(NSG, 512, 128)

    wb = jnp.concatenate([blockdiag_in(bb_re), blockdiag_in(bb_im)], axis=2).astype(MXU_DTYPE)
    wc = jnp.concatenate([blockdiag_out(s5_c_re[0]), -blockdiag_out(s5_c_im[0])], axis=1).astype(MXU_DTYPE)
    ab = jnp.stack([ab_re.reshape(NST), ab_im.reshape(NST)])
    y_ssm, s5_st = _s5_fwd(u, wb, wc, ab, Bl, S)
    (s5o,) = rw("s5_post", _s5_post, R=256, tiled=[(y_ssm, SW, 0), (u, SW, 0)], full=[s5_d],
                out_tiled=[(SW, MXU_DTYPE)])
    z = _mm([s5o], [full["w_glu"]], F32, "glu")
    mix_tiled = [(gates, D, 0), (gates, D, 1), (y_a, D, 0), (z, D, 0), (z, D, 1)]
    (mixed_in,) = rw("mix", _mix, R=256, tiled=mix_tiled, out_tiled=[(D, MXU_DTYPE)])
    mixed = _mm([mixed_in], [full["w_out"]], F32, "out_proj")

    def norm2_fwd(xv, mx, gt, sc, sh, g):
        x1 = xv + gt * mx
        return x1, _norm_mod(x1, g, sc, sh)

    x1, h2 = rw("norm2", norm2_fwd, R=256, tiled=[(x2d, D, 0), (mixed, D, 0)],
                batch=[(mod, D, GT1), (mod, D, SC2), (mod, D, SH2)], full=[norm2_g],
                out_tiled=[(D, F32), (D, MXU_DTYPE)])
    up = _mm([h2], [full["w_ffn_up"]], F32, "ffn_up")
    conv_tiled = [(up, DFF, 0), (up, DFF, 1)]
    conv_full = [conv_wg, conv_wu, conv_bg, conv_bu]

    def act_fwd(*a):
        return _silu_gate(*_conv_act(*a))

    (act,) = rw("ffn_act", act_fwd, R=128, tiled=conv_tiled, prev=conv_tiled, full=conv_full,
                out_tiled=[(DFF, MXU_DTYPE)])
    ffn = _mm([act], [full["w_ffn_down"]], F32, "ffn_down")

    def head(x1v, fv, tv, gt, g):
        x2 = x1v + gt * fv
        y, vjp = jax.vjp(_rms, x2, g)
        e = y - tv
        dx2, dg = vjp(e * (1.0 / D))
        loss = jnp.sum(e * e, keepdims=True) * jnp.ones((1, LANES), F32)
        return dx2, dx2 * gt, jnp.sum(dx2 * fv, axis=0, keepdims=True), dg.reshape(1, D), loss

    dx2, d_ffn, d_gt2, g_norm_f, loss_acc = rw(
        "head", head, R=256, tiled=[(x1, D, 0), (ffn, D, 0), (tgt, D, 0)], batch=[(mod, D, GT2)],
        full=[norm_f_g.reshape(1, D)], out_tiled=[(D, F32), (D, MXU_DTYPE)], out_batch=[D],
        out_acc=[(1, D), (1, LANES)])
    loss = lax.psum(0.5 / D * loss_acc[0, 0], ("x", "y", "c"))

    d_act = _mm([d_ffn], [full["w_ffn_down"]], F32, "d_act", bt=True)
    g_w_ffn_down = _mm_tn(act, d_ffn, "g_ffn_down")

    def act_bwd(ug, uu, dact, hg, hu, wg, wu, bg, bu):
        gate, upv = _conv_act(ug, uu, hg, hu, wg, wu, bg, bu)
        _, vjp_s = jax.vjp(_silu_gate, gate, upv)
        d_gate, d_upv = vjp_s(dact)
        def taps(dh, xv, h):
            return [jnp.sum(dh * _shift_down(xv, h, 2), axis=0, keepdims=True),
                    jnp.sum(dh * _shift_down(xv, h, 1), axis=0, keepdims=True),
                    jnp.sum(dh * xv, axis=0, keepdims=True), jnp.sum(dh, axis=0, keepdims=True)]
        return (d_gate, d_upv, *taps(d_gate, ug, hg), *taps(d_upv, uu, hu))

    dh_g, dh_u, *tapg = rw(
        "ffn_act_bwd", act_bwd, R=128, tiled=conv_tiled + [(d_act, DFF, 0)], prev=conv_tiled, full=conv_full,
        out_tiled=[(DFF, F32), (DFF, F32)], out_acc=[(1, DFF)] * 8)
    g_cw_g, g_cb_g = jnp.concatenate(tapg[0:3], axis=0), tapg[3]
    g_cw_u, g_cb_u = jnp.concatenate(tapg[4:7], axis=0), tapg[7]

    def conv_t(dg, du_, ng, nu, wg, wu):
        def ct(d, n, w):
            return w[2:3] * d + w[1:2] * _shift_up(d, n, 1) + w[0:1] * _shift_up(d, n, 2)
        return jnp.concatenate([ct(dg, ng, wg), ct(du_, nu, wu)], axis=1)

    (d_up,) = rw("conv_bwd", conv_t, R=128, tiled=[(dh_g, DFF, 0), (dh_u, DFF, 0)],
                 nxt=[(dh_g, DFF, 0), (dh_u, DFF, 0)], full=[conv_wg, conv_wu], out_tiled=[(2 * DFF, MXU_DTYPE)])
    d_h2 = _mm([d_up], [full["w_ffn_up"]], F32, "d_h2", bt=True)
    g_w_ffn_up = _mm_tn(h2, d_up, "g_ffn_up")

    def norm2_bwd(x1v, dh2, dx2v, mx, gt, sc, sh, g):
        _, vjp = jax.vjp(_norm_mod, x1v, g, sc, sh)
        dxn, dg, dsc, dsh = vjp(dh2)
        dx1 = dx2v + dxn
        return dx1, dx1 * gt, jnp.sum(dx1 * mx, axis=0, keepdims=True), dsc, dsh, dg

    dx1, d_mixed, d_gt1, d_sc2, d_sh2, g_norm2 = rw(
        "norm2_bwd", norm2_bwd, R=256, tiled=[(x1, D, 0), (d_h2, D, 0), (dx2, D, 0), (mixed, D, 0)],
        batch=[(mod, D, GT1), (mod, D, SC2), (mod, D, SH2)], full=[norm2_g],
        out_tiled=[(D, F32), (D, MXU_DTYPE)], out_batch=[D, D, D], out_acc=[(1, D)])

    d_mixed_in = _mm([d_mixed], [full["w_out"]], F32, "d_mixed_in", bt=True)
    g_w_out = _mm_tn(mixed_in, d_mixed, "g_w_out")

    def mix_bwd(ga, gb, ya, za, zb, dm):
        _, vjp = jax.vjp(_mix, ga, gb, ya, za, zb)
        dga, dgb, dya, dza, dzb = vjp(dm)
        return jnp.concatenate([dga, dgb], axis=1), dya, jnp.concatenate([dza, dzb], axis=1)

    d_gates, d_ya, d_z = rw("mix_bwd", mix_bwd, R=256, tiled=mix_tiled + [(d_mixed_in, D, 0)],
                            out_tiled=[(2 * D, MXU_DTYPE), (D, MXU_DTYPE), (2 * D, MXU_DTYPE)])
    d_o_rwkv = _mm([d_ya], [full["w_out_rwkv"]], F32, "d_o_rwkv", bt=True)
    g_w_out_rwkv = _mm_tn(o_rwkv, d_ya, "g_out_rwkv")
    d_s5o = _mm([d_z], [full["w_glu"]], F32, "d_s5o", bt=True)
    g_w_glu = _mm_tn(s5o, d_z, "g_glu")

    def s5_post_bwd(ys, uv, ds, dd):
        _, vjp = jax.vjp(_s5_post, ys, uv, dd)
        return vjp(ds)

    d_yssm, d_u_direct, g_s5_d = rw("s5_post_bwd", s5_post_bwd, R=256,
                                    tiled=[(y_ssm, SW, 0), (u, SW, 0), (d_s5o, SW, 0)], full=[s5_d],
                                    out_tiled=[(SW, F32), (SW, F32)], out_acc=[(1, SW)])
    d_u_ssm, d_wb, d_wc, d_ab = _s5_bwd(u, d_yssm, wb, wc, ab, s5_st, Bl, S)

    def diag_in(dw):
        t = dw.reshape(NSG, 8, SGC, 8, SP)
        return jnp.einsum("ab,sacbp->sapc", eye8, t).reshape(NG, SP * SGC)

    def diag_out(dw):
        t = dw.reshape(NSG, 8, SP, 8, SGC)
        return jnp.einsum("ab,sapbc->sacp", eye8, t).reshape(NG, SGC, SP)

    g_s5_c_re = diag_out(d_wc[:, :512])
    g_s5_c_im = -diag_out(d_wc[:, 512:])
    disc_cts = (d_ab[0].reshape(NG, SP), d_ab[1].reshape(NG, SP), diag_in(d_wb[:, :, :512]), diag_in(d_wb[:, :, 512:]))
    g_a_re, g_a_im, g_log_dt, g_b_re, g_b_im = _s5_disc_bwd(*s5_in, disc_cts)

    def post_bwd(yv, rv, kv, vv, gv, do, *pp):
        _, vjp = jax.vjp(lambda *a: _rwkv_post(*a, pp[3]), yv, rv, kv, vv, gv, *pp[:3])
        return vjp(do)

    dy_wkv, dr_b, dk_b, dv_b, dg_, g_ln_g, g_ln_b, g_r_k = rw(
        "rwkv_post_bwd", post_bwd, R=256,
        tiled=[(y_wkv, RW, 0), (r_, RW, 0), (k_, RW, 0), (v_, RW, 0), (g_, RW, 0), (d_o_rwkv, RW, 0)],
        full=post_params, out_tiled=[(RW, F32)] * 5, out_acc=[(1, RW)] * 3)
    dr3, dw3, dk3, dv3, da3, db3 = _wkv_bwd(r_, w_, k_, v_, a_, b_, dy_wkv, ck, Bl, S)

    def prep_bwd(pv, dr1, dr2, dwv, dk1, dk2, dv1, dv2, dav, dbv, dgv, ph, mu, *pp):
        prev = _shift_down(pv, ph, 1)
        ps = pv + (prev - pv) * mu
        _, vjp = jax.vjp(lambda *q: _rwkv_prep(*q, pp[7]), *_split_ps(ps), *pp[:7])
        grads = vjp((dr1 + dr2, dwv, dk1 + dk2, dv1 + dv2, dav, dbv, dgv))
        dps = jnp.concatenate(grads[:5], axis=1)
        return (dps,) + tuple(grads[5:]) + (jnp.sum(dps * (prev - pv), axis=0, keepdims=True),)

    prep_outs = rw(
        "rwkv_prep_bwd", prep_bwd, R=256,
        tiled=[(p, SHIFT, 0), (dr3, RW, 0), (dr_b, RW, 0), (dw3, RW, 0), (dk3, RW, 0), (dk_b, RW, 0),
               (dv3, RW, 0), (dv_b, RW, 0), (da3, RW, 0), (db3, RW, 0), (dg_, RW, 0)],
        prev=[(p, SHIFT, 0)], full=[mu_shift] + prep_params,
        out_tiled=[(SHIFT, F32)],
        out_acc=[(1, RW), (LW + LA, RW), (1, RW), (LW + LA, RW), (LG, RW), (1, RW), (1, RW), (1, SHIFT)])
    d_ps, g_w0, g_w_up_p, g_a0, g_a_up_p, g_g_up, g_k_k, g_k_a, g_mu = prep_outs

    def shift_bwd(dps, nx, mu):
        return dps * (1.0 - mu) + _shift_up(dps * mu, nx * mu, 1)

    (d_p,) = rw("shift_bwd", shift_bwd, R=256, tiled=[(d_ps, SHIFT, 0)], nxt=[(d_ps, SHIFT, 0)], full=[mu_shift],
                out_tiled=[(SHIFT, MXU_DTYPE)])
    (d_u,) = rw("d_u", lambda a1, a2: a1 + a2, R=256, tiled=[(d_u_direct, SW, 0), (d_u_ssm, SW, 0)],
                out_tiled=[(SW, MXU_DTYPE)])
    d_h1 = _mm([d_p, d_u, d_gates], [w_p, w_u, w_g], F32, "d_h1", bt=True)
    g_w_in = jnp.concatenate([_mm_tn(h1, d_p, "g_w_p"), _mm_tn(h1, d_u, "g_w_u"), _mm_tn(h1, d_gates, "g_w_g")], axis=1)

    def norm1_bwd(xv, dh1, dx1v, sc, sh, g):
        _, vjp = jax.vjp(_norm_mod, xv, g, sc, sh)
        dxn, dg, dsc, dsh = vjp(dh1)
        return dx1v + dxn, dsc, dsh, dg

    grad_x, d_sc1, d_sh1, g_norm1 = rw(
        "norm1_bwd", norm1_bwd, R=256, tiled=[(x2d, D, 0), (d_h1, D, 0), (dx1, D, 0)],
        batch=[(mod, D, SC1), (mod, D, SH1)], full=[norm1_g], out_tiled=[(D, F32)], out_batch=[D, D], out_acc=[(1, D)])

    dmod = jnp.concatenate([d_sh1, d_sc1, d_gt1, d_sh2, d_sc2, d_gt2], axis=2).reshape(Bl, 6 * D)
    dmod_all = _gather_two_level([], [dmod], "gather_dmod")[1][0].reshape(8 * Bl, 6 * D)
    dmod_cols = lax.dynamic_slice_in_dim(dmod_all, chip * ncol, ncol, 1)
    g_w_ada, g_b_ada = _ada_bwd(c_all, dmod_cols, dmod_all)

    small = {"norm1_g": g_norm1, "mu_shift": g_mu, "rwkv_w0": g_w0, "rwkv_a0": g_a0, "rwkv_k_k": g_k_k,
             "rwkv_k_a": g_k_a, "rwkv_r_k": g_r_k, "rwkv_ln_g": g_ln_g, "rwkv_ln_b": g_ln_b, "s5_a_re": g_a_re,
             "s5_a_im": g_a_im, "s5_log_dt": g_log_dt, "s5_b_re": g_b_re, "s5_b_im": g_b_im, "s5_c_re": g_s5_c_re,
             "s5_c_im": g_s5_c_im, "s5_d": g_s5_d, "norm2_g": g_norm2,
             "ffn_conv_b": jnp.concatenate([g_cb_g, g_cb_u], axis=1), "norm_f_g": g_norm_f}
    small_names = list(small)
    g_conv_w = jnp.concatenate([g_cw_g, g_cw_u], axis=1)
    shard_small = {"rwkv_w_up": g_w_up_p[:LW], "rwkv_a_up": g_a_up_p[LW:], "rwkv_g_up": g_g_up, "ffn_conv_w": g_conv_w}
    parts = [small[n] for n in small_names] + [_to_shards(shard_small[n], ax) for n, _, ax in BIG_SMALL]
    spack = _pack_rows(parts, F32, SUBLANES)
    s_all = _gather_two_level([], [spack], "gather_gsmall")[1][0]
    s_sum = _sum_slots(s_all.reshape((8,) + spack.shape), F32, "sum_gsmall").reshape(-1)
    grads = {}
    off = 0
    for n in small_names:
        grads[n] = s_sum[off:off + W[n].size].reshape(W[n].shape)
        off += W[n].size
    for n, shape, axis in BIG_SMALL:
        ss = _shard_shape(shape, axis)
        k4 = 4 * math.prod(ss)
        sh4 = s_sum[off:off + k4].reshape(4, math.prod(ss))
        grads[n] = lax.dynamic_index_in_dim(sh4, chip, 0, keepdims=False).reshape((1,) + ss)
        off += k4

    big_g = {"w_in": g_w_in, "w_out_rwkv": g_w_out_rwkv, "w_glu": g_w_glu, "w_out": g_w_out,
             "w_ffn_up": g_w_ffn_up, "w_ffn_down": g_w_ffn_down}
    gsh = [_to_shards(big_g[n], ax).astype(MXU_DTYPE) for n, _, ax in BIG]
    nbig = len(gsh)
    sds = jax.ShapeDtypeStruct
    moves = [(i, i, lambda ref, me, peer, j=j, r=g.shape[1]: ref.at[j, _half(r, peer[2])],
              lambda ref, me, j=j: ref.at[me[2], j]) for i, g in enumerate(gsh) for j in range(4)]
    pair = _exchange("rs_pair", PAIR_FLIPS, gsh, [sds((2, 4, g.shape[1] // 2, g.shape[2]), MXU_DTYPE) for g in gsh], moves)
    chip_part = [_sum_slots(p.reshape(2, -1, p.shape[-1]), MXU_DTYPE, "rs_pair_sum%d" % i).reshape(p.shape[1:])
                 for i, p in enumerate(pair)]
    moves = [(i, i, lambda ref, me, peer: ref.at[_chip_of(peer)], lambda ref, me: ref.at[_chip_of(me)])
             for i in range(nbig)]
    recv = _exchange("rs_chips", CHIP_FLIPS, chip_part, [sds(p.shape, MXU_DTYPE) for p in chip_part], moves)
    g_half = [_sum_slots(rv, F32, "rs_chip_sum%d" % i) for i, rv in enumerate(recv)]
    moves = [(i, i, lambda ref, me, peer: ref, lambda ref, me, r=2 * g.shape[0]: ref.at[_half(r, me[2])])
             for i, g in enumerate(g_half)]
    g_full = _exchange("rs_share", PAIR_FLIPS, g_half, [sds((2 * g.shape[0], g.shape[1]), F32) for g in g_half], moves)
    for (n, _, _), g in zip(BIG, g_full):
        grads[n] = g[None]
    grads["w_ada"] = g_w_ada[None]
    grads["b_ada"] = g_b_ada

    delta, new_m, new_v = {}, {}, {}
    to2 = lambda z: z.reshape(-1, z.shape[-1])
    for n in ["w_ada"] + [b[0] for b in BIG]:
        d_, m_, v2_ = _adamw(to2(W[n]), to2(grads[n]), to2(M[n]), to2(V[n]), "adamw_" + n)
        delta[n], new_m[n], new_v[n] = (z.reshape(W[n].shape) for z in (d_, m_, v2_))
    rest = [n for n in names if n not in delta]
    packs = [_pack_rows([src[n] for n in rest], F32, SUBLANES) for src in (W, grads, M, V)]
    d_, m_, v2_ = _adamw(*packs, "adamw_small")
    shapes = [W[n].shape for n in rest]
    for dst, z in ((delta, d_), (new_m, m_), (new_v, v2_)):
        for n, val in zip(rest, _unpack(z.reshape(-1), shapes)):
            dst[n] = val

    return (loss, grad_x.reshape(Bl, S, D), *[grads[n] for n in names], *[delta[n] for n in names],
            *[new_m[n] for n in names], *[new_v[n] for n in names])
```

```python
import functools
import math

import jax
import jax.numpy as jnp
from jax import lax
from jax.experimental import pallas as pl
from jax.experimental.pallas import tpu as pltpu

F32 = jnp.float32
BF16 = jnp.bfloat16
MXU_DTYPE = jnp.bfloat16
MESH_IDS = pl.DeviceIdType.MESH
HIGHEST = lax.Precision.HIGHEST

D = 1024
RW, NH, HD = 512, 8, 64
LW, LA, LG = 64, 64, 128
SW, SGC, NG, SP = 512, 16, 32, 64
NSG = 4
SHIFT = 3 * RW + LW + LA + LG
DFF = 2816
RMS_EPS, GN_EPS, L2_EPS = 1e-6, 64e-5, 1e-12
LR, B1, B2, ADAM_EPS, WD, STEP = 0.001, 0.9, 0.999, 1e-8, 0.01, 10
DECAY_SCALE = math.exp(-0.5)
GELU_C = math.sqrt(2.0 / math.pi)

VMEM_LIMIT = 52 * 1024 * 1024
SUBLANES, LANES = 8, 128


def _pick(n, cap):
    if n <= cap:
        return n
    best = None
    for t in range(LANES, cap + 1, LANES):
        if n % t == 0:
            best = t
    assert best is not None, (n, cap)
    return best


def _params(sem=None, vmem=VMEM_LIMIT):
    return pltpu.CompilerParams(dimension_semantics=sem, vmem_limit_bytes=vmem)


def _chip_of(p):
    return 2 * p[0] + p[1]


def _me():
    return (lax.axis_index("x"), lax.axis_index("y"), lax.axis_index("c"))


def _half(rows, core):
    h = rows // 2
    return pl.ds(pl.multiple_of(core * h, 16 if h % 16 == 0 else SUBLANES), h)


def _exchange(name, flips, srcs, outs, moves, stage=None):
    ns, no, nf, nm = len(srcs), len(outs), len(flips), len(moves)
    nstage = nm * nf if stage else 0

    def body(*refs):
        src_refs, out_refs = refs[:ns], refs[ns:ns + no]
        send_sems, recv_sems, loc_sems = refs[ns + no:ns + no + 3]
        stage_bufs = refs[ns + no + 3:ns + no + 3 + nstage]
        me = _me()
        copies, locs, loads = [], [], []
        for m, (si, oi, src_sel, dst_sel) in enumerate(moves):
            for k, f in enumerate(flips):
                peer = tuple(1 - v if b else v for v, b in zip(me, f))
                piece = src_sel(src_refs[si], me, peer)
                if stage:
                    ld = pltpu.make_async_copy(piece, stage_bufs[m * nf + k], refs[-1].at[m * nf + k])
                    ld.start()
                    loads.append(ld)
                    piece = stage_bufs[m * nf + k]
                copies.append(pltpu.make_async_remote_copy(
                    src_ref=piece, dst_ref=dst_sel(out_refs[oi], me),
                    send_sem=send_sems.at[m * nf + k], recv_sem=recv_sems.at[m * nf + k],
                    device_id=peer, device_id_type=MESH_IDS))
            loc = pltpu.make_async_copy(src_sel(src_refs[si], me, me), dst_sel(out_refs[oi], me), loc_sems.at[m])
            loc.start()
            locs.append(loc)
        for i, cp in enumerate(copies):
            if stage:
                loads[i].wait()
            cp.start()
        for cp in copies:
            cp.wait_recv()
        for cp in copies:
            cp.wait_send()
        for loc in locs:
            loc.wait()

    scratch = [pltpu.SemaphoreType.DMA((nm * nf,)), pltpu.SemaphoreType.DMA((nm * nf,)), pltpu.SemaphoreType.DMA((nm,))]
    if stage:
        scratch += [pltpu.VMEM(shp, dt) for shp, dt in stage for _ in flips] + [pltpu.SemaphoreType.DMA((nstage,))]
    return pl.pallas_call(
        body, name=name, out_shape=list(outs),
        in_specs=[pl.BlockSpec(memory_space=pl.ANY)] * ns,
        out_specs=[pl.BlockSpec(memory_space=pl.ANY)] * no,
        scratch_shapes=scratch, compiler_params=_params(),
    )(*srcs)


CHIP_FLIPS = ((1, 0, 0), (0, 1, 0), (1, 1, 0))
PAIR_FLIPS = ((0, 0, 1),)


def _gather_two_level(chip_arrs, dev_arrs, name):
    arrs = list(chip_arrs) + list(dev_arrs)
    n, nchip = len(arrs), len(chip_arrs)
    NS = 7

    def body(*refs):
        srcs, outs = refs[:n], refs[n:2 * n]
        send_sems, recv_sems, loc_sems = refs[2 * n:]
        x, y, c = _me()
        sib = (x, y, 1 - c)
        chips = [(1 - x, y), (x, 1 - y), (1 - x, 1 - y)]
        mine = 2 * x + y
        ids = [2 * cx + cy for cx, cy in chips]

        def part(i, slot, core):
            if i < nchip:
                return outs[i].at[slot, _half(arrs[i].shape[0], core)]
            return outs[i].at[slot, core]

        def rcopy(i, k, src, dst, to):
            return pltpu.make_async_remote_copy(src_ref=src, dst_ref=dst, send_sem=send_sems.at[i * NS + k],
                                                recv_sem=recv_sems.at[i * NS + k], device_id=to, device_id_type=MESH_IDS)

        started, locs = [], []
        for i in range(n):
            own = srcs[i].at[_half(arrs[i].shape[0], c)] if i < nchip else srcs[i]
            loc = pltpu.make_async_copy(srcs[i], outs[i].at[mine] if i < nchip else outs[i].at[mine, c], loc_sems.at[i])
            loc.start()
            locs.append(loc)
            for f, chip in enumerate(chips):
                cp = rcopy(i, f, own, part(i, mine, c), (*chip, c))
                cp.start()
                started.append(cp)
            if i >= nchip:
                cp = rcopy(i, 6, own, part(i, mine, c), sib)
                cp.start()
                started.append(cp)
        for i in range(n):
            for f in range(3):
                land = part(i, ids[f], c)
                rcopy(i, f, land, land, sib).wait_recv()
                fw = rcopy(i, 3 + f, land, land, sib)
                fw.start()
                started.append(fw)
        for i in range(n):
            for f in range(3):
                land = part(i, ids[f], 1 - c)
                rcopy(i, 3 + f, land, land, sib).wait_recv()
            if i >= nchip:
                land = part(i, mine, 1 - c)
                rcopy(i, 6, land, land, sib).wait_recv()
        for cp in started:
            cp.wait_send()
        for loc in locs:
            loc.wait()

    outs = [jax.ShapeDtypeStruct((4,) + a.shape, a.dtype) for a in chip_arrs]
    outs += [jax.ShapeDtypeStruct((4, 2) + a.shape, a.dtype) for a in dev_arrs]
    res = pl.pallas_call(
        body, name=name, out_shape=outs,
        in_specs=[pl.BlockSpec(memory_space=pl.ANY)] * n, out_specs=[pl.BlockSpec(memory_space=pl.ANY)] * n,
        scratch_shapes=[pltpu.SemaphoreType.DMA((n * NS,)), pltpu.SemaphoreType.DMA((n * NS,)),
                        pltpu.SemaphoreType.DMA((n,))],
    )(*arrs)
    return res[:nchip], res[nchip:]


def _mm(As, Bs, out_dtype, name, tm=512, cap=1408, bt=False):
    n = len(As)
    M, N = As[0].shape[0], Bs[0].shape[0 if bt else 1]
    tm = min(tm, M)
    tn = _pick(N, cap)
    dims = (((1,), (1,)), ((), ())) if bt else (((1,), (0,)), ((), ()))

    def body(*refs):
        o = refs[2 * n]
        acc = None
        for a, b in zip(refs[:n], refs[n:2 * n]):
            d = lax.dot_general(a[...].astype(MXU_DTYPE), b[...].astype(MXU_DTYPE), dims, preferred_element_type=F32)
            acc = d if acc is None else acc + d
        o[...] = acc.astype(o.dtype)

    in_specs = [pl.BlockSpec((tm, a.shape[1]), lambda i, j: (i, 0)) for a in As]
    if bt:
        in_specs += [pl.BlockSpec((tn, b.shape[1]), lambda i, j: (j, 0)) for b in Bs]
    else:
        in_specs += [pl.BlockSpec((b.shape[0], tn), lambda i, j: (0, j)) for b in Bs]
    return pl.pallas_call(
        body, name=name, grid=(M // tm, N // tn), in_specs=in_specs,
        out_specs=pl.BlockSpec((tm, tn), lambda i, j: (i, j)),
        out_shape=jax.ShapeDtypeStruct((M, N), out_dtype),
        compiler_params=_params(("parallel", "parallel")),
    )(*As, *Bs)


def _mm_tn(A, G, name, tt=1024, cap=1024):
    T, Ka = A.shape
    N = G.shape[1]
    tt = min(tt, T)
    tk = _pick(Ka, cap)
    tn = _pick(N, cap)

    def body(a, g, o):
        @pl.when(pl.program_id(2) == 0)
        def _():
            o[...] = jnp.zeros(o.shape, F32)
        o[...] += lax.dot_general(a[...].astype(MXU_DTYPE), g[...].astype(MXU_DTYPE),
                                  (((0,), (0,)), ((), ())), preferred_element_type=F32)

    return pl.pallas_call(
        body, name=name, grid=(Ka // tk, N // tn, T // tt),
        in_specs=[pl.BlockSpec((tt, tk), lambda i, j, t: (t, i)), pl.BlockSpec((tt, tn), lambda i, j, t: (t, j))],
        out_specs=pl.BlockSpec((tk, tn), lambda i, j, t: (i, j)),
        out_shape=jax.ShapeDtypeStruct((Ka, N), F32),
        compiler_params=_params(("parallel", "parallel", "arbitrary")),
    )(A, G)


def _rowwise(name, fn, *, Bl, S, R, tiled=(), prev=(), nxt=(), batch=(), full=(),
             out_tiled=(), out_batch=(), out_acc=()):
    R = min(R, S)
    nS = S // R
    T = Bl * S
    hb = R // SUBLANES
    n_in = len(tiled) + len(prev) + len(nxt) + len(batch) + len(full)

    in_specs, args = [], []
    for a, wd, cb in tiled:
        in_specs.append(pl.BlockSpec((R, wd), lambda b, i, cb=cb: (b * nS + i, cb)))
        args.append(a)
    for a, wd, cb in prev:
        in_specs.append(pl.BlockSpec((SUBLANES, wd), lambda b, i, cb=cb: (jnp.maximum((b * nS + i) * hb - 1, 0), cb)))
        args.append(a)
    for a, wd, cb in nxt:
        in_specs.append(pl.BlockSpec((SUBLANES, wd), lambda b, i, cb=cb: (jnp.minimum((b * nS + i + 1) * hb, T // SUBLANES - 1), cb)))
        args.append(a)
    for a, wd, cb in batch:
        in_specs.append(pl.BlockSpec((1, 1, wd), lambda b, i, cb=cb: (b, 0, cb)))
        args.append(a)
    for a in full:
        in_specs.append(pl.BlockSpec(a.shape, lambda b, i, nd=a.ndim: (0,) * nd))
        args.append(a)

    out_specs, out_shape = [], []
    for C, dt in out_tiled:
        out_specs.append(pl.BlockSpec((R, C), lambda b, i: (b * nS + i, 0)))
        out_shape.append(jax.ShapeDtypeStruct((T, C), dt))
    for C in out_batch:
        out_specs.append(pl.BlockSpec((1, 1, C), lambda b, i: (b, 0, 0)))
        out_shape.append(jax.ShapeDtypeStruct((Bl, 1, C), F32))
    for shp in out_acc:
        out_specs.append(pl.BlockSpec(shp, lambda b, i, nd=len(shp): (0,) * nd))
        out_shape.append(jax.ShapeDtypeStruct(shp, F32))

    nt, npv, nnx, nbt = len(tiled), len(prev), len(nxt), len(batch)

    def body(*refs):
        b, i = pl.program_id(0), pl.program_id(1)
        ins, outs = refs[:n_in], refs[n_in:]
        vals = [r[...] for r in ins[:nt]]
        vals += [jnp.where(i > 0, r[...], jnp.zeros(r.shape, r.dtype)) for r in ins[nt:nt + npv]]
        vals += [jnp.where(i < nS - 1, r[...], jnp.zeros(r.shape, r.dtype)) for r in ins[nt + npv:nt + npv + nnx]]
        vals += [r[0] for r in ins[nt + npv + nnx:nt + npv + nnx + nbt]]
        vals += [r[...] for r in ins[nt + npv + nnx + nbt:]]
        res = fn(*vals)
        if not isinstance(res, (tuple, list)):
            res = (res,)
        k = 0
        for _ in out_tiled:
            outs[k][...] = res[k].astype(outs[k].dtype)
            k += 1
        for _ in out_batch:
            o = outs[k]

            @pl.when(i == 0)
            def _(o=o):
                o[...] = jnp.zeros(o.shape, F32)
            o[0] += res[k]
            k += 1
        for _ in out_acc:
            o = outs[k]

            @pl.when((i == 0) & (b == 0))
            def _(o=o):
                o[...] = jnp.zeros(o.shape, F32)
            o[...] += res[k]
            k += 1

    out = pl.pallas_call(
        body, name=name, grid=(Bl, nS), in_specs=in_specs, out_specs=out_specs, out_shape=out_shape,
        compiler_params=_params(("arbitrary", "arbitrary")),
    )(*args)
    return out


def _shift_down(x, halo, k):
    row = lax.broadcasted_iota(jnp.int32, x.shape, 0)
    out = pltpu.roll(x, k, 0)
    for j in range(k):
        out = jnp.where(row == j, halo[SUBLANES - k + j:SUBLANES - k + j + 1, :], out)
    return out


def _shift_up(x, halo, k):
    n = x.shape[0]
    row = lax.broadcasted_iota(jnp.int32, x.shape, 0)
    out = pltpu.roll(x, n - k, 0)
    for j in range(k):
        out = jnp.where(row == n - k + j, halo[j:j + 1, :], out)
    return out


def _dotm(a, b):
    return jnp.dot(a.astype(MXU_DTYPE), b.astype(MXU_DTYPE), preferred_element_type=F32)


def _split_bf16(x):
    hi = x.astype(BF16)
    return hi, (x - hi.astype(F32)).astype(BF16)


def _headsum_2pass(x, hm):
    hi, lo = _split_bf16(x)
    hb = hm.astype(BF16)
    return jnp.dot(hi, hb, preferred_element_type=F32) + jnp.dot(lo, hb, preferred_element_type=F32)


@jax.custom_vjp
def _headsum(x, hm):
    return _headsum_2pass(x, hm)


_headsum.defvjp(lambda x, hm: (_headsum_2pass(x, hm), hm),
                lambda hm, g: (_headsum_2pass(g, hm), jnp.zeros_like(hm)))


def _sigmoid(x):
    return 1.0 / (1.0 + jnp.exp(-x))


def _rms(x, g):
    return x * lax.rsqrt(jnp.mean(x * x, axis=-1, keepdims=True) + RMS_EPS) * g


def _norm_mod(x, g, sc, sh):
    return _rms(x, g) * (1.0 + sc) + sh


def _split_ps(ps):
    return (ps[:, 0:RW], ps[:, RW:2 * RW], ps[:, 2 * RW:3 * RW], ps[:, 3 * RW:3 * RW + LW + LA],
            ps[:, 3 * RW + LW + LA:SHIFT])


def _rwkv_prep(r, k, v, wa, gd, w0, w_up_p, a0, a_up_p, g_up, k_k, k_a, hm):
    w_raw = w0 + _dotm(jnp.tanh(wa), w_up_p)
    decay = jnp.exp(-DECAY_SCALE * _sigmoid(w_raw))
    a = _sigmoid(a0 + _dotm(wa, a_up_p))
    g = _dotm(_sigmoid(gd), g_up)
    kk = k * k_k
    kk = kk * lax.rsqrt(_headsum(kk * kk, hm) + L2_EPS)
    k2 = k * (1.0 + (a - 1.0) * k_a)
    return r, decay, k2, v, -kk, kk * a, g


def _rwkv_post(y, r, k2, v, g, ln_g, ln_b, r_k, hm):
    mean = _headsum(y, hm) * (1.0 / HD)
    yc = y - mean
    var = _headsum(yc * yc, hm) * (1.0 / HD)
    yn = yc * lax.rsqrt(var + GN_EPS) * ln_g + ln_b
    bonus = _headsum(r * k2 * r_k, hm) * v
    return (yn + bonus) * g


def _gelu(x):
    return 0.5 * x * (1.0 + jnp.tanh(GELU_C * (x + 0.044715 * (x * x * x))))


def _s5_post(yssm, u, d):
    return _gelu(yssm + d * u)


def _mix(ga, gb, ya, za, zb):
    return _sigmoid(ga) * ya + _sigmoid(gb) * (za * _sigmoid(zb))


def _conv_act(up_g, up_u, hg, hu, w_g, w_u, b_g, b_u):
    def conv(x, h, w, b):
        return b + w[0:1] * _shift_down(x, h, 2) + w[1:2] * _shift_down(x, h, 1) + w[2:3] * x
    gate = conv(up_g, hg, w_g, b_g)
    upv = conv(up_u, hu, w_u, b_u)
    return gate, upv


def _silu_gate(gate, upv):
    return gate * _sigmoid(gate) * upv


WKV_L = 64
_NT, _NN, _TN = ((1,), (1,)), ((1,), (0,)), ((0,), (0,))


def _dotw(x, y, dims):
    return lax.dot_general(x.astype(MXU_DTYPE), y.astype(MXU_DTYPE), (dims, ((), ())), preferred_element_type=F32)


def _dot3(x, y, dims):
    (xh, xl), (yh, yl) = _split_bf16(x), _split_bf16(y)
    d = lambda p, q: lax.dot_general(p, q, (dims, ((), ())), preferred_element_type=F32)
    return d(xh, yh) + d(xh, yl) + d(xl, yh)


@jax.custom_vjp
def _gram3(x, y):
    return _dot3(x, y, _NT)


_gram3.defvjp(lambda x, y: (_dot3(x, y, _NT), (x, y)),
              lambda res, g: (_dot3(g, res[1], _NN), _dot3(g, res[0], _TN)))


def _wkv_chunk(s0, r, w, k, v, a, b):
    y, s1 = _wkv_chunks((s0,), (r,), (w,), (k,), (v,), (a,), (b,))
    return y[0], s1[0]


def _wkv_chunks(s0, r, w, k, v, a, b):
    each = lambda f, *ls: tuple(f(*xs) for xs in zip(*ls))
    L = r[0].shape[0]
    n2 = 2 * L
    lane_head = lax.broadcasted_iota(jnp.int32, (2, 1, 2 * HD), 2) // HD
    head_mask = (lane_head == lax.broadcasted_iota(jnp.int32, (2, 1, 2 * HD), 0)).astype(F32)
    ri = lax.broadcasted_iota(jnp.int32, (n2, n2), 0)
    ci = lax.broadcasted_iota(jnp.int32, (n2, n2), 1)
    same = (ri // L) == (ci // L)
    strict = same & ((ci % L) < (ri % L))
    incl = same & ((ci % L) <= (ri % L))
    si = lax.broadcasted_iota(jnp.int32, (2 * HD, 2 * HD), 0) // HD
    sj = lax.broadcasted_iota(jnp.int32, (2 * HD, 2 * HD), 1) // HD
    tri = (lax.broadcasted_iota(jnp.int32, (L, L), 0) >= lax.broadcasted_iota(jnp.int32, (L, L), 1)).astype(F32)

    stack = lambda z: (z[None] * head_mask).reshape(n2, 2 * HD)
    dup = lambda z: jnp.broadcast_to(z[None], (2, L, 2 * HD)).reshape(n2, 2 * HD)
    gram = _gram3
    nt, nn, tn = (lambda x, y, d=d: _dotw(x, y, d) for d in (_NT, _NN, _TN))
    add = lambda x, y: x + y

    lw = each(jnp.log, w)
    cum = each(lambda z: jnp.dot(tri, z, preferred_element_type=F32, precision=HIGHEST), lw)
    tot = each(lambda z: jnp.sum(z, axis=0, keepdims=True), lw)
    a2 = each(lambda av, cv, lv: stack(av * jnp.exp(cv - lv)), a, cum, lw)
    r2 = each(lambda rv, cv: stack(rv * jnp.exp(cv)), r, cum)
    v2 = each(stack, v)
    b2 = each(lambda bv, cv: dup(bv * jnp.exp(-cv)), b, cum)
    k2 = each(lambda kv, cv: dup(kv * jnp.exp(-cv)), k, cum)
    n_ab = each(lambda x, y: jnp.where(strict, gram(x, y), 0.0), a2, b2)
    n_ak = each(lambda x, y: jnp.where(strict, gram(x, y), 0.0), a2, k2)
    m_rb = each(lambda x, y: jnp.where(incl, gram(x, y), 0.0), r2, b2)
    m_rk = each(lambda x, y: jnp.where(incl, gram(x, y), 0.0), r2, k2)
    u = each(add, each(nt, a2, s0), each(nn, n_ak, v2))
    q = n_ab
    steps = L.bit_length() - 1
    for i in range(steps):
        u = each(add, u, each(nn, q, u))
        if i < steps - 1:
            q = each(nn, q, q)
    y2 = each(lambda x, y, z: x + y + z, each(nt, r2, s0), each(nn, m_rb, u), each(nn, m_rk, v2))
    y = each(lambda z: jnp.sum(z.reshape(2, L, 2 * HD), axis=0), y2)
    b3 = each(lambda bv, tv, cv: dup(bv * jnp.exp(tv - cv)), b, tot, cum)
    k3 = each(lambda kv, tv, cv: dup(kv * jnp.exp(tv - cv)), k, tot, cum)
    upd = each(add, each(tn, u, b3), each(tn, v2, k3))
    s1 = each(lambda sv, tv, uv: sv * jnp.exp(tv) + jnp.where(si == sj, uv, 0.0), s0, tot, upd)
    return y, s1


NPAIR = NH // 2


def _wkv_fwd(r, w, k, v, a, b, Bl, S):
    L = WKV_L
    nC = S // L

    def body(r_ref, w_ref, k_ref, v_ref, a_ref, b_ref, y_ref, ck_ref, s_ref):
        @pl.when(pl.program_id(1) == 0)
        def _():
            s_ref[...] = jnp.zeros(s_ref.shape, F32)
        cols = [slice(p * 2 * HD, (p + 1) * 2 * HD) for p in range(NPAIR)]
        s0 = tuple(s_ref[p] for p in range(NPAIR))
        ops = [tuple(z[:, cols[p]] for p in range(NPAIR)) for z in (r_ref, w_ref, k_ref, v_ref, a_ref, b_ref)]
        y, s1 = _wkv_chunks(s0, *ops)
        for p in range(NPAIR):
            ck_ref[0, 0, p] = s0[p]
            y_ref[:, cols[p]] = y[p]
            s_ref[p] = s1[p]

    row_spec = pl.BlockSpec((L, RW), lambda bb, c: (bb * nC + c, 0))
    return pl.pallas_call(
        body, name="wkv_fwd", grid=(Bl, nC), in_specs=[row_spec] * 6,
        out_specs=[row_spec, pl.BlockSpec((1, 1, NPAIR, 2 * HD, 2 * HD), lambda bb, c: (bb, c, 0, 0, 0))],
        out_shape=[jax.ShapeDtypeStruct((Bl * S, RW), F32), jax.ShapeDtypeStruct((Bl, nC, NPAIR, 2 * HD, 2 * HD), F32)],
        scratch_shapes=[pltpu.VMEM((NPAIR, 2 * HD, 2 * HD), F32)],
        compiler_params=_params(("arbitrary", "arbitrary")),
    )(r, w, k, v, a, b)


def _wkv_bwd(r, w, k, v, a, b, dy, ck, Bl, S):
    L = WKV_L
    nC = S // L

    def body(r_ref, w_ref, k_ref, v_ref, a_ref, b_ref, dy_ref, ck_ref,
             dr_ref, dw_ref, dk_ref, dv_ref, da_ref, db_ref, ds_ref):
        @pl.when(pl.program_id(1) == 0)
        def _():
            ds_ref[...] = jnp.zeros(ds_ref.shape, F32)
        cols = [slice(p * 2 * HD, (p + 1) * 2 * HD) for p in range(NPAIR)]
        s0 = tuple(ck_ref[0, 0, p] for p in range(NPAIR))
        ops = [tuple(z[:, cols[p]] for p in range(NPAIR)) for z in (r_ref, w_ref, k_ref, v_ref, a_ref, b_ref)]
        cts = (tuple(dy_ref[:, cols[p]] for p in range(NPAIR)), tuple(ds_ref[p] for p in range(NPAIR)))
        ds0, *grads = jax.vjp(_wkv_chunks, s0, *ops)[1](cts)
        for p in range(NPAIR):
            ds_ref[p] = ds0[p]
            for o, g in zip((dr_ref, dw_ref, dk_ref, dv_ref, da_ref, db_ref), grads):
                o[:, cols[p]] = g[p]

    row_spec = pl.BlockSpec((L, RW), lambda bb, c: (bb * nC + nC - 1 - c, 0))
    rows = jax.ShapeDtypeStruct((Bl * S, RW), F32)
    return pl.pallas_call(
        body, name="wkv_bwd", grid=(Bl, nC),
        in_specs=[row_spec] * 7 + [pl.BlockSpec((1, 1, NPAIR, 2 * HD, 2 * HD), lambda bb, c: (bb, nC - 1 - c, 0, 0, 0))],
        out_specs=[row_spec] * 6, out_shape=[rows] * 6,
        scratch_shapes=[pltpu.VMEM((NPAIR, 2 * HD, 2 * HD), F32)],
        compiler_params=_params(("arbitrary", "arbitrary")),
    )(r, w, k, v, a, b, dy, ck)


NST = NG * SP


def _cmul(ar, ai, br, bi):
    return ar * br - ai * bi, ar * bi + ai * br


def _s5_tiles(are, aim, reverse):
    if reverse:
        aim = -aim
    row = lax.broadcasted_iota(jnp.int32, (SUBLANES, NST), 0)
    pw = [(are, aim)]
    for _ in range(SUBLANES - 1):
        pw.append(_cmul(pw[-1][0], pw[-1][1], are, aim))
    bc = lambda z: jnp.broadcast_to(z, (SUBLANES, NST))
    ms = []
    for kk in (1, 2, 4):
        cond = (row < SUBLANES - kk) if reverse else (row >= kk)
        ms.append((jnp.where(cond, bc(pw[kk - 1][0]), 0.0), jnp.where(cond, bc(pw[kk - 1][1]), 0.0)))
    pr = jnp.zeros((SUBLANES, NST), F32)
    pi = jnp.zeros((SUBLANES, NST), F32)
    for i in range(SUBLANES):
        n = SUBLANES - i if reverse else i + 1
        pr = jnp.where(row == i, bc(pw[n - 1][0]), pr)
        pi = jnp.where(row == i, bc(pw[n - 1][1]), pi)
    return ms, (pr, pi)


def _s5_block(re, im, ms, pc, cre, cim, sg, reverse):
    ln = slice(sg * 512, (sg + 1) * 512)
    for (mr, mi), kk in zip(ms, (1, 2, 4)):
        sh = SUBLANES - kk if reverse else kk
        sre, sim = pltpu.roll(re, sh, 0), pltpu.roll(im, sh, 0)
        tr, ti = _cmul(mr[:, ln], mi[:, ln], sre, sim)
        re, im = re + tr, im + ti
    tr, ti = _cmul(pc[0][:, ln], pc[1][:, ln], cre[:, ln], cim[:, ln])
    return re + tr, im + ti


def _s5_scan(X_ref, n_rows, ms, pc, cre, cim, reverse, visit=None, acc0=None):
    nblk = n_rows // SUBLANES

    def it(i, carry):
        cre, cim, acc = carry
        j = nblk - 1 - i if reverse else i
        rows = pl.ds(pl.multiple_of(j * SUBLANES, SUBLANES), SUBLANES)
        edge = 0 if reverse else SUBLANES - 1
        blocks, ncre, ncim = [], [], []
        for sg in range(NSG):
            lr = slice(sg * 1024, sg * 1024 + 512)
            li = slice(sg * 1024 + 512, (sg + 1) * 1024)
            re, im = _s5_block(X_ref[rows, lr], X_ref[rows, li], ms, pc, cre, cim, sg, reverse)
            X_ref[rows, lr] = re
            X_ref[rows, li] = im
            blocks.append((re, im))
            ncre.append(re[edge:edge + 1])
            ncim.append(im[edge:edge + 1])
        if visit is not None:
            acc = visit(j, blocks, acc)
        return jnp.concatenate(ncre, axis=1), jnp.concatenate(ncim, axis=1), acc

    return lax.fori_loop(0, nblk, it, (cre, cim, acc0 if acc0 is not None else 0))


def _s5_fwd(u, wb, wc, ab, Bl, S, R=256):
    R = min(R, S)
    nC = S // R

    def body(u_ref, wb_ref, wc_ref, ab_ref, y_ref, st_ref, X_ref, car_ref):
        @pl.when(pl.program_id(1) == 0)
        def _():
            car_ref[...] = jnp.zeros(car_ref.shape, F32)
        st_ref[0, 0] = car_ref[...]
        ms, pc = _s5_tiles(ab_ref[0:1], ab_ref[1:2], False)
        for sg in range(NSG):
            X_ref[:, sg * 1024:(sg + 1) * 1024] = _dotm(u_ref[:, sg * 128:(sg + 1) * 128], wb_ref[sg])
        cre, cim, _ = _s5_scan(X_ref, R, ms, pc, car_ref[0:1], car_ref[1:2], False)
        car_ref[0:1] = cre
        car_ref[1:2] = cim
        for sg in range(NSG):
            y_ref[:, sg * 128:(sg + 1) * 128] = _dotm(X_ref[:, sg * 1024:(sg + 1) * 1024], wc_ref[sg])

    return pl.pallas_call(
        body, name="s5_fwd", grid=(Bl, nC),
        in_specs=[pl.BlockSpec((R, SW), lambda b, c: (b * nC + c, 0)),
                  pl.BlockSpec(wb.shape, lambda b, c: (0, 0, 0)), pl.BlockSpec(wc.shape, lambda b, c: (0, 0, 0)),
                  pl.BlockSpec(ab.shape, lambda b, c: (0, 0))],
        out_specs=[pl.BlockSpec((R, SW), lambda b, c: (b * nC + c, 0)),
                   pl.BlockSpec((1, 1, 2, NST), lambda b, c: (b, c, 0, 0))],
        out_shape=[jax.ShapeDtypeStruct((Bl * S, SW), F32), jax.ShapeDtypeStruct((Bl, nC, 2, NST), F32)],
        scratch_shapes=[pltpu.VMEM((R, 2 * NST), F32), pltpu.VMEM((2, NST), F32)],
        compiler_params=_params(("arbitrary", "arbitrary")),
    )(u, wb, wc, ab)


def _s5_bwd(u, dy, wb, wc, ab, st, Bl, S, R=256):
    R = min(R, S)
    nC = S // R

    def body(u_ref, dy_ref, wb_ref, wc_ref, ab_ref, st_ref, du_ref, dwb_ref, dwc_ref, dab_ref,
             X_ref, G_ref, car_ref):
        first = (pl.program_id(0) == 0) & (pl.program_id(1) == 0)

        @pl.when(first)
        def _():
            dwb_ref[...] = jnp.zeros(dwb_ref.shape, F32)
            dwc_ref[...] = jnp.zeros(dwc_ref.shape, F32)
            dab_ref[...] = jnp.zeros(dab_ref.shape, F32)

        @pl.when(pl.program_id(1) == 0)
        def _():
            car_ref[...] = jnp.zeros(car_ref.shape, F32)

        are, aim = ab_ref[0:1], ab_ref[1:2]
        ms, pc = _s5_tiles(are, aim, False)
        for sg in range(NSG):
            X_ref[:, sg * 1024:(sg + 1) * 1024] = _dotm(u_ref[:, sg * 128:(sg + 1) * 128], wb_ref[sg])
        _s5_scan(X_ref, R, ms, pc, st_ref[0, 0, 0:1], st_ref[0, 0, 1:2], False)
        dyv = dy_ref[...].astype(MXU_DTYPE)
        for sg in range(NSG):
            G_ref[:, sg * 1024:(sg + 1) * 1024] = lax.dot_general(
                dyv[:, sg * 128:(sg + 1) * 128], wc_ref[sg].astype(MXU_DTYPE), (((1,), (1,)), ((), ())),
                preferred_element_type=F32)
        rms_, rpc = _s5_tiles(are, aim, True)
        row = lax.broadcasted_iota(jnp.int32, (SUBLANES, 512), 0)

        def visit(j, blocks, acc):
            before = pl.multiple_of(jnp.maximum(j - 1, 0) * SUBLANES, SUBLANES)
            prow = X_ref[pl.ds(before, SUBLANES), :][SUBLANES - 1:SUBLANES]
            rows = pl.ds(pl.multiple_of(j * SUBLANES, SUBLANES), SUBLANES)
            are_acc, aim_acc = [], []
            for sg in range(NSG):
                lr = slice(sg * 1024, sg * 1024 + 512)
                li = slice(sg * 1024 + 512, (sg + 1) * 1024)
                ln = slice(sg * 512, (sg + 1) * 512)
                pre = jnp.where(j > 0, prow[:, lr], st_ref[0, 0, 0:1, ln])
                pim = jnp.where(j > 0, prow[:, li], st_ref[0, 0, 1:2, ln])
                xre = jnp.where(row == 0, pre, pltpu.roll(X_ref[rows, lr], 1, 0))
                xim = jnp.where(row == 0, pim, pltpu.roll(X_ref[rows, li], 1, 0))
                dre, dim = blocks[sg]
                are_acc.append(dre * xre + dim * xim)
                aim_acc.append(dim * xre - dre * xim)
            return acc[0] + jnp.concatenate(are_acc, axis=1), acc[1] + jnp.concatenate(aim_acc, axis=1)

        zero = jnp.zeros((SUBLANES, NST), F32)
        cre, cim, acc = _s5_scan(G_ref, R, rms_, rpc, car_ref[0:1], car_ref[1:2], True, visit, (zero, zero))
        car_ref[0:1] = cre
        car_ref[1:2] = cim
        dab_ref[0:1] += jnp.sum(acc[0], axis=0, keepdims=True)
        dab_ref[1:2] += jnp.sum(acc[1], axis=0, keepdims=True)
        uv = u_ref[...].astype(MXU_DTYPE)
        for sg in range(NSG):
            cs = slice(sg * 1024, (sg + 1) * 1024)
            us = slice(sg * 128, (sg + 1) * 128)
            gx = G_ref[:, cs].astype(MXU_DTYPE)
            dwb_ref[sg] += lax.dot_general(uv[:, us], gx, (((0,), (0,)), ((), ())), preferred_element_type=F32)
            dwc_ref[sg] += lax.dot_general(X_ref[:, cs].astype(MXU_DTYPE), dyv[:, us], (((0,), (0,)), ((), ())),
                                           preferred_element_type=F32)
            du_ref[:, us] = lax.dot_general(gx, wb_ref[sg].astype(MXU_DTYPE), (((1,), (1,)), ((), ())),
                                            preferred_element_type=F32)

    rmap = lambda b, c: (b * nC + nC - 1 - c, 0)
    return pl.pallas_call(
        body, name="s5_bwd", grid=(Bl, nC),
        in_specs=[pl.BlockSpec((R, SW), rmap), pl.BlockSpec((R, SW), rmap),
                  pl.BlockSpec(wb.shape, lambda b, c: (0, 0, 0)), pl.BlockSpec(wc.shape, lambda b, c: (0, 0, 0)),
                  pl.BlockSpec(ab.shape, lambda b, c: (0, 0)),
                  pl.BlockSpec((1, 1, 2, NST), lambda b, c: (b, nC - 1 - c, 0, 0))],
        out_specs=[pl.BlockSpec((R, SW), rmap), pl.BlockSpec(wb.shape, lambda b, c: (0, 0, 0)),
                   pl.BlockSpec(wc.shape, lambda b, c: (0, 0, 0)), pl.BlockSpec((2, NST), lambda b, c: (0, 0))],
        out_shape=[jax.ShapeDtypeStruct((Bl * S, SW), F32), jax.ShapeDtypeStruct(wb.shape, F32),
                   jax.ShapeDtypeStruct(wc.shape, F32), jax.ShapeDtypeStruct((2, NST), F32)],
        scratch_shapes=[pltpu.VMEM((R, 2 * NST), F32), pltpu.VMEM((R, 2 * NST), F32), pltpu.VMEM((2, NST), F32)],
        compiler_params=_params(("arbitrary", "arbitrary")),
    )(u, dy, wb, wc, ab, st)


def _s5_disc_math(a_re, a_im, log_dt, b_re, b_im, expand):
    dt = jnp.exp(log_dt)
    z_re, z_im = a_re * dt, a_im * dt
    mag = jnp.exp(z_re)
    ab_re, ab_im = mag * jnp.cos(z_im), mag * jnp.sin(z_im)
    den = a_re * a_re + a_im * a_im
    q_re = ((ab_re - 1.0) * a_re + ab_im * a_im) / den
    q_im = (ab_im * a_re - (ab_re - 1.0) * a_im) / den
    qe_re = jnp.dot(q_re, expand, preferred_element_type=F32, precision=HIGHEST)
    qe_im = jnp.dot(q_im, expand, preferred_element_type=F32, precision=HIGHEST)
    return ab_re, ab_im, qe_re * b_re - qe_im * b_im, qe_re * b_im + qe_im * b_re


def _whole(shape):
    return pl.BlockSpec(shape, lambda nd=len(shape): (0,) * nd)


def _s5_disc(a_re, a_im, log_dt, b_re, b_im, expand):
    def body(a, b, c, d, e, f, o0, o1, o2, o3):
        res = _s5_disc_math(a[...], b[...], c[...], d[...], e[...], f[...])
        for o, v in zip((o0, o1, o2, o3), res):
            o[...] = v
    ins = (a_re, a_im, log_dt, b_re, b_im, expand)
    outs = [jax.ShapeDtypeStruct(a_re.shape, F32)] * 2 + [jax.ShapeDtypeStruct(b_re.shape, F32)] * 2
    return pl.pallas_call(body, name="s5_disc", in_specs=[_whole(x.shape) for x in ins],
                          out_specs=[_whole(o.shape) for o in outs], out_shape=outs)(*ins)


def _s5_disc_bwd(a_re, a_im, log_dt, b_re, b_im, expand, cts):
    def body(a, b, c, d, e, f, g0, g1, g2, g3, o0, o1, o2, o3, o4):
        fn = lambda *p: _s5_disc_math(*p, f[...])
        _, vjp = jax.vjp(fn, a[...], b[...], c[...], d[...], e[...])
        for o, v in zip((o0, o1, o2, o3, o4), vjp((g0[...], g1[...], g2[...], g3[...]))):
            o[...] = v
    ins = (a_re, a_im, log_dt, b_re, b_im, expand) + tuple(cts)
    outs = [jax.ShapeDtypeStruct(x.shape, F32) for x in (a_re, a_im, log_dt, b_re, b_im)]
    return pl.pallas_call(body, name="s5_disc_bwd", in_specs=[_whole(x.shape) for x in ins],
                          out_specs=[_whole(o.shape) for o in outs], out_shape=outs)(*ins)


def _ada_fwd(c_all, w_shard, b_shard):
    def body(c_ref, w_ref, b_ref, o_ref):
        cv = c_ref[...]
        o_ref[...] = _dotm(cv * _sigmoid(cv), w_ref[...]) + b_ref[...]
    n = w_shard.shape[1]
    return pl.pallas_call(
        body, name="ada_fwd", in_specs=[_whole(c_all.shape), _whole(w_shard.shape), _whole(b_shard.shape)],
        out_specs=_whole((c_all.shape[0], n)), out_shape=jax.ShapeDtypeStruct((c_all.shape[0], n), F32),
        compiler_params=_params(),
    )(c_all, w_shard, b_shard)


def _ada_bwd(c_all, dmod_cols, dmod_all):
    def body(c_ref, dc_ref, da_ref, gw_ref, gb_ref):
        cv = c_ref[...]
        gw_ref[...] = lax.dot_general((cv * _sigmoid(cv)).astype(MXU_DTYPE), dc_ref[...].astype(MXU_DTYPE),
                                      (((0,), (0,)), ((), ())), preferred_element_type=F32)
        gb_ref[...] = jnp.sum(da_ref[...], axis=0, keepdims=True)
    n = dmod_cols.shape[1]
    return pl.pallas_call(
        body, name="ada_bwd", in_specs=[_whole(c_all.shape), _whole(dmod_cols.shape), _whole(dmod_all.shape)],
        out_specs=[_whole((D, n)), _whole((1, dmod_all.shape[1]))],
        out_shape=[jax.ShapeDtypeStruct((D, n), F32), jax.ShapeDtypeStruct((1, dmod_all.shape[1]), F32)],
        compiler_params=_params(),
    )(c_all, dmod_cols, dmod_all)


def _rows_block(n_rows, cap=512):
    if n_rows <= cap:
        return n_rows
    for t in range(cap - cap % SUBLANES, 0, -SUBLANES):
        if n_rows % t == 0:
            return t
    return n_rows


def _adamw(w, g, m, v, name):
    rows, cols = w.shape
    tr = _rows_block(rows, max(SUBLANES, (1 << 19) // max(cols, 1) // SUBLANES * SUBLANES))

    def body(w_ref, g_ref, m_ref, v_ref, d_ref, nm_ref, nv_ref):
        gv = g_ref[...]
        nm = B1 * m_ref[...] + (1.0 - B1) * gv
        nv = B2 * v_ref[...] + (1.0 - B2) * (gv * gv)
        m_hat = nm / (1.0 - B1 ** STEP)
        v_hat = nv / (1.0 - B2 ** STEP)
        d_ref[...] = -LR * (m_hat / (jnp.sqrt(v_hat) + ADAM_EPS) + WD * w_ref[...])
        nm_ref[...] = nm
        nv_ref[...] = nv

    spec = pl.BlockSpec((tr, cols), lambda i: (i, 0))
    sd = jax.ShapeDtypeStruct((rows, cols), F32)
    return pl.pallas_call(body, name=name, grid=(rows // tr,), in_specs=[spec] * 4, out_specs=[spec] * 3,
                          out_shape=[sd] * 3, compiler_params=_params(("parallel",)))(w, g, m, v)


def _sum_slots(x, out_dtype, name):
    n, rows, cols = x.shape
    tr = _rows_block(rows)

    def body(x_ref, o_ref):
        acc = x_ref[0].astype(F32)
        for j in range(1, n):
            acc = acc + x_ref[j].astype(F32)
        o_ref[...] = acc.astype(o_ref.dtype)

    return pl.pallas_call(
        body, name=name, grid=(rows // tr,), in_specs=[pl.BlockSpec((n, tr, cols), lambda i: (0, i, 0))],
        out_specs=pl.BlockSpec((tr, cols), lambda i: (i, 0)), out_shape=jax.ShapeDtypeStruct((rows, cols), out_dtype),
        compiler_params=_params(("parallel",)))(x)


PACK_COLS = 1024


def _pack_rows(parts, dtype, row_mult):
    flat = jnp.concatenate([p.reshape(-1).astype(dtype) for p in parts])
    per = PACK_COLS * row_mult
    n = -(-flat.shape[0] // per) * per
    flat = jnp.pad(flat, (0, n - flat.shape[0]))
    return flat.reshape(n // PACK_COLS, PACK_COLS)


def _unpack(flat, shapes):
    out, off = [], 0
    for s in shapes:
        n = math.prod(s)
        out.append(flat[off:off + n].reshape(s))
        off += n
    return out


BIG = (("w_in", (D, SHIFT + SW + 2 * D), 1), ("w_out_rwkv", (RW, D), 1), ("w_glu", (SW, 2 * D), 1),
       ("w_out", (D, D), 0), ("w_ffn_up", (D, 2 * DFF), 1), ("w_ffn_down", (DFF, D), 0))
BIG_SMALL = (("rwkv_w_up", (LW, RW), 1), ("rwkv_a_up", (LA, RW), 1), ("rwkv_g_up", (LG, RW), 1),
             ("ffn_conv_w", (3, 2 * DFF), 1))


def _shard_shape(shape, axis):
    return (shape[0] // 4, shape[1]) if axis == 0 else (shape[0], shape[1] // 4)


def _to_shards(g, axis):
    r, C = g.shape
    return g.reshape(4, r // 4, C) if axis == 0 else g.reshape(r, 4, C // 4).transpose(1, 0, 2)


def _from_shards(x, axis):
    _, r, C = x.shape
    return x.reshape(4 * r, C) if axis == 0 else x.transpose(1, 0, 2).reshape(r, 4 * C)


def kernel(x, c, w_ada, b_ada, norm1_g, w_in, mu_shift, rwkv_w0, rwkv_w_up, rwkv_a0, rwkv_a_up, rwkv_g_up, rwkv_k_k, rwkv_k_a, rwkv_r_k, rwkv_ln_g, rwkv_ln_b, w_out_rwkv, s5_a_re, s5_a_im, s5_log_dt, s5_b_re, s5_b_im, s5_c_re, s5_c_im, s5_d, w_glu, w_out, norm2_g, w_ffn_up, ffn_conv_w, ffn_conv_b, w_ffn_down, norm_f_g, loss_target, m_w_ada, m_b_ada, m_norm1_g, m_w_in, m_mu_shift, m_rwkv_w0, m_rwkv_w_up, m_rwkv_a0, m_rwkv_a_up, m_rwkv_g_up, m_rwkv_k_k, m_rwkv_k_a, m_rwkv_r_k, m_rwkv_ln_g, m_rwkv_ln_b, m_w_out_rwkv, m_s5_a_re, m_s5_a_im, m_s5_log_dt, m_s5_b_re, m_s5_b_im, m_s5_c_re, m_s5_c_im, m_s5_d, m_w_glu, m_w_out, m_norm2_g, m_w_ffn_up, m_ffn_conv_w, m_ffn_conv_b, m_w_ffn_down, m_norm_f_g, v_w_ada, v_b_ada, v_norm1_g, v_w_in, v_mu_shift, v_rwkv_w0, v_rwkv_w_up, v_rwkv_a0, v_rwkv_a_up, v_rwkv_g_up, v_rwkv_k_k, v_rwkv_k_a, v_rwkv_r_k, v_rwkv_ln_g, v_rwkv_ln_b, v_w_out_rwkv, v_s5_a_re, v_s5_a_im, v_s5_log_dt, v_s5_b_re, v_s5_b_im, v_s5_c_re, v_s5_c_im, v_s5_d, v_w_glu, v_w_out, v_norm2_g, v_w_ffn_up, v_ffn_conv_w, v_ffn_conv_b, v_w_ffn_down, v_norm_f_g):
    names = ["w_ada", "b_ada", "norm1_g", "w_in", "mu_shift", "rwkv_w0", "rwkv_w_up", "rwkv_a0", "rwkv_a_up",
             "rwkv_g_up", "rwkv_k_k", "rwkv_k_a", "rwkv_r_k", "rwkv_ln_g", "rwkv_ln_b", "w_out_rwkv", "s5_a_re",
             "s5_a_im", "s5_log_dt", "s5_b_re", "s5_b_im", "s5_c_re", "s5_c_im", "s5_d", "w_glu", "w_out", "norm2_g",
             "w_ffn_up", "ffn_conv_w", "ffn_conv_b", "w_ffn_down", "norm_f_g"]
    env = dict(locals())
    W = {n: env[n] for n in names}
    M = {n: env["m_" + n] for n in names}
    V = {n: env["v_" + n] for n in names}

    Bl, S, _ = x.shape
    T = Bl * S
    ix, iy, ic = lax.axis_index("x"), lax.axis_index("y"), lax.axis_index("c")
    chip = 2 * ix + iy
    dev = 2 * chip + ic
    rw = functools.partial(_rowwise, Bl=Bl, S=S)

    chip_arrs = [W[n][0].astype(MXU_DTYPE) for n, _, _ in BIG] + [W[n][0] for n, _, _ in BIG_SMALL[:3]]
    got_chip, got_dev = _gather_two_level(chip_arrs, [W["ffn_conv_w"][0], c], "gather_w")
    full = {n: _from_shards(g, axis) for (n, _, axis), g in zip(BIG + BIG_SMALL[:3], got_chip)}
    full["ffn_conv_w"] = _from_shards(got_dev[0][:, 0], 1)
    c_all = got_dev[1].reshape(8 * Bl, D)
    w_p, w_u, w_g = full["w_in"][:, :SHIFT], full["w_in"][:, SHIFT:SHIFT + SW], full["w_in"][:, SHIFT + SW:]
    zeros_l = jnp.zeros((LW, RW), F32)
    w_up_p = jnp.concatenate([full["rwkv_w_up"], zeros_l], axis=0)
    a_up_p = jnp.concatenate([zeros_l, full["rwkv_a_up"]], axis=0)
    g_up = full["rwkv_g_up"]
    conv_w = full["ffn_conv_w"]
    conv_wg, conv_wu = conv_w[:, :DFF], conv_w[:, DFF:]
    conv_bg, conv_bu = ffn_conv_b[:, :DFF], ffn_conv_b[:, DFF:]
    hm = jnp.kron(jnp.eye(NH, dtype=F32), jnp.ones((HD, HD), F32))

    ncol = 6 * D // 4
    b_ada_cols = lax.dynamic_slice_in_dim(b_ada, chip * ncol, ncol, 1)
    mod_part = _ada_fwd(c_all, w_ada[0], b_ada_cols)
    mod4 = _gather_two_level([], [mod_part], "gather_mod")[1][0][:, 0]
    mod = lax.dynamic_slice_in_dim(mod4, dev * Bl, Bl, 1).transpose(1, 0, 2).reshape(Bl, 1, 6 * D)
    SH1, SC1, GT1, SH2, SC2, GT2 = range(6)

    x2d = x.reshape(T, D)
    tgt = loss_target.reshape(T, D)

    (h1,) = rw("norm1", lambda xv, sc, sh, g: _norm_mod(xv, g, sc, sh), R=256, tiled=[(x2d, D, 0)],
               batch=[(mod, D, SC1), (mod, D, SH1)], full=[norm1_g], out_tiled=[(D, MXU_DTYPE)])
    p = _mm([h1], [w_p], F32, "proj_p")
    u = _mm([h1], [w_u], F32, "proj_u")
    gates = _mm([h1], [w_g], F32, "proj_g")

    prep_params = [rwkv_w0, w_up_p, rwkv_a0, a_up_p, g_up, rwkv_k_k, rwkv_k_a, hm]

    def prep_fwd(pv, ph, mu, *pp):
        ps = pv + (_shift_down(pv, ph, 1) - pv) * mu
        return _rwkv_prep(*_split_ps(ps), *pp)

    r_, w_, k_, v_, a_, b_, g_ = rw("rwkv_prep", prep_fwd, R=256, tiled=[(p, SHIFT, 0)], prev=[(p, SHIFT, 0)],
                                    full=[mu_shift] + prep_params, out_tiled=[(RW, F32)] * 7)
    y_wkv, ck = _wkv_fwd(r_, w_, k_, v_, a_, b_, Bl, S)
    r_k_row = rwkv_r_k.reshape(1, RW)
    post_params = [rwkv_ln_g, rwkv_ln_b, r_k_row, hm]
    (o_rwkv,) = rw("rwkv_post", _rwkv_post, R=256,
                   tiled=[(y_wkv, RW, 0), (r_, RW, 0), (k_, RW, 0), (v_, RW, 0), (g_, RW, 0)],
                   full=post_params, out_tiled=[(RW, MXU_DTYPE)])
    y_a = _mm([o_rwkv], [full["w_out_rwkv"]], F32, "out_rwkv")

    expand = jnp.kron(jnp.eye(SP, dtype=F32), jnp.ones((1, SGC), F32))
    s5_in = (s5_a_re[0], s5_a_im[0], s5_log_dt[0].reshape(NG, 1), s5_b_re[0].reshape(NG, SP * SGC),
             s5_b_im[0].reshape(NG, SP * SGC), expand)
    ab_re, ab_im, bb_re, bb_im = _s5_disc(*s5_in)
    eye8 = jnp.eye(8, dtype=F32)

    def blockdiag_in(bb):
        t = bb.reshape(NSG, 8, SP, SGC)
        return jnp.einsum("ab,sapc->sacbp", eye8, t).reshape(NSG, 128, 512)

    def blockdiag_out(cc):
        t = cc.reshape(NSG, 8, SGC, SP)
        return jnp.einsum("ab,sacp->sapbc", eye8, t).reshape(NSG, 512, 128)

    wb = jnp.concatenate([blockdiag_in(bb_re), blockdiag_in(bb_im)], axis=2).astype(MXU_DTYPE)
    wc = jnp.concatenate([blockdiag_out(s5_c_re[0]), -blockdiag_out(s5_c_im[0])], axis=1).astype(MXU_DTYPE)
    ab = jnp.stack([ab_re.reshape(NST), ab_im.reshape(NST)])
    y_ssm, s5_st = _s5_fwd(u, wb, wc, ab, Bl, S)
    (s5o,) = rw("s5_post", _s5_post, R=256, tiled=[(y_ssm, SW, 0), (u, SW, 0)], full=[s5_d],
                out_tiled=[(SW, MXU_DTYPE)])
    z = _mm([s5o], [full["w_glu"]], F32, "glu")
    mix_tiled = [(gates, D, 0), (gates, D, 1), (y_a, D, 0), (z, D, 0), (z, D, 1)]
    (mixed_in,) = rw("mix", _mix, R=256, tiled=mix_tiled, out_tiled=[(D, MXU_DTYPE)])
    mixed = _mm([mixed_in], [full["w_out"]], F32, "out_proj")

    def norm2_fwd(xv, mx, gt, sc, sh, g):
        x1 = xv + gt * mx
        return x1, _norm_mod(x1, g, sc, sh)

    x1, h2 = rw("norm2", norm2_fwd, R=256, tiled=[(x2d, D, 0), (mixed, D, 0)],
                batch=[(mod, D, GT1), (mod, D, SC2), (mod, D, SH2)], full=[norm2_g],
                out_tiled=[(D, F32), (D, MXU_DTYPE)])
    up = _mm([h2], [full["w_ffn_up"]], F32, "ffn_up")
    conv_tiled = [(up, DFF, 0), (up, DFF, 1)]
    conv_full = [conv_wg, conv_wu, conv_bg, conv_bu]

    def act_fwd(*a):
        return _silu_gate(*_conv_act(*a))

    (act,) = rw("ffn_act", act_fwd, R=128, tiled=conv_tiled, prev=conv_tiled, full=conv_full,
                out_tiled=[(DFF, MXU_DTYPE)])
    ffn = _mm([act], [full["w_ffn_down"]], F32, "ffn_down")

    def head(x1v, fv, tv, gt, g):
        x2 = x1v + gt * fv
        y, vjp = jax.vjp(_rms, x2, g)
        e = y - tv
        dx2, dg = vjp(e * (1.0 / D))
        loss = jnp.sum(e * e, keepdims=True) * jnp.ones((1, LANES), F32)
        return dx2, dx2 * gt, jnp.sum(dx2 * fv, axis=0, keepdims=True), dg.reshape(1, D), loss

    dx2, d_ffn, d_gt2, g_norm_f, loss_acc = rw(
        "head", head, R=256, tiled=[(x1, D, 0), (ffn, D, 0), (tgt, D, 0)], batch=[(mod, D, GT2)],
        full=[norm_f_g.reshape(1, D)], out_tiled=[(D, F32), (D, MXU_DTYPE)], out_batch=[D],
        out_acc=[(1, D), (1, LANES)])
    loss = lax.psum(0.5 / D * loss_acc[0, 0], ("x", "y", "c"))

    d_act = _mm([d_ffn], [full["w_ffn_down"]], F32, "d_act", bt=True)
    g_w_ffn_down = _mm_tn(act, d_ffn, "g_ffn_down")

    def act_bwd(ug, uu, dact, hg, hu, wg, wu, bg, bu):
        gate, upv = _conv_act(ug, uu, hg, hu, wg, wu, bg, bu)
        _, vjp_s = jax.vjp(_silu_gate, gate, upv)
        d_gate, d_upv = vjp_s(dact)
        def taps(dh, xv, h):
            return [jnp.sum(dh * _shift_down(xv, h, 2), axis=0, keepdims=True),
                    jnp.sum(dh * _shift_down(xv, h, 1), axis=0, keepdims=True),
                    jnp.sum(dh * xv, axis=0, keepdims=True), jnp.sum(dh, axis=0, keepdims=True)]
        return (d_gate, d_upv, *taps(d_gate, ug, hg), *taps(d_upv, uu, hu))

    dh_g, dh_u, *tapg = rw(
        "ffn_act_bwd", act_bwd, R=128, tiled=conv_tiled + [(d_act, DFF, 0)], prev=conv_tiled, full=conv_full,
        out_tiled=[(DFF, F32), (DFF, F32)], out_acc=[(1, DFF)] * 8)
    g_cw_g, g_cb_g = jnp.concatenate(tapg[0:3], axis=0), tapg[3]
    g_cw_u, g_cb_u = jnp.concatenate(tapg[4:7], axis=0), tapg[7]

    def conv_t(dg, du_, ng, nu, wg, wu):
        def ct(d, n, w):
            return w[2:3] * d + w[1:2] * _shift_up(d, n, 1) + w[0:1] * _shift_up(d, n, 2)
        return jnp.concatenate([ct(dg, ng, wg), ct(du_, nu, wu)], axis=1)

    (d_up,) = rw("conv_bwd", conv_t, R=128, tiled=[(dh_g, DFF, 0), (dh_u, DFF, 0)],
                 nxt=[(dh_g, DFF, 0), (dh_u, DFF, 0)], full=[conv_wg, conv_wu], out_tiled=[(2 * DFF, MXU_DTYPE)])
    d_h2 = _mm([d_up], [full["w_ffn_up"]], F32, "d_h2", bt=True)
    g_w_ffn_up = _mm_tn(h2, d_up, "g_ffn_up")

    def norm2_bwd(x1v, dh2, dx2v, mx, gt, sc, sh, g):
        _, vjp = jax.vjp(_norm_mod, x1v, g, sc, sh)
        dxn, dg, dsc, dsh = vjp(dh2)
        dx1 = dx2v + dxn
        return dx1, dx1 * gt, jnp.sum(dx1 * mx, axis=0, keepdims=True), dsc, dsh, dg

    dx1, d_mixed, d_gt1, d_sc2, d_sh2, g_norm2 = rw(
        "norm2_bwd", norm2_bwd, R=256, tiled=[(x1, D, 0), (d_h2, D, 0), (dx2, D, 0), (mixed, D, 0)],
        batch=[(mod, D, GT1), (mod, D, SC2), (mod, D, SH2)], full=[norm2_g],
        out_tiled=[(D, F32), (D, MXU_DTYPE)], out_batch=[D, D, D], out_acc=[(1, D)])

    d_mixed_in = _mm([d_mixed], [full["w_out"]], F32, "d_mixed_in", bt=True)
    g_w_out = _mm_tn(mixed_in, d_mixed, "g_w_out")

    def mix_bwd(ga, gb, ya, za, zb, dm):
        _, vjp = jax.vjp(_mix, ga, gb, ya, za, zb)
        dga, dgb, dya, dza, dzb = vjp(dm)
        return jnp.concatenate([dga, dgb], axis=1), dya, jnp.concatenate([dza, dzb], axis=1)

    d_gates, d_ya, d_z = rw("mix_bwd", mix_bwd, R=256, tiled=mix_tiled + [(d_mixed_in, D, 0)],
                            out_tiled=[(2 * D, MXU_DTYPE), (D, MXU_DTYPE), (2 * D, MXU_DTYPE)])
    d_o_rwkv = _mm([d_ya], [full["w_out_rwkv"]], F32, "d_o_rwkv", bt=True)
    g_w_out_rwkv = _mm_tn(o_rwkv, d_ya, "g_out_rwkv")
    d_s5o = _mm([d_z], [full["w_glu"]], F32, "d_s5o", bt=True)
    g_w_glu = _mm_tn(s5o, d_z, "g_glu")

    def s5_post_bwd(ys, uv, ds, dd):
        _, vjp = jax.vjp(_s5_post, ys, uv, dd)
        return vjp(ds)

    d_yssm, d_u_direct, g_s5_d = rw("s5_post_bwd", s5_post_bwd, R=256,
                                    tiled=[(y_ssm, SW, 0), (u, SW, 0), (d_s5o, SW, 0)], full=[s5_d],
                                    out_tiled=[(SW, F32), (SW, F32)], out_acc=[(1, SW)])
    d_u_ssm, d_wb, d_wc, d_ab = _s5_bwd(u, d_yssm, wb, wc, ab, s5_st, Bl, S)

    def diag_in(dw):
        t = dw.reshape(NSG, 8, SGC, 8, SP)
        return jnp.einsum("ab,sacbp->sapc", eye8, t).reshape(NG, SP * SGC)

    def diag_out(dw):
        t = dw.reshape(NSG, 8, SP, 8, SGC)
        return jnp.einsum("ab,sapbc->sacp", eye8, t).reshape(NG, SGC, SP)

    g_s5_c_re = diag_out(d_wc[:, :512])
    g_s5_c_im = -diag_out(d_wc[:, 512:])
    disc_cts = (d_ab[0].reshape(NG, SP), d_ab[1].reshape(NG, SP), diag_in(d_wb[:, :, :512]), diag_in(d_wb[:, :, 512:]))
    g_a_re, g_a_im, g_log_dt, g_b_re, g_b_im = _s5_disc_bwd(*s5_in, disc_cts)

    def post_bwd(yv, rv, kv, vv, gv, do, *pp):
        _, vjp = jax.vjp(lambda *a: _rwkv_post(*a, pp[3]), yv, rv, kv, vv, gv, *pp[:3])
        return vjp(do)

    dy_wkv, dr_b, dk_b, dv_b, dg_, g_ln_g, g_ln_b, g_r_k = rw(
        "rwkv_post_bwd", post_bwd, R=256,
        tiled=[(y_wkv, RW, 0), (r_, RW, 0), (k_, RW, 0), (v_, RW, 0), (g_, RW, 0), (d_o_rwkv, RW, 0)],
        full=post_params, out_tiled=[(RW, F32)] * 5, out_acc=[(1, RW)] * 3)
    dr3, dw3, dk3, dv3, da3, db3 = _wkv_bwd(r_, w_, k_, v_, a_, b_, dy_wkv, ck, Bl, S)

    def prep_bwd(pv, dr1, dr2, dwv, dk1, dk2, dv1, dv2, dav, dbv, dgv, ph, mu, *pp):
        prev = _shift_down(pv, ph, 1)
        ps = pv + (prev - pv) * mu
        _, vjp = jax.vjp(lambda *q: _rwkv_prep(*q, pp[7]), *_split_ps(ps), *pp[:7])
        grads = vjp((dr1 + dr2, dwv, dk1 + dk2, dv1 + dv2, dav, dbv, dgv))
        dps = jnp.concatenate(grads[:5], axis=1)
        return (dps,) + tuple(grads[5:]) + (jnp.sum(dps * (prev - pv), axis=0, keepdims=True),)

    prep_outs = rw(
        "rwkv_prep_bwd", prep_bwd, R=256,
        tiled=[(p, SHIFT, 0), (dr3, RW, 0), (dr_b, RW, 0), (dw3, RW, 0), (dk3, RW, 0), (dk_b, RW, 0),
               (dv3, RW, 0), (dv_b, RW, 0), (da3, RW, 0), (db3, RW, 0), (dg_, RW, 0)],
        prev=[(p, SHIFT, 0)], full=[mu_shift] + prep_params,
        out_tiled=[(SHIFT, F32)],
        out_acc=[(1, RW), (LW + LA, RW), (1, RW), (LW + LA, RW), (LG, RW), (1, RW), (1, RW), (1, SHIFT)])
    d_ps, g_w0, g_w_up_p, g_a0, g_a_up_p, g_g_up, g_k_k, g_k_a, g_mu = prep_outs

    def shift_bwd(dps, nx, mu):
        return dps * (1.0 - mu) + _shift_up(dps * mu, nx * mu, 1)

    (d_p,) = rw("shift_bwd", shift_bwd, R=256, tiled=[(d_ps, SHIFT, 0)], nxt=[(d_ps, SHIFT, 0)], full=[mu_shift],
                out_tiled=[(SHIFT, MXU_DTYPE)])
    (d_u,) = rw("d_u", lambda a1, a2: a1 + a2, R=256, tiled=[(d_u_direct, SW, 0), (d_u_ssm, SW, 0)],
                out_tiled=[(SW, MXU_DTYPE)])
    d_h1 = _mm([d_p, d_u, d_gates], [w_p, w_u, w_g], F32, "d_h1", bt=True)
    g_w_in = jnp.concatenate([_mm_tn(h1, d_p, "g_w_p"), _mm_tn(h1, d_u, "g_w_u"), _mm_tn(h1, d_gates, "g_w_g")], axis=1)

    def norm1_bwd(xv, dh1, dx1v, sc, sh, g):
        _, vjp = jax.vjp(_norm_mod, xv, g, sc, sh)
        dxn, dg, dsc, dsh = vjp(dh1)
        return dx1v + dxn, dsc, dsh, dg

    grad_x, d_sc1, d_sh1, g_norm1 = rw(
        "norm1_bwd", norm1_bwd, R=256, tiled=[(x2d, D, 0), (d_h1, D, 0), (dx1, D, 0)],
        batch=[(mod, D, SC1), (mod, D, SH1)], full=[norm1_g], out_tiled=[(D, F32)], out_batch=[D, D], out_acc=[(1, D)])

    dmod = jnp.concatenate([d_sh1, d_sc1, d_gt1, d_sh2, d_sc2, d_gt2], axis=2).reshape(Bl, 6 * D)
    dmod_all = _gather_two_level([], [dmod], "gather_dmod")[1][0].reshape(8 * Bl, 6 * D)
    dmod_cols = lax.dynamic_slice_in_dim(dmod_all, chip * ncol, ncol, 1)
    g_w_ada, g_b_ada = _ada_bwd(c_all, dmod_cols, dmod_all)

    small = {"norm1_g": g_norm1, "mu_shift": g_mu, "rwkv_w0": g_w0, "rwkv_a0": g_a0, "rwkv_k_k": g_k_k,
             "rwkv_k_a": g_k_a, "rwkv_r_k": g_r_k, "rwkv_ln_g": g_ln_g, "rwkv_ln_b": g_ln_b, "s5_a_re": g_a_re,
             "s5_a_im": g_a_im, "s5_log_dt": g_log_dt, "s5_b_re": g_b_re, "s5_b_im": g_b_im, "s5_c_re": g_s5_c_re,
             "s5_c_im": g_s5_c_im, "s5_d": g_s5_d, "norm2_g": g_norm2,
             "ffn_conv_b": jnp.concatenate([g_cb_g, g_cb_u], axis=1), "norm_f_g": g_norm_f}
    small_names = list(small)
    g_conv_w = jnp.concatenate([g_cw_g, g_cw_u], axis=1)
    shard_small = {"rwkv_w_up": g_w_up_p[:LW], "rwkv_a_up": g_a_up_p[LW:], "rwkv_g_up": g_g_up, "ffn_conv_w": g_conv_w}
    parts = [small[n] for n in small_names] + [_to_shards(shard_small[n], ax) for n, _, ax in BIG_SMALL]
    spack = _pack_rows(parts, F32, SUBLANES)
    s_all = _gather_two_level([], [spack], "gather_gsmall")[1][0]
    s_sum = _sum_slots(s_all.reshape((8,) + spack.shape), F32, "sum_gsmall").reshape(-1)
    grads = {}
    off = 0
    for n in small_names:
        grads[n] = s_sum[off:off + W[n].size].reshape(W[n].shape)
        off += W[n].size
    for n, shape, axis in BIG_SMALL:
        ss = _shard_shape(shape, axis)
        k4 = 4 * math.prod(ss)
        sh4 = s_sum[off:off + k4].reshape(4, math.prod(ss))
        grads[n] = lax.dynamic_index_in_dim(sh4, chip, 0, keepdims=False).reshape((1,) + ss)
        off += k4

    big_g = {"w_in": g_w_in, "w_out_rwkv": g_w_out_rwkv, "w_glu": g_w_glu, "w_out": g_w_out,
             "w_ffn_up": g_w_ffn_up, "w_ffn_down": g_w_ffn_down}
    gsh = [_to_shards(big_g[n], ax).astype(MXU_DTYPE) for n, _, ax in BIG]
    nbig = len(gsh)
    sds = jax.ShapeDtypeStruct
    moves = [(i, i, lambda ref, me, peer, j=j, r=g.shape[1]: ref.at[j, _half(r, peer[2])],
              lambda ref, me, j=j: ref.at[me[2], j]) for i, g in enumerate(gsh) for j in range(4)]
    pair = _exchange("rs_pair", PAIR_FLIPS, gsh, [sds((2, 4, g.shape[1] // 2, g.shape[2]), MXU_DTYPE) for g in gsh], moves,
                     stage=[((g.shape[1] // 2, g.shape[2]), MXU_DTYPE) for g in gsh for _ in range(4)])
    chip_part = [_sum_slots(p.reshape(2, -1, p.shape[-1]), MXU_DTYPE, "rs_pair_sum%d" % i).reshape(p.shape[1:])
                 for i, p in enumerate(pair)]
    moves = [(i, i, lambda ref, me, peer: ref.at[_chip_of(peer)], lambda ref, me: ref.at[_chip_of(me)])
             for i in range(nbig)]
    recv = _exchange("rs_chips", CHIP_FLIPS, chip_part, [sds(p.shape, MXU_DTYPE) for p in chip_part], moves)
    g_half = [_sum_slots(rv, F32, "rs_chip_sum%d" % i) for i, rv in enumerate(recv)]
    moves = [(i, i, lambda ref, me, peer: ref, lambda ref, me, r=2 * g.shape[0]: ref.at[_half(r, me[2])])
             for i, g in enumerate(g_half)]
    g_full = _exchange("rs_share", PAIR_FLIPS, g_half, [sds((2 * g.shape[0], g.shape[1]), F32) for g in g_half], moves,
                       stage=[(g.shape, F32) for g in g_half])
    for (n, _, _), g in zip(BIG, g_full):
        grads[n] = g[None]
    grads["w_ada"] = g_w_ada[None]
    grads["b_ada"] = g_b_ada

    delta, new_m, new_v = {}, {}, {}
    to2 = lambda z: z.reshape(-1, z.shape[-1])
    for n in ["w_ada"] + [b[0] for b in BIG]:
        d_, m_, v2_ = _adamw(to2(W[n]), to2(grads[n]), to2(M[n]), to2(V[n]), "adamw_" + n)
        delta[n], new_m[n], new_v[n] = (z.reshape(W[n].shape) for z in (d_, m_, v2_))
    rest = [n for n in names if n not in delta]
    packs = [_pack_rows([src[n] for n in rest], F32, SUBLANES) for src in (W, grads, M, V)]
    d_, m_, v2_ = _adamw(*packs, "adamw_small")
    shapes = [W[n].shape for n in rest]
    for dst, z in ((delta, d_), (new_m, m_), (new_v, v2_)):
        for n, val in zip(rest, _unpack(z.reshape(-1), shapes)):
            dst[n] = val

    return (loss, grad_x.reshape(Bl, S, D), *[grads[n] for n in names], *[delta[n] for n in names],
            *[new_m[n] for n in names], *[new_v[n] for n in names])
```

```python
import functools
import math

import jax
import jax.numpy as jnp
from jax import lax
from jax.experimental import pallas as pl
from jax.experimental.pallas import tpu as pltpu

F32 = jnp.float32
BF16 = jnp.bfloat16
MXU_DTYPE = jnp.bfloat16
MESH_IDS = pl.DeviceIdType.MESH
HIGHEST = lax.Precision.HIGHEST

D = 1024
RW, NH, HD = 512, 8, 64
LW, LA, LG = 64, 64, 128
SW, SGC, NG, SP = 512, 16, 32, 64
NSG = 4
SHIFT = 3 * RW + LW + LA + LG
DFF = 2816
RMS_EPS, GN_EPS, L2_EPS = 1e-6, 64e-5, 1e-12
LR, B1, B2, ADAM_EPS, WD, STEP = 0.001, 0.9, 0.999, 1e-8, 0.01, 10
DECAY_SCALE = math.exp(-0.5)
GELU_C = math.sqrt(2.0 / math.pi)

VMEM_LIMIT = 52 * 1024 * 1024
SUBLANES, LANES = 8, 128


def _pick(n, cap):
    if n <= cap:
        return n
    best = None
    for t in range(LANES, cap + 1, LANES):
        if n % t == 0:
            best = t
    assert best is not None, (n, cap)
    return best


def _params(sem=None, vmem=VMEM_LIMIT):
    return pltpu.CompilerParams(dimension_semantics=sem, vmem_limit_bytes=vmem)


def _chip_of(p):
    return 2 * p[0] + p[1]


def _me():
    return (lax.axis_index("x"), lax.axis_index("y"), lax.axis_index("c"))


def _half(rows, core):
    h = rows // 2
    return pl.ds(pl.multiple_of(core * h, 16 if h % 16 == 0 else SUBLANES), h)


def _exchange(name, flips, srcs, outs, moves, stage=None):
    ns, no, nf, nm = len(srcs), len(outs), len(flips), len(moves)
    nstage = nm * nf if stage else 0

    def body(*refs):
        src_refs, out_refs = refs[:ns], refs[ns:ns + no]
        send_sems, recv_sems, loc_sems = refs[ns + no:ns + no + 3]
        stage_bufs = refs[ns + no + 3:ns + no + 3 + nstage]
        me = _me()
        copies, locs, loads = [], [], []
        for m, (si, oi, src_sel, dst_sel) in enumerate(moves):
            for k, f in enumerate(flips):
                peer = tuple(1 - v if b else v for v, b in zip(me, f))
                piece = src_sel(src_refs[si], me, peer)
                if stage:
                    ld = pltpu.make_async_copy(piece, stage_bufs[m * nf + k], refs[-1].at[m * nf + k])
                    ld.start()
                    loads.append(ld)
                    piece = stage_bufs[m * nf + k]
                copies.append(pltpu.make_async_remote_copy(
                    src_ref=piece, dst_ref=dst_sel(out_refs[oi], me),
                    send_sem=send_sems.at[m * nf + k], recv_sem=recv_sems.at[m * nf + k],
                    device_id=peer, device_id_type=MESH_IDS))
            loc = pltpu.make_async_copy(src_sel(src_refs[si], me, me), dst_sel(out_refs[oi], me), loc_sems.at[m])
            loc.start()
            locs.append(loc)
        for i, cp in enumerate(copies):
            if stage:
                loads[i].wait()
            cp.start()
        for cp in copies:
            cp.wait_recv()
        for cp in copies:
            cp.wait_send()
        for loc in locs:
            loc.wait()

    scratch = [pltpu.SemaphoreType.DMA((nm * nf,)), pltpu.SemaphoreType.DMA((nm * nf,)), pltpu.SemaphoreType.DMA((nm,))]
    if stage:
        scratch += [pltpu.VMEM(shp, dt) for shp, dt in stage for _ in flips] + [pltpu.SemaphoreType.DMA((nstage,))]
    return pl.pallas_call(
        body, name=name, out_shape=list(outs),
        in_specs=[pl.BlockSpec(memory_space=pl.ANY)] * ns,
        out_specs=[pl.BlockSpec(memory_space=pl.ANY)] * no,
        scratch_shapes=scratch, compiler_params=_params(),
    )(*srcs)


CHIP_FLIPS = ((1, 0, 0), (0, 1, 0), (1, 1, 0))
PAIR_FLIPS = ((0, 0, 1),)
ALL_FLIPS = CHIP_FLIPS + ((1, 0, 1), (0, 1, 1), (1, 1, 1)) + PAIR_FLIPS


def _gather_two_level(chip_arrs, dev_arrs, name):
    arrs = list(chip_arrs) + list(dev_arrs)
    n, nchip = len(arrs), len(chip_arrs)
    NS = 7

    def body(*refs):
        srcs, outs = refs[:n], refs[n:2 * n]
        send_sems, recv_sems, loc_sems = refs[2 * n:]
        x, y, c = _me()
        sib = (x, y, 1 - c)
        chips = [(1 - x, y), (x, 1 - y), (1 - x, 1 - y)]
        mine = 2 * x + y
        ids = [2 * cx + cy for cx, cy in chips]

        def part(i, slot, core):
            if i < nchip:
                return outs[i].at[slot, _half(arrs[i].shape[0], core)]
            return outs[i].at[slot, core]

        def rcopy(i, k, src, dst, to):
            return pltpu.make_async_remote_copy(src_ref=src, dst_ref=dst, send_sem=send_sems.at[i * NS + k],
                                                recv_sem=recv_sems.at[i * NS + k], device_id=to, device_id_type=MESH_IDS)

        started, locs = [], []
        for i in range(n):
            own = srcs[i].at[_half(arrs[i].shape[0], c)] if i < nchip else srcs[i]
            loc = pltpu.make_async_copy(srcs[i], outs[i].at[mine] if i < nchip else outs[i].at[mine, c], loc_sems.at[i])
            loc.start()
            locs.append(loc)
            for f, chip in enumerate(chips):
                cp = rcopy(i, f, own, part(i, mine, c), (*chip, c))
                cp.start()
                started.append(cp)
            if i >= nchip:
                cp = rcopy(i, 6, own, part(i, mine, c), sib)
                cp.start()
                started.append(cp)
        for i in range(n):
            for f in range(3):
                land = part(i, ids[f], c)
                rcopy(i, f, land, land, sib).wait_recv()
                fw = rcopy(i, 3 + f, land, land, sib)
                fw.start()
                started.append(fw)
        for i in range(n):
            for f in range(3):
                land = part(i, ids[f], 1 - c)
                rcopy(i, 3 + f, land, land, sib).wait_recv()
            if i >= nchip:
                land = part(i, mine, 1 - c)
                rcopy(i, 6, land, land, sib).wait_recv()
        for cp in started:
            cp.wait_send()
        for loc in locs:
            loc.wait()

    outs = [jax.ShapeDtypeStruct((4,) + a.shape, a.dtype) for a in chip_arrs]
    outs += [jax.ShapeDtypeStruct((4, 2) + a.shape, a.dtype) for a in dev_arrs]
    res = pl.pallas_call(
        body, name=name, out_shape=outs,
        in_specs=[pl.BlockSpec(memory_space=pl.ANY)] * n, out_specs=[pl.BlockSpec(memory_space=pl.ANY)] * n,
        scratch_shapes=[pltpu.SemaphoreType.DMA((n * NS,)), pltpu.SemaphoreType.DMA((n * NS,)),
                        pltpu.SemaphoreType.DMA((n,))],
    )(*arrs)
    return res[:nchip], res[nchip:]


def _mm(As, Bs, out_dtype, name, tm=512, cap=1408, bt=False):
    n = len(As)
    M, N = As[0].shape[0], Bs[0].shape[0 if bt else 1]
    tm = min(tm, M)
    tn = _pick(N, cap)
    dims = (((1,), (1,)), ((), ())) if bt else (((1,), (0,)), ((), ()))

    def body(*refs):
        o = refs[2 * n]
        acc = None
        for a, b in zip(refs[:n], refs[n:2 * n]):
            d = lax.dot_general(a[...].astype(MXU_DTYPE), b[...].astype(MXU_DTYPE), dims, preferred_element_type=F32)
            acc = d if acc is None else acc + d
        o[...] = acc.astype(o.dtype)

    in_specs = [pl.BlockSpec((tm, a.shape[1]), lambda i, j: (i, 0)) for a in As]
    if bt:
        in_specs += [pl.BlockSpec((tn, b.shape[1]), lambda i, j: (j, 0)) for b in Bs]
    else:
        in_specs += [pl.BlockSpec((b.shape[0], tn), lambda i, j: (0, j)) for b in Bs]
    return pl.pallas_call(
        body, name=name, grid=(M // tm, N // tn), in_specs=in_specs,
        out_specs=pl.BlockSpec((tm, tn), lambda i, j: (i, j)),
        out_shape=jax.ShapeDtypeStruct((M, N), out_dtype),
        compiler_params=_params(("parallel", "parallel")),
    )(*As, *Bs)


def _mm_tn(A, G, name, tt=1024, cap=1024):
    T, Ka = A.shape
    N = G.shape[1]
    tt = min(tt, T)
    tk = _pick(Ka, cap)
    tn = _pick(N, cap)

    def body(a, g, o):
        @pl.when(pl.program_id(2) == 0)
        def _():
            o[...] = jnp.zeros(o.shape, F32)
        o[...] += lax.dot_general(a[...].astype(MXU_DTYPE), g[...].astype(MXU_DTYPE),
                                  (((0,), (0,)), ((), ())), preferred_element_type=F32)

    return pl.pallas_call(
        body, name=name, grid=(Ka // tk, N // tn, T // tt),
        in_specs=[pl.BlockSpec((tt, tk), lambda i, j, t: (t, i)), pl.BlockSpec((tt, tn), lambda i, j, t: (t, j))],
        out_specs=pl.BlockSpec((tk, tn), lambda i, j, t: (i, j)),
        out_shape=jax.ShapeDtypeStruct((Ka, N), F32),
        compiler_params=_params(("parallel", "parallel", "arbitrary")),
    )(A, G)


def _rowwise(name, fn, *, Bl, S, R, tiled=(), prev=(), nxt=(), batch=(), full=(),
             out_tiled=(), out_batch=(), out_acc=()):
    R = min(R, S)
    nS = S // R
    T = Bl * S
    hb = R // SUBLANES
    n_in = len(tiled) + len(prev) + len(nxt) + len(batch) + len(full)

    in_specs, args = [], []
    for a, wd, cb in tiled:
        in_specs.append(pl.BlockSpec((R, wd), lambda b, i, cb=cb: (b * nS + i, cb)))
        args.append(a)
    for a, wd, cb in prev:
        in_specs.append(pl.BlockSpec((SUBLANES, wd), lambda b, i, cb=cb: (jnp.maximum((b * nS + i) * hb - 1, 0), cb)))
        args.append(a)
    for a, wd, cb in nxt:
        in_specs.append(pl.BlockSpec((SUBLANES, wd), lambda b, i, cb=cb: (jnp.minimum((b * nS + i + 1) * hb, T // SUBLANES - 1), cb)))
        args.append(a)
    for a, wd, cb in batch:
        in_specs.append(pl.BlockSpec((1, 1, wd), lambda b, i, cb=cb: (b, 0, cb)))
        args.append(a)
    for a in full:
        in_specs.append(pl.BlockSpec(a.shape, lambda b, i, nd=a.ndim: (0,) * nd))
        args.append(a)

    out_specs, out_shape = [], []
    for C, dt in out_tiled:
        out_specs.append(pl.BlockSpec((R, C), lambda b, i: (b * nS + i, 0)))
        out_shape.append(jax.ShapeDtypeStruct((T, C), dt))
    for C in out_batch:
        out_specs.append(pl.BlockSpec((1, 1, C), lambda b, i: (b, 0, 0)))
        out_shape.append(jax.ShapeDtypeStruct((Bl, 1, C), F32))
    for shp in out_acc:
        out_specs.append(pl.BlockSpec(shp, lambda b, i, nd=len(shp): (0,) * nd))
        out_shape.append(jax.ShapeDtypeStruct(shp, F32))

    nt, npv, nnx, nbt = len(tiled), len(prev), len(nxt), len(batch)

    def body(*refs):
        b, i = pl.program_id(0), pl.program_id(1)
        ins, outs = refs[:n_in], refs[n_in:]
        vals = [r[...] for r in ins[:nt]]
        vals += [jnp.where(i > 0, r[...], jnp.zeros(r.shape, r.dtype)) for r in ins[nt:nt + npv]]
        vals += [jnp.where(i < nS - 1, r[...], jnp.zeros(r.shape, r.dtype)) for r in ins[nt + npv:nt + npv + nnx]]
        vals += [r[0] for r in ins[nt + npv + nnx:nt + npv + nnx + nbt]]
        vals += [r[...] for r in ins[nt + npv + nnx + nbt:]]
        res = fn(*vals)
        if not isinstance(res, (tuple, list)):
            res = (res,)
        k = 0
        for _ in out_tiled:
            outs[k][...] = res[k].astype(outs[k].dtype)
            k += 1
        for _ in out_batch:
            o = outs[k]

            @pl.when(i == 0)
            def _(o=o):
                o[...] = jnp.zeros(o.shape, F32)
            o[0] += res[k]
            k += 1
        for _ in out_acc:
            o = outs[k]

            @pl.when((i == 0) & (b == 0))
            def _(o=o):
                o[...] = jnp.zeros(o.shape, F32)
            o[...] += res[k]
            k += 1

    out = pl.pallas_call(
        body, name=name, grid=(Bl, nS), in_specs=in_specs, out_specs=out_specs, out_shape=out_shape,
        compiler_params=_params(("arbitrary", "arbitrary")),
    )(*args)
    return out


def _shift_down(x, halo, k):
    row = lax.broadcasted_iota(jnp.int32, x.shape, 0)
    out = pltpu.roll(x, k, 0)
    for j in range(k):
        out = jnp.where(row == j, halo[SUBLANES - k + j:SUBLANES - k + j + 1, :], out)
    return out


def _shift_up(x, halo, k):
    n = x.shape[0]
    row = lax.broadcasted_iota(jnp.int32, x.shape, 0)
    out = pltpu.roll(x, n - k, 0)
    for j in range(k):
        out = jnp.where(row == n - k + j, halo[j:j + 1, :], out)
    return out


def _dotm(a, b):
    return jnp.dot(a.astype(MXU_DTYPE), b.astype(MXU_DTYPE), preferred_element_type=F32)


def _split_bf16(x):
    hi = x.astype(BF16)
    return hi, (x - hi.astype(F32)).astype(BF16)


def _headsum_2pass(x, hm):
    hi, lo = _split_bf16(x)
    hb = hm.astype(BF16)
    return jnp.dot(hi, hb, preferred_element_type=F32) + jnp.dot(lo, hb, preferred_element_type=F32)


@jax.custom_vjp
def _headsum(x, hm):
    return _headsum_2pass(x, hm)


_headsum.defvjp(lambda x, hm: (_headsum_2pass(x, hm), hm),
                lambda hm, g: (_headsum_2pass(g, hm), jnp.zeros_like(hm)))


def _sigmoid(x):
    return 1.0 / (1.0 + jnp.exp(-x))


def _rms(x, g):
    return x * lax.rsqrt(jnp.mean(x * x, axis=-1, keepdims=True) + RMS_EPS) * g


def _norm_mod(x, g, sc, sh):
    return _rms(x, g) * (1.0 + sc) + sh


def _split_ps(ps):
    return (ps[:, 0:RW], ps[:, RW:2 * RW], ps[:, 2 * RW:3 * RW], ps[:, 3 * RW:3 * RW + LW + LA],
            ps[:, 3 * RW + LW + LA:SHIFT])


def _rwkv_prep(r, k, v, wa, gd, w0, w_up_p, a0, a_up_p, g_up, k_k, k_a, hm):
    w_raw = w0 + _dotm(jnp.tanh(wa), w_up_p)
    decay = jnp.exp(-DECAY_SCALE * _sigmoid(w_raw))
    a = _sigmoid(a0 + _dotm(wa, a_up_p))
    g = _dotm(_sigmoid(gd), g_up)
    kk = k * k_k
    kk = kk * lax.rsqrt(_headsum(kk * kk, hm) + L2_EPS)
    k2 = k * (1.0 + (a - 1.0) * k_a)
    return r, decay, k2, v, -kk, kk * a, g


def _rwkv_post(y, r, k2, v, g, ln_g, ln_b, r_k, hm):
    mean = _headsum(y, hm) * (1.0 / HD)
    yc = y - mean
    var = _headsum(yc * yc, hm) * (1.0 / HD)
    yn = yc * lax.rsqrt(var + GN_EPS) * ln_g + ln_b
    bonus = _headsum(r * k2 * r_k, hm) * v
    return (yn + bonus) * g


def _gelu(x):
    return 0.5 * x * (1.0 + jnp.tanh(GELU_C * (x + 0.044715 * (x * x * x))))


def _s5_post(yssm, u, d):
    return _gelu(yssm + d * u)


def _mix(ga, gb, ya, za, zb):
    return _sigmoid(ga) * ya + _sigmoid(gb) * (za * _sigmoid(zb))


def _conv_act(up_g, up_u, hg, hu, w_g, w_u, b_g, b_u):
    def conv(x, h, w, b):
        return b + w[0:1] * _shift_down(x, h, 2) + w[1:2] * _shift_down(x, h, 1) + w[2:3] * x
    gate = conv(up_g, hg, w_g, b_g)
    upv = conv(up_u, hu, w_u, b_u)
    return gate, upv


def _silu_gate(gate, upv):
    return gate * _sigmoid(gate) * upv


WKV_L = 64
_NT, _NN, _TN = ((1,), (1,)), ((1,), (0,)), ((0,), (0,))


def _dotw(x, y, dims):
    return lax.dot_general(x.astype(MXU_DTYPE), y.astype(MXU_DTYPE), (dims, ((), ())), preferred_element_type=F32)


def _dot3(x, y, dims):
    (xh, xl), (yh, yl) = _split_bf16(x), _split_bf16(y)
    d = lambda p, q: lax.dot_general(p, q, (dims, ((), ())), preferred_element_type=F32)
    return d(xh, yh) + d(xh, yl) + d(xl, yh)


@jax.custom_vjp
def _gram3(x, y):
    return _dot3(x, y, _NT)


_gram3.defvjp(lambda x, y: (_dot3(x, y, _NT), (x, y)),
              lambda res, g: (_dot3(g, res[1], _NN), _dot3(g, res[0], _TN)))


def _wkv_chunk(s0, r, w, k, v, a, b):
    y, s1 = _wkv_chunks((s0,), (r,), (w,), (k,), (v,), (a,), (b,))
    return y[0], s1[0]


def _wkv_chunks(s0, r, w, k, v, a, b):
    each = lambda f, *ls: tuple(f(*xs) for xs in zip(*ls))
    L = r[0].shape[0]
    n2 = 2 * L
    lane_head = lax.broadcasted_iota(jnp.int32, (2, 1, 2 * HD), 2) // HD
    head_mask = (lane_head == lax.broadcasted_iota(jnp.int32, (2, 1, 2 * HD), 0)).astype(F32)
    ri = lax.broadcasted_iota(jnp.int32, (n2, n2), 0)
    ci = lax.broadcasted_iota(jnp.int32, (n2, n2), 1)
    same = (ri // L) == (ci // L)
    strict = same & ((ci % L) < (ri % L))
    incl = same & ((ci % L) <= (ri % L))
    si = lax.broadcasted_iota(jnp.int32, (2 * HD, 2 * HD), 0) // HD
    sj = lax.broadcasted_iota(jnp.int32, (2 * HD, 2 * HD), 1) // HD
    tri = (lax.broadcasted_iota(jnp.int32, (L, L), 0) >= lax.broadcasted_iota(jnp.int32, (L, L), 1)).astype(F32)

    stack = lambda z: (z[None] * head_mask).reshape(n2, 2 * HD)
    dup = lambda z: jnp.broadcast_to(z[None], (2, L, 2 * HD)).reshape(n2, 2 * HD)
    gram = _gram3
    nt, nn, tn = (lambda x, y, d=d: _dotw(x, y, d) for d in (_NT, _NN, _TN))
    add = lambda x, y: x + y

    lw = each(jnp.log, w)
    cum = each(lambda z: jnp.dot(tri, z, preferred_element_type=F32, precision=HIGHEST), lw)
    tot = each(lambda z: jnp.sum(z, axis=0, keepdims=True), lw)
    a2 = each(lambda av, cv, lv: stack(av * jnp.exp(cv - lv)), a, cum, lw)
    r2 = each(lambda rv, cv: stack(rv * jnp.exp(cv)), r, cum)
    v2 = each(stack, v)
    b2 = each(lambda bv, cv: dup(bv * jnp.exp(-cv)), b, cum)
    k2 = each(lambda kv, cv: dup(kv * jnp.exp(-cv)), k, cum)
    n_ab = each(lambda x, y: jnp.where(strict, gram(x, y), 0.0), a2, b2)
    n_ak = each(lambda x, y: jnp.where(strict, gram(x, y), 0.0), a2, k2)
    m_rb = each(lambda x, y: jnp.where(incl, gram(x, y), 0.0), r2, b2)
    m_rk = each(lambda x, y: jnp.where(incl, gram(x, y), 0.0), r2, k2)
    u = each(add, each(nt, a2, s0), each(nn, n_ak, v2))
    q = n_ab
    steps = L.bit_length() - 1
    for i in range(steps):
        u = each(add, u, each(nn, q, u))
        if i < steps - 1:
            q = each(nn, q, q)
    y2 = each(lambda x, y, z: x + y + z, each(nt, r2, s0), each(nn, m_rb, u), each(nn, m_rk, v2))
    y = each(lambda z: jnp.sum(z.reshape(2, L, 2 * HD), axis=0), y2)
    b3 = each(lambda bv, tv, cv: dup(bv * jnp.exp(tv - cv)), b, tot, cum)
    k3 = each(lambda kv, tv, cv: dup(kv * jnp.exp(tv - cv)), k, tot, cum)
    upd = each(add, each(tn, u, b3), each(tn, v2, k3))
    s1 = each(lambda sv, tv, uv: sv * jnp.exp(tv) + jnp.where(si == sj, uv, 0.0), s0, tot, upd)
    return y, s1


NPAIR = NH // 2


def _wkv_nb(Bl):
    return 2 if Bl % 2 == 0 else 1


def _wkv_fwd(r, w, k, v, a, b, Bl, S):
    L = WKV_L
    nC = S // L
    nb = _wkv_nb(Bl)
    chains = [(bi, p, slice(p * 2 * HD, (p + 1) * 2 * HD)) for bi in range(nb) for p in range(NPAIR)]

    def body(r_ref, w_ref, k_ref, v_ref, a_ref, b_ref, y_ref, ck_ref, s_ref):
        @pl.when(pl.program_id(1) == 0)
        def _():
            s_ref[...] = jnp.zeros(s_ref.shape, F32)
        s0 = tuple(s_ref[bi, p] for bi, p, _ in chains)
        ops = [tuple(z[bi, :, cs] for bi, _, cs in chains) for z in (r_ref, w_ref, k_ref, v_ref, a_ref, b_ref)]
        y, s1 = _wkv_chunks(s0, *ops)
        for i, (bi, p, cs) in enumerate(chains):
            ck_ref[bi, 0, p] = s0[i]
            y_ref[bi, :, cs] = y[i]
            s_ref[bi, p] = s1[i]

    to3 = lambda z: z.reshape(Bl, S, RW)
    row_spec = pl.BlockSpec((nb, L, RW), lambda g, c: (g, c, 0))
    y, ck = pl.pallas_call(
        body, name="wkv_fwd", grid=(Bl // nb, nC), in_specs=[row_spec] * 6,
        out_specs=[row_spec, pl.BlockSpec((nb, 1, NPAIR, 2 * HD, 2 * HD), lambda g, c: (g, c, 0, 0, 0))],
        out_shape=[jax.ShapeDtypeStruct((Bl, S, RW), F32), jax.ShapeDtypeStruct((Bl, nC, NPAIR, 2 * HD, 2 * HD), F32)],
        scratch_shapes=[pltpu.VMEM((nb, NPAIR, 2 * HD, 2 * HD), F32)],
        compiler_params=_params(("arbitrary", "arbitrary")),
    )(*(to3(z) for z in (r, w, k, v, a, b)))
    return y.reshape(Bl * S, RW), ck


def _wkv_bwd(r, w, k, v, a, b, dy, ck, Bl, S):
    L = WKV_L
    nC = S // L
    nb = _wkv_nb(Bl)
    chains = [(bi, p, slice(p * 2 * HD, (p + 1) * 2 * HD)) for bi in range(nb) for p in range(NPAIR)]

    def body(r_ref, w_ref, k_ref, v_ref, a_ref, b_ref, dy_ref, ck_ref,
             dr_ref, dw_ref, dk_ref, dv_ref, da_ref, db_ref, ds_ref):
        @pl.when(pl.program_id(1) == 0)
        def _():
            ds_ref[...] = jnp.zeros(ds_ref.shape, F32)
        s0 = tuple(ck_ref[bi, 0, p] for bi, p, _ in chains)
        ops = [tuple(z[bi, :, cs] for bi, _, cs in chains) for z in (r_ref, w_ref, k_ref, v_ref, a_ref, b_ref)]
        cts = (tuple(dy_ref[bi, :, cs] for bi, _, cs in chains), tuple(ds_ref[bi, p] for bi, p, _ in chains))
        ds0, *grads = jax.vjp(_wkv_chunks, s0, *ops)[1](cts)
        for i, (bi, p, cs) in enumerate(chains):
            ds_ref[bi, p] = ds0[i]
            for o, g in zip((dr_ref, dw_ref, dk_ref, dv_ref, da_ref, db_ref), grads):
                o[bi, :, cs] = g[i]

    to3 = lambda z: z.reshape(Bl, S, RW)
    row_spec = pl.BlockSpec((nb, L, RW), lambda g, c: (g, nC - 1 - c, 0))
    rows = jax.ShapeDtypeStruct((Bl, S, RW), F32)
    outs = pl.pallas_call(
        body, name="wkv_bwd", grid=(Bl // nb, nC),
        in_specs=[row_spec] * 7 + [pl.BlockSpec((nb, 1, NPAIR, 2 * HD, 2 * HD), lambda g, c: (g, nC - 1 - c, 0, 0, 0))],
        out_specs=[row_spec] * 6, out_shape=[rows] * 6,
        scratch_shapes=[pltpu.VMEM((nb, NPAIR, 2 * HD, 2 * HD), F32)],
        compiler_params=_params(("arbitrary", "arbitrary")),
    )(*(to3(z) for z in (r, w, k, v, a, b, dy)), ck)
    return [o.reshape(Bl * S, RW) for o in outs]


NST = NG * SP


def _cmul(ar, ai, br, bi):
    return ar * br - ai * bi, ar * bi + ai * br


def _s5_tiles(are, aim, reverse):
    if reverse:
        aim = -aim
    row = lax.broadcasted_iota(jnp.int32, (SUBLANES, NST), 0)
    pw = [(are, aim)]
    for _ in range(SUBLANES - 1):
        pw.append(_cmul(pw[-1][0], pw[-1][1], are, aim))
    bc = lambda z: jnp.broadcast_to(z, (SUBLANES, NST))
    ms = []
    for kk in (1, 2, 4):
        cond = (row < SUBLANES - kk) if reverse else (row >= kk)
        ms.append((jnp.where(cond, bc(pw[kk - 1][0]), 0.0), jnp.where(cond, bc(pw[kk - 1][1]), 0.0)))
    pr = jnp.zeros((SUBLANES, NST), F32)
    pi = jnp.zeros((SUBLANES, NST), F32)
    for i in range(SUBLANES):
        n = SUBLANES - i if reverse else i + 1
        pr = jnp.where(row == i, bc(pw[n - 1][0]), pr)
        pi = jnp.where(row == i, bc(pw[n - 1][1]), pi)
    return ms, (pr, pi)


def _s5_block(re, im, ms, pc, cre, cim, sg, reverse):
    ln = slice(sg * 512, (sg + 1) * 512)
    for (mr, mi), kk in zip(ms, (1, 2, 4)):
        sh = SUBLANES - kk if reverse else kk
        sre, sim = pltpu.roll(re, sh, 0), pltpu.roll(im, sh, 0)
        tr, ti = _cmul(mr[:, ln], mi[:, ln], sre, sim)
        re, im = re + tr, im + ti
    tr, ti = _cmul(pc[0][:, ln], pc[1][:, ln], cre[:, ln], cim[:, ln])
    return re + tr, im + ti


def _s5_scan(X_ref, n_rows, ms, pc, cre, cim, reverse, visit=None, acc0=None):
    nblk = n_rows // SUBLANES

    def it(i, carry):
        cre, cim, acc = carry
        j = nblk - 1 - i if reverse else i
        rows = pl.ds(pl.multiple_of(j * SUBLANES, SUBLANES), SUBLANES)
        edge = 0 if reverse else SUBLANES - 1
        blocks, ncre, ncim = [], [], []
        for sg in range(NSG):
            lr = slice(sg * 1024, sg * 1024 + 512)
            li = slice(sg * 1024 + 512, (sg + 1) * 1024)
            re, im = _s5_block(X_ref[rows, lr], X_ref[rows, li], ms, pc, cre, cim, sg, reverse)
            X_ref[rows, lr] = re
            X_ref[rows, li] = im
            blocks.append((re, im))
            ncre.append(re[edge:edge + 1])
            ncim.append(im[edge:edge + 1])
        if visit is not None:
            acc = visit(j, blocks, acc)
        return jnp.concatenate(ncre, axis=1), jnp.concatenate(ncim, axis=1), acc

    return lax.fori_loop(0, nblk, it, (cre, cim, acc0 if acc0 is not None else 0))


def _s5_fwd(u, wb, wc, ab, Bl, S, R=256):
    R = min(R, S)
    nC = S // R

    def body(u_ref, wb_ref, wc_ref, ab_ref, y_ref, st_ref, X_ref, car_ref):
        @pl.when(pl.program_id(1) == 0)
        def _():
            car_ref[...] = jnp.zeros(car_ref.shape, F32)
        st_ref[0, 0] = car_ref[...]
        ms, pc = _s5_tiles(ab_ref[0:1], ab_ref[1:2], False)
        for sg in range(NSG):
            X_ref[:, sg * 1024:(sg + 1) * 1024] = _dotm(u_ref[:, sg * 128:(sg + 1) * 128], wb_ref[sg])
        cre, cim, _ = _s5_scan(X_ref, R, ms, pc, car_ref[0:1], car_ref[1:2], False)
        car_ref[0:1] = cre
        car_ref[1:2] = cim
        for sg in range(NSG):
            y_ref[:, sg * 128:(sg + 1) * 128] = _dotm(X_ref[:, sg * 1024:(sg + 1) * 1024], wc_ref[sg])

    return pl.pallas_call(
        body, name="s5_fwd", grid=(Bl, nC),
        in_specs=[pl.BlockSpec((R, SW), lambda b, c: (b * nC + c, 0)),
                  pl.BlockSpec(wb.shape, lambda b, c: (0, 0, 0)), pl.BlockSpec(wc.shape, lambda b, c: (0, 0, 0)),
                  pl.BlockSpec(ab.shape, lambda b, c: (0, 0))],
        out_specs=[pl.BlockSpec((R, SW), lambda b, c: (b * nC + c, 0)),
                   pl.BlockSpec((1, 1, 2, NST), lambda b, c: (b, c, 0, 0))],
        out_shape=[jax.ShapeDtypeStruct((Bl * S, SW), F32), jax.ShapeDtypeStruct((Bl, nC, 2, NST), F32)],
        scratch_shapes=[pltpu.VMEM((R, 2 * NST), F32), pltpu.VMEM((2, NST), F32)],
        compiler_params=_params(("arbitrary", "arbitrary")),
    )(u, wb, wc, ab)


def _s5_bwd(u, dy, wb, wc, ab, st, Bl, S, R=256):
    R = min(R, S)
    nC = S // R

    def body(u_ref, dy_ref, wb_ref, wc_ref, ab_ref, st_ref, du_ref, dwb_ref, dwc_ref, dab_ref,
             X_ref, G_ref, car_ref):
        first = (pl.program_id(0) == 0) & (pl.program_id(1) == 0)

        @pl.when(first)
        def _():
            dwb_ref[...] = jnp.zeros(dwb_ref.shape, F32)
            dwc_ref[...] = jnp.zeros(dwc_ref.shape, F32)
            dab_ref[...] = jnp.zeros(dab_ref.shape, F32)

        @pl.when(pl.program_id(1) == 0)
        def _():
            car_ref[...] = jnp.zeros(car_ref.shape, F32)

        are, aim = ab_ref[0:1], ab_ref[1:2]
        ms, pc = _s5_tiles(are, aim, False)
        for sg in range(NSG):
            X_ref[:, sg * 1024:(sg + 1) * 1024] = _dotm(u_ref[:, sg * 128:(sg + 1) * 128], wb_ref[sg])
        _s5_scan(X_ref, R, ms, pc, st_ref[0, 0, 0:1], st_ref[0, 0, 1:2], False)
        dyv = dy_ref[...].astype(MXU_DTYPE)
        for sg in range(NSG):
            G_ref[:, sg * 1024:(sg + 1) * 1024] = lax.dot_general(
                dyv[:, sg * 128:(sg + 1) * 128], wc_ref[sg].astype(MXU_DTYPE), (((1,), (1,)), ((), ())),
                preferred_element_type=F32)
        rms_, rpc = _s5_tiles(are, aim, True)
        row = lax.broadcasted_iota(jnp.int32, (SUBLANES, 512), 0)

        def visit(j, blocks, acc):
            before = pl.multiple_of(jnp.maximum(j - 1, 0) * SUBLANES, SUBLANES)
            prow = X_ref[pl.ds(before, SUBLANES), :][SUBLANES - 1:SUBLANES]
            rows = pl.ds(pl.multiple_of(j * SUBLANES, SUBLANES), SUBLANES)
            are_acc, aim_acc = [], []
            for sg in range(NSG):
                lr = slice(sg * 1024, sg * 1024 + 512)
                li = slice(sg * 1024 + 512, (sg + 1) * 1024)
                ln = slice(sg * 512, (sg + 1) * 512)
                pre = jnp.where(j > 0, prow[:, lr], st_ref[0, 0, 0:1, ln])
                pim = jnp.where(j > 0, prow[:, li], st_ref[0, 0, 1:2, ln])
                xre = jnp.where(row == 0, pre, pltpu.roll(X_ref[rows, lr], 1, 0))
                xim = jnp.where(row == 0, pim, pltpu.roll(X_ref[rows, li], 1, 0))
                dre, dim = blocks[sg]
                are_acc.append(dre * xre + dim * xim)
                aim_acc.append(dim * xre - dre * xim)
            return acc[0] + jnp.concatenate(are_acc, axis=1), acc[1] + jnp.concatenate(aim_acc, axis=1)

        zero = jnp.zeros((SUBLANES, NST), F32)
        cre, cim, acc = _s5_scan(G_ref, R, rms_, rpc, car_ref[0:1], car_ref[1:2], True, visit, (zero, zero))
        car_ref[0:1] = cre
        car_ref[1:2] = cim
        dab_ref[0:1] += jnp.sum(acc[0], axis=0, keepdims=True)
        dab_ref[1:2] += jnp.sum(acc[1], axis=0, keepdims=True)
        uv = u_ref[...].astype(MXU_DTYPE)
        for sg in range(NSG):
            cs = slice(sg * 1024, (sg + 1) * 1024)
            us = slice(sg * 128, (sg + 1) * 128)
            gx = G_ref[:, cs].astype(MXU_DTYPE)
            dwb_ref[sg] += lax.dot_general(uv[:, us], gx, (((0,), (0,)), ((), ())), preferred_element_type=F32)
            dwc_ref[sg] += lax.dot_general(X_ref[:, cs].astype(MXU_DTYPE), dyv[:, us], (((0,), (0,)), ((), ())),
                                           preferred_element_type=F32)
            du_ref[:, us] = lax.dot_general(gx, wb_ref[sg].astype(MXU_DTYPE), (((1,), (1,)), ((), ())),
                                            preferred_element_type=F32)

    rmap = lambda b, c: (b * nC + nC - 1 - c, 0)
    return pl.pallas_call(
        body, name="s5_bwd", grid=(Bl, nC),
        in_specs=[pl.BlockSpec((R, SW), rmap), pl.BlockSpec((R, SW), rmap),
                  pl.BlockSpec(wb.shape, lambda b, c: (0, 0, 0)), pl.BlockSpec(wc.shape, lambda b, c: (0, 0, 0)),
                  pl.BlockSpec(ab.shape, lambda b, c: (0, 0)),
                  pl.BlockSpec((1, 1, 2, NST), lambda b, c: (b, nC - 1 - c, 0, 0))],
        out_specs=[pl.BlockSpec((R, SW), rmap), pl.BlockSpec(wb.shape, lambda b, c: (0, 0, 0)),
                   pl.BlockSpec(wc.shape, lambda b, c: (0, 0, 0)), pl.BlockSpec((2, NST), lambda b, c: (0, 0))],
        out_shape=[jax.ShapeDtypeStruct((Bl * S, SW), F32), jax.ShapeDtypeStruct(wb.shape, F32),
                   jax.ShapeDtypeStruct(wc.shape, F32), jax.ShapeDtypeStruct((2, NST), F32)],
        scratch_shapes=[pltpu.VMEM((R, 2 * NST), F32), pltpu.VMEM((R, 2 * NST), F32), pltpu.VMEM((2, NST), F32)],
        compiler_params=_params(("arbitrary", "arbitrary")),
    )(u, dy, wb, wc, ab, st)


def _s5_disc_math(a_re, a_im, log_dt, b_re, b_im, expand):
    dt = jnp.exp(log_dt)
    z_re, z_im = a_re * dt, a_im * dt
    mag = jnp.exp(z_re)
    ab_re, ab_im = mag * jnp.cos(z_im), mag * jnp.sin(z_im)
    den = a_re * a_re + a_im * a_im
    q_re = ((ab_re - 1.0) * a_re + ab_im * a_im) / den
    q_im = (ab_im * a_re - (ab_re - 1.0) * a_im) / den
    qe_re = jnp.dot(q_re, expand, preferred_element_type=F32, precision=HIGHEST)
    qe_im = jnp.dot(q_im, expand, preferred_element_type=F32, precision=HIGHEST)
    return ab_re, ab_im, qe_re * b_re - qe_im * b_im, qe_re * b_im + qe_im * b_re


def _whole(shape):
    return pl.BlockSpec(shape, lambda nd=len(shape): (0,) * nd)


def _s5_disc(a_re, a_im, log_dt, b_re, b_im, expand):
    def body(a, b, c, d, e, f, o0, o1, o2, o3):
        res = _s5_disc_math(a[...], b[...], c[...], d[...], e[...], f[...])
        for o, v in zip((o0, o1, o2, o3), res):
            o[...] = v
    ins = (a_re, a_im, log_dt, b_re, b_im, expand)
    outs = [jax.ShapeDtypeStruct(a_re.shape, F32)] * 2 + [jax.ShapeDtypeStruct(b_re.shape, F32)] * 2
    return pl.pallas_call(body, name="s5_disc", in_specs=[_whole(x.shape) for x in ins],
                          out_specs=[_whole(o.shape) for o in outs], out_shape=outs)(*ins)


def _s5_disc_bwd(a_re, a_im, log_dt, b_re, b_im, expand, cts):
    def body(a, b, c, d, e, f, g0, g1, g2, g3, o0, o1, o2, o3, o4):
        fn = lambda *p: _s5_disc_math(*p, f[...])
        _, vjp = jax.vjp(fn, a[...], b[...], c[...], d[...], e[...])
        for o, v in zip((o0, o1, o2, o3, o4), vjp((g0[...], g1[...], g2[...], g3[...]))):
            o[...] = v
    ins = (a_re, a_im, log_dt, b_re, b_im, expand) + tuple(cts)
    outs = [jax.ShapeDtypeStruct(x.shape, F32) for x in (a_re, a_im, log_dt, b_re, b_im)]
    return pl.pallas_call(body, name="s5_disc_bwd", in_specs=[_whole(x.shape) for x in ins],
                          out_specs=[_whole(o.shape) for o in outs], out_shape=outs)(*ins)


def _ada_fwd(c_all, w_shard, b_shard):
    def body(c_ref, w_ref, b_ref, o_ref):
        cv = c_ref[...]
        o_ref[...] = _dotm(cv * _sigmoid(cv), w_ref[...]) + b_ref[...]
    n = w_shard.shape[1]
    return pl.pallas_call(
        body, name="ada_fwd", in_specs=[_whole(c_all.shape), _whole(w_shard.shape), _whole(b_shard.shape)],
        out_specs=_whole((c_all.shape[0], n)), out_shape=jax.ShapeDtypeStruct((c_all.shape[0], n), F32),
        compiler_params=_params(),
    )(c_all, w_shard, b_shard)


def _ada_bwd(c_all, dmod_cols, dmod_all):
    def body(c_ref, dc_ref, da_ref, gw_ref, gb_ref):
        cv = c_ref[...]
        gw_ref[...] = lax.dot_general((cv * _sigmoid(cv)).astype(MXU_DTYPE), dc_ref[...].astype(MXU_DTYPE),
                                      (((0,), (0,)), ((), ())), preferred_element_type=F32)
        gb_ref[...] = jnp.sum(da_ref[...], axis=0, keepdims=True)
    n = dmod_cols.shape[1]
    return pl.pallas_call(
        body, name="ada_bwd", in_specs=[_whole(c_all.shape), _whole(dmod_cols.shape), _whole(dmod_all.shape)],
        out_specs=[_whole((D, n)), _whole((1, dmod_all.shape[1]))],
        out_shape=[jax.ShapeDtypeStruct((D, n), F32), jax.ShapeDtypeStruct((1, dmod_all.shape[1]), F32)],
        compiler_params=_params(),
    )(c_all, dmod_cols, dmod_all)


def _rows_block(n_rows, cap=512):
    if n_rows <= cap:
        return n_rows
    for t in range(cap - cap % SUBLANES, 0, -SUBLANES):
        if n_rows % t == 0:
            return t
    return n_rows


def _adamw(w, g, m, v, name):
    rows, cols = w.shape
    tr = _rows_block(rows, max(SUBLANES, (1 << 19) // max(cols, 1) // SUBLANES * SUBLANES))

    def body(w_ref, g_ref, m_ref, v_ref, d_ref, nm_ref, nv_ref):
        gv = g_ref[...]
        nm = B1 * m_ref[...] + (1.0 - B1) * gv
        nv = B2 * v_ref[...] + (1.0 - B2) * (gv * gv)
        m_hat = nm / (1.0 - B1 ** STEP)
        v_hat = nv / (1.0 - B2 ** STEP)
        d_ref[...] = -LR * (m_hat / (jnp.sqrt(v_hat) + ADAM_EPS) + WD * w_ref[...])
        nm_ref[...] = nm
        nv_ref[...] = nv

    spec = pl.BlockSpec((tr, cols), lambda i: (i, 0))
    sd = jax.ShapeDtypeStruct((rows, cols), F32)
    return pl.pallas_call(body, name=name, grid=(rows // tr,), in_specs=[spec] * 4, out_specs=[spec] * 3,
                          out_shape=[sd] * 3, compiler_params=_params(("parallel",)))(w, g, m, v)


def _sum_slots(x, out_dtype, name):
    n, rows, cols = x.shape
    tr = _rows_block(rows)

    def body(x_ref, o_ref):
        acc = x_ref[0].astype(F32)
        for j in range(1, n):
            acc = acc + x_ref[j].astype(F32)
        o_ref[...] = acc.astype(o_ref.dtype)

    return pl.pallas_call(
        body, name=name, grid=(rows // tr,), in_specs=[pl.BlockSpec((n, tr, cols), lambda i: (0, i, 0))],
        out_specs=pl.BlockSpec((tr, cols), lambda i: (i, 0)), out_shape=jax.ShapeDtypeStruct((rows, cols), out_dtype),
        compiler_params=_params(("parallel",)))(x)


PACK_COLS = 1024


def _pack_rows(parts, dtype, row_mult):
    flat = jnp.concatenate([p.reshape(-1).astype(dtype) for p in parts])
    per = PACK_COLS * row_mult
    n = -(-flat.shape[0] // per) * per
    flat = jnp.pad(flat, (0, n - flat.shape[0]))
    return flat.reshape(n // PACK_COLS, PACK_COLS)


def _unpack(flat, shapes):
    out, off = [], 0
    for s in shapes:
        n = math.prod(s)
        out.append(flat[off:off + n].reshape(s))
        off += n
    return out


BIG = (("w_in", (D, SHIFT + SW + 2 * D), 1), ("w_out_rwkv", (RW, D), 1), ("w_glu", (SW, 2 * D), 1),
       ("w_out", (D, D), 0), ("w_ffn_up", (D, 2 * DFF), 1), ("w_ffn_down", (DFF, D), 0))
BIG_SMALL = (("rwkv_w_up", (LW, RW), 1), ("rwkv_a_up", (LA, RW), 1), ("rwkv_g_up", (LG, RW), 1),
             ("ffn_conv_w", (3, 2 * DFF), 1))


def _shard_shape(shape, axis):
    return (shape[0] // 4, shape[1]) if axis == 0 else (shape[0], shape[1] // 4)


def _to_shards(g, axis):
    r, C = g.shape
    return g.reshape(4, r // 4, C) if axis == 0 else g.reshape(r, 4, C // 4).transpose(1, 0, 2)


def _from_shards(x, axis):
    _, r, C = x.shape
    return x.reshape(4 * r, C) if axis == 0 else x.transpose(1, 0, 2).reshape(r, 4 * C)


def kernel(x, c, w_ada, b_ada, norm1_g, w_in, mu_shift, rwkv_w0, rwkv_w_up, rwkv_a0, rwkv_a_up, rwkv_g_up, rwkv_k_k, rwkv_k_a, rwkv_r_k, rwkv_ln_g, rwkv_ln_b, w_out_rwkv, s5_a_re, s5_a_im, s5_log_dt, s5_b_re, s5_b_im, s5_c_re, s5_c_im, s5_d, w_glu, w_out, norm2_g, w_ffn_up, ffn_conv_w, ffn_conv_b, w_ffn_down, norm_f_g, loss_target, m_w_ada, m_b_ada, m_norm1_g, m_w_in, m_mu_shift, m_rwkv_w0, m_rwkv_w_up, m_rwkv_a0, m_rwkv_a_up, m_rwkv_g_up, m_rwkv_k_k, m_rwkv_k_a, m_rwkv_r_k, m_rwkv_ln_g, m_rwkv_ln_b, m_w_out_rwkv, m_s5_a_re, m_s5_a_im, m_s5_log_dt, m_s5_b_re, m_s5_b_im, m_s5_c_re, m_s5_c_im, m_s5_d, m_w_glu, m_w_out, m_norm2_g, m_w_ffn_up, m_ffn_conv_w, m_ffn_conv_b, m_w_ffn_down, m_norm_f_g, v_w_ada, v_b_ada, v_norm1_g, v_w_in, v_mu_shift, v_rwkv_w0, v_rwkv_w_up, v_rwkv_a0, v_rwkv_a_up, v_rwkv_g_up, v_rwkv_k_k, v_rwkv_k_a, v_rwkv_r_k, v_rwkv_ln_g, v_rwkv_ln_b, v_w_out_rwkv, v_s5_a_re, v_s5_a_im, v_s5_log_dt, v_s5_b_re, v_s5_b_im, v_s5_c_re, v_s5_c_im, v_s5_d, v_w_glu, v_w_out, v_norm2_g, v_w_ffn_up, v_ffn_conv_w, v_ffn_conv_b, v_w_ffn_down, v_norm_f_g):
    names = ["w_ada", "b_ada", "norm1_g", "w_in", "mu_shift", "rwkv_w0", "rwkv_w_up", "rwkv_a0", "rwkv_a_up",
             "rwkv_g_up", "rwkv_k_k", "rwkv_k_a", "rwkv_r_k", "rwkv_ln_g", "rwkv_ln_b", "w_out_rwkv", "s5_a_re",
             "s5_a_im", "s5_log_dt", "s5_b_re", "s5_b_im", "s5_c_re", "s5_c_im", "s5_d", "w_glu", "w_out", "norm2_g",
             "w_ffn_up", "ffn_conv_w", "ffn_conv_b", "w_ffn_down", "norm_f_g"]
    env = dict(locals())
    W = {n: env[n] for n in names}
    M = {n: env["m_" + n] for n in names}
    V = {n: env["v_" + n] for n in names}

    Bl, S, _ = x.shape
    T = Bl * S
    ix, iy, ic = lax.axis_index("x"), lax.axis_index("y"), lax.axis_index("c")
    chip = 2 * ix + iy
    dev = 2 * chip + ic
    rw = functools.partial(_rowwise, Bl=Bl, S=S)

    chip_arrs = [W[n][0].astype(MXU_DTYPE) for n, _, _ in BIG] + [W[n][0] for n, _, _ in BIG_SMALL[:3]]
    got_chip, got_dev = _gather_two_level(chip_arrs, [W["ffn_conv_w"][0], c], "gather_w")
    full = {n: _from_shards(g, axis) for (n, _, axis), g in zip(BIG + BIG_SMALL[:3], got_chip)}
    full["ffn_conv_w"] = _from_shards(got_dev[0][:, 0], 1)
    c_all = got_dev[1].reshape(8 * Bl, D)
    w_p, w_u, w_g = full["w_in"][:, :SHIFT], full["w_in"][:, SHIFT:SHIFT + SW], full["w_in"][:, SHIFT + SW:]
    zeros_l = jnp.zeros((LW, RW), F32)
    w_up_p = jnp.concatenate([full["rwkv_w_up"], zeros_l], axis=0)
    a_up_p = jnp.concatenate([zeros_l, full["rwkv_a_up"]], axis=0)
    g_up = full["rwkv_g_up"]
    conv_w = full["ffn_conv_w"]
    conv_wg, conv_wu = conv_w[:, :DFF], conv_w[:, DFF:]
    conv_bg, conv_bu = ffn_conv_b[:, :DFF], ffn_conv_b[:, DFF:]
    hm = jnp.kron(jnp.eye(NH, dtype=F32), jnp.ones((HD, HD), F32))

    ncol = 6 * D // 4
    b_ada_cols = lax.dynamic_slice_in_dim(b_ada, chip * ncol, ncol, 1)
    mod_part = _ada_fwd(c_all, w_ada[0], b_ada_cols)
    mod4 = _gather_two_level([], [mod_part], "gather_mod")[1][0][:, 0]
    mod = lax.dynamic_slice_in_dim(mod4, dev * Bl, Bl, 1).transpose(1, 0, 2).reshape(Bl, 1, 6 * D)
    SH1, SC1, GT1, SH2, SC2, GT2 = range(6)

    x2d = x.reshape(T, D)
    tgt = loss_target.reshape(T, D)

    (h1,) = rw("norm1", lambda xv, sc, sh, g: _norm_mod(xv, g, sc, sh), R=256, tiled=[(x2d, D, 0)],
               batch=[(mod, D, SC1), (mod, D, SH1)], full=[norm1_g], out_tiled=[(D, MXU_DTYPE)])
    p = _mm([h1], [w_p], F32, "proj_p")
    u = _mm([h1], [w_u], F32, "proj_u")
    gates = _mm([h1], [w_g], F32, "proj_g")

    prep_params = [rwkv_w0, w_up_p, rwkv_a0, a_up_p, g_up, rwkv_k_k, rwkv_k_a, hm]

    def prep_fwd(pv, ph, mu, *pp):
        ps = pv + (_shift_down(pv, ph, 1) - pv) * mu
        return _rwkv_prep(*_split_ps(ps), *pp)

    r_, w_, k_, v_, a_, b_, g_ = rw("rwkv_prep", prep_fwd, R=256, tiled=[(p, SHIFT, 0)], prev=[(p, SHIFT, 0)],
                                    full=[mu_shift] + prep_params, out_tiled=[(RW, F32)] * 7)
    y_wkv, ck = _wkv_fwd(r_, w_, k_, v_, a_, b_, Bl, S)
    r_k_row = rwkv_r_k.reshape(1, RW)
    post_params = [rwkv_ln_g, rwkv_ln_b, r_k_row, hm]
    (o_rwkv,) = rw("rwkv_post", _rwkv_post, R=256,
                   tiled=[(y_wkv, RW, 0), (r_, RW, 0), (k_, RW, 0), (v_, RW, 0), (g_, RW, 0)],
                   full=post_params, out_tiled=[(RW, MXU_DTYPE)])
    y_a = _mm([o_rwkv], [full["w_out_rwkv"]], F32, "out_rwkv")

    expand = jnp.kron(jnp.eye(SP, dtype=F32), jnp.ones((1, SGC), F32))
    s5_in = (s5_a_re[0], s5_a_im[0], s5_log_dt[0].reshape(NG, 1), s5_b_re[0].reshape(NG, SP * SGC),
             s5_b_im[0].reshape(NG, SP * SGC), expand)
    ab_re, ab_im, bb_re, bb_im = _s5_disc(*s5_in)
    eye8 = jnp.eye(8, dtype=F32)

    def blockdiag_in(bb):
        t = bb.reshape(NSG, 8, SP, SGC)
        return jnp.einsum("ab,sapc->sacbp", eye8, t).reshape(NSG, 128, 512)

    def blockdiag_out(cc):
        t = cc.reshape(NSG, 8, SGC, SP)
        return jnp.einsum("ab,sacp->sapbc", eye8, t).reshape(NSG, 512, 128)

    wb = jnp.concatenate([blockdiag_in(bb_re), blockdiag_in(bb_im)], axis=2).astype(MXU_DTYPE)
    wc = jnp.concatenate([blockdiag_out(s5_c_re[0]), -blockdiag_out(s5_c_im[0])], axis=1).astype(MXU_DTYPE)
    ab = jnp.stack([ab_re.reshape(NST), ab_im.reshape(NST)])
    y_ssm, s5_st = _s5_fwd(u, wb, wc, ab, Bl, S)
    (s5o,) = rw("s5_post", _s5_post, R=256, tiled=[(y_ssm, SW, 0), (u, SW, 0)], full=[s5_d],
                out_tiled=[(SW, MXU_DTYPE)])
    z = _mm([s5o], [full["w_glu"]], F32, "glu")
    mix_tiled = [(gates, D, 0), (gates, D, 1), (y_a, D, 0), (z, D, 0), (z, D, 1)]
    (mixed_in,) = rw("mix", _mix, R=256, tiled=mix_tiled, out_tiled=[(D, MXU_DTYPE)])
    mixed = _mm([mixed_in], [full["w_out"]], F32, "out_proj")

    def norm2_fwd(xv, mx, gt, sc, sh, g):
        x1 = xv + gt * mx
        return x1, _norm_mod(x1, g, sc, sh)

    x1, h2 = rw("norm2", norm2_fwd, R=256, tiled=[(x2d, D, 0), (mixed, D, 0)],
                batch=[(mod, D, GT1), (mod, D, SC2), (mod, D, SH2)], full=[norm2_g],
                out_tiled=[(D, F32), (D, MXU_DTYPE)])
    up = _mm([h2], [full["w_ffn_up"]], F32, "ffn_up")
    conv_tiled = [(up, DFF, 0), (up, DFF, 1)]
    conv_full = [conv_wg, conv_wu, conv_bg, conv_bu]

    def act_fwd(*a):
        return _silu_gate(*_conv_act(*a))

    (act,) = rw("ffn_act", act_fwd, R=128, tiled=conv_tiled, prev=conv_tiled, full=conv_full,
                out_tiled=[(DFF, MXU_DTYPE)])
    ffn = _mm([act], [full["w_ffn_down"]], F32, "ffn_down")

    def head(x1v, fv, tv, gt, g):
        x2 = x1v + gt * fv
        y, vjp = jax.vjp(_rms, x2, g)
        e = y - tv
        dx2, dg = vjp(e * (1.0 / D))
        loss = jnp.sum(e * e, keepdims=True) * jnp.ones((1, LANES), F32)
        return dx2, dx2 * gt, jnp.sum(dx2 * fv, axis=0, keepdims=True), dg.reshape(1, D), loss

    dx2, d_ffn, d_gt2, g_norm_f, loss_acc = rw(
        "head", head, R=256, tiled=[(x1, D, 0), (ffn, D, 0), (tgt, D, 0)], batch=[(mod, D, GT2)],
        full=[norm_f_g.reshape(1, D)], out_tiled=[(D, F32), (D, MXU_DTYPE)], out_batch=[D],
        out_acc=[(1, D), (1, LANES)])
    loss = lax.psum(0.5 / D * loss_acc[0, 0], ("x", "y", "c"))

    d_act = _mm([d_ffn], [full["w_ffn_down"]], F32, "d_act", bt=True)
    g_w_ffn_down = _mm_tn(act, d_ffn, "g_ffn_down")

    def act_bwd(ug, uu, dact, hg, hu, wg, wu, bg, bu):
        gate, upv = _conv_act(ug, uu, hg, hu, wg, wu, bg, bu)
        _, vjp_s = jax.vjp(_silu_gate, gate, upv)
        d_gate, d_upv = vjp_s(dact)
        def taps(dh, xv, h):
            return [jnp.sum(dh * _shift_down(xv, h, 2), axis=0, keepdims=True),
                    jnp.sum(dh * _shift_down(xv, h, 1), axis=0, keepdims=True),
                    jnp.sum(dh * xv, axis=0, keepdims=True), jnp.sum(dh, axis=0, keepdims=True)]
        return (d_gate, d_upv, *taps(d_gate, ug, hg), *taps(d_upv, uu, hu))

    dh_g, dh_u, *tapg = rw(
        "ffn_act_bwd", act_bwd, R=128, tiled=conv_tiled + [(d_act, DFF, 0)], prev=conv_tiled, full=conv_full,
        out_tiled=[(DFF, F32), (DFF, F32)], out_acc=[(1, DFF)] * 8)
    g_cw_g, g_cb_g = jnp.concatenate(tapg[0:3], axis=0), tapg[3]
    g_cw_u, g_cb_u = jnp.concatenate(tapg[4:7], axis=0), tapg[7]

    def conv_t(dg, du_, ng, nu, wg, wu):
        def ct(d, n, w):
            return w[2:3] * d + w[1:2] * _shift_up(d, n, 1) + w[0:1] * _shift_up(d, n, 2)
        return jnp.concatenate([ct(dg, ng, wg), ct(du_, nu, wu)], axis=1)

    (d_up,) = rw("conv_bwd", conv_t, R=128, tiled=[(dh_g, DFF, 0), (dh_u, DFF, 0)],
                 nxt=[(dh_g, DFF, 0), (dh_u, DFF, 0)], full=[conv_wg, conv_wu], out_tiled=[(2 * DFF, MXU_DTYPE)])
    d_h2 = _mm([d_up], [full["w_ffn_up"]], F32, "d_h2", bt=True)
    g_w_ffn_up = _mm_tn(h2, d_up, "g_ffn_up")

    def norm2_bwd(x1v, dh2, dx2v, mx, gt, sc, sh, g):
        _, vjp = jax.vjp(_norm_mod, x1v, g, sc, sh)
        dxn, dg, dsc, dsh = vjp(dh2)
        dx1 = dx2v + dxn
        return dx1, dx1 * gt, jnp.sum(dx1 * mx, axis=0, keepdims=True), dsc, dsh, dg

    dx1, d_mixed, d_gt1, d_sc2, d_sh2, g_norm2 = rw(
        "norm2_bwd", norm2_bwd, R=256, tiled=[(x1, D, 0), (d_h2, D, 0), (dx2, D, 0), (mixed, D, 0)],
        batch=[(mod, D, GT1), (mod, D, SC2), (mod, D, SH2)], full=[norm2_g],
        out_tiled=[(D, F32), (D, MXU_DTYPE)], out_batch=[D, D, D], out_acc=[(1, D)])

    d_mixed_in = _mm([d_mixed], [full["w_out"]], F32, "d_mixed_in", bt=True)
    g_w_out = _mm_tn(mixed_in, d_mixed, "g_w_out")

    def mix_bwd(ga, gb, ya, za, zb, dm):
        _, vjp = jax.vjp(_mix, ga, gb, ya, za, zb)
        dga, dgb, dya, dza, dzb = vjp(dm)
        return jnp.concatenate([dga, dgb], axis=1), dya, jnp.concatenate([dza, dzb], axis=1)

    d_gates, d_ya, d_z = rw("mix_bwd", mix_bwd, R=256, tiled=mix_tiled + [(d_mixed_in, D, 0)],
                            out_tiled=[(2 * D, MXU_DTYPE), (D, MXU_DTYPE), (2 * D, MXU_DTYPE)])
    d_o_rwkv = _mm([d_ya], [full["w_out_rwkv"]], F32, "d_o_rwkv", bt=True)
    g_w_out_rwkv = _mm_tn(o_rwkv, d_ya, "g_out_rwkv")
    d_s5o = _mm([d_z], [full["w_glu"]], F32, "d_s5o", bt=True)
    g_w_glu = _mm_tn(s5o, d_z, "g_glu")

    def s5_post_bwd(ys, uv, ds, dd):
        _, vjp = jax.vjp(_s5_post, ys, uv, dd)
        return vjp(ds)

    d_yssm, d_u_direct, g_s5_d = rw("s5_post_bwd", s5_post_bwd, R=256,
                                    tiled=[(y_ssm, SW, 0), (u, SW, 0), (d_s5o, SW, 0)], full=[s5_d],
                                    out_tiled=[(SW, F32), (SW, F32)], out_acc=[(1, SW)])
    d_u_ssm, d_wb, d_wc, d_ab = _s5_bwd(u, d_yssm, wb, wc, ab, s5_st, Bl, S)

    def diag_in(dw):
        t = dw.reshape(NSG, 8, SGC, 8, SP)
        return jnp.einsum("ab,sacbp->sapc", eye8, t).reshape(NG, SP * SGC)

    def diag_out(dw):
        t = dw.reshape(NSG, 8, SP, 8, SGC)
        return jnp.einsum("ab,sapbc->sacp", eye8, t).reshape(NG, SGC, SP)

    g_s5_c_re = diag_out(d_wc[:, :512])
    g_s5_c_im = -diag_out(d_wc[:, 512:])
    disc_cts = (d_ab[0].reshape(NG, SP), d_ab[1].reshape(NG, SP), diag_in(d_wb[:, :, :512]), diag_in(d_wb[:, :, 512:]))
    g_a_re, g_a_im, g_log_dt, g_b_re, g_b_im = _s5_disc_bwd(*s5_in, disc_cts)

    def post_bwd(yv, rv, kv, vv, gv, do, *pp):
        _, vjp = jax.vjp(lambda *a: _rwkv_post(*a, pp[3]), yv, rv, kv, vv, gv, *pp[:3])
        return vjp(do)

    dy_wkv, dr_b, dk_b, dv_b, dg_, g_ln_g, g_ln_b, g_r_k = rw(
        "rwkv_post_bwd", post_bwd, R=256,
        tiled=[(y_wkv, RW, 0), (r_, RW, 0), (k_, RW, 0), (v_, RW, 0), (g_, RW, 0), (d_o_rwkv, RW, 0)],
        full=post_params, out_tiled=[(RW, F32)] * 5, out_acc=[(1, RW)] * 3)
    dr3, dw3, dk3, dv3, da3, db3 = _wkv_bwd(r_, w_, k_, v_, a_, b_, dy_wkv, ck, Bl, S)

    def prep_bwd(pv, dr1, dr2, dwv, dk1, dk2, dv1, dv2, dav, dbv, dgv, ph, mu, *pp):
        prev = _shift_down(pv, ph, 1)
        ps = pv + (prev - pv) * mu
        _, vjp = jax.vjp(lambda *q: _rwkv_prep(*q, pp[7]), *_split_ps(ps), *pp[:7])
        grads = vjp((dr1 + dr2, dwv, dk1 + dk2, dv1 + dv2, dav, dbv, dgv))
        dps = jnp.concatenate(grads[:5], axis=1)
        return (dps,) + tuple(grads[5:]) + (jnp.sum(dps * (prev - pv), axis=0, keepdims=True),)

    prep_outs = rw(
        "rwkv_prep_bwd", prep_bwd, R=256,
        tiled=[(p, SHIFT, 0), (dr3, RW, 0), (dr_b, RW, 0), (dw3, RW, 0), (dk3, RW, 0), (dk_b, RW, 0),
               (dv3, RW, 0), (dv_b, RW, 0), (da3, RW, 0), (db3, RW, 0), (dg_, RW, 0)],
        prev=[(p, SHIFT, 0)], full=[mu_shift] + prep_params,
        out_tiled=[(SHIFT, F32)],
        out_acc=[(1, RW), (LW + LA, RW), (1, RW), (LW + LA, RW), (LG, RW), (1, RW), (1, RW), (1, SHIFT)])
    d_ps, g_w0, g_w_up_p, g_a0, g_a_up_p, g_g_up, g_k_k, g_k_a, g_mu = prep_outs

    def shift_bwd(dps, nx, mu):
        return dps * (1.0 - mu) + _shift_up(dps * mu, nx * mu, 1)

    (d_p,) = rw("shift_bwd", shift_bwd, R=256, tiled=[(d_ps, SHIFT, 0)], nxt=[(d_ps, SHIFT, 0)], full=[mu_shift],
                out_tiled=[(SHIFT, MXU_DTYPE)])
    (d_u,) = rw("d_u", lambda a1, a2: a1 + a2, R=256, tiled=[(d_u_direct, SW, 0), (d_u_ssm, SW, 0)],
                out_tiled=[(SW, MXU_DTYPE)])
    d_h1 = _mm([d_p, d_u, d_gates], [w_p, w_u, w_g], F32, "d_h1", bt=True)
    g_w_in = jnp.concatenate([_mm_tn(h1, d_p, "g_w_p"), _mm_tn(h1, d_u, "g_w_u"), _mm_tn(h1, d_gates, "g_w_g")], axis=1)

    def norm1_bwd(xv, dh1, dx1v, sc, sh, g):
        _, vjp = jax.vjp(_norm_mod, xv, g, sc, sh)
        dxn, dg, dsc, dsh = vjp(dh1)
        return dx1v + dxn, dsc, dsh, dg

    grad_x, d_sc1, d_sh1, g_norm1 = rw(
        "norm1_bwd", norm1_bwd, R=256, tiled=[(x2d, D, 0), (d_h1, D, 0), (dx1, D, 0)],
        batch=[(mod, D, SC1), (mod, D, SH1)], full=[norm1_g], out_tiled=[(D, F32)], out_batch=[D, D], out_acc=[(1, D)])

    dmod = jnp.concatenate([d_sh1, d_sc1, d_gt1, d_sh2, d_sc2, d_gt2], axis=2).reshape(Bl, 6 * D)
    dmod_all = _gather_two_level([], [dmod], "gather_dmod")[1][0].reshape(8 * Bl, 6 * D)
    dmod_cols = lax.dynamic_slice_in_dim(dmod_all, chip * ncol, ncol, 1)
    g_w_ada, g_b_ada = _ada_bwd(c_all, dmod_cols, dmod_all)

    small = {"norm1_g": g_norm1, "mu_shift": g_mu, "rwkv_w0": g_w0, "rwkv_a0": g_a0, "rwkv_k_k": g_k_k,
             "rwkv_k_a": g_k_a, "rwkv_r_k": g_r_k, "rwkv_ln_g": g_ln_g, "rwkv_ln_b": g_ln_b, "s5_a_re": g_a_re,
             "s5_a_im": g_a_im, "s5_log_dt": g_log_dt, "s5_b_re": g_b_re, "s5_b_im": g_b_im, "s5_c_re": g_s5_c_re,
             "s5_c_im": g_s5_c_im, "s5_d": g_s5_d, "norm2_g": g_norm2,
             "ffn_conv_b": jnp.concatenate([g_cb_g, g_cb_u], axis=1), "norm_f_g": g_norm_f}
    small_names = list(small)
    g_conv_w = jnp.concatenate([g_cw_g, g_cw_u], axis=1)
    shard_small = {"rwkv_w_up": g_w_up_p[:LW], "rwkv_a_up": g_a_up_p[LW:], "rwkv_g_up": g_g_up, "ffn_conv_w": g_conv_w}
    parts = [small[n] for n in small_names] + [_to_shards(shard_small[n], ax) for n, _, ax in BIG_SMALL]
    spack = _pack_rows(parts, F32, SUBLANES)
    s_all = _gather_two_level([], [spack], "gather_gsmall")[1][0]
    s_sum = _sum_slots(s_all.reshape((8,) + spack.shape), F32, "sum_gsmall").reshape(-1)
    grads = {}
    off = 0
    for n in small_names:
        grads[n] = s_sum[off:off + W[n].size].reshape(W[n].shape)
        off += W[n].size
    for n, shape, axis in BIG_SMALL:
        ss = _shard_shape(shape, axis)
        k4 = 4 * math.prod(ss)
        sh4 = s_sum[off:off + k4].reshape(4, math.prod(ss))
        grads[n] = lax.dynamic_index_in_dim(sh4, chip, 0, keepdims=False).reshape((1,) + ss)
        off += k4

    big_g = {"w_in": g_w_in, "w_out_rwkv": g_w_out_rwkv, "w_glu": g_w_glu, "w_out": g_w_out,
             "w_ffn_up": g_w_ffn_up, "w_ffn_down": g_w_ffn_down}
    gsh = [_to_shards(big_g[n], ax).astype(MXU_DTYPE) for n, _, ax in BIG]
    nbig = len(gsh)
    sds = jax.ShapeDtypeStruct
    moves = [(i, i, lambda ref, me, peer, r=g.shape[1]: ref.at[_chip_of(peer), _half(r, peer[2])],
              lambda ref, me: ref.at[2 * _chip_of(me) + me[2]]) for i, g in enumerate(gsh)]
    recv = _exchange("rs_all", ALL_FLIPS, gsh, [sds((8, g.shape[1] // 2, g.shape[2]), MXU_DTYPE) for g in gsh], moves)
    g_half = [_sum_slots(rv, F32, "rs_sum%d" % i) for i, rv in enumerate(recv)]
    moves = [(i, i, lambda ref, me, peer: ref, lambda ref, me, r=2 * g.shape[0]: ref.at[_half(r, me[2])])
             for i, g in enumerate(g_half)]
    g_full = _exchange("rs_share", PAIR_FLIPS, g_half, [sds((2 * g.shape[0], g.shape[1]), F32) for g in g_half], moves,
                       stage=[(g.shape, F32) for g in g_half])
    for (n, _, _), g in zip(BIG, g_full):
        grads[n] = g[None]
    grads["w_ada"] = g_w_ada[None]
    grads["b_ada"] = g_b_ada

    delta, new_m, new_v = {}, {}, {}
    to2 = lambda z: z.reshape(-1, z.shape[-1])
    for n in ["w_ada"] + [b[0] for b in BIG]:
        d_, m_, v2_ = _adamw(to2(W[n]), to2(grads[n]), to2(M[n]), to2(V[n]), "adamw_" + n)
        delta[n], new_m[n], new_v[n] = (z.reshape(W[n].shape) for z in (d_, m_, v2_))
    rest = [n for n in names if n not in delta]
    packs = [_pack_rows([src[n] for n in rest], F32, SUBLANES) for src in (W, grads, M, V)]
    d_, m_, v2_ = _adamw(*packs, "adamw_small")
    shapes = [W[n].shape for n in rest]
    for dst, z in ((delta, d_), (new_m, m_), (new_v, v2_)):
        for n, val in zip(rest, _unpack(z.reshape(-1), shapes)):
            dst[n] = val

    return (loss, grad_x.reshape(Bl, S, D), *[grads[n] for n in names], *[delta[n] for n in names],
            *[new_m[n] for n in names], *[new_v[n] for n in names])
```

```python
import functools
import math

import jax
import jax.numpy as jnp
from jax import lax
from jax.experimental import pallas as pl
from jax.experimental.pallas import tpu as pltpu

F32 = jnp.float32
BF16 = jnp.bfloat16
MXU_DTYPE = jnp.bfloat16
MESH_IDS = pl.DeviceIdType.MESH
HIGHEST = lax.Precision.HIGHEST

D = 1024
RW, NH, HD = 512, 8, 64
LW, LA, LG = 64, 64, 128
SW, SGC, NG, SP = 512, 16, 32, 64
NSG = 4
SHIFT = 3 * RW + LW + LA + LG
DFF = 2816
RMS_EPS, GN_EPS, L2_EPS = 1e-6, 64e-5, 1e-12
LR, B1, B2, ADAM_EPS, WD, STEP = 0.001, 0.9, 0.999, 1e-8, 0.01, 10
DECAY_SCALE = math.exp(-0.5)
GELU_C = math.sqrt(2.0 / math.pi)

VMEM_LIMIT = 52 * 1024 * 1024
SUBLANES, LANES = 8, 128
HALO = 16


def _pick(n, cap):
    if n <= cap:
        return n
    best = None
    for t in range(LANES, cap + 1, LANES):
        if n % t == 0:
            best = t
    assert best is not None, (n, cap)
    return best


def _params(sem=None, vmem=VMEM_LIMIT):
    return pltpu.CompilerParams(dimension_semantics=sem, vmem_limit_bytes=vmem)


def _chip_of(p):
    return 2 * p[0] + p[1]


def _me():
    return (lax.axis_index("x"), lax.axis_index("y"), lax.axis_index("c"))


def _half(rows, core):
    h = rows // 2
    return pl.ds(pl.multiple_of(core * h, 16 if h % 16 == 0 else SUBLANES), h)


def _exchange(name, flips, srcs, outs, moves):
    ns, no, nf, nm = len(srcs), len(outs), len(flips), len(moves)

    def body(*refs):
        src_refs, out_refs = refs[:ns], refs[ns:ns + no]
        send_sems, recv_sems, loc_sems = refs[ns + no:]
        me = _me()
        copies, locs = [], []
        for m, (si, oi, src_sel, dst_sel) in enumerate(moves):
            for k, f in enumerate(flips):
                peer = tuple(1 - v if b else v for v, b in zip(me, f))
                cp = pltpu.make_async_remote_copy(
                    src_ref=src_sel(src_refs[si], me, peer), dst_ref=dst_sel(out_refs[oi], me),
                    send_sem=send_sems.at[m * nf + k], recv_sem=recv_sems.at[m * nf + k],
                    device_id=peer, device_id_type=MESH_IDS)
                cp.start()
                copies.append(cp)
            loc = pltpu.make_async_copy(src_sel(src_refs[si], me, me), dst_sel(out_refs[oi], me), loc_sems.at[m])
            loc.start()
            locs.append(loc)
        for cp in copies:
            cp.wait_recv()
        for cp in copies:
            cp.wait_send()
        for loc in locs:
            loc.wait()

    return pl.pallas_call(
        body, name=name, out_shape=list(outs),
        in_specs=[pl.BlockSpec(memory_space=pl.ANY)] * ns,
        out_specs=[pl.BlockSpec(memory_space=pl.ANY)] * no,
        scratch_shapes=[pltpu.SemaphoreType.DMA((nm * nf,)), pltpu.SemaphoreType.DMA((nm * nf,)),
                        pltpu.SemaphoreType.DMA((nm,))],
    )(*srcs)


CHIP_FLIPS = ((1, 0, 0), (0, 1, 0), (1, 1, 0))
PAIR_FLIPS = ((0, 0, 1),)
ALL_FLIPS = CHIP_FLIPS + ((1, 0, 1), (0, 1, 1), (1, 1, 1)) + PAIR_FLIPS


def _gather_two_level(chip_arrs, dev_arrs, name):
    arrs = list(chip_arrs) + list(dev_arrs)
    n, nchip = len(arrs), len(chip_arrs)
    NS = 7

    def body(*refs):
        srcs, outs = refs[:n], refs[n:2 * n]
        send_sems, recv_sems, loc_sems = refs[2 * n:]
        x, y, c = _me()
        sib = (x, y, 1 - c)
        chips = [(1 - x, y), (x, 1 - y), (1 - x, 1 - y)]
        mine = 2 * x + y
        ids = [2 * cx + cy for cx, cy in chips]

        def part(i, slot, core):
            if i < nchip:
                return outs[i].at[slot, _half(arrs[i].shape[0], core)]
            return outs[i].at[slot, core]

        def rcopy(i, k, src, dst, to):
            return pltpu.make_async_remote_copy(src_ref=src, dst_ref=dst, send_sem=send_sems.at[i * NS + k],
                                                recv_sem=recv_sems.at[i * NS + k], device_id=to, device_id_type=MESH_IDS)

        started, locs = [], []
        for i in range(n):
            own = srcs[i].at[_half(arrs[i].shape[0], c)] if i < nchip else srcs[i]
            loc = pltpu.make_async_copy(srcs[i], outs[i].at[mine] if i < nchip else outs[i].at[mine, c], loc_sems.at[i])
            loc.start()
            locs.append(loc)
            for f, chip in enumerate(chips):
                cp = rcopy(i, f, own, part(i, mine, c), (*chip, c))
                cp.start()
                started.append(cp)
            if i >= nchip:
                cp = rcopy(i, 6, own, part(i, mine, c), sib)
                cp.start()
                started.append(cp)
        for i in range(n):
            for f in range(3):
                land = part(i, ids[f], c)
                rcopy(i, f, land, land, sib).wait_recv()
                fw = rcopy(i, 3 + f, land, land, sib)
                fw.start()
                started.append(fw)
        for i in range(n):
            for f in range(3):
                land = part(i, ids[f], 1 - c)
                rcopy(i, 3 + f, land, land, sib).wait_recv()
            if i >= nchip:
                land = part(i, mine, 1 - c)
                rcopy(i, 6, land, land, sib).wait_recv()
        for cp in started:
            cp.wait_send()
        for loc in locs:
            loc.wait()

    outs = [jax.ShapeDtypeStruct((4,) + a.shape, a.dtype) for a in chip_arrs]
    outs += [jax.ShapeDtypeStruct((4, 2) + a.shape, a.dtype) for a in dev_arrs]
    res = pl.pallas_call(
        body, name=name, out_shape=outs,
        in_specs=[pl.BlockSpec(memory_space=pl.ANY)] * n, out_specs=[pl.BlockSpec(memory_space=pl.ANY)] * n,
        scratch_shapes=[pltpu.SemaphoreType.DMA((n * NS,)), pltpu.SemaphoreType.DMA((n * NS,)),
                        pltpu.SemaphoreType.DMA((n,))],
    )(*arrs)
    return res[:nchip], res[nchip:]


def _mm(As, Bs, out_dtype, name, tm=512, cap=1408, bt=False):
    n = len(As)
    M, N = As[0].shape[0], Bs[0].shape[0 if bt else 1]
    tm = min(tm, M)
    tn = _pick(N, cap)
    dims = (((1,), (1,)), ((), ())) if bt else (((1,), (0,)), ((), ()))

    def body(*refs):
        o = refs[2 * n]
        acc = None
        for a, b in zip(refs[:n], refs[n:2 * n]):
            d = lax.dot_general(a[...].astype(MXU_DTYPE), b[...].astype(MXU_DTYPE), dims, preferred_element_type=F32)
            acc = d if acc is None else acc + d
        o[...] = acc.astype(o.dtype)

    in_specs = [pl.BlockSpec((tm, a.shape[1]), lambda i, j: (i, 0)) for a in As]
    if bt:
        in_specs += [pl.BlockSpec((tn, b.shape[1]), lambda i, j: (j, 0)) for b in Bs]
    else:
        in_specs += [pl.BlockSpec((b.shape[0], tn), lambda i, j: (0, j)) for b in Bs]
    return pl.pallas_call(
        body, name=name, grid=(M // tm, N // tn), in_specs=in_specs,
        out_specs=pl.BlockSpec((tm, tn), lambda i, j: (i, j)),
        out_shape=jax.ShapeDtypeStruct((M, N), out_dtype),
        compiler_params=_params(("parallel", "parallel")),
    )(*As, *Bs)


def _mm_tn(A, G, name, tt=1024, cap=1024):
    T, Ka = A.shape
    N = G.shape[1]
    tt = min(tt, T)
    tk = _pick(Ka, cap)
    tn = _pick(N, cap)

    def body(a, g, o):
        @pl.when(pl.program_id(2) == 0)
        def _():
            o[...] = jnp.zeros(o.shape, F32)
        o[...] += lax.dot_general(a[...].astype(MXU_DTYPE), g[...].astype(MXU_DTYPE),
                                  (((0,), (0,)), ((), ())), preferred_element_type=F32)

    return pl.pallas_call(
        body, name=name, grid=(Ka // tk, N // tn, T // tt),
        in_specs=[pl.BlockSpec((tt, tk), lambda i, j, t: (t, i)), pl.BlockSpec((tt, tn), lambda i, j, t: (t, j))],
        out_specs=pl.BlockSpec((tk, tn), lambda i, j, t: (i, j)),
        out_shape=jax.ShapeDtypeStruct((Ka, N), F32),
        compiler_params=_params(("parallel", "parallel", "arbitrary")),
    )(A, G)


def _rowwise(name, fn, *, Bl, S, R, tiled=(), prev=(), nxt=(), batch=(), full=(),
             out_tiled=(), out_batch=(), out_acc=()):
    R = min(R, S)
    nS = S // R
    T = Bl * S
    hb = R // HALO
    n_in = len(tiled) + len(prev) + len(nxt) + len(batch) + len(full)

    in_specs, args = [], []
    for a, wd, cb in tiled:
        in_specs.append(pl.BlockSpec((R, wd), lambda b, i, cb=cb: (b * nS + i, cb)))
        args.append(a)
    for a, wd, cb in prev:
        in_specs.append(pl.BlockSpec((HALO, wd), lambda b, i, cb=cb: (jnp.maximum((b * nS + i) * hb - 1, 0), cb)))
        args.append(a)
    for a, wd, cb in nxt:
        in_specs.append(pl.BlockSpec((HALO, wd), lambda b, i, cb=cb: (jnp.minimum((b * nS + i + 1) * hb, T // HALO - 1), cb)))
        args.append(a)
    for a, wd, cb in batch:
        in_specs.append(pl.BlockSpec((1, 1, wd), lambda b, i, cb=cb: (b, 0, cb)))
        args.append(a)
    for a in full:
        in_specs.append(pl.BlockSpec(a.shape, lambda b, i, nd=a.ndim: (0,) * nd))
        args.append(a)

    out_specs, out_shape = [], []
    for C, dt in out_tiled:
        out_specs.append(pl.BlockSpec((R, C), lambda b, i: (b * nS + i, 0)))
        out_shape.append(jax.ShapeDtypeStruct((T, C), dt))
    for C in out_batch:
        out_specs.append(pl.BlockSpec((1, 1, C), lambda b, i: (b, 0, 0)))
        out_shape.append(jax.ShapeDtypeStruct((Bl, 1, C), F32))
    for shp in out_acc:
        out_specs.append(pl.BlockSpec(shp, lambda b, i, nd=len(shp): (0,) * nd))
        out_shape.append(jax.ShapeDtypeStruct(shp, F32))

    nt, npv, nnx, nbt = len(tiled), len(prev), len(nxt), len(batch)

    def body(*refs):
        b, i = pl.program_id(0), pl.program_id(1)
        ins, outs = refs[:n_in], refs[n_in:]
        vals = [r[...] for r in ins[:nt]]
        vals += [jnp.where(i > 0, r[...], jnp.zeros(r.shape, r.dtype)) for r in ins[nt:nt + npv]]
        vals += [jnp.where(i < nS - 1, r[...], jnp.zeros(r.shape, r.dtype)) for r in ins[nt + npv:nt + npv + nnx]]
        vals += [r[0] for r in ins[nt + npv + nnx:nt + npv + nnx + nbt]]
        vals += [r[...] for r in ins[nt + npv + nnx + nbt:]]
        res = fn(*vals)
        if not isinstance(res, (tuple, list)):
            res = (res,)
        k = 0
        for _ in out_tiled:
            outs[k][...] = res[k].astype(outs[k].dtype)
            k += 1
        for _ in out_batch:
            o = outs[k]

            @pl.when(i == 0)
            def _(o=o):
                o[...] = jnp.zeros(o.shape, F32)
            o[0] += res[k]
            k += 1
        for _ in out_acc:
            o = outs[k]

            @pl.when((i == 0) & (b == 0))
            def _(o=o):
                o[...] = jnp.zeros(o.shape, F32)
            o[...] += res[k]
            k += 1

    out = pl.pallas_call(
        body, name=name, grid=(Bl, nS), in_specs=in_specs, out_specs=out_specs, out_shape=out_shape,
        compiler_params=_params(("arbitrary", "arbitrary")),
    )(*args)
    return out


def _shift_down(x, halo, k):
    row = lax.broadcasted_iota(jnp.int32, x.shape, 0)
    out = pltpu.roll(x, k, 0)
    for j in range(k):
        out = jnp.where(row == j, halo[HALO - k + j:HALO - k + j + 1, :], out)
    return out


def _shift_up(x, halo, k):
    n = x.shape[0]
    row = lax.broadcasted_iota(jnp.int32, x.shape, 0)
    out = pltpu.roll(x, n - k, 0)
    for j in range(k):
        out = jnp.where(row == n - k + j, halo[j:j + 1, :], out)
    return out


def _dotm(a, b):
    return jnp.dot(a.astype(MXU_DTYPE), b.astype(MXU_DTYPE), preferred_element_type=F32)


def _split_bf16(x):
    hi = x.astype(BF16)
    return hi, (x - hi.astype(F32)).astype(BF16)


def _headsum_2pass(x, hm):
    hi, lo = _split_bf16(x)
    hb = hm.astype(BF16)
    return jnp.dot(hi, hb, preferred_element_type=F32) + jnp.dot(lo, hb, preferred_element_type=F32)


@jax.custom_vjp
def _headsum(x, hm):
    return _headsum_2pass(x, hm)


_headsum.defvjp(lambda x, hm: (_headsum_2pass(x, hm), hm),
                lambda hm, g: (_headsum_2pass(g, hm), jnp.zeros_like(hm)))


def _sigmoid(x):
    return 1.0 / (1.0 + jnp.exp(-x))


def _rms(x, g):
    return x * lax.rsqrt(jnp.mean(x * x, axis=-1, keepdims=True) + RMS_EPS) * g


def _norm_mod(x, g, sc, sh):
    return _rms(x, g) * (1.0 + sc) + sh


def _split_ps(ps):
    return (ps[:, 0:RW], ps[:, RW:2 * RW], ps[:, 2 * RW:3 * RW], ps[:, 3 * RW:3 * RW + LW + LA],
            ps[:, 3 * RW + LW + LA:SHIFT])


def _rwkv_prep(r, k, v, wa, gd, w0, w_up_p, a0, a_up_p, g_up, k_k, k_a, hm):
    w_raw = w0 + _dotm(jnp.tanh(wa), w_up_p)
    decay = jnp.exp(-DECAY_SCALE * _sigmoid(w_raw))
    a = _sigmoid(a0 + _dotm(wa, a_up_p))
    g = _dotm(_sigmoid(gd), g_up)
    kk = k * k_k
    kk = kk * lax.rsqrt(_headsum(kk * kk, hm) + L2_EPS)
    k2 = k * (1.0 + (a - 1.0) * k_a)
    return r, decay, k2, v, -kk, kk * a, g


def _rwkv_post(y, r, k2, v, g, ln_g, ln_b, r_k, hm):
    mean = _headsum(y, hm) * (1.0 / HD)
    yc = y - mean
    var = _headsum(yc * yc, hm) * (1.0 / HD)
    yn = yc * lax.rsqrt(var + GN_EPS) * ln_g + ln_b
    bonus = _headsum(r * k2 * r_k, hm) * v
    return (yn + bonus) * g


def _gelu(x):
    return 0.5 * x * (1.0 + jnp.tanh(GELU_C * (x + 0.044715 * (x * x * x))))


def _s5_post(yssm, u, d):
    return _gelu(yssm + d * u)


def _mix(ga, gb, ya, za, zb):
    return _sigmoid(ga) * ya + _sigmoid(gb) * (za * _sigmoid(zb))


def _conv_act(up_g, up_u, hg, hu, w_g, w_u, b_g, b_u):
    up_g, up_u, hg, hu = (z.astype(F32) for z in (up_g, up_u, hg, hu))

    def conv(x, h, w, b):
        return b + w[0:1] * _shift_down(x, h, 2) + w[1:2] * _shift_down(x, h, 1) + w[2:3] * x
    gate = conv(up_g, hg, w_g, b_g)
    upv = conv(up_u, hu, w_u, b_u)
    return gate, upv


def _silu_gate(gate, upv):
    return gate * _sigmoid(gate) * upv


WKV_L = 64
_NT, _NN, _TN = ((1,), (1,)), ((1,), (0,)), ((0,), (0,))


def _dotw(x, y, dims):
    return lax.dot_general(x.astype(MXU_DTYPE), y.astype(MXU_DTYPE), (dims, ((), ())), preferred_element_type=F32)


def _dot3(x, y, dims):
    (xh, xl), (yh, yl) = _split_bf16(x), _split_bf16(y)
    d = lambda p, q: lax.dot_general(p, q, (dims, ((), ())), preferred_element_type=F32)
    return d(xh, yh) + d(xh, yl) + d(xl, yh)


@jax.custom_vjp
def _gram3(x, y):
    return _dot3(x, y, _NT)


_gram3.defvjp(lambda x, y: (_dot3(x, y, _NT), (x, y)),
              lambda res, g: (_dot3(g, res[1], _NN), _dot3(g, res[0], _TN)))


def _wkv_chunk(s0, r, w, k, v, a, b):
    y, s1 = _wkv_chunks((s0,), (r,), (w,), (k,), (v,), (a,), (b,))
    return y[0], s1[0]


def _wkv_chunks(s0, r, w, k, v, a, b):
    each = lambda f, *ls: tuple(f(*xs) for xs in zip(*ls))
    L = r[0].shape[0]
    n2 = 2 * L
    lane_head = lax.broadcasted_iota(jnp.int32, (2, 1, 2 * HD), 2) // HD
    head_mask = (lane_head == lax.broadcasted_iota(jnp.int32, (2, 1, 2 * HD), 0)).astype(F32)
    ri = lax.broadcasted_iota(jnp.int32, (n2, n2), 0)
    ci = lax.broadcasted_iota(jnp.int32, (n2, n2), 1)
    same = (ri // L) == (ci // L)
    strict = same & ((ci % L) < (ri % L))
    incl = same & ((ci % L) <= (ri % L))
    si = lax.broadcasted_iota(jnp.int32, (2 * HD, 2 * HD), 0) // HD
    sj = lax.broadcasted_iota(jnp.int32, (2 * HD, 2 * HD), 1) // HD
    tri = (lax.broadcasted_iota(jnp.int32, (L, L), 0) >= lax.broadcasted_iota(jnp.int32, (L, L), 1)).astype(F32)

    stack = lambda z: (z[None] * head_mask).reshape(n2, 2 * HD)
    dup = lambda z: jnp.broadcast_to(z[None], (2, L, 2 * HD)).reshape(n2, 2 * HD)
    gram = _gram3
    nt, nn, tn = (lambda x, y, d=d: _dotw(x, y, d) for d in (_NT, _NN, _TN))
    add = lambda x, y: x + y

    lw = each(jnp.log, w)
    cum = each(lambda z: jnp.dot(tri, z, preferred_element_type=F32, precision=HIGHEST), lw)
    tot = each(lambda z: jnp.sum(z, axis=0, keepdims=True), lw)
    a2 = each(lambda av, cv, lv: stack(av * jnp.exp(cv - lv)), a, cum, lw)
    r2 = each(lambda rv, cv: stack(rv * jnp.exp(cv)), r, cum)
    v2 = each(stack, v)
    b2 = each(lambda bv, cv: dup(bv * jnp.exp(-cv)), b, cum)
    k2 = each(lambda kv, cv: dup(kv * jnp.exp(-cv)), k, cum)
    n_ab = each(lambda x, y: jnp.where(strict, gram(x, y), 0.0), a2, b2)
    n_ak = each(lambda x, y: jnp.where(strict, gram(x, y), 0.0), a2, k2)
    m_rb = each(lambda x, y: jnp.where(incl, gram(x, y), 0.0), r2, b2)
    m_rk = each(lambda x, y: jnp.where(incl, gram(x, y), 0.0), r2, k2)
    u = each(add, each(nt, a2, s0), each(nn, n_ak, v2))
    q = n_ab
    steps = L.bit_length() - 1
    for i in range(steps):
        u = each(add, u, each(nn, q, u))
        if i < steps - 1:
            q = each(nn, q, q)
    y2 = each(lambda x, y, z: x + y + z, each(nt, r2, s0), each(nn, m_rb, u), each(nn, m_rk, v2))
    y = each(lambda z: jnp.sum(z.reshape(2, L, 2 * HD), axis=0), y2)
    b3 = each(lambda bv, tv, cv: dup(bv * jnp.exp(tv - cv)), b, tot, cum)
    k3 = each(lambda kv, tv, cv: dup(kv * jnp.exp(tv - cv)), k, tot, cum)
    upd = each(add, each(tn, u, b3), each(tn, v2, k3))
    s1 = each(lambda sv, tv, uv: sv * jnp.exp(tv) + jnp.where(si == sj, uv, 0.0), s0, tot, upd)
    return y, s1


NPAIR = NH // 2


def _wkv_nb(Bl):
    return 2 if Bl % 2 == 0 else 1


def _wkv_fwd(r, w, k, v, a, b, Bl, S):
    L = WKV_L
    nC = S // L
    nb = _wkv_nb(Bl)
    chains = [(bi, p, slice(p * 2 * HD, (p + 1) * 2 * HD)) for bi in range(nb) for p in range(NPAIR)]

    def body(r_ref, w_ref, k_ref, v_ref, a_ref, b_ref, y_ref, ck_ref, s_ref):
        @pl.when(pl.program_id(1) == 0)
        def _():
            s_ref[...] = jnp.zeros(s_ref.shape, F32)
        s0 = tuple(s_ref[bi, p] for bi, p, _ in chains)
        ops = [tuple(z[bi, :, cs] for bi, _, cs in chains) for z in (r_ref, w_ref, k_ref, v_ref, a_ref, b_ref)]
        y, s1 = _wkv_chunks(s0, *ops)
        for i, (bi, p, cs) in enumerate(chains):
            ck_ref[bi, 0, p] = s0[i]
            y_ref[bi, :, cs] = y[i]
            s_ref[bi, p] = s1[i]

    to3 = lambda z: z.reshape(Bl, S, RW)
    row_spec = pl.BlockSpec((nb, L, RW), lambda g, c: (g, c, 0))
    y, ck = pl.pallas_call(
        body, name="wkv_fwd", grid=(Bl // nb, nC), in_specs=[row_spec] * 6,
        out_specs=[row_spec, pl.BlockSpec((nb, 1, NPAIR, 2 * HD, 2 * HD), lambda g, c: (g, c, 0, 0, 0))],
        out_shape=[jax.ShapeDtypeStruct((Bl, S, RW), F32), jax.ShapeDtypeStruct((Bl, nC, NPAIR, 2 * HD, 2 * HD), F32)],
        scratch_shapes=[pltpu.VMEM((nb, NPAIR, 2 * HD, 2 * HD), F32)],
        compiler_params=_params(("arbitrary", "arbitrary")),
    )(*(to3(z) for z in (r, w, k, v, a, b)))
    return y.reshape(Bl * S, RW), ck


def _wkv_bwd(r, w, k, v, a, b, dy, ck, Bl, S):
    L = WKV_L
    nC = S // L
    nb = _wkv_nb(Bl)
    chains = [(bi, p, slice(p * 2 * HD, (p + 1) * 2 * HD)) for bi in range(nb) for p in range(NPAIR)]

    def body(r_ref, w_ref, k_ref, v_ref, a_ref, b_ref, dy_ref, ck_ref,
             dr_ref, dw_ref, dk_ref, dv_ref, da_ref, db_ref, ds_ref):
        @pl.when(pl.program_id(1) == 0)
        def _():
            ds_ref[...] = jnp.zeros(ds_ref.shape, F32)
        s0 = tuple(ck_ref[bi, 0, p] for bi, p, _ in chains)
        ops = [tuple(z[bi, :, cs] for bi, _, cs in chains) for z in (r_ref, w_ref, k_ref, v_ref, a_ref, b_ref)]
        cts = (tuple(dy_ref[bi, :, cs] for bi, _, cs in chains), tuple(ds_ref[bi, p] for bi, p, _ in chains))
        ds0, *grads = jax.vjp(_wkv_chunks, s0, *ops)[1](cts)
        for i, (bi, p, cs) in enumerate(chains):
            ds_ref[bi, p] = ds0[i]
            for o, g in zip((dr_ref, dw_ref, dk_ref, dv_ref, da_ref, db_ref), grads):
                o[bi, :, cs] = g[i]

    to3 = lambda z: z.reshape(Bl, S, RW)
    row_spec = pl.BlockSpec((nb, L, RW), lambda g, c: (g, nC - 1 - c, 0))
    rows = jax.ShapeDtypeStruct((Bl, S, RW), F32)
    outs = pl.pallas_call(
        body, name="wkv_bwd", grid=(Bl // nb, nC),
        in_specs=[row_spec] * 7 + [pl.BlockSpec((nb, 1, NPAIR, 2 * HD, 2 * HD), lambda g, c: (g, nC - 1 - c, 0, 0, 0))],
        out_specs=[row_spec] * 6, out_shape=[rows] * 6,
        scratch_shapes=[pltpu.VMEM((nb, NPAIR, 2 * HD, 2 * HD), F32)],
        compiler_params=_params(("arbitrary", "arbitrary")),
    )(*(to3(z) for z in (r, w, k, v, a, b, dy)), ck)
    return [o.reshape(Bl * S, RW) for o in outs]


NST = NG * SP


def _cmul(ar, ai, br, bi):
    return ar * br - ai * bi, ar * bi + ai * br


def _s5_tiles(are, aim, reverse):
    if reverse:
        aim = -aim
    row = lax.broadcasted_iota(jnp.int32, (SUBLANES, NST), 0)
    pw = [(are, aim)]
    for _ in range(SUBLANES - 1):
        pw.append(_cmul(pw[-1][0], pw[-1][1], are, aim))
    bc = lambda z: jnp.broadcast_to(z, (SUBLANES, NST))
    ms = []
    for kk in (1, 2, 4):
        cond = (row < SUBLANES - kk) if reverse else (row >= kk)
        ms.append((jnp.where(cond, bc(pw[kk - 1][0]), 0.0), jnp.where(cond, bc(pw[kk - 1][1]), 0.0)))
    pr = jnp.zeros((SUBLANES, NST), F32)
    pi = jnp.zeros((SUBLANES, NST), F32)
    for i in range(SUBLANES):
        n = SUBLANES - i if reverse else i + 1
        pr = jnp.where(row == i, bc(pw[n - 1][0]), pr)
        pi = jnp.where(row == i, bc(pw[n - 1][1]), pi)
    return ms, (pr, pi)


def _s5_block(re, im, ms, pc, cre, cim, sg, reverse):
    ln = slice(sg * 512, (sg + 1) * 512)
    for (mr, mi), kk in zip(ms, (1, 2, 4)):
        sh = SUBLANES - kk if reverse else kk
        sre, sim = pltpu.roll(re, sh, 0), pltpu.roll(im, sh, 0)
        tr, ti = _cmul(mr[:, ln], mi[:, ln], sre, sim)
        re, im = re + tr, im + ti
    tr, ti = _cmul(pc[0][:, ln], pc[1][:, ln], cre[:, ln], cim[:, ln])
    return re + tr, im + ti


def _s5_scan(X_ref, n_rows, ms, pc, cre, cim, reverse, visit=None, acc0=None):
    nblk = n_rows // SUBLANES

    def it(i, carry):
        cre, cim, acc = carry
        j = nblk - 1 - i if reverse else i
        rows = pl.ds(pl.multiple_of(j * SUBLANES, SUBLANES), SUBLANES)
        edge = 0 if reverse else SUBLANES - 1
        blocks, ncre, ncim = [], [], []
        for sg in range(NSG):
            lr = slice(sg * 1024, sg * 1024 + 512)
            li = slice(sg * 1024 + 512, (sg + 1) * 1024)
            re, im = _s5_block(X_ref[rows, lr], X_ref[rows, li], ms, pc, cre, cim, sg, reverse)
            X_ref[rows, lr] = re
            X_ref[rows, li] = im
            blocks.append((re, im))
            ncre.append(re[edge:edge + 1])
            ncim.append(im[edge:edge + 1])
        if visit is not None:
            acc = visit(j, blocks, acc)
        return jnp.concatenate(ncre, axis=1), jnp.concatenate(ncim, axis=1), acc

    return lax.fori_loop(0, nblk, it, (cre, cim, acc0 if acc0 is not None else 0))


def _s5_fwd(u, wb, wc, ab, Bl, S, R=256):
    R = min(R, S)
    nC = S // R

    def body(u_ref, wb_ref, wc_ref, ab_ref, y_ref, st_ref, X_ref, car_ref):
        @pl.when(pl.program_id(1) == 0)
        def _():
            car_ref[...] = jnp.zeros(car_ref.shape, F32)
        st_ref[0, 0] = car_ref[...]
        ms, pc = _s5_tiles(ab_ref[0:1], ab_ref[1:2], False)
        for sg in range(NSG):
            X_ref[:, sg * 1024:(sg + 1) * 1024] = _dotm(u_ref[:, sg * 128:(sg + 1) * 128], wb_ref[sg])
        cre, cim, _ = _s5_scan(X_ref, R, ms, pc, car_ref[0:1], car_ref[1:2], False)
        car_ref[0:1] = cre
        car_ref[1:2] = cim
        for sg in range(NSG):
            y_ref[:, sg * 128:(sg + 1) * 128] = _dotm(X_ref[:, sg * 1024:(sg + 1) * 1024], wc_ref[sg])

    return pl.pallas_call(
        body, name="s5_fwd", grid=(Bl, nC),
        in_specs=[pl.BlockSpec((R, SW), lambda b, c: (b * nC + c, 0)),
                  pl.BlockSpec(wb.shape, lambda b, c: (0, 0, 0)), pl.BlockSpec(wc.shape, lambda b, c: (0, 0, 0)),
                  pl.BlockSpec(ab.shape, lambda b, c: (0, 0))],
        out_specs=[pl.BlockSpec((R, SW), lambda b, c: (b * nC + c, 0)),
                   pl.BlockSpec((1, 1, 2, NST), lambda b, c: (b, c, 0, 0)),
                   pl.BlockSpec((R, 2 * NST), lambda b, c: (b * nC + c, 0))],
        out_shape=[jax.ShapeDtypeStruct((Bl * S, SW), F32), jax.ShapeDtypeStruct((Bl, nC, 2, NST), F32),
                   jax.ShapeDtypeStruct((Bl * S, 2 * NST), F32)],
        scratch_shapes=[pltpu.VMEM((2, NST), F32)],
        compiler_params=_params(("arbitrary", "arbitrary")),
    )(u, wb, wc, ab)


def _s5_bwd(u, dy, wb, wc, ab, st, xs, Bl, S, R=256):
    R = min(R, S)
    nC = S // R

    def body(u_ref, dy_ref, wb_ref, wc_ref, ab_ref, st_ref, X_ref, du_ref, dwb_ref, dwc_ref, dab_ref,
             G_ref, car_ref):
        first = (pl.program_id(0) == 0) & (pl.program_id(1) == 0)

        @pl.when(first)
        def _():
            dwb_ref[...] = jnp.zeros(dwb_ref.shape, F32)
            dwc_ref[...] = jnp.zeros(dwc_ref.shape, F32)
            dab_ref[...] = jnp.zeros(dab_ref.shape, F32)

        @pl.when(pl.program_id(1) == 0)
        def _():
            car_ref[...] = jnp.zeros(car_ref.shape, F32)

        are, aim = ab_ref[0:1], ab_ref[1:2]
        dyv = dy_ref[...].astype(MXU_DTYPE)
        for sg in range(NSG):
            G_ref[:, sg * 1024:(sg + 1) * 1024] = lax.dot_general(
                dyv[:, sg * 128:(sg + 1) * 128], wc_ref[sg].astype(MXU_DTYPE), (((1,), (1,)), ((), ())),
                preferred_element_type=F32)
        rms_, rpc = _s5_tiles(are, aim, True)
        row = lax.broadcasted_iota(jnp.int32, (SUBLANES, 512), 0)

        def visit(j, blocks, acc):
            before = pl.multiple_of(jnp.maximum(j - 1, 0) * SUBLANES, SUBLANES)
            prow = X_ref[pl.ds(before, SUBLANES), :][SUBLANES - 1:SUBLANES]
            rows = pl.ds(pl.multiple_of(j * SUBLANES, SUBLANES), SUBLANES)
            are_acc, aim_acc = [], []
            for sg in range(NSG):
                lr = slice(sg * 1024, sg * 1024 + 512)
                li = slice(sg * 1024 + 512, (sg + 1) * 1024)
                ln = slice(sg * 512, (sg + 1) * 512)
                pre = jnp.where(j > 0, prow[:, lr], st_ref[0, 0, 0:1, ln])
                pim = jnp.where(j > 0, prow[:, li], st_ref[0, 0, 1:2, ln])
                xre = jnp.where(row == 0, pre, pltpu.roll(X_ref[rows, lr], 1, 0))
                xim = jnp.where(row == 0, pim, pltpu.roll(X_ref[rows, li], 1, 0))
                dre, dim = blocks[sg]
                are_acc.append(dre * xre + dim * xim)
                aim_acc.append(dim * xre - dre * xim)
            return acc[0] + jnp.concatenate(are_acc, axis=1), acc[1] + jnp.concatenate(aim_acc, axis=1)

        zero = jnp.zeros((SUBLANES, NST), F32)
        cre, cim, acc = _s5_scan(G_ref, R, rms_, rpc, car_ref[0:1], car_ref[1:2], True, visit, (zero, zero))
        car_ref[0:1] = cre
        car_ref[1:2] = cim
        dab_ref[0:1] += jnp.sum(acc[0], axis=0, keepdims=True)
        dab_ref[1:2] += jnp.sum(acc[1], axis=0, keepdims=True)
        uv = u_ref[...].astype(MXU_DTYPE)
        for sg in range(NSG):
            cs = slice(sg * 1024, (sg + 1) * 1024)
            us = slice(sg * 128, (sg + 1) * 128)
            gx = G_ref[:, cs].astype(MXU_DTYPE)
            dwb_ref[sg] += lax.dot_general(uv[:, us], gx, (((0,), (0,)), ((), ())), preferred_element_type=F32)
            dwc_ref[sg] += lax.dot_general(X_ref[:, cs].astype(MXU_DTYPE), dyv[:, us], (((0,), (0,)), ((), ())),
                                           preferred_element_type=F32)
            du_ref[:, us] = lax.dot_general(gx, wb_ref[sg].astype(MXU_DTYPE), (((1,), (1,)), ((), ())),
                                            preferred_element_type=F32)

    rmap = lambda b, c: (b * nC + nC - 1 - c, 0)
    return pl.pallas_call(
        body, name="s5_bwd", grid=(Bl, nC),
        in_specs=[pl.BlockSpec((R, SW), rmap), pl.BlockSpec((R, SW), rmap),
                  pl.BlockSpec(wb.shape, lambda b, c: (0, 0, 0)), pl.BlockSpec(wc.shape, lambda b, c: (0, 0, 0)),
                  pl.BlockSpec(ab.shape, lambda b, c: (0, 0)),
                  pl.BlockSpec((1, 1, 2, NST), lambda b, c: (b, nC - 1 - c, 0, 0)),
                  pl.BlockSpec((R, 2 * NST), rmap)],
        out_specs=[pl.BlockSpec((R, SW), rmap), pl.BlockSpec(wb.shape, lambda b, c: (0, 0, 0)),
                   pl.BlockSpec(wc.shape, lambda b, c: (0, 0, 0)), pl.BlockSpec((2, NST), lambda b, c: (0, 0))],
        out_shape=[jax.ShapeDtypeStruct((Bl * S, SW), F32), jax.ShapeDtypeStruct(wb.shape, F32),
                   jax.ShapeDtypeStruct(wc.shape, F32), jax.ShapeDtypeStruct((2, NST), F32)],
        scratch_shapes=[pltpu.VMEM((R, 2 * NST), F32), pltpu.VMEM((2, NST), F32)],
        compiler_params=_params(("arbitrary", "arbitrary")),
    )(u, dy, wb, wc, ab, st, xs)


def _s5_disc_math(a_re, a_im, log_dt, b_re, b_im, expand):
    dt = jnp.exp(log_dt)
    z_re, z_im = a_re * dt, a_im * dt
    mag = jnp.exp(z_re)
    ab_re, ab_im = mag * jnp.cos(z_im), mag * jnp.sin(z_im)
    den = a_re * a_re + a_im * a_im
    q_re = ((ab_re - 1.0) * a_re + ab_im * a_im) / den
    q_im = (ab_im * a_re - (ab_re - 1.0) * a_im) / den
    qe_re = jnp.dot(q_re, expand, preferred_element_type=F32, precision=HIGHEST)
    qe_im = jnp.dot(q_im, expand, preferred_element_type=F32, precision=HIGHEST)
    return ab_re, ab_im, qe_re * b_re - qe_im * b_im, qe_re * b_im + qe_im * b_re


def _whole(shape):
    return pl.BlockSpec(shape, lambda nd=len(shape): (0,) * nd)


def _s5_disc(a_re, a_im, log_dt, b_re, b_im, expand):
    def body(a, b, c, d, e, f, o0, o1, o2, o3):
        res = _s5_disc_math(a[...], b[...], c[...], d[...], e[...], f[...])
        for o, v in zip((o0, o1, o2, o3), res):
            o[...] = v
    ins = (a_re, a_im, log_dt, b_re, b_im, expand)
    outs = [jax.ShapeDtypeStruct(a_re.shape, F32)] * 2 + [jax.ShapeDtypeStruct(b_re.shape, F32)] * 2
    return pl.pallas_call(body, name="s5_disc", in_specs=[_whole(x.shape) for x in ins],
                          out_specs=[_whole(o.shape) for o in outs], out_shape=outs)(*ins)


def _s5_disc_bwd(a_re, a_im, log_dt, b_re, b_im, expand, cts):
    def body(a, b, c, d, e, f, g0, g1, g2, g3, o0, o1, o2, o3, o4):
        fn = lambda *p: _s5_disc_math(*p, f[...])
        _, vjp = jax.vjp(fn, a[...], b[...], c[...], d[...], e[...])
        for o, v in zip((o0, o1, o2, o3, o4), vjp((g0[...], g1[...], g2[...], g3[...]))):
            o[...] = v
    ins = (a_re, a_im, log_dt, b_re, b_im, expand) + tuple(cts)
    outs = [jax.ShapeDtypeStruct(x.shape, F32) for x in (a_re, a_im, log_dt, b_re, b_im)]
    return pl.pallas_call(body, name="s5_disc_bwd", in_specs=[_whole(x.shape) for x in ins],
                          out_specs=[_whole(o.shape) for o in outs], out_shape=outs)(*ins)


def _ada_fwd(c_all, w_shard, b_shard):
    def body(c_ref, w_ref, b_ref, o_ref):
        cv = c_ref[...]
        o_ref[...] = _dotm(cv * _sigmoid(cv), w_ref[...]) + b_ref[...]
    n = w_shard.shape[1]
    return pl.pallas_call(
        body, name="ada_fwd", in_specs=[_whole(c_all.shape), _whole(w_shard.shape), _whole(b_shard.shape)],
        out_specs=_whole((c_all.shape[0], n)), out_shape=jax.ShapeDtypeStruct((c_all.shape[0], n), F32),
        compiler_params=_params(),
    )(c_all, w_shard, b_shard)


def _ada_bwd(c_all, dmod_cols, dmod_all):
    def body(c_ref, dc_ref, da_ref, gw_ref, gb_ref):
        cv = c_ref[...]
        gw_ref[...] = lax.dot_general((cv * _sigmoid(cv)).astype(MXU_DTYPE), dc_ref[...].astype(MXU_DTYPE),
                                      (((0,), (0,)), ((), ())), preferred_element_type=F32)
        gb_ref[...] = jnp.sum(da_ref[...], axis=0, keepdims=True)
    n = dmod_cols.shape[1]
    return pl.pallas_call(
        body, name="ada_bwd", in_specs=[_whole(c_all.shape), _whole(dmod_cols.shape), _whole(dmod_all.shape)],
        out_specs=[_whole((D, n)), _whole((1, dmod_all.shape[1]))],
        out_shape=[jax.ShapeDtypeStruct((D, n), F32), jax.ShapeDtypeStruct((1, dmod_all.shape[1]), F32)],
        compiler_params=_params(),
    )(c_all, dmod_cols, dmod_all)


def _rows_block(n_rows, cap=512):
    if n_rows <= cap:
        return n_rows
    for t in range(cap - cap % SUBLANES, 0, -SUBLANES):
        if n_rows % t == 0:
            return t
    return n_rows


def _adamw(w, g, m, v, name):
    rows, cols = w.shape
    tr = _rows_block(rows, max(SUBLANES, (1 << 19) // max(cols, 1) // SUBLANES * SUBLANES))

    def body(w_ref, g_ref, m_ref, v_ref, d_ref, nm_ref, nv_ref):
        gv = g_ref[...]
        nm = B1 * m_ref[...] + (1.0 - B1) * gv
        nv = B2 * v_ref[...] + (1.0 - B2) * (gv * gv)
        m_hat = nm / (1.0 - B1 ** STEP)
        v_hat = nv / (1.0 - B2 ** STEP)
        d_ref[...] = -LR * (m_hat / (jnp.sqrt(v_hat) + ADAM_EPS) + WD * w_ref[...])
        nm_ref[...] = nm
        nv_ref[...] = nv

    spec = pl.BlockSpec((tr, cols), lambda i: (i, 0))
    sd = jax.ShapeDtypeStruct((rows, cols), F32)
    return pl.pallas_call(body, name=name, grid=(rows // tr,), in_specs=[spec] * 4, out_specs=[spec] * 3,
                          out_shape=[sd] * 3, compiler_params=_params(("parallel",)))(w, g, m, v)


def _sum_slots(x, out_dtype, name):
    n, rows, cols = x.shape
    tr = _rows_block(rows)

    def body(x_ref, o_ref):
        acc = x_ref[0].astype(F32)
        for j in range(1, n):
            acc = acc + x_ref[j].astype(F32)
        o_ref[...] = acc.astype(o_ref.dtype)

    return pl.pallas_call(
        body, name=name, grid=(rows // tr,), in_specs=[pl.BlockSpec((n, tr, cols), lambda i: (0, i, 0))],
        out_specs=pl.BlockSpec((tr, cols), lambda i: (i, 0)), out_shape=jax.ShapeDtypeStruct((rows, cols), out_dtype),
        compiler_params=_params(("parallel",)))(x)


PACK_COLS = 1024


def _pack_rows(parts, dtype, row_mult):
    flat = jnp.concatenate([p.reshape(-1).astype(dtype) for p in parts])
    per = PACK_COLS * row_mult
    n = -(-flat.shape[0] // per) * per
    flat = jnp.pad(flat, (0, n - flat.shape[0]))
    return flat.reshape(n // PACK_COLS, PACK_COLS)


def _unpack(flat, shapes):
    out, off = [], 0
    for s in shapes:
        n = math.prod(s)
        out.append(flat[off:off + n].reshape(s))
        off += n
    return out


BIG = (("w_in", (D, SHIFT + SW + 2 * D), 1), ("w_out_rwkv", (RW, D), 1), ("w_glu", (SW, 2 * D), 1),
       ("w_out", (D, D), 0), ("w_ffn_up", (D, 2 * DFF), 1), ("w_ffn_down", (DFF, D), 0))
BIG_SMALL = (("rwkv_w_up", (LW, RW), 1), ("rwkv_a_up", (LA, RW), 1), ("rwkv_g_up", (LG, RW), 1),
             ("ffn_conv_w", (3, 2 * DFF), 1))


def _shard_shape(shape, axis):
    return (shape[0] // 4, shape[1]) if axis == 0 else (shape[0], shape[1] // 4)


def _to_shards(g, axis):
    r, C = g.shape
    return g.reshape(4, r // 4, C) if axis == 0 else g.reshape(r, 4, C // 4).transpose(1, 0, 2)


def _from_shards(x, axis):
    _, r, C = x.shape
    return x.reshape(4 * r, C) if axis == 0 else x.transpose(1, 0, 2).reshape(r, 4 * C)


def kernel(x, c, w_ada, b_ada, norm1_g, w_in, mu_shift, rwkv_w0, rwkv_w_up, rwkv_a0, rwkv_a_up, rwkv_g_up, rwkv_k_k, rwkv_k_a, rwkv_r_k, rwkv_ln_g, rwkv_ln_b, w_out_rwkv, s5_a_re, s5_a_im, s5_log_dt, s5_b_re, s5_b_im, s5_c_re, s5_c_im, s5_d, w_glu, w_out, norm2_g, w_ffn_up, ffn_conv_w, ffn_conv_b, w_ffn_down, norm_f_g, loss_target, m_w_ada, m_b_ada, m_norm1_g, m_w_in, m_mu_shift, m_rwkv_w0, m_rwkv_w_up, m_rwkv_a0, m_rwkv_a_up, m_rwkv_g_up, m_rwkv_k_k, m_rwkv_k_a, m_rwkv_r_k, m_rwkv_ln_g, m_rwkv_ln_b, m_w_out_rwkv, m_s5_a_re, m_s5_a_im, m_s5_log_dt, m_s5_b_re, m_s5_b_im, m_s5_c_re, m_s5_c_im, m_s5_d, m_w_glu, m_w_out, m_norm2_g, m_w_ffn_up, m_ffn_conv_w, m_ffn_conv_b, m_w_ffn_down, m_norm_f_g, v_w_ada, v_b_ada, v_norm1_g, v_w_in, v_mu_shift, v_rwkv_w0, v_rwkv_w_up, v_rwkv_a0, v_rwkv_a_up, v_rwkv_g_up, v_rwkv_k_k, v_rwkv_k_a, v_rwkv_r_k, v_rwkv_ln_g, v_rwkv_ln_b, v_w_out_rwkv, v_s5_a_re, v_s5_a_im, v_s5_log_dt, v_s5_b_re, v_s5_b_im, v_s5_c_re, v_s5_c_im, v_s5_d, v_w_glu, v_w_out, v_norm2_g, v_w_ffn_up, v_ffn_conv_w, v_ffn_conv_b, v_w_ffn_down, v_norm_f_g):
    names = ["w_ada", "b_ada", "norm1_g", "w_in", "mu_shift", "rwkv_w0", "rwkv_w_up", "rwkv_a0", "rwkv_a_up",
             "rwkv_g_up", "rwkv_k_k", "rwkv_k_a", "rwkv_r_k", "rwkv_ln_g", "rwkv_ln_b", "w_out_rwkv", "s5_a_re",
             "s5_a_im", "s5_log_dt", "s5_b_re", "s5_b_im", "s5_c_re", "s5_c_im", "s5_d", "w_glu", "w_out", "norm2_g",
             "w_ffn_up", "ffn_conv_w", "ffn_conv_b", "w_ffn_down", "norm_f_g"]
    env = dict(locals())
    W = {n: env[n] for n in names}
    M = {n: env["m_" + n] for n in names}
    V = {n: env["v_" + n] for n in names}

    Bl, S, _ = x.shape
    T = Bl * S
    ix, iy, ic = lax.axis_index("x"), lax.axis_index("y"), lax.axis_index("c")
    chip = 2 * ix + iy
    dev = 2 * chip + ic
    rw = functools.partial(_rowwise, Bl=Bl, S=S)

    chip_arrs = [W[n][0].astype(MXU_DTYPE) for n, _, _ in BIG] + [W[n][0] for n, _, _ in BIG_SMALL[:3]]
    got_chip, got_dev = _gather_two_level(chip_arrs, [W["ffn_conv_w"][0], c], "gather_w")
    full = {n: _from_shards(g, axis) for (n, _, axis), g in zip(BIG + BIG_SMALL[:3], got_chip)}
    full["ffn_conv_w"] = _from_shards(got_dev[0][:, 0], 1)
    c_all = got_dev[1].reshape(8 * Bl, D)
    w_p, w_u, w_g = full["w_in"][:, :SHIFT], full["w_in"][:, SHIFT:SHIFT + SW], full["w_in"][:, SHIFT + SW:]
    zeros_l = jnp.zeros((LW, RW), F32)
    w_up_p = jnp.concatenate([full["rwkv_w_up"], zeros_l], axis=0)
    a_up_p = jnp.concatenate([zeros_l, full["rwkv_a_up"]], axis=0)
    g_up = full["rwkv_g_up"]
    conv_w = full["ffn_conv_w"]
    conv_wg, conv_wu = conv_w[:, :DFF], conv_w[:, DFF:]
    conv_bg, conv_bu = ffn_conv_b[:, :DFF], ffn_conv_b[:, DFF:]
    hm = jnp.kron(jnp.eye(NH, dtype=F32), jnp.ones((HD, HD), F32))

    ncol = 6 * D // 4
    b_ada_cols = lax.dynamic_slice_in_dim(b_ada, chip * ncol, ncol, 1)
    mod_part = _ada_fwd(c_all, w_ada[0], b_ada_cols)
    mod4 = _gather_two_level([], [mod_part], "gather_mod")[1][0][:, 0]
    mod = lax.dynamic_slice_in_dim(mod4, dev * Bl, Bl, 1).transpose(1, 0, 2).reshape(Bl, 1, 6 * D)
    SH1, SC1, GT1, SH2, SC2, GT2 = range(6)

    x2d = x.reshape(T, D)
    tgt = loss_target.reshape(T, D)

    (h1,) = rw("norm1", lambda xv, sc, sh, g: _norm_mod(xv, g, sc, sh), R=256, tiled=[(x2d, D, 0)],
               batch=[(mod, D, SC1), (mod, D, SH1)], full=[norm1_g], out_tiled=[(D, MXU_DTYPE)])
    p = _mm([h1], [w_p], F32, "proj_p")
    u = _mm([h1], [w_u], F32, "proj_u")
    gates = _mm([h1], [w_g], F32, "proj_g")

    prep_params = [rwkv_w0, w_up_p, rwkv_a0, a_up_p, g_up, rwkv_k_k, rwkv_k_a, hm]

    def prep_fwd(pv, ph, mu, *pp):
        ps = pv + (_shift_down(pv, ph, 1) - pv) * mu
        return _rwkv_prep(*_split_ps(ps), *pp)

    r_, w_, k_, v_, a_, b_, g_ = rw("rwkv_prep", prep_fwd, R=256, tiled=[(p, SHIFT, 0)], prev=[(p, SHIFT, 0)],
                                    full=[mu_shift] + prep_params, out_tiled=[(RW, F32)] * 7)
    y_wkv, ck = _wkv_fwd(r_, w_, k_, v_, a_, b_, Bl, S)
    r_k_row = rwkv_r_k.reshape(1, RW)
    post_params = [rwkv_ln_g, rwkv_ln_b, r_k_row, hm]
    (o_rwkv,) = rw("rwkv_post", _rwkv_post, R=256,
                   tiled=[(y_wkv, RW, 0), (r_, RW, 0), (k_, RW, 0), (v_, RW, 0), (g_, RW, 0)],
                   full=post_params, out_tiled=[(RW, MXU_DTYPE)])
    y_a = _mm([o_rwkv], [full["w_out_rwkv"]], F32, "out_rwkv")

    expand = jnp.kron(jnp.eye(SP, dtype=F32), jnp.ones((1, SGC), F32))
    s5_in = (s5_a_re[0], s5_a_im[0], s5_log_dt[0].reshape(NG, 1), s5_b_re[0].reshape(NG, SP * SGC),
             s5_b_im[0].reshape(NG, SP * SGC), expand)
    ab_re, ab_im, bb_re, bb_im = _s5_disc(*s5_in)
    eye8 = jnp.eye(8, dtype=F32)

    def blockdiag_in(bb):
        t = bb.reshape(NSG, 8, SP, SGC)
        return jnp.einsum("ab,sapc->sacbp", eye8, t).reshape(NSG, 128, 512)

    def blockdiag_out(cc):
        t = cc.reshape(NSG, 8, SGC, SP)
        return jnp.einsum("ab,sacp->sapbc", eye8, t).reshape(NSG, 512, 128)

    wb = jnp.concatenate([blockdiag_in(bb_re), blockdiag_in(bb_im)], axis=2).astype(MXU_DTYPE)
    wc = jnp.concatenate([blockdiag_out(s5_c_re[0]), -blockdiag_out(s5_c_im[0])], axis=1).astype(MXU_DTYPE)
    ab = jnp.stack([ab_re.reshape(NST), ab_im.reshape(NST)])
    y_ssm, s5_st, s5_x = _s5_fwd(u, wb, wc, ab, Bl, S)
    (s5o,) = rw("s5_post", _s5_post, R=256, tiled=[(y_ssm, SW, 0), (u, SW, 0)], full=[s5_d],
                out_tiled=[(SW, MXU_DTYPE)])
    z = _mm([s5o], [full["w_glu"]], F32, "glu")
    mix_tiled = [(gates, D, 0), (gates, D, 1), (y_a, D, 0), (z, D, 0), (z, D, 1)]
    (mixed_in,) = rw("mix", _mix, R=256, tiled=mix_tiled, out_tiled=[(D, MXU_DTYPE)])
    mixed = _mm([mixed_in], [full["w_out"]], F32, "out_proj")

    def norm2_fwd(xv, mx, gt, sc, sh, g):
        x1 = xv + gt * mx
        return x1, _norm_mod(x1, g, sc, sh)

    x1, h2 = rw("norm2", norm2_fwd, R=256, tiled=[(x2d, D, 0), (mixed, D, 0)],
                batch=[(mod, D, GT1), (mod, D, SC2), (mod, D, SH2)], full=[norm2_g],
                out_tiled=[(D, F32), (D, MXU_DTYPE)])
    up = _mm([h2], [full["w_ffn_up"]], MXU_DTYPE, "ffn_up")
    conv_tiled = [(up, DFF, 0), (up, DFF, 1)]
    conv_full = [conv_wg, conv_wu, conv_bg, conv_bu]

    def act_fwd(*a):
        return _silu_gate(*_conv_act(*a))

    (act,) = rw("ffn_act", act_fwd, R=128, tiled=conv_tiled, prev=conv_tiled, full=conv_full,
                out_tiled=[(DFF, MXU_DTYPE)])
    ffn = _mm([act], [full["w_ffn_down"]], F32, "ffn_down")

    def head(x1v, fv, tv, gt, g):
        x2 = x1v + gt * fv
        y, vjp = jax.vjp(_rms, x2, g)
        e = y - tv
        dx2, dg = vjp(e * (1.0 / D))
        loss = jnp.sum(e * e, keepdims=True) * jnp.ones((1, LANES), F32)
        return dx2, dx2 * gt, jnp.sum(dx2 * fv, axis=0, keepdims=True), dg.reshape(1, D), loss

    dx2, d_ffn, d_gt2, g_norm_f, loss_acc = rw(
        "head", head, R=256, tiled=[(x1, D, 0), (ffn, D, 0), (tgt, D, 0)], batch=[(mod, D, GT2)],
        full=[norm_f_g.reshape(1, D)], out_tiled=[(D, F32), (D, MXU_DTYPE)], out_batch=[D],
        out_acc=[(1, D), (1, LANES)])
    loss = lax.psum(0.5 / D * loss_acc[0, 0], ("x", "y", "c"))

    d_act = _mm([d_ffn], [full["w_ffn_down"]], F32, "d_act", bt=True)
    g_w_ffn_down = _mm_tn(act, d_ffn, "g_ffn_down")

    def act_bwd(ug, uu, dact, hg, hu, wg, wu, bg, bu):
        ug, uu, hg, hu = (z.astype(F32) for z in (ug, uu, hg, hu))
        gate, upv = _conv_act(ug, uu, hg, hu, wg, wu, bg, bu)
        _, vjp_s = jax.vjp(_silu_gate, gate, upv)
        d_gate, d_upv = vjp_s(dact)
        def taps(dh, xv, h):
            return [jnp.sum(dh * _shift_down(xv, h, 2), axis=0, keepdims=True),
                    jnp.sum(dh * _shift_down(xv, h, 1), axis=0, keepdims=True),
                    jnp.sum(dh * xv, axis=0, keepdims=True), jnp.sum(dh, axis=0, keepdims=True)]
        return (d_gate, d_upv, *taps(d_gate, ug, hg), *taps(d_upv, uu, hu))

    dh_g, dh_u, *tapg = rw(
        "ffn_act_bwd", act_bwd, R=128, tiled=conv_tiled + [(d_act, DFF, 0)], prev=conv_tiled, full=conv_full,
        out_tiled=[(DFF, MXU_DTYPE), (DFF, MXU_DTYPE)], out_acc=[(1, DFF)] * 8)
    g_cw_g, g_cb_g = jnp.concatenate(tapg[0:3], axis=0), tapg[3]
    g_cw_u, g_cb_u = jnp.concatenate(tapg[4:7], axis=0), tapg[7]

    def conv_t(dg, du_, ng, nu, wg, wu):
        dg, du_, ng, nu = (z.astype(F32) for z in (dg, du_, ng, nu))

        def ct(d, n, w):
            return w[2:3] * d + w[1:2] * _shift_up(d, n, 1) + w[0:1] * _shift_up(d, n, 2)
        return jnp.concatenate([ct(dg, ng, wg), ct(du_, nu, wu)], axis=1)

    (d_up,) = rw("conv_bwd", conv_t, R=128, tiled=[(dh_g, DFF, 0), (dh_u, DFF, 0)],
                 nxt=[(dh_g, DFF, 0), (dh_u, DFF, 0)], full=[conv_wg, conv_wu], out_tiled=[(2 * DFF, MXU_DTYPE)])
    d_h2 = _mm([d_up], [full["w_ffn_up"]], F32, "d_h2", bt=True)
    g_w_ffn_up = _mm_tn(h2, d_up, "g_ffn_up")

    def norm2_bwd(x1v, dh2, dx2v, mx, gt, sc, sh, g):
        _, vjp = jax.vjp(_norm_mod, x1v, g, sc, sh)
        dxn, dg, dsc, dsh = vjp(dh2)
        dx1 = dx2v + dxn
        return dx1, dx1 * gt, jnp.sum(dx1 * mx, axis=0, keepdims=True), dsc, dsh, dg

    dx1, d_mixed, d_gt1, d_sc2, d_sh2, g_norm2 = rw(
        "norm2_bwd", norm2_bwd, R=256, tiled=[(x1, D, 0), (d_h2, D, 0), (dx2, D, 0), (mixed, D, 0)],
        batch=[(mod, D, GT1), (mod, D, SC2), (mod, D, SH2)], full=[norm2_g],
        out_tiled=[(D, F32), (D, MXU_DTYPE)], out_batch=[D, D, D], out_acc=[(1, D)])

    d_mixed_in = _mm([d_mixed], [full["w_out"]], F32, "d_mixed_in", bt=True)
    g_w_out = _mm_tn(mixed_in, d_mixed, "g_w_out")

    def mix_bwd(ga, gb, ya, za, zb, dm):
        _, vjp = jax.vjp(_mix, ga, gb, ya, za, zb)
        dga, dgb, dya, dza, dzb = vjp(dm)
        return jnp.concatenate([dga, dgb], axis=1), dya, jnp.concatenate([dza, dzb], axis=1)

    d_gates, d_ya, d_z = rw("mix_bwd", mix_bwd, R=256, tiled=mix_tiled + [(d_mixed_in, D, 0)],
                            out_tiled=[(2 * D, MXU_DTYPE), (D, MXU_DTYPE), (2 * D, MXU_DTYPE)])
    d_o_rwkv = _mm([d_ya], [full["w_out_rwkv"]], F32, "d_o_rwkv", bt=True)
    g_w_out_rwkv = _mm_tn(o_rwkv, d_ya, "g_out_rwkv")
    d_s5o = _mm([d_z], [full["w_glu"]], F32, "d_s5o", bt=True)
    g_w_glu = _mm_tn(s5o, d_z, "g_glu")

    def s5_post_bwd(ys, uv, ds, dd):
        _, vjp = jax.vjp(_s5_post, ys, uv, dd)
        return vjp(ds)

    d_yssm, d_u_direct, g_s5_d = rw("s5_post_bwd", s5_post_bwd, R=256,
                                    tiled=[(y_ssm, SW, 0), (u, SW, 0), (d_s5o, SW, 0)], full=[s5_d],
                                    out_tiled=[(SW, F32), (SW, F32)], out_acc=[(1, SW)])
    d_u_ssm, d_wb, d_wc, d_ab = _s5_bwd(u, d_yssm, wb, wc, ab, s5_st, s5_x, Bl, S)

    def diag_in(dw):
        t = dw.reshape(NSG, 8, SGC, 8, SP)
        return jnp.einsum("ab,sacbp->sapc", eye8, t).reshape(NG, SP * SGC)

    def diag_out(dw):
        t = dw.reshape(NSG, 8, SP, 8, SGC)
        return jnp.einsum("ab,sapbc->sacp", eye8, t).reshape(NG, SGC, SP)

    g_s5_c_re = diag_out(d_wc[:, :512])
    g_s5_c_im = -diag_out(d_wc[:, 512:])
    disc_cts = (d_ab[0].reshape(NG, SP), d_ab[1].reshape(NG, SP), diag_in(d_wb[:, :, :512]), diag_in(d_wb[:, :, 512:]))
    g_a_re, g_a_im, g_log_dt, g_b_re, g_b_im = _s5_disc_bwd(*s5_in, disc_cts)

    def post_bwd(yv, rv, kv, vv, gv, do, *pp):
        _, vjp = jax.vjp(lambda *a: _rwkv_post(*a, pp[3]), yv, rv, kv, vv, gv, *pp[:3])
        return vjp(do)

    dy_wkv, dr_b, dk_b, dv_b, dg_, g_ln_g, g_ln_b, g_r_k = rw(
        "rwkv_post_bwd", post_bwd, R=256,
        tiled=[(y_wkv, RW, 0), (r_, RW, 0), (k_, RW, 0), (v_, RW, 0), (g_, RW, 0), (d_o_rwkv, RW, 0)],
        full=post_params, out_tiled=[(RW, F32)] * 5, out_acc=[(1, RW)] * 3)
    dr3, dw3, dk3, dv3, da3, db3 = _wkv_bwd(r_, w_, k_, v_, a_, b_, dy_wkv, ck, Bl, S)

    def prep_bwd(pv, dr1, dr2, dwv, dk1, dk2, dv1, dv2, dav, dbv, dgv, ph, mu, *pp):
        prev = _shift_down(pv, ph, 1)
        ps = pv + (prev - pv) * mu
        _, vjp = jax.vjp(lambda *q: _rwkv_prep(*q, pp[7]), *_split_ps(ps), *pp[:7])
        grads = vjp((dr1 + dr2, dwv, dk1 + dk2, dv1 + dv2, dav, dbv, dgv))
        dps = jnp.concatenate(grads[:5], axis=1)
        return (dps,) + tuple(grads[5:]) + (jnp.sum(dps * (prev - pv), axis=0, keepdims=True),)

    prep_outs = rw(
        "rwkv_prep_bwd", prep_bwd, R=256,
        tiled=[(p, SHIFT, 0), (dr3, RW, 0), (dr_b, RW, 0), (dw3, RW, 0), (dk3, RW, 0), (dk_b, RW, 0),
               (dv3, RW, 0), (dv_b, RW, 0), (da3, RW, 0), (db3, RW, 0), (dg_, RW, 0)],
        prev=[(p, SHIFT, 0)], full=[mu_shift] + prep_params,
        out_tiled=[(SHIFT, F32)],
        out_acc=[(1, RW), (LW + LA, RW), (1, RW), (LW + LA, RW), (LG, RW), (1, RW), (1, RW), (1, SHIFT)])
    d_ps, g_w0, g_w_up_p, g_a0, g_a_up_p, g_g_up, g_k_k, g_k_a, g_mu = prep_outs

    def shift_bwd(dps, nx, mu):
        return dps * (1.0 - mu) + _shift_up(dps * mu, nx * mu, 1)

    (d_p,) = rw("shift_bwd", shift_bwd, R=256, tiled=[(d_ps, SHIFT, 0)], nxt=[(d_ps, SHIFT, 0)], full=[mu_shift],
                out_tiled=[(SHIFT, MXU_DTYPE)])
    (d_u,) = rw("d_u", lambda a1, a2: a1 + a2, R=256, tiled=[(d_u_direct, SW, 0), (d_u_ssm, SW, 0)],
                out_tiled=[(SW, MXU_DTYPE)])
    d_h1 = _mm([d_p, d_u, d_gates], [w_p, w_u, w_g], F32, "d_h1", bt=True)
    g_w_in = jnp.concatenate([_mm_tn(h1, d_p, "g_w_p"), _mm_tn(h1, d_u, "g_w_u"), _mm_tn(h1, d_gates, "g_w_g")], axis=1)

    def norm1_bwd(xv, dh1, dx1v, sc, sh, g):
        _, vjp = jax.vjp(_norm_mod, xv, g, sc, sh)
        dxn, dg, dsc, dsh = vjp(dh1)
        return dx1v + dxn, dsc, dsh, dg

    grad_x, d_sc1, d_sh1, g_norm1 = rw(
        "norm1_bwd", norm1_bwd, R=256, tiled=[(x2d, D, 0), (d_h1, D, 0), (dx1, D, 0)],
        batch=[(mod, D, SC1), (mod, D, SH1)], full=[norm1_g], out_tiled=[(D, F32)], out_batch=[D, D], out_acc=[(1, D)])

    dmod = jnp.concatenate([d_sh1, d_sc1, d_gt1, d_sh2, d_sc2, d_gt2], axis=2).reshape(Bl, 6 * D)
    dmod_all = _gather_two_level([], [dmod], "gather_dmod")[1][0].reshape(8 * Bl, 6 * D)
    dmod_cols = lax.dynamic_slice_in_dim(dmod_all, chip * ncol, ncol, 1)
    g_w_ada, g_b_ada = _ada_bwd(c_all, dmod_cols, dmod_all)

    small = {"norm1_g": g_norm1, "mu_shift": g_mu, "rwkv_w0": g_w0, "rwkv_a0": g_a0, "rwkv_k_k": g_k_k,
             "rwkv_k_a": g_k_a, "rwkv_r_k": g_r_k, "rwkv_ln_g": g_ln_g, "rwkv_ln_b": g_ln_b, "s5_a_re": g_a_re,
             "s5_a_im": g_a_im, "s5_log_dt": g_log_dt, "s5_b_re": g_b_re, "s5_b_im": g_b_im, "s5_c_re": g_s5_c_re,
             "s5_c_im": g_s5_c_im, "s5_d": g_s5_d, "norm2_g": g_norm2,
             "ffn_conv_b": jnp.concatenate([g_cb_g, g_cb_u], axis=1), "norm_f_g": g_norm_f}
    small_names = list(small)
    g_conv_w = jnp.concatenate([g_cw_g, g_cw_u], axis=1)
    shard_small = {"rwkv_w_up": g_w_up_p[:LW], "rwkv_a_up": g_a_up_p[LW:], "rwkv_g_up": g_g_up, "ffn_conv_w": g_conv_w}
    parts = [small[n] for n in small_names] + [_to_shards(shard_small[n], ax) for n, _, ax in BIG_SMALL]
    spack = _pack_rows(parts, F32, SUBLANES)
    s_all = _gather_two_level([], [spack], "gather_gsmall")[1][0]
    s_sum = _sum_slots(s_all.reshape((8,) + spack.shape), F32, "sum_gsmall").reshape(-1)
    grads = {}
    off = 0
    for n in small_names:
        grads[n] = s_sum[off:off + W[n].size].reshape(W[n].shape)
        off += W[n].size
    for n, shape, axis in BIG_SMALL:
        ss = _shard_shape(shape, axis)
        k4 = 4 * math.prod(ss)
        sh4 = s_sum[off:off + k4].reshape(4, math.prod(ss))
        grads[n] = lax.dynamic_index_in_dim(sh4, chip, 0, keepdims=False).reshape((1,) + ss)
        off += k4

    big_g = {"w_in": g_w_in, "w_out_rwkv": g_w_out_rwkv, "w_glu": g_w_glu, "w_out": g_w_out,
             "w_ffn_up": g_w_ffn_up, "w_ffn_down": g_w_ffn_down}
    gsh = [_to_shards(big_g[n], ax).astype(MXU_DTYPE) for n, _, ax in BIG]
    nbig = len(gsh)
    sds = jax.ShapeDtypeStruct
    moves = [(i, i, lambda ref, me, peer, r=g.shape[1]: ref.at[_chip_of(peer), _half(r, peer[2])],
              lambda ref, me: ref.at[2 * _chip_of(me) + me[2]]) for i, g in enumerate(gsh)]
    recv = _exchange("rs_all", ALL_FLIPS, gsh, [sds((8, g.shape[1] // 2, g.shape[2]), MXU_DTYPE) for g in gsh], moves)
    g_half = [_sum_slots(rv, F32, "rs_sum%d" % i) for i, rv in enumerate(recv)]
    moves = [(i, i, lambda ref, me, peer: ref, lambda ref, me, r=2 * g.shape[0]: ref.at[_half(r, me[2])])
             for i, g in enumerate(g_half)]
    g_full = _exchange("rs_share", PAIR_FLIPS, g_half, [sds((2 * g.shape[0], g.shape[1]), F32) for g in g_half], moves)
    for (n, _, _), g in zip(BIG, g_full):
        grads[n] = g[None]
    grads["w_ada"] = g_w_ada[None]
    grads["b_ada"] = g_b_ada

    delta, new_m, new_v = {}, {}, {}
    to2 = lambda z: z.reshape(-1, z.shape[-1])
    for n in ["w_ada"] + [b[0] for b in BIG]:
        d_, m_, v2_ = _adamw(to2(W[n]), to2(grads[n]), to2(M[n]), to2(V[n]), "adamw_" + n)
        delta[n], new_m[n], new_v[n] = (z.reshape(W[n].shape) for z in (d_, m_, v2_))
    rest = [n for n in names if n not in delta]
    packs = [_pack_rows([src[n] for n in rest], F32, SUBLANES) for src in (W, grads, M, V)]
    d_, m_, v2_ = _adamw(*packs, "adamw_small")
    shapes = [W[n].shape for n in rest]
    for dst, z in ((delta, d_), (new_m, m_), (new_v, v2_)):
        for n, val in zip(rest, _unpack(z.reshape(-1), shapes)):
            dst[n] = val

    return (loss, grad_x.reshape(Bl, S, D), *[grads[n] for n in names], *[delta[n] for n in names],
            *[new_m[n] for n in names], *[new_v[n] for n in names])
```

```python
import functools
import math

import jax
import jax.numpy as jnp
from jax import lax
from jax.experimental import pallas as pl
from jax.experimental.pallas import tpu as pltpu

F32 = jnp.float32
BF16 = jnp.bfloat16
MXU_DTYPE = jnp.bfloat16
MESH_IDS = pl.DeviceIdType.MESH
HIGHEST = lax.Precision.HIGHEST

D = 1024
RW, NH, HD = 512, 8, 64
LW, LA, LG = 64, 64, 128
SW, SGC, NG, SP = 512, 16, 32, 64
NSG = 4
SHIFT = 3 * RW + LW + LA + LG
DFF = 2816
RMS_EPS, GN_EPS, L2_EPS = 1e-6, 64e-5, 1e-12
LR, B1, B2, ADAM_EPS, WD, STEP = 0.001, 0.9, 0.999, 1e-8, 0.01, 10
DECAY_SCALE = math.exp(-0.5)
GELU_C = math.sqrt(2.0 / math.pi)

VMEM_LIMIT = 52 * 1024 * 1024
SUBLANES, LANES = 8, 128
HALO = 16


def _pick(n, cap):
    if n <= cap:
        return n
    best = None
    for t in range(LANES, cap + 1, LANES):
        if n % t == 0:
            best = t
    assert best is not None, (n, cap)
    return best


def _params(sem=None, vmem=VMEM_LIMIT):
    return pltpu.CompilerParams(dimension_semantics=sem, vmem_limit_bytes=vmem)


def _chip_of(p):
    return 2 * p[0] + p[1]


def _me():
    return (lax.axis_index("x"), lax.axis_index("y"), lax.axis_index("c"))


def _half(rows, core):
    h = rows // 2
    return pl.ds(pl.multiple_of(core * h, 16 if h % 16 == 0 else SUBLANES), h)


def _exchange(name, flips, srcs, outs, moves):
    ns, no, nf, nm = len(srcs), len(outs), len(flips), len(moves)

    def body(*refs):
        src_refs, out_refs = refs[:ns], refs[ns:ns + no]
        send_sems, recv_sems, loc_sems = refs[ns + no:]
        me = _me()
        copies, locs = [], []
        for m, (si, oi, src_sel, dst_sel) in enumerate(moves):
            for k, f in enumerate(flips):
                peer = tuple(1 - v if b else v for v, b in zip(me, f))
                cp = pltpu.make_async_remote_copy(
                    src_ref=src_sel(src_refs[si], me, peer), dst_ref=dst_sel(out_refs[oi], me),
                    send_sem=send_sems.at[m * nf + k], recv_sem=recv_sems.at[m * nf + k],
                    device_id=peer, device_id_type=MESH_IDS)
                cp.start()
                copies.append(cp)
            loc = pltpu.make_async_copy(src_sel(src_refs[si], me, me), dst_sel(out_refs[oi], me), loc_sems.at[m])
            loc.start()
            locs.append(loc)
        for cp in copies:
            cp.wait_recv()
        for cp in copies:
            cp.wait_send()
        for loc in locs:
            loc.wait()

    return pl.pallas_call(
        body, name=name, out_shape=list(outs),
        in_specs=[pl.BlockSpec(memory_space=pl.ANY)] * ns,
        out_specs=[pl.BlockSpec(memory_space=pl.ANY)] * no,
        scratch_shapes=[pltpu.SemaphoreType.DMA((nm * nf,)), pltpu.SemaphoreType.DMA((nm * nf,)),
                        pltpu.SemaphoreType.DMA((nm,))],
    )(*srcs)


_HBM = pl.BlockSpec(memory_space=pltpu.HBM)
_SEM = pl.BlockSpec(memory_space=pltpu.SEMAPHORE)
_DATAFLOW = pltpu.SideEffectType.DATAFLOW_SIDE_EFFECTING


def _split_copies(flips, moves, src_refs, land_refs, send_sems, recv_sems):
    me = _me()
    nf = len(flips)
    out = []
    for m, (si, li, src_sel, dst_sel) in enumerate(moves):
        for k, f in enumerate(flips):
            peer = tuple(1 - v if b else v for v, b in zip(me, f))
            out.append(pltpu.make_async_remote_copy(
                src_ref=src_sel(src_refs[si], me, peer), dst_ref=dst_sel(land_refs[li], me, k),
                send_sem=send_sems.at[m * nf + k], recv_sem=recv_sems.at[m * nf + k],
                device_id=peer, device_id_type=MESH_IDS))
    return out


def _send_start(name, flips, srcs, land_shapes, moves):
    ns, nl = len(srcs), len(land_shapes)
    n = len(moves) * len(flips)

    def body(*refs):
        for cp in _split_copies(flips, moves, refs[:ns], refs[ns:ns + nl], refs[ns + nl], refs[ns + nl + 1]):
            cp.start()
        refs[-1][...] = jnp.zeros(refs[-1].shape, F32)

    hbm = lambda z: pltpu.with_memory_space_constraint(z, pltpu.HBM)
    lands = [lax.empty(s.shape, s.dtype) for s in land_shapes]
    res = pl.pallas_call(
        body, name=name,
        out_shape=(pltpu.SemaphoreType.DMA((n,)), pltpu.SemaphoreType.DMA((n,)),
                   *[pltpu.HBM(z.shape, z.dtype) for z in srcs], *[pltpu.HBM(s.shape, s.dtype) for s in land_shapes],
                   jax.ShapeDtypeStruct((SUBLANES, LANES), F32)),
        in_specs=[_HBM] * (ns + nl),
        out_specs=(_SEM, _SEM, *[_HBM] * (ns + nl), pl.BlockSpec(memory_space=pltpu.VMEM)),
        input_output_aliases={i: 2 + i for i in range(ns + nl)},
        compiler_params=pltpu.CompilerParams(has_side_effects=_DATAFLOW),
    )(*[hbm(z) for z in srcs], *[hbm(z) for z in lands])
    return {"sems": res[:2], "srcs": list(res[2:2 + ns]), "lands": list(res[2 + ns:2 + ns + nl]), "token": res[-1][0, 0]}


def _send_wait(name, flips, started, moves, after):
    srcs, lands = started["srcs"], started["lands"]
    ns, nl = len(srcs), len(lands)

    def body(*refs):
        for cp in _split_copies(flips, moves, refs[:ns], refs[ns:ns + nl], refs[ns + nl], refs[ns + nl + 1]):
            cp.wait_send()
            cp.wait_recv()

    res = pl.pallas_call(
        body, name=name, out_shape=[pltpu.HBM(z.shape, z.dtype) for z in srcs + lands],
        in_specs=[_HBM] * (ns + nl) + [_SEM, _SEM, pl.BlockSpec(memory_space=pl.ANY)],
        out_specs=[_HBM] * (ns + nl), input_output_aliases={i: i for i in range(ns + nl)},
        compiler_params=pltpu.CompilerParams(has_side_effects=_DATAFLOW),
    )(*srcs, *lands, *started["sems"], after)
    return list(res[:ns]), list(res[ns:])


CHIP_FLIPS = ((1, 0, 0), (0, 1, 0), (1, 1, 0))
PAIR_FLIPS = ((0, 0, 1),)
ALL_FLIPS = CHIP_FLIPS + ((1, 0, 1), (0, 1, 1), (1, 1, 1)) + PAIR_FLIPS


def _gather_two_level(chip_arrs, dev_arrs, name):
    arrs = list(chip_arrs) + list(dev_arrs)
    n, nchip = len(arrs), len(chip_arrs)
    NS = 7

    def body(*refs):
        srcs, outs = refs[:n], refs[n:2 * n]
        send_sems, recv_sems, loc_sems = refs[2 * n:]
        x, y, c = _me()
        sib = (x, y, 1 - c)
        chips = [(1 - x, y), (x, 1 - y), (1 - x, 1 - y)]
        mine = 2 * x + y
        ids = [2 * cx + cy for cx, cy in chips]

        def part(i, slot, core):
            if i < nchip:
                return outs[i].at[slot, _half(arrs[i].shape[0], core)]
            return outs[i].at[slot, core]

        def rcopy(i, k, src, dst, to):
            return pltpu.make_async_remote_copy(src_ref=src, dst_ref=dst, send_sem=send_sems.at[i * NS + k],
                                                recv_sem=recv_sems.at[i * NS + k], device_id=to, device_id_type=MESH_IDS)

        started, locs = [], []
        for i in range(n):
            own = srcs[i].at[_half(arrs[i].shape[0], c)] if i < nchip else srcs[i]
            loc = pltpu.make_async_copy(srcs[i], outs[i].at[mine] if i < nchip else outs[i].at[mine, c], loc_sems.at[i])
            loc.start()
            locs.append(loc)
            for f, chip in enumerate(chips):
                cp = rcopy(i, f, own, part(i, mine, c), (*chip, c))
                cp.start()
                started.append(cp)
            if i >= nchip:
                cp = rcopy(i, 6, own, part(i, mine, c), sib)
                cp.start()
                started.append(cp)
        for i in range(n):
            for f in range(3):
                land = part(i, ids[f], c)
                rcopy(i, f, land, land, sib).wait_recv()
                fw = rcopy(i, 3 + f, land, land, sib)
                fw.start()
                started.append(fw)
        for i in range(n):
            for f in range(3):
                land = part(i, ids[f], 1 - c)
                rcopy(i, 3 + f, land, land, sib).wait_recv()
            if i >= nchip:
                land = part(i, mine, 1 - c)
                rcopy(i, 6, land, land, sib).wait_recv()
        for cp in started:
            cp.wait_send()
        for loc in locs:
            loc.wait()

    outs = [jax.ShapeDtypeStruct((4,) + a.shape, a.dtype) for a in chip_arrs]
    outs += [jax.ShapeDtypeStruct((4, 2) + a.shape, a.dtype) for a in dev_arrs]
    res = pl.pallas_call(
        body, name=name, out_shape=outs,
        in_specs=[pl.BlockSpec(memory_space=pl.ANY)] * n, out_specs=[pl.BlockSpec(memory_space=pl.ANY)] * n,
        scratch_shapes=[pltpu.SemaphoreType.DMA((n * NS,)), pltpu.SemaphoreType.DMA((n * NS,)),
                        pltpu.SemaphoreType.DMA((n,))],
    )(*arrs)
    return res[:nchip], res[nchip:]


def _mm(As, Bs, out_dtype, name, tm=512, cap=1408, bt=False):
    n = len(As)
    M, N = As[0].shape[0], Bs[0].shape[0 if bt else 1]
    tm = min(tm, M)
    tn = _pick(N, cap)
    dims = (((1,), (1,)), ((), ())) if bt else (((1,), (0,)), ((), ()))

    def body(*refs):
        o = refs[2 * n]
        acc = None
        for a, b in zip(refs[:n], refs[n:2 * n]):
            d = lax.dot_general(a[...].astype(MXU_DTYPE), b[...].astype(MXU_DTYPE), dims, preferred_element_type=F32)
            acc = d if acc is None else acc + d
        o[...] = acc.astype(o.dtype)

    in_specs = [pl.BlockSpec((tm, a.shape[1]), lambda i, j: (i, 0)) for a in As]
    if bt:
        in_specs += [pl.BlockSpec((tn, b.shape[1]), lambda i, j: (j, 0)) for b in Bs]
    else:
        in_specs += [pl.BlockSpec((b.shape[0], tn), lambda i, j: (0, j)) for b in Bs]
    return pl.pallas_call(
        body, name=name, grid=(M // tm, N // tn), in_specs=in_specs,
        out_specs=pl.BlockSpec((tm, tn), lambda i, j: (i, j)),
        out_shape=jax.ShapeDtypeStruct((M, N), out_dtype),
        compiler_params=_params(("parallel", "parallel")),
    )(*As, *Bs)


def _mm_tn(A, G, name, tt=1024, cap=1024):
    T, Ka = A.shape
    N = G.shape[1]
    tt = min(tt, T)
    tk = _pick(Ka, cap)
    tn = _pick(N, cap)

    def body(a, g, o):
        @pl.when(pl.program_id(2) == 0)
        def _():
            o[...] = jnp.zeros(o.shape, F32)
        o[...] += lax.dot_general(a[...].astype(MXU_DTYPE), g[...].astype(MXU_DTYPE),
                                  (((0,), (0,)), ((), ())), preferred_element_type=F32)

    return pl.pallas_call(
        body, name=name, grid=(Ka // tk, N // tn, T // tt),
        in_specs=[pl.BlockSpec((tt, tk), lambda i, j, t: (t, i)), pl.BlockSpec((tt, tn), lambda i, j, t: (t, j))],
        out_specs=pl.BlockSpec((tk, tn), lambda i, j, t: (i, j)),
        out_shape=jax.ShapeDtypeStruct((Ka, N), F32),
        compiler_params=_params(("parallel", "parallel", "arbitrary")),
    )(A, G)


def _rowwise(name, fn, *, Bl, S, R, tiled=(), prev=(), nxt=(), batch=(), full=(),
             out_tiled=(), out_batch=(), out_acc=()):
    R = min(R, S)
    nS = S // R
    T = Bl * S
    hb = R // HALO
    n_in = len(tiled) + len(prev) + len(nxt) + len(batch) + len(full)

    in_specs, args = [], []
    for a, wd, cb in tiled:
        in_specs.append(pl.BlockSpec((R, wd), lambda b, i, cb=cb: (b * nS + i, cb)))
        args.append(a)
    for a, wd, cb in prev:
        in_specs.append(pl.BlockSpec((HALO, wd), lambda b, i, cb=cb: (jnp.maximum((b * nS + i) * hb - 1, 0), cb)))
        args.append(a)
    for a, wd, cb in nxt:
        in_specs.append(pl.BlockSpec((HALO, wd), lambda b, i, cb=cb: (jnp.minimum((b * nS + i + 1) * hb, T // HALO - 1), cb)))
        args.append(a)
    for a, wd, cb in batch:
        in_specs.append(pl.BlockSpec((1, 1, wd), lambda b, i, cb=cb: (b, 0, cb)))
        args.append(a)
    for a in full:
        in_specs.append(pl.BlockSpec(a.shape, lambda b, i, nd=a.ndim: (0,) * nd))
        args.append(a)

    out_specs, out_shape = [], []
    for C, dt in out_tiled:
        out_specs.append(pl.BlockSpec((R, C), lambda b, i: (b * nS + i, 0)))
        out_shape.append(jax.ShapeDtypeStruct((T, C), dt))
    for C in out_batch:
        out_specs.append(pl.BlockSpec((1, 1, C), lambda b, i: (b, 0, 0)))
        out_shape.append(jax.ShapeDtypeStruct((Bl, 1, C), F32))
    for shp in out_acc:
        out_specs.append(pl.BlockSpec(shp, lambda b, i, nd=len(shp): (0,) * nd))
        out_shape.append(jax.ShapeDtypeStruct(shp, F32))

    nt, npv, nnx, nbt = len(tiled), len(prev), len(nxt), len(batch)

    def body(*refs):
        b, i = pl.program_id(0), pl.program_id(1)
        ins, outs = refs[:n_in], refs[n_in:]
        vals = [r[...] for r in ins[:nt]]
        vals += [jnp.where(i > 0, r[...], jnp.zeros(r.shape, r.dtype)) for r in ins[nt:nt + npv]]
        vals += [jnp.where(i < nS - 1, r[...], jnp.zeros(r.shape, r.dtype)) for r in ins[nt + npv:nt + npv + nnx]]
        vals += [r[0] for r in ins[nt + npv + nnx:nt + npv + nnx + nbt]]
        vals += [r[...] for r in ins[nt + npv + nnx + nbt:]]
        res = fn(*vals)
        if not isinstance(res, (tuple, list)):
            res = (res,)
        k = 0
        for _ in out_tiled:
            outs[k][...] = res[k].astype(outs[k].dtype)
            k += 1
        for _ in out_batch:
            o = outs[k]

            @pl.when(i == 0)
            def _(o=o):
                o[...] = jnp.zeros(o.shape, F32)
            o[0] += res[k]
            k += 1
        for _ in out_acc:
            o = outs[k]

            @pl.when((i == 0) & (b == 0))
            def _(o=o):
                o[...] = jnp.zeros(o.shape, F32)
            o[...] += res[k]
            k += 1

    out = pl.pallas_call(
        body, name=name, grid=(Bl, nS), in_specs=in_specs, out_specs=out_specs, out_shape=out_shape,
        compiler_params=_params(("arbitrary", "arbitrary")),
    )(*args)
    return out


def _shift_down(x, halo, k):
    row = lax.broadcasted_iota(jnp.int32, x.shape, 0)
    out = pltpu.roll(x, k, 0)
    for j in range(k):
        out = jnp.where(row == j, halo[HALO - k + j:HALO - k + j + 1, :], out)
    return out


def _shift_up(x, halo, k):
    n = x.shape[0]
    row = lax.broadcasted_iota(jnp.int32, x.shape, 0)
    out = pltpu.roll(x, n - k, 0)
    for j in range(k):
        out = jnp.where(row == n - k + j, halo[j:j + 1, :], out)
    return out


def _dotm(a, b):
    return jnp.dot(a.astype(MXU_DTYPE), b.astype(MXU_DTYPE), preferred_element_type=F32)


def _split_bf16(x):
    hi = x.astype(BF16)
    return hi, (x - hi.astype(F32)).astype(BF16)


def _headsum_2pass(x, hm):
    hi, lo = _split_bf16(x)
    hb = hm.astype(BF16)
    return jnp.dot(hi, hb, preferred_element_type=F32) + jnp.dot(lo, hb, preferred_element_type=F32)


@jax.custom_vjp
def _headsum(x, hm):
    return _headsum_2pass(x, hm)


_headsum.defvjp(lambda x, hm: (_headsum_2pass(x, hm), hm),
                lambda hm, g: (_headsum_2pass(g, hm), jnp.zeros_like(hm)))


def _sigmoid(x):
    return 1.0 / (1.0 + jnp.exp(-x))


def _rms(x, g):
    return x * lax.rsqrt(jnp.mean(x * x, axis=-1, keepdims=True) + RMS_EPS) * g


def _norm_mod(x, g, sc, sh):
    return _rms(x, g) * (1.0 + sc) + sh


def _split_ps(ps):
    return (ps[:, 0:RW], ps[:, RW:2 * RW], ps[:, 2 * RW:3 * RW], ps[:, 3 * RW:3 * RW + LW + LA],
            ps[:, 3 * RW + LW + LA:SHIFT])


def _rwkv_prep(r, k, v, wa, gd, w0, w_up_p, a0, a_up_p, g_up, k_k, k_a, hm):
    w_raw = w0 + _dotm(jnp.tanh(wa), w_up_p)
    decay = jnp.exp(-DECAY_SCALE * _sigmoid(w_raw))
    a = _sigmoid(a0 + _dotm(wa, a_up_p))
    g = _dotm(_sigmoid(gd), g_up)
    kk = k * k_k
    kk = kk * lax.rsqrt(_headsum(kk * kk, hm) + L2_EPS)
    k2 = k * (1.0 + (a - 1.0) * k_a)
    return r, decay, k2, v, -kk, kk * a, g


def _rwkv_post(y, r, k2, v, g, ln_g, ln_b, r_k, hm):
    mean = _headsum(y, hm) * (1.0 / HD)
    yc = y - mean
    var = _headsum(yc * yc, hm) * (1.0 / HD)
    yn = yc * lax.rsqrt(var + GN_EPS) * ln_g + ln_b
    bonus = _headsum(r * k2 * r_k, hm) * v
    return (yn + bonus) * g


def _gelu(x):
    return 0.5 * x * (1.0 + jnp.tanh(GELU_C * (x + 0.044715 * (x * x * x))))


def _s5_post(yssm, u, d):
    return _gelu(yssm + d * u)


def _mix(ga, gb, ya, za, zb):
    return _sigmoid(ga) * ya + _sigmoid(gb) * (za * _sigmoid(zb))


def _conv_act(up_g, up_u, hg, hu, w_g, w_u, b_g, b_u):
    up_g, up_u, hg, hu = (z.astype(F32) for z in (up_g, up_u, hg, hu))

    def conv(x, h, w, b):
        return b + w[0:1] * _shift_down(x, h, 2) + w[1:2] * _shift_down(x, h, 1) + w[2:3] * x
    gate = conv(up_g, hg, w_g, b_g)
    upv = conv(up_u, hu, w_u, b_u)
    return gate, upv


def _silu_gate(gate, upv):
    return gate * _sigmoid(gate) * upv


WKV_L = 64
_NT, _NN, _TN = ((1,), (1,)), ((1,), (0,)), ((0,), (0,))


def _dotw(x, y, dims):
    return lax.dot_general(x.astype(MXU_DTYPE), y.astype(MXU_DTYPE), (dims, ((), ())), preferred_element_type=F32)


def _dot3(x, y, dims):
    (xh, xl), (yh, yl) = _split_bf16(x), _split_bf16(y)
    d = lambda p, q: lax.dot_general(p, q, (dims, ((), ())), preferred_element_type=F32)
    return d(xh, yh) + d(xh, yl) + d(xl, yh)


@jax.custom_vjp
def _gram3(x, y):
    return _dot3(x, y, _NT)


_gram3.defvjp(lambda x, y: (_dot3(x, y, _NT), (x, y)),
              lambda res, g: (_dot3(g, res[1], _NN), _dot3(g, res[0], _TN)))


def _wkv_chunk(s0, r, w, k, v, a, b):
    y, s1 = _wkv_chunks((s0,), (r,), (w,), (k,), (v,), (a,), (b,))
    return y[0], s1[0]


def _wkv_chunks(s0, r, w, k, v, a, b):
    each = lambda f, *ls: tuple(f(*xs) for xs in zip(*ls))
    L = r[0].shape[0]
    n2 = 2 * L
    lane_head = lax.broadcasted_iota(jnp.int32, (2, 1, 2 * HD), 2) // HD
    head_mask = (lane_head == lax.broadcasted_iota(jnp.int32, (2, 1, 2 * HD), 0)).astype(F32)
    ri = lax.broadcasted_iota(jnp.int32, (n2, n2), 0)
    ci = lax.broadcasted_iota(jnp.int32, (n2, n2), 1)
    same = (ri // L) == (ci // L)
    strict = same & ((ci % L) < (ri % L))
    incl = same & ((ci % L) <= (ri % L))
    si = lax.broadcasted_iota(jnp.int32, (2 * HD, 2 * HD), 0) // HD
    sj = lax.broadcasted_iota(jnp.int32, (2 * HD, 2 * HD), 1) // HD
    tri = (lax.broadcasted_iota(jnp.int32, (L, L), 0) >= lax.broadcasted_iota(jnp.int32, (L, L), 1)).astype(F32)

    stack = lambda z: (z[None] * head_mask).reshape(n2, 2 * HD)
    dup = lambda z: jnp.broadcast_to(z[None], (2, L, 2 * HD)).reshape(n2, 2 * HD)
    gram = _gram3
    nt, nn, tn = (lambda x, y, d=d: _dotw(x, y, d) for d in (_NT, _NN, _TN))
    add = lambda x, y: x + y

    lw = each(jnp.log, w)
    cum = each(lambda z: jnp.dot(tri, z, preferred_element_type=F32, precision=HIGHEST), lw)
    tot = each(lambda z: jnp.sum(z, axis=0, keepdims=True), lw)
    a2 = each(lambda av, cv, lv: stack(av * jnp.exp(cv - lv)), a, cum, lw)
    r2 = each(lambda rv, cv: stack(rv * jnp.exp(cv)), r, cum)
    v2 = each(stack, v)
    b2 = each(lambda bv, cv: dup(bv * jnp.exp(-cv)), b, cum)
    k2 = each(lambda kv, cv: dup(kv * jnp.exp(-cv)), k, cum)
    n_ab = each(lambda x, y: jnp.where(strict, gram(x, y), 0.0), a2, b2)
    n_ak = each(lambda x, y: jnp.where(strict, gram(x, y), 0.0), a2, k2)
    m_rb = each(lambda x, y: jnp.where(incl, gram(x, y), 0.0), r2, b2)
    m_rk = each(lambda x, y: jnp.where(incl, gram(x, y), 0.0), r2, k2)
    u = each(add, each(nt, a2, s0), each(nn, n_ak, v2))
    q = n_ab
    steps = L.bit_length() - 1
    for i in range(steps):
        u = each(add, u, each(nn, q, u))
        if i < steps - 1:
            q = each(nn, q, q)
    y2 = each(lambda x, y, z: x + y + z, each(nt, r2, s0), each(nn, m_rb, u), each(nn, m_rk, v2))
    y = each(lambda z: jnp.sum(z.reshape(2, L, 2 * HD), axis=0), y2)
    b3 = each(lambda bv, tv, cv: dup(bv * jnp.exp(tv - cv)), b, tot, cum)
    k3 = each(lambda kv, tv, cv: dup(kv * jnp.exp(tv - cv)), k, tot, cum)
    upd = each(add, each(tn, u, b3), each(tn, v2, k3))
    s1 = each(lambda sv, tv, uv: sv * jnp.exp(tv) + jnp.where(si == sj, uv, 0.0), s0, tot, upd)
    return y, s1


NPAIR = NH // 2


def _wkv_nb(Bl):
    return 2 if Bl % 2 == 0 else 1


def _wkv_fwd(r, w, k, v, a, b, Bl, S):
    L = WKV_L
    nC = S // L
    nb = _wkv_nb(Bl)
    chains = [(bi, p, slice(p * 2 * HD, (p + 1) * 2 * HD)) for bi in range(nb) for p in range(NPAIR)]

    def body(r_ref, w_ref, k_ref, v_ref, a_ref, b_ref, y_ref, ck_ref, s_ref):
        @pl.when(pl.program_id(1) == 0)
        def _():
            s_ref[...] = jnp.zeros(s_ref.shape, F32)
        s0 = tuple(s_ref[bi, p] for bi, p, _ in chains)
        ops = [tuple(z[bi, :, cs] for bi, _, cs in chains) for z in (r_ref, w_ref, k_ref, v_ref, a_ref, b_ref)]
        y, s1 = _wkv_chunks(s0, *ops)
        for i, (bi, p, cs) in enumerate(chains):
            ck_ref[bi, 0, p] = s0[i]
            y_ref[bi, :, cs] = y[i]
            s_ref[bi, p] = s1[i]

    to3 = lambda z: z.reshape(Bl, S, RW)
    row_spec = pl.BlockSpec((nb, L, RW), lambda g, c: (g, c, 0))
    y, ck = pl.pallas_call(
        body, name="wkv_fwd", grid=(Bl // nb, nC), in_specs=[row_spec] * 6,
        out_specs=[row_spec, pl.BlockSpec((nb, 1, NPAIR, 2 * HD, 2 * HD), lambda g, c: (g, c, 0, 0, 0))],
        out_shape=[jax.ShapeDtypeStruct((Bl, S, RW), F32), jax.ShapeDtypeStruct((Bl, nC, NPAIR, 2 * HD, 2 * HD), F32)],
        scratch_shapes=[pltpu.VMEM((nb, NPAIR, 2 * HD, 2 * HD), F32)],
        compiler_params=_params(("arbitrary", "arbitrary")),
    )(*(to3(z) for z in (r, w, k, v, a, b)))
    return y.reshape(Bl * S, RW), ck


def _wkv_bwd(r, w, k, v, a, b, dy, ck, Bl, S):
    L = WKV_L
    nC = S // L
    nb = _wkv_nb(Bl)
    chains = [(bi, p, slice(p * 2 * HD, (p + 1) * 2 * HD)) for bi in range(nb) for p in range(NPAIR)]

    def body(r_ref, w_ref, k_ref, v_ref, a_ref, b_ref, dy_ref, ck_ref,
             dr_ref, dw_ref, dk_ref, dv_ref, da_ref, db_ref, ds_ref):
        @pl.when(pl.program_id(1) == 0)
        def _():
            ds_ref[...] = jnp.zeros(ds_ref.shape, F32)
        s0 = tuple(ck_ref[bi, 0, p] for bi, p, _ in chains)
        ops = [tuple(z[bi, :, cs] for bi, _, cs in chains) for z in (r_ref, w_ref, k_ref, v_ref, a_ref, b_ref)]
        cts = (tuple(dy_ref[bi, :, cs] for bi, _, cs in chains), tuple(ds_ref[bi, p] for bi, p, _ in chains))
        ds0, *grads = jax.vjp(_wkv_chunks, s0, *ops)[1](cts)
        for i, (bi, p, cs) in enumerate(chains):
            ds_ref[bi, p] = ds0[i]
            for o, g in zip((dr_ref, dw_ref, dk_ref, dv_ref, da_ref, db_ref), grads):
                o[bi, :, cs] = g[i]

    to3 = lambda z: z.reshape(Bl, S, RW)
    row_spec = pl.BlockSpec((nb, L, RW), lambda g, c: (g, nC - 1 - c, 0))
    rows = jax.ShapeDtypeStruct((Bl, S, RW), F32)
    outs = pl.pallas_call(
        body, name="wkv_bwd", grid=(Bl // nb, nC),
        in_specs=[row_spec] * 7 + [pl.BlockSpec((nb, 1, NPAIR, 2 * HD, 2 * HD), lambda g, c: (g, nC - 1 - c, 0, 0, 0))],
        out_specs=[row_spec] * 6, out_shape=[rows] * 6,
        scratch_shapes=[pltpu.VMEM((nb, NPAIR, 2 * HD, 2 * HD), F32)],
        compiler_params=_params(("arbitrary", "arbitrary")),
    )(*(to3(z) for z in (r, w, k, v, a, b, dy)), ck)
    return [o.reshape(Bl * S, RW) for o in outs]


NST = NG * SP


def _cmul(ar, ai, br, bi):
    return ar * br - ai * bi, ar * bi + ai * br


def _s5_tiles(are, aim, reverse):
    if reverse:
        aim = -aim
    row = lax.broadcasted_iota(jnp.int32, (SUBLANES, NST), 0)
    pw = [(are, aim)]
    for _ in range(SUBLANES - 1):
        pw.append(_cmul(pw[-1][0], pw[-1][1], are, aim))
    bc = lambda z: jnp.broadcast_to(z, (SUBLANES, NST))
    ms = []
    for kk in (1, 2, 4):
        cond = (row < SUBLANES - kk) if reverse else (row >= kk)
        ms.append((jnp.where(cond, bc(pw[kk - 1][0]), 0.0), jnp.where(cond, bc(pw[kk - 1][1]), 0.0)))
    pr = jnp.zeros((SUBLANES, NST), F32)
    pi = jnp.zeros((SUBLANES, NST), F32)
    for i in range(SUBLANES):
        n = SUBLANES - i if reverse else i + 1
        pr = jnp.where(row == i, bc(pw[n - 1][0]), pr)
        pi = jnp.where(row == i, bc(pw[n - 1][1]), pi)
    return ms, (pr, pi)


def _s5_block(re, im, ms, pc, cre, cim, sg, reverse):
    ln = slice(sg * 512, (sg + 1) * 512)
    for (mr, mi), kk in zip(ms, (1, 2, 4)):
        sh = SUBLANES - kk if reverse else kk
        sre, sim = pltpu.roll(re, sh, 0), pltpu.roll(im, sh, 0)
        tr, ti = _cmul(mr[:, ln], mi[:, ln], sre, sim)
        re, im = re + tr, im + ti
    tr, ti = _cmul(pc[0][:, ln], pc[1][:, ln], cre[:, ln], cim[:, ln])
    return re + tr, im + ti


def _s5_scan(X_ref, n_rows, ms, pc, cre, cim, reverse, visit=None, acc0=None):
    nblk = n_rows // SUBLANES

    def it(i, carry):
        cre, cim, acc = carry
        j = nblk - 1 - i if reverse else i
        rows = pl.ds(pl.multiple_of(j * SUBLANES, SUBLANES), SUBLANES)
        edge = 0 if reverse else SUBLANES - 1
        blocks, ncre, ncim = [], [], []
        for sg in range(NSG):
            lr = slice(sg * 1024, sg * 1024 + 512)
            li = slice(sg * 1024 + 512, (sg + 1) * 1024)
            re, im = _s5_block(X_ref[rows, lr], X_ref[rows, li], ms, pc, cre, cim, sg, reverse)
            X_ref[rows, lr] = re
            X_ref[rows, li] = im
            blocks.append((re, im))
            ncre.append(re[edge:edge + 1])
            ncim.append(im[edge:edge + 1])
        if visit is not None:
            acc = visit(j, blocks, acc)
        return jnp.concatenate(ncre, axis=1), jnp.concatenate(ncim, axis=1), acc

    return lax.fori_loop(0, nblk, it, (cre, cim, acc0 if acc0 is not None else 0))


def _s5_fwd(u, wb, wc, ab, Bl, S, R=256):
    R = min(R, S)
    nC = S // R

    def body(u_ref, wb_ref, wc_ref, ab_ref, y_ref, st_ref, X_ref, car_ref):
        @pl.when(pl.program_id(1) == 0)
        def _():
            car_ref[...] = jnp.zeros(car_ref.shape, F32)
        st_ref[0, 0] = car_ref[...]
        ms, pc = _s5_tiles(ab_ref[0:1], ab_ref[1:2], False)
        for sg in range(NSG):
            X_ref[:, sg * 1024:(sg + 1) * 1024] = _dotm(u_ref[:, sg * 128:(sg + 1) * 128], wb_ref[sg])
        cre, cim, _ = _s5_scan(X_ref, R, ms, pc, car_ref[0:1], car_ref[1:2], False)
        car_ref[0:1] = cre
        car_ref[1:2] = cim
        for sg in range(NSG):
            y_ref[:, sg * 128:(sg + 1) * 128] = _dotm(X_ref[:, sg * 1024:(sg + 1) * 1024], wc_ref[sg])

    return pl.pallas_call(
        body, name="s5_fwd", grid=(Bl, nC),
        in_specs=[pl.BlockSpec((R, SW), lambda b, c: (b * nC + c, 0)),
                  pl.BlockSpec(wb.shape, lambda b, c: (0, 0, 0)), pl.BlockSpec(wc.shape, lambda b, c: (0, 0, 0)),
                  pl.BlockSpec(ab.shape, lambda b, c: (0, 0))],
        out_specs=[pl.BlockSpec((R, SW), lambda b, c: (b * nC + c, 0)),
                   pl.BlockSpec((1, 1, 2, NST), lambda b, c: (b, c, 0, 0)),
                   pl.BlockSpec((R, 2 * NST), lambda b, c: (b * nC + c, 0))],
        out_shape=[jax.ShapeDtypeStruct((Bl * S, SW), F32), jax.ShapeDtypeStruct((Bl, nC, 2, NST), F32),
                   jax.ShapeDtypeStruct((Bl * S, 2 * NST), F32)],
        scratch_shapes=[pltpu.VMEM((2, NST), F32)],
        compiler_params=_params(("arbitrary", "arbitrary")),
    )(u, wb, wc, ab)


def _s5_bwd(u, dy, wb, wc, ab, st, xs, Bl, S, R=256):
    R = min(R, S)
    nC = S // R

    def body(u_ref, dy_ref, wb_ref, wc_ref, ab_ref, st_ref, X_ref, du_ref, dwb_ref, dwc_ref, dab_ref,
             G_ref, car_ref):
        first = (pl.program_id(0) == 0) & (pl.program_id(1) == 0)

        @pl.when(first)
        def _():
            dwb_ref[...] = jnp.zeros(dwb_ref.shape, F32)
            dwc_ref[...] = jnp.zeros(dwc_ref.shape, F32)
            dab_ref[...] = jnp.zeros(dab_ref.shape, F32)

        @pl.when(pl.program_id(1) == 0)
        def _():
            car_ref[...] = jnp.zeros(car_ref.shape, F32)

        are, aim = ab_ref[0:1], ab_ref[1:2]
        dyv = dy_ref[...].astype(MXU_DTYPE)
        for sg in range(NSG):
            G_ref[:, sg * 1024:(sg + 1) * 1024] = lax.dot_general(
                dyv[:, sg * 128:(sg + 1) * 128], wc_ref[sg].astype(MXU_DTYPE), (((1,), (1,)), ((), ())),
                preferred_element_type=F32)
        rms_, rpc = _s5_tiles(are, aim, True)
        row = lax.broadcasted_iota(jnp.int32, (SUBLANES, 512), 0)

        def visit(j, blocks, acc):
            before = pl.multiple_of(jnp.maximum(j - 1, 0) * SUBLANES, SUBLANES)
            prow = X_ref[pl.ds(before, SUBLANES), :][SUBLANES - 1:SUBLANES]
            rows = pl.ds(pl.multiple_of(j * SUBLANES, SUBLANES), SUBLANES)
            are_acc, aim_acc = [], []
            for sg in range(NSG):
                lr = slice(sg * 1024, sg * 1024 + 512)
                li = slice(sg * 1024 + 512, (sg + 1) * 1024)
                ln = slice(sg * 512, (sg + 1) * 512)
                pre = jnp.where(j > 0, prow[:, lr], st_ref[0, 0, 0:1, ln])
                pim = jnp.where(j > 0, prow[:, li], st_ref[0, 0, 1:2, ln])
                xre = jnp.where(row == 0, pre, pltpu.roll(X_ref[rows, lr], 1, 0))
                xim = jnp.where(row == 0, pim, pltpu.roll(X_ref[rows, li], 1, 0))
                dre, dim = blocks[sg]
                are_acc.append(dre * xre + dim * xim)
                aim_acc.append(dim * xre - dre * xim)
            return acc[0] + jnp.concatenate(are_acc, axis=1), acc[1] + jnp.concatenate(aim_acc, axis=1)

        zero = jnp.zeros((SUBLANES, NST), F32)
        cre, cim, acc = _s5_scan(G_ref, R, rms_, rpc, car_ref[0:1], car_ref[1:2], True, visit, (zero, zero))
        car_ref[0:1] = cre
        car_ref[1:2] = cim
        dab_ref[0:1] += jnp.sum(acc[0], axis=0, keepdims=True)
        dab_ref[1:2] += jnp.sum(acc[1], axis=0, keepdims=True)
        uv = u_ref[...].astype(MXU_DTYPE)
        for sg in range(NSG):
            cs = slice(sg * 1024, (sg + 1) * 1024)
            us = slice(sg * 128, (sg + 1) * 128)
            gx = G_ref[:, cs].astype(MXU_DTYPE)
            dwb_ref[sg] += lax.dot_general(uv[:, us], gx, (((0,), (0,)), ((), ())), preferred_element_type=F32)
            dwc_ref[sg] += lax.dot_general(X_ref[:, cs].astype(MXU_DTYPE), dyv[:, us], (((0,), (0,)), ((), ())),
                                           preferred_element_type=F32)
            du_ref[:, us] = lax.dot_general(gx, wb_ref[sg].astype(MXU_DTYPE), (((1,), (1,)), ((), ())),
                                            preferred_element_type=F32)

    rmap = lambda b, c: (b * nC + nC - 1 - c, 0)
    return pl.pallas_call(
        body, name="s5_bwd", grid=(Bl, nC),
        in_specs=[pl.BlockSpec((R, SW), rmap), pl.BlockSpec((R, SW), rmap),
                  pl.BlockSpec(wb.shape, lambda b, c: (0, 0, 0)), pl.BlockSpec(wc.shape, lambda b, c: (0, 0, 0)),
                  pl.BlockSpec(ab.shape, lambda b, c: (0, 0)),
                  pl.BlockSpec((1, 1, 2, NST), lambda b, c: (b, nC - 1 - c, 0, 0)),
                  pl.BlockSpec((R, 2 * NST), rmap)],
        out_specs=[pl.BlockSpec((R, SW), rmap), pl.BlockSpec(wb.shape, lambda b, c: (0, 0, 0)),
                   pl.BlockSpec(wc.shape, lambda b, c: (0, 0, 0)), pl.BlockSpec((2, NST), lambda b, c: (0, 0))],
        out_shape=[jax.ShapeDtypeStruct((Bl * S, SW), F32), jax.ShapeDtypeStruct(wb.shape, F32),
                   jax.ShapeDtypeStruct(wc.shape, F32), jax.ShapeDtypeStruct((2, NST), F32)],
        scratch_shapes=[pltpu.VMEM((R, 2 * NST), F32), pltpu.VMEM((2, NST), F32)],
        compiler_params=_params(("arbitrary", "arbitrary")),
    )(u, dy, wb, wc, ab, st, xs)


def _s5_disc_math(a_re, a_im, log_dt, b_re, b_im, expand):
    dt = jnp.exp(log_dt)
    z_re, z_im = a_re * dt, a_im * dt
    mag = jnp.exp(z_re)
    ab_re, ab_im = mag * jnp.cos(z_im), mag * jnp.sin(z_im)
    den = a_re * a_re + a_im * a_im
    q_re = ((ab_re - 1.0) * a_re + ab_im * a_im) / den
    q_im = (ab_im * a_re - (ab_re - 1.0) * a_im) / den
    qe_re = jnp.dot(q_re, expand, preferred_element_type=F32, precision=HIGHEST)
    qe_im = jnp.dot(q_im, expand, preferred_element_type=F32, precision=HIGHEST)
    return ab_re, ab_im, qe_re * b_re - qe_im * b_im, qe_re * b_im + qe_im * b_re


def _whole(shape):
    return pl.BlockSpec(shape, lambda nd=len(shape): (0,) * nd)


def _s5_disc(a_re, a_im, log_dt, b_re, b_im, expand):
    def body(a, b, c, d, e, f, o0, o1, o2, o3):
        res = _s5_disc_math(a[...], b[...], c[...], d[...], e[...], f[...])
        for o, v in zip((o0, o1, o2, o3), res):
            o[...] = v
    ins = (a_re, a_im, log_dt, b_re, b_im, expand)
    outs = [jax.ShapeDtypeStruct(a_re.shape, F32)] * 2 + [jax.ShapeDtypeStruct(b_re.shape, F32)] * 2
    return pl.pallas_call(body, name="s5_disc", in_specs=[_whole(x.shape) for x in ins],
                          out_specs=[_whole(o.shape) for o in outs], out_shape=outs)(*ins)


def _s5_disc_bwd(a_re, a_im, log_dt, b_re, b_im, expand, cts):
    def body(a, b, c, d, e, f, g0, g1, g2, g3, o0, o1, o2, o3, o4):
        fn = lambda *p: _s5_disc_math(*p, f[...])
        _, vjp = jax.vjp(fn, a[...], b[...], c[...], d[...], e[...])
        for o, v in zip((o0, o1, o2, o3, o4), vjp((g0[...], g1[...], g2[...], g3[...]))):
            o[...] = v
    ins = (a_re, a_im, log_dt, b_re, b_im, expand) + tuple(cts)
    outs = [jax.ShapeDtypeStruct(x.shape, F32) for x in (a_re, a_im, log_dt, b_re, b_im)]
    return pl.pallas_call(body, name="s5_disc_bwd", in_specs=[_whole(x.shape) for x in ins],
                          out_specs=[_whole(o.shape) for o in outs], out_shape=outs)(*ins)


def _ada_fwd(c_all, w_shard, b_shard):
    def body(c_ref, w_ref, b_ref, o_ref):
        cv = c_ref[...]
        o_ref[...] = _dotm(cv * _sigmoid(cv), w_ref[...]) + b_ref[...]
    n = w_shard.shape[1]
    return pl.pallas_call(
        body, name="ada_fwd", in_specs=[_whole(c_all.shape), _whole(w_shard.shape), _whole(b_shard.shape)],
        out_specs=_whole((c_all.shape[0], n)), out_shape=jax.ShapeDtypeStruct((c_all.shape[0], n), F32),
        compiler_params=_params(),
    )(c_all, w_shard, b_shard)


def _ada_bwd(c_all, dmod_cols, dmod_all):
    def body(c_ref, dc_ref, da_ref, gw_ref, gb_ref):
        cv = c_ref[...]
        gw_ref[...] = lax.dot_general((cv * _sigmoid(cv)).astype(MXU_DTYPE), dc_ref[...].astype(MXU_DTYPE),
                                      (((0,), (0,)), ((), ())), preferred_element_type=F32)
        gb_ref[...] = jnp.sum(da_ref[...], axis=0, keepdims=True)
    n = dmod_cols.shape[1]
    return pl.pallas_call(
        body, name="ada_bwd", in_specs=[_whole(c_all.shape), _whole(dmod_cols.shape), _whole(dmod_all.shape)],
        out_specs=[_whole((D, n)), _whole((1, dmod_all.shape[1]))],
        out_shape=[jax.ShapeDtypeStruct((D, n), F32), jax.ShapeDtypeStruct((1, dmod_all.shape[1]), F32)],
        compiler_params=_params(),
    )(c_all, dmod_cols, dmod_all)


def _rows_block(n_rows, cap=512):
    if n_rows <= cap:
        return n_rows
    for t in range(cap - cap % SUBLANES, 0, -SUBLANES):
        if n_rows % t == 0:
            return t
    return n_rows


def _adamw(w, g, m, v, name):
    rows, cols = w.shape
    tr = _rows_block(rows, max(SUBLANES, (1 << 19) // max(cols, 1) // SUBLANES * SUBLANES))

    def body(w_ref, g_ref, m_ref, v_ref, d_ref, nm_ref, nv_ref):
        gv = g_ref[...]
        nm = B1 * m_ref[...] + (1.0 - B1) * gv
        nv = B2 * v_ref[...] + (1.0 - B2) * (gv * gv)
        m_hat = nm / (1.0 - B1 ** STEP)
        v_hat = nv / (1.0 - B2 ** STEP)
        d_ref[...] = -LR * (m_hat / (jnp.sqrt(v_hat) + ADAM_EPS) + WD * w_ref[...])
        nm_ref[...] = nm
        nv_ref[...] = nv

    spec = pl.BlockSpec((tr, cols), lambda i: (i, 0))
    sd = jax.ShapeDtypeStruct((rows, cols), F32)
    return pl.pallas_call(body, name=name, grid=(rows // tr,), in_specs=[spec] * 4, out_specs=[spec] * 3,
                          out_shape=[sd] * 3, compiler_params=_params(("parallel",)))(w, g, m, v)


def _sum_slots(x, out_dtype, name):
    xs = x if isinstance(x, (list, tuple)) else [x]
    _, rows, cols = xs[0].shape
    tr = _rows_block(rows)

    def body(*refs):
        acc = None
        for x_ref in refs[:-1]:
            for j in range(x_ref.shape[0]):
                term = x_ref[j].astype(F32)
                acc = term if acc is None else acc + term
        refs[-1][...] = acc.astype(refs[-1].dtype)

    return pl.pallas_call(
        body, name=name, grid=(rows // tr,),
        in_specs=[pl.BlockSpec((z.shape[0], tr, cols), lambda i: (0, i, 0)) for z in xs],
        out_specs=pl.BlockSpec((tr, cols), lambda i: (i, 0)), out_shape=jax.ShapeDtypeStruct((rows, cols), out_dtype),
        compiler_params=_params(("parallel",)))(*xs)


PACK_COLS = 1024


def _pack_rows(parts, dtype, row_mult):
    flat = jnp.concatenate([p.reshape(-1).astype(dtype) for p in parts])
    per = PACK_COLS * row_mult
    n = -(-flat.shape[0] // per) * per
    flat = jnp.pad(flat, (0, n - flat.shape[0]))
    return flat.reshape(n // PACK_COLS, PACK_COLS)


def _unpack(flat, shapes):
    out, off = [], 0
    for s in shapes:
        n = math.prod(s)
        out.append(flat[off:off + n].reshape(s))
        off += n
    return out


BIG = (("w_in", (D, SHIFT + SW + 2 * D), 1), ("w_out_rwkv", (RW, D), 1), ("w_glu", (SW, 2 * D), 1),
       ("w_out", (D, D), 0), ("w_ffn_up", (D, 2 * DFF), 1), ("w_ffn_down", (DFF, D), 0))
BIG_SMALL = (("rwkv_w_up", (LW, RW), 1), ("rwkv_a_up", (LA, RW), 1), ("rwkv_g_up", (LG, RW), 1),
             ("ffn_conv_w", (3, 2 * DFF), 1))
BIG_LATE = BIG[4:]


def _shard_shape(shape, axis):
    return (shape[0] // 4, shape[1]) if axis == 0 else (shape[0], shape[1] // 4)


def _to_shards(g, axis):
    r, C = g.shape
    return g.reshape(4, r // 4, C) if axis == 0 else g.reshape(r, 4, C // 4).transpose(1, 0, 2)


def _from_shards(x, axis):
    _, r, C = x.shape
    return x.reshape(4 * r, C) if axis == 0 else x.transpose(1, 0, 2).reshape(r, 4 * C)


def kernel(x, c, w_ada, b_ada, norm1_g, w_in, mu_shift, rwkv_w0, rwkv_w_up, rwkv_a0, rwkv_a_up, rwkv_g_up, rwkv_k_k, rwkv_k_a, rwkv_r_k, rwkv_ln_g, rwkv_ln_b, w_out_rwkv, s5_a_re, s5_a_im, s5_log_dt, s5_b_re, s5_b_im, s5_c_re, s5_c_im, s5_d, w_glu, w_out, norm2_g, w_ffn_up, ffn_conv_w, ffn_conv_b, w_ffn_down, norm_f_g, loss_target, m_w_ada, m_b_ada, m_norm1_g, m_w_in, m_mu_shift, m_rwkv_w0, m_rwkv_w_up, m_rwkv_a0, m_rwkv_a_up, m_rwkv_g_up, m_rwkv_k_k, m_rwkv_k_a, m_rwkv_r_k, m_rwkv_ln_g, m_rwkv_ln_b, m_w_out_rwkv, m_s5_a_re, m_s5_a_im, m_s5_log_dt, m_s5_b_re, m_s5_b_im, m_s5_c_re, m_s5_c_im, m_s5_d, m_w_glu, m_w_out, m_norm2_g, m_w_ffn_up, m_ffn_conv_w, m_ffn_conv_b, m_w_ffn_down, m_norm_f_g, v_w_ada, v_b_ada, v_norm1_g, v_w_in, v_mu_shift, v_rwkv_w0, v_rwkv_w_up, v_rwkv_a0, v_rwkv_a_up, v_rwkv_g_up, v_rwkv_k_k, v_rwkv_k_a, v_rwkv_r_k, v_rwkv_ln_g, v_rwkv_ln_b, v_w_out_rwkv, v_s5_a_re, v_s5_a_im, v_s5_log_dt, v_s5_b_re, v_s5_b_im, v_s5_c_re, v_s5_c_im, v_s5_d, v_w_glu, v_w_out, v_norm2_g, v_w_ffn_up, v_ffn_conv_w, v_ffn_conv_b, v_w_ffn_down, v_norm_f_g):
    names = ["w_ada", "b_ada", "norm1_g", "w_in", "mu_shift", "rwkv_w0", "rwkv_w_up", "rwkv_a0", "rwkv_a_up",
             "rwkv_g_up", "rwkv_k_k", "rwkv_k_a", "rwkv_r_k", "rwkv_ln_g", "rwkv_ln_b", "w_out_rwkv", "s5_a_re",
             "s5_a_im", "s5_log_dt", "s5_b_re", "s5_b_im", "s5_c_re", "s5_c_im", "s5_d", "w_glu", "w_out", "norm2_g",
             "w_ffn_up", "ffn_conv_w", "ffn_conv_b", "w_ffn_down", "norm_f_g"]
    env = dict(locals())
    W = {n: env[n] for n in names}
    M = {n: env["m_" + n] for n in names}
    V = {n: env["v_" + n] for n in names}

    Bl, S, _ = x.shape
    T = Bl * S
    ix, iy, ic = lax.axis_index("x"), lax.axis_index("y"), lax.axis_index("c")
    chip = 2 * ix + iy
    dev = 2 * chip + ic
    rw = functools.partial(_rowwise, Bl=Bl, S=S)

    late = [W[n][0].astype(MXU_DTYPE) for n, _, _ in BIG_LATE]
    late_moves = [(i, i, lambda ref, me, peer: ref, lambda ref, me, k: ref.at[_chip_of(me)]) for i in range(len(late))]
    late_start = _send_start("gather_ffn_start", CHIP_FLIPS, late,
                             [jax.ShapeDtypeStruct((4,) + z.shape, z.dtype) for z in late], late_moves)
    norm1_g = norm1_g + late_start["token"]
    now = [b for b in BIG if b not in BIG_LATE]
    chip_arrs = [W[n][0].astype(MXU_DTYPE) for n, _, _ in now] + [W[n][0] for n, _, _ in BIG_SMALL[:3]]
    got_chip, got_dev = _gather_two_level(chip_arrs, [W["ffn_conv_w"][0], c], "gather_w")
    full = {n: _from_shards(g, axis) for (n, _, axis), g in zip(tuple(now) + BIG_SMALL[:3], got_chip)}
    full["ffn_conv_w"] = _from_shards(got_dev[0][:, 0], 1)
    c_all = got_dev[1].reshape(8 * Bl, D)
    w_p, w_u, w_g = full["w_in"][:, :SHIFT], full["w_in"][:, SHIFT:SHIFT + SW], full["w_in"][:, SHIFT + SW:]
    zeros_l = jnp.zeros((LW, RW), F32)
    w_up_p = jnp.concatenate([full["rwkv_w_up"], zeros_l], axis=0)
    a_up_p = jnp.concatenate([zeros_l, full["rwkv_a_up"]], axis=0)
    g_up = full["rwkv_g_up"]
    conv_w = full["ffn_conv_w"]
    conv_wg, conv_wu = conv_w[:, :DFF], conv_w[:, DFF:]
    conv_bg, conv_bu = ffn_conv_b[:, :DFF], ffn_conv_b[:, DFF:]
    hm = jnp.kron(jnp.eye(NH, dtype=F32), jnp.ones((HD, HD), F32))

    ncol = 6 * D // 4
    b_ada_cols = lax.dynamic_slice_in_dim(b_ada, chip * ncol, ncol, 1)
    mod_part = _ada_fwd(c_all, w_ada[0], b_ada_cols)
    mod4 = _gather_two_level([], [mod_part], "gather_mod")[1][0][:, 0]
    mod = lax.dynamic_slice_in_dim(mod4, dev * Bl, Bl, 1).transpose(1, 0, 2).reshape(Bl, 1, 6 * D)
    SH1, SC1, GT1, SH2, SC2, GT2 = range(6)

    x2d = x.reshape(T, D)
    tgt = loss_target.reshape(T, D)

    (h1,) = rw("norm1", lambda xv, sc, sh, g: _norm_mod(xv, g, sc, sh), R=256, tiled=[(x2d, D, 0)],
               batch=[(mod, D, SC1), (mod, D, SH1)], full=[norm1_g], out_tiled=[(D, MXU_DTYPE)])
    p = _mm([h1], [w_p], F32, "proj_p")
    u = _mm([h1], [w_u], F32, "proj_u")
    gates = _mm([h1], [w_g], F32, "proj_g")

    prep_params = [rwkv_w0, w_up_p, rwkv_a0, a_up_p, g_up, rwkv_k_k, rwkv_k_a, hm]

    def prep_fwd(pv, ph, mu, *pp):
        ps = pv + (_shift_down(pv, ph, 1) - pv) * mu
        return _rwkv_prep(*_split_ps(ps), *pp)

    r_, w_, k_, v_, a_, b_, g_ = rw("rwkv_prep", prep_fwd, R=256, tiled=[(p, SHIFT, 0)], prev=[(p, SHIFT, 0)],
                                    full=[mu_shift] + prep_params, out_tiled=[(RW, F32)] * 7)
    y_wkv, ck = _wkv_fwd(r_, w_, k_, v_, a_, b_, Bl, S)
    r_k_row = rwkv_r_k.reshape(1, RW)
    post_params = [rwkv_ln_g, rwkv_ln_b, r_k_row, hm]
    (o_rwkv,) = rw("rwkv_post", _rwkv_post, R=256,
                   tiled=[(y_wkv, RW, 0), (r_, RW, 0), (k_, RW, 0), (v_, RW, 0), (g_, RW, 0)],
                   full=post_params, out_tiled=[(RW, MXU_DTYPE)])
    y_a = _mm([o_rwkv], [full["w_out_rwkv"]], F32, "out_rwkv")

    expand = jnp.kron(jnp.eye(SP, dtype=F32), jnp.ones((1, SGC), F32))
    s5_in = (s5_a_re[0], s5_a_im[0], s5_log_dt[0].reshape(NG, 1), s5_b_re[0].reshape(NG, SP * SGC),
             s5_b_im[0].reshape(NG, SP * SGC), expand)
    ab_re, ab_im, bb_re, bb_im = _s5_disc(*s5_in)
    eye8 = jnp.eye(8, dtype=F32)

    def blockdiag_in(bb):
        t = bb.reshape(NSG, 8, SP, SGC)
        return jnp.einsum("ab,sapc->sacbp", eye8, t).reshape(NSG, 128, 512)

    def blockdiag_out(cc):
        t = cc.reshape(NSG, 8, SGC, SP)
        return jnp.einsum("ab,sacp->sapbc", eye8, t).reshape(NSG, 512, 128)

    wb = jnp.concatenate([blockdiag_in(bb_re), blockdiag_in(bb_im)], axis=2).astype(MXU_DTYPE)
    wc = jnp.concatenate([blockdiag_out(s5_c_re[0]), -blockdiag_out(s5_c_im[0])], axis=1).astype(MXU_DTYPE)
    ab = jnp.stack([ab_re.reshape(NST), ab_im.reshape(NST)])
    y_ssm, s5_st, s5_x = _s5_fwd(u, wb, wc, ab, Bl, S)
    (s5o,) = rw("s5_post", _s5_post, R=256, tiled=[(y_ssm, SW, 0), (u, SW, 0)], full=[s5_d],
                out_tiled=[(SW, MXU_DTYPE)])
    z = _mm([s5o], [full["w_glu"]], F32, "glu")
    mix_tiled = [(gates, D, 0), (gates, D, 1), (y_a, D, 0), (z, D, 0), (z, D, 1)]
    (mixed_in,) = rw("mix", _mix, R=256, tiled=mix_tiled, out_tiled=[(D, MXU_DTYPE)])
    mixed = _mm([mixed_in], [full["w_out"]], F32, "out_proj")

    def norm2_fwd(xv, mx, gt, sc, sh, g):
        x1 = xv + gt * mx
        return x1, _norm_mod(x1, g, sc, sh)

    x1, h2 = rw("norm2", norm2_fwd, R=256, tiled=[(x2d, D, 0), (mixed, D, 0)],
                batch=[(mod, D, GT1), (mod, D, SC2), (mod, D, SH2)], full=[norm2_g],
                out_tiled=[(D, F32), (D, MXU_DTYPE)])
    late_own, late_got = _send_wait("gather_ffn_wait", CHIP_FLIPS, late_start, late_moves, h2)
    for (n, _, axis), own, got in zip(BIG_LATE, late_own, late_got):
        full[n] = _from_shards(lax.dynamic_update_slice(got, own[None], (chip, 0, 0)), axis)
    up = _mm([h2], [full["w_ffn_up"]], MXU_DTYPE, "ffn_up")
    conv_tiled = [(up, DFF, 0), (up, DFF, 1)]
    conv_full = [conv_wg, conv_wu, conv_bg, conv_bu]

    def act_fwd(*a):
        return _silu_gate(*_conv_act(*a))

    (act,) = rw("ffn_act", act_fwd, R=128, tiled=conv_tiled, prev=conv_tiled, full=conv_full,
                out_tiled=[(DFF, MXU_DTYPE)])
    ffn = _mm([act], [full["w_ffn_down"]], F32, "ffn_down")

    def head(x1v, fv, tv, gt, g):
        x2 = x1v + gt * fv
        y, vjp = jax.vjp(_rms, x2, g)
        e = y - tv
        dx2, dg = vjp(e * (1.0 / D))
        loss = jnp.sum(e * e, keepdims=True) * jnp.ones((1, LANES), F32)
        return dx2, dx2 * gt, jnp.sum(dx2 * fv, axis=0, keepdims=True), dg.reshape(1, D), loss

    dx2, d_ffn, d_gt2, g_norm_f, loss_acc = rw(
        "head", head, R=256, tiled=[(x1, D, 0), (ffn, D, 0), (tgt, D, 0)], batch=[(mod, D, GT2)],
        full=[norm_f_g.reshape(1, D)], out_tiled=[(D, F32), (D, MXU_DTYPE)], out_batch=[D],
        out_acc=[(1, D), (1, LANES)])
    loss = lax.psum(0.5 / D * loss_acc[0, 0], ("x", "y", "c"))

    d_act = _mm([d_ffn], [full["w_ffn_down"]], F32, "d_act", bt=True)
    g_w_ffn_down = _mm_tn(act, d_ffn, "g_ffn_down")

    def act_bwd(ug, uu, dact, hg, hu, wg, wu, bg, bu):
        ug, uu, hg, hu = (z.astype(F32) for z in (ug, uu, hg, hu))
        gate, upv = _conv_act(ug, uu, hg, hu, wg, wu, bg, bu)
        _, vjp_s = jax.vjp(_silu_gate, gate, upv)
        d_gate, d_upv = vjp_s(dact)
        def taps(dh, xv, h):
            return [jnp.sum(dh * _shift_down(xv, h, 2), axis=0, keepdims=True),
                    jnp.sum(dh * _shift_down(xv, h, 1), axis=0, keepdims=True),
                    jnp.sum(dh * xv, axis=0, keepdims=True), jnp.sum(dh, axis=0, keepdims=True)]
        return (d_gate, d_upv, *taps(d_gate, ug, hg), *taps(d_upv, uu, hu))

    dh_g, dh_u, *tapg = rw(
        "ffn_act_bwd", act_bwd, R=128, tiled=conv_tiled + [(d_act, DFF, 0)], prev=conv_tiled, full=conv_full,
        out_tiled=[(DFF, MXU_DTYPE), (DFF, MXU_DTYPE)], out_acc=[(1, DFF)] * 8)
    g_cw_g, g_cb_g = jnp.concatenate(tapg[0:3], axis=0), tapg[3]
    g_cw_u, g_cb_u = jnp.concatenate(tapg[4:7], axis=0), tapg[7]

    def conv_t(dg, du_, ng, nu, wg, wu):
        dg, du_, ng, nu = (z.astype(F32) for z in (dg, du_, ng, nu))

        def ct(d, n, w):
            return w[2:3] * d + w[1:2] * _shift_up(d, n, 1) + w[0:1] * _shift_up(d, n, 2)
        return jnp.concatenate([ct(dg, ng, wg), ct(du_, nu, wu)], axis=1)

    (d_up,) = rw("conv_bwd", conv_t, R=128, tiled=[(dh_g, DFF, 0), (dh_u, DFF, 0)],
                 nxt=[(dh_g, DFF, 0), (dh_u, DFF, 0)], full=[conv_wg, conv_wu], out_tiled=[(2 * DFF, MXU_DTYPE)])
    d_h2 = _mm([d_up], [full["w_ffn_up"]], F32, "d_h2", bt=True)
    g_w_ffn_up = _mm_tn(h2, d_up, "g_ffn_up")

    sds = jax.ShapeDtypeStruct
    reduce_src = lambda r: (lambda ref, me, peer: ref.at[_chip_of(peer), _half(r, peer[2])])
    gsh_late = [_to_shards(g, ax).astype(MXU_DTYPE) for g, (_, _, ax) in zip((g_w_ffn_up, g_w_ffn_down), BIG_LATE)]
    rsl_moves = [(i, i, reduce_src(g.shape[1]), lambda ref, me, k: ref.at[k]) for i, g in enumerate(gsh_late)]
    rsl = _send_start("rs_ffn_start", ALL_FLIPS, gsh_late,
                      [sds((len(ALL_FLIPS), g.shape[1] // 2, g.shape[2]), MXU_DTYPE) for g in gsh_late], rsl_moves)
    norm2_g = norm2_g + rsl["token"]

    def norm2_bwd(x1v, dh2, dx2v, mx, gt, sc, sh, g):
        _, vjp = jax.vjp(_norm_mod, x1v, g, sc, sh)
        dxn, dg, dsc, dsh = vjp(dh2)
        dx1 = dx2v + dxn
        return dx1, dx1 * gt, jnp.sum(dx1 * mx, axis=0, keepdims=True), dsc, dsh, dg

    dx1, d_mixed, d_gt1, d_sc2, d_sh2, g_norm2 = rw(
        "norm2_bwd", norm2_bwd, R=256, tiled=[(x1, D, 0), (d_h2, D, 0), (dx2, D, 0), (mixed, D, 0)],
        batch=[(mod, D, GT1), (mod, D, SC2), (mod, D, SH2)], full=[norm2_g],
        out_tiled=[(D, F32), (D, MXU_DTYPE)], out_batch=[D, D, D], out_acc=[(1, D)])

    d_mixed_in = _mm([d_mixed], [full["w_out"]], F32, "d_mixed_in", bt=True)
    g_w_out = _mm_tn(mixed_in, d_mixed, "g_w_out")

    def mix_bwd(ga, gb, ya, za, zb, dm):
        _, vjp = jax.vjp(_mix, ga, gb, ya, za, zb)
        dga, dgb, dya, dza, dzb = vjp(dm)
        return jnp.concatenate([dga, dgb], axis=1), dya, jnp.concatenate([dza, dzb], axis=1)

    d_gates, d_ya, d_z = rw("mix_bwd", mix_bwd, R=256, tiled=mix_tiled + [(d_mixed_in, D, 0)],
                            out_tiled=[(2 * D, MXU_DTYPE), (D, MXU_DTYPE), (2 * D, MXU_DTYPE)])
    d_o_rwkv = _mm([d_ya], [full["w_out_rwkv"]], F32, "d_o_rwkv", bt=True)
    g_w_out_rwkv = _mm_tn(o_rwkv, d_ya, "g_out_rwkv")
    d_s5o = _mm([d_z], [full["w_glu"]], F32, "d_s5o", bt=True)
    g_w_glu = _mm_tn(s5o, d_z, "g_glu")

    def s5_post_bwd(ys, uv, ds, dd):
        _, vjp = jax.vjp(_s5_post, ys, uv, dd)
        return vjp(ds)

    d_yssm, d_u_direct, g_s5_d = rw("s5_post_bwd", s5_post_bwd, R=256,
                                    tiled=[(y_ssm, SW, 0), (u, SW, 0), (d_s5o, SW, 0)], full=[s5_d],
                                    out_tiled=[(SW, F32), (SW, F32)], out_acc=[(1, SW)])
    d_u_ssm, d_wb, d_wc, d_ab = _s5_bwd(u, d_yssm, wb, wc, ab, s5_st, s5_x, Bl, S)

    def diag_in(dw):
        t = dw.reshape(NSG, 8, SGC, 8, SP)
        return jnp.einsum("ab,sacbp->sapc", eye8, t).reshape(NG, SP * SGC)

    def diag_out(dw):
        t = dw.reshape(NSG, 8, SP, 8, SGC)
        return jnp.einsum("ab,sapbc->sacp", eye8, t).reshape(NG, SGC, SP)

    g_s5_c_re = diag_out(d_wc[:, :512])
    g_s5_c_im = -diag_out(d_wc[:, 512:])
    disc_cts = (d_ab[0].reshape(NG, SP), d_ab[1].reshape(NG, SP), diag_in(d_wb[:, :, :512]), diag_in(d_wb[:, :, 512:]))
    g_a_re, g_a_im, g_log_dt, g_b_re, g_b_im = _s5_disc_bwd(*s5_in, disc_cts)

    def post_bwd(yv, rv, kv, vv, gv, do, *pp):
        _, vjp = jax.vjp(lambda *a: _rwkv_post(*a, pp[3]), yv, rv, kv, vv, gv, *pp[:3])
        return vjp(do)

    dy_wkv, dr_b, dk_b, dv_b, dg_, g_ln_g, g_ln_b, g_r_k = rw(
        "rwkv_post_bwd", post_bwd, R=256,
        tiled=[(y_wkv, RW, 0), (r_, RW, 0), (k_, RW, 0), (v_, RW, 0), (g_, RW, 0), (d_o_rwkv, RW, 0)],
        full=post_params, out_tiled=[(RW, F32)] * 5, out_acc=[(1, RW)] * 3)
    dr3, dw3, dk3, dv3, da3, db3 = _wkv_bwd(r_, w_, k_, v_, a_, b_, dy_wkv, ck, Bl, S)

    gsh_own, rsl_got = _send_wait("rs_ffn_wait", ALL_FLIPS, rsl, rsl_moves, dr3)
    g_half_late = []
    for i, (g, got) in enumerate(zip(gsh_own, rsl_got)):
        h = g.shape[1] // 2
        own = lax.dynamic_slice(g, (chip, ic * h, 0), (1, h, g.shape[2]))
        g_half_late.append(_sum_slots([own, got], F32, "rs_ffn_sum%d" % i))
    share_moves = [(i, i, lambda ref, me, peer: ref, lambda ref, me, k: ref) for i in range(len(g_half_late))]
    shl = _send_start("share_ffn_start", PAIR_FLIPS, g_half_late, [sds(g.shape, F32) for g in g_half_late], share_moves)
    mu_shift = mu_shift + shl["token"]

    def prep_bwd(pv, dr1, dr2, dwv, dk1, dk2, dv1, dv2, dav, dbv, dgv, ph, mu, *pp):
        prev = _shift_down(pv, ph, 1)
        ps = pv + (prev - pv) * mu
        _, vjp = jax.vjp(lambda *q: _rwkv_prep(*q, pp[7]), *_split_ps(ps), *pp[:7])
        grads = vjp((dr1 + dr2, dwv, dk1 + dk2, dv1 + dv2, dav, dbv, dgv))
        dps = jnp.concatenate(grads[:5], axis=1)
        return (dps,) + tuple(grads[5:]) + (jnp.sum(dps * (prev - pv), axis=0, keepdims=True),)

    prep_outs = rw(
        "rwkv_prep_bwd", prep_bwd, R=256,
        tiled=[(p, SHIFT, 0), (dr3, RW, 0), (dr_b, RW, 0), (dw3, RW, 0), (dk3, RW, 0), (dk_b, RW, 0),
               (dv3, RW, 0), (dv_b, RW, 0), (da3, RW, 0), (db3, RW, 0), (dg_, RW, 0)],
        prev=[(p, SHIFT, 0)], full=[mu_shift] + prep_params,
        out_tiled=[(SHIFT, F32)],
        out_acc=[(1, RW), (LW + LA, RW), (1, RW), (LW + LA, RW), (LG, RW), (1, RW), (1, RW), (1, SHIFT)])
    d_ps, g_w0, g_w_up_p, g_a0, g_a_up_p, g_g_up, g_k_k, g_k_a, g_mu = prep_outs

    def shift_bwd(dps, nx, mu):
        return dps * (1.0 - mu) + _shift_up(dps * mu, nx * mu, 1)

    (d_p,) = rw("shift_bwd", shift_bwd, R=256, tiled=[(d_ps, SHIFT, 0)], nxt=[(d_ps, SHIFT, 0)], full=[mu_shift],
                out_tiled=[(SHIFT, MXU_DTYPE)])
    (d_u,) = rw("d_u", lambda a1, a2: a1 + a2, R=256, tiled=[(d_u_direct, SW, 0), (d_u_ssm, SW, 0)],
                out_tiled=[(SW, MXU_DTYPE)])
    d_h1 = _mm([d_p, d_u, d_gates], [w_p, w_u, w_g], F32, "d_h1", bt=True)
    g_w_in = jnp.concatenate([_mm_tn(h1, d_p, "g_w_p"), _mm_tn(h1, d_u, "g_w_u"), _mm_tn(h1, d_gates, "g_w_g")], axis=1)

    def norm1_bwd(xv, dh1, dx1v, sc, sh, g):
        _, vjp = jax.vjp(_norm_mod, xv, g, sc, sh)
        dxn, dg, dsc, dsh = vjp(dh1)
        return dx1v + dxn, dsc, dsh, dg

    grad_x, d_sc1, d_sh1, g_norm1 = rw(
        "norm1_bwd", norm1_bwd, R=256, tiled=[(x2d, D, 0), (d_h1, D, 0), (dx1, D, 0)],
        batch=[(mod, D, SC1), (mod, D, SH1)], full=[norm1_g], out_tiled=[(D, F32)], out_batch=[D, D], out_acc=[(1, D)])

    dmod = jnp.concatenate([d_sh1, d_sc1, d_gt1, d_sh2, d_sc2, d_gt2], axis=2).reshape(Bl, 6 * D)
    dmod_all = _gather_two_level([], [dmod], "gather_dmod")[1][0].reshape(8 * Bl, 6 * D)
    dmod_cols = lax.dynamic_slice_in_dim(dmod_all, chip * ncol, ncol, 1)
    g_w_ada, g_b_ada = _ada_bwd(c_all, dmod_cols, dmod_all)

    small = {"norm1_g": g_norm1, "mu_shift": g_mu, "rwkv_w0": g_w0, "rwkv_a0": g_a0, "rwkv_k_k": g_k_k,
             "rwkv_k_a": g_k_a, "rwkv_r_k": g_r_k, "rwkv_ln_g": g_ln_g, "rwkv_ln_b": g_ln_b, "s5_a_re": g_a_re,
             "s5_a_im": g_a_im, "s5_log_dt": g_log_dt, "s5_b_re": g_b_re, "s5_b_im": g_b_im, "s5_c_re": g_s5_c_re,
             "s5_c_im": g_s5_c_im, "s5_d": g_s5_d, "norm2_g": g_norm2,
             "ffn_conv_b": jnp.concatenate([g_cb_g, g_cb_u], axis=1), "norm_f_g": g_norm_f}
    small_names = list(small)
    g_conv_w = jnp.concatenate([g_cw_g, g_cw_u], axis=1)
    shard_small = {"rwkv_w_up": g_w_up_p[:LW], "rwkv_a_up": g_a_up_p[LW:], "rwkv_g_up": g_g_up, "ffn_conv_w": g_conv_w}
    parts = [small[n] for n in small_names] + [_to_shards(shard_small[n], ax) for n, _, ax in BIG_SMALL]
    spack = _pack_rows(parts, F32, SUBLANES)
    s_all = _gather_two_level([], [spack], "gather_gsmall")[1][0]
    s_sum = _sum_slots(s_all.reshape((8,) + spack.shape), F32, "sum_gsmall").reshape(-1)
    grads = {}
    off = 0
    for n in small_names:
        grads[n] = s_sum[off:off + W[n].size].reshape(W[n].shape)
        off += W[n].size
    for n, shape, axis in BIG_SMALL:
        ss = _shard_shape(shape, axis)
        k4 = 4 * math.prod(ss)
        sh4 = s_sum[off:off + k4].reshape(4, math.prod(ss))
        grads[n] = lax.dynamic_index_in_dim(sh4, chip, 0, keepdims=False).reshape((1,) + ss)
        off += k4

    big_g = {"w_in": g_w_in, "w_out_rwkv": g_w_out_rwkv, "w_glu": g_w_glu, "w_out": g_w_out}
    gsh = [_to_shards(big_g[n], ax).astype(MXU_DTYPE) for n, _, ax in now]
    moves = [(i, i, reduce_src(g.shape[1]), lambda ref, me: ref.at[2 * _chip_of(me) + me[2]]) for i, g in enumerate(gsh)]
    recv = _exchange("rs_all", ALL_FLIPS, gsh, [sds((8, g.shape[1] // 2, g.shape[2]), MXU_DTYPE) for g in gsh], moves)
    g_half = [_sum_slots(rv, F32, "rs_sum%d" % i) for i, rv in enumerate(recv)]
    moves = [(i, i, lambda ref, me, peer: ref, lambda ref, me, r=2 * g.shape[0]: ref.at[_half(r, me[2])])
             for i, g in enumerate(g_half)]
    g_full = _exchange("rs_share", PAIR_FLIPS, g_half, [sds((2 * g.shape[0], g.shape[1]), F32) for g in g_half], moves)
    for (n, _, _), g in zip(now, g_full):
        grads[n] = g[None]
    mine_h, got_h = _send_wait("share_ffn_wait", PAIR_FLIPS, shl, share_moves, g_full[0])
    for (n, _, _), mh, gh in zip(BIG_LATE, mine_h, got_h):
        grads[n] = jnp.concatenate([jnp.where(ic == 0, mh, gh), jnp.where(ic == 0, gh, mh)], axis=0)[None]
    grads["w_ada"] = g_w_ada[None]
    grads["b_ada"] = g_b_ada

    delta, new_m, new_v = {}, {}, {}
    to2 = lambda z: z.reshape(-1, z.shape[-1])
    for n in ["w_ada"] + [b[0] for b in BIG]:
        d_, m_, v2_ = _adamw(to2(W[n]), to2(grads[n]), to2(M[n]), to2(V[n]), "adamw_" + n)
        delta[n], new_m[n], new_v[n] = (z.reshape(W[n].shape) for z in (d_, m_, v2_))
    rest = [n for n in names if n not in delta]
    packs = [_pack_rows([src[n] for n in rest], F32, SUBLANES) for src in (W, grads, M, V)]
    d_, m_, v2_ = _adamw(*packs, "adamw_small")
    shapes = [W[n].shape for n in rest]
    for dst, z in ((delta, d_), (new_m, m_), (new_v, v2_)):
        for n, val in zip(rest, _unpack(z.reshape(-1), shapes)):
            dst[n] = val

    return (loss, grad_x.reshape(Bl, S, D), *[grads[n] for n in names], *[delta[n] for n in names],
            *[new_m[n] for n in names], *[new_v[n] for n in names])
```

```python
import functools
import math

import jax
import jax.numpy as jnp
from jax import lax
from jax.experimental import pallas as pl
from jax.experimental.pallas import tpu as pltpu

F32 = jnp.float32
BF16 = jnp.bfloat16
MXU_DTYPE = jnp.bfloat16
MESH_IDS = pl.DeviceIdType.MESH
HIGHEST = lax.Precision.HIGHEST

D = 1024
RW, NH, HD = 512, 8, 64
LW, LA, LG = 64, 64, 128
SW, SGC, NG, SP = 512, 16, 32, 64
NSG = 4
SHIFT = 3 * RW + LW + LA + LG
DFF = 2816
RMS_EPS, GN_EPS, L2_EPS = 1e-6, 64e-5, 1e-12
LR, B1, B2, ADAM_EPS, WD, STEP = 0.001, 0.9, 0.999, 1e-8, 0.01, 10
DECAY_SCALE = math.exp(-0.5)
GELU_C = math.sqrt(2.0 / math.pi)

VMEM_LIMIT = 52 * 1024 * 1024
SUBLANES, LANES = 8, 128
HALO = 16


def _pick(n, cap):
    if n <= cap:
        return n
    best = None
    for t in range(LANES, cap + 1, LANES):
        if n % t == 0:
            best = t
    assert best is not None, (n, cap)
    return best


def _params(sem=None, vmem=VMEM_LIMIT):
    return pltpu.CompilerParams(dimension_semantics=sem, vmem_limit_bytes=vmem)


def _chip_of(p):
    return 2 * p[0] + p[1]


def _me():
    return (lax.axis_index("x"), lax.axis_index("y"), lax.axis_index("c"))


def _half(rows, core):
    h = rows // 2
    return pl.ds(pl.multiple_of(core * h, 16 if h % 16 == 0 else SUBLANES), h)


_HBM =pl.BlockSpec(memory_space=pltpu.HBM)
_SEM = pl.BlockSpec(memory_space=pltpu.SEMAPHORE)
_DATAFLOW = pltpu.SideEffectType.DATAFLOW_SIDE_EFFECTING


def _split_copies(flips, moves, src_refs, land_refs, send_sems, recv_sems):
    me = _me()
    nf = len(flips)
    out = []
    for m, (si, li, src_sel, dst_sel) in enumerate(moves):
        for k, f in enumerate(flips):
            peer = tuple(1 - v if b else v for v, b in zip(me, f))
            out.append(pltpu.make_async_remote_copy(
                src_ref=src_sel(src_refs[si], me, peer), dst_ref=dst_sel(land_refs[li], me, k),
                send_sem=send_sems.at[m * nf + k], recv_sem=recv_sems.at[m * nf + k],
                device_id=peer, device_id_type=MESH_IDS))
    return out


def _send_start(name, flips, srcs, land_shapes, moves):
    ns, nl = len(srcs), len(land_shapes)
    n = len(moves) * len(flips)

    def body(*refs):
        for cp in _split_copies(flips, moves, refs[:ns], refs[ns:ns + nl], refs[ns + nl], refs[ns + nl + 1]):
            cp.start()
        refs[-1][...] = jnp.zeros(refs[-1].shape, F32)

    hbm = lambda z: pltpu.with_memory_space_constraint(z, pltpu.HBM)
    lands = [lax.empty(s.shape, s.dtype) for s in land_shapes]
    res = pl.pallas_call(
        body, name=name,
        out_shape=(pltpu.SemaphoreType.DMA((n,)), pltpu.SemaphoreType.DMA((n,)),
                   *[pltpu.HBM(z.shape, z.dtype) for z in srcs], *[pltpu.HBM(s.shape, s.dtype) for s in land_shapes],
                   jax.ShapeDtypeStruct((SUBLANES, LANES), F32)),
        in_specs=[_HBM] * (ns + nl),
        out_specs=(_SEM, _SEM, *[_HBM] * (ns + nl), pl.BlockSpec(memory_space=pltpu.VMEM)),
        input_output_aliases={i: 2 + i for i in range(ns + nl)},
        compiler_params=pltpu.CompilerParams(has_side_effects=_DATAFLOW),
    )(*[hbm(z) for z in srcs], *[hbm(z) for z in lands])
    return {"sems": res[:2], "srcs": list(res[2:2 + ns]), "lands": list(res[2 + ns:2 + ns + nl]), "token": res[-1][0, 0]}


def _send_wait(name, flips, started, moves, after):
    srcs, lands = started["srcs"], started["lands"]
    ns, nl = len(srcs), len(lands)

    def body(*refs):
        for cp in _split_copies(flips, moves, refs[:ns], refs[ns:ns + nl], refs[ns + nl], refs[ns + nl + 1]):
            cp.wait_send()
            cp.wait_recv()

    res = pl.pallas_call(
        body, name=name, out_shape=[pltpu.HBM(z.shape, z.dtype) for z in srcs + lands],
        in_specs=[_HBM] * (ns + nl) + [_SEM, _SEM, pl.BlockSpec(memory_space=pl.ANY)],
        out_specs=[_HBM] * (ns + nl), input_output_aliases={i: i for i in range(ns + nl)},
        compiler_params=pltpu.CompilerParams(has_side_effects=_DATAFLOW),
    )(*srcs, *lands, *started["sems"], after)
    return list(res[:ns]), list(res[ns:])


CHIP_FLIPS = ((1, 0, 0), (0, 1, 0), (1, 1, 0))
PAIR_FLIPS = ((0, 0, 1),)
ALL_FLIPS = CHIP_FLIPS + ((1, 0, 1), (0, 1, 1), (1, 1, 1)) + PAIR_FLIPS


def _gather_two_level(chip_arrs, dev_arrs, name):
    arrs = list(chip_arrs) + list(dev_arrs)
    n, nchip = len(arrs), len(chip_arrs)
    NS = 7

    def body(*refs):
        srcs, outs = refs[:n], refs[n:2 * n]
        send_sems, recv_sems, loc_sems = refs[2 * n:]
        x, y, c = _me()
        sib = (x, y, 1 - c)
        chips = [(1 - x, y), (x, 1 - y), (1 - x, 1 - y)]
        mine = 2 * x + y
        ids = [2 * cx + cy for cx, cy in chips]

        def part(i, slot, core):
            if i < nchip:
                return outs[i].at[slot, _half(arrs[i].shape[0], core)]
            return outs[i].at[slot, core]

        def rcopy(i, k, src, dst, to):
            return pltpu.make_async_remote_copy(src_ref=src, dst_ref=dst, send_sem=send_sems.at[i * NS + k],
                                                recv_sem=recv_sems.at[i * NS + k], device_id=to, device_id_type=MESH_IDS)

        started, locs = [], []
        for i in range(n):
            own = srcs[i].at[_half(arrs[i].shape[0], c)] if i < nchip else srcs[i]
            loc = pltpu.make_async_copy(srcs[i], outs[i].at[mine] if i < nchip else outs[i].at[mine, c], loc_sems.at[i])
            loc.start()
            locs.append(loc)
            for f, chip in enumerate(chips):
                cp = rcopy(i, f, own, part(i, mine, c), (*chip, c))
                cp.start()
                started.append(cp)
            if i >= nchip:
                cp = rcopy(i, 6, own, part(i, mine, c), sib)
                cp.start()
                started.append(cp)
        for i in range(n):
            for f in range(3):
                land = part(i, ids[f], c)
                rcopy(i, f, land, land, sib).wait_recv()
                fw = rcopy(i, 3 + f, land, land, sib)
                fw.start()
                started.append(fw)
        for i in range(n):
            for f in range(3):
                land = part(i, ids[f], 1 - c)
                rcopy(i, 3 + f, land, land, sib).wait_recv()
            if i >= nchip:
                land = part(i, mine, 1 - c)
                rcopy(i, 6, land, land, sib).wait_recv()
        for cp in started:
            cp.wait_send()
        for loc in locs:
            loc.wait()

    outs = [jax.ShapeDtypeStruct((4,) + a.shape, a.dtype) for a in chip_arrs]
    outs += [jax.ShapeDtypeStruct((4, 2) + a.shape, a.dtype) for a in dev_arrs]
    res = pl.pallas_call(
        body, name=name, out_shape=outs,
        in_specs=[pl.BlockSpec(memory_space=pl.ANY)] * n, out_specs=[pl.BlockSpec(memory_space=pl.ANY)] * n,
        scratch_shapes=[pltpu.SemaphoreType.DMA((n * NS,)), pltpu.SemaphoreType.DMA((n * NS,)),
                        pltpu.SemaphoreType.DMA((n,))],
    )(*arrs)
    return res[:nchip], res[nchip:]


def _mm(As, Bs, out_dtype, name, tm=512, cap=1408, bt=False):
    n = len(As)
    M, N = As[0].shape[0], Bs[0].shape[0 if bt else 1]
    tm = min(tm, M)
    tn = _pick(N, cap)
    dims = (((1,), (1,)), ((), ())) if bt else (((1,), (0,)), ((), ()))

    def body(*refs):
        o = refs[2 * n]
        acc = None
        for a, b in zip(refs[:n], refs[n:2 * n]):
            d = lax.dot_general(a[...].astype(MXU_DTYPE), b[...].astype(MXU_DTYPE), dims, preferred_element_type=F32)
            acc = d if acc is None else acc + d
        o[...] = acc.astype(o.dtype)

    in_specs = [pl.BlockSpec((tm, a.shape[1]), lambda i, j: (i, 0)) for a in As]
    if bt:
        in_specs += [pl.BlockSpec((tn, b.shape[1]), lambda i, j: (j, 0)) for b in Bs]
    else:
        in_specs += [pl.BlockSpec((b.shape[0], tn), lambda i, j: (0, j)) for b in Bs]
    return pl.pallas_call(
        body, name=name, grid=(M // tm, N // tn), in_specs=in_specs,
        out_specs=pl.BlockSpec((tm, tn), lambda i, j: (i, j)),
        out_shape=jax.ShapeDtypeStruct((M, N), out_dtype),
        compiler_params=_params(("parallel", "parallel")),
    )(*As, *Bs)


def _mm_tn(A, G, name, tt=1024, cap=1024):
    T, Ka = A.shape
    N = G.shape[1]
    tt = min(tt, T)
    tk = _pick(Ka, cap)
    tn = _pick(N, cap)

    def body(a, g, o):
        @pl.when(pl.program_id(2) == 0)
        def _():
            o[...] = jnp.zeros(o.shape, F32)
        o[...] += lax.dot_general(a[...].astype(MXU_DTYPE), g[...].astype(MXU_DTYPE),
                                  (((0,), (0,)), ((), ())), preferred_element_type=F32)

    return pl.pallas_call(
        body, name=name, grid=(Ka // tk, N // tn, T // tt),
        in_specs=[pl.BlockSpec((tt, tk), lambda i, j, t: (t, i)), pl.BlockSpec((tt, tn), lambda i, j, t: (t, j))],
        out_specs=pl.BlockSpec((tk, tn), lambda i, j, t: (i, j)),
        out_shape=jax.ShapeDtypeStruct((Ka, N), F32),
        compiler_params=_params(("parallel", "parallel", "arbitrary")),
    )(A, G)


def _rowwise(name, fn, *, Bl, S, R, tiled=(), prev=(), nxt=(), batch=(), full=(),
             out_tiled=(), out_batch=(), out_acc=()):
    R = min(R, S)
    nS = S // R
    T = Bl * S
    hb = R // HALO
    n_in = len(tiled) + len(prev) + len(nxt) + len(batch) + len(full)

    in_specs, args = [], []
    for a, wd, cb in tiled:
        in_specs.append(pl.BlockSpec((R, wd), lambda b, i, cb=cb: (b * nS + i, cb)))
        args.append(a)
    for a, wd, cb in prev:
        in_specs.append(pl.BlockSpec((HALO, wd), lambda b, i, cb=cb: (jnp.maximum((b * nS + i) * hb - 1, 0), cb)))
        args.append(a)
    for a, wd, cb in nxt:
        in_specs.append(pl.BlockSpec((HALO, wd), lambda b, i, cb=cb: (jnp.minimum((b * nS + i + 1) * hb, T // HALO - 1), cb)))
        args.append(a)
    for a, wd, cb in batch:
        in_specs.append(pl.BlockSpec((1, 1, wd), lambda b, i, cb=cb: (b, 0, cb)))
        args.append(a)
    for a in full:
        in_specs.append(pl.BlockSpec(a.shape, lambda b, i, nd=a.ndim: (0,) * nd))
        args.append(a)

    out_specs, out_shape = [], []
    for C, dt in out_tiled:
        out_specs.append(pl.BlockSpec((R, C), lambda b, i: (b * nS + i, 0)))
        out_shape.append(jax.ShapeDtypeStruct((T, C), dt))
    for C in out_batch:
        out_specs.append(pl.BlockSpec((1, 1, C), lambda b, i: (b, 0, 0)))
        out_shape.append(jax.ShapeDtypeStruct((Bl, 1, C), F32))
    for shp in out_acc:
        out_specs.append(pl.BlockSpec(shp, lambda b, i, nd=len(shp): (0,) * nd))
        out_shape.append(jax.ShapeDtypeStruct(shp, F32))

    nt, npv, nnx, nbt = len(tiled), len(prev), len(nxt), len(batch)

    def body(*refs):
        b, i = pl.program_id(0), pl.program_id(1)
        ins, outs = refs[:n_in], refs[n_in:]
        vals = [r[...] for r in ins[:nt]]
        vals += [jnp.where(i > 0, r[...], jnp.zeros(r.shape, r.dtype)) for r in ins[nt:nt + npv]]
        vals += [jnp.where(i < nS - 1, r[...], jnp.zeros(r.shape, r.dtype)) for r in ins[nt + npv:nt + npv + nnx]]
        vals += [r[0] for r in ins[nt + npv + nnx:nt + npv + nnx + nbt]]
        vals += [r[...] for r in ins[nt + npv + nnx + nbt:]]
        res = fn(*vals)
        if not isinstance(res, (tuple, list)):
            res = (res,)
        k = 0
        for _ in out_tiled:
            outs[k][...] = res[k].astype(outs[k].dtype)
            k += 1
        for _ in out_batch:
            o = outs[k]

            @pl.when(i == 0)
            def _(o=o):
                o[...] = jnp.zeros(o.shape, F32)
            o[0] += res[k]
            k += 1
        for _ in out_acc:
            o = outs[k]

            @pl.when((i == 0) & (b == 0))
            def _(o=o):
                o[...] = jnp.zeros(o.shape, F32)
            o[...] += res[k]
            k += 1

    out = pl.pallas_call(
        body, name=name, grid=(Bl, nS), in_specs=in_specs, out_specs=out_specs, out_shape=out_shape,
        compiler_params=_params(("arbitrary", "arbitrary")),
    )(*args)
    return out


def _shift_down(x, halo, k):
    row = lax.broadcasted_iota(jnp.int32, x.shape, 0)
    out = pltpu.roll(x, k, 0)
    for j in range(k):
        out = jnp.where(row == j, halo[HALO - k + j:HALO - k + j + 1, :], out)
    return out


def _shift_up(x, halo, k):
    n = x.shape[0]
    row = lax.broadcasted_iota(jnp.int32, x.shape, 0)
    out = pltpu.roll(x, n - k, 0)
    for j in range(k):
        out = jnp.where(row == n - k + j, halo[j:j + 1, :], out)
    return out


def _dotm(a, b):
    return jnp.dot(a.astype(MXU_DTYPE), b.astype(MXU_DTYPE), preferred_element_type=F32)


def _split_bf16(x):
    hi = x.astype(BF16)
    return hi, (x - hi.astype(F32)).astype(BF16)


def _headsum_2pass(x, hm):
    hi, lo = _split_bf16(x)
    hb = hm.astype(BF16)
    return jnp.dot(hi, hb, preferred_element_type=F32) + jnp.dot(lo, hb, preferred_element_type=F32)


@jax.custom_vjp
def _headsum(x, hm):
    return _headsum_2pass(x, hm)


_headsum.defvjp(lambda x, hm: (_headsum_2pass(x, hm), hm),
                lambda hm, g: (_headsum_2pass(g, hm), jnp.zeros_like(hm)))


def _sigmoid(x):
    return 1.0 / (1.0 + jnp.exp(-x))


def _rms(x, g):
    return x * lax.rsqrt(jnp.mean(x * x, axis=-1, keepdims=True) + RMS_EPS) * g


def _norm_mod(x, g, sc, sh):
    return _rms(x, g) * (1.0 + sc) + sh


def _split_ps(ps):
    return (ps[:, 0:RW], ps[:, RW:2 * RW], ps[:, 2 * RW:3 * RW], ps[:, 3 * RW:3 * RW + LW + LA],
            ps[:, 3 * RW + LW + LA:SHIFT])


def _rwkv_prep(r, k, v, wa, gd, w0, w_up_p, a0, a_up_p, g_up, k_k, k_a, hm):
    w_raw = w0 + _dotm(jnp.tanh(wa), w_up_p)
    decay = jnp.exp(-DECAY_SCALE * _sigmoid(w_raw))
    a = _sigmoid(a0 + _dotm(wa, a_up_p))
    g = _dotm(_sigmoid(gd), g_up)
    kk = k * k_k
    kk = kk * lax.rsqrt(_headsum(kk * kk, hm) + L2_EPS)
    k2 = k * (1.0 + (a - 1.0) * k_a)
    return r, decay, k2, v, -kk, kk * a, g


def _rwkv_post(y, r, k2, v, g, ln_g, ln_b, r_k, hm):
    mean = _headsum(y, hm) * (1.0 / HD)
    yc = y - mean
    var = _headsum(yc * yc, hm) * (1.0 / HD)
    yn = yc * lax.rsqrt(var + GN_EPS) * ln_g + ln_b
    bonus = _headsum(r * k2 * r_k, hm) * v
    return (yn + bonus) * g


def _gelu(x):
    return 0.5 * x * (1.0 + jnp.tanh(GELU_C * (x + 0.044715 * (x * x * x))))


def _s5_post(yssm, u, d):
    return _gelu(yssm + d * u)


def _mix(ga, gb, ya, za, zb):
    return _sigmoid(ga) * ya + _sigmoid(gb) * (za * _sigmoid(zb))


def _conv_act(up_g, up_u, hg, hu, w_g, w_u, b_g, b_u):
    up_g, up_u, hg, hu = (z.astype(F32) for z in (up_g, up_u, hg, hu))

    def conv(x, h, w, b):
        return b + w[0:1] * _shift_down(x, h, 2) + w[1:2] * _shift_down(x, h, 1) + w[2:3] * x
    gate = conv(up_g, hg, w_g, b_g)
    upv = conv(up_u, hu, w_u, b_u)
    return gate, upv


def _silu_gate(gate, upv):
    return gate * _sigmoid(gate) * upv


WKV_L = 64
_NT, _NN, _TN = ((1,), (1,)), ((1,), (0,)), ((0,), (0,))


def _dotw(x, y, dims):
    return lax.dot_general(x.astype(MXU_DTYPE), y.astype(MXU_DTYPE), (dims, ((), ())), preferred_element_type=F32)


def _dot3(x, y, dims):
    (xh, xl), (yh, yl) = _split_bf16(x), _split_bf16(y)
    d = lambda p, q: lax.dot_general(p, q, (dims, ((), ())), preferred_element_type=F32)
    return d(xh, yh) + d(xh, yl) + d(xl, yh)


@jax.custom_vjp
def _gram3(x, y):
    return _dot3(x, y, _NT)


_gram3.defvjp(lambda x, y: (_dot3(x, y, _NT), (x, y)),
              lambda res, g: (_dot3(g, res[1], _NN), _dot3(g, res[0], _TN)))


def _wkv_chunk(s0, r, w, k, v, a, b):
    y, s1 = _wkv_chunks((s0,), (r,), (w,), (k,), (v,), (a,), (b,))
    return y[0], s1[0]


def _wkv_chunks(s0, r, w, k, v, a, b):
    each = lambda f, *ls: tuple(f(*xs) for xs in zip(*ls))
    L = r[0].shape[0]
    n2 = 2 * L
    lane_head = lax.broadcasted_iota(jnp.int32, (2, 1, 2 * HD), 2) // HD
    head_mask = (lane_head == lax.broadcasted_iota(jnp.int32, (2, 1, 2 * HD), 0)).astype(F32)
    ri = lax.broadcasted_iota(jnp.int32, (n2, n2), 0)
    ci = lax.broadcasted_iota(jnp.int32, (n2, n2), 1)
    same = (ri // L) == (ci // L)
    strict = same & ((ci % L) < (ri % L))
    incl = same & ((ci % L) <= (ri % L))
    si = lax.broadcasted_iota(jnp.int32, (2 * HD, 2 * HD), 0) // HD
    sj = lax.broadcasted_iota(jnp.int32, (2 * HD, 2 * HD), 1) // HD
    tri = (lax.broadcasted_iota(jnp.int32, (L, L), 0) >= lax.broadcasted_iota(jnp.int32, (L, L), 1)).astype(F32)

    stack = lambda z: (z[None] * head_mask).reshape(n2, 2 * HD)
    dup = lambda z: jnp.broadcast_to(z[None], (2, L, 2 * HD)).reshape(n2, 2 * HD)
    gram = _gram3
    nt, nn, tn = (lambda x, y, d=d: _dotw(x, y, d) for d in (_NT, _NN, _TN))
    add = lambda x, y: x + y

    lw = each(jnp.log, w)
    cum = each(lambda z: jnp.dot(tri, z, preferred_element_type=F32, precision=HIGHEST), lw)
    tot = each(lambda z: jnp.sum(z, axis=0, keepdims=True), lw)
    a2 = each(lambda av, cv, lv: stack(av * jnp.exp(cv - lv)), a, cum, lw)
    r2 = each(lambda rv, cv: stack(rv * jnp.exp(cv)), r, cum)
    v2 = each(stack, v)
    b2 = each(lambda bv, cv: dup(bv * jnp.exp(-cv)), b, cum)
    k2 = each(lambda kv, cv: dup(kv * jnp.exp(-cv)), k, cum)
    n_ab = each(lambda x, y: jnp.where(strict, gram(x, y), 0.0), a2, b2)
    n_ak = each(lambda x, y: jnp.where(strict, gram(x, y), 0.0), a2, k2)
    m_rb = each(lambda x, y: jnp.where(incl, gram(x, y), 0.0), r2, b2)
    m_rk = each(lambda x, y: jnp.where(incl, gram(x, y), 0.0), r2, k2)
    u = each(add, each(nt, a2, s0), each(nn, n_ak, v2))
    q = n_ab
    steps = L.bit_length() - 1
    for i in range(steps):
        u = each(add, u, each(nn, q, u))
        if i < steps - 1:
            q = each(nn, q, q)
    y2 = each(lambda x, y, z: x + y + z, each(nt, r2, s0), each(nn, m_rb, u), each(nn, m_rk, v2))
    y = each(lambda z: jnp.sum(z.reshape(2, L, 2 * HD), axis=0), y2)
    b3 = each(lambda bv, tv, cv: dup(bv * jnp.exp(tv - cv)), b, tot, cum)
    k3 = each(lambda kv, tv, cv: dup(kv * jnp.exp(tv - cv)), k, tot, cum)
    upd = each(add, each(tn, u, b3), each(tn, v2, k3))
    s1 = each(lambda sv, tv, uv: sv * jnp.exp(tv) + jnp.where(si == sj, uv, 0.0), s0, tot, upd)
    return y, s1


NPAIR = NH // 2


def _wkv_nb(Bl):
    return 2 if Bl % 2 == 0 else 1


def _wkv_fwd(r, w, k, v, a, b, Bl, S):
    L = WKV_L
    nC = S // L
    nb = _wkv_nb(Bl)
    chains = [(bi, p, slice(p * 2 * HD, (p + 1) * 2 * HD)) for bi in range(nb) for p in range(NPAIR)]

    def body(r_ref, w_ref, k_ref, v_ref, a_ref, b_ref, y_ref, ck_ref, s_ref):
        @pl.when(pl.program_id(1) == 0)
        def _():
            s_ref[...] = jnp.zeros(s_ref.shape, F32)
        s0 = tuple(s_ref[bi, p] for bi, p, _ in chains)
        ops = [tuple(z[bi, :, cs] for bi, _, cs in chains) for z in (r_ref, w_ref, k_ref, v_ref, a_ref, b_ref)]
        y, s1 = _wkv_chunks(s0, *ops)
        for i, (bi, p, cs) in enumerate(chains):
            ck_ref[bi, 0, p] = s0[i]
            y_ref[bi, :, cs] = y[i]
            s_ref[bi, p] = s1[i]

    to3 = lambda z: z.reshape(Bl, S, RW)
    row_spec = pl.BlockSpec((nb, L, RW), lambda g, c: (g, c, 0))
    y, ck = pl.pallas_call(
        body, name="wkv_fwd", grid=(Bl // nb, nC), in_specs=[row_spec] * 6,
        out_specs=[row_spec, pl.BlockSpec((nb, 1, NPAIR, 2 * HD, 2 * HD), lambda g, c: (g, c, 0, 0, 0))],
        out_shape=[jax.ShapeDtypeStruct((Bl, S, RW), F32), jax.ShapeDtypeStruct((Bl, nC, NPAIR, 2 * HD, 2 * HD), F32)],
        scratch_shapes=[pltpu.VMEM((nb, NPAIR, 2 * HD, 2 * HD), F32)],
        compiler_params=_params(("arbitrary", "arbitrary")),
    )(*(to3(z) for z in (r, w, k, v, a, b)))
    return y.reshape(Bl * S, RW), ck


def _wkv_bwd(r, w, k, v, a, b, dy, ck, Bl, S):
    L = WKV_L
    nC = S // L
    nb = _wkv_nb(Bl)
    chains = [(bi, p, slice(p * 2 * HD, (p + 1) * 2 * HD)) for bi in range(nb) for p in range(NPAIR)]

    def body(r_ref, w_ref, k_ref, v_ref, a_ref, b_ref, dy_ref, ck_ref,
             dr_ref, dw_ref, dk_ref, dv_ref, da_ref, db_ref, ds_ref):
        @pl.when(pl.program_id(1) == 0)
        def _():
            ds_ref[...] = jnp.zeros(ds_ref.shape, F32)
        s0 = tuple(ck_ref[bi, 0, p] for bi, p, _ in chains)
        ops = [tuple(z[bi, :, cs] for bi, _, cs in chains) for z in (r_ref, w_ref, k_ref, v_ref, a_ref, b_ref)]
        cts = (tuple(dy_ref[bi, :, cs] for bi, _, cs in chains), tuple(ds_ref[bi, p] for bi, p, _ in chains))
        ds0, *grads = jax.vjp(_wkv_chunks, s0, *ops)[1](cts)
        for i, (bi, p, cs) in enumerate(chains):
            ds_ref[bi, p] = ds0[i]
            for o, g in zip((dr_ref, dw_ref, dk_ref, dv_ref, da_ref, db_ref), grads):
                o[bi, :, cs] = g[i]

    to3 = lambda z: z.reshape(Bl, S, RW)
    row_spec = pl.BlockSpec((nb, L, RW), lambda g, c: (g, nC - 1 - c, 0))
    rows = jax.ShapeDtypeStruct((Bl, S, RW), F32)
    outs = pl.pallas_call(
        body, name="wkv_bwd", grid=(Bl // nb, nC),
        in_specs=[row_spec] * 7 + [pl.BlockSpec((nb, 1, NPAIR, 2 * HD, 2 * HD), lambda g, c: (g, nC - 1 - c, 0, 0, 0))],
        out_specs=[row_spec] * 6, out_shape=[rows] * 6,
        scratch_shapes=[pltpu.VMEM((nb, NPAIR, 2 * HD, 2 * HD), F32)],
        compiler_params=_params(("arbitrary", "arbitrary")),
    )(*(to3(z) for z in (r, w, k, v, a, b, dy)), ck)
    return [o.reshape(Bl * S, RW) for o in outs]


NST = NG * SP


def _cmul(ar, ai, br, bi):
    return ar * br - ai * bi, ar * bi + ai * br


def _s5_tiles(are, aim, reverse):
    if reverse:
        aim = -aim
    row = lax.broadcasted_iota(jnp.int32, (SUBLANES, NST), 0)
    pw = [(are, aim)]
    for _ in range(SUBLANES - 1):
        pw.append(_cmul(pw[-1][0], pw[-1][1], are, aim))
    bc = lambda z: jnp.broadcast_to(z, (SUBLANES, NST))
    ms = []
    for kk in (1, 2, 4):
        cond = (row < SUBLANES - kk) if reverse else (row >= kk)
        ms.append((jnp.where(cond, bc(pw[kk - 1][0]), 0.0), jnp.where(cond, bc(pw[kk - 1][1]), 0.0)))
    pr = jnp.zeros((SUBLANES, NST), F32)
    pi = jnp.zeros((SUBLANES, NST), F32)
    for i in range(SUBLANES):
        n = SUBLANES - i if reverse else i + 1
        pr = jnp.where(row == i, bc(pw[n - 1][0]), pr)
        pi = jnp.where(row == i, bc(pw[n - 1][1]), pi)
    return ms, (pr, pi)


def _s5_block(re, im, ms, pc, cre, cim, sg, reverse):
    ln = slice(sg * 512, (sg + 1) * 512)
    for (mr, mi), kk in zip(ms, (1, 2, 4)):
        sh = SUBLANES - kk if reverse else kk
        sre, sim = pltpu.roll(re, sh, 0), pltpu.roll(im, sh, 0)
        tr, ti = _cmul(mr[:, ln], mi[:, ln], sre, sim)
        re, im = re + tr, im + ti
    tr, ti = _cmul(pc[0][:, ln], pc[1][:, ln], cre[:, ln], cim[:, ln])
    return re + tr, im + ti


def _s5_scan(X_ref, n_rows, ms, pc, cre, cim, reverse, visit=None, acc0=None):
    nblk = n_rows // SUBLANES

    def it(i, carry):
        cre, cim, acc = carry
        j = nblk - 1 - i if reverse else i
        rows = pl.ds(pl.multiple_of(j * SUBLANES, SUBLANES), SUBLANES)
        edge = 0 if reverse else SUBLANES - 1
        blocks, ncre, ncim = [], [], []
        for sg in range(NSG):
            lr = slice(sg * 1024, sg * 1024 + 512)
            li = slice(sg * 1024 + 512, (sg + 1) * 1024)
            re, im = _s5_block(X_ref[rows, lr], X_ref[rows, li], ms, pc, cre, cim, sg, reverse)
            X_ref[rows, lr] = re
            X_ref[rows, li] = im
            blocks.append((re, im))
            ncre.append(re[edge:edge + 1])
            ncim.append(im[edge:edge + 1])
        if visit is not None:
            acc = visit(j, blocks, acc)
        return jnp.concatenate(ncre, axis=1), jnp.concatenate(ncim, axis=1), acc

    return lax.fori_loop(0, nblk, it, (cre, cim, acc0 if acc0 is not None else 0))


def _s5_fwd(u, wb, wc, ab, Bl, S, R=256):
    R = min(R, S)
    nC = S // R

    def body(u_ref, wb_ref, wc_ref, ab_ref, y_ref, st_ref, X_ref, car_ref):
        @pl.when(pl.program_id(1) == 0)
        def _():
            car_ref[...] = jnp.zeros(car_ref.shape, F32)
        st_ref[0, 0] = car_ref[...]
        ms, pc = _s5_tiles(ab_ref[0:1], ab_ref[1:2], False)
        for sg in range(NSG):
            X_ref[:, sg * 1024:(sg + 1) * 1024] = _dotm(u_ref[:, sg * 128:(sg + 1) * 128], wb_ref[sg])
        cre, cim, _ = _s5_scan(X_ref, R, ms, pc, car_ref[0:1], car_ref[1:2], False)
        car_ref[0:1] = cre
        car_ref[1:2] = cim
        for sg in range(NSG):
            y_ref[:, sg * 128:(sg + 1) * 128] = _dotm(X_ref[:, sg * 1024:(sg + 1) * 1024], wc_ref[sg])

    return pl.pallas_call(
        body, name="s5_fwd", grid=(Bl, nC),
        in_specs=[pl.BlockSpec((R, SW), lambda b, c: (b * nC + c, 0)),
                  pl.BlockSpec(wb.shape, lambda b, c: (0, 0, 0)), pl.BlockSpec(wc.shape, lambda b, c: (0, 0, 0)),
                  pl.BlockSpec(ab.shape, lambda b, c: (0, 0))],
        out_specs=[pl.BlockSpec((R, SW), lambda b, c: (b * nC + c, 0)),
                   pl.BlockSpec((1, 1, 2, NST), lambda b, c: (b, c, 0, 0)),
                   pl.BlockSpec((R, 2 * NST), lambda b, c: (b * nC + c, 0))],
        out_shape=[jax.ShapeDtypeStruct((Bl * S, SW), F32), jax.ShapeDtypeStruct((Bl, nC, 2, NST), F32),
                   jax.ShapeDtypeStruct((Bl * S, 2 * NST), F32)],
        scratch_shapes=[pltpu.VMEM((2, NST), F32)],
        compiler_params=_params(("arbitrary", "arbitrary")),
    )(u, wb, wc, ab)


def _s5_bwd(u, dy, wb, wc, ab, st, xs, Bl, S, R=256):
    R = min(R, S)
    nC = S // R

    def body(u_ref, dy_ref, wb_ref, wc_ref, ab_ref, st_ref, X_ref, du_ref, dwb_ref, dwc_ref, dab_ref,
             G_ref, car_ref):
        first = (pl.program_id(0) == 0) & (pl.program_id(1) == 0)

        @pl.when(first)
        def _():
            dwb_ref[...] = jnp.zeros(dwb_ref.shape, F32)
            dwc_ref[...] = jnp.zeros(dwc_ref.shape, F32)
            dab_ref[...] = jnp.zeros(dab_ref.shape, F32)

        @pl.when(pl.program_id(1) == 0)
        def _():
            car_ref[...] = jnp.zeros(car_ref.shape, F32)

        are, aim = ab_ref[0:1], ab_ref[1:2]
        dyv = dy_ref[...].astype(MXU_DTYPE)
        for sg in range(NSG):
            G_ref[:, sg * 1024:(sg + 1) * 1024] = lax.dot_general(
                dyv[:, sg * 128:(sg + 1) * 128], wc_ref[sg].astype(MXU_DTYPE), (((1,), (1,)), ((), ())),
                preferred_element_type=F32)
        rms_, rpc = _s5_tiles(are, aim, True)
        row = lax.broadcasted_iota(jnp.int32, (SUBLANES, 512), 0)

        def visit(j, blocks, acc):
            before = pl.multiple_of(jnp.maximum(j - 1, 0) * SUBLANES, SUBLANES)
            prow = X_ref[pl.ds(before, SUBLANES), :][SUBLANES - 1:SUBLANES]
            rows = pl.ds(pl.multiple_of(j * SUBLANES, SUBLANES), SUBLANES)
            are_acc, aim_acc = [], []
            for sg in range(NSG):
                lr = slice(sg * 1024, sg * 1024 + 512)
                li = slice(sg * 1024 + 512, (sg + 1) * 1024)
                ln = slice(sg * 512, (sg + 1) * 512)
                pre = jnp.where(j > 0, prow[:, lr], st_ref[0, 0, 0:1, ln])
                pim = jnp.where(j > 0, prow[:, li], st_ref[0, 0, 1:2, ln])
                xre = jnp.where(row == 0, pre, pltpu.roll(X_ref[rows, lr], 1, 0))
                xim = jnp.where(row == 0, pim, pltpu.roll(X_ref[rows, li], 1, 0))
                dre, dim = blocks[sg]
                are_acc.append(dre * xre + dim * xim)
                aim_acc.append(dim * xre - dre * xim)
            return acc[0] + jnp.concatenate(are_acc, axis=1), acc[1] + jnp.concatenate(aim_acc, axis=1)

        zero = jnp.zeros((SUBLANES, NST), F32)
        cre, cim, acc = _s5_scan(G_ref, R, rms_, rpc, car_ref[0:1], car_ref[1:2], True, visit, (zero, zero))
        car_ref[0:1] = cre
        car_ref[1:2] = cim
        dab_ref[0:1] += jnp.sum(acc[0], axis=0, keepdims=True)
        dab_ref[1:2] += jnp.sum(acc[1], axis=0, keepdims=True)
        uv = u_ref[...].astype(MXU_DTYPE)
        for sg in range(NSG):
            cs = slice(sg * 1024, (sg + 1) * 1024)
            us = slice(sg * 128, (sg + 1) * 128)
            gx = G_ref[:, cs].astype(MXU_DTYPE)
            dwb_ref[sg] += lax.dot_general(uv[:, us], gx, (((0,), (0,)), ((), ())), preferred_element_type=F32)
            dwc_ref[sg] += lax.dot_general(X_ref[:, cs].astype(MXU_DTYPE), dyv[:, us], (((0,), (0,)), ((), ())),
                                           preferred_element_type=F32)
            du_ref[:, us] = lax.dot_general(gx, wb_ref[sg].astype(MXU_DTYPE), (((1,), (1,)), ((), ())),
                                            preferred_element_type=F32)

    rmap = lambda b, c: (b * nC + nC - 1 - c, 0)
    return pl.pallas_call(
        body, name="s5_bwd", grid=(Bl, nC),
        in_specs=[pl.BlockSpec((R, SW), rmap), pl.BlockSpec((R, SW), rmap),
                  pl.BlockSpec(wb.shape, lambda b, c: (0, 0, 0)), pl.BlockSpec(wc.shape, lambda b, c: (0, 0, 0)),
                  pl.BlockSpec(ab.shape, lambda b, c: (0, 0)),
                  pl.BlockSpec((1, 1, 2, NST), lambda b, c: (b, nC - 1 - c, 0, 0)),
                  pl.BlockSpec((R, 2 * NST), rmap)],
        out_specs=[pl.BlockSpec((R, SW), rmap), pl.BlockSpec(wb.shape, lambda b, c: (0, 0, 0)),
                   pl.BlockSpec(wc.shape, lambda b, c: (0, 0, 0)), pl.BlockSpec((2, NST), lambda b, c: (0, 0))],
        out_shape=[jax.ShapeDtypeStruct((Bl * S, SW), F32), jax.ShapeDtypeStruct(wb.shape, F32),
                   jax.ShapeDtypeStruct(wc.shape, F32), jax.ShapeDtypeStruct((2, NST), F32)],
        scratch_shapes=[pltpu.VMEM((R, 2 * NST), F32), pltpu.VMEM((2, NST), F32)],
        compiler_params=_params(("arbitrary", "arbitrary")),
    )(u, dy, wb, wc, ab, st, xs)


def _s5_disc_math(a_re, a_im, log_dt, b_re, b_im, expand):
    dt = jnp.exp(log_dt)
    z_re, z_im = a_re * dt, a_im * dt
    mag = jnp.exp(z_re)
    ab_re, ab_im = mag * jnp.cos(z_im), mag * jnp.sin(z_im)
    den = a_re * a_re + a_im * a_im
    q_re = ((ab_re - 1.0) * a_re + ab_im * a_im) / den
    q_im = (ab_im * a_re - (ab_re - 1.0) * a_im) / den
    qe_re = jnp.dot(q_re, expand, preferred_element_type=F32, precision=HIGHEST)
    qe_im = jnp.dot(q_im, expand, preferred_element_type=F32, precision=HIGHEST)
    return ab_re, ab_im, qe_re * b_re - qe_im * b_im, qe_re * b_im + qe_im * b_re


def _whole(shape):
    return pl.BlockSpec(shape, lambda nd=len(shape): (0,) * nd)


def _s5_disc(a_re, a_im, log_dt, b_re, b_im, expand):
    def body(a, b, c, d, e, f, o0, o1, o2, o3):
        res = _s5_disc_math(a[...], b[...], c[...], d[...], e[...], f[...])
        for o, v in zip((o0, o1, o2, o3), res):
            o[...] = v
    ins = (a_re, a_im, log_dt, b_re, b_im, expand)
    outs = [jax.ShapeDtypeStruct(a_re.shape, F32)] * 2 + [jax.ShapeDtypeStruct(b_re.shape, F32)] * 2
    return pl.pallas_call(body, name="s5_disc", in_specs=[_whole(x.shape) for x in ins],
                          out_specs=[_whole(o.shape) for o in outs], out_shape=outs)(*ins)


def _s5_disc_bwd(a_re, a_im, log_dt, b_re, b_im, expand, cts):
    def body(a, b, c, d, e, f, g0, g1, g2, g3, o0, o1, o2, o3, o4):
        fn = lambda *p: _s5_disc_math(*p, f[...])
        _, vjp = jax.vjp(fn, a[...], b[...], c[...], d[...], e[...])
        for o, v in zip((o0, o1, o2, o3, o4), vjp((g0[...], g1[...], g2[...], g3[...]))):
            o[...] = v
    ins = (a_re, a_im, log_dt, b_re, b_im, expand) + tuple(cts)
    outs = [jax.ShapeDtypeStruct(x.shape, F32) for x in (a_re, a_im, log_dt, b_re, b_im)]
    return pl.pallas_call(body, name="s5_disc_bwd", in_specs=[_whole(x.shape) for x in ins],
                          out_specs=[_whole(o.shape) for o in outs], out_shape=outs)(*ins)


def _ada_fwd(c_all, w_shard, b_shard):
    def body(c_ref, w_ref, b_ref, o_ref):
        cv = c_ref[...]
        o_ref[...] = _dotm(cv * _sigmoid(cv), w_ref[...]) + b_ref[...]
    n = w_shard.shape[1]
    return pl.pallas_call(
        body, name="ada_fwd", in_specs=[_whole(c_all.shape), _whole(w_shard.shape), _whole(b_shard.shape)],
        out_specs=_whole((c_all.shape[0], n)), out_shape=jax.ShapeDtypeStruct((c_all.shape[0], n), F32),
        compiler_params=_params(),
    )(c_all, w_shard, b_shard)


def _ada_bwd(c_all, dmod_cols, dmod_all):
    def body(c_ref, dc_ref, da_ref, gw_ref, gb_ref):
        cv = c_ref[...]
        gw_ref[...] = lax.dot_general((cv * _sigmoid(cv)).astype(MXU_DTYPE), dc_ref[...].astype(MXU_DTYPE),
                                      (((0,), (0,)), ((), ())), preferred_element_type=F32)
        gb_ref[...] = jnp.sum(da_ref[...], axis=0, keepdims=True)
    n = dmod_cols.shape[1]
    return pl.pallas_call(
        body, name="ada_bwd", in_specs=[_whole(c_all.shape), _whole(dmod_cols.shape), _whole(dmod_all.shape)],
        out_specs=[_whole((D, n)), _whole((1, dmod_all.shape[1]))],
        out_shape=[jax.ShapeDtypeStruct((D, n), F32), jax.ShapeDtypeStruct((1, dmod_all.shape[1]), F32)],
        compiler_params=_params(),
    )(c_all, dmod_cols, dmod_all)


def _rows_block(n_rows, cap=512):
    if n_rows <= cap:
        return n_rows
    for t in range(cap - cap % SUBLANES, 0, -SUBLANES):
        if n_rows % t == 0:
            return t
    return n_rows


def _adamw(w, g, m, v, name):
    rows, cols = w.shape
    tr = _rows_block(rows, max(SUBLANES, (1 << 19) // max(cols, 1) // SUBLANES * SUBLANES))

    def body(w_ref, g_ref, m_ref, v_ref, d_ref, nm_ref, nv_ref):
        gv = g_ref[...]
        nm = B1 * m_ref[...] + (1.0 - B1) * gv
        nv = B2 * v_ref[...] + (1.0 - B2) * (gv * gv)
        m_hat = nm / (1.0 - B1 ** STEP)
        v_hat = nv / (1.0 - B2 ** STEP)
        d_ref[...] = -LR * (m_hat / (jnp.sqrt(v_hat) + ADAM_EPS) + WD * w_ref[...])
        nm_ref[...] = nm
        nv_ref[...] = nv

    spec = pl.BlockSpec((tr, cols), lambda i: (i, 0))
    sd = jax.ShapeDtypeStruct((rows, cols), F32)
    return pl.pallas_call(body, name=name, grid=(rows // tr,), in_specs=[spec] * 4, out_specs=[spec] * 3,
                          out_shape=[sd] * 3, compiler_params=_params(("parallel",)))(w, g, m, v)


def _sum_slots(x, out_dtype, name):
    xs = x if isinstance(x, (list, tuple)) else [x]
    _, rows, cols = xs[0].shape
    tr = _rows_block(rows)

    def body(*refs):
        acc = None
        for x_ref in refs[:-1]:
            for j in range(x_ref.shape[0]):
                term = x_ref[j].astype(F32)
                acc = term if acc is None else acc + term
        refs[-1][...] = acc.astype(refs[-1].dtype)

    return pl.pallas_call(
        body, name=name, grid=(rows // tr,),
        in_specs=[pl.BlockSpec((z.shape[0], tr, cols), lambda i: (0, i, 0)) for z in xs],
        out_specs=pl.BlockSpec((tr, cols), lambda i: (i, 0)), out_shape=jax.ShapeDtypeStruct((rows, cols), out_dtype),
        compiler_params=_params(("parallel",)))(*xs)


PACK_COLS = 1024


def _pack_rows(parts, dtype, row_mult):
    flat = jnp.concatenate([p.reshape(-1).astype(dtype) for p in parts])
    per = PACK_COLS * row_mult
    n = -(-flat.shape[0] // per) * per
    flat = jnp.pad(flat, (0, n - flat.shape[0]))
    return flat.reshape(n // PACK_COLS, PACK_COLS)


def _unpack(flat, shapes):
    out, off = [], 0
    for s in shapes:
        n = math.prod(s)
        out.append(flat[off:off + n].reshape(s))
        off += n
    return out


BIG = (("w_in", (D, SHIFT + SW + 2 * D), 1), ("w_out_rwkv", (RW, D), 1), ("w_glu", (SW, 2 * D), 1),
       ("w_out", (D, D), 0), ("w_ffn_up", (D, 2 * DFF), 1), ("w_ffn_down", (DFF, D), 0))
BIG_SMALL = (("rwkv_w_up", (LW, RW), 1), ("rwkv_a_up", (LA, RW), 1), ("rwkv_g_up", (LG, RW), 1),
             ("ffn_conv_w", (3, 2 * DFF), 1))
BIG_LATE = BIG[4:]


def _shard_shape(shape, axis):
    return (shape[0] // 4, shape[1]) if axis == 0 else (shape[0], shape[1] // 4)


def _to_shards(g, axis):
    r, C = g.shape
    return g.reshape(4, r // 4, C) if axis == 0 else g.reshape(r, 4, C // 4).transpose(1, 0, 2)


def _from_shards(x, axis):
    _, r, C = x.shape
    return x.reshape(4 * r, C) if axis == 0 else x.transpose(1, 0, 2).reshape(r, 4 * C)


def kernel(x, c, w_ada, b_ada, norm1_g, w_in, mu_shift, rwkv_w0, rwkv_w_up, rwkv_a0, rwkv_a_up, rwkv_g_up, rwkv_k_k, rwkv_k_a, rwkv_r_k, rwkv_ln_g, rwkv_ln_b, w_out_rwkv, s5_a_re, s5_a_im, s5_log_dt, s5_b_re, s5_b_im, s5_c_re, s5_c_im, s5_d, w_glu, w_out, norm2_g, w_ffn_up, ffn_conv_w, ffn_conv_b, w_ffn_down, norm_f_g, loss_target, m_w_ada, m_b_ada, m_norm1_g, m_w_in, m_mu_shift, m_rwkv_w0, m_rwkv_w_up, m_rwkv_a0, m_rwkv_a_up, m_rwkv_g_up, m_rwkv_k_k, m_rwkv_k_a, m_rwkv_r_k, m_rwkv_ln_g, m_rwkv_ln_b, m_w_out_rwkv, m_s5_a_re, m_s5_a_im, m_s5_log_dt, m_s5_b_re, m_s5_b_im, m_s5_c_re, m_s5_c_im, m_s5_d, m_w_glu, m_w_out, m_norm2_g, m_w_ffn_up, m_ffn_conv_w, m_ffn_conv_b, m_w_ffn_down, m_norm_f_g, v_w_ada, v_b_ada, v_norm1_g, v_w_in, v_mu_shift, v_rwkv_w0, v_rwkv_w_up, v_rwkv_a0, v_rwkv_a_up, v_rwkv_g_up, v_rwkv_k_k, v_rwkv_k_a, v_rwkv_r_k, v_rwkv_ln_g, v_rwkv_ln_b, v_w_out_rwkv, v_s5_a_re, v_s5_a_im, v_s5_log_dt, v_s5_b_re, v_s5_b_im, v_s5_c_re, v_s5_c_im, v_s5_d, v_w_glu, v_w_out, v_norm2_g, v_w_ffn_up, v_ffn_conv_w, v_ffn_conv_b, v_w_ffn_down, v_norm_f_g):
    names = ["w_ada", "b_ada", "norm1_g", "w_in", "mu_shift", "rwkv_w0", "rwkv_w_up", "rwkv_a0", "rwkv_a_up",
             "rwkv_g_up", "rwkv_k_k", "rwkv_k_a", "rwkv_r_k", "rwkv_ln_g", "rwkv_ln_b", "w_out_rwkv", "s5_a_re",
             "s5_a_im", "s5_log_dt", "s5_b_re", "s5_b_im", "s5_c_re", "s5_c_im", "s5_d", "w_glu", "w_out", "norm2_g",
             "w_ffn_up", "ffn_conv_w", "ffn_conv_b", "w_ffn_down", "norm_f_g"]
    env = dict(locals())
    W = {n: env[n] for n in names}
    M = {n: env["m_" + n] for n in names}
    V = {n: env["v_" + n] for n in names}

    Bl, S, _ = x.shape
    T = Bl * S
    ix, iy, ic = lax.axis_index("x"), lax.axis_index("y"), lax.axis_index("c")
    chip = 2 * ix + iy
    dev = 2 * chip + ic
    rw = functools.partial(_rowwise, Bl=Bl, S=S)

    now = [b for b in BIG if b not in BIG_LATE]
    chip_arrs = [W[n][0].astype(MXU_DTYPE) for n, _, _ in now] + [W[n][0] for n, _, _ in BIG_SMALL[:3]]
    got_chip, got_dev = _gather_two_level(chip_arrs, [W["ffn_conv_w"][0], c], "gather_w")
    full = {n: _from_shards(g, axis) for (n, _, axis), g in zip(tuple(now) + BIG_SMALL[:3], got_chip)}
    full["ffn_conv_w"] = _from_shards(got_dev[0][:, 0], 1)
    c_all = got_dev[1].reshape(8 * Bl, D)
    w_p, w_u, w_g = full["w_in"][:, :SHIFT], full["w_in"][:, SHIFT:SHIFT + SW], full["w_in"][:, SHIFT + SW:]
    zeros_l = jnp.zeros((LW, RW), F32)
    w_up_p = jnp.concatenate([full["rwkv_w_up"], zeros_l], axis=0)
    a_up_p = jnp.concatenate([zeros_l, full["rwkv_a_up"]], axis=0)
    g_up = full["rwkv_g_up"]
    conv_w = full["ffn_conv_w"]
    conv_wg, conv_wu = conv_w[:, :DFF], conv_w[:, DFF:]
    conv_bg, conv_bu = ffn_conv_b[:, :DFF], ffn_conv_b[:, DFF:]
    hm = jnp.kron(jnp.eye(NH, dtype=F32), jnp.ones((HD, HD), F32))

    ncol = 6 * D // 4
    b_ada_cols = lax.dynamic_slice_in_dim(b_ada, chip * ncol, ncol, 1)
    mod_part = _ada_fwd(c_all, w_ada[0], b_ada_cols)
    mod4 = _gather_two_level([], [mod_part], "gather_mod")[1][0][:, 0]
    mod4, late = lax.optimization_barrier((mod4, [W[n][0].astype(MXU_DTYPE) for n, _, _ in BIG_LATE]))
    late_moves = [(i, i, lambda ref, me, peer: ref, lambda ref, me, k: ref.at[_chip_of(me)]) for i in range(len(late))]
    late_start = _send_start("gather_ffn_start", CHIP_FLIPS, late,
                             [jax.ShapeDtypeStruct((4,) + z.shape, z.dtype) for z in late], late_moves)
    norm1_g = norm1_g + late_start["token"]
    mod = lax.dynamic_slice_in_dim(mod4, dev * Bl, Bl, 1).transpose(1, 0, 2).reshape(Bl, 1, 6 * D)
    SH1, SC1, GT1, SH2, SC2, GT2 = range(6)

    x2d = x.reshape(T, D)
    tgt = loss_target.reshape(T, D)

    (h1,) = rw("norm1", lambda xv, sc, sh, g: _norm_mod(xv, g, sc, sh), R=256, tiled=[(x2d, D, 0)],
               batch=[(mod, D, SC1), (mod, D, SH1)], full=[norm1_g], out_tiled=[(D, MXU_DTYPE)])
    p = _mm([h1], [w_p], F32, "proj_p")
    u = _mm([h1], [w_u], F32, "proj_u")
    gates = _mm([h1], [w_g], F32, "proj_g")

    prep_params = [rwkv_w0, w_up_p, rwkv_a0, a_up_p, g_up, rwkv_k_k, rwkv_k_a, hm]

    def prep_fwd(pv, ph, mu, *pp):
        ps = pv + (_shift_down(pv, ph, 1) - pv) * mu
        return _rwkv_prep(*_split_ps(ps), *pp)

    r_, w_, k_, v_, a_, b_, g_ = rw("rwkv_prep", prep_fwd, R=256, tiled=[(p, SHIFT, 0)], prev=[(p, SHIFT, 0)],
                                    full=[mu_shift] + prep_params, out_tiled=[(RW, F32)] * 7)
    y_wkv, ck = _wkv_fwd(r_, w_, k_, v_, a_, b_, Bl, S)
    r_k_row = rwkv_r_k.reshape(1, RW)
    post_params = [rwkv_ln_g, rwkv_ln_b, r_k_row, hm]
    (o_rwkv,) = rw("rwkv_post", _rwkv_post, R=256,
                   tiled=[(y_wkv, RW, 0), (r_, RW, 0), (k_, RW, 0), (v_, RW, 0), (g_, RW, 0)],
                   full=post_params, out_tiled=[(RW, MXU_DTYPE)])
    y_a = _mm([o_rwkv], [full["w_out_rwkv"]], F32, "out_rwkv")

    expand = jnp.kron(jnp.eye(SP, dtype=F32), jnp.ones((1, SGC), F32))
    s5_in = (s5_a_re[0], s5_a_im[0], s5_log_dt[0].reshape(NG, 1), s5_b_re[0].reshape(NG, SP * SGC),
             s5_b_im[0].reshape(NG, SP * SGC), expand)
    ab_re, ab_im, bb_re, bb_im = _s5_disc(*s5_in)
    eye8 = jnp.eye(8, dtype=F32)

    def blockdiag_in(bb):
        t = bb.reshape(NSG, 8, SP, SGC)
        return jnp.einsum("ab,sapc->sacbp", eye8, t).reshape(NSG, 128, 512)

    def blockdiag_out(cc):
        t = cc.reshape(NSG, 8, SGC, SP)
        return jnp.einsum("ab,sacp->sapbc", eye8, t).reshape(NSG, 512, 128)

    wb = jnp.concatenate([blockdiag_in(bb_re), blockdiag_in(bb_im)], axis=2).astype(MXU_DTYPE)
    wc = jnp.concatenate([blockdiag_out(s5_c_re[0]), -blockdiag_out(s5_c_im[0])], axis=1).astype(MXU_DTYPE)
    ab = jnp.stack([ab_re.reshape(NST), ab_im.reshape(NST)])
    y_ssm, s5_st, s5_x = _s5_fwd(u, wb, wc, ab, Bl, S)
    (s5o,) = rw("s5_post", _s5_post, R=256, tiled=[(y_ssm, SW, 0), (u, SW, 0)], full=[s5_d],
                out_tiled=[(SW, MXU_DTYPE)])
    z = _mm([s5o], [full["w_glu"]], F32, "glu")
    mix_tiled = [(gates, D, 0), (gates, D, 1), (y_a, D, 0), (z, D, 0), (z, D, 1)]
    (mixed_in,) = rw("mix", _mix, R=256, tiled=mix_tiled, out_tiled=[(D, MXU_DTYPE)])
    mixed = _mm([mixed_in], [full["w_out"]], F32, "out_proj")

    def norm2_fwd(xv, mx, gt, sc, sh, g):
        x1 = xv + gt * mx
        return x1, _norm_mod(x1, g, sc, sh)

    x1, h2 = rw("norm2", norm2_fwd, R=256, tiled=[(x2d, D, 0), (mixed, D, 0)],
                batch=[(mod, D, GT1), (mod, D, SC2), (mod, D, SH2)], full=[norm2_g],
                out_tiled=[(D, F32), (D, MXU_DTYPE)])
    late_own, late_got = _send_wait("gather_ffn_wait", CHIP_FLIPS, late_start, late_moves, h2)
    for (n, _, axis), own, got in zip(BIG_LATE, late_own, late_got):
        full[n] = _from_shards(lax.dynamic_update_slice(got, own[None], (chip, 0, 0)), axis)
    up = _mm([h2], [full["w_ffn_up"]], MXU_DTYPE, "ffn_up")
    conv_tiled = [(up, DFF, 0), (up, DFF, 1)]
    conv_full = [conv_wg, conv_wu, conv_bg, conv_bu]

    def act_fwd(*a):
        return _silu_gate(*_conv_act(*a))

    (act,) = rw("ffn_act", act_fwd, R=128, tiled=conv_tiled, prev=conv_tiled, full=conv_full,
                out_tiled=[(DFF, MXU_DTYPE)])
    ffn = _mm([act], [full["w_ffn_down"]], F32, "ffn_down")

    def head(x1v, fv, tv, gt, g):
        x2 = x1v + gt * fv
        y, vjp = jax.vjp(_rms, x2, g)
        e = y - tv
        dx2, dg = vjp(e * (1.0 / D))
        loss = jnp.sum(e * e, keepdims=True) * jnp.ones((1, LANES), F32)
        return dx2, dx2 * gt, jnp.sum(dx2 * fv, axis=0, keepdims=True), dg.reshape(1, D), loss

    dx2, d_ffn, d_gt2, g_norm_f, loss_acc = rw(
        "head", head, R=256, tiled=[(x1, D, 0), (ffn, D, 0), (tgt, D, 0)], batch=[(mod, D, GT2)],
        full=[norm_f_g.reshape(1, D)], out_tiled=[(D, F32), (D, MXU_DTYPE)], out_batch=[D],
        out_acc=[(1, D), (1, LANES)])
    loss = lax.psum(0.5 / D * loss_acc[0, 0], ("x", "y", "c"))

    d_act = _mm([d_ffn], [full["w_ffn_down"]], F32, "d_act", bt=True)
    g_w_ffn_down = _mm_tn(act, d_ffn, "g_ffn_down")

    def act_bwd(ug, uu, dact, hg, hu, wg, wu, bg, bu):
        ug, uu, hg, hu = (z.astype(F32) for z in (ug, uu, hg, hu))
        gate, upv = _conv_act(ug, uu, hg, hu, wg, wu, bg, bu)
        _, vjp_s = jax.vjp(_silu_gate, gate, upv)
        d_gate, d_upv = vjp_s(dact)
        def taps(dh, xv, h):
            return [jnp.sum(dh * _shift_down(xv, h, 2), axis=0, keepdims=True),
                    jnp.sum(dh * _shift_down(xv, h, 1), axis=0, keepdims=True),
                    jnp.sum(dh * xv, axis=0, keepdims=True), jnp.sum(dh, axis=0, keepdims=True)]
        return (d_gate, d_upv, *taps(d_gate, ug, hg), *taps(d_upv, uu, hu))

    dh_g, dh_u, *tapg = rw(
        "ffn_act_bwd", act_bwd, R=128, tiled=conv_tiled + [(d_act, DFF, 0)], prev=conv_tiled, full=conv_full,
        out_tiled=[(DFF, MXU_DTYPE), (DFF, MXU_DTYPE)], out_acc=[(1, DFF)] * 8)
    g_cw_g, g_cb_g = jnp.concatenate(tapg[0:3], axis=0), tapg[3]
    g_cw_u, g_cb_u = jnp.concatenate(tapg[4:7], axis=0), tapg[7]

    def conv_t(dg, du_, ng, nu, wg, wu):
        dg, du_, ng, nu = (z.astype(F32) for z in (dg, du_, ng, nu))

        def ct(d, n, w):
            return w[2:3] * d + w[1:2] * _shift_up(d, n, 1) + w[0:1] * _shift_up(d, n, 2)
        return jnp.concatenate([ct(dg, ng, wg), ct(du_, nu, wu)], axis=1)

    (d_up,) = rw("conv_bwd", conv_t, R=128, tiled=[(dh_g, DFF, 0), (dh_u, DFF, 0)],
                 nxt=[(dh_g, DFF, 0), (dh_u, DFF, 0)], full=[conv_wg, conv_wu], out_tiled=[(2 * DFF, MXU_DTYPE)])
    d_h2 = _mm([d_up], [full["w_ffn_up"]], F32, "d_h2", bt=True)
    g_w_ffn_up = _mm_tn(h2, d_up, "g_ffn_up")

    sds = jax.ShapeDtypeStruct
    reduce_src = lambda r: (lambda ref, me, peer: ref.at[_chip_of(peer), _half(r, peer[2])])

    def reduced_halves(tag, started, moves, after):
        gsh_own, got = _send_wait("rs_%s_wait" % tag, ALL_FLIPS, started, moves, after)
        halves = []
        for i, (g, gt) in enumerate(zip(gsh_own, got)):
            h = g.shape[1] // 2
            own = lax.dynamic_slice(g, (chip, ic * h, 0), (1, h, g.shape[2]))
            halves.append(_sum_slots([own, gt], F32, "rs_%s_sum%d" % (tag, i)))
        return halves

    def share_start(tag, halves):
        moves = [(i, i, lambda ref, me, peer: ref, lambda ref, me, k: ref) for i in range(len(halves))]
        return _send_start("share_%s_start" % tag, PAIR_FLIPS, halves, [sds(g.shape, F32) for g in halves], moves), moves

    def share_finish(tag, started, moves, after, group, grads):
        mine_h, got_h = _send_wait("share_%s_wait" % tag, PAIR_FLIPS, started, moves, after)
        for (n, _, _), mh, gh in zip(group, mine_h, got_h):
            grads[n] = jnp.concatenate([jnp.where(ic == 0, mh, gh), jnp.where(ic == 0, gh, mh)], axis=0)[None]

    gsh_late = [_to_shards(g, ax).astype(MXU_DTYPE) for g, (_, _, ax) in zip((g_w_ffn_up, g_w_ffn_down), BIG_LATE)]
    rsl_moves = [(i, i, reduce_src(g.shape[1]), lambda ref, me, k: ref.at[k]) for i, g in enumerate(gsh_late)]
    rsl = _send_start("rs_ffn_start", ALL_FLIPS, gsh_late,
                      [sds((len(ALL_FLIPS), g.shape[1] // 2, g.shape[2]), MXU_DTYPE) for g in gsh_late], rsl_moves)
    norm2_g = norm2_g + rsl["token"]

    def norm2_bwd(x1v, dh2, dx2v, mx, gt, sc, sh, g):
        _, vjp = jax.vjp(_norm_mod, x1v, g, sc, sh)
        dxn, dg, dsc, dsh = vjp(dh2)
        dx1 = dx2v + dxn
        return dx1, dx1 * gt, jnp.sum(dx1 * mx, axis=0, keepdims=True), dsc, dsh, dg

    dx1, d_mixed, d_gt1, d_sc2, d_sh2, g_norm2 = rw(
        "norm2_bwd", norm2_bwd, R=256, tiled=[(x1, D, 0), (d_h2, D, 0), (dx2, D, 0), (mixed, D, 0)],
        batch=[(mod, D, GT1), (mod, D, SC2), (mod, D, SH2)], full=[norm2_g],
        out_tiled=[(D, F32), (D, MXU_DTYPE)], out_batch=[D, D, D], out_acc=[(1, D)])

    d_mixed_in = _mm([d_mixed], [full["w_out"]], F32, "d_mixed_in", bt=True)
    g_w_out = _mm_tn(mixed_in, d_mixed, "g_w_out")

    def mix_bwd(ga, gb, ya, za, zb, dm):
        _, vjp = jax.vjp(_mix, ga, gb, ya, za, zb)
        dga, dgb, dya, dza, dzb = vjp(dm)
        return jnp.concatenate([dga, dgb], axis=1), dya, jnp.concatenate([dza, dzb], axis=1)

    d_gates, d_ya, d_z = rw("mix_bwd", mix_bwd, R=256, tiled=mix_tiled + [(d_mixed_in, D, 0)],
                            out_tiled=[(2 * D, MXU_DTYPE), (D, MXU_DTYPE), (2 * D, MXU_DTYPE)])
    d_o_rwkv = _mm([d_ya], [full["w_out_rwkv"]], F32, "d_o_rwkv", bt=True)
    g_w_out_rwkv = _mm_tn(o_rwkv, d_ya, "g_out_rwkv")
    d_s5o = _mm([d_z], [full["w_glu"]], F32, "d_s5o", bt=True)
    g_w_glu = _mm_tn(s5o, d_z, "g_glu")

    def s5_post_bwd(ys, uv, ds, dd):
        _, vjp = jax.vjp(_s5_post, ys, uv, dd)
        return vjp(ds)

    d_yssm, d_u_direct, g_s5_d = rw("s5_post_bwd", s5_post_bwd, R=256,
                                    tiled=[(y_ssm, SW, 0), (u, SW, 0), (d_s5o, SW, 0)], full=[s5_d],
                                    out_tiled=[(SW, F32), (SW, F32)], out_acc=[(1, SW)])
    d_u_ssm, d_wb, d_wc, d_ab = _s5_bwd(u, d_yssm, wb, wc, ab, s5_st, s5_x, Bl, S)

    def diag_in(dw):
        t = dw.reshape(NSG, 8, SGC, 8, SP)
        return jnp.einsum("ab,sacbp->sapc", eye8, t).reshape(NG, SP * SGC)

    def diag_out(dw):
        t = dw.reshape(NSG, 8, SP, 8, SGC)
        return jnp.einsum("ab,sapbc->sacp", eye8, t).reshape(NG, SGC, SP)

    g_s5_c_re = diag_out(d_wc[:, :512])
    g_s5_c_im = -diag_out(d_wc[:, 512:])
    disc_cts = (d_ab[0].reshape(NG, SP), d_ab[1].reshape(NG, SP), diag_in(d_wb[:, :, :512]), diag_in(d_wb[:, :, 512:]))
    g_a_re, g_a_im, g_log_dt, g_b_re, g_b_im = _s5_disc_bwd(*s5_in, disc_cts)

    def post_bwd(yv, rv, kv, vv, gv, do, *pp):
        _, vjp = jax.vjp(lambda *a: _rwkv_post(*a, pp[3]), yv, rv, kv, vv, gv, *pp[:3])
        return vjp(do)

    dy_wkv, dr_b, dk_b, dv_b, dg_, g_ln_g, g_ln_b, g_r_k = rw(
        "rwkv_post_bwd", post_bwd, R=256,
        tiled=[(y_wkv, RW, 0), (r_, RW, 0), (k_, RW, 0), (v_, RW, 0), (g_, RW, 0), (d_o_rwkv, RW, 0)],
        full=post_params, out_tiled=[(RW, F32)] * 5, out_acc=[(1, RW)] * 3)
    dr3, dw3, dk3, dv3, da3, db3 = _wkv_bwd(r_, w_, k_, v_, a_, b_, dy_wkv, ck, Bl, S)

    shl, shl_moves = share_start("ffn", reduced_halves("ffn", rsl, rsl_moves, dr3))
    mu_shift = mu_shift + shl["token"]

    def prep_bwd(pv, dr1, dr2, dwv, dk1, dk2, dv1, dv2, dav, dbv, dgv, ph, mu, *pp):
        prev = _shift_down(pv, ph, 1)
        ps = pv + (prev - pv) * mu
        _, vjp = jax.vjp(lambda *q: _rwkv_prep(*q, pp[7]), *_split_ps(ps), *pp[:7])
        grads = vjp((dr1 + dr2, dwv, dk1 + dk2, dv1 + dv2, dav, dbv, dgv))
        dps = jnp.concatenate(grads[:5], axis=1)
        return (dps,) + tuple(grads[5:]) + (jnp.sum(dps * (prev - pv), axis=0, keepdims=True),)

    prep_outs = rw(
        "rwkv_prep_bwd", prep_bwd, R=256,
        tiled=[(p, SHIFT, 0), (dr3, RW, 0), (dr_b, RW, 0), (dw3, RW, 0), (dk3, RW, 0), (dk_b, RW, 0),
               (dv3, RW, 0), (dv_b, RW, 0), (da3, RW, 0), (db3, RW, 0), (dg_, RW, 0)],
        prev=[(p, SHIFT, 0)], full=[mu_shift] + prep_params,
        out_tiled=[(SHIFT, F32)],
        out_acc=[(1, RW), (LW + LA, RW), (1, RW), (LW + LA, RW), (LG, RW), (1, RW), (1, RW), (1, SHIFT)])
    d_ps, g_w0, g_w_up_p, g_a0, g_a_up_p, g_g_up, g_k_k, g_k_a, g_mu = prep_outs

    def shift_bwd(dps, nx, mu):
        return dps * (1.0 - mu) + _shift_up(dps * mu, nx * mu, 1)

    (d_p,) = rw("shift_bwd", shift_bwd, R=256, tiled=[(d_ps, SHIFT, 0)], nxt=[(d_ps, SHIFT, 0)], full=[mu_shift],
                out_tiled=[(SHIFT, MXU_DTYPE)])
    (d_u,) = rw("d_u", lambda a1, a2: a1 + a2, R=256, tiled=[(d_u_direct, SW, 0), (d_u_ssm, SW, 0)],
                out_tiled=[(SW, MXU_DTYPE)])
    g_w_in = jnp.concatenate([_mm_tn(h1, d_p, "g_w_p"), _mm_tn(h1, d_u, "g_w_u"), _mm_tn(h1, d_gates, "g_w_g")], axis=1)
    big_g = {"w_in": g_w_in, "w_out_rwkv": g_w_out_rwkv, "w_glu": g_w_glu, "w_out": g_w_out}
    gsh_now = [_to_shards(big_g[n], ax).astype(MXU_DTYPE) for n, _, ax in now]
    rsn_moves = [(i, i, reduce_src(g.shape[1]), lambda ref, me, k: ref.at[k]) for i, g in enumerate(gsh_now)]
    rsn = _send_start("rs_mix_start", ALL_FLIPS, gsh_now,
                      [sds((len(ALL_FLIPS), g.shape[1] // 2, g.shape[2]), MXU_DTYPE) for g in gsh_now], rsn_moves)
    norm1_g = norm1_g + rsn["token"]
    d_h1 = _mm([d_p, d_u, d_gates], [w_p, w_u, w_g], F32, "d_h1", bt=True)

    def norm1_bwd(xv, dh1, dx1v, sc, sh, g):
        _, vjp = jax.vjp(_norm_mod, xv, g, sc, sh)
        dxn, dg, dsc, dsh = vjp(dh1)
        return dx1v + dxn, dsc, dsh, dg

    grad_x, d_sc1, d_sh1, g_norm1 = rw(
        "norm1_bwd", norm1_bwd, R=256, tiled=[(x2d, D, 0), (d_h1, D, 0), (dx1, D, 0)],
        batch=[(mod, D, SC1), (mod, D, SH1)], full=[norm1_g], out_tiled=[(D, F32)], out_batch=[D, D], out_acc=[(1, D)])

    shn, shn_moves = share_start("mix", reduced_halves("mix", rsn, rsn_moves, grad_x))

    dmod =jnp.concatenate([d_sh1, d_sc1, d_gt1, d_sh2, d_sc2, d_gt2], axis=2).reshape(Bl, 6 * D)
    dmod_all = _gather_two_level([], [dmod], "gather_dmod")[1][0].reshape(8 * Bl, 6 * D)
    dmod_cols = lax.dynamic_slice_in_dim(dmod_all, chip * ncol, ncol, 1)
    g_w_ada, g_b_ada = _ada_bwd(c_all, dmod_cols, dmod_all)

    small = {"norm1_g": g_norm1, "mu_shift": g_mu, "rwkv_w0": g_w0, "rwkv_a0": g_a0, "rwkv_k_k": g_k_k,
             "rwkv_k_a": g_k_a, "rwkv_r_k": g_r_k, "rwkv_ln_g": g_ln_g, "rwkv_ln_b": g_ln_b, "s5_a_re": g_a_re,
             "s5_a_im": g_a_im, "s5_log_dt": g_log_dt, "s5_b_re": g_b_re, "s5_b_im": g_b_im, "s5_c_re": g_s5_c_re,
             "s5_c_im": g_s5_c_im, "s5_d": g_s5_d, "norm2_g": g_norm2,
             "ffn_conv_b": jnp.concatenate([g_cb_g, g_cb_u], axis=1), "norm_f_g": g_norm_f}
    small_names = list(small)
    g_conv_w = jnp.concatenate([g_cw_g, g_cw_u], axis=1)
    shard_small = {"rwkv_w_up": g_w_up_p[:LW], "rwkv_a_up": g_a_up_p[LW:], "rwkv_g_up": g_g_up, "ffn_conv_w": g_conv_w}
    parts = [small[n] for n in small_names] + [_to_shards(shard_small[n], ax) for n, _, ax in BIG_SMALL]
    spack = _pack_rows(parts, F32, SUBLANES)
    s_all = _gather_two_level([], [spack], "gather_gsmall")[1][0]
    s_sum = _sum_slots(s_all.reshape((8,) + spack.shape), F32, "sum_gsmall").reshape(-1)
    grads = {}
    off = 0
    for n in small_names:
        grads[n] = s_sum[off:off + W[n].size].reshape(W[n].shape)
        off += W[n].size
    for n, shape, axis in BIG_SMALL:
        ss = _shard_shape(shape, axis)
        k4 = 4 * math.prod(ss)
        sh4 = s_sum[off:off + k4].reshape(4, math.prod(ss))
        grads[n] = lax.dynamic_index_in_dim(sh4, chip, 0, keepdims=False).reshape((1,) + ss)
        off += k4

    share_finish("ffn", shl, shl_moves, s_sum, BIG_LATE, grads)
    share_finish("mix", shn, shn_moves, grads[BIG_LATE[0][0]], now, grads)
    grads["w_ada"] = g_w_ada[None]
    grads["b_ada"] = g_b_ada

    delta, new_m, new_v = {}, {}, {}
    to2 = lambda z: z.reshape(-1, z.shape[-1])
    for n in ["w_ada"] + [b[0] for b in BIG]:
        d_, m_, v2_ = _adamw(to2(W[n]), to2(grads[n]), to2(M[n]), to2(V[n]), "adamw_" + n)
        delta[n], new_m[n], new_v[n] = (z.reshape(W[n].shape) for z in (d_, m_, v2_))
    rest = [n for n in names if n not in delta]
    packs = [_pack_rows([src[n] for n in rest], F32, SUBLANES) for src in (W, grads, M, V)]
    d_, m_, v2_ = _adamw(*packs, "adamw_small")
    shapes = [W[n].shape for n in rest]
    for dst, z in ((delta, d_), (new_m, m_), (new_v, v2_)):
        for n, val in zip(rest, _unpack(z.reshape(-1), shapes)):
            dst[n] = val

    return (loss, grad_x.reshape(Bl, S, D), *[grads[n] for n in names], *[delta[n] for n in names],
            *[new_m[n] for n in names], *[new_v[n] for n in names])
```

```python
import functools
import math

import jax
import jax.numpy as jnp
from jax import lax
from jax.experimental import pallas as pl
from jax.experimental.pallas import tpu as pltpu

F32 = jnp.float32
BF16 = jnp.bfloat16
MXU_DTYPE = jnp.bfloat16
MESH_IDS = pl.DeviceIdType.MESH
HIGHEST = lax.Precision.HIGHEST

D = 1024
RW, NH, HD = 512, 8, 64
LW, LA, LG = 64, 64, 128
SW, SGC, NG, SP = 512, 16, 32, 64
NSG = 4
SHIFT = 3 * RW + LW + LA + LG
DFF = 2816
RMS_EPS, GN_EPS, L2_EPS = 1e-6, 64e-5, 1e-12
LR, B1, B2, ADAM_EPS, WD, STEP = 0.001, 0.9, 0.999, 1e-8, 0.01, 10
DECAY_SCALE = math.exp(-0.5)
GELU_C = math.sqrt(2.0 / math.pi)

VMEM_LIMIT = 52 * 1024 * 1024
SUBLANES, LANES = 8, 128
HALO = 16


def _pick(n, cap):
    if n <= cap:
        return n
    best = None
    for t in range(LANES, cap + 1, LANES):
        if n % t == 0:
            best = t
    assert best is not None, (n, cap)
    return best


def _params(sem=None, vmem=VMEM_LIMIT):
    return pltpu.CompilerParams(dimension_semantics=sem, vmem_limit_bytes=vmem)


def _chip_of(p):
    return 2 * p[0] + p[1]


def _me():
    return (lax.axis_index("x"), lax.axis_index("y"), lax.axis_index("c"))


def _half(rows, core):
    h = rows // 2
    return pl.ds(pl.multiple_of(core * h, 16 if h % 16 == 0 else SUBLANES), h)


_HBM =pl.BlockSpec(memory_space=pltpu.HBM)
_SEM = pl.BlockSpec(memory_space=pltpu.SEMAPHORE)
_DATAFLOW = pltpu.SideEffectType.DATAFLOW_SIDE_EFFECTING


def _split_copies(flips, moves, src_refs, land_refs, send_sems, recv_sems):
    me = _me()
    nf = len(flips)
    out = []
    for m, (si, li, src_sel, dst_sel) in enumerate(moves):
        for k, f in enumerate(flips):
            peer = tuple(1 - v if b else v for v, b in zip(me, f))
            out.append(pltpu.make_async_remote_copy(
                src_ref=src_sel(src_refs[si], me, peer), dst_ref=dst_sel(land_refs[li], me, k),
                send_sem=send_sems.at[m * nf + k], recv_sem=recv_sems.at[m * nf + k],
                device_id=peer, device_id_type=MESH_IDS))
    return out


def _send_start(name, flips, srcs, land_shapes, moves):
    ns, nl = len(srcs), len(land_shapes)
    n = len(moves) * len(flips)

    def body(*refs):
        for cp in _split_copies(flips, moves, refs[:ns], refs[ns:ns + nl], refs[ns + nl], refs[ns + nl + 1]):
            cp.start()
        refs[-1][...] = jnp.zeros(refs[-1].shape, F32)

    hbm = lambda z: pltpu.with_memory_space_constraint(z, pltpu.HBM)
    lands = [lax.empty(s.shape, s.dtype) for s in land_shapes]
    res = pl.pallas_call(
        body, name=name,
        out_shape=(pltpu.SemaphoreType.DMA((n,)), pltpu.SemaphoreType.DMA((n,)),
                   *[pltpu.HBM(z.shape, z.dtype) for z in srcs], *[pltpu.HBM(s.shape, s.dtype) for s in land_shapes],
                   jax.ShapeDtypeStruct((SUBLANES, LANES), F32)),
        in_specs=[_HBM] * (ns + nl),
        out_specs=(_SEM, _SEM, *[_HBM] * (ns + nl), pl.BlockSpec(memory_space=pltpu.VMEM)),
        input_output_aliases={i: 2 + i for i in range(ns + nl)},
        compiler_params=pltpu.CompilerParams(has_side_effects=_DATAFLOW),
    )(*[hbm(z) for z in srcs], *[hbm(z) for z in lands])
    return {"sems": res[:2], "srcs": list(res[2:2 + ns]), "lands": list(res[2 + ns:2 + ns + nl]), "token": res[-1][0, 0]}


def _send_wait(name, flips, started, moves, after):
    srcs, lands = started["srcs"], started["lands"]
    ns, nl = len(srcs), len(lands)

    def body(*refs):
        for cp in _split_copies(flips, moves, refs[:ns], refs[ns:ns + nl], refs[ns + nl], refs[ns + nl + 1]):
            cp.wait_send()
            cp.wait_recv()

    res = pl.pallas_call(
        body, name=name, out_shape=[pltpu.HBM(z.shape, z.dtype) for z in srcs + lands],
        in_specs=[_HBM] * (ns + nl) + [_SEM, _SEM, pl.BlockSpec(memory_space=pl.ANY)],
        out_specs=[_HBM] * (ns + nl), input_output_aliases={i: i for i in range(ns + nl)},
        compiler_params=pltpu.CompilerParams(has_side_effects=_DATAFLOW),
    )(*srcs, *lands, *started["sems"], after)
    return list(res[:ns]), list(res[ns:])


CHIP_FLIPS = ((1, 0, 0), (0, 1, 0), (1, 1, 0))
PAIR_FLIPS = ((0, 0, 1),)
ALL_FLIPS = CHIP_FLIPS + ((1, 0, 1), (0, 1, 1), (1, 1, 1)) + PAIR_FLIPS


def _gather_two_level(chip_arrs, dev_arrs, name):
    arrs = list(chip_arrs) + list(dev_arrs)
    n, nchip = len(arrs), len(chip_arrs)
    NS = 7

    def body(*refs):
        srcs, outs = refs[:n], refs[n:2 * n]
        send_sems, recv_sems, loc_sems = refs[2 * n:]
        x, y, c = _me()
        sib = (x, y, 1 - c)
        chips = [(1 - x, y), (x, 1 - y), (1 - x, 1 - y)]
        mine = 2 * x + y
        ids = [2 * cx + cy for cx, cy in chips]

        def part(i, slot, core):
            if i < nchip:
                return outs[i].at[slot, _half(arrs[i].shape[0], core)]
            return outs[i].at[slot, core]

        def rcopy(i, k, src, dst, to):
            return pltpu.make_async_remote_copy(src_ref=src, dst_ref=dst, send_sem=send_sems.at[i * NS + k],
                                                recv_sem=recv_sems.at[i * NS + k], device_id=to, device_id_type=MESH_IDS)

        started, locs = [], []
        for i in range(n):
            own = srcs[i].at[_half(arrs[i].shape[0], c)] if i < nchip else srcs[i]
            loc = pltpu.make_async_copy(srcs[i], outs[i].at[mine] if i < nchip else outs[i].at[mine, c], loc_sems.at[i])
            loc.start()
            locs.append(loc)
            for f, chip in enumerate(chips):
                cp = rcopy(i, f, own, part(i, mine, c), (*chip, c))
                cp.start()
                started.append(cp)
            if i >= nchip:
                cp = rcopy(i, 6, own, part(i, mine, c), sib)
                cp.start()
                started.append(cp)
        for i in range(n):
            for f in range(3):
                land = part(i, ids[f], c)
                rcopy(i, f, land, land, sib).wait_recv()
                fw = rcopy(i, 3 + f, land, land, sib)
                fw.start()
                started.append(fw)
        for i in range(n):
            for f in range(3):
                land = part(i, ids[f], 1 - c)
                rcopy(i, 3 + f, land, land, sib).wait_recv()
            if i >= nchip:
                land = part(i, mine, 1 - c)
                rcopy(i, 6, land, land, sib).wait_recv()
        for cp in started:
            cp.wait_send()
        for loc in locs:
            loc.wait()

    outs = [jax.ShapeDtypeStruct((4,) + a.shape, a.dtype) for a in chip_arrs]
    outs += [jax.ShapeDtypeStruct((4, 2) + a.shape, a.dtype) for a in dev_arrs]
    res = pl.pallas_call(
        body, name=name, out_shape=outs,
        in_specs=[pl.BlockSpec(memory_space=pl.ANY)] * n, out_specs=[pl.BlockSpec(memory_space=pl.ANY)] * n,
        scratch_shapes=[pltpu.SemaphoreType.DMA((n * NS,)), pltpu.SemaphoreType.DMA((n * NS,)),
                        pltpu.SemaphoreType.DMA((n,))],
    )(*arrs)
    return res[:nchip], res[nchip:]


def _mm(As, Bs, out_dtype, name, tm=512, cap=1408, bt=False):
    n = len(As)
    M, N = As[0].shape[0], Bs[0].shape[0 if bt else 1]
    tm = min(tm, M)
    tn = _pick(N, cap)
    dims = (((1,), (1,)), ((), ())) if bt else (((1,), (0,)), ((), ()))

    def body(*refs):
        o = refs[2 * n]
        acc = None
        for a, b in zip(refs[:n], refs[n:2 * n]):
            d = lax.dot_general(a[...].astype(MXU_DTYPE), b[...].astype(MXU_DTYPE), dims, preferred_element_type=F32)
            acc = d if acc is None else acc + d
        o[...] = acc.astype(o.dtype)

    in_specs = [pl.BlockSpec((tm, a.shape[1]), lambda i, j: (i, 0)) for a in As]
    if bt:
        in_specs += [pl.BlockSpec((tn, b.shape[1]), lambda i, j: (j, 0)) for b in Bs]
    else:
        in_specs += [pl.BlockSpec((b.shape[0], tn), lambda i, j: (0, j)) for b in Bs]
    return pl.pallas_call(
        body, name=name, grid=(M // tm, N // tn), in_specs=in_specs,
        out_specs=pl.BlockSpec((tm, tn), lambda i, j: (i, j)),
        out_shape=jax.ShapeDtypeStruct((M, N), out_dtype),
        compiler_params=_params(("parallel", "parallel")),
    )(*As, *Bs)


def _mm_tn(A, G, name, tt=1024, cap=1408):
    T, Ka = A.shape
    N = G.shape[1]
    tt = min(tt, T)
    tk = _pick(Ka, cap)
    tn = _pick(N, cap)

    def body(a, g, o):
        @pl.when(pl.program_id(2) == 0)
        def _():
            o[...] = jnp.zeros(o.shape, F32)
        o[...] += lax.dot_general(a[...].astype(MXU_DTYPE), g[...].astype(MXU_DTYPE),
                                  (((0,), (0,)), ((), ())), preferred_element_type=F32)

    return pl.pallas_call(
        body, name=name, grid=(Ka // tk, N // tn, T // tt),
        in_specs=[pl.BlockSpec((tt, tk), lambda i, j, t: (t, i)), pl.BlockSpec((tt, tn), lambda i, j, t: (t, j))],
        out_specs=pl.BlockSpec((tk, tn), lambda i, j, t: (i, j)),
        out_shape=jax.ShapeDtypeStruct((Ka, N), F32),
        compiler_params=_params(("parallel", "parallel", "arbitrary")),
    )(A, G)


def _rowwise(name, fn, *, Bl, S, R, tiled=(), prev=(), nxt=(), batch=(), full=(),
             out_tiled=(), out_batch=(), out_acc=()):
    R = min(R, S)
    nS = S // R
    T = Bl * S
    hb = R // HALO
    n_in = len(tiled) + len(prev) + len(nxt) + len(batch) + len(full)

    in_specs, args = [], []
    for a, wd, cb in tiled:
        in_specs.append(pl.BlockSpec((R, wd), lambda b, i, cb=cb: (b * nS + i, cb)))
        args.append(a)
    for a, wd, cb in prev:
        in_specs.append(pl.BlockSpec((HALO, wd), lambda b, i, cb=cb: (jnp.maximum((b * nS + i) * hb - 1, 0), cb)))
        args.append(a)
    for a, wd, cb in nxt:
        in_specs.append(pl.BlockSpec((HALO, wd), lambda b, i, cb=cb: (jnp.minimum((b * nS + i + 1) * hb, T // HALO - 1), cb)))
        args.append(a)
    for a, wd, cb in batch:
        in_specs.append(pl.BlockSpec((1, 1, wd), lambda b, i, cb=cb: (b, 0, cb)))
        args.append(a)
    for a in full:
        in_specs.append(pl.BlockSpec(a.shape, lambda b, i, nd=a.ndim: (0,) * nd))
        args.append(a)

    out_specs, out_shape = [], []
    for C, dt in out_tiled:
        out_specs.append(pl.BlockSpec((R, C), lambda b, i: (b * nS + i, 0)))
        out_shape.append(jax.ShapeDtypeStruct((T, C), dt))
    for C in out_batch:
        out_specs.append(pl.BlockSpec((1, 1, C), lambda b, i: (b, 0, 0)))
        out_shape.append(jax.ShapeDtypeStruct((Bl, 1, C), F32))
    for shp in out_acc:
        out_specs.append(pl.BlockSpec(shp, lambda b, i, nd=len(shp): (0,) * nd))
        out_shape.append(jax.ShapeDtypeStruct(shp, F32))

    nt, npv, nnx, nbt = len(tiled), len(prev), len(nxt), len(batch)

    def body(*refs):
        b, i = pl.program_id(0), pl.program_id(1)
        ins, outs = refs[:n_in], refs[n_in:]
        vals = [r[...] for r in ins[:nt]]
        vals += [jnp.where(i > 0, r[...], jnp.zeros(r.shape, r.dtype)) for r in ins[nt:nt + npv]]
        vals += [jnp.where(i < nS - 1, r[...], jnp.zeros(r.shape, r.dtype)) for r in ins[nt + npv:nt + npv + nnx]]
        vals += [r[0] for r in ins[nt + npv + nnx:nt + npv + nnx + nbt]]
        vals += [r[...] for r in ins[nt + npv + nnx + nbt:]]
        res = fn(*vals)
        if not isinstance(res, (tuple, list)):
            res = (res,)
        k = 0
        for _ in out_tiled:
            outs[k][...] = res[k].astype(outs[k].dtype)
            k += 1
        for _ in out_batch:
            o = outs[k]

            @pl.when(i == 0)
            def _(o=o):
                o[...] = jnp.zeros(o.shape, F32)
            o[0] += res[k]
            k += 1
        for _ in out_acc:
            o = outs[k]

            @pl.when((i == 0) & (b == 0))
            def _(o=o):
                o[...] = jnp.zeros(o.shape, F32)
            o[...] += res[k]
            k += 1

    out = pl.pallas_call(
        body, name=name, grid=(Bl, nS), in_specs=in_specs, out_specs=out_specs, out_shape=out_shape,
        compiler_params=_params(("arbitrary", "arbitrary")),
    )(*args)
    return out


def _shift_down(x, halo, k):
    rolled = pltpu.roll(x, k, 0)
    row = lax.broadcasted_iota(jnp.int32, (SUBLANES, x.shape[1]), 0)
    head = rolled[0:SUBLANES]
    for j in range(k):
        head = jnp.where(row == j, halo[HALO - k + j:HALO - k + j + 1, :], head)
    return jnp.concatenate([head, rolled[SUBLANES:]], axis=0)


def _shift_up(x, halo, k):
    n = x.shape[0]
    rolled = pltpu.roll(x, n - k, 0)
    row = lax.broadcasted_iota(jnp.int32, (SUBLANES, x.shape[1]), 0)
    tail = rolled[n - SUBLANES:]
    for j in range(k):
        tail = jnp.where(row == SUBLANES - k + j, halo[j:j + 1, :], tail)
    return jnp.concatenate([rolled[:n - SUBLANES], tail], axis=0)


def _dotm(a, b):
    return jnp.dot(a.astype(MXU_DTYPE), b.astype(MXU_DTYPE), preferred_element_type=F32)


def _split_bf16(x):
    hi = x.astype(BF16)
    return hi, (x - hi.astype(F32)).astype(BF16)


def _headsum_2pass(x, hm):
    hi, lo = _split_bf16(x)
    hb = hm.astype(BF16)
    return jnp.dot(hi, hb, preferred_element_type=F32) + jnp.dot(lo, hb, preferred_element_type=F32)


@jax.custom_vjp
def _headsum(x, hm):
    return _headsum_2pass(x, hm)


_headsum.defvjp(lambda x, hm: (_headsum_2pass(x, hm), hm),
                lambda hm, g: (_headsum_2pass(g, hm), jnp.zeros_like(hm)))


def _sigmoid(x):
    return 1.0 / (1.0 + jnp.exp(-x))


def _rms(x, g):
    return x * lax.rsqrt(jnp.mean(x * x, axis=-1, keepdims=True) + RMS_EPS) * g


def _norm_mod(x, g, sc, sh):
    return _rms(x, g) * (1.0 + sc) + sh


def _split_ps(ps):
    return (ps[:, 0:RW], ps[:, RW:2 * RW], ps[:, 2 * RW:3 * RW], ps[:, 3 * RW:3 * RW + LW + LA],
            ps[:, 3 * RW + LW + LA:SHIFT])


def _rwkv_prep(r, k, v, wa, gd, w0, w_up_p, a0, a_up_p, g_up, k_k, k_a, hm):
    w_raw = w0 + _dotm(jnp.tanh(wa), w_up_p)
    decay = jnp.exp(-DECAY_SCALE * _sigmoid(w_raw))
    a = _sigmoid(a0 + _dotm(wa, a_up_p))
    g = _dotm(_sigmoid(gd), g_up)
    kk = k * k_k
    kk = kk * lax.rsqrt(_headsum(kk * kk, hm) + L2_EPS)
    k2 = k * (1.0 + (a - 1.0) * k_a)
    return r, decay, k2, v, -kk, kk * a, g


def _rwkv_post(y, r, k2, v, g, ln_g, ln_b, r_k, hm):
    mean = _headsum(y, hm) * (1.0 / HD)
    yc = y - mean
    var = _headsum(yc * yc, hm) * (1.0 / HD)
    yn = yc * lax.rsqrt(var + GN_EPS) * ln_g + ln_b
    bonus = _headsum(r * k2 * r_k, hm) * v
    return (yn + bonus) * g


def _gelu(x):
    return 0.5 * x * (1.0 + jnp.tanh(GELU_C * (x + 0.044715 * (x * x * x))))


def _s5_post(yssm, u, d):
    return _gelu(yssm + d * u)


def _mix(ga, gb, ya, za, zb):
    return _sigmoid(ga) * ya + _sigmoid(gb) * (za * _sigmoid(zb))


def _conv_act(up_g, up_u, hg, hu, w_g, w_u, b_g, b_u):
    gate, upv = _conv3(up_g, hg, w_g, b_g)[0], _conv3(up_u, hu, w_u, b_u)[0]
    return gate, upv


def _conv3(x, h, w, b):
    x, h = x.astype(F32), h.astype(F32)
    s2, s1 = _shift_down(x, h, 2), _shift_down(x, h, 1)
    return b + w[0:1] * s2 + w[1:2] * s1 + w[2:3] * x, (s2, s1, x)


def _silu_gate(gate, upv):
    return gate * _sigmoid(gate) * upv


WKV_L = 64
_NT, _NN, _TN = ((1,), (1,)), ((1,), (0,)), ((0,), (0,))


def _dotw(x, y, dims):
    return lax.dot_general(x.astype(MXU_DTYPE), y.astype(MXU_DTYPE), (dims, ((), ())), preferred_element_type=F32)


def _dot3(x, y, dims):
    (xh, xl), (yh, yl) = _split_bf16(x), _split_bf16(y)
    d = lambda p, q: lax.dot_general(p, q, (dims, ((), ())), preferred_element_type=F32)
    return d(xh, yh) + d(xh, yl) + d(xl, yh)


@jax.custom_vjp
def _gram3(x, y):
    return _dot3(x, y, _NT)


_gram3.defvjp(lambda x, y: (_dot3(x, y, _NT), (x, y)),
              lambda res, g: (_dot3(g, res[1], _NN), _dot3(g, res[0], _TN)))


def _wkv_chunk(s0, r, w, k, v, a, b):
    y, s1 = _wkv_chunks((s0,), (r,), (w,), (k,), (v,), (a,), (b,))
    return y[0], s1[0]


def _wkv_chunks(s0, r, w, k, v, a, b):
    each = lambda f, *ls: tuple(f(*xs) for xs in zip(*ls))
    L = r[0].shape[0]
    n2 = 2 * L
    lane_head = lax.broadcasted_iota(jnp.int32, (2, 1, 2 * HD), 2) // HD
    head_mask = (lane_head == lax.broadcasted_iota(jnp.int32, (2, 1, 2 * HD), 0)).astype(F32)
    ri = lax.broadcasted_iota(jnp.int32, (n2, n2), 0)
    ci = lax.broadcasted_iota(jnp.int32, (n2, n2), 1)
    same = (ri // L) == (ci // L)
    strict = same & ((ci % L) < (ri % L))
    incl = same & ((ci % L) <= (ri % L))
    si = lax.broadcasted_iota(jnp.int32, (2 * HD, 2 * HD), 0) // HD
    sj = lax.broadcasted_iota(jnp.int32, (2 * HD, 2 * HD), 1) // HD
    tri = (lax.broadcasted_iota(jnp.int32, (L, L), 0) >= lax.broadcasted_iota(jnp.int32, (L, L), 1)).astype(F32)

    stack = lambda z: (z[None] * head_mask).reshape(n2, 2 * HD)
    dup = lambda z: jnp.broadcast_to(z[None], (2, L, 2 * HD)).reshape(n2, 2 * HD)
    gram = _gram3
    nt, nn, tn = (lambda x, y, d=d: _dotw(x, y, d) for d in (_NT, _NN, _TN))
    add = lambda x, y: x + y

    lw = each(jnp.log, w)
    cum = each(lambda z: jnp.dot(tri, z, preferred_element_type=F32, precision=HIGHEST), lw)
    tot = each(lambda z: jnp.sum(z, axis=0, keepdims=True), lw)
    a2 = each(lambda av, cv, lv: stack(av * jnp.exp(cv - lv)), a, cum, lw)
    r2 = each(lambda rv, cv: stack(rv * jnp.exp(cv)), r, cum)
    v2 = each(stack, v)
    b2 = each(lambda bv, cv: dup(bv * jnp.exp(-cv)), b, cum)
    k2 = each(lambda kv, cv: dup(kv * jnp.exp(-cv)), k, cum)
    n_ab = each(lambda x, y: jnp.where(strict, gram(x, y), 0.0), a2, b2)
    n_ak = each(lambda x, y: jnp.where(strict, gram(x, y), 0.0), a2, k2)
    m_rb = each(lambda x, y: jnp.where(incl, gram(x, y), 0.0), r2, b2)
    m_rk = each(lambda x, y: jnp.where(incl, gram(x, y), 0.0), r2, k2)
    u = each(add, each(nt, a2, s0), each(nn, n_ak, v2))
    q = n_ab
    steps = L.bit_length() - 1
    for i in range(steps):
        u = each(add, u, each(nn, q, u))
        if i < steps - 1:
            q = each(nn, q, q)
    y2 = each(lambda x, y, z: x + y + z, each(nt, r2, s0), each(nn, m_rb, u), each(nn, m_rk, v2))
    y = each(lambda z: jnp.sum(z.reshape(2, L, 2 * HD), axis=0), y2)
    b3 = each(lambda bv, tv, cv: dup(bv * jnp.exp(tv - cv)), b, tot, cum)
    k3 = each(lambda kv, tv, cv: dup(kv * jnp.exp(tv - cv)), k, tot, cum)
    upd = each(add, each(tn, u, b3), each(tn, v2, k3))
    s1 = each(lambda sv, tv, uv: sv * jnp.exp(tv) + jnp.where(si == sj, uv, 0.0), s0, tot, upd)
    return y, s1


NPAIR = NH // 2


def _wkv_nb(Bl):
    return 2 if Bl % 2 == 0 else 1


def _wkv_fwd(r, w, k, v, a, b, Bl, S):
    L = WKV_L
    nC = S // L
    nb = _wkv_nb(Bl)
    chains = [(bi, p, slice(p * 2 * HD, (p + 1) * 2 * HD)) for bi in range(nb) for p in range(NPAIR)]

    def body(r_ref, w_ref, k_ref, v_ref, a_ref, b_ref, y_ref, ck_ref, s_ref):
        @pl.when(pl.program_id(1) == 0)
        def _():
            s_ref[...] = jnp.zeros(s_ref.shape, F32)
        s0 = tuple(s_ref[bi, p] for bi, p, _ in chains)
        ops = [tuple(z[bi, :, cs] for bi, _, cs in chains) for z in (r_ref, w_ref, k_ref, v_ref, a_ref, b_ref)]
        y, s1 = _wkv_chunks(s0, *ops)
        for i, (bi, p, cs) in enumerate(chains):
            ck_ref[bi, 0, p] = s0[i]
            y_ref[bi, :, cs] = y[i]
            s_ref[bi, p] = s1[i]

    to3 = lambda z: z.reshape(Bl, S, RW)
    row_spec = pl.BlockSpec((nb, L, RW), lambda g, c: (g, c, 0))
    y, ck = pl.pallas_call(
        body, name="wkv_fwd", grid=(Bl // nb, nC), in_specs=[row_spec] * 6,
        out_specs=[row_spec, pl.BlockSpec((nb, 1, NPAIR, 2 * HD, 2 * HD), lambda g, c: (g, c, 0, 0, 0))],
        out_shape=[jax.ShapeDtypeStruct((Bl, S, RW), F32), jax.ShapeDtypeStruct((Bl, nC, NPAIR, 2 * HD, 2 * HD), F32)],
        scratch_shapes=[pltpu.VMEM((nb, NPAIR, 2 * HD, 2 * HD), F32)],
        compiler_params=_params(("arbitrary", "arbitrary")),
    )(*(to3(z) for z in (r, w, k, v, a, b)))
    return y.reshape(Bl * S, RW), ck


def _wkv_bwd(r, w, k, v, a, b, dy, ck, Bl, S):
    L = WKV_L
    nC = S // L
    nb = _wkv_nb(Bl)
    chains = [(bi, p, slice(p * 2 * HD, (p + 1) * 2 * HD)) for bi in range(nb) for p in range(NPAIR)]

    def body(r_ref, w_ref, k_ref, v_ref, a_ref, b_ref, dy_ref, ck_ref,
             dr_ref, dw_ref, dk_ref, dv_ref, da_ref, db_ref, ds_ref):
        @pl.when(pl.program_id(1) == 0)
        def _():
            ds_ref[...] = jnp.zeros(ds_ref.shape, F32)
        s0 = tuple(ck_ref[bi, 0, p] for bi, p, _ in chains)
        ops = [tuple(z[bi, :, cs] for bi, _, cs in chains) for z in (r_ref, w_ref, k_ref, v_ref, a_ref, b_ref)]
        cts = (tuple(dy_ref[bi, :, cs] for bi, _, cs in chains), tuple(ds_ref[bi, p] for bi, p, _ in chains))
        ds0, *grads = jax.vjp(_wkv_chunks, s0, *ops)[1](cts)
        for i, (bi, p, cs) in enumerate(chains):
            ds_ref[bi, p] = ds0[i]
            for o, g in zip((dr_ref, dw_ref, dk_ref, dv_ref, da_ref, db_ref), grads):
                o[bi, :, cs] = g[i]

    to3 = lambda z: z.reshape(Bl, S, RW)
    row_spec = pl.BlockSpec((nb, L, RW), lambda g, c: (g, nC - 1 - c, 0))
    rows = jax.ShapeDtypeStruct((Bl, S, RW), F32)
    outs = pl.pallas_call(
        body, name="wkv_bwd", grid=(Bl // nb, nC),
        in_specs=[row_spec] * 7 + [pl.BlockSpec((nb, 1, NPAIR, 2 * HD, 2 * HD), lambda g, c: (g, nC - 1 - c, 0, 0, 0))],
        out_specs=[row_spec] * 6, out_shape=[rows] * 6,
        scratch_shapes=[pltpu.VMEM((nb, NPAIR, 2 * HD, 2 * HD), F32)],
        compiler_params=_params(("arbitrary", "arbitrary")),
    )(*(to3(z) for z in (r, w, k, v, a, b, dy)), ck)
    return [o.reshape(Bl * S, RW) for o in outs]


NST = NG * SP


def _cmul(ar, ai, br, bi):
    return ar * br - ai * bi, ar * bi + ai * br


def _s5_tiles(are, aim, reverse):
    if reverse:
        aim = -aim
    row = lax.broadcasted_iota(jnp.int32, (SUBLANES, NST), 0)
    pw = [(are, aim)]
    for _ in range(SUBLANES - 1):
        pw.append(_cmul(pw[-1][0], pw[-1][1], are, aim))
    bc = lambda z: jnp.broadcast_to(z, (SUBLANES, NST))
    ms = []
    for kk in (1, 2, 4):
        cond = (row < SUBLANES - kk) if reverse else (row >= kk)
        ms.append((jnp.where(cond, bc(pw[kk - 1][0]), 0.0), jnp.where(cond, bc(pw[kk - 1][1]), 0.0)))
    pr = jnp.zeros((SUBLANES, NST), F32)
    pi = jnp.zeros((SUBLANES, NST), F32)
    for i in range(SUBLANES):
        n = SUBLANES - i if reverse else i + 1
        pr = jnp.where(row == i, bc(pw[n - 1][0]), pr)
        pi = jnp.where(row == i, bc(pw[n - 1][1]), pi)
    return ms, (pr, pi)


def _s5_block(re, im, ms, pc, cre, cim, sg, reverse):
    ln = slice(sg * 512, (sg + 1) * 512)
    for (mr, mi), kk in zip(ms, (1, 2, 4)):
        sh = SUBLANES - kk if reverse else kk
        sre, sim = pltpu.roll(re, sh, 0), pltpu.roll(im, sh, 0)
        tr, ti = _cmul(mr[:, ln], mi[:, ln], sre, sim)
        re, im = re + tr, im + ti
    tr, ti = _cmul(pc[0][:, ln], pc[1][:, ln], cre[:, ln], cim[:, ln])
    return re + tr, im + ti


def _s5_scan(X_ref, n_rows, ms, pc, cre, cim, reverse, visit=None, acc0=None):
    nblk = n_rows // SUBLANES

    def it(i, carry):
        cre, cim, acc = carry
        j = nblk - 1 - i if reverse else i
        rows = pl.ds(pl.multiple_of(j * SUBLANES, SUBLANES), SUBLANES)
        edge = 0 if reverse else SUBLANES - 1
        blocks, ncre, ncim = [], [], []
        for sg in range(NSG):
            lr = slice(sg * 1024, sg * 1024 + 512)
            li = slice(sg * 1024 + 512, (sg + 1) * 1024)
            re, im = _s5_block(X_ref[rows, lr], X_ref[rows, li], ms, pc, cre, cim, sg, reverse)
            X_ref[rows, lr] = re
            X_ref[rows, li] = im
            blocks.append((re, im))
            ncre.append(re[edge:edge + 1])
            ncim.append(im[edge:edge + 1])
        if visit is not None:
            acc = visit(j, blocks, acc)
        return jnp.concatenate(ncre, axis=1), jnp.concatenate(ncim, axis=1), acc

    return lax.fori_loop(0, nblk, it, (cre, cim, acc0 if acc0 is not None else 0))


def _s5_fwd(u, wb, wc, ab, Bl, S, R=256):
    R = min(R, S)
    nC = S // R

    def body(u_ref, wb_ref, wc_ref, ab_ref, y_ref, st_ref, X_ref, car_ref):
        @pl.when(pl.program_id(1) == 0)
        def _():
            car_ref[...] = jnp.zeros(car_ref.shape, F32)
        st_ref[0, 0] = car_ref[...]
        ms, pc = _s5_tiles(ab_ref[0:1], ab_ref[1:2], False)
        for sg in range(NSG):
            X_ref[:, sg * 1024:(sg + 1) * 1024] = _dotm(u_ref[:, sg * 128:(sg + 1) * 128], wb_ref[sg])
        cre, cim, _ = _s5_scan(X_ref, R, ms, pc, car_ref[0:1], car_ref[1:2], False)
        car_ref[0:1] = cre
        car_ref[1:2] = cim
        for sg in range(NSG):
            y_ref[:, sg * 128:(sg + 1) * 128] = _dotm(X_ref[:, sg * 1024:(sg + 1) * 1024], wc_ref[sg])

    return pl.pallas_call(
        body, name="s5_fwd", grid=(Bl, nC),
        in_specs=[pl.BlockSpec((R, SW), lambda b, c: (b * nC + c, 0)),
                  pl.BlockSpec(wb.shape, lambda b, c: (0, 0, 0)), pl.BlockSpec(wc.shape, lambda b, c: (0, 0, 0)),
                  pl.BlockSpec(ab.shape, lambda b, c: (0, 0))],
        out_specs=[pl.BlockSpec((R, SW), lambda b, c: (b * nC + c, 0)),
                   pl.BlockSpec((1, 1, 2, NST), lambda b, c: (b, c, 0, 0)),
                   pl.BlockSpec((R, 2 * NST), lambda b, c: (b * nC + c, 0))],
        out_shape=[jax.ShapeDtypeStruct((Bl * S, SW), F32), jax.ShapeDtypeStruct((Bl, nC, 2, NST), F32),
                   jax.ShapeDtypeStruct((Bl * S, 2 * NST), F32)],
        scratch_shapes=[pltpu.VMEM((2, NST), F32)],
        compiler_params=_params(("arbitrary", "arbitrary")),
    )(u, wb, wc, ab)


def _s5_bwd(u, dy, wb, wc, ab, st, xs, Bl, S, R=256):
    R = min(R, S)
    nC = S // R

    def body(u_ref, dy_ref, wb_ref, wc_ref, ab_ref, st_ref, X_ref, du_ref, dwb_ref, dwc_ref, dab_ref,
             G_ref, car_ref):
        first = (pl.program_id(0) == 0) & (pl.program_id(1) == 0)

        @pl.when(first)
        def _():
            dwb_ref[...] = jnp.zeros(dwb_ref.shape, F32)
            dwc_ref[...] = jnp.zeros(dwc_ref.shape, F32)
            dab_ref[...] = jnp.zeros(dab_ref.shape, F32)

        @pl.when(pl.program_id(1) == 0)
        def _():
            car_ref[...] = jnp.zeros(car_ref.shape, F32)

        are, aim = ab_ref[0:1], ab_ref[1:2]
        dyv = dy_ref[...].astype(MXU_DTYPE)
        for sg in range(NSG):
            G_ref[:, sg * 1024:(sg + 1) * 1024] = lax.dot_general(
                dyv[:, sg * 128:(sg + 1) * 128], wc_ref[sg].astype(MXU_DTYPE), (((1,), (1,)), ((), ())),
                preferred_element_type=F32)
        rms_, rpc = _s5_tiles(are, aim, True)
        row = lax.broadcasted_iota(jnp.int32, (SUBLANES, 512), 0)

        def visit(j, blocks, acc):
            before = pl.multiple_of(jnp.maximum(j - 1, 0) * SUBLANES, SUBLANES)
            prow = X_ref[pl.ds(before, SUBLANES), :][SUBLANES - 1:SUBLANES]
            rows = pl.ds(pl.multiple_of(j * SUBLANES, SUBLANES), SUBLANES)
            are_acc, aim_acc = [], []
            for sg in range(NSG):
                lr = slice(sg * 1024, sg * 1024 + 512)
                li = slice(sg * 1024 + 512, (sg + 1) * 1024)
                ln = slice(sg * 512, (sg + 1) * 512)
                pre = jnp.where(j > 0, prow[:, lr], st_ref[0, 0, 0:1, ln])
                pim = jnp.where(j > 0, prow[:, li], st_ref[0, 0, 1:2, ln])
                xre = jnp.where(row == 0, pre, pltpu.roll(X_ref[rows, lr], 1, 0))
                xim = jnp.where(row == 0, pim, pltpu.roll(X_ref[rows, li], 1, 0))
                dre, dim = blocks[sg]
                are_acc.append(dre * xre + dim * xim)
                aim_acc.append(dim * xre - dre * xim)
            return acc[0] + jnp.concatenate(are_acc, axis=1), acc[1] + jnp.concatenate(aim_acc, axis=1)

        zero = jnp.zeros((SUBLANES, NST), F32)
        cre, cim, acc = _s5_scan(G_ref, R, rms_, rpc, car_ref[0:1], car_ref[1:2], True, visit, (zero, zero))
        car_ref[0:1] = cre
        car_ref[1:2] = cim
        dab_ref[0:1] += jnp.sum(acc[0], axis=0, keepdims=True)
        dab_ref[1:2] += jnp.sum(acc[1], axis=0, keepdims=True)
        uv = u_ref[...].astype(MXU_DTYPE)
        for sg in range(NSG):
            cs = slice(sg * 1024, (sg + 1) * 1024)
            us = slice(sg * 128, (sg + 1) * 128)
            gx = G_ref[:, cs].astype(MXU_DTYPE)
            dwb_ref[sg] += lax.dot_general(uv[:, us], gx, (((0,), (0,)), ((), ())), preferred_element_type=F32)
            dwc_ref[sg] += lax.dot_general(X_ref[:, cs].astype(MXU_DTYPE), dyv[:, us], (((0,), (0,)), ((), ())),
                                           preferred_element_type=F32)
            du_ref[:, us] = lax.dot_general(gx, wb_ref[sg].astype(MXU_DTYPE), (((1,), (1,)), ((), ())),
                                            preferred_element_type=F32)

    rmap = lambda b, c: (b * nC + nC - 1 - c, 0)
    return pl.pallas_call(
        body, name="s5_bwd", grid=(Bl, nC),
        in_specs=[pl.BlockSpec((R, SW), rmap), pl.BlockSpec((R, SW), rmap),
                  pl.BlockSpec(wb.shape, lambda b, c: (0, 0, 0)), pl.BlockSpec(wc.shape, lambda b, c: (0, 0, 0)),
                  pl.BlockSpec(ab.shape, lambda b, c: (0, 0)),
                  pl.BlockSpec((1, 1, 2, NST), lambda b, c: (b, nC - 1 - c, 0, 0)),
                  pl.BlockSpec((R, 2 * NST), rmap)],
        out_specs=[pl.BlockSpec((R, SW), rmap), pl.BlockSpec(wb.shape, lambda b, c: (0, 0, 0)),
                   pl.BlockSpec(wc.shape, lambda b, c: (0, 0, 0)), pl.BlockSpec((2, NST), lambda b, c: (0, 0))],
        out_shape=[jax.ShapeDtypeStruct((Bl * S, SW), F32), jax.ShapeDtypeStruct(wb.shape, F32),
                   jax.ShapeDtypeStruct(wc.shape, F32), jax.ShapeDtypeStruct((2, NST), F32)],
        scratch_shapes=[pltpu.VMEM((R, 2 * NST), F32), pltpu.VMEM((2, NST), F32)],
        compiler_params=_params(("arbitrary", "arbitrary")),
    )(u, dy, wb, wc, ab, st, xs)


def _s5_disc_math(a_re, a_im, log_dt, b_re, b_im, expand):
    dt = jnp.exp(log_dt)
    z_re, z_im = a_re * dt, a_im * dt
    mag = jnp.exp(z_re)
    ab_re, ab_im = mag * jnp.cos(z_im), mag * jnp.sin(z_im)
    den = a_re * a_re + a_im * a_im
    q_re = ((ab_re - 1.0) * a_re + ab_im * a_im) / den
    q_im = (ab_im * a_re - (ab_re - 1.0) * a_im) / den
    qe_re = jnp.dot(q_re, expand, preferred_element_type=F32, precision=HIGHEST)
    qe_im = jnp.dot(q_im, expand, preferred_element_type=F32, precision=HIGHEST)
    return ab_re, ab_im, qe_re * b_re - qe_im * b_im, qe_re * b_im + qe_im * b_re


def _whole(shape):
    return pl.BlockSpec(shape, lambda nd=len(shape): (0,) * nd)


def _s5_disc(a_re, a_im, log_dt, b_re, b_im, expand):
    def body(a, b, c, d, e, f, o0, o1, o2, o3):
        res = _s5_disc_math(a[...], b[...], c[...], d[...], e[...], f[...])
        for o, v in zip((o0, o1, o2, o3), res):
            o[...] = v
    ins = (a_re, a_im, log_dt, b_re, b_im, expand)
    outs = [jax.ShapeDtypeStruct(a_re.shape, F32)] * 2 + [jax.ShapeDtypeStruct(b_re.shape, F32)] * 2
    return pl.pallas_call(body, name="s5_disc", in_specs=[_whole(x.shape) for x in ins],
                          out_specs=[_whole(o.shape) for o in outs], out_shape=outs)(*ins)


def _s5_disc_bwd(a_re, a_im, log_dt, b_re, b_im, expand, cts):
    def body(a, b, c, d, e, f, g0, g1, g2, g3, o0, o1, o2, o3, o4):
        fn = lambda *p: _s5_disc_math(*p, f[...])
        _, vjp = jax.vjp(fn, a[...], b[...], c[...], d[...], e[...])
        for o, v in zip((o0, o1, o2, o3, o4), vjp((g0[...], g1[...], g2[...], g3[...]))):
            o[...] = v
    ins = (a_re, a_im, log_dt, b_re, b_im, expand) + tuple(cts)
    outs = [jax.ShapeDtypeStruct(x.shape, F32) for x in (a_re, a_im, log_dt, b_re, b_im)]
    return pl.pallas_call(body, name="s5_disc_bwd", in_specs=[_whole(x.shape) for x in ins],
                          out_specs=[_whole(o.shape) for o in outs], out_shape=outs)(*ins)


def _ada_fwd(c_all, w_shard, b_shard):
    def body(c_ref, w_ref, b_ref, o_ref):
        cv = c_ref[...]
        o_ref[...] = _dotm(cv * _sigmoid(cv), w_ref[...]) + b_ref[...]
    n = w_shard.shape[1]
    return pl.pallas_call(
        body, name="ada_fwd", in_specs=[_whole(c_all.shape), _whole(w_shard.shape), _whole(b_shard.shape)],
        out_specs=_whole((c_all.shape[0], n)), out_shape=jax.ShapeDtypeStruct((c_all.shape[0], n), F32),
        compiler_params=_params(),
    )(c_all, w_shard, b_shard)


def _ada_bwd(c_all, dmod_cols, dmod_all):
    def body(c_ref, dc_ref, da_ref, gw_ref, gb_ref):
        cv = c_ref[...]
        gw_ref[...] = lax.dot_general((cv * _sigmoid(cv)).astype(MXU_DTYPE), dc_ref[...].astype(MXU_DTYPE),
                                      (((0,), (0,)), ((), ())), preferred_element_type=F32)
        gb_ref[...] = jnp.sum(da_ref[...], axis=0, keepdims=True)
    n = dmod_cols.shape[1]
    return pl.pallas_call(
        body, name="ada_bwd", in_specs=[_whole(c_all.shape), _whole(dmod_cols.shape), _whole(dmod_all.shape)],
        out_specs=[_whole((D, n)), _whole((1, dmod_all.shape[1]))],
        out_shape=[jax.ShapeDtypeStruct((D, n), F32), jax.ShapeDtypeStruct((1, dmod_all.shape[1]), F32)],
        compiler_params=_params(),
    )(c_all, dmod_cols, dmod_all)


def _rows_block(n_rows, cap=512):
    if n_rows <= cap:
        return n_rows
    for t in range(cap - cap % SUBLANES, 0, -SUBLANES):
        if n_rows % t == 0:
            return t
    return n_rows


def _adamw(w, g, m, v, name):
    rows, cols = w.shape
    tr = _rows_block(rows, max(SUBLANES, (1 << 19) // max(cols, 1) // SUBLANES * SUBLANES))

    def body(w_ref, g_ref, m_ref, v_ref, d_ref, nm_ref, nv_ref):
        gv = g_ref[...]
        nm = B1 * m_ref[...] + (1.0 - B1) * gv
        nv = B2 * v_ref[...] + (1.0 - B2) * (gv * gv)
        m_hat = nm / (1.0 - B1 ** STEP)
        v_hat = nv / (1.0 - B2 ** STEP)
        d_ref[...] = -LR * (m_hat / (jnp.sqrt(v_hat) + ADAM_EPS) + WD * w_ref[...])
        nm_ref[...] = nm
        nv_ref[...] = nv

    spec = pl.BlockSpec((tr, cols), lambda i: (i, 0))
    sd = jax.ShapeDtypeStruct((rows, cols), F32)
    return pl.pallas_call(body, name=name, grid=(rows // tr,), in_specs=[spec] * 4, out_specs=[spec] * 3,
                          out_shape=[sd] * 3, compiler_params=_params(("parallel",)))(w, g, m, v)


def _sum_slots(x, out_dtype, name):
    xs = x if isinstance(x, (list, tuple)) else [x]
    _, rows, cols = xs[0].shape
    tr = _rows_block(rows)

    def body(*refs):
        acc = None
        for x_ref in refs[:-1]:
            for j in range(x_ref.shape[0]):
                term = x_ref[j].astype(F32)
                acc = term if acc is None else acc + term
        refs[-1][...] = acc.astype(refs[-1].dtype)

    return pl.pallas_call(
        body, name=name, grid=(rows // tr,),
        in_specs=[pl.BlockSpec((z.shape[0], tr, cols), lambda i: (0, i, 0)) for z in xs],
        out_specs=pl.BlockSpec((tr, cols), lambda i: (i, 0)), out_shape=jax.ShapeDtypeStruct((rows, cols), out_dtype),
        compiler_params=_params(("parallel",)))(*xs)


PACK_COLS = 1024


def _pack_rows(parts, dtype, row_mult):
    flat = jnp.concatenate([p.reshape(-1).astype(dtype) for p in parts])
    per = PACK_COLS * row_mult
    n = -(-flat.shape[0] // per) * per
    flat = jnp.pad(flat, (0, n - flat.shape[0]))
    return flat.reshape(n // PACK_COLS, PACK_COLS)


def _unpack(flat, shapes):
    out, off = [], 0
    for s in shapes:
        n = math.prod(s)
        out.append(flat[off:off + n].reshape(s))
        off += n
    return out


BIG = (("w_in", (D, SHIFT + SW + 2 * D), 1), ("w_out_rwkv", (RW, D), 1), ("w_glu", (SW, 2 * D), 1),
       ("w_out", (D, D), 0), ("w_ffn_up", (D, 2 * DFF), 1), ("w_ffn_down", (DFF, D), 0))
BIG_SMALL = (("rwkv_w_up", (LW, RW), 1), ("rwkv_a_up", (LA, RW), 1), ("rwkv_g_up", (LG, RW), 1),
             ("ffn_conv_w", (3, 2 * DFF), 1))
BIG_LATE = BIG[4:]


def _shard_shape(shape, axis):
    return (shape[0] // 4, shape[1]) if axis == 0 else (shape[0], shape[1] // 4)


def _to_shards(g, axis):
    r, C = g.shape
    return g.reshape(4, r // 4, C) if axis == 0 else g.reshape(r, 4, C // 4).transpose(1, 0, 2)


def _from_shards(x, axis):
    _, r, C = x.shape
    return x.reshape(4 * r, C) if axis == 0 else x.transpose(1, 0, 2).reshape(r, 4 * C)


def kernel(x, c, w_ada, b_ada, norm1_g, w_in, mu_shift, rwkv_w0, rwkv_w_up, rwkv_a0, rwkv_a_up, rwkv_g_up, rwkv_k_k, rwkv_k_a, rwkv_r_k, rwkv_ln_g, rwkv_ln_b, w_out_rwkv, s5_a_re, s5_a_im, s5_log_dt, s5_b_re, s5_b_im, s5_c_re, s5_c_im, s5_d, w_glu, w_out, norm2_g, w_ffn_up, ffn_conv_w, ffn_conv_b, w_ffn_down, norm_f_g, loss_target, m_w_ada, m_b_ada, m_norm1_g, m_w_in, m_mu_shift, m_rwkv_w0, m_rwkv_w_up, m_rwkv_a0, m_rwkv_a_up, m_rwkv_g_up, m_rwkv_k_k, m_rwkv_k_a, m_rwkv_r_k, m_rwkv_ln_g, m_rwkv_ln_b, m_w_out_rwkv, m_s5_a_re, m_s5_a_im, m_s5_log_dt, m_s5_b_re, m_s5_b_im, m_s5_c_re, m_s5_c_im, m_s5_d, m_w_glu, m_w_out, m_norm2_g, m_w_ffn_up, m_ffn_conv_w, m_ffn_conv_b, m_w_ffn_down, m_norm_f_g, v_w_ada, v_b_ada, v_norm1_g, v_w_in, v_mu_shift, v_rwkv_w0, v_rwkv_w_up, v_rwkv_a0, v_rwkv_a_up, v_rwkv_g_up, v_rwkv_k_k, v_rwkv_k_a, v_rwkv_r_k, v_rwkv_ln_g, v_rwkv_ln_b, v_w_out_rwkv, v_s5_a_re, v_s5_a_im, v_s5_log_dt, v_s5_b_re, v_s5_b_im, v_s5_c_re, v_s5_c_im, v_s5_d, v_w_glu, v_w_out, v_norm2_g, v_w_ffn_up, v_ffn_conv_w, v_ffn_conv_b, v_w_ffn_down, v_norm_f_g):
    names = ["w_ada", "b_ada", "norm1_g", "w_in", "mu_shift", "rwkv_w0", "rwkv_w_up", "rwkv_a0", "rwkv_a_up",
             "rwkv_g_up", "rwkv_k_k", "rwkv_k_a", "rwkv_r_k", "rwkv_ln_g", "rwkv_ln_b", "w_out_rwkv", "s5_a_re",
             "s5_a_im", "s5_log_dt", "s5_b_re", "s5_b_im", "s5_c_re", "s5_c_im", "s5_d", "w_glu", "w_out", "norm2_g",
             "w_ffn_up", "ffn_conv_w", "ffn_conv_b", "w_ffn_down", "norm_f_g"]
    env = dict(locals())
    W = {n: env[n] for n in names}
    M = {n: env["m_" + n] for n in names}
    V = {n: env["v_" + n] for n in names}

    Bl, S, _ = x.shape
    T = Bl * S
    ix, iy, ic = lax.axis_index("x"), lax.axis_index("y"), lax.axis_index("c")
    chip = 2 * ix + iy
    dev = 2 * chip + ic
    rw = functools.partial(_rowwise, Bl=Bl, S=S)

    now = [b for b in BIG if b not in BIG_LATE]
    chip_arrs = [W[n][0].astype(MXU_DTYPE) for n, _, _ in now] + [W[n][0] for n, _, _ in BIG_SMALL[:3]]
    got_chip, got_dev = _gather_two_level(chip_arrs, [W["ffn_conv_w"][0], c], "gather_w")
    full = {n: _from_shards(g, axis) for (n, _, axis), g in zip(tuple(now) + BIG_SMALL[:3], got_chip)}
    full["ffn_conv_w"] = _from_shards(got_dev[0][:, 0], 1)
    c_all = got_dev[1].reshape(8 * Bl, D)
    w_p, w_u, w_g = full["w_in"][:, :SHIFT], full["w_in"][:, SHIFT:SHIFT + SW], full["w_in"][:, SHIFT + SW:]
    zeros_l = jnp.zeros((LW, RW), F32)
    w_up_p = jnp.concatenate([full["rwkv_w_up"], zeros_l], axis=0)
    a_up_p = jnp.concatenate([zeros_l, full["rwkv_a_up"]], axis=0)
    g_up = full["rwkv_g_up"]
    conv_w = full["ffn_conv_w"]
    conv_wg, conv_wu = conv_w[:, :DFF], conv_w[:, DFF:]
    conv_bg, conv_bu = ffn_conv_b[:, :DFF], ffn_conv_b[:, DFF:]
    hm = jnp.kron(jnp.eye(NH, dtype=F32), jnp.ones((HD, HD), F32))

    ncol = 6 * D // 4
    b_ada_cols = lax.dynamic_slice_in_dim(b_ada, chip * ncol, ncol, 1)
    mod_part = _ada_fwd(c_all, w_ada[0], b_ada_cols)
    mod4 = _gather_two_level([], [mod_part], "gather_mod")[1][0][:, 0]
    mod4, late = lax.optimization_barrier((mod4, [W[n][0].astype(MXU_DTYPE) for n, _, _ in BIG_LATE]))
    late_moves = [(i, i, lambda ref, me, peer: ref, lambda ref, me, k: ref.at[_chip_of(me)]) for i in range(len(late))]
    late_start = _send_start("gather_ffn_start", CHIP_FLIPS, late,
                             [jax.ShapeDtypeStruct((4,) + z.shape, z.dtype) for z in late], late_moves)
    norm1_g = norm1_g + late_start["token"]
    mod = lax.dynamic_slice_in_dim(mod4, dev * Bl, Bl, 1).transpose(1, 0, 2).reshape(Bl, 1, 6 * D)
    SH1, SC1, GT1, SH2, SC2, GT2 = range(6)

    x2d = x.reshape(T, D)
    tgt = loss_target.reshape(T, D)

    (h1,) = rw("norm1", lambda xv, sc, sh, g: _norm_mod(xv, g, sc, sh), R=256, tiled=[(x2d, D, 0)],
               batch=[(mod, D, SC1), (mod, D, SH1)], full=[norm1_g], out_tiled=[(D, MXU_DTYPE)])
    p = _mm([h1], [w_p], F32, "proj_p")
    u = _mm([h1], [w_u], F32, "proj_u")
    gates = _mm([h1], [w_g], F32, "proj_g")

    prep_params = [rwkv_w0, w_up_p, rwkv_a0, a_up_p, g_up, rwkv_k_k, rwkv_k_a, hm]

    def prep_fwd(pv, ph, mu, *pp):
        ps = pv + (_shift_down(pv, ph, 1) - pv) * mu
        return _rwkv_prep(*_split_ps(ps), *pp)

    r_, w_, k_, v_, a_, b_, g_ = rw("rwkv_prep", prep_fwd, R=256, tiled=[(p, SHIFT, 0)], prev=[(p, SHIFT, 0)],
                                    full=[mu_shift] + prep_params, out_tiled=[(RW, F32)] * 7)
    y_wkv, ck = _wkv_fwd(r_, w_, k_, v_, a_, b_, Bl, S)
    r_k_row = rwkv_r_k.reshape(1, RW)
    post_params = [rwkv_ln_g, rwkv_ln_b, r_k_row, hm]
    (o_rwkv,) = rw("rwkv_post", _rwkv_post, R=256,
                   tiled=[(y_wkv, RW, 0), (r_, RW, 0), (k_, RW, 0), (v_, RW, 0), (g_, RW, 0)],
                   full=post_params, out_tiled=[(RW, MXU_DTYPE)])
    y_a = _mm([o_rwkv], [full["w_out_rwkv"]], F32, "out_rwkv")

    expand = jnp.kron(jnp.eye(SP, dtype=F32), jnp.ones((1, SGC), F32))
    s5_in = (s5_a_re[0], s5_a_im[0], s5_log_dt[0].reshape(NG, 1), s5_b_re[0].reshape(NG, SP * SGC),
             s5_b_im[0].reshape(NG, SP * SGC), expand)
    ab_re, ab_im, bb_re, bb_im = _s5_disc(*s5_in)
    eye8 = jnp.eye(8, dtype=F32)

    def blockdiag_in(bb):
        t = bb.reshape(NSG, 8, SP, SGC)
        return jnp.einsum("ab,sapc->sacbp", eye8, t).reshape(NSG, 128, 512)

    def blockdiag_out(cc):
        t = cc.reshape(NSG, 8, SGC, SP)
        return jnp.einsum("ab,sacp->sapbc", eye8, t).reshape(NSG, 512, 128)

    wb = jnp.concatenate([blockdiag_in(bb_re), blockdiag_in(bb_im)], axis=2).astype(MXU_DTYPE)
    wc = jnp.concatenate([blockdiag_out(s5_c_re[0]), -blockdiag_out(s5_c_im[0])], axis=1).astype(MXU_DTYPE)
    ab = jnp.stack([ab_re.reshape(NST), ab_im.reshape(NST)])
    y_ssm, s5_st, s5_x = _s5_fwd(u, wb, wc, ab, Bl, S)
    (s5o,) = rw("s5_post", _s5_post, R=256, tiled=[(y_ssm, SW, 0), (u, SW, 0)], full=[s5_d],
                out_tiled=[(SW, MXU_DTYPE)])
    z = _mm([s5o], [full["w_glu"]], F32, "glu")
    mix_tiled = [(gates, D, 0), (gates, D, 1), (y_a, D, 0), (z, D, 0), (z, D, 1)]
    (mixed_in,) = rw("mix", _mix, R=256, tiled=mix_tiled, out_tiled=[(D, MXU_DTYPE)])
    mixed = _mm([mixed_in], [full["w_out"]], F32, "out_proj")

    def norm2_fwd(xv, mx, gt, sc, sh, g):
        x1 = xv + gt * mx
        return x1, _norm_mod(x1, g, sc, sh)

    x1, h2 = rw("norm2", norm2_fwd, R=256, tiled=[(x2d, D, 0), (mixed, D, 0)],
                batch=[(mod, D, GT1), (mod, D, SC2), (mod, D, SH2)], full=[norm2_g],
                out_tiled=[(D, F32), (D, MXU_DTYPE)])
    late_own, late_got = _send_wait("gather_ffn_wait", CHIP_FLIPS, late_start, late_moves, h2)
    for (n, _, axis), own, got in zip(BIG_LATE, late_own, late_got):
        full[n] = _from_shards(lax.dynamic_update_slice(got, own[None], (chip, 0, 0)), axis)
    up = _mm([h2], [full["w_ffn_up"]], MXU_DTYPE, "ffn_up")
    conv_tiled = [(up, DFF, 0), (up, DFF, 1)]
    conv_full = [conv_wg, conv_wu, conv_bg, conv_bu]

    def act_fwd(*a):
        return _silu_gate(*_conv_act(*a))

    (act,) = rw("ffn_act", act_fwd, R=128, tiled=conv_tiled, prev=conv_tiled, full=conv_full,
                out_tiled=[(DFF, MXU_DTYPE)])
    ffn = _mm([act], [full["w_ffn_down"]], F32, "ffn_down")

    def head(x1v, fv, tv, gt, g):
        x2 = x1v + gt * fv
        y, vjp = jax.vjp(_rms, x2, g)
        e = y - tv
        dx2, dg = vjp(e * (1.0 / D))
        loss = jnp.sum(e * e, keepdims=True) * jnp.ones((1, LANES), F32)
        return dx2, dx2 * gt, jnp.sum(dx2 * fv, axis=0, keepdims=True), dg.reshape(1, D), loss

    dx2, d_ffn, d_gt2, g_norm_f, loss_acc = rw(
        "head", head, R=256, tiled=[(x1, D, 0), (ffn, D, 0), (tgt, D, 0)], batch=[(mod, D, GT2)],
        full=[norm_f_g.reshape(1, D)], out_tiled=[(D, F32), (D, MXU_DTYPE)], out_batch=[D],
        out_acc=[(1, D), (1, LANES)])
    loss = lax.psum(0.5 / D * loss_acc[0, 0], ("x", "y", "c"))

    d_act = _mm([d_ffn], [full["w_ffn_down"]], F32, "d_act", bt=True)
    g_w_ffn_down = _mm_tn(act, d_ffn, "g_ffn_down")

    def act_bwd(ug, uu, dact, hg, hu, wg, wu, bg, bu):
        (gate, taps_g), (upv, taps_u) = _conv3(ug, hg, wg, bg), _conv3(uu, hu, wu, bu)
        _, vjp_s = jax.vjp(_silu_gate, gate, upv)
        d_gate, d_upv = vjp_s(dact)
        def taps(dh, shifted):
            return [jnp.sum(dh * s, axis=0, keepdims=True) for s in shifted] + [jnp.sum(dh, axis=0, keepdims=True)]
        return (d_gate, d_upv, *taps(d_gate, taps_g), *taps(d_upv, taps_u))

    dh_g, dh_u, *tapg = rw(
        "ffn_act_bwd", act_bwd, R=128, tiled=conv_tiled + [(d_act, DFF, 0)], prev=conv_tiled, full=conv_full,
        out_tiled=[(DFF, MXU_DTYPE), (DFF, MXU_DTYPE)], out_acc=[(1, DFF)] * 8)
    g_cw_g, g_cb_g = jnp.concatenate(tapg[0:3], axis=0), tapg[3]
    g_cw_u, g_cb_u = jnp.concatenate(tapg[4:7], axis=0), tapg[7]

    def conv_t(dg, du_, ng, nu, wg, wu):
        dg, du_, ng, nu = (z.astype(F32) for z in (dg, du_, ng, nu))

        def ct(d, n, w):
            return w[2:3] * d + w[1:2] * _shift_up(d, n, 1) + w[0:1] * _shift_up(d, n, 2)
        return jnp.concatenate([ct(dg, ng, wg), ct(du_, nu, wu)], axis=1)

    (d_up,) = rw("conv_bwd", conv_t, R=128, tiled=[(dh_g, DFF, 0), (dh_u, DFF, 0)],
                 nxt=[(dh_g, DFF, 0), (dh_u, DFF, 0)], full=[conv_wg, conv_wu], out_tiled=[(2 * DFF, MXU_DTYPE)])
    d_h2 = _mm([d_up], [full["w_ffn_up"]], F32, "d_h2", bt=True)
    g_w_ffn_up = _mm_tn(h2, d_up, "g_ffn_up")

    sds = jax.ShapeDtypeStruct
    reduce_src = lambda r: (lambda ref, me, peer: ref.at[_chip_of(peer), _half(r, peer[2])])

    def reduced_halves(tag, started, moves, after):
        gsh_own, got = _send_wait("rs_%s_wait" % tag, ALL_FLIPS, started, moves, after)
        halves = []
        for i, (g, gt) in enumerate(zip(gsh_own, got)):
            h = g.shape[1] // 2
            own = lax.dynamic_slice(g, (chip, ic * h, 0), (1, h, g.shape[2]))
            halves.append(_sum_slots([own, gt], F32, "rs_%s_sum%d" % (tag, i)))
        return halves

    def share_start(tag, halves):
        moves = [(i, i, lambda ref, me, peer: ref, lambda ref, me, k: ref) for i in range(len(halves))]
        return _send_start("share_%s_start" % tag, PAIR_FLIPS, halves, [sds(g.shape, F32) for g in halves], moves), moves

    def share_finish(tag, started, moves, after, group, grads):
        mine_h, got_h = _send_wait("share_%s_wait" % tag, PAIR_FLIPS, started, moves, after)
        for (n, _, _), mh, gh in zip(group, mine_h, got_h):
            grads[n] = jnp.concatenate([jnp.where(ic == 0, mh, gh), jnp.where(ic == 0, gh, mh)], axis=0)[None]

    gsh_late = [_to_shards(g, ax).astype(MXU_DTYPE) for g, (_, _, ax) in zip((g_w_ffn_up, g_w_ffn_down), BIG_LATE)]
    rsl_moves = [(i, i, reduce_src(g.shape[1]), lambda ref, me, k: ref.at[k]) for i, g in enumerate(gsh_late)]
    rsl = _send_start("rs_ffn_start", ALL_FLIPS, gsh_late,
                      [sds((len(ALL_FLIPS), g.shape[1] // 2, g.shape[2]), MXU_DTYPE) for g in gsh_late], rsl_moves)
    norm2_g = norm2_g + rsl["token"]

    def norm2_bwd(x1v, dh2, dx2v, mx, gt, sc, sh, g):
        _, vjp = jax.vjp(_norm_mod, x1v, g, sc, sh)
        dxn, dg, dsc, dsh = vjp(dh2)
        dx1 = dx2v + dxn
        return dx1, dx1 * gt, jnp.sum(dx1 * mx, axis=0, keepdims=True), dsc, dsh, dg

    dx1, d_mixed, d_gt1, d_sc2, d_sh2, g_norm2 = rw(
        "norm2_bwd", norm2_bwd, R=256, tiled=[(x1, D, 0), (d_h2, D, 0), (dx2, D, 0), (mixed, D, 0)],
        batch=[(mod, D, GT1), (mod, D, SC2), (mod, D, SH2)], full=[norm2_g],
        out_tiled=[(D, F32), (D, MXU_DTYPE)], out_batch=[D, D, D], out_acc=[(1, D)])

    d_mixed_in = _mm([d_mixed], [full["w_out"]], F32, "d_mixed_in", bt=True)
    g_w_out = _mm_tn(mixed_in, d_mixed, "g_w_out")

    def mix_bwd(ga, gb, ya, za, zb, dm):
        _, vjp = jax.vjp(_mix, ga, gb, ya, za, zb)
        dga, dgb, dya, dza, dzb = vjp(dm)
        return jnp.concatenate([dga, dgb], axis=1), dya, jnp.concatenate([dza, dzb], axis=1)

    d_gates, d_ya, d_z = rw("mix_bwd", mix_bwd, R=256, tiled=mix_tiled + [(d_mixed_in, D, 0)],
                            out_tiled=[(2 * D, MXU_DTYPE), (D, MXU_DTYPE), (2 * D, MXU_DTYPE)])
    d_o_rwkv = _mm([d_ya], [full["w_out_rwkv"]], F32, "d_o_rwkv", bt=True)
    g_w_out_rwkv = _mm_tn(o_rwkv, d_ya, "g_out_rwkv")
    d_s5o = _mm([d_z], [full["w_glu"]], F32, "d_s5o", bt=True)
    g_w_glu = _mm_tn(s5o, d_z, "g_glu")

    def s5_post_bwd(ys, uv, ds, dd):
        _, vjp = jax.vjp(_s5_post, ys, uv, dd)
        return vjp(ds)

    d_yssm, d_u_direct, g_s5_d = rw("s5_post_bwd", s5_post_bwd, R=256,
                                    tiled=[(y_ssm, SW, 0), (u, SW, 0), (d_s5o, SW, 0)], full=[s5_d],
                                    out_tiled=[(SW, F32), (SW, F32)], out_acc=[(1, SW)])
    d_u_ssm, d_wb, d_wc, d_ab = _s5_bwd(u, d_yssm, wb, wc, ab, s5_st, s5_x, Bl, S)

    def diag_in(dw):
        t = dw.reshape(NSG, 8, SGC, 8, SP)
        return jnp.einsum("ab,sacbp->sapc", eye8, t).reshape(NG, SP * SGC)

    def diag_out(dw):
        t = dw.reshape(NSG, 8, SP, 8, SGC)
        return jnp.einsum("ab,sapbc->sacp", eye8, t).reshape(NG, SGC, SP)

    g_s5_c_re = diag_out(d_wc[:, :512])
    g_s5_c_im = -diag_out(d_wc[:, 512:])
    disc_cts = (d_ab[0].reshape(NG, SP), d_ab[1].reshape(NG, SP), diag_in(d_wb[:, :, :512]), diag_in(d_wb[:, :, 512:]))
    g_a_re, g_a_im, g_log_dt, g_b_re, g_b_im = _s5_disc_bwd(*s5_in, disc_cts)

    def post_bwd(yv, rv, kv, vv, gv, do, *pp):
        _, vjp = jax.vjp(lambda *a: _rwkv_post(*a, pp[3]), yv, rv, kv, vv, gv, *pp[:3])
        return vjp(do)

    dy_wkv, dr_b, dk_b, dv_b, dg_, g_ln_g, g_ln_b, g_r_k = rw(
        "rwkv_post_bwd", post_bwd, R=256,
        tiled=[(y_wkv, RW, 0), (r_, RW, 0), (k_, RW, 0), (v_, RW, 0), (g_, RW, 0), (d_o_rwkv, RW, 0)],
        full=post_params, out_tiled=[(RW, F32)] * 5, out_acc=[(1, RW)] * 3)
    dr3, dw3, dk3, dv3, da3, db3 = _wkv_bwd(r_, w_, k_, v_, a_, b_, dy_wkv, ck, Bl, S)

    shl, shl_moves = share_start("ffn", reduced_halves("ffn", rsl, rsl_moves, dr3))
    mu_shift = mu_shift + shl["token"]

    def prep_bwd(pv, dr1, dr2, dwv, dk1, dk2, dv1, dv2, dav, dbv, dgv, ph, mu, *pp):
        prev = _shift_down(pv, ph, 1)
        ps = pv + (prev - pv) * mu
        _, vjp = jax.vjp(lambda *q: _rwkv_prep(*q, pp[7]), *_split_ps(ps), *pp[:7])
        grads = vjp((dr1 + dr2, dwv, dk1 + dk2, dv1 + dv2, dav, dbv, dgv))
        dps = jnp.concatenate(grads[:5], axis=1)
        return (dps,) + tuple(grads[5:]) + (jnp.sum(dps * (prev - pv), axis=0, keepdims=True),)

    prep_outs = rw(
        "rwkv_prep_bwd", prep_bwd, R=256,
        tiled=[(p, SHIFT, 0), (dr3, RW, 0), (dr_b, RW, 0), (dw3, RW, 0), (dk3, RW, 0), (dk_b, RW, 0),
               (dv3, RW, 0), (dv_b, RW, 0), (da3, RW, 0), (db3, RW, 0), (dg_, RW, 0)],
        prev=[(p, SHIFT, 0)], full=[mu_shift] + prep_params,
        out_tiled=[(SHIFT, F32)],
        out_acc=[(1, RW), (LW + LA, RW), (1, RW), (LW + LA, RW), (LG, RW), (1, RW), (1, RW), (1, SHIFT)])
    d_ps, g_w0, g_w_up_p, g_a0, g_a_up_p, g_g_up, g_k_k, g_k_a, g_mu = prep_outs

    small = {"mu_shift": g_mu, "rwkv_w0": g_w0, "rwkv_a0": g_a0, "rwkv_k_k": g_k_k,
             "rwkv_k_a": g_k_a, "rwkv_r_k": g_r_k, "rwkv_ln_g": g_ln_g, "rwkv_ln_b": g_ln_b, "s5_a_re": g_a_re,
             "s5_a_im": g_a_im, "s5_log_dt": g_log_dt, "s5_b_re": g_b_re, "s5_b_im": g_b_im, "s5_c_re": g_s5_c_re,
             "s5_c_im": g_s5_c_im, "s5_d": g_s5_d, "norm2_g": g_norm2,
             "ffn_conv_b": jnp.concatenate([g_cb_g, g_cb_u], axis=1), "norm_f_g": g_norm_f}
    small_names = list(small)
    g_conv_w = jnp.concatenate([g_cw_g, g_cw_u], axis=1)
    shard_small = {"rwkv_w_up": g_w_up_p[:LW], "rwkv_a_up": g_a_up_p[LW:], "rwkv_g_up": g_g_up, "ffn_conv_w": g_conv_w}
    parts = [small[n] for n in small_names] + [_to_shards(shard_small[n], ax) for n, _, ax in BIG_SMALL]
    spack = _pack_rows(parts, F32, SUBLANES)
    sm_moves = [(0, 0, lambda ref, me, peer: ref, lambda ref, me, k: ref.at[2 * _chip_of(me) + me[2]])]
    sm = _send_start("gsmall_start", ALL_FLIPS, [spack], [sds((8,) + spack.shape, F32)], sm_moves)
    mu_shift = mu_shift + sm["token"]

    def shift_bwd(dps, nx, mu):
        return dps * (1.0 - mu) + _shift_up(dps * mu, nx * mu, 1)

    (d_p,) = rw("shift_bwd", shift_bwd, R=256, tiled=[(d_ps, SHIFT, 0)], nxt=[(d_ps, SHIFT, 0)], full=[mu_shift],
                out_tiled=[(SHIFT, MXU_DTYPE)])
    (d_u,) = rw("d_u", lambda a1, a2: a1 + a2, R=256, tiled=[(d_u_direct, SW, 0), (d_u_ssm, SW, 0)],
                out_tiled=[(SW, MXU_DTYPE)])
    g_w_in = jnp.concatenate([_mm_tn(h1, d_p, "g_w_p"), _mm_tn(h1, d_u, "g_w_u"), _mm_tn(h1, d_gates, "g_w_g")], axis=1)
    big_g = {"w_in": g_w_in, "w_out_rwkv": g_w_out_rwkv, "w_glu": g_w_glu, "w_out": g_w_out}
    gsh_now = [_to_shards(big_g[n], ax).astype(MXU_DTYPE) for n, _, ax in now]
    rsn_moves = [(i, i, reduce_src(g.shape[1]), lambda ref, me, k: ref.at[k]) for i, g in enumerate(gsh_now)]
    rsn = _send_start("rs_mix_start", ALL_FLIPS, gsh_now,
                      [sds((len(ALL_FLIPS), g.shape[1] // 2, g.shape[2]), MXU_DTYPE) for g in gsh_now], rsn_moves)
    norm1_g = norm1_g + rsn["token"]
    d_h1 = _mm([d_p, d_u, d_gates], [w_p, w_u, w_g], F32, "d_h1", bt=True)

    def norm1_bwd(xv, dh1, dx1v, sc, sh, g):
        _, vjp = jax.vjp(_norm_mod, xv, g, sc, sh)
        dxn, dg, dsc, dsh = vjp(dh1)
        return dx1v + dxn, dsc, dsh, dg

    grad_x, d_sc1, d_sh1, g_norm1 = rw(
        "norm1_bwd", norm1_bwd, R=256, tiled=[(x2d, D, 0), (d_h1, D, 0), (dx1, D, 0)],
        batch=[(mod, D, SC1), (mod, D, SH1)], full=[norm1_g], out_tiled=[(D, F32)], out_batch=[D, D], out_acc=[(1, D)])

    shn, shn_moves = share_start("mix", reduced_halves("mix", rsn, rsn_moves, grad_x))

    dmod = jnp.concatenate([d_sh1, d_sc1, d_gt1, d_sh2, d_sc2, d_gt2], axis=2).reshape(Bl, 6 * D)
    last_all = _gather_two_level([], [dmod, g_norm1], "gather_dmod")[1]
    dmod_all = last_all[0].reshape(8 * Bl, 6 * D)
    dmod_cols = lax.dynamic_slice_in_dim(dmod_all, chip * ncol, ncol, 1)
    g_w_ada, g_b_ada = _ada_bwd(c_all, dmod_cols, dmod_all)

    grads = {"norm1_g": _sum_slots(last_all[1].reshape(8, 1, D), F32, "sum_norm1")}
    sm_own, sm_got = _send_wait("gsmall_wait", ALL_FLIPS, sm, sm_moves, g_b_ada)
    s_all = lax.dynamic_update_slice(sm_got[0], sm_own[0][None], (dev, 0, 0))
    s_sum = _sum_slots(s_all, F32, "sum_gsmall").reshape(-1)
    off = 0
    for n in small_names:
        grads[n] = s_sum[off:off + W[n].size].reshape(W[n].shape)
        off += W[n].size
    for n, shape, axis in BIG_SMALL:
        ss = _shard_shape(shape, axis)
        k4 = 4 * math.prod(ss)
        sh4 = s_sum[off:off + k4].reshape(4, math.prod(ss))
        grads[n] = lax.dynamic_index_in_dim(sh4, chip, 0, keepdims=False).reshape((1,) + ss)
        off += k4

    share_finish("ffn", shl, shl_moves, s_sum, BIG_LATE, grads)
    share_finish("mix", shn, shn_moves, grads[BIG_LATE[0][0]], now, grads)
    grads["w_ada"] = g_w_ada[None]
    grads["b_ada"] = g_b_ada

    delta, new_m, new_v = {}, {}, {}
    to2 = lambda z: z.reshape(-1, z.shape[-1])
    for n in ["w_ada"] + [b[0] for b in BIG]:
        d_, m_, v2_ = _adamw(to2(W[n]), to2(grads[n]), to2(M[n]), to2(V[n]), "adamw_" + n)
        delta[n], new_m[n], new_v[n] = (z.reshape(W[n].shape) for z in (d_, m_, v2_))
    rest = [n for n in names if n not in delta]
    packs = [_pack_rows([src[n] for n in rest], F32, SUBLANES) for src in (W, grads, M, V)]
    d_, m_, v2_ = _adamw(*packs, "adamw_small")
    shapes = [W[n].shape for n in rest]
    for dst, z in ((delta, d_), (new_m, m_), (new_v, v2_)):
        for n, val in zip(rest, _unpack(z.reshape(-1), shapes)):
            dst[n] = val

    return (loss, grad_x.reshape(Bl, S, D), *[grads[n] for n in names], *[delta[n] for n in names],
            *[new_m[n] for n in names], *[new_v[n] for n in names])
```

```python
import functools
import math

import jax
import jax.numpy as jnp
from jax import lax
from jax.experimental import pallas as pl
from jax.experimental.pallas import tpu as pltpu

F32 = jnp.float32
BF16 = jnp.bfloat16
MXU_DTYPE = jnp.bfloat16
MESH_IDS = pl.DeviceIdType.MESH
HIGHEST = lax.Precision.HIGHEST

D = 1024
RW, NH, HD = 512, 8, 64
LW, LA, LG = 64, 64, 128
SW, SGC, NG, SP = 512, 16, 32, 64
NSG = 4
SHIFT = 3 * RW + LW + LA + LG
DFF = 2816
RMS_EPS, GN_EPS, L2_EPS = 1e-6, 64e-5, 1e-12
LR, B1, B2, ADAM_EPS, WD, STEP = 0.001, 0.9, 0.999, 1e-8, 0.01, 10
DECAY_SCALE = math.exp(-0.5)
GELU_C = math.sqrt(2.0 / math.pi)

VMEM_LIMIT = 52 * 1024 * 1024
SUBLANES, LANES = 8, 128
HALO = 16


def _pick(n, cap):
    if n <= cap:
        return n
    best = None
    for t in range(LANES, cap + 1, LANES):
        if n % t == 0:
            best = t
    assert best is not None, (n, cap)
    return best


def _params(sem=None, vmem=VMEM_LIMIT):
    return pltpu.CompilerParams(dimension_semantics=sem, vmem_limit_bytes=vmem)


def _chip_of(p):
    return 2 * p[0] + p[1]


def _me():
    return (lax.axis_index("x"), lax.axis_index("y"), lax.axis_index("c"))


def _half(rows, core):
    h = rows // 2
    return pl.ds(pl.multiple_of(core * h, 16 if h % 16 == 0 else SUBLANES), h)


_HBM =pl.BlockSpec(memory_space=pltpu.HBM)
_SEM = pl.BlockSpec(memory_space=pltpu.SEMAPHORE)
_DATAFLOW = pltpu.SideEffectType.DATAFLOW_SIDE_EFFECTING


def _split_copies(flips, moves, src_refs, land_refs, send_sems, recv_sems):
    me = _me()
    nf = len(flips)
    out = []
    for m, (si, li, src_sel, dst_sel) in enumerate(moves):
        for k, f in enumerate(flips):
            peer = tuple(1 - v if b else v for v, b in zip(me, f))
            out.append(pltpu.make_async_remote_copy(
                src_ref=src_sel(src_refs[si], me, peer), dst_ref=dst_sel(land_refs[li], me, k),
                send_sem=send_sems.at[m * nf + k], recv_sem=recv_sems.at[m * nf + k],
                device_id=peer, device_id_type=MESH_IDS))
    return out


def _send_start(name, flips, srcs, land_shapes, moves):
    ns, nl = len(srcs), len(land_shapes)
    n = len(moves) * len(flips)

    def body(*refs):
        for cp in _split_copies(flips, moves, refs[:ns], refs[ns:ns + nl], refs[ns + nl], refs[ns + nl + 1]):
            cp.start()
        refs[-1][...] = jnp.zeros(refs[-1].shape, F32)

    hbm = lambda z: pltpu.with_memory_space_constraint(z, pltpu.HBM)
    lands = [lax.empty(s.shape, s.dtype) for s in land_shapes]
    res = pl.pallas_call(
        body, name=name,
        out_shape=(pltpu.SemaphoreType.DMA((n,)), pltpu.SemaphoreType.DMA((n,)),
                   *[pltpu.HBM(z.shape, z.dtype) for z in srcs], *[pltpu.HBM(s.shape, s.dtype) for s in land_shapes],
                   jax.ShapeDtypeStruct((SUBLANES, LANES), F32)),
        in_specs=[_HBM] * (ns + nl),
        out_specs=(_SEM, _SEM, *[_HBM] * (ns + nl), pl.BlockSpec(memory_space=pltpu.VMEM)),
        input_output_aliases={i: 2 + i for i in range(ns + nl)},
        compiler_params=pltpu.CompilerParams(has_side_effects=_DATAFLOW),
    )(*[hbm(z) for z in srcs], *[hbm(z) for z in lands])
    return {"sems": res[:2], "srcs": list(res[2:2 + ns]), "lands": list(res[2 + ns:2 + ns + nl]), "token": res[-1][0, 0]}


def _send_wait(name, flips, started, moves, after):
    srcs, lands = started["srcs"], started["lands"]
    ns, nl = len(srcs), len(lands)

    def body(*refs):
        for cp in _split_copies(flips, moves, refs[:ns], refs[ns:ns + nl], refs[ns + nl], refs[ns + nl + 1]):
            cp.wait_send()
            cp.wait_recv()

    res = pl.pallas_call(
        body, name=name, out_shape=[pltpu.HBM(z.shape, z.dtype) for z in srcs + lands],
        in_specs=[_HBM] * (ns + nl) + [_SEM, _SEM, pl.BlockSpec(memory_space=pl.ANY)],
        out_specs=[_HBM] * (ns + nl), input_output_aliases={i: i for i in range(ns + nl)},
        compiler_params=pltpu.CompilerParams(has_side_effects=_DATAFLOW),
    )(*srcs, *lands, *started["sems"], after)
    return list(res[:ns]), list(res[ns:])


CHIP_FLIPS = ((1, 0, 0), (0, 1, 0), (1, 1, 0))
PAIR_FLIPS = ((0, 0, 1),)
ALL_FLIPS = CHIP_FLIPS + ((1, 0, 1), (0, 1, 1), (1, 1, 1)) + PAIR_FLIPS


def _gather_two_level(chip_arrs, dev_arrs, name):
    arrs = list(chip_arrs) + list(dev_arrs)
    n, nchip = len(arrs), len(chip_arrs)
    NS = 7

    def body(*refs):
        srcs, outs = refs[:n], refs[n:2 * n]
        send_sems, recv_sems, loc_sems = refs[2 * n:]
        x, y, c = _me()
        sib = (x, y, 1 - c)
        chips = [(1 - x, y), (x, 1 - y), (1 - x, 1 - y)]
        mine = 2 * x + y
        ids = [2 * cx + cy for cx, cy in chips]

        def part(i, slot, core):
            if i < nchip:
                return outs[i].at[slot, _half(arrs[i].shape[0], core)]
            return outs[i].at[slot, core]

        def rcopy(i, k, src, dst, to):
            return pltpu.make_async_remote_copy(src_ref=src, dst_ref=dst, send_sem=send_sems.at[i * NS + k],
                                                recv_sem=recv_sems.at[i * NS + k], device_id=to, device_id_type=MESH_IDS)

        started, locs = [], []
        for i in range(n):
            own = srcs[i].at[_half(arrs[i].shape[0], c)] if i < nchip else srcs[i]
            loc = pltpu.make_async_copy(srcs[i], outs[i].at[mine] if i < nchip else outs[i].at[mine, c], loc_sems.at[i])
            loc.start()
            locs.append(loc)
            for f, chip in enumerate(chips):
                cp = rcopy(i, f, own, part(i, mine, c), (*chip, c))
                cp.start()
                started.append(cp)
            if i >= nchip:
                cp = rcopy(i, 6, own, part(i, mine, c), sib)
                cp.start()
                started.append(cp)
        for i in range(n):
            for f in range(3):
                land = part(i, ids[f], c)
                rcopy(i, f, land, land, sib).wait_recv()
                fw = rcopy(i, 3 + f, land, land, sib)
                fw.start()
                started.append(fw)
        for i in range(n):
            for f in range(3):
                land = part(i, ids[f], 1 - c)
                rcopy(i, 3 + f, land, land, sib).wait_recv()
            if i >= nchip:
                land = part(i, mine, 1 - c)
                rcopy(i, 6, land, land, sib).wait_recv()
        for cp in started:
            cp.wait_send()
        for loc in locs:
            loc.wait()

    outs = [jax.ShapeDtypeStruct((4,) + a.shape, a.dtype) for a in chip_arrs]
    outs += [jax.ShapeDtypeStruct((4, 2) + a.shape, a.dtype) for a in dev_arrs]
    res = pl.pallas_call(
        body, name=name, out_shape=outs,
        in_specs=[pl.BlockSpec(memory_space=pl.ANY)] * n, out_specs=[pl.BlockSpec(memory_space=pl.ANY)] * n,
        scratch_shapes=[pltpu.SemaphoreType.DMA((n * NS,)), pltpu.SemaphoreType.DMA((n * NS,)),
                        pltpu.SemaphoreType.DMA((n,))],
    )(*arrs)
    return res[:nchip], res[nchip:]


def _mm(As, Bs, out_dtype, name, tm=512, cap=1408, bt=False):
    n = len(As)
    M, N = As[0].shape[0], Bs[0].shape[0 if bt else 1]
    tm = min(tm, M)
    tn = _pick(N, cap)
    dims = (((1,), (1,)), ((), ())) if bt else (((1,), (0,)), ((), ()))

    def body(*refs):
        o = refs[2 * n]
        acc = None
        for a, b in zip(refs[:n], refs[n:2 * n]):
            d = lax.dot_general(a[...].astype(MXU_DTYPE), b[...].astype(MXU_DTYPE), dims, preferred_element_type=F32)
            acc = d if acc is None else acc + d
        o[...] = acc.astype(o.dtype)

    in_specs = [pl.BlockSpec((tm, a.shape[1]), lambda i, j: (i, 0)) for a in As]
    if bt:
        in_specs += [pl.BlockSpec((tn, b.shape[1]), lambda i, j: (j, 0)) for b in Bs]
    else:
        in_specs += [pl.BlockSpec((b.shape[0], tn), lambda i, j: (0, j)) for b in Bs]
    return pl.pallas_call(
        body, name=name, grid=(M // tm, N // tn), in_specs=in_specs,
        out_specs=pl.BlockSpec((tm, tn), lambda i, j: (i, j)),
        out_shape=jax.ShapeDtypeStruct((M, N), out_dtype),
        compiler_params=_params(("parallel", "parallel")),
    )(*As, *Bs)


def _mm_tn(A, G, name, tt=1024, cap=1408):
    T, Ka = A.shape
    N = G.shape[1]
    tt = min(tt, T)
    tk = _pick(Ka, cap)
    tn = _pick(N, cap)

    def body(a, g, o):
        @pl.when(pl.program_id(2) == 0)
        def _():
            o[...] = jnp.zeros(o.shape, F32)
        o[...] += lax.dot_general(a[...].astype(MXU_DTYPE), g[...].astype(MXU_DTYPE),
                                  (((0,), (0,)), ((), ())), preferred_element_type=F32)

    return pl.pallas_call(
        body, name=name, grid=(Ka // tk, N // tn, T // tt),
        in_specs=[pl.BlockSpec((tt, tk), lambda i, j, t: (t, i)), pl.BlockSpec((tt, tn), lambda i, j, t: (t, j))],
        out_specs=pl.BlockSpec((tk, tn), lambda i, j, t: (i, j)),
        out_shape=jax.ShapeDtypeStruct((Ka, N), F32),
        compiler_params=_params(("parallel", "parallel", "arbitrary")),
    )(A, G)


def _rowwise(name, fn, *, Bl, S, R, tiled=(), prev=(), nxt=(), batch=(), full=(),
             out_tiled=(), out_batch=(), out_acc=()):
    R = min(R, S)
    nS = S // R
    T = Bl * S
    hb = R // HALO
    n_in = len(tiled) + len(prev) + len(nxt) + len(batch) + len(full)

    in_specs, args = [], []
    for a, wd, cb in tiled:
        in_specs.append(pl.BlockSpec((R, wd), lambda b, i, cb=cb: (b * nS + i, cb)))
        args.append(a)
    for a, wd, cb in prev:
        in_specs.append(pl.BlockSpec((HALO, wd), lambda b, i, cb=cb: (jnp.maximum((b * nS + i) * hb - 1, 0), cb)))
        args.append(a)
    for a, wd, cb in nxt:
        in_specs.append(pl.BlockSpec((HALO, wd), lambda b, i, cb=cb: (jnp.minimum((b * nS + i + 1) * hb, T // HALO - 1), cb)))
        args.append(a)
    for a, wd, cb in batch:
        in_specs.append(pl.BlockSpec((1, 1, wd), lambda b, i, cb=cb: (b, 0, cb)))
        args.append(a)
    for a in full:
        in_specs.append(pl.BlockSpec(a.shape, lambda b, i, nd=a.ndim: (0,) * nd))
        args.append(a)

    out_specs, out_shape = [], []
    for C, dt in out_tiled:
        out_specs.append(pl.BlockSpec((R, C), lambda b, i: (b * nS + i, 0)))
        out_shape.append(jax.ShapeDtypeStruct((T, C), dt))
    for C in out_batch:
        out_specs.append(pl.BlockSpec((1, 1, C), lambda b, i: (b, 0, 0)))
        out_shape.append(jax.ShapeDtypeStruct((Bl, 1, C), F32))
    for shp in out_acc:
        out_specs.append(pl.BlockSpec(shp, lambda b, i, nd=len(shp): (0,) * nd))
        out_shape.append(jax.ShapeDtypeStruct(shp, F32))

    nt, npv, nnx, nbt = len(tiled), len(prev), len(nxt), len(batch)

    def body(*refs):
        b, i = pl.program_id(0), pl.program_id(1)
        ins, outs = refs[:n_in], refs[n_in:]
        vals = [r[...] for r in ins[:nt]]
        vals += [jnp.where(i > 0, r[...], jnp.zeros(r.shape, r.dtype)) for r in ins[nt:nt + npv]]
        vals += [jnp.where(i < nS - 1, r[...], jnp.zeros(r.shape, r.dtype)) for r in ins[nt + npv:nt + npv + nnx]]
        vals += [r[0] for r in ins[nt + npv + nnx:nt + npv + nnx + nbt]]
        vals += [r[...] for r in ins[nt + npv + nnx + nbt:]]
        res = fn(*vals)
        if not isinstance(res, (tuple, list)):
            res = (res,)
        k = 0
        for _ in out_tiled:
            outs[k][...] = res[k].astype(outs[k].dtype)
            k += 1
        for _ in out_batch:
            o = outs[k]

            @pl.when(i == 0)
            def _(o=o):
                o[...] = jnp.zeros(o.shape, F32)
            o[0] += res[k]
            k += 1
        for _ in out_acc:
            o = outs[k]

            @pl.when((i == 0) & (b == 0))
            def _(o=o):
                o[...] = jnp.zeros(o.shape, F32)
            o[...] += res[k]
            k += 1

    out = pl.pallas_call(
        body, name=name, grid=(Bl, nS), in_specs=in_specs, out_specs=out_specs, out_shape=out_shape,
        compiler_params=_params(("arbitrary", "arbitrary")),
    )(*args)
    return out


def _shift_down(x, halo, k):
    rolled = pltpu.roll(x, k, 0)
    row = lax.broadcasted_iota(jnp.int32, (SUBLANES, x.shape[1]), 0)
    head = rolled[0:SUBLANES]
    for j in range(k):
        head = jnp.where(row == j, halo[HALO - k + j:HALO - k + j + 1, :], head)
    return jnp.concatenate([head, rolled[SUBLANES:]], axis=0)


def _shift_up(x, halo, k):
    n = x.shape[0]
    rolled = pltpu.roll(x, n - k, 0)
    row = lax.broadcasted_iota(jnp.int32, (SUBLANES, x.shape[1]), 0)
    tail = rolled[n - SUBLANES:]
    for j in range(k):
        tail = jnp.where(row == SUBLANES - k + j, halo[j:j + 1, :], tail)
    return jnp.concatenate([rolled[:n - SUBLANES], tail], axis=0)


def _dotm(a, b):
    return jnp.dot(a.astype(MXU_DTYPE), b.astype(MXU_DTYPE), preferred_element_type=F32)


def _split_bf16(x):
    hi = x.astype(BF16)
    return hi, (x - hi.astype(F32)).astype(BF16)


def _headsum_2pass(x, hm):
    hi, lo = _split_bf16(x)
    hb = hm.astype(BF16)
    return jnp.dot(hi, hb, preferred_element_type=F32) + jnp.dot(lo, hb, preferred_element_type=F32)


@jax.custom_vjp
def _headsum(x, hm):
    return _headsum_2pass(x, hm)


_headsum.defvjp(lambda x, hm: (_headsum_2pass(x, hm), hm),
                lambda hm, g: (_headsum_2pass(g, hm), jnp.zeros_like(hm)))


def _sigmoid(x):
    return 1.0 / (1.0 + jnp.exp(-x))


def _rms(x, g):
    return x * lax.rsqrt(jnp.mean(x * x, axis=-1, keepdims=True) + RMS_EPS) * g


def _norm_mod(x, g, sc, sh):
    return _rms(x, g) * (1.0 + sc) + sh


def _split_ps(ps):
    return (ps[:, 0:RW], ps[:, RW:2 * RW], ps[:, 2 * RW:3 * RW], ps[:, 3 * RW:3 * RW + LW + LA],
            ps[:, 3 * RW + LW + LA:SHIFT])


def _rwkv_prep(r, k, v, wa, gd, w0, w_up_p, a0, a_up_p, g_up, k_k, k_a, hm):
    w_raw = w0 + _dotm(jnp.tanh(wa), w_up_p)
    decay = jnp.exp(-DECAY_SCALE * _sigmoid(w_raw))
    a = _sigmoid(a0 + _dotm(wa, a_up_p))
    g = _dotm(_sigmoid(gd), g_up)
    kk = k * k_k
    kk = kk * lax.rsqrt(_headsum(kk * kk, hm) + L2_EPS)
    k2 = k * (1.0 + (a - 1.0) * k_a)
    return r, decay, k2, v, -kk, kk * a, g


def _rwkv_post(y, r, k2, v, g, ln_g, ln_b, r_k, hm):
    mean = _headsum(y, hm) * (1.0 / HD)
    yc = y - mean
    var = _headsum(yc * yc, hm) * (1.0 / HD)
    yn = yc * lax.rsqrt(var + GN_EPS) * ln_g + ln_b
    bonus = _headsum(r * k2 * r_k, hm) * v
    return (yn + bonus) * g


def _gelu(x):
    return 0.5 * x * (1.0 + jnp.tanh(GELU_C * (x + 0.044715 * (x * x * x))))


def _s5_post(yssm, u, d):
    return _gelu(yssm + d * u)


def _mix(ga, gb, ya, za, zb):
    return _sigmoid(ga) * ya + _sigmoid(gb) * (za * _sigmoid(zb))


def _conv_act(up_g, up_u, hg, hu, w_g, w_u, b_g, b_u):
    gate, upv = _conv3(up_g, hg, w_g, b_g)[0], _conv3(up_u, hu, w_u, b_u)[0]
    return gate, upv


def _conv3(x, h, w, b):
    x, h = x.astype(F32), h.astype(F32)
    s2, s1 = _shift_down(x, h, 2), _shift_down(x, h, 1)
    return b + w[0:1] * s2 + w[1:2] * s1 + w[2:3] * x, (s2, s1, x)


def _silu_gate(gate, upv):
    return gate * _sigmoid(gate) * upv


WKV_L = 64
_NT, _NN, _TN = ((1,), (1,)), ((1,), (0,)), ((0,), (0,))


def _dotw(x, y, dims):
    return lax.dot_general(x.astype(MXU_DTYPE), y.astype(MXU_DTYPE), (dims, ((), ())), preferred_element_type=F32)


def _dot3(x, y, dims):
    (xh, xl), (yh, yl) = _split_bf16(x), _split_bf16(y)
    d = lambda p, q: lax.dot_general(p, q, (dims, ((), ())), preferred_element_type=F32)
    return d(xh, yh) + d(xh, yl) + d(xl, yh)


@jax.custom_vjp
def _gram3(x, y):
    return _dot3(x, y, _NT)


_gram3.defvjp(lambda x, y: (_dot3(x, y, _NT), (x, y)),
              lambda res, g: (_dot3(g, res[1], _NN), _dot3(g, res[0], _TN)))


def _tri_solve_fwd(ns, xs):
    each = lambda f, *ls: tuple(f(*zs) for zs in zip(*ls))
    size = ns[0].shape[0]
    eye = (lax.broadcasted_iota(jnp.int32, (size, size), 0) == lax.broadcasted_iota(jnp.int32, (size, size), 1)).astype(F32)
    ts = each(lambda n: n + eye, ns)
    qs = ns
    for _ in range(WKV_L.bit_length() - 2):
        qs = each(lambda q: _dotw(q, q, _NN), qs)
        ts = each(lambda t, q: t + _dotw(t, q, _NN), ts, qs)
    us = each(lambda t, x: _dotw(t, x, _NN), ts, xs)
    return us, (ts, us)


def _tri_solve_bwd(res, dus):
    ts, us = res
    each = lambda f, *ls: tuple(f(*zs) for zs in zip(*ls))
    dxs = each(lambda t, du: _dotw(t, du, _TN), ts, dus)
    return each(lambda dx, u: _dotw(dx, u, _NT), dxs, us), dxs


@jax.custom_vjp
def _tri_solve(ns, xs):
    return _tri_solve_fwd(ns, xs)[0]


_tri_solve.defvjp(_tri_solve_fwd, _tri_solve_bwd)


def _wkv_chunk(s0, r, w, k, v, a, b):
    y, s1 = _wkv_chunks((s0,), (r,), (w,), (k,), (v,), (a,), (b,))
    return y[0], s1[0]


def _wkv_chunks(s0, r, w, k, v, a, b):
    each = lambda f, *ls: tuple(f(*xs) for xs in zip(*ls))
    L = r[0].shape[0]
    n2 = 2 * L
    lane_head = lax.broadcasted_iota(jnp.int32, (2, 1, 2 * HD), 2) // HD
    head_mask = (lane_head == lax.broadcasted_iota(jnp.int32, (2, 1, 2 * HD), 0)).astype(F32)
    ri = lax.broadcasted_iota(jnp.int32, (n2, n2), 0)
    ci = lax.broadcasted_iota(jnp.int32, (n2, n2), 1)
    same = (ri // L) == (ci // L)
    strict = same & ((ci % L) < (ri % L))
    incl = same & ((ci % L) <= (ri % L))
    si = lax.broadcasted_iota(jnp.int32, (2 * HD, 2 * HD), 0) // HD
    sj = lax.broadcasted_iota(jnp.int32, (2 * HD, 2 * HD), 1) // HD
    tri = (lax.broadcasted_iota(jnp.int32, (L, L), 0) >= lax.broadcasted_iota(jnp.int32, (L, L), 1)).astype(F32)

    stack = lambda z: (z[None] * head_mask).reshape(n2, 2 * HD)
    dup = lambda z: jnp.broadcast_to(z[None], (2, L, 2 * HD)).reshape(n2, 2 * HD)
    gram = _gram3
    nt, nn, tn = (lambda x, y, d=d: _dotw(x, y, d) for d in (_NT, _NN, _TN))
    add = lambda x, y: x + y

    lw = each(jnp.log, w)
    cum = each(lambda z: jnp.dot(tri, z, preferred_element_type=F32, precision=HIGHEST), lw)
    tot = each(lambda z: jnp.sum(z, axis=0, keepdims=True), lw)
    a2 = each(lambda av, cv, lv: stack(av * jnp.exp(cv - lv)), a, cum, lw)
    r2 = each(lambda rv, cv: stack(rv * jnp.exp(cv)), r, cum)
    v2 = each(stack, v)
    b2 = each(lambda bv, cv: dup(bv * jnp.exp(-cv)), b, cum)
    k2 = each(lambda kv, cv: dup(kv * jnp.exp(-cv)), k, cum)
    n_ab = each(lambda x, y: jnp.where(strict, gram(x, y), 0.0), a2, b2)
    n_ak = each(lambda x, y: jnp.where(strict, gram(x, y), 0.0), a2, k2)
    m_rb = each(lambda x, y: jnp.where(incl, gram(x, y), 0.0), r2, b2)
    m_rk = each(lambda x, y: jnp.where(incl, gram(x, y), 0.0), r2, k2)
    u = _tri_solve(n_ab, each(add, each(nt, a2, s0), each(nn, n_ak, v2)))
    y2 = each(lambda x, y, z: x + y + z, each(nt, r2, s0), each(nn, m_rb, u), each(nn, m_rk, v2))
    y = each(lambda z: jnp.sum(z.reshape(2, L, 2 * HD), axis=0), y2)
    b3 = each(lambda bv, tv, cv: dup(bv * jnp.exp(tv - cv)), b, tot, cum)
    k3 = each(lambda kv, tv, cv: dup(kv * jnp.exp(tv - cv)), k, tot, cum)
    upd = each(add, each(tn, u, b3), each(tn, v2, k3))
    s1 = each(lambda sv, tv, uv: sv * jnp.exp(tv) + jnp.where(si == sj, uv, 0.0), s0, tot, upd)
    return y, s1


NPAIR = NH // 2


def _wkv_nb(Bl):
    return 2 if Bl % 2 == 0 else 1


def _wkv_fwd(r, w, k, v, a, b, Bl, S):
    L = WKV_L
    nC = S // L
    nb = _wkv_nb(Bl)
    chains = [(bi, p, slice(p * 2 * HD, (p + 1) * 2 * HD)) for bi in range(nb) for p in range(NPAIR)]

    def body(r_ref, w_ref, k_ref, v_ref, a_ref, b_ref, y_ref, ck_ref, s_ref):
        @pl.when(pl.program_id(1) == 0)
        def _():
            s_ref[...] = jnp.zeros(s_ref.shape, F32)
        s0 = tuple(s_ref[bi, p] for bi, p, _ in chains)
        ops = [tuple(z[bi, :, cs] for bi, _, cs in chains) for z in (r_ref, w_ref, k_ref, v_ref, a_ref, b_ref)]
        y, s1 = _wkv_chunks(s0, *ops)
        for i, (bi, p, cs) in enumerate(chains):
            ck_ref[bi, 0, p] = s0[i]
            y_ref[bi, :, cs] = y[i]
            s_ref[bi, p] = s1[i]

    to3 = lambda z: z.reshape(Bl, S, RW)
    row_spec = pl.BlockSpec((nb, L, RW), lambda g, c: (g, c, 0))
    y, ck = pl.pallas_call(
        body, name="wkv_fwd", grid=(Bl // nb, nC), in_specs=[row_spec] * 6,
        out_specs=[row_spec, pl.BlockSpec((nb, 1, NPAIR, 2 * HD, 2 * HD), lambda g, c: (g, c, 0, 0, 0))],
        out_shape=[jax.ShapeDtypeStruct((Bl, S, RW), F32), jax.ShapeDtypeStruct((Bl, nC, NPAIR, 2 * HD, 2 * HD), F32)],
        scratch_shapes=[pltpu.VMEM((nb, NPAIR, 2 * HD, 2 * HD), F32)],
        compiler_params=_params(("arbitrary", "arbitrary")),
    )(*(to3(z) for z in (r, w, k, v, a, b)))
    return y.reshape(Bl * S, RW), ck


def _wkv_bwd(r, w, k, v, a, b, dy, ck, Bl, S):
    L = WKV_L
    nC = S // L
    nb = _wkv_nb(Bl)
    chains = [(bi, p, slice(p * 2 * HD, (p + 1) * 2 * HD)) for bi in range(nb) for p in range(NPAIR)]

    def body(r_ref, w_ref, k_ref, v_ref, a_ref, b_ref, dy_ref, ck_ref,
             dr_ref, dw_ref, dk_ref, dv_ref, da_ref, db_ref, ds_ref):
        @pl.when(pl.program_id(1) == 0)
        def _():
            ds_ref[...] = jnp.zeros(ds_ref.shape, F32)
        s0 = tuple(ck_ref[bi, 0, p] for bi, p, _ in chains)
        ops = [tuple(z[bi, :, cs] for bi, _, cs in chains) for z in (r_ref, w_ref, k_ref, v_ref, a_ref, b_ref)]
        cts = (tuple(dy_ref[bi, :, cs] for bi, _, cs in chains), tuple(ds_ref[bi, p] for bi, p, _ in chains))
        ds0, *grads = jax.vjp(_wkv_chunks, s0, *ops)[1](cts)
        for i, (bi, p, cs) in enumerate(chains):
            ds_ref[bi, p] = ds0[i]
            for o, g in zip((dr_ref, dw_ref, dk_ref, dv_ref, da_ref, db_ref), grads):
                o[bi, :, cs] = g[i]

    to3 = lambda z: z.reshape(Bl, S, RW)
    row_spec = pl.BlockSpec((nb, L, RW), lambda g, c: (g, nC - 1 - c, 0))
    rows = jax.ShapeDtypeStruct((Bl, S, RW), F32)
    outs = pl.pallas_call(
        body, name="wkv_bwd", grid=(Bl // nb, nC),
        in_specs=[row_spec] * 7 + [pl.BlockSpec((nb, 1, NPAIR, 2 * HD, 2 * HD), lambda g, c: (g, nC - 1 - c, 0, 0, 0))],
        out_specs=[row_spec] * 6, out_shape=[rows] * 6,
        scratch_shapes=[pltpu.VMEM((nb, NPAIR, 2 * HD, 2 * HD), F32)],
        compiler_params=_params(("arbitrary", "arbitrary")),
    )(*(to3(z) for z in (r, w, k, v, a, b, dy)), ck)
    return [o.reshape(Bl * S, RW) for o in outs]


NST = NG * SP


def _cmul(ar, ai, br, bi):
    return ar * br - ai * bi, ar * bi + ai * br


def _s5_tiles(are, aim, reverse):
    if reverse:
        aim = -aim
    row = lax.broadcasted_iota(jnp.int32, (SUBLANES, NST), 0)
    pw = [(are, aim)]
    for _ in range(SUBLANES - 1):
        pw.append(_cmul(pw[-1][0], pw[-1][1], are, aim))
    bc = lambda z: jnp.broadcast_to(z, (SUBLANES, NST))
    ms = []
    for kk in (1, 2, 4):
        cond = (row < SUBLANES - kk) if reverse else (row >= kk)
        ms.append((jnp.where(cond, bc(pw[kk - 1][0]), 0.0), jnp.where(cond, bc(pw[kk - 1][1]), 0.0)))
    pr = jnp.zeros((SUBLANES, NST), F32)
    pi = jnp.zeros((SUBLANES, NST), F32)
    for i in range(SUBLANES):
        n = SUBLANES - i if reverse else i + 1
        pr = jnp.where(row == i, bc(pw[n - 1][0]), pr)
        pi = jnp.where(row == i, bc(pw[n - 1][1]), pi)
    return ms, (pr, pi)


def _s5_block(re, im, ms, pc, cre, cim, sg, reverse):
    ln = slice(sg * 512, (sg + 1) * 512)
    for (mr, mi), kk in zip(ms, (1, 2, 4)):
        sh = SUBLANES - kk if reverse else kk
        sre, sim = pltpu.roll(re, sh, 0), pltpu.roll(im, sh, 0)
        tr, ti = _cmul(mr[:, ln], mi[:, ln], sre, sim)
        re, im = re + tr, im + ti
    tr, ti = _cmul(pc[0][:, ln], pc[1][:, ln], cre[:, ln], cim[:, ln])
    return re + tr, im + ti


def _s5_scan(X_ref, n_rows, ms, pc, cre, cim, reverse, visit=None, acc0=None):
    nblk = n_rows // SUBLANES

    def it(i, carry):
        cre, cim, acc = carry
        j = nblk - 1 - i if reverse else i
        rows = pl.ds(pl.multiple_of(j * SUBLANES, SUBLANES), SUBLANES)
        edge = 0 if reverse else SUBLANES - 1
        blocks, ncre, ncim = [], [], []
        for sg in range(NSG):
            lr = slice(sg * 1024, sg * 1024 + 512)
            li = slice(sg * 1024 + 512, (sg + 1) * 1024)
            re, im = _s5_block(X_ref[rows, lr], X_ref[rows, li], ms, pc, cre, cim, sg, reverse)
            X_ref[rows, lr] = re
            X_ref[rows, li] = im
            blocks.append((re, im))
            ncre.append(re[edge:edge + 1])
            ncim.append(im[edge:edge + 1])
        if visit is not None:
            acc = visit(j, blocks, acc)
        return jnp.concatenate(ncre, axis=1), jnp.concatenate(ncim, axis=1), acc

    return lax.fori_loop(0, nblk, it, (cre, cim, acc0 if acc0 is not None else 0))


def _s5_fwd(u, wb, wc, ab, d, Bl, S, R=256):
    R = min(R, S)
    nC = S // R

    def body(u_ref, wb_ref, wc_ref, ab_ref, d_ref, y_ref, st_ref, X_ref, o_ref, car_ref):
        @pl.when(pl.program_id(1) == 0)
        def _():
            car_ref[...] = jnp.zeros(car_ref.shape, F32)
        st_ref[0, 0] = car_ref[...]
        ms, pc = _s5_tiles(ab_ref[0:1], ab_ref[1:2], False)
        for sg in range(NSG):
            X_ref[:, sg * 1024:(sg + 1) * 1024] = _dotm(u_ref[:, sg * 128:(sg + 1) * 128], wb_ref[sg])
        cre, cim, _ = _s5_scan(X_ref, R, ms, pc, car_ref[0:1], car_ref[1:2], False)
        car_ref[0:1] = cre
        car_ref[1:2] = cim
        for sg in range(NSG):
            y_ref[:, sg * 128:(sg + 1) * 128] = _dotm(X_ref[:, sg * 1024:(sg + 1) * 1024], wc_ref[sg])
        o_ref[...] = _s5_post(y_ref[...], u_ref[...], d_ref[...]).astype(o_ref.dtype)

    rows = pl.BlockSpec((R, SW), lambda b, c: (b * nC + c, 0))
    return pl.pallas_call(
        body, name="s5_fwd", grid=(Bl, nC),
        in_specs=[rows, pl.BlockSpec(wb.shape, lambda b, c: (0, 0, 0)), pl.BlockSpec(wc.shape, lambda b, c: (0, 0, 0)),
                  pl.BlockSpec(ab.shape, lambda b, c: (0, 0)), pl.BlockSpec(d.shape, lambda b, c: (0, 0))],
        out_specs=[rows, pl.BlockSpec((1, 1, 2, NST), lambda b, c: (b, c, 0, 0)),
                   pl.BlockSpec((R, 2 * NST), lambda b, c: (b * nC + c, 0)), rows],
        out_shape=[jax.ShapeDtypeStruct((Bl * S, SW), F32), jax.ShapeDtypeStruct((Bl, nC, 2, NST), F32),
                   jax.ShapeDtypeStruct((Bl * S, 2 * NST), F32), jax.ShapeDtypeStruct((Bl * S, SW), MXU_DTYPE)],
        scratch_shapes=[pltpu.VMEM((2, NST), F32)],
        compiler_params=_params(("arbitrary", "arbitrary")),
    )(u, wb, wc, ab, d)


def _s5_bwd(u, y, do, d, wb, wc, ab, st, xs, Bl, S, R=256):
    R = min(R, S)
    nC = S // R

    def body(u_ref, y_ref, do_ref, d_ref, wb_ref, wc_ref, ab_ref, st_ref, X_ref,
             du_ref, dwb_ref, dwc_ref, dab_ref, dd_ref, G_ref, car_ref):
        first = (pl.program_id(0) == 0) & (pl.program_id(1) == 0)

        @pl.when(first)
        def _():
            for o in (dwb_ref, dwc_ref, dab_ref, dd_ref):
                o[...] = jnp.zeros(o.shape, F32)

        @pl.when(pl.program_id(1) == 0)
        def _():
            car_ref[...] = jnp.zeros(car_ref.shape, F32)

        are, aim = ab_ref[0:1], ab_ref[1:2]
        dy, du_direct, dd = jax.vjp(_s5_post, y_ref[...], u_ref[...], d_ref[...])[1](do_ref[...])
        dd_ref[...] += dd
        dyv = dy.astype(MXU_DTYPE)
        for sg in range(NSG):
            G_ref[:, sg * 1024:(sg + 1) * 1024] = lax.dot_general(
                dyv[:, sg * 128:(sg + 1) * 128], wc_ref[sg].astype(MXU_DTYPE), (((1,), (1,)), ((), ())),
                preferred_element_type=F32)
        rms_, rpc = _s5_tiles(are, aim, True)
        row = lax.broadcasted_iota(jnp.int32, (SUBLANES, 512), 0)

        def visit(j, blocks, acc):
            before = pl.multiple_of(jnp.maximum(j - 1, 0) * SUBLANES, SUBLANES)
            prow = X_ref[pl.ds(before, SUBLANES), :][SUBLANES - 1:SUBLANES]
            rows = pl.ds(pl.multiple_of(j * SUBLANES, SUBLANES), SUBLANES)
            are_acc, aim_acc = [], []
            for sg in range(NSG):
                lr = slice(sg * 1024, sg * 1024 + 512)
                li = slice(sg * 1024 + 512, (sg + 1) * 1024)
                ln = slice(sg * 512, (sg + 1) * 512)
                pre = jnp.where(j > 0, prow[:, lr], st_ref[0, 0, 0:1, ln])
                pim = jnp.where(j > 0, prow[:, li], st_ref[0, 0, 1:2, ln])
                xre = jnp.where(row == 0, pre, pltpu.roll(X_ref[rows, lr], 1, 0))
                xim = jnp.where(row == 0, pim, pltpu.roll(X_ref[rows, li], 1, 0))
                dre, dim = blocks[sg]
                are_acc.append(dre * xre + dim * xim)
                aim_acc.append(dim * xre - dre * xim)
            return acc[0] + jnp.concatenate(are_acc, axis=1), acc[1] + jnp.concatenate(aim_acc, axis=1)

        zero = jnp.zeros((SUBLANES, NST), F32)
        cre, cim, acc = _s5_scan(G_ref, R, rms_, rpc, car_ref[0:1], car_ref[1:2], True, visit, (zero, zero))
        car_ref[0:1] = cre
        car_ref[1:2] = cim
        dab_ref[0:1] += jnp.sum(acc[0], axis=0, keepdims=True)
        dab_ref[1:2] += jnp.sum(acc[1], axis=0, keepdims=True)
        uv = u_ref[...].astype(MXU_DTYPE)
        for sg in range(NSG):
            cs = slice(sg * 1024, (sg + 1) * 1024)
            us = slice(sg * 128, (sg + 1) * 128)
            gx = G_ref[:, cs].astype(MXU_DTYPE)
            dwb_ref[sg] += lax.dot_general(uv[:, us], gx, (((0,), (0,)), ((), ())), preferred_element_type=F32)
            dwc_ref[sg] += lax.dot_general(X_ref[:, cs].astype(MXU_DTYPE), dyv[:, us], (((0,), (0,)), ((), ())),
                                           preferred_element_type=F32)
            du_ssm = lax.dot_general(gx, wb_ref[sg].astype(MXU_DTYPE), (((1,), (1,)), ((), ())),
                                     preferred_element_type=F32)
            du_ref[:, us] = (du_ssm + du_direct[:, us]).astype(du_ref.dtype)

    rmap = lambda b, c: (b * nC + nC - 1 - c, 0)
    rows = pl.BlockSpec((R, SW), rmap)
    return pl.pallas_call(
        body, name="s5_bwd", grid=(Bl, nC),
        in_specs=[rows, rows, rows, pl.BlockSpec(d.shape, lambda b, c: (0, 0)),
                  pl.BlockSpec(wb.shape, lambda b, c: (0, 0, 0)), pl.BlockSpec(wc.shape, lambda b, c: (0, 0, 0)),
                  pl.BlockSpec(ab.shape, lambda b, c: (0, 0)),
                  pl.BlockSpec((1, 1, 2, NST), lambda b, c: (b, nC - 1 - c, 0, 0)),
                  pl.BlockSpec((R, 2 * NST), rmap)],
        out_specs=[rows, pl.BlockSpec(wb.shape, lambda b, c: (0, 0, 0)),
                   pl.BlockSpec(wc.shape, lambda b, c: (0, 0, 0)), pl.BlockSpec((2, NST), lambda b, c: (0, 0)),
                   pl.BlockSpec(d.shape, lambda b, c: (0, 0))],
        out_shape=[jax.ShapeDtypeStruct((Bl * S, SW), MXU_DTYPE), jax.ShapeDtypeStruct(wb.shape, F32),
                   jax.ShapeDtypeStruct(wc.shape, F32), jax.ShapeDtypeStruct((2, NST), F32),
                   jax.ShapeDtypeStruct(d.shape, F32)],
        scratch_shapes=[pltpu.VMEM((R, 2 * NST), F32), pltpu.VMEM((2, NST), F32)],
        compiler_params=_params(("arbitrary", "arbitrary")),
    )(u, y, do, d, wb, wc, ab, st, xs)


def _s5_disc_math(a_re, a_im, log_dt, b_re, b_im, expand):
    dt = jnp.exp(log_dt)
    z_re, z_im = a_re * dt, a_im * dt
    mag = jnp.exp(z_re)
    ab_re, ab_im = mag * jnp.cos(z_im), mag * jnp.sin(z_im)
    den = a_re * a_re + a_im * a_im
    q_re = ((ab_re - 1.0) * a_re + ab_im * a_im) / den
    q_im = (ab_im * a_re - (ab_re - 1.0) * a_im) / den
    qe_re = jnp.dot(q_re, expand, preferred_element_type=F32, precision=HIGHEST)
    qe_im = jnp.dot(q_im, expand, preferred_element_type=F32, precision=HIGHEST)
    return ab_re, ab_im, qe_re * b_re - qe_im * b_im, qe_re * b_im + qe_im * b_re


def _whole(shape):
    return pl.BlockSpec(shape, lambda nd=len(shape): (0,) * nd)


def _s5_disc(a_re, a_im, log_dt, b_re, b_im, expand):
    def body(a, b, c, d, e, f, o0, o1, o2, o3):
        res = _s5_disc_math(a[...], b[...], c[...], d[...], e[...], f[...])
        for o, v in zip((o0, o1, o2, o3), res):
            o[...] = v
    ins = (a_re, a_im, log_dt, b_re, b_im, expand)
    outs = [jax.ShapeDtypeStruct(a_re.shape, F32)] * 2 + [jax.ShapeDtypeStruct(b_re.shape, F32)] * 2
    return pl.pallas_call(body, name="s5_disc", in_specs=[_whole(x.shape) for x in ins],
                          out_specs=[_whole(o.shape) for o in outs], out_shape=outs)(*ins)


def _s5_disc_bwd(a_re, a_im, log_dt, b_re, b_im, expand, cts):
    def body(a, b, c, d, e, f, g0, g1, g2, g3, o0, o1, o2, o3, o4):
        fn = lambda *p: _s5_disc_math(*p, f[...])
        _, vjp = jax.vjp(fn, a[...], b[...], c[...], d[...], e[...])
        for o, v in zip((o0, o1, o2, o3, o4), vjp((g0[...], g1[...], g2[...], g3[...]))):
            o[...] = v
    ins = (a_re, a_im, log_dt, b_re, b_im, expand) + tuple(cts)
    outs = [jax.ShapeDtypeStruct(x.shape, F32) for x in (a_re, a_im, log_dt, b_re, b_im)]
    return pl.pallas_call(body, name="s5_disc_bwd", in_specs=[_whole(x.shape) for x in ins],
                          out_specs=[_whole(o.shape) for o in outs], out_shape=outs)(*ins)


def _ada_fwd(c_all, w_shard, b_shard):
    def body(c_ref, w_ref, b_ref, o_ref):
        cv = c_ref[...]
        o_ref[...] = _dotm(cv * _sigmoid(cv), w_ref[...]) + b_ref[...]
    n = w_shard.shape[1]
    return pl.pallas_call(
        body, name="ada_fwd", in_specs=[_whole(c_all.shape), _whole(w_shard.shape), _whole(b_shard.shape)],
        out_specs=_whole((c_all.shape[0], n)), out_shape=jax.ShapeDtypeStruct((c_all.shape[0], n), F32),
        compiler_params=_params(),
    )(c_all, w_shard, b_shard)


def _ada_bwd(c_all, dmod_cols, dmod_all):
    def body(c_ref, dc_ref, da_ref, gw_ref, gb_ref):
        cv = c_ref[...]
        gw_ref[...] = lax.dot_general((cv * _sigmoid(cv)).astype(MXU_DTYPE), dc_ref[...].astype(MXU_DTYPE),
                                      (((0,), (0,)), ((), ())), preferred_element_type=F32)
        gb_ref[...] = jnp.sum(da_ref[...], axis=0, keepdims=True)
    n = dmod_cols.shape[1]
    return pl.pallas_call(
        body, name="ada_bwd", in_specs=[_whole(c_all.shape), _whole(dmod_cols.shape), _whole(dmod_all.shape)],
        out_specs=[_whole((D, n)), _whole((1, dmod_all.shape[1]))],
        out_shape=[jax.ShapeDtypeStruct((D, n), F32), jax.ShapeDtypeStruct((1, dmod_all.shape[1]), F32)],
        compiler_params=_params(),
    )(c_all, dmod_cols, dmod_all)


def _rows_block(n_rows, cap=512):
    if n_rows <= cap:
        return n_rows
    for t in range(cap - cap % SUBLANES, 0, -SUBLANES):
        if n_rows % t == 0:
            return t
    return n_rows


def _adamw(w, g, m, v, name):
    rows, cols = w.shape
    tr = _rows_block(rows, max(SUBLANES, (1 << 19) // max(cols, 1) // SUBLANES * SUBLANES))

    def body(w_ref, g_ref, m_ref, v_ref, d_ref, nm_ref, nv_ref):
        gv = g_ref[...]
        nm = B1 * m_ref[...] + (1.0 - B1) * gv
        nv = B2 * v_ref[...] + (1.0 - B2) * (gv * gv)
        m_hat = nm / (1.0 - B1 ** STEP)
        v_hat = nv / (1.0 - B2 ** STEP)
        d_ref[...] = -LR * (m_hat / (jnp.sqrt(v_hat) + ADAM_EPS) + WD * w_ref[...])
        nm_ref[...] = nm
        nv_ref[...] = nv

    spec = pl.BlockSpec((tr, cols), lambda i: (i, 0))
    sd = jax.ShapeDtypeStruct((rows, cols), F32)
    return pl.pallas_call(body, name=name, grid=(rows // tr,), in_specs=[spec] * 4, out_specs=[spec] * 3,
                          out_shape=[sd] * 3, compiler_params=_params(("parallel",)))(w, g, m, v)


def _sum_slots(x, out_dtype, name):
    xs = x if isinstance(x, (list, tuple)) else [x]
    _, rows, cols = xs[0].shape
    tr = _rows_block(rows)

    def body(*refs):
        acc = None
        for x_ref in refs[:-1]:
            for j in range(x_ref.shape[0]):
                term = x_ref[j].astype(F32)
                acc = term if acc is None else acc + term
        refs[-1][...] = acc.astype(refs[-1].dtype)

    return pl.pallas_call(
        body, name=name, grid=(rows // tr,),
        in_specs=[pl.BlockSpec((z.shape[0], tr, cols), lambda i: (0, i, 0)) for z in xs],
        out_specs=pl.BlockSpec((tr, cols), lambda i: (i, 0)), out_shape=jax.ShapeDtypeStruct((rows, cols), out_dtype),
        compiler_params=_params(("parallel",)))(*xs)


PACK_COLS = 1024


def _pack_rows(parts, dtype, row_mult):
    flat = jnp.concatenate([p.reshape(-1).astype(dtype) for p in parts])
    per = PACK_COLS * row_mult
    n = -(-flat.shape[0] // per) * per
    flat = jnp.pad(flat, (0, n - flat.shape[0]))
    return flat.reshape(n // PACK_COLS, PACK_COLS)


def _unpack(flat, shapes):
    out, off = [], 0
    for s in shapes:
        n = math.prod(s)
        out.append(flat[off:off + n].reshape(s))
        off += n
    return out


BIG = (("w_in", (D, SHIFT + SW + 2 * D), 1), ("w_out_rwkv", (RW, D), 1), ("w_glu", (SW, 2 * D), 1),
       ("w_out", (D, D), 0), ("w_ffn_up", (D, 2 * DFF), 1), ("w_ffn_down", (DFF, D), 0))
BIG_SMALL = (("rwkv_w_up", (LW, RW), 1), ("rwkv_a_up", (LA, RW), 1), ("rwkv_g_up", (LG, RW), 1),
             ("ffn_conv_w", (3, 2 * DFF), 1))
BIG_LATE = BIG[4:]


def _shard_shape(shape, axis):
    return (shape[0] // 4, shape[1]) if axis == 0 else (shape[0], shape[1] // 4)


def _to_shards(g, axis):
    r, C = g.shape
    return g.reshape(4, r // 4, C) if axis == 0 else g.reshape(r, 4, C // 4).transpose(1, 0, 2)


def _from_shards(x, axis):
    _, r, C = x.shape
    return x.reshape(4 * r, C) if axis == 0 else x.transpose(1, 0, 2).reshape(r, 4 * C)


def kernel(x, c, w_ada, b_ada, norm1_g, w_in, mu_shift, rwkv_w0, rwkv_w_up, rwkv_a0, rwkv_a_up, rwkv_g_up, rwkv_k_k, rwkv_k_a, rwkv_r_k, rwkv_ln_g, rwkv_ln_b, w_out_rwkv, s5_a_re, s5_a_im, s5_log_dt, s5_b_re, s5_b_im, s5_c_re, s5_c_im, s5_d, w_glu, w_out, norm2_g, w_ffn_up, ffn_conv_w, ffn_conv_b, w_ffn_down, norm_f_g, loss_target, m_w_ada, m_b_ada, m_norm1_g, m_w_in, m_mu_shift, m_rwkv_w0, m_rwkv_w_up, m_rwkv_a0, m_rwkv_a_up, m_rwkv_g_up, m_rwkv_k_k, m_rwkv_k_a, m_rwkv_r_k, m_rwkv_ln_g, m_rwkv_ln_b, m_w_out_rwkv, m_s5_a_re, m_s5_a_im, m_s5_log_dt, m_s5_b_re, m_s5_b_im, m_s5_c_re, m_s5_c_im, m_s5_d, m_w_glu, m_w_out, m_norm2_g, m_w_ffn_up, m_ffn_conv_w, m_ffn_conv_b, m_w_ffn_down, m_norm_f_g, v_w_ada, v_b_ada, v_norm1_g, v_w_in, v_mu_shift, v_rwkv_w0, v_rwkv_w_up, v_rwkv_a0, v_rwkv_a_up, v_rwkv_g_up, v_rwkv_k_k, v_rwkv_k_a, v_rwkv_r_k, v_rwkv_ln_g, v_rwkv_ln_b, v_w_out_rwkv, v_s5_a_re, v_s5_a_im, v_s5_log_dt, v_s5_b_re, v_s5_b_im, v_s5_c_re, v_s5_c_im, v_s5_d, v_w_glu, v_w_out, v_norm2_g, v_w_ffn_up, v_ffn_conv_w, v_ffn_conv_b, v_w_ffn_down, v_norm_f_g):
    names = ["w_ada", "b_ada", "norm1_g", "w_in", "mu_shift", "rwkv_w0", "rwkv_w_up", "rwkv_a0", "rwkv_a_up",
             "rwkv_g_up", "rwkv_k_k", "rwkv_k_a", "rwkv_r_k", "rwkv_ln_g", "rwkv_ln_b", "w_out_rwkv", "s5_a_re",
             "s5_a_im", "s5_log_dt", "s5_b_re", "s5_b_im", "s5_c_re", "s5_c_im", "s5_d", "w_glu", "w_out", "norm2_g",
             "w_ffn_up", "ffn_conv_w", "ffn_conv_b", "w_ffn_down", "norm_f_g"]
    env = dict(locals())
    W = {n: env[n] for n in names}
    M = {n: env["m_" + n] for n in names}
    V = {n: env["v_" + n] for n in names}

    Bl, S, _ = x.shape
    T = Bl * S
    ix, iy, ic = lax.axis_index("x"), lax.axis_index("y"), lax.axis_index("c")
    chip = 2 * ix + iy
    dev = 2 * chip + ic
    rw = functools.partial(_rowwise, Bl=Bl, S=S)

    now = [b for b in BIG if b not in BIG_LATE]
    chip_arrs = [W[n][0].astype(MXU_DTYPE) for n, _, _ in now] + [W[n][0] for n, _, _ in BIG_SMALL[:3]]
    got_chip, got_dev = _gather_two_level(chip_arrs, [W["ffn_conv_w"][0], c], "gather_w")
    full = {n: _from_shards(g, axis) for (n, _, axis), g in zip(tuple(now) + BIG_SMALL[:3], got_chip)}
    full["ffn_conv_w"] = _from_shards(got_dev[0][:, 0], 1)
    c_all = got_dev[1].reshape(8 * Bl, D)
    w_p, w_u, w_g = full["w_in"][:, :SHIFT], full["w_in"][:, SHIFT:SHIFT + SW], full["w_in"][:, SHIFT + SW:]
    zeros_l = jnp.zeros((LW, RW), F32)
    w_up_p = jnp.concatenate([full["rwkv_w_up"], zeros_l], axis=0)
    a_up_p = jnp.concatenate([zeros_l, full["rwkv_a_up"]], axis=0)
    g_up = full["rwkv_g_up"]
    conv_w = full["ffn_conv_w"]
    conv_wg, conv_wu = conv_w[:, :DFF], conv_w[:, DFF:]
    conv_bg, conv_bu = ffn_conv_b[:, :DFF], ffn_conv_b[:, DFF:]
    hm = jnp.kron(jnp.eye(NH, dtype=F32), jnp.ones((HD, HD), F32))

    ncol = 6 * D // 4
    b_ada_cols = lax.dynamic_slice_in_dim(b_ada, chip * ncol, ncol, 1)
    mod_part = _ada_fwd(c_all, w_ada[0], b_ada_cols)
    mod4 = _gather_two_level([], [mod_part], "gather_mod")[1][0][:, 0]
    mod4, late = lax.optimization_barrier((mod4, [W[n][0].astype(MXU_DTYPE) for n, _, _ in BIG_LATE]))
    late_moves = [(i, i, lambda ref, me, peer: ref, lambda ref, me, k: ref.at[_chip_of(me)]) for i in range(len(late))]
    late_start = _send_start("gather_ffn_start", CHIP_FLIPS, late,
                             [jax.ShapeDtypeStruct((4,) + z.shape, z.dtype) for z in late], late_moves)
    norm1_g = norm1_g + late_start["token"]
    mod = lax.dynamic_slice_in_dim(mod4, dev * Bl, Bl, 1).transpose(1, 0, 2).reshape(Bl, 1, 6 * D)
    SH1, SC1, GT1, SH2, SC2, GT2 = range(6)

    x2d = x.reshape(T, D)
    tgt = loss_target.reshape(T, D)

    (h1,) = rw("norm1", lambda xv, sc, sh, g: _norm_mod(xv, g, sc, sh), R=256, tiled=[(x2d, D, 0)],
               batch=[(mod, D, SC1), (mod, D, SH1)], full=[norm1_g], out_tiled=[(D, MXU_DTYPE)])
    p = _mm([h1], [w_p], F32, "proj_p")
    u = _mm([h1], [w_u], F32, "proj_u")
    gates = _mm([h1], [w_g], F32, "proj_g")

    prep_params = [rwkv_w0, w_up_p, rwkv_a0, a_up_p, g_up, rwkv_k_k, rwkv_k_a, hm]

    def prep_fwd(pv, ph, mu, *pp):
        ps = pv + (_shift_down(pv, ph, 1) - pv) * mu
        return _rwkv_prep(*_split_ps(ps), *pp)

    r_, w_, k_, v_, a_, b_, g_ = rw("rwkv_prep", prep_fwd, R=256, tiled=[(p, SHIFT, 0)], prev=[(p, SHIFT, 0)],
                                    full=[mu_shift] + prep_params, out_tiled=[(RW, F32)] * 7)
    y_wkv, ck = _wkv_fwd(r_, w_, k_, v_, a_, b_, Bl, S)
    r_k_row = rwkv_r_k.reshape(1, RW)
    post_params = [rwkv_ln_g, rwkv_ln_b, r_k_row, hm]
    (o_rwkv,) = rw("rwkv_post", _rwkv_post, R=256,
                   tiled=[(y_wkv, RW, 0), (r_, RW, 0), (k_, RW, 0), (v_, RW, 0), (g_, RW, 0)],
                   full=post_params, out_tiled=[(RW, MXU_DTYPE)])
    y_a = _mm([o_rwkv], [full["w_out_rwkv"]], F32, "out_rwkv")

    expand = jnp.kron(jnp.eye(SP, dtype=F32), jnp.ones((1, SGC), F32))
    s5_in = (s5_a_re[0], s5_a_im[0], s5_log_dt[0].reshape(NG, 1), s5_b_re[0].reshape(NG, SP * SGC),
             s5_b_im[0].reshape(NG, SP * SGC), expand)
    ab_re, ab_im, bb_re, bb_im = _s5_disc(*s5_in)
    eye8 = jnp.eye(8, dtype=F32)

    def blockdiag_in(bb):
        t = bb.reshape(NSG, 8, SP, SGC)
        return jnp.einsum("ab,sapc->sacbp", eye8, t).reshape(NSG, 128, 512)

    def blockdiag_out(cc):
        t = cc.reshape(NSG, 8, SGC, SP)
        return jnp.einsum("ab,sacp->sapbc", eye8, t).reshape(NSG, 512, 128)

    wb = jnp.concatenate([blockdiag_in(bb_re), blockdiag_in(bb_im)], axis=2).astype(MXU_DTYPE)
    wc = jnp.concatenate([blockdiag_out(s5_c_re[0]), -blockdiag_out(s5_c_im[0])], axis=1).astype(MXU_DTYPE)
    ab = jnp.stack([ab_re.reshape(NST), ab_im.reshape(NST)])
    y_ssm, s5_st, s5_x, s5o = _s5_fwd(u, wb, wc, ab, s5_d, Bl, S)
    z = _mm([s5o], [full["w_glu"]], F32, "glu")
    mix_tiled = [(gates, D, 0), (gates, D, 1), (y_a, D, 0), (z, D, 0), (z, D, 1)]
    (mixed_in,) = rw("mix", _mix, R=256, tiled=mix_tiled, out_tiled=[(D, MXU_DTYPE)])
    mixed = _mm([mixed_in], [full["w_out"]], F32, "out_proj")

    def norm2_fwd(xv, mx, gt, sc, sh, g):
        x1 = xv + gt * mx
        return x1, _norm_mod(x1, g, sc, sh)

    x1, h2 = rw("norm2", norm2_fwd, R=256, tiled=[(x2d, D, 0), (mixed, D, 0)],
                batch=[(mod, D, GT1), (mod, D, SC2), (mod, D, SH2)], full=[norm2_g],
                out_tiled=[(D, F32), (D, MXU_DTYPE)])
    late_own, late_got = _send_wait("gather_ffn_wait", CHIP_FLIPS, late_start, late_moves, h2)
    for (n, _, axis), own, got in zip(BIG_LATE, late_own, late_got):
        full[n] = _from_shards(lax.dynamic_update_slice(got, own[None], (chip, 0, 0)), axis)
    up = _mm([h2], [full["w_ffn_up"]], MXU_DTYPE, "ffn_up")
    conv_tiled = [(up, DFF, 0), (up, DFF, 1)]
    conv_full = [conv_wg, conv_wu, conv_bg, conv_bu]

    def act_fwd(*a):
        return _silu_gate(*_conv_act(*a))

    (act,) = rw("ffn_act", act_fwd, R=128, tiled=conv_tiled, prev=conv_tiled, full=conv_full,
                out_tiled=[(DFF, MXU_DTYPE)])
    ffn = _mm([act], [full["w_ffn_down"]], F32, "ffn_down")

    def head(x1v, fv, tv, gt, g):
        x2 = x1v + gt * fv
        y, vjp = jax.vjp(_rms, x2, g)
        e = y - tv
        dx2, dg = vjp(e * (1.0 / D))
        loss = jnp.sum(e * e, keepdims=True) * jnp.ones((1, LANES), F32)
        return dx2, dx2 * gt, jnp.sum(dx2 * fv, axis=0, keepdims=True), dg.reshape(1, D), loss

    dx2, d_ffn, d_gt2, g_norm_f, loss_acc = rw(
        "head", head, R=256, tiled=[(x1, D, 0), (ffn, D, 0), (tgt, D, 0)], batch=[(mod, D, GT2)],
        full=[norm_f_g.reshape(1, D)], out_tiled=[(D, F32), (D, MXU_DTYPE)], out_batch=[D],
        out_acc=[(1, D), (1, LANES)])
    loss = lax.psum(0.5 / D * loss_acc[0, 0], ("x", "y", "c"))

    d_act = _mm([d_ffn], [full["w_ffn_down"]], F32, "d_act", bt=True)
    g_w_ffn_down = _mm_tn(act, d_ffn, "g_ffn_down")

    def act_bwd(ug, uu, dact, hg, hu, wg, wu, bg, bu):
        (gate, taps_g), (upv, taps_u) = _conv3(ug, hg, wg, bg), _conv3(uu, hu, wu, bu)
        _, vjp_s = jax.vjp(_silu_gate, gate, upv)
        d_gate, d_upv = vjp_s(dact)
        def taps(dh, shifted):
            return [jnp.sum(dh * s, axis=0, keepdims=True) for s in shifted] + [jnp.sum(dh, axis=0, keepdims=True)]
        return (d_gate, d_upv, *taps(d_gate, taps_g), *taps(d_upv, taps_u))

    dh_g, dh_u, *tapg = rw(
        "ffn_act_bwd", act_bwd, R=128, tiled=conv_tiled + [(d_act, DFF, 0)], prev=conv_tiled, full=conv_full,
        out_tiled=[(DFF, MXU_DTYPE), (DFF, MXU_DTYPE)], out_acc=[(1, DFF)] * 8)
    g_cw_g, g_cb_g = jnp.concatenate(tapg[0:3], axis=0), tapg[3]
    g_cw_u, g_cb_u = jnp.concatenate(tapg[4:7], axis=0), tapg[7]

    def conv_t(dg, du_, ng, nu, wg, wu):
        dg, du_, ng, nu = (z.astype(F32) for z in (dg, du_, ng, nu))

        def ct(d, n, w):
            return w[2:3] * d + w[1:2] * _shift_up(d, n, 1) + w[0:1] * _shift_up(d, n, 2)
        return jnp.concatenate([ct(dg, ng, wg), ct(du_, nu, wu)], axis=1)

    (d_up,) = rw("conv_bwd", conv_t, R=128, tiled=[(dh_g, DFF, 0), (dh_u, DFF, 0)],
                 nxt=[(dh_g, DFF, 0), (dh_u, DFF, 0)], full=[conv_wg, conv_wu], out_tiled=[(2 * DFF, MXU_DTYPE)])
    d_h2 = _mm([d_up], [full["w_ffn_up"]], F32, "d_h2", bt=True)
    g_w_ffn_up = _mm_tn(h2, d_up, "g_ffn_up")

    sds = jax.ShapeDtypeStruct
    reduce_src = lambda r: (lambda ref, me, peer: ref.at[_chip_of(peer), _half(r, peer[2])])

    def reduced_halves(tag, started, moves, after):
        gsh_own, got = _send_wait("rs_%s_wait" % tag, ALL_FLIPS, started, moves, after)
        halves = []
        for i, (g, gt) in enumerate(zip(gsh_own, got)):
            h = g.shape[1] // 2
            own = lax.dynamic_slice(g, (chip, ic * h, 0), (1, h, g.shape[2]))
            halves.append(_sum_slots([own, gt], F32, "rs_%s_sum%d" % (tag, i)))
        return halves

    def share_start(tag, halves):
        moves = [(i, i, lambda ref, me, peer: ref, lambda ref, me, k: ref) for i in range(len(halves))]
        return _send_start("share_%s_start" % tag, PAIR_FLIPS, halves, [sds(g.shape, F32) for g in halves], moves), moves

    def share_finish(tag, started, moves, after, group, grads):
        mine_h, got_h = _send_wait("share_%s_wait" % tag, PAIR_FLIPS, started, moves, after)
        for (n, _, _), mh, gh in zip(group, mine_h, got_h):
            grads[n] = jnp.concatenate([jnp.where(ic == 0, mh, gh), jnp.where(ic == 0, gh, mh)], axis=0)[None]

    gsh_late = [_to_shards(g, ax).astype(MXU_DTYPE) for g, (_, _, ax) in zip((g_w_ffn_up, g_w_ffn_down), BIG_LATE)]
    rsl_moves = [(i, i, reduce_src(g.shape[1]), lambda ref, me, k: ref.at[k]) for i, g in enumerate(gsh_late)]
    rsl = _send_start("rs_ffn_start", ALL_FLIPS, gsh_late,
                      [sds((len(ALL_FLIPS), g.shape[1] // 2, g.shape[2]), MXU_DTYPE) for g in gsh_late], rsl_moves)
    norm2_g = norm2_g + rsl["token"]

    def norm2_bwd(x1v, dh2, dx2v, mx, gt, sc, sh, g):
        _, vjp = jax.vjp(_norm_mod, x1v, g, sc, sh)
        dxn, dg, dsc, dsh = vjp(dh2)
        dx1 = dx2v + dxn
        return dx1, dx1 * gt, jnp.sum(dx1 * mx, axis=0, keepdims=True), dsc, dsh, dg

    dx1, d_mixed, d_gt1, d_sc2, d_sh2, g_norm2 = rw(
        "norm2_bwd", norm2_bwd, R=256, tiled=[(x1, D, 0), (d_h2, D, 0), (dx2, D, 0), (mixed, D, 0)],
        batch=[(mod, D, GT1), (mod, D, SC2), (mod, D, SH2)], full=[norm2_g],
        out_tiled=[(D, F32), (D, MXU_DTYPE)], out_batch=[D, D, D], out_acc=[(1, D)])

    d_mixed_in = _mm([d_mixed], [full["w_out"]], F32, "d_mixed_in", bt=True)
    g_w_out = _mm_tn(mixed_in, d_mixed, "g_w_out")

    def mix_bwd(ga, gb, ya, za, zb, dm):
        _, vjp = jax.vjp(_mix, ga, gb, ya, za, zb)
        dga, dgb, dya, dza, dzb = vjp(dm)
        return jnp.concatenate([dga, dgb], axis=1), dya, jnp.concatenate([dza, dzb], axis=1)

    d_gates, d_ya, d_z = rw("mix_bwd", mix_bwd, R=256, tiled=mix_tiled + [(d_mixed_in, D, 0)],
                            out_tiled=[(2 * D, MXU_DTYPE), (D, MXU_DTYPE), (2 * D, MXU_DTYPE)])
    d_o_rwkv = _mm([d_ya], [full["w_out_rwkv"]], F32, "d_o_rwkv", bt=True)
    g_w_out_rwkv = _mm_tn(o_rwkv, d_ya, "g_out_rwkv")
    d_s5o = _mm([d_z], [full["w_glu"]], F32, "d_s5o", bt=True)
    g_w_glu = _mm_tn(s5o, d_z, "g_glu")

    d_u, d_wb, d_wc, d_ab, g_s5_d = _s5_bwd(u, y_ssm, d_s5o, s5_d, wb, wc, ab, s5_st, s5_x, Bl, S)

    def diag_in(dw):
        t = dw.reshape(NSG, 8, SGC, 8, SP)
        return jnp.einsum("ab,sacbp->sapc", eye8, t).reshape(NG, SP * SGC)

    def diag_out(dw):
        t = dw.reshape(NSG, 8, SP, 8, SGC)
        return jnp.einsum("ab,sapbc->sacp", eye8, t).reshape(NG, SGC, SP)

    g_s5_c_re = diag_out(d_wc[:, :512])
    g_s5_c_im = -diag_out(d_wc[:, 512:])
    disc_cts = (d_ab[0].reshape(NG, SP), d_ab[1].reshape(NG, SP), diag_in(d_wb[:, :, :512]), diag_in(d_wb[:, :, 512:]))
    g_a_re, g_a_im, g_log_dt, g_b_re, g_b_im = _s5_disc_bwd(*s5_in, disc_cts)

    def post_bwd(yv, rv, kv, vv, gv, do, *pp):
        _, vjp = jax.vjp(lambda *a: _rwkv_post(*a, pp[3]), yv, rv, kv, vv, gv, *pp[:3])
        return vjp(do)

    dy_wkv, dr_b, dk_b, dv_b, dg_, g_ln_g, g_ln_b, g_r_k = rw(
        "rwkv_post_bwd", post_bwd, R=256,
        tiled=[(y_wkv, RW, 0), (r_, RW, 0), (k_, RW, 0), (v_, RW, 0), (g_, RW, 0), (d_o_rwkv, RW, 0)],
        full=post_params, out_tiled=[(RW, F32)] * 5, out_acc=[(1, RW)] * 3)
    dr3, dw3, dk3, dv3, da3, db3 = _wkv_bwd(r_, w_, k_, v_, a_, b_, dy_wkv, ck, Bl, S)

    shl, shl_moves = share_start("ffn", reduced_halves("ffn", rsl, rsl_moves, dr3))
    mu_shift = mu_shift + shl["token"]

    def prep_bwd(pv, dr1, dr2, dwv, dk1, dk2, dv1, dv2, dav, dbv, dgv, ph, mu, *pp):
        prev = _shift_down(pv, ph, 1)
        ps = pv + (prev - pv) * mu
        _, vjp = jax.vjp(lambda *q: _rwkv_prep(*q, pp[7]), *_split_ps(ps), *pp[:7])
        grads = vjp((dr1 + dr2, dwv, dk1 + dk2, dv1 + dv2, dav, dbv, dgv))
        dps = jnp.concatenate(grads[:5], axis=1)
        return (dps,) + tuple(grads[5:]) + (jnp.sum(dps * (prev - pv), axis=0, keepdims=True),)

    prep_outs = rw(
        "rwkv_prep_bwd", prep_bwd, R=256,
        tiled=[(p, SHIFT, 0), (dr3, RW, 0), (dr_b, RW, 0), (dw3, RW, 0), (dk3, RW, 0), (dk_b, RW, 0),
               (dv3, RW, 0), (dv_b, RW, 0), (da3, RW, 0), (db3, RW, 0), (dg_, RW, 0)],
        prev=[(p, SHIFT, 0)], full=[mu_shift] + prep_params,
        out_tiled=[(SHIFT, F32)],
        out_acc=[(1, RW), (LW + LA, RW), (1, RW), (LW + LA, RW), (LG, RW), (1, RW), (1, RW), (1, SHIFT)])
    d_ps, g_w0, g_w_up_p, g_a0, g_a_up_p, g_g_up, g_k_k, g_k_a, g_mu = prep_outs

    small = {"mu_shift": g_mu, "rwkv_w0": g_w0, "rwkv_a0": g_a0, "rwkv_k_k": g_k_k,
             "rwkv_k_a": g_k_a, "rwkv_r_k": g_r_k, "rwkv_ln_g": g_ln_g, "rwkv_ln_b": g_ln_b, "s5_a_re": g_a_re,
             "s5_a_im": g_a_im, "s5_log_dt": g_log_dt, "s5_b_re": g_b_re, "s5_b_im": g_b_im, "s5_c_re": g_s5_c_re,
             "s5_c_im": g_s5_c_im, "s5_d": g_s5_d, "norm2_g": g_norm2,
             "ffn_conv_b": jnp.concatenate([g_cb_g, g_cb_u], axis=1), "norm_f_g": g_norm_f}
    small_names = list(small)
    g_conv_w = jnp.concatenate([g_cw_g, g_cw_u], axis=1)
    shard_small = {"rwkv_w_up": g_w_up_p[:LW], "rwkv_a_up": g_a_up_p[LW:], "rwkv_g_up": g_g_up, "ffn_conv_w": g_conv_w}
    parts = [small[n] for n in small_names] + [_to_shards(shard_small[n], ax) for n, _, ax in BIG_SMALL]
    spack = _pack_rows(parts, F32, SUBLANES)
    sm_moves = [(0, 0, lambda ref, me, peer: ref, lambda ref, me, k: ref.at[2 * _chip_of(me) + me[2]])]
    sm = _send_start("gsmall_start", ALL_FLIPS, [spack], [sds((8,) + spack.shape, F32)], sm_moves)
    mu_shift = mu_shift + sm["token"]

    def shift_bwd(dps, nx, mu):
        return dps * (1.0 - mu) + _shift_up(dps * mu, nx * mu, 1)

    (d_p,) = rw("shift_bwd", shift_bwd, R=256, tiled=[(d_ps, SHIFT, 0)], nxt=[(d_ps, SHIFT, 0)], full=[mu_shift],
                out_tiled=[(SHIFT, MXU_DTYPE)])
    g_w_in = jnp.concatenate([_mm_tn(h1, d_p, "g_w_p"), _mm_tn(h1, d_u, "g_w_u"), _mm_tn(h1, d_gates, "g_w_g")], axis=1)
    big_g = {"w_in": g_w_in, "w_out_rwkv": g_w_out_rwkv, "w_glu": g_w_glu, "w_out": g_w_out}
    gsh_now = [_to_shards(big_g[n], ax).astype(MXU_DTYPE) for n, _, ax in now]
    rsn_moves = [(i, i, reduce_src(g.shape[1]), lambda ref, me, k: ref.at[k]) for i, g in enumerate(gsh_now)]
    rsn = _send_start("rs_mix_start", ALL_FLIPS, gsh_now,
                      [sds((len(ALL_FLIPS), g.shape[1] // 2, g.shape[2]), MXU_DTYPE) for g in gsh_now], rsn_moves)
    norm1_g = norm1_g + rsn["token"]
    d_h1 = _mm([d_p, d_u, d_gates], [w_p, w_u, w_g], F32, "d_h1", bt=True)

    def norm1_bwd(xv, dh1, dx1v, sc, sh, g):
        _, vjp = jax.vjp(_norm_mod, xv, g, sc, sh)
        dxn, dg, dsc, dsh = vjp(dh1)
        return dx1v + dxn, dsc, dsh, dg

    grad_x, d_sc1, d_sh1, g_norm1 = rw(
        "norm1_bwd", norm1_bwd, R=256, tiled=[(x2d, D, 0), (d_h1, D, 0), (dx1, D, 0)],
        batch=[(mod, D, SC1), (mod, D, SH1)], full=[norm1_g], out_tiled=[(D, F32)], out_batch=[D, D], out_acc=[(1, D)])

    shn, shn_moves = share_start("mix", reduced_halves("mix", rsn, rsn_moves, grad_x))

    dmod = jnp.concatenate([d_sh1, d_sc1, d_gt1, d_sh2, d_sc2, d_gt2], axis=2).reshape(Bl, 6 * D)
    last_all = _gather_two_level([], [dmod, g_norm1], "gather_dmod")[1]
    dmod_all = last_all[0].reshape(8 * Bl, 6 * D)
    dmod_cols = lax.dynamic_slice_in_dim(dmod_all, chip * ncol, ncol, 1)
    g_w_ada, g_b_ada = _ada_bwd(c_all, dmod_cols, dmod_all)

    grads = {"norm1_g": _sum_slots(last_all[1].reshape(8, 1, D), F32, "sum_norm1")}
    sm_own, sm_got = _send_wait("gsmall_wait", ALL_FLIPS, sm, sm_moves, g_b_ada)
    s_all = lax.dynamic_update_slice(sm_got[0], sm_own[0][None], (dev, 0, 0))
    s_sum = _sum_slots(s_all, F32, "sum_gsmall").reshape(-1)
    off = 0
    for n in small_names:
        grads[n] = s_sum[off:off + W[n].size].reshape(W[n].shape)
        off += W[n].size
    for n, shape, axis in BIG_SMALL:
        ss = _shard_shape(shape, axis)
        k4 = 4 * math.prod(ss)
        sh4 = s_sum[off:off + k4].reshape(4, math.prod(ss))
        grads[n] = lax.dynamic_index_in_dim(sh4, chip, 0, keepdims=False).reshape((1,) + ss)
        off += k4

    share_finish("ffn", shl, shl_moves, s_sum, BIG_LATE, grads)
    share_finish("mix", shn, shn_moves, grads[BIG_LATE[0][0]], now, grads)
    grads["w_ada"] = g_w_ada[None]
    grads["b_ada"] = g_b_ada

    delta, new_m, new_v = {}, {}, {}
    to2 = lambda z: z.reshape(-1, z.shape[-1])
    for n in ["w_ada"] + [b[0] for b in BIG]:
        d_, m_, v2_ = _adamw(to2(W[n]), to2(grads[n]), to2(M[n]), to2(V[n]), "adamw_" + n)
        delta[n], new_m[n], new_v[n] = (z.reshape(W[n].shape) for z in (d_, m_, v2_))
    rest = [n for n in names if n not in delta]
    packs = [_pack_rows([src[n] for n in rest], F32, SUBLANES) for src in (W, grads, M, V)]
    d_, m_, v2_ = _adamw(*packs, "adamw_small")
    shapes = [W[n].shape for n in rest]
    for dst, z in ((delta, d_), (new_m, m_), (new_v, v2_)):
        for n, val in zip(rest, _unpack(z.reshape(-1), shapes)):
            dst[n] = val

    return (loss, grad_x.reshape(Bl, S, D), *[grads[n] for n in names], *[delta[n] for n in names],
            *[new_m[n] for n in names], *[new_v[n] for n in names])
```

```python
import functools
import math

import jax
import jax.numpy as jnp
from jax import lax
from jax.experimental import pallas as pl
from jax.experimental.pallas import tpu as pltpu

F32 = jnp.float32
BF16 = jnp.bfloat16
MXU_DTYPE = jnp.bfloat16
MESH_IDS = pl.DeviceIdType.MESH
HIGHEST = lax.Precision.HIGHEST

D = 1024
RW, NH, HD = 512, 8, 64
LW, LA, LG = 64, 64, 128
SW, SGC, NG, SP = 512, 16, 32, 64
NSG = 4
SHIFT = 3 * RW + LW + LA + LG
DFF = 2816
RMS_EPS, GN_EPS, L2_EPS = 1e-6, 64e-5, 1e-12
LR, B1, B2, ADAM_EPS, WD, STEP = 0.001, 0.9, 0.999, 1e-8, 0.01, 10
DECAY_SCALE = math.exp(-0.5)
GELU_C = math.sqrt(2.0 / math.pi)

VMEM_LIMIT = 52 * 1024 * 1024
SUBLANES, LANES = 8, 128
HALO = 16


def _pick(n, cap):
    if n <= cap:
        return n
    best = None
    for t in range(LANES, cap + 1, LANES):
        if n % t == 0:
            best = t
    assert best is not None, (n, cap)
    return best


def _params(sem=None, vmem=VMEM_LIMIT):
    return pltpu.CompilerParams(dimension_semantics=sem, vmem_limit_bytes=vmem)


def _chip_of(p):
    return 2 * p[0] + p[1]


def _me():
    return (lax.axis_index("x"), lax.axis_index("y"), lax.axis_index("c"))


def _half(rows, core):
    h = rows // 2
    return pl.ds(pl.multiple_of(core * h, 16 if h % 16 == 0 else SUBLANES), h)


_HBM =pl.BlockSpec(memory_space=pltpu.HBM)
_SEM = pl.BlockSpec(memory_space=pltpu.SEMAPHORE)
_DATAFLOW = pltpu.SideEffectType.DATAFLOW_SIDE_EFFECTING


def _split_copies(flips, moves, src_refs, land_refs, send_sems, recv_sems):
    me = _me()
    nf = len(flips)
    out = []
    for m, (si, li, src_sel, dst_sel) in enumerate(moves):
        for k, f in enumerate(flips):
            peer = tuple(1 - v if b else v for v, b in zip(me, f))
            out.append(pltpu.make_async_remote_copy(
                src_ref=src_sel(src_refs[si], me, peer), dst_ref=dst_sel(land_refs[li], me, k),
                send_sem=send_sems.at[m * nf + k], recv_sem=recv_sems.at[m * nf + k],
                device_id=peer, device_id_type=MESH_IDS))
    return out


def _send_start(name, flips, srcs, land_shapes, moves):
    ns, nl = len(srcs), len(land_shapes)
    n = len(moves) * len(flips)

    def body(*refs):
        for cp in _split_copies(flips, moves, refs[:ns], refs[ns:ns + nl], refs[ns + nl], refs[ns + nl + 1]):
            cp.start()
        refs[-1][...] = jnp.zeros(refs[-1].shape, F32)

    hbm = lambda z: pltpu.with_memory_space_constraint(z, pltpu.HBM)
    lands = [lax.empty(s.shape, s.dtype) for s in land_shapes]
    res = pl.pallas_call(
        body, name=name,
        out_shape=(pltpu.SemaphoreType.DMA((n,)), pltpu.SemaphoreType.DMA((n,)),
                   *[pltpu.HBM(z.shape, z.dtype) for z in srcs], *[pltpu.HBM(s.shape, s.dtype) for s in land_shapes],
                   jax.ShapeDtypeStruct((SUBLANES, LANES), F32)),
        in_specs=[_HBM] * (ns + nl),
        out_specs=(_SEM, _SEM, *[_HBM] * (ns + nl), pl.BlockSpec(memory_space=pltpu.VMEM)),
        input_output_aliases={i: 2 + i for i in range(ns + nl)},
        compiler_params=pltpu.CompilerParams(has_side_effects=_DATAFLOW),
    )(*[hbm(z) for z in srcs], *[hbm(z) for z in lands])
    return {"sems": res[:2], "srcs": list(res[2:2 + ns]), "lands": list(res[2 + ns:2 + ns + nl]), "token": res[-1][0, 0]}


def _send_wait(name, flips, started, moves, after):
    srcs, lands = started["srcs"], started["lands"]
    ns, nl = len(srcs), len(lands)

    def body(*refs):
        for cp in _split_copies(flips, moves, refs[:ns], refs[ns:ns + nl], refs[ns + nl], refs[ns + nl + 1]):
            cp.wait_send()
            cp.wait_recv()

    res = pl.pallas_call(
        body, name=name, out_shape=[pltpu.HBM(z.shape, z.dtype) for z in srcs + lands],
        in_specs=[_HBM] * (ns + nl) + [_SEM, _SEM, pl.BlockSpec(memory_space=pl.ANY)],
        out_specs=[_HBM] * (ns + nl), input_output_aliases={i: i for i in range(ns + nl)},
        compiler_params=pltpu.CompilerParams(has_side_effects=_DATAFLOW),
    )(*srcs, *lands, *started["sems"], after)
    return list(res[:ns]), list(res[ns:])


CHIP_FLIPS = ((1, 0, 0), (0, 1, 0), (1, 1, 0))
PAIR_FLIPS = ((0, 0, 1),)
ALL_FLIPS = CHIP_FLIPS + ((1, 0, 1), (0, 1, 1), (1, 1, 1)) + PAIR_FLIPS


def _gather_two_level(chip_arrs, dev_arrs, name):
    arrs = list(chip_arrs) + list(dev_arrs)
    n, nchip = len(arrs), len(chip_arrs)
    NS = 7

    def body(*refs):
        srcs, outs = refs[:n], refs[n:2 * n]
        send_sems, recv_sems, loc_sems = refs[2 * n:]
        x, y, c = _me()
        sib = (x, y, 1 - c)
        chips = [(1 - x, y), (x, 1 - y), (1 - x, 1 - y)]
        mine = 2 * x + y
        ids = [2 * cx + cy for cx, cy in chips]

        def part(i, slot, core):
            if i < nchip:
                return outs[i].at[slot, _half(arrs[i].shape[0], core)]
            return outs[i].at[slot, core]

        def rcopy(i, k, src, dst, to):
            return pltpu.make_async_remote_copy(src_ref=src, dst_ref=dst, send_sem=send_sems.at[i * NS + k],
                                                recv_sem=recv_sems.at[i * NS + k], device_id=to, device_id_type=MESH_IDS)

        started, locs = [], []
        for i in range(n):
            own = srcs[i].at[_half(arrs[i].shape[0], c)] if i < nchip else srcs[i]
            loc = pltpu.make_async_copy(srcs[i], outs[i].at[mine] if i < nchip else outs[i].at[mine, c], loc_sems.at[i])
            loc.start()
            locs.append(loc)
            for f, chip in enumerate(chips):
                cp = rcopy(i, f, own, part(i, mine, c), (*chip, c))
                cp.start()
                started.append(cp)
            if i >= nchip:
                cp = rcopy(i, 6, own, part(i, mine, c), sib)
                cp.start()
                started.append(cp)
        for i in range(n):
            for f in range(3):
                land = part(i, ids[f], c)
                rcopy(i, f, land, land, sib).wait_recv()
                fw = rcopy(i, 3 + f, land, land, sib)
                fw.start()
                started.append(fw)
        for i in range(n):
            for f in range(3):
                land = part(i, ids[f], 1 - c)
                rcopy(i, 3 + f, land, land, sib).wait_recv()
            if i >= nchip:
                land = part(i, mine, 1 - c)
                rcopy(i, 6, land, land, sib).wait_recv()
        for cp in started:
            cp.wait_send()
        for loc in locs:
            loc.wait()

    outs = [jax.ShapeDtypeStruct((4,) + a.shape, a.dtype) for a in chip_arrs]
    outs += [jax.ShapeDtypeStruct((4, 2) + a.shape, a.dtype) for a in dev_arrs]
    res = pl.pallas_call(
        body, name=name, out_shape=outs,
        in_specs=[pl.BlockSpec(memory_space=pl.ANY)] * n, out_specs=[pl.BlockSpec(memory_space=pl.ANY)] * n,
        scratch_shapes=[pltpu.SemaphoreType.DMA((n * NS,)), pltpu.SemaphoreType.DMA((n * NS,)),
                        pltpu.SemaphoreType.DMA((n,))],
    )(*arrs)
    return res[:nchip], res[nchip:]


def _mm(As, Bs, out_dtype, name, tm=512, cap=1408, bt=False):
    n = len(As)
    M, N = As[0].shape[0], Bs[0].shape[0 if bt else 1]
    if sum(a.shape[1] for a in As) <= 1024:
        tm = 2 * tm
    tm = min(tm, M)
    tn = _pick(N, cap)
    dims = (((1,), (1,)), ((), ())) if bt else (((1,), (0,)), ((), ()))

    def body(*refs):
        o = refs[2 * n]
        acc = None
        for a, b in zip(refs[:n], refs[n:2 * n]):
            d = lax.dot_general(a[...].astype(MXU_DTYPE), b[...].astype(MXU_DTYPE), dims, preferred_element_type=F32)
            acc = d if acc is None else acc + d
        o[...] = acc.astype(o.dtype)

    in_specs = [pl.BlockSpec((tm, a.shape[1]), lambda i, j: (i, 0)) for a in As]
    if bt:
        in_specs += [pl.BlockSpec((tn, b.shape[1]), lambda i, j: (j, 0)) for b in Bs]
    else:
        in_specs += [pl.BlockSpec((b.shape[0], tn), lambda i, j: (0, j)) for b in Bs]
    return pl.pallas_call(
        body, name=name, grid=(M // tm, N // tn), in_specs=in_specs,
        out_specs=pl.BlockSpec((tm, tn), lambda i, j: (i, j)),
        out_shape=jax.ShapeDtypeStruct((M, N), out_dtype),
        compiler_params=_params(("parallel", "parallel")),
    )(*As, *Bs)


def _mm_tn(A, G, name, tt=1024, cap=1408):
    T, Ka = A.shape
    N = G.shape[1]
    tt = min(tt, T)
    tk = _pick(Ka, cap)
    tn = _pick(N, cap)

    def body(a, g, o):
        @pl.when(pl.program_id(2) == 0)
        def _():
            o[...] = jnp.zeros(o.shape, F32)
        o[...] += lax.dot_general(a[...].astype(MXU_DTYPE), g[...].astype(MXU_DTYPE),
                                  (((0,), (0,)), ((), ())), preferred_element_type=F32)

    return pl.pallas_call(
        body, name=name, grid=(Ka // tk, N // tn, T // tt),
        in_specs=[pl.BlockSpec((tt, tk), lambda i, j, t: (t, i)), pl.BlockSpec((tt, tn), lambda i, j, t: (t, j))],
        out_specs=pl.BlockSpec((tk, tn), lambda i, j, t: (i, j)),
        out_shape=jax.ShapeDtypeStruct((Ka, N), F32),
        compiler_params=_params(("parallel", "parallel", "arbitrary")),
    )(A, G)


def _rowwise(name, fn, *, Bl, S, R, tiled=(), prev=(), nxt=(), batch=(), full=(),
             out_tiled=(), out_batch=(), out_acc=()):
    R = min(R, S)
    nS = S // R
    T = Bl * S
    hb = R // HALO
    n_in = len(tiled) + len(prev) + len(nxt) + len(batch) + len(full)

    in_specs, args = [], []
    for a, wd, cb in tiled:
        in_specs.append(pl.BlockSpec((R, wd), lambda b, i, cb=cb: (b * nS + i, cb)))
        args.append(a)
    for a, wd, cb in prev:
        in_specs.append(pl.BlockSpec((HALO, wd), lambda b, i, cb=cb: (jnp.maximum((b * nS + i) * hb - 1, 0), cb)))
        args.append(a)
    for a, wd, cb in nxt:
        in_specs.append(pl.BlockSpec((HALO, wd), lambda b, i, cb=cb: (jnp.minimum((b * nS + i + 1) * hb, T // HALO - 1), cb)))
        args.append(a)
    for a, wd, cb in batch:
        in_specs.append(pl.BlockSpec((1, 1, wd), lambda b, i, cb=cb: (b, 0, cb)))
        args.append(a)
    for a in full:
        in_specs.append(pl.BlockSpec(a.shape, lambda b, i, nd=a.ndim: (0,) * nd))
        args.append(a)

    out_specs, out_shape = [], []
    for C, dt in out_tiled:
        out_specs.append(pl.BlockSpec((R, C), lambda b, i: (b * nS + i, 0)))
        out_shape.append(jax.ShapeDtypeStruct((T, C), dt))
    for C in out_batch:
        out_specs.append(pl.BlockSpec((1, 1, C), lambda b, i: (b, 0, 0)))
        out_shape.append(jax.ShapeDtypeStruct((Bl, 1, C), F32))
    for shp in out_acc:
        out_specs.append(pl.BlockSpec(shp, lambda b, i, nd=len(shp): (0,) * nd))
        out_shape.append(jax.ShapeDtypeStruct(shp, F32))

    nt, npv, nnx, nbt = len(tiled), len(prev), len(nxt), len(batch)

    def body(*refs):
        b, i = pl.program_id(0), pl.program_id(1)
        ins, outs = refs[:n_in], refs[n_in:]
        vals = [r[...] for r in ins[:nt]]
        vals += [jnp.where(i > 0, r[...], jnp.zeros(r.shape, r.dtype)) for r in ins[nt:nt + npv]]
        vals += [jnp.where(i < nS - 1, r[...], jnp.zeros(r.shape, r.dtype)) for r in ins[nt + npv:nt + npv + nnx]]
        vals += [r[0] for r in ins[nt + npv + nnx:nt + npv + nnx + nbt]]
        vals += [r[...] for r in ins[nt + npv + nnx + nbt:]]
        res = fn(*vals)
        if not isinstance(res, (tuple, list)):
            res = (res,)
        k = 0
        for _ in out_tiled:
            outs[k][...] = res[k].astype(outs[k].dtype)
            k += 1
        for _ in out_batch:
            o = outs[k]

            @pl.when(i == 0)
            def _(o=o):
                o[...] = jnp.zeros(o.shape, F32)
            o[0] += res[k]
            k += 1
        for _ in out_acc:
            o = outs[k]

            @pl.when((i == 0) & (b == 0))
            def _(o=o):
                o[...] = jnp.zeros(o.shape, F32)
            o[...] += res[k]
            k += 1

    out = pl.pallas_call(
        body, name=name, grid=(Bl, nS), in_specs=in_specs, out_specs=out_specs, out_shape=out_shape,
        compiler_params=_params(("arbitrary", "arbitrary")),
    )(*args)
    return out


def _colwise(name, fn, *, Bl, S, R, W, strip, tiled=(), prev=(), nxt=(), full=(), out_tiled=(), n_acc=0):
    R = min(R, S)
    nS = S // R
    T = Bl * S
    hb = R // HALO
    nt, npv, nnx, nfl = len(tiled), len(prev), len(nxt), len(full)
    n_in = nt + npv + nnx + nfl
    in_specs = [pl.BlockSpec((R, W), lambda b, i, cb=cb: (b * nS + i, cb)) for _, cb in tiled]
    in_specs += [pl.BlockSpec((HALO, W), lambda b, i, cb=cb: (jnp.maximum((b * nS + i) * hb - 1, 0), cb)) for _, cb in prev]
    in_specs += [pl.BlockSpec((HALO, W), lambda b, i, cb=cb: (jnp.minimum((b * nS + i + 1) * hb, T // HALO - 1), cb))
                 for _, cb in nxt]
    in_specs += [pl.BlockSpec(a.shape, lambda b, i: (0, 0)) for a in full]
    out_specs = [pl.BlockSpec((R, m * W), lambda b, i: (b * nS + i, 0)) for m, _ in out_tiled]
    out_specs += [pl.BlockSpec((1, W), lambda b, i: (0, 0))] * n_acc
    out_shape = [jax.ShapeDtypeStruct((T, m * W), dt) for m, dt in out_tiled] + [jax.ShapeDtypeStruct((1, W), F32)] * n_acc

    def body(*refs):
        b, i = pl.program_id(0), pl.program_id(1)
        ins, outs = refs[:n_in], refs[n_in:]

        @pl.when((i == 0) & (b == 0))
        def _():
            for o in outs[len(out_tiled):]:
                o[...] = jnp.zeros(o.shape, F32)

        def col(j, carry):
            cs = pl.ds(pl.multiple_of(j * strip, strip), strip)
            vals = [r[:, cs] for r in ins[:nt]]
            vals += [jnp.where(i > 0, r[:, cs], jnp.zeros((HALO, strip), r.dtype)) for r in ins[nt:nt + npv]]
            vals += [jnp.where(i < nS - 1, r[:, cs], jnp.zeros((HALO, strip), r.dtype)) for r in ins[nt + npv:nt + npv + nnx]]
            vals += [r[:, cs] for r in ins[nt + npv + nnx:]]
            res = fn(*vals)
            for k, (m, _) in enumerate(out_tiled):
                for q in range(m):
                    outs[k][:, pl.ds(pl.multiple_of(q * W + j * strip, strip), strip)] = res[k][q].astype(outs[k].dtype)
            for k in range(len(out_tiled), len(outs)):
                outs[k][:, cs] += res[k]
            return carry

        lax.fori_loop(0, W // strip, col, 0)

    return pl.pallas_call(
        body, name=name, grid=(Bl, nS), in_specs=in_specs, out_specs=out_specs, out_shape=out_shape,
        compiler_params=_params(("arbitrary", "arbitrary")),
    )(*[a for a, _ in tiled], *[a for a, _ in prev], *[a for a, _ in nxt], *full)


def _shift_down(x, halo, k):
    rolled = pltpu.roll(x, k, 0)
    row = lax.broadcasted_iota(jnp.int32, (SUBLANES, x.shape[1]), 0)
    head = rolled[0:SUBLANES]
    for j in range(k):
        head = jnp.where(row == j, halo[HALO - k + j:HALO - k + j + 1, :], head)
    return jnp.concatenate([head, rolled[SUBLANES:]], axis=0)


def _shift_up(x, halo, k):
    n = x.shape[0]
    rolled = pltpu.roll(x, n - k, 0)
    row = lax.broadcasted_iota(jnp.int32, (SUBLANES, x.shape[1]), 0)
    tail = rolled[n - SUBLANES:]
    for j in range(k):
        tail = jnp.where(row == SUBLANES - k + j, halo[j:j + 1, :], tail)
    return jnp.concatenate([rolled[:n - SUBLANES], tail], axis=0)


def _dotm(a, b):
    return jnp.dot(a.astype(MXU_DTYPE), b.astype(MXU_DTYPE), preferred_element_type=F32)


def _split_bf16(x):
    hi = x.astype(BF16)
    return hi, (x - hi.astype(F32)).astype(BF16)


def _headsum_2pass(x, hm):
    hi, lo = _split_bf16(x)
    hb = hm.astype(BF16)
    return jnp.dot(hi, hb, preferred_element_type=F32) + jnp.dot(lo, hb, preferred_element_type=F32)


@jax.custom_vjp
def _headsum(x, hm):
    return _headsum_2pass(x, hm)


_headsum.defvjp(lambda x, hm: (_headsum_2pass(x, hm), hm),
                lambda hm, g: (_headsum_2pass(g, hm), jnp.zeros_like(hm)))


def _sigmoid(x):
    return 1.0 / (1.0 + jnp.exp(-x))


def _rms(x, g):
    return x * lax.rsqrt(jnp.mean(x * x, axis=-1, keepdims=True) + RMS_EPS) * g


def _norm_mod(x, g, sc, sh):
    return _rms(x, g) * (1.0 + sc) + sh


def _split_ps(ps):
    return (ps[:, 0:RW], ps[:, RW:2 * RW], ps[:, 2 * RW:3 * RW], ps[:, 3 * RW:3 * RW + LW + LA],
            ps[:, 3 * RW + LW + LA:SHIFT])


def _rwkv_prep(r, k, v, wa, gd, w0, w_up_p, a0, a_up_p, g_up, k_k, k_a, hm):
    w_raw = w0 + _dotm(jnp.tanh(wa), w_up_p)
    decay = jnp.exp(-DECAY_SCALE * _sigmoid(w_raw))
    a = _sigmoid(a0 + _dotm(wa, a_up_p))
    g = _dotm(_sigmoid(gd), g_up)
    kk = k * k_k
    kk = kk * lax.rsqrt(_headsum(kk * kk, hm) + L2_EPS)
    k2 = k * (1.0 + (a - 1.0) * k_a)
    return r, decay, k2, v, -kk, kk * a, g


def _rwkv_post(y, r, k2, v, g, ln_g, ln_b, r_k, hm):
    mean = _headsum(y, hm) * (1.0 / HD)
    yc = y - mean
    var = _headsum(yc * yc, hm) * (1.0 / HD)
    yn = yc * lax.rsqrt(var + GN_EPS) * ln_g + ln_b
    bonus = _headsum(r * k2 * r_k, hm) * v
    return (yn + bonus) * g


def _gelu(x):
    return 0.5 * x * (1.0 + jnp.tanh(GELU_C * (x + 0.044715 * (x * x * x))))


def _s5_post(yssm, u, d):
    return _gelu(yssm + d * u)


def _mix(ga, gb, ya, za, zb):
    return _sigmoid(ga) * ya + _sigmoid(gb) * (za * _sigmoid(zb))


def _conv_act(up_g, up_u, hg, hu, w_g, w_u, b_g, b_u):
    gate, upv = _conv3(up_g, hg, w_g, b_g)[0], _conv3(up_u, hu, w_u, b_u)[0]
    return gate, upv


def _conv3(x, h, w, b):
    x, h = x.astype(F32), h.astype(F32)
    s2, s1 = _shift_down(x, h, 2), _shift_down(x, h, 1)
    return b + w[0:1] * s2 + w[1:2] * s1 + w[2:3] * x, (s2, s1, x)


def _silu_gate(gate, upv):
    return gate * _sigmoid(gate) * upv


WKV_L = 64
_NT, _NN, _TN = ((1,), (1,)), ((1,), (0,)), ((0,), (0,))


def _dotw(x, y, dims):
    return lax.dot_general(x.astype(MXU_DTYPE), y.astype(MXU_DTYPE), (dims, ((), ())), preferred_element_type=F32)


def _dot3(x, y, dims):
    (xh, xl), (yh, yl) = _split_bf16(x), _split_bf16(y)
    d = lambda p, q: lax.dot_general(p, q, (dims, ((), ())), preferred_element_type=F32)
    return d(xh, yh) + d(xh, yl) + d(xl, yh)


@jax.custom_vjp
def _gram3(x, y):
    return _dot3(x, y, _NT)


_gram3.defvjp(lambda x, y: (_dot3(x, y, _NT), (x, y)),
              lambda res, g: (_dot3(g, res[1], _NN), _dot3(g, res[0], _TN)))


def _tri_solve_fwd(ns, xs):
    each = lambda f, *ls: tuple(f(*zs) for zs in zip(*ls))
    size = ns[0].shape[0]
    eye = (lax.broadcasted_iota(jnp.int32, (size, size), 0) == lax.broadcasted_iota(jnp.int32, (size, size), 1)).astype(F32)
    ts = each(lambda n: n + eye, ns)
    qs = ns
    for _ in range(WKV_L.bit_length() - 2):
        qs = each(lambda q: _dotw(q, q, _NN), qs)
        ts = each(lambda t, q: t + _dotw(t, q, _NN), ts, qs)
    us = each(lambda t, x: _dotw(t, x, _NN), ts, xs)
    return us, (ts, us)


def _tri_solve_bwd(res, dus):
    ts, us = res
    each = lambda f, *ls: tuple(f(*zs) for zs in zip(*ls))
    dxs = each(lambda t, du: _dotw(t, du, _TN), ts, dus)
    return each(lambda dx, u: _dotw(dx, u, _NT), dxs, us), dxs


@jax.custom_vjp
def _tri_solve(ns, xs):
    return _tri_solve_fwd(ns, xs)[0]


_tri_solve.defvjp(_tri_solve_fwd, _tri_solve_bwd)


def _wkv_chunk(s0, r, w, k, v, a, b):
    y, s1 = _wkv_chunks((s0,), (r,), (w,), (k,), (v,), (a,), (b,))
    return y[0], s1[0]


def _wkv_chunks(s0, r, w, k, v, a, b):
    each = lambda f, *ls: tuple(f(*xs) for xs in zip(*ls))
    L = r[0].shape[0]
    n2 = 2 * L
    lane_head = lax.broadcasted_iota(jnp.int32, (2, 1, 2 * HD), 2) // HD
    head_mask = (lane_head == lax.broadcasted_iota(jnp.int32, (2, 1, 2 * HD), 0)).astype(F32)
    ri = lax.broadcasted_iota(jnp.int32, (n2, n2), 0)
    ci = lax.broadcasted_iota(jnp.int32, (n2, n2), 1)
    same = (ri // L) == (ci // L)
    strict = same & ((ci % L) < (ri % L))
    incl = same & ((ci % L) <= (ri % L))
    si = lax.broadcasted_iota(jnp.int32, (2 * HD, 2 * HD), 0) // HD
    sj = lax.broadcasted_iota(jnp.int32, (2 * HD, 2 * HD), 1) // HD
    tri = (lax.broadcasted_iota(jnp.int32, (L, L), 0) >= lax.broadcasted_iota(jnp.int32, (L, L), 1)).astype(F32)

    stack = lambda z: (z[None] * head_mask).reshape(n2, 2 * HD)
    dup = lambda z: jnp.broadcast_to(z[None], (2, L, 2 * HD)).reshape(n2, 2 * HD)
    gram = _gram3
    nt, nn, tn = (lambda x, y, d=d: _dotw(x, y, d) for d in (_NT, _NN, _TN))
    add = lambda x, y: x + y

    lw = each(jnp.log, w)
    cum = each(lambda z: jnp.dot(tri, z, preferred_element_type=F32, precision=HIGHEST), lw)
    tot = each(lambda z: jnp.sum(z, axis=0, keepdims=True), lw)
    a2 = each(lambda av, cv, lv: stack(av * jnp.exp(cv - lv)), a, cum, lw)
    r2 = each(lambda rv, cv: stack(rv * jnp.exp(cv)), r, cum)
    v2 = each(stack, v)
    b2 = each(lambda bv, cv: dup(bv * jnp.exp(-cv)), b, cum)
    k2 = each(lambda kv, cv: dup(kv * jnp.exp(-cv)), k, cum)
    n_ab = each(lambda x, y: jnp.where(strict, gram(x, y), 0.0), a2, b2)
    n_ak = each(lambda x, y: jnp.where(strict, gram(x, y), 0.0), a2, k2)
    m_rb = each(lambda x, y: jnp.where(incl, gram(x, y), 0.0), r2, b2)
    m_rk = each(lambda x, y: jnp.where(incl, gram(x, y), 0.0), r2, k2)
    u = _tri_solve(n_ab, each(add, each(nt, a2, s0), each(nn, n_ak, v2)))
    y2 = each(lambda x, y, z: x + y + z, each(nt, r2, s0), each(nn, m_rb, u), each(nn, m_rk, v2))
    y = each(lambda z: jnp.sum(z.reshape(2, L, 2 * HD), axis=0), y2)
    b3 = each(lambda bv, tv, cv: dup(bv * jnp.exp(tv - cv)), b, tot, cum)
    k3 = each(lambda kv, tv, cv: dup(kv * jnp.exp(tv - cv)), k, tot, cum)
    upd = each(add, each(tn, u, b3), each(tn, v2, k3))
    s1 = each(lambda sv, tv, uv: sv * jnp.exp(tv) + jnp.where(si == sj, uv, 0.0), s0, tot, upd)
    return y, s1


NPAIR = NH // 2


def _wkv_nb(Bl):
    return 2 if Bl % 2 == 0 else 1


def _wkv_fwd(r, w, k, v, a, b, Bl, S):
    L = WKV_L
    nC = S // L
    nb = _wkv_nb(Bl)
    chains = [(bi, p, slice(p * 2 * HD, (p + 1) * 2 * HD)) for bi in range(nb) for p in range(NPAIR)]

    def body(r_ref, w_ref, k_ref, v_ref, a_ref, b_ref, y_ref, ck_ref, s_ref):
        @pl.when(pl.program_id(1) == 0)
        def _():
            s_ref[...] = jnp.zeros(s_ref.shape, F32)
        s0 = tuple(s_ref[bi, p] for bi, p, _ in chains)
        ops = [tuple(z[bi, :, cs] for bi, _, cs in chains) for z in (r_ref, w_ref, k_ref, v_ref, a_ref, b_ref)]
        y, s1 = _wkv_chunks(s0, *ops)
        for i, (bi, p, cs) in enumerate(chains):
            ck_ref[bi, 0, p] = s0[i]
            y_ref[bi, :, cs] = y[i]
            s_ref[bi, p] = s1[i]

    to3 = lambda z: z.reshape(Bl, S, RW)
    row_spec = pl.BlockSpec((nb, L, RW), lambda g, c: (g, c, 0))
    y, ck = pl.pallas_call(
        body, name="wkv_fwd", grid=(Bl // nb, nC), in_specs=[row_spec] * 6,
        out_specs=[row_spec, pl.BlockSpec((nb, 1, NPAIR, 2 * HD, 2 * HD), lambda g, c: (g, c, 0, 0, 0))],
        out_shape=[jax.ShapeDtypeStruct((Bl, S, RW), F32), jax.ShapeDtypeStruct((Bl, nC, NPAIR, 2 * HD, 2 * HD), F32)],
        scratch_shapes=[pltpu.VMEM((nb, NPAIR, 2 * HD, 2 * HD), F32)],
        compiler_params=_params(("arbitrary", "arbitrary")),
    )(*(to3(z) for z in (r, w, k, v, a, b)))
    return y.reshape(Bl * S, RW), ck


def _wkv_bwd(r, w, k, v, a, b, dy, ck, Bl, S):
    L = WKV_L
    nC = S // L
    nb = _wkv_nb(Bl)
    chains = [(bi, p, slice(p * 2 * HD, (p + 1) * 2 * HD)) for bi in range(nb) for p in range(NPAIR)]

    def body(r_ref, w_ref, k_ref, v_ref, a_ref, b_ref, dy_ref, ck_ref,
             dr_ref, dw_ref, dk_ref, dv_ref, da_ref, db_ref, ds_ref):
        @pl.when(pl.program_id(1) == 0)
        def _():
            ds_ref[...] = jnp.zeros(ds_ref.shape, F32)
        s0 = tuple(ck_ref[bi, 0, p] for bi, p, _ in chains)
        ops = [tuple(z[bi, :, cs] for bi, _, cs in chains) for z in (r_ref, w_ref, k_ref, v_ref, a_ref, b_ref)]
        cts = (tuple(dy_ref[bi, :, cs] for bi, _, cs in chains), tuple(ds_ref[bi, p] for bi, p, _ in chains))
        ds0, *grads = jax.vjp(_wkv_chunks, s0, *ops)[1](cts)
        for i, (bi, p, cs) in enumerate(chains):
            ds_ref[bi, p] = ds0[i]
            for o, g in zip((dr_ref, dw_ref, dk_ref, dv_ref, da_ref, db_ref), grads):
                o[bi, :, cs] = g[i]

    to3 = lambda z: z.reshape(Bl, S, RW)
    row_spec = pl.BlockSpec((nb, L, RW), lambda g, c: (g, nC - 1 - c, 0))
    rows = jax.ShapeDtypeStruct((Bl, S, RW), F32)
    outs = pl.pallas_call(
        body, name="wkv_bwd", grid=(Bl // nb, nC),
        in_specs=[row_spec] * 7 + [pl.BlockSpec((nb, 1, NPAIR, 2 * HD, 2 * HD), lambda g, c: (g, nC - 1 - c, 0, 0, 0))],
        out_specs=[row_spec] * 6, out_shape=[rows] * 6,
        scratch_shapes=[pltpu.VMEM((nb, NPAIR, 2 * HD, 2 * HD), F32)],
        compiler_params=_params(("arbitrary", "arbitrary")),
    )(*(to3(z) for z in (r, w, k, v, a, b, dy)), ck)
    return [o.reshape(Bl * S, RW) for o in outs]


NST = NG * SP


def _cmul(ar, ai, br, bi):
    return ar * br - ai * bi, ar * bi + ai * br


def _s5_tiles(are, aim, reverse):
    if reverse:
        aim = -aim
    row = lax.broadcasted_iota(jnp.int32, (SUBLANES, NST), 0)
    pw = [(are, aim)]
    for _ in range(SUBLANES - 1):
        pw.append(_cmul(pw[-1][0], pw[-1][1], are, aim))
    bc = lambda z: jnp.broadcast_to(z, (SUBLANES, NST))
    ms = []
    for kk in (1, 2, 4):
        cond = (row < SUBLANES - kk) if reverse else (row >= kk)
        ms.append((jnp.where(cond, bc(pw[kk - 1][0]), 0.0), jnp.where(cond, bc(pw[kk - 1][1]), 0.0)))
    pr = jnp.zeros((SUBLANES, NST), F32)
    pi = jnp.zeros((SUBLANES, NST), F32)
    for i in range(SUBLANES):
        n = SUBLANES - i if reverse else i + 1
        pr = jnp.where(row == i, bc(pw[n - 1][0]), pr)
        pi = jnp.where(row == i, bc(pw[n - 1][1]), pi)
    return ms, (pr, pi)


def _s5_block(re, im, ms, pc, cre, cim, sg, reverse):
    ln = slice(sg * 512, (sg + 1) * 512)
    for (mr, mi), kk in zip(ms, (1, 2, 4)):
        sh = SUBLANES - kk if reverse else kk
        sre, sim = pltpu.roll(re, sh, 0), pltpu.roll(im, sh, 0)
        tr, ti = _cmul(mr[:, ln], mi[:, ln], sre, sim)
        re, im = re + tr, im + ti
    tr, ti = _cmul(pc[0][:, ln], pc[1][:, ln], cre[:, ln], cim[:, ln])
    return re + tr, im + ti


def _s5_scan(X_ref, n_rows, ms, pc, cre, cim, reverse, visit=None, acc0=None):
    nblk = n_rows // SUBLANES

    def it(i, carry):
        cre, cim, acc = carry
        j = nblk - 1 - i if reverse else i
        rows = pl.ds(pl.multiple_of(j * SUBLANES, SUBLANES), SUBLANES)
        edge = 0 if reverse else SUBLANES - 1
        blocks, ncre, ncim = [], [], []
        for sg in range(NSG):
            lr = slice(sg * 1024, sg * 1024 + 512)
            li = slice(sg * 1024 + 512, (sg + 1) * 1024)
            re, im = _s5_block(X_ref[rows, lr], X_ref[rows, li], ms, pc, cre, cim, sg, reverse)
            X_ref[rows, lr] = re
            X_ref[rows, li] = im
            blocks.append((re, im))
            ncre.append(re[edge:edge + 1])
            ncim.append(im[edge:edge + 1])
        if visit is not None:
            acc = visit(j, blocks, acc)
        return jnp.concatenate(ncre, axis=1), jnp.concatenate(ncim, axis=1), acc

    return lax.fori_loop(0, nblk, it, (cre, cim, acc0 if acc0 is not None else 0))


def _s5_fwd(u, wb, wc, ab, d, Bl, S, R=256):
    R = min(R, S)
    nC = S // R

    def body(u_ref, wb_ref, wc_ref, ab_ref, d_ref, y_ref, st_ref, X_ref, o_ref, car_ref):
        @pl.when(pl.program_id(1) == 0)
        def _():
            car_ref[...] = jnp.zeros(car_ref.shape, F32)
        st_ref[0, 0] = car_ref[...]
        ms, pc = _s5_tiles(ab_ref[0:1], ab_ref[1:2], False)
        for sg in range(NSG):
            X_ref[:, sg * 1024:(sg + 1) * 1024] = _dotm(u_ref[:, sg * 128:(sg + 1) * 128], wb_ref[sg])
        cre, cim, _ = _s5_scan(X_ref, R, ms, pc, car_ref[0:1], car_ref[1:2], False)
        car_ref[0:1] = cre
        car_ref[1:2] = cim
        for sg in range(NSG):
            y_ref[:, sg * 128:(sg + 1) * 128] = _dotm(X_ref[:, sg * 1024:(sg + 1) * 1024], wc_ref[sg])
        o_ref[...] = _s5_post(y_ref[...], u_ref[...], d_ref[...]).astype(o_ref.dtype)

    rows = pl.BlockSpec((R, SW), lambda b, c: (b * nC + c, 0))
    return pl.pallas_call(
        body, name="s5_fwd", grid=(Bl, nC),
        in_specs=[rows, pl.BlockSpec(wb.shape, lambda b, c: (0, 0, 0)), pl.BlockSpec(wc.shape, lambda b, c: (0, 0, 0)),
                  pl.BlockSpec(ab.shape, lambda b, c: (0, 0)), pl.BlockSpec(d.shape, lambda b, c: (0, 0))],
        out_specs=[rows, pl.BlockSpec((1, 1, 2, NST), lambda b, c: (b, c, 0, 0)),
                   pl.BlockSpec((R, 2 * NST), lambda b, c: (b * nC + c, 0)), rows],
        out_shape=[jax.ShapeDtypeStruct((Bl * S, SW), F32), jax.ShapeDtypeStruct((Bl, nC, 2, NST), F32),
                   jax.ShapeDtypeStruct((Bl * S, 2 * NST), F32), jax.ShapeDtypeStruct((Bl * S, SW), MXU_DTYPE)],
        scratch_shapes=[pltpu.VMEM((2, NST), F32)],
        compiler_params=_params(("arbitrary", "arbitrary")),
    )(u, wb, wc, ab, d)


def _s5_bwd(u, y, do, d, wb, wc, ab, st, xs, Bl, S, R=256):
    R = min(R, S)
    nC = S // R

    def body(u_ref, y_ref, do_ref, d_ref, wb_ref, wc_ref, ab_ref, st_ref, X_ref,
             du_ref, dwb_ref, dwc_ref, dab_ref, dd_ref, G_ref, car_ref):
        first = (pl.program_id(0) == 0) & (pl.program_id(1) == 0)

        @pl.when(first)
        def _():
            for o in (dwb_ref, dwc_ref, dab_ref, dd_ref):
                o[...] = jnp.zeros(o.shape, F32)

        @pl.when(pl.program_id(1) == 0)
        def _():
            car_ref[...] = jnp.zeros(car_ref.shape, F32)

        are, aim = ab_ref[0:1], ab_ref[1:2]
        dy, du_direct, dd = jax.vjp(_s5_post, y_ref[...], u_ref[...], d_ref[...])[1](do_ref[...])
        dd_ref[...] += dd
        dyv = dy.astype(MXU_DTYPE)
        for sg in range(NSG):
            G_ref[:, sg * 1024:(sg + 1) * 1024] = lax.dot_general(
                dyv[:, sg * 128:(sg + 1) * 128], wc_ref[sg].astype(MXU_DTYPE), (((1,), (1,)), ((), ())),
                preferred_element_type=F32)
        rms_, rpc = _s5_tiles(are, aim, True)
        row = lax.broadcasted_iota(jnp.int32, (SUBLANES, 512), 0)

        def visit(j, blocks, acc):
            before = pl.multiple_of(jnp.maximum(j - 1, 0) * SUBLANES, SUBLANES)
            prow = X_ref[pl.ds(before, SUBLANES), :][SUBLANES - 1:SUBLANES]
            rows = pl.ds(pl.multiple_of(j * SUBLANES, SUBLANES), SUBLANES)
            are_acc, aim_acc = [], []
            for sg in range(NSG):
                lr = slice(sg * 1024, sg * 1024 + 512)
                li = slice(sg * 1024 + 512, (sg + 1) * 1024)
                ln = slice(sg * 512, (sg + 1) * 512)
                pre = jnp.where(j > 0, prow[:, lr], st_ref[0, 0, 0:1, ln])
                pim = jnp.where(j > 0, prow[:, li], st_ref[0, 0, 1:2, ln])
                xre = jnp.where(row == 0, pre, pltpu.roll(X_ref[rows, lr], 1, 0))
                xim = jnp.where(row == 0, pim, pltpu.roll(X_ref[rows, li], 1, 0))
                dre, dim = blocks[sg]
                are_acc.append(dre * xre + dim * xim)
                aim_acc.append(dim * xre - dre * xim)
            return acc[0] + jnp.concatenate(are_acc, axis=1), acc[1] + jnp.concatenate(aim_acc, axis=1)

        zero = jnp.zeros((SUBLANES, NST), F32)
        cre, cim, acc = _s5_scan(G_ref, R, rms_, rpc, car_ref[0:1], car_ref[1:2], True, visit, (zero, zero))
        car_ref[0:1] = cre
        car_ref[1:2] = cim
        dab_ref[0:1] += jnp.sum(acc[0], axis=0, keepdims=True)
        dab_ref[1:2] += jnp.sum(acc[1], axis=0, keepdims=True)
        uv = u_ref[...].astype(MXU_DTYPE)
        for sg in range(NSG):
            cs = slice(sg * 1024, (sg + 1) * 1024)
            us = slice(sg * 128, (sg + 1) * 128)
            gx = G_ref[:, cs].astype(MXU_DTYPE)
            dwb_ref[sg] += lax.dot_general(uv[:, us], gx, (((0,), (0,)), ((), ())), preferred_element_type=F32)
            dwc_ref[sg] += lax.dot_general(X_ref[:, cs].astype(MXU_DTYPE), dyv[:, us], (((0,), (0,)), ((), ())),
                                           preferred_element_type=F32)
            du_ssm = lax.dot_general(gx, wb_ref[sg].astype(MXU_DTYPE), (((1,), (1,)), ((), ())),
                                     preferred_element_type=F32)
            du_ref[:, us] = (du_ssm + du_direct[:, us]).astype(du_ref.dtype)

    rmap = lambda b, c: (b * nC + nC - 1 - c, 0)
    rows = pl.BlockSpec((R, SW), rmap)
    return pl.pallas_call(
        body, name="s5_bwd", grid=(Bl, nC),
        in_specs=[rows, rows, rows, pl.BlockSpec(d.shape, lambda b, c: (0, 0)),
                  pl.BlockSpec(wb.shape, lambda b, c: (0, 0, 0)), pl.BlockSpec(wc.shape, lambda b, c: (0, 0, 0)),
                  pl.BlockSpec(ab.shape, lambda b, c: (0, 0)),
                  pl.BlockSpec((1, 1, 2, NST), lambda b, c: (b, nC - 1 - c, 0, 0)),
                  pl.BlockSpec((R, 2 * NST), rmap)],
        out_specs=[rows, pl.BlockSpec(wb.shape, lambda b, c: (0, 0, 0)),
                   pl.BlockSpec(wc.shape, lambda b, c: (0, 0, 0)), pl.BlockSpec((2, NST), lambda b, c: (0, 0)),
                   pl.BlockSpec(d.shape, lambda b, c: (0, 0))],
        out_shape=[jax.ShapeDtypeStruct((Bl * S, SW), MXU_DTYPE), jax.ShapeDtypeStruct(wb.shape, F32),
                   jax.ShapeDtypeStruct(wc.shape, F32), jax.ShapeDtypeStruct((2, NST), F32),
                   jax.ShapeDtypeStruct(d.shape, F32)],
        scratch_shapes=[pltpu.VMEM((R, 2 * NST), F32), pltpu.VMEM((2, NST), F32)],
        compiler_params=_params(("arbitrary", "arbitrary")),
    )(u, y, do, d, wb, wc, ab, st, xs)


def _s5_disc_math(a_re, a_im, log_dt, b_re, b_im, expand):
    dt = jnp.exp(log_dt)
    z_re, z_im = a_re * dt, a_im * dt
    mag = jnp.exp(z_re)
    ab_re, ab_im = mag * jnp.cos(z_im), mag * jnp.sin(z_im)
    den = a_re * a_re + a_im * a_im
    q_re = ((ab_re - 1.0) * a_re + ab_im * a_im) / den
    q_im = (ab_im * a_re - (ab_re - 1.0) * a_im) / den
    qe_re = jnp.dot(q_re, expand, preferred_element_type=F32, precision=HIGHEST)
    qe_im = jnp.dot(q_im, expand, preferred_element_type=F32, precision=HIGHEST)
    return ab_re, ab_im, qe_re * b_re - qe_im * b_im, qe_re * b_im + qe_im * b_re


def _whole(shape):
    return pl.BlockSpec(shape, lambda nd=len(shape): (0,) * nd)


def _s5_disc(a_re, a_im, log_dt, b_re, b_im, expand):
    def body(a, b, c, d, e, f, o0, o1, o2, o3):
        res = _s5_disc_math(a[...], b[...], c[...], d[...], e[...], f[...])
        for o, v in zip((o0, o1, o2, o3), res):
            o[...] = v
    ins = (a_re, a_im, log_dt, b_re, b_im, expand)
    outs = [jax.ShapeDtypeStruct(a_re.shape, F32)] * 2 + [jax.ShapeDtypeStruct(b_re.shape, F32)] * 2
    return pl.pallas_call(body, name="s5_disc", in_specs=[_whole(x.shape) for x in ins],
                          out_specs=[_whole(o.shape) for o in outs], out_shape=outs)(*ins)


def _s5_disc_bwd(a_re, a_im, log_dt, b_re, b_im, expand, cts):
    def body(a, b, c, d, e, f, g0, g1, g2, g3, o0, o1, o2, o3, o4):
        fn = lambda *p: _s5_disc_math(*p, f[...])
        _, vjp = jax.vjp(fn, a[...], b[...], c[...], d[...], e[...])
        for o, v in zip((o0, o1, o2, o3, o4), vjp((g0[...], g1[...], g2[...], g3[...]))):
            o[...] = v
    ins = (a_re, a_im, log_dt, b_re, b_im, expand) + tuple(cts)
    outs = [jax.ShapeDtypeStruct(x.shape, F32) for x in (a_re, a_im, log_dt, b_re, b_im)]
    return pl.pallas_call(body, name="s5_disc_bwd", in_specs=[_whole(x.shape) for x in ins],
                          out_specs=[_whole(o.shape) for o in outs], out_shape=outs)(*ins)


def _ada_fwd(c_all, w_shard, b_shard):
    def body(c_ref, w_ref, b_ref, o_ref):
        cv = c_ref[...]
        o_ref[...] = _dotm(cv * _sigmoid(cv), w_ref[...]) + b_ref[...]
    n = w_shard.shape[1]
    return pl.pallas_call(
        body, name="ada_fwd", in_specs=[_whole(c_all.shape), _whole(w_shard.shape), _whole(b_shard.shape)],
        out_specs=_whole((c_all.shape[0], n)), out_shape=jax.ShapeDtypeStruct((c_all.shape[0], n), F32),
        compiler_params=_params(),
    )(c_all, w_shard, b_shard)


def _ada_bwd(c_all, dmod_cols, dmod_all):
    def body(c_ref, dc_ref, da_ref, gw_ref, gb_ref):
        cv = c_ref[...]
        gw_ref[...] = lax.dot_general((cv * _sigmoid(cv)).astype(MXU_DTYPE), dc_ref[...].astype(MXU_DTYPE),
                                      (((0,), (0,)), ((), ())), preferred_element_type=F32)
        gb_ref[...] = jnp.sum(da_ref[...], axis=0, keepdims=True)
    n = dmod_cols.shape[1]
    return pl.pallas_call(
        body, name="ada_bwd", in_specs=[_whole(c_all.shape), _whole(dmod_cols.shape), _whole(dmod_all.shape)],
        out_specs=[_whole((D, n)), _whole((1, dmod_all.shape[1]))],
        out_shape=[jax.ShapeDtypeStruct((D, n), F32), jax.ShapeDtypeStruct((1, dmod_all.shape[1]), F32)],
        compiler_params=_params(),
    )(c_all, dmod_cols, dmod_all)


def _rows_block(n_rows, cap=512):
    if n_rows <= cap:
        return n_rows
    for t in range(cap - cap % SUBLANES, 0, -SUBLANES):
        if n_rows % t == 0:
            return t
    return n_rows


def _adamw(w, g, m, v, name):
    rows, cols = w.shape
    tr = _rows_block(rows, max(SUBLANES, (1 << 19) // max(cols, 1) // SUBLANES * SUBLANES))

    def body(w_ref, g_ref, m_ref, v_ref, d_ref, nm_ref, nv_ref):
        gv = g_ref[...]
        nm = B1 * m_ref[...] + (1.0 - B1) * gv
        nv = B2 * v_ref[...] + (1.0 - B2) * (gv * gv)
        m_hat = nm / (1.0 - B1 ** STEP)
        v_hat = nv / (1.0 - B2 ** STEP)
        d_ref[...] = -LR * (m_hat / (jnp.sqrt(v_hat) + ADAM_EPS) + WD * w_ref[...])
        nm_ref[...] = nm
        nv_ref[...] = nv

    spec = pl.BlockSpec((tr, cols), lambda i: (i, 0))
    sd = jax.ShapeDtypeStruct((rows, cols), F32)
    return pl.pallas_call(body, name=name, grid=(rows // tr,), in_specs=[spec] * 4, out_specs=[spec] * 3,
                          out_shape=[sd] * 3, compiler_params=_params(("parallel",)))(w, g, m, v)


def _sum_slots(x, out_dtype, name):
    xs = x if isinstance(x, (list, tuple)) else [x]
    _, rows, cols = xs[0].shape
    tr = _rows_block(rows)

    def body(*refs):
        acc = None
        for x_ref in refs[:-1]:
            for j in range(x_ref.shape[0]):
                term = x_ref[j].astype(F32)
                acc = term if acc is None else acc + term
        refs[-1][...] = acc.astype(refs[-1].dtype)

    return pl.pallas_call(
        body, name=name, grid=(rows // tr,),
        in_specs=[pl.BlockSpec((z.shape[0], tr, cols), lambda i: (0, i, 0)) for z in xs],
        out_specs=pl.BlockSpec((tr, cols), lambda i: (i, 0)), out_shape=jax.ShapeDtypeStruct((rows, cols), out_dtype),
        compiler_params=_params(("parallel",)))(*xs)


PACK_COLS = 1024


def _pack_rows(parts, dtype, row_mult):
    flat = jnp.concatenate([p.reshape(-1).astype(dtype) for p in parts])
    per = PACK_COLS * row_mult
    n = -(-flat.shape[0] // per) * per
    flat = jnp.pad(flat, (0, n - flat.shape[0]))
    return flat.reshape(n // PACK_COLS, PACK_COLS)


def _unpack(flat, shapes):
    out, off = [], 0
    for s in shapes:
        n = math.prod(s)
        out.append(flat[off:off + n].reshape(s))
        off += n
    return out


BIG = (("w_in", (D, SHIFT + SW + 2 * D), 1), ("w_out_rwkv", (RW, D), 1), ("w_glu", (SW, 2 * D), 1),
       ("w_out", (D, D), 0), ("w_ffn_up", (D, 2 * DFF), 1), ("w_ffn_down", (DFF, D), 0))
BIG_SMALL = (("rwkv_w_up", (LW, RW), 1), ("rwkv_a_up", (LA, RW), 1), ("rwkv_g_up", (LG, RW), 1),
             ("ffn_conv_w", (3, 2 * DFF), 1))
BIG_LATE = BIG[4:]


def _shard_shape(shape, axis):
    return (shape[0] // 4, shape[1]) if axis == 0 else (shape[0], shape[1] // 4)


def _to_shards(g, axis):
    r, C = g.shape
    return g.reshape(4, r // 4, C) if axis == 0 else g.reshape(r, 4, C // 4).transpose(1, 0, 2)


def _from_shards(x, axis):
    _, r, C = x.shape
    return x.reshape(4 * r, C) if axis == 0 else x.transpose(1, 0, 2).reshape(r, 4 * C)


def kernel(x, c, w_ada, b_ada, norm1_g, w_in, mu_shift, rwkv_w0, rwkv_w_up, rwkv_a0, rwkv_a_up, rwkv_g_up, rwkv_k_k, rwkv_k_a, rwkv_r_k, rwkv_ln_g, rwkv_ln_b, w_out_rwkv, s5_a_re, s5_a_im, s5_log_dt, s5_b_re, s5_b_im, s5_c_re, s5_c_im, s5_d, w_glu, w_out, norm2_g, w_ffn_up, ffn_conv_w, ffn_conv_b, w_ffn_down, norm_f_g, loss_target, m_w_ada, m_b_ada, m_norm1_g, m_w_in, m_mu_shift, m_rwkv_w0, m_rwkv_w_up, m_rwkv_a0, m_rwkv_a_up, m_rwkv_g_up, m_rwkv_k_k, m_rwkv_k_a, m_rwkv_r_k, m_rwkv_ln_g, m_rwkv_ln_b, m_w_out_rwkv, m_s5_a_re, m_s5_a_im, m_s5_log_dt, m_s5_b_re, m_s5_b_im, m_s5_c_re, m_s5_c_im, m_s5_d, m_w_glu, m_w_out, m_norm2_g, m_w_ffn_up, m_ffn_conv_w, m_ffn_conv_b, m_w_ffn_down, m_norm_f_g, v_w_ada, v_b_ada, v_norm1_g, v_w_in, v_mu_shift, v_rwkv_w0, v_rwkv_w_up, v_rwkv_a0, v_rwkv_a_up, v_rwkv_g_up, v_rwkv_k_k, v_rwkv_k_a, v_rwkv_r_k, v_rwkv_ln_g, v_rwkv_ln_b, v_w_out_rwkv, v_s5_a_re, v_s5_a_im, v_s5_log_dt, v_s5_b_re, v_s5_b_im, v_s5_c_re, v_s5_c_im, v_s5_d, v_w_glu, v_w_out, v_norm2_g, v_w_ffn_up, v_ffn_conv_w, v_ffn_conv_b, v_w_ffn_down, v_norm_f_g):
    names = ["w_ada", "b_ada", "norm1_g", "w_in", "mu_shift", "rwkv_w0", "rwkv_w_up", "rwkv_a0", "rwkv_a_up",
             "rwkv_g_up", "rwkv_k_k", "rwkv_k_a", "rwkv_r_k", "rwkv_ln_g", "rwkv_ln_b", "w_out_rwkv", "s5_a_re",
             "s5_a_im", "s5_log_dt", "s5_b_re", "s5_b_im", "s5_c_re", "s5_c_im", "s5_d", "w_glu", "w_out", "norm2_g",
             "w_ffn_up", "ffn_conv_w", "ffn_conv_b", "w_ffn_down", "norm_f_g"]
    env = dict(locals())
    W = {n: env[n] for n in names}
    M = {n: env["m_" + n] for n in names}
    V = {n: env["v_" + n] for n in names}

    Bl, S, _ = x.shape
    T = Bl * S
    ix, iy, ic = lax.axis_index("x"), lax.axis_index("y"), lax.axis_index("c")
    chip = 2 * ix + iy
    dev = 2 * chip + ic
    rw = functools.partial(_rowwise, Bl=Bl, S=S)

    now = [b for b in BIG if b not in BIG_LATE]
    chip_arrs = [W[n][0].astype(MXU_DTYPE) for n, _, _ in now] + [W[n][0] for n, _, _ in BIG_SMALL[:3]]
    got_chip, got_dev = _gather_two_level(chip_arrs, [W["ffn_conv_w"][0], c], "gather_w")
    full = {n: _from_shards(g, axis) for (n, _, axis), g in zip(tuple(now) + BIG_SMALL[:3], got_chip)}
    full["ffn_conv_w"] = _from_shards(got_dev[0][:, 0], 1)
    c_all = got_dev[1].reshape(8 * Bl, D)
    w_p, w_u, w_g = full["w_in"][:, :SHIFT], full["w_in"][:, SHIFT:SHIFT + SW], full["w_in"][:, SHIFT + SW:]
    zeros_l = jnp.zeros((LW, RW), F32)
    w_up_p = jnp.concatenate([full["rwkv_w_up"], zeros_l], axis=0)
    a_up_p = jnp.concatenate([zeros_l, full["rwkv_a_up"]], axis=0)
    g_up = full["rwkv_g_up"]
    conv_w = full["ffn_conv_w"]
    conv_wg, conv_wu = conv_w[:, :DFF], conv_w[:, DFF:]
    conv_bg, conv_bu = ffn_conv_b[:, :DFF], ffn_conv_b[:, DFF:]
    hm = jnp.kron(jnp.eye(NH, dtype=F32), jnp.ones((HD, HD), F32))

    ncol = 6 * D // 4
    b_ada_cols = lax.dynamic_slice_in_dim(b_ada, chip * ncol, ncol, 1)
    mod_part = _ada_fwd(c_all, w_ada[0], b_ada_cols)
    mod4 = _gather_two_level([], [mod_part], "gather_mod")[1][0][:, 0]
    mod4, late = lax.optimization_barrier((mod4, [W[n][0].astype(MXU_DTYPE) for n, _, _ in BIG_LATE]))
    late_moves = [(i, i, lambda ref, me, peer: ref, lambda ref, me, k: ref.at[_chip_of(me)]) for i in range(len(late))]
    late_start = _send_start("gather_ffn_start", CHIP_FLIPS, late,
                             [jax.ShapeDtypeStruct((4,) + z.shape, z.dtype) for z in late], late_moves)
    norm1_g = norm1_g + late_start["token"]
    mod = lax.dynamic_slice_in_dim(mod4, dev * Bl, Bl, 1).transpose(1, 0, 2).reshape(Bl, 1, 6 * D)
    SH1, SC1, GT1, SH2, SC2, GT2 = range(6)

    x2d = x.reshape(T, D)
    tgt = loss_target.reshape(T, D)

    (h1,) = rw("norm1", lambda xv, sc, sh, g: _norm_mod(xv, g, sc, sh), R=256, tiled=[(x2d, D, 0)],
               batch=[(mod, D, SC1), (mod, D, SH1)], full=[norm1_g], out_tiled=[(D, MXU_DTYPE)])
    p = _mm([h1], [w_p], F32, "proj_p")
    u = _mm([h1], [w_u], F32, "proj_u")
    gates = _mm([h1], [w_g], F32, "proj_g")

    prep_params = [rwkv_w0, w_up_p, rwkv_a0, a_up_p, g_up, rwkv_k_k, rwkv_k_a, hm]

    def prep_fwd(pv, ph, mu, *pp):
        ps = pv + (_shift_down(pv, ph, 1) - pv) * mu
        return _rwkv_prep(*_split_ps(ps), *pp)

    r_, w_, k_, v_, a_, b_, g_ = rw("rwkv_prep", prep_fwd, R=256, tiled=[(p, SHIFT, 0)], prev=[(p, SHIFT, 0)],
                                    full=[mu_shift] + prep_params, out_tiled=[(RW, F32)] * 7)
    y_wkv, ck = _wkv_fwd(r_, w_, k_, v_, a_, b_, Bl, S)
    r_k_row = rwkv_r_k.reshape(1, RW)
    post_params = [rwkv_ln_g, rwkv_ln_b, r_k_row, hm]
    (o_rwkv,) = rw("rwkv_post", _rwkv_post, R=256,
                   tiled=[(y_wkv, RW, 0), (r_, RW, 0), (k_, RW, 0), (v_, RW, 0), (g_, RW, 0)],
                   full=post_params, out_tiled=[(RW, MXU_DTYPE)])
    y_a = _mm([o_rwkv], [full["w_out_rwkv"]], F32, "out_rwkv")

    expand = jnp.kron(jnp.eye(SP, dtype=F32), jnp.ones((1, SGC), F32))
    s5_in = (s5_a_re[0], s5_a_im[0], s5_log_dt[0].reshape(NG, 1), s5_b_re[0].reshape(NG, SP * SGC),
             s5_b_im[0].reshape(NG, SP * SGC), expand)
    ab_re, ab_im, bb_re, bb_im = _s5_disc(*s5_in)
    eye8 = jnp.eye(8, dtype=F32)

    def blockdiag_in(bb):
        t = bb.reshape(NSG, 8, SP, SGC)
        return jnp.einsum("ab,sapc->sacbp", eye8, t).reshape(NSG, 128, 512)

    def blockdiag_out(cc):
        t = cc.reshape(NSG, 8, SGC, SP)
        return jnp.einsum("ab,sacp->sapbc", eye8, t).reshape(NSG, 512, 128)

    wb = jnp.concatenate([blockdiag_in(bb_re), blockdiag_in(bb_im)], axis=2).astype(MXU_DTYPE)
    wc = jnp.concatenate([blockdiag_out(s5_c_re[0]), -blockdiag_out(s5_c_im[0])], axis=1).astype(MXU_DTYPE)
    ab = jnp.stack([ab_re.reshape(NST), ab_im.reshape(NST)])
    y_ssm, s5_st, s5_x, s5o = _s5_fwd(u, wb, wc, ab, s5_d, Bl, S)
    z = _mm([s5o], [full["w_glu"]], F32, "glu")
    mix_tiled = [(gates, D, 0), (gates, D, 1), (y_a, D, 0), (z, D, 0), (z, D, 1)]
    (mixed_in,) = rw("mix", _mix, R=256, tiled=mix_tiled, out_tiled=[(D, MXU_DTYPE)])
    mixed = _mm([mixed_in], [full["w_out"]], F32, "out_proj")

    def norm2_fwd(xv, mx, gt, sc, sh, g):
        x1 = xv + gt * mx
        return x1, _norm_mod(x1, g, sc, sh)

    x1, h2 = rw("norm2", norm2_fwd, R=256, tiled=[(x2d, D, 0), (mixed, D, 0)],
                batch=[(mod, D, GT1), (mod, D, SC2), (mod, D, SH2)], full=[norm2_g],
                out_tiled=[(D, F32), (D, MXU_DTYPE)])
    late_own, late_got = _send_wait("gather_ffn_wait", CHIP_FLIPS, late_start, late_moves, h2)
    for (n, _, axis), own, got in zip(BIG_LATE, late_own, late_got):
        full[n] = _from_shards(lax.dynamic_update_slice(got, own[None], (chip, 0, 0)), axis)
    up = _mm([h2], [full["w_ffn_up"]], MXU_DTYPE, "ffn_up")
    conv_tiled = [(up, 0), (up, 1)]
    conv_full = [conv_wg, conv_wu, conv_bg, conv_bu]
    cw = functools.partial(_colwise, Bl=Bl, S=S, R=128, W=DFF, strip=LANES)

    def act_fwd(*a):
        return ((_silu_gate(*_conv_act(*a)),),)

    (act,) = cw("ffn_act", act_fwd, tiled=conv_tiled, prev=conv_tiled, full=conv_full, out_tiled=[(1, MXU_DTYPE)])
    ffn = _mm([act], [full["w_ffn_down"]], F32, "ffn_down")

    def head(x1v, fv, tv, gt, g):
        x2 = x1v + gt * fv
        y, vjp = jax.vjp(_rms, x2, g)
        e = y - tv
        dx2, dg = vjp(e * (1.0 / D))
        loss = jnp.sum(e * e, keepdims=True) * jnp.ones((1, LANES), F32)
        return dx2, dx2 * gt, jnp.sum(dx2 * fv, axis=0, keepdims=True), dg.reshape(1, D), loss

    dx2, d_ffn, d_gt2, g_norm_f, loss_acc = rw(
        "head", head, R=256, tiled=[(x1, D, 0), (ffn, D, 0), (tgt, D, 0)], batch=[(mod, D, GT2)],
        full=[norm_f_g.reshape(1, D)], out_tiled=[(D, F32), (D, MXU_DTYPE)], out_batch=[D],
        out_acc=[(1, D), (1, LANES)])
    loss = lax.psum(0.5 / D * loss_acc[0, 0], ("x", "y", "c"))

    d_act = _mm([d_ffn], [full["w_ffn_down"]], F32, "d_act", bt=True)
    g_w_ffn_down = _mm_tn(act, d_ffn, "g_ffn_down")

    def act_bwd(ug, uu, dact, hg, hu, wg, wu, bg, bu):
        (gate, taps_g), (upv, taps_u) = _conv3(ug, hg, wg, bg), _conv3(uu, hu, wu, bu)
        _, vjp_s = jax.vjp(_silu_gate, gate, upv)
        d_gate, d_upv = vjp_s(dact)
        def taps(dh, shifted):
            return [jnp.sum(dh * s, axis=0, keepdims=True) for s in shifted] + [jnp.sum(dh, axis=0, keepdims=True)]
        return ((d_gate,), (d_upv,), *taps(d_gate, taps_g), *taps(d_upv, taps_u))

    dh_g, dh_u, *tapg = cw("ffn_act_bwd", act_bwd, tiled=conv_tiled + [(d_act, 0)], prev=conv_tiled, full=conv_full,
                           out_tiled=[(1, MXU_DTYPE), (1, MXU_DTYPE)], n_acc=8)
    g_cw_g, g_cb_g = jnp.concatenate(tapg[0:3], axis=0), tapg[3]
    g_cw_u, g_cb_u = jnp.concatenate(tapg[4:7], axis=0), tapg[7]

    def conv_t(dg, du_, ng, nu, wg, wu):
        dg, du_, ng, nu = (z.astype(F32) for z in (dg, du_, ng, nu))

        def ct(d, n, w):
            return w[2:3] * d + w[1:2] * _shift_up(d, n, 1) + w[0:1] * _shift_up(d, n, 2)
        return ((ct(dg, ng, wg), ct(du_, nu, wu)),)

    (d_up,) = cw("conv_bwd", conv_t, tiled=[(dh_g, 0), (dh_u, 0)], nxt=[(dh_g, 0), (dh_u, 0)],
                 full=[conv_wg, conv_wu], out_tiled=[(2, MXU_DTYPE)])
    d_h2 = _mm([d_up], [full["w_ffn_up"]], F32, "d_h2", bt=True)
    g_w_ffn_up = _mm_tn(h2, d_up, "g_ffn_up")

    sds = jax.ShapeDtypeStruct
    reduce_src = lambda r: (lambda ref, me, peer: ref.at[_chip_of(peer), _half(r, peer[2])])

    def reduced_halves(tag, started, moves, after):
        gsh_own, got = _send_wait("rs_%s_wait" % tag, ALL_FLIPS, started, moves, after)
        halves = []
        for i, (g, gt) in enumerate(zip(gsh_own, got)):
            h = g.shape[1] // 2
            own = lax.dynamic_slice(g, (chip, ic * h, 0), (1, h, g.shape[2]))
            halves.append(_sum_slots([own, gt], F32, "rs_%s_sum%d" % (tag, i)))
        return halves

    def share_start(tag, halves):
        moves = [(i, i, lambda ref, me, peer: ref, lambda ref, me, k: ref) for i in range(len(halves))]
        return _send_start("share_%s_start" % tag, PAIR_FLIPS, halves, [sds(g.shape, F32) for g in halves], moves), moves

    def share_finish(tag, started, moves, after, group, grads):
        mine_h, got_h = _send_wait("share_%s_wait" % tag, PAIR_FLIPS, started, moves, after)
        for (n, _, _), mh, gh in zip(group, mine_h, got_h):
            grads[n] = jnp.concatenate([jnp.where(ic == 0, mh, gh), jnp.where(ic == 0, gh, mh)], axis=0)[None]

    gsh_late = [_to_shards(g, ax).astype(MXU_DTYPE) for g, (_, _, ax) in zip((g_w_ffn_up, g_w_ffn_down), BIG_LATE)]
    rsl_moves = [(i, i, reduce_src(g.shape[1]), lambda ref, me, k: ref.at[k]) for i, g in enumerate(gsh_late)]
    rsl = _send_start("rs_ffn_start", ALL_FLIPS, gsh_late,
                      [sds((len(ALL_FLIPS), g.shape[1] // 2, g.shape[2]), MXU_DTYPE) for g in gsh_late], rsl_moves)
    norm2_g = norm2_g + rsl["token"]

    def norm2_bwd(x1v, dh2, dx2v, mx, gt, sc, sh, g):
        _, vjp = jax.vjp(_norm_mod, x1v, g, sc, sh)
        dxn, dg, dsc, dsh = vjp(dh2)
        dx1 = dx2v + dxn
        return dx1, dx1 * gt, jnp.sum(dx1 * mx, axis=0, keepdims=True), dsc, dsh, dg

    dx1, d_mixed, d_gt1, d_sc2, d_sh2, g_norm2 = rw(
        "norm2_bwd", norm2_bwd, R=256, tiled=[(x1, D, 0), (d_h2, D, 0), (dx2, D, 0), (mixed, D, 0)],
        batch=[(mod, D, GT1), (mod, D, SC2), (mod, D, SH2)], full=[norm2_g],
        out_tiled=[(D, F32), (D, MXU_DTYPE)], out_batch=[D, D, D], out_acc=[(1, D)])

    d_mixed_in = _mm([d_mixed], [full["w_out"]], F32, "d_mixed_in", bt=True)
    g_w_out = _mm_tn(mixed_in, d_mixed, "g_w_out")

    def mix_bwd(ga, gb, ya, za, zb, dm):
        _, vjp = jax.vjp(_mix, ga, gb, ya, za, zb)
        dga, dgb, dya, dza, dzb = vjp(dm)
        return jnp.concatenate([dga, dgb], axis=1), dya, jnp.concatenate([dza, dzb], axis=1)

    d_gates, d_ya, d_z = rw("mix_bwd", mix_bwd, R=256, tiled=mix_tiled + [(d_mixed_in, D, 0)],
                            out_tiled=[(2 * D, MXU_DTYPE), (D, MXU_DTYPE), (2 * D, MXU_DTYPE)])
    d_o_rwkv = _mm([d_ya], [full["w_out_rwkv"]], F32, "d_o_rwkv", bt=True)
    g_w_out_rwkv = _mm_tn(o_rwkv, d_ya, "g_out_rwkv")
    d_s5o = _mm([d_z], [full["w_glu"]], F32, "d_s5o", bt=True)
    g_w_glu = _mm_tn(s5o, d_z, "g_glu")

    d_u, d_wb, d_wc, d_ab, g_s5_d = _s5_bwd(u, y_ssm, d_s5o, s5_d, wb, wc, ab, s5_st, s5_x, Bl, S)

    def diag_in(dw):
        t = dw.reshape(NSG, 8, SGC, 8, SP)
        return jnp.einsum("ab,sacbp->sapc", eye8, t).reshape(NG, SP * SGC)

    def diag_out(dw):
        t = dw.reshape(NSG, 8, SP, 8, SGC)
        return jnp.einsum("ab,sapbc->sacp", eye8, t).reshape(NG, SGC, SP)

    g_s5_c_re = diag_out(d_wc[:, :512])
    g_s5_c_im = -diag_out(d_wc[:, 512:])
    disc_cts = (d_ab[0].reshape(NG, SP), d_ab[1].reshape(NG, SP), diag_in(d_wb[:, :, :512]), diag_in(d_wb[:, :, 512:]))
    g_a_re, g_a_im, g_log_dt, g_b_re, g_b_im = _s5_disc_bwd(*s5_in, disc_cts)

    def post_bwd(yv, rv, kv, vv, gv, do, *pp):
        _, vjp = jax.vjp(lambda *a: _rwkv_post(*a, pp[3]), yv, rv, kv, vv, gv, *pp[:3])
        return vjp(do)

    dy_wkv, dr_b, dk_b, dv_b, dg_, g_ln_g, g_ln_b, g_r_k = rw(
        "rwkv_post_bwd", post_bwd, R=256,
        tiled=[(y_wkv, RW, 0), (r_, RW, 0), (k_, RW, 0), (v_, RW, 0), (g_, RW, 0), (d_o_rwkv, RW, 0)],
        full=post_params, out_tiled=[(RW, F32)] * 5, out_acc=[(1, RW)] * 3)
    dr3, dw3, dk3, dv3, da3, db3 = _wkv_bwd(r_, w_, k_, v_, a_, b_, dy_wkv, ck, Bl, S)

    shl, shl_moves = share_start("ffn", reduced_halves("ffn", rsl, rsl_moves, dr3))
    mu_shift = mu_shift + shl["token"]

    def prep_bwd(pv, dr1, dr2, dwv, dk1, dk2, dv1, dv2, dav, dbv, dgv, ph, mu, *pp):
        prev = _shift_down(pv, ph, 1)
        ps = pv + (prev - pv) * mu
        _, vjp = jax.vjp(lambda *q: _rwkv_prep(*q, pp[7]), *_split_ps(ps), *pp[:7])
        grads = vjp((dr1 + dr2, dwv, dk1 + dk2, dv1 + dv2, dav, dbv, dgv))
        dps = jnp.concatenate(grads[:5], axis=1)
        return (dps,) + tuple(grads[5:]) + (jnp.sum(dps * (prev - pv), axis=0, keepdims=True),)

    prep_outs = rw(
        "rwkv_prep_bwd", prep_bwd, R=256,
        tiled=[(p, SHIFT, 0), (dr3, RW, 0), (dr_b, RW, 0), (dw3, RW, 0), (dk3, RW, 0), (dk_b, RW, 0),
               (dv3, RW, 0), (dv_b, RW, 0), (da3, RW, 0), (db3, RW, 0), (dg_, RW, 0)],
        prev=[(p, SHIFT, 0)], full=[mu_shift] + prep_params,
        out_tiled=[(SHIFT, F32)],
        out_acc=[(1, RW), (LW + LA, RW), (1, RW), (LW + LA, RW), (LG, RW), (1, RW), (1, RW), (1, SHIFT)])
    d_ps, g_w0, g_w_up_p, g_a0, g_a_up_p, g_g_up, g_k_k, g_k_a, g_mu = prep_outs

    small = {"mu_shift": g_mu, "rwkv_w0": g_w0, "rwkv_a0": g_a0, "rwkv_k_k": g_k_k,
             "rwkv_k_a": g_k_a, "rwkv_r_k": g_r_k, "rwkv_ln_g": g_ln_g, "rwkv_ln_b": g_ln_b, "s5_a_re": g_a_re,
             "s5_a_im": g_a_im, "s5_log_dt": g_log_dt, "s5_b_re": g_b_re, "s5_b_im": g_b_im, "s5_c_re": g_s5_c_re,
             "s5_c_im": g_s5_c_im, "s5_d": g_s5_d, "norm2_g": g_norm2,
             "ffn_conv_b": jnp.concatenate([g_cb_g, g_cb_u], axis=1), "norm_f_g": g_norm_f}
    small_names = list(small)
    g_conv_w = jnp.concatenate([g_cw_g, g_cw_u], axis=1)
    shard_small = {"rwkv_w_up": g_w_up_p[:LW], "rwkv_a_up": g_a_up_p[LW:], "rwkv_g_up": g_g_up, "ffn_conv_w": g_conv_w}
    parts = [small[n] for n in small_names] + [_to_shards(shard_small[n], ax) for n, _, ax in BIG_SMALL]
    spack = _pack_rows(parts, F32, SUBLANES)
    sm_moves = [(0, 0, lambda ref, me, peer: ref, lambda ref, me, k: ref.at[2 * _chip_of(me) + me[2]])]
    sm = _send_start("gsmall_start", ALL_FLIPS, [spack], [sds((8,) + spack.shape, F32)], sm_moves)
    mu_shift = mu_shift + sm["token"]

    def shift_bwd(dps, nx, mu):
        return dps * (1.0 - mu) + _shift_up(dps * mu, nx * mu, 1)

    (d_p,) = rw("shift_bwd", shift_bwd, R=256, tiled=[(d_ps, SHIFT, 0)], nxt=[(d_ps, SHIFT, 0)], full=[mu_shift],
                out_tiled=[(SHIFT, MXU_DTYPE)])
    g_w_in = jnp.concatenate([_mm_tn(h1, d_p, "g_w_p"), _mm_tn(h1, d_u, "g_w_u"), _mm_tn(h1, d_gates, "g_w_g")], axis=1)
    big_g = {"w_in": g_w_in, "w_out_rwkv": g_w_out_rwkv, "w_glu": g_w_glu, "w_out": g_w_out}
    gsh_now = [_to_shards(big_g[n], ax).astype(MXU_DTYPE) for n, _, ax in now]
    rsn_moves = [(i, i, reduce_src(g.shape[1]), lambda ref, me, k: ref.at[k]) for i, g in enumerate(gsh_now)]
    rsn = _send_start("rs_mix_start", ALL_FLIPS, gsh_now,
                      [sds((len(ALL_FLIPS), g.shape[1] // 2, g.shape[2]), MXU_DTYPE) for g in gsh_now], rsn_moves)
    norm1_g = norm1_g + rsn["token"]
    d_h1 = _mm([d_p, d_u, d_gates], [w_p, w_u, w_g], F32, "d_h1", bt=True)

    def norm1_bwd(xv, dh1, dx1v, sc, sh, g):
        _, vjp = jax.vjp(_norm_mod, xv, g, sc, sh)
        dxn, dg, dsc, dsh = vjp(dh1)
        return dx1v + dxn, dsc, dsh, dg

    grad_x, d_sc1, d_sh1, g_norm1 = rw(
        "norm1_bwd", norm1_bwd, R=256, tiled=[(x2d, D, 0), (d_h1, D, 0), (dx1, D, 0)],
        batch=[(mod, D, SC1), (mod, D, SH1)], full=[norm1_g], out_tiled=[(D, F32)], out_batch=[D, D], out_acc=[(1, D)])

    dmod = jnp.concatenate([d_sh1, d_sc1, d_gt1, d_sh2, d_sc2, d_gt2], axis=2).reshape(Bl, 6 * D)
    last_all = _gather_two_level([], [dmod, g_norm1], "gather_dmod")[1]
    dmod_all = last_all[0].reshape(8 * Bl, 6 * D)
    shn, shn_moves = share_start("mix", reduced_halves("mix", rsn, rsn_moves, dmod_all))
    dmod_cols = lax.dynamic_slice_in_dim(dmod_all, chip * ncol, ncol, 1)
    g_w_ada, g_b_ada = _ada_bwd(c_all, dmod_cols, dmod_all)

    grads = {"norm1_g": _sum_slots(last_all[1].reshape(8, 1, D), F32, "sum_norm1")}
    sm_own, sm_got = _send_wait("gsmall_wait", ALL_FLIPS, sm, sm_moves, g_b_ada)
    s_all = lax.dynamic_update_slice(sm_got[0], sm_own[0][None], (dev, 0, 0))
    s_sum = _sum_slots(s_all, F32, "sum_gsmall").reshape(-1)
    off = 0
    for n in small_names:
        grads[n] = s_sum[off:off + W[n].size].reshape(W[n].shape)
        off += W[n].size
    for n, shape, axis in BIG_SMALL:
        ss = _shard_shape(shape, axis)
        k4 = 4 * math.prod(ss)
        sh4 = s_sum[off:off + k4].reshape(4, math.prod(ss))
        grads[n] = lax.dynamic_index_in_dim(sh4, chip, 0, keepdims=False).reshape((1,) + ss)
        off += k4

    share_finish("ffn", shl, shl_moves, s_sum, BIG_LATE, grads)
    share_finish("mix", shn, shn_moves, grads[BIG_LATE[0][0]], now, grads)
    grads["w_ada"] = g_w_ada[None]
    grads["b_ada"] = g_b_ada

    delta, new_m, new_v = {}, {}, {}
    to2 = lambda z: z.reshape(-1, z.shape[-1])
    for n in ["w_ada"] + [b[0] for b in BIG]:
        d_, m_, v2_ = _adamw(to2(W[n]), to2(grads[n]), to2(M[n]), to2(V[n]), "adamw_" + n)
        delta[n], new_m[n], new_v[n] = (z.reshape(W[n].shape) for z in (d_, m_, v2_))
    rest = [n for n in names if n not in delta]
    packs = [_pack_rows([src[n] for n in rest], F32, SUBLANES) for src in (W, grads, M, V)]
    d_, m_, v2_ = _adamw(*packs, "adamw_small")
    shapes = [W[n].shape for n in rest]
    for dst, z in ((delta, d_), (new_m, m_), (new_v, v2_)):
        for n, val in zip(rest, _unpack(z.reshape(-1), shapes)):
            dst[n] = val

    return (loss, grad_x.reshape(Bl, S, D), *[grads[n] for n in names], *[delta[n] for n in names],
            *[new_m[n] for n in names], *[new_v[n] for n in names])
```

```python
import functools
import math

import jax
import jax.numpy as jnp
from jax import lax
from jax.experimental import pallas as pl
from jax.experimental.pallas import tpu as pltpu

F32 = jnp.float32
BF16 = jnp.bfloat16
MXU_DTYPE = jnp.bfloat16
MESH_IDS = pl.DeviceIdType.MESH
HIGHEST = lax.Precision.HIGHEST

D = 1024
RW, NH, HD = 512, 8, 64
LW, LA, LG = 64, 64, 128
SW, SGC, NG, SP = 512, 16, 32, 64
NSG = 4
SHIFT = 3 * RW + LW + LA + LG
DFF = 2816
RMS_EPS, GN_EPS, L2_EPS = 1e-6, 64e-5, 1e-12
LR, B1, B2, ADAM_EPS, WD, STEP = 0.001, 0.9, 0.999, 1e-8, 0.01, 10
DECAY_SCALE = math.exp(-0.5)
GELU_C = math.sqrt(2.0 / math.pi)

VMEM_LIMIT = 52 * 1024 * 1024
SUBLANES, LANES = 8, 128
HALO = 16


def _pick(n, cap):
    if n <= cap:
        return n
    best = None
    for t in range(LANES, cap + 1, LANES):
        if n % t == 0:
            best = t
    assert best is not None, (n, cap)
    return best


def _params(sem=None, vmem=VMEM_LIMIT):
    return pltpu.CompilerParams(dimension_semantics=sem, vmem_limit_bytes=vmem)


def _chip_of(p):
    return 2 * p[0] + p[1]


def _me():
    return (lax.axis_index("x"), lax.axis_index("y"), lax.axis_index("c"))


def _half(rows, core):
    h = rows // 2
    return pl.ds(pl.multiple_of(core * h, 16 if h % 16 == 0 else SUBLANES), h)


_HBM =pl.BlockSpec(memory_space=pltpu.HBM)
_SEM = pl.BlockSpec(memory_space=pltpu.SEMAPHORE)
_DATAFLOW = pltpu.SideEffectType.DATAFLOW_SIDE_EFFECTING


def _split_copies(flips, moves, src_refs, land_refs, send_sems, recv_sems):
    me = _me()
    nf = len(flips)
    out = []
    for m, (si, li, src_sel, dst_sel) in enumerate(moves):
        for k, f in enumerate(flips):
            peer = tuple(1 - v if b else v for v, b in zip(me, f))
            out.append(pltpu.make_async_remote_copy(
                src_ref=src_sel(src_refs[si], me, peer), dst_ref=dst_sel(land_refs[li], me, k),
                send_sem=send_sems.at[m * nf + k], recv_sem=recv_sems.at[m * nf + k],
                device_id=peer, device_id_type=MESH_IDS))
    return out


def _send_start(name, flips, srcs, land_shapes, moves):
    ns, nl = len(srcs), len(land_shapes)
    n = len(moves) * len(flips)

    def body(*refs):
        for cp in _split_copies(flips, moves, refs[:ns], refs[ns:ns + nl], refs[ns + nl], refs[ns + nl + 1]):
            cp.start()
        refs[-1][...] = jnp.zeros(refs[-1].shape, F32)

    hbm = lambda z: pltpu.with_memory_space_constraint(z, pltpu.HBM)
    lands = [lax.empty(s.shape, s.dtype) for s in land_shapes]
    res = pl.pallas_call(
        body, name=name,
        out_shape=(pltpu.SemaphoreType.DMA((n,)), pltpu.SemaphoreType.DMA((n,)),
                   *[pltpu.HBM(z.shape, z.dtype) for z in srcs], *[pltpu.HBM(s.shape, s.dtype) for s in land_shapes],
                   jax.ShapeDtypeStruct((SUBLANES, LANES), F32)),
        in_specs=[_HBM] * (ns + nl),
        out_specs=(_SEM, _SEM, *[_HBM] * (ns + nl), pl.BlockSpec(memory_space=pltpu.VMEM)),
        input_output_aliases={i: 2 + i for i in range(ns + nl)},
        compiler_params=pltpu.CompilerParams(has_side_effects=_DATAFLOW),
    )(*[hbm(z) for z in srcs], *[hbm(z) for z in lands])
    return {"sems": res[:2], "srcs": list(res[2:2 + ns]), "lands": list(res[2 + ns:2 + ns + nl]), "token": res[-1][0, 0]}


def _send_wait(name, flips, started, moves, after):
    srcs, lands = started["srcs"], started["lands"]
    ns, nl = len(srcs), len(lands)

    def body(*refs):
        for cp in _split_copies(flips, moves, refs[:ns], refs[ns:ns + nl], refs[ns + nl], refs[ns + nl + 1]):
            cp.wait_send()
            cp.wait_recv()

    res = pl.pallas_call(
        body, name=name, out_shape=[pltpu.HBM(z.shape, z.dtype) for z in srcs + lands],
        in_specs=[_HBM] * (ns + nl) + [_SEM, _SEM, pl.BlockSpec(memory_space=pl.ANY)],
        out_specs=[_HBM] * (ns + nl), input_output_aliases={i: i for i in range(ns + nl)},
        compiler_params=pltpu.CompilerParams(has_side_effects=_DATAFLOW),
    )(*srcs, *lands, *started["sems"], after)
    return list(res[:ns]), list(res[ns:])


CHIP_FLIPS = ((1, 0, 0), (0, 1, 0), (1, 1, 0))
PAIR_FLIPS = ((0, 0, 1),)
ALL_FLIPS = CHIP_FLIPS + ((1, 0, 1), (0, 1, 1), (1, 1, 1)) + PAIR_FLIPS


def _gather_two_level(chip_arrs, dev_arrs, name):
    arrs = list(chip_arrs) + list(dev_arrs)
    n, nchip = len(arrs), len(chip_arrs)
    NS = 7

    def body(*refs):
        srcs, outs = refs[:n], refs[n:2 * n]
        send_sems, recv_sems, loc_sems = refs[2 * n:]
        x, y, c = _me()
        sib = (x, y, 1 - c)
        chips = [(1 - x, y), (x, 1 - y), (1 - x, 1 - y)]
        mine = 2 * x + y
        ids = [2 * cx + cy for cx, cy in chips]

        def part(i, slot, core):
            if i < nchip:
                return outs[i].at[slot, _half(arrs[i].shape[0], core)]
            return outs[i].at[slot, core]

        def rcopy(i, k, src, dst, to):
            return pltpu.make_async_remote_copy(src_ref=src, dst_ref=dst, send_sem=send_sems.at[i * NS + k],
                                                recv_sem=recv_sems.at[i * NS + k], device_id=to, device_id_type=MESH_IDS)

        started, locs = [], []
        for i in range(n):
            own = srcs[i].at[_half(arrs[i].shape[0], c)] if i < nchip else srcs[i]
            loc = pltpu.make_async_copy(srcs[i], outs[i].at[mine] if i < nchip else outs[i].at[mine, c], loc_sems.at[i])
            loc.start()
            locs.append(loc)
            for f, chip in enumerate(chips):
                cp = rcopy(i, f, own, part(i, mine, c), (*chip, c))
                cp.start()
                started.append(cp)
            if i >= nchip:
                cp = rcopy(i, 6, own, part(i, mine, c), sib)
                cp.start()
                started.append(cp)
        for i in range(n):
            for f in range(3):
                land = part(i, ids[f], c)
                rcopy(i, f, land, land, sib).wait_recv()
                fw = rcopy(i, 3 + f, land, land, sib)
                fw.start()
                started.append(fw)
        for i in range(n):
            for f in range(3):
                land = part(i, ids[f], 1 - c)
                rcopy(i, 3 + f, land, land, sib).wait_recv()
            if i >= nchip:
                land = part(i, mine, 1 - c)
                rcopy(i, 6, land, land, sib).wait_recv()
        for cp in started:
            cp.wait_send()
        for loc in locs:
            loc.wait()

    outs = [jax.ShapeDtypeStruct((4,) + a.shape, a.dtype) for a in chip_arrs]
    outs += [jax.ShapeDtypeStruct((4, 2) + a.shape, a.dtype) for a in dev_arrs]
    res = pl.pallas_call(
        body, name=name, out_shape=outs,
        in_specs=[pl.BlockSpec(memory_space=pl.ANY)] * n, out_specs=[pl.BlockSpec(memory_space=pl.ANY)] * n,
        scratch_shapes=[pltpu.SemaphoreType.DMA((n * NS,)), pltpu.SemaphoreType.DMA((n * NS,)),
                        pltpu.SemaphoreType.DMA((n,))],
    )(*arrs)
    return res[:nchip], res[nchip:]


def _mm(As, Bs, out_dtype, name, tm=512, cap=1408, bt=False):
    n = len(As)
    M, N = As[0].shape[0], Bs[0].shape[0 if bt else 1]
    if sum(a.shape[1] for a in As) <= 1024:
        tm = 2 * tm
    tm = min(tm, M)
    tn = _pick(N, cap)
    dims = (((1,), (1,)), ((), ())) if bt else (((1,), (0,)), ((), ()))

    def body(*refs):
        o = refs[2 * n]
        acc = None
        for a, b in zip(refs[:n], refs[n:2 * n]):
            d = lax.dot_general(a[...].astype(MXU_DTYPE), b[...].astype(MXU_DTYPE), dims, preferred_element_type=F32)
            acc = d if acc is None else acc + d
        o[...] = acc.astype(o.dtype)

    in_specs = [pl.BlockSpec((tm, a.shape[1]), lambda i, j: (i, 0)) for a in As]
    if bt:
        in_specs += [pl.BlockSpec((tn, b.shape[1]), lambda i, j: (j, 0)) for b in Bs]
    else:
        in_specs += [pl.BlockSpec((b.shape[0], tn), lambda i, j: (0, j)) for b in Bs]
    return pl.pallas_call(
        body, name=name, grid=(M // tm, N // tn), in_specs=in_specs,
        out_specs=pl.BlockSpec((tm, tn), lambda i, j: (i, j)),
        out_shape=jax.ShapeDtypeStruct((M, N), out_dtype),
        compiler_params=_params(("parallel", "parallel")),
    )(*As, *Bs)


def _mm_tn(A, G, name, tt=1024, cap=1408):
    T, Ka = A.shape
    N = G.shape[1]
    tt = min(tt, T)
    tk = _pick(Ka, cap)
    tn = _pick(N, cap)

    def body(a, g, o):
        @pl.when(pl.program_id(2) == 0)
        def _():
            o[...] = jnp.zeros(o.shape, F32)
        o[...] += lax.dot_general(a[...].astype(MXU_DTYPE), g[...].astype(MXU_DTYPE),
                                  (((0,), (0,)), ((), ())), preferred_element_type=F32)

    return pl.pallas_call(
        body, name=name, grid=(Ka // tk, N // tn, T // tt),
        in_specs=[pl.BlockSpec((tt, tk), lambda i, j, t: (t, i)), pl.BlockSpec((tt, tn), lambda i, j, t: (t, j))],
        out_specs=pl.BlockSpec((tk, tn), lambda i, j, t: (i, j)),
        out_shape=jax.ShapeDtypeStruct((Ka, N), F32),
        compiler_params=_params(("parallel", "parallel", "arbitrary")),
    )(A, G)


def _rowwise(name, fn, *, Bl, S, R, tiled=(), prev=(), nxt=(), batch=(), full=(),
             out_tiled=(), out_batch=(), out_acc=()):
    R = min(R, S)
    nS = S // R
    T = Bl * S
    hb = R // HALO
    n_in = len(tiled) + len(prev) + len(nxt) + len(batch) + len(full)

    in_specs, args = [], []
    for a, wd, cb in tiled:
        in_specs.append(pl.BlockSpec((R, wd), lambda b, i, cb=cb: (b * nS + i, cb)))
        args.append(a)
    for a, wd, cb in prev:
        in_specs.append(pl.BlockSpec((HALO, wd), lambda b, i, cb=cb: (jnp.maximum((b * nS + i) * hb - 1, 0), cb)))
        args.append(a)
    for a, wd, cb in nxt:
        in_specs.append(pl.BlockSpec((HALO, wd), lambda b, i, cb=cb: (jnp.minimum((b * nS + i + 1) * hb, T // HALO - 1), cb)))
        args.append(a)
    for a, wd, cb in batch:
        in_specs.append(pl.BlockSpec((1, 1, wd), lambda b, i, cb=cb: (b, 0, cb)))
        args.append(a)
    for a in full:
        in_specs.append(pl.BlockSpec(a.shape, lambda b, i, nd=a.ndim: (0,) * nd))
        args.append(a)

    out_specs, out_shape = [], []
    for C, dt in out_tiled:
        out_specs.append(pl.BlockSpec((R, C), lambda b, i: (b * nS + i, 0)))
        out_shape.append(jax.ShapeDtypeStruct((T, C), dt))
    for C in out_batch:
        out_specs.append(pl.BlockSpec((1, 1, C), lambda b, i: (b, 0, 0)))
        out_shape.append(jax.ShapeDtypeStruct((Bl, 1, C), F32))
    for shp in out_acc:
        out_specs.append(pl.BlockSpec(shp, lambda b, i, nd=len(shp): (0,) * nd))
        out_shape.append(jax.ShapeDtypeStruct(shp, F32))

    nt, npv, nnx, nbt = len(tiled), len(prev), len(nxt), len(batch)

    def body(*refs):
        b, i = pl.program_id(0), pl.program_id(1)
        ins, outs = refs[:n_in], refs[n_in:]
        vals = [r[...] for r in ins[:nt]]
        vals += [jnp.where(i > 0, r[...], jnp.zeros(r.shape, r.dtype)) for r in ins[nt:nt + npv]]
        vals += [jnp.where(i < nS - 1, r[...], jnp.zeros(r.shape, r.dtype)) for r in ins[nt + npv:nt + npv + nnx]]
        vals += [r[0] for r in ins[nt + npv + nnx:nt + npv + nnx + nbt]]
        vals += [r[...] for r in ins[nt + npv + nnx + nbt:]]
        res = fn(*vals)
        if not isinstance(res, (tuple, list)):
            res = (res,)
        k = 0
        for _ in out_tiled:
            outs[k][...] = res[k].astype(outs[k].dtype)
            k += 1
        for _ in out_batch:
            o = outs[k]

            @pl.when(i == 0)
            def _(o=o):
                o[...] = jnp.zeros(o.shape, F32)
            o[0] += res[k]
            k += 1
        for _ in out_acc:
            o = outs[k]

            @pl.when((i == 0) & (b == 0))
            def _(o=o):
                o[...] = jnp.zeros(o.shape, F32)
            o[...] += res[k]
            k += 1

    out = pl.pallas_call(
        body, name=name, grid=(Bl, nS), in_specs=in_specs, out_specs=out_specs, out_shape=out_shape,
        compiler_params=_params(("arbitrary", "arbitrary")),
    )(*args)
    return out


def _colwise(name, fn, *, Bl, S, R, W, strip, tiled=(), prev=(), nxt=(), full=(), out_tiled=(), n_acc=0):
    R = min(R, S)
    nS = S // R
    T = Bl * S
    hb = R // HALO
    nt, npv, nnx, nfl = len(tiled), len(prev), len(nxt), len(full)
    n_in = nt + npv + nnx + nfl
    in_specs = [pl.BlockSpec((R, W), lambda b, i, cb=cb: (b * nS + i, cb)) for _, cb in tiled]
    in_specs += [pl.BlockSpec((HALO, W), lambda b, i, cb=cb: (jnp.maximum((b * nS + i) * hb - 1, 0), cb)) for _, cb in prev]
    in_specs += [pl.BlockSpec((HALO, W), lambda b, i, cb=cb: (jnp.minimum((b * nS + i + 1) * hb, T // HALO - 1), cb))
                 for _, cb in nxt]
    in_specs += [pl.BlockSpec(a.shape, lambda b, i: (0, 0)) for a in full]
    out_specs = [pl.BlockSpec((R, m * W), lambda b, i: (b * nS + i, 0)) for m, _ in out_tiled]
    out_specs += [pl.BlockSpec((1, W), lambda b, i: (0, 0))] * n_acc
    out_shape = [jax.ShapeDtypeStruct((T, m * W), dt) for m, dt in out_tiled] + [jax.ShapeDtypeStruct((1, W), F32)] * n_acc

    def body(*refs):
        b, i = pl.program_id(0), pl.program_id(1)
        ins, outs = refs[:n_in], refs[n_in:]

        @pl.when((i == 0) & (b == 0))
        def _():
            for o in outs[len(out_tiled):]:
                o[...] = jnp.zeros(o.shape, F32)

        def col(j, carry):
            cs = pl.ds(pl.multiple_of(j * strip, strip), strip)
            vals = [r[:, cs] for r in ins[:nt]]
            vals += [jnp.where(i > 0, r[:, cs], jnp.zeros((HALO, strip), r.dtype)) for r in ins[nt:nt + npv]]
            vals += [jnp.where(i < nS - 1, r[:, cs], jnp.zeros((HALO, strip), r.dtype)) for r in ins[nt + npv:nt + npv + nnx]]
            vals += [r[:, cs] for r in ins[nt + npv + nnx:]]
            res = fn(*vals)
            for k, (m, _) in enumerate(out_tiled):
                for q in range(m):
                    outs[k][:, pl.ds(pl.multiple_of(q * W + j * strip, strip), strip)] = res[k][q].astype(outs[k].dtype)
            for k in range(len(out_tiled), len(outs)):
                outs[k][:, cs] += res[k]
            return carry

        lax.fori_loop(0, W // strip, col, 0)

    return pl.pallas_call(
        body, name=name, grid=(Bl, nS), in_specs=in_specs, out_specs=out_specs, out_shape=out_shape,
        compiler_params=_params(("arbitrary", "arbitrary")),
    )(*[a for a, _ in tiled], *[a for a, _ in prev], *[a for a, _ in nxt], *full)


def _shift_down(x, halo, k):
    rolled = pltpu.roll(x, k, 0)
    row = lax.broadcasted_iota(jnp.int32, (SUBLANES, x.shape[1]), 0)
    head = rolled[0:SUBLANES]
    for j in range(k):
        head = jnp.where(row == j, halo[HALO - k + j:HALO - k + j + 1, :], head)
    return jnp.concatenate([head, rolled[SUBLANES:]], axis=0)


def _shift_up(x, halo, k):
    n = x.shape[0]
    rolled = pltpu.roll(x, n - k, 0)
    row = lax.broadcasted_iota(jnp.int32, (SUBLANES, x.shape[1]), 0)
    tail = rolled[n - SUBLANES:]
    for j in range(k):
        tail = jnp.where(row == SUBLANES - k + j, halo[j:j + 1, :], tail)
    return jnp.concatenate([rolled[:n - SUBLANES], tail], axis=0)


def _dotm(a, b):
    return jnp.dot(a.astype(MXU_DTYPE), b.astype(MXU_DTYPE), preferred_element_type=F32)


def _split_bf16(x):
    hi = x.astype(BF16)
    return hi, (x - hi.astype(F32)).astype(BF16)


def _headsum_2pass(x, hm):
    hi, lo = _split_bf16(x)
    hb = hm.astype(BF16)
    return jnp.dot(hi, hb, preferred_element_type=F32) + jnp.dot(lo, hb, preferred_element_type=F32)


@jax.custom_vjp
def _headsum(x, hm):
    return _headsum_2pass(x, hm)


_headsum.defvjp(lambda x, hm: (_headsum_2pass(x, hm), hm),
                lambda hm, g: (_headsum_2pass(g, hm), jnp.zeros_like(hm)))


def _sigmoid(x):
    return 1.0 / (1.0 + jnp.exp(-x))


def _rms(x, g):
    return x * lax.rsqrt(jnp.mean(x * x, axis=-1, keepdims=True) + RMS_EPS) * g


def _norm_mod(x, g, sc, sh):
    return _rms(x, g) * (1.0 + sc) + sh


def _split_ps(ps):
    return (ps[:, 0:RW], ps[:, RW:2 * RW], ps[:, 2 * RW:3 * RW], ps[:, 3 * RW:3 * RW + LW + LA],
            ps[:, 3 * RW + LW + LA:SHIFT])


def _rwkv_prep(r, k, v, wa, gd, w0, w_up_p, a0, a_up_p, g_up, k_k, k_a, hm):
    w_raw = w0 + _dotm(jnp.tanh(wa), w_up_p)
    decay = jnp.exp(-DECAY_SCALE * _sigmoid(w_raw))
    a = _sigmoid(a0 + _dotm(wa, a_up_p))
    g = _dotm(_sigmoid(gd), g_up)
    kk = k * k_k
    kk = kk * lax.rsqrt(_headsum(kk * kk, hm) + L2_EPS)
    k2 = k * (1.0 + (a - 1.0) * k_a)
    return r, decay, k2, v, -kk, kk * a, g


def _rwkv_post(y, r, k2, v, g, ln_g, ln_b, r_k, hm):
    mean = _headsum(y, hm) * (1.0 / HD)
    yc = y - mean
    var = _headsum(yc * yc, hm) * (1.0 / HD)
    yn = yc * lax.rsqrt(var + GN_EPS) * ln_g + ln_b
    bonus = _headsum(r * k2 * r_k, hm) * v
    return (yn + bonus) * g


def _gelu(x):
    return 0.5 * x * (1.0 + jnp.tanh(GELU_C * (x + 0.044715 * (x * x * x))))


def _s5_post(yssm, u, d):
    return _gelu(yssm + d * u)


def _mix(ga, gb, ya, za, zb):
    return _sigmoid(ga) * ya + _sigmoid(gb) * (za * _sigmoid(zb))


def _conv_act(up_g, up_u, hg, hu, w_g, w_u, b_g, b_u):
    gate, upv = _conv3(up_g, hg, w_g, b_g)[0], _conv3(up_u, hu, w_u, b_u)[0]
    return gate, upv


def _conv3(x, h, w, b):
    x, h = x.astype(F32), h.astype(F32)
    s2, s1 = _shift_down(x, h, 2), _shift_down(x, h, 1)
    return b + w[0:1] * s2 + w[1:2] * s1 + w[2:3] * x, (s2, s1, x)


def _silu_gate(gate, upv):
    return gate * _sigmoid(gate) * upv


WKV_L = 64
_NT, _NN, _TN = ((1,), (1,)), ((1,), (0,)), ((0,), (0,))


def _dotw(x, y, dims):
    return lax.dot_general(x.astype(MXU_DTYPE), y.astype(MXU_DTYPE), (dims, ((), ())), preferred_element_type=F32)


def _dot3(x, y, dims):
    (xh, xl), (yh, yl) = _split_bf16(x), _split_bf16(y)
    d = lambda p, q: lax.dot_general(p, q, (dims, ((), ())), preferred_element_type=F32)
    return d(xh, yh) + d(xh, yl) + d(xl, yh)


@jax.custom_vjp
def _gram3(x, y):
    return _dot3(x, y, _NT)


_gram3.defvjp(lambda x, y: (_dot3(x, y, _NT), (x, y)),
              lambda res, g: (_dot3(g, res[1], _NN), _dot3(g, res[0], _TN)))


def _tri_solve_fwd(ns, xs):
    each = lambda f, *ls: tuple(f(*zs) for zs in zip(*ls))
    size = ns[0].shape[0]
    eye = (lax.broadcasted_iota(jnp.int32, (size, size), 0) == lax.broadcasted_iota(jnp.int32, (size, size), 1)).astype(F32)
    ts = each(lambda n: n + eye, ns)
    qs = ns
    for _ in range(WKV_L.bit_length() - 2):
        qs = each(lambda q: _dotw(q, q, _NN), qs)
        ts = each(lambda t, q: t + _dotw(t, q, _NN), ts, qs)
    us = each(lambda t, x: _dotw(t, x, _NN), ts, xs)
    return us, (ts, us)


def _tri_solve_bwd(res, dus):
    ts, us = res
    each = lambda f, *ls: tuple(f(*zs) for zs in zip(*ls))
    dxs = each(lambda t, du: _dotw(t, du, _TN), ts, dus)
    return each(lambda dx, u: _dotw(dx, u, _NT), dxs, us), dxs


@jax.custom_vjp
def _tri_solve(ns, xs):
    return _tri_solve_fwd(ns, xs)[0]


_tri_solve.defvjp(_tri_solve_fwd, _tri_solve_bwd)


def _wkv_chunk(s0, r, w, k, v, a, b):
    y, s1 = _wkv_chunks((s0,), (r,), (w,), (k,), (v,), (a,), (b,))
    return y[0], s1[0]


def _wkv_chunks(s0, r, w, k, v, a, b):
    each = lambda f, *ls: tuple(f(*xs) for xs in zip(*ls))
    L = r[0].shape[0]
    n2 = 2 * L
    lane_head = lax.broadcasted_iota(jnp.int32, (2, 1, 2 * HD), 2) // HD
    head_mask = (lane_head == lax.broadcasted_iota(jnp.int32, (2, 1, 2 * HD), 0)).astype(F32)
    ri = lax.broadcasted_iota(jnp.int32, (n2, n2), 0)
    ci = lax.broadcasted_iota(jnp.int32, (n2, n2), 1)
    same = (ri // L) == (ci // L)
    strict = same & ((ci % L) < (ri % L))
    incl = same & ((ci % L) <= (ri % L))
    si = lax.broadcasted_iota(jnp.int32, (2 * HD, 2 * HD), 0) // HD
    sj = lax.broadcasted_iota(jnp.int32, (2 * HD, 2 * HD), 1) // HD
    tri = (lax.broadcasted_iota(jnp.int32, (L, L), 0) >= lax.broadcasted_iota(jnp.int32, (L, L), 1)).astype(F32)

    stack = lambda z: (z[None] * head_mask).reshape(n2, 2 * HD)
    dup = lambda z: jnp.broadcast_to(z[None], (2, L, 2 * HD)).reshape(n2, 2 * HD)
    gram = _gram3
    nt, nn, tn = (lambda x, y, d=d: _dotw(x, y, d) for d in (_NT, _NN, _TN))
    add = lambda x, y: x + y

    lw = each(jnp.log, w)
    cum = each(lambda z: jnp.dot(tri, z, preferred_element_type=F32, precision=HIGHEST), lw)
    tot = each(lambda z: jnp.sum(z, axis=0, keepdims=True), lw)
    a2 = each(lambda av, cv, lv: stack(av * jnp.exp(cv - lv)), a, cum, lw)
    r2 = each(lambda rv, cv: stack(rv * jnp.exp(cv)), r, cum)
    v2 = each(stack, v)
    b2 = each(lambda bv, cv: dup(bv * jnp.exp(-cv)), b, cum)
    k2 = each(lambda kv, cv: dup(kv * jnp.exp(-cv)), k, cum)
    n_ab = each(lambda x, y: jnp.where(strict, gram(x, y), 0.0), a2, b2)
    n_ak = each(lambda x, y: jnp.where(strict, gram(x, y), 0.0), a2, k2)
    m_rb = each(lambda x, y: jnp.where(incl, gram(x, y), 0.0), r2, b2)
    m_rk = each(lambda x, y: jnp.where(incl, gram(x, y), 0.0), r2, k2)
    u = _tri_solve(n_ab, each(add, each(nt, a2, s0), each(nn, n_ak, v2)))
    y2 = each(lambda x, y, z: x + y + z, each(nt, r2, s0), each(nn, m_rb, u), each(nn, m_rk, v2))
    y = each(lambda z: jnp.sum(z.reshape(2, L, 2 * HD), axis=0), y2)
    b3 = each(lambda bv, tv, cv: dup(bv * jnp.exp(tv - cv)), b, tot, cum)
    k3 = each(lambda kv, tv, cv: dup(kv * jnp.exp(tv - cv)), k, tot, cum)
    upd = each(add, each(tn, u, b3), each(tn, v2, k3))
    s1 = each(lambda sv, tv, uv: sv * jnp.exp(tv) + jnp.where(si == sj, uv, 0.0), s0, tot, upd)
    return y, s1


NPAIR = NH // 2


def _wkv_nb(Bl):
    return 2 if Bl % 2 == 0 else 1


def _wkv_fwd(r, w, k, v, a, b, Bl, S):
    L = WKV_L
    nC = S // L
    nb = _wkv_nb(Bl)
    chains = [(bi, p, slice(p * 2 * HD, (p + 1) * 2 * HD)) for bi in range(nb) for p in range(NPAIR)]

    def body(r_ref, w_ref, k_ref, v_ref, a_ref, b_ref, y_ref, ck_ref, s_ref):
        @pl.when(pl.program_id(1) == 0)
        def _():
            s_ref[...] = jnp.zeros(s_ref.shape, F32)
        s0 = tuple(s_ref[bi, p] for bi, p, _ in chains)
        ops = [tuple(z[bi, :, cs] for bi, _, cs in chains) for z in (r_ref, w_ref, k_ref, v_ref, a_ref, b_ref)]
        y, s1 = _wkv_chunks(s0, *ops)
        for i, (bi, p, cs) in enumerate(chains):
            ck_ref[bi, 0, p] = s0[i]
            y_ref[bi, :, cs] = y[i]
            s_ref[bi, p] = s1[i]

    to3 = lambda z: z.reshape(Bl, S, RW)
    row_spec = pl.BlockSpec((nb, L, RW), lambda g, c: (g, c, 0))
    y, ck = pl.pallas_call(
        body, name="wkv_fwd", grid=(Bl // nb, nC), in_specs=[row_spec] * 6,
        out_specs=[row_spec, pl.BlockSpec((nb, 1, NPAIR, 2 * HD, 2 * HD), lambda g, c: (g, c, 0, 0, 0))],
        out_shape=[jax.ShapeDtypeStruct((Bl, S, RW), F32), jax.ShapeDtypeStruct((Bl, nC, NPAIR, 2 * HD, 2 * HD), F32)],
        scratch_shapes=[pltpu.VMEM((nb, NPAIR, 2 * HD, 2 * HD), F32)],
        compiler_params=_params(("arbitrary", "arbitrary")),
    )(*(to3(z) for z in (r, w, k, v, a, b)))
    return y.reshape(Bl * S, RW), ck


def _wkv_bwd(r, w, k, v, a, b, dy, ck, Bl, S):
    L = WKV_L
    nC = S // L
    nb = _wkv_nb(Bl)
    chains = [(bi, p, slice(p * 2 * HD, (p + 1) * 2 * HD)) for bi in range(nb) for p in range(NPAIR)]

    def body(r_ref, w_ref, k_ref, v_ref, a_ref, b_ref, dy_ref, ck_ref,
             dr_ref, dw_ref, dk_ref, dv_ref, da_ref, db_ref, ds_ref):
        @pl.when(pl.program_id(1) == 0)
        def _():
            ds_ref[...] = jnp.zeros(ds_ref.shape, F32)
        s0 = tuple(ck_ref[bi, 0, p] for bi, p, _ in chains)
        ops = [tuple(z[bi, :, cs] for bi, _, cs in chains) for z in (r_ref, w_ref, k_ref, v_ref, a_ref, b_ref)]
        cts = (tuple(dy_ref[bi, :, cs] for bi, _, cs in chains), tuple(ds_ref[bi, p] for bi, p, _ in chains))
        ds0, *grads = jax.vjp(_wkv_chunks, s0, *ops)[1](cts)
        for i, (bi, p, cs) in enumerate(chains):
            ds_ref[bi, p] = ds0[i]
            for o, g in zip((dr_ref, dw_ref, dk_ref, dv_ref, da_ref, db_ref), grads):
                o[bi, :, cs] = g[i]

    to3 = lambda z: z.reshape(Bl, S, RW)
    row_spec = pl.BlockSpec((nb, L, RW), lambda g, c: (g, nC - 1 - c, 0))
    rows = jax.ShapeDtypeStruct((Bl, S, RW), F32)
    outs = pl.pallas_call(
        body, name="wkv_bwd", grid=(Bl // nb, nC),
        in_specs=[row_spec] * 7 + [pl.BlockSpec((nb, 1, NPAIR, 2 * HD, 2 * HD), lambda g, c: (g, nC - 1 - c, 0, 0, 0))],
        out_specs=[row_spec] * 6, out_shape=[rows] * 6,
        scratch_shapes=[pltpu.VMEM((nb, NPAIR, 2 * HD, 2 * HD), F32)],
        compiler_params=_params(("arbitrary", "arbitrary")),
    )(*(to3(z) for z in (r, w, k, v, a, b, dy)), ck)
    return [o.reshape(Bl * S, RW) for o in outs]


NST = NG * SP


def _cmul(ar, ai, br, bi):
    return ar * br - ai * bi, ar * bi + ai * br


def _s5_tiles(are, aim, reverse):
    if reverse:
        aim = -aim
    row = lax.broadcasted_iota(jnp.int32, (SUBLANES, NST), 0)
    pw = [(are, aim)]
    for _ in range(SUBLANES - 1):
        pw.append(_cmul(pw[-1][0], pw[-1][1], are, aim))
    bc = lambda z: jnp.broadcast_to(z, (SUBLANES, NST))
    ms = []
    for kk in (1, 2, 4):
        cond = (row < SUBLANES - kk) if reverse else (row >= kk)
        ms.append((jnp.where(cond, bc(pw[kk - 1][0]), 0.0), jnp.where(cond, bc(pw[kk - 1][1]), 0.0)))
    pr = jnp.zeros((SUBLANES, NST), F32)
    pi = jnp.zeros((SUBLANES, NST), F32)
    for i in range(SUBLANES):
        n = SUBLANES - i if reverse else i + 1
        pr = jnp.where(row == i, bc(pw[n - 1][0]), pr)
        pi = jnp.where(row == i, bc(pw[n - 1][1]), pi)
    return ms, (pr, pi)


def _s5_block(re, im, ms, pc, cre, cim, sg, reverse):
    ln = slice(sg * 512, (sg + 1) * 512)
    for (mr, mi), kk in zip(ms, (1, 2, 4)):
        sh = SUBLANES - kk if reverse else kk
        sre, sim = pltpu.roll(re, sh, 0), pltpu.roll(im, sh, 0)
        tr, ti = _cmul(mr[:, ln], mi[:, ln], sre, sim)
        re, im = re + tr, im + ti
    tr, ti = _cmul(pc[0][:, ln], pc[1][:, ln], cre[:, ln], cim[:, ln])
    return re + tr, im + ti


def _s5_scan(X_ref, n_rows, ms, pc, cre, cim, reverse, visit=None, acc0=None):
    nblk = n_rows // SUBLANES

    def it(i, carry):
        cre, cim, acc = carry
        j = nblk - 1 - i if reverse else i
        rows = pl.ds(pl.multiple_of(j * SUBLANES, SUBLANES), SUBLANES)
        edge = 0 if reverse else SUBLANES - 1
        blocks, ncre, ncim = [], [], []
        for sg in range(NSG):
            lr = slice(sg * 1024, sg * 1024 + 512)
            li = slice(sg * 1024 + 512, (sg + 1) * 1024)
            re, im = _s5_block(X_ref[rows, lr], X_ref[rows, li], ms, pc, cre, cim, sg, reverse)
            X_ref[rows, lr] = re
            X_ref[rows, li] = im
            blocks.append((re, im))
            ncre.append(re[edge:edge + 1])
            ncim.append(im[edge:edge + 1])
        if visit is not None:
            acc = visit(j, blocks, acc)
        return jnp.concatenate(ncre, axis=1), jnp.concatenate(ncim, axis=1), acc

    return lax.fori_loop(0, nblk, it, (cre, cim, acc0 if acc0 is not None else 0))


def _s5_fwd(u, wb, wc, ab, d, Bl, S, R=256):
    R = min(R, S)
    nC = S // R

    def body(u_ref, wb_ref, wc_ref, ab_ref, d_ref, y_ref, st_ref, X_ref, o_ref, car_ref):
        @pl.when(pl.program_id(1) == 0)
        def _():
            car_ref[...] = jnp.zeros(car_ref.shape, F32)
        st_ref[0, 0] = car_ref[...]
        ms, pc = _s5_tiles(ab_ref[0:1], ab_ref[1:2], False)
        for sg in range(NSG):
            X_ref[:, sg * 1024:(sg + 1) * 1024] = _dotm(u_ref[:, sg * 128:(sg + 1) * 128], wb_ref[sg])
        cre, cim, _ = _s5_scan(X_ref, R, ms, pc, car_ref[0:1], car_ref[1:2], False)
        car_ref[0:1] = cre
        car_ref[1:2] = cim
        for sg in range(NSG):
            y_ref[:, sg * 128:(sg + 1) * 128] = _dotm(X_ref[:, sg * 1024:(sg + 1) * 1024], wc_ref[sg])
        o_ref[...] = _s5_post(y_ref[...], u_ref[...], d_ref[...]).astype(o_ref.dtype)

    rows = pl.BlockSpec((R, SW), lambda b, c: (b * nC + c, 0))
    return pl.pallas_call(
        body, name="s5_fwd", grid=(Bl, nC),
        in_specs=[rows, pl.BlockSpec(wb.shape, lambda b, c: (0, 0, 0)), pl.BlockSpec(wc.shape, lambda b, c: (0, 0, 0)),
                  pl.BlockSpec(ab.shape, lambda b, c: (0, 0)), pl.BlockSpec(d.shape, lambda b, c: (0, 0))],
        out_specs=[rows, pl.BlockSpec((1, 1, 2, NST), lambda b, c: (b, c, 0, 0)),
                   pl.BlockSpec((R, 2 * NST), lambda b, c: (b * nC + c, 0)), rows],
        out_shape=[jax.ShapeDtypeStruct((Bl * S, SW), F32), jax.ShapeDtypeStruct((Bl, nC, 2, NST), F32),
                   jax.ShapeDtypeStruct((Bl * S, 2 * NST), F32), jax.ShapeDtypeStruct((Bl * S, SW), MXU_DTYPE)],
        scratch_shapes=[pltpu.VMEM((2, NST), F32)],
        compiler_params=_params(("arbitrary", "arbitrary")),
    )(u, wb, wc, ab, d)


def _s5_bwd(u, y, do, d, wb, wc, ab, st, xs, Bl, S, R=256):
    R = min(R, S)
    nC = S // R

    def body(u_ref, y_ref, do_ref, d_ref, wb_ref, wc_ref, ab_ref, st_ref, X_ref,
             du_ref, dwb_ref, dwc_ref, dab_ref, dd_ref, G_ref, car_ref):
        first = (pl.program_id(0) == 0) & (pl.program_id(1) == 0)

        @pl.when(first)
        def _():
            for o in (dwb_ref, dwc_ref, dab_ref, dd_ref):
                o[...] = jnp.zeros(o.shape, F32)

        @pl.when(pl.program_id(1) == 0)
        def _():
            car_ref[...] = jnp.zeros(car_ref.shape, F32)

        are, aim = ab_ref[0:1], ab_ref[1:2]
        dy, du_direct, dd = jax.vjp(_s5_post, y_ref[...], u_ref[...], d_ref[...])[1](do_ref[...])
        dd_ref[...] += dd
        dyv = dy.astype(MXU_DTYPE)
        for sg in range(NSG):
            G_ref[:, sg * 1024:(sg + 1) * 1024] = lax.dot_general(
                dyv[:, sg * 128:(sg + 1) * 128], wc_ref[sg].astype(MXU_DTYPE), (((1,), (1,)), ((), ())),
                preferred_element_type=F32)
        rms_, rpc = _s5_tiles(are, aim, True)
        row = lax.broadcasted_iota(jnp.int32, (SUBLANES, 512), 0)

        def visit(j, blocks, acc):
            before = pl.multiple_of(jnp.maximum(j - 1, 0) * SUBLANES, SUBLANES)
            prow = X_ref[pl.ds(before, SUBLANES), :][SUBLANES - 1:SUBLANES]
            rows = pl.ds(pl.multiple_of(j * SUBLANES, SUBLANES), SUBLANES)
            are_acc, aim_acc = [], []
            for sg in range(NSG):
                lr = slice(sg * 1024, sg * 1024 + 512)
                li = slice(sg * 1024 + 512, (sg + 1) * 1024)
                ln = slice(sg * 512, (sg + 1) * 512)
                pre = jnp.where(j > 0, prow[:, lr], st_ref[0, 0, 0:1, ln])
                pim = jnp.where(j > 0, prow[:, li], st_ref[0, 0, 1:2, ln])
                xre = jnp.where(row == 0, pre, pltpu.roll(X_ref[rows, lr], 1, 0))
                xim = jnp.where(row == 0, pim, pltpu.roll(X_ref[rows, li], 1, 0))
                dre, dim = blocks[sg]
                are_acc.append(dre * xre + dim * xim)
                aim_acc.append(dim * xre - dre * xim)
            return acc[0] + jnp.concatenate(are_acc, axis=1), acc[1] + jnp.concatenate(aim_acc, axis=1)

        zero = jnp.zeros((SUBLANES, NST), F32)
        cre, cim, acc = _s5_scan(G_ref, R, rms_, rpc, car_ref[0:1], car_ref[1:2], True, visit, (zero, zero))
        car_ref[0:1] = cre
        car_ref[1:2] = cim
        dab_ref[0:1] += jnp.sum(acc[0], axis=0, keepdims=True)
        dab_ref[1:2] += jnp.sum(acc[1], axis=0, keepdims=True)
        uv = u_ref[...].astype(MXU_DTYPE)
        for sg in range(NSG):
            cs = slice(sg * 1024, (sg + 1) * 1024)
            us = slice(sg * 128, (sg + 1) * 128)
            gx = G_ref[:, cs].astype(MXU_DTYPE)
            dwb_ref[sg] += lax.dot_general(uv[:, us], gx, (((0,), (0,)), ((), ())), preferred_element_type=F32)
            dwc_ref[sg] += lax.dot_general(X_ref[:, cs].astype(MXU_DTYPE), dyv[:, us], (((0,), (0,)), ((), ())),
                                           preferred_element_type=F32)
            du_ssm = lax.dot_general(gx, wb_ref[sg].astype(MXU_DTYPE), (((1,), (1,)), ((), ())),
                                     preferred_element_type=F32)
            du_ref[:, us] = (du_ssm + du_direct[:, us]).astype(du_ref.dtype)

    rmap = lambda b, c: (b * nC + nC - 1 - c, 0)
    rows = pl.BlockSpec((R, SW), rmap)
    return pl.pallas_call(
        body, name="s5_bwd", grid=(Bl, nC),
        in_specs=[rows, rows, rows, pl.BlockSpec(d.shape, lambda b, c: (0, 0)),
                  pl.BlockSpec(wb.shape, lambda b, c: (0, 0, 0)), pl.BlockSpec(wc.shape, lambda b, c: (0, 0, 0)),
                  pl.BlockSpec(ab.shape, lambda b, c: (0, 0)),
                  pl.BlockSpec((1, 1, 2, NST), lambda b, c: (b, nC - 1 - c, 0, 0)),
                  pl.BlockSpec((R, 2 * NST), rmap)],
        out_specs=[rows, pl.BlockSpec(wb.shape, lambda b, c: (0, 0, 0)),
                   pl.BlockSpec(wc.shape, lambda b, c: (0, 0, 0)), pl.BlockSpec((2, NST), lambda b, c: (0, 0)),
                   pl.BlockSpec(d.shape, lambda b, c: (0, 0))],
        out_shape=[jax.ShapeDtypeStruct((Bl * S, SW), MXU_DTYPE), jax.ShapeDtypeStruct(wb.shape, F32),
                   jax.ShapeDtypeStruct(wc.shape, F32), jax.ShapeDtypeStruct((2, NST), F32),
                   jax.ShapeDtypeStruct(d.shape, F32)],
        scratch_shapes=[pltpu.VMEM((R, 2 * NST), F32), pltpu.VMEM((2, NST), F32)],
        compiler_params=_params(("arbitrary", "arbitrary")),
    )(u, y, do, d, wb, wc, ab, st, xs)


def _s5_disc_math(a_re, a_im, log_dt, b_re, b_im, expand):
    dt = jnp.exp(log_dt)
    z_re, z_im = a_re * dt, a_im * dt
    mag = jnp.exp(z_re)
    ab_re, ab_im = mag * jnp.cos(z_im), mag * jnp.sin(z_im)
    den = a_re * a_re + a_im * a_im
    q_re = ((ab_re - 1.0) * a_re + ab_im * a_im) / den
    q_im = (ab_im * a_re - (ab_re - 1.0) * a_im) / den
    qe_re = jnp.dot(q_re, expand, preferred_element_type=F32, precision=HIGHEST)
    qe_im = jnp.dot(q_im, expand, preferred_element_type=F32, precision=HIGHEST)
    return ab_re, ab_im, qe_re * b_re - qe_im * b_im, qe_re * b_im + qe_im * b_re


def _whole(shape):
    return pl.BlockSpec(shape, lambda nd=len(shape): (0,) * nd)


def _s5_disc(a_re, a_im, log_dt, b_re, b_im, expand):
    def body(a, b, c, d, e, f, o0, o1, o2, o3):
        res = _s5_disc_math(a[...], b[...], c[...], d[...], e[...], f[...])
        for o, v in zip((o0, o1, o2, o3), res):
            o[...] = v
    ins = (a_re, a_im, log_dt, b_re, b_im, expand)
    outs = [jax.ShapeDtypeStruct(a_re.shape, F32)] * 2 + [jax.ShapeDtypeStruct(b_re.shape, F32)] * 2
    return pl.pallas_call(body, name="s5_disc", in_specs=[_whole(x.shape) for x in ins],
                          out_specs=[_whole(o.shape) for o in outs], out_shape=outs)(*ins)


def _s5_disc_bwd(a_re, a_im, log_dt, b_re, b_im, expand, cts):
    def body(a, b, c, d, e, f, g0, g1, g2, g3, o0, o1, o2, o3, o4):
        fn = lambda *p: _s5_disc_math(*p, f[...])
        _, vjp = jax.vjp(fn, a[...], b[...], c[...], d[...], e[...])
        for o, v in zip((o0, o1, o2, o3, o4), vjp((g0[...], g1[...], g2[...], g3[...]))):
            o[...] = v
    ins = (a_re, a_im, log_dt, b_re, b_im, expand) + tuple(cts)
    outs = [jax.ShapeDtypeStruct(x.shape, F32) for x in (a_re, a_im, log_dt, b_re, b_im)]
    return pl.pallas_call(body, name="s5_disc_bwd", in_specs=[_whole(x.shape) for x in ins],
                          out_specs=[_whole(o.shape) for o in outs], out_shape=outs)(*ins)


def _ada_fwd(c_all, w_shard, b_shard):
    def body(c_ref, w_ref, b_ref, o_ref):
        cv = c_ref[...]
        o_ref[...] = _dotm(cv * _sigmoid(cv), w_ref[...]) + b_ref[...]
    n = w_shard.shape[1]
    return pl.pallas_call(
        body, name="ada_fwd", in_specs=[_whole(c_all.shape), _whole(w_shard.shape), _whole(b_shard.shape)],
        out_specs=_whole((c_all.shape[0], n)), out_shape=jax.ShapeDtypeStruct((c_all.shape[0], n), F32),
        compiler_params=_params(),
    )(c_all, w_shard, b_shard)


def _ada_bwd(c_all, dmod_cols, dmod_all):
    def body(c_ref, dc_ref, da_ref, gw_ref, gb_ref):
        cv = c_ref[...]
        gw_ref[...] = lax.dot_general((cv * _sigmoid(cv)).astype(MXU_DTYPE), dc_ref[...].astype(MXU_DTYPE),
                                      (((0,), (0,)), ((), ())), preferred_element_type=F32)
        gb_ref[...] = jnp.sum(da_ref[...], axis=0, keepdims=True)
    n = dmod_cols.shape[1]
    return pl.pallas_call(
        body, name="ada_bwd", in_specs=[_whole(c_all.shape), _whole(dmod_cols.shape), _whole(dmod_all.shape)],
        out_specs=[_whole((D, n)), _whole((1, dmod_all.shape[1]))],
        out_shape=[jax.ShapeDtypeStruct((D, n), F32), jax.ShapeDtypeStruct((1, dmod_all.shape[1]), F32)],
        compiler_params=_params(),
    )(c_all, dmod_cols, dmod_all)


def _rows_block(n_rows, cap=512):
    if n_rows <= cap:
        return n_rows
    for t in range(cap - cap % SUBLANES, 0, -SUBLANES):
        if n_rows % t == 0:
            return t
    return n_rows


def _adamw(w, g, m, v, name):
    rows, cols = w.shape
    tr = _rows_block(rows, max(SUBLANES, (1 << 19) // max(cols, 1) // SUBLANES * SUBLANES))

    def body(w_ref, g_ref, m_ref, v_ref, d_ref, nm_ref, nv_ref):
        gv = g_ref[...]
        nm = B1 * m_ref[...] + (1.0 - B1) * gv
        nv = B2 * v_ref[...] + (1.0 - B2) * (gv * gv)
        m_hat = nm / (1.0 - B1 ** STEP)
        v_hat = nv / (1.0 - B2 ** STEP)
        d_ref[...] = -LR * (m_hat / (jnp.sqrt(v_hat) + ADAM_EPS) + WD * w_ref[...])
        nm_ref[...] = nm
        nv_ref[...] = nv

    spec = pl.BlockSpec((tr, cols), lambda i: (i, 0))
    sd = jax.ShapeDtypeStruct((rows, cols), F32)
    return pl.pallas_call(body, name=name, grid=(rows // tr,), in_specs=[spec] * 4, out_specs=[spec] * 3,
                          out_shape=[sd] * 3, compiler_params=_params(("parallel",)))(w, g, m, v)


def _sum_slots(x, out_dtype, name):
    xs = x if isinstance(x, (list, tuple)) else [x]
    _, rows, cols = xs[0].shape
    tr = _rows_block(rows)

    def body(*refs):
        acc = None
        for x_ref in refs[:-1]:
            for j in range(x_ref.shape[0]):
                term = x_ref[j].astype(F32)
                acc = term if acc is None else acc + term
        refs[-1][...] = acc.astype(refs[-1].dtype)

    return pl.pallas_call(
        body, name=name, grid=(rows // tr,),
        in_specs=[pl.BlockSpec((z.shape[0], tr, cols), lambda i: (0, i, 0)) for z in xs],
        out_specs=pl.BlockSpec((tr, cols), lambda i: (i, 0)), out_shape=jax.ShapeDtypeStruct((rows, cols), out_dtype),
        compiler_params=_params(("parallel",)))(*xs)


PACK_COLS = 1024


def _pack_rows(parts, dtype, row_mult):
    flat = jnp.concatenate([p.reshape(-1).astype(dtype) for p in parts])
    per = PACK_COLS * row_mult
    n = -(-flat.shape[0] // per) * per
    flat = jnp.pad(flat, (0, n - flat.shape[0]))
    return flat.reshape(n // PACK_COLS, PACK_COLS)


def _unpack(flat, shapes):
    out, off = [], 0
    for s in shapes:
        n = math.prod(s)
        out.append(flat[off:off + n].reshape(s))
        off += n
    return out


BIG = (("w_in", (D, SHIFT + SW + 2 * D), 1), ("w_out_rwkv", (RW, D), 1), ("w_glu", (SW, 2 * D), 1),
       ("w_out", (D, D), 0), ("w_ffn_up", (D, 2 * DFF), 1), ("w_ffn_down", (DFF, D), 0))
BIG_SMALL = (("rwkv_w_up", (LW, RW), 1), ("rwkv_a_up", (LA, RW), 1), ("rwkv_g_up", (LG, RW), 1),
             ("ffn_conv_w", (3, 2 * DFF), 1))
BIG_LATE = BIG[4:]
BIG_MID = BIG[1:4]


def _shard_shape(shape, axis):
    return (shape[0] // 4, shape[1]) if axis == 0 else (shape[0], shape[1] // 4)


def _to_shards(g, axis):
    r, C = g.shape
    return g.reshape(4, r // 4, C) if axis == 0 else g.reshape(r, 4, C // 4).transpose(1, 0, 2)


def _from_shards(x, axis):
    _, r, C = x.shape
    return x.reshape(4 * r, C) if axis == 0 else x.transpose(1, 0, 2).reshape(r, 4 * C)


def kernel(x, c, w_ada, b_ada, norm1_g, w_in, mu_shift, rwkv_w0, rwkv_w_up, rwkv_a0, rwkv_a_up, rwkv_g_up, rwkv_k_k, rwkv_k_a, rwkv_r_k, rwkv_ln_g, rwkv_ln_b, w_out_rwkv, s5_a_re, s5_a_im, s5_log_dt, s5_b_re, s5_b_im, s5_c_re, s5_c_im, s5_d, w_glu, w_out, norm2_g, w_ffn_up, ffn_conv_w, ffn_conv_b, w_ffn_down, norm_f_g, loss_target, m_w_ada, m_b_ada, m_norm1_g, m_w_in, m_mu_shift, m_rwkv_w0, m_rwkv_w_up, m_rwkv_a0, m_rwkv_a_up, m_rwkv_g_up, m_rwkv_k_k, m_rwkv_k_a, m_rwkv_r_k, m_rwkv_ln_g, m_rwkv_ln_b, m_w_out_rwkv, m_s5_a_re, m_s5_a_im, m_s5_log_dt, m_s5_b_re, m_s5_b_im, m_s5_c_re, m_s5_c_im, m_s5_d, m_w_glu, m_w_out, m_norm2_g, m_w_ffn_up, m_ffn_conv_w, m_ffn_conv_b, m_w_ffn_down, m_norm_f_g, v_w_ada, v_b_ada, v_norm1_g, v_w_in, v_mu_shift, v_rwkv_w0, v_rwkv_w_up, v_rwkv_a0, v_rwkv_a_up, v_rwkv_g_up, v_rwkv_k_k, v_rwkv_k_a, v_rwkv_r_k, v_rwkv_ln_g, v_rwkv_ln_b, v_w_out_rwkv, v_s5_a_re, v_s5_a_im, v_s5_log_dt, v_s5_b_re, v_s5_b_im, v_s5_c_re, v_s5_c_im, v_s5_d, v_w_glu, v_w_out, v_norm2_g, v_w_ffn_up, v_ffn_conv_w, v_ffn_conv_b, v_w_ffn_down, v_norm_f_g):
    names = ["w_ada", "b_ada", "norm1_g", "w_in", "mu_shift", "rwkv_w0", "rwkv_w_up", "rwkv_a0", "rwkv_a_up",
             "rwkv_g_up", "rwkv_k_k", "rwkv_k_a", "rwkv_r_k", "rwkv_ln_g", "rwkv_ln_b", "w_out_rwkv", "s5_a_re",
             "s5_a_im", "s5_log_dt", "s5_b_re", "s5_b_im", "s5_c_re", "s5_c_im", "s5_d", "w_glu", "w_out", "norm2_g",
             "w_ffn_up", "ffn_conv_w", "ffn_conv_b", "w_ffn_down", "norm_f_g"]
    env = dict(locals())
    W = {n: env[n] for n in names}
    M = {n: env["m_" + n] for n in names}
    V = {n: env["v_" + n] for n in names}

    Bl, S, _ = x.shape
    T = Bl * S
    ix, iy, ic = lax.axis_index("x"), lax.axis_index("y"), lax.axis_index("c")
    chip = 2 * ix + iy
    dev = 2 * chip + ic
    rw = functools.partial(_rowwise, Bl=Bl, S=S)

    now = [b for b in BIG if b not in BIG_LATE]
    chip_arrs = [W[n][0].astype(MXU_DTYPE) for n, _, _ in now] + [W[n][0] for n, _, _ in BIG_SMALL[:3]]
    got_chip, got_dev = _gather_two_level(chip_arrs, [W["ffn_conv_w"][0], c], "gather_w")
    full = {n: _from_shards(g, axis) for (n, _, axis), g in zip(tuple(now) + BIG_SMALL[:3], got_chip)}
    full["ffn_conv_w"] = _from_shards(got_dev[0][:, 0], 1)
    c_all = got_dev[1].reshape(8 * Bl, D)
    w_p, w_u, w_g = full["w_in"][:, :SHIFT], full["w_in"][:, SHIFT:SHIFT + SW], full["w_in"][:, SHIFT + SW:]
    zeros_l = jnp.zeros((LW, RW), F32)
    w_up_p = jnp.concatenate([full["rwkv_w_up"], zeros_l], axis=0)
    a_up_p = jnp.concatenate([zeros_l, full["rwkv_a_up"]], axis=0)
    g_up = full["rwkv_g_up"]
    conv_w = full["ffn_conv_w"]
    conv_wg, conv_wu = conv_w[:, :DFF], conv_w[:, DFF:]
    conv_bg, conv_bu = ffn_conv_b[:, :DFF], ffn_conv_b[:, DFF:]
    hm = jnp.kron(jnp.eye(NH, dtype=F32), jnp.ones((HD, HD), F32))

    ncol = 6 * D // 4
    b_ada_cols = lax.dynamic_slice_in_dim(b_ada, chip * ncol, ncol, 1)
    mod_part = _ada_fwd(c_all, w_ada[0], b_ada_cols)
    mod4 = _gather_two_level([], [mod_part], "gather_mod")[1][0][:, 0]
    mod4, late = lax.optimization_barrier((mod4, [W[n][0].astype(MXU_DTYPE) for n, _, _ in BIG_LATE]))
    late_moves = [(i, i, lambda ref, me, peer: ref, lambda ref, me, k: ref.at[_chip_of(me)]) for i in range(len(late))]
    late_start = _send_start("gather_ffn_start", CHIP_FLIPS, late,
                             [jax.ShapeDtypeStruct((4,) + z.shape, z.dtype) for z in late], late_moves)
    norm1_g = norm1_g + late_start["token"]
    mod = lax.dynamic_slice_in_dim(mod4, dev * Bl, Bl, 1).transpose(1, 0, 2).reshape(Bl, 1, 6 * D)
    SH1, SC1, GT1, SH2, SC2, GT2 = range(6)

    x2d = x.reshape(T, D)
    tgt = loss_target.reshape(T, D)

    (h1,) = rw("norm1", lambda xv, sc, sh, g: _norm_mod(xv, g, sc, sh), R=256, tiled=[(x2d, D, 0)],
               batch=[(mod, D, SC1), (mod, D, SH1)], full=[norm1_g], out_tiled=[(D, MXU_DTYPE)])
    p = _mm([h1], [w_p], F32, "proj_p")
    u = _mm([h1], [w_u], F32, "proj_u")
    gates = _mm([h1], [w_g], F32, "proj_g")

    prep_params = [rwkv_w0, w_up_p, rwkv_a0, a_up_p, g_up, rwkv_k_k, rwkv_k_a, hm]

    def prep_fwd(pv, ph, mu, *pp):
        ps = pv + (_shift_down(pv, ph, 1) - pv) * mu
        return _rwkv_prep(*_split_ps(ps), *pp)

    r_, w_, k_, v_, a_, b_, g_ = rw("rwkv_prep", prep_fwd, R=256, tiled=[(p, SHIFT, 0)], prev=[(p, SHIFT, 0)],
                                    full=[mu_shift] + prep_params, out_tiled=[(RW, F32)] * 7)
    y_wkv, ck = _wkv_fwd(r_, w_, k_, v_, a_, b_, Bl, S)
    r_k_row = rwkv_r_k.reshape(1, RW)
    post_params = [rwkv_ln_g, rwkv_ln_b, r_k_row, hm]
    (o_rwkv,) = rw("rwkv_post", _rwkv_post, R=256,
                   tiled=[(y_wkv, RW, 0), (r_, RW, 0), (k_, RW, 0), (v_, RW, 0), (g_, RW, 0)],
                   full=post_params, out_tiled=[(RW, MXU_DTYPE)])
    y_a = _mm([o_rwkv], [full["w_out_rwkv"]], F32, "out_rwkv")

    expand = jnp.kron(jnp.eye(SP, dtype=F32), jnp.ones((1, SGC), F32))
    s5_in = (s5_a_re[0], s5_a_im[0], s5_log_dt[0].reshape(NG, 1), s5_b_re[0].reshape(NG, SP * SGC),
             s5_b_im[0].reshape(NG, SP * SGC), expand)
    ab_re, ab_im, bb_re, bb_im = _s5_disc(*s5_in)
    eye8 = jnp.eye(8, dtype=F32)

    def blockdiag_in(bb):
        t = bb.reshape(NSG, 8, SP, SGC)
        return jnp.einsum("ab,sapc->sacbp", eye8, t).reshape(NSG, 128, 512)

    def blockdiag_out(cc):
        t = cc.reshape(NSG, 8, SGC, SP)
        return jnp.einsum("ab,sacp->sapbc", eye8, t).reshape(NSG, 512, 128)

    wb = jnp.concatenate([blockdiag_in(bb_re), blockdiag_in(bb_im)], axis=2).astype(MXU_DTYPE)
    wc = jnp.concatenate([blockdiag_out(s5_c_re[0]), -blockdiag_out(s5_c_im[0])], axis=1).astype(MXU_DTYPE)
    ab = jnp.stack([ab_re.reshape(NST), ab_im.reshape(NST)])
    y_ssm, s5_st, s5_x, s5o = _s5_fwd(u, wb, wc, ab, s5_d, Bl, S)
    z = _mm([s5o], [full["w_glu"]], F32, "glu")
    mix_tiled = [(gates, D, 0), (gates, D, 1), (y_a, D, 0), (z, D, 0), (z, D, 1)]
    (mixed_in,) = rw("mix", _mix, R=256, tiled=mix_tiled, out_tiled=[(D, MXU_DTYPE)])
    mixed = _mm([mixed_in], [full["w_out"]], F32, "out_proj")

    def norm2_fwd(xv, mx, gt, sc, sh, g):
        x1 = xv + gt * mx
        return x1, _norm_mod(x1, g, sc, sh)

    x1, h2 = rw("norm2", norm2_fwd, R=256, tiled=[(x2d, D, 0), (mixed, D, 0)],
                batch=[(mod, D, GT1), (mod, D, SC2), (mod, D, SH2)], full=[norm2_g],
                out_tiled=[(D, F32), (D, MXU_DTYPE)])
    late_own, late_got = _send_wait("gather_ffn_wait", CHIP_FLIPS, late_start, late_moves, h2)
    for (n, _, axis), own, got in zip(BIG_LATE, late_own, late_got):
        full[n] = _from_shards(lax.dynamic_update_slice(got, own[None], (chip, 0, 0)), axis)
    up = _mm([h2], [full["w_ffn_up"]], MXU_DTYPE, "ffn_up")
    conv_tiled = [(up, 0), (up, 1)]
    conv_full = [conv_wg, conv_wu, conv_bg, conv_bu]
    cw = functools.partial(_colwise, Bl=Bl, S=S, R=128, W=DFF, strip=LANES)

    def act_fwd(*a):
        return ((_silu_gate(*_conv_act(*a)),),)

    (act,) = cw("ffn_act", act_fwd, tiled=conv_tiled, prev=conv_tiled, full=conv_full, out_tiled=[(1, MXU_DTYPE)])
    ffn = _mm([act], [full["w_ffn_down"]], F32, "ffn_down")

    def head(x1v, fv, tv, gt, g):
        x2 = x1v + gt * fv
        y, vjp = jax.vjp(_rms, x2, g)
        e = y - tv
        dx2, dg = vjp(e * (1.0 / D))
        loss = jnp.sum(e * e, keepdims=True) * jnp.ones((1, LANES), F32)
        return dx2, dx2 * gt, jnp.sum(dx2 * fv, axis=0, keepdims=True), dg.reshape(1, D), loss

    dx2, d_ffn, d_gt2, g_norm_f, loss_acc = rw(
        "head", head, R=256, tiled=[(x1, D, 0), (ffn, D, 0), (tgt, D, 0)], batch=[(mod, D, GT2)],
        full=[norm_f_g.reshape(1, D)], out_tiled=[(D, F32), (D, MXU_DTYPE)], out_batch=[D],
        out_acc=[(1, D), (1, LANES)])
    loss = lax.psum(0.5 / D * loss_acc[0, 0], ("x", "y", "c"))

    d_act = _mm([d_ffn], [full["w_ffn_down"]], F32, "d_act", bt=True)
    g_w_ffn_down = _mm_tn(act, d_ffn, "g_ffn_down")

    def act_bwd(ug, uu, dact, hg, hu, wg, wu, bg, bu):
        (gate, taps_g), (upv, taps_u) = _conv3(ug, hg, wg, bg), _conv3(uu, hu, wu, bu)
        _, vjp_s = jax.vjp(_silu_gate, gate, upv)
        d_gate, d_upv = vjp_s(dact)
        def taps(dh, shifted):
            return [jnp.sum(dh * s, axis=0, keepdims=True) for s in shifted] + [jnp.sum(dh, axis=0, keepdims=True)]
        return ((d_gate,), (d_upv,), *taps(d_gate, taps_g), *taps(d_upv, taps_u))

    dh_g, dh_u, *tapg = cw("ffn_act_bwd", act_bwd, tiled=conv_tiled + [(d_act, 0)], prev=conv_tiled, full=conv_full,
                           out_tiled=[(1, MXU_DTYPE), (1, MXU_DTYPE)], n_acc=8)
    g_cw_g, g_cb_g = jnp.concatenate(tapg[0:3], axis=0), tapg[3]
    g_cw_u, g_cb_u = jnp.concatenate(tapg[4:7], axis=0), tapg[7]

    def conv_t(dg, du_, ng, nu, wg, wu):
        dg, du_, ng, nu = (z.astype(F32) for z in (dg, du_, ng, nu))

        def ct(d, n, w):
            return w[2:3] * d + w[1:2] * _shift_up(d, n, 1) + w[0:1] * _shift_up(d, n, 2)
        return ((ct(dg, ng, wg), ct(du_, nu, wu)),)

    (d_up,) = cw("conv_bwd", conv_t, tiled=[(dh_g, 0), (dh_u, 0)], nxt=[(dh_g, 0), (dh_u, 0)],
                 full=[conv_wg, conv_wu], out_tiled=[(2, MXU_DTYPE)])
    d_h2 = _mm([d_up], [full["w_ffn_up"]], F32, "d_h2", bt=True)
    g_w_ffn_up = _mm_tn(h2, d_up, "g_ffn_up")

    sds = jax.ShapeDtypeStruct
    reduce_src = lambda r: (lambda ref, me, peer: ref.at[_chip_of(peer), _half(r, peer[2])])

    def reduced_halves(tag, started, moves, after):
        gsh_own, got = _send_wait("rs_%s_wait" % tag, ALL_FLIPS, started, moves, after)
        halves = []
        for i, (g, gt) in enumerate(zip(gsh_own, got)):
            h = g.shape[1] // 2
            own = lax.dynamic_slice(g, (chip, ic * h, 0), (1, h, g.shape[2]))
            halves.append(_sum_slots([own, gt], F32, "rs_%s_sum%d" % (tag, i)))
        return halves

    def share_start(tag, halves):
        moves = [(i, i, lambda ref, me, peer: ref, lambda ref, me, k: ref) for i in range(len(halves))]
        return _send_start("share_%s_start" % tag, PAIR_FLIPS, halves, [sds(g.shape, F32) for g in halves], moves), moves

    def share_finish(tag, started, moves, after, group, grads):
        mine_h, got_h = _send_wait("share_%s_wait" % tag, PAIR_FLIPS, started, moves, after)
        for (n, _, _), mh, gh in zip(group, mine_h, got_h):
            grads[n] = jnp.concatenate([jnp.where(ic == 0, mh, gh), jnp.where(ic == 0, gh, mh)], axis=0)[None]

    def reduce_start(tag, group, mats):
        gsh = [_to_shards(g, ax).astype(MXU_DTYPE) for g, (_, _, ax) in zip(mats, group)]
        moves = [(i, i, reduce_src(g.shape[1]), lambda ref, me, k: ref.at[k]) for i, g in enumerate(gsh)]
        lands = [sds((len(ALL_FLIPS), g.shape[1] // 2, g.shape[2]), MXU_DTYPE) for g in gsh]
        return _send_start("rs_%s_start" % tag, ALL_FLIPS, gsh, lands, moves), moves

    rsl, rsl_moves = reduce_start("ffn", BIG_LATE, (g_w_ffn_up, g_w_ffn_down))
    norm2_g = norm2_g + rsl["token"]

    def norm2_bwd(x1v, dh2, dx2v, mx, gt, sc, sh, g):
        _, vjp = jax.vjp(_norm_mod, x1v, g, sc, sh)
        dxn, dg, dsc, dsh = vjp(dh2)
        dx1 = dx2v + dxn
        return dx1, dx1 * gt, jnp.sum(dx1 * mx, axis=0, keepdims=True), dsc, dsh, dg

    dx1, d_mixed, d_gt1, d_sc2, d_sh2, g_norm2 = rw(
        "norm2_bwd", norm2_bwd, R=256, tiled=[(x1, D, 0), (d_h2, D, 0), (dx2, D, 0), (mixed, D, 0)],
        batch=[(mod, D, GT1), (mod, D, SC2), (mod, D, SH2)], full=[norm2_g],
        out_tiled=[(D, F32), (D, MXU_DTYPE)], out_batch=[D, D, D], out_acc=[(1, D)])

    d_mixed_in = _mm([d_mixed], [full["w_out"]], F32, "d_mixed_in", bt=True)
    g_w_out = _mm_tn(mixed_in, d_mixed, "g_w_out")

    def mix_bwd(ga, gb, ya, za, zb, dm):
        _, vjp = jax.vjp(_mix, ga, gb, ya, za, zb)
        dga, dgb, dya, dza, dzb = vjp(dm)
        return jnp.concatenate([dga, dgb], axis=1), dya, jnp.concatenate([dza, dzb], axis=1)

    d_gates, d_ya, d_z = rw("mix_bwd", mix_bwd, R=256, tiled=mix_tiled + [(d_mixed_in, D, 0)],
                            out_tiled=[(2 * D, MXU_DTYPE), (D, MXU_DTYPE), (2 * D, MXU_DTYPE)])
    d_o_rwkv = _mm([d_ya], [full["w_out_rwkv"]], F32, "d_o_rwkv", bt=True)
    g_w_out_rwkv = _mm_tn(o_rwkv, d_ya, "g_out_rwkv")
    d_s5o = _mm([d_z], [full["w_glu"]], F32, "d_s5o", bt=True)
    g_w_glu = _mm_tn(s5o, d_z, "g_glu")
    rsm, rsm_moves = reduce_start("mid", BIG_MID, (g_w_out_rwkv, g_w_glu, g_w_out))
    s5_d = s5_d + rsm["token"]

    d_u, d_wb, d_wc, d_ab, g_s5_d = _s5_bwd(u, y_ssm, d_s5o, s5_d, wb, wc, ab, s5_st, s5_x, Bl, S)

    def diag_in(dw):
        t = dw.reshape(NSG, 8, SGC, 8, SP)
        return jnp.einsum("ab,sacbp->sapc", eye8, t).reshape(NG, SP * SGC)

    def diag_out(dw):
        t = dw.reshape(NSG, 8, SP, 8, SGC)
        return jnp.einsum("ab,sapbc->sacp", eye8, t).reshape(NG, SGC, SP)

    g_s5_c_re = diag_out(d_wc[:, :512])
    g_s5_c_im = -diag_out(d_wc[:, 512:])
    disc_cts = (d_ab[0].reshape(NG, SP), d_ab[1].reshape(NG, SP), diag_in(d_wb[:, :, :512]), diag_in(d_wb[:, :, 512:]))
    g_a_re, g_a_im, g_log_dt, g_b_re, g_b_im = _s5_disc_bwd(*s5_in, disc_cts)

    def post_bwd(yv, rv, kv, vv, gv, do, *pp):
        _, vjp = jax.vjp(lambda *a: _rwkv_post(*a, pp[3]), yv, rv, kv, vv, gv, *pp[:3])
        return vjp(do)

    dy_wkv, dr_b, dk_b, dv_b, dg_, g_ln_g, g_ln_b, g_r_k = rw(
        "rwkv_post_bwd", post_bwd, R=256,
        tiled=[(y_wkv, RW, 0), (r_, RW, 0), (k_, RW, 0), (v_, RW, 0), (g_, RW, 0), (d_o_rwkv, RW, 0)],
        full=post_params, out_tiled=[(RW, F32)] * 5, out_acc=[(1, RW)] * 3)
    dr3, dw3, dk3, dv3, da3, db3 = _wkv_bwd(r_, w_, k_, v_, a_, b_, dy_wkv, ck, Bl, S)

    shl, shl_moves = share_start("ffn", reduced_halves("ffn", rsl, rsl_moves, dr3))
    shm, shm_moves = share_start("mid", reduced_halves("mid", rsm, rsm_moves, dr3))
    mu_shift = mu_shift + (shl["token"] + shm["token"])

    def prep_bwd(pv, dr1, dr2, dwv, dk1, dk2, dv1, dv2, dav, dbv, dgv, ph, mu, *pp):
        prev = _shift_down(pv, ph, 1)
        ps = pv + (prev - pv) * mu
        _, vjp = jax.vjp(lambda *q: _rwkv_prep(*q, pp[7]), *_split_ps(ps), *pp[:7])
        grads = vjp((dr1 + dr2, dwv, dk1 + dk2, dv1 + dv2, dav, dbv, dgv))
        dps = jnp.concatenate(grads[:5], axis=1)
        return (dps,) + tuple(grads[5:]) + (jnp.sum(dps * (prev - pv), axis=0, keepdims=True),)

    prep_outs = rw(
        "rwkv_prep_bwd", prep_bwd, R=256,
        tiled=[(p, SHIFT, 0), (dr3, RW, 0), (dr_b, RW, 0), (dw3, RW, 0), (dk3, RW, 0), (dk_b, RW, 0),
               (dv3, RW, 0), (dv_b, RW, 0), (da3, RW, 0), (db3, RW, 0), (dg_, RW, 0)],
        prev=[(p, SHIFT, 0)], full=[mu_shift] + prep_params,
        out_tiled=[(SHIFT, F32)],
        out_acc=[(1, RW), (LW + LA, RW), (1, RW), (LW + LA, RW), (LG, RW), (1, RW), (1, RW), (1, SHIFT)])
    d_ps, g_w0, g_w_up_p, g_a0, g_a_up_p, g_g_up, g_k_k, g_k_a, g_mu = prep_outs

    small = {"mu_shift": g_mu, "rwkv_w0": g_w0, "rwkv_a0": g_a0, "rwkv_k_k": g_k_k,
             "rwkv_k_a": g_k_a, "rwkv_r_k": g_r_k, "rwkv_ln_g": g_ln_g, "rwkv_ln_b": g_ln_b, "s5_a_re": g_a_re,
             "s5_a_im": g_a_im, "s5_log_dt": g_log_dt, "s5_b_re": g_b_re, "s5_b_im": g_b_im, "s5_c_re": g_s5_c_re,
             "s5_c_im": g_s5_c_im, "s5_d": g_s5_d, "norm2_g": g_norm2,
             "ffn_conv_b": jnp.concatenate([g_cb_g, g_cb_u], axis=1), "norm_f_g": g_norm_f}
    small_names = list(small)
    g_conv_w = jnp.concatenate([g_cw_g, g_cw_u], axis=1)
    shard_small = {"rwkv_w_up": g_w_up_p[:LW], "rwkv_a_up": g_a_up_p[LW:], "rwkv_g_up": g_g_up, "ffn_conv_w": g_conv_w}
    parts = [small[n] for n in small_names] + [_to_shards(shard_small[n], ax) for n, _, ax in BIG_SMALL]
    spack = _pack_rows(parts, F32, SUBLANES)
    sm_moves = [(0, 0, lambda ref, me, peer: ref, lambda ref, me, k: ref.at[2 * _chip_of(me) + me[2]])]
    sm = _send_start("gsmall_start", ALL_FLIPS, [spack], [sds((8,) + spack.shape, F32)], sm_moves)
    mu_shift = mu_shift + sm["token"]

    def shift_bwd(dps, nx, mu):
        return dps * (1.0 - mu) + _shift_up(dps * mu, nx * mu, 1)

    (d_p,) = rw("shift_bwd", shift_bwd, R=256, tiled=[(d_ps, SHIFT, 0)], nxt=[(d_ps, SHIFT, 0)], full=[mu_shift],
                out_tiled=[(SHIFT, MXU_DTYPE)])
    g_w_in = jnp.concatenate([_mm_tn(h1, d_p, "g_w_p"), _mm_tn(h1, d_u, "g_w_u"), _mm_tn(h1, d_gates, "g_w_g")], axis=1)
    rsn, rsn_moves = reduce_start("in", BIG[:1], (g_w_in,))
    norm1_g = norm1_g + rsn["token"]
    d_h1 = _mm([d_p, d_u, d_gates], [w_p, w_u, w_g], F32, "d_h1", bt=True)

    def norm1_bwd(xv, dh1, dx1v, sc, sh, g):
        _, vjp = jax.vjp(_norm_mod, xv, g, sc, sh)
        dxn, dg, dsc, dsh = vjp(dh1)
        return dx1v + dxn, dsc, dsh, dg

    grad_x, d_sc1, d_sh1, g_norm1 = rw(
        "norm1_bwd", norm1_bwd, R=256, tiled=[(x2d, D, 0), (d_h1, D, 0), (dx1, D, 0)],
        batch=[(mod, D, SC1), (mod, D, SH1)], full=[norm1_g], out_tiled=[(D, F32)], out_batch=[D, D], out_acc=[(1, D)])

    dmod = jnp.concatenate([d_sh1, d_sc1, d_gt1, d_sh2, d_sc2, d_gt2], axis=2).reshape(Bl, 6 * D)
    last_all = _gather_two_level([], [dmod, g_norm1], "gather_dmod")[1]
    dmod_all = last_all[0].reshape(8 * Bl, 6 * D)
    shn, shn_moves = share_start("in", reduced_halves("in", rsn, rsn_moves, dmod_all))
    dmod_cols = lax.dynamic_slice_in_dim(dmod_all, chip * ncol, ncol, 1)
    g_w_ada, g_b_ada = _ada_bwd(c_all, dmod_cols, dmod_all)

    grads = {"norm1_g": _sum_slots(last_all[1].reshape(8, 1, D), F32, "sum_norm1")}
    sm_own, sm_got = _send_wait("gsmall_wait", ALL_FLIPS, sm, sm_moves, g_b_ada)
    s_all = lax.dynamic_update_slice(sm_got[0], sm_own[0][None], (dev, 0, 0))
    s_sum = _sum_slots(s_all, F32, "sum_gsmall").reshape(-1)
    off = 0
    for n in small_names:
        grads[n] = s_sum[off:off + W[n].size].reshape(W[n].shape)
        off += W[n].size
    for n, shape, axis in BIG_SMALL:
        ss = _shard_shape(shape, axis)
        k4 = 4 * math.prod(ss)
        sh4 = s_sum[off:off + k4].reshape(4, math.prod(ss))
        grads[n] = lax.dynamic_index_in_dim(sh4, chip, 0, keepdims=False).reshape((1,) + ss)
        off += k4

    share_finish("ffn", shl, shl_moves, s_sum, BIG_LATE, grads)
    share_finish("mid", shm, shm_moves, grads[BIG_LATE[0][0]], BIG_MID, grads)
    grads["w_ada"] = g_w_ada[None]
    grads["b_ada"] = g_b_ada

    delta, new_m, new_v = {}, {}, {}
    to2 = lambda z: z.reshape(-1, z.shape[-1])

    def adamw(n):
        d_, m_, v2_ = _adamw(to2(W[n]), to2(grads[n]), to2(M[n]), to2(V[n]), "adamw_" + n)
        delta[n], new_m[n], new_v[n] = (z.reshape(W[n].shape) for z in (d_, m_, v2_))

    for n in ["w_ada"] + [b[0] for b in BIG[1:]]:
        adamw(n)
    rest = [n for n in names if n not in delta and n != "w_in"]
    packs = [_pack_rows([src[n] for n in rest], F32, SUBLANES) for src in (W, grads, M, V)]
    d_, m_, v2_ = _adamw(*packs, "adamw_small")
    shapes = [W[n].shape for n in rest]
    for dst, z in ((delta, d_), (new_m, m_), (new_v, v2_)):
        for n, val in zip(rest, _unpack(z.reshape(-1), shapes)):
            dst[n] = val
    share_finish("in", shn, shn_moves, d_, BIG[:1], grads)
    adamw("w_in")

    return (loss, grad_x.reshape(Bl, S, D), *[grads[n] for n in names], *[delta[n] for n in names],
            *[new_m[n] for n in names], *[new_v[n] for n in names])
```

```python
import functools
import math

import jax
import jax.numpy as jnp
from jax import lax
from jax.experimental import pallas as pl
from jax.experimental.pallas import tpu as pltpu

F32 = jnp.float32
BF16 = jnp.bfloat16
MXU_DTYPE = jnp.bfloat16
MESH_IDS = pl.DeviceIdType.MESH
HIGHEST = lax.Precision.HIGHEST

D = 1024
RW, NH, HD = 512, 8, 64
LW, LA, LG = 64, 64, 128
SW, SGC, NG, SP = 512, 16, 32, 64
NSG = 4
SHIFT = 3 * RW + LW + LA + LG
DFF = 2816
RMS_EPS, GN_EPS, L2_EPS = 1e-6, 64e-5, 1e-12
LR, B1, B2, ADAM_EPS, WD, STEP = 0.001, 0.9, 0.999, 1e-8, 0.01, 10
DECAY_SCALE = math.exp(-0.5)
GELU_C = math.sqrt(2.0 / math.pi)

VMEM_LIMIT = 52 * 1024 * 1024
SUBLANES, LANES = 8, 128
HALO = 16


def _pick(n, cap):
    if n <= cap:
        return n
    best = None
    for t in range(LANES, cap + 1, LANES):
        if n % t == 0:
            best = t
    assert best is not None, (n, cap)
    return best


def _params(sem=None, vmem=VMEM_LIMIT):
    return pltpu.CompilerParams(dimension_semantics=sem, vmem_limit_bytes=vmem)


def _chip_of(p):
    return 2 * p[0] + p[1]


def _me():
    return (lax.axis_index("x"), lax.axis_index("y"), lax.axis_index("c"))


def _half(rows, core):
    h = rows // 2
    return pl.ds(pl.multiple_of(core * h, 16 if h % 16 == 0 else SUBLANES), h)


_HBM =pl.BlockSpec(memory_space=pltpu.HBM)
_SEM = pl.BlockSpec(memory_space=pltpu.SEMAPHORE)
_DATAFLOW = pltpu.SideEffectType.DATAFLOW_SIDE_EFFECTING


def _split_copies(flips, moves, src_refs, land_refs, send_sems, recv_sems):
    me = _me()
    nf = len(flips)
    out = []
    for m, (si, li, src_sel, dst_sel) in enumerate(moves):
        for k, f in enumerate(flips):
            peer = tuple(1 - v if b else v for v, b in zip(me, f))
            out.append(pltpu.make_async_remote_copy(
                src_ref=src_sel(src_refs[si], me, peer), dst_ref=dst_sel(land_refs[li], me, k),
                send_sem=send_sems.at[m * nf + k], recv_sem=recv_sems.at[m * nf + k],
                device_id=peer, device_id_type=MESH_IDS))
    return out


def _send_start(name, flips, srcs, land_shapes, moves):
    ns, nl = len(srcs), len(land_shapes)
    n = len(moves) * len(flips)

    def body(*refs):
        for cp in _split_copies(flips, moves, refs[:ns], refs[ns:ns + nl], refs[ns + nl], refs[ns + nl + 1]):
            cp.start()
        refs[-1][...] = jnp.zeros(refs[-1].shape, F32)

    hbm = lambda z: pltpu.with_memory_space_constraint(z, pltpu.HBM)
    lands = [lax.empty(s.shape, s.dtype) for s in land_shapes]
    res = pl.pallas_call(
        body, name=name,
        out_shape=(pltpu.SemaphoreType.DMA((n,)), pltpu.SemaphoreType.DMA((n,)),
                   *[pltpu.HBM(z.shape, z.dtype) for z in srcs], *[pltpu.HBM(s.shape, s.dtype) for s in land_shapes],
                   jax.ShapeDtypeStruct((SUBLANES, LANES), F32)),
        in_specs=[_HBM] * (ns + nl),
        out_specs=(_SEM, _SEM, *[_HBM] * (ns + nl), pl.BlockSpec(memory_space=pltpu.VMEM)),
        input_output_aliases={i: 2 + i for i in range(ns + nl)},
        compiler_params=pltpu.CompilerParams(has_side_effects=_DATAFLOW),
    )(*[hbm(z) for z in srcs], *[hbm(z) for z in lands])
    return {"sems": res[:2], "srcs": list(res[2:2 + ns]), "lands": list(res[2 + ns:2 + ns + nl]), "token": res[-1][0, 0]}


def _send_wait(name, flips, started, moves, after):
    srcs, lands = started["srcs"], started["lands"]
    ns, nl = len(srcs), len(lands)

    def body(*refs):
        for cp in _split_copies(flips, moves, refs[:ns], refs[ns:ns + nl], refs[ns + nl], refs[ns + nl + 1]):
            cp.wait_send()
            cp.wait_recv()

    res = pl.pallas_call(
        body, name=name, out_shape=[pltpu.HBM(z.shape, z.dtype) for z in srcs + lands],
        in_specs=[_HBM] * (ns + nl) + [_SEM, _SEM, pl.BlockSpec(memory_space=pl.ANY)],
        out_specs=[_HBM] * (ns + nl), input_output_aliases={i: i for i in range(ns + nl)},
        compiler_params=pltpu.CompilerParams(has_side_effects=_DATAFLOW),
    )(*srcs, *lands, *started["sems"], after)
    return list(res[:ns]), list(res[ns:])


CHIP_FLIPS = ((1, 0, 0), (0, 1, 0), (1, 1, 0))
PAIR_FLIPS = ((0, 0, 1),)
ALL_FLIPS = CHIP_FLIPS + ((1, 0, 1), (0, 1, 1), (1, 1, 1)) + PAIR_FLIPS


def _gather_two_level(chip_arrs, dev_arrs, name):
    arrs = list(chip_arrs) + list(dev_arrs)
    n, nchip = len(arrs), len(chip_arrs)
    NS = 7

    def body(*refs):
        srcs, outs = refs[:n], refs[n:2 * n]
        send_sems, recv_sems, loc_sems = refs[2 * n:]
        x, y, c = _me()
        sib = (x, y, 1 - c)
        chips = [(1 - x, y), (x, 1 - y), (1 - x, 1 - y)]
        mine = 2 * x + y
        ids = [2 * cx + cy for cx, cy in chips]

        def part(i, slot, core):
            if i < nchip:
                return outs[i].at[slot, _half(arrs[i].shape[0], core)]
            return outs[i].at[slot, core]

        def rcopy(i, k, src, dst, to):
            return pltpu.make_async_remote_copy(src_ref=src, dst_ref=dst, send_sem=send_sems.at[i * NS + k],
                                                recv_sem=recv_sems.at[i * NS + k], device_id=to, device_id_type=MESH_IDS)

        started, locs = [], []
        for i in range(n):
            own = srcs[i].at[_half(arrs[i].shape[0], c)] if i < nchip else srcs[i]
            loc = pltpu.make_async_copy(srcs[i], outs[i].at[mine] if i < nchip else outs[i].at[mine, c], loc_sems.at[i])
            loc.start()
            locs.append(loc)
            for f, chip in enumerate(chips):
                cp = rcopy(i, f, own, part(i, mine, c), (*chip, c))
                cp.start()
                started.append(cp)
            if i >= nchip:
                cp = rcopy(i, 6, own, part(i, mine, c), sib)
                cp.start()
                started.append(cp)
        for i in range(n):
            for f in range(3):
                land = part(i, ids[f], c)
                rcopy(i, f, land, land, sib).wait_recv()
                fw = rcopy(i, 3 + f, land, land, sib)
                fw.start()
                started.append(fw)
        for i in range(n):
            for f in range(3):
                land = part(i, ids[f], 1 - c)
                rcopy(i, 3 + f, land, land, sib).wait_recv()
            if i >= nchip:
                land = part(i, mine, 1 - c)
                rcopy(i, 6, land, land, sib).wait_recv()
        for cp in started:
            cp.wait_send()
        for loc in locs:
            loc.wait()

    outs = [jax.ShapeDtypeStruct((4,) + a.shape, a.dtype) for a in chip_arrs]
    outs += [jax.ShapeDtypeStruct((4, 2) + a.shape, a.dtype) for a in dev_arrs]
    res = pl.pallas_call(
        body, name=name, out_shape=outs,
        in_specs=[pl.BlockSpec(memory_space=pl.ANY)] * n, out_specs=[pl.BlockSpec(memory_space=pl.ANY)] * n,
        scratch_shapes=[pltpu.SemaphoreType.DMA((n * NS,)), pltpu.SemaphoreType.DMA((n * NS,)),
                        pltpu.SemaphoreType.DMA((n,))],
    )(*arrs)
    return res[:nchip], res[nchip:]


def _mm(As, Bs, out_dtype, name, tm=512, cap=1408, bt=False):
    n = len(As)
    M, N = As[0].shape[0], Bs[0].shape[0 if bt else 1]
    if sum(a.shape[1] for a in As) <= 1024:
        tm = 2 * tm
    tm = min(tm, M)
    tn = _pick(N, cap)
    dims = (((1,), (1,)), ((), ())) if bt else (((1,), (0,)), ((), ()))

    def body(*refs):
        o = refs[2 * n]
        acc = None
        for a, b in zip(refs[:n], refs[n:2 * n]):
            d = lax.dot_general(a[...].astype(MXU_DTYPE), b[...].astype(MXU_DTYPE), dims, preferred_element_type=F32)
            acc = d if acc is None else acc + d
        o[...] = acc.astype(o.dtype)

    in_specs = [pl.BlockSpec((tm, a.shape[1]), lambda i, j: (i, 0)) for a in As]
    if bt:
        in_specs += [pl.BlockSpec((tn, b.shape[1]), lambda i, j: (j, 0)) for b in Bs]
    else:
        in_specs += [pl.BlockSpec((b.shape[0], tn), lambda i, j: (0, j)) for b in Bs]
    return pl.pallas_call(
        body, name=name, grid=(M // tm, N // tn), in_specs=in_specs,
        out_specs=pl.BlockSpec((tm, tn), lambda i, j: (i, j)),
        out_shape=jax.ShapeDtypeStruct((M, N), out_dtype),
        compiler_params=_params(("parallel", "parallel")),
    )(*As, *Bs)


def _mm_tn(A, G, name, tt=1024, cap=1408):
    T, Ka = A.shape
    N = G.shape[1]
    tt = min(tt, T)
    tk = _pick(Ka, cap)
    tn = _pick(N, cap)

    def body(a, g, o):
        @pl.when(pl.program_id(2) == 0)
        def _():
            o[...] = jnp.zeros(o.shape, F32)
        o[...] += lax.dot_general(a[...].astype(MXU_DTYPE), g[...].astype(MXU_DTYPE),
                                  (((0,), (0,)), ((), ())), preferred_element_type=F32)

    return pl.pallas_call(
        body, name=name, grid=(Ka // tk, N // tn, T // tt),
        in_specs=[pl.BlockSpec((tt, tk), lambda i, j, t: (t, i)), pl.BlockSpec((tt, tn), lambda i, j, t: (t, j))],
        out_specs=pl.BlockSpec((tk, tn), lambda i, j, t: (i, j)),
        out_shape=jax.ShapeDtypeStruct((Ka, N), F32),
        compiler_params=_params(("parallel", "parallel", "arbitrary")),
    )(A, G)


def _rowwise(name, fn, *, Bl, S, R, tiled=(), prev=(), nxt=(), batch=(), full=(),
             out_tiled=(), out_batch=(), out_acc=()):
    R = min(R, S)
    nS = S // R
    T = Bl * S
    hb = R // HALO
    n_in = len(tiled) + len(prev) + len(nxt) + len(batch) + len(full)

    in_specs, args = [], []
    for a, wd, cb in tiled:
        in_specs.append(pl.BlockSpec((R, wd), lambda b, i, cb=cb: (b * nS + i, cb)))
        args.append(a)
    for a, wd, cb in prev:
        in_specs.append(pl.BlockSpec((HALO, wd), lambda b, i, cb=cb: (jnp.maximum((b * nS + i) * hb - 1, 0), cb)))
        args.append(a)
    for a, wd, cb in nxt:
        in_specs.append(pl.BlockSpec((HALO, wd), lambda b, i, cb=cb: (jnp.minimum((b * nS + i + 1) * hb, T // HALO - 1), cb)))
        args.append(a)
    for a, wd, cb in batch:
        in_specs.append(pl.BlockSpec((1, 1, wd), lambda b, i, cb=cb: (b, 0, cb)))
        args.append(a)
    for a in full:
        in_specs.append(pl.BlockSpec(a.shape, lambda b, i, nd=a.ndim: (0,) * nd))
        args.append(a)

    out_specs, out_shape = [], []
    for C, dt in out_tiled:
        out_specs.append(pl.BlockSpec((R, C), lambda b, i: (b * nS + i, 0)))
        out_shape.append(jax.ShapeDtypeStruct((T, C), dt))
    for C in out_batch:
        out_specs.append(pl.BlockSpec((1, 1, C), lambda b, i: (b, 0, 0)))
        out_shape.append(jax.ShapeDtypeStruct((Bl, 1, C), F32))
    for shp in out_acc:
        out_specs.append(pl.BlockSpec(shp, lambda b, i, nd=len(shp): (0,) * nd))
        out_shape.append(jax.ShapeDtypeStruct(shp, F32))

    nt, npv, nnx, nbt = len(tiled), len(prev), len(nxt), len(batch)

    def body(*refs):
        b, i = pl.program_id(0), pl.program_id(1)
        ins, outs = refs[:n_in], refs[n_in:]
        vals = [r[...] for r in ins[:nt]]
        vals += [jnp.where(i > 0, r[...], jnp.zeros(r.shape, r.dtype)) for r in ins[nt:nt + npv]]
        vals += [jnp.where(i < nS - 1, r[...], jnp.zeros(r.shape, r.dtype)) for r in ins[nt + npv:nt + npv + nnx]]
        vals += [r[0] for r in ins[nt + npv + nnx:nt + npv + nnx + nbt]]
        vals += [r[...] for r in ins[nt + npv + nnx + nbt:]]
        res = fn(*vals)
        if not isinstance(res, (tuple, list)):
            res = (res,)
        k = 0
        for _ in out_tiled:
            outs[k][...] = res[k].astype(outs[k].dtype)
            k += 1
        for _ in out_batch:
            o = outs[k]

            @pl.when(i == 0)
            def _(o=o):
                o[...] = jnp.zeros(o.shape, F32)
            o[0] += res[k]
            k += 1
        for _ in out_acc:
            o = outs[k]

            @pl.when((i == 0) & (b == 0))
            def _(o=o):
                o[...] = jnp.zeros(o.shape, F32)
            o[...] += res[k]
            k += 1

    out = pl.pallas_call(
        body, name=name, grid=(Bl, nS), in_specs=in_specs, out_specs=out_specs, out_shape=out_shape,
        compiler_params=_params(("arbitrary", "arbitrary")),
    )(*args)
    return out


def _colwise(name, fn, *, Bl, S, R, W, strip, tiled=(), prev=(), nxt=(), full=(), out_tiled=(), n_acc=0):
    R = min(R, S)
    nS = S // R
    T = Bl * S
    hb = R // HALO
    nt, npv, nnx, nfl = len(tiled), len(prev), len(nxt), len(full)
    n_in = nt + npv + nnx + nfl
    in_specs = [pl.BlockSpec((R, W), lambda b, i, cb=cb: (b * nS + i, cb)) for _, cb in tiled]
    in_specs += [pl.BlockSpec((HALO, W), lambda b, i, cb=cb: (jnp.maximum((b * nS + i) * hb - 1, 0), cb)) for _, cb in prev]
    in_specs += [pl.BlockSpec((HALO, W), lambda b, i, cb=cb: (jnp.minimum((b * nS + i + 1) * hb, T // HALO - 1), cb))
                 for _, cb in nxt]
    in_specs += [pl.BlockSpec(a.shape, lambda b, i: (0, 0)) for a in full]
    out_specs = [pl.BlockSpec((R, m * W), lambda b, i: (b * nS + i, 0)) for m, _ in out_tiled]
    out_specs += [pl.BlockSpec((1, W), lambda b, i: (0, 0))] * n_acc
    out_shape = [jax.ShapeDtypeStruct((T, m * W), dt) for m, dt in out_tiled] + [jax.ShapeDtypeStruct((1, W), F32)] * n_acc

    def body(*refs):
        b, i = pl.program_id(0), pl.program_id(1)
        ins, outs = refs[:n_in], refs[n_in:]

        @pl.when((i == 0) & (b == 0))
        def _():
            for o in outs[len(out_tiled):]:
                o[...] = jnp.zeros(o.shape, F32)

        def col(j, carry):
            cs = pl.ds(pl.multiple_of(j * strip, strip), strip)
            vals = [r[:, cs] for r in ins[:nt]]
            vals += [jnp.where(i > 0, r[:, cs], jnp.zeros((HALO, strip), r.dtype)) for r in ins[nt:nt + npv]]
            vals += [jnp.where(i < nS - 1, r[:, cs], jnp.zeros((HALO, strip), r.dtype)) for r in ins[nt + npv:nt + npv + nnx]]
            vals += [r[:, cs] for r in ins[nt + npv + nnx:]]
            res = fn(*vals)
            for k, (m, _) in enumerate(out_tiled):
                for q in range(m):
                    outs[k][:, pl.ds(pl.multiple_of(q * W + j * strip, strip), strip)] = res[k][q].astype(outs[k].dtype)
            for k in range(len(out_tiled), len(outs)):
                outs[k][:, cs] += res[k]
            return carry

        lax.fori_loop(0, W // strip, col, 0)

    return pl.pallas_call(
        body, name=name, grid=(Bl, nS), in_specs=in_specs, out_specs=out_specs, out_shape=out_shape,
        compiler_params=_params(("arbitrary", "arbitrary")),
    )(*[a for a, _ in tiled], *[a for a, _ in prev], *[a for a, _ in nxt], *full)


def _shift_down(x, halo, k):
    rolled = pltpu.roll(x, k, 0)
    row = lax.broadcasted_iota(jnp.int32, (SUBLANES, x.shape[1]), 0)
    head = rolled[0:SUBLANES]
    for j in range(k):
        head = jnp.where(row == j, halo[HALO - k + j:HALO - k + j + 1, :], head)
    return jnp.concatenate([head, rolled[SUBLANES:]], axis=0)


def _shift_up(x, halo, k):
    n = x.shape[0]
    rolled = pltpu.roll(x, n - k, 0)
    row = lax.broadcasted_iota(jnp.int32, (SUBLANES, x.shape[1]), 0)
    tail = rolled[n - SUBLANES:]
    for j in range(k):
        tail = jnp.where(row == SUBLANES - k + j, halo[j:j + 1, :], tail)
    return jnp.concatenate([rolled[:n - SUBLANES], tail], axis=0)


def _dotm(a, b):
    return jnp.dot(a.astype(MXU_DTYPE), b.astype(MXU_DTYPE), preferred_element_type=F32)


def _split_bf16(x):
    hi = x.astype(BF16)
    return hi, (x - hi.astype(F32)).astype(BF16)


def _headsum_2pass(x, hm):
    hi, lo = _split_bf16(x)
    hb = hm.astype(BF16)
    return jnp.dot(hi, hb, preferred_element_type=F32) + jnp.dot(lo, hb, preferred_element_type=F32)


@jax.custom_vjp
def _headsum(x, hm):
    return _headsum_2pass(x, hm)


_headsum.defvjp(lambda x, hm: (_headsum_2pass(x, hm), hm),
                lambda hm, g: (_headsum_2pass(g, hm), jnp.zeros_like(hm)))


def _sigmoid(x):
    return 1.0 / (1.0 + jnp.exp(-x))


def _rms(x, g):
    return x * lax.rsqrt(jnp.mean(x * x, axis=-1, keepdims=True) + RMS_EPS) * g


def _norm_mod(x, g, sc, sh):
    return _rms(x, g) * (1.0 + sc) + sh


def _split_ps(ps):
    return (ps[:, 0:RW], ps[:, RW:2 * RW], ps[:, 2 * RW:3 * RW], ps[:, 3 * RW:3 * RW + LW + LA],
            ps[:, 3 * RW + LW + LA:SHIFT])


def _rwkv_prep(r, k, v, wa, gd, w0, w_up_p, a0, a_up_p, g_up, k_k, k_a, hm):
    w_raw = w0 + _dotm(jnp.tanh(wa), w_up_p)
    decay = jnp.exp(-DECAY_SCALE * _sigmoid(w_raw))
    a = _sigmoid(a0 + _dotm(wa, a_up_p))
    g = _dotm(_sigmoid(gd), g_up)
    kk = k * k_k
    kk = kk * lax.rsqrt(_headsum(kk * kk, hm) + L2_EPS)
    k2 = k * (1.0 + (a - 1.0) * k_a)
    return r, decay, k2, v, -kk, kk * a, g


def _rwkv_post(y, r, k2, v, g, ln_g, ln_b, r_k, hm):
    mean = _headsum(y, hm) * (1.0 / HD)
    yc = y - mean
    var = _headsum(yc * yc, hm) * (1.0 / HD)
    yn = yc * lax.rsqrt(var + GN_EPS) * ln_g + ln_b
    bonus = _headsum(r * k2 * r_k, hm) * v
    return (yn + bonus) * g


def _gelu(x):
    return 0.5 * x * (1.0 + jnp.tanh(GELU_C * (x + 0.044715 * (x * x * x))))


def _s5_post(yssm, u, d):
    return _gelu(yssm + d * u)


def _mix(ga, gb, ya, za, zb):
    return _sigmoid(ga) * ya + _sigmoid(gb) * (za * _sigmoid(zb))


def _conv_act(up_g, up_u, hg, hu, w_g, w_u, b_g, b_u):
    gate, upv = _conv3(up_g, hg, w_g, b_g)[0], _conv3(up_u, hu, w_u, b_u)[0]
    return gate, upv


def _conv3(x, h, w, b):
    x, h = x.astype(F32), h.astype(F32)
    s2, s1 = _shift_down(x, h, 2), _shift_down(x, h, 1)
    return b + w[0:1] * s2 + w[1:2] * s1 + w[2:3] * x, (s2, s1, x)


def _silu_gate(gate, upv):
    return gate * _sigmoid(gate) * upv


WKV_L = 64
_NT, _NN, _TN = ((1,), (1,)), ((1,), (0,)), ((0,), (0,))


def _dotw(x, y, dims):
    return lax.dot_general(x.astype(MXU_DTYPE), y.astype(MXU_DTYPE), (dims, ((), ())), preferred_element_type=F32)


def _dot3(x, y, dims):
    (xh, xl), (yh, yl) = _split_bf16(x), _split_bf16(y)
    d = lambda p, q: lax.dot_general(p, q, (dims, ((), ())), preferred_element_type=F32)
    return d(xh, yh) + d(xh, yl) + d(xl, yh)


@jax.custom_vjp
def _gram3(x, y):
    return _dot3(x, y, _NT)


_gram3.defvjp(lambda x, y: (_dot3(x, y, _NT), (x, y)),
              lambda res, g: (_dot3(g, res[1], _NN), _dot3(g, res[0], _TN)))


def _tri_solve_fwd(ns, xs):
    each = lambda f, *ls: tuple(f(*zs) for zs in zip(*ls))
    size = ns[0].shape[0]
    eye = (lax.broadcasted_iota(jnp.int32, (size, size), 0) == lax.broadcasted_iota(jnp.int32, (size, size), 1)).astype(F32)
    ts = each(lambda n: n + eye, ns)
    qs = ns
    for _ in range(WKV_L.bit_length() - 2):
        qs = each(lambda q: _dotw(q, q, _NN), qs)
        ts = each(lambda t, q: t + _dotw(t, q, _NN), ts, qs)
    us = each(lambda t, x: _dotw(t, x, _NN), ts, xs)
    return us, (ts, us)


def _tri_solve_bwd(res, dus):
    ts, us = res
    each = lambda f, *ls: tuple(f(*zs) for zs in zip(*ls))
    dxs = each(lambda t, du: _dotw(t, du, _TN), ts, dus)
    return each(lambda dx, u: _dotw(dx, u, _NT), dxs, us), dxs


@jax.custom_vjp
def _tri_solve(ns, xs):
    return _tri_solve_fwd(ns, xs)[0]


_tri_solve.defvjp(_tri_solve_fwd, _tri_solve_bwd)


def _wkv_chunk(s0, r, w, k, v, a, b):
    y, s1 = _wkv_chunks((s0,), (r,), (w,), (k,), (v,), (a,), (b,))
    return y[0], s1[0]


def _wkv_chunks(s0, r, w, k, v, a, b):
    each = lambda f, *ls: tuple(f(*xs) for xs in zip(*ls))
    L = r[0].shape[0]
    n2 = 2 * L
    lane_head = lax.broadcasted_iota(jnp.int32, (2, 1, 2 * HD), 2) // HD
    head_mask = (lane_head == lax.broadcasted_iota(jnp.int32, (2, 1, 2 * HD), 0)).astype(F32)
    ri = lax.broadcasted_iota(jnp.int32, (n2, n2), 0)
    ci = lax.broadcasted_iota(jnp.int32, (n2, n2), 1)
    same = (ri // L) == (ci // L)
    strict = same & ((ci % L) < (ri % L))
    incl = same & ((ci % L) <= (ri % L))
    si = lax.broadcasted_iota(jnp.int32, (2 * HD, 2 * HD), 0) // HD
    sj = lax.broadcasted_iota(jnp.int32, (2 * HD, 2 * HD), 1) // HD
    tri = (lax.broadcasted_iota(jnp.int32, (L, L), 0) >= lax.broadcasted_iota(jnp.int32, (L, L), 1)).astype(F32)

    stack = lambda z: (z[None] * head_mask).reshape(n2, 2 * HD)
    dup = lambda z: jnp.broadcast_to(z[None], (2, L, 2 * HD)).reshape(n2, 2 * HD)
    gram = _gram3
    nt, nn, tn = (lambda x, y, d=d: _dotw(x, y, d) for d in (_NT, _NN, _TN))
    add = lambda x, y: x + y

    lw = each(jnp.log, w)
    cum = each(lambda z: jnp.dot(tri, z, preferred_element_type=F32, precision=HIGHEST), lw)
    tot = each(lambda z: jnp.sum(z, axis=0, keepdims=True), lw)
    a2 = each(lambda av, cv, lv: stack(av * jnp.exp(cv - lv)), a, cum, lw)
    r2 = each(lambda rv, cv: stack(rv * jnp.exp(cv)), r, cum)
    v2 = each(stack, v)
    b2 = each(lambda bv, cv: dup(bv * jnp.exp(-cv)), b, cum)
    k2 = each(lambda kv, cv: dup(kv * jnp.exp(-cv)), k, cum)
    n_ab = each(lambda x, y: jnp.where(strict, gram(x, y), 0.0), a2, b2)
    n_ak = each(lambda x, y: jnp.where(strict, gram(x, y), 0.0), a2, k2)
    m_rb = each(lambda x, y: jnp.where(incl, gram(x, y), 0.0), r2, b2)
    m_rk = each(lambda x, y: jnp.where(incl, gram(x, y), 0.0), r2, k2)
    u = _tri_solve(n_ab, each(add, each(nt, a2, s0), each(nn, n_ak, v2)))
    y2 = each(lambda x, y, z: x + y + z, each(nt, r2, s0), each(nn, m_rb, u), each(nn, m_rk, v2))
    y = each(lambda z: jnp.sum(z.reshape(2, L, 2 * HD), axis=0), y2)
    b3 = each(lambda bv, tv, cv: dup(bv * jnp.exp(tv - cv)), b, tot, cum)
    k3 = each(lambda kv, tv, cv: dup(kv * jnp.exp(tv - cv)), k, tot, cum)
    upd = each(add, each(tn, u, b3), each(tn, v2, k3))
    s1 = each(lambda sv, tv, uv: sv * jnp.exp(tv) + jnp.where(si == sj, uv, 0.0), s0, tot, upd)
    return y, s1


NPAIR = NH // 2


def _wkv_nb(Bl):
    return 4 if Bl % 4 == 0 else 2 if Bl % 2 == 0 else 1


def _wkv_fwd(r, w, k, v, a, b, Bl, S):
    L = WKV_L
    nC = S // L
    nb = _wkv_nb(Bl)
    chains = [(bi, p, slice(p * 2 * HD, (p + 1) * 2 * HD)) for bi in range(nb) for p in range(NPAIR)]

    def body(r_ref, w_ref, k_ref, v_ref, a_ref, b_ref, y_ref, ck_ref, s_ref):
        @pl.when(pl.program_id(1) == 0)
        def _():
            s_ref[...] = jnp.zeros(s_ref.shape, F32)
        s0 = tuple(s_ref[bi, p] for bi, p, _ in chains)
        ops = [tuple(z[bi, :, cs] for bi, _, cs in chains) for z in (r_ref, w_ref, k_ref, v_ref, a_ref, b_ref)]
        y, s1 = _wkv_chunks(s0, *ops)
        for i, (bi, p, cs) in enumerate(chains):
            ck_ref[bi, 0, p] = s0[i]
            y_ref[bi, :, cs] = y[i]
            s_ref[bi, p] = s1[i]

    to3 = lambda z: z.reshape(Bl, S, RW)
    row_spec = pl.BlockSpec((nb, L, RW), lambda g, c: (g, c, 0))
    y, ck = pl.pallas_call(
        body, name="wkv_fwd", grid=(Bl // nb, nC), in_specs=[row_spec] * 6,
        out_specs=[row_spec, pl.BlockSpec((nb, 1, NPAIR, 2 * HD, 2 * HD), lambda g, c: (g, c, 0, 0, 0))],
        out_shape=[jax.ShapeDtypeStruct((Bl, S, RW), F32), jax.ShapeDtypeStruct((Bl, nC, NPAIR, 2 * HD, 2 * HD), F32)],
        scratch_shapes=[pltpu.VMEM((nb, NPAIR, 2 * HD, 2 * HD), F32)],
        compiler_params=_params(("arbitrary", "arbitrary")),
    )(*(to3(z) for z in (r, w, k, v, a, b)))
    return y.reshape(Bl * S, RW), ck


def _wkv_bwd(r, w, k, v, a, b, dy, ck, Bl, S):
    L = WKV_L
    nC = S // L
    nb = _wkv_nb(Bl)
    chains = [(bi, p, slice(p * 2 * HD, (p + 1) * 2 * HD)) for bi in range(nb) for p in range(NPAIR)]

    def body(r_ref, w_ref, k_ref, v_ref, a_ref, b_ref, dy_ref, ck_ref,
             dr_ref, dw_ref, dk_ref, dv_ref, da_ref, db_ref, ds_ref):
        @pl.when(pl.program_id(1) == 0)
        def _():
            ds_ref[...] = jnp.zeros(ds_ref.shape, F32)
        s0 = tuple(ck_ref[bi, 0, p] for bi, p, _ in chains)
        ops = [tuple(z[bi, :, cs] for bi, _, cs in chains) for z in (r_ref, w_ref, k_ref, v_ref, a_ref, b_ref)]
        cts = (tuple(dy_ref[bi, :, cs] for bi, _, cs in chains), tuple(ds_ref[bi, p] for bi, p, _ in chains))
        ds0, *grads = jax.vjp(_wkv_chunks, s0, *ops)[1](cts)
        for i, (bi, p, cs) in enumerate(chains):
            ds_ref[bi, p] = ds0[i]
            for o, g in zip((dr_ref, dw_ref, dk_ref, dv_ref, da_ref, db_ref), grads):
                o[bi, :, cs] = g[i]

    to3 = lambda z: z.reshape(Bl, S, RW)
    row_spec = pl.BlockSpec((nb, L, RW), lambda g, c: (g, nC - 1 - c, 0))
    rows = jax.ShapeDtypeStruct((Bl, S, RW), F32)
    outs = pl.pallas_call(
        body, name="wkv_bwd", grid=(Bl // nb, nC),
        in_specs=[row_spec] * 7 + [pl.BlockSpec((nb, 1, NPAIR, 2 * HD, 2 * HD), lambda g, c: (g, nC - 1 - c, 0, 0, 0))],
        out_specs=[row_spec] * 6, out_shape=[rows] * 6,
        scratch_shapes=[pltpu.VMEM((nb, NPAIR, 2 * HD, 2 * HD), F32)],
        compiler_params=_params(("arbitrary", "arbitrary")),
    )(*(to3(z) for z in (r, w, k, v, a, b, dy)), ck)
    return [o.reshape(Bl * S, RW) for o in outs]


NST = NG * SP


def _cmul(ar, ai, br, bi):
    return ar * br - ai * bi, ar * bi + ai * br


def _s5_tiles(are, aim, reverse):
    if reverse:
        aim = -aim
    row = lax.broadcasted_iota(jnp.int32, (SUBLANES, NST), 0)
    pw = [(are, aim)]
    for _ in range(SUBLANES - 1):
        pw.append(_cmul(pw[-1][0], pw[-1][1], are, aim))
    bc = lambda z: jnp.broadcast_to(z, (SUBLANES, NST))
    ms = []
    for kk in (1, 2, 4):
        cond = (row < SUBLANES - kk) if reverse else (row >= kk)
        ms.append((jnp.where(cond, bc(pw[kk - 1][0]), 0.0), jnp.where(cond, bc(pw[kk - 1][1]), 0.0)))
    pr = jnp.zeros((SUBLANES, NST), F32)
    pi = jnp.zeros((SUBLANES, NST), F32)
    for i in range(SUBLANES):
        n = SUBLANES - i if reverse else i + 1
        pr = jnp.where(row == i, bc(pw[n - 1][0]), pr)
        pi = jnp.where(row == i, bc(pw[n - 1][1]), pi)
    return ms, (pr, pi)


def _s5_block(re, im, ms, pc, cre, cim, sg, reverse):
    ln = slice(sg * 512, (sg + 1) * 512)
    for (mr, mi), kk in zip(ms, (1, 2, 4)):
        sh = SUBLANES - kk if reverse else kk
        sre, sim = pltpu.roll(re, sh, 0), pltpu.roll(im, sh, 0)
        tr, ti = _cmul(mr[:, ln], mi[:, ln], sre, sim)
        re, im = re + tr, im + ti
    tr, ti = _cmul(pc[0][:, ln], pc[1][:, ln], cre[:, ln], cim[:, ln])
    return re + tr, im + ti


def _s5_scan(X_ref, n_rows, ms, pc, cre, cim, reverse, visit=None, acc0=None):
    nblk = n_rows // SUBLANES

    def it(i, carry):
        cre, cim, acc = carry
        j = nblk - 1 - i if reverse else i
        rows = pl.ds(pl.multiple_of(j * SUBLANES, SUBLANES), SUBLANES)
        edge = 0 if reverse else SUBLANES - 1
        blocks, ncre, ncim = [], [], []
        for sg in range(NSG):
            lr = slice(sg * 1024, sg * 1024 + 512)
            li = slice(sg * 1024 + 512, (sg + 1) * 1024)
            re, im = _s5_block(X_ref[rows, lr], X_ref[rows, li], ms, pc, cre, cim, sg, reverse)
            X_ref[rows, lr] = re
            X_ref[rows, li] = im
            blocks.append((re, im))
            ncre.append(re[edge:edge + 1])
            ncim.append(im[edge:edge + 1])
        if visit is not None:
            acc = visit(j, blocks, acc)
        return jnp.concatenate(ncre, axis=1), jnp.concatenate(ncim, axis=1), acc

    return lax.fori_loop(0, nblk, it, (cre, cim, acc0 if acc0 is not None else 0))


def _s5_fwd(u, wb, wc, ab, d, Bl, S, R=256):
    R = min(R, S)
    nC = S // R

    def body(u_ref, wb_ref, wc_ref, ab_ref, d_ref, y_ref, st_ref, X_ref, o_ref, car_ref):
        @pl.when(pl.program_id(1) == 0)
        def _():
            car_ref[...] = jnp.zeros(car_ref.shape, F32)
        st_ref[0, 0] = car_ref[...]
        ms, pc = _s5_tiles(ab_ref[0:1], ab_ref[1:2], False)
        for sg in range(NSG):
            X_ref[:, sg * 1024:(sg + 1) * 1024] = _dotm(u_ref[:, sg * 128:(sg + 1) * 128], wb_ref[sg])
        cre, cim, _ = _s5_scan(X_ref, R, ms, pc, car_ref[0:1], car_ref[1:2], False)
        car_ref[0:1] = cre
        car_ref[1:2] = cim
        for sg in range(NSG):
            y_ref[:, sg * 128:(sg + 1) * 128] = _dotm(X_ref[:, sg * 1024:(sg + 1) * 1024], wc_ref[sg])
        o_ref[...] = _s5_post(y_ref[...], u_ref[...], d_ref[...]).astype(o_ref.dtype)

    rows = pl.BlockSpec((R, SW), lambda b, c: (b * nC + c, 0))
    return pl.pallas_call(
        body, name="s5_fwd", grid=(Bl, nC),
        in_specs=[rows, pl.BlockSpec(wb.shape, lambda b, c: (0, 0, 0)), pl.BlockSpec(wc.shape, lambda b, c: (0, 0, 0)),
                  pl.BlockSpec(ab.shape, lambda b, c: (0, 0)), pl.BlockSpec(d.shape, lambda b, c: (0, 0))],
        out_specs=[rows, pl.BlockSpec((1, 1, 2, NST), lambda b, c: (b, c, 0, 0)),
                   pl.BlockSpec((R, 2 * NST), lambda b, c: (b * nC + c, 0)), rows],
        out_shape=[jax.ShapeDtypeStruct((Bl * S, SW), F32), jax.ShapeDtypeStruct((Bl, nC, 2, NST), F32),
                   jax.ShapeDtypeStruct((Bl * S, 2 * NST), F32), jax.ShapeDtypeStruct((Bl * S, SW), MXU_DTYPE)],
        scratch_shapes=[pltpu.VMEM((2, NST), F32)],
        compiler_params=_params(("arbitrary", "arbitrary")),
    )(u, wb, wc, ab, d)


def _s5_bwd(u, y, do, d, wb, wc, ab, st, xs, Bl, S, R=256):
    R = min(R, S)
    nC = S // R

    def body(u_ref, y_ref, do_ref, d_ref, wb_ref, wc_ref, ab_ref, st_ref, X_ref,
             du_ref, dwb_ref, dwc_ref, dab_ref, dd_ref, G_ref, car_ref):
        first = (pl.program_id(0) == 0) & (pl.program_id(1) == 0)

        @pl.when(first)
        def _():
            for o in (dwb_ref, dwc_ref, dab_ref, dd_ref):
                o[...] = jnp.zeros(o.shape, F32)

        @pl.when(pl.program_id(1) == 0)
        def _():
            car_ref[...] = jnp.zeros(car_ref.shape, F32)

        are, aim = ab_ref[0:1], ab_ref[1:2]
        dy, du_direct, dd = jax.vjp(_s5_post, y_ref[...], u_ref[...], d_ref[...])[1](do_ref[...])
        dd_ref[...] += dd
        dyv = dy.astype(MXU_DTYPE)
        for sg in range(NSG):
            G_ref[:, sg * 1024:(sg + 1) * 1024] = lax.dot_general(
                dyv[:, sg * 128:(sg + 1) * 128], wc_ref[sg].astype(MXU_DTYPE), (((1,), (1,)), ((), ())),
                preferred_element_type=F32)
        rms_, rpc = _s5_tiles(are, aim, True)
        row = lax.broadcasted_iota(jnp.int32, (SUBLANES, 512), 0)

        def visit(j, blocks, acc):
            before = pl.multiple_of(jnp.maximum(j - 1, 0) * SUBLANES, SUBLANES)
            prow = X_ref[pl.ds(before, SUBLANES), :][SUBLANES - 1:SUBLANES]
            rows = pl.ds(pl.multiple_of(j * SUBLANES, SUBLANES), SUBLANES)
            are_acc, aim_acc = [], []
            for sg in range(NSG):
                lr = slice(sg * 1024, sg * 1024 + 512)
                li = slice(sg * 1024 + 512, (sg + 1) * 1024)
                ln = slice(sg * 512, (sg + 1) * 512)
                pre = jnp.where(j > 0, prow[:, lr], st_ref[0, 0, 0:1, ln])
                pim = jnp.where(j > 0, prow[:, li], st_ref[0, 0, 1:2, ln])
                xre = jnp.where(row == 0, pre, pltpu.roll(X_ref[rows, lr], 1, 0))
                xim = jnp.where(row == 0, pim, pltpu.roll(X_ref[rows, li], 1, 0))
                dre, dim = blocks[sg]
                are_acc.append(dre * xre + dim * xim)
                aim_acc.append(dim * xre - dre * xim)
            return acc[0] + jnp.concatenate(are_acc, axis=1), acc[1] + jnp.concatenate(aim_acc, axis=1)

        zero = jnp.zeros((SUBLANES, NST), F32)
        cre, cim, acc = _s5_scan(G_ref, R, rms_, rpc, car_ref[0:1], car_ref[1:2], True, visit, (zero, zero))
        car_ref[0:1] = cre
        car_ref[1:2] = cim
        dab_ref[0:1] += jnp.sum(acc[0], axis=0, keepdims=True)
        dab_ref[1:2] += jnp.sum(acc[1], axis=0, keepdims=True)
        uv = u_ref[...].astype(MXU_DTYPE)
        for sg in range(NSG):
            cs = slice(sg * 1024, (sg + 1) * 1024)
            us = slice(sg * 128, (sg + 1) * 128)
            gx = G_ref[:, cs].astype(MXU_DTYPE)
            dwb_ref[sg] += lax.dot_general(uv[:, us], gx, (((0,), (0,)), ((), ())), preferred_element_type=F32)
            dwc_ref[sg] += lax.dot_general(X_ref[:, cs].astype(MXU_DTYPE), dyv[:, us], (((0,), (0,)), ((), ())),
                                           preferred_element_type=F32)
            du_ssm = lax.dot_general(gx, wb_ref[sg].astype(MXU_DTYPE), (((1,), (1,)), ((), ())),
                                     preferred_element_type=F32)
            du_ref[:, us] = (du_ssm + du_direct[:, us]).astype(du_ref.dtype)

    rmap = lambda b, c: (b * nC + nC - 1 - c, 0)
    rows = pl.BlockSpec((R, SW), rmap)
    return pl.pallas_call(
        body, name="s5_bwd", grid=(Bl, nC),
        in_specs=[rows, rows, rows, pl.BlockSpec(d.shape, lambda b, c: (0, 0)),
                  pl.BlockSpec(wb.shape, lambda b, c: (0, 0, 0)), pl.BlockSpec(wc.shape, lambda b, c: (0, 0, 0)),
                  pl.BlockSpec(ab.shape, lambda b, c: (0, 0)),
                  pl.BlockSpec((1, 1, 2, NST), lambda b, c: (b, nC - 1 - c, 0, 0)),
                  pl.BlockSpec((R, 2 * NST), rmap)],
        out_specs=[rows, pl.BlockSpec(wb.shape, lambda b, c: (0, 0, 0)),
                   pl.BlockSpec(wc.shape, lambda b, c: (0, 0, 0)), pl.BlockSpec((2, NST), lambda b, c: (0, 0)),
                   pl.BlockSpec(d.shape, lambda b, c: (0, 0))],
        out_shape=[jax.ShapeDtypeStruct((Bl * S, SW), MXU_DTYPE), jax.ShapeDtypeStruct(wb.shape, F32),
                   jax.ShapeDtypeStruct(wc.shape, F32), jax.ShapeDtypeStruct((2, NST), F32),
                   jax.ShapeDtypeStruct(d.shape, F32)],
        scratch_shapes=[pltpu.VMEM((R, 2 * NST), F32), pltpu.VMEM((2, NST), F32)],
        compiler_params=_params(("arbitrary", "arbitrary")),
    )(u, y, do, d, wb, wc, ab, st, xs)


def _s5_disc_math(a_re, a_im, log_dt, b_re, b_im, expand):
    dt = jnp.exp(log_dt)
    z_re, z_im = a_re * dt, a_im * dt
    mag = jnp.exp(z_re)
    ab_re, ab_im = mag * jnp.cos(z_im), mag * jnp.sin(z_im)
    den = a_re * a_re + a_im * a_im
    q_re = ((ab_re - 1.0) * a_re + ab_im * a_im) / den
    q_im = (ab_im * a_re - (ab_re - 1.0) * a_im) / den
    qe_re = jnp.dot(q_re, expand, preferred_element_type=F32, precision=HIGHEST)
    qe_im = jnp.dot(q_im, expand, preferred_element_type=F32, precision=HIGHEST)
    return ab_re, ab_im, qe_re * b_re - qe_im * b_im, qe_re * b_im + qe_im * b_re


def _whole(shape):
    return pl.BlockSpec(shape, lambda nd=len(shape): (0,) * nd)


def _s5_disc(a_re, a_im, log_dt, b_re, b_im, expand):
    def body(a, b, c, d, e, f, o0, o1, o2, o3):
        res = _s5_disc_math(a[...], b[...], c[...], d[...], e[...], f[...])
        for o, v in zip((o0, o1, o2, o3), res):
            o[...] = v
    ins = (a_re, a_im, log_dt, b_re, b_im, expand)
    outs = [jax.ShapeDtypeStruct(a_re.shape, F32)] * 2 + [jax.ShapeDtypeStruct(b_re.shape, F32)] * 2
    return pl.pallas_call(body, name="s5_disc", in_specs=[_whole(x.shape) for x in ins],
                          out_specs=[_whole(o.shape) for o in outs], out_shape=outs)(*ins)


def _s5_disc_bwd(a_re, a_im, log_dt, b_re, b_im, expand, cts):
    def body(a, b, c, d, e, f, g0, g1, g2, g3, o0, o1, o2, o3, o4):
        fn = lambda *p: _s5_disc_math(*p, f[...])
        _, vjp = jax.vjp(fn, a[...], b[...], c[...], d[...], e[...])
        for o, v in zip((o0, o1, o2, o3, o4), vjp((g0[...], g1[...], g2[...], g3[...]))):
            o[...] = v
    ins = (a_re, a_im, log_dt, b_re, b_im, expand) + tuple(cts)
    outs = [jax.ShapeDtypeStruct(x.shape, F32) for x in (a_re, a_im, log_dt, b_re, b_im)]
    return pl.pallas_call(body, name="s5_disc_bwd", in_specs=[_whole(x.shape) for x in ins],
                          out_specs=[_whole(o.shape) for o in outs], out_shape=outs)(*ins)


def _ada_fwd(c_all, w_shard, b_shard):
    def body(c_ref, w_ref, b_ref, o_ref):
        cv = c_ref[...]
        o_ref[...] = _dotm(cv * _sigmoid(cv), w_ref[...]) + b_ref[...]
    n = w_shard.shape[1]
    return pl.pallas_call(
        body, name="ada_fwd", in_specs=[_whole(c_all.shape), _whole(w_shard.shape), _whole(b_shard.shape)],
        out_specs=_whole((c_all.shape[0], n)), out_shape=jax.ShapeDtypeStruct((c_all.shape[0], n), F32),
        compiler_params=_params(),
    )(c_all, w_shard, b_shard)


def _ada_bwd(c_all, dmod_cols, dmod_all):
    def body(c_ref, dc_ref, da_ref, gw_ref, gb_ref):
        cv = c_ref[...]
        gw_ref[...] = lax.dot_general((cv * _sigmoid(cv)).astype(MXU_DTYPE), dc_ref[...].astype(MXU_DTYPE),
                                      (((0,), (0,)), ((), ())), preferred_element_type=F32)
        gb_ref[...] = jnp.sum(da_ref[...], axis=0, keepdims=True)
    n = dmod_cols.shape[1]
    return pl.pallas_call(
        body, name="ada_bwd", in_specs=[_whole(c_all.shape), _whole(dmod_cols.shape), _whole(dmod_all.shape)],
        out_specs=[_whole((D, n)), _whole((1, dmod_all.shape[1]))],
        out_shape=[jax.ShapeDtypeStruct((D, n), F32), jax.ShapeDtypeStruct((1, dmod_all.shape[1]), F32)],
        compiler_params=_params(),
    )(c_all, dmod_cols, dmod_all)


def _rows_block(n_rows, cap=512):
    if n_rows <= cap:
        return n_rows
    for t in range(cap - cap % SUBLANES, 0, -SUBLANES):
        if n_rows % t == 0:
            return t
    return n_rows


def _adamw(w, g, m, v, name):
    rows, cols = w.shape
    tr = _rows_block(rows, max(SUBLANES, (1 << 19) // max(cols, 1) // SUBLANES * SUBLANES))

    def body(w_ref, g_ref, m_ref, v_ref, d_ref, nm_ref, nv_ref):
        gv = g_ref[...]
        nm = B1 * m_ref[...] + (1.0 - B1) * gv
        nv = B2 * v_ref[...] + (1.0 - B2) * (gv * gv)
        m_hat = nm / (1.0 - B1 ** STEP)
        v_hat = nv / (1.0 - B2 ** STEP)
        d_ref[...] = -LR * (m_hat / (jnp.sqrt(v_hat) + ADAM_EPS) + WD * w_ref[...])
        nm_ref[...] = nm
        nv_ref[...] = nv

    spec = pl.BlockSpec((tr, cols), lambda i: (i, 0))
    sd = jax.ShapeDtypeStruct((rows, cols), F32)
    return pl.pallas_call(body, name=name, grid=(rows // tr,), in_specs=[spec] * 4, out_specs=[spec] * 3,
                          out_shape=[sd] * 3, compiler_params=_params(("parallel",)))(w, g, m, v)


def _sum_slots(x, out_dtype, name):
    xs = x if isinstance(x, (list, tuple)) else [x]
    _, rows, cols = xs[0].shape
    tr = _rows_block(rows)

    def body(*refs):
        acc = None
        for x_ref in refs[:-1]:
            for j in range(x_ref.shape[0]):
                term = x_ref[j].astype(F32)
                acc = term if acc is None else acc + term
        refs[-1][...] = acc.astype(refs[-1].dtype)

    return pl.pallas_call(
        body, name=name, grid=(rows // tr,),
        in_specs=[pl.BlockSpec((z.shape[0], tr, cols), lambda i: (0, i, 0)) for z in xs],
        out_specs=pl.BlockSpec((tr, cols), lambda i: (i, 0)), out_shape=jax.ShapeDtypeStruct((rows, cols), out_dtype),
        compiler_params=_params(("parallel",)))(*xs)


PACK_COLS = 1024


def _pack_rows(parts, dtype, row_mult):
    flat = jnp.concatenate([p.reshape(-1).astype(dtype) for p in parts])
    per = PACK_COLS * row_mult
    n = -(-flat.shape[0] // per) * per
    flat = jnp.pad(flat, (0, n - flat.shape[0]))
    return flat.reshape(n // PACK_COLS, PACK_COLS)


def _unpack(flat, shapes):
    out, off = [], 0
    for s in shapes:
        n = math.prod(s)
        out.append(flat[off:off + n].reshape(s))
        off += n
    return out


BIG = (("w_in", (D, SHIFT + SW + 2 * D), 1), ("w_out_rwkv", (RW, D), 1), ("w_glu", (SW, 2 * D), 1),
       ("w_out", (D, D), 0), ("w_ffn_up", (D, 2 * DFF), 1), ("w_ffn_down", (DFF, D), 0))
BIG_SMALL = (("rwkv_w_up", (LW, RW), 1), ("rwkv_a_up", (LA, RW), 1), ("rwkv_g_up", (LG, RW), 1),
             ("ffn_conv_w", (3, 2 * DFF), 1))
BIG_LATE = BIG[4:]
BIG_MID = BIG[1:4]


def _shard_shape(shape, axis):
    return (shape[0] // 4, shape[1]) if axis == 0 else (shape[0], shape[1] // 4)


def _to_shards(g, axis):
    r, C = g.shape
    return g.reshape(4, r // 4, C) if axis == 0 else g.reshape(r, 4, C // 4).transpose(1, 0, 2)


def _from_shards(x, axis):
    _, r, C = x.shape
    return x.reshape(4 * r, C) if axis == 0 else x.transpose(1, 0, 2).reshape(r, 4 * C)


def kernel(x, c, w_ada, b_ada, norm1_g, w_in, mu_shift, rwkv_w0, rwkv_w_up, rwkv_a0, rwkv_a_up, rwkv_g_up, rwkv_k_k, rwkv_k_a, rwkv_r_k, rwkv_ln_g, rwkv_ln_b, w_out_rwkv, s5_a_re, s5_a_im, s5_log_dt, s5_b_re, s5_b_im, s5_c_re, s5_c_im, s5_d, w_glu, w_out, norm2_g, w_ffn_up, ffn_conv_w, ffn_conv_b, w_ffn_down, norm_f_g, loss_target, m_w_ada, m_b_ada, m_norm1_g, m_w_in, m_mu_shift, m_rwkv_w0, m_rwkv_w_up, m_rwkv_a0, m_rwkv_a_up, m_rwkv_g_up, m_rwkv_k_k, m_rwkv_k_a, m_rwkv_r_k, m_rwkv_ln_g, m_rwkv_ln_b, m_w_out_rwkv, m_s5_a_re, m_s5_a_im, m_s5_log_dt, m_s5_b_re, m_s5_b_im, m_s5_c_re, m_s5_c_im, m_s5_d, m_w_glu, m_w_out, m_norm2_g, m_w_ffn_up, m_ffn_conv_w, m_ffn_conv_b, m_w_ffn_down, m_norm_f_g, v_w_ada, v_b_ada, v_norm1_g, v_w_in, v_mu_shift, v_rwkv_w0, v_rwkv_w_up, v_rwkv_a0, v_rwkv_a_up, v_rwkv_g_up, v_rwkv_k_k, v_rwkv_k_a, v_rwkv_r_k, v_rwkv_ln_g, v_rwkv_ln_b, v_w_out_rwkv, v_s5_a_re, v_s5_a_im, v_s5_log_dt, v_s5_b_re, v_s5_b_im, v_s5_c_re, v_s5_c_im, v_s5_d, v_w_glu, v_w_out, v_norm2_g, v_w_ffn_up, v_ffn_conv_w, v_ffn_conv_b, v_w_ffn_down, v_norm_f_g):
    names = ["w_ada", "b_ada", "norm1_g", "w_in", "mu_shift", "rwkv_w0", "rwkv_w_up", "rwkv_a0", "rwkv_a_up",
             "rwkv_g_up", "rwkv_k_k", "rwkv_k_a", "rwkv_r_k", "rwkv_ln_g", "rwkv_ln_b", "w_out_rwkv", "s5_a_re",
             "s5_a_im", "s5_log_dt", "s5_b_re", "s5_b_im", "s5_c_re", "s5_c_im", "s5_d", "w_glu", "w_out", "norm2_g",
             "w_ffn_up", "ffn_conv_w", "ffn_conv_b", "w_ffn_down", "norm_f_g"]
    env = dict(locals())
    W = {n: env[n] for n in names}
    M = {n: env["m_" + n] for n in names}
    V = {n: env["v_" + n] for n in names}

    Bl, S, _ = x.shape
    T = Bl * S
    ix, iy, ic = lax.axis_index("x"), lax.axis_index("y"), lax.axis_index("c")
    chip = 2 * ix + iy
    dev = 2 * chip + ic
    rw = functools.partial(_rowwise, Bl=Bl, S=S)

    now = [b for b in BIG if b not in BIG_LATE]
    chip_arrs = [W[n][0].astype(MXU_DTYPE) for n, _, _ in now] + [W[n][0] for n, _, _ in BIG_SMALL[:3]]
    got_chip, got_dev = _gather_two_level(chip_arrs, [W["ffn_conv_w"][0], c], "gather_w")
    full = {n: _from_shards(g, axis) for (n, _, axis), g in zip(tuple(now) + BIG_SMALL[:3], got_chip)}
    full["ffn_conv_w"] = _from_shards(got_dev[0][:, 0], 1)
    c_all = got_dev[1].reshape(8 * Bl, D)
    w_p, w_u, w_g = full["w_in"][:, :SHIFT], full["w_in"][:, SHIFT:SHIFT + SW], full["w_in"][:, SHIFT + SW:]
    zeros_l = jnp.zeros((LW, RW), F32)
    w_up_p = jnp.concatenate([full["rwkv_w_up"], zeros_l], axis=0)
    a_up_p = jnp.concatenate([zeros_l, full["rwkv_a_up"]], axis=0)
    g_up = full["rwkv_g_up"]
    conv_w = full["ffn_conv_w"]
    conv_wg, conv_wu = conv_w[:, :DFF], conv_w[:, DFF:]
    conv_bg, conv_bu = ffn_conv_b[:, :DFF], ffn_conv_b[:, DFF:]
    hm = jnp.kron(jnp.eye(NH, dtype=F32), jnp.ones((HD, HD), F32))

    ncol = 6 * D // 4
    b_ada_cols = lax.dynamic_slice_in_dim(b_ada, chip * ncol, ncol, 1)
    mod_part = _ada_fwd(c_all, w_ada[0], b_ada_cols)
    mod4 = _gather_two_level([], [mod_part], "gather_mod")[1][0][:, 0]
    mod4, late = lax.optimization_barrier((mod4, [W[n][0].astype(MXU_DTYPE) for n, _, _ in BIG_LATE]))
    late_moves = [(i, i, lambda ref, me, peer: ref, lambda ref, me, k: ref.at[_chip_of(me)]) for i in range(len(late))]
    late_start = _send_start("gather_ffn_start", CHIP_FLIPS, late,
                             [jax.ShapeDtypeStruct((4,) + z.shape, z.dtype) for z in late], late_moves)
    norm1_g = norm1_g + late_start["token"]
    mod = lax.dynamic_slice_in_dim(mod4, dev * Bl, Bl, 1).transpose(1, 0, 2).reshape(Bl, 1, 6 * D)
    SH1, SC1, GT1, SH2, SC2, GT2 = range(6)

    x2d = x.reshape(T, D)
    tgt = loss_target.reshape(T, D)

    (h1,) = rw("norm1", lambda xv, sc, sh, g: _norm_mod(xv, g, sc, sh), R=512, tiled=[(x2d, D, 0)],
               batch=[(mod, D, SC1), (mod, D, SH1)], full=[norm1_g], out_tiled=[(D, MXU_DTYPE)])
    p = _mm([h1], [w_p], F32, "proj_p")
    u = _mm([h1], [w_u], F32, "proj_u")
    gates = _mm([h1], [w_g], F32, "proj_g")

    prep_params = [rwkv_w0, w_up_p, rwkv_a0, a_up_p, g_up, rwkv_k_k, rwkv_k_a, hm]

    def prep_fwd(pv, ph, mu, *pp):
        ps = pv + (_shift_down(pv, ph, 1) - pv) * mu
        return _rwkv_prep(*_split_ps(ps), *pp)

    r_, w_, k_, v_, a_, b_, g_ = rw("rwkv_prep", prep_fwd, R=256, tiled=[(p, SHIFT, 0)], prev=[(p, SHIFT, 0)],
                                    full=[mu_shift] + prep_params, out_tiled=[(RW, F32)] * 7)
    y_wkv, ck = _wkv_fwd(r_, w_, k_, v_, a_, b_, Bl, S)
    r_k_row = rwkv_r_k.reshape(1, RW)
    post_params = [rwkv_ln_g, rwkv_ln_b, r_k_row, hm]
    (o_rwkv,) = rw("rwkv_post", _rwkv_post, R=256,
                   tiled=[(y_wkv, RW, 0), (r_, RW, 0), (k_, RW, 0), (v_, RW, 0), (g_, RW, 0)],
                   full=post_params, out_tiled=[(RW, MXU_DTYPE)])
    y_a = _mm([o_rwkv], [full["w_out_rwkv"]], F32, "out_rwkv")

    expand = jnp.kron(jnp.eye(SP, dtype=F32), jnp.ones((1, SGC), F32))
    s5_in = (s5_a_re[0], s5_a_im[0], s5_log_dt[0].reshape(NG, 1), s5_b_re[0].reshape(NG, SP * SGC),
             s5_b_im[0].reshape(NG, SP * SGC), expand)
    ab_re, ab_im, bb_re, bb_im = _s5_disc(*s5_in)
    eye8 = jnp.eye(8, dtype=F32)

    def blockdiag_in(bb):
        t = bb.reshape(NSG, 8, SP, SGC)
        return jnp.einsum("ab,sapc->sacbp", eye8, t).reshape(NSG, 128, 512)

    def blockdiag_out(cc):
        t = cc.reshape(NSG, 8, SGC, SP)
        return jnp.einsum("ab,sacp->sapbc", eye8, t).reshape(NSG, 512, 128)

    wb = jnp.concatenate([blockdiag_in(bb_re), blockdiag_in(bb_im)], axis=2).astype(MXU_DTYPE)
    wc = jnp.concatenate([blockdiag_out(s5_c_re[0]), -blockdiag_out(s5_c_im[0])], axis=1).astype(MXU_DTYPE)
    ab = jnp.stack([ab_re.reshape(NST), ab_im.reshape(NST)])
    y_ssm, s5_st, s5_x, s5o = _s5_fwd(u, wb, wc, ab, s5_d, Bl, S)
    z = _mm([s5o], [full["w_glu"]], F32, "glu")
    mix_tiled = [(gates, D, 0), (gates, D, 1), (y_a, D, 0), (z, D, 0), (z, D, 1)]
    (mixed_in,) = rw("mix", _mix, R=256, tiled=mix_tiled, out_tiled=[(D, MXU_DTYPE)])
    mixed = _mm([mixed_in], [full["w_out"]], F32, "out_proj")

    def norm2_fwd(xv, mx, gt, sc, sh, g):
        x1 = xv + gt * mx
        return x1, _norm_mod(x1, g, sc, sh)

    x1, h2 = rw("norm2", norm2_fwd, R=512, tiled=[(x2d, D, 0), (mixed, D, 0)],
                batch=[(mod, D, GT1), (mod, D, SC2), (mod, D, SH2)], full=[norm2_g],
                out_tiled=[(D, F32), (D, MXU_DTYPE)])
    late_own, late_got = _send_wait("gather_ffn_wait", CHIP_FLIPS, late_start, late_moves, h2)
    for (n, _, axis), own, got in zip(BIG_LATE, late_own, late_got):
        full[n] = _from_shards(lax.dynamic_update_slice(got, own[None], (chip, 0, 0)), axis)
    up = _mm([h2], [full["w_ffn_up"]], MXU_DTYPE, "ffn_up")
    conv_tiled = [(up, 0), (up, 1)]
    conv_full = [conv_wg, conv_wu, conv_bg, conv_bu]
    cw = functools.partial(_colwise, Bl=Bl, S=S, R=128, W=DFF, strip=LANES)

    def act_fwd(*a):
        return ((_silu_gate(*_conv_act(*a)),),)

    (act,) = cw("ffn_act", act_fwd, tiled=conv_tiled, prev=conv_tiled, full=conv_full, out_tiled=[(1, MXU_DTYPE)])
    ffn = _mm([act], [full["w_ffn_down"]], F32, "ffn_down")

    def head(x1v, fv, tv, gt, g):
        x2 = x1v + gt * fv
        y, vjp = jax.vjp(_rms, x2, g)
        e = y - tv
        dx2, dg = vjp(e * (1.0 / D))
        loss = jnp.sum(e * e, keepdims=True) * jnp.ones((1, LANES), F32)
        return dx2, dx2 * gt, jnp.sum(dx2 * fv, axis=0, keepdims=True), dg.reshape(1, D), loss

    dx2, d_ffn, d_gt2, g_norm_f, loss_acc = rw(
        "head", head, R=512, tiled=[(x1, D, 0), (ffn, D, 0), (tgt, D, 0)], batch=[(mod, D, GT2)],
        full=[norm_f_g.reshape(1, D)], out_tiled=[(D, F32), (D, MXU_DTYPE)], out_batch=[D],
        out_acc=[(1, D), (1, LANES)])
    loss = lax.psum(0.5 / D * loss_acc[0, 0], ("x", "y", "c"))

    d_act = _mm([d_ffn], [full["w_ffn_down"]], F32, "d_act", bt=True)
    g_w_ffn_down = _mm_tn(act, d_ffn, "g_ffn_down")

    def act_bwd(ug, uu, dact, hg, hu, wg, wu, bg, bu):
        (gate, taps_g), (upv, taps_u) = _conv3(ug, hg, wg, bg), _conv3(uu, hu, wu, bu)
        _, vjp_s = jax.vjp(_silu_gate, gate, upv)
        d_gate, d_upv = vjp_s(dact)
        def taps(dh, shifted):
            return [jnp.sum(dh * s, axis=0, keepdims=True) for s in shifted] + [jnp.sum(dh, axis=0, keepdims=True)]
        return ((d_gate,), (d_upv,), *taps(d_gate, taps_g), *taps(d_upv, taps_u))

    dh_g, dh_u, *tapg = cw("ffn_act_bwd", act_bwd, tiled=conv_tiled + [(d_act, 0)], prev=conv_tiled, full=conv_full,
                           out_tiled=[(1, MXU_DTYPE), (1, MXU_DTYPE)], n_acc=8)
    g_cw_g, g_cb_g = jnp.concatenate(tapg[0:3], axis=0), tapg[3]
    g_cw_u, g_cb_u = jnp.concatenate(tapg[4:7], axis=0), tapg[7]

    def conv_t(dg, du_, ng, nu, wg, wu):
        dg, du_, ng, nu = (z.astype(F32) for z in (dg, du_, ng, nu))

        def ct(d, n, w):
            return w[2:3] * d + w[1:2] * _shift_up(d, n, 1) + w[0:1] * _shift_up(d, n, 2)
        return ((ct(dg, ng, wg), ct(du_, nu, wu)),)

    (d_up,) = cw("conv_bwd", conv_t, tiled=[(dh_g, 0), (dh_u, 0)], nxt=[(dh_g, 0), (dh_u, 0)],
                 full=[conv_wg, conv_wu], out_tiled=[(2, MXU_DTYPE)])
    d_h2 = _mm([d_up], [full["w_ffn_up"]], F32, "d_h2", bt=True)
    g_w_ffn_up = _mm_tn(h2, d_up, "g_ffn_up")

    sds = jax.ShapeDtypeStruct
    reduce_src = lambda r: (lambda ref, me, peer: ref.at[_chip_of(peer), _half(r, peer[2])])

    def reduced_halves(tag, started, moves, after):
        gsh_own, got = _send_wait("rs_%s_wait" % tag, ALL_FLIPS, started, moves, after)
        halves = []
        for i, (g, gt) in enumerate(zip(gsh_own, got)):
            h = g.shape[1] // 2
            own = lax.dynamic_slice(g, (chip, ic * h, 0), (1, h, g.shape[2]))
            halves.append(_sum_slots([own, gt], F32, "rs_%s_sum%d" % (tag, i)))
        return halves

    def share_start(tag, halves):
        moves = [(i, i, lambda ref, me, peer: ref, lambda ref, me, k: ref) for i in range(len(halves))]
        return _send_start("share_%s_start" % tag, PAIR_FLIPS, halves, [sds(g.shape, F32) for g in halves], moves), moves

    def share_finish(tag, started, moves, after, group, grads):
        mine_h, got_h = _send_wait("share_%s_wait" % tag, PAIR_FLIPS, started, moves, after)
        for (n, _, _), mh, gh in zip(group, mine_h, got_h):
            grads[n] = jnp.concatenate([jnp.where(ic == 0, mh, gh), jnp.where(ic == 0, gh, mh)], axis=0)[None]

    def reduce_start(tag, group, mats):
        gsh = [_to_shards(g, ax).astype(MXU_DTYPE) for g, (_, _, ax) in zip(mats, group)]
        moves = [(i, i, reduce_src(g.shape[1]), lambda ref, me, k: ref.at[k]) for i, g in enumerate(gsh)]
        lands = [sds((len(ALL_FLIPS), g.shape[1] // 2, g.shape[2]), MXU_DTYPE) for g in gsh]
        return _send_start("rs_%s_start" % tag, ALL_FLIPS, gsh, lands, moves), moves

    rsl, rsl_moves = reduce_start("ffn", BIG_LATE, (g_w_ffn_up, g_w_ffn_down))
    norm2_g = norm2_g + rsl["token"]

    def norm2_bwd(x1v, dh2, dx2v, mx, gt, sc, sh, g):
        _, vjp = jax.vjp(_norm_mod, x1v, g, sc, sh)
        dxn, dg, dsc, dsh = vjp(dh2)
        dx1 = dx2v + dxn
        return dx1, dx1 * gt, jnp.sum(dx1 * mx, axis=0, keepdims=True), dsc, dsh, dg

    dx1, d_mixed, d_gt1, d_sc2, d_sh2, g_norm2 = rw(
        "norm2_bwd", norm2_bwd, R=512, tiled=[(x1, D, 0), (d_h2, D, 0), (dx2, D, 0), (mixed, D, 0)],
        batch=[(mod, D, GT1), (mod, D, SC2), (mod, D, SH2)], full=[norm2_g],
        out_tiled=[(D, F32), (D, MXU_DTYPE)], out_batch=[D, D, D], out_acc=[(1, D)])

    d_mixed_in = _mm([d_mixed], [full["w_out"]], F32, "d_mixed_in", bt=True)
    g_w_out = _mm_tn(mixed_in, d_mixed, "g_w_out")

    def mix_bwd(ga, gb, ya, za, zb, dm):
        _, vjp = jax.vjp(_mix, ga, gb, ya, za, zb)
        dga, dgb, dya, dza, dzb = vjp(dm)
        return jnp.concatenate([dga, dgb], axis=1), dya, jnp.concatenate([dza, dzb], axis=1)

    d_gates, d_ya, d_z = rw("mix_bwd", mix_bwd, R=256, tiled=mix_tiled + [(d_mixed_in, D, 0)],
                            out_tiled=[(2 * D, MXU_DTYPE), (D, MXU_DTYPE), (2 * D, MXU_DTYPE)])
    d_o_rwkv = _mm([d_ya], [full["w_out_rwkv"]], F32, "d_o_rwkv", bt=True)
    g_w_out_rwkv = _mm_tn(o_rwkv, d_ya, "g_out_rwkv")
    d_s5o = _mm([d_z], [full["w_glu"]], F32, "d_s5o", bt=True)
    g_w_glu = _mm_tn(s5o, d_z, "g_glu")
    rsm, rsm_moves = reduce_start("mid", BIG_MID, (g_w_out_rwkv, g_w_glu, g_w_out))
    s5_d = s5_d + rsm["token"]

    d_u, d_wb, d_wc, d_ab, g_s5_d = _s5_bwd(u, y_ssm, d_s5o, s5_d, wb, wc, ab, s5_st, s5_x, Bl, S)

    def diag_in(dw):
        t = dw.reshape(NSG, 8, SGC, 8, SP)
        return jnp.einsum("ab,sacbp->sapc", eye8, t).reshape(NG, SP * SGC)

    def diag_out(dw):
        t = dw.reshape(NSG, 8, SP, 8, SGC)
        return jnp.einsum("ab,sapbc->sacp", eye8, t).reshape(NG, SGC, SP)

    g_s5_c_re = diag_out(d_wc[:, :512])
    g_s5_c_im = -diag_out(d_wc[:, 512:])
    disc_cts = (d_ab[0].reshape(NG, SP), d_ab[1].reshape(NG, SP), diag_in(d_wb[:, :, :512]), diag_in(d_wb[:, :, 512:]))
    g_a_re, g_a_im, g_log_dt, g_b_re, g_b_im = _s5_disc_bwd(*s5_in, disc_cts)

    def post_bwd(yv, rv, kv, vv, gv, do, *pp):
        _, vjp = jax.vjp(lambda *a: _rwkv_post(*a, pp[3]), yv, rv, kv, vv, gv, *pp[:3])
        return vjp(do)

    dy_wkv, dr_b, dk_b, dv_b, dg_, g_ln_g, g_ln_b, g_r_k = rw(
        "rwkv_post_bwd", post_bwd, R=256,
        tiled=[(y_wkv, RW, 0), (r_, RW, 0), (k_, RW, 0), (v_, RW, 0), (g_, RW, 0), (d_o_rwkv, RW, 0)],
        full=post_params, out_tiled=[(RW, F32)] * 5, out_acc=[(1, RW)] * 3)
    dr3, dw3, dk3, dv3, da3, db3 = _wkv_bwd(r_, w_, k_, v_, a_, b_, dy_wkv, ck, Bl, S)

    shl, shl_moves = share_start("ffn", reduced_halves("ffn", rsl, rsl_moves, dr3))
    shm, shm_moves = share_start("mid", reduced_halves("mid", rsm, rsm_moves, dr3))
    mu_shift = mu_shift + (shl["token"] + shm["token"])

    def prep_bwd(pv, dr1, dr2, dwv, dk1, dk2, dv1, dv2, dav, dbv, dgv, ph, mu, *pp):
        prev = _shift_down(pv, ph, 1)
        ps = pv + (prev - pv) * mu
        _, vjp = jax.vjp(lambda *q: _rwkv_prep(*q, pp[7]), *_split_ps(ps), *pp[:7])
        grads = vjp((dr1 + dr2, dwv, dk1 + dk2, dv1 + dv2, dav, dbv, dgv))
        dps = jnp.concatenate(grads[:5], axis=1)
        return (dps,) + tuple(grads[5:]) + (jnp.sum(dps * (prev - pv), axis=0, keepdims=True),)

    prep_outs = rw(
        "rwkv_prep_bwd", prep_bwd, R=256,
        tiled=[(p, SHIFT, 0), (dr3, RW, 0), (dr_b, RW, 0), (dw3, RW, 0), (dk3, RW, 0), (dk_b, RW, 0),
               (dv3, RW, 0), (dv_b, RW, 0), (da3, RW, 0), (db3, RW, 0), (dg_, RW, 0)],
        prev=[(p, SHIFT, 0)], full=[mu_shift] + prep_params,
        out_tiled=[(SHIFT, F32)],
        out_acc=[(1, RW), (LW + LA, RW), (1, RW), (LW + LA, RW), (LG, RW), (1, RW), (1, RW), (1, SHIFT)])
    d_ps, g_w0, g_w_up_p, g_a0, g_a_up_p, g_g_up, g_k_k, g_k_a, g_mu = prep_outs

    small = {"mu_shift": g_mu, "rwkv_w0": g_w0, "rwkv_a0": g_a0, "rwkv_k_k": g_k_k,
             "rwkv_k_a": g_k_a, "rwkv_r_k": g_r_k, "rwkv_ln_g": g_ln_g, "rwkv_ln_b": g_ln_b, "s5_a_re": g_a_re,
             "s5_a_im": g_a_im, "s5_log_dt": g_log_dt, "s5_b_re": g_b_re, "s5_b_im": g_b_im, "s5_c_re": g_s5_c_re,
             "s5_c_im": g_s5_c_im, "s5_d": g_s5_d, "norm2_g": g_norm2,
             "ffn_conv_b": jnp.concatenate([g_cb_g, g_cb_u], axis=1), "norm_f_g": g_norm_f}
    small_names = list(small)
    g_conv_w = jnp.concatenate([g_cw_g, g_cw_u], axis=1)
    shard_small = {"rwkv_w_up": g_w_up_p[:LW], "rwkv_a_up": g_a_up_p[LW:], "rwkv_g_up": g_g_up, "ffn_conv_w": g_conv_w}
    parts = [small[n] for n in small_names] + [_to_shards(shard_small[n], ax) for n, _, ax in BIG_SMALL]
    spack = _pack_rows(parts, F32, SUBLANES)
    sm_moves = [(0, 0, lambda ref, me, peer: ref, lambda ref, me, k: ref.at[2 * _chip_of(me) + me[2]])]
    sm = _send_start("gsmall_start", ALL_FLIPS, [spack], [sds((8,) + spack.shape, F32)], sm_moves)
    mu_shift = mu_shift + sm["token"]

    def shift_bwd(dps, nx, mu):
        return dps * (1.0 - mu) + _shift_up(dps * mu, nx * mu, 1)

    (d_p,) = rw("shift_bwd", shift_bwd, R=256, tiled=[(d_ps, SHIFT, 0)], nxt=[(d_ps, SHIFT, 0)], full=[mu_shift],
                out_tiled=[(SHIFT, MXU_DTYPE)])
    g_w_in = jnp.concatenate([_mm_tn(h1, d_p, "g_w_p"), _mm_tn(h1, d_u, "g_w_u"), _mm_tn(h1, d_gates, "g_w_g")], axis=1)
    rsn, rsn_moves = reduce_start("in", BIG[:1], (g_w_in,))
    norm1_g = norm1_g + rsn["token"]
    d_h1 = _mm([d_p, d_u, d_gates], [w_p, w_u, w_g], F32, "d_h1", bt=True)

    def norm1_bwd(xv, dh1, dx1v, sc, sh, g):
        _, vjp = jax.vjp(_norm_mod, xv, g, sc, sh)
        dxn, dg, dsc, dsh = vjp(dh1)
        return dx1v + dxn, dsc, dsh, dg

    grad_x, d_sc1, d_sh1, g_norm1 = rw(
        "norm1_bwd", norm1_bwd, R=512, tiled=[(x2d, D, 0), (d_h1, D, 0), (dx1, D, 0)],
        batch=[(mod, D, SC1), (mod, D, SH1)], full=[norm1_g], out_tiled=[(D, F32)], out_batch=[D, D], out_acc=[(1, D)])

    dmod = jnp.concatenate([d_sh1, d_sc1, d_gt1, d_sh2, d_sc2, d_gt2], axis=2).reshape(Bl, 6 * D)
    last_all = _gather_two_level([], [dmod, g_norm1], "gather_dmod")[1]
    dmod_all = last_all[0].reshape(8 * Bl, 6 * D)
    shn, shn_moves = share_start("in", reduced_halves("in", rsn, rsn_moves, dmod_all))
    dmod_cols = lax.dynamic_slice_in_dim(dmod_all, chip * ncol, ncol, 1)
    g_w_ada, g_b_ada = _ada_bwd(c_all, dmod_cols, dmod_all)

    grads = {"norm1_g": _sum_slots(last_all[1].reshape(8, 1, D), F32, "sum_norm1")}
    sm_own, sm_got = _send_wait("gsmall_wait", ALL_FLIPS, sm, sm_moves, g_b_ada)
    s_all = lax.dynamic_update_slice(sm_got[0], sm_own[0][None], (dev, 0, 0))
    s_sum = _sum_slots(s_all, F32, "sum_gsmall").reshape(-1)
    off = 0
    for n in small_names:
        grads[n] = s_sum[off:off + W[n].size].reshape(W[n].shape)
        off += W[n].size
    for n, shape, axis in BIG_SMALL:
        ss = _shard_shape(shape, axis)
        k4 = 4 * math.prod(ss)
        sh4 = s_sum[off:off + k4].reshape(4, math.prod(ss))
        grads[n] = lax.dynamic_index_in_dim(sh4, chip, 0, keepdims=False).reshape((1,) + ss)
        off += k4

    share_finish("ffn", shl, shl_moves, s_sum, BIG_LATE, grads)
    share_finish("mid", shm, shm_moves, grads[BIG_LATE[0][0]], BIG_MID, grads)
    grads["w_ada"] = g_w_ada[None]
    grads["b_ada"] = g_b_ada

    delta, new_m, new_v = {}, {}, {}
    to2 = lambda z: z.reshape(-1, z.shape[-1])

    def adamw(n):
        d_, m_, v2_ = _adamw(to2(W[n]), to2(grads[n]), to2(M[n]), to2(V[n]), "adamw_" + n)
        delta[n], new_m[n], new_v[n] = (z.reshape(W[n].shape) for z in (d_, m_, v2_))

    for n in ["w_ada"] + [b[0] for b in BIG[1:]]:
        adamw(n)
    rest = [n for n in names if n not in delta and n != "w_in"]
    packs = [_pack_rows([src[n] for n in rest], F32, SUBLANES) for src in (W, grads, M, V)]
    d_, m_, v2_ = _adamw(*packs, "adamw_small")
    shapes = [W[n].shape for n in rest]
    for dst, z in ((delta, d_), (new_m, m_), (new_v, v2_)):
        for n, val in zip(rest, _unpack(z.reshape(-1), shapes)):
            dst[n] = val
    share_finish("in", shn, shn_moves, d_, BIG[:1], grads)
    adamw("w_in")

    return (loss, grad_x.reshape(Bl, S, D), *[grads[n] for n in names], *[delta[n] for n in names],
            *[new_m[n] for n in names], *[new_v[n] for n in names])
```

```python
import functools
import math

import jax
import jax.numpy as jnp
from jax import lax
from jax.experimental import pallas as pl
from jax.experimental.pallas import tpu as pltpu

F32 = jnp.float32
BF16 = jnp.bfloat16
MXU_DTYPE = jnp.bfloat16
MESH_IDS = pl.DeviceIdType.MESH
HIGHEST = lax.Precision.HIGHEST

D = 1024
RW, NH, HD = 512, 8, 64
LW, LA, LG = 64, 64, 128
SW, SGC, NG, SP = 512, 16, 32, 64
NSG = 4
SHIFT = 3 * RW + LW + LA + LG
DFF = 2816
RMS_EPS, GN_EPS, L2_EPS = 1e-6, 64e-5, 1e-12
LR, B1, B2, ADAM_EPS, WD, STEP = 0.001, 0.9, 0.999, 1e-8, 0.01, 10
DECAY_SCALE = math.exp(-0.5)
GELU_C = math.sqrt(2.0 / math.pi)

VMEM_LIMIT = 52 * 1024 * 1024
SUBLANES, LANES = 8, 128
HALO = 16


def _pick(n, cap):
    if n <= cap:
        return n
    best = None
    for t in range(LANES, cap + 1, LANES):
        if n % t == 0:
            best = t
    assert best is not None, (n, cap)
    return best


def _params(sem=None, vmem=VMEM_LIMIT):
    return pltpu.CompilerParams(dimension_semantics=sem, vmem_limit_bytes=vmem)


def _chip_of(p):
    return 2 * p[0] + p[1]


def _me():
    return (lax.axis_index("x"), lax.axis_index("y"), lax.axis_index("c"))


def _half(rows, core):
    h = rows // 2
    return pl.ds(pl.multiple_of(core * h, 16 if h % 16 == 0 else SUBLANES), h)


_HBM =pl.BlockSpec(memory_space=pltpu.HBM)
_SEM = pl.BlockSpec(memory_space=pltpu.SEMAPHORE)
_DATAFLOW = pltpu.SideEffectType.DATAFLOW_SIDE_EFFECTING


def _split_copies(flips, moves, src_refs, land_refs, send_sems, recv_sems):
    me = _me()
    nf = len(flips)
    out = []
    for m, (si, li, src_sel, dst_sel) in enumerate(moves):
        for k, f in enumerate(flips):
            peer = tuple(1 - v if b else v for v, b in zip(me, f))
            out.append(pltpu.make_async_remote_copy(
                src_ref=src_sel(src_refs[si], me, peer), dst_ref=dst_sel(land_refs[li], me, k),
                send_sem=send_sems.at[m * nf + k], recv_sem=recv_sems.at[m * nf + k],
                device_id=peer, device_id_type=MESH_IDS))
    return out


def _send_start(name, flips, srcs, land_shapes, moves):
    ns, nl = len(srcs), len(land_shapes)
    n = len(moves) * len(flips)

    def body(*refs):
        for cp in _split_copies(flips, moves, refs[:ns], refs[ns:ns + nl], refs[ns + nl], refs[ns + nl + 1]):
            cp.start()
        refs[-1][...] = jnp.zeros(refs[-1].shape, F32)

    hbm = lambda z: pltpu.with_memory_space_constraint(z, pltpu.HBM)
    lands = [lax.empty(s.shape, s.dtype) for s in land_shapes]
    res = pl.pallas_call(
        body, name=name,
        out_shape=(pltpu.SemaphoreType.DMA((n,)), pltpu.SemaphoreType.DMA((n,)),
                   *[pltpu.HBM(z.shape, z.dtype) for z in srcs], *[pltpu.HBM(s.shape, s.dtype) for s in land_shapes],
                   jax.ShapeDtypeStruct((SUBLANES, LANES), F32)),
        in_specs=[_HBM] * (ns + nl),
        out_specs=(_SEM, _SEM, *[_HBM] * (ns + nl), pl.BlockSpec(memory_space=pltpu.VMEM)),
        input_output_aliases={i: 2 + i for i in range(ns + nl)},
        compiler_params=pltpu.CompilerParams(has_side_effects=_DATAFLOW),
    )(*[hbm(z) for z in srcs], *[hbm(z) for z in lands])
    return {"sems": res[:2], "srcs": list(res[2:2 + ns]), "lands": list(res[2 + ns:2 + ns + nl]), "token": res[-1][0, 0]}


def _send_wait(name, flips, started, moves, after):
    srcs, lands = started["srcs"], started["lands"]
    ns, nl = len(srcs), len(lands)

    def body(*refs):
        for cp in _split_copies(flips, moves, refs[:ns], refs[ns:ns + nl], refs[ns + nl], refs[ns + nl + 1]):
            cp.wait_send()
            cp.wait_recv()

    res = pl.pallas_call(
        body, name=name, out_shape=[pltpu.HBM(z.shape, z.dtype) for z in srcs + lands],
        in_specs=[_HBM] * (ns + nl) + [_SEM, _SEM, pl.BlockSpec(memory_space=pl.ANY)],
        out_specs=[_HBM] * (ns + nl), input_output_aliases={i: i for i in range(ns + nl)},
        compiler_params=pltpu.CompilerParams(has_side_effects=_DATAFLOW),
    )(*srcs, *lands, *started["sems"], after)
    return list(res[:ns]), list(res[ns:])


CHIP_FLIPS = ((1, 0, 0), (0, 1, 0), (1, 1, 0))
PAIR_FLIPS = ((0, 0, 1),)
ALL_FLIPS = CHIP_FLIPS + ((1, 0, 1), (0, 1, 1), (1, 1, 1)) + PAIR_FLIPS


def _gather_two_level(chip_arrs, dev_arrs, name):
    arrs = list(chip_arrs) + list(dev_arrs)
    n, nchip = len(arrs), len(chip_arrs)
    NS = 7

    def body(*refs):
        srcs, outs = refs[:n], refs[n:2 * n]
        send_sems, recv_sems, loc_sems = refs[2 * n:]
        x, y, c = _me()
        sib = (x, y, 1 - c)
        chips = [(1 - x, y), (x, 1 - y), (1 - x, 1 - y)]
        mine = 2 * x + y
        ids = [2 * cx + cy for cx, cy in chips]

        def part(i, slot, core):
            if i < nchip:
                return outs[i].at[slot, _half(arrs[i].shape[0], core)]
            return outs[i].at[slot, core]

        def rcopy(i, k, src, dst, to):
            return pltpu.make_async_remote_copy(src_ref=src, dst_ref=dst, send_sem=send_sems.at[i * NS + k],
                                                recv_sem=recv_sems.at[i * NS + k], device_id=to, device_id_type=MESH_IDS)

        started, locs = [], []
        for i in range(n):
            own = srcs[i].at[_half(arrs[i].shape[0], c)] if i < nchip else srcs[i]
            loc = pltpu.make_async_copy(srcs[i], outs[i].at[mine] if i < nchip else outs[i].at[mine, c], loc_sems.at[i])
            loc.start()
            locs.append(loc)
            for f, chip in enumerate(chips):
                cp = rcopy(i, f, own, part(i, mine, c), (*chip, c))
                cp.start()
                started.append(cp)
            if i >= nchip:
                cp = rcopy(i, 6, own, part(i, mine, c), sib)
                cp.start()
                started.append(cp)
        for i in range(n):
            for f in range(3):
                land = part(i, ids[f], c)
                rcopy(i, f, land, land, sib).wait_recv()
                fw = rcopy(i, 3 + f, land, land, sib)
                fw.start()
                started.append(fw)
        for i in range(n):
            for f in range(3):
                land = part(i, ids[f], 1 - c)
                rcopy(i, 3 + f, land, land, sib).wait_recv()
            if i >= nchip:
                land = part(i, mine, 1 - c)
                rcopy(i, 6, land, land, sib).wait_recv()
        for cp in started:
            cp.wait_send()
        for loc in locs:
            loc.wait()

    outs = [jax.ShapeDtypeStruct((4,) + a.shape, a.dtype) for a in chip_arrs]
    outs += [jax.ShapeDtypeStruct((4, 2) + a.shape, a.dtype) for a in dev_arrs]
    res = pl.pallas_call(
        body, name=name, out_shape=outs,
        in_specs=[pl.BlockSpec(memory_space=pl.ANY)] * n, out_specs=[pl.BlockSpec(memory_space=pl.ANY)] * n,
        scratch_shapes=[pltpu.SemaphoreType.DMA((n * NS,)), pltpu.SemaphoreType.DMA((n * NS,)),
                        pltpu.SemaphoreType.DMA((n,))],
    )(*arrs)
    return res[:nchip], res[nchip:]


def _mm(As, Bs, out_dtype, name, tm=512, cap=1408, bt=False):
    n = len(As)
    M, N = As[0].shape[0], Bs[0].shape[0 if bt else 1]
    if sum(a.shape[1] for a in As) <= 1024:
        tm = 2 * tm
    tm = min(tm, M)
    tn = _pick(N, cap)
    dims = (((1,), (1,)), ((), ())) if bt else (((1,), (0,)), ((), ()))

    def body(*refs):
        o = refs[2 * n]
        acc = None
        for a, b in zip(refs[:n], refs[n:2 * n]):
            d = lax.dot_general(a[...].astype(MXU_DTYPE), b[...].astype(MXU_DTYPE), dims, preferred_element_type=F32)
            acc = d if acc is None else acc + d
        o[...] = acc.astype(o.dtype)

    in_specs = [pl.BlockSpec((tm, a.shape[1]), lambda i, j: (i, 0)) for a in As]
    if bt:
        in_specs += [pl.BlockSpec((tn, b.shape[1]), lambda i, j: (j, 0)) for b in Bs]
    else:
        in_specs += [pl.BlockSpec((b.shape[0], tn), lambda i, j: (0, j)) for b in Bs]
    return pl.pallas_call(
        body, name=name, grid=(M // tm, N // tn), in_specs=in_specs,
        out_specs=pl.BlockSpec((tm, tn), lambda i, j: (i, j)),
        out_shape=jax.ShapeDtypeStruct((M, N), out_dtype),
        compiler_params=_params(("parallel", "parallel")),
    )(*As, *Bs)


def _mm_tn(A, G, name, tt=1024, cap=1408):
    T, Ka = A.shape
    N = G.shape[1]
    tt = min(tt, T)
    tk = _pick(Ka, cap)
    tn = _pick(N, cap)

    def body(a, g, o):
        @pl.when(pl.program_id(2) == 0)
        def _():
            o[...] = jnp.zeros(o.shape, F32)
        o[...] += lax.dot_general(a[...].astype(MXU_DTYPE), g[...].astype(MXU_DTYPE),
                                  (((0,), (0,)), ((), ())), preferred_element_type=F32)

    return pl.pallas_call(
        body, name=name, grid=(Ka // tk, N // tn, T // tt),
        in_specs=[pl.BlockSpec((tt, tk), lambda i, j, t: (t, i)), pl.BlockSpec((tt, tn), lambda i, j, t: (t, j))],
        out_specs=pl.BlockSpec((tk, tn), lambda i, j, t: (i, j)),
        out_shape=jax.ShapeDtypeStruct((Ka, N), F32),
        compiler_params=_params(("parallel", "parallel", "arbitrary")),
    )(A, G)


def _rowwise(name, fn, *, Bl, S, R, tiled=(), prev=(), nxt=(), batch=(), full=(),
             out_tiled=(), out_batch=(), out_acc=()):
    R = min(R, S)
    nS = S // R
    T = Bl * S
    hb = R // HALO
    n_in = len(tiled) + len(prev) + len(nxt) + len(batch) + len(full)

    in_specs, args = [], []
    for a, wd, cb in tiled:
        in_specs.append(pl.BlockSpec((R, wd), lambda b, i, cb=cb: (b * nS + i, cb)))
        args.append(a)
    for a, wd, cb in prev:
        in_specs.append(pl.BlockSpec((HALO, wd), lambda b, i, cb=cb: (jnp.maximum((b * nS + i) * hb - 1, 0), cb)))
        args.append(a)
    for a, wd, cb in nxt:
        in_specs.append(pl.BlockSpec((HALO, wd), lambda b, i, cb=cb: (jnp.minimum((b * nS + i + 1) * hb, T // HALO - 1), cb)))
        args.append(a)
    for a, wd, cb in batch:
        in_specs.append(pl.BlockSpec((1, 1, wd), lambda b, i, cb=cb: (b, 0, cb)))
        args.append(a)
    for a in full:
        in_specs.append(pl.BlockSpec(a.shape, lambda b, i, nd=a.ndim: (0,) * nd))
        args.append(a)

    out_specs, out_shape = [], []
    for C, dt in out_tiled:
        out_specs.append(pl.BlockSpec((R, C), lambda b, i: (b * nS + i, 0)))
        out_shape.append(jax.ShapeDtypeStruct((T, C), dt))
    for C in out_batch:
        out_specs.append(pl.BlockSpec((1, 1, C), lambda b, i: (b, 0, 0)))
        out_shape.append(jax.ShapeDtypeStruct((Bl, 1, C), F32))
    for shp in out_acc:
        out_specs.append(pl.BlockSpec(shp, lambda b, i, nd=len(shp): (0,) * nd))
        out_shape.append(jax.ShapeDtypeStruct(shp, F32))

    nt, npv, nnx, nbt = len(tiled), len(prev), len(nxt), len(batch)

    def body(*refs):
        b, i = pl.program_id(0), pl.program_id(1)
        ins, outs = refs[:n_in], refs[n_in:]
        vals = [r[...] for r in ins[:nt]]
        vals += [jnp.where(i > 0, r[...], jnp.zeros(r.shape, r.dtype)) for r in ins[nt:nt + npv]]
        vals += [jnp.where(i < nS - 1, r[...], jnp.zeros(r.shape, r.dtype)) for r in ins[nt + npv:nt + npv + nnx]]
        vals += [r[0] for r in ins[nt + npv + nnx:nt + npv + nnx + nbt]]
        vals += [r[...] for r in ins[nt + npv + nnx + nbt:]]
        res = fn(*vals)
        if not isinstance(res, (tuple, list)):
            res = (res,)
        k = 0
        for _ in out_tiled:
            outs[k][...] = res[k].astype(outs[k].dtype)
            k += 1
        for _ in out_batch:
            o = outs[k]

            @pl.when(i == 0)
            def _(o=o):
                o[...] = jnp.zeros(o.shape, F32)
            o[0] += res[k]
            k += 1
        for _ in out_acc:
            o = outs[k]

            @pl.when((i == 0) & (b == 0))
            def _(o=o):
                o[...] = jnp.zeros(o.shape, F32)
            o[...] += res[k]
            k += 1

    out = pl.pallas_call(
        body, name=name, grid=(Bl, nS), in_specs=in_specs, out_specs=out_specs, out_shape=out_shape,
        compiler_params=_params(("arbitrary", "arbitrary")),
    )(*args)
    return out


def _colwise(name, fn, *, Bl, S, R, W, strip, tiled=(), prev=(), nxt=(), full=(), out_tiled=(), n_acc=0):
    R = min(R, S)
    nS = S // R
    T = Bl * S
    hb = R // HALO
    nt, npv, nnx, nfl = len(tiled), len(prev), len(nxt), len(full)
    n_in = nt + npv + nnx + nfl
    in_specs = [pl.BlockSpec((R, W), lambda b, i, cb=cb: (b * nS + i, cb)) for _, cb in tiled]
    in_specs += [pl.BlockSpec((HALO, W), lambda b, i, cb=cb: (jnp.maximum((b * nS + i) * hb - 1, 0), cb)) for _, cb in prev]
    in_specs += [pl.BlockSpec((HALO, W), lambda b, i, cb=cb: (jnp.minimum((b * nS + i + 1) * hb, T // HALO - 1), cb))
                 for _, cb in nxt]
    in_specs += [pl.BlockSpec(a.shape, lambda b, i: (0, 0)) for a in full]
    out_specs = [pl.BlockSpec((R, m * W), lambda b, i: (b * nS + i, 0)) for m, _ in out_tiled]
    out_specs += [pl.BlockSpec((1, W), lambda b, i: (0, 0))] * n_acc
    out_shape = [jax.ShapeDtypeStruct((T, m * W), dt) for m, dt in out_tiled] + [jax.ShapeDtypeStruct((1, W), F32)] * n_acc

    def body(*refs):
        b, i = pl.program_id(0), pl.program_id(1)
        ins, outs = refs[:n_in], refs[n_in:]

        @pl.when((i == 0) & (b == 0))
        def _():
            for o in outs[len(out_tiled):]:
                o[...] = jnp.zeros(o.shape, F32)

        def col(j, carry):
            cs = pl.ds(pl.multiple_of(j * strip, strip), strip)
            vals = [r[:, cs] for r in ins[:nt]]
            vals += [jnp.where(i > 0, r[:, cs], jnp.zeros((HALO, strip), r.dtype)) for r in ins[nt:nt + npv]]
            vals += [jnp.where(i < nS - 1, r[:, cs], jnp.zeros((HALO, strip), r.dtype)) for r in ins[nt + npv:nt + npv + nnx]]
            vals += [r[:, cs] for r in ins[nt + npv + nnx:]]
            res = fn(*vals)
            for k, (m, _) in enumerate(out_tiled):
                for q in range(m):
                    outs[k][:, pl.ds(pl.multiple_of(q * W + j * strip, strip), strip)] = res[k][q].astype(outs[k].dtype)
            for k in range(len(out_tiled), len(outs)):
                outs[k][:, cs] += res[k]
            return carry

        lax.fori_loop(0, W // strip, col, 0)

    return pl.pallas_call(
        body, name=name, grid=(Bl, nS), in_specs=in_specs, out_specs=out_specs, out_shape=out_shape,
        compiler_params=_params(("arbitrary", "arbitrary")),
    )(*[a for a, _ in tiled], *[a for a, _ in prev], *[a for a, _ in nxt], *full)


def _shift_down(x, halo, k):
    rolled = pltpu.roll(x, k, 0)
    row = lax.broadcasted_iota(jnp.int32, (SUBLANES, x.shape[1]), 0)
    head = rolled[0:SUBLANES]
    for j in range(k):
        head = jnp.where(row == j, halo[HALO - k + j:HALO - k + j + 1, :], head)
    return jnp.concatenate([head, rolled[SUBLANES:]], axis=0)


def _shift_up(x, halo, k):
    n = x.shape[0]
    rolled = pltpu.roll(x, n - k, 0)
    row = lax.broadcasted_iota(jnp.int32, (SUBLANES, x.shape[1]), 0)
    tail = rolled[n - SUBLANES:]
    for j in range(k):
        tail = jnp.where(row == SUBLANES - k + j, halo[j:j + 1, :], tail)
    return jnp.concatenate([rolled[:n - SUBLANES], tail], axis=0)


def _dotm(a, b):
    return jnp.dot(a.astype(MXU_DTYPE), b.astype(MXU_DTYPE), preferred_element_type=F32)


def _split_bf16(x):
    hi = x.astype(BF16)
    return hi, (x - hi.astype(F32)).astype(BF16)


def _headsum_2pass(x, hm):
    hi, lo = _split_bf16(x)
    hb = hm.astype(BF16)
    return jnp.dot(hi, hb, preferred_element_type=F32) + jnp.dot(lo, hb, preferred_element_type=F32)


@jax.custom_vjp
def _headsum(x, hm):
    return _headsum_2pass(x, hm)


_headsum.defvjp(lambda x, hm: (_headsum_2pass(x, hm), hm),
                lambda hm, g: (_headsum_2pass(g, hm), jnp.zeros_like(hm)))


def _sigmoid(x):
    return 0.5 * jnp.tanh(0.5 * x) + 0.5


def _rms(x, g):
    return x * lax.rsqrt(jnp.mean(x * x, axis=-1, keepdims=True) + RMS_EPS) * g


def _norm_mod(x, g, sc, sh):
    return _rms(x, g) * (1.0 + sc) + sh


def _split_ps(ps):
    return (ps[:, 0:RW], ps[:, RW:2 * RW], ps[:, 2 * RW:3 * RW], ps[:, 3 * RW:3 * RW + LW + LA],
            ps[:, 3 * RW + LW + LA:SHIFT])


def _rwkv_prep(r, k, v, wa, gd, w0, w_up_p, a0, a_up_p, g_up, k_k, k_a, hm):
    w_raw = w0 + _dotm(jnp.tanh(wa), w_up_p)
    decay = jnp.exp(-DECAY_SCALE * _sigmoid(w_raw))
    a = _sigmoid(a0 + _dotm(wa, a_up_p))
    g = _dotm(_sigmoid(gd), g_up)
    kk = k * k_k
    kk = kk * lax.rsqrt(_headsum(kk * kk, hm) + L2_EPS)
    k2 = k * (1.0 + (a - 1.0) * k_a)
    return r, decay, k2, v, -kk, kk * a, g


def _rwkv_post(y, r, k2, v, g, ln_g, ln_b, r_k, hm):
    mean = _headsum(y, hm) * (1.0 / HD)
    yc = y - mean
    var = _headsum(yc * yc, hm) * (1.0 / HD)
    yn = yc * lax.rsqrt(var + GN_EPS) * ln_g + ln_b
    bonus = _headsum(r * k2 * r_k, hm) * v
    return (yn + bonus) * g


def _gelu(x):
    return 0.5 * x * (1.0 + jnp.tanh(GELU_C * (x + 0.044715 * (x * x * x))))


def _s5_post(yssm, u, d):
    return _gelu(yssm + d * u)


def _mix(ga, gb, ya, za, zb):
    return _sigmoid(ga) * ya + _sigmoid(gb) * (za * _sigmoid(zb))


def _conv_act(up_g, up_u, hg, hu, w_g, w_u, b_g, b_u):
    gate, upv = _conv3(up_g, hg, w_g, b_g)[0], _conv3(up_u, hu, w_u, b_u)[0]
    return gate, upv


def _conv3(x, h, w, b):
    x, h = x.astype(F32), h.astype(F32)
    s2, s1 = _shift_down(x, h, 2), _shift_down(x, h, 1)
    return b + w[0:1] * s2 + w[1:2] * s1 + w[2:3] * x, (s2, s1, x)


def _silu_gate(gate, upv):
    return gate * _sigmoid(gate) * upv


WKV_L = 64
_NT, _NN, _TN = ((1,), (1,)), ((1,), (0,)), ((0,), (0,))


def _dotw(x, y, dims):
    return lax.dot_general(x.astype(MXU_DTYPE), y.astype(MXU_DTYPE), (dims, ((), ())), preferred_element_type=F32)


def _dot3(x, y, dims):
    (xh, xl), (yh, yl) = _split_bf16(x), _split_bf16(y)
    d = lambda p, q: lax.dot_general(p, q, (dims, ((), ())), preferred_element_type=F32)
    return d(xh, yh) + d(xh, yl) + d(xl, yh)


@jax.custom_vjp
def _gram3(x, y):
    return _dot3(x, y, _NT)


_gram3.defvjp(lambda x, y: (_dot3(x, y, _NT), (x, y)),
              lambda res, g: (_dot3(g, res[1], _NN), _dot3(g, res[0], _TN)))


def _tri_solve_fwd(ns, xs):
    each = lambda f, *ls: tuple(f(*zs) for zs in zip(*ls))
    size = ns[0].shape[0]
    eye = (lax.broadcasted_iota(jnp.int32, (size, size), 0) == lax.broadcasted_iota(jnp.int32, (size, size), 1)).astype(F32)
    ts = each(lambda n: n + eye, ns)
    qs = ns
    for _ in range(WKV_L.bit_length() - 2):
        qs = each(lambda q: _dotw(q, q, _NN), qs)
        ts = each(lambda t, q: t + _dotw(t, q, _NN), ts, qs)
    us = each(lambda t, x: _dotw(t, x, _NN), ts, xs)
    return us, (ts, us)


def _tri_solve_bwd(res, dus):
    ts, us = res
    each = lambda f, *ls: tuple(f(*zs) for zs in zip(*ls))
    dxs = each(lambda t, du: _dotw(t, du, _TN), ts, dus)
    return each(lambda dx, u: _dotw(dx, u, _NT), dxs, us), dxs


@jax.custom_vjp
def _tri_solve(ns, xs):
    return _tri_solve_fwd(ns, xs)[0]


_tri_solve.defvjp(_tri_solve_fwd, _tri_solve_bwd)


def _wkv_chunk(s0, r, w, k, v, a, b):
    y, s1 = _wkv_chunks((s0,), (r,), (w,), (k,), (v,), (a,), (b,))
    return y[0], s1[0]


def _wkv_chunks(s0, r, w, k, v, a, b):
    each = lambda f, *ls: tuple(f(*xs) for xs in zip(*ls))
    L = r[0].shape[0]
    n2 = 2 * L
    lane_head = lax.broadcasted_iota(jnp.int32, (2, 1, 2 * HD), 2) // HD
    head_mask = (lane_head == lax.broadcasted_iota(jnp.int32, (2, 1, 2 * HD), 0)).astype(F32)
    ri = lax.broadcasted_iota(jnp.int32, (n2, n2), 0)
    ci = lax.broadcasted_iota(jnp.int32, (n2, n2), 1)
    same = (ri // L) == (ci // L)
    strict = same & ((ci % L) < (ri % L))
    incl = same & ((ci % L) <= (ri % L))
    si = lax.broadcasted_iota(jnp.int32, (2 * HD, 2 * HD), 0) // HD
    sj = lax.broadcasted_iota(jnp.int32, (2 * HD, 2 * HD), 1) // HD
    tri = (lax.broadcasted_iota(jnp.int32, (L, L), 0) >= lax.broadcasted_iota(jnp.int32, (L, L), 1)).astype(F32)

    stack = lambda z: (z[None] * head_mask).reshape(n2, 2 * HD)
    dup = lambda z: jnp.broadcast_to(z[None], (2, L, 2 * HD)).reshape(n2, 2 * HD)
    gram = _gram3
    nt, nn, tn = (lambda x, y, d=d: _dotw(x, y, d) for d in (_NT, _NN, _TN))
    add = lambda x, y: x + y

    lw = each(jnp.log, w)
    cum = each(lambda z: jnp.dot(tri, z, preferred_element_type=F32, precision=HIGHEST), lw)
    tot = each(lambda z: jnp.sum(z, axis=0, keepdims=True), lw)
    a2 = each(lambda av, cv, lv: stack(av * jnp.exp(cv - lv)), a, cum, lw)
    r2 = each(lambda rv, cv: stack(rv * jnp.exp(cv)), r, cum)
    v2 = each(stack, v)
    b2 = each(lambda bv, cv: dup(bv * jnp.exp(-cv)), b, cum)
    k2 = each(lambda kv, cv: dup(kv * jnp.exp(-cv)), k, cum)
    n_ab = each(lambda x, y: jnp.where(strict, gram(x, y), 0.0), a2, b2)
    n_ak = each(lambda x, y: jnp.where(strict, gram(x, y), 0.0), a2, k2)
    m_rb = each(lambda x, y: jnp.where(incl, gram(x, y), 0.0), r2, b2)
    m_rk = each(lambda x, y: jnp.where(incl, gram(x, y), 0.0), r2, k2)
    u = _tri_solve(n_ab, each(add, each(nt, a2, s0), each(nn, n_ak, v2)))
    y2 = each(lambda x, y, z: x + y + z, each(nt, r2, s0), each(nn, m_rb, u), each(nn, m_rk, v2))
    y = each(lambda z: jnp.sum(z.reshape(2, L, 2 * HD), axis=0), y2)
    b3 = each(lambda bv, tv, cv: dup(bv * jnp.exp(tv - cv)), b, tot, cum)
    k3 = each(lambda kv, tv, cv: dup(kv * jnp.exp(tv - cv)), k, tot, cum)
    upd = each(add, each(tn, u, b3), each(tn, v2, k3))
    s1 = each(lambda sv, tv, uv: sv * jnp.exp(tv) + jnp.where(si == sj, uv, 0.0), s0, tot, upd)
    return y, s1


NPAIR = NH // 2


def _wkv_nb(Bl):
    return 4 if Bl % 4 == 0 else 2 if Bl % 2 == 0 else 1


def _wkv_fwd(r, w, k, v, a, b, Bl, S):
    L = WKV_L
    nC = S // L
    nb = _wkv_nb(Bl)
    chains = [(bi, p, slice(p * 2 * HD, (p + 1) * 2 * HD)) for bi in range(nb) for p in range(NPAIR)]

    def body(r_ref, w_ref, k_ref, v_ref, a_ref, b_ref, y_ref, ck_ref, s_ref):
        @pl.when(pl.program_id(1) == 0)
        def _():
            s_ref[...] = jnp.zeros(s_ref.shape, F32)
        s0 = tuple(s_ref[bi, p] for bi, p, _ in chains)
        ops = [tuple(z[bi, :, cs] for bi, _, cs in chains) for z in (r_ref, w_ref, k_ref, v_ref, a_ref, b_ref)]
        y, s1 = _wkv_chunks(s0, *ops)
        for i, (bi, p, cs) in enumerate(chains):
            ck_ref[bi, 0, p] = s0[i]
            y_ref[bi, :, cs] = y[i]
            s_ref[bi, p] = s1[i]

    to3 = lambda z: z.reshape(Bl, S, RW)
    row_spec = pl.BlockSpec((nb, L, RW), lambda g, c: (g, c, 0))
    y, ck = pl.pallas_call(
        body, name="wkv_fwd", grid=(Bl // nb, nC), in_specs=[row_spec] * 6,
        out_specs=[row_spec, pl.BlockSpec((nb, 1, NPAIR, 2 * HD, 2 * HD), lambda g, c: (g, c, 0, 0, 0))],
        out_shape=[jax.ShapeDtypeStruct((Bl, S, RW), F32), jax.ShapeDtypeStruct((Bl, nC, NPAIR, 2 * HD, 2 * HD), F32)],
        scratch_shapes=[pltpu.VMEM((nb, NPAIR, 2 * HD, 2 * HD), F32)],
        compiler_params=_params(("arbitrary", "arbitrary")),
    )(*(to3(z) for z in (r, w, k, v, a, b)))
    return y.reshape(Bl * S, RW), ck


def _wkv_bwd(r, w, k, v, a, b, dy, ck, Bl, S):
    L = WKV_L
    nC = S // L
    nb = _wkv_nb(Bl)
    chains = [(bi, p, slice(p * 2 * HD, (p + 1) * 2 * HD)) for bi in range(nb) for p in range(NPAIR)]

    def body(r_ref, w_ref, k_ref, v_ref, a_ref, b_ref, dy_ref, ck_ref,
             dr_ref, dw_ref, dk_ref, dv_ref, da_ref, db_ref, ds_ref):
        @pl.when(pl.program_id(1) == 0)
        def _():
            ds_ref[...] = jnp.zeros(ds_ref.shape, F32)
        s0 = tuple(ck_ref[bi, 0, p] for bi, p, _ in chains)
        ops = [tuple(z[bi, :, cs] for bi, _, cs in chains) for z in (r_ref, w_ref, k_ref, v_ref, a_ref, b_ref)]
        cts = (tuple(dy_ref[bi, :, cs] for bi, _, cs in chains), tuple(ds_ref[bi, p] for bi, p, _ in chains))
        ds0, *grads = jax.vjp(_wkv_chunks, s0, *ops)[1](cts)
        for i, (bi, p, cs) in enumerate(chains):
            ds_ref[bi, p] = ds0[i]
            for o, g in zip((dr_ref, dw_ref, dk_ref, dv_ref, da_ref, db_ref), grads):
                o[bi, :, cs] = g[i]

    to3 = lambda z: z.reshape(Bl, S, RW)
    row_spec = pl.BlockSpec((nb, L, RW), lambda g, c: (g, nC - 1 - c, 0))
    rows = jax.ShapeDtypeStruct((Bl, S, RW), F32)
    outs = pl.pallas_call(
        body, name="wkv_bwd", grid=(Bl // nb, nC),
        in_specs=[row_spec] * 7 + [pl.BlockSpec((nb, 1, NPAIR, 2 * HD, 2 * HD), lambda g, c: (g, nC - 1 - c, 0, 0, 0))],
        out_specs=[row_spec] * 6, out_shape=[rows] * 6,
        scratch_shapes=[pltpu.VMEM((nb, NPAIR, 2 * HD, 2 * HD), F32)],
        compiler_params=_params(("arbitrary", "arbitrary")),
    )(*(to3(z) for z in (r, w, k, v, a, b, dy)), ck)
    return [o.reshape(Bl * S, RW) for o in outs]


NST = NG * SP


def _cmul(ar, ai, br, bi):
    return ar * br - ai * bi, ar * bi + ai * br


def _s5_tiles(are, aim, reverse):
    if reverse:
        aim = -aim
    row = lax.broadcasted_iota(jnp.int32, (SUBLANES, NST), 0)
    pw = [(are, aim)]
    for _ in range(SUBLANES - 1):
        pw.append(_cmul(pw[-1][0], pw[-1][1], are, aim))
    bc = lambda z: jnp.broadcast_to(z, (SUBLANES, NST))
    ms = []
    for kk in (1, 2, 4):
        cond = (row < SUBLANES - kk) if reverse else (row >= kk)
        ms.append((jnp.where(cond, bc(pw[kk - 1][0]), 0.0), jnp.where(cond, bc(pw[kk - 1][1]), 0.0)))
    pr = jnp.zeros((SUBLANES, NST), F32)
    pi = jnp.zeros((SUBLANES, NST), F32)
    for i in range(SUBLANES):
        n = SUBLANES - i if reverse else i + 1
        pr = jnp.where(row == i, bc(pw[n - 1][0]), pr)
        pi = jnp.where(row == i, bc(pw[n - 1][1]), pi)
    return ms, (pr, pi)


def _s5_block(re, im, ms, pc, cre, cim, sg, reverse):
    ln = slice(sg * 512, (sg + 1) * 512)
    for (mr, mi), kk in zip(ms, (1, 2, 4)):
        sh = SUBLANES - kk if reverse else kk
        sre, sim = pltpu.roll(re, sh, 0), pltpu.roll(im, sh, 0)
        tr, ti = _cmul(mr[:, ln], mi[:, ln], sre, sim)
        re, im = re + tr, im + ti
    tr, ti = _cmul(pc[0][:, ln], pc[1][:, ln], cre[:, ln], cim[:, ln])
    return re + tr, im + ti


def _s5_scan(X_ref, n_rows, ms, pc, cre, cim, reverse, visit=None, acc0=None):
    nblk = n_rows // SUBLANES

    def it(i, carry):
        cre, cim, acc = carry
        j = nblk - 1 - i if reverse else i
        rows = pl.ds(pl.multiple_of(j * SUBLANES, SUBLANES), SUBLANES)
        edge = 0 if reverse else SUBLANES - 1
        blocks, ncre, ncim = [], [], []
        for sg in range(NSG):
            lr = slice(sg * 1024, sg * 1024 + 512)
            li = slice(sg * 1024 + 512, (sg + 1) * 1024)
            re, im = _s5_block(X_ref[rows, lr], X_ref[rows, li], ms, pc, cre, cim, sg, reverse)
            X_ref[rows, lr] = re
            X_ref[rows, li] = im
            blocks.append((re, im))
            ncre.append(re[edge:edge + 1])
            ncim.append(im[edge:edge + 1])
        if visit is not None:
            acc = visit(j, blocks, acc)
        return jnp.concatenate(ncre, axis=1), jnp.concatenate(ncim, axis=1), acc

    return lax.fori_loop(0, nblk, it, (cre, cim, acc0 if acc0 is not None else 0))


def _s5_fwd(u, wb, wc, ab, d, Bl, S, R=256):
    R = min(R, S)
    nC = S // R

    def body(u_ref, wb_ref, wc_ref, ab_ref, d_ref, y_ref, st_ref, X_ref, o_ref, car_ref):
        @pl.when(pl.program_id(1) == 0)
        def _():
            car_ref[...] = jnp.zeros(car_ref.shape, F32)
        st_ref[0, 0] = car_ref[...]
        ms, pc = _s5_tiles(ab_ref[0:1], ab_ref[1:2], False)
        for sg in range(NSG):
            X_ref[:, sg * 1024:(sg + 1) * 1024] = _dotm(u_ref[:, sg * 128:(sg + 1) * 128], wb_ref[sg])
        cre, cim, _ = _s5_scan(X_ref, R, ms, pc, car_ref[0:1], car_ref[1:2], False)
        car_ref[0:1] = cre
        car_ref[1:2] = cim
        for sg in range(NSG):
            y_ref[:, sg * 128:(sg + 1) * 128] = _dotm(X_ref[:, sg * 1024:(sg + 1) * 1024], wc_ref[sg])
        o_ref[...] = _s5_post(y_ref[...], u_ref[...], d_ref[...]).astype(o_ref.dtype)

    rows = pl.BlockSpec((R, SW), lambda b, c: (b * nC + c, 0))
    return pl.pallas_call(
        body, name="s5_fwd", grid=(Bl, nC),
        in_specs=[rows, pl.BlockSpec(wb.shape, lambda b, c: (0, 0, 0)), pl.BlockSpec(wc.shape, lambda b, c: (0, 0, 0)),
                  pl.BlockSpec(ab.shape, lambda b, c: (0, 0)), pl.BlockSpec(d.shape, lambda b, c: (0, 0))],
        out_specs=[rows, pl.BlockSpec((1, 1, 2, NST), lambda b, c: (b, c, 0, 0)),
                   pl.BlockSpec((R, 2 * NST), lambda b, c: (b * nC + c, 0)), rows],
        out_shape=[jax.ShapeDtypeStruct((Bl * S, SW), F32), jax.ShapeDtypeStruct((Bl, nC, 2, NST), F32),
                   jax.ShapeDtypeStruct((Bl * S, 2 * NST), F32), jax.ShapeDtypeStruct((Bl * S, SW), MXU_DTYPE)],
        scratch_shapes=[pltpu.VMEM((2, NST), F32)],
        compiler_params=_params(("arbitrary", "arbitrary")),
    )(u, wb, wc, ab, d)


def _s5_bwd(u, y, do, d, wb, wc, ab, st, xs, Bl, S, R=256):
    R = min(R, S)
    nC = S // R

    def body(u_ref, y_ref, do_ref, d_ref, wb_ref, wc_ref, ab_ref, st_ref, X_ref,
             du_ref, dwb_ref, dwc_ref, dab_ref, dd_ref, G_ref, car_ref):
        first = (pl.program_id(0) == 0) & (pl.program_id(1) == 0)

        @pl.when(first)
        def _():
            for o in (dwb_ref, dwc_ref, dab_ref, dd_ref):
                o[...] = jnp.zeros(o.shape, F32)

        @pl.when(pl.program_id(1) == 0)
        def _():
            car_ref[...] = jnp.zeros(car_ref.shape, F32)

        are, aim = ab_ref[0:1], ab_ref[1:2]
        dy, du_direct, dd = jax.vjp(_s5_post, y_ref[...], u_ref[...], d_ref[...])[1](do_ref[...])
        dd_ref[...] += dd
        dyv = dy.astype(MXU_DTYPE)
        for sg in range(NSG):
            G_ref[:, sg * 1024:(sg + 1) * 1024] = lax.dot_general(
                dyv[:, sg * 128:(sg + 1) * 128], wc_ref[sg].astype(MXU_DTYPE), (((1,), (1,)), ((), ())),
                preferred_element_type=F32)
        rms_, rpc = _s5_tiles(are, aim, True)
        row = lax.broadcasted_iota(jnp.int32, (SUBLANES, 512), 0)

        def visit(j, blocks, acc):
            before = pl.multiple_of(jnp.maximum(j - 1, 0) * SUBLANES, SUBLANES)
            prow = X_ref[pl.ds(before, SUBLANES), :][SUBLANES - 1:SUBLANES]
            rows = pl.ds(pl.multiple_of(j * SUBLANES, SUBLANES), SUBLANES)
            are_acc, aim_acc = [], []
            for sg in range(NSG):
                lr = slice(sg * 1024, sg * 1024 + 512)
                li = slice(sg * 1024 + 512, (sg + 1) * 1024)
                ln = slice(sg * 512, (sg + 1) * 512)
                pre = jnp.where(j > 0, prow[:, lr], st_ref[0, 0, 0:1, ln])
                pim = jnp.where(j > 0, prow[:, li], st_ref[0, 0, 1:2, ln])
                xre = jnp.where(row == 0, pre, pltpu.roll(X_ref[rows, lr], 1, 0))
                xim = jnp.where(row == 0, pim, pltpu.roll(X_ref[rows, li], 1, 0))
                dre, dim = blocks[sg]
                are_acc.append(dre * xre + dim * xim)
                aim_acc.append(dim * xre - dre * xim)
            return acc[0] + jnp.concatenate(are_acc, axis=1), acc[1] + jnp.concatenate(aim_acc, axis=1)

        zero = jnp.zeros((SUBLANES, NST), F32)
        cre, cim, acc = _s5_scan(G_ref, R, rms_, rpc, car_ref[0:1], car_ref[1:2], True, visit, (zero, zero))
        car_ref[0:1] = cre
        car_ref[1:2] = cim
        dab_ref[0:1] += jnp.sum(acc[0], axis=0, keepdims=True)
        dab_ref[1:2] += jnp.sum(acc[1], axis=0, keepdims=True)
        uv = u_ref[...].astype(MXU_DTYPE)
        for sg in range(NSG):
            cs = slice(sg * 1024, (sg + 1) * 1024)
            us = slice(sg * 128, (sg + 1) * 128)
            gx = G_ref[:, cs].astype(MXU_DTYPE)
            dwb_ref[sg] += lax.dot_general(uv[:, us], gx, (((0,), (0,)), ((), ())), preferred_element_type=F32)
            dwc_ref[sg] += lax.dot_general(X_ref[:, cs].astype(MXU_DTYPE), dyv[:, us], (((0,), (0,)), ((), ())),
                                           preferred_element_type=F32)
            du_ssm = lax.dot_general(gx, wb_ref[sg].astype(MXU_DTYPE), (((1,), (1,)), ((), ())),
                                     preferred_element_type=F32)
            du_ref[:, us] = (du_ssm + du_direct[:, us]).astype(du_ref.dtype)

    rmap = lambda b, c: (b * nC + nC - 1 - c, 0)
    rows = pl.BlockSpec((R, SW), rmap)
    return pl.pallas_call(
        body, name="s5_bwd", grid=(Bl, nC),
        in_specs=[rows, rows, rows, pl.BlockSpec(d.shape, lambda b, c: (0, 0)),
                  pl.BlockSpec(wb.shape, lambda b, c: (0, 0, 0)), pl.BlockSpec(wc.shape, lambda b, c: (0, 0, 0)),
                  pl.BlockSpec(ab.shape, lambda b, c: (0, 0)),
                  pl.BlockSpec((1, 1, 2, NST), lambda b, c: (b, nC - 1 - c, 0, 0)),
                  pl.BlockSpec((R, 2 * NST), rmap)],
        out_specs=[rows, pl.BlockSpec(wb.shape, lambda b, c: (0, 0, 0)),
                   pl.BlockSpec(wc.shape, lambda b, c: (0, 0, 0)), pl.BlockSpec((2, NST), lambda b, c: (0, 0)),
                   pl.BlockSpec(d.shape, lambda b, c: (0, 0))],
        out_shape=[jax.ShapeDtypeStruct((Bl * S, SW), MXU_DTYPE), jax.ShapeDtypeStruct(wb.shape, F32),
                   jax.ShapeDtypeStruct(wc.shape, F32), jax.ShapeDtypeStruct((2, NST), F32),
                   jax.ShapeDtypeStruct(d.shape, F32)],
        scratch_shapes=[pltpu.VMEM((R, 2 * NST), F32), pltpu.VMEM((2, NST), F32)],
        compiler_params=_params(("arbitrary", "arbitrary")),
    )(u, y, do, d, wb, wc, ab, st, xs)


def _s5_disc_math(a_re, a_im, log_dt, b_re, b_im, expand):
    dt = jnp.exp(log_dt)
    z_re, z_im = a_re * dt, a_im * dt
    mag = jnp.exp(z_re)
    ab_re, ab_im = mag * jnp.cos(z_im), mag * jnp.sin(z_im)
    den = a_re * a_re + a_im * a_im
    q_re = ((ab_re - 1.0) * a_re + ab_im * a_im) / den
    q_im = (ab_im * a_re - (ab_re - 1.0) * a_im) / den
    qe_re = jnp.dot(q_re, expand, preferred_element_type=F32, precision=HIGHEST)
    qe_im = jnp.dot(q_im, expand, preferred_element_type=F32, precision=HIGHEST)
    return ab_re, ab_im, qe_re * b_re - qe_im * b_im, qe_re * b_im + qe_im * b_re


def _whole(shape):
    return pl.BlockSpec(shape, lambda nd=len(shape): (0,) * nd)


def _s5_disc(a_re, a_im, log_dt, b_re, b_im, expand):
    def body(a, b, c, d, e, f, o0, o1, o2, o3):
        res = _s5_disc_math(a[...], b[...], c[...], d[...], e[...], f[...])
        for o, v in zip((o0, o1, o2, o3), res):
            o[...] = v
    ins = (a_re, a_im, log_dt, b_re, b_im, expand)
    outs = [jax.ShapeDtypeStruct(a_re.shape, F32)] * 2 + [jax.ShapeDtypeStruct(b_re.shape, F32)] * 2
    return pl.pallas_call(body, name="s5_disc", in_specs=[_whole(x.shape) for x in ins],
                          out_specs=[_whole(o.shape) for o in outs], out_shape=outs)(*ins)


def _s5_disc_bwd(a_re, a_im, log_dt, b_re, b_im, expand, cts):
    def body(a, b, c, d, e, f, g0, g1, g2, g3, o0, o1, o2, o3, o4):
        fn = lambda *p: _s5_disc_math(*p, f[...])
        _, vjp = jax.vjp(fn, a[...], b[...], c[...], d[...], e[...])
        for o, v in zip((o0, o1, o2, o3, o4), vjp((g0[...], g1[...], g2[...], g3[...]))):
            o[...] = v
    ins = (a_re, a_im, log_dt, b_re, b_im, expand) + tuple(cts)
    outs = [jax.ShapeDtypeStruct(x.shape, F32) for x in (a_re, a_im, log_dt, b_re, b_im)]
    return pl.pallas_call(body, name="s5_disc_bwd", in_specs=[_whole(x.shape) for x in ins],
                          out_specs=[_whole(o.shape) for o in outs], out_shape=outs)(*ins)


def _ada_fwd(c_all, w_shard, b_shard):
    def body(c_ref, w_ref, b_ref, o_ref):
        cv = c_ref[...]
        o_ref[...] = _dotm(cv * _sigmoid(cv), w_ref[...]) + b_ref[...]
    n = w_shard.shape[1]
    return pl.pallas_call(
        body, name="ada_fwd", in_specs=[_whole(c_all.shape), _whole(w_shard.shape), _whole(b_shard.shape)],
        out_specs=_whole((c_all.shape[0], n)), out_shape=jax.ShapeDtypeStruct((c_all.shape[0], n), F32),
        compiler_params=_params(),
    )(c_all, w_shard, b_shard)


def _ada_bwd(c_all, dmod_cols, dmod_all):
    def body(c_ref, dc_ref, da_ref, gw_ref, gb_ref):
        cv = c_ref[...]
        gw_ref[...] = lax.dot_general((cv * _sigmoid(cv)).astype(MXU_DTYPE), dc_ref[...].astype(MXU_DTYPE),
                                      (((0,), (0,)), ((), ())), preferred_element_type=F32)
        gb_ref[...] = jnp.sum(da_ref[...], axis=0, keepdims=True)
    n = dmod_cols.shape[1]
    return pl.pallas_call(
        body, name="ada_bwd", in_specs=[_whole(c_all.shape), _whole(dmod_cols.shape), _whole(dmod_all.shape)],
        out_specs=[_whole((D, n)), _whole((1, dmod_all.shape[1]))],
        out_shape=[jax.ShapeDtypeStruct((D, n), F32), jax.ShapeDtypeStruct((1, dmod_all.shape[1]), F32)],
        compiler_params=_params(),
    )(c_all, dmod_cols, dmod_all)


def _rows_block(n_rows, cap=512):
    if n_rows <= cap:
        return n_rows
    for t in range(cap - cap % SUBLANES, 0, -SUBLANES):
        if n_rows % t == 0:
            return t
    return n_rows


def _adamw(w, g, m, v, name):
    rows, cols = w.shape
    tr = _rows_block(rows, max(SUBLANES, (1 << 19) // max(cols, 1) // SUBLANES * SUBLANES))

    def body(w_ref, g_ref, m_ref, v_ref, d_ref, nm_ref, nv_ref):
        gv = g_ref[...]
        nm = B1 * m_ref[...] + (1.0 - B1) * gv
        nv = B2 * v_ref[...] + (1.0 - B2) * (gv * gv)
        m_hat = nm / (1.0 - B1 ** STEP)
        v_hat = nv / (1.0 - B2 ** STEP)
        d_ref[...] = -LR * (m_hat / (jnp.sqrt(v_hat) + ADAM_EPS) + WD * w_ref[...])
        nm_ref[...] = nm
        nv_ref[...] = nv

    spec = pl.BlockSpec((tr, cols), lambda i: (i, 0))
    sd = jax.ShapeDtypeStruct((rows, cols), F32)
    return pl.pallas_call(body, name=name, grid=(rows // tr,), in_specs=[spec] * 4, out_specs=[spec] * 3,
                          out_shape=[sd] * 3, compiler_params=_params(("parallel",)))(w, g, m, v)


def _sum_slots(x, out_dtype, name):
    xs = x if isinstance(x, (list, tuple)) else [x]
    _, rows, cols = xs[0].shape
    tr = _rows_block(rows)

    def body(*refs):
        acc = None
        for x_ref in refs[:-1]:
            for j in range(x_ref.shape[0]):
                term = x_ref[j].astype(F32)
                acc = term if acc is None else acc + term
        refs[-1][...] = acc.astype(refs[-1].dtype)

    return pl.pallas_call(
        body, name=name, grid=(rows // tr,),
        in_specs=[pl.BlockSpec((z.shape[0], tr, cols), lambda i: (0, i, 0)) for z in xs],
        out_specs=pl.BlockSpec((tr, cols), lambda i: (i, 0)), out_shape=jax.ShapeDtypeStruct((rows, cols), out_dtype),
        compiler_params=_params(("parallel",)))(*xs)


PACK_COLS = 1024


def _pack_rows(parts, dtype, row_mult):
    flat = jnp.concatenate([p.reshape(-1).astype(dtype) for p in parts])
    per = PACK_COLS * row_mult
    n = -(-flat.shape[0] // per) * per
    flat = jnp.pad(flat, (0, n - flat.shape[0]))
    return flat.reshape(n // PACK_COLS, PACK_COLS)


def _unpack(flat, shapes):
    out, off = [], 0
    for s in shapes:
        n = math.prod(s)
        out.append(flat[off:off + n].reshape(s))
        off += n
    return out


BIG = (("w_in", (D, SHIFT + SW + 2 * D), 1), ("w_out_rwkv", (RW, D), 1), ("w_glu", (SW, 2 * D), 1),
       ("w_out", (D, D), 0), ("w_ffn_up", (D, 2 * DFF), 1), ("w_ffn_down", (DFF, D), 0))
BIG_SMALL = (("rwkv_w_up", (LW, RW), 1), ("rwkv_a_up", (LA, RW), 1), ("rwkv_g_up", (LG, RW), 1),
             ("ffn_conv_w", (3, 2 * DFF), 1))
BIG_LATE = BIG[4:]
BIG_MID = BIG[1:4]


def _shard_shape(shape, axis):
    return (shape[0] // 4, shape[1]) if axis == 0 else (shape[0], shape[1] // 4)


def _to_shards(g, axis):
    r, C = g.shape
    return g.reshape(4, r // 4, C) if axis == 0 else g.reshape(r, 4, C // 4).transpose(1, 0, 2)


def _from_shards(x, axis):
    _, r, C = x.shape
    return x.reshape(4 * r, C) if axis == 0 else x.transpose(1, 0, 2).reshape(r, 4 * C)


def kernel(x, c, w_ada, b_ada, norm1_g, w_in, mu_shift, rwkv_w0, rwkv_w_up, rwkv_a0, rwkv_a_up, rwkv_g_up, rwkv_k_k, rwkv_k_a, rwkv_r_k, rwkv_ln_g, rwkv_ln_b, w_out_rwkv, s5_a_re, s5_a_im, s5_log_dt, s5_b_re, s5_b_im, s5_c_re, s5_c_im, s5_d, w_glu, w_out, norm2_g, w_ffn_up, ffn_conv_w, ffn_conv_b, w_ffn_down, norm_f_g, loss_target, m_w_ada, m_b_ada, m_norm1_g, m_w_in, m_mu_shift, m_rwkv_w0, m_rwkv_w_up, m_rwkv_a0, m_rwkv_a_up, m_rwkv_g_up, m_rwkv_k_k, m_rwkv_k_a, m_rwkv_r_k, m_rwkv_ln_g, m_rwkv_ln_b, m_w_out_rwkv, m_s5_a_re, m_s5_a_im, m_s5_log_dt, m_s5_b_re, m_s5_b_im, m_s5_c_re, m_s5_c_im, m_s5_d, m_w_glu, m_w_out, m_norm2_g, m_w_ffn_up, m_ffn_conv_w, m_ffn_conv_b, m_w_ffn_down, m_norm_f_g, v_w_ada, v_b_ada, v_norm1_g, v_w_in, v_mu_shift, v_rwkv_w0, v_rwkv_w_up, v_rwkv_a0, v_rwkv_a_up, v_rwkv_g_up, v_rwkv_k_k, v_rwkv_k_a, v_rwkv_r_k, v_rwkv_ln_g, v_rwkv_ln_b, v_w_out_rwkv, v_s5_a_re, v_s5_a_im, v_s5_log_dt, v_s5_b_re, v_s5_b_im, v_s5_c_re, v_s5_c_im, v_s5_d, v_w_glu, v_w_out, v_norm2_g, v_w_ffn_up, v_ffn_conv_w, v_ffn_conv_b, v_w_ffn_down, v_norm_f_g):
    names = ["w_ada", "b_ada", "norm1_g", "w_in", "mu_shift", "rwkv_w0", "rwkv_w_up", "rwkv_a0", "rwkv_a_up",
             "rwkv_g_up", "rwkv_k_k", "rwkv_k_a", "rwkv_r_k", "rwkv_ln_g", "rwkv_ln_b", "w_out_rwkv", "s5_a_re",
             "s5_a_im", "s5_log_dt", "s5_b_re", "s5_b_im", "s5_c_re", "s5_c_im", "s5_d", "w_glu", "w_out", "norm2_g",
             "w_ffn_up", "ffn_conv_w", "ffn_conv_b", "w_ffn_down", "norm_f_g"]
    env = dict(locals())
    W = {n: env[n] for n in names}
    M = {n: env["m_" + n] for n in names}
    V = {n: env["v_" + n] for n in names}

    Bl, S, _ = x.shape
    T = Bl * S
    ix, iy, ic = lax.axis_index("x"), lax.axis_index("y"), lax.axis_index("c")
    chip = 2 * ix + iy
    dev = 2 * chip + ic
    rw = functools.partial(_rowwise, Bl=Bl, S=S)

    got_chip, got_dev = _gather_two_level([W[n][0] for n, _, _ in BIG_SMALL[:3]], [W["ffn_conv_w"][0], c], "gather_w")
    full = {n: _from_shards(g, axis) for (n, _, axis), g in zip(BIG_SMALL[:3], got_chip)}
    full["ffn_conv_w"] = _from_shards(got_dev[0][:, 0], 1)
    c_all = got_dev[1].reshape(8 * Bl, D)
    zeros_l = jnp.zeros((LW, RW), F32)
    w_up_p = jnp.concatenate([full["rwkv_w_up"], zeros_l], axis=0)
    a_up_p = jnp.concatenate([zeros_l, full["rwkv_a_up"]], axis=0)
    g_up = full["rwkv_g_up"]
    conv_w = full["ffn_conv_w"]
    conv_wg, conv_wu = conv_w[:, :DFF], conv_w[:, DFF:]
    conv_bg, conv_bu = ffn_conv_b[:, :DFF], ffn_conv_b[:, DFF:]
    hm = jnp.kron(jnp.eye(NH, dtype=F32), jnp.ones((HD, HD), F32))

    ncol = 6 * D // 4
    b_ada_cols = lax.dynamic_slice_in_dim(b_ada, chip * ncol, ncol, 1)
    mod_part = _ada_fwd(c_all, w_ada[0], b_ada_cols)
    mod4 = _gather_two_level([], [mod_part], "gather_mod")[1][0][:, 0]
    mod4, shards = lax.optimization_barrier((mod4, [W[n][0].astype(MXU_DTYPE) for n, _, _ in BIG]))

    def push_shards(tag, arrs):
        moves = [(i, i, lambda ref, me, peer: ref, lambda ref, me, k: ref.at[_chip_of(me)]) for i in range(len(arrs))]
        lands = [jax.ShapeDtypeStruct((4,) + z.shape, z.dtype) for z in arrs]
        return _send_start("gather_%s_start" % tag, CHIP_FLIPS, arrs, lands, moves), moves

    def pushed_shards(tag, started, moves, after, group):
        owns, gots = _send_wait("gather_%s_wait" % tag, CHIP_FLIPS, started, moves, after)
        for (n, _, axis), own, got in zip(group, owns, gots):
            full[n] = _from_shards(lax.dynamic_update_slice(got, own[None], (chip, 0, 0)), axis)

    first_start, first_moves = push_shards("in", shards[:1])
    late_start, late_moves = push_shards("rest", shards[1:])
    norm1_g = norm1_g + (first_start["token"] + late_start["token"])
    mod =lax.dynamic_slice_in_dim(mod4, dev * Bl, Bl, 1).transpose(1, 0, 2).reshape(Bl, 1, 6 * D)
    SH1, SC1, GT1, SH2, SC2, GT2 = range(6)

    x2d = x.reshape(T, D)
    tgt = loss_target.reshape(T, D)

    (h1,) = rw("norm1", lambda xv, sc, sh, g: _norm_mod(xv, g, sc, sh), R=512, tiled=[(x2d, D, 0)],
               batch=[(mod, D, SC1), (mod, D, SH1)], full=[norm1_g], out_tiled=[(D, MXU_DTYPE)])
    pushed_shards("in", first_start, first_moves, h1, BIG[:1])
    w_p, w_u, w_g = full["w_in"][:, :SHIFT], full["w_in"][:, SHIFT:SHIFT + SW], full["w_in"][:, SHIFT + SW:]
    p = _mm([h1], [w_p], F32, "proj_p")
    u = _mm([h1], [w_u], F32, "proj_u")
    gates = _mm([h1], [w_g], MXU_DTYPE, "proj_g")

    prep_params = [rwkv_w0, w_up_p, rwkv_a0, a_up_p, g_up, rwkv_k_k, rwkv_k_a, hm]

    def prep_fwd(pv, ph, mu, *pp):
        ps = pv + (_shift_down(pv, ph, 1) - pv) * mu
        return _rwkv_prep(*_split_ps(ps), *pp)

    r_, w_, k_, v_, a_, b_, g_ = rw("rwkv_prep", prep_fwd, R=256, tiled=[(p, SHIFT, 0)], prev=[(p, SHIFT, 0)],
                                    full=[mu_shift] + prep_params, out_tiled=[(RW, F32)] * 7)
    y_wkv, ck = _wkv_fwd(r_, w_, k_, v_, a_, b_, Bl, S)
    r_k_row = rwkv_r_k.reshape(1, RW)
    post_params = [rwkv_ln_g, rwkv_ln_b, r_k_row, hm]
    (o_rwkv,) = rw("rwkv_post", _rwkv_post, R=256,
                   tiled=[(y_wkv, RW, 0), (r_, RW, 0), (k_, RW, 0), (v_, RW, 0), (g_, RW, 0)],
                   full=post_params, out_tiled=[(RW, MXU_DTYPE)])
    pushed_shards("rest", late_start, late_moves, o_rwkv, BIG[1:])
    y_a = _mm([o_rwkv], [full["w_out_rwkv"]], MXU_DTYPE, "out_rwkv")

    expand = jnp.kron(jnp.eye(SP, dtype=F32), jnp.ones((1, SGC), F32))
    s5_in = (s5_a_re[0], s5_a_im[0], s5_log_dt[0].reshape(NG, 1), s5_b_re[0].reshape(NG, SP * SGC),
             s5_b_im[0].reshape(NG, SP * SGC), expand)
    ab_re, ab_im, bb_re, bb_im = _s5_disc(*s5_in)
    eye8 = jnp.eye(8, dtype=F32)

    def blockdiag_in(bb):
        t = bb.reshape(NSG, 8, SP, SGC)
        return jnp.einsum("ab,sapc->sacbp", eye8, t).reshape(NSG, 128, 512)

    def blockdiag_out(cc):
        t = cc.reshape(NSG, 8, SGC, SP)
        return jnp.einsum("ab,sacp->sapbc", eye8, t).reshape(NSG, 512, 128)

    wb = jnp.concatenate([blockdiag_in(bb_re), blockdiag_in(bb_im)], axis=2).astype(MXU_DTYPE)
    wc = jnp.concatenate([blockdiag_out(s5_c_re[0]), -blockdiag_out(s5_c_im[0])], axis=1).astype(MXU_DTYPE)
    ab = jnp.stack([ab_re.reshape(NST), ab_im.reshape(NST)])
    y_ssm, s5_st, s5_x, s5o = _s5_fwd(u, wb, wc, ab, s5_d, Bl, S)
    z = _mm([s5o], [full["w_glu"]], MXU_DTYPE, "glu")
    mix_tiled = [(gates, D, 0), (gates, D, 1), (y_a, D, 0), (z, D, 0), (z, D, 1)]
    (mixed_in,) = rw("mix", lambda *a: _mix(*(v.astype(F32) for v in a)), R=256, tiled=mix_tiled,
                     out_tiled=[(D, MXU_DTYPE)])
    mixed = _mm([mixed_in], [full["w_out"]], F32, "out_proj")

    def norm2_fwd(xv, mx, gt, sc, sh, g):
        x1 = xv + gt * mx
        return x1, _norm_mod(x1, g, sc, sh)

    x1, h2 = rw("norm2", norm2_fwd, R=512, tiled=[(x2d, D, 0), (mixed, D, 0)],
                batch=[(mod, D, GT1), (mod, D, SC2), (mod, D, SH2)], full=[norm2_g],
                out_tiled=[(D, F32), (D, MXU_DTYPE)])
    up =_mm([h2], [full["w_ffn_up"]], MXU_DTYPE, "ffn_up")
    conv_tiled = [(up, 0), (up, 1)]
    conv_full = [conv_wg, conv_wu, conv_bg, conv_bu]
    cw = functools.partial(_colwise, Bl=Bl, S=S, R=128, W=DFF, strip=LANES)

    def act_fwd(*a):
        return ((_silu_gate(*_conv_act(*a)),),)

    (act,) = cw("ffn_act", act_fwd, tiled=conv_tiled, prev=conv_tiled, full=conv_full, out_tiled=[(1, MXU_DTYPE)])
    ffn = _mm([act], [full["w_ffn_down"]], F32, "ffn_down")

    def head(x1v, fv, tv, gt, g):
        x2 = x1v + gt * fv
        y, vjp = jax.vjp(_rms, x2, g)
        e = y - tv
        dx2, dg = vjp(e * (1.0 / D))
        loss = jnp.sum(e * e, keepdims=True) * jnp.ones((1, LANES), F32)
        return dx2, dx2 * gt, jnp.sum(dx2 * fv, axis=0, keepdims=True), dg.reshape(1, D), loss

    dx2, d_ffn, d_gt2, g_norm_f, loss_acc = rw(
        "head", head, R=512, tiled=[(x1, D, 0), (ffn, D, 0), (tgt, D, 0)], batch=[(mod, D, GT2)],
        full=[norm_f_g.reshape(1, D)], out_tiled=[(D, F32), (D, MXU_DTYPE)], out_batch=[D],
        out_acc=[(1, D), (1, LANES)])
    loss = lax.psum(0.5 / D * loss_acc[0, 0], ("x", "y", "c"))

    d_act = _mm([d_ffn], [full["w_ffn_down"]], F32, "d_act", bt=True)
    g_w_ffn_down = _mm_tn(act, d_ffn, "g_ffn_down")

    def act_bwd(ug, uu, dact, hg, hu, wg, wu, bg, bu):
        (gate, taps_g), (upv, taps_u) = _conv3(ug, hg, wg, bg), _conv3(uu, hu, wu, bu)
        _, vjp_s = jax.vjp(_silu_gate, gate, upv)
        d_gate, d_upv = vjp_s(dact)
        def taps(dh, shifted):
            return [jnp.sum(dh * s, axis=0, keepdims=True) for s in shifted] + [jnp.sum(dh, axis=0, keepdims=True)]
        return ((d_gate,), (d_upv,), *taps(d_gate, taps_g), *taps(d_upv, taps_u))

    dh_g, dh_u, *tapg = cw("ffn_act_bwd", act_bwd, tiled=conv_tiled + [(d_act, 0)], prev=conv_tiled, full=conv_full,
                           out_tiled=[(1, MXU_DTYPE), (1, MXU_DTYPE)], n_acc=8)
    g_cw_g, g_cb_g = jnp.concatenate(tapg[0:3], axis=0), tapg[3]
    g_cw_u, g_cb_u = jnp.concatenate(tapg[4:7], axis=0), tapg[7]

    def conv_t(dg, du_, ng, nu, wg, wu):
        dg, du_, ng, nu = (z.astype(F32) for z in (dg, du_, ng, nu))

        def ct(d, n, w):
            return w[2:3] * d + w[1:2] * _shift_up(d, n, 1) + w[0:1] * _shift_up(d, n, 2)
        return ((ct(dg, ng, wg), ct(du_, nu, wu)),)

    (d_up,) = cw("conv_bwd", conv_t, tiled=[(dh_g, 0), (dh_u, 0)], nxt=[(dh_g, 0), (dh_u, 0)],
                 full=[conv_wg, conv_wu], out_tiled=[(2, MXU_DTYPE)])
    d_h2 = _mm([d_up], [full["w_ffn_up"]], F32, "d_h2", bt=True)
    g_w_ffn_up = _mm_tn(h2, d_up, "g_ffn_up")

    sds = jax.ShapeDtypeStruct
    reduce_src = lambda r: (lambda ref, me, peer: ref.at[_chip_of(peer), _half(r, peer[2])])

    def reduced_halves(tag, started, moves, after):
        gsh_own, got = _send_wait("rs_%s_wait" % tag, ALL_FLIPS, started, moves, after)
        halves = []
        for i, (g, gt) in enumerate(zip(gsh_own, got)):
            h = g.shape[1] // 2
            own = lax.dynamic_slice(g, (chip, ic * h, 0), (1, h, g.shape[2]))
            halves.append(_sum_slots([own, gt], F32, "rs_%s_sum%d" % (tag, i)))
        return halves

    def share_start(tag, halves):
        moves = [(i, i, lambda ref, me, peer: ref, lambda ref, me, k: ref) for i in range(len(halves))]
        return _send_start("share_%s_start" % tag, PAIR_FLIPS, halves, [sds(g.shape, F32) for g in halves], moves), moves

    def share_finish(tag, started, moves, after, group, grads):
        mine_h, got_h = _send_wait("share_%s_wait" % tag, PAIR_FLIPS, started, moves, after)
        for (n, _, _), mh, gh in zip(group, mine_h, got_h):
            grads[n] = jnp.concatenate([jnp.where(ic == 0, mh, gh), jnp.where(ic == 0, gh, mh)], axis=0)[None]

    def reduce_start(tag, group, mats):
        gsh = [_to_shards(g, ax).astype(MXU_DTYPE) for g, (_, _, ax) in zip(mats, group)]
        moves = [(i, i, reduce_src(g.shape[1]), lambda ref, me, k: ref.at[k]) for i, g in enumerate(gsh)]
        lands = [sds((len(ALL_FLIPS), g.shape[1] // 2, g.shape[2]), MXU_DTYPE) for g in gsh]
        return _send_start("rs_%s_start" % tag, ALL_FLIPS, gsh, lands, moves), moves

    rsl, rsl_moves = reduce_start("ffn", BIG_LATE, (g_w_ffn_up, g_w_ffn_down))
    norm2_g = norm2_g + rsl["token"]

    def norm2_bwd(x1v, dh2, dx2v, mx, gt, sc, sh, g):
        _, vjp = jax.vjp(_norm_mod, x1v, g, sc, sh)
        dxn, dg, dsc, dsh = vjp(dh2)
        dx1 = dx2v + dxn
        return dx1, dx1 * gt, jnp.sum(dx1 * mx, axis=0, keepdims=True), dsc, dsh, dg

    dx1, d_mixed, d_gt1, d_sc2, d_sh2, g_norm2 = rw(
        "norm2_bwd", norm2_bwd, R=512, tiled=[(x1, D, 0), (d_h2, D, 0), (dx2, D, 0), (mixed, D, 0)],
        batch=[(mod, D, GT1), (mod, D, SC2), (mod, D, SH2)], full=[norm2_g],
        out_tiled=[(D, F32), (D, MXU_DTYPE)], out_batch=[D, D, D], out_acc=[(1, D)])

    d_mixed_in = _mm([d_mixed], [full["w_out"]], MXU_DTYPE, "d_mixed_in", bt=True)
    g_w_out = _mm_tn(mixed_in, d_mixed, "g_w_out")

    def mix_bwd(*a):
        ga, gb, ya, za, zb, dm = (v.astype(F32) for v in a)
        _, vjp = jax.vjp(_mix, ga, gb, ya, za, zb)
        dga, dgb, dya, dza, dzb = vjp(dm)
        return jnp.concatenate([dga, dgb], axis=1), dya, jnp.concatenate([dza, dzb], axis=1)

    d_gates, d_ya, d_z = rw("mix_bwd", mix_bwd, R=256, tiled=mix_tiled + [(d_mixed_in, D, 0)],
                            out_tiled=[(2 * D, MXU_DTYPE), (D, MXU_DTYPE), (2 * D, MXU_DTYPE)])
    d_o_rwkv = _mm([d_ya], [full["w_out_rwkv"]], F32, "d_o_rwkv", bt=True)
    g_w_out_rwkv = _mm_tn(o_rwkv, d_ya, "g_out_rwkv")
    d_s5o = _mm([d_z], [full["w_glu"]], F32, "d_s5o", bt=True)
    g_w_glu = _mm_tn(s5o, d_z, "g_glu")
    rsm, rsm_moves = reduce_start("mid", BIG_MID, (g_w_out_rwkv, g_w_glu, g_w_out))
    s5_d = s5_d + rsm["token"]

    d_u, d_wb, d_wc, d_ab, g_s5_d = _s5_bwd(u, y_ssm, d_s5o, s5_d, wb, wc, ab, s5_st, s5_x, Bl, S)

    def diag_in(dw):
        t = dw.reshape(NSG, 8, SGC, 8, SP)
        return jnp.einsum("ab,sacbp->sapc", eye8, t).reshape(NG, SP * SGC)

    def diag_out(dw):
        t = dw.reshape(NSG, 8, SP, 8, SGC)
        return jnp.einsum("ab,sapbc->sacp", eye8, t).reshape(NG, SGC, SP)

    g_s5_c_re = diag_out(d_wc[:, :512])
    g_s5_c_im = -diag_out(d_wc[:, 512:])
    disc_cts = (d_ab[0].reshape(NG, SP), d_ab[1].reshape(NG, SP), diag_in(d_wb[:, :, :512]), diag_in(d_wb[:, :, 512:]))
    g_a_re, g_a_im, g_log_dt, g_b_re, g_b_im = _s5_disc_bwd(*s5_in, disc_cts)

    def post_bwd(yv, rv, kv, vv, gv, do, *pp):
        _, vjp = jax.vjp(lambda *a: _rwkv_post(*a, pp[3]), yv, rv, kv, vv, gv, *pp[:3])
        return vjp(do)

    dy_wkv, dr_b, dk_b, dv_b, dg_, g_ln_g, g_ln_b, g_r_k = rw(
        "rwkv_post_bwd", post_bwd, R=256,
        tiled=[(y_wkv, RW, 0), (r_, RW, 0), (k_, RW, 0), (v_, RW, 0), (g_, RW, 0), (d_o_rwkv, RW, 0)],
        full=post_params, out_tiled=[(RW, F32)] * 5, out_acc=[(1, RW)] * 3)
    dr3, dw3, dk3, dv3, da3, db3 = _wkv_bwd(r_, w_, k_, v_, a_, b_, dy_wkv, ck, Bl, S)

    shl, shl_moves = share_start("ffn", reduced_halves("ffn", rsl, rsl_moves, dr3))
    shm, shm_moves = share_start("mid", reduced_halves("mid", rsm, rsm_moves, dr3))
    mu_shift = mu_shift + (shl["token"] + shm["token"])

    def prep_bwd(pv, dr1, dr2, dwv, dk1, dk2, dv1, dv2, dav, dbv, dgv, ph, mu, *pp):
        prev = _shift_down(pv, ph, 1)
        ps = pv + (prev - pv) * mu
        _, vjp = jax.vjp(lambda *q: _rwkv_prep(*q, pp[7]), *_split_ps(ps), *pp[:7])
        grads = vjp((dr1 + dr2, dwv, dk1 + dk2, dv1 + dv2, dav, dbv, dgv))
        dps = jnp.concatenate(grads[:5], axis=1)
        return (dps,) + tuple(grads[5:]) + (jnp.sum(dps * (prev - pv), axis=0, keepdims=True),)

    prep_outs = rw(
        "rwkv_prep_bwd", prep_bwd, R=256,
        tiled=[(p, SHIFT, 0), (dr3, RW, 0), (dr_b, RW, 0), (dw3, RW, 0), (dk3, RW, 0), (dk_b, RW, 0),
               (dv3, RW, 0), (dv_b, RW, 0), (da3, RW, 0), (db3, RW, 0), (dg_, RW, 0)],
        prev=[(p, SHIFT, 0)], full=[mu_shift] + prep_params,
        out_tiled=[(SHIFT, F32)],
        out_acc=[(1, RW), (LW + LA, RW), (1, RW), (LW + LA, RW), (LG, RW), (1, RW), (1, RW), (1, SHIFT)])
    d_ps, g_w0, g_w_up_p, g_a0, g_a_up_p, g_g_up, g_k_k, g_k_a, g_mu = prep_outs

    small = {"mu_shift": g_mu, "rwkv_w0": g_w0, "rwkv_a0": g_a0, "rwkv_k_k": g_k_k,
             "rwkv_k_a": g_k_a, "rwkv_r_k": g_r_k, "rwkv_ln_g": g_ln_g, "rwkv_ln_b": g_ln_b, "s5_a_re": g_a_re,
             "s5_a_im": g_a_im, "s5_log_dt": g_log_dt, "s5_b_re": g_b_re, "s5_b_im": g_b_im, "s5_c_re": g_s5_c_re,
             "s5_c_im": g_s5_c_im, "s5_d": g_s5_d, "norm2_g": g_norm2,
             "ffn_conv_b": jnp.concatenate([g_cb_g, g_cb_u], axis=1), "norm_f_g": g_norm_f}
    small_names = list(small)
    g_conv_w = jnp.concatenate([g_cw_g, g_cw_u], axis=1)
    shard_small = {"rwkv_w_up": g_w_up_p[:LW], "rwkv_a_up": g_a_up_p[LW:], "rwkv_g_up": g_g_up, "ffn_conv_w": g_conv_w}
    parts = [small[n] for n in small_names] + [_to_shards(shard_small[n], ax) for n, _, ax in BIG_SMALL]
    spack = _pack_rows(parts, F32, SUBLANES)
    sm_moves = [(0, 0, lambda ref, me, peer: ref, lambda ref, me, k: ref.at[2 * _chip_of(me) + me[2]])]
    sm = _send_start("gsmall_start", ALL_FLIPS, [spack], [sds((8,) + spack.shape, F32)], sm_moves)
    mu_shift = mu_shift + sm["token"]

    def shift_bwd(dps, nx, mu):
        return dps * (1.0 - mu) + _shift_up(dps * mu, nx * mu, 1)

    (d_p,) = rw("shift_bwd", shift_bwd, R=256, tiled=[(d_ps, SHIFT, 0)], nxt=[(d_ps, SHIFT, 0)], full=[mu_shift],
                out_tiled=[(SHIFT, MXU_DTYPE)])
    g_w_in = jnp.concatenate([_mm_tn(h1, d_p, "g_w_p"), _mm_tn(h1, d_u, "g_w_u"), _mm_tn(h1, d_gates, "g_w_g")], axis=1)
    rsn, rsn_moves = reduce_start("in", BIG[:1], (g_w_in,))
    norm1_g = norm1_g + rsn["token"]
    d_h1 = _mm([d_p, d_u, d_gates], [w_p, w_u, w_g], F32, "d_h1", bt=True)

    def norm1_bwd(xv, dh1, dx1v, sc, sh, g):
        _, vjp = jax.vjp(_norm_mod, xv, g, sc, sh)
        dxn, dg, dsc, dsh = vjp(dh1)
        return dx1v + dxn, dsc, dsh, dg

    grad_x, d_sc1, d_sh1, g_norm1 = rw(
        "norm1_bwd", norm1_bwd, R=512, tiled=[(x2d, D, 0), (d_h1, D, 0), (dx1, D, 0)],
        batch=[(mod, D, SC1), (mod, D, SH1)], full=[norm1_g], out_tiled=[(D, F32)], out_batch=[D, D], out_acc=[(1, D)])

    dmod = jnp.concatenate([d_sh1, d_sc1, d_gt1, d_sh2, d_sc2, d_gt2], axis=2).reshape(Bl, 6 * D)
    last_all = _gather_two_level([], [dmod, g_norm1], "gather_dmod")[1]
    dmod_all = last_all[0].reshape(8 * Bl, 6 * D)
    shn, shn_moves = share_start("in", reduced_halves("in", rsn, rsn_moves, dmod_all))
    dmod_cols = lax.dynamic_slice_in_dim(dmod_all, chip * ncol, ncol, 1)
    g_w_ada, g_b_ada = _ada_bwd(c_all, dmod_cols, dmod_all)

    grads = {"norm1_g": _sum_slots(last_all[1].reshape(8, 1, D), F32, "sum_norm1")}
    sm_own, sm_got = _send_wait("gsmall_wait", ALL_FLIPS, sm, sm_moves, g_b_ada)
    s_all = lax.dynamic_update_slice(sm_got[0], sm_own[0][None], (dev, 0, 0))
    s_sum = _sum_slots(s_all, F32, "sum_gsmall").reshape(-1)
    off = 0
    for n in small_names:
        grads[n] = s_sum[off:off + W[n].size].reshape(W[n].shape)
        off += W[n].size
    for n, shape, axis in BIG_SMALL:
        ss = _shard_shape(shape, axis)
        k4 = 4 * math.prod(ss)
        sh4 = s_sum[off:off + k4].reshape(4, math.prod(ss))
        grads[n] = lax.dynamic_index_in_dim(sh4, chip, 0, keepdims=False).reshape((1,) + ss)
        off += k4

    share_finish("ffn", shl, shl_moves, s_sum, BIG_LATE, grads)
    share_finish("mid", shm, shm_moves, grads[BIG_LATE[0][0]], BIG_MID, grads)
    grads["w_ada"] = g_w_ada[None]
    grads["b_ada"] = g_b_ada

    delta, new_m, new_v = {}, {}, {}
    to2 = lambda z: z.reshape(-1, z.shape[-1])

    def adamw(n):
        d_, m_, v2_ = _adamw(to2(W[n]), to2(grads[n]), to2(M[n]), to2(V[n]), "adamw_" + n)
        delta[n], new_m[n], new_v[n] = (z.reshape(W[n].shape) for z in (d_, m_, v2_))

    for n in ["w_ada"] + [b[0] for b in BIG[1:]]:
        adamw(n)
    rest = [n for n in names if n not in delta and n != "w_in"]
    packs = [_pack_rows([src[n] for n in rest], F32, SUBLANES) for src in (W, grads, M, V)]
    d_, m_, v2_ = _adamw(*packs, "adamw_small")
    shapes = [W[n].shape for n in rest]
    for dst, z in ((delta, d_), (new_m, m_), (new_v, v2_)):
        for n, val in zip(rest, _unpack(z.reshape(-1), shapes)):
            dst[n] = val
    share_finish("in", shn, shn_moves, d_, BIG[:1], grads)
    adamw("w_in")

    return (loss, grad_x.reshape(Bl, S, D), *[grads[n] for n in names], *[delta[n] for n in names],
            *[new_m[n] for n in names], *[new_v[n] for n in names])
```

```python
import functools
import math

import jax
import jax.numpy as jnp
from jax import lax
from jax.experimental import pallas as pl
from jax.experimental.pallas import tpu as pltpu

F32 = jnp.float32
BF16 = jnp.bfloat16
MXU_DTYPE = jnp.bfloat16
MESH_IDS = pl.DeviceIdType.MESH
HIGHEST = lax.Precision.HIGHEST

D = 1024
RW, NH, HD = 512, 8, 64
LW, LA, LG = 64, 64, 128
SW, SGC, NG, SP = 512, 16, 32, 64
NSG = 4
SHIFT = 3 * RW + LW + LA + LG
DFF = 2816
RMS_EPS, GN_EPS, L2_EPS = 1e-6, 64e-5, 1e-12
LR, B1, B2, ADAM_EPS, WD, STEP = 0.001, 0.9, 0.999, 1e-8, 0.01, 10
DECAY_SCALE = math.exp(-0.5)
GELU_C = math.sqrt(2.0 / math.pi)

VMEM_LIMIT = 52 * 1024 * 1024
SUBLANES, LANES = 8, 128
HALO = 16


def _pick(n, cap):
    if n <= cap:
        return n
    best = None
    for t in range(LANES, cap + 1, LANES):
        if n % t == 0:
            best = t
    assert best is not None, (n, cap)
    return best


def _params(sem=None, vmem=VMEM_LIMIT):
    return pltpu.CompilerParams(dimension_semantics=sem, vmem_limit_bytes=vmem)


def _chip_of(p):
    return 2 * p[0] + p[1]


def _me():
    return (lax.axis_index("x"), lax.axis_index("y"), lax.axis_index("c"))


def _half(rows, core):
    h = rows // 2
    return pl.ds(pl.multiple_of(core * h, 16 if h % 16 == 0 else SUBLANES), h)


_HBM =pl.BlockSpec(memory_space=pltpu.HBM)
_SEM = pl.BlockSpec(memory_space=pltpu.SEMAPHORE)
_DATAFLOW = pltpu.SideEffectType.DATAFLOW_SIDE_EFFECTING


def _split_copies(flips, moves, src_refs, land_refs, send_sems, recv_sems):
    me = _me()
    nf = len(flips)
    out = []
    for m, (si, li, src_sel, dst_sel) in enumerate(moves):
        for k, f in enumerate(flips):
            peer = tuple(1 - v if b else v for v, b in zip(me, f))
            out.append(pltpu.make_async_remote_copy(
                src_ref=src_sel(src_refs[si], me, peer), dst_ref=dst_sel(land_refs[li], me, k),
                send_sem=send_sems.at[m * nf + k], recv_sem=recv_sems.at[m * nf + k],
                device_id=peer, device_id_type=MESH_IDS))
    return out


def _send_start(name, flips, srcs, land_shapes, moves):
    ns, nl = len(srcs), len(land_shapes)
    n = len(moves) * len(flips)

    def body(*refs):
        for cp in _split_copies(flips, moves, refs[:ns], refs[ns:ns + nl], refs[ns + nl], refs[ns + nl + 1]):
            cp.start()
        refs[-1][...] = jnp.zeros(refs[-1].shape, F32)

    hbm = lambda z: pltpu.with_memory_space_constraint(z, pltpu.HBM)
    lands = [lax.empty(s.shape, s.dtype) for s in land_shapes]
    res = pl.pallas_call(
        body, name=name,
        out_shape=(pltpu.SemaphoreType.DMA((n,)), pltpu.SemaphoreType.DMA((n,)),
                   *[pltpu.HBM(z.shape, z.dtype) for z in srcs], *[pltpu.HBM(s.shape, s.dtype) for s in land_shapes],
                   jax.ShapeDtypeStruct((SUBLANES, LANES), F32)),
        in_specs=[_HBM] * (ns + nl),
        out_specs=(_SEM, _SEM, *[_HBM] * (ns + nl), pl.BlockSpec(memory_space=pltpu.VMEM)),
        input_output_aliases={i: 2 + i for i in range(ns + nl)},
        compiler_params=pltpu.CompilerParams(has_side_effects=_DATAFLOW),
    )(*[hbm(z) for z in srcs], *[hbm(z) for z in lands])
    return {"sems": res[:2], "srcs": list(res[2:2 + ns]), "lands": list(res[2 + ns:2 + ns + nl]), "token": res[-1][0, 0]}


def _send_wait(name, flips, started, moves, after):
    srcs, lands = started["srcs"], started["lands"]
    ns, nl = len(srcs), len(lands)

    def body(*refs):
        for cp in _split_copies(flips, moves, refs[:ns], refs[ns:ns + nl], refs[ns + nl], refs[ns + nl + 1]):
            cp.wait_send()
            cp.wait_recv()

    res = pl.pallas_call(
        body, name=name, out_shape=[pltpu.HBM(z.shape, z.dtype) for z in srcs + lands],
        in_specs=[_HBM] * (ns + nl) + [_SEM, _SEM, pl.BlockSpec(memory_space=pl.ANY)],
        out_specs=[_HBM] * (ns + nl), input_output_aliases={i: i for i in range(ns + nl)},
        compiler_params=pltpu.CompilerParams(has_side_effects=_DATAFLOW),
    )(*srcs, *lands, *started["sems"], after)
    return list(res[:ns]), list(res[ns:])


CHIP_FLIPS = ((1, 0, 0), (0, 1, 0), (1, 1, 0))
PAIR_FLIPS = ((0, 0, 1),)
ALL_FLIPS = CHIP_FLIPS + ((1, 0, 1), (0, 1, 1), (1, 1, 1)) + PAIR_FLIPS


def _gather_two_level(chip_arrs, dev_arrs, name):
    arrs = list(chip_arrs) + list(dev_arrs)
    n, nchip = len(arrs), len(chip_arrs)
    NS = 7

    def body(*refs):
        srcs, outs = refs[:n], refs[n:2 * n]
        send_sems, recv_sems, loc_sems = refs[2 * n:]
        x, y, c = _me()
        sib = (x, y, 1 - c)
        chips = [(1 - x, y), (x, 1 - y), (1 - x, 1 - y)]
        mine = 2 * x + y
        ids = [2 * cx + cy for cx, cy in chips]

        def part(i, slot, core):
            if i < nchip:
                return outs[i].at[slot, _half(arrs[i].shape[0], core)]
            return outs[i].at[slot, core]

        def rcopy(i, k, src, dst, to):
            return pltpu.make_async_remote_copy(src_ref=src, dst_ref=dst, send_sem=send_sems.at[i * NS + k],
                                                recv_sem=recv_sems.at[i * NS + k], device_id=to, device_id_type=MESH_IDS)

        started, locs = [], []
        for i in range(n):
            own = srcs[i].at[_half(arrs[i].shape[0], c)] if i < nchip else srcs[i]
            loc = pltpu.make_async_copy(srcs[i], outs[i].at[mine] if i < nchip else outs[i].at[mine, c], loc_sems.at[i])
            loc.start()
            locs.append(loc)
            for f, chip in enumerate(chips):
                cp = rcopy(i, f, own, part(i, mine, c), (*chip, c))
                cp.start()
                started.append(cp)
            if i >= nchip:
                cp = rcopy(i, 6, own, part(i, mine, c), sib)
                cp.start()
                started.append(cp)
        for i in range(n):
            for f in range(3):
                land = part(i, ids[f], c)
                rcopy(i, f, land, land, sib).wait_recv()
                fw = rcopy(i, 3 + f, land, land, sib)
                fw.start()
                started.append(fw)
        for i in range(n):
            for f in range(3):
                land = part(i, ids[f], 1 - c)
                rcopy(i, 3 + f, land, land, sib).wait_recv()
            if i >= nchip:
                land = part(i, mine, 1 - c)
                rcopy(i, 6, land, land, sib).wait_recv()
        for cp in started:
            cp.wait_send()
        for loc in locs:
            loc.wait()

    outs = [jax.ShapeDtypeStruct((4,) + a.shape, a.dtype) for a in chip_arrs]
    outs += [jax.ShapeDtypeStruct((4, 2) + a.shape, a.dtype) for a in dev_arrs]
    res = pl.pallas_call(
        body, name=name, out_shape=outs,
        in_specs=[pl.BlockSpec(memory_space=pl.ANY)] * n, out_specs=[pl.BlockSpec(memory_space=pl.ANY)] * n,
        scratch_shapes=[pltpu.SemaphoreType.DMA((n * NS,)), pltpu.SemaphoreType.DMA((n * NS,)),
                        pltpu.SemaphoreType.DMA((n,))],
    )(*arrs)
    return res[:nchip], res[nchip:]


def _mm(As, Bs, out_dtype, name, tm=512, cap=1408, bt=False):
    n = len(As)
    M, N = As[0].shape[0], Bs[0].shape[0 if bt else 1]
    if sum(a.shape[1] for a in As) <= 1024:
        tm = 2 * tm
    tm = min(tm, M)
    tn = _pick(N, cap)
    dims = (((1,), (1,)), ((), ())) if bt else (((1,), (0,)), ((), ()))

    def body(*refs):
        o = refs[2 * n]
        acc = None
        for a, b in zip(refs[:n], refs[n:2 * n]):
            d = lax.dot_general(a[...].astype(MXU_DTYPE), b[...].astype(MXU_DTYPE), dims, preferred_element_type=F32)
            acc = d if acc is None else acc + d
        o[...] = acc.astype(o.dtype)

    in_specs = [pl.BlockSpec((tm, a.shape[1]), lambda i, j: (i, 0)) for a in As]
    if bt:
        in_specs += [pl.BlockSpec((tn, b.shape[1]), lambda i, j: (j, 0)) for b in Bs]
    else:
        in_specs += [pl.BlockSpec((b.shape[0], tn), lambda i, j: (0, j)) for b in Bs]
    return pl.pallas_call(
        body, name=name, grid=(M // tm, N // tn), in_specs=in_specs,
        out_specs=pl.BlockSpec((tm, tn), lambda i, j: (i, j)),
        out_shape=jax.ShapeDtypeStruct((M, N), out_dtype),
        compiler_params=_params(("parallel", "parallel")),
    )(*As, *Bs)


def _mm_tn(A, G, name, tt=1024, cap=1408):
    T, Ka = A.shape
    N = G.shape[1]
    tt = min(tt, T)
    tk = _pick(Ka, cap)
    tn = _pick(N, cap)

    def body(a, g, o):
        @pl.when(pl.program_id(2) == 0)
        def _():
            o[...] = jnp.zeros(o.shape, F32)
        o[...] += lax.dot_general(a[...].astype(MXU_DTYPE), g[...].astype(MXU_DTYPE),
                                  (((0,), (0,)), ((), ())), preferred_element_type=F32)

    return pl.pallas_call(
        body, name=name, grid=(Ka // tk, N // tn, T // tt),
        in_specs=[pl.BlockSpec((tt, tk), lambda i, j, t: (t, i)), pl.BlockSpec((tt, tn), lambda i, j, t: (t, j))],
        out_specs=pl.BlockSpec((tk, tn), lambda i, j, t: (i, j)),
        out_shape=jax.ShapeDtypeStruct((Ka, N), F32),
        compiler_params=_params(("parallel", "parallel", "arbitrary")),
    )(A, G)


def _rowwise(name, fn, *, Bl, S, R, tiled=(), prev=(), nxt=(), batch=(), full=(),
             out_tiled=(), out_batch=(), out_acc=()):
    R = min(R, S)
    nS = S // R
    T = Bl * S
    hb = R // HALO
    n_in = len(tiled) + len(prev) + len(nxt) + len(batch) + len(full)

    in_specs, args = [], []
    for a, wd, cb in tiled:
        in_specs.append(pl.BlockSpec((R, wd), lambda b, i, cb=cb: (b * nS + i, cb)))
        args.append(a)
    for a, wd, cb in prev:
        in_specs.append(pl.BlockSpec((HALO, wd), lambda b, i, cb=cb: (jnp.maximum((b * nS + i) * hb - 1, 0), cb)))
        args.append(a)
    for a, wd, cb in nxt:
        in_specs.append(pl.BlockSpec((HALO, wd), lambda b, i, cb=cb: (jnp.minimum((b * nS + i + 1) * hb, T // HALO - 1), cb)))
        args.append(a)
    for a, wd, cb in batch:
        in_specs.append(pl.BlockSpec((1, 1, wd), lambda b, i, cb=cb: (b, 0, cb)))
        args.append(a)
    for a in full:
        in_specs.append(pl.BlockSpec(a.shape, lambda b, i, nd=a.ndim: (0,) * nd))
        args.append(a)

    out_specs, out_shape = [], []
    for C, dt in out_tiled:
        out_specs.append(pl.BlockSpec((R, C), lambda b, i: (b * nS + i, 0)))
        out_shape.append(jax.ShapeDtypeStruct((T, C), dt))
    for C in out_batch:
        out_specs.append(pl.BlockSpec((1, 1, C), lambda b, i: (b, 0, 0)))
        out_shape.append(jax.ShapeDtypeStruct((Bl, 1, C), F32))
    for shp in out_acc:
        out_specs.append(pl.BlockSpec(shp, lambda b, i, nd=len(shp): (0,) * nd))
        out_shape.append(jax.ShapeDtypeStruct(shp, F32))

    nt, npv, nnx, nbt = len(tiled), len(prev), len(nxt), len(batch)

    def body(*refs):
        b, i = pl.program_id(0), pl.program_id(1)
        ins, outs = refs[:n_in], refs[n_in:]
        vals = [r[...] for r in ins[:nt]]
        vals += [jnp.where(i > 0, r[...], jnp.zeros(r.shape, r.dtype)) for r in ins[nt:nt + npv]]
        vals += [jnp.where(i < nS - 1, r[...], jnp.zeros(r.shape, r.dtype)) for r in ins[nt + npv:nt + npv + nnx]]
        vals += [r[0] for r in ins[nt + npv + nnx:nt + npv + nnx + nbt]]
        vals += [r[...] for r in ins[nt + npv + nnx + nbt:]]
        res = fn(*vals)
        if not isinstance(res, (tuple, list)):
            res = (res,)
        k = 0
        for _ in out_tiled:
            outs[k][...] = res[k].astype(outs[k].dtype)
            k += 1
        for _ in out_batch:
            o = outs[k]

            @pl.when(i == 0)
            def _(o=o):
                o[...] = jnp.zeros(o.shape, F32)
            o[0] += res[k]
            k += 1
        for _ in out_acc:
            o = outs[k]

            @pl.when((i == 0) & (b == 0))
            def _(o=o):
                o[...] = jnp.zeros(o.shape, F32)
            o[...] += res[k]
            k += 1

    out = pl.pallas_call(
        body, name=name, grid=(Bl, nS), in_specs=in_specs, out_specs=out_specs, out_shape=out_shape,
        compiler_params=_params(("arbitrary", "arbitrary")),
    )(*args)
    return out


def _colwise(name, fn, *, Bl, S, R, W, strip, tiled=(), prev=(), nxt=(), full=(), out_tiled=(), n_acc=0):
    R = min(R, S)
    nS = S // R
    T = Bl * S
    hb = R // HALO
    nt, npv, nnx, nfl = len(tiled), len(prev), len(nxt), len(full)
    n_in = nt + npv + nnx + nfl
    in_specs = [pl.BlockSpec((R, W), lambda b, i, cb=cb: (b * nS + i, cb)) for _, cb in tiled]
    in_specs += [pl.BlockSpec((HALO, W), lambda b, i, cb=cb: (jnp.maximum((b * nS + i) * hb - 1, 0), cb)) for _, cb in prev]
    in_specs += [pl.BlockSpec((HALO, W), lambda b, i, cb=cb: (jnp.minimum((b * nS + i + 1) * hb, T // HALO - 1), cb))
                 for _, cb in nxt]
    in_specs += [pl.BlockSpec(a.shape, lambda b, i: (0, 0)) for a in full]
    out_specs = [pl.BlockSpec((R, m * W), lambda b, i: (b * nS + i, 0)) for m, _ in out_tiled]
    out_specs += [pl.BlockSpec((1, W), lambda b, i: (0, 0))] * n_acc
    out_shape = [jax.ShapeDtypeStruct((T, m * W), dt) for m, dt in out_tiled] + [jax.ShapeDtypeStruct((1, W), F32)] * n_acc

    def body(*refs):
        b, i = pl.program_id(0), pl.program_id(1)
        ins, outs = refs[:n_in], refs[n_in:]

        @pl.when((i == 0) & (b == 0))
        def _():
            for o in outs[len(out_tiled):]:
                o[...] = jnp.zeros(o.shape, F32)

        def col(j, carry):
            cs = pl.ds(pl.multiple_of(j * strip, strip), strip)
            vals = [r[:, cs] for r in ins[:nt]]
            vals += [jnp.where(i > 0, r[:, cs], jnp.zeros((HALO, strip), r.dtype)) for r in ins[nt:nt + npv]]
            vals += [jnp.where(i < nS - 1, r[:, cs], jnp.zeros((HALO, strip), r.dtype)) for r in ins[nt + npv:nt + npv + nnx]]
            vals += [r[:, cs] for r in ins[nt + npv + nnx:]]
            res = fn(*vals)
            for k, (m, _) in enumerate(out_tiled):
                for q in range(m):
                    outs[k][:, pl.ds(pl.multiple_of(q * W + j * strip, strip), strip)] = res[k][q].astype(outs[k].dtype)
            for k in range(len(out_tiled), len(outs)):
                outs[k][:, cs] += res[k]
            return carry

        lax.fori_loop(0, W // strip, col, 0)

    return pl.pallas_call(
        body, name=name, grid=(Bl, nS), in_specs=in_specs, out_specs=out_specs, out_shape=out_shape,
        compiler_params=_params(("arbitrary", "arbitrary")),
    )(*[a for a, _ in tiled], *[a for a, _ in prev], *[a for a, _ in nxt], *full)


def _shift_down(x, halo, k):
    rolled = pltpu.roll(x, k, 0)
    row = lax.broadcasted_iota(jnp.int32, (SUBLANES, x.shape[1]), 0)
    head = rolled[0:SUBLANES]
    for j in range(k):
        head = jnp.where(row == j, halo[HALO - k + j:HALO - k + j + 1, :], head)
    return jnp.concatenate([head, rolled[SUBLANES:]], axis=0)


def _shift_up(x, halo, k):
    n = x.shape[0]
    rolled = pltpu.roll(x, n - k, 0)
    row = lax.broadcasted_iota(jnp.int32, (SUBLANES, x.shape[1]), 0)
    tail = rolled[n - SUBLANES:]
    for j in range(k):
        tail = jnp.where(row == SUBLANES - k + j, halo[j:j + 1, :], tail)
    return jnp.concatenate([rolled[:n - SUBLANES], tail], axis=0)


def _dotm(a, b):
    return jnp.dot(a.astype(MXU_DTYPE), b.astype(MXU_DTYPE), preferred_element_type=F32)


def _split_bf16(x):
    hi = x.astype(BF16)
    return hi, (x - hi.astype(F32)).astype(BF16)


def _headsum_2pass(x, hm):
    hi, lo = _split_bf16(x)
    hb = hm.astype(BF16)
    return jnp.dot(hi, hb, preferred_element_type=F32) + jnp.dot(lo, hb, preferred_element_type=F32)


@jax.custom_vjp
def _headsum(x, hm):
    return _headsum_2pass(x, hm)


_headsum.defvjp(lambda x, hm: (_headsum_2pass(x, hm), hm),
                lambda hm, g: (_headsum_2pass(g, hm), jnp.zeros_like(hm)))


def _sigmoid(x):
    return 0.5 * jnp.tanh(0.5 * x) + 0.5


def _rms(x, g):
    return x * lax.rsqrt(jnp.mean(x * x, axis=-1, keepdims=True) + RMS_EPS) * g


def _norm_mod(x, g, sc, sh):
    return _rms(x, g) * (1.0 + sc) + sh


def _split_ps(ps):
    return (ps[:, 0:RW], ps[:, RW:2 * RW], ps[:, 2 * RW:3 * RW], ps[:, 3 * RW:3 * RW + LW + LA],
            ps[:, 3 * RW + LW + LA:SHIFT])


def _rwkv_prep(r, k, v, wa, gd, w0, w_up_p, a0, a_up_p, g_up, k_k, k_a, hm):
    w_raw = w0 + _dotm(jnp.tanh(wa), w_up_p)
    decay = jnp.exp(-DECAY_SCALE * _sigmoid(w_raw))
    a = _sigmoid(a0 + _dotm(wa, a_up_p))
    g = _dotm(_sigmoid(gd), g_up)
    kk = k * k_k
    kk = kk * lax.rsqrt(_headsum(kk * kk, hm) + L2_EPS)
    k2 = k * (1.0 + (a - 1.0) * k_a)
    return r, decay, k2, v, -kk, kk * a, g


def _rwkv_post(y, r, k2, v, g, ln_g, ln_b, r_k, hm):
    mean = _headsum(y, hm) * (1.0 / HD)
    yc = y - mean
    var = _headsum(yc * yc, hm) * (1.0 / HD)
    yn = yc * lax.rsqrt(var + GN_EPS) * ln_g + ln_b
    bonus = _headsum(r * k2 * r_k, hm) * v
    return (yn + bonus) * g


def _gelu(x):
    return 0.5 * x * (1.0 + jnp.tanh(GELU_C * (x + 0.044715 * (x * x * x))))


def _s5_post(yssm, u, d):
    return _gelu(yssm + d * u)


def _mix(ga, gb, ya, za, zb):
    return _sigmoid(ga) * ya + _sigmoid(gb) * (za * _sigmoid(zb))


def _conv_act(up_g, up_u, hg, hu, w_g, w_u, b_g, b_u):
    gate, upv = _conv3(up_g, hg, w_g, b_g)[0], _conv3(up_u, hu, w_u, b_u)[0]
    return gate, upv


def _conv3(x, h, w, b):
    x, h = x.astype(F32), h.astype(F32)
    s2, s1 = _shift_down(x, h, 2), _shift_down(x, h, 1)
    return b + w[0:1] * s2 + w[1:2] * s1 + w[2:3] * x, (s2, s1, x)


def _silu_gate(gate, upv):
    return gate * _sigmoid(gate) * upv


WKV_L = 64
_NT, _NN, _TN = ((1,), (1,)), ((1,), (0,)), ((0,), (0,))


def _dotw(x, y, dims):
    return lax.dot_general(x.astype(MXU_DTYPE), y.astype(MXU_DTYPE), (dims, ((), ())), preferred_element_type=F32)


def _dot3(x, y, dims):
    (xh, xl), (yh, yl) = _split_bf16(x), _split_bf16(y)
    d = lambda p, q: lax.dot_general(p, q, (dims, ((), ())), preferred_element_type=F32)
    return d(xh, yh) + d(xh, yl) + d(xl, yh)


@jax.custom_vjp
def _gram3(x, y):
    return _dot3(x, y, _NT)


_gram3.defvjp(lambda x, y: (_dot3(x, y, _NT), (x, y)),
              lambda res, g: (_dot3(g, res[1], _NN), _dot3(g, res[0], _TN)))


def _tri_solve_fwd(ns, xs):
    each = lambda f, *ls: tuple(f(*zs) for zs in zip(*ls))
    size = ns[0].shape[0]
    eye = (lax.broadcasted_iota(jnp.int32, (size, size), 0) == lax.broadcasted_iota(jnp.int32, (size, size), 1)).astype(F32)
    ts = each(lambda n: n + eye, ns)
    qs = ns
    for _ in range(WKV_L.bit_length() - 2):
        qs = each(lambda q: _dotw(q, q, _NN), qs)
        ts = each(lambda t, q: t + _dotw(t, q, _NN), ts, qs)
    us = each(lambda t, x: _dotw(t, x, _NN), ts, xs)
    return us, (ts, us)


def _tri_solve_bwd(res, dus):
    ts, us = res
    each = lambda f, *ls: tuple(f(*zs) for zs in zip(*ls))
    dxs = each(lambda t, du: _dotw(t, du, _TN), ts, dus)
    return each(lambda dx, u: _dotw(dx, u, _NT), dxs, us), dxs


@jax.custom_vjp
def _tri_solve(ns, xs):
    return _tri_solve_fwd(ns, xs)[0]


_tri_solve.defvjp(_tri_solve_fwd, _tri_solve_bwd)


def _wkv_chunk(s0, r, w, k, v, a, b):
    y, s1 = _wkv_chunks((s0,), (r,), (w,), (k,), (v,), (a,), (b,))
    return y[0], s1[0]


def _wkv_chunks(s0, r, w, k, v, a, b):
    each = lambda f, *ls: tuple(f(*xs) for xs in zip(*ls))
    L = r[0].shape[0]
    n2 = 2 * L
    lane_head = lax.broadcasted_iota(jnp.int32, (2, 1, 2 * HD), 2) // HD
    head_mask = (lane_head == lax.broadcasted_iota(jnp.int32, (2, 1, 2 * HD), 0)).astype(F32)
    ri = lax.broadcasted_iota(jnp.int32, (n2, n2), 0)
    ci = lax.broadcasted_iota(jnp.int32, (n2, n2), 1)
    same = (ri // L) == (ci // L)
    strict = same & ((ci % L) < (ri % L))
    incl = same & ((ci % L) <= (ri % L))
    si = lax.broadcasted_iota(jnp.int32, (2 * HD, 2 * HD), 0) // HD
    sj = lax.broadcasted_iota(jnp.int32, (2 * HD, 2 * HD), 1) // HD
    tri = (lax.broadcasted_iota(jnp.int32, (L, L), 0) >= lax.broadcasted_iota(jnp.int32, (L, L), 1)).astype(F32)

    stack = lambda z: (z[None] * head_mask).reshape(n2, 2 * HD)
    dup = lambda z: jnp.broadcast_to(z[None], (2, L, 2 * HD)).reshape(n2, 2 * HD)
    gram = _gram3
    nt, nn, tn = (lambda x, y, d=d: _dotw(x, y, d) for d in (_NT, _NN, _TN))
    add = lambda x, y: x + y

    lw = each(jnp.log, w)
    cum = each(lambda z: jnp.dot(tri, z, preferred_element_type=F32, precision=HIGHEST), lw)
    tot = each(lambda z: jnp.sum(z, axis=0, keepdims=True), lw)
    a2 = each(lambda av, cv, lv: stack(av * jnp.exp(cv - lv)), a, cum, lw)
    r2 = each(lambda rv, cv: stack(rv * jnp.exp(cv)), r, cum)
    v2 = each(stack, v)
    b2 = each(lambda bv, cv: dup(bv * jnp.exp(-cv)), b, cum)
    k2 = each(lambda kv, cv: dup(kv * jnp.exp(-cv)), k, cum)
    n_ab = each(lambda x, y: jnp.where(strict, gram(x, y), 0.0), a2, b2)
    n_ak = each(lambda x, y: jnp.where(strict, gram(x, y), 0.0), a2, k2)
    m_rb = each(lambda x, y: jnp.where(incl, gram(x, y), 0.0), r2, b2)
    m_rk = each(lambda x, y: jnp.where(incl, gram(x, y), 0.0), r2, k2)
    u = _tri_solve(n_ab, each(add, each(nt, a2, s0), each(nn, n_ak, v2)))
    y2 = each(lambda x, y, z: x + y + z, each(nt, r2, s0), each(nn, m_rb, u), each(nn, m_rk, v2))
    y = each(lambda z: jnp.sum(z.reshape(2, L, 2 * HD), axis=0), y2)
    b3 = each(lambda bv, tv, cv: dup(bv * jnp.exp(tv - cv)), b, tot, cum)
    k3 = each(lambda kv, tv, cv: dup(kv * jnp.exp(tv - cv)), k, tot, cum)
    upd = each(add, each(tn, u, b3), each(tn, v2, k3))
    s1 = each(lambda sv, tv, uv: sv * jnp.exp(tv) + jnp.where(si == sj, uv, 0.0), s0, tot, upd)
    return y, s1


NPAIR = NH // 2


def _wkv_nb(Bl):
    return 4 if Bl % 4 == 0 else 2 if Bl % 2 == 0 else 1


def _wkv_fwd(r, w, k, v, a, b, Bl, S):
    L = WKV_L
    nC = S // L
    nb = _wkv_nb(Bl)
    chains = [(bi, p, slice(p * 2 * HD, (p + 1) * 2 * HD)) for bi in range(nb) for p in range(NPAIR)]

    def body(r_ref, w_ref, k_ref, v_ref, a_ref, b_ref, y_ref, ck_ref, s_ref):
        @pl.when(pl.program_id(1) == 0)
        def _():
            s_ref[...] = jnp.zeros(s_ref.shape, F32)
        s0 = tuple(s_ref[bi, p] for bi, p, _ in chains)
        ops = [tuple(z[bi, :, cs] for bi, _, cs in chains) for z in (r_ref, w_ref, k_ref, v_ref, a_ref, b_ref)]
        y, s1 = _wkv_chunks(s0, *ops)
        for i, (bi, p, cs) in enumerate(chains):
            ck_ref[bi, 0, p] = s0[i]
            y_ref[bi, :, cs] = y[i]
            s_ref[bi, p] = s1[i]

    to3 = lambda z: z.reshape(Bl, S, RW)
    row_spec = pl.BlockSpec((nb, L, RW), lambda g, c: (g, c, 0))
    y, ck = pl.pallas_call(
        body, name="wkv_fwd", grid=(Bl // nb, nC), in_specs=[row_spec] * 6,
        out_specs=[row_spec, pl.BlockSpec((nb, 1, NPAIR, 2 * HD, 2 * HD), lambda g, c: (g, c, 0, 0, 0))],
        out_shape=[jax.ShapeDtypeStruct((Bl, S, RW), F32), jax.ShapeDtypeStruct((Bl, nC, NPAIR, 2 * HD, 2 * HD), F32)],
        scratch_shapes=[pltpu.VMEM((nb, NPAIR, 2 * HD, 2 * HD), F32)],
        compiler_params=_params(("arbitrary", "arbitrary")),
    )(*(to3(z) for z in (r, w, k, v, a, b)))
    return y.reshape(Bl * S, RW), ck


def _wkv_bwd(r, w, k, v, a, b, dy, ck, Bl, S):
    L = WKV_L
    nC = S // L
    nb = _wkv_nb(Bl)
    chains = [(bi, p, slice(p * 2 * HD, (p + 1) * 2 * HD)) for bi in range(nb) for p in range(NPAIR)]

    def body(r_ref, w_ref, k_ref, v_ref, a_ref, b_ref, dy_ref, ck_ref,
             dr_ref, dw_ref, dk_ref, dv_ref, da_ref, db_ref, ds_ref):
        @pl.when(pl.program_id(1) == 0)
        def _():
            ds_ref[...] = jnp.zeros(ds_ref.shape, F32)
        s0 = tuple(ck_ref[bi, 0, p] for bi, p, _ in chains)
        ops = [tuple(z[bi, :, cs] for bi, _, cs in chains) for z in (r_ref, w_ref, k_ref, v_ref, a_ref, b_ref)]
        cts = (tuple(dy_ref[bi, :, cs] for bi, _, cs in chains), tuple(ds_ref[bi, p] for bi, p, _ in chains))
        ds0, *grads = jax.vjp(_wkv_chunks, s0, *ops)[1](cts)
        for i, (bi, p, cs) in enumerate(chains):
            ds_ref[bi, p] = ds0[i]
            for o, g in zip((dr_ref, dw_ref, dk_ref, dv_ref, da_ref, db_ref), grads):
                o[bi, :, cs] = g[i]

    to3 = lambda z: z.reshape(Bl, S, RW)
    row_spec = pl.BlockSpec((nb, L, RW), lambda g, c: (g, nC - 1 - c, 0))
    rows = jax.ShapeDtypeStruct((Bl, S, RW), F32)
    outs = pl.pallas_call(
        body, name="wkv_bwd", grid=(Bl // nb, nC),
        in_specs=[row_spec] * 7 + [pl.BlockSpec((nb, 1, NPAIR, 2 * HD, 2 * HD), lambda g, c: (g, nC - 1 - c, 0, 0, 0))],
        out_specs=[row_spec] * 6, out_shape=[rows] * 6,
        scratch_shapes=[pltpu.VMEM((nb, NPAIR, 2 * HD, 2 * HD), F32)],
        compiler_params=_params(("arbitrary", "arbitrary")),
    )(*(to3(z) for z in (r, w, k, v, a, b, dy)), ck)
    return [o.reshape(Bl * S, RW) for o in outs]


NST = NG * SP


def _cmul(ar, ai, br, bi):
    return ar * br - ai * bi, ar * bi + ai * br


def _s5_tiles(are, aim, reverse):
    if reverse:
        aim = -aim
    row = lax.broadcasted_iota(jnp.int32, (SUBLANES, NST), 0)
    pw = [(are, aim)]
    for _ in range(SUBLANES - 1):
        pw.append(_cmul(pw[-1][0], pw[-1][1], are, aim))
    bc = lambda z: jnp.broadcast_to(z, (SUBLANES, NST))
    ms = []
    for kk in (1, 2, 4):
        cond = (row < SUBLANES - kk) if reverse else (row >= kk)
        ms.append((jnp.where(cond, bc(pw[kk - 1][0]), 0.0), jnp.where(cond, bc(pw[kk - 1][1]), 0.0)))
    pr = jnp.zeros((SUBLANES, NST), F32)
    pi = jnp.zeros((SUBLANES, NST), F32)
    for i in range(SUBLANES):
        n = SUBLANES - i if reverse else i + 1
        pr = jnp.where(row == i, bc(pw[n - 1][0]), pr)
        pi = jnp.where(row == i, bc(pw[n - 1][1]), pi)
    return ms, (pr, pi)


def _s5_block(re, im, ms, pc, cre, cim, sg, reverse):
    ln = slice(sg * 512, (sg + 1) * 512)
    for (mr, mi), kk in zip(ms, (1, 2, 4)):
        sh = SUBLANES - kk if reverse else kk
        sre, sim = pltpu.roll(re, sh, 0), pltpu.roll(im, sh, 0)
        tr, ti = _cmul(mr[:, ln], mi[:, ln], sre, sim)
        re, im = re + tr, im + ti
    tr, ti = _cmul(pc[0][:, ln], pc[1][:, ln], cre[:, ln], cim[:, ln])
    return re + tr, im + ti


def _s5_scan(X_ref, n_rows, ms, pc, cre, cim, reverse, visit=None, acc0=None):
    nblk = n_rows // SUBLANES

    def it(i, carry):
        cre, cim, acc = carry
        j = nblk - 1 - i if reverse else i
        rows = pl.ds(pl.multiple_of(j * SUBLANES, SUBLANES), SUBLANES)
        edge = 0 if reverse else SUBLANES - 1
        blocks, ncre, ncim = [], [], []
        for sg in range(NSG):
            lr = slice(sg * 1024, sg * 1024 + 512)
            li = slice(sg * 1024 + 512, (sg + 1) * 1024)
            re, im = _s5_block(X_ref[rows, lr], X_ref[rows, li], ms, pc, cre, cim, sg, reverse)
            X_ref[rows, lr] = re
            X_ref[rows, li] = im
            blocks.append((re, im))
            ncre.append(re[edge:edge + 1])
            ncim.append(im[edge:edge + 1])
        if visit is not None:
            acc = visit(j, blocks, acc)
        return jnp.concatenate(ncre, axis=1), jnp.concatenate(ncim, axis=1), acc

    return lax.fori_loop(0, nblk, it, (cre, cim, acc0 if acc0 is not None else 0))


def _s5_fwd(u, wb, wc, ab, d, Bl, S, R=256):
    R = min(R, S)
    nC = S // R

    def body(u_ref, wb_ref, wc_ref, ab_ref, d_ref, y_ref, st_ref, X_ref, o_ref, car_ref):
        @pl.when(pl.program_id(1) == 0)
        def _():
            car_ref[...] = jnp.zeros(car_ref.shape, F32)
        st_ref[0, 0] = car_ref[...]
        ms, pc = _s5_tiles(ab_ref[0:1], ab_ref[1:2], False)
        for sg in range(NSG):
            X_ref[:, sg * 1024:(sg + 1) * 1024] = _dotm(u_ref[:, sg * 128:(sg + 1) * 128], wb_ref[sg])
        cre, cim, _ = _s5_scan(X_ref, R, ms, pc, car_ref[0:1], car_ref[1:2], False)
        car_ref[0:1] = cre
        car_ref[1:2] = cim
        for sg in range(NSG):
            y_ref[:, sg * 128:(sg + 1) * 128] = _dotm(X_ref[:, sg * 1024:(sg + 1) * 1024], wc_ref[sg])
        o_ref[...] = _s5_post(y_ref[...], u_ref[...], d_ref[...]).astype(o_ref.dtype)

    rows = pl.BlockSpec((R, SW), lambda b, c: (b * nC + c, 0))
    return pl.pallas_call(
        body, name="s5_fwd", grid=(Bl, nC),
        in_specs=[rows, pl.BlockSpec(wb.shape, lambda b, c: (0, 0, 0)), pl.BlockSpec(wc.shape, lambda b, c: (0, 0, 0)),
                  pl.BlockSpec(ab.shape, lambda b, c: (0, 0)), pl.BlockSpec(d.shape, lambda b, c: (0, 0))],
        out_specs=[rows, pl.BlockSpec((1, 1, 2, NST), lambda b, c: (b, c, 0, 0)),
                   pl.BlockSpec((R, 2 * NST), lambda b, c: (b * nC + c, 0)), rows],
        out_shape=[jax.ShapeDtypeStruct((Bl * S, SW), F32), jax.ShapeDtypeStruct((Bl, nC, 2, NST), F32),
                   jax.ShapeDtypeStruct((Bl * S, 2 * NST), F32), jax.ShapeDtypeStruct((Bl * S, SW), MXU_DTYPE)],
        scratch_shapes=[pltpu.VMEM((2, NST), F32)],
        compiler_params=_params(("arbitrary", "arbitrary")),
    )(u, wb, wc, ab, d)


def _s5_bwd(u, y, do, d, wb, wc, ab, st, xs, Bl, S, R=256):
    R = min(R, S)
    nC = S // R

    def body(u_ref, y_ref, do_ref, d_ref, wb_ref, wc_ref, ab_ref, st_ref, X_ref,
             du_ref, dwb_ref, dwc_ref, dab_ref, dd_ref, G_ref, car_ref):
        first = (pl.program_id(0) == 0) & (pl.program_id(1) == 0)

        @pl.when(first)
        def _():
            for o in (dwb_ref, dwc_ref, dab_ref, dd_ref):
                o[...] = jnp.zeros(o.shape, F32)

        @pl.when(pl.program_id(1) == 0)
        def _():
            car_ref[...] = jnp.zeros(car_ref.shape, F32)

        are, aim = ab_ref[0:1], ab_ref[1:2]
        dy, du_direct, dd = jax.vjp(_s5_post, y_ref[...], u_ref[...], d_ref[...])[1](do_ref[...])
        dd_ref[...] += dd
        dyv = dy.astype(MXU_DTYPE)
        for sg in range(NSG):
            G_ref[:, sg * 1024:(sg + 1) * 1024] = lax.dot_general(
                dyv[:, sg * 128:(sg + 1) * 128], wc_ref[sg].astype(MXU_DTYPE), (((1,), (1,)), ((), ())),
                preferred_element_type=F32)
        rms_, rpc = _s5_tiles(are, aim, True)
        row = lax.broadcasted_iota(jnp.int32, (SUBLANES, 512), 0)

        def visit(j, blocks, acc):
            before = pl.multiple_of(jnp.maximum(j - 1, 0) * SUBLANES, SUBLANES)
            prow = X_ref[pl.ds(before, SUBLANES), :][SUBLANES - 1:SUBLANES]
            rows = pl.ds(pl.multiple_of(j * SUBLANES, SUBLANES), SUBLANES)
            are_acc, aim_acc = [], []
            for sg in range(NSG):
                lr = slice(sg * 1024, sg * 1024 + 512)
                li = slice(sg * 1024 + 512, (sg + 1) * 1024)
                ln = slice(sg * 512, (sg + 1) * 512)
                pre = jnp.where(j > 0, prow[:, lr], st_ref[0, 0, 0:1, ln])
                pim = jnp.where(j > 0, prow[:, li], st_ref[0, 0, 1:2, ln])
                xre = jnp.where(row == 0, pre, pltpu.roll(X_ref[rows, lr], 1, 0))
                xim = jnp.where(row == 0, pim, pltpu.roll(X_ref[rows, li], 1, 0))
                dre, dim = blocks[sg]
                are_acc.append(dre * xre + dim * xim)
                aim_acc.append(dim * xre - dre * xim)
            return acc[0] + jnp.concatenate(are_acc, axis=1), acc[1] + jnp.concatenate(aim_acc, axis=1)

        zero = jnp.zeros((SUBLANES, NST), F32)
        cre, cim, acc = _s5_scan(G_ref, R, rms_, rpc, car_ref[0:1], car_ref[1:2], True, visit, (zero, zero))
        car_ref[0:1] = cre
        car_ref[1:2] = cim
        dab_ref[0:1] += jnp.sum(acc[0], axis=0, keepdims=True)
        dab_ref[1:2] += jnp.sum(acc[1], axis=0, keepdims=True)
        uv = u_ref[...].astype(MXU_DTYPE)
        for sg in range(NSG):
            cs = slice(sg * 1024, (sg + 1) * 1024)
            us = slice(sg * 128, (sg + 1) * 128)
            gx = G_ref[:, cs].astype(MXU_DTYPE)
            dwb_ref[sg] += lax.dot_general(uv[:, us], gx, (((0,), (0,)), ((), ())), preferred_element_type=F32)
            dwc_ref[sg] += lax.dot_general(X_ref[:, cs].astype(MXU_DTYPE), dyv[:, us], (((0,), (0,)), ((), ())),
                                           preferred_element_type=F32)
            du_ssm = lax.dot_general(gx, wb_ref[sg].astype(MXU_DTYPE), (((1,), (1,)), ((), ())),
                                     preferred_element_type=F32)
            du_ref[:, us] = (du_ssm + du_direct[:, us]).astype(du_ref.dtype)

    rmap = lambda b, c: (b * nC + nC - 1 - c, 0)
    rows = pl.BlockSpec((R, SW), rmap)
    return pl.pallas_call(
        body, name="s5_bwd", grid=(Bl, nC),
        in_specs=[rows, rows, rows, pl.BlockSpec(d.shape, lambda b, c: (0, 0)),
                  pl.BlockSpec(wb.shape, lambda b, c: (0, 0, 0)), pl.BlockSpec(wc.shape, lambda b, c: (0, 0, 0)),
                  pl.BlockSpec(ab.shape, lambda b, c: (0, 0)),
                  pl.BlockSpec((1, 1, 2, NST), lambda b, c: (b, nC - 1 - c, 0, 0)),
                  pl.BlockSpec((R, 2 * NST), rmap)],
        out_specs=[rows, pl.BlockSpec(wb.shape, lambda b, c: (0, 0, 0)),
                   pl.BlockSpec(wc.shape, lambda b, c: (0, 0, 0)), pl.BlockSpec((2, NST), lambda b, c: (0, 0)),
                   pl.BlockSpec(d.shape, lambda b, c: (0, 0))],
        out_shape=[jax.ShapeDtypeStruct((Bl * S, SW), MXU_DTYPE), jax.ShapeDtypeStruct(wb.shape, F32),
                   jax.ShapeDtypeStruct(wc.shape, F32), jax.ShapeDtypeStruct((2, NST), F32),
                   jax.ShapeDtypeStruct(d.shape, F32)],
        scratch_shapes=[pltpu.VMEM((R, 2 * NST), F32), pltpu.VMEM((2, NST), F32)],
        compiler_params=_params(("arbitrary", "arbitrary")),
    )(u, y, do, d, wb, wc, ab, st, xs)


def _s5_disc_math(a_re, a_im, log_dt, b_re, b_im, expand):
    dt = jnp.exp(log_dt)
    z_re, z_im = a_re * dt, a_im * dt
    mag = jnp.exp(z_re)
    ab_re, ab_im = mag * jnp.cos(z_im), mag * jnp.sin(z_im)
    den = a_re * a_re + a_im * a_im
    q_re = ((ab_re - 1.0) * a_re + ab_im * a_im) / den
    q_im = (ab_im * a_re - (ab_re - 1.0) * a_im) / den
    qe_re = jnp.dot(q_re, expand, preferred_element_type=F32, precision=HIGHEST)
    qe_im = jnp.dot(q_im, expand, preferred_element_type=F32, precision=HIGHEST)
    return ab_re, ab_im, qe_re * b_re - qe_im * b_im, qe_re * b_im + qe_im * b_re


def _whole(shape):
    return pl.BlockSpec(shape, lambda nd=len(shape): (0,) * nd)


def _s5_disc(a_re, a_im, log_dt, b_re, b_im, expand):
    def body(a, b, c, d, e, f, o0, o1, o2, o3):
        res = _s5_disc_math(a[...], b[...], c[...], d[...], e[...], f[...])
        for o, v in zip((o0, o1, o2, o3), res):
            o[...] = v
    ins = (a_re, a_im, log_dt, b_re, b_im, expand)
    outs = [jax.ShapeDtypeStruct(a_re.shape, F32)] * 2 + [jax.ShapeDtypeStruct(b_re.shape, F32)] * 2
    return pl.pallas_call(body, name="s5_disc", in_specs=[_whole(x.shape) for x in ins],
                          out_specs=[_whole(o.shape) for o in outs], out_shape=outs)(*ins)


def _s5_disc_bwd(a_re, a_im, log_dt, b_re, b_im, expand, cts):
    def body(a, b, c, d, e, f, g0, g1, g2, g3, o0, o1, o2, o3, o4):
        fn = lambda *p: _s5_disc_math(*p, f[...])
        _, vjp = jax.vjp(fn, a[...], b[...], c[...], d[...], e[...])
        for o, v in zip((o0, o1, o2, o3, o4), vjp((g0[...], g1[...], g2[...], g3[...]))):
            o[...] = v
    ins = (a_re, a_im, log_dt, b_re, b_im, expand) + tuple(cts)
    outs = [jax.ShapeDtypeStruct(x.shape, F32) for x in (a_re, a_im, log_dt, b_re, b_im)]
    return pl.pallas_call(body, name="s5_disc_bwd", in_specs=[_whole(x.shape) for x in ins],
                          out_specs=[_whole(o.shape) for o in outs], out_shape=outs)(*ins)


def _ada_fwd(c_all, w_shard, b_shard):
    def body(c_ref, w_ref, b_ref, o_ref):
        cv = c_ref[...]
        o_ref[...] = _dotm(cv * _sigmoid(cv), w_ref[...]) + b_ref[...]
    n = w_shard.shape[1]
    return pl.pallas_call(
        body, name="ada_fwd", in_specs=[_whole(c_all.shape), _whole(w_shard.shape), _whole(b_shard.shape)],
        out_specs=_whole((c_all.shape[0], n)), out_shape=jax.ShapeDtypeStruct((c_all.shape[0], n), F32),
        compiler_params=_params(),
    )(c_all, w_shard, b_shard)


def _ada_bwd(c_all, dmod_cols, dmod_all):
    def body(c_ref, dc_ref, da_ref, gw_ref, gb_ref):
        cv = c_ref[...]
        gw_ref[...] = lax.dot_general((cv * _sigmoid(cv)).astype(MXU_DTYPE), dc_ref[...].astype(MXU_DTYPE),
                                      (((0,), (0,)), ((), ())), preferred_element_type=F32)
        gb_ref[...] = jnp.sum(da_ref[...], axis=0, keepdims=True)
    n = dmod_cols.shape[1]
    return pl.pallas_call(
        body, name="ada_bwd", in_specs=[_whole(c_all.shape), _whole(dmod_cols.shape), _whole(dmod_all.shape)],
        out_specs=[_whole((D, n)), _whole((1, dmod_all.shape[1]))],
        out_shape=[jax.ShapeDtypeStruct((D, n), F32), jax.ShapeDtypeStruct((1, dmod_all.shape[1]), F32)],
        compiler_params=_params(),
    )(c_all, dmod_cols, dmod_all)


def _rows_block(n_rows, cap=512):
    if n_rows <= cap:
        return n_rows
    for t in range(cap - cap % SUBLANES, 0, -SUBLANES):
        if n_rows % t == 0:
            return t
    return n_rows


def _adamw(w, g, m, v, name):
    rows, cols = w.shape
    tr = _rows_block(rows, max(SUBLANES, (1 << 19) // max(cols, 1) // SUBLANES * SUBLANES))

    def body(w_ref, g_ref, m_ref, v_ref, d_ref, nm_ref, nv_ref):
        gv = g_ref[...]
        nm = B1 * m_ref[...] + (1.0 - B1) * gv
        nv = B2 * v_ref[...] + (1.0 - B2) * (gv * gv)
        m_hat = nm / (1.0 - B1 ** STEP)
        v_hat = nv / (1.0 - B2 ** STEP)
        d_ref[...] = -LR * (m_hat / (jnp.sqrt(v_hat) + ADAM_EPS) + WD * w_ref[...])
        nm_ref[...] = nm
        nv_ref[...] = nv

    spec = pl.BlockSpec((tr, cols), lambda i: (i, 0))
    sd = jax.ShapeDtypeStruct((rows, cols), F32)
    return pl.pallas_call(body, name=name, grid=(rows // tr,), in_specs=[spec] * 4, out_specs=[spec] * 3,
                          out_shape=[sd] * 3, compiler_params=_params(("parallel",)))(w, g, m, v)


def _sum_slots(x, out_dtype, name):
    xs = x if isinstance(x, (list, tuple)) else [x]
    _, rows, cols = xs[0].shape
    tr = _rows_block(rows)

    def body(*refs):
        acc = None
        for x_ref in refs[:-1]:
            for j in range(x_ref.shape[0]):
                term = x_ref[j].astype(F32)
                acc = term if acc is None else acc + term
        refs[-1][...] = acc.astype(refs[-1].dtype)

    return pl.pallas_call(
        body, name=name, grid=(rows // tr,),
        in_specs=[pl.BlockSpec((z.shape[0], tr, cols), lambda i: (0, i, 0)) for z in xs],
        out_specs=pl.BlockSpec((tr, cols), lambda i: (i, 0)), out_shape=jax.ShapeDtypeStruct((rows, cols), out_dtype),
        compiler_params=_params(("parallel",)))(*xs)


PACK_COLS = 1024


def _pack_rows(parts, dtype, row_mult):
    flat = jnp.concatenate([p.reshape(-1).astype(dtype) for p in parts])
    per = PACK_COLS * row_mult
    n = -(-flat.shape[0] // per) * per
    flat = jnp.pad(flat, (0, n - flat.shape[0]))
    return flat.reshape(n // PACK_COLS, PACK_COLS)


def _unpack(flat, shapes):
    out, off = [], 0
    for s in shapes:
        n = math.prod(s)
        out.append(flat[off:off + n].reshape(s))
        off += n
    return out


BIG = (("w_in", (D, SHIFT + SW + 2 * D), 1), ("w_out_rwkv", (RW, D), 1), ("w_glu", (SW, 2 * D), 1),
       ("w_out", (D, D), 0), ("w_ffn_up", (D, 2 * DFF), 1), ("w_ffn_down", (DFF, D), 0))
BIG_SMALL = (("rwkv_w_up", (LW, RW), 1), ("rwkv_a_up", (LA, RW), 1), ("rwkv_g_up", (LG, RW), 1),
             ("ffn_conv_w", (3, 2 * DFF), 1))
BIG_LATE = BIG[4:]
BIG_MID = BIG[1:4]


def _shard_shape(shape, axis):
    return (shape[0] // 4, shape[1]) if axis == 0 else (shape[0], shape[1] // 4)


def _to_shards(g, axis):
    r, C = g.shape
    return g.reshape(4, r // 4, C) if axis == 0 else g.reshape(r, 4, C // 4).transpose(1, 0, 2)


def _from_shards(x, axis):
    _, r, C = x.shape
    return x.reshape(4 * r, C) if axis == 0 else x.transpose(1, 0, 2).reshape(r, 4 * C)


def kernel(x, c, w_ada, b_ada, norm1_g, w_in, mu_shift, rwkv_w0, rwkv_w_up, rwkv_a0, rwkv_a_up, rwkv_g_up, rwkv_k_k, rwkv_k_a, rwkv_r_k, rwkv_ln_g, rwkv_ln_b, w_out_rwkv, s5_a_re, s5_a_im, s5_log_dt, s5_b_re, s5_b_im, s5_c_re, s5_c_im, s5_d, w_glu, w_out, norm2_g, w_ffn_up, ffn_conv_w, ffn_conv_b, w_ffn_down, norm_f_g, loss_target, m_w_ada, m_b_ada, m_norm1_g, m_w_in, m_mu_shift, m_rwkv_w0, m_rwkv_w_up, m_rwkv_a0, m_rwkv_a_up, m_rwkv_g_up, m_rwkv_k_k, m_rwkv_k_a, m_rwkv_r_k, m_rwkv_ln_g, m_rwkv_ln_b, m_w_out_rwkv, m_s5_a_re, m_s5_a_im, m_s5_log_dt, m_s5_b_re, m_s5_b_im, m_s5_c_re, m_s5_c_im, m_s5_d, m_w_glu, m_w_out, m_norm2_g, m_w_ffn_up, m_ffn_conv_w, m_ffn_conv_b, m_w_ffn_down, m_norm_f_g, v_w_ada, v_b_ada, v_norm1_g, v_w_in, v_mu_shift, v_rwkv_w0, v_rwkv_w_up, v_rwkv_a0, v_rwkv_a_up, v_rwkv_g_up, v_rwkv_k_k, v_rwkv_k_a, v_rwkv_r_k, v_rwkv_ln_g, v_rwkv_ln_b, v_w_out_rwkv, v_s5_a_re, v_s5_a_im, v_s5_log_dt, v_s5_b_re, v_s5_b_im, v_s5_c_re, v_s5_c_im, v_s5_d, v_w_glu, v_w_out, v_norm2_g, v_w_ffn_up, v_ffn_conv_w, v_ffn_conv_b, v_w_ffn_down, v_norm_f_g):
    names = ["w_ada", "b_ada", "norm1_g", "w_in", "mu_shift", "rwkv_w0", "rwkv_w_up", "rwkv_a0", "rwkv_a_up",
             "rwkv_g_up", "rwkv_k_k", "rwkv_k_a", "rwkv_r_k", "rwkv_ln_g", "rwkv_ln_b", "w_out_rwkv", "s5_a_re",
             "s5_a_im", "s5_log_dt", "s5_b_re", "s5_b_im", "s5_c_re", "s5_c_im", "s5_d", "w_glu", "w_out", "norm2_g",
             "w_ffn_up", "ffn_conv_w", "ffn_conv_b", "w_ffn_down", "norm_f_g"]
    env = dict(locals())
    W = {n: env[n] for n in names}
    M = {n: env["m_" + n] for n in names}
    V = {n: env["v_" + n] for n in names}

    Bl, S, _ = x.shape
    T = Bl * S
    ix, iy, ic = lax.axis_index("x"), lax.axis_index("y"), lax.axis_index("c")
    chip = 2 * ix + iy
    dev = 2 * chip + ic
    rw = functools.partial(_rowwise, Bl=Bl, S=S)

    got_chip, got_dev = _gather_two_level([W[n][0] for n, _, _ in BIG_SMALL[:3]], [W["ffn_conv_w"][0], c], "gather_w")
    full = {n: _from_shards(g, axis) for (n, _, axis), g in zip(BIG_SMALL[:3], got_chip)}
    full["ffn_conv_w"] = _from_shards(got_dev[0][:, 0], 1)
    c_all = got_dev[1].reshape(8 * Bl, D)
    zeros_l = jnp.zeros((LW, RW), F32)
    w_up_p = jnp.concatenate([full["rwkv_w_up"], zeros_l], axis=0)
    a_up_p = jnp.concatenate([zeros_l, full["rwkv_a_up"]], axis=0)
    g_up = full["rwkv_g_up"]
    conv_w = full["ffn_conv_w"]
    conv_wg, conv_wu = conv_w[:, :DFF], conv_w[:, DFF:]
    conv_bg, conv_bu = ffn_conv_b[:, :DFF], ffn_conv_b[:, DFF:]
    hm = jnp.kron(jnp.eye(NH, dtype=F32), jnp.ones((HD, HD), F32))

    ncol = 6 * D // 4
    b_ada_cols = lax.dynamic_slice_in_dim(b_ada, chip * ncol, ncol, 1)
    mod_part = _ada_fwd(c_all, w_ada[0], b_ada_cols)
    mod4 = _gather_two_level([], [mod_part], "gather_mod")[1][0][:, 0]
    mod4, shards = lax.optimization_barrier((mod4, [W[n][0].astype(MXU_DTYPE) for n, _, _ in BIG]))

    def push_shards(tag, arrs):
        moves = [(i, i, lambda ref, me, peer: ref, lambda ref, me, k: ref.at[_chip_of(me)]) for i in range(len(arrs))]
        lands = [jax.ShapeDtypeStruct((4,) + z.shape, z.dtype) for z in arrs]
        return _send_start("gather_%s_start" % tag, CHIP_FLIPS, arrs, lands, moves), moves

    def pushed_shards(tag, started, moves, after, group):
        owns, gots = _send_wait("gather_%s_wait" % tag, CHIP_FLIPS, started, moves, after)
        for (n, _, axis), own, got in zip(group, owns, gots):
            full[n] = _from_shards(lax.dynamic_update_slice(got, own[None], (chip, 0, 0)), axis)

    first_start, first_moves = push_shards("in", shards[:1])
    norm1_g = norm1_g + first_start["token"]
    mod =lax.dynamic_slice_in_dim(mod4, dev * Bl, Bl, 1).transpose(1, 0, 2).reshape(Bl, 1, 6 * D)
    SH1, SC1, GT1, SH2, SC2, GT2 = range(6)

    x2d = x.reshape(T, D)
    tgt = loss_target.reshape(T, D)

    (h1,) = rw("norm1", lambda xv, sc, sh, g: _norm_mod(xv, g, sc, sh), R=512, tiled=[(x2d, D, 0)],
               batch=[(mod, D, SC1), (mod, D, SH1)], full=[norm1_g], out_tiled=[(D, MXU_DTYPE)])
    pushed_shards("in", first_start, first_moves, h1, BIG[:1])
    full["w_in"], rest = lax.optimization_barrier((full["w_in"], shards[1:]))
    late_start, late_moves = push_shards("rest", rest)
    mu_shift = mu_shift + late_start["token"]
    w_p, w_u, w_g = full["w_in"][:, :SHIFT], full["w_in"][:, SHIFT:SHIFT + SW], full["w_in"][:, SHIFT + SW:]
    p = _mm([h1], [w_p], F32, "proj_p")
    u = _mm([h1], [w_u], F32, "proj_u")
    gates = _mm([h1], [w_g], MXU_DTYPE, "proj_g")

    prep_params = [rwkv_w0, w_up_p, rwkv_a0, a_up_p, g_up, rwkv_k_k, rwkv_k_a, hm]

    def prep_fwd(pv, ph, mu, *pp):
        ps = pv + (_shift_down(pv, ph, 1) - pv) * mu
        return _rwkv_prep(*_split_ps(ps), *pp)

    r_, w_, k_, v_, a_, b_, g_ = rw("rwkv_prep", prep_fwd, R=256, tiled=[(p, SHIFT, 0)], prev=[(p, SHIFT, 0)],
                                    full=[mu_shift] + prep_params, out_tiled=[(RW, F32)] * 7)
    y_wkv, ck = _wkv_fwd(r_, w_, k_, v_, a_, b_, Bl, S)
    r_k_row = rwkv_r_k.reshape(1, RW)
    post_params = [rwkv_ln_g, rwkv_ln_b, r_k_row, hm]
    (o_rwkv,) = rw("rwkv_post", _rwkv_post, R=256,
                   tiled=[(y_wkv, RW, 0), (r_, RW, 0), (k_, RW, 0), (v_, RW, 0), (g_, RW, 0)],
                   full=post_params, out_tiled=[(RW, MXU_DTYPE)])
    pushed_shards("rest", late_start, late_moves, o_rwkv, BIG[1:])
    y_a = _mm([o_rwkv], [full["w_out_rwkv"]], MXU_DTYPE, "out_rwkv")

    expand = jnp.kron(jnp.eye(SP, dtype=F32), jnp.ones((1, SGC), F32))
    s5_in = (s5_a_re[0], s5_a_im[0], s5_log_dt[0].reshape(NG, 1), s5_b_re[0].reshape(NG, SP * SGC),
             s5_b_im[0].reshape(NG, SP * SGC), expand)
    ab_re, ab_im, bb_re, bb_im = _s5_disc(*s5_in)
    eye8 = jnp.eye(8, dtype=F32)

    def blockdiag_in(bb):
        t = bb.reshape(NSG, 8, SP, SGC)
        return jnp.einsum("ab,sapc->sacbp", eye8, t).reshape(NSG, 128, 512)

    def blockdiag_out(cc):
        t = cc.reshape(NSG, 8, SGC, SP)
        return jnp.einsum("ab,sacp->sapbc", eye8, t).reshape(NSG, 512, 128)

    wb = jnp.concatenate([blockdiag_in(bb_re), blockdiag_in(bb_im)], axis=2).astype(MXU_DTYPE)
    wc = jnp.concatenate([blockdiag_out(s5_c_re[0]), -blockdiag_out(s5_c_im[0])], axis=1).astype(MXU_DTYPE)
    ab = jnp.stack([ab_re.reshape(NST), ab_im.reshape(NST)])
    y_ssm, s5_st, s5_x, s5o = _s5_fwd(u, wb, wc, ab, s5_d, Bl, S)
    z = _mm([s5o], [full["w_glu"]], MXU_DTYPE, "glu")
    mix_tiled = [(gates, D, 0), (gates, D, 1), (y_a, D, 0), (z, D, 0), (z, D, 1)]
    (mixed_in,) = rw("mix", lambda *a: _mix(*(v.astype(F32) for v in a)), R=256, tiled=mix_tiled,
                     out_tiled=[(D, MXU_DTYPE)])
    mixed = _mm([mixed_in], [full["w_out"]], F32, "out_proj")

    def norm2_fwd(xv, mx, gt, sc, sh, g):
        x1 = xv + gt * mx
        return x1, _norm_mod(x1, g, sc, sh)

    x1, h2 = rw("norm2", norm2_fwd, R=512, tiled=[(x2d, D, 0), (mixed, D, 0)],
                batch=[(mod, D, GT1), (mod, D, SC2), (mod, D, SH2)], full=[norm2_g],
                out_tiled=[(D, F32), (D, MXU_DTYPE)])
    up =_mm([h2], [full["w_ffn_up"]], MXU_DTYPE, "ffn_up")
    conv_tiled = [(up, 0), (up, 1)]
    conv_full = [conv_wg, conv_wu, conv_bg, conv_bu]
    cw = functools.partial(_colwise, Bl=Bl, S=S, R=128, W=DFF, strip=LANES)

    def act_fwd(*a):
        return ((_silu_gate(*_conv_act(*a)),),)

    (act,) = cw("ffn_act", act_fwd, tiled=conv_tiled, prev=conv_tiled, full=conv_full, out_tiled=[(1, MXU_DTYPE)])
    ffn = _mm([act], [full["w_ffn_down"]], F32, "ffn_down")

    def head(x1v, fv, tv, gt, g):
        x2 = x1v + gt * fv
        y, vjp = jax.vjp(_rms, x2, g)
        e = y - tv
        dx2, dg = vjp(e * (1.0 / D))
        loss = jnp.sum(e * e, keepdims=True) * jnp.ones((1, LANES), F32)
        return dx2, dx2 * gt, jnp.sum(dx2 * fv, axis=0, keepdims=True), dg.reshape(1, D), loss

    dx2, d_ffn, d_gt2, g_norm_f, loss_acc = rw(
        "head", head, R=512, tiled=[(x1, D, 0), (ffn, D, 0), (tgt, D, 0)], batch=[(mod, D, GT2)],
        full=[norm_f_g.reshape(1, D)], out_tiled=[(D, F32), (D, MXU_DTYPE)], out_batch=[D],
        out_acc=[(1, D), (1, LANES)])
    loss = lax.psum(0.5 / D * loss_acc[0, 0], ("x", "y", "c"))

    d_act = _mm([d_ffn], [full["w_ffn_down"]], F32, "d_act", bt=True)
    g_w_ffn_down = _mm_tn(act, d_ffn, "g_ffn_down")

    def act_bwd(ug, uu, dact, hg, hu, wg, wu, bg, bu):
        (gate, taps_g), (upv, taps_u) = _conv3(ug, hg, wg, bg), _conv3(uu, hu, wu, bu)
        _, vjp_s = jax.vjp(_silu_gate, gate, upv)
        d_gate, d_upv = vjp_s(dact)
        def taps(dh, shifted):
            return [jnp.sum(dh * s, axis=0, keepdims=True) for s in shifted] + [jnp.sum(dh, axis=0, keepdims=True)]
        return ((d_gate,), (d_upv,), *taps(d_gate, taps_g), *taps(d_upv, taps_u))

    dh_g, dh_u, *tapg = cw("ffn_act_bwd", act_bwd, tiled=conv_tiled + [(d_act, 0)], prev=conv_tiled, full=conv_full,
                           out_tiled=[(1, MXU_DTYPE), (1, MXU_DTYPE)], n_acc=8)
    g_cw_g, g_cb_g = jnp.concatenate(tapg[0:3], axis=0), tapg[3]
    g_cw_u, g_cb_u = jnp.concatenate(tapg[4:7], axis=0), tapg[7]

    def conv_t(dg, du_, ng, nu, wg, wu):
        dg, du_, ng, nu = (z.astype(F32) for z in (dg, du_, ng, nu))

        def ct(d, n, w):
            return w[2:3] * d + w[1:2] * _shift_up(d, n, 1) + w[0:1] * _shift_up(d, n, 2)
        return ((ct(dg, ng, wg), ct(du_, nu, wu)),)

    (d_up,) = cw("conv_bwd", conv_t, tiled=[(dh_g, 0), (dh_u, 0)], nxt=[(dh_g, 0), (dh_u, 0)],
                 full=[conv_wg, conv_wu], out_tiled=[(2, MXU_DTYPE)])
    d_h2 = _mm([d_up], [full["w_ffn_up"]], F32, "d_h2", bt=True)
    g_w_ffn_up = _mm_tn(h2, d_up, "g_ffn_up")

    sds = jax.ShapeDtypeStruct
    reduce_src = lambda r: (lambda ref, me, peer: ref.at[_chip_of(peer), _half(r, peer[2])])

    def reduced_halves(tag, started, moves, after):
        gsh_own, got = _send_wait("rs_%s_wait" % tag, ALL_FLIPS, started, moves, after)
        halves = []
        for i, (g, gt) in enumerate(zip(gsh_own, got)):
            h = g.shape[1] // 2
            own = lax.dynamic_slice(g, (chip, ic * h, 0), (1, h, g.shape[2]))
            halves.append(_sum_slots([own, gt], F32, "rs_%s_sum%d" % (tag, i)))
        return halves

    def share_start(tag, halves):
        moves = [(i, i, lambda ref, me, peer: ref, lambda ref, me, k: ref) for i in range(len(halves))]
        return _send_start("share_%s_start" % tag, PAIR_FLIPS, halves, [sds(g.shape, F32) for g in halves], moves), moves

    def share_finish(tag, started, moves, after, group, grads):
        mine_h, got_h = _send_wait("share_%s_wait" % tag, PAIR_FLIPS, started, moves, after)
        for (n, _, _), mh, gh in zip(group, mine_h, got_h):
            grads[n] = jnp.concatenate([jnp.where(ic == 0, mh, gh), jnp.where(ic == 0, gh, mh)], axis=0)[None]

    def reduce_start(tag, group, mats):
        gsh = [_to_shards(g, ax).astype(MXU_DTYPE) for g, (_, _, ax) in zip(mats, group)]
        moves = [(i, i, reduce_src(g.shape[1]), lambda ref, me, k: ref.at[k]) for i, g in enumerate(gsh)]
        lands = [sds((len(ALL_FLIPS), g.shape[1] // 2, g.shape[2]), MXU_DTYPE) for g in gsh]
        return _send_start("rs_%s_start" % tag, ALL_FLIPS, gsh, lands, moves), moves

    rsl, rsl_moves = reduce_start("ffn", BIG_LATE, (g_w_ffn_up, g_w_ffn_down))
    norm2_g = norm2_g + rsl["token"]

    def norm2_bwd(x1v, dh2, dx2v, mx, gt, sc, sh, g):
        _, vjp = jax.vjp(_norm_mod, x1v, g, sc, sh)
        dxn, dg, dsc, dsh = vjp(dh2)
        dx1 = dx2v + dxn
        return dx1, dx1 * gt, jnp.sum(dx1 * mx, axis=0, keepdims=True), dsc, dsh, dg

    dx1, d_mixed, d_gt1, d_sc2, d_sh2, g_norm2 = rw(
        "norm2_bwd", norm2_bwd, R=512, tiled=[(x1, D, 0), (d_h2, D, 0), (dx2, D, 0), (mixed, D, 0)],
        batch=[(mod, D, GT1), (mod, D, SC2), (mod, D, SH2)], full=[norm2_g],
        out_tiled=[(D, F32), (D, MXU_DTYPE)], out_batch=[D, D, D], out_acc=[(1, D)])

    d_mixed_in = _mm([d_mixed], [full["w_out"]], MXU_DTYPE, "d_mixed_in", bt=True)
    g_w_out = _mm_tn(mixed_in, d_mixed, "g_w_out")

    def mix_bwd(*a):
        ga, gb, ya, za, zb, dm = (v.astype(F32) for v in a)
        _, vjp = jax.vjp(_mix, ga, gb, ya, za, zb)
        dga, dgb, dya, dza, dzb = vjp(dm)
        return jnp.concatenate([dga, dgb], axis=1), dya, jnp.concatenate([dza, dzb], axis=1)

    d_gates, d_ya, d_z = rw("mix_bwd", mix_bwd, R=256, tiled=mix_tiled + [(d_mixed_in, D, 0)],
                            out_tiled=[(2 * D, MXU_DTYPE), (D, MXU_DTYPE), (2 * D, MXU_DTYPE)])
    d_o_rwkv = _mm([d_ya], [full["w_out_rwkv"]], F32, "d_o_rwkv", bt=True)
    g_w_out_rwkv = _mm_tn(o_rwkv, d_ya, "g_out_rwkv")
    d_s5o = _mm([d_z], [full["w_glu"]], F32, "d_s5o", bt=True)
    g_w_glu = _mm_tn(s5o, d_z, "g_glu")
    rsm, rsm_moves = reduce_start("mid", BIG_MID, (g_w_out_rwkv, g_w_glu, g_w_out))
    s5_d = s5_d + rsm["token"]

    d_u, d_wb, d_wc, d_ab, g_s5_d = _s5_bwd(u, y_ssm, d_s5o, s5_d, wb, wc, ab, s5_st, s5_x, Bl, S)

    def diag_in(dw):
        t = dw.reshape(NSG, 8, SGC, 8, SP)
        return jnp.einsum("ab,sacbp->sapc", eye8, t).reshape(NG, SP * SGC)

    def diag_out(dw):
        t = dw.reshape(NSG, 8, SP, 8, SGC)
        return jnp.einsum("ab,sapbc->sacp", eye8, t).reshape(NG, SGC, SP)

    g_s5_c_re = diag_out(d_wc[:, :512])
    g_s5_c_im = -diag_out(d_wc[:, 512:])
    disc_cts = (d_ab[0].reshape(NG, SP), d_ab[1].reshape(NG, SP), diag_in(d_wb[:, :, :512]), diag_in(d_wb[:, :, 512:]))
    g_a_re, g_a_im, g_log_dt, g_b_re, g_b_im = _s5_disc_bwd(*s5_in, disc_cts)

    def post_bwd(yv, rv, kv, vv, gv, do, *pp):
        _, vjp = jax.vjp(lambda *a: _rwkv_post(*a, pp[3]), yv, rv, kv, vv, gv, *pp[:3])
        return vjp(do)

    dy_wkv, dr_b, dk_b, dv_b, dg_, g_ln_g, g_ln_b, g_r_k = rw(
        "rwkv_post_bwd", post_bwd, R=256,
        tiled=[(y_wkv, RW, 0), (r_, RW, 0), (k_, RW, 0), (v_, RW, 0), (g_, RW, 0), (d_o_rwkv, RW, 0)],
        full=post_params, out_tiled=[(RW, F32)] * 5, out_acc=[(1, RW)] * 3)
    dr3, dw3, dk3, dv3, da3, db3 = _wkv_bwd(r_, w_, k_, v_, a_, b_, dy_wkv, ck, Bl, S)

    shl, shl_moves = share_start("ffn", reduced_halves("ffn", rsl, rsl_moves, dr3))
    shm, shm_moves = share_start("mid", reduced_halves("mid", rsm, rsm_moves, dr3))
    mu_shift = mu_shift + (shl["token"] + shm["token"])

    def prep_bwd(pv, dr1, dr2, dwv, dk1, dk2, dv1, dv2, dav, dbv, dgv, ph, mu, *pp):
        prev = _shift_down(pv, ph, 1)
        ps = pv + (prev - pv) * mu
        _, vjp = jax.vjp(lambda *q: _rwkv_prep(*q, pp[7]), *_split_ps(ps), *pp[:7])
        grads = vjp((dr1 + dr2, dwv, dk1 + dk2, dv1 + dv2, dav, dbv, dgv))
        dps = jnp.concatenate(grads[:5], axis=1)
        return (dps,) + tuple(grads[5:]) + (jnp.sum(dps * (prev - pv), axis=0, keepdims=True),)

    prep_outs = rw(
        "rwkv_prep_bwd", prep_bwd, R=256,
        tiled=[(p, SHIFT, 0), (dr3, RW, 0), (dr_b, RW, 0), (dw3, RW, 0), (dk3, RW, 0), (dk_b, RW, 0),
               (dv3, RW, 0), (dv_b, RW, 0), (da3, RW, 0), (db3, RW, 0), (dg_, RW, 0)],
        prev=[(p, SHIFT, 0)], full=[mu_shift] + prep_params,
        out_tiled=[(SHIFT, F32)],
        out_acc=[(1, RW), (LW + LA, RW), (1, RW), (LW + LA, RW), (LG, RW), (1, RW), (1, RW), (1, SHIFT)])
    d_ps, g_w0, g_w_up_p, g_a0, g_a_up_p, g_g_up, g_k_k, g_k_a, g_mu = prep_outs

    small = {"mu_shift": g_mu, "rwkv_w0": g_w0, "rwkv_a0": g_a0, "rwkv_k_k": g_k_k,
             "rwkv_k_a": g_k_a, "rwkv_r_k": g_r_k, "rwkv_ln_g": g_ln_g, "rwkv_ln_b": g_ln_b, "s5_a_re": g_a_re,
             "s5_a_im": g_a_im, "s5_log_dt": g_log_dt, "s5_b_re": g_b_re, "s5_b_im": g_b_im, "s5_c_re": g_s5_c_re,
             "s5_c_im": g_s5_c_im, "s5_d": g_s5_d, "norm2_g": g_norm2,
             "ffn_conv_b": jnp.concatenate([g_cb_g, g_cb_u], axis=1), "norm_f_g": g_norm_f}
    small_names = list(small)
    g_conv_w = jnp.concatenate([g_cw_g, g_cw_u], axis=1)
    shard_small = {"rwkv_w_up": g_w_up_p[:LW], "rwkv_a_up": g_a_up_p[LW:], "rwkv_g_up": g_g_up, "ffn_conv_w": g_conv_w}
    parts = [small[n] for n in small_names] + [_to_shards(shard_small[n], ax) for n, _, ax in BIG_SMALL]
    spack = _pack_rows(parts, F32, SUBLANES)
    sm_moves = [(0, 0, lambda ref, me, peer: ref, lambda ref, me, k: ref.at[2 * _chip_of(me) + me[2]])]
    sm = _send_start("gsmall_start", ALL_FLIPS, [spack], [sds((8,) + spack.shape, F32)], sm_moves)
    mu_shift = mu_shift + sm["token"]

    def shift_bwd(dps, nx, mu):
        return dps * (1.0 - mu) + _shift_up(dps * mu, nx * mu, 1)

    (d_p,) = rw("shift_bwd", shift_bwd, R=256, tiled=[(d_ps, SHIFT, 0)], nxt=[(d_ps, SHIFT, 0)], full=[mu_shift],
                out_tiled=[(SHIFT, MXU_DTYPE)])
    g_w_in = jnp.concatenate([_mm_tn(h1, d_p, "g_w_p"), _mm_tn(h1, d_u, "g_w_u"), _mm_tn(h1, d_gates, "g_w_g")], axis=1)
    rsn, rsn_moves = reduce_start("in", BIG[:1], (g_w_in,))
    norm1_g = norm1_g + rsn["token"]
    d_h1 = _mm([d_p, d_u, d_gates], [w_p, w_u, w_g], F32, "d_h1", bt=True)

    def norm1_bwd(xv, dh1, dx1v, sc, sh, g):
        _, vjp = jax.vjp(_norm_mod, xv, g, sc, sh)
        dxn, dg, dsc, dsh = vjp(dh1)
        return dx1v + dxn, dsc, dsh, dg

    grad_x, d_sc1, d_sh1, g_norm1 = rw(
        "norm1_bwd", norm1_bwd, R=512, tiled=[(x2d, D, 0), (d_h1, D, 0), (dx1, D, 0)],
        batch=[(mod, D, SC1), (mod, D, SH1)], full=[norm1_g], out_tiled=[(D, F32)], out_batch=[D, D], out_acc=[(1, D)])

    dmod = jnp.concatenate([d_sh1, d_sc1, d_gt1, d_sh2, d_sc2, d_gt2], axis=2).reshape(Bl, 6 * D)
    last_all = _gather_two_level([], [dmod, g_norm1], "gather_dmod")[1]
    dmod_all = last_all[0].reshape(8 * Bl, 6 * D)
    shn, shn_moves = share_start("in", reduced_halves("in", rsn, rsn_moves, dmod_all))
    dmod_cols = lax.dynamic_slice_in_dim(dmod_all, chip * ncol, ncol, 1)
    g_w_ada, g_b_ada = _ada_bwd(c_all, dmod_cols, dmod_all)

    grads = {"norm1_g": _sum_slots(last_all[1].reshape(8, 1, D), F32, "sum_norm1")}
    sm_own, sm_got = _send_wait("gsmall_wait", ALL_FLIPS, sm, sm_moves, g_b_ada)
    s_all = lax.dynamic_update_slice(sm_got[0], sm_own[0][None], (dev, 0, 0))
    s_sum = _sum_slots(s_all, F32, "sum_gsmall").reshape(-1)
    off = 0
    for n in small_names:
        grads[n] = s_sum[off:off + W[n].size].reshape(W[n].shape)
        off += W[n].size
    for n, shape, axis in BIG_SMALL:
        ss = _shard_shape(shape, axis)
        k4 = 4 * math.prod(ss)
        sh4 = s_sum[off:off + k4].reshape(4, math.prod(ss))
        grads[n] = lax.dynamic_index_in_dim(sh4, chip, 0, keepdims=False).reshape((1,) + ss)
        off += k4

    share_finish("ffn", shl, shl_moves, s_sum, BIG_LATE, grads)
    share_finish("mid", shm, shm_moves, grads[BIG_LATE[0][0]], BIG_MID, grads)
    grads["w_ada"] = g_w_ada[None]
    grads["b_ada"] = g_b_ada

    delta, new_m, new_v = {}, {}, {}
    to2 = lambda z: z.reshape(-1, z.shape[-1])

    def adamw(n):
        d_, m_, v2_ = _adamw(to2(W[n]), to2(grads[n]), to2(M[n]), to2(V[n]), "adamw_" + n)
        delta[n], new_m[n], new_v[n] = (z.reshape(W[n].shape) for z in (d_, m_, v2_))

    for n in ["w_ada"] + [b[0] for b in BIG[1:]]:
        adamw(n)
    rest = [n for n in names if n not in delta and n != "w_in"]
    packs = [_pack_rows([src[n] for n in rest], F32, SUBLANES) for src in (W, grads, M, V)]
    d_, m_, v2_ = _adamw(*packs, "adamw_small")
    shapes = [W[n].shape for n in rest]
    for dst, z in ((delta, d_), (new_m, m_), (new_v, v2_)):
        for n, val in zip(rest, _unpack(z.reshape(-1), shapes)):
            dst[n] = val
    share_finish("in", shn, shn_moves, d_, BIG[:1], grads)
    adamw("w_in")

    return (loss, grad_x.reshape(Bl, S, D), *[grads[n] for n in names], *[delta[n] for n in names],
            *[new_m[n] for n in names], *[new_v[n] for n in names])
```

```python
import functools
import math

import jax
import jax.numpy as jnp
from jax import lax
from jax.experimental import pallas as pl
from jax.experimental.pallas import tpu as pltpu

F32 = jnp.float32
BF16 = jnp.bfloat16
MXU_DTYPE = jnp.bfloat16
MESH_IDS = pl.DeviceIdType.MESH
HIGHEST = lax.Precision.HIGHEST

D = 1024
RW, NH, HD = 512, 8, 64
LW, LA, LG = 64, 64, 128
SW, SGC, NG, SP = 512, 16, 32, 64
NSG = 4
SHIFT = 3 * RW + LW + LA + LG
DFF = 2816
RMS_EPS, GN_EPS, L2_EPS = 1e-6, 64e-5, 1e-12
LR, B1, B2, ADAM_EPS, WD, STEP = 0.001, 0.9, 0.999, 1e-8, 0.01, 10
DECAY_SCALE = math.exp(-0.5)
GELU_C = math.sqrt(2.0 / math.pi)

VMEM_LIMIT = 52 * 1024 * 1024
SUBLANES, LANES = 8, 128
HALO = 16


def _pick(n, cap):
    if n <= cap:
        return n
    best = None
    for t in range(LANES, cap + 1, LANES):
        if n % t == 0:
            best = t
    assert best is not None, (n, cap)
    return best


def _params(sem=None, vmem=VMEM_LIMIT):
    return pltpu.CompilerParams(dimension_semantics=sem, vmem_limit_bytes=vmem)


def _chip_of(p):
    return 2 * p[0] + p[1]


def _me():
    return (lax.axis_index("x"), lax.axis_index("y"), lax.axis_index("c"))


def _half(rows, core):
    h = rows // 2
    return pl.ds(pl.multiple_of(core * h, 16 if h % 16 == 0 else SUBLANES), h)


_HBM =pl.BlockSpec(memory_space=pltpu.HBM)
_SEM = pl.BlockSpec(memory_space=pltpu.SEMAPHORE)
_DATAFLOW = pltpu.SideEffectType.DATAFLOW_SIDE_EFFECTING


def _split_copies(flips, moves, src_refs, land_refs, send_sems, recv_sems):
    me = _me()
    nf = len(flips)
    out = []
    for m, (si, li, src_sel, dst_sel) in enumerate(moves):
        for k, f in enumerate(flips):
            peer = tuple(1 - v if b else v for v, b in zip(me, f))
            out.append(pltpu.make_async_remote_copy(
                src_ref=src_sel(src_refs[si], me, peer), dst_ref=dst_sel(land_refs[li], me, k),
                send_sem=send_sems.at[m * nf + k], recv_sem=recv_sems.at[m * nf + k],
                device_id=peer, device_id_type=MESH_IDS))
    return out


def _send_start(name, flips, srcs, land_shapes, moves):
    ns, nl = len(srcs), len(land_shapes)
    n = len(moves) * len(flips)

    def body(*refs):
        for cp in _split_copies(flips, moves, refs[:ns], refs[ns:ns + nl], refs[ns + nl], refs[ns + nl + 1]):
            cp.start()
        refs[-1][...] = jnp.zeros(refs[-1].shape, F32)

    hbm = lambda z: pltpu.with_memory_space_constraint(z, pltpu.HBM)
    lands = [lax.empty(s.shape, s.dtype) for s in land_shapes]
    res = pl.pallas_call(
        body, name=name,
        out_shape=(pltpu.SemaphoreType.DMA((n,)), pltpu.SemaphoreType.DMA((n,)),
                   *[pltpu.HBM(z.shape, z.dtype) for z in srcs], *[pltpu.HBM(s.shape, s.dtype) for s in land_shapes],
                   jax.ShapeDtypeStruct((SUBLANES, LANES), F32)),
        in_specs=[_HBM] * (ns + nl),
        out_specs=(_SEM, _SEM, *[_HBM] * (ns + nl), pl.BlockSpec(memory_space=pltpu.VMEM)),
        input_output_aliases={i: 2 + i for i in range(ns + nl)},
        compiler_params=pltpu.CompilerParams(has_side_effects=_DATAFLOW),
    )(*[hbm(z) for z in srcs], *[hbm(z) for z in lands])
    return {"sems": res[:2], "srcs": list(res[2:2 + ns]), "lands": list(res[2 + ns:2 + ns + nl]), "token": res[-1][0, 0]}


def _send_wait(name, flips, started, moves, after):
    srcs, lands = started["srcs"], started["lands"]
    ns, nl = len(srcs), len(lands)

    def body(*refs):
        for cp in _split_copies(flips, moves, refs[:ns], refs[ns:ns + nl], refs[ns + nl], refs[ns + nl + 1]):
            cp.wait_send()
            cp.wait_recv()

    res = pl.pallas_call(
        body, name=name, out_shape=[pltpu.HBM(z.shape, z.dtype) for z in srcs + lands],
        in_specs=[_HBM] * (ns + nl) + [_SEM, _SEM, pl.BlockSpec(memory_space=pl.ANY)],
        out_specs=[_HBM] * (ns + nl), input_output_aliases={i: i for i in range(ns + nl)},
        compiler_params=pltpu.CompilerParams(has_side_effects=_DATAFLOW),
    )(*srcs, *lands, *started["sems"], after)
    return list(res[:ns]), list(res[ns:])


CHIP_FLIPS = ((1, 0, 0), (0, 1, 0), (1, 1, 0))
PAIR_FLIPS = ((0, 0, 1),)
ALL_FLIPS = CHIP_FLIPS + ((1, 0, 1), (0, 1, 1), (1, 1, 1)) + PAIR_FLIPS


def _gather_two_level(chip_arrs, dev_arrs, name):
    arrs = list(chip_arrs) + list(dev_arrs)
    n, nchip = len(arrs), len(chip_arrs)
    NS = 7

    def body(*refs):
        srcs, outs = refs[:n], refs[n:2 * n]
        send_sems, recv_sems, loc_sems = refs[2 * n:]
        x, y, c = _me()
        sib = (x, y, 1 - c)
        chips = [(1 - x, y), (x, 1 - y), (1 - x, 1 - y)]
        mine = 2 * x + y
        ids = [2 * cx + cy for cx, cy in chips]

        def part(i, slot, core):
            if i < nchip:
                return outs[i].at[slot, _half(arrs[i].shape[0], core)]
            return outs[i].at[slot, core]

        def rcopy(i, k, src, dst, to):
            return pltpu.make_async_remote_copy(src_ref=src, dst_ref=dst, send_sem=send_sems.at[i * NS + k],
                                                recv_sem=recv_sems.at[i * NS + k], device_id=to, device_id_type=MESH_IDS)

        started, locs = [], []
        for i in range(n):
            own = srcs[i].at[_half(arrs[i].shape[0], c)] if i < nchip else srcs[i]
            loc = pltpu.make_async_copy(srcs[i], outs[i].at[mine] if i < nchip else outs[i].at[mine, c], loc_sems.at[i])
            loc.start()
            locs.append(loc)
            for f, chip in enumerate(chips):
                cp = rcopy(i, f, own, part(i, mine, c), (*chip, c))
                cp.start()
                started.append(cp)
            if i >= nchip:
                cp = rcopy(i, 6, own, part(i, mine, c), sib)
                cp.start()
                started.append(cp)
        for i in range(n):
            for f in range(3):
                land = part(i, ids[f], c)
                rcopy(i, f, land, land, sib).wait_recv()
                fw = rcopy(i, 3 + f, land, land, sib)
                fw.start()
                started.append(fw)
        for i in range(n):
            for f in range(3):
                land = part(i, ids[f], 1 - c)
                rcopy(i, 3 + f, land, land, sib).wait_recv()
            if i >= nchip:
                land = part(i, mine, 1 - c)
                rcopy(i, 6, land, land, sib).wait_recv()
        for cp in started:
            cp.wait_send()
        for loc in locs:
            loc.wait()

    outs = [jax.ShapeDtypeStruct((4,) + a.shape, a.dtype) for a in chip_arrs]
    outs += [jax.ShapeDtypeStruct((4, 2) + a.shape, a.dtype) for a in dev_arrs]
    res = pl.pallas_call(
        body, name=name, out_shape=outs,
        in_specs=[pl.BlockSpec(memory_space=pl.ANY)] * n, out_specs=[pl.BlockSpec(memory_space=pl.ANY)] * n,
        scratch_shapes=[pltpu.SemaphoreType.DMA((n * NS,)), pltpu.SemaphoreType.DMA((n * NS,)),
                        pltpu.SemaphoreType.DMA((n,))],
    )(*arrs)
    return res[:nchip], res[nchip:]


def _mm(As, Bs, out_dtype, name, tm=512, cap=1408, bt=False):
    n = len(As)
    M, N = As[0].shape[0], Bs[0].shape[0 if bt else 1]
    if sum(a.shape[1] for a in As) <= 1024:
        tm = 2 * tm
    tm = min(tm, M)
    tn = _pick(N, cap)
    dims = (((1,), (1,)), ((), ())) if bt else (((1,), (0,)), ((), ()))

    def body(*refs):
        o = refs[2 * n]
        acc = None
        for a, b in zip(refs[:n], refs[n:2 * n]):
            d = lax.dot_general(a[...].astype(MXU_DTYPE), b[...].astype(MXU_DTYPE), dims, preferred_element_type=F32)
            acc = d if acc is None else acc + d
        o[...] = acc.astype(o.dtype)

    in_specs = [pl.BlockSpec((tm, a.shape[1]), lambda i, j: (i, 0)) for a in As]
    if bt:
        in_specs += [pl.BlockSpec((tn, b.shape[1]), lambda i, j: (j, 0)) for b in Bs]
    else:
        in_specs += [pl.BlockSpec((b.shape[0], tn), lambda i, j: (0, j)) for b in Bs]
    return pl.pallas_call(
        body, name=name, grid=(M // tm, N // tn), in_specs=in_specs,
        out_specs=pl.BlockSpec((tm, tn), lambda i, j: (i, j)),
        out_shape=jax.ShapeDtypeStruct((M, N), out_dtype),
        compiler_params=_params(("parallel", "parallel")),
    )(*As, *Bs)


def _mm_tn(A, G, name, tt=1024, cap=1408):
    T, Ka = A.shape
    N = G.shape[1]
    tt = min(tt, T)
    tk = _pick(Ka, cap)
    tn = _pick(N, cap)

    def body(a, g, o):
        @pl.when(pl.program_id(2) == 0)
        def _():
            o[...] = jnp.zeros(o.shape, F32)
        o[...] += lax.dot_general(a[...].astype(MXU_DTYPE), g[...].astype(MXU_DTYPE),
                                  (((0,), (0,)), ((), ())), preferred_element_type=F32)

    return pl.pallas_call(
        body, name=name, grid=(Ka // tk, N // tn, T // tt),
        in_specs=[pl.BlockSpec((tt, tk), lambda i, j, t: (t, i)), pl.BlockSpec((tt, tn), lambda i, j, t: (t, j))],
        out_specs=pl.BlockSpec((tk, tn), lambda i, j, t: (i, j)),
        out_shape=jax.ShapeDtypeStruct((Ka, N), F32),
        compiler_params=_params(("parallel", "parallel", "arbitrary")),
    )(A, G)


def _rowwise(name, fn, *, Bl, S, R, tiled=(), prev=(), nxt=(), batch=(), full=(),
             out_tiled=(), out_batch=(), out_acc=()):
    R = min(R, S)
    nS = S // R
    T = Bl * S
    hb = R // HALO
    n_in = len(tiled) + len(prev) + len(nxt) + len(batch) + len(full)

    in_specs, args = [], []
    for a, wd, cb in tiled:
        in_specs.append(pl.BlockSpec((R, wd), lambda b, i, cb=cb: (b * nS + i, cb)))
        args.append(a)
    for a, wd, cb in prev:
        in_specs.append(pl.BlockSpec((HALO, wd), lambda b, i, cb=cb: (jnp.maximum((b * nS + i) * hb - 1, 0), cb)))
        args.append(a)
    for a, wd, cb in nxt:
        in_specs.append(pl.BlockSpec((HALO, wd), lambda b, i, cb=cb: (jnp.minimum((b * nS + i + 1) * hb, T // HALO - 1), cb)))
        args.append(a)
    for a, wd, cb in batch:
        in_specs.append(pl.BlockSpec((1, 1, wd), lambda b, i, cb=cb: (b, 0, cb)))
        args.append(a)
    for a in full:
        in_specs.append(pl.BlockSpec(a.shape, lambda b, i, nd=a.ndim: (0,) * nd))
        args.append(a)

    out_specs, out_shape = [], []
    for C, dt in out_tiled:
        out_specs.append(pl.BlockSpec((R, C), lambda b, i: (b * nS + i, 0)))
        out_shape.append(jax.ShapeDtypeStruct((T, C), dt))
    for C in out_batch:
        out_specs.append(pl.BlockSpec((1, 1, C), lambda b, i: (b, 0, 0)))
        out_shape.append(jax.ShapeDtypeStruct((Bl, 1, C), F32))
    for shp in out_acc:
        out_specs.append(pl.BlockSpec(shp, lambda b, i, nd=len(shp): (0,) * nd))
        out_shape.append(jax.ShapeDtypeStruct(shp, F32))

    nt, npv, nnx, nbt = len(tiled), len(prev), len(nxt), len(batch)

    def body(*refs):
        b, i = pl.program_id(0), pl.program_id(1)
        ins, outs = refs[:n_in], refs[n_in:]
        vals = [r[...] for r in ins[:nt]]
        vals += [jnp.where(i > 0, r[...], jnp.zeros(r.shape, r.dtype)) for r in ins[nt:nt + npv]]
        vals += [jnp.where(i < nS - 1, r[...], jnp.zeros(r.shape, r.dtype)) for r in ins[nt + npv:nt + npv + nnx]]
        vals += [r[0] for r in ins[nt + npv + nnx:nt + npv + nnx + nbt]]
        vals += [r[...] for r in ins[nt + npv + nnx + nbt:]]
        res = fn(*vals)
        if not isinstance(res, (tuple, list)):
            res = (res,)
        k = 0
        for _ in out_tiled:
            outs[k][...] = res[k].astype(outs[k].dtype)
            k += 1
        for _ in out_batch:
            o = outs[k]

            @pl.when(i == 0)
            def _(o=o):
                o[...] = jnp.zeros(o.shape, F32)
            o[0] += res[k]
            k += 1
        for _ in out_acc:
            o = outs[k]

            @pl.when((i == 0) & (b == 0))
            def _(o=o):
                o[...] = jnp.zeros(o.shape, F32)
            o[...] += res[k]
            k += 1

    out = pl.pallas_call(
        body, name=name, grid=(Bl, nS), in_specs=in_specs, out_specs=out_specs, out_shape=out_shape,
        compiler_params=_params(("arbitrary", "arbitrary")),
    )(*args)
    return out


def _colwise(name, fn, *, Bl, S, R, W, strip, tiled=(), prev=(), nxt=(), full=(), out_tiled=(), n_acc=0):
    R = min(R, S)
    nS = S // R
    T = Bl * S
    hb = R // HALO
    nt, npv, nnx, nfl = len(tiled), len(prev), len(nxt), len(full)
    n_in = nt + npv + nnx + nfl
    in_specs = [pl.BlockSpec((R, W), lambda b, i, cb=cb: (b * nS + i, cb)) for _, cb in tiled]
    in_specs += [pl.BlockSpec((HALO, W), lambda b, i, cb=cb: (jnp.maximum((b * nS + i) * hb - 1, 0), cb)) for _, cb in prev]
    in_specs += [pl.BlockSpec((HALO, W), lambda b, i, cb=cb: (jnp.minimum((b * nS + i + 1) * hb, T // HALO - 1), cb))
                 for _, cb in nxt]
    in_specs += [pl.BlockSpec(a.shape, lambda b, i: (0, 0)) for a in full]
    out_specs = [pl.BlockSpec((R, m * W), lambda b, i: (b * nS + i, 0)) for m, _ in out_tiled]
    out_specs += [pl.BlockSpec((1, W), lambda b, i: (0, 0))] * n_acc
    out_shape = [jax.ShapeDtypeStruct((T, m * W), dt) for m, dt in out_tiled] + [jax.ShapeDtypeStruct((1, W), F32)] * n_acc

    def body(*refs):
        b, i = pl.program_id(0), pl.program_id(1)
        ins, outs = refs[:n_in], refs[n_in:]

        @pl.when((i == 0) & (b == 0))
        def _():
            for o in outs[len(out_tiled):]:
                o[...] = jnp.zeros(o.shape, F32)

        def col(j, carry):
            cs = pl.ds(pl.multiple_of(j * strip, strip), strip)
            vals = [r[:, cs] for r in ins[:nt]]
            vals += [jnp.where(i > 0, r[:, cs], jnp.zeros((HALO, strip), r.dtype)) for r in ins[nt:nt + npv]]
            vals += [jnp.where(i < nS - 1, r[:, cs], jnp.zeros((HALO, strip), r.dtype)) for r in ins[nt + npv:nt + npv + nnx]]
            vals += [r[:, cs] for r in ins[nt + npv + nnx:]]
            res = fn(*vals)
            for k, (m, _) in enumerate(out_tiled):
                for q in range(m):
                    outs[k][:, pl.ds(pl.multiple_of(q * W + j * strip, strip), strip)] = res[k][q].astype(outs[k].dtype)
            for k in range(len(out_tiled), len(outs)):
                outs[k][:, cs] += res[k]
            return carry

        lax.fori_loop(0, W // strip, col, 0)

    return pl.pallas_call(
        body, name=name, grid=(Bl, nS), in_specs=in_specs, out_specs=out_specs, out_shape=out_shape,
        compiler_params=_params(("arbitrary", "arbitrary")),
    )(*[a for a, _ in tiled], *[a for a, _ in prev], *[a for a, _ in nxt], *full)


def _shift_down(x, halo, k):
    rolled = pltpu.roll(x, k, 0)
    row = lax.broadcasted_iota(jnp.int32, (SUBLANES, x.shape[1]), 0)
    head = rolled[0:SUBLANES]
    for j in range(k):
        head = jnp.where(row == j, halo[HALO - k + j:HALO - k + j + 1, :], head)
    return jnp.concatenate([head, rolled[SUBLANES:]], axis=0)


def _shift_up(x, halo, k):
    n = x.shape[0]
    rolled = pltpu.roll(x, n - k, 0)
    row = lax.broadcasted_iota(jnp.int32, (SUBLANES, x.shape[1]), 0)
    tail = rolled[n - SUBLANES:]
    for j in range(k):
        tail = jnp.where(row == SUBLANES - k + j, halo[j:j + 1, :], tail)
    return jnp.concatenate([rolled[:n - SUBLANES], tail], axis=0)


def _dotm(a, b):
    return jnp.dot(a.astype(MXU_DTYPE), b.astype(MXU_DTYPE), preferred_element_type=F32)


def _split_bf16(x):
    hi = x.astype(BF16)
    return hi, (x - hi.astype(F32)).astype(BF16)


def _headsum_2pass(x, hm):
    hi, lo = _split_bf16(x)
    hb = hm.astype(BF16)
    return jnp.dot(hi, hb, preferred_element_type=F32) + jnp.dot(lo, hb, preferred_element_type=F32)


@jax.custom_vjp
def _headsum(x, hm):
    return _headsum_2pass(x, hm)


_headsum.defvjp(lambda x, hm: (_headsum_2pass(x, hm), hm),
                lambda hm, g: (_headsum_2pass(g, hm), jnp.zeros_like(hm)))


def _sigmoid(x):
    return 0.5 * jnp.tanh(0.5 * x) + 0.5


def _rms(x, g):
    return x * lax.rsqrt(jnp.mean(x * x, axis=-1, keepdims=True) + RMS_EPS) * g


def _norm_mod(x, g, sc, sh):
    return _rms(x, g) * (1.0 + sc) + sh


def _split_ps(ps):
    return (ps[:, 0:RW], ps[:, RW:2 * RW], ps[:, 2 * RW:3 * RW], ps[:, 3 * RW:3 * RW + LW + LA],
            ps[:, 3 * RW + LW + LA:SHIFT])


def _rwkv_prep(r, k, v, wa, gd, w0, w_up_p, a0, a_up_p, g_up, k_k, k_a, hm):
    w_raw = w0 + _dotm(jnp.tanh(wa), w_up_p)
    decay = jnp.exp(-DECAY_SCALE * _sigmoid(w_raw))
    a = _sigmoid(a0 + _dotm(wa, a_up_p))
    g = _dotm(_sigmoid(gd), g_up)
    kk = k * k_k
    kk = kk * lax.rsqrt(_headsum(kk * kk, hm) + L2_EPS)
    k2 = k * (1.0 + (a - 1.0) * k_a)
    return r, decay, k2, v, -kk, kk * a, g


def _rwkv_post(y, r, k2, v, g, ln_g, ln_b, r_k, hm):
    mean = _headsum(y, hm) * (1.0 / HD)
    yc = y - mean
    var = _headsum(yc * yc, hm) * (1.0 / HD)
    yn = yc * lax.rsqrt(var + GN_EPS) * ln_g + ln_b
    bonus = _headsum(r * k2 * r_k, hm) * v
    return (yn + bonus) * g


def _gelu(x):
    return 0.5 * x * (1.0 + jnp.tanh(GELU_C * (x + 0.044715 * (x * x * x))))


def _s5_post(yssm, u, d):
    return _gelu(yssm + d * u)


def _mix(ga, gb, ya, za, zb):
    return _sigmoid(ga) * ya + _sigmoid(gb) * (za * _sigmoid(zb))


def _conv_act(up_g, up_u, hg, hu, w_g, w_u, b_g, b_u):
    gate, upv = _conv3(up_g, hg, w_g, b_g)[0], _conv3(up_u, hu, w_u, b_u)[0]
    return gate, upv


def _conv3(x, h, w, b):
    x, h = x.astype(F32), h.astype(F32)
    s2, s1 = _shift_down(x, h, 2), _shift_down(x, h, 1)
    return b + w[0:1] * s2 + w[1:2] * s1 + w[2:3] * x, (s2, s1, x)


def _silu_gate(gate, upv):
    return gate * _sigmoid(gate) * upv


WKV_L = 64
_NT, _NN, _TN = ((1,), (1,)), ((1,), (0,)), ((0,), (0,))


def _dotw(x, y, dims):
    return lax.dot_general(x.astype(MXU_DTYPE), y.astype(MXU_DTYPE), (dims, ((), ())), preferred_element_type=F32)


def _dot3(x, y, dims):
    (xh, xl), (yh, yl) = _split_bf16(x), _split_bf16(y)
    d = lambda p, q: lax.dot_general(p, q, (dims, ((), ())), preferred_element_type=F32)
    return d(xh, yh) + d(xh, yl) + d(xl, yh)


@jax.custom_vjp
def _gram3(x, y):
    return _dot3(x, y, _NT)


_gram3.defvjp(lambda x, y: (_dot3(x, y, _NT), (x, y)),
              lambda res, g: (_dot3(g, res[1], _NN), _dot3(g, res[0], _TN)))


def _tri_solve_fwd(ns, xs):
    each = lambda f, *ls: tuple(f(*zs) for zs in zip(*ls))
    size = ns[0].shape[0]
    eye = (lax.broadcasted_iota(jnp.int32, (size, size), 0) == lax.broadcasted_iota(jnp.int32, (size, size), 1)).astype(F32)
    ts = each(lambda n: n + eye, ns)
    qs = ns
    for _ in range(WKV_L.bit_length() - 2):
        qs = each(lambda q: _dotw(q, q, _NN), qs)
        ts = each(lambda t, q: t + _dotw(t, q, _NN), ts, qs)
    us = each(lambda t, x: _dotw(t, x, _NN), ts, xs)
    return us, (ts, us)


def _tri_solve_bwd(res, dus):
    ts, us = res
    each = lambda f, *ls: tuple(f(*zs) for zs in zip(*ls))
    dxs = each(lambda t, du: _dotw(t, du, _TN), ts, dus)
    return each(lambda dx, u: _dotw(dx, u, _NT), dxs, us), dxs


@jax.custom_vjp
def _tri_solve(ns, xs):
    return _tri_solve_fwd(ns, xs)[0]


_tri_solve.defvjp(_tri_solve_fwd, _tri_solve_bwd)


def _wkv_chunk(s0, r, w, k, v, a, b):
    y, s1 = _wkv_chunks((s0,), (r,), (w,), (k,), (v,), (a,), (b,))
    return y[0], s1[0]


def _wkv_chunks(s0, r, w, k, v, a, b):
    each = lambda f, *ls: tuple(f(*xs) for xs in zip(*ls))
    L = r[0].shape[0]
    n2 = 2 * L
    lane_head = lax.broadcasted_iota(jnp.int32, (2, 1, 2 * HD), 2) // HD
    head_mask = (lane_head == lax.broadcasted_iota(jnp.int32, (2, 1, 2 * HD), 0)).astype(F32)
    ri = lax.broadcasted_iota(jnp.int32, (n2, n2), 0)
    ci = lax.broadcasted_iota(jnp.int32, (n2, n2), 1)
    same = (ri // L) == (ci // L)
    strict = same & ((ci % L) < (ri % L))
    incl = same & ((ci % L) <= (ri % L))
    si = lax.broadcasted_iota(jnp.int32, (2 * HD, 2 * HD), 0) // HD
    sj = lax.broadcasted_iota(jnp.int32, (2 * HD, 2 * HD), 1) // HD
    tri = (lax.broadcasted_iota(jnp.int32, (L, L), 0) >= lax.broadcasted_iota(jnp.int32, (L, L), 1)).astype(F32)

    stack = lambda z: (z[None] * head_mask).reshape(n2, 2 * HD)
    dup = lambda z: jnp.broadcast_to(z[None], (2, L, 2 * HD)).reshape(n2, 2 * HD)
    gram = _gram3
    nt, nn, tn = (lambda x, y, d=d: _dotw(x, y, d) for d in (_NT, _NN, _TN))
    add = lambda x, y: x + y

    lw = each(jnp.log, w)
    cum = each(lambda z: jnp.dot(tri, z, preferred_element_type=F32, precision=HIGHEST), lw)
    tot = each(lambda z: jnp.sum(z, axis=0, keepdims=True), lw)
    a2 = each(lambda av, cv, lv: stack(av * jnp.exp(cv - lv)), a, cum, lw)
    r2 = each(lambda rv, cv: stack(rv * jnp.exp(cv)), r, cum)
    v2 = each(stack, v)
    b2 = each(lambda bv, cv: dup(bv * jnp.exp(-cv)), b, cum)
    k2 = each(lambda kv, cv: dup(kv * jnp.exp(-cv)), k, cum)
    n_ab = each(lambda x, y: jnp.where(strict, gram(x, y), 0.0), a2, b2)
    n_ak = each(lambda x, y: jnp.where(strict, gram(x, y), 0.0), a2, k2)
    m_rb = each(lambda x, y: jnp.where(incl, gram(x, y), 0.0), r2, b2)
    m_rk = each(lambda x, y: jnp.where(incl, gram(x, y), 0.0), r2, k2)
    u = _tri_solve(n_ab, each(add, each(nt, a2, s0), each(nn, n_ak, v2)))
    y2 = each(lambda x, y, z: x + y + z, each(nt, r2, s0), each(nn, m_rb, u), each(nn, m_rk, v2))
    y = each(lambda z: jnp.sum(z.reshape(2, L, 2 * HD), axis=0), y2)
    b3 = each(lambda bv, tv, cv: dup(bv * jnp.exp(tv - cv)), b, tot, cum)
    k3 = each(lambda kv, tv, cv: dup(kv * jnp.exp(tv - cv)), k, tot, cum)
    upd = each(add, each(tn, u, b3), each(tn, v2, k3))
    s1 = each(lambda sv, tv, uv: sv * jnp.exp(tv) + jnp.where(si == sj, uv, 0.0), s0, tot, upd)
    return y, s1


NPAIR = NH // 2


def _wkv_nb(Bl):
    return 4 if Bl % 4 == 0 else 2 if Bl % 2 == 0 else 1


def _wkv_fwd(r, w, k, v, a, b, Bl, S):
    L = WKV_L
    nC = S // L
    nb = _wkv_nb(Bl)
    chains = [(bi, p, slice(p * 2 * HD, (p + 1) * 2 * HD)) for bi in range(nb) for p in range(NPAIR)]

    def body(r_ref, w_ref, k_ref, v_ref, a_ref, b_ref, y_ref, ck_ref, s_ref):
        @pl.when(pl.program_id(1) == 0)
        def _():
            s_ref[...] = jnp.zeros(s_ref.shape, F32)
        s0 = tuple(s_ref[bi, p] for bi, p, _ in chains)
        ops = [tuple(z[bi, :, cs] for bi, _, cs in chains) for z in (r_ref, w_ref, k_ref, v_ref, a_ref, b_ref)]
        y, s1 = _wkv_chunks(s0, *ops)
        for i, (bi, p, cs) in enumerate(chains):
            ck_ref[bi, 0, p] = s0[i]
            y_ref[bi, :, cs] = y[i]
            s_ref[bi, p] = s1[i]

    to3 = lambda z: z.reshape(Bl, S, RW)
    row_spec = pl.BlockSpec((nb, L, RW), lambda g, c: (g, c, 0))
    y, ck = pl.pallas_call(
        body, name="wkv_fwd", grid=(Bl // nb, nC), in_specs=[row_spec] * 6,
        out_specs=[row_spec, pl.BlockSpec((nb, 1, NPAIR, 2 * HD, 2 * HD), lambda g, c: (g, c, 0, 0, 0))],
        out_shape=[jax.ShapeDtypeStruct((Bl, S, RW), F32), jax.ShapeDtypeStruct((Bl, nC, NPAIR, 2 * HD, 2 * HD), F32)],
        scratch_shapes=[pltpu.VMEM((nb, NPAIR, 2 * HD, 2 * HD), F32)],
        compiler_params=_params(("arbitrary", "arbitrary")),
    )(*(to3(z) for z in (r, w, k, v, a, b)))
    return y.reshape(Bl * S, RW), ck


def _wkv_bwd(r, w, k, v, a, b, dy, ck, Bl, S):
    L = WKV_L
    nC = S // L
    nb = _wkv_nb(Bl)
    chains = [(bi, p, slice(p * 2 * HD, (p + 1) * 2 * HD)) for bi in range(nb) for p in range(NPAIR)]

    def body(r_ref, w_ref, k_ref, v_ref, a_ref, b_ref, dy_ref, ck_ref,
             dr_ref, dw_ref, dk_ref, dv_ref, da_ref, db_ref, ds_ref):
        @pl.when(pl.program_id(1) == 0)
        def _():
            ds_ref[...] = jnp.zeros(ds_ref.shape, F32)
        s0 = tuple(ck_ref[bi, 0, p] for bi, p, _ in chains)
        ops = [tuple(z[bi, :, cs] for bi, _, cs in chains) for z in (r_ref, w_ref, k_ref, v_ref, a_ref, b_ref)]
        cts = (tuple(dy_ref[bi, :, cs] for bi, _, cs in chains), tuple(ds_ref[bi, p] for bi, p, _ in chains))
        ds0, *grads = jax.vjp(_wkv_chunks, s0, *ops)[1](cts)
        for i, (bi, p, cs) in enumerate(chains):
            ds_ref[bi, p] = ds0[i]
            for o, g in zip((dr_ref, dw_ref, dk_ref, dv_ref, da_ref, db_ref), grads):
                o[bi, :, cs] = g[i]

    to3 = lambda z: z.reshape(Bl, S, RW)
    row_spec = pl.BlockSpec((nb, L, RW), lambda g, c: (g, nC - 1 - c, 0))
    rows = jax.ShapeDtypeStruct((Bl, S, RW), F32)
    outs = pl.pallas_call(
        body, name="wkv_bwd", grid=(Bl // nb, nC),
        in_specs=[row_spec] * 7 + [pl.BlockSpec((nb, 1, NPAIR, 2 * HD, 2 * HD), lambda g, c: (g, nC - 1 - c, 0, 0, 0))],
        out_specs=[row_spec] * 6, out_shape=[rows] * 6,
        scratch_shapes=[pltpu.VMEM((nb, NPAIR, 2 * HD, 2 * HD), F32)],
        compiler_params=_params(("arbitrary", "arbitrary")),
    )(*(to3(z) for z in (r, w, k, v, a, b, dy)), ck)
    return [o.reshape(Bl * S, RW) for o in outs]


NST = NG * SP


def _cmul(ar, ai, br, bi):
    return ar * br - ai * bi, ar * bi + ai * br


def _s5_tiles(are, aim, reverse):
    if reverse:
        aim = -aim
    row = lax.broadcasted_iota(jnp.int32, (SUBLANES, NST), 0)
    pw = [(are, aim)]
    for _ in range(SUBLANES - 1):
        pw.append(_cmul(pw[-1][0], pw[-1][1], are, aim))
    bc = lambda z: jnp.broadcast_to(z, (SUBLANES, NST))
    ms = []
    for kk in (1, 2, 4):
        cond = (row < SUBLANES - kk) if reverse else (row >= kk)
        ms.append((jnp.where(cond, bc(pw[kk - 1][0]), 0.0), jnp.where(cond, bc(pw[kk - 1][1]), 0.0)))
    pr = jnp.zeros((SUBLANES, NST), F32)
    pi = jnp.zeros((SUBLANES, NST), F32)
    for i in range(SUBLANES):
        n = SUBLANES - i if reverse else i + 1
        pr = jnp.where(row == i, bc(pw[n - 1][0]), pr)
        pi = jnp.where(row == i, bc(pw[n - 1][1]), pi)
    return ms, (pr, pi)


def _s5_block(re, im, ms, pc, cre, cim, sg, reverse):
    ln = slice(sg * 512, (sg + 1) * 512)
    for (mr, mi), kk in zip(ms, (1, 2, 4)):
        sh = SUBLANES - kk if reverse else kk
        sre, sim = pltpu.roll(re, sh, 0), pltpu.roll(im, sh, 0)
        tr, ti = _cmul(mr[:, ln], mi[:, ln], sre, sim)
        re, im = re + tr, im + ti
    tr, ti = _cmul(pc[0][:, ln], pc[1][:, ln], cre[:, ln], cim[:, ln])
    return re + tr, im + ti


def _s5_scan(X_ref, n_rows, ms, pc, cre, cim, reverse, visit=None, acc0=None):
    nblk = n_rows // SUBLANES

    def it(i, carry):
        cre, cim, acc = carry
        j = nblk - 1 - i if reverse else i
        rows = pl.ds(pl.multiple_of(j * SUBLANES, SUBLANES), SUBLANES)
        edge = 0 if reverse else SUBLANES - 1
        blocks, ncre, ncim = [], [], []
        for sg in range(NSG):
            lr = slice(sg * 1024, sg * 1024 + 512)
            li = slice(sg * 1024 + 512, (sg + 1) * 1024)
            re, im = _s5_block(X_ref[rows, lr], X_ref[rows, li], ms, pc, cre, cim, sg, reverse)
            X_ref[rows, lr] = re
            X_ref[rows, li] = im
            blocks.append((re, im))
            ncre.append(re[edge:edge + 1])
            ncim.append(im[edge:edge + 1])
        if visit is not None:
            acc = visit(j, blocks, acc)
        return jnp.concatenate(ncre, axis=1), jnp.concatenate(ncim, axis=1), acc

    return lax.fori_loop(0, nblk, it, (cre, cim, acc0 if acc0 is not None else 0))


def _s5_fwd(u, wb, wc, ab, d, Bl, S, R=256):
    R = min(R, S)
    nC = S // R

    def body(u_ref, wb_ref, wc_ref, ab_ref, d_ref, y_ref, st_ref, X_ref, o_ref, car_ref):
        @pl.when(pl.program_id(1) == 0)
        def _():
            car_ref[...] = jnp.zeros(car_ref.shape, F32)
        st_ref[0, 0] = car_ref[...]
        ms, pc = _s5_tiles(ab_ref[0:1], ab_ref[1:2], False)
        for sg in range(NSG):
            X_ref[:, sg * 1024:(sg + 1) * 1024] = _dotm(u_ref[:, sg * 128:(sg + 1) * 128], wb_ref[sg])
        cre, cim, _ = _s5_scan(X_ref, R, ms, pc, car_ref[0:1], car_ref[1:2], False)
        car_ref[0:1] = cre
        car_ref[1:2] = cim
        for sg in range(NSG):
            y_ref[:, sg * 128:(sg + 1) * 128] = _dotm(X_ref[:, sg * 1024:(sg + 1) * 1024], wc_ref[sg])
        o_ref[...] = _s5_post(y_ref[...], u_ref[...], d_ref[...]).astype(o_ref.dtype)

    rows = pl.BlockSpec((R, SW), lambda b, c: (b * nC + c, 0))
    return pl.pallas_call(
        body, name="s5_fwd", grid=(Bl, nC),
        in_specs=[rows, pl.BlockSpec(wb.shape, lambda b, c: (0, 0, 0)), pl.BlockSpec(wc.shape, lambda b, c: (0, 0, 0)),
                  pl.BlockSpec(ab.shape, lambda b, c: (0, 0)), pl.BlockSpec(d.shape, lambda b, c: (0, 0))],
        out_specs=[rows, pl.BlockSpec((1, 1, 2, NST), lambda b, c: (b, c, 0, 0)),
                   pl.BlockSpec((R, 2 * NST), lambda b, c: (b * nC + c, 0)), rows],
        out_shape=[jax.ShapeDtypeStruct((Bl * S, SW), F32), jax.ShapeDtypeStruct((Bl, nC, 2, NST), F32),
                   jax.ShapeDtypeStruct((Bl * S, 2 * NST), F32), jax.ShapeDtypeStruct((Bl * S, SW), MXU_DTYPE)],
        scratch_shapes=[pltpu.VMEM((2, NST), F32)],
        compiler_params=_params(("arbitrary", "arbitrary")),
    )(u, wb, wc, ab, d)


def _s5_bwd(u, y, do, d, wb, wc, ab, st, xs, Bl, S, R=256):
    R = min(R, S)
    nC = S // R

    def body(u_ref, y_ref, do_ref, d_ref, wb_ref, wc_ref, ab_ref, st_ref, X_ref,
             du_ref, dwb_ref, dwc_ref, dab_ref, dd_ref, G_ref, car_ref):
        first = (pl.program_id(0) == 0) & (pl.program_id(1) == 0)

        @pl.when(first)
        def _():
            for o in (dwb_ref, dwc_ref, dab_ref, dd_ref):
                o[...] = jnp.zeros(o.shape, F32)

        @pl.when(pl.program_id(1) == 0)
        def _():
            car_ref[...] = jnp.zeros(car_ref.shape, F32)

        are, aim = ab_ref[0:1], ab_ref[1:2]
        dy, du_direct, dd = jax.vjp(_s5_post, y_ref[...], u_ref[...], d_ref[...])[1](do_ref[...])
        dd_ref[...] += dd
        dyv = dy.astype(MXU_DTYPE)
        for sg in range(NSG):
            G_ref[:, sg * 1024:(sg + 1) * 1024] = lax.dot_general(
                dyv[:, sg * 128:(sg + 1) * 128], wc_ref[sg].astype(MXU_DTYPE), (((1,), (1,)), ((), ())),
                preferred_element_type=F32)
        rms_, rpc = _s5_tiles(are, aim, True)
        row = lax.broadcasted_iota(jnp.int32, (SUBLANES, 512), 0)

        def visit(j, blocks, acc):
            before = pl.multiple_of(jnp.maximum(j - 1, 0) * SUBLANES, SUBLANES)
            prow = X_ref[pl.ds(before, SUBLANES), :][SUBLANES - 1:SUBLANES]
            rows = pl.ds(pl.multiple_of(j * SUBLANES, SUBLANES), SUBLANES)
            are_acc, aim_acc = [], []
            for sg in range(NSG):
                lr = slice(sg * 1024, sg * 1024 + 512)
                li = slice(sg * 1024 + 512, (sg + 1) * 1024)
                ln = slice(sg * 512, (sg + 1) * 512)
                pre = jnp.where(j > 0, prow[:, lr], st_ref[0, 0, 0:1, ln])
                pim = jnp.where(j > 0, prow[:, li], st_ref[0, 0, 1:2, ln])
                xre = jnp.where(row == 0, pre, pltpu.roll(X_ref[rows, lr], 1, 0))
                xim = jnp.where(row == 0, pim, pltpu.roll(X_ref[rows, li], 1, 0))
                dre, dim = blocks[sg]
                are_acc.append(dre * xre + dim * xim)
                aim_acc.append(dim * xre - dre * xim)
            return acc[0] + jnp.concatenate(are_acc, axis=1), acc[1] + jnp.concatenate(aim_acc, axis=1)

        zero = jnp.zeros((SUBLANES, NST), F32)
        cre, cim, acc = _s5_scan(G_ref, R, rms_, rpc, car_ref[0:1], car_ref[1:2], True, visit, (zero, zero))
        car_ref[0:1] = cre
        car_ref[1:2] = cim
        dab_ref[0:1] += jnp.sum(acc[0], axis=0, keepdims=True)
        dab_ref[1:2] += jnp.sum(acc[1], axis=0, keepdims=True)
        uv = u_ref[...].astype(MXU_DTYPE)
        for sg in range(NSG):
            cs = slice(sg * 1024, (sg + 1) * 1024)
            us = slice(sg * 128, (sg + 1) * 128)
            gx = G_ref[:, cs].astype(MXU_DTYPE)
            dwb_ref[sg] += lax.dot_general(uv[:, us], gx, (((0,), (0,)), ((), ())), preferred_element_type=F32)
            dwc_ref[sg] += lax.dot_general(X_ref[:, cs].astype(MXU_DTYPE), dyv[:, us], (((0,), (0,)), ((), ())),
                                           preferred_element_type=F32)
            du_ssm = lax.dot_general(gx, wb_ref[sg].astype(MXU_DTYPE), (((1,), (1,)), ((), ())),
                                     preferred_element_type=F32)
            du_ref[:, us] = (du_ssm + du_direct[:, us]).astype(du_ref.dtype)

    rmap = lambda b, c: (b * nC + nC - 1 - c, 0)
    rows = pl.BlockSpec((R, SW), rmap)
    return pl.pallas_call(
        body, name="s5_bwd", grid=(Bl, nC),
        in_specs=[rows, rows, rows, pl.BlockSpec(d.shape, lambda b, c: (0, 0)),
                  pl.BlockSpec(wb.shape, lambda b, c: (0, 0, 0)), pl.BlockSpec(wc.shape, lambda b, c: (0, 0, 0)),
                  pl.BlockSpec(ab.shape, lambda b, c: (0, 0)),
                  pl.BlockSpec((1, 1, 2, NST), lambda b, c: (b, nC - 1 - c, 0, 0)),
                  pl.BlockSpec((R, 2 * NST), rmap)],
        out_specs=[rows, pl.BlockSpec(wb.shape, lambda b, c: (0, 0, 0)),
                   pl.BlockSpec(wc.shape, lambda b, c: (0, 0, 0)), pl.BlockSpec((2, NST), lambda b, c: (0, 0)),
                   pl.BlockSpec(d.shape, lambda b, c: (0, 0))],
        out_shape=[jax.ShapeDtypeStruct((Bl * S, SW), MXU_DTYPE), jax.ShapeDtypeStruct(wb.shape, F32),
                   jax.ShapeDtypeStruct(wc.shape, F32), jax.ShapeDtypeStruct((2, NST), F32),
                   jax.ShapeDtypeStruct(d.shape, F32)],
        scratch_shapes=[pltpu.VMEM((R, 2 * NST), F32), pltpu.VMEM((2, NST), F32)],
        compiler_params=_params(("arbitrary", "arbitrary")),
    )(u, y, do, d, wb, wc, ab, st, xs)


def _s5_disc_math(a_re, a_im, log_dt, b_re, b_im, expand):
    dt = jnp.exp(log_dt)
    z_re, z_im = a_re * dt, a_im * dt
    mag = jnp.exp(z_re)
    ab_re, ab_im = mag * jnp.cos(z_im), mag * jnp.sin(z_im)
    den = a_re * a_re + a_im * a_im
    q_re = ((ab_re - 1.0) * a_re + ab_im * a_im) / den
    q_im = (ab_im * a_re - (ab_re - 1.0) * a_im) / den
    qe_re = jnp.dot(q_re, expand, preferred_element_type=F32, precision=HIGHEST)
    qe_im = jnp.dot(q_im, expand, preferred_element_type=F32, precision=HIGHEST)
    return ab_re, ab_im, qe_re * b_re - qe_im * b_im, qe_re * b_im + qe_im * b_re


def _whole(shape):
    return pl.BlockSpec(shape, lambda nd=len(shape): (0,) * nd)


def _s5_disc(a_re, a_im, log_dt, b_re, b_im, expand):
    def body(a, b, c, d, e, f, o0, o1, o2, o3):
        res = _s5_disc_math(a[...], b[...], c[...], d[...], e[...], f[...])
        for o, v in zip((o0, o1, o2, o3), res):
            o[...] = v
    ins = (a_re, a_im, log_dt, b_re, b_im, expand)
    outs = [jax.ShapeDtypeStruct(a_re.shape, F32)] * 2 + [jax.ShapeDtypeStruct(b_re.shape, F32)] * 2
    return pl.pallas_call(body, name="s5_disc", in_specs=[_whole(x.shape) for x in ins],
                          out_specs=[_whole(o.shape) for o in outs], out_shape=outs)(*ins)


def _s5_disc_bwd(a_re, a_im, log_dt, b_re, b_im, expand, cts):
    def body(a, b, c, d, e, f, g0, g1, g2, g3, o0, o1, o2, o3, o4):
        fn = lambda *p: _s5_disc_math(*p, f[...])
        _, vjp = jax.vjp(fn, a[...], b[...], c[...], d[...], e[...])
        for o, v in zip((o0, o1, o2, o3, o4), vjp((g0[...], g1[...], g2[...], g3[...]))):
            o[...] = v
    ins = (a_re, a_im, log_dt, b_re, b_im, expand) + tuple(cts)
    outs = [jax.ShapeDtypeStruct(x.shape, F32) for x in (a_re, a_im, log_dt, b_re, b_im)]
    return pl.pallas_call(body, name="s5_disc_bwd", in_specs=[_whole(x.shape) for x in ins],
                          out_specs=[_whole(o.shape) for o in outs], out_shape=outs)(*ins)


def _ada_fwd(c_all, w_shard, b_shard):
    def body(c_ref, w_ref, b_ref, o_ref):
        cv = c_ref[...]
        o_ref[...] = _dotm(cv * _sigmoid(cv), w_ref[...]) + b_ref[...]
    n = w_shard.shape[1]
    return pl.pallas_call(
        body, name="ada_fwd", in_specs=[_whole(c_all.shape), _whole(w_shard.shape), _whole(b_shard.shape)],
        out_specs=_whole((c_all.shape[0], n)), out_shape=jax.ShapeDtypeStruct((c_all.shape[0], n), F32),
        compiler_params=_params(),
    )(c_all, w_shard, b_shard)


def _ada_bwd(c_all, dmod_cols, dmod_all):
    def body(c_ref, dc_ref, da_ref, gw_ref, gb_ref):
        cv = c_ref[...]
        gw_ref[...] = lax.dot_general((cv * _sigmoid(cv)).astype(MXU_DTYPE), dc_ref[...].astype(MXU_DTYPE),
                                      (((0,), (0,)), ((), ())), preferred_element_type=F32)
        gb_ref[...] = jnp.sum(da_ref[...], axis=0, keepdims=True)
    n = dmod_cols.shape[1]
    return pl.pallas_call(
        body, name="ada_bwd", in_specs=[_whole(c_all.shape), _whole(dmod_cols.shape), _whole(dmod_all.shape)],
        out_specs=[_whole((D, n)), _whole((1, dmod_all.shape[1]))],
        out_shape=[jax.ShapeDtypeStruct((D, n), F32), jax.ShapeDtypeStruct((1, dmod_all.shape[1]), F32)],
        compiler_params=_params(),
    )(c_all, dmod_cols, dmod_all)


def _rows_block(n_rows, cap=512):
    if n_rows <= cap:
        return n_rows
    for t in range(cap - cap % SUBLANES, 0, -SUBLANES):
        if n_rows % t == 0:
            return t
    return n_rows


def _adamw(w, g, m, v, name):
    rows, cols = w.shape
    tr = _rows_block(rows, max(SUBLANES, (1 << 19) // max(cols, 1) // SUBLANES * SUBLANES))

    def body(w_ref, g_ref, m_ref, v_ref, d_ref, nm_ref, nv_ref):
        gv = g_ref[...]
        nm = B1 * m_ref[...] + (1.0 - B1) * gv
        nv = B2 * v_ref[...] + (1.0 - B2) * (gv * gv)
        m_hat = nm / (1.0 - B1 ** STEP)
        v_hat = nv / (1.0 - B2 ** STEP)
        d_ref[...] = -LR * (m_hat / (jnp.sqrt(v_hat) + ADAM_EPS) + WD * w_ref[...])
        nm_ref[...] = nm
        nv_ref[...] = nv

    spec = pl.BlockSpec((tr, cols), lambda i: (i, 0))
    sd = jax.ShapeDtypeStruct((rows, cols), F32)
    return pl.pallas_call(body, name=name, grid=(rows // tr,), in_specs=[spec] * 4, out_specs=[spec] * 3,
                          out_shape=[sd] * 3, compiler_params=_params(("parallel",)))(w, g, m, v)


def _sum_slots(x, out_dtype, name):
    xs = x if isinstance(x, (list, tuple)) else [x]
    _, rows, cols = xs[0].shape
    tr = _rows_block(rows)

    def body(*refs):
        acc = None
        for x_ref in refs[:-1]:
            for j in range(x_ref.shape[0]):
                term = x_ref[j].astype(F32)
                acc = term if acc is None else acc + term
        refs[-1][...] = acc.astype(refs[-1].dtype)

    return pl.pallas_call(
        body, name=name, grid=(rows // tr,),
        in_specs=[pl.BlockSpec((z.shape[0], tr, cols), lambda i: (0, i, 0)) for z in xs],
        out_specs=pl.BlockSpec((tr, cols), lambda i: (i, 0)), out_shape=jax.ShapeDtypeStruct((rows, cols), out_dtype),
        compiler_params=_params(("parallel",)))(*xs)


PACK_COLS = 1024


def _pack_rows(parts, dtype, row_mult):
    flat = jnp.concatenate([p.reshape(-1).astype(dtype) for p in parts])
    per = PACK_COLS * row_mult
    n = -(-flat.shape[0] // per) * per
    flat = jnp.pad(flat, (0, n - flat.shape[0]))
    return flat.reshape(n // PACK_COLS, PACK_COLS)


def _unpack(flat, shapes):
    out, off = [], 0
    for s in shapes:
        n = math.prod(s)
        out.append(flat[off:off + n].reshape(s))
        off += n
    return out


BIG = (("w_in", (D, SHIFT + SW + 2 * D), 1), ("w_out_rwkv", (RW, D), 1), ("w_glu", (SW, 2 * D), 1),
       ("w_out", (D, D), 0), ("w_ffn_up", (D, 2 * DFF), 1), ("w_ffn_down", (DFF, D), 0))
BIG_SMALL = (("rwkv_w_up", (LW, RW), 1), ("rwkv_a_up", (LA, RW), 1), ("rwkv_g_up", (LG, RW), 1),
             ("ffn_conv_w", (3, 2 * DFF), 1))
BIG_LATE = BIG[4:]
BIG_MID = BIG[1:4]


def _shard_shape(shape, axis):
    return (shape[0] // 4, shape[1]) if axis == 0 else (shape[0], shape[1] // 4)


def _to_shards(g, axis):
    r, C = g.shape
    return g.reshape(4, r // 4, C) if axis == 0 else g.reshape(r, 4, C // 4).transpose(1, 0, 2)


def _from_shards(x, axis):
    _, r, C = x.shape
    return x.reshape(4 * r, C) if axis == 0 else x.transpose(1, 0, 2).reshape(r, 4 * C)


def kernel(x, c, w_ada, b_ada, norm1_g, w_in, mu_shift, rwkv_w0, rwkv_w_up, rwkv_a0, rwkv_a_up, rwkv_g_up, rwkv_k_k, rwkv_k_a, rwkv_r_k, rwkv_ln_g, rwkv_ln_b, w_out_rwkv, s5_a_re, s5_a_im, s5_log_dt, s5_b_re, s5_b_im, s5_c_re, s5_c_im, s5_d, w_glu, w_out, norm2_g, w_ffn_up, ffn_conv_w, ffn_conv_b, w_ffn_down, norm_f_g, loss_target, m_w_ada, m_b_ada, m_norm1_g, m_w_in, m_mu_shift, m_rwkv_w0, m_rwkv_w_up, m_rwkv_a0, m_rwkv_a_up, m_rwkv_g_up, m_rwkv_k_k, m_rwkv_k_a, m_rwkv_r_k, m_rwkv_ln_g, m_rwkv_ln_b, m_w_out_rwkv, m_s5_a_re, m_s5_a_im, m_s5_log_dt, m_s5_b_re, m_s5_b_im, m_s5_c_re, m_s5_c_im, m_s5_d, m_w_glu, m_w_out, m_norm2_g, m_w_ffn_up, m_ffn_conv_w, m_ffn_conv_b, m_w_ffn_down, m_norm_f_g, v_w_ada, v_b_ada, v_norm1_g, v_w_in, v_mu_shift, v_rwkv_w0, v_rwkv_w_up, v_rwkv_a0, v_rwkv_a_up, v_rwkv_g_up, v_rwkv_k_k, v_rwkv_k_a, v_rwkv_r_k, v_rwkv_ln_g, v_rwkv_ln_b, v_w_out_rwkv, v_s5_a_re, v_s5_a_im, v_s5_log_dt, v_s5_b_re, v_s5_b_im, v_s5_c_re, v_s5_c_im, v_s5_d, v_w_glu, v_w_out, v_norm2_g, v_w_ffn_up, v_ffn_conv_w, v_ffn_conv_b, v_w_ffn_down, v_norm_f_g):
    names = ["w_ada", "b_ada", "norm1_g", "w_in", "mu_shift", "rwkv_w0", "rwkv_w_up", "rwkv_a0", "rwkv_a_up",
             "rwkv_g_up", "rwkv_k_k", "rwkv_k_a", "rwkv_r_k", "rwkv_ln_g", "rwkv_ln_b", "w_out_rwkv", "s5_a_re",
             "s5_a_im", "s5_log_dt", "s5_b_re", "s5_b_im", "s5_c_re", "s5_c_im", "s5_d", "w_glu", "w_out", "norm2_g",
             "w_ffn_up", "ffn_conv_w", "ffn_conv_b", "w_ffn_down", "norm_f_g"]
    env = dict(locals())
    W = {n: env[n] for n in names}
    M = {n: env["m_" + n] for n in names}
    V = {n: env["v_" + n] for n in names}

    Bl, S, _ = x.shape
    T = Bl * S
    ix, iy, ic = lax.axis_index("x"), lax.axis_index("y"), lax.axis_index("c")
    chip = 2 * ix + iy
    dev = 2 * chip + ic
    rw = functools.partial(_rowwise, Bl=Bl, S=S)

    got_chip, got_dev = _gather_two_level([W[n][0] for n, _, _ in BIG_SMALL[:3]], [W["ffn_conv_w"][0], c], "gather_w")
    full = {n: _from_shards(g, axis) for (n, _, axis), g in zip(BIG_SMALL[:3], got_chip)}
    full["ffn_conv_w"] = _from_shards(got_dev[0][:, 0], 1)
    c_all = got_dev[1].reshape(8 * Bl, D)
    zeros_l = jnp.zeros((LW, RW), F32)
    w_up_p = jnp.concatenate([full["rwkv_w_up"], zeros_l], axis=0)
    a_up_p = jnp.concatenate([zeros_l, full["rwkv_a_up"]], axis=0)
    g_up = full["rwkv_g_up"]
    conv_w = full["ffn_conv_w"]
    conv_wg, conv_wu = conv_w[:, :DFF], conv_w[:, DFF:]
    conv_bg, conv_bu = ffn_conv_b[:, :DFF], ffn_conv_b[:, DFF:]
    hm = jnp.kron(jnp.eye(NH, dtype=F32), jnp.ones((HD, HD), F32))

    ncol = 6 * D // 4
    b_ada_cols = lax.dynamic_slice_in_dim(b_ada, chip * ncol, ncol, 1)
    mod_part = _ada_fwd(c_all, w_ada[0], b_ada_cols)
    mod4 = _gather_two_level([], [mod_part], "gather_mod")[1][0][:, 0]
    mod4, shards = lax.optimization_barrier((mod4, [W[n][0].astype(MXU_DTYPE) for n, _, _ in BIG]))

    def push_shards(tag, arrs):
        moves = [(i, i, lambda ref, me, peer: ref, lambda ref, me, k: ref.at[_chip_of(me)]) for i in range(len(arrs))]
        lands = [jax.ShapeDtypeStruct((4,) + z.shape, z.dtype) for z in arrs]
        return _send_start("gather_%s_start" % tag, CHIP_FLIPS, arrs, lands, moves), moves

    def pushed_shards(tag, started, moves, after, group):
        owns, gots = _send_wait("gather_%s_wait" % tag, CHIP_FLIPS, started, moves, after)
        for (n, _, axis), own, got in zip(group, owns, gots):
            full[n] = _from_shards(lax.dynamic_update_slice(got, own[None], (chip, 0, 0)), axis)

    first_start, first_moves = push_shards("in", shards[:1])
    norm1_g = norm1_g + first_start["token"]
    mod =lax.dynamic_slice_in_dim(mod4, dev * Bl, Bl, 1).transpose(1, 0, 2).reshape(Bl, 1, 6 * D)
    SH1, SC1, GT1, SH2, SC2, GT2 = range(6)

    x2d = x.reshape(T, D)
    tgt = loss_target.reshape(T, D)

    (h1,) = rw("norm1", lambda xv, sc, sh, g: _norm_mod(xv, g, sc, sh), R=512, tiled=[(x2d, D, 0)],
               batch=[(mod, D, SC1), (mod, D, SH1)], full=[norm1_g], out_tiled=[(D, MXU_DTYPE)])
    pushed_shards("in", first_start, first_moves, h1, BIG[:1])
    full["w_in"], rest = lax.optimization_barrier((full["w_in"], shards[1:]))
    late_start, late_moves = push_shards("rest", rest)
    mu_shift = mu_shift + late_start["token"]
    w_p, w_u, w_g = full["w_in"][:, :SHIFT], full["w_in"][:, SHIFT:SHIFT + SW], full["w_in"][:, SHIFT + SW:]
    p = _mm([h1], [w_p], F32, "proj_p")
    u = _mm([h1], [w_u], F32, "proj_u")
    gates = _mm([h1], [w_g], MXU_DTYPE, "proj_g")

    prep_params = [rwkv_w0, w_up_p, rwkv_a0, a_up_p, g_up, rwkv_k_k, rwkv_k_a, hm]

    def prep_fwd(pv, ph, mu, *pp):
        ps = pv + (_shift_down(pv, ph, 1) - pv) * mu
        return _rwkv_prep(*_split_ps(ps), *pp)

    r_, w_, k_, v_, a_, b_, g_ = rw("rwkv_prep", prep_fwd, R=256, tiled=[(p, SHIFT, 0)], prev=[(p, SHIFT, 0)],
                                    full=[mu_shift] + prep_params, out_tiled=[(RW, F32)] * 7)
    y_wkv, ck = _wkv_fwd(r_, w_, k_, v_, a_, b_, Bl, S)
    r_k_row = rwkv_r_k.reshape(1, RW)
    post_params = [rwkv_ln_g, rwkv_ln_b, r_k_row, hm]
    (o_rwkv,) = rw("rwkv_post", _rwkv_post, R=256,
                   tiled=[(y_wkv, RW, 0), (r_, RW, 0), (k_, RW, 0), (v_, RW, 0), (g_, RW, 0)],
                   full=post_params, out_tiled=[(RW, MXU_DTYPE)])
    pushed_shards("rest", late_start, late_moves, o_rwkv, BIG[1:])
    y_a = _mm([o_rwkv], [full["w_out_rwkv"]], MXU_DTYPE, "out_rwkv")

    expand = jnp.kron(jnp.eye(SP, dtype=F32), jnp.ones((1, SGC), F32))
    s5_in = (s5_a_re[0], s5_a_im[0], s5_log_dt[0].reshape(NG, 1), s5_b_re[0].reshape(NG, SP * SGC),
             s5_b_im[0].reshape(NG, SP * SGC), expand)
    ab_re, ab_im, bb_re, bb_im = _s5_disc(*s5_in)
    eye8 = jnp.eye(8, dtype=F32)

    def blockdiag_in(bb):
        t = bb.reshape(NSG, 8, SP, SGC)
        return jnp.einsum("ab,sapc->sacbp", eye8, t).reshape(NSG, 128, 512)

    def blockdiag_out(cc):
        t = cc.reshape(NSG, 8, SGC, SP)
        return jnp.einsum("ab,sacp->sapbc", eye8, t).reshape(NSG, 512, 128)

    wb = jnp.concatenate([blockdiag_in(bb_re), blockdiag_in(bb_im)], axis=2).astype(MXU_DTYPE)
    wc = jnp.concatenate([blockdiag_out(s5_c_re[0]), -blockdiag_out(s5_c_im[0])], axis=1).astype(MXU_DTYPE)
    ab = jnp.stack([ab_re.reshape(NST), ab_im.reshape(NST)])
    y_ssm, s5_st, s5_x, s5o = _s5_fwd(u, wb, wc, ab, s5_d, Bl, S)
    z = _mm([s5o], [full["w_glu"]], MXU_DTYPE, "glu")
    mix_tiled = [(gates, D, 0), (gates, D, 1), (y_a, D, 0), (z, D, 0), (z, D, 1)]
    (mixed_in,) = rw("mix", lambda *a: _mix(*(v.astype(F32) for v in a)), R=256, tiled=mix_tiled,
                     out_tiled=[(D, MXU_DTYPE)])
    mixed = _mm([mixed_in], [full["w_out"]], F32, "out_proj")

    def norm2_fwd(xv, mx, gt, sc, sh, g):
        x1 = xv + gt * mx
        return x1, _norm_mod(x1, g, sc, sh)

    x1, h2 = rw("norm2", norm2_fwd, R=512, tiled=[(x2d, D, 0), (mixed, D, 0)],
                batch=[(mod, D, GT1), (mod, D, SC2), (mod, D, SH2)], full=[norm2_g],
                out_tiled=[(D, F32), (D, MXU_DTYPE)])
    up =_mm([h2], [full["w_ffn_up"]], MXU_DTYPE, "ffn_up")
    conv_tiled = [(up, 0), (up, 1)]
    conv_full = [conv_wg, conv_wu, conv_bg, conv_bu]
    cw = functools.partial(_colwise, Bl=Bl, S=S, R=128, W=DFF, strip=LANES)

    def act_fwd(*a):
        return ((_silu_gate(*_conv_act(*a)),),)

    (act,) = cw("ffn_act", act_fwd, tiled=conv_tiled, prev=conv_tiled, full=conv_full, out_tiled=[(1, MXU_DTYPE)])
    ffn = _mm([act], [full["w_ffn_down"]], F32, "ffn_down")

    def head(x1v, fv, tv, gt, g):
        x2 = x1v + gt * fv
        y, vjp = jax.vjp(_rms, x2, g)
        e = y - tv
        dx2, dg = vjp(e * (1.0 / D))
        loss = jnp.sum(e * e, keepdims=True) * jnp.ones((1, LANES), F32)
        return dx2, dx2 * gt, jnp.sum(dx2 * fv, axis=0, keepdims=True), dg.reshape(1, D), loss

    dx2, d_ffn, d_gt2, g_norm_f, loss_acc = rw(
        "head", head, R=512, tiled=[(x1, D, 0), (ffn, D, 0), (tgt, D, 0)], batch=[(mod, D, GT2)],
        full=[norm_f_g.reshape(1, D)], out_tiled=[(D, F32), (D, MXU_DTYPE)], out_batch=[D],
        out_acc=[(1, D), (1, LANES)])
    loss = lax.psum(0.5 / D * loss_acc[0, 0], ("x", "y", "c"))

    d_act = _mm([d_ffn], [full["w_ffn_down"]], F32, "d_act", bt=True)
    g_w_ffn_down = _mm_tn(act, d_ffn, "g_ffn_down")

    def act_bwd(ug, uu, dact, hg, hu, wg, wu, bg, bu):
        (gate, taps_g), (upv, taps_u) = _conv3(ug, hg, wg, bg), _conv3(uu, hu, wu, bu)
        _, vjp_s = jax.vjp(_silu_gate, gate, upv)
        d_gate, d_upv = vjp_s(dact)
        def taps(dh, shifted):
            return [jnp.sum(dh * s, axis=0, keepdims=True) for s in shifted] + [jnp.sum(dh, axis=0, keepdims=True)]
        return ((d_gate,), (d_upv,), *taps(d_gate, taps_g), *taps(d_upv, taps_u))

    dh_g, dh_u, *tapg = cw("ffn_act_bwd", act_bwd, tiled=conv_tiled + [(d_act, 0)], prev=conv_tiled, full=conv_full,
                           out_tiled=[(1, MXU_DTYPE), (1, MXU_DTYPE)], n_acc=8)
    g_cw_g, g_cb_g = jnp.concatenate(tapg[0:3], axis=0), tapg[3]
    g_cw_u, g_cb_u = jnp.concatenate(tapg[4:7], axis=0), tapg[7]

    def conv_t(dg, du_, ng, nu, wg, wu):
        dg, du_, ng, nu = (z.astype(F32) for z in (dg, du_, ng, nu))

        def ct(d, n, w):
            return w[2:3] * d + w[1:2] * _shift_up(d, n, 1) + w[0:1] * _shift_up(d, n, 2)
        return ((ct(dg, ng, wg), ct(du_, nu, wu)),)

    (d_up,) = cw("conv_bwd", conv_t, tiled=[(dh_g, 0), (dh_u, 0)], nxt=[(dh_g, 0), (dh_u, 0)],
                 full=[conv_wg, conv_wu], out_tiled=[(2, MXU_DTYPE)])
    d_h2 = _mm([d_up], [full["w_ffn_up"]], F32, "d_h2", bt=True)
    g_w_ffn_up = _mm_tn(h2, d_up, "g_ffn_up")

    sds = jax.ShapeDtypeStruct
    reduce_src = lambda r: (lambda ref, me, peer: ref.at[_chip_of(peer), _half(r, peer[2])])

    def reduced_halves(tag, started, moves, after):
        gsh_own, got = _send_wait("rs_%s_wait" % tag, ALL_FLIPS, started, moves, after)
        halves = []
        for i, (g, gt) in enumerate(zip(gsh_own, got)):
            h = g.shape[1] // 2
            own = lax.dynamic_slice(g, (chip, ic * h, 0), (1, h, g.shape[2]))
            halves.append(_sum_slots([own, gt], F32, "rs_%s_sum%d" % (tag, i)))
        return halves

    def share_start(tag, halves):
        moves = [(i, i, lambda ref, me, peer: ref, lambda ref, me, k, r=2 * g.shape[0]: ref.at[_half(r, me[2])])
                 for i, g in enumerate(halves)]
        lands = [sds((2 * g.shape[0], g.shape[1]), F32) for g in halves]
        return _send_start("share_%s_start" % tag, PAIR_FLIPS, halves, lands, moves), moves

    def share_finish(tag, started, moves, after, group, grads):
        mine_h, got = _send_wait("share_%s_wait" % tag, PAIR_FLIPS, started, moves, after)
        for (n, _, _), mh, whole in zip(group, mine_h, got):
            grads[n] = lax.dynamic_update_slice(whole, mh, (ic * mh.shape[0], 0))[None]

    def reduce_start(tag, group, mats):
        gsh = [_to_shards(g, ax).astype(MXU_DTYPE) for g, (_, _, ax) in zip(mats, group)]
        moves = [(i, i, reduce_src(g.shape[1]), lambda ref, me, k: ref.at[k]) for i, g in enumerate(gsh)]
        lands = [sds((len(ALL_FLIPS), g.shape[1] // 2, g.shape[2]), MXU_DTYPE) for g in gsh]
        return _send_start("rs_%s_start" % tag, ALL_FLIPS, gsh, lands, moves), moves

    rsl, rsl_moves = reduce_start("ffn", BIG_LATE, (g_w_ffn_up, g_w_ffn_down))
    norm2_g = norm2_g + rsl["token"]

    def norm2_bwd(x1v, dh2, dx2v, mx, gt, sc, sh, g):
        _, vjp = jax.vjp(_norm_mod, x1v, g, sc, sh)
        dxn, dg, dsc, dsh = vjp(dh2)
        dx1 = dx2v + dxn
        return dx1, dx1 * gt, jnp.sum(dx1 * mx, axis=0, keepdims=True), dsc, dsh, dg

    dx1, d_mixed, d_gt1, d_sc2, d_sh2, g_norm2 = rw(
        "norm2_bwd", norm2_bwd, R=512, tiled=[(x1, D, 0), (d_h2, D, 0), (dx2, D, 0), (mixed, D, 0)],
        batch=[(mod, D, GT1), (mod, D, SC2), (mod, D, SH2)], full=[norm2_g],
        out_tiled=[(D, F32), (D, MXU_DTYPE)], out_batch=[D, D, D], out_acc=[(1, D)])

    d_mixed_in = _mm([d_mixed], [full["w_out"]], MXU_DTYPE, "d_mixed_in", bt=True)
    g_w_out = _mm_tn(mixed_in, d_mixed, "g_w_out")

    def mix_bwd(*a):
        ga, gb, ya, za, zb, dm = (v.astype(F32) for v in a)
        _, vjp = jax.vjp(_mix, ga, gb, ya, za, zb)
        dga, dgb, dya, dza, dzb = vjp(dm)
        return jnp.concatenate([dga, dgb], axis=1), dya, jnp.concatenate([dza, dzb], axis=1)

    d_gates, d_ya, d_z = rw("mix_bwd", mix_bwd, R=256, tiled=mix_tiled + [(d_mixed_in, D, 0)],
                            out_tiled=[(2 * D, MXU_DTYPE), (D, MXU_DTYPE), (2 * D, MXU_DTYPE)])
    d_o_rwkv = _mm([d_ya], [full["w_out_rwkv"]], F32, "d_o_rwkv", bt=True)
    g_w_out_rwkv = _mm_tn(o_rwkv, d_ya, "g_out_rwkv")
    d_s5o = _mm([d_z], [full["w_glu"]], F32, "d_s5o", bt=True)
    g_w_glu = _mm_tn(s5o, d_z, "g_glu")
    rsm, rsm_moves = reduce_start("mid", BIG_MID, (g_w_out_rwkv, g_w_glu, g_w_out))
    s5_d = s5_d + rsm["token"]

    d_u, d_wb, d_wc, d_ab, g_s5_d = _s5_bwd(u, y_ssm, d_s5o, s5_d, wb, wc, ab, s5_st, s5_x, Bl, S)

    def diag_in(dw):
        t = dw.reshape(NSG, 8, SGC, 8, SP)
        return jnp.einsum("ab,sacbp->sapc", eye8, t).reshape(NG, SP * SGC)

    def diag_out(dw):
        t = dw.reshape(NSG, 8, SP, 8, SGC)
        return jnp.einsum("ab,sapbc->sacp", eye8, t).reshape(NG, SGC, SP)

    g_s5_c_re = diag_out(d_wc[:, :512])
    g_s5_c_im = -diag_out(d_wc[:, 512:])
    disc_cts = (d_ab[0].reshape(NG, SP), d_ab[1].reshape(NG, SP), diag_in(d_wb[:, :, :512]), diag_in(d_wb[:, :, 512:]))
    g_a_re, g_a_im, g_log_dt, g_b_re, g_b_im = _s5_disc_bwd(*s5_in, disc_cts)

    def post_bwd(yv, rv, kv, vv, gv, do, *pp):
        _, vjp = jax.vjp(lambda *a: _rwkv_post(*a, pp[3]), yv, rv, kv, vv, gv, *pp[:3])
        return vjp(do)

    dy_wkv, dr_b, dk_b, dv_b, dg_, g_ln_g, g_ln_b, g_r_k = rw(
        "rwkv_post_bwd", post_bwd, R=256,
        tiled=[(y_wkv, RW, 0), (r_, RW, 0), (k_, RW, 0), (v_, RW, 0), (g_, RW, 0), (d_o_rwkv, RW, 0)],
        full=post_params, out_tiled=[(RW, F32)] * 5, out_acc=[(1, RW)] * 3)
    dr3, dw3, dk3, dv3, da3, db3 = _wkv_bwd(r_, w_, k_, v_, a_, b_, dy_wkv, ck, Bl, S)

    shl, shl_moves = share_start("ffn", reduced_halves("ffn", rsl, rsl_moves, dr3))
    shm, shm_moves = share_start("mid", reduced_halves("mid", rsm, rsm_moves, dr3))
    mu_shift = mu_shift + (shl["token"] + shm["token"])

    def prep_bwd(pv, dr1, dr2, dwv, dk1, dk2, dv1, dv2, dav, dbv, dgv, ph, mu, *pp):
        prev = _shift_down(pv, ph, 1)
        ps = pv + (prev - pv) * mu
        _, vjp = jax.vjp(lambda *q: _rwkv_prep(*q, pp[7]), *_split_ps(ps), *pp[:7])
        grads = vjp((dr1 + dr2, dwv, dk1 + dk2, dv1 + dv2, dav, dbv, dgv))
        dps = jnp.concatenate(grads[:5], axis=1)
        return (dps,) + tuple(grads[5:]) + (jnp.sum(dps * (prev - pv), axis=0, keepdims=True),)

    prep_outs = rw(
        "rwkv_prep_bwd", prep_bwd, R=256,
        tiled=[(p, SHIFT, 0), (dr3, RW, 0), (dr_b, RW, 0), (dw3, RW, 0), (dk3, RW, 0), (dk_b, RW, 0),
               (dv3, RW, 0), (dv_b, RW, 0), (da3, RW, 0), (db3, RW, 0), (dg_, RW, 0)],
        prev=[(p, SHIFT, 0)], full=[mu_shift] + prep_params,
        out_tiled=[(SHIFT, F32)],
        out_acc=[(1, RW), (LW + LA, RW), (1, RW), (LW + LA, RW), (LG, RW), (1, RW), (1, RW), (1, SHIFT)])
    d_ps, g_w0, g_w_up_p, g_a0, g_a_up_p, g_g_up, g_k_k, g_k_a, g_mu = prep_outs

    small = {"mu_shift": g_mu, "rwkv_w0": g_w0, "rwkv_a0": g_a0, "rwkv_k_k": g_k_k,
             "rwkv_k_a": g_k_a, "rwkv_r_k": g_r_k, "rwkv_ln_g": g_ln_g, "rwkv_ln_b": g_ln_b, "s5_a_re": g_a_re,
             "s5_a_im": g_a_im, "s5_log_dt": g_log_dt, "s5_b_re": g_b_re, "s5_b_im": g_b_im, "s5_c_re": g_s5_c_re,
             "s5_c_im": g_s5_c_im, "s5_d": g_s5_d, "norm2_g": g_norm2,
             "ffn_conv_b": jnp.concatenate([g_cb_g, g_cb_u], axis=1), "norm_f_g": g_norm_f}
    small_names = list(small)
    g_conv_w = jnp.concatenate([g_cw_g, g_cw_u], axis=1)
    shard_small = {"rwkv_w_up": g_w_up_p[:LW], "rwkv_a_up": g_a_up_p[LW:], "rwkv_g_up": g_g_up, "ffn_conv_w": g_conv_w}
    parts = [small[n] for n in small_names] + [_to_shards(shard_small[n], ax) for n, _, ax in BIG_SMALL]
    spack = _pack_rows(parts, F32, SUBLANES)
    sm_moves = [(0, 0, lambda ref, me, peer: ref, lambda ref, me, k: ref.at[2 * _chip_of(me) + me[2]])]
    sm = _send_start("gsmall_start", ALL_FLIPS, [spack], [sds((8,) + spack.shape, F32)], sm_moves)
    mu_shift = mu_shift + sm["token"]

    def shift_bwd(dps, nx, mu):
        return dps * (1.0 - mu) + _shift_up(dps * mu, nx * mu, 1)

    (d_p,) = rw("shift_bwd", shift_bwd, R=256, tiled=[(d_ps, SHIFT, 0)], nxt=[(d_ps, SHIFT, 0)], full=[mu_shift],
                out_tiled=[(SHIFT, MXU_DTYPE)])
    g_w_in = jnp.concatenate([_mm_tn(h1, d_p, "g_w_p"), _mm_tn(h1, d_u, "g_w_u"), _mm_tn(h1, d_gates, "g_w_g")], axis=1)
    rsn, rsn_moves = reduce_start("in", BIG[:1], (g_w_in,))
    norm1_g = norm1_g + rsn["token"]
    d_h1 = _mm([d_p, d_u, d_gates], [w_p, w_u, w_g], F32, "d_h1", bt=True)

    def norm1_bwd(xv, dh1, dx1v, sc, sh, g):
        _, vjp = jax.vjp(_norm_mod, xv, g, sc, sh)
        dxn, dg, dsc, dsh = vjp(dh1)
        return dx1v + dxn, dsc, dsh, dg

    grad_x, d_sc1, d_sh1, g_norm1 = rw(
        "norm1_bwd", norm1_bwd, R=512, tiled=[(x2d, D, 0), (d_h1, D, 0), (dx1, D, 0)],
        batch=[(mod, D, SC1), (mod, D, SH1)], full=[norm1_g], out_tiled=[(D, F32)], out_batch=[D, D], out_acc=[(1, D)])

    dmod = jnp.concatenate([d_sh1, d_sc1, d_gt1, d_sh2, d_sc2, d_gt2], axis=2).reshape(Bl, 6 * D)
    last_all = _gather_two_level([], [dmod, g_norm1], "gather_dmod")[1]
    dmod_all = last_all[0].reshape(8 * Bl, 6 * D)
    shn, shn_moves = share_start("in", reduced_halves("in", rsn, rsn_moves, dmod_all))
    dmod_cols = lax.dynamic_slice_in_dim(dmod_all, chip * ncol, ncol, 1)
    g_w_ada, g_b_ada = _ada_bwd(c_all, dmod_cols, dmod_all)

    grads = {"norm1_g": _sum_slots(last_all[1].reshape(8, 1, D), F32, "sum_norm1")}
    sm_own, sm_got = _send_wait("gsmall_wait", ALL_FLIPS, sm, sm_moves, g_b_ada)
    s_all = lax.dynamic_update_slice(sm_got[0], sm_own[0][None], (dev, 0, 0))
    s_sum = _sum_slots(s_all, F32, "sum_gsmall").reshape(-1)
    off = 0
    for n in small_names:
        grads[n] = s_sum[off:off + W[n].size].reshape(W[n].shape)
        off += W[n].size
    for n, shape, axis in BIG_SMALL:
        ss = _shard_shape(shape, axis)
        k4 = 4 * math.prod(ss)
        sh4 = s_sum[off:off + k4].reshape(4, math.prod(ss))
        grads[n] = lax.dynamic_index_in_dim(sh4, chip, 0, keepdims=False).reshape((1,) + ss)
        off += k4

    share_finish("ffn", shl, shl_moves, s_sum, BIG_LATE, grads)
    share_finish("mid", shm, shm_moves, grads[BIG_LATE[0][0]], BIG_MID, grads)
    grads["w_ada"] = g_w_ada[None]
    grads["b_ada"] = g_b_ada

    delta, new_m, new_v = {}, {}, {}
    to2 = lambda z: z.reshape(-1, z.shape[-1])

    def adamw(n):
        d_, m_, v2_ = _adamw(to2(W[n]), to2(grads[n]), to2(M[n]), to2(V[n]), "adamw_" + n)
        delta[n], new_m[n], new_v[n] = (z.reshape(W[n].shape) for z in (d_, m_, v2_))

    for n in ["w_ada"] + [b[0] for b in BIG[1:]]:
        adamw(n)
    rest = [n for n in names if n not in delta and n != "w_in"]
    packs = [_pack_rows([src[n] for n in rest], F32, SUBLANES) for src in (W, grads, M, V)]
    d_, m_, v2_ = _adamw(*packs, "adamw_small")
    shapes = [W[n].shape for n in rest]
    for dst, z in ((delta, d_), (new_m, m_), (new_v, v2_)):
        for n, val in zip(rest, _unpack(z.reshape(-1), shapes)):
            dst[n] = val
    share_finish("in", shn, shn_moves, d_, BIG[:1], grads)
    adamw("w_in")

    return (loss, grad_x.reshape(Bl, S, D), *[grads[n] for n in names], *[delta[n] for n in names],
            *[new_m[n] for n in names], *[new_v[n] for n in names])
```

```python
import functools
import math

import jax
import jax.numpy as jnp
from jax import lax
from jax.experimental import pallas as pl
from jax.experimental.pallas import tpu as pltpu

F32 = jnp.float32
BF16 = jnp.bfloat16
MXU_DTYPE = jnp.bfloat16
MESH_IDS = pl.DeviceIdType.MESH
HIGHEST = lax.Precision.HIGHEST

D = 1024
RW, NH, HD = 512, 8, 64
LW, LA, LG = 64, 64, 128
SW, SGC, NG, SP = 512, 16, 32, 64
NSG = 4
SHIFT = 3 * RW + LW + LA + LG
DFF = 2816
RMS_EPS, GN_EPS, L2_EPS = 1e-6, 64e-5, 1e-12
LR, B1, B2, ADAM_EPS, WD, STEP = 0.001, 0.9, 0.999, 1e-8, 0.01, 10
DECAY_SCALE = math.exp(-0.5)
GELU_C = math.sqrt(2.0 / math.pi)

VMEM_LIMIT = 52 * 1024 * 1024
SUBLANES, LANES = 8, 128
HALO = 16


def _pick(n, cap):
    if n <= cap:
        return n
    best = None
    for t in range(LANES, cap + 1, LANES):
        if n % t == 0:
            best = t
    assert best is not None, (n, cap)
    return best


def _params(sem=None, vmem=VMEM_LIMIT):
    return pltpu.CompilerParams(dimension_semantics=sem, vmem_limit_bytes=vmem)


def _chip_of(p):
    return 2 * p[0] + p[1]


def _me():
    return (lax.axis_index("x"), lax.axis_index("y"), lax.axis_index("c"))


def _half(rows, core):
    h = rows // 2
    return pl.ds(pl.multiple_of(core * h, 16 if h % 16 == 0 else SUBLANES), h)


_HBM =pl.BlockSpec(memory_space=pltpu.HBM)
_SEM = pl.BlockSpec(memory_space=pltpu.SEMAPHORE)
_DATAFLOW = pltpu.SideEffectType.DATAFLOW_SIDE_EFFECTING


def _split_copies(flips, moves, src_refs, land_refs, send_sems, recv_sems):
    me = _me()
    nf = len(flips)
    out = []
    for m, (si, li, src_sel, dst_sel) in enumerate(moves):
        for k, f in enumerate(flips):
            peer = tuple(1 - v if b else v for v, b in zip(me, f))
            out.append(pltpu.make_async_remote_copy(
                src_ref=src_sel(src_refs[si], me, peer), dst_ref=dst_sel(land_refs[li], me, k),
                send_sem=send_sems.at[m * nf + k], recv_sem=recv_sems.at[m * nf + k],
                device_id=peer, device_id_type=MESH_IDS))
    return out


def _send_start(name, flips, srcs, land_shapes, moves):
    ns, nl = len(srcs), len(land_shapes)
    n = len(moves) * len(flips)

    def body(*refs):
        for cp in _split_copies(flips, moves, refs[:ns], refs[ns:ns + nl], refs[ns + nl], refs[ns + nl + 1]):
            cp.start()
        refs[-1][...] = jnp.zeros(refs[-1].shape, F32)

    hbm = lambda z: pltpu.with_memory_space_constraint(z, pltpu.HBM)
    lands = [lax.empty(s.shape, s.dtype) for s in land_shapes]
    res = pl.pallas_call(
        body, name=name,
        out_shape=(pltpu.SemaphoreType.DMA((n,)), pltpu.SemaphoreType.DMA((n,)),
                   *[pltpu.HBM(z.shape, z.dtype) for z in srcs], *[pltpu.HBM(s.shape, s.dtype) for s in land_shapes],
                   jax.ShapeDtypeStruct((SUBLANES, LANES), F32)),
        in_specs=[_HBM] * (ns + nl),
        out_specs=(_SEM, _SEM, *[_HBM] * (ns + nl), pl.BlockSpec(memory_space=pltpu.VMEM)),
        input_output_aliases={i: 2 + i for i in range(ns + nl)},
        compiler_params=pltpu.CompilerParams(has_side_effects=_DATAFLOW),
    )(*[hbm(z) for z in srcs], *[hbm(z) for z in lands])
    return {"sems": res[:2], "srcs": list(res[2:2 + ns]), "lands": list(res[2 + ns:2 + ns + nl]), "token": res[-1][0, 0]}


def _send_wait(name, flips, started, moves, after):
    srcs, lands = started["srcs"], started["lands"]
    ns, nl = len(srcs), len(lands)

    def body(*refs):
        for cp in _split_copies(flips, moves, refs[:ns], refs[ns:ns + nl], refs[ns + nl], refs[ns + nl + 1]):
            cp.wait_send()
            cp.wait_recv()

    res = pl.pallas_call(
        body, name=name, out_shape=[pltpu.HBM(z.shape, z.dtype) for z in srcs + lands],
        in_specs=[_HBM] * (ns + nl) + [_SEM, _SEM, pl.BlockSpec(memory_space=pl.ANY)],
        out_specs=[_HBM] * (ns + nl), input_output_aliases={i: i for i in range(ns + nl)},
        compiler_params=pltpu.CompilerParams(has_side_effects=_DATAFLOW),
    )(*srcs, *lands, *started["sems"], after)
    return list(res[:ns]), list(res[ns:])


CHIP_FLIPS = ((1, 0, 0), (0, 1, 0), (1, 1, 0))
PAIR_FLIPS = ((0, 0, 1),)
ALL_FLIPS = CHIP_FLIPS + ((1, 0, 1), (0, 1, 1), (1, 1, 1)) + PAIR_FLIPS


def _gather_two_level(chip_arrs, dev_arrs, name):
    arrs = list(chip_arrs) + list(dev_arrs)
    n, nchip = len(arrs), len(chip_arrs)
    NS = 7

    def body(*refs):
        srcs, outs = refs[:n], refs[n:2 * n]
        send_sems, recv_sems, loc_sems = refs[2 * n:]
        x, y, c = _me()
        sib = (x, y, 1 - c)
        chips = [(1 - x, y), (x, 1 - y), (1 - x, 1 - y)]
        mine = 2 * x + y
        ids = [2 * cx + cy for cx, cy in chips]

        def part(i, slot, core):
            if i < nchip:
                return outs[i].at[slot, _half(arrs[i].shape[0], core)]
            return outs[i].at[slot, core]

        def rcopy(i, k, src, dst, to):
            return pltpu.make_async_remote_copy(src_ref=src, dst_ref=dst, send_sem=send_sems.at[i * NS + k],
                                                recv_sem=recv_sems.at[i * NS + k], device_id=to, device_id_type=MESH_IDS)

        started, locs = [], []
        for i in range(n):
            own = srcs[i].at[_half(arrs[i].shape[0], c)] if i < nchip else srcs[i]
            loc = pltpu.make_async_copy(srcs[i], outs[i].at[mine] if i < nchip else outs[i].at[mine, c], loc_sems.at[i])
            loc.start()
            locs.append(loc)
            for f, chip in enumerate(chips):
                cp = rcopy(i, f, own, part(i, mine, c), (*chip, c))
                cp.start()
                started.append(cp)
            if i >= nchip:
                cp = rcopy(i, 6, own, part(i, mine, c), sib)
                cp.start()
                started.append(cp)
        for i in range(n):
            for f in range(3):
                land = part(i, ids[f], c)
                rcopy(i, f, land, land, sib).wait_recv()
                fw = rcopy(i, 3 + f, land, land, sib)
                fw.start()
                started.append(fw)
        for i in range(n):
            for f in range(3):
                land = part(i, ids[f], 1 - c)
                rcopy(i, 3 + f, land, land, sib).wait_recv()
            if i >= nchip:
                land = part(i, mine, 1 - c)
                rcopy(i, 6, land, land, sib).wait_recv()
        for cp in started:
            cp.wait_send()
        for loc in locs:
            loc.wait()

    outs = [jax.ShapeDtypeStruct((4,) + a.shape, a.dtype) for a in chip_arrs]
    outs += [jax.ShapeDtypeStruct((4, 2) + a.shape, a.dtype) for a in dev_arrs]
    res = pl.pallas_call(
        body, name=name, out_shape=outs,
        in_specs=[pl.BlockSpec(memory_space=pl.ANY)] * n, out_specs=[pl.BlockSpec(memory_space=pl.ANY)] * n,
        scratch_shapes=[pltpu.SemaphoreType.DMA((n * NS,)), pltpu.SemaphoreType.DMA((n * NS,)),
                        pltpu.SemaphoreType.DMA((n,))],
    )(*arrs)
    return res[:nchip], res[nchip:]


def _mm(As, Bs, out_dtype, name, tm=512, cap=1408, bt=False):
    n = len(As)
    M, N = As[0].shape[0], Bs[0].shape[0 if bt else 1]
    if sum(a.shape[1] for a in As) <= 1024:
        tm = 2 * tm
    tm = min(tm, M)
    tn = _pick(N, cap)
    dims = (((1,), (1,)), ((), ())) if bt else (((1,), (0,)), ((), ()))

    def body(*refs):
        o = refs[2 * n]
        acc = None
        for a, b in zip(refs[:n], refs[n:2 * n]):
            d = lax.dot_general(a[...].astype(MXU_DTYPE), b[...].astype(MXU_DTYPE), dims, preferred_element_type=F32)
            acc = d if acc is None else acc + d
        o[...] = acc.astype(o.dtype)

    in_specs = [pl.BlockSpec((tm, a.shape[1]), lambda i, j: (i, 0)) for a in As]
    if bt:
        in_specs += [pl.BlockSpec((tn, b.shape[1]), lambda i, j: (j, 0)) for b in Bs]
    else:
        in_specs += [pl.BlockSpec((b.shape[0], tn), lambda i, j: (0, j)) for b in Bs]
    return pl.pallas_call(
        body, name=name, grid=(M // tm, N // tn), in_specs=in_specs,
        out_specs=pl.BlockSpec((tm, tn), lambda i, j: (i, j)),
        out_shape=jax.ShapeDtypeStruct((M, N), out_dtype),
        compiler_params=_params(("parallel", "parallel")),
    )(*As, *Bs)


def _mm_tn(A, G, name, tt=1024, cap=1408):
    T, Ka = A.shape
    N = G.shape[1]
    tt = min(tt, T)
    tk = _pick(Ka, cap)
    tn = _pick(N, cap)

    def body(a, g, o):
        @pl.when(pl.program_id(2) == 0)
        def _():
            o[...] = jnp.zeros(o.shape, F32)
        o[...] += lax.dot_general(a[...].astype(MXU_DTYPE), g[...].astype(MXU_DTYPE),
                                  (((0,), (0,)), ((), ())), preferred_element_type=F32)

    return pl.pallas_call(
        body, name=name, grid=(Ka // tk, N // tn, T // tt),
        in_specs=[pl.BlockSpec((tt, tk), lambda i, j, t: (t, i)), pl.BlockSpec((tt, tn), lambda i, j, t: (t, j))],
        out_specs=pl.BlockSpec((tk, tn), lambda i, j, t: (i, j)),
        out_shape=jax.ShapeDtypeStruct((Ka, N), F32),
        compiler_params=_params(("parallel", "parallel", "arbitrary")),
    )(A, G)


def _rowwise(name, fn, *, Bl, S, R, tiled=(), prev=(), nxt=(), batch=(), full=(),
             out_tiled=(), out_batch=(), out_acc=()):
    R = min(R, S)
    nS = S // R
    T = Bl * S
    hb = R // HALO
    n_in = len(tiled) + len(prev) + len(nxt) + len(batch) + len(full)

    in_specs, args = [], []
    for a, wd, cb in tiled:
        in_specs.append(pl.BlockSpec((R, wd), lambda b, i, cb=cb: (b * nS + i, cb)))
        args.append(a)
    for a, wd, cb in prev:
        in_specs.append(pl.BlockSpec((HALO, wd), lambda b, i, cb=cb: (jnp.maximum((b * nS + i) * hb - 1, 0), cb)))
        args.append(a)
    for a, wd, cb in nxt:
        in_specs.append(pl.BlockSpec((HALO, wd), lambda b, i, cb=cb: (jnp.minimum((b * nS + i + 1) * hb, T // HALO - 1), cb)))
        args.append(a)
    for a, wd, cb in batch:
        in_specs.append(pl.BlockSpec((1, 1, wd), lambda b, i, cb=cb: (b, 0, cb)))
        args.append(a)
    for a in full:
        in_specs.append(pl.BlockSpec(a.shape, lambda b, i, nd=a.ndim: (0,) * nd))
        args.append(a)

    out_specs, out_shape = [], []
    for C, dt in out_tiled:
        out_specs.append(pl.BlockSpec((R, C), lambda b, i: (b * nS + i, 0)))
        out_shape.append(jax.ShapeDtypeStruct((T, C), dt))
    for C in out_batch:
        out_specs.append(pl.BlockSpec((1, 1, C), lambda b, i: (b, 0, 0)))
        out_shape.append(jax.ShapeDtypeStruct((Bl, 1, C), F32))
    for shp in out_acc:
        out_specs.append(pl.BlockSpec(shp, lambda b, i, nd=len(shp): (0,) * nd))
        out_shape.append(jax.ShapeDtypeStruct(shp, F32))

    nt, npv, nnx, nbt = len(tiled), len(prev), len(nxt), len(batch)

    def body(*refs):
        b, i = pl.program_id(0), pl.program_id(1)
        ins, outs = refs[:n_in], refs[n_in:]
        vals = [r[...] for r in ins[:nt]]
        vals += [jnp.where(i > 0, r[...], jnp.zeros(r.shape, r.dtype)) for r in ins[nt:nt + npv]]
        vals += [jnp.where(i < nS - 1, r[...], jnp.zeros(r.shape, r.dtype)) for r in ins[nt + npv:nt + npv + nnx]]
        vals += [r[0] for r in ins[nt + npv + nnx:nt + npv + nnx + nbt]]
        vals += [r[...] for r in ins[nt + npv + nnx + nbt:]]
        res = fn(*vals)
        if not isinstance(res, (tuple, list)):
            res = (res,)
        k = 0
        for _ in out_tiled:
            outs[k][...] = res[k].astype(outs[k].dtype)
            k += 1
        for _ in out_batch:
            o = outs[k]

            @pl.when(i == 0)
            def _(o=o):
                o[...] = jnp.zeros(o.shape, F32)
            o[0] += res[k]
            k += 1
        for _ in out_acc:
            o = outs[k]

            @pl.when((i == 0) & (b == 0))
            def _(o=o):
                o[...] = jnp.zeros(o.shape, F32)
            o[...] += res[k]
            k += 1

    out = pl.pallas_call(
        body, name=name, grid=(Bl, nS), in_specs=in_specs, out_specs=out_specs, out_shape=out_shape,
        compiler_params=_params(("arbitrary", "arbitrary")),
    )(*args)
    return out


def _colwise(name, fn, *, Bl, S, R, W, strip, tiled=(), prev=(), nxt=(), full=(), out_tiled=(), n_acc=0):
    R = min(R, S)
    nS = S // R
    T = Bl * S
    hb = R // HALO
    nt, npv, nnx, nfl = len(tiled), len(prev), len(nxt), len(full)
    n_in = nt + npv + nnx + nfl
    in_specs = [pl.BlockSpec((R, W), lambda b, i, cb=cb: (b * nS + i, cb)) for _, cb in tiled]
    in_specs += [pl.BlockSpec((HALO, W), lambda b, i, cb=cb: (jnp.maximum((b * nS + i) * hb - 1, 0), cb)) for _, cb in prev]
    in_specs += [pl.BlockSpec((HALO, W), lambda b, i, cb=cb: (jnp.minimum((b * nS + i + 1) * hb, T // HALO - 1), cb))
                 for _, cb in nxt]
    in_specs += [pl.BlockSpec(a.shape, lambda b, i: (0, 0)) for a in full]
    out_specs = [pl.BlockSpec((R, m * W), lambda b, i: (b * nS + i, 0)) for m, _ in out_tiled]
    out_specs += [pl.BlockSpec((1, W), lambda b, i: (0, 0))] * n_acc
    out_shape = [jax.ShapeDtypeStruct((T, m * W), dt) for m, dt in out_tiled] + [jax.ShapeDtypeStruct((1, W), F32)] * n_acc

    def body(*refs):
        b, i = pl.program_id(0), pl.program_id(1)
        ins, outs = refs[:n_in], refs[n_in:]

        @pl.when((i == 0) & (b == 0))
        def _():
            for o in outs[len(out_tiled):]:
                o[...] = jnp.zeros(o.shape, F32)

        def col(j, carry):
            cs = pl.ds(pl.multiple_of(j * strip, strip), strip)
            vals = [r[:, cs] for r in ins[:nt]]
            vals += [jnp.where(i > 0, r[:, cs], jnp.zeros((HALO, strip), r.dtype)) for r in ins[nt:nt + npv]]
            vals += [jnp.where(i < nS - 1, r[:, cs], jnp.zeros((HALO, strip), r.dtype)) for r in ins[nt + npv:nt + npv + nnx]]
            vals += [r[:, cs] for r in ins[nt + npv + nnx:]]
            res = fn(*vals)
            for k, (m, _) in enumerate(out_tiled):
                for q in range(m):
                    outs[k][:, pl.ds(pl.multiple_of(q * W + j * strip, strip), strip)] = res[k][q].astype(outs[k].dtype)
            for k in range(len(out_tiled), len(outs)):
                outs[k][:, cs] += res[k]
            return carry

        lax.fori_loop(0, W // strip, col, 0)

    return pl.pallas_call(
        body, name=name, grid=(Bl, nS), in_specs=in_specs, out_specs=out_specs, out_shape=out_shape,
        compiler_params=_params(("arbitrary", "arbitrary")),
    )(*[a for a, _ in tiled], *[a for a, _ in prev], *[a for a, _ in nxt], *full)


def _shift_down(x, halo, k):
    rolled = pltpu.roll(x, k, 0)
    row = lax.broadcasted_iota(jnp.int32, (SUBLANES, x.shape[1]), 0)
    head = rolled[0:SUBLANES]
    for j in range(k):
        head = jnp.where(row == j, halo[HALO - k + j:HALO - k + j + 1, :], head)
    return jnp.concatenate([head, rolled[SUBLANES:]], axis=0)


def _shift_up(x, halo, k):
    n = x.shape[0]
    rolled = pltpu.roll(x, n - k, 0)
    row = lax.broadcasted_iota(jnp.int32, (SUBLANES, x.shape[1]), 0)
    tail = rolled[n - SUBLANES:]
    for j in range(k):
        tail = jnp.where(row == SUBLANES - k + j, halo[j:j + 1, :], tail)
    return jnp.concatenate([rolled[:n - SUBLANES], tail], axis=0)


def _dotm(a, b):
    return jnp.dot(a.astype(MXU_DTYPE), b.astype(MXU_DTYPE), preferred_element_type=F32)


def _split_bf16(x):
    hi = x.astype(BF16)
    return hi, (x - hi.astype(F32)).astype(BF16)


def _headsum_2pass(x, hm):
    hi, lo = _split_bf16(x)
    hb = hm.astype(BF16)
    return jnp.dot(hi, hb, preferred_element_type=F32) + jnp.dot(lo, hb, preferred_element_type=F32)


@jax.custom_vjp
def _headsum(x, hm):
    return _headsum_2pass(x, hm)


_headsum.defvjp(lambda x, hm: (_headsum_2pass(x, hm), hm),
                lambda hm, g: (_headsum_2pass(g, hm), jnp.zeros_like(hm)))


def _sigmoid(x):
    return 0.5 * jnp.tanh(0.5 * x) + 0.5


def _rms(x, g):
    return x * lax.rsqrt(jnp.mean(x * x, axis=-1, keepdims=True) + RMS_EPS) * g


def _norm_mod(x, g, sc, sh):
    return _rms(x, g) * (1.0 + sc) + sh


def _split_ps(ps):
    return (ps[:, 0:RW], ps[:, RW:2 * RW], ps[:, 2 * RW:3 * RW], ps[:, 3 * RW:3 * RW + LW + LA],
            ps[:, 3 * RW + LW + LA:SHIFT])


def _rwkv_prep(r, k, v, wa, gd, w0, w_up_p, a0, a_up_p, g_up, k_k, k_a, hm):
    w_raw = w0 + _dotm(jnp.tanh(wa), w_up_p)
    decay = jnp.exp(-DECAY_SCALE * _sigmoid(w_raw))
    a = _sigmoid(a0 + _dotm(wa, a_up_p))
    g = _dotm(_sigmoid(gd), g_up)
    kk = k * k_k
    kk = kk * lax.rsqrt(_headsum(kk * kk, hm) + L2_EPS)
    k2 = k * (1.0 + (a - 1.0) * k_a)
    return r, decay, k2, v, -kk, kk * a, g


def _rwkv_post(y, r, k2, v, g, ln_g, ln_b, r_k, hm):
    mean = _headsum(y, hm) * (1.0 / HD)
    yc = y - mean
    var = _headsum(yc * yc, hm) * (1.0 / HD)
    yn = yc * lax.rsqrt(var + GN_EPS) * ln_g + ln_b
    bonus = _headsum(r * k2 * r_k, hm) * v
    return (yn + bonus) * g


def _gelu(x):
    return 0.5 * x * (1.0 + jnp.tanh(GELU_C * (x + 0.044715 * (x * x * x))))


def _s5_post(yssm, u, d):
    return _gelu(yssm + d * u)


def _mix(ga, gb, ya, za, zb):
    return _sigmoid(ga) * ya + _sigmoid(gb) * (za * _sigmoid(zb))


def _conv_act(up_g, up_u, hg, hu, w_g, w_u, b_g, b_u):
    gate, upv = _conv3(up_g, hg, w_g, b_g)[0], _conv3(up_u, hu, w_u, b_u)[0]
    return gate, upv


def _conv3(x, h, w, b):
    x, h = x.astype(F32), h.astype(F32)
    s2, s1 = _shift_down(x, h, 2), _shift_down(x, h, 1)
    return b + w[0:1] * s2 + w[1:2] * s1 + w[2:3] * x, (s2, s1, x)


def _silu_gate(gate, upv):
    return gate * _sigmoid(gate) * upv


WKV_L = 64
_NT, _NN, _TN = ((1,), (1,)), ((1,), (0,)), ((0,), (0,))


def _dotw(x, y, dims):
    return lax.dot_general(x.astype(MXU_DTYPE), y.astype(MXU_DTYPE), (dims, ((), ())), preferred_element_type=F32)


def _dot3(x, y, dims):
    (xh, xl), (yh, yl) = _split_bf16(x), _split_bf16(y)
    d = lambda p, q: lax.dot_general(p, q, (dims, ((), ())), preferred_element_type=F32)
    return d(xh, yh) + d(xh, yl) + d(xl, yh)


@jax.custom_vjp
def _gram3(x, y):
    return _dot3(x, y, _NT)


_gram3.defvjp(lambda x, y: (_dot3(x, y, _NT), (x, y)),
              lambda res, g: (_dot3(g, res[1], _NN), _dot3(g, res[0], _TN)))


def _tri_solve_fwd(ns, xs):
    each = lambda f, *ls: tuple(f(*zs) for zs in zip(*ls))
    size = ns[0].shape[0]
    eye = (lax.broadcasted_iota(jnp.int32, (size, size), 0) == lax.broadcasted_iota(jnp.int32, (size, size), 1)).astype(F32)
    ts = each(lambda n: n + eye, ns)
    qs = ns
    for _ in range(WKV_L.bit_length() - 2):
        qs = each(lambda q: _dotw(q, q, _NN), qs)
        ts = each(lambda t, q: t + _dotw(t, q, _NN), ts, qs)
    us = each(lambda t, x: _dotw(t, x, _NN), ts, xs)
    return us, (ts, us)


def _tri_solve_bwd(res, dus):
    ts, us = res
    each = lambda f, *ls: tuple(f(*zs) for zs in zip(*ls))
    dxs = each(lambda t, du: _dotw(t, du, _TN), ts, dus)
    return each(lambda dx, u: _dotw(dx, u, _NT), dxs, us), dxs


@jax.custom_vjp
def _tri_solve(ns, xs):
    return _tri_solve_fwd(ns, xs)[0]


_tri_solve.defvjp(_tri_solve_fwd, _tri_solve_bwd)


@jax.custom_vjp
def _tri_apply(ns, xs, ts):
    return tuple(_dotw(t, x, _NN) for t, x in zip(ts, xs))


_tri_apply.defvjp(lambda ns, xs, ts: (lambda us: (us, (ts, us)))(tuple(_dotw(t, x, _NN) for t, x in zip(ts, xs))),
                  lambda res, dus: _tri_solve_bwd(res, dus) + (tuple(jnp.zeros_like(t) for t in res[0]),))


def _wkv_chunk(s0, r, w, k, v, a, b):
    y, s1 = _wkv_chunks((s0,), (r,), (w,), (k,), (v,), (a,), (b,))
    return y[0], s1[0]


def _wkv_chunks(s0, r, w, k, v, a, b, tinv=None, want_tinv=False):
    each = lambda f, *ls: tuple(f(*xs) for xs in zip(*ls))
    L = r[0].shape[0]
    n2 = 2 * L
    lane_head = lax.broadcasted_iota(jnp.int32, (2, 1, 2 * HD), 2) // HD
    head_mask = (lane_head == lax.broadcasted_iota(jnp.int32, (2, 1, 2 * HD), 0)).astype(F32)
    ri = lax.broadcasted_iota(jnp.int32, (n2, n2), 0)
    ci = lax.broadcasted_iota(jnp.int32, (n2, n2), 1)
    same = (ri // L) == (ci // L)
    strict = same & ((ci % L) < (ri % L))
    incl = same & ((ci % L) <= (ri % L))
    si = lax.broadcasted_iota(jnp.int32, (2 * HD, 2 * HD), 0) // HD
    sj = lax.broadcasted_iota(jnp.int32, (2 * HD, 2 * HD), 1) // HD
    tri = (lax.broadcasted_iota(jnp.int32, (L, L), 0) >= lax.broadcasted_iota(jnp.int32, (L, L), 1)).astype(F32)

    stack = lambda z: (z[None] * head_mask).reshape(n2, 2 * HD)
    dup = lambda z: jnp.broadcast_to(z[None], (2, L, 2 * HD)).reshape(n2, 2 * HD)
    gram = _gram3
    nt, nn, tn = (lambda x, y, d=d: _dotw(x, y, d) for d in (_NT, _NN, _TN))
    add = lambda x, y: x + y

    lw = each(jnp.log, w)
    cum = each(lambda z: jnp.dot(tri, z, preferred_element_type=F32, precision=HIGHEST), lw)
    tot = each(lambda z: jnp.sum(z, axis=0, keepdims=True), lw)
    a2 = each(lambda av, cv, lv: stack(av * jnp.exp(cv - lv)), a, cum, lw)
    r2 = each(lambda rv, cv: stack(rv * jnp.exp(cv)), r, cum)
    v2 = each(stack, v)
    b2 = each(lambda bv, cv: dup(bv * jnp.exp(-cv)), b, cum)
    k2 = each(lambda kv, cv: dup(kv * jnp.exp(-cv)), k, cum)
    n_ab = each(lambda x, y: jnp.where(strict, gram(x, y), 0.0), a2, b2)
    n_ak = each(lambda x, y: jnp.where(strict, gram(x, y), 0.0), a2, k2)
    m_rb = each(lambda x, y: jnp.where(incl, gram(x, y), 0.0), r2, b2)
    m_rk = each(lambda x, y: jnp.where(incl, gram(x, y), 0.0), r2, k2)
    x = each(add, each(nt, a2, s0), each(nn, n_ak, v2))
    if want_tinv:
        u, (tinv, _) = _tri_solve_fwd(n_ab, x)
    else:
        u = _tri_solve(n_ab, x) if tinv is None else _tri_apply(n_ab, x, tinv)
    y2 = each(lambda x, y, z: x + y + z, each(nt, r2, s0), each(nn, m_rb, u), each(nn, m_rk, v2))
    y = each(lambda z: jnp.sum(z.reshape(2, L, 2 * HD), axis=0), y2)
    b3 = each(lambda bv, tv, cv: dup(bv * jnp.exp(tv - cv)), b, tot, cum)
    k3 = each(lambda kv, tv, cv: dup(kv * jnp.exp(tv - cv)), k, tot, cum)
    upd = each(add, each(tn, u, b3), each(tn, v2, k3))
    s1 = each(lambda sv, tv, uv: sv * jnp.exp(tv) + jnp.where(si == sj, uv, 0.0), s0, tot, upd)
    return (y, s1, tinv) if want_tinv else (y, s1)


NPAIR = NH // 2


def _wkv_nb(Bl):
    return 4 if Bl % 4 == 0 else 2 if Bl % 2 == 0 else 1


def _wkv_fwd(r, w, k, v, a, b, Bl, S):
    L = WKV_L
    nC = S // L
    nb = _wkv_nb(Bl)
    chains = [(bi, p, slice(p * 2 * HD, (p + 1) * 2 * HD)) for bi in range(nb) for p in range(NPAIR)]

    def body(r_ref, w_ref, k_ref, v_ref, a_ref, b_ref, y_ref, ck_ref, ti_ref, s_ref):
        @pl.when(pl.program_id(1) == 0)
        def _():
            s_ref[...] = jnp.zeros(s_ref.shape, F32)
        s0 = tuple(s_ref[bi, p] for bi, p, _ in chains)
        ops = [tuple(z[bi, :, cs] for bi, _, cs in chains) for z in (r_ref, w_ref, k_ref, v_ref, a_ref, b_ref)]
        y, s1, tinv = _wkv_chunks(s0, *ops, want_tinv=True)
        for i, (bi, p, cs) in enumerate(chains):
            ck_ref[bi, 0, p] = s0[i]
            ti_ref[bi, 0, p] = tinv[i]
            y_ref[bi, :, cs] = y[i]
            s_ref[bi, p] = s1[i]

    to3 = lambda z: z.reshape(Bl, S, RW)
    row_spec = pl.BlockSpec((nb, L, RW), lambda g, c: (g, c, 0))
    mats = jax.ShapeDtypeStruct((Bl, nC, NPAIR, 2 * HD, 2 * HD), F32)
    mat_spec = pl.BlockSpec((nb, 1, NPAIR, 2 * HD, 2 * HD), lambda g, c: (g, c, 0, 0, 0))
    y, ck, ti = pl.pallas_call(
        body, name="wkv_fwd", grid=(Bl // nb, nC), in_specs=[row_spec] * 6,
        out_specs=[row_spec, mat_spec, mat_spec],
        out_shape=[jax.ShapeDtypeStruct((Bl, S, RW), F32), mats, mats],
        scratch_shapes=[pltpu.VMEM((nb, NPAIR, 2 * HD, 2 * HD), F32)],
        compiler_params=_params(("arbitrary", "arbitrary")),
    )(*(to3(z) for z in (r, w, k, v, a, b)))
    return y.reshape(Bl * S, RW), ck, ti


def _wkv_bwd(r, w, k, v, a, b, dy, ck, ti, Bl, S):
    L = WKV_L
    nC = S // L
    nb = _wkv_nb(Bl)
    chains = [(bi, p, slice(p * 2 * HD, (p + 1) * 2 * HD)) for bi in range(nb) for p in range(NPAIR)]

    def body(r_ref, w_ref, k_ref, v_ref, a_ref, b_ref, dy_ref, ck_ref, ti_ref,
             dr_ref, dw_ref, dk_ref, dv_ref, da_ref, db_ref, ds_ref):
        @pl.when(pl.program_id(1) == 0)
        def _():
            ds_ref[...] = jnp.zeros(ds_ref.shape, F32)
        s0 = tuple(ck_ref[bi, 0, p] for bi, p, _ in chains)
        tinv = tuple(ti_ref[bi, 0, p] for bi, p, _ in chains)
        ops = [tuple(z[bi, :, cs] for bi, _, cs in chains) for z in (r_ref, w_ref, k_ref, v_ref, a_ref, b_ref)]
        cts = (tuple(dy_ref[bi, :, cs] for bi, _, cs in chains), tuple(ds_ref[bi, p] for bi, p, _ in chains))
        ds0, *grads = jax.vjp(lambda *z: _wkv_chunks(*z, tinv=tinv), s0, *ops)[1](cts)
        for i, (bi, p, cs) in enumerate(chains):
            ds_ref[bi, p] = ds0[i]
            for o, g in zip((dr_ref, dw_ref, dk_ref, dv_ref, da_ref, db_ref), grads):
                o[bi, :, cs] = g[i]

    to3 = lambda z: z.reshape(Bl, S, RW)
    row_spec = pl.BlockSpec((nb, L, RW), lambda g, c: (g, nC - 1 - c, 0))
    rows = jax.ShapeDtypeStruct((Bl, S, RW), F32)
    mat_spec = pl.BlockSpec((nb, 1, NPAIR, 2 * HD, 2 * HD), lambda g, c: (g, nC - 1 - c, 0, 0, 0))
    outs = pl.pallas_call(
        body, name="wkv_bwd", grid=(Bl // nb, nC),
        in_specs=[row_spec] * 7 + [mat_spec, mat_spec],
        out_specs=[row_spec] * 6, out_shape=[rows] * 6,
        scratch_shapes=[pltpu.VMEM((nb, NPAIR, 2 * HD, 2 * HD), F32)],
        compiler_params=_params(("arbitrary", "arbitrary")),
    )(*(to3(z) for z in (r, w, k, v, a, b, dy)), ck, ti)
    return [o.reshape(Bl * S, RW) for o in outs]


NST = NG * SP


def _cmul(ar, ai, br, bi):
    return ar * br - ai * bi, ar * bi + ai * br


def _s5_tiles(are, aim, reverse):
    if reverse:
        aim = -aim
    row = lax.broadcasted_iota(jnp.int32, (SUBLANES, NST), 0)
    pw = [(are, aim)]
    for _ in range(SUBLANES - 1):
        pw.append(_cmul(pw[-1][0], pw[-1][1], are, aim))
    bc = lambda z: jnp.broadcast_to(z, (SUBLANES, NST))
    ms = []
    for kk in (1, 2, 4):
        cond = (row < SUBLANES - kk) if reverse else (row >= kk)
        ms.append((jnp.where(cond, bc(pw[kk - 1][0]), 0.0), jnp.where(cond, bc(pw[kk - 1][1]), 0.0)))
    pr = jnp.zeros((SUBLANES, NST), F32)
    pi = jnp.zeros((SUBLANES, NST), F32)
    for i in range(SUBLANES):
        n = SUBLANES - i if reverse else i + 1
        pr = jnp.where(row == i, bc(pw[n - 1][0]), pr)
        pi = jnp.where(row == i, bc(pw[n - 1][1]), pi)
    return ms, (pr, pi)


def _s5_block(re, im, ms, pc, cre, cim, sg, reverse):
    ln = slice(sg * 512, (sg + 1) * 512)
    for (mr, mi), kk in zip(ms, (1, 2, 4)):
        sh = SUBLANES - kk if reverse else kk
        sre, sim = pltpu.roll(re, sh, 0), pltpu.roll(im, sh, 0)
        tr, ti = _cmul(mr[:, ln], mi[:, ln], sre, sim)
        re, im = re + tr, im + ti
    tr, ti = _cmul(pc[0][:, ln], pc[1][:, ln], cre[:, ln], cim[:, ln])
    return re + tr, im + ti


def _s5_scan(X_ref, n_rows, ms, pc, cre, cim, reverse, visit=None, acc0=None):
    nblk = n_rows // SUBLANES

    def it(i, carry):
        cre, cim, acc = carry
        j = nblk - 1 - i if reverse else i
        rows = pl.ds(pl.multiple_of(j * SUBLANES, SUBLANES), SUBLANES)
        edge = 0 if reverse else SUBLANES - 1
        blocks, ncre, ncim = [], [], []
        for sg in range(NSG):
            lr = slice(sg * 1024, sg * 1024 + 512)
            li = slice(sg * 1024 + 512, (sg + 1) * 1024)
            re, im = _s5_block(X_ref[rows, lr], X_ref[rows, li], ms, pc, cre, cim, sg, reverse)
            X_ref[rows, lr] = re
            X_ref[rows, li] = im
            blocks.append((re, im))
            ncre.append(re[edge:edge + 1])
            ncim.append(im[edge:edge + 1])
        if visit is not None:
            acc = visit(j, blocks, acc)
        return jnp.concatenate(ncre, axis=1), jnp.concatenate(ncim, axis=1), acc

    return lax.fori_loop(0, nblk, it, (cre, cim, acc0 if acc0 is not None else 0))


def _s5_fwd(u, wb, wc, ab, d, Bl, S, R=256):
    R = min(R, S)
    nC = S // R

    def body(u_ref, wb_ref, wc_ref, ab_ref, d_ref, y_ref, st_ref, X_ref, o_ref, car_ref):
        @pl.when(pl.program_id(1) == 0)
        def _():
            car_ref[...] = jnp.zeros(car_ref.shape, F32)
        st_ref[0, 0] = car_ref[...]
        ms, pc = _s5_tiles(ab_ref[0:1], ab_ref[1:2], False)
        for sg in range(NSG):
            X_ref[:, sg * 1024:(sg + 1) * 1024] = _dotm(u_ref[:, sg * 128:(sg + 1) * 128], wb_ref[sg])
        cre, cim, _ = _s5_scan(X_ref, R, ms, pc, car_ref[0:1], car_ref[1:2], False)
        car_ref[0:1] = cre
        car_ref[1:2] = cim
        for sg in range(NSG):
            y_ref[:, sg * 128:(sg + 1) * 128] = _dotm(X_ref[:, sg * 1024:(sg + 1) * 1024], wc_ref[sg])
        o_ref[...] = _s5_post(y_ref[...], u_ref[...], d_ref[...]).astype(o_ref.dtype)

    rows = pl.BlockSpec((R, SW), lambda b, c: (b * nC + c, 0))
    return pl.pallas_call(
        body, name="s5_fwd", grid=(Bl, nC),
        in_specs=[rows, pl.BlockSpec(wb.shape, lambda b, c: (0, 0, 0)), pl.BlockSpec(wc.shape, lambda b, c: (0, 0, 0)),
                  pl.BlockSpec(ab.shape, lambda b, c: (0, 0)), pl.BlockSpec(d.shape, lambda b, c: (0, 0))],
        out_specs=[rows, pl.BlockSpec((1, 1, 2, NST), lambda b, c: (b, c, 0, 0)),
                   pl.BlockSpec((R, 2 * NST), lambda b, c: (b * nC + c, 0)), rows],
        out_shape=[jax.ShapeDtypeStruct((Bl * S, SW), F32), jax.ShapeDtypeStruct((Bl, nC, 2, NST), F32),
                   jax.ShapeDtypeStruct((Bl * S, 2 * NST), F32), jax.ShapeDtypeStruct((Bl * S, SW), MXU_DTYPE)],
        scratch_shapes=[pltpu.VMEM((2, NST), F32)],
        compiler_params=_params(("arbitrary", "arbitrary")),
    )(u, wb, wc, ab, d)


def _s5_bwd(u, y, do, d, wb, wc, ab, st, xs, Bl, S, R=256):
    R = min(R, S)
    nC = S // R

    def body(u_ref, y_ref, do_ref, d_ref, wb_ref, wc_ref, ab_ref, st_ref, X_ref,
             du_ref, dwb_ref, dwc_ref, dab_ref, dd_ref, G_ref, car_ref):
        first = (pl.program_id(0) == 0) & (pl.program_id(1) == 0)

        @pl.when(first)
        def _():
            for o in (dwb_ref, dwc_ref, dab_ref, dd_ref):
                o[...] = jnp.zeros(o.shape, F32)

        @pl.when(pl.program_id(1) == 0)
        def _():
            car_ref[...] = jnp.zeros(car_ref.shape, F32)

        are, aim = ab_ref[0:1], ab_ref[1:2]
        dy, du_direct, dd = jax.vjp(_s5_post, y_ref[...], u_ref[...], d_ref[...])[1](do_ref[...])
        dd_ref[...] += dd
        dyv = dy.astype(MXU_DTYPE)
        for sg in range(NSG):
            G_ref[:, sg * 1024:(sg + 1) * 1024] = lax.dot_general(
                dyv[:, sg * 128:(sg + 1) * 128], wc_ref[sg].astype(MXU_DTYPE), (((1,), (1,)), ((), ())),
                preferred_element_type=F32)
        rms_, rpc = _s5_tiles(are, aim, True)
        row = lax.broadcasted_iota(jnp.int32, (SUBLANES, 512), 0)

        def visit(j, blocks, acc):
            before = pl.multiple_of(jnp.maximum(j - 1, 0) * SUBLANES, SUBLANES)
            prow = X_ref[pl.ds(before, SUBLANES), :][SUBLANES - 1:SUBLANES]
            rows = pl.ds(pl.multiple_of(j * SUBLANES, SUBLANES), SUBLANES)
            are_acc, aim_acc = [], []
            for sg in range(NSG):
                lr = slice(sg * 1024, sg * 1024 + 512)
                li = slice(sg * 1024 + 512, (sg + 1) * 1024)
                ln = slice(sg * 512, (sg + 1) * 512)
                pre = jnp.where(j > 0, prow[:, lr], st_ref[0, 0, 0:1, ln])
                pim = jnp.where(j > 0, prow[:, li], st_ref[0, 0, 1:2, ln])
                xre = jnp.where(row == 0, pre, pltpu.roll(X_ref[rows, lr], 1, 0))
                xim = jnp.where(row == 0, pim, pltpu.roll(X_ref[rows, li], 1, 0))
                dre, dim = blocks[sg]
                are_acc.append(dre * xre + dim * xim)
                aim_acc.append(dim * xre - dre * xim)
            return acc[0] + jnp.concatenate(are_acc, axis=1), acc[1] + jnp.concatenate(aim_acc, axis=1)

        zero = jnp.zeros((SUBLANES, NST), F32)
        cre, cim, acc = _s5_scan(G_ref, R, rms_, rpc, car_ref[0:1], car_ref[1:2], True, visit, (zero, zero))
        car_ref[0:1] = cre
        car_ref[1:2] = cim
        dab_ref[0:1] += jnp.sum(acc[0], axis=0, keepdims=True)
        dab_ref[1:2] += jnp.sum(acc[1], axis=0, keepdims=True)
        uv = u_ref[...].astype(MXU_DTYPE)
        for sg in range(NSG):
            cs = slice(sg * 1024, (sg + 1) * 1024)
            us = slice(sg * 128, (sg + 1) * 128)
            gx = G_ref[:, cs].astype(MXU_DTYPE)
            dwb_ref[sg] += lax.dot_general(uv[:, us], gx, (((0,), (0,)), ((), ())), preferred_element_type=F32)
            dwc_ref[sg] += lax.dot_general(X_ref[:, cs].astype(MXU_DTYPE), dyv[:, us], (((0,), (0,)), ((), ())),
                                           preferred_element_type=F32)
            du_ssm = lax.dot_general(gx, wb_ref[sg].astype(MXU_DTYPE), (((1,), (1,)), ((), ())),
                                     preferred_element_type=F32)
            du_ref[:, us] = (du_ssm + du_direct[:, us]).astype(du_ref.dtype)

    rmap = lambda b, c: (b * nC + nC - 1 - c, 0)
    rows = pl.BlockSpec((R, SW), rmap)
    return pl.pallas_call(
        body, name="s5_bwd", grid=(Bl, nC),
        in_specs=[rows, rows, rows, pl.BlockSpec(d.shape, lambda b, c: (0, 0)),
                  pl.BlockSpec(wb.shape, lambda b, c: (0, 0, 0)), pl.BlockSpec(wc.shape, lambda b, c: (0, 0, 0)),
                  pl.BlockSpec(ab.shape, lambda b, c: (0, 0)),
                  pl.BlockSpec((1, 1, 2, NST), lambda b, c: (b, nC - 1 - c, 0, 0)),
                  pl.BlockSpec((R, 2 * NST), rmap)],
        out_specs=[rows, pl.BlockSpec(wb.shape, lambda b, c: (0, 0, 0)),
                   pl.BlockSpec(wc.shape, lambda b, c: (0, 0, 0)), pl.BlockSpec((2, NST), lambda b, c: (0, 0)),
                   pl.BlockSpec(d.shape, lambda b, c: (0, 0))],
        out_shape=[jax.ShapeDtypeStruct((Bl * S, SW), MXU_DTYPE), jax.ShapeDtypeStruct(wb.shape, F32),
                   jax.ShapeDtypeStruct(wc.shape, F32), jax.ShapeDtypeStruct((2, NST), F32),
                   jax.ShapeDtypeStruct(d.shape, F32)],
        scratch_shapes=[pltpu.VMEM((R, 2 * NST), F32), pltpu.VMEM((2, NST), F32)],
        compiler_params=_params(("arbitrary", "arbitrary")),
    )(u, y, do, d, wb, wc, ab, st, xs)


def _s5_disc_math(a_re, a_im, log_dt, b_re, b_im, expand):
    dt = jnp.exp(log_dt)
    z_re, z_im = a_re * dt, a_im * dt
    mag = jnp.exp(z_re)
    ab_re, ab_im = mag * jnp.cos(z_im), mag * jnp.sin(z_im)
    den = a_re * a_re + a_im * a_im
    q_re = ((ab_re - 1.0) * a_re + ab_im * a_im) / den
    q_im = (ab_im * a_re - (ab_re - 1.0) * a_im) / den
    qe_re = jnp.dot(q_re, expand, preferred_element_type=F32, precision=HIGHEST)
    qe_im = jnp.dot(q_im, expand, preferred_element_type=F32, precision=HIGHEST)
    return ab_re, ab_im, qe_re * b_re - qe_im * b_im, qe_re * b_im + qe_im * b_re


def _whole(shape):
    return pl.BlockSpec(shape, lambda nd=len(shape): (0,) * nd)


def _s5_disc(a_re, a_im, log_dt, b_re, b_im, expand):
    def body(a, b, c, d, e, f, o0, o1, o2, o3):
        res = _s5_disc_math(a[...], b[...], c[...], d[...], e[...], f[...])
        for o, v in zip((o0, o1, o2, o3), res):
            o[...] = v
    ins = (a_re, a_im, log_dt, b_re, b_im, expand)
    outs = [jax.ShapeDtypeStruct(a_re.shape, F32)] * 2 + [jax.ShapeDtypeStruct(b_re.shape, F32)] * 2
    return pl.pallas_call(body, name="s5_disc", in_specs=[_whole(x.shape) for x in ins],
                          out_specs=[_whole(o.shape) for o in outs], out_shape=outs)(*ins)


def _s5_disc_bwd(a_re, a_im, log_dt, b_re, b_im, expand, cts):
    def body(a, b, c, d, e, f, g0, g1, g2, g3, o0, o1, o2, o3, o4):
        fn = lambda *p: _s5_disc_math(*p, f[...])
        _, vjp = jax.vjp(fn, a[...], b[...], c[...], d[...], e[...])
        for o, v in zip((o0, o1, o2, o3, o4), vjp((g0[...], g1[...], g2[...], g3[...]))):
            o[...] = v
    ins = (a_re, a_im, log_dt, b_re, b_im, expand) + tuple(cts)
    outs = [jax.ShapeDtypeStruct(x.shape, F32) for x in (a_re, a_im, log_dt, b_re, b_im)]
    return pl.pallas_call(body, name="s5_disc_bwd", in_specs=[_whole(x.shape) for x in ins],
                          out_specs=[_whole(o.shape) for o in outs], out_shape=outs)(*ins)


def _ada_fwd(c_all, w_shard, b_shard):
    def body(c_ref, w_ref, b_ref, o_ref):
        cv = c_ref[...]
        o_ref[...] = _dotm(cv * _sigmoid(cv), w_ref[...]) + b_ref[...]
    n = w_shard.shape[1]
    return pl.pallas_call(
        body, name="ada_fwd", in_specs=[_whole(c_all.shape), _whole(w_shard.shape), _whole(b_shard.shape)],
        out_specs=_whole((c_all.shape[0], n)), out_shape=jax.ShapeDtypeStruct((c_all.shape[0], n), F32),
        compiler_params=_params(),
    )(c_all, w_shard, b_shard)


def _ada_bwd(c_all, dmod_cols, dmod_all):
    def body(c_ref, dc_ref, da_ref, gw_ref, gb_ref):
        cv = c_ref[...]
        gw_ref[...] = lax.dot_general((cv * _sigmoid(cv)).astype(MXU_DTYPE), dc_ref[...].astype(MXU_DTYPE),
                                      (((0,), (0,)), ((), ())), preferred_element_type=F32)
        gb_ref[...] = jnp.sum(da_ref[...], axis=0, keepdims=True)
    n = dmod_cols.shape[1]
    return pl.pallas_call(
        body, name="ada_bwd", in_specs=[_whole(c_all.shape), _whole(dmod_cols.shape), _whole(dmod_all.shape)],
        out_specs=[_whole((D, n)), _whole((1, dmod_all.shape[1]))],
        out_shape=[jax.ShapeDtypeStruct((D, n), F32), jax.ShapeDtypeStruct((1, dmod_all.shape[1]), F32)],
        compiler_params=_params(),
    )(c_all, dmod_cols, dmod_all)


def _rows_block(n_rows, cap=512):
    if n_rows <= cap:
        return n_rows
    for t in range(cap - cap % SUBLANES, 0, -SUBLANES):
        if n_rows % t == 0:
            return t
    return n_rows


def _adamw(w, g, m, v, name):
    rows, cols = w.shape
    tr = _rows_block(rows, max(SUBLANES, (1 << 19) // max(cols, 1) // SUBLANES * SUBLANES))

    def body(w_ref, g_ref, m_ref, v_ref, d_ref, nm_ref, nv_ref):
        gv = g_ref[...]
        nm = B1 * m_ref[...] + (1.0 - B1) * gv
        nv = B2 * v_ref[...] + (1.0 - B2) * (gv * gv)
        m_hat = nm / (1.0 - B1 ** STEP)
        v_hat = nv / (1.0 - B2 ** STEP)
        d_ref[...] = -LR * (m_hat / (jnp.sqrt(v_hat) + ADAM_EPS) + WD * w_ref[...])
        nm_ref[...] = nm
        nv_ref[...] = nv

    spec = pl.BlockSpec((tr, cols), lambda i: (i, 0))
    sd = jax.ShapeDtypeStruct((rows, cols), F32)
    return pl.pallas_call(body, name=name, grid=(rows // tr,), in_specs=[spec] * 4, out_specs=[spec] * 3,
                          out_shape=[sd] * 3, compiler_params=_params(("parallel",)))(w, g, m, v)


def _sum_slots(x, out_dtype, name):
    xs = x if isinstance(x, (list, tuple)) else [x]
    _, rows, cols = xs[0].shape
    tr = _rows_block(rows)

    def body(*refs):
        acc = None
        for x_ref in refs[:-1]:
            for j in range(x_ref.shape[0]):
                term = x_ref[j].astype(F32)
                acc = term if acc is None else acc + term
        refs[-1][...] = acc.astype(refs[-1].dtype)

    return pl.pallas_call(
        body, name=name, grid=(rows // tr,),
        in_specs=[pl.BlockSpec((z.shape[0], tr, cols), lambda i: (0, i, 0)) for z in xs],
        out_specs=pl.BlockSpec((tr, cols), lambda i: (i, 0)), out_shape=jax.ShapeDtypeStruct((rows, cols), out_dtype),
        compiler_params=_params(("parallel",)))(*xs)


PACK_COLS = 1024


def _pack_rows(parts, dtype, row_mult):
    flat = jnp.concatenate([p.reshape(-1).astype(dtype) for p in parts])
    per = PACK_COLS * row_mult
    n = -(-flat.shape[0] // per) * per
    flat = jnp.pad(flat, (0, n - flat.shape[0]))
    return flat.reshape(n // PACK_COLS, PACK_COLS)


def _unpack(flat, shapes):
    out, off = [], 0
    for s in shapes:
        n = math.prod(s)
        out.append(flat[off:off + n].reshape(s))
        off += n
    return out


BIG = (("w_in", (D, SHIFT + SW + 2 * D), 1), ("w_out_rwkv", (RW, D), 1), ("w_glu", (SW, 2 * D), 1),
       ("w_out", (D, D), 0), ("w_ffn_up", (D, 2 * DFF), 1), ("w_ffn_down", (DFF, D), 0))
BIG_SMALL = (("rwkv_w_up", (LW, RW), 1), ("rwkv_a_up", (LA, RW), 1), ("rwkv_g_up", (LG, RW), 1),
             ("ffn_conv_w", (3, 2 * DFF), 1))
BIG_LATE = BIG[4:]
BIG_MID = BIG[1:4]


def _shard_shape(shape, axis):
    return (shape[0] // 4, shape[1]) if axis == 0 else (shape[0], shape[1] // 4)


def _to_shards(g, axis):
    r, C = g.shape
    return g.reshape(4, r // 4, C) if axis == 0 else g.reshape(r, 4, C // 4).transpose(1, 0, 2)


def _from_shards(x, axis):
    _, r, C = x.shape
    return x.reshape(4 * r, C) if axis == 0 else x.transpose(1, 0, 2).reshape(r, 4 * C)


def kernel(x, c, w_ada, b_ada, norm1_g, w_in, mu_shift, rwkv_w0, rwkv_w_up, rwkv_a0, rwkv_a_up, rwkv_g_up, rwkv_k_k, rwkv_k_a, rwkv_r_k, rwkv_ln_g, rwkv_ln_b, w_out_rwkv, s5_a_re, s5_a_im, s5_log_dt, s5_b_re, s5_b_im, s5_c_re, s5_c_im, s5_d, w_glu, w_out, norm2_g, w_ffn_up, ffn_conv_w, ffn_conv_b, w_ffn_down, norm_f_g, loss_target, m_w_ada, m_b_ada, m_norm1_g, m_w_in, m_mu_shift, m_rwkv_w0, m_rwkv_w_up, m_rwkv_a0, m_rwkv_a_up, m_rwkv_g_up, m_rwkv_k_k, m_rwkv_k_a, m_rwkv_r_k, m_rwkv_ln_g, m_rwkv_ln_b, m_w_out_rwkv, m_s5_a_re, m_s5_a_im, m_s5_log_dt, m_s5_b_re, m_s5_b_im, m_s5_c_re, m_s5_c_im, m_s5_d, m_w_glu, m_w_out, m_norm2_g, m_w_ffn_up, m_ffn_conv_w, m_ffn_conv_b, m_w_ffn_down, m_norm_f_g, v_w_ada, v_b_ada, v_norm1_g, v_w_in, v_mu_shift, v_rwkv_w0, v_rwkv_w_up, v_rwkv_a0, v_rwkv_a_up, v_rwkv_g_up, v_rwkv_k_k, v_rwkv_k_a, v_rwkv_r_k, v_rwkv_ln_g, v_rwkv_ln_b, v_w_out_rwkv, v_s5_a_re, v_s5_a_im, v_s5_log_dt, v_s5_b_re, v_s5_b_im, v_s5_c_re, v_s5_c_im, v_s5_d, v_w_glu, v_w_out, v_norm2_g, v_w_ffn_up, v_ffn_conv_w, v_ffn_conv_b, v_w_ffn_down, v_norm_f_g):
    names = ["w_ada", "b_ada", "norm1_g", "w_in", "mu_shift", "rwkv_w0", "rwkv_w_up", "rwkv_a0", "rwkv_a_up",
             "rwkv_g_up", "rwkv_k_k", "rwkv_k_a", "rwkv_r_k", "rwkv_ln_g", "rwkv_ln_b", "w_out_rwkv", "s5_a_re",
             "s5_a_im", "s5_log_dt", "s5_b_re", "s5_b_im", "s5_c_re", "s5_c_im", "s5_d", "w_glu", "w_out", "norm2_g",
             "w_ffn_up", "ffn_conv_w", "ffn_conv_b", "w_ffn_down", "norm_f_g"]
    env = dict(locals())
    W = {n: env[n] for n in names}
    M = {n: env["m_" + n] for n in names}
    V = {n: env["v_" + n] for n in names}

    Bl, S, _ = x.shape
    T = Bl * S
    ix, iy, ic = lax.axis_index("x"), lax.axis_index("y"), lax.axis_index("c")
    chip = 2 * ix + iy
    dev = 2 * chip + ic
    rw = functools.partial(_rowwise, Bl=Bl, S=S)

    got_chip, got_dev = _gather_two_level([W[n][0] for n, _, _ in BIG_SMALL[:3]], [W["ffn_conv_w"][0], c], "gather_w")
    full = {n: _from_shards(g, axis) for (n, _, axis), g in zip(BIG_SMALL[:3], got_chip)}
    full["ffn_conv_w"] = _from_shards(got_dev[0][:, 0], 1)
    c_all = got_dev[1].reshape(8 * Bl, D)
    zeros_l = jnp.zeros((LW, RW), F32)
    w_up_p = jnp.concatenate([full["rwkv_w_up"], zeros_l], axis=0)
    a_up_p = jnp.concatenate([zeros_l, full["rwkv_a_up"]], axis=0)
    g_up = full["rwkv_g_up"]
    conv_w = full["ffn_conv_w"]
    conv_wg, conv_wu = conv_w[:, :DFF], conv_w[:, DFF:]
    conv_bg, conv_bu = ffn_conv_b[:, :DFF], ffn_conv_b[:, DFF:]
    hm = jnp.kron(jnp.eye(NH, dtype=F32), jnp.ones((HD, HD), F32))

    ncol = 6 * D // 4
    b_ada_cols = lax.dynamic_slice_in_dim(b_ada, chip * ncol, ncol, 1)
    mod_part = _ada_fwd(c_all, w_ada[0], b_ada_cols)
    mod4 = _gather_two_level([], [mod_part], "gather_mod")[1][0][:, 0]
    mod4, shards = lax.optimization_barrier((mod4, [W[n][0].astype(MXU_DTYPE) for n, _, _ in BIG]))

    def push_shards(tag, arrs):
        moves = [(i, i, lambda ref, me, peer: ref, lambda ref, me, k: ref.at[_chip_of(me)]) for i in range(len(arrs))]
        lands = [jax.ShapeDtypeStruct((4,) + z.shape, z.dtype) for z in arrs]
        return _send_start("gather_%s_start" % tag, CHIP_FLIPS, arrs, lands, moves), moves

    def pushed_shards(tag, started, moves, after, group):
        owns, gots = _send_wait("gather_%s_wait" % tag, CHIP_FLIPS, started, moves, after)
        for (n, _, axis), own, got in zip(group, owns, gots):
            full[n] = _from_shards(lax.dynamic_update_slice(got, own[None], (chip, 0, 0)), axis)

    first_start, first_moves = push_shards("in", shards[:1])
    norm1_g = norm1_g + first_start["token"]
    mod =lax.dynamic_slice_in_dim(mod4, dev * Bl, Bl, 1).transpose(1, 0, 2).reshape(Bl, 1, 6 * D)
    SH1, SC1, GT1, SH2, SC2, GT2 = range(6)

    x2d = x.reshape(T, D)
    tgt = loss_target.reshape(T, D)

    (h1,) = rw("norm1", lambda xv, sc, sh, g: _norm_mod(xv, g, sc, sh), R=512, tiled=[(x2d, D, 0)],
               batch=[(mod, D, SC1), (mod, D, SH1)], full=[norm1_g], out_tiled=[(D, MXU_DTYPE)])
    pushed_shards("in", first_start, first_moves, h1, BIG[:1])
    full["w_in"], rest = lax.optimization_barrier((full["w_in"], shards[1:]))
    late_start, late_moves = push_shards("rest", rest)
    mu_shift = mu_shift + late_start["token"]
    w_p, w_u, w_g = full["w_in"][:, :SHIFT], full["w_in"][:, SHIFT:SHIFT + SW], full["w_in"][:, SHIFT + SW:]
    p = _mm([h1], [w_p], F32, "proj_p")
    u = _mm([h1], [w_u], F32, "proj_u")
    gates = _mm([h1], [w_g], MXU_DTYPE, "proj_g")

    prep_params = [rwkv_w0, w_up_p, rwkv_a0, a_up_p, g_up, rwkv_k_k, rwkv_k_a, hm]

    def prep_fwd(pv, ph, mu, *pp):
        ps = pv + (_shift_down(pv, ph, 1) - pv) * mu
        return _rwkv_prep(*_split_ps(ps), *pp)

    r_, w_, k_, v_, a_, b_, g_ = rw("rwkv_prep", prep_fwd, R=256, tiled=[(p, SHIFT, 0)], prev=[(p, SHIFT, 0)],
                                    full=[mu_shift] + prep_params, out_tiled=[(RW, F32)] * 7)
    y_wkv, ck, tinv = _wkv_fwd(r_, w_, k_, v_, a_, b_, Bl, S)
    r_k_row = rwkv_r_k.reshape(1, RW)
    post_params = [rwkv_ln_g, rwkv_ln_b, r_k_row, hm]
    (o_rwkv,) = rw("rwkv_post", _rwkv_post, R=256,
                   tiled=[(y_wkv, RW, 0), (r_, RW, 0), (k_, RW, 0), (v_, RW, 0), (g_, RW, 0)],
                   full=post_params, out_tiled=[(RW, MXU_DTYPE)])
    pushed_shards("rest", late_start, late_moves, o_rwkv, BIG[1:])
    y_a = _mm([o_rwkv], [full["w_out_rwkv"]], MXU_DTYPE, "out_rwkv")

    expand = jnp.kron(jnp.eye(SP, dtype=F32), jnp.ones((1, SGC), F32))
    s5_in = (s5_a_re[0], s5_a_im[0], s5_log_dt[0].reshape(NG, 1), s5_b_re[0].reshape(NG, SP * SGC),
             s5_b_im[0].reshape(NG, SP * SGC), expand)
    ab_re, ab_im, bb_re, bb_im = _s5_disc(*s5_in)
    eye8 = jnp.eye(8, dtype=F32)

    def blockdiag_in(bb):
        t = bb.reshape(NSG, 8, SP, SGC)
        return jnp.einsum("ab,sapc->sacbp", eye8, t).reshape(NSG, 128, 512)

    def blockdiag_out(cc):
        t = cc.reshape(NSG, 8, SGC, SP)
        return jnp.einsum("ab,sacp->sapbc", eye8, t).reshape(NSG, 512, 128)

    wb = jnp.concatenate([blockdiag_in(bb_re), blockdiag_in(bb_im)], axis=2).astype(MXU_DTYPE)
    wc = jnp.concatenate([blockdiag_out(s5_c_re[0]), -blockdiag_out(s5_c_im[0])], axis=1).astype(MXU_DTYPE)
    ab = jnp.stack([ab_re.reshape(NST), ab_im.reshape(NST)])
    y_ssm, s5_st, s5_x, s5o = _s5_fwd(u, wb, wc, ab, s5_d, Bl, S)
    z = _mm([s5o], [full["w_glu"]], MXU_DTYPE, "glu")
    mix_tiled = [(gates, D, 0), (gates, D, 1), (y_a, D, 0), (z, D, 0), (z, D, 1)]
    (mixed_in,) = rw("mix", lambda *a: _mix(*(v.astype(F32) for v in a)), R=256, tiled=mix_tiled,
                     out_tiled=[(D, MXU_DTYPE)])
    mixed = _mm([mixed_in], [full["w_out"]], F32, "out_proj")

    def norm2_fwd(xv, mx, gt, sc, sh, g):
        x1 = xv + gt * mx
        return x1, _norm_mod(x1, g, sc, sh)

    x1, h2 = rw("norm2", norm2_fwd, R=512, tiled=[(x2d, D, 0), (mixed, D, 0)],
                batch=[(mod, D, GT1), (mod, D, SC2), (mod, D, SH2)], full=[norm2_g],
                out_tiled=[(D, F32), (D, MXU_DTYPE)])
    up =_mm([h2], [full["w_ffn_up"]], MXU_DTYPE, "ffn_up")
    conv_tiled = [(up, 0), (up, 1)]
    conv_full = [conv_wg, conv_wu, conv_bg, conv_bu]
    cw = functools.partial(_colwise, Bl=Bl, S=S, R=128, W=DFF, strip=LANES)

    def act_fwd(*a):
        return ((_silu_gate(*_conv_act(*a)),),)

    (act,) = cw("ffn_act", act_fwd, tiled=conv_tiled, prev=conv_tiled, full=conv_full, out_tiled=[(1, MXU_DTYPE)])
    ffn = _mm([act], [full["w_ffn_down"]], F32, "ffn_down")

    def head(x1v, fv, tv, gt, g):
        x2 = x1v + gt * fv
        y, vjp = jax.vjp(_rms, x2, g)
        e = y - tv
        dx2, dg = vjp(e * (1.0 / D))
        loss = jnp.sum(e * e, keepdims=True) * jnp.ones((1, LANES), F32)
        return dx2, dx2 * gt, jnp.sum(dx2 * fv, axis=0, keepdims=True), dg.reshape(1, D), loss

    dx2, d_ffn, d_gt2, g_norm_f, loss_acc = rw(
        "head", head, R=512, tiled=[(x1, D, 0), (ffn, D, 0), (tgt, D, 0)], batch=[(mod, D, GT2)],
        full=[norm_f_g.reshape(1, D)], out_tiled=[(D, F32), (D, MXU_DTYPE)], out_batch=[D],
        out_acc=[(1, D), (1, LANES)])
    loss = lax.psum(0.5 / D * loss_acc[0, 0], ("x", "y", "c"))

    d_act = _mm([d_ffn], [full["w_ffn_down"]], F32, "d_act", bt=True)
    g_w_ffn_down = _mm_tn(act, d_ffn, "g_ffn_down")

    def act_bwd(ug, uu, dact, hg, hu, wg, wu, bg, bu):
        (gate, taps_g), (upv, taps_u) = _conv3(ug, hg, wg, bg), _conv3(uu, hu, wu, bu)
        _, vjp_s = jax.vjp(_silu_gate, gate, upv)
        d_gate, d_upv = vjp_s(dact)
        def taps(dh, shifted):
            return [jnp.sum(dh * s, axis=0, keepdims=True) for s in shifted] + [jnp.sum(dh, axis=0, keepdims=True)]
        return ((d_gate,), (d_upv,), *taps(d_gate, taps_g), *taps(d_upv, taps_u))

    dh_g, dh_u, *tapg = cw("ffn_act_bwd", act_bwd, tiled=conv_tiled + [(d_act, 0)], prev=conv_tiled, full=conv_full,
                           out_tiled=[(1, MXU_DTYPE), (1, MXU_DTYPE)], n_acc=8)
    g_cw_g, g_cb_g = jnp.concatenate(tapg[0:3], axis=0), tapg[3]
    g_cw_u, g_cb_u = jnp.concatenate(tapg[4:7], axis=0), tapg[7]

    def conv_t(dg, du_, ng, nu, wg, wu):
        dg, du_, ng, nu = (z.astype(F32) for z in (dg, du_, ng, nu))

        def ct(d, n, w):
            return w[2:3] * d + w[1:2] * _shift_up(d, n, 1) + w[0:1] * _shift_up(d, n, 2)
        return ((ct(dg, ng, wg), ct(du_, nu, wu)),)

    (d_up,) = cw("conv_bwd", conv_t, tiled=[(dh_g, 0), (dh_u, 0)], nxt=[(dh_g, 0), (dh_u, 0)],
                 full=[conv_wg, conv_wu], out_tiled=[(2, MXU_DTYPE)])
    d_h2 = _mm([d_up], [full["w_ffn_up"]], F32, "d_h2", bt=True)
    g_w_ffn_up = _mm_tn(h2, d_up, "g_ffn_up")

    sds = jax.ShapeDtypeStruct
    reduce_src = lambda r: (lambda ref, me, peer: ref.at[_chip_of(peer), _half(r, peer[2])])

    def reduced_halves(tag, started, moves, after):
        gsh_own, got = _send_wait("rs_%s_wait" % tag, ALL_FLIPS, started, moves, after)
        halves = []
        for i, (g, gt) in enumerate(zip(gsh_own, got)):
            h = g.shape[1] // 2
            own = lax.dynamic_slice(g, (chip, ic * h, 0), (1, h, g.shape[2]))
            halves.append(_sum_slots([own, gt], F32, "rs_%s_sum%d" % (tag, i)))
        return halves

    def share_start(tag, halves):
        moves = [(i, i, lambda ref, me, peer: ref, lambda ref, me, k, r=2 * g.shape[0]: ref.at[_half(r, me[2])])
                 for i, g in enumerate(halves)]
        lands = [sds((2 * g.shape[0], g.shape[1]), F32) for g in halves]
        return _send_start("share_%s_start" % tag, PAIR_FLIPS, halves, lands, moves), moves

    def share_finish(tag, started, moves, after, group, grads):
        mine_h, got = _send_wait("share_%s_wait" % tag, PAIR_FLIPS, started, moves, after)
        for (n, _, _), mh, whole in zip(group, mine_h, got):
            grads[n] = lax.dynamic_update_slice(whole, mh, (ic * mh.shape[0], 0))[None]

    def reduce_start(tag, group, mats):
        gsh = [_to_shards(g, ax).astype(MXU_DTYPE) for g, (_, _, ax) in zip(mats, group)]
        moves = [(i, i, reduce_src(g.shape[1]), lambda ref, me, k: ref.at[k]) for i, g in enumerate(gsh)]
        lands = [sds((len(ALL_FLIPS), g.shape[1] // 2, g.shape[2]), MXU_DTYPE) for g in gsh]
        return _send_start("rs_%s_start" % tag, ALL_FLIPS, gsh, lands, moves), moves

    rsl, rsl_moves = reduce_start("ffn", BIG_LATE, (g_w_ffn_up, g_w_ffn_down))
    norm2_g = norm2_g + rsl["token"]

    def norm2_bwd(x1v, dh2, dx2v, mx, gt, sc, sh, g):
        _, vjp = jax.vjp(_norm_mod, x1v, g, sc, sh)
        dxn, dg, dsc, dsh = vjp(dh2)
        dx1 = dx2v + dxn
        return dx1, dx1 * gt, jnp.sum(dx1 * mx, axis=0, keepdims=True), dsc, dsh, dg

    dx1, d_mixed, d_gt1, d_sc2, d_sh2, g_norm2 = rw(
        "norm2_bwd", norm2_bwd, R=512, tiled=[(x1, D, 0), (d_h2, D, 0), (dx2, D, 0), (mixed, D, 0)],
        batch=[(mod, D, GT1), (mod, D, SC2), (mod, D, SH2)], full=[norm2_g],
        out_tiled=[(D, F32), (D, MXU_DTYPE)], out_batch=[D, D, D], out_acc=[(1, D)])

    d_mixed_in = _mm([d_mixed], [full["w_out"]], MXU_DTYPE, "d_mixed_in", bt=True)
    g_w_out = _mm_tn(mixed_in, d_mixed, "g_w_out")

    def mix_bwd(*a):
        ga, gb, ya, za, zb, dm = (v.astype(F32) for v in a)
        _, vjp = jax.vjp(_mix, ga, gb, ya, za, zb)
        dga, dgb, dya, dza, dzb = vjp(dm)
        return jnp.concatenate([dga, dgb], axis=1), dya, jnp.concatenate([dza, dzb], axis=1)

    d_gates, d_ya, d_z = rw("mix_bwd", mix_bwd, R=256, tiled=mix_tiled + [(d_mixed_in, D, 0)],
                            out_tiled=[(2 * D, MXU_DTYPE), (D, MXU_DTYPE), (2 * D, MXU_DTYPE)])
    d_o_rwkv = _mm([d_ya], [full["w_out_rwkv"]], F32, "d_o_rwkv", bt=True)
    g_w_out_rwkv = _mm_tn(o_rwkv, d_ya, "g_out_rwkv")
    d_s5o = _mm([d_z], [full["w_glu"]], F32, "d_s5o", bt=True)
    g_w_glu = _mm_tn(s5o, d_z, "g_glu")
    rsm, rsm_moves = reduce_start("mid", BIG_MID, (g_w_out_rwkv, g_w_glu, g_w_out))
    s5_d = s5_d + rsm["token"]

    d_u, d_wb, d_wc, d_ab, g_s5_d = _s5_bwd(u, y_ssm, d_s5o, s5_d, wb, wc, ab, s5_st, s5_x, Bl, S)

    def diag_in(dw):
        t = dw.reshape(NSG, 8, SGC, 8, SP)
        return jnp.einsum("ab,sacbp->sapc", eye8, t).reshape(NG, SP * SGC)

    def diag_out(dw):
        t = dw.reshape(NSG, 8, SP, 8, SGC)
        return jnp.einsum("ab,sapbc->sacp", eye8, t).reshape(NG, SGC, SP)

    g_s5_c_re = diag_out(d_wc[:, :512])
    g_s5_c_im = -diag_out(d_wc[:, 512:])
    disc_cts = (d_ab[0].reshape(NG, SP), d_ab[1].reshape(NG, SP), diag_in(d_wb[:, :, :512]), diag_in(d_wb[:, :, 512:]))
    g_a_re, g_a_im, g_log_dt, g_b_re, g_b_im = _s5_disc_bwd(*s5_in, disc_cts)

    def post_bwd(yv, rv, kv, vv, gv, do, *pp):
        _, vjp = jax.vjp(lambda *a: _rwkv_post(*a, pp[3]), yv, rv, kv, vv, gv, *pp[:3])
        return vjp(do)

    dy_wkv, dr_b, dk_b, dv_b, dg_, g_ln_g, g_ln_b, g_r_k = rw(
        "rwkv_post_bwd", post_bwd, R=256,
        tiled=[(y_wkv, RW, 0), (r_, RW, 0), (k_, RW, 0), (v_, RW, 0), (g_, RW, 0), (d_o_rwkv, RW, 0)],
        full=post_params, out_tiled=[(RW, F32)] * 5, out_acc=[(1, RW)] * 3)
    dr3, dw3, dk3, dv3, da3, db3 = _wkv_bwd(r_, w_, k_, v_, a_, b_, dy_wkv, ck, tinv, Bl, S)

    shl, shl_moves = share_start("ffn", reduced_halves("ffn", rsl, rsl_moves, dr3))
    shm, shm_moves = share_start("mid", reduced_halves("mid", rsm, rsm_moves, dr3))
    mu_shift = mu_shift + (shl["token"] + shm["token"])

    def prep_bwd(pv, dr1, dr2, dwv, dk1, dk2, dv1, dv2, dav, dbv, dgv, ph, mu, *pp):
        prev = _shift_down(pv, ph, 1)
        ps = pv + (prev - pv) * mu
        _, vjp = jax.vjp(lambda *q: _rwkv_prep(*q, pp[7]), *_split_ps(ps), *pp[:7])
        grads = vjp((dr1 + dr2, dwv, dk1 + dk2, dv1 + dv2, dav, dbv, dgv))
        dps = jnp.concatenate(grads[:5], axis=1)
        return (dps,) + tuple(grads[5:]) + (jnp.sum(dps * (prev - pv), axis=0, keepdims=True),)

    prep_outs = rw(
        "rwkv_prep_bwd", prep_bwd, R=256,
        tiled=[(p, SHIFT, 0), (dr3, RW, 0), (dr_b, RW, 0), (dw3, RW, 0), (dk3, RW, 0), (dk_b, RW, 0),
               (dv3, RW, 0), (dv_b, RW, 0), (da3, RW, 0), (db3, RW, 0), (dg_, RW, 0)],
        prev=[(p, SHIFT, 0)], full=[mu_shift] + prep_params,
        out_tiled=[(SHIFT, F32)],
        out_acc=[(1, RW), (LW + LA, RW), (1, RW), (LW + LA, RW), (LG, RW), (1, RW), (1, RW), (1, SHIFT)])
    d_ps, g_w0, g_w_up_p, g_a0, g_a_up_p, g_g_up, g_k_k, g_k_a, g_mu = prep_outs

    small = {"mu_shift": g_mu, "rwkv_w0": g_w0, "rwkv_a0": g_a0, "rwkv_k_k": g_k_k,
             "rwkv_k_a": g_k_a, "rwkv_r_k": g_r_k, "rwkv_ln_g": g_ln_g, "rwkv_ln_b": g_ln_b, "s5_a_re": g_a_re,
             "s5_a_im": g_a_im, "s5_log_dt": g_log_dt, "s5_b_re": g_b_re, "s5_b_im": g_b_im, "s5_c_re": g_s5_c_re,
             "s5_c_im": g_s5_c_im, "s5_d": g_s5_d, "norm2_g": g_norm2,
             "ffn_conv_b": jnp.concatenate([g_cb_g, g_cb_u], axis=1), "norm_f_g": g_norm_f}
    small_names = list(small)
    g_conv_w = jnp.concatenate([g_cw_g, g_cw_u], axis=1)
    shard_small = {"rwkv_w_up": g_w_up_p[:LW], "rwkv_a_up": g_a_up_p[LW:], "rwkv_g_up": g_g_up, "ffn_conv_w": g_conv_w}
    parts = [small[n] for n in small_names] + [_to_shards(shard_small[n], ax) for n, _, ax in BIG_SMALL]
    spack = _pack_rows(parts, F32, SUBLANES)
    sm_moves = [(0, 0, lambda ref, me, peer: ref, lambda ref, me, k: ref.at[2 * _chip_of(me) + me[2]])]
    sm = _send_start("gsmall_start", ALL_FLIPS, [spack], [sds((8,) + spack.shape, F32)], sm_moves)
    mu_shift = mu_shift + sm["token"]

    def shift_bwd(dps, nx, mu):
        return dps * (1.0 - mu) + _shift_up(dps * mu, nx * mu, 1)

    (d_p,) = rw("shift_bwd", shift_bwd, R=256, tiled=[(d_ps, SHIFT, 0)], nxt=[(d_ps, SHIFT, 0)], full=[mu_shift],
                out_tiled=[(SHIFT, MXU_DTYPE)])
    g_w_in = jnp.concatenate([_mm_tn(h1, d_p, "g_w_p"), _mm_tn(h1, d_u, "g_w_u"), _mm_tn(h1, d_gates, "g_w_g")], axis=1)
    rsn, rsn_moves = reduce_start("in", BIG[:1], (g_w_in,))
    norm1_g = norm1_g + rsn["token"]
    d_h1 = _mm([d_p, d_u, d_gates], [w_p, w_u, w_g], F32, "d_h1", bt=True)

    def norm1_bwd(xv, dh1, dx1v, sc, sh, g):
        _, vjp = jax.vjp(_norm_mod, xv, g, sc, sh)
        dxn, dg, dsc, dsh = vjp(dh1)
        return dx1v + dxn, dsc, dsh, dg

    grad_x, d_sc1, d_sh1, g_norm1 = rw(
        "norm1_bwd", norm1_bwd, R=512, tiled=[(x2d, D, 0), (d_h1, D, 0), (dx1, D, 0)],
        batch=[(mod, D, SC1), (mod, D, SH1)], full=[norm1_g], out_tiled=[(D, F32)], out_batch=[D, D], out_acc=[(1, D)])

    dmod = jnp.concatenate([d_sh1, d_sc1, d_gt1, d_sh2, d_sc2, d_gt2], axis=2).reshape(Bl, 6 * D)
    last_all = _gather_two_level([], [dmod, g_norm1], "gather_dmod")[1]
    dmod_all = last_all[0].reshape(8 * Bl, 6 * D)
    shn, shn_moves = share_start("in", reduced_halves("in", rsn, rsn_moves, dmod_all))
    dmod_cols = lax.dynamic_slice_in_dim(dmod_all, chip * ncol, ncol, 1)
    g_w_ada, g_b_ada = _ada_bwd(c_all, dmod_cols, dmod_all)

    grads = {"norm1_g": _sum_slots(last_all[1].reshape(8, 1, D), F32, "sum_norm1")}
    sm_own, sm_got = _send_wait("gsmall_wait", ALL_FLIPS, sm, sm_moves, g_b_ada)
    s_all = lax.dynamic_update_slice(sm_got[0], sm_own[0][None], (dev, 0, 0))
    s_sum = _sum_slots(s_all, F32, "sum_gsmall").reshape(-1)
    off = 0
    for n in small_names:
        grads[n] = s_sum[off:off + W[n].size].reshape(W[n].shape)
        off += W[n].size
    for n, shape, axis in BIG_SMALL:
        ss = _shard_shape(shape, axis)
        k4 = 4 * math.prod(ss)
        sh4 = s_sum[off:off + k4].reshape(4, math.prod(ss))
        grads[n] = lax.dynamic_index_in_dim(sh4, chip, 0, keepdims=False).reshape((1,) + ss)
        off += k4

    share_finish("ffn", shl, shl_moves, s_sum, BIG_LATE, grads)
    share_finish("mid", shm, shm_moves, grads[BIG_LATE[0][0]], BIG_MID, grads)
    grads["w_ada"] = g_w_ada[None]
    grads["b_ada"] = g_b_ada

    delta, new_m, new_v = {}, {}, {}
    to2 = lambda z: z.reshape(-1, z.shape[-1])

    def adamw(n):
        d_, m_, v2_ = _adamw(to2(W[n]), to2(grads[n]), to2(M[n]), to2(V[n]), "adamw_" + n)
        delta[n], new_m[n], new_v[n] = (z.reshape(W[n].shape) for z in (d_, m_, v2_))

    for n in ["w_ada"] + [b[0] for b in BIG[1:]]:
        adamw(n)
    rest = [n for n in names if n not in delta and n != "w_in"]
    packs = [_pack_rows([src[n] for n in rest], F32, SUBLANES) for src in (W, grads, M, V)]
    d_, m_, v2_ = _adamw(*packs, "adamw_small")
    shapes = [W[n].shape for n in rest]
    for dst, z in ((delta, d_), (new_m, m_), (new_v, v2_)):
        for n, val in zip(rest, _unpack(z.reshape(-1), shapes)):
            dst[n] = val
    share_finish("in", shn, shn_moves, d_, BIG[:1], grads)
    adamw("w_in")

    return (loss, grad_x.reshape(Bl, S, D), *[grads[n] for n in names], *[delta[n] for n in names],
            *[new_m[n] for n in names], *[new_v[n] for n in names])
```

```python
import functools
import math

import jax
import jax.numpy as jnp
from jax import lax
from jax.experimental import pallas as pl
from jax.experimental.pallas import tpu as pltpu

F32 = jnp.float32
BF16 = jnp.bfloat16
MXU_DTYPE = jnp.bfloat16
MESH_IDS = pl.DeviceIdType.MESH
HIGHEST = lax.Precision.HIGHEST

D = 1024
RW, NH, HD = 512, 8, 64
LW, LA, LG = 64, 64, 128
SW, SGC, NG, SP = 512, 16, 32, 64
NSG = 4
SHIFT = 3 * RW + LW + LA + LG
DFF = 2816
RMS_EPS, GN_EPS, L2_EPS = 1e-6, 64e-5, 1e-12
LR, B1, B2, ADAM_EPS, WD, STEP = 0.001, 0.9, 0.999, 1e-8, 0.01, 10
DECAY_SCALE = math.exp(-0.5)
GELU_C = math.sqrt(2.0 / math.pi)

VMEM_LIMIT = 52 * 1024 * 1024
SUBLANES, LANES = 8, 128
HALO = 16


def _pick(n, cap):
    if n <= cap:
        return n
    best = None
    for t in range(LANES, cap + 1, LANES):
        if n % t == 0:
            best = t
    assert best is not None, (n, cap)
    return best


def _params(sem=None, vmem=VMEM_LIMIT):
    return pltpu.CompilerParams(dimension_semantics=sem, vmem_limit_bytes=vmem)


def _chip_of(p):
    return 2 * p[0] + p[1]


def _me():
    return (lax.axis_index("x"), lax.axis_index("y"), lax.axis_index("c"))


def _half(rows, core):
    h = rows // 2
    return pl.ds(pl.multiple_of(core * h, 16 if h % 16 == 0 else SUBLANES), h)


_HBM =pl.BlockSpec(memory_space=pltpu.HBM)
_SEM = pl.BlockSpec(memory_space=pltpu.SEMAPHORE)
_DATAFLOW = pltpu.SideEffectType.DATAFLOW_SIDE_EFFECTING


def _split_copies(flips, moves, src_refs, land_refs, send_sems, recv_sems):
    me = _me()
    nf = len(flips)
    out = []
    for m, (si, li, src_sel, dst_sel) in enumerate(moves):
        for k, f in enumerate(flips):
            peer = tuple(1 - v if b else v for v, b in zip(me, f))
            out.append(pltpu.make_async_remote_copy(
                src_ref=src_sel(src_refs[si], me, peer), dst_ref=dst_sel(land_refs[li], me, k),
                send_sem=send_sems.at[m * nf + k], recv_sem=recv_sems.at[m * nf + k],
                device_id=peer, device_id_type=MESH_IDS))
    return out


def _send_start(name, flips, srcs, land_shapes, moves):
    ns, nl = len(srcs), len(land_shapes)
    n = len(moves) * len(flips)

    def body(*refs):
        for cp in _split_copies(flips, moves, refs[:ns], refs[ns:ns + nl], refs[ns + nl], refs[ns + nl + 1]):
            cp.start()
        refs[-1][...] = jnp.zeros(refs[-1].shape, F32)

    hbm = lambda z: pltpu.with_memory_space_constraint(z, pltpu.HBM)
    lands = [lax.empty(s.shape, s.dtype) for s in land_shapes]
    res = pl.pallas_call(
        body, name=name,
        out_shape=(pltpu.SemaphoreType.DMA((n,)), pltpu.SemaphoreType.DMA((n,)),
                   *[pltpu.HBM(z.shape, z.dtype) for z in srcs], *[pltpu.HBM(s.shape, s.dtype) for s in land_shapes],
                   jax.ShapeDtypeStruct((SUBLANES, LANES), F32)),
        in_specs=[_HBM] * (ns + nl),
        out_specs=(_SEM, _SEM, *[_HBM] * (ns + nl), pl.BlockSpec(memory_space=pltpu.VMEM)),
        input_output_aliases={i: 2 + i for i in range(ns + nl)},
        compiler_params=pltpu.CompilerParams(has_side_effects=_DATAFLOW),
    )(*[hbm(z) for z in srcs], *[hbm(z) for z in lands])
    return {"sems": res[:2], "srcs": list(res[2:2 + ns]), "lands": list(res[2 + ns:2 + ns + nl]), "token": res[-1][0, 0]}


def _send_wait(name, flips, started, moves, after):
    srcs, lands = started["srcs"], started["lands"]
    ns, nl = len(srcs), len(lands)

    def body(*refs):
        for cp in _split_copies(flips, moves, refs[:ns], refs[ns:ns + nl], refs[ns + nl], refs[ns + nl + 1]):
            cp.wait_send()
            cp.wait_recv()

    res = pl.pallas_call(
        body, name=name, out_shape=[pltpu.HBM(z.shape, z.dtype) for z in srcs + lands],
        in_specs=[_HBM] * (ns + nl) + [_SEM, _SEM, pl.BlockSpec(memory_space=pl.ANY)],
        out_specs=[_HBM] * (ns + nl), input_output_aliases={i: i for i in range(ns + nl)},
        compiler_params=pltpu.CompilerParams(has_side_effects=_DATAFLOW),
    )(*srcs, *lands, *started["sems"], after)
    return list(res[:ns]), list(res[ns:])


CHIP_FLIPS = ((1, 0, 0), (0, 1, 0), (1, 1, 0))
PAIR_FLIPS = ((0, 0, 1),)
ALL_FLIPS = CHIP_FLIPS + ((1, 0, 1), (0, 1, 1), (1, 1, 1)) + PAIR_FLIPS


def _gather_two_level(chip_arrs, dev_arrs, name):
    arrs = list(chip_arrs) + list(dev_arrs)
    n, nchip = len(arrs), len(chip_arrs)
    NS = 7

    def body(*refs):
        srcs, outs = refs[:n], refs[n:2 * n]
        send_sems, recv_sems, loc_sems = refs[2 * n:]
        x, y, c = _me()
        sib = (x, y, 1 - c)
        chips = [(1 - x, y), (x, 1 - y), (1 - x, 1 - y)]
        mine = 2 * x + y
        ids = [2 * cx + cy for cx, cy in chips]

        def part(i, slot, core):
            if i < nchip:
                return outs[i].at[slot, _half(arrs[i].shape[0], core)]
            return outs[i].at[slot, core]

        def rcopy(i, k, src, dst, to):
            return pltpu.make_async_remote_copy(src_ref=src, dst_ref=dst, send_sem=send_sems.at[i * NS + k],
                                                recv_sem=recv_sems.at[i * NS + k], device_id=to, device_id_type=MESH_IDS)

        started, locs = [], []
        for i in range(n):
            own = srcs[i].at[_half(arrs[i].shape[0], c)] if i < nchip else srcs[i]
            loc = pltpu.make_async_copy(srcs[i], outs[i].at[mine] if i < nchip else outs[i].at[mine, c], loc_sems.at[i])
            loc.start()
            locs.append(loc)
            for f, chip in enumerate(chips):
                cp = rcopy(i, f, own, part(i, mine, c), (*chip, c))
                cp.start()
                started.append(cp)
            if i >= nchip:
                cp = rcopy(i, 6, own, part(i, mine, c), sib)
                cp.start()
                started.append(cp)
        for i in range(n):
            for f in range(3):
                land = part(i, ids[f], c)
                rcopy(i, f, land, land, sib).wait_recv()
                fw = rcopy(i, 3 + f, land, land, sib)
                fw.start()
                started.append(fw)
        for i in range(n):
            for f in range(3):
                land = part(i, ids[f], 1 - c)
                rcopy(i, 3 + f, land, land, sib).wait_recv()
            if i >= nchip:
                land = part(i, mine, 1 - c)
                rcopy(i, 6, land, land, sib).wait_recv()
        for cp in started:
            cp.wait_send()
        for loc in locs:
            loc.wait()

    outs = [jax.ShapeDtypeStruct((4,) + a.shape, a.dtype) for a in chip_arrs]
    outs += [jax.ShapeDtypeStruct((4, 2) + a.shape, a.dtype) for a in dev_arrs]
    res = pl.pallas_call(
        body, name=name, out_shape=outs,
        in_specs=[pl.BlockSpec(memory_space=pl.ANY)] * n, out_specs=[pl.BlockSpec(memory_space=pl.ANY)] * n,
        scratch_shapes=[pltpu.SemaphoreType.DMA((n * NS,)), pltpu.SemaphoreType.DMA((n * NS,)),
                        pltpu.SemaphoreType.DMA((n,))],
    )(*arrs)
    return res[:nchip], res[nchip:]


def _mm(As, Bs, out_dtype, name, tm=512, cap=1408, bt=False):
    n = len(As)
    M, N = As[0].shape[0], Bs[0].shape[0 if bt else 1]
    if sum(a.shape[1] for a in As) <= 1024:
        tm = 2 * tm
    tm = min(tm, M)
    tn = _pick(N, cap)
    dims = (((1,), (1,)), ((), ())) if bt else (((1,), (0,)), ((), ()))

    def body(*refs):
        o = refs[2 * n]
        acc = None
        for a, b in zip(refs[:n], refs[n:2 * n]):
            d = lax.dot_general(a[...].astype(MXU_DTYPE), b[...].astype(MXU_DTYPE), dims, preferred_element_type=F32)
            acc = d if acc is None else acc + d
        o[...] = acc.astype(o.dtype)

    in_specs = [pl.BlockSpec((tm, a.shape[1]), lambda i, j: (i, 0)) for a in As]
    if bt:
        in_specs += [pl.BlockSpec((tn, b.shape[1]), lambda i, j: (j, 0)) for b in Bs]
    else:
        in_specs += [pl.BlockSpec((b.shape[0], tn), lambda i, j: (0, j)) for b in Bs]
    return pl.pallas_call(
        body, name=name, grid=(M // tm, N // tn), in_specs=in_specs,
        out_specs=pl.BlockSpec((tm, tn), lambda i, j: (i, j)),
        out_shape=jax.ShapeDtypeStruct((M, N), out_dtype),
        compiler_params=_params(("parallel", "parallel")),
    )(*As, *Bs)


def _mm_tn(A, G, name, tt=1024, cap=1408):
    T, Ka = A.shape
    N = G.shape[1]
    tt = min(tt, T)
    tk = _pick(Ka, cap)
    tn = _pick(N, cap)

    def body(a, g, o):
        @pl.when(pl.program_id(2) == 0)
        def _():
            o[...] = jnp.zeros(o.shape, F32)
        o[...] += lax.dot_general(a[...].astype(MXU_DTYPE), g[...].astype(MXU_DTYPE),
                                  (((0,), (0,)), ((), ())), preferred_element_type=F32)

    return pl.pallas_call(
        body, name=name, grid=(Ka // tk, N // tn, T // tt),
        in_specs=[pl.BlockSpec((tt, tk), lambda i, j, t: (t, i)), pl.BlockSpec((tt, tn), lambda i, j, t: (t, j))],
        out_specs=pl.BlockSpec((tk, tn), lambda i, j, t: (i, j)),
        out_shape=jax.ShapeDtypeStruct((Ka, N), F32),
        compiler_params=_params(("parallel", "parallel", "arbitrary")),
    )(A, G)


def _rowwise(name, fn, *, Bl, S, R, tiled=(), prev=(), nxt=(), batch=(), full=(),
             out_tiled=(), out_batch=(), out_acc=()):
    R = min(R, S)
    nS = S // R
    T = Bl * S
    hb = R // HALO
    n_in = len(tiled) + len(prev) + len(nxt) + len(batch) + len(full)

    in_specs, args = [], []
    for a, wd, cb in tiled:
        in_specs.append(pl.BlockSpec((R, wd), lambda b, i, cb=cb: (b * nS + i, cb)))
        args.append(a)
    for a, wd, cb in prev:
        in_specs.append(pl.BlockSpec((HALO, wd), lambda b, i, cb=cb: (jnp.maximum((b * nS + i) * hb - 1, 0), cb)))
        args.append(a)
    for a, wd, cb in nxt:
        in_specs.append(pl.BlockSpec((HALO, wd), lambda b, i, cb=cb: (jnp.minimum((b * nS + i + 1) * hb, T // HALO - 1), cb)))
        args.append(a)
    for a, wd, cb in batch:
        in_specs.append(pl.BlockSpec((1, 1, wd), lambda b, i, cb=cb: (b, 0, cb)))
        args.append(a)
    for a in full:
        in_specs.append(pl.BlockSpec(a.shape, lambda b, i, nd=a.ndim: (0,) * nd))
        args.append(a)

    out_specs, out_shape = [], []
    for C, dt in out_tiled:
        out_specs.append(pl.BlockSpec((R, C), lambda b, i: (b * nS + i, 0)))
        out_shape.append(jax.ShapeDtypeStruct((T, C), dt))
    for C in out_batch:
        out_specs.append(pl.BlockSpec((1, 1, C), lambda b, i: (b, 0, 0)))
        out_shape.append(jax.ShapeDtypeStruct((Bl, 1, C), F32))
    for shp in out_acc:
        out_specs.append(pl.BlockSpec(shp, lambda b, i, nd=len(shp): (0,) * nd))
        out_shape.append(jax.ShapeDtypeStruct(shp, F32))

    nt, npv, nnx, nbt = len(tiled), len(prev), len(nxt), len(batch)

    def body(*refs):
        b, i = pl.program_id(0), pl.program_id(1)
        ins, outs = refs[:n_in], refs[n_in:]
        vals = [r[...] for r in ins[:nt]]
        vals += [jnp.where(i > 0, r[...], jnp.zeros(r.shape, r.dtype)) for r in ins[nt:nt + npv]]
        vals += [jnp.where(i < nS - 1, r[...], jnp.zeros(r.shape, r.dtype)) for r in ins[nt + npv:nt + npv + nnx]]
        vals += [r[0] for r in ins[nt + npv + nnx:nt + npv + nnx + nbt]]
        vals += [r[...] for r in ins[nt + npv + nnx + nbt:]]
        res = fn(*vals)
        if not isinstance(res, (tuple, list)):
            res = (res,)
        k = 0
        for _ in out_tiled:
            outs[k][...] = res[k].astype(outs[k].dtype)
            k += 1
        for _ in out_batch:
            o = outs[k]

            @pl.when(i == 0)
            def _(o=o):
                o[...] = jnp.zeros(o.shape, F32)
            o[0] += res[k]
            k += 1
        for _ in out_acc:
            o = outs[k]

            @pl.when((i == 0) & (b == 0))
            def _(o=o):
                o[...] = jnp.zeros(o.shape, F32)
            o[...] += res[k]
            k += 1

    out = pl.pallas_call(
        body, name=name, grid=(Bl, nS), in_specs=in_specs, out_specs=out_specs, out_shape=out_shape,
        compiler_params=_params(("arbitrary", "arbitrary")),
    )(*args)
    return out


def _colwise(name, fn, *, Bl, S, R, W, strip, tiled=(), prev=(), nxt=(), full=(), out_tiled=(), n_acc=0):
    R = min(R, S)
    nS = S // R
    T = Bl * S
    hb = R // HALO
    nt, npv, nnx, nfl = len(tiled), len(prev), len(nxt), len(full)
    n_in = nt + npv + nnx + nfl
    in_specs = [pl.BlockSpec((R, W), lambda b, i, cb=cb: (b * nS + i, cb)) for _, cb in tiled]
    in_specs += [pl.BlockSpec((HALO, W), lambda b, i, cb=cb: (jnp.maximum((b * nS + i) * hb - 1, 0), cb)) for _, cb in prev]
    in_specs += [pl.BlockSpec((HALO, W), lambda b, i, cb=cb: (jnp.minimum((b * nS + i + 1) * hb, T // HALO - 1), cb))
                 for _, cb in nxt]
    in_specs += [pl.BlockSpec(a.shape, lambda b, i: (0, 0)) for a in full]
    out_specs = [pl.BlockSpec((R, m * W), lambda b, i: (b * nS + i, 0)) for m, _ in out_tiled]
    out_specs += [pl.BlockSpec((1, W), lambda b, i: (0, 0))] * n_acc
    out_shape = [jax.ShapeDtypeStruct((T, m * W), dt) for m, dt in out_tiled] + [jax.ShapeDtypeStruct((1, W), F32)] * n_acc

    def body(*refs):
        b, i = pl.program_id(0), pl.program_id(1)
        ins, outs = refs[:n_in], refs[n_in:]

        @pl.when((i == 0) & (b == 0))
        def _():
            for o in outs[len(out_tiled):]:
                o[...] = jnp.zeros(o.shape, F32)

        def col(j, carry):
            cs = pl.ds(pl.multiple_of(j * strip, strip), strip)
            vals = [r[:, cs] for r in ins[:nt]]
            vals += [jnp.where(i > 0, r[:, cs], jnp.zeros((HALO, strip), r.dtype)) for r in ins[nt:nt + npv]]
            vals += [jnp.where(i < nS - 1, r[:, cs], jnp.zeros((HALO, strip), r.dtype)) for r in ins[nt + npv:nt + npv + nnx]]
            vals += [r[:, cs] for r in ins[nt + npv + nnx:]]
            res = fn(*vals)
            for k, (m, _) in enumerate(out_tiled):
                for q in range(m):
                    outs[k][:, pl.ds(pl.multiple_of(q * W + j * strip, strip), strip)] = res[k][q].astype(outs[k].dtype)
            for k in range(len(out_tiled), len(outs)):
                outs[k][:, cs] += res[k]
            return carry

        lax.fori_loop(0, W // strip, col, 0)

    return pl.pallas_call(
        body, name=name, grid=(Bl, nS), in_specs=in_specs, out_specs=out_specs, out_shape=out_shape,
        compiler_params=_params(("arbitrary", "arbitrary")),
    )(*[a for a, _ in tiled], *[a for a, _ in prev], *[a for a, _ in nxt], *full)


def _shift_down(x, halo, k):
    rolled = pltpu.roll(x, k, 0)
    row = lax.broadcasted_iota(jnp.int32, (SUBLANES, x.shape[1]), 0)
    head = rolled[0:SUBLANES]
    for j in range(k):
        head = jnp.where(row == j, halo[HALO - k + j:HALO - k + j + 1, :], head)
    return jnp.concatenate([head, rolled[SUBLANES:]], axis=0)


def _shift_up(x, halo, k):
    n = x.shape[0]
    rolled = pltpu.roll(x, n - k, 0)
    row = lax.broadcasted_iota(jnp.int32, (SUBLANES, x.shape[1]), 0)
    tail = rolled[n - SUBLANES:]
    for j in range(k):
        tail = jnp.where(row == SUBLANES - k + j, halo[j:j + 1, :], tail)
    return jnp.concatenate([rolled[:n - SUBLANES], tail], axis=0)


def _dotm(a, b):
    return jnp.dot(a.astype(MXU_DTYPE), b.astype(MXU_DTYPE), preferred_element_type=F32)


def _split_bf16(x):
    hi = x.astype(BF16)
    return hi, (x - hi.astype(F32)).astype(BF16)


def _headsum_2pass(x, hm):
    hi, lo = _split_bf16(x)
    hb = hm.astype(BF16)
    return jnp.dot(hi, hb, preferred_element_type=F32) + jnp.dot(lo, hb, preferred_element_type=F32)


@jax.custom_vjp
def _headsum(x, hm):
    return _headsum_2pass(x, hm)


_headsum.defvjp(lambda x, hm: (_headsum_2pass(x, hm), hm),
                lambda hm, g: (_headsum_2pass(g, hm), jnp.zeros_like(hm)))


def _sigmoid(x):
    return 0.5 * jnp.tanh(0.5 * x) + 0.5


def _rms(x, g):
    return x * lax.rsqrt(jnp.mean(x * x, axis=-1, keepdims=True) + RMS_EPS) * g


def _norm_mod(x, g, sc, sh):
    return _rms(x, g) * (1.0 + sc) + sh


def _split_ps(ps):
    return (ps[:, 0:RW], ps[:, RW:2 * RW], ps[:, 2 * RW:3 * RW], ps[:, 3 * RW:3 * RW + LW + LA],
            ps[:, 3 * RW + LW + LA:SHIFT])


def _rwkv_prep(r, k, v, wa, gd, w0, w_up_p, a0, a_up_p, g_up, k_k, k_a, hm):
    w_raw = w0 + _dotm(jnp.tanh(wa), w_up_p)
    decay = jnp.exp(-DECAY_SCALE * _sigmoid(w_raw))
    a = _sigmoid(a0 + _dotm(wa, a_up_p))
    g = _dotm(_sigmoid(gd), g_up)
    kk = k * k_k
    kk = kk * lax.rsqrt(_headsum(kk * kk, hm) + L2_EPS)
    k2 = k * (1.0 + (a - 1.0) * k_a)
    return r, decay, k2, v, -kk, kk * a, g


def _rwkv_post(y, r, k2, v, g, ln_g, ln_b, r_k, hm):
    mean = _headsum(y, hm) * (1.0 / HD)
    yc = y - mean
    var = _headsum(yc * yc, hm) * (1.0 / HD)
    yn = yc * lax.rsqrt(var + GN_EPS) * ln_g + ln_b
    bonus = _headsum(r * k2 * r_k, hm) * v
    return (yn + bonus) * g


def _gelu(x):
    return 0.5 * x * (1.0 + jnp.tanh(GELU_C * (x + 0.044715 * (x * x * x))))


def _s5_post(yssm, u, d):
    return _gelu(yssm + d * u)


def _mix(ga, gb, ya, za, zb):
    return _sigmoid(ga) * ya + _sigmoid(gb) * (za * _sigmoid(zb))


def _conv_act(up_g, up_u, hg, hu, w_g, w_u, b_g, b_u):
    gate, upv = _conv3(up_g, hg, w_g, b_g)[0], _conv3(up_u, hu, w_u, b_u)[0]
    return gate, upv


def _conv3(x, h, w, b):
    x, h = x.astype(F32), h.astype(F32)
    s2, s1 = _shift_down(x, h, 2), _shift_down(x, h, 1)
    return b + w[0:1] * s2 + w[1:2] * s1 + w[2:3] * x, (s2, s1, x)


def _silu_gate(gate, upv):
    return gate * _sigmoid(gate) * upv


WKV_L = 64
_NT, _NN, _TN = ((1,), (1,)), ((1,), (0,)), ((0,), (0,))


def _dotw(x, y, dims):
    return lax.dot_general(x.astype(MXU_DTYPE), y.astype(MXU_DTYPE), (dims, ((), ())), preferred_element_type=F32)


def _dot3(x, y, dims):
    (xh, xl), (yh, yl) = _split_bf16(x), _split_bf16(y)
    d = lambda p, q: lax.dot_general(p, q, (dims, ((), ())), preferred_element_type=F32)
    return d(xh, yh) + d(xh, yl) + d(xl, yh)


@jax.custom_vjp
def _gram3(x, y):
    return _dot3(x, y, _NT)


_gram3.defvjp(lambda x, y: (_dot3(x, y, _NT), (x, y)),
              lambda res, g: (_dot3(g, res[1], _NN), _dot3(g, res[0], _TN)))


@jax.custom_vjp
def _gram_known(x, y, value):
    return value


_gram_known.defvjp(lambda x, y, value: (value, (x, y, value)),
                   lambda res, g: (_dot3(g, res[1], _NN), _dot3(g, res[0], _TN), jnp.zeros_like(res[2])))


def _tri_solve_fwd(ns, xs):
    each = lambda f, *ls: tuple(f(*zs) for zs in zip(*ls))
    size = ns[0].shape[0]
    eye = (lax.broadcasted_iota(jnp.int32, (size, size), 0) == lax.broadcasted_iota(jnp.int32, (size, size), 1)).astype(F32)
    ts = each(lambda n: n + eye, ns)
    qs = ns
    for _ in range(WKV_L.bit_length() - 2):
        qs = each(lambda q: _dotw(q, q, _NN), qs)
        ts = each(lambda t, q: t + _dotw(t, q, _NN), ts, qs)
    us = each(lambda t, x: _dotw(t, x, _NN), ts, xs)
    return us, (ts, us)


def _tri_solve_bwd(res, dus):
    ts, us = res
    each = lambda f, *ls: tuple(f(*zs) for zs in zip(*ls))
    dxs = each(lambda t, du: _dotw(t, du, _TN), ts, dus)
    return each(lambda dx, u: _dotw(dx, u, _NT), dxs, us), dxs


@jax.custom_vjp
def _tri_solve(ns, xs):
    return _tri_solve_fwd(ns, xs)[0]


_tri_solve.defvjp(_tri_solve_fwd, _tri_solve_bwd)


@jax.custom_vjp
def _tri_apply(ns, xs, ts):
    return tuple(_dotw(t, x, _NN) for t, x in zip(ts, xs))


_tri_apply.defvjp(lambda ns, xs, ts: (lambda us: (us, (ts, us)))(tuple(_dotw(t, x, _NN) for t, x in zip(ts, xs))),
                  lambda res, dus: _tri_solve_bwd(res, dus) + (tuple(jnp.zeros_like(t) for t in res[0]),))


def _wkv_chunk(s0, r, w, k, v, a, b):
    y, s1 = _wkv_chunks((s0,), (r,), (w,), (k,), (v,), (a,), (b,))
    return y[0], s1[0]


def _wkv_chunks(s0, r, w, k, v, a, b, tinv=None, want_tinv=False):
    each = lambda f, *ls: tuple(f(*xs) for xs in zip(*ls))
    L = r[0].shape[0]
    n2 = 2 * L
    lane_head = lax.broadcasted_iota(jnp.int32, (2, 1, 2 * HD), 2) // HD
    head_mask = (lane_head == lax.broadcasted_iota(jnp.int32, (2, 1, 2 * HD), 0)).astype(F32)
    ri = lax.broadcasted_iota(jnp.int32, (n2, n2), 0)
    ci = lax.broadcasted_iota(jnp.int32, (n2, n2), 1)
    same = (ri // L) == (ci // L)
    strict = same & ((ci % L) < (ri % L))
    incl = same & ((ci % L) <= (ri % L))
    si = lax.broadcasted_iota(jnp.int32, (2 * HD, 2 * HD), 0) // HD
    sj = lax.broadcasted_iota(jnp.int32, (2 * HD, 2 * HD), 1) // HD
    tri = (lax.broadcasted_iota(jnp.int32, (L, L), 0) >= lax.broadcasted_iota(jnp.int32, (L, L), 1)).astype(F32)

    stack = lambda z: (z[None] * head_mask).reshape(n2, 2 * HD)
    dup = lambda z: jnp.broadcast_to(z[None], (2, L, 2 * HD)).reshape(n2, 2 * HD)
    gram = _gram3
    nt, nn, tn = (lambda x, y, d=d: _dotw(x, y, d) for d in (_NT, _NN, _TN))
    add = lambda x, y: x + y

    lw = each(jnp.log, w)
    cum = each(lambda z: jnp.dot(tri, z, preferred_element_type=F32, precision=HIGHEST), lw)
    tot = each(lambda z: jnp.sum(z, axis=0, keepdims=True), lw)
    a2 = each(lambda av, cv, lv: stack(av * jnp.exp(cv - lv)), a, cum, lw)
    r2 = each(lambda rv, cv: stack(rv * jnp.exp(cv)), r, cum)
    v2 = each(stack, v)
    b2 = each(lambda bv, cv: dup(bv * jnp.exp(-cv)), b, cum)
    k2 = each(lambda kv, cv: dup(kv * jnp.exp(-cv)), k, cum)
    n_ab = each(lambda x, y: jnp.where(strict, gram(x, y), 0.0), a2, b2)
    if tinv is None:
        n_ak = each(lambda x, y: jnp.where(strict, gram(x, y), 0.0), a2, k2)
        m_rb = each(lambda x, y: jnp.where(incl, gram(x, y), 0.0), r2, b2)
        m_rk = each(lambda x, y: jnp.where(incl, gram(x, y), 0.0), r2, k2)
    else:
        tinv, k_ak, k_rb, k_rk = tinv
        n_ak = each(lambda x, y, g: jnp.where(strict, _gram_known(x, y, g), 0.0), a2, k2, k_ak)
        m_rb = each(lambda x, y, g: jnp.where(incl, _gram_known(x, y, g), 0.0), r2, b2, k_rb)
        m_rk = each(lambda x, y, g: jnp.where(incl, _gram_known(x, y, g), 0.0), r2, k2, k_rk)
    x = each(add, each(nt, a2, s0), each(nn, n_ak, v2))
    if want_tinv:
        u, (tinv, _) = _tri_solve_fwd(n_ab, x)
        tinv = (tinv, n_ak, m_rb, m_rk)
    else:
        u = _tri_solve(n_ab, x) if tinv is None else _tri_apply(n_ab, x, tinv)
    y2 = each(lambda x, y, z: x + y + z, each(nt, r2, s0), each(nn, m_rb, u), each(nn, m_rk, v2))
    y = each(lambda z: jnp.sum(z.reshape(2, L, 2 * HD), axis=0), y2)
    b3 = each(lambda bv, tv, cv: dup(bv * jnp.exp(tv - cv)), b, tot, cum)
    k3 = each(lambda kv, tv, cv: dup(kv * jnp.exp(tv - cv)), k, tot, cum)
    upd = each(add, each(tn, u, b3), each(tn, v2, k3))
    s1 = each(lambda sv, tv, uv: sv * jnp.exp(tv) + jnp.where(si == sj, uv, 0.0), s0, tot, upd)
    return (y, s1, tinv) if want_tinv else (y, s1)


NPAIR = NH // 2


def _wkv_nb(Bl):
    return 4 if Bl % 4 == 0 else 2 if Bl % 2 == 0 else 1


def _wkv_fwd(r, w, k, v, a, b, Bl, S):
    L = WKV_L
    nC = S // L
    nb = _wkv_nb(Bl)
    chains = [(bi, p, slice(p * 2 * HD, (p + 1) * 2 * HD)) for bi in range(nb) for p in range(NPAIR)]

    def body(r_ref, w_ref, k_ref, v_ref, a_ref, b_ref, y_ref, ck_ref, ti_ref, s_ref):
        @pl.when(pl.program_id(1) == 0)
        def _():
            s_ref[...] = jnp.zeros(s_ref.shape, F32)
        s0 = tuple(s_ref[bi, p] for bi, p, _ in chains)
        ops = [tuple(z[bi, :, cs] for bi, _, cs in chains) for z in (r_ref, w_ref, k_ref, v_ref, a_ref, b_ref)]
        y, s1, kept = _wkv_chunks(s0, *ops, want_tinv=True)
        for i, (bi, p, cs) in enumerate(chains):
            ck_ref[bi, 0, p] = s0[i]
            for q in range(4):
                ti_ref[bi, 0, p, q] = kept[q][i]
            y_ref[bi, :, cs] = y[i]
            s_ref[bi, p] = s1[i]

    to3 = lambda z: z.reshape(Bl, S, RW)
    row_spec = pl.BlockSpec((nb, L, RW), lambda g, c: (g, c, 0))
    mats = jax.ShapeDtypeStruct((Bl, nC, NPAIR, 2 * HD, 2 * HD), F32)
    mat_spec = pl.BlockSpec((nb, 1, NPAIR, 2 * HD, 2 * HD), lambda g, c: (g, c, 0, 0, 0))
    y, ck, ti = pl.pallas_call(
        body, name="wkv_fwd", grid=(Bl // nb, nC), in_specs=[row_spec] * 6,
        out_specs=[row_spec, mat_spec, pl.BlockSpec((nb, 1, NPAIR, 4, 2 * HD, 2 * HD), lambda g, c: (g, c, 0, 0, 0, 0))],
        out_shape=[jax.ShapeDtypeStruct((Bl, S, RW), F32), mats,
                   jax.ShapeDtypeStruct((Bl, nC, NPAIR, 4, 2 * HD, 2 * HD), F32)],
        scratch_shapes=[pltpu.VMEM((nb, NPAIR, 2 * HD, 2 * HD), F32)],
        compiler_params=_params(("arbitrary", "arbitrary")),
    )(*(to3(z) for z in (r, w, k, v, a, b)))
    return y.reshape(Bl * S, RW), ck, ti


def _wkv_bwd(r, w, k, v, a, b, dy, ck, ti, Bl, S):
    L = WKV_L
    nC = S // L
    nb = _wkv_nb(Bl)
    chains = [(bi, p, slice(p * 2 * HD, (p + 1) * 2 * HD)) for bi in range(nb) for p in range(NPAIR)]

    def body(r_ref, w_ref, k_ref, v_ref, a_ref, b_ref, dy_ref, ck_ref, ti_ref,
             dr_ref, dw_ref, dk_ref, dv_ref, da_ref, db_ref, ds_ref):
        @pl.when(pl.program_id(1) == 0)
        def _():
            ds_ref[...] = jnp.zeros(ds_ref.shape, F32)
        s0 = tuple(ck_ref[bi, 0, p] for bi, p, _ in chains)
        tinv = tuple(tuple(ti_ref[bi, 0, p, q] for bi, p, _ in chains) for q in range(4))
        ops = [tuple(z[bi, :, cs] for bi, _, cs in chains) for z in (r_ref, w_ref, k_ref, v_ref, a_ref, b_ref)]
        cts = (tuple(dy_ref[bi, :, cs] for bi, _, cs in chains), tuple(ds_ref[bi, p] for bi, p, _ in chains))
        ds0, *grads = jax.vjp(lambda *z: _wkv_chunks(*z, tinv=tinv), s0, *ops)[1](cts)
        for i, (bi, p, cs) in enumerate(chains):
            ds_ref[bi, p] = ds0[i]
            for o, g in zip((dr_ref, dw_ref, dk_ref, dv_ref, da_ref, db_ref), grads):
                o[bi, :, cs] = g[i]

    to3 = lambda z: z.reshape(Bl, S, RW)
    row_spec = pl.BlockSpec((nb, L, RW), lambda g, c: (g, nC - 1 - c, 0))
    rows = jax.ShapeDtypeStruct((Bl, S, RW), F32)
    mat_spec = pl.BlockSpec((nb, 1, NPAIR, 2 * HD, 2 * HD), lambda g, c: (g, nC - 1 - c, 0, 0, 0))
    outs = pl.pallas_call(
        body, name="wkv_bwd", grid=(Bl // nb, nC),
        in_specs=[row_spec] * 7 + [mat_spec, pl.BlockSpec((nb, 1, NPAIR, 4, 2 * HD, 2 * HD),
                                                          lambda g, c: (g, nC - 1 - c, 0, 0, 0, 0))],
        out_specs=[row_spec] * 6, out_shape=[rows] * 6,
        scratch_shapes=[pltpu.VMEM((nb, NPAIR, 2 * HD, 2 * HD), F32)],
        compiler_params=_params(("arbitrary", "arbitrary")),
    )(*(to3(z) for z in (r, w, k, v, a, b, dy)), ck, ti)
    return [o.reshape(Bl * S, RW) for o in outs]


NST = NG * SP


def _cmul(ar, ai, br, bi):
    return ar * br - ai * bi, ar * bi + ai * br


def _s5_tiles(are, aim, reverse):
    if reverse:
        aim = -aim
    row = lax.broadcasted_iota(jnp.int32, (SUBLANES, NST), 0)
    pw = [(are, aim)]
    for _ in range(SUBLANES - 1):
        pw.append(_cmul(pw[-1][0], pw[-1][1], are, aim))
    bc = lambda z: jnp.broadcast_to(z, (SUBLANES, NST))
    ms = []
    for kk in (1, 2, 4):
        cond = (row < SUBLANES - kk) if reverse else (row >= kk)
        ms.append((jnp.where(cond, bc(pw[kk - 1][0]), 0.0), jnp.where(cond, bc(pw[kk - 1][1]), 0.0)))
    pr = jnp.zeros((SUBLANES, NST), F32)
    pi = jnp.zeros((SUBLANES, NST), F32)
    for i in range(SUBLANES):
        n = SUBLANES - i if reverse else i + 1
        pr = jnp.where(row == i, bc(pw[n - 1][0]), pr)
        pi = jnp.where(row == i, bc(pw[n - 1][1]), pi)
    return ms, (pr, pi)


def _s5_block(re, im, ms, pc, cre, cim, sg, reverse):
    ln = slice(sg * 512, (sg + 1) * 512)
    for (mr, mi), kk in zip(ms, (1, 2, 4)):
        sh = SUBLANES - kk if reverse else kk
        sre, sim = pltpu.roll(re, sh, 0), pltpu.roll(im, sh, 0)
        tr, ti = _cmul(mr[:, ln], mi[:, ln], sre, sim)
        re, im = re + tr, im + ti
    tr, ti = _cmul(pc[0][:, ln], pc[1][:, ln], cre[:, ln], cim[:, ln])
    return re + tr, im + ti


def _s5_scan(X_ref, n_rows, ms, pc, cre, cim, reverse, visit=None, acc0=None):
    nblk = n_rows // SUBLANES

    def it(i, carry):
        cre, cim, acc = carry
        j = nblk - 1 - i if reverse else i
        rows = pl.ds(pl.multiple_of(j * SUBLANES, SUBLANES), SUBLANES)
        edge = 0 if reverse else SUBLANES - 1
        blocks, ncre, ncim = [], [], []
        for sg in range(NSG):
            lr = slice(sg * 1024, sg * 1024 + 512)
            li = slice(sg * 1024 + 512, (sg + 1) * 1024)
            re, im = _s5_block(X_ref[rows, lr], X_ref[rows, li], ms, pc, cre, cim, sg, reverse)
            X_ref[rows, lr] = re
            X_ref[rows, li] = im
            blocks.append((re, im))
            ncre.append(re[edge:edge + 1])
            ncim.append(im[edge:edge + 1])
        if visit is not None:
            acc = visit(j, blocks, acc)
        return jnp.concatenate(ncre, axis=1), jnp.concatenate(ncim, axis=1), acc

    return lax.fori_loop(0, nblk, it, (cre, cim, acc0 if acc0 is not None else 0))


def _s5_fwd(u, wb, wc, ab, d, Bl, S, R=256):
    R = min(R, S)
    nC = S // R

    def body(u_ref, wb_ref, wc_ref, ab_ref, d_ref, y_ref, st_ref, X_ref, o_ref, car_ref):
        @pl.when(pl.program_id(1) == 0)
        def _():
            car_ref[...] = jnp.zeros(car_ref.shape, F32)
        st_ref[0, 0] = car_ref[...]
        ms, pc = _s5_tiles(ab_ref[0:1], ab_ref[1:2], False)
        for sg in range(NSG):
            X_ref[:, sg * 1024:(sg + 1) * 1024] = _dotm(u_ref[:, sg * 128:(sg + 1) * 128], wb_ref[sg])
        cre, cim, _ = _s5_scan(X_ref, R, ms, pc, car_ref[0:1], car_ref[1:2], False)
        car_ref[0:1] = cre
        car_ref[1:2] = cim
        for sg in range(NSG):
            y_ref[:, sg * 128:(sg + 1) * 128] = _dotm(X_ref[:, sg * 1024:(sg + 1) * 1024], wc_ref[sg])
        o_ref[...] = _s5_post(y_ref[...], u_ref[...], d_ref[...]).astype(o_ref.dtype)

    rows = pl.BlockSpec((R, SW), lambda b, c: (b * nC + c, 0))
    return pl.pallas_call(
        body, name="s5_fwd", grid=(Bl, nC),
        in_specs=[rows, pl.BlockSpec(wb.shape, lambda b, c: (0, 0, 0)), pl.BlockSpec(wc.shape, lambda b, c: (0, 0, 0)),
                  pl.BlockSpec(ab.shape, lambda b, c: (0, 0)), pl.BlockSpec(d.shape, lambda b, c: (0, 0))],
        out_specs=[rows, pl.BlockSpec((1, 1, 2, NST), lambda b, c: (b, c, 0, 0)),
                   pl.BlockSpec((R, 2 * NST), lambda b, c: (b * nC + c, 0)), rows],
        out_shape=[jax.ShapeDtypeStruct((Bl * S, SW), F32), jax.ShapeDtypeStruct((Bl, nC, 2, NST), F32),
                   jax.ShapeDtypeStruct((Bl * S, 2 * NST), F32), jax.ShapeDtypeStruct((Bl * S, SW), MXU_DTYPE)],
        scratch_shapes=[pltpu.VMEM((2, NST), F32)],
        compiler_params=_params(("arbitrary", "arbitrary")),
    )(u, wb, wc, ab, d)


def _s5_bwd(u, y, do, d, wb, wc, ab, st, xs, Bl, S, R=256):
    R = min(R, S)
    nC = S // R

    def body(u_ref, y_ref, do_ref, d_ref, wb_ref, wc_ref, ab_ref, st_ref, X_ref,
             du_ref, dwb_ref, dwc_ref, dab_ref, dd_ref, G_ref, car_ref):
        first = (pl.program_id(0) == 0) & (pl.program_id(1) == 0)

        @pl.when(first)
        def _():
            for o in (dwb_ref, dwc_ref, dab_ref, dd_ref):
                o[...] = jnp.zeros(o.shape, F32)

        @pl.when(pl.program_id(1) == 0)
        def _():
            car_ref[...] = jnp.zeros(car_ref.shape, F32)

        are, aim = ab_ref[0:1], ab_ref[1:2]
        dy, du_direct, dd = jax.vjp(_s5_post, y_ref[...], u_ref[...], d_ref[...])[1](do_ref[...])
        dd_ref[...] += dd
        dyv = dy.astype(MXU_DTYPE)
        for sg in range(NSG):
            G_ref[:, sg * 1024:(sg + 1) * 1024] = lax.dot_general(
                dyv[:, sg * 128:(sg + 1) * 128], wc_ref[sg].astype(MXU_DTYPE), (((1,), (1,)), ((), ())),
                preferred_element_type=F32)
        rms_, rpc = _s5_tiles(are, aim, True)
        row = lax.broadcasted_iota(jnp.int32, (SUBLANES, 512), 0)

        def visit(j, blocks, acc):
            before = pl.multiple_of(jnp.maximum(j - 1, 0) * SUBLANES, SUBLANES)
            prow = X_ref[pl.ds(before, SUBLANES), :][SUBLANES - 1:SUBLANES]
            rows = pl.ds(pl.multiple_of(j * SUBLANES, SUBLANES), SUBLANES)
            are_acc, aim_acc = [], []
            for sg in range(NSG):
                lr = slice(sg * 1024, sg * 1024 + 512)
                li = slice(sg * 1024 + 512, (sg + 1) * 1024)
                ln = slice(sg * 512, (sg + 1) * 512)
                pre = jnp.where(j > 0, prow[:, lr], st_ref[0, 0, 0:1, ln])
                pim = jnp.where(j > 0, prow[:, li], st_ref[0, 0, 1:2, ln])
                xre = jnp.where(row == 0, pre, pltpu.roll(X_ref[rows, lr], 1, 0))
                xim = jnp.where(row == 0, pim, pltpu.roll(X_ref[rows, li], 1, 0))
                dre, dim = blocks[sg]
                are_acc.append(dre * xre + dim * xim)
                aim_acc.append(dim * xre - dre * xim)
            return acc[0] + jnp.concatenate(are_acc, axis=1), acc[1] + jnp.concatenate(aim_acc, axis=1)

        zero = jnp.zeros((SUBLANES, NST), F32)
        cre, cim, acc = _s5_scan(G_ref, R, rms_, rpc, car_ref[0:1], car_ref[1:2], True, visit, (zero, zero))
        car_ref[0:1] = cre
        car_ref[1:2] = cim
        dab_ref[0:1] += jnp.sum(acc[0], axis=0, keepdims=True)
        dab_ref[1:2] += jnp.sum(acc[1], axis=0, keepdims=True)
        uv = u_ref[...].astype(MXU_DTYPE)
        for sg in range(NSG):
            cs = slice(sg * 1024, (sg + 1) * 1024)
            us = slice(sg * 128, (sg + 1) * 128)
            gx = G_ref[:, cs].astype(MXU_DTYPE)
            dwb_ref[sg] += lax.dot_general(uv[:, us], gx, (((0,), (0,)), ((), ())), preferred_element_type=F32)
            dwc_ref[sg] += lax.dot_general(X_ref[:, cs].astype(MXU_DTYPE), dyv[:, us], (((0,), (0,)), ((), ())),
                                           preferred_element_type=F32)
            du_ssm = lax.dot_general(gx, wb_ref[sg].astype(MXU_DTYPE), (((1,), (1,)), ((), ())),
                                     preferred_element_type=F32)
            du_ref[:, us] = (du_ssm + du_direct[:, us]).astype(du_ref.dtype)

    rmap = lambda b, c: (b * nC + nC - 1 - c, 0)
    rows = pl.BlockSpec((R, SW), rmap)
    return pl.pallas_call(
        body, name="s5_bwd", grid=(Bl, nC),
        in_specs=[rows, rows, rows, pl.BlockSpec(d.shape, lambda b, c: (0, 0)),
                  pl.BlockSpec(wb.shape, lambda b, c: (0, 0, 0)), pl.BlockSpec(wc.shape, lambda b, c: (0, 0, 0)),
                  pl.BlockSpec(ab.shape, lambda b, c: (0, 0)),
                  pl.BlockSpec((1, 1, 2, NST), lambda b, c: (b, nC - 1 - c, 0, 0)),
                  pl.BlockSpec((R, 2 * NST), rmap)],
        out_specs=[rows, pl.BlockSpec(wb.shape, lambda b, c: (0, 0, 0)),
                   pl.BlockSpec(wc.shape, lambda b, c: (0, 0, 0)), pl.BlockSpec((2, NST), lambda b, c: (0, 0)),
                   pl.BlockSpec(d.shape, lambda b, c: (0, 0))],
        out_shape=[jax.ShapeDtypeStruct((Bl * S, SW), MXU_DTYPE), jax.ShapeDtypeStruct(wb.shape, F32),
                   jax.ShapeDtypeStruct(wc.shape, F32), jax.ShapeDtypeStruct((2, NST), F32),
                   jax.ShapeDtypeStruct(d.shape, F32)],
        scratch_shapes=[pltpu.VMEM((R, 2 * NST), F32), pltpu.VMEM((2, NST), F32)],
        compiler_params=_params(("arbitrary", "arbitrary")),
    )(u, y, do, d, wb, wc, ab, st, xs)


def _s5_disc_math(a_re, a_im, log_dt, b_re, b_im, expand):
    dt = jnp.exp(log_dt)
    z_re, z_im = a_re * dt, a_im * dt
    mag = jnp.exp(z_re)
    ab_re, ab_im = mag * jnp.cos(z_im), mag * jnp.sin(z_im)
    den = a_re * a_re + a_im * a_im
    q_re = ((ab_re - 1.0) * a_re + ab_im * a_im) / den
    q_im = (ab_im * a_re - (ab_re - 1.0) * a_im) / den
    qe_re = jnp.dot(q_re, expand, preferred_element_type=F32, precision=HIGHEST)
    qe_im = jnp.dot(q_im, expand, preferred_element_type=F32, precision=HIGHEST)
    return ab_re, ab_im, qe_re * b_re - qe_im * b_im, qe_re * b_im + qe_im * b_re


def _whole(shape):
    return pl.BlockSpec(shape, lambda nd=len(shape): (0,) * nd)


def _s5_disc(a_re, a_im, log_dt, b_re, b_im, expand):
    def body(a, b, c, d, e, f, o0, o1, o2, o3):
        res = _s5_disc_math(a[...], b[...], c[...], d[...], e[...], f[...])
        for o, v in zip((o0, o1, o2, o3), res):
            o[...] = v
    ins = (a_re, a_im, log_dt, b_re, b_im, expand)
    outs = [jax.ShapeDtypeStruct(a_re.shape, F32)] * 2 + [jax.ShapeDtypeStruct(b_re.shape, F32)] * 2
    return pl.pallas_call(body, name="s5_disc", in_specs=[_whole(x.shape) for x in ins],
                          out_specs=[_whole(o.shape) for o in outs], out_shape=outs)(*ins)


def _s5_disc_bwd(a_re, a_im, log_dt, b_re, b_im, expand, cts):
    def body(a, b, c, d, e, f, g0, g1, g2, g3, o0, o1, o2, o3, o4):
        fn = lambda *p: _s5_disc_math(*p, f[...])
        _, vjp = jax.vjp(fn, a[...], b[...], c[...], d[...], e[...])
        for o, v in zip((o0, o1, o2, o3, o4), vjp((g0[...], g1[...], g2[...], g3[...]))):
            o[...] = v
    ins = (a_re, a_im, log_dt, b_re, b_im, expand) + tuple(cts)
    outs = [jax.ShapeDtypeStruct(x.shape, F32) for x in (a_re, a_im, log_dt, b_re, b_im)]
    return pl.pallas_call(body, name="s5_disc_bwd", in_specs=[_whole(x.shape) for x in ins],
                          out_specs=[_whole(o.shape) for o in outs], out_shape=outs)(*ins)


def _ada_fwd(c_all, w_shard, b_shard):
    def body(c_ref, w_ref, b_ref, o_ref):
        cv = c_ref[...]
        o_ref[...] = _dotm(cv * _sigmoid(cv), w_ref[...]) + b_ref[...]
    n = w_shard.shape[1]
    return pl.pallas_call(
        body, name="ada_fwd", in_specs=[_whole(c_all.shape), _whole(w_shard.shape), _whole(b_shard.shape)],
        out_specs=_whole((c_all.shape[0], n)), out_shape=jax.ShapeDtypeStruct((c_all.shape[0], n), F32),
        compiler_params=_params(),
    )(c_all, w_shard, b_shard)


def _ada_bwd(c_all, dmod_cols, dmod_all):
    def body(c_ref, dc_ref, da_ref, gw_ref, gb_ref):
        cv = c_ref[...]
        gw_ref[...] = lax.dot_general((cv * _sigmoid(cv)).astype(MXU_DTYPE), dc_ref[...].astype(MXU_DTYPE),
                                      (((0,), (0,)), ((), ())), preferred_element_type=F32)
        gb_ref[...] = jnp.sum(da_ref[...], axis=0, keepdims=True)
    n = dmod_cols.shape[1]
    return pl.pallas_call(
        body, name="ada_bwd", in_specs=[_whole(c_all.shape), _whole(dmod_cols.shape), _whole(dmod_all.shape)],
        out_specs=[_whole((D, n)), _whole((1, dmod_all.shape[1]))],
        out_shape=[jax.ShapeDtypeStruct((D, n), F32), jax.ShapeDtypeStruct((1, dmod_all.shape[1]), F32)],
        compiler_params=_params(),
    )(c_all, dmod_cols, dmod_all)


def _rows_block(n_rows, cap=512):
    if n_rows <= cap:
        return n_rows
    for t in range(cap - cap % SUBLANES, 0, -SUBLANES):
        if n_rows % t == 0:
            return t
    return n_rows


def _adamw(w, g, m, v, name):
    rows, cols = w.shape
    tr = _rows_block(rows, max(SUBLANES, (1 << 19) // max(cols, 1) // SUBLANES * SUBLANES))

    def body(w_ref, g_ref, m_ref, v_ref, d_ref, nm_ref, nv_ref):
        gv = g_ref[...]
        nm = B1 * m_ref[...] + (1.0 - B1) * gv
        nv = B2 * v_ref[...] + (1.0 - B2) * (gv * gv)
        m_hat = nm / (1.0 - B1 ** STEP)
        v_hat = nv / (1.0 - B2 ** STEP)
        d_ref[...] = -LR * (m_hat / (jnp.sqrt(v_hat) + ADAM_EPS) + WD * w_ref[...])
        nm_ref[...] = nm
        nv_ref[...] = nv

    spec = pl.BlockSpec((tr, cols), lambda i: (i, 0))
    sd = jax.ShapeDtypeStruct((rows, cols), F32)
    return pl.pallas_call(body, name=name, grid=(rows // tr,), in_specs=[spec] * 4, out_specs=[spec] * 3,
                          out_shape=[sd] * 3, compiler_params=_params(("parallel",)))(w, g, m, v)


def _sum_slots(x, out_dtype, name):
    xs = x if isinstance(x, (list, tuple)) else [x]
    _, rows, cols = xs[0].shape
    tr = _rows_block(rows)

    def body(*refs):
        acc = None
        for x_ref in refs[:-1]:
            for j in range(x_ref.shape[0]):
                term = x_ref[j].astype(F32)
                acc = term if acc is None else acc + term
        refs[-1][...] = acc.astype(refs[-1].dtype)

    return pl.pallas_call(
        body, name=name, grid=(rows // tr,),
        in_specs=[pl.BlockSpec((z.shape[0], tr, cols), lambda i: (0, i, 0)) for z in xs],
        out_specs=pl.BlockSpec((tr, cols), lambda i: (i, 0)), out_shape=jax.ShapeDtypeStruct((rows, cols), out_dtype),
        compiler_params=_params(("parallel",)))(*xs)


PACK_COLS = 1024


def _pack_rows(parts, dtype, row_mult):
    flat = jnp.concatenate([p.reshape(-1).astype(dtype) for p in parts])
    per = PACK_COLS * row_mult
    n = -(-flat.shape[0] // per) * per
    flat = jnp.pad(flat, (0, n - flat.shape[0]))
    return flat.reshape(n // PACK_COLS, PACK_COLS)


def _unpack(flat, shapes):
    out, off = [], 0
    for s in shapes:
        n = math.prod(s)
        out.append(flat[off:off + n].reshape(s))
        off += n
    return out


BIG = (("w_in", (D, SHIFT + SW + 2 * D), 1), ("w_out_rwkv", (RW, D), 1), ("w_glu", (SW, 2 * D), 1),
       ("w_out", (D, D), 0), ("w_ffn_up", (D, 2 * DFF), 1), ("w_ffn_down", (DFF, D), 0))
BIG_SMALL = (("rwkv_w_up", (LW, RW), 1), ("rwkv_a_up", (LA, RW), 1), ("rwkv_g_up", (LG, RW), 1),
             ("ffn_conv_w", (3, 2 * DFF), 1))
BIG_LATE = BIG[4:]
BIG_MID = BIG[1:4]


def _shard_shape(shape, axis):
    return (shape[0] // 4, shape[1]) if axis == 0 else (shape[0], shape[1] // 4)


def _to_shards(g, axis):
    r, C = g.shape
    return g.reshape(4, r // 4, C) if axis == 0 else g.reshape(r, 4, C // 4).transpose(1, 0, 2)


def _from_shards(x, axis):
    _, r, C = x.shape
    return x.reshape(4 * r, C) if axis == 0 else x.transpose(1, 0, 2).reshape(r, 4 * C)


def kernel(x, c, w_ada, b_ada, norm1_g, w_in, mu_shift, rwkv_w0, rwkv_w_up, rwkv_a0, rwkv_a_up, rwkv_g_up, rwkv_k_k, rwkv_k_a, rwkv_r_k, rwkv_ln_g, rwkv_ln_b, w_out_rwkv, s5_a_re, s5_a_im, s5_log_dt, s5_b_re, s5_b_im, s5_c_re, s5_c_im, s5_d, w_glu, w_out, norm2_g, w_ffn_up, ffn_conv_w, ffn_conv_b, w_ffn_down, norm_f_g, loss_target, m_w_ada, m_b_ada, m_norm1_g, m_w_in, m_mu_shift, m_rwkv_w0, m_rwkv_w_up, m_rwkv_a0, m_rwkv_a_up, m_rwkv_g_up, m_rwkv_k_k, m_rwkv_k_a, m_rwkv_r_k, m_rwkv_ln_g, m_rwkv_ln_b, m_w_out_rwkv, m_s5_a_re, m_s5_a_im, m_s5_log_dt, m_s5_b_re, m_s5_b_im, m_s5_c_re, m_s5_c_im, m_s5_d, m_w_glu, m_w_out, m_norm2_g, m_w_ffn_up, m_ffn_conv_w, m_ffn_conv_b, m_w_ffn_down, m_norm_f_g, v_w_ada, v_b_ada, v_norm1_g, v_w_in, v_mu_shift, v_rwkv_w0, v_rwkv_w_up, v_rwkv_a0, v_rwkv_a_up, v_rwkv_g_up, v_rwkv_k_k, v_rwkv_k_a, v_rwkv_r_k, v_rwkv_ln_g, v_rwkv_ln_b, v_w_out_rwkv, v_s5_a_re, v_s5_a_im, v_s5_log_dt, v_s5_b_re, v_s5_b_im, v_s5_c_re, v_s5_c_im, v_s5_d, v_w_glu, v_w_out, v_norm2_g, v_w_ffn_up, v_ffn_conv_w, v_ffn_conv_b, v_w_ffn_down, v_norm_f_g):
    names = ["w_ada", "b_ada", "norm1_g", "w_in", "mu_shift", "rwkv_w0", "rwkv_w_up", "rwkv_a0", "rwkv_a_up",
             "rwkv_g_up", "rwkv_k_k", "rwkv_k_a", "rwkv_r_k", "rwkv_ln_g", "rwkv_ln_b", "w_out_rwkv", "s5_a_re",
             "s5_a_im", "s5_log_dt", "s5_b_re", "s5_b_im", "s5_c_re", "s5_c_im", "s5_d", "w_glu", "w_out", "norm2_g",
             "w_ffn_up", "ffn_conv_w", "ffn_conv_b", "w_ffn_down", "norm_f_g"]
    env = dict(locals())
    W = {n: env[n] for n in names}
    M = {n: env["m_" + n] for n in names}
    V = {n: env["v_" + n] for n in names}

    Bl, S, _ = x.shape
    T = Bl * S
    ix, iy, ic = lax.axis_index("x"), lax.axis_index("y"), lax.axis_index("c")
    chip = 2 * ix + iy
    dev = 2 * chip + ic
    rw = functools.partial(_rowwise, Bl=Bl, S=S)

    got_chip, got_dev = _gather_two_level([W[n][0] for n, _, _ in BIG_SMALL[:3]], [W["ffn_conv_w"][0], c], "gather_w")
    full = {n: _from_shards(g, axis) for (n, _, axis), g in zip(BIG_SMALL[:3], got_chip)}
    full["ffn_conv_w"] = _from_shards(got_dev[0][:, 0], 1)
    c_all = got_dev[1].reshape(8 * Bl, D)
    zeros_l = jnp.zeros((LW, RW), F32)
    w_up_p = jnp.concatenate([full["rwkv_w_up"], zeros_l], axis=0)
    a_up_p = jnp.concatenate([zeros_l, full["rwkv_a_up"]], axis=0)
    g_up = full["rwkv_g_up"]
    conv_w = full["ffn_conv_w"]
    conv_wg, conv_wu = conv_w[:, :DFF], conv_w[:, DFF:]
    conv_bg, conv_bu = ffn_conv_b[:, :DFF], ffn_conv_b[:, DFF:]
    hm = jnp.kron(jnp.eye(NH, dtype=F32), jnp.ones((HD, HD), F32))

    ncol = 6 * D // 4
    b_ada_cols = lax.dynamic_slice_in_dim(b_ada, chip * ncol, ncol, 1)
    mod_part = _ada_fwd(c_all, w_ada[0], b_ada_cols)
    mod4 = _gather_two_level([], [mod_part], "gather_mod")[1][0][:, 0]
    mod4, shards = lax.optimization_barrier((mod4, [W[n][0].astype(MXU_DTYPE) for n, _, _ in BIG]))

    def push_shards(tag, arrs):
        moves = [(i, i, lambda ref, me, peer: ref, lambda ref, me, k: ref.at[_chip_of(me)]) for i in range(len(arrs))]
        lands = [jax.ShapeDtypeStruct((4,) + z.shape, z.dtype) for z in arrs]
        return _send_start("gather_%s_start" % tag, CHIP_FLIPS, arrs, lands, moves), moves

    def pushed_shards(tag, started, moves, after, group):
        owns, gots = _send_wait("gather_%s_wait" % tag, CHIP_FLIPS, started, moves, after)
        for (n, _, axis), own, got in zip(group, owns, gots):
            full[n] = _from_shards(lax.dynamic_update_slice(got, own[None], (chip, 0, 0)), axis)

    first_start, first_moves = push_shards("in", shards[:1])
    norm1_g = norm1_g + first_start["token"]
    mod =lax.dynamic_slice_in_dim(mod4, dev * Bl, Bl, 1).transpose(1, 0, 2).reshape(Bl, 1, 6 * D)
    SH1, SC1, GT1, SH2, SC2, GT2 = range(6)

    x2d = x.reshape(T, D)
    tgt = loss_target.reshape(T, D)

    (h1,) = rw("norm1", lambda xv, sc, sh, g: _norm_mod(xv, g, sc, sh), R=512, tiled=[(x2d, D, 0)],
               batch=[(mod, D, SC1), (mod, D, SH1)], full=[norm1_g], out_tiled=[(D, MXU_DTYPE)])
    pushed_shards("in", first_start, first_moves, h1, BIG[:1])
    full["w_in"], rest = lax.optimization_barrier((full["w_in"], shards[1:]))
    late_start, late_moves = push_shards("rest", rest)
    mu_shift = mu_shift + late_start["token"]
    w_p, w_u, w_g = full["w_in"][:, :SHIFT], full["w_in"][:, SHIFT:SHIFT + SW], full["w_in"][:, SHIFT + SW:]
    p = _mm([h1], [w_p], F32, "proj_p")
    u = _mm([h1], [w_u], F32, "proj_u")
    gates = _mm([h1], [w_g], MXU_DTYPE, "proj_g")

    prep_params = [rwkv_w0, w_up_p, rwkv_a0, a_up_p, g_up, rwkv_k_k, rwkv_k_a, hm]

    def prep_fwd(pv, ph, mu, *pp):
        ps = pv + (_shift_down(pv, ph, 1) - pv) * mu
        return _rwkv_prep(*_split_ps(ps), *pp)

    r_, w_, k_, v_, a_, b_, g_ = rw("rwkv_prep", prep_fwd, R=256, tiled=[(p, SHIFT, 0)], prev=[(p, SHIFT, 0)],
                                    full=[mu_shift] + prep_params, out_tiled=[(RW, F32)] * 7)
    y_wkv, ck, tinv = _wkv_fwd(r_, w_, k_, v_, a_, b_, Bl, S)
    r_k_row = rwkv_r_k.reshape(1, RW)
    post_params = [rwkv_ln_g, rwkv_ln_b, r_k_row, hm]
    (o_rwkv,) = rw("rwkv_post", _rwkv_post, R=256,
                   tiled=[(y_wkv, RW, 0), (r_, RW, 0), (k_, RW, 0), (v_, RW, 0), (g_, RW, 0)],
                   full=post_params, out_tiled=[(RW, MXU_DTYPE)])
    pushed_shards("rest", late_start, late_moves, o_rwkv, BIG[1:])
    y_a = _mm([o_rwkv], [full["w_out_rwkv"]], MXU_DTYPE, "out_rwkv")

    expand = jnp.kron(jnp.eye(SP, dtype=F32), jnp.ones((1, SGC), F32))
    s5_in = (s5_a_re[0], s5_a_im[0], s5_log_dt[0].reshape(NG, 1), s5_b_re[0].reshape(NG, SP * SGC),
             s5_b_im[0].reshape(NG, SP * SGC), expand)
    ab_re, ab_im, bb_re, bb_im = _s5_disc(*s5_in)
    eye8 = jnp.eye(8, dtype=F32)

    def blockdiag_in(bb):
        t = bb.reshape(NSG, 8, SP, SGC)
        return jnp.einsum("ab,sapc->sacbp", eye8, t).reshape(NSG, 128, 512)

    def blockdiag_out(cc):
        t = cc.reshape(NSG, 8, SGC, SP)
        return jnp.einsum("ab,sacp->sapbc", eye8, t).reshape(NSG, 512, 128)

    wb = jnp.concatenate([blockdiag_in(bb_re), blockdiag_in(bb_im)], axis=2).astype(MXU_DTYPE)
    wc = jnp.concatenate([blockdiag_out(s5_c_re[0]), -blockdiag_out(s5_c_im[0])], axis=1).astype(MXU_DTYPE)
    ab = jnp.stack([ab_re.reshape(NST), ab_im.reshape(NST)])
    y_ssm, s5_st, s5_x, s5o = _s5_fwd(u, wb, wc, ab, s5_d, Bl, S)
    z = _mm([s5o], [full["w_glu"]], MXU_DTYPE, "glu")
    mix_tiled = [(gates, D, 0), (gates, D, 1), (y_a, D, 0), (z, D, 0), (z, D, 1)]
    (mixed_in,) = rw("mix", lambda *a: _mix(*(v.astype(F32) for v in a)), R=256, tiled=mix_tiled,
                     out_tiled=[(D, MXU_DTYPE)])
    mixed = _mm([mixed_in], [full["w_out"]], F32, "out_proj")

    def norm2_fwd(xv, mx, gt, sc, sh, g):
        x1 = xv + gt * mx
        return x1, _norm_mod(x1, g, sc, sh)

    x1, h2 = rw("norm2", norm2_fwd, R=512, tiled=[(x2d, D, 0), (mixed, D, 0)],
                batch=[(mod, D, GT1), (mod, D, SC2), (mod, D, SH2)], full=[norm2_g],
                out_tiled=[(D, F32), (D, MXU_DTYPE)])
    up =_mm([h2], [full["w_ffn_up"]], MXU_DTYPE, "ffn_up")
    conv_tiled = [(up, 0), (up, 1)]
    conv_full = [conv_wg, conv_wu, conv_bg, conv_bu]
    cw = functools.partial(_colwise, Bl=Bl, S=S, R=128, W=DFF, strip=LANES)

    def act_fwd(*a):
        return ((_silu_gate(*_conv_act(*a)),),)

    (act,) = cw("ffn_act", act_fwd, tiled=conv_tiled, prev=conv_tiled, full=conv_full, out_tiled=[(1, MXU_DTYPE)])
    ffn = _mm([act], [full["w_ffn_down"]], F32, "ffn_down")

    def head(x1v, fv, tv, gt, g):
        x2 = x1v + gt * fv
        y, vjp = jax.vjp(_rms, x2, g)
        e = y - tv
        dx2, dg = vjp(e * (1.0 / D))
        loss = jnp.sum(e * e, keepdims=True) * jnp.ones((1, LANES), F32)
        return dx2, dx2 * gt, jnp.sum(dx2 * fv, axis=0, keepdims=True), dg.reshape(1, D), loss

    dx2, d_ffn, d_gt2, g_norm_f, loss_acc = rw(
        "head", head, R=512, tiled=[(x1, D, 0), (ffn, D, 0), (tgt, D, 0)], batch=[(mod, D, GT2)],
        full=[norm_f_g.reshape(1, D)], out_tiled=[(D, F32), (D, MXU_DTYPE)], out_batch=[D],
        out_acc=[(1, D), (1, LANES)])
    loss = lax.psum(0.5 / D * loss_acc[0, 0], ("x", "y", "c"))

    d_act = _mm([d_ffn], [full["w_ffn_down"]], F32, "d_act", bt=True)
    g_w_ffn_down = _mm_tn(act, d_ffn, "g_ffn_down")

    def act_bwd(ug, uu, dact, hg, hu, wg, wu, bg, bu):
        (gate, taps_g), (upv, taps_u) = _conv3(ug, hg, wg, bg), _conv3(uu, hu, wu, bu)
        _, vjp_s = jax.vjp(_silu_gate, gate, upv)
        d_gate, d_upv = vjp_s(dact)
        def taps(dh, shifted):
            return [jnp.sum(dh * s, axis=0, keepdims=True) for s in shifted] + [jnp.sum(dh, axis=0, keepdims=True)]
        return ((d_gate,), (d_upv,), *taps(d_gate, taps_g), *taps(d_upv, taps_u))

    dh_g, dh_u, *tapg = cw("ffn_act_bwd", act_bwd, tiled=conv_tiled + [(d_act, 0)], prev=conv_tiled, full=conv_full,
                           out_tiled=[(1, MXU_DTYPE), (1, MXU_DTYPE)], n_acc=8)
    g_cw_g, g_cb_g = jnp.concatenate(tapg[0:3], axis=0), tapg[3]
    g_cw_u, g_cb_u = jnp.concatenate(tapg[4:7], axis=0), tapg[7]

    def conv_t(dg, du_, ng, nu, wg, wu):
        dg, du_, ng, nu = (z.astype(F32) for z in (dg, du_, ng, nu))

        def ct(d, n, w):
            return w[2:3] * d + w[1:2] * _shift_up(d, n, 1) + w[0:1] * _shift_up(d, n, 2)
        return ((ct(dg, ng, wg), ct(du_, nu, wu)),)

    (d_up,) = cw("conv_bwd", conv_t, tiled=[(dh_g, 0), (dh_u, 0)], nxt=[(dh_g, 0), (dh_u, 0)],
                 full=[conv_wg, conv_wu], out_tiled=[(2, MXU_DTYPE)])
    d_h2 = _mm([d_up], [full["w_ffn_up"]], F32, "d_h2", bt=True)
    g_w_ffn_up = _mm_tn(h2, d_up, "g_ffn_up")

    sds = jax.ShapeDtypeStruct
    reduce_src = lambda r: (lambda ref, me, peer: ref.at[_chip_of(peer), _half(r, peer[2])])

    def reduced_halves(tag, started, moves, after):
        gsh_own, got = _send_wait("rs_%s_wait" % tag, ALL_FLIPS, started, moves, after)
        halves = []
        for i, (g, gt) in enumerate(zip(gsh_own, got)):
            h = g.shape[1] // 2
            own = lax.dynamic_slice(g, (chip, ic * h, 0), (1, h, g.shape[2]))
            halves.append(_sum_slots([own, gt], F32, "rs_%s_sum%d" % (tag, i)))
        return halves

    def share_start(tag, halves):
        moves = [(i, i, lambda ref, me, peer: ref, lambda ref, me, k, r=2 * g.shape[0]: ref.at[_half(r, me[2])])
                 for i, g in enumerate(halves)]
        lands = [sds((2 * g.shape[0], g.shape[1]), F32) for g in halves]
        return _send_start("share_%s_start" % tag, PAIR_FLIPS, halves, lands, moves), moves

    def share_finish(tag, started, moves, after, group, grads):
        mine_h, got = _send_wait("share_%s_wait" % tag, PAIR_FLIPS, started, moves, after)
        for (n, _, _), mh, whole in zip(group, mine_h, got):
            grads[n] = lax.dynamic_update_slice(whole, mh, (ic * mh.shape[0], 0))[None]

    def reduce_start(tag, group, mats):
        gsh = [_to_shards(g, ax).astype(MXU_DTYPE) for g, (_, _, ax) in zip(mats, group)]
        moves = [(i, i, reduce_src(g.shape[1]), lambda ref, me, k: ref.at[k]) for i, g in enumerate(gsh)]
        lands = [sds((len(ALL_FLIPS), g.shape[1] // 2, g.shape[2]), MXU_DTYPE) for g in gsh]
        return _send_start("rs_%s_start" % tag, ALL_FLIPS, gsh, lands, moves), moves

    rsl, rsl_moves = reduce_start("ffn", BIG_LATE, (g_w_ffn_up, g_w_ffn_down))
    norm2_g = norm2_g + rsl["token"]

    def norm2_bwd(x1v, dh2, dx2v, mx, gt, sc, sh, g):
        _, vjp = jax.vjp(_norm_mod, x1v, g, sc, sh)
        dxn, dg, dsc, dsh = vjp(dh2)
        dx1 = dx2v + dxn
        return dx1, dx1 * gt, jnp.sum(dx1 * mx, axis=0, keepdims=True), dsc, dsh, dg

    dx1, d_mixed, d_gt1, d_sc2, d_sh2, g_norm2 = rw(
        "norm2_bwd", norm2_bwd, R=512, tiled=[(x1, D, 0), (d_h2, D, 0), (dx2, D, 0), (mixed, D, 0)],
        batch=[(mod, D, GT1), (mod, D, SC2), (mod, D, SH2)], full=[norm2_g],
        out_tiled=[(D, F32), (D, MXU_DTYPE)], out_batch=[D, D, D], out_acc=[(1, D)])

    d_mixed_in = _mm([d_mixed], [full["w_out"]], MXU_DTYPE, "d_mixed_in", bt=True)
    g_w_out = _mm_tn(mixed_in, d_mixed, "g_w_out")

    def mix_bwd(*a):
        ga, gb, ya, za, zb, dm = (v.astype(F32) for v in a)
        _, vjp = jax.vjp(_mix, ga, gb, ya, za, zb)
        dga, dgb, dya, dza, dzb = vjp(dm)
        return jnp.concatenate([dga, dgb], axis=1), dya, jnp.concatenate([dza, dzb], axis=1)

    d_gates, d_ya, d_z = rw("mix_bwd", mix_bwd, R=256, tiled=mix_tiled + [(d_mixed_in, D, 0)],
                            out_tiled=[(2 * D, MXU_DTYPE), (D, MXU_DTYPE), (2 * D, MXU_DTYPE)])
    d_o_rwkv = _mm([d_ya], [full["w_out_rwkv"]], F32, "d_o_rwkv", bt=True)
    g_w_out_rwkv = _mm_tn(o_rwkv, d_ya, "g_out_rwkv")
    d_s5o = _mm([d_z], [full["w_glu"]], F32, "d_s5o", bt=True)
    g_w_glu = _mm_tn(s5o, d_z, "g_glu")
    rsm, rsm_moves = reduce_start("mid", BIG_MID, (g_w_out_rwkv, g_w_glu, g_w_out))
    s5_d = s5_d + rsm["token"]

    d_u, d_wb, d_wc, d_ab, g_s5_d = _s5_bwd(u, y_ssm, d_s5o, s5_d, wb, wc, ab, s5_st, s5_x, Bl, S)

    def diag_in(dw):
        t = dw.reshape(NSG, 8, SGC, 8, SP)
        return jnp.einsum("ab,sacbp->sapc", eye8, t).reshape(NG, SP * SGC)

    def diag_out(dw):
        t = dw.reshape(NSG, 8, SP, 8, SGC)
        return jnp.einsum("ab,sapbc->sacp", eye8, t).reshape(NG, SGC, SP)

    g_s5_c_re = diag_out(d_wc[:, :512])
    g_s5_c_im = -diag_out(d_wc[:, 512:])
    disc_cts = (d_ab[0].reshape(NG, SP), d_ab[1].reshape(NG, SP), diag_in(d_wb[:, :, :512]), diag_in(d_wb[:, :, 512:]))
    g_a_re, g_a_im, g_log_dt, g_b_re, g_b_im = _s5_disc_bwd(*s5_in, disc_cts)

    def post_bwd(yv, rv, kv, vv, gv, do, *pp):
        _, vjp = jax.vjp(lambda *a: _rwkv_post(*a, pp[3]), yv, rv, kv, vv, gv, *pp[:3])
        return vjp(do)

    dy_wkv, dr_b, dk_b, dv_b, dg_, g_ln_g, g_ln_b, g_r_k = rw(
        "rwkv_post_bwd", post_bwd, R=256,
        tiled=[(y_wkv, RW, 0), (r_, RW, 0), (k_, RW, 0), (v_, RW, 0), (g_, RW, 0), (d_o_rwkv, RW, 0)],
        full=post_params, out_tiled=[(RW, F32)] * 5, out_acc=[(1, RW)] * 3)
    dr3, dw3, dk3, dv3, da3, db3 = _wkv_bwd(r_, w_, k_, v_, a_, b_, dy_wkv, ck, tinv, Bl, S)

    shl, shl_moves = share_start("ffn", reduced_halves("ffn", rsl, rsl_moves, dr3))
    shm, shm_moves = share_start("mid", reduced_halves("mid", rsm, rsm_moves, dr3))
    mu_shift = mu_shift + (shl["token"] + shm["token"])

    def prep_bwd(pv, dr1, dr2, dwv, dk1, dk2, dv1, dv2, dav, dbv, dgv, ph, mu, *pp):
        prev = _shift_down(pv, ph, 1)
        ps = pv + (prev - pv) * mu
        _, vjp = jax.vjp(lambda *q: _rwkv_prep(*q, pp[7]), *_split_ps(ps), *pp[:7])
        grads = vjp((dr1 + dr2, dwv, dk1 + dk2, dv1 + dv2, dav, dbv, dgv))
        dps = jnp.concatenate(grads[:5], axis=1)
        return (dps,) + tuple(grads[5:]) + (jnp.sum(dps * (prev - pv), axis=0, keepdims=True),)

    prep_outs = rw(
        "rwkv_prep_bwd", prep_bwd, R=256,
        tiled=[(p, SHIFT, 0), (dr3, RW, 0), (dr_b, RW, 0), (dw3, RW, 0), (dk3, RW, 0), (dk_b, RW, 0),
               (dv3, RW, 0), (dv_b, RW, 0), (da3, RW, 0), (db3, RW, 0), (dg_, RW, 0)],
        prev=[(p, SHIFT, 0)], full=[mu_shift] + prep_params,
        out_tiled=[(SHIFT, F32)],
        out_acc=[(1, RW), (LW + LA, RW), (1, RW), (LW + LA, RW), (LG, RW), (1, RW), (1, RW), (1, SHIFT)])
    d_ps, g_w0, g_w_up_p, g_a0, g_a_up_p, g_g_up, g_k_k, g_k_a, g_mu = prep_outs

    small = {"mu_shift": g_mu, "rwkv_w0": g_w0, "rwkv_a0": g_a0, "rwkv_k_k": g_k_k,
             "rwkv_k_a": g_k_a, "rwkv_r_k": g_r_k, "rwkv_ln_g": g_ln_g, "rwkv_ln_b": g_ln_b, "s5_a_re": g_a_re,
             "s5_a_im": g_a_im, "s5_log_dt": g_log_dt, "s5_b_re": g_b_re, "s5_b_im": g_b_im, "s5_c_re": g_s5_c_re,
             "s5_c_im": g_s5_c_im, "s5_d": g_s5_d, "norm2_g": g_norm2,
             "ffn_conv_b": jnp.concatenate([g_cb_g, g_cb_u], axis=1), "norm_f_g": g_norm_f}
    small_names = list(small)
    g_conv_w = jnp.concatenate([g_cw_g, g_cw_u], axis=1)
    shard_small = {"rwkv_w_up": g_w_up_p[:LW], "rwkv_a_up": g_a_up_p[LW:], "rwkv_g_up": g_g_up, "ffn_conv_w": g_conv_w}
    parts = [small[n] for n in small_names] + [_to_shards(shard_small[n], ax) for n, _, ax in BIG_SMALL]
    spack = _pack_rows(parts, F32, SUBLANES)
    sm_moves = [(0, 0, lambda ref, me, peer: ref, lambda ref, me, k: ref.at[2 * _chip_of(me) + me[2]])]
    sm = _send_start("gsmall_start", ALL_FLIPS, [spack], [sds((8,) + spack.shape, F32)], sm_moves)
    mu_shift = mu_shift + sm["token"]

    def shift_bwd(dps, nx, mu):
        return dps * (1.0 - mu) + _shift_up(dps * mu, nx * mu, 1)

    (d_p,) = rw("shift_bwd", shift_bwd, R=256, tiled=[(d_ps, SHIFT, 0)], nxt=[(d_ps, SHIFT, 0)], full=[mu_shift],
                out_tiled=[(SHIFT, MXU_DTYPE)])
    g_w_in = jnp.concatenate([_mm_tn(h1, d_p, "g_w_p"), _mm_tn(h1, d_u, "g_w_u"), _mm_tn(h1, d_gates, "g_w_g")], axis=1)
    rsn, rsn_moves = reduce_start("in", BIG[:1], (g_w_in,))
    norm1_g = norm1_g + rsn["token"]
    d_h1 = _mm([d_p, d_u, d_gates], [w_p, w_u, w_g], F32, "d_h1", bt=True)

    def norm1_bwd(xv, dh1, dx1v, sc, sh, g):
        _, vjp = jax.vjp(_norm_mod, xv, g, sc, sh)
        dxn, dg, dsc, dsh = vjp(dh1)
        return dx1v + dxn, dsc, dsh, dg

    grad_x, d_sc1, d_sh1, g_norm1 = rw(
        "norm1_bwd", norm1_bwd, R=512, tiled=[(x2d, D, 0), (d_h1, D, 0), (dx1, D, 0)],
        batch=[(mod, D, SC1), (mod, D, SH1)], full=[norm1_g], out_tiled=[(D, F32)], out_batch=[D, D], out_acc=[(1, D)])

    dmod = jnp.concatenate([d_sh1, d_sc1, d_gt1, d_sh2, d_sc2, d_gt2], axis=2).reshape(Bl, 6 * D)
    last_all = _gather_two_level([], [dmod, g_norm1], "gather_dmod")[1]
    dmod_all = last_all[0].reshape(8 * Bl, 6 * D)
    shn, shn_moves = share_start("in", reduced_halves("in", rsn, rsn_moves, dmod_all))
    dmod_cols = lax.dynamic_slice_in_dim(dmod_all, chip * ncol, ncol, 1)
    g_w_ada, g_b_ada = _ada_bwd(c_all, dmod_cols, dmod_all)

    grads = {"norm1_g": _sum_slots(last_all[1].reshape(8, 1, D), F32, "sum_norm1")}
    sm_own, sm_got = _send_wait("gsmall_wait", ALL_FLIPS, sm, sm_moves, g_b_ada)
    s_all = lax.dynamic_update_slice(sm_got[0], sm_own[0][None], (dev, 0, 0))
    s_sum = _sum_slots(s_all, F32, "sum_gsmall").reshape(-1)
    off = 0
    for n in small_names:
        grads[n] = s_sum[off:off + W[n].size].reshape(W[n].shape)
        off += W[n].size
    for n, shape, axis in BIG_SMALL:
        ss = _shard_shape(shape, axis)
        k4 = 4 * math.prod(ss)
        sh4 = s_sum[off:off + k4].reshape(4, math.prod(ss))
        grads[n] = lax.dynamic_index_in_dim(sh4, chip, 0, keepdims=False).reshape((1,) + ss)
        off += k4

    share_finish("ffn", shl, shl_moves, s_sum, BIG_LATE, grads)
    share_finish("mid", shm, shm_moves, grads[BIG_LATE[0][0]], BIG_MID, grads)
    grads["w_ada"] = g_w_ada[None]
    grads["b_ada"] = g_b_ada

    delta, new_m, new_v = {}, {}, {}
    to2 = lambda z: z.reshape(-1, z.shape[-1])

    def adamw(n):
        d_, m_, v2_ = _adamw(to2(W[n]), to2(grads[n]), to2(M[n]), to2(V[n]), "adamw_" + n)
        delta[n], new_m[n], new_v[n] = (z.reshape(W[n].shape) for z in (d_, m_, v2_))

    for n in ["w_ada"] + [b[0] for b in BIG[1:]]:
        adamw(n)
    rest = [n for n in names if n not in delta and n != "w_in"]
    packs = [_pack_rows([src[n] for n in rest], F32, SUBLANES) for src in (W, grads, M, V)]
    d_, m_, v2_ = _adamw(*packs, "adamw_small")
    shapes = [W[n].shape for n in rest]
    for dst, z in ((delta, d_), (new_m, m_), (new_v, v2_)):
        for n, val in zip(rest, _unpack(z.reshape(-1), shapes)):
            dst[n] = val
    share_finish("in", shn, shn_moves, d_, BIG[:1], grads)
    adamw("w_in")

    return (loss, grad_x.reshape(Bl, S, D), *[grads[n] for n in names], *[delta[n] for n in names],
            *[new_m[n] for n in names], *[new_v[n] for n in names])
```

```python
import functools
import math

import jax
import jax.numpy as jnp
from jax import lax
from jax.experimental import pallas as pl
from jax.experimental.pallas import tpu as pltpu

F32 = jnp.float32
BF16 = jnp.bfloat16
MXU_DTYPE = jnp.bfloat16
MESH_IDS = pl.DeviceIdType.MESH
HIGHEST = lax.Precision.HIGHEST

D = 1024
RW, NH, HD = 512, 8, 64
LW, LA, LG = 64, 64, 128
SW, SGC, NG, SP = 512, 16, 32, 64
NSG = 4
SHIFT = 3 * RW + LW + LA + LG
DFF = 2816
RMS_EPS, GN_EPS, L2_EPS = 1e-6, 64e-5, 1e-12
LR, B1, B2, ADAM_EPS, WD, STEP = 0.001, 0.9, 0.999, 1e-8, 0.01, 10
DECAY_SCALE = math.exp(-0.5)
GELU_C = math.sqrt(2.0 / math.pi)

VMEM_LIMIT = 52 * 1024 * 1024
SUBLANES, LANES = 8, 128
HALO = 16


def _pick(n, cap):
    if n <= cap:
        return n
    best = None
    for t in range(LANES, cap + 1, LANES):
        if n % t == 0:
            best = t
    assert best is not None, (n, cap)
    return best


def _params(sem=None, vmem=VMEM_LIMIT):
    return pltpu.CompilerParams(dimension_semantics=sem, vmem_limit_bytes=vmem)


def _chip_of(p):
    return 2 * p[0] + p[1]


def _me():
    return (lax.axis_index("x"), lax.axis_index("y"), lax.axis_index("c"))


def _half(rows, core):
    h = rows // 2
    return pl.ds(pl.multiple_of(core * h, 16 if h % 16 == 0 else SUBLANES), h)


_HBM =pl.BlockSpec(memory_space=pltpu.HBM)
_SEM = pl.BlockSpec(memory_space=pltpu.SEMAPHORE)
_DATAFLOW = pltpu.SideEffectType.DATAFLOW_SIDE_EFFECTING


def _split_copies(flips, moves, src_refs, land_refs, send_sems, recv_sems):
    me = _me()
    nf = len(flips)
    out = []
    for m, (si, li, src_sel, dst_sel) in enumerate(moves):
        for k, f in enumerate(flips):
            peer = tuple(1 - v if b else v for v, b in zip(me, f))
            out.append(pltpu.make_async_remote_copy(
                src_ref=src_sel(src_refs[si], me, peer), dst_ref=dst_sel(land_refs[li], me, k),
                send_sem=send_sems.at[m * nf + k], recv_sem=recv_sems.at[m * nf + k],
                device_id=peer, device_id_type=MESH_IDS))
    return out


def _send_start(name, flips, srcs, land_shapes, moves):
    ns, nl = len(srcs), len(land_shapes)
    n = len(moves) * len(flips)

    def body(*refs):
        for cp in _split_copies(flips, moves, refs[:ns], refs[ns:ns + nl], refs[ns + nl], refs[ns + nl + 1]):
            cp.start()
        refs[-1][...] = jnp.zeros(refs[-1].shape, F32)

    hbm = lambda z: pltpu.with_memory_space_constraint(z, pltpu.HBM)
    lands = [lax.empty(s.shape, s.dtype) for s in land_shapes]
    res = pl.pallas_call(
        body, name=name,
        out_shape=(pltpu.SemaphoreType.DMA((n,)), pltpu.SemaphoreType.DMA((n,)),
                   *[pltpu.HBM(z.shape, z.dtype) for z in srcs], *[pltpu.HBM(s.shape, s.dtype) for s in land_shapes],
                   jax.ShapeDtypeStruct((SUBLANES, LANES), F32)),
        in_specs=[_HBM] * (ns + nl),
        out_specs=(_SEM, _SEM, *[_HBM] * (ns + nl), pl.BlockSpec(memory_space=pltpu.VMEM)),
        input_output_aliases={i: 2 + i for i in range(ns + nl)},
        compiler_params=pltpu.CompilerParams(has_side_effects=_DATAFLOW),
    )(*[hbm(z) for z in srcs], *[hbm(z) for z in lands])
    return {"sems": res[:2], "srcs": list(res[2:2 + ns]), "lands": list(res[2 + ns:2 + ns + nl]), "token": res[-1][0, 0]}


def _send_wait(name, flips, started, moves, after):
    srcs, lands = started["srcs"], started["lands"]
    ns, nl = len(srcs), len(lands)

    def body(*refs):
        for cp in _split_copies(flips, moves, refs[:ns], refs[ns:ns + nl], refs[ns + nl], refs[ns + nl + 1]):
            cp.wait_send()
            cp.wait_recv()

    res = pl.pallas_call(
        body, name=name, out_shape=[pltpu.HBM(z.shape, z.dtype) for z in srcs + lands],
        in_specs=[_HBM] * (ns + nl) + [_SEM, _SEM, pl.BlockSpec(memory_space=pl.ANY)],
        out_specs=[_HBM] * (ns + nl), input_output_aliases={i: i for i in range(ns + nl)},
        compiler_params=pltpu.CompilerParams(has_side_effects=_DATAFLOW),
    )(*srcs, *lands, *started["sems"], after)
    return list(res[:ns]), list(res[ns:])


CHIP_FLIPS = ((1, 0, 0), (0, 1, 0), (1, 1, 0))
PAIR_FLIPS = ((0, 0, 1),)
ALL_FLIPS = CHIP_FLIPS + ((1, 0, 1), (0, 1, 1), (1, 1, 1)) + PAIR_FLIPS


def _gather_two_level(chip_arrs, dev_arrs, name):
    arrs = list(chip_arrs) + list(dev_arrs)
    n, nchip = len(arrs), len(chip_arrs)
    NS = 7

    def body(*refs):
        srcs, outs = refs[:n], refs[n:2 * n]
        send_sems, recv_sems, loc_sems = refs[2 * n:]
        x, y, c = _me()
        sib = (x, y, 1 - c)
        chips = [(1 - x, y), (x, 1 - y), (1 - x, 1 - y)]
        mine = 2 * x + y
        ids = [2 * cx + cy for cx, cy in chips]

        def part(i, slot, core):
            if i < nchip:
                return outs[i].at[slot, _half(arrs[i].shape[0], core)]
            return outs[i].at[slot, core]

        def rcopy(i, k, src, dst, to):
            return pltpu.make_async_remote_copy(src_ref=src, dst_ref=dst, send_sem=send_sems.at[i * NS + k],
                                                recv_sem=recv_sems.at[i * NS + k], device_id=to, device_id_type=MESH_IDS)

        started, locs = [], []
        for i in range(n):
            own = srcs[i].at[_half(arrs[i].shape[0], c)] if i < nchip else srcs[i]
            loc = pltpu.make_async_copy(srcs[i], outs[i].at[mine] if i < nchip else outs[i].at[mine, c], loc_sems.at[i])
            loc.start()
            locs.append(loc)
            for f, chip in enumerate(chips):
                cp = rcopy(i, f, own, part(i, mine, c), (*chip, c))
                cp.start()
                started.append(cp)
            if i >= nchip:
                cp = rcopy(i, 6, own, part(i, mine, c), sib)
                cp.start()
                started.append(cp)
        for i in range(n):
            for f in range(3):
                land = part(i, ids[f], c)
                rcopy(i, f, land, land, sib).wait_recv()
                fw = rcopy(i, 3 + f, land, land, sib)
                fw.start()
                started.append(fw)
        for i in range(n):
            for f in range(3):
                land = part(i, ids[f], 1 - c)
                rcopy(i, 3 + f, land, land, sib).wait_recv()
            if i >= nchip:
                land = part(i, mine, 1 - c)
                rcopy(i, 6, land, land, sib).wait_recv()
        for cp in started:
            cp.wait_send()
        for loc in locs:
            loc.wait()

    outs = [jax.ShapeDtypeStruct((4,) + a.shape, a.dtype) for a in chip_arrs]
    outs += [jax.ShapeDtypeStruct((4, 2) + a.shape, a.dtype) for a in dev_arrs]
    res = pl.pallas_call(
        body, name=name, out_shape=outs,
        in_specs=[pl.BlockSpec(memory_space=pl.ANY)] * n, out_specs=[pl.BlockSpec(memory_space=pl.ANY)] * n,
        scratch_shapes=[pltpu.SemaphoreType.DMA((n * NS,)), pltpu.SemaphoreType.DMA((n * NS,)),
                        pltpu.SemaphoreType.DMA((n,))],
    )(*arrs)
    return res[:nchip], res[nchip:]


def _mm(As, Bs, out_dtype, name, tm=512, cap=1408, bt=False):
    n = len(As)
    M, N = As[0].shape[0], Bs[0].shape[0 if bt else 1]
    if sum(a.shape[1] for a in As) <= 1024:
        tm = 2 * tm
    tm = min(tm, M)
    tn = _pick(N, cap)
    dims = (((1,), (1,)), ((), ())) if bt else (((1,), (0,)), ((), ()))

    def body(*refs):
        o = refs[2 * n]
        acc = None
        for a, b in zip(refs[:n], refs[n:2 * n]):
            d = lax.dot_general(a[...].astype(MXU_DTYPE), b[...].astype(MXU_DTYPE), dims, preferred_element_type=F32)
            acc = d if acc is None else acc + d
        o[...] = acc.astype(o.dtype)

    in_specs = [pl.BlockSpec((tm, a.shape[1]), lambda i, j: (i, 0)) for a in As]
    if bt:
        in_specs += [pl.BlockSpec((tn, b.shape[1]), lambda i, j: (j, 0)) for b in Bs]
    else:
        in_specs += [pl.BlockSpec((b.shape[0], tn), lambda i, j: (0, j)) for b in Bs]
    return pl.pallas_call(
        body, name=name, grid=(M // tm, N // tn), in_specs=in_specs,
        out_specs=pl.BlockSpec((tm, tn), lambda i, j: (i, j)),
        out_shape=jax.ShapeDtypeStruct((M, N), out_dtype),
        compiler_params=_params(("parallel", "parallel")),
    )(*As, *Bs)


def _mm_tn(A, G, name, tt=1024, cap=1408):
    T, Ka = A.shape
    N = G.shape[1]
    tt = min(tt, T)
    tk = _pick(Ka, cap)
    tn = _pick(N, cap)

    def body(a, g, o):
        @pl.when(pl.program_id(2) == 0)
        def _():
            o[...] = jnp.zeros(o.shape, F32)
        o[...] += lax.dot_general(a[...].astype(MXU_DTYPE), g[...].astype(MXU_DTYPE),
                                  (((0,), (0,)), ((), ())), preferred_element_type=F32)

    return pl.pallas_call(
        body, name=name, grid=(Ka // tk, N // tn, T // tt),
        in_specs=[pl.BlockSpec((tt, tk), lambda i, j, t: (t, i)), pl.BlockSpec((tt, tn), lambda i, j, t: (t, j))],
        out_specs=pl.BlockSpec((tk, tn), lambda i, j, t: (i, j)),
        out_shape=jax.ShapeDtypeStruct((Ka, N), F32),
        compiler_params=_params(("parallel", "parallel", "arbitrary")),
    )(A, G)


def _rowwise(name, fn, *, Bl, S, R, tiled=(), prev=(), nxt=(), batch=(), full=(),
             out_tiled=(), out_batch=(), out_acc=()):
    R = min(R, S)
    nS = S // R
    T = Bl * S
    hb = R // HALO
    n_in = len(tiled) + len(prev) + len(nxt) + len(batch) + len(full)

    in_specs, args = [], []
    for a, wd, cb in tiled:
        in_specs.append(pl.BlockSpec((R, wd), lambda b, i, cb=cb: (b * nS + i, cb)))
        args.append(a)
    for a, wd, cb in prev:
        in_specs.append(pl.BlockSpec((HALO, wd), lambda b, i, cb=cb: (jnp.maximum((b * nS + i) * hb - 1, 0), cb)))
        args.append(a)
    for a, wd, cb in nxt:
        in_specs.append(pl.BlockSpec((HALO, wd), lambda b, i, cb=cb: (jnp.minimum((b * nS + i + 1) * hb, T // HALO - 1), cb)))
        args.append(a)
    for a, wd, cb in batch:
        in_specs.append(pl.BlockSpec((1, 1, wd), lambda b, i, cb=cb: (b, 0, cb)))
        args.append(a)
    for a in full:
        in_specs.append(pl.BlockSpec(a.shape, lambda b, i, nd=a.ndim: (0,) * nd))
        args.append(a)

    out_specs, out_shape = [], []
    for C, dt in out_tiled:
        out_specs.append(pl.BlockSpec((R, C), lambda b, i: (b * nS + i, 0)))
        out_shape.append(jax.ShapeDtypeStruct((T, C), dt))
    for C in out_batch:
        out_specs.append(pl.BlockSpec((1, 1, C), lambda b, i: (b, 0, 0)))
        out_shape.append(jax.ShapeDtypeStruct((Bl, 1, C), F32))
    for shp in out_acc:
        out_specs.append(pl.BlockSpec(shp, lambda b, i, nd=len(shp): (0,) * nd))
        out_shape.append(jax.ShapeDtypeStruct(shp, F32))

    nt, npv, nnx, nbt = len(tiled), len(prev), len(nxt), len(batch)

    def body(*refs):
        b, i = pl.program_id(0), pl.program_id(1)
        ins, outs = refs[:n_in], refs[n_in:]
        vals = [r[...] for r in ins[:nt]]
        vals += [jnp.where(i > 0, r[...], jnp.zeros(r.shape, r.dtype)) for r in ins[nt:nt + npv]]
        vals += [jnp.where(i < nS - 1, r[...], jnp.zeros(r.shape, r.dtype)) for r in ins[nt + npv:nt + npv + nnx]]
        vals += [r[0] for r in ins[nt + npv + nnx:nt + npv + nnx + nbt]]
        vals += [r[...] for r in ins[nt + npv + nnx + nbt:]]
        res = fn(*vals)
        if not isinstance(res, (tuple, list)):
            res = (res,)
        k = 0
        for _ in out_tiled:
            outs[k][...] = res[k].astype(outs[k].dtype)
            k += 1
        for _ in out_batch:
            o = outs[k]

            @pl.when(i == 0)
            def _(o=o):
                o[...] = jnp.zeros(o.shape, F32)
            o[0] += res[k]
            k += 1
        for _ in out_acc:
            o = outs[k]

            @pl.when((i == 0) & (b == 0))
            def _(o=o):
                o[...] = jnp.zeros(o.shape, F32)
            o[...] += res[k]
            k += 1

    out = pl.pallas_call(
        body, name=name, grid=(Bl, nS), in_specs=in_specs, out_specs=out_specs, out_shape=out_shape,
        compiler_params=_params(("arbitrary", "arbitrary")),
    )(*args)
    return out


def _colwise(name, fn, *, Bl, S, R, W, strip, tiled=(), prev=(), nxt=(), full=(), out_tiled=(), n_acc=0):
    R = min(R, S)
    nS = S // R
    T = Bl * S
    hb = R // HALO
    nt, npv, nnx, nfl = len(tiled), len(prev), len(nxt), len(full)
    n_in = nt + npv + nnx + nfl
    in_specs = [pl.BlockSpec((R, W), lambda b, i, cb=cb: (b * nS + i, cb)) for _, cb in tiled]
    in_specs += [pl.BlockSpec((HALO, W), lambda b, i, cb=cb: (jnp.maximum((b * nS + i) * hb - 1, 0), cb)) for _, cb in prev]
    in_specs += [pl.BlockSpec((HALO, W), lambda b, i, cb=cb: (jnp.minimum((b * nS + i + 1) * hb, T // HALO - 1), cb))
                 for _, cb in nxt]
    in_specs += [pl.BlockSpec(a.shape, lambda b, i: (0, 0)) for a in full]
    out_specs = [pl.BlockSpec((R, m * W), lambda b, i: (b * nS + i, 0)) for m, _ in out_tiled]
    out_specs += [pl.BlockSpec((1, W), lambda b, i: (0, 0))] * n_acc
    out_shape = [jax.ShapeDtypeStruct((T, m * W), dt) for m, dt in out_tiled] + [jax.ShapeDtypeStruct((1, W), F32)] * n_acc

    def body(*refs):
        b, i = pl.program_id(0), pl.program_id(1)
        ins, outs = refs[:n_in], refs[n_in:]

        @pl.when((i == 0) & (b == 0))
        def _():
            for o in outs[len(out_tiled):]:
                o[...] = jnp.zeros(o.shape, F32)

        def col(j, carry):
            cs = pl.ds(pl.multiple_of(j * strip, strip), strip)
            vals = [r[:, cs] for r in ins[:nt]]
            vals += [jnp.where(i > 0, r[:, cs], jnp.zeros((HALO, strip), r.dtype)) for r in ins[nt:nt + npv]]
            vals += [jnp.where(i < nS - 1, r[:, cs], jnp.zeros((HALO, strip), r.dtype)) for r in ins[nt + npv:nt + npv + nnx]]
            vals += [r[:, cs] for r in ins[nt + npv + nnx:]]
            res = fn(*vals)
            for k, (m, _) in enumerate(out_tiled):
                for q in range(m):
                    outs[k][:, pl.ds(pl.multiple_of(q * W + j * strip, strip), strip)] = res[k][q].astype(outs[k].dtype)
            for k in range(len(out_tiled), len(outs)):
                outs[k][:, cs] += res[k]
            return carry

        lax.fori_loop(0, W // strip, col, 0)

    return pl.pallas_call(
        body, name=name, grid=(Bl, nS), in_specs=in_specs, out_specs=out_specs, out_shape=out_shape,
        compiler_params=_params(("arbitrary", "arbitrary")),
    )(*[a for a, _ in tiled], *[a for a, _ in prev], *[a for a, _ in nxt], *full)


def _shift_down(x, halo, k):
    rolled = pltpu.roll(x, k, 0)
    row = lax.broadcasted_iota(jnp.int32, (SUBLANES, x.shape[1]), 0)
    head = rolled[0:SUBLANES]
    for j in range(k):
        head = jnp.where(row == j, halo[HALO - k + j:HALO - k + j + 1, :], head)
    return jnp.concatenate([head, rolled[SUBLANES:]], axis=0)


def _shift_up(x, halo, k):
    n = x.shape[0]
    rolled = pltpu.roll(x, n - k, 0)
    row = lax.broadcasted_iota(jnp.int32, (SUBLANES, x.shape[1]), 0)
    tail = rolled[n - SUBLANES:]
    for j in range(k):
        tail = jnp.where(row == SUBLANES - k + j, halo[j:j + 1, :], tail)
    return jnp.concatenate([rolled[:n - SUBLANES], tail], axis=0)


def _dotm(a, b):
    return jnp.dot(a.astype(MXU_DTYPE), b.astype(MXU_DTYPE), preferred_element_type=F32)


def _split_bf16(x):
    hi = x.astype(BF16)
    return hi, (x - hi.astype(F32)).astype(BF16)


def _headsum_2pass(x, hm):
    hi, lo = _split_bf16(x)
    hb = hm.astype(BF16)
    return jnp.dot(hi, hb, preferred_element_type=F32) + jnp.dot(lo, hb, preferred_element_type=F32)


@jax.custom_vjp
def _headsum(x, hm):
    return _headsum_2pass(x, hm)


_headsum.defvjp(lambda x, hm: (_headsum_2pass(x, hm), hm),
                lambda hm, g: (_headsum_2pass(g, hm), jnp.zeros_like(hm)))


def _sigmoid(x):
    return 0.5 * jnp.tanh(0.5 * x) + 0.5


def _rms(x, g):
    return x * lax.rsqrt(jnp.mean(x * x, axis=-1, keepdims=True) + RMS_EPS) * g


def _norm_mod(x, g, sc, sh):
    return _rms(x, g) * (1.0 + sc) + sh


def _split_ps(ps):
    return (ps[:, 0:RW], ps[:, RW:2 * RW], ps[:, 2 * RW:3 * RW], ps[:, 3 * RW:3 * RW + LW + LA],
            ps[:, 3 * RW + LW + LA:SHIFT])


def _rwkv_prep(r, k, v, wa, gd, w0, w_up_p, a0, a_up_p, g_up, k_k, k_a, hm):
    w_raw = w0 + _dotm(jnp.tanh(wa), w_up_p)
    decay = jnp.exp(-DECAY_SCALE * _sigmoid(w_raw))
    a = _sigmoid(a0 + _dotm(wa, a_up_p))
    g = _dotm(_sigmoid(gd), g_up)
    kk = k * k_k
    kk = kk * lax.rsqrt(_headsum(kk * kk, hm) + L2_EPS)
    k2 = k * (1.0 + (a - 1.0) * k_a)
    return r, decay, k2, v, -kk, kk * a, g


def _rwkv_post(y, r, k2, v, g, ln_g, ln_b, r_k, hm):
    mean = _headsum(y, hm) * (1.0 / HD)
    yc = y - mean
    var = _headsum(yc * yc, hm) * (1.0 / HD)
    yn = yc * lax.rsqrt(var + GN_EPS) * ln_g + ln_b
    bonus = _headsum(r * k2 * r_k, hm) * v
    return (yn + bonus) * g


def _gelu(x):
    return 0.5 * x * (1.0 + jnp.tanh(GELU_C * (x + 0.044715 * (x * x * x))))


def _s5_post(yssm, u, d):
    return _gelu(yssm + d * u)


def _mix(ga, gb, ya, za, zb):
    return _sigmoid(ga) * ya + _sigmoid(gb) * (za * _sigmoid(zb))


def _conv_act(up_g, up_u, hg, hu, w_g, w_u, b_g, b_u):
    gate, upv = _conv3(up_g, hg, w_g, b_g)[0], _conv3(up_u, hu, w_u, b_u)[0]
    return gate, upv


def _conv3(x, h, w, b):
    x, h = x.astype(F32), h.astype(F32)
    s2, s1 = _shift_down(x, h, 2), _shift_down(x, h, 1)
    return b + w[0:1] * s2 + w[1:2] * s1 + w[2:3] * x, (s2, s1, x)


def _silu_gate(gate, upv):
    return gate * _sigmoid(gate) * upv


WKV_L = 64
WKV_KEPT = 5
_NT, _NN, _TN = ((1,), (1,)), ((1,), (0,)), ((0,), (0,))


def _dotw(x, y, dims):
    return lax.dot_general(x.astype(MXU_DTYPE), y.astype(MXU_DTYPE), (dims, ((), ())), preferred_element_type=F32)


def _dot3(x, y, dims):
    (xh, xl), (yh, yl) = _split_bf16(x), _split_bf16(y)
    d = lambda p, q: lax.dot_general(p, q, (dims, ((), ())), preferred_element_type=F32)
    return d(xh, yh) + d(xh, yl) + d(xl, yh)


@jax.custom_vjp
def _gram3(x, y):
    return _dot3(x, y, _NT)


_gram3.defvjp(lambda x, y: (_dot3(x, y, _NT), (x, y)),
              lambda res, g: (_dot3(g, res[1], _NN), _dot3(g, res[0], _TN)))


@jax.custom_vjp
def _gram_known(x, y, value):
    return value


_gram_known.defvjp(lambda x, y, value: (value, (x, y, value)),
                   lambda res, g: (_dot3(g, res[1], _NN), _dot3(g, res[0], _TN), jnp.zeros_like(res[2])))


def _tri_solve_fwd(ns, xs):
    each = lambda f, *ls: tuple(f(*zs) for zs in zip(*ls))
    size = ns[0].shape[0]
    eye = (lax.broadcasted_iota(jnp.int32, (size, size), 0) == lax.broadcasted_iota(jnp.int32, (size, size), 1)).astype(F32)
    ts = each(lambda n: n + eye, ns)
    qs = ns
    for _ in range(WKV_L.bit_length() - 2):
        qs = each(lambda q: _dotw(q, q, _NN), qs)
        ts = each(lambda t, q: t + _dotw(t, q, _NN), ts, qs)
    us = each(lambda t, x: _dotw(t, x, _NN), ts, xs)
    return us, (ts, us)


def _tri_solve_bwd(res, dus):
    ts, us = res
    each = lambda f, *ls: tuple(f(*zs) for zs in zip(*ls))
    dxs = each(lambda t, du: _dotw(t, du, _TN), ts, dus)
    return each(lambda dx, u: _dotw(dx, u, _NT), dxs, us), dxs


@jax.custom_vjp
def _tri_solve(ns, xs):
    return _tri_solve_fwd(ns, xs)[0]


_tri_solve.defvjp(_tri_solve_fwd, _tri_solve_bwd)


@jax.custom_vjp
def _tri_known(ns, xs, ts, us):
    return us


_tri_known.defvjp(lambda ns, xs, ts, us: (us, (ts, us)),
                  lambda res, dus: _tri_solve_bwd(res, dus) + tuple(tuple(jnp.zeros_like(z) for z in r) for r in res))


def _wkv_chunk(s0, r, w, k, v, a, b):
    y, s1 = _wkv_chunks((s0,), (r,), (w,), (k,), (v,), (a,), (b,))
    return y[0], s1[0]


def _wkv_chunks(s0, r, w, k, v, a, b, tinv=None, want_tinv=False):
    each = lambda f, *ls: tuple(f(*xs) for xs in zip(*ls))
    L = r[0].shape[0]
    n2 = 2 * L
    lane_head = lax.broadcasted_iota(jnp.int32, (2, 1, 2 * HD), 2) // HD
    head_mask = (lane_head == lax.broadcasted_iota(jnp.int32, (2, 1, 2 * HD), 0)).astype(F32)
    ri = lax.broadcasted_iota(jnp.int32, (n2, n2), 0)
    ci = lax.broadcasted_iota(jnp.int32, (n2, n2), 1)
    same = (ri // L) == (ci // L)
    strict = same & ((ci % L) < (ri % L))
    incl = same & ((ci % L) <= (ri % L))
    si = lax.broadcasted_iota(jnp.int32, (2 * HD, 2 * HD), 0) // HD
    sj = lax.broadcasted_iota(jnp.int32, (2 * HD, 2 * HD), 1) // HD
    tri = (lax.broadcasted_iota(jnp.int32, (L, L), 0) >= lax.broadcasted_iota(jnp.int32, (L, L), 1)).astype(F32)

    stack = lambda z: (z[None] * head_mask).reshape(n2, 2 * HD)
    dup = lambda z: jnp.broadcast_to(z[None], (2, L, 2 * HD)).reshape(n2, 2 * HD)
    gram = _gram3
    nt, nn, tn = (lambda x, y, d=d: _dotw(x, y, d) for d in (_NT, _NN, _TN))
    add = lambda x, y: x + y

    lw = each(jnp.log, w)
    cum = each(lambda z: jnp.dot(tri, z, preferred_element_type=F32, precision=HIGHEST), lw)
    tot = each(lambda z: jnp.sum(z, axis=0, keepdims=True), lw)
    a2 = each(lambda av, cv, lv: stack(av * jnp.exp(cv - lv)), a, cum, lw)
    r2 = each(lambda rv, cv: stack(rv * jnp.exp(cv)), r, cum)
    v2 = each(stack, v)
    b2 = each(lambda bv, cv: dup(bv * jnp.exp(-cv)), b, cum)
    k2 = each(lambda kv, cv: dup(kv * jnp.exp(-cv)), k, cum)
    n_ab = each(lambda x, y: jnp.where(strict, gram(x, y), 0.0), a2, b2)
    if tinv is None:
        n_ak = each(lambda x, y: jnp.where(strict, gram(x, y), 0.0), a2, k2)
        m_rb = each(lambda x, y: jnp.where(incl, gram(x, y), 0.0), r2, b2)
        m_rk = each(lambda x, y: jnp.where(incl, gram(x, y), 0.0), r2, k2)
    else:
        tinv, k_u, k_ak, k_rb, k_rk = tinv
        n_ak = each(lambda x, y, g: jnp.where(strict, _gram_known(x, y, g), 0.0), a2, k2, k_ak)
        m_rb = each(lambda x, y, g: jnp.where(incl, _gram_known(x, y, g), 0.0), r2, b2, k_rb)
        m_rk = each(lambda x, y, g: jnp.where(incl, _gram_known(x, y, g), 0.0), r2, k2, k_rk)
    x = each(add, each(nt, a2, s0), each(nn, n_ak, v2))
    if want_tinv:
        u, (tinv, _) = _tri_solve_fwd(n_ab, x)
        tinv = (tinv, u, n_ak, m_rb, m_rk)
    else:
        u = _tri_solve(n_ab, x) if tinv is None else _tri_known(n_ab, x, tinv, k_u)
    y2 = each(lambda x, y, z: x + y + z, each(nt, r2, s0), each(nn, m_rb, u), each(nn, m_rk, v2))
    y = each(lambda z: jnp.sum(z.reshape(2, L, 2 * HD), axis=0), y2)
    b3 = each(lambda bv, tv, cv: dup(bv * jnp.exp(tv - cv)), b, tot, cum)
    k3 = each(lambda kv, tv, cv: dup(kv * jnp.exp(tv - cv)), k, tot, cum)
    upd = each(add, each(tn, u, b3), each(tn, v2, k3))
    s1 = each(lambda sv, tv, uv: sv * jnp.exp(tv) + jnp.where(si == sj, uv, 0.0), s0, tot, upd)
    return (y, s1, tinv) if want_tinv else (y, s1)


NPAIR = NH // 2


def _wkv_nb(Bl):
    return 4 if Bl % 4 == 0 else 2 if Bl % 2 == 0 else 1


def _wkv_fwd(r, w, k, v, a, b, Bl, S):
    L = WKV_L
    nC = S // L
    nb = _wkv_nb(Bl)
    chains = [(bi, p, slice(p * 2 * HD, (p + 1) * 2 * HD)) for bi in range(nb) for p in range(NPAIR)]

    def body(r_ref, w_ref, k_ref, v_ref, a_ref, b_ref, y_ref, ck_ref, ti_ref, s_ref):
        @pl.when(pl.program_id(1) == 0)
        def _():
            s_ref[...] = jnp.zeros(s_ref.shape, F32)
        s0 = tuple(s_ref[bi, p] for bi, p, _ in chains)
        ops = [tuple(z[bi, :, cs] for bi, _, cs in chains) for z in (r_ref, w_ref, k_ref, v_ref, a_ref, b_ref)]
        y, s1, kept = _wkv_chunks(s0, *ops, want_tinv=True)
        for i, (bi, p, cs) in enumerate(chains):
            ck_ref[bi, 0, p] = s0[i]
            for q in range(WKV_KEPT):
                ti_ref[bi, 0, p, q] = kept[q][i]
            y_ref[bi, :, cs] = y[i]
            s_ref[bi, p] = s1[i]

    to3 = lambda z: z.reshape(Bl, S, RW)
    row_spec = pl.BlockSpec((nb, L, RW), lambda g, c: (g, c, 0))
    mats = jax.ShapeDtypeStruct((Bl, nC, NPAIR, 2 * HD, 2 * HD), F32)
    mat_spec = pl.BlockSpec((nb, 1, NPAIR, 2 * HD, 2 * HD), lambda g, c: (g, c, 0, 0, 0))
    y, ck, ti = pl.pallas_call(
        body, name="wkv_fwd", grid=(Bl // nb, nC), in_specs=[row_spec] * 6,
        out_specs=[row_spec, mat_spec,
                   pl.BlockSpec((nb, 1, NPAIR, WKV_KEPT, 2 * HD, 2 * HD), lambda g, c: (g, c, 0, 0, 0, 0))],
        out_shape=[jax.ShapeDtypeStruct((Bl, S, RW), F32), mats,
                   jax.ShapeDtypeStruct((Bl, nC, NPAIR, WKV_KEPT, 2 * HD, 2 * HD), F32)],
        scratch_shapes=[pltpu.VMEM((nb, NPAIR, 2 * HD, 2 * HD), F32)],
        compiler_params=_params(("arbitrary", "arbitrary")),
    )(*(to3(z) for z in (r, w, k, v, a, b)))
    return y.reshape(Bl * S, RW), ck, ti


def _wkv_bwd(r, w, k, v, a, b, dy, ck, ti, Bl, S):
    L = WKV_L
    nC = S // L
    nb = _wkv_nb(Bl)
    chains = [(bi, p, slice(p * 2 * HD, (p + 1) * 2 * HD)) for bi in range(nb) for p in range(NPAIR)]

    def body(r_ref, w_ref, k_ref, v_ref, a_ref, b_ref, dy_ref, ck_ref, ti_ref,
             dr_ref, dw_ref, dk_ref, dv_ref, da_ref, db_ref, ds_ref):
        @pl.when(pl.program_id(1) == 0)
        def _():
            ds_ref[...] = jnp.zeros(ds_ref.shape, F32)
        s0 = tuple(ck_ref[bi, 0, p] for bi, p, _ in chains)
        tinv = tuple(tuple(ti_ref[bi, 0, p, q] for bi, p, _ in chains) for q in range(WKV_KEPT))
        ops = [tuple(z[bi, :, cs] for bi, _, cs in chains) for z in (r_ref, w_ref, k_ref, v_ref, a_ref, b_ref)]
        cts = (tuple(dy_ref[bi, :, cs] for bi, _, cs in chains), tuple(ds_ref[bi, p] for bi, p, _ in chains))
        ds0, *grads = jax.vjp(lambda *z: _wkv_chunks(*z, tinv=tinv), s0, *ops)[1](cts)
        for i, (bi, p, cs) in enumerate(chains):
            ds_ref[bi, p] = ds0[i]
            for o, g in zip((dr_ref, dw_ref, dk_ref, dv_ref, da_ref, db_ref), grads):
                o[bi, :, cs] = g[i]

    to3 = lambda z: z.reshape(Bl, S, RW)
    row_spec = pl.BlockSpec((nb, L, RW), lambda g, c: (g, nC - 1 - c, 0))
    rows = jax.ShapeDtypeStruct((Bl, S, RW), F32)
    mat_spec = pl.BlockSpec((nb, 1, NPAIR, 2 * HD, 2 * HD), lambda g, c: (g, nC - 1 - c, 0, 0, 0))
    outs = pl.pallas_call(
        body, name="wkv_bwd", grid=(Bl // nb, nC),
        in_specs=[row_spec] * 7 + [mat_spec, pl.BlockSpec((nb, 1, NPAIR, WKV_KEPT, 2 * HD, 2 * HD),
                                                          lambda g, c: (g, nC - 1 - c, 0, 0, 0, 0))],
        out_specs=[row_spec] * 6, out_shape=[rows] * 6,
        scratch_shapes=[pltpu.VMEM((nb, NPAIR, 2 * HD, 2 * HD), F32)],
        compiler_params=_params(("arbitrary", "arbitrary")),
    )(*(to3(z) for z in (r, w, k, v, a, b, dy)), ck, ti)
    return [o.reshape(Bl * S, RW) for o in outs]


NST = NG * SP


def _cmul(ar, ai, br, bi):
    return ar * br - ai * bi, ar * bi + ai * br


def _s5_tiles(are, aim, reverse):
    if reverse:
        aim = -aim
    row = lax.broadcasted_iota(jnp.int32, (SUBLANES, NST), 0)
    pw = [(are, aim)]
    for _ in range(SUBLANES - 1):
        pw.append(_cmul(pw[-1][0], pw[-1][1], are, aim))
    bc = lambda z: jnp.broadcast_to(z, (SUBLANES, NST))
    ms = []
    for kk in (1, 2, 4):
        cond = (row < SUBLANES - kk) if reverse else (row >= kk)
        ms.append((jnp.where(cond, bc(pw[kk - 1][0]), 0.0), jnp.where(cond, bc(pw[kk - 1][1]), 0.0)))
    pr = jnp.zeros((SUBLANES, NST), F32)
    pi = jnp.zeros((SUBLANES, NST), F32)
    for i in range(SUBLANES):
        n = SUBLANES - i if reverse else i + 1
        pr = jnp.where(row == i, bc(pw[n - 1][0]), pr)
        pi = jnp.where(row == i, bc(pw[n - 1][1]), pi)
    return ms, (pr, pi)


def _s5_block(re, im, ms, pc, cre, cim, sg, reverse):
    ln = slice(sg * 512, (sg + 1) * 512)
    for (mr, mi), kk in zip(ms, (1, 2, 4)):
        sh = SUBLANES - kk if reverse else kk
        sre, sim = pltpu.roll(re, sh, 0), pltpu.roll(im, sh, 0)
        tr, ti = _cmul(mr[:, ln], mi[:, ln], sre, sim)
        re, im = re + tr, im + ti
    tr, ti = _cmul(pc[0][:, ln], pc[1][:, ln], cre[:, ln], cim[:, ln])
    return re + tr, im + ti


def _s5_scan(X_ref, n_rows, ms, pc, cre, cim, reverse, visit=None, acc0=None):
    nblk = n_rows // SUBLANES

    def it(i, carry):
        cre, cim, acc = carry
        j = nblk - 1 - i if reverse else i
        rows = pl.ds(pl.multiple_of(j * SUBLANES, SUBLANES), SUBLANES)
        edge = 0 if reverse else SUBLANES - 1
        blocks, ncre, ncim = [], [], []
        for sg in range(NSG):
            lr = slice(sg * 1024, sg * 1024 + 512)
            li = slice(sg * 1024 + 512, (sg + 1) * 1024)
            re, im = _s5_block(X_ref[rows, lr], X_ref[rows, li], ms, pc, cre, cim, sg, reverse)
            X_ref[rows, lr] = re
            X_ref[rows, li] = im
            blocks.append((re, im))
            ncre.append(re[edge:edge + 1])
            ncim.append(im[edge:edge + 1])
        if visit is not None:
            acc = visit(j, blocks, acc)
        return jnp.concatenate(ncre, axis=1), jnp.concatenate(ncim, axis=1), acc

    return lax.fori_loop(0, nblk, it, (cre, cim, acc0 if acc0 is not None else 0))


def _s5_fwd(u, wb, wc, ab, d, Bl, S, R=256):
    R = min(R, S)
    nC = S // R

    def body(u_ref, wb_ref, wc_ref, ab_ref, d_ref, y_ref, st_ref, X_ref, o_ref, car_ref):
        @pl.when(pl.program_id(1) == 0)
        def _():
            car_ref[...] = jnp.zeros(car_ref.shape, F32)
        st_ref[0, 0] = car_ref[...]
        ms, pc = _s5_tiles(ab_ref[0:1], ab_ref[1:2], False)
        for sg in range(NSG):
            X_ref[:, sg * 1024:(sg + 1) * 1024] = _dotm(u_ref[:, sg * 128:(sg + 1) * 128], wb_ref[sg])
        cre, cim, _ = _s5_scan(X_ref, R, ms, pc, car_ref[0:1], car_ref[1:2], False)
        car_ref[0:1] = cre
        car_ref[1:2] = cim
        for sg in range(NSG):
            y_ref[:, sg * 128:(sg + 1) * 128] = _dotm(X_ref[:, sg * 1024:(sg + 1) * 1024], wc_ref[sg])
        o_ref[...] = _s5_post(y_ref[...], u_ref[...], d_ref[...]).astype(o_ref.dtype)

    rows = pl.BlockSpec((R, SW), lambda b, c: (b * nC + c, 0))
    return pl.pallas_call(
        body, name="s5_fwd", grid=(Bl, nC),
        in_specs=[rows, pl.BlockSpec(wb.shape, lambda b, c: (0, 0, 0)), pl.BlockSpec(wc.shape, lambda b, c: (0, 0, 0)),
                  pl.BlockSpec(ab.shape, lambda b, c: (0, 0)), pl.BlockSpec(d.shape, lambda b, c: (0, 0))],
        out_specs=[rows, pl.BlockSpec((1, 1, 2, NST), lambda b, c: (b, c, 0, 0)),
                   pl.BlockSpec((R, 2 * NST), lambda b, c: (b * nC + c, 0)), rows],
        out_shape=[jax.ShapeDtypeStruct((Bl * S, SW), F32), jax.ShapeDtypeStruct((Bl, nC, 2, NST), F32),
                   jax.ShapeDtypeStruct((Bl * S, 2 * NST), F32), jax.ShapeDtypeStruct((Bl * S, SW), MXU_DTYPE)],
        scratch_shapes=[pltpu.VMEM((2, NST), F32)],
        compiler_params=_params(("arbitrary", "arbitrary")),
    )(u, wb, wc, ab, d)


def _s5_bwd(u, y, do, d, wb, wc, ab, st, xs, Bl, S, R=256):
    R = min(R, S)
    nC = S // R

    def body(u_ref, y_ref, do_ref, d_ref, wb_ref, wc_ref, ab_ref, st_ref, X_ref,
             du_ref, dwb_ref, dwc_ref, dab_ref, dd_ref, G_ref, car_ref):
        first = (pl.program_id(0) == 0) & (pl.program_id(1) == 0)

        @pl.when(first)
        def _():
            for o in (dwb_ref, dwc_ref, dab_ref, dd_ref):
                o[...] = jnp.zeros(o.shape, F32)

        @pl.when(pl.program_id(1) == 0)
        def _():
            car_ref[...] = jnp.zeros(car_ref.shape, F32)

        are, aim = ab_ref[0:1], ab_ref[1:2]
        dy, du_direct, dd = jax.vjp(_s5_post, y_ref[...], u_ref[...], d_ref[...])[1](do_ref[...])
        dd_ref[...] += dd
        dyv = dy.astype(MXU_DTYPE)
        for sg in range(NSG):
            G_ref[:, sg * 1024:(sg + 1) * 1024] = lax.dot_general(
                dyv[:, sg * 128:(sg + 1) * 128], wc_ref[sg].astype(MXU_DTYPE), (((1,), (1,)), ((), ())),
                preferred_element_type=F32)
        rms_, rpc = _s5_tiles(are, aim, True)
        row = lax.broadcasted_iota(jnp.int32, (SUBLANES, 512), 0)

        def visit(j, blocks, acc):
            before = pl.multiple_of(jnp.maximum(j - 1, 0) * SUBLANES, SUBLANES)
            prow = X_ref[pl.ds(before, SUBLANES), :][SUBLANES - 1:SUBLANES]
            rows = pl.ds(pl.multiple_of(j * SUBLANES, SUBLANES), SUBLANES)
            are_acc, aim_acc = [], []
            for sg in range(NSG):
                lr = slice(sg * 1024, sg * 1024 + 512)
                li = slice(sg * 1024 + 512, (sg + 1) * 1024)
                ln = slice(sg * 512, (sg + 1) * 512)
                pre = jnp.where(j > 0, prow[:, lr], st_ref[0, 0, 0:1, ln])
                pim = jnp.where(j > 0, prow[:, li], st_ref[0, 0, 1:2, ln])
                xre = jnp.where(row == 0, pre, pltpu.roll(X_ref[rows, lr], 1, 0))
                xim = jnp.where(row == 0, pim, pltpu.roll(X_ref[rows, li], 1, 0))
                dre, dim = blocks[sg]
                are_acc.append(dre * xre + dim * xim)
                aim_acc.append(dim * xre - dre * xim)
            return acc[0] + jnp.concatenate(are_acc, axis=1), acc[1] + jnp.concatenate(aim_acc, axis=1)

        zero = jnp.zeros((SUBLANES, NST), F32)
        cre, cim, acc = _s5_scan(G_ref, R, rms_, rpc, car_ref[0:1], car_ref[1:2], True, visit, (zero, zero))
        car_ref[0:1] = cre
        car_ref[1:2] = cim
        dab_ref[0:1] += jnp.sum(acc[0], axis=0, keepdims=True)
        dab_ref[1:2] += jnp.sum(acc[1], axis=0, keepdims=True)
        uv = u_ref[...].astype(MXU_DTYPE)
        for sg in range(NSG):
            cs = slice(sg * 1024, (sg + 1) * 1024)
            us = slice(sg * 128, (sg + 1) * 128)
            gx = G_ref[:, cs].astype(MXU_DTYPE)
            dwb_ref[sg] += lax.dot_general(uv[:, us], gx, (((0,), (0,)), ((), ())), preferred_element_type=F32)
            dwc_ref[sg] += lax.dot_general(X_ref[:, cs].astype(MXU_DTYPE), dyv[:, us], (((0,), (0,)), ((), ())),
                                           preferred_element_type=F32)
            du_ssm = lax.dot_general(gx, wb_ref[sg].astype(MXU_DTYPE), (((1,), (1,)), ((), ())),
                                     preferred_element_type=F32)
            du_ref[:, us] = (du_ssm + du_direct[:, us]).astype(du_ref.dtype)

    rmap = lambda b, c: (b * nC + nC - 1 - c, 0)
    rows = pl.BlockSpec((R, SW), rmap)
    return pl.pallas_call(
        body, name="s5_bwd", grid=(Bl, nC),
        in_specs=[rows, rows, rows, pl.BlockSpec(d.shape, lambda b, c: (0, 0)),
                  pl.BlockSpec(wb.shape, lambda b, c: (0, 0, 0)), pl.BlockSpec(wc.shape, lambda b, c: (0, 0, 0)),
                  pl.BlockSpec(ab.shape, lambda b, c: (0, 0)),
                  pl.BlockSpec((1, 1, 2, NST), lambda b, c: (b, nC - 1 - c, 0, 0)),
                  pl.BlockSpec((R, 2 * NST), rmap)],
        out_specs=[rows, pl.BlockSpec(wb.shape, lambda b, c: (0, 0, 0)),
                   pl.BlockSpec(wc.shape, lambda b, c: (0, 0, 0)), pl.BlockSpec((2, NST), lambda b, c: (0, 0)),
                   pl.BlockSpec(d.shape, lambda b, c: (0, 0))],
        out_shape=[jax.ShapeDtypeStruct((Bl * S, SW), MXU_DTYPE), jax.ShapeDtypeStruct(wb.shape, F32),
                   jax.ShapeDtypeStruct(wc.shape, F32), jax.ShapeDtypeStruct((2, NST), F32),
                   jax.ShapeDtypeStruct(d.shape, F32)],
        scratch_shapes=[pltpu.VMEM((R, 2 * NST), F32), pltpu.VMEM((2, NST), F32)],
        compiler_params=_params(("arbitrary", "arbitrary")),
    )(u, y, do, d, wb, wc, ab, st, xs)


def _s5_disc_math(a_re, a_im, log_dt, b_re, b_im, expand):
    dt = jnp.exp(log_dt)
    z_re, z_im = a_re * dt, a_im * dt
    mag = jnp.exp(z_re)
    ab_re, ab_im = mag * jnp.cos(z_im), mag * jnp.sin(z_im)
    den = a_re * a_re + a_im * a_im
    q_re = ((ab_re - 1.0) * a_re + ab_im * a_im) / den
    q_im = (ab_im * a_re - (ab_re - 1.0) * a_im) / den
    qe_re = jnp.dot(q_re, expand, preferred_element_type=F32, precision=HIGHEST)
    qe_im = jnp.dot(q_im, expand, preferred_element_type=F32, precision=HIGHEST)
    return ab_re, ab_im, qe_re * b_re - qe_im * b_im, qe_re * b_im + qe_im * b_re


def _whole(shape):
    return pl.BlockSpec(shape, lambda nd=len(shape): (0,) * nd)


def _s5_disc(a_re, a_im, log_dt, b_re, b_im, expand):
    def body(a, b, c, d, e, f, o0, o1, o2, o3):
        res = _s5_disc_math(a[...], b[...], c[...], d[...], e[...], f[...])
        for o, v in zip((o0, o1, o2, o3), res):
            o[...] = v
    ins = (a_re, a_im, log_dt, b_re, b_im, expand)
    outs = [jax.ShapeDtypeStruct(a_re.shape, F32)] * 2 + [jax.ShapeDtypeStruct(b_re.shape, F32)] * 2
    return pl.pallas_call(body, name="s5_disc", in_specs=[_whole(x.shape) for x in ins],
                          out_specs=[_whole(o.shape) for o in outs], out_shape=outs)(*ins)


def _s5_disc_bwd(a_re, a_im, log_dt, b_re, b_im, expand, cts):
    def body(a, b, c, d, e, f, g0, g1, g2, g3, o0, o1, o2, o3, o4):
        fn = lambda *p: _s5_disc_math(*p, f[...])
        _, vjp = jax.vjp(fn, a[...], b[...], c[...], d[...], e[...])
        for o, v in zip((o0, o1, o2, o3, o4), vjp((g0[...], g1[...], g2[...], g3[...]))):
            o[...] = v
    ins = (a_re, a_im, log_dt, b_re, b_im, expand) + tuple(cts)
    outs = [jax.ShapeDtypeStruct(x.shape, F32) for x in (a_re, a_im, log_dt, b_re, b_im)]
    return pl.pallas_call(body, name="s5_disc_bwd", in_specs=[_whole(x.shape) for x in ins],
                          out_specs=[_whole(o.shape) for o in outs], out_shape=outs)(*ins)


def _ada_fwd(c_all, w_shard, b_shard):
    def body(c_ref, w_ref, b_ref, o_ref):
        cv = c_ref[...]
        o_ref[...] = _dotm(cv * _sigmoid(cv), w_ref[...]) + b_ref[...]
    n = w_shard.shape[1]
    return pl.pallas_call(
        body, name="ada_fwd", in_specs=[_whole(c_all.shape), _whole(w_shard.shape), _whole(b_shard.shape)],
        out_specs=_whole((c_all.shape[0], n)), out_shape=jax.ShapeDtypeStruct((c_all.shape[0], n), F32),
        compiler_params=_params(),
    )(c_all, w_shard, b_shard)


def _ada_bwd(c_all, dmod_cols, dmod_all):
    def body(c_ref, dc_ref, da_ref, gw_ref, gb_ref):
        cv = c_ref[...]
        gw_ref[...] = lax.dot_general((cv * _sigmoid(cv)).astype(MXU_DTYPE), dc_ref[...].astype(MXU_DTYPE),
                                      (((0,), (0,)), ((), ())), preferred_element_type=F32)
        gb_ref[...] = jnp.sum(da_ref[...], axis=0, keepdims=True)
    n = dmod_cols.shape[1]
    return pl.pallas_call(
        body, name="ada_bwd", in_specs=[_whole(c_all.shape), _whole(dmod_cols.shape), _whole(dmod_all.shape)],
        out_specs=[_whole((D, n)), _whole((1, dmod_all.shape[1]))],
        out_shape=[jax.ShapeDtypeStruct((D, n), F32), jax.ShapeDtypeStruct((1, dmod_all.shape[1]), F32)],
        compiler_params=_params(),
    )(c_all, dmod_cols, dmod_all)


def _rows_block(n_rows, cap=512):
    if n_rows <= cap:
        return n_rows
    for t in range(cap - cap % SUBLANES, 0, -SUBLANES):
        if n_rows % t == 0:
            return t
    return n_rows


def _adamw(w, g, m, v, name):
    rows, cols = w.shape
    tr = _rows_block(rows, max(SUBLANES, (1 << 19) // max(cols, 1) // SUBLANES * SUBLANES))

    def body(w_ref, g_ref, m_ref, v_ref, d_ref, nm_ref, nv_ref):
        gv = g_ref[...]
        nm = B1 * m_ref[...] + (1.0 - B1) * gv
        nv = B2 * v_ref[...] + (1.0 - B2) * (gv * gv)
        m_hat = nm / (1.0 - B1 ** STEP)
        v_hat = nv / (1.0 - B2 ** STEP)
        d_ref[...] = -LR * (m_hat / (jnp.sqrt(v_hat) + ADAM_EPS) + WD * w_ref[...])
        nm_ref[...] = nm
        nv_ref[...] = nv

    spec = pl.BlockSpec((tr, cols), lambda i: (i, 0))
    sd = jax.ShapeDtypeStruct((rows, cols), F32)
    return pl.pallas_call(body, name=name, grid=(rows // tr,), in_specs=[spec] * 4, out_specs=[spec] * 3,
                          out_shape=[sd] * 3, compiler_params=_params(("parallel",)))(w, g, m, v)


def _sum_slots(x, out_dtype, name):
    xs = x if isinstance(x, (list, tuple)) else [x]
    _, rows, cols = xs[0].shape
    tr = _rows_block(rows)

    def body(*refs):
        acc = None
        for x_ref in refs[:-1]:
            for j in range(x_ref.shape[0]):
                term = x_ref[j].astype(F32)
                acc = term if acc is None else acc + term
        refs[-1][...] = acc.astype(refs[-1].dtype)

    return pl.pallas_call(
        body, name=name, grid=(rows // tr,),
        in_specs=[pl.BlockSpec((z.shape[0], tr, cols), lambda i: (0, i, 0)) for z in xs],
        out_specs=pl.BlockSpec((tr, cols), lambda i: (i, 0)), out_shape=jax.ShapeDtypeStruct((rows, cols), out_dtype),
        compiler_params=_params(("parallel",)))(*xs)


PACK_COLS = 1024


def _pack_rows(parts, dtype, row_mult):
    flat = jnp.concatenate([p.reshape(-1).astype(dtype) for p in parts])
    per = PACK_COLS * row_mult
    n = -(-flat.shape[0] // per) * per
    flat = jnp.pad(flat, (0, n - flat.shape[0]))
    return flat.reshape(n // PACK_COLS, PACK_COLS)


def _unpack(flat, shapes):
    out, off = [], 0
    for s in shapes:
        n = math.prod(s)
        out.append(flat[off:off + n].reshape(s))
        off += n
    return out


BIG = (("w_in", (D, SHIFT + SW + 2 * D), 1), ("w_out_rwkv", (RW, D), 1), ("w_glu", (SW, 2 * D), 1),
       ("w_out", (D, D), 0), ("w_ffn_up", (D, 2 * DFF), 1), ("w_ffn_down", (DFF, D), 0))
BIG_SMALL = (("rwkv_w_up", (LW, RW), 1), ("rwkv_a_up", (LA, RW), 1), ("rwkv_g_up", (LG, RW), 1),
             ("ffn_conv_w", (3, 2 * DFF), 1))
BIG_LATE = BIG[4:]
BIG_MID = BIG[1:4]


def _shard_shape(shape, axis):
    return (shape[0] // 4, shape[1]) if axis == 0 else (shape[0], shape[1] // 4)


def _to_shards(g, axis):
    r, C = g.shape
    return g.reshape(4, r // 4, C) if axis == 0 else g.reshape(r, 4, C // 4).transpose(1, 0, 2)


def _from_shards(x, axis):
    _, r, C = x.shape
    return x.reshape(4 * r, C) if axis == 0 else x.transpose(1, 0, 2).reshape(r, 4 * C)


def kernel(x, c, w_ada, b_ada, norm1_g, w_in, mu_shift, rwkv_w0, rwkv_w_up, rwkv_a0, rwkv_a_up, rwkv_g_up, rwkv_k_k, rwkv_k_a, rwkv_r_k, rwkv_ln_g, rwkv_ln_b, w_out_rwkv, s5_a_re, s5_a_im, s5_log_dt, s5_b_re, s5_b_im, s5_c_re, s5_c_im, s5_d, w_glu, w_out, norm2_g, w_ffn_up, ffn_conv_w, ffn_conv_b, w_ffn_down, norm_f_g, loss_target, m_w_ada, m_b_ada, m_norm1_g, m_w_in, m_mu_shift, m_rwkv_w0, m_rwkv_w_up, m_rwkv_a0, m_rwkv_a_up, m_rwkv_g_up, m_rwkv_k_k, m_rwkv_k_a, m_rwkv_r_k, m_rwkv_ln_g, m_rwkv_ln_b, m_w_out_rwkv, m_s5_a_re, m_s5_a_im, m_s5_log_dt, m_s5_b_re, m_s5_b_im, m_s5_c_re, m_s5_c_im, m_s5_d, m_w_glu, m_w_out, m_norm2_g, m_w_ffn_up, m_ffn_conv_w, m_ffn_conv_b, m_w_ffn_down, m_norm_f_g, v_w_ada, v_b_ada, v_norm1_g, v_w_in, v_mu_shift, v_rwkv_w0, v_rwkv_w_up, v_rwkv_a0, v_rwkv_a_up, v_rwkv_g_up, v_rwkv_k_k, v_rwkv_k_a, v_rwkv_r_k, v_rwkv_ln_g, v_rwkv_ln_b, v_w_out_rwkv, v_s5_a_re, v_s5_a_im, v_s5_log_dt, v_s5_b_re, v_s5_b_im, v_s5_c_re, v_s5_c_im, v_s5_d, v_w_glu, v_w_out, v_norm2_g, v_w_ffn_up, v_ffn_conv_w, v_ffn_conv_b, v_w_ffn_down, v_norm_f_g):
    names = ["w_ada", "b_ada", "norm1_g", "w_in", "mu_shift", "rwkv_w0", "rwkv_w_up", "rwkv_a0", "rwkv_a_up",
             "rwkv_g_up", "rwkv_k_k", "rwkv_k_a", "rwkv_r_k", "rwkv_ln_g", "rwkv_ln_b", "w_out_rwkv", "s5_a_re",
             "s5_a_im", "s5_log_dt", "s5_b_re", "s5_b_im", "s5_c_re", "s5_c_im", "s5_d", "w_glu", "w_out", "norm2_g",
             "w_ffn_up", "ffn_conv_w", "ffn_conv_b", "w_ffn_down", "norm_f_g"]
    env = dict(locals())
    W = {n: env[n] for n in names}
    M = {n: env["m_" + n] for n in names}
    V = {n: env["v_" + n] for n in names}

    Bl, S, _ = x.shape
    T = Bl * S
    ix, iy, ic = lax.axis_index("x"), lax.axis_index("y"), lax.axis_index("c")
    chip = 2 * ix + iy
    dev = 2 * chip + ic
    rw = functools.partial(_rowwise, Bl=Bl, S=S)

    got_chip, got_dev = _gather_two_level([W[n][0] for n, _, _ in BIG_SMALL[:3]], [W["ffn_conv_w"][0], c], "gather_w")
    full = {n: _from_shards(g, axis) for (n, _, axis), g in zip(BIG_SMALL[:3], got_chip)}
    full["ffn_conv_w"] = _from_shards(got_dev[0][:, 0], 1)
    c_all = got_dev[1].reshape(8 * Bl, D)
    zeros_l = jnp.zeros((LW, RW), F32)
    w_up_p = jnp.concatenate([full["rwkv_w_up"], zeros_l], axis=0)
    a_up_p = jnp.concatenate([zeros_l, full["rwkv_a_up"]], axis=0)
    g_up = full["rwkv_g_up"]
    conv_w = full["ffn_conv_w"]
    conv_wg, conv_wu = conv_w[:, :DFF], conv_w[:, DFF:]
    conv_bg, conv_bu = ffn_conv_b[:, :DFF], ffn_conv_b[:, DFF:]
    hm = jnp.kron(jnp.eye(NH, dtype=F32), jnp.ones((HD, HD), F32))

    ncol = 6 * D // 4
    b_ada_cols = lax.dynamic_slice_in_dim(b_ada, chip * ncol, ncol, 1)
    mod_part = _ada_fwd(c_all, w_ada[0], b_ada_cols)
    mod4 = _gather_two_level([], [mod_part], "gather_mod")[1][0][:, 0]
    mod4, shards = lax.optimization_barrier((mod4, [W[n][0].astype(MXU_DTYPE) for n, _, _ in BIG]))

    def push_shards(tag, arrs):
        moves = [(i, i, lambda ref, me, peer: ref, lambda ref, me, k: ref.at[_chip_of(me)]) for i in range(len(arrs))]
        lands = [jax.ShapeDtypeStruct((4,) + z.shape, z.dtype) for z in arrs]
        return _send_start("gather_%s_start" % tag, CHIP_FLIPS, arrs, lands, moves), moves

    def pushed_shards(tag, started, moves, after, group):
        owns, gots = _send_wait("gather_%s_wait" % tag, CHIP_FLIPS, started, moves, after)
        for (n, _, axis), own, got in zip(group, owns, gots):
            full[n] = _from_shards(lax.dynamic_update_slice(got, own[None], (chip, 0, 0)), axis)

    first_start, first_moves = push_shards("in", shards[:1])
    norm1_g = norm1_g + first_start["token"]
    mod =lax.dynamic_slice_in_dim(mod4, dev * Bl, Bl, 1).transpose(1, 0, 2).reshape(Bl, 1, 6 * D)
    SH1, SC1, GT1, SH2, SC2, GT2 = range(6)

    x2d = x.reshape(T, D)
    tgt = loss_target.reshape(T, D)

    (h1,) = rw("norm1", lambda xv, sc, sh, g: _norm_mod(xv, g, sc, sh), R=512, tiled=[(x2d, D, 0)],
               batch=[(mod, D, SC1), (mod, D, SH1)], full=[norm1_g], out_tiled=[(D, MXU_DTYPE)])
    pushed_shards("in", first_start, first_moves, h1, BIG[:1])
    full["w_in"], rest = lax.optimization_barrier((full["w_in"], shards[1:]))
    late_start, late_moves = push_shards("rest", rest)
    mu_shift = mu_shift + late_start["token"]
    w_p, w_u, w_g = full["w_in"][:, :SHIFT], full["w_in"][:, SHIFT:SHIFT + SW], full["w_in"][:, SHIFT + SW:]
    p = _mm([h1], [w_p], F32, "proj_p")
    u = _mm([h1], [w_u], F32, "proj_u")
    gates = _mm([h1], [w_g], MXU_DTYPE, "proj_g")

    prep_params = [rwkv_w0, w_up_p, rwkv_a0, a_up_p, g_up, rwkv_k_k, rwkv_k_a, hm]

    def prep_fwd(pv, ph, mu, *pp):
        ps = pv + (_shift_down(pv, ph, 1) - pv) * mu
        return _rwkv_prep(*_split_ps(ps), *pp)

    r_, w_, k_, v_, a_, b_, g_ = rw("rwkv_prep", prep_fwd, R=256, tiled=[(p, SHIFT, 0)], prev=[(p, SHIFT, 0)],
                                    full=[mu_shift] + prep_params, out_tiled=[(RW, F32)] * 7)
    y_wkv, ck, tinv = _wkv_fwd(r_, w_, k_, v_, a_, b_, Bl, S)
    r_k_row = rwkv_r_k.reshape(1, RW)
    post_params = [rwkv_ln_g, rwkv_ln_b, r_k_row, hm]
    (o_rwkv,) = rw("rwkv_post", _rwkv_post, R=256,
                   tiled=[(y_wkv, RW, 0), (r_, RW, 0), (k_, RW, 0), (v_, RW, 0), (g_, RW, 0)],
                   full=post_params, out_tiled=[(RW, MXU_DTYPE)])
    pushed_shards("rest", late_start, late_moves, o_rwkv, BIG[1:])
    y_a = _mm([o_rwkv], [full["w_out_rwkv"]], MXU_DTYPE, "out_rwkv")

    expand = jnp.kron(jnp.eye(SP, dtype=F32), jnp.ones((1, SGC), F32))
    s5_in = (s5_a_re[0], s5_a_im[0], s5_log_dt[0].reshape(NG, 1), s5_b_re[0].reshape(NG, SP * SGC),
             s5_b_im[0].reshape(NG, SP * SGC), expand)
    ab_re, ab_im, bb_re, bb_im = _s5_disc(*s5_in)
    eye8 = jnp.eye(8, dtype=F32)

    def blockdiag_in(bb):
        t = bb.reshape(NSG, 8, SP, SGC)
        return jnp.einsum("ab,sapc->sacbp", eye8, t).reshape(NSG, 128, 512)

    def blockdiag_out(cc):
        t = cc.reshape(NSG, 8, SGC, SP)
        return jnp.einsum("ab,sacp->sapbc", eye8, t).reshape(NSG, 512, 128)

    wb = jnp.concatenate([blockdiag_in(bb_re), blockdiag_in(bb_im)], axis=2).astype(MXU_DTYPE)
    wc = jnp.concatenate([blockdiag_out(s5_c_re[0]), -blockdiag_out(s5_c_im[0])], axis=1).astype(MXU_DTYPE)
    ab = jnp.stack([ab_re.reshape(NST), ab_im.reshape(NST)])
    y_ssm, s5_st, s5_x, s5o = _s5_fwd(u, wb, wc, ab, s5_d, Bl, S)
    z = _mm([s5o], [full["w_glu"]], MXU_DTYPE, "glu")
    mix_tiled = [(gates, D, 0), (gates, D, 1), (y_a, D, 0), (z, D, 0), (z, D, 1)]
    (mixed_in,) = rw("mix", lambda *a: _mix(*(v.astype(F32) for v in a)), R=256, tiled=mix_tiled,
                     out_tiled=[(D, MXU_DTYPE)])
    mixed = _mm([mixed_in], [full["w_out"]], F32, "out_proj")

    def norm2_fwd(xv, mx, gt, sc, sh, g):
        x1 = xv + gt * mx
        return x1, _norm_mod(x1, g, sc, sh)

    x1, h2 = rw("norm2", norm2_fwd, R=512, tiled=[(x2d, D, 0), (mixed, D, 0)],
                batch=[(mod, D, GT1), (mod, D, SC2), (mod, D, SH2)], full=[norm2_g],
                out_tiled=[(D, F32), (D, MXU_DTYPE)])
    up =_mm([h2], [full["w_ffn_up"]], MXU_DTYPE, "ffn_up")
    conv_tiled = [(up, 0), (up, 1)]
    conv_full = [conv_wg, conv_wu, conv_bg, conv_bu]
    cw = functools.partial(_colwise, Bl=Bl, S=S, R=128, W=DFF, strip=LANES)

    def act_fwd(*a):
        return ((_silu_gate(*_conv_act(*a)),),)

    (act,) = cw("ffn_act", act_fwd, tiled=conv_tiled, prev=conv_tiled, full=conv_full, out_tiled=[(1, MXU_DTYPE)])
    ffn = _mm([act], [full["w_ffn_down"]], F32, "ffn_down")

    def head(x1v, fv, tv, gt, g):
        x2 = x1v + gt * fv
        y, vjp = jax.vjp(_rms, x2, g)
        e = y - tv
        dx2, dg = vjp(e * (1.0 / D))
        loss = jnp.sum(e * e, keepdims=True) * jnp.ones((1, LANES), F32)
        return dx2, dx2 * gt, jnp.sum(dx2 * fv, axis=0, keepdims=True), dg.reshape(1, D), loss

    dx2, d_ffn, d_gt2, g_norm_f, loss_acc = rw(
        "head", head, R=512, tiled=[(x1, D, 0), (ffn, D, 0), (tgt, D, 0)], batch=[(mod, D, GT2)],
        full=[norm_f_g.reshape(1, D)], out_tiled=[(D, F32), (D, MXU_DTYPE)], out_batch=[D],
        out_acc=[(1, D), (1, LANES)])
    loss = lax.psum(0.5 / D * loss_acc[0, 0], ("x", "y", "c"))

    d_act = _mm([d_ffn], [full["w_ffn_down"]], F32, "d_act", bt=True)
    g_w_ffn_down = _mm_tn(act, d_ffn, "g_ffn_down")

    def act_bwd(ug, uu, dact, hg, hu, wg, wu, bg, bu):
        (gate, taps_g), (upv, taps_u) = _conv3(ug, hg, wg, bg), _conv3(uu, hu, wu, bu)
        _, vjp_s = jax.vjp(_silu_gate, gate, upv)
        d_gate, d_upv = vjp_s(dact)
        def taps(dh, shifted):
            return [jnp.sum(dh * s, axis=0, keepdims=True) for s in shifted] + [jnp.sum(dh, axis=0, keepdims=True)]
        return ((d_gate,), (d_upv,), *taps(d_gate, taps_g), *taps(d_upv, taps_u))

    dh_g, dh_u, *tapg = cw("ffn_act_bwd", act_bwd, tiled=conv_tiled + [(d_act, 0)], prev=conv_tiled, full=conv_full,
                           out_tiled=[(1, MXU_DTYPE), (1, MXU_DTYPE)], n_acc=8)
    g_cw_g, g_cb_g = jnp.concatenate(tapg[0:3], axis=0), tapg[3]
    g_cw_u, g_cb_u = jnp.concatenate(tapg[4:7], axis=0), tapg[7]

    def conv_t(dg, du_, ng, nu, wg, wu):
        dg, du_, ng, nu = (z.astype(F32) for z in (dg, du_, ng, nu))

        def ct(d, n, w):
            return w[2:3] * d + w[1:2] * _shift_up(d, n, 1) + w[0:1] * _shift_up(d, n, 2)
        return ((ct(dg, ng, wg), ct(du_, nu, wu)),)

    (d_up,) = cw("conv_bwd", conv_t, tiled=[(dh_g, 0), (dh_u, 0)], nxt=[(dh_g, 0), (dh_u, 0)],
                 full=[conv_wg, conv_wu], out_tiled=[(2, MXU_DTYPE)])
    d_h2 = _mm([d_up], [full["w_ffn_up"]], F32, "d_h2", bt=True)
    g_w_ffn_up = _mm_tn(h2, d_up, "g_ffn_up")

    sds = jax.ShapeDtypeStruct
    reduce_src = lambda r: (lambda ref, me, peer: ref.at[_chip_of(peer), _half(r, peer[2])])

    def reduced_halves(tag, started, moves, after):
        gsh_own, got = _send_wait("rs_%s_wait" % tag, ALL_FLIPS, started, moves, after)
        halves = []
        for i, (g, gt) in enumerate(zip(gsh_own, got)):
            h = g.shape[1] // 2
            own = lax.dynamic_slice(g, (chip, ic * h, 0), (1, h, g.shape[2]))
            halves.append(_sum_slots([own, gt], F32, "rs_%s_sum%d" % (tag, i)))
        return halves

    def share_start(tag, halves):
        moves = [(i, i, lambda ref, me, peer: ref, lambda ref, me, k, r=2 * g.shape[0]: ref.at[_half(r, me[2])])
                 for i, g in enumerate(halves)]
        lands = [sds((2 * g.shape[0], g.shape[1]), F32) for g in halves]
        return _send_start("share_%s_start" % tag, PAIR_FLIPS, halves, lands, moves), moves

    def share_finish(tag, started, moves, after, group, grads):
        mine_h, got = _send_wait("share_%s_wait" % tag, PAIR_FLIPS, started, moves, after)
        for (n, _, _), mh, whole in zip(group, mine_h, got):
            grads[n] = lax.dynamic_update_slice(whole, mh, (ic * mh.shape[0], 0))[None]

    def reduce_start(tag, group, mats):
        gsh = [_to_shards(g, ax).astype(MXU_DTYPE) for g, (_, _, ax) in zip(mats, group)]
        moves = [(i, i, reduce_src(g.shape[1]), lambda ref, me, k: ref.at[k]) for i, g in enumerate(gsh)]
        lands = [sds((len(ALL_FLIPS), g.shape[1] // 2, g.shape[2]), MXU_DTYPE) for g in gsh]
        return _send_start("rs_%s_start" % tag, ALL_FLIPS, gsh, lands, moves), moves

    rsl, rsl_moves = reduce_start("ffn", BIG_LATE, (g_w_ffn_up, g_w_ffn_down))
    norm2_g = norm2_g + rsl["token"]

    def norm2_bwd(x1v, dh2, dx2v, mx, gt, sc, sh, g):
        _, vjp = jax.vjp(_norm_mod, x1v, g, sc, sh)
        dxn, dg, dsc, dsh = vjp(dh2)
        dx1 = dx2v + dxn
        return dx1, dx1 * gt, jnp.sum(dx1 * mx, axis=0, keepdims=True), dsc, dsh, dg

    dx1, d_mixed, d_gt1, d_sc2, d_sh2, g_norm2 = rw(
        "norm2_bwd", norm2_bwd, R=512, tiled=[(x1, D, 0), (d_h2, D, 0), (dx2, D, 0), (mixed, D, 0)],
        batch=[(mod, D, GT1), (mod, D, SC2), (mod, D, SH2)], full=[norm2_g],
        out_tiled=[(D, F32), (D, MXU_DTYPE)], out_batch=[D, D, D], out_acc=[(1, D)])

    d_mixed_in = _mm([d_mixed], [full["w_out"]], MXU_DTYPE, "d_mixed_in", bt=True)
    g_w_out = _mm_tn(mixed_in, d_mixed, "g_w_out")

    def mix_bwd(*a):
        ga, gb, ya, za, zb, dm = (v.astype(F32) for v in a)
        _, vjp = jax.vjp(_mix, ga, gb, ya, za, zb)
        dga, dgb, dya, dza, dzb = vjp(dm)
        return jnp.concatenate([dga, dgb], axis=1), dya, jnp.concatenate([dza, dzb], axis=1)

    d_gates, d_ya, d_z = rw("mix_bwd", mix_bwd, R=256, tiled=mix_tiled + [(d_mixed_in, D, 0)],
                            out_tiled=[(2 * D, MXU_DTYPE), (D, MXU_DTYPE), (2 * D, MXU_DTYPE)])
    d_o_rwkv = _mm([d_ya], [full["w_out_rwkv"]], F32, "d_o_rwkv", bt=True)
    g_w_out_rwkv = _mm_tn(o_rwkv, d_ya, "g_out_rwkv")
    d_s5o = _mm([d_z], [full["w_glu"]], F32, "d_s5o", bt=True)
    g_w_glu = _mm_tn(s5o, d_z, "g_glu")
    rsm, rsm_moves = reduce_start("mid", BIG_MID, (g_w_out_rwkv, g_w_glu, g_w_out))
    s5_d = s5_d + rsm["token"]

    d_u, d_wb, d_wc, d_ab, g_s5_d = _s5_bwd(u, y_ssm, d_s5o, s5_d, wb, wc, ab, s5_st, s5_x, Bl, S)

    def diag_in(dw):
        t = dw.reshape(NSG, 8, SGC, 8, SP)
        return jnp.einsum("ab,sacbp->sapc", eye8, t).reshape(NG, SP * SGC)

    def diag_out(dw):
        t = dw.reshape(NSG, 8, SP, 8, SGC)
        return jnp.einsum("ab,sapbc->sacp", eye8, t).reshape(NG, SGC, SP)

    g_s5_c_re = diag_out(d_wc[:, :512])
    g_s5_c_im = -diag_out(d_wc[:, 512:])
    disc_cts = (d_ab[0].reshape(NG, SP), d_ab[1].reshape(NG, SP), diag_in(d_wb[:, :, :512]), diag_in(d_wb[:, :, 512:]))
    g_a_re, g_a_im, g_log_dt, g_b_re, g_b_im = _s5_disc_bwd(*s5_in, disc_cts)

    def post_bwd(yv, rv, kv, vv, gv, do, *pp):
        _, vjp = jax.vjp(lambda *a: _rwkv_post(*a, pp[3]), yv, rv, kv, vv, gv, *pp[:3])
        return vjp(do)

    dy_wkv, dr_b, dk_b, dv_b, dg_, g_ln_g, g_ln_b, g_r_k = rw(
        "rwkv_post_bwd", post_bwd, R=256,
        tiled=[(y_wkv, RW, 0), (r_, RW, 0), (k_, RW, 0), (v_, RW, 0), (g_, RW, 0), (d_o_rwkv, RW, 0)],
        full=post_params, out_tiled=[(RW, F32)] * 5, out_acc=[(1, RW)] * 3)
    dr3, dw3, dk3, dv3, da3, db3 = _wkv_bwd(r_, w_, k_, v_, a_, b_, dy_wkv, ck, tinv, Bl, S)

    shl, shl_moves = share_start("ffn", reduced_halves("ffn", rsl, rsl_moves, dr3))
    shm, shm_moves = share_start("mid", reduced_halves("mid", rsm, rsm_moves, dr3))
    mu_shift = mu_shift + (shl["token"] + shm["token"])

    def prep_bwd(pv, dr1, dr2, dwv, dk1, dk2, dv1, dv2, dav, dbv, dgv, ph, mu, *pp):
        prev = _shift_down(pv, ph, 1)
        ps = pv + (prev - pv) * mu
        _, vjp = jax.vjp(lambda *q: _rwkv_prep(*q, pp[7]), *_split_ps(ps), *pp[:7])
        grads = vjp((dr1 + dr2, dwv, dk1 + dk2, dv1 + dv2, dav, dbv, dgv))
        dps = jnp.concatenate(grads[:5], axis=1)
        return (dps,) + tuple(grads[5:]) + (jnp.sum(dps * (prev - pv), axis=0, keepdims=True),)

    prep_outs = rw(
        "rwkv_prep_bwd", prep_bwd, R=256,
        tiled=[(p, SHIFT, 0), (dr3, RW, 0), (dr_b, RW, 0), (dw3, RW, 0), (dk3, RW, 0), (dk_b, RW, 0),
               (dv3, RW, 0), (dv_b, RW, 0), (da3, RW, 0), (db3, RW, 0), (dg_, RW, 0)],
        prev=[(p, SHIFT, 0)], full=[mu_shift] + prep_params,
        out_tiled=[(SHIFT, F32)],
        out_acc=[(1, RW), (LW + LA, RW), (1, RW), (LW + LA, RW), (LG, RW), (1, RW), (1, RW), (1, SHIFT)])
    d_ps, g_w0, g_w_up_p, g_a0, g_a_up_p, g_g_up, g_k_k, g_k_a, g_mu = prep_outs

    small = {"mu_shift": g_mu, "rwkv_w0": g_w0, "rwkv_a0": g_a0, "rwkv_k_k": g_k_k,
             "rwkv_k_a": g_k_a, "rwkv_r_k": g_r_k, "rwkv_ln_g": g_ln_g, "rwkv_ln_b": g_ln_b, "s5_a_re": g_a_re,
             "s5_a_im": g_a_im, "s5_log_dt": g_log_dt, "s5_b_re": g_b_re, "s5_b_im": g_b_im, "s5_c_re": g_s5_c_re,
             "s5_c_im": g_s5_c_im, "s5_d": g_s5_d, "norm2_g": g_norm2,
             "ffn_conv_b": jnp.concatenate([g_cb_g, g_cb_u], axis=1), "norm_f_g": g_norm_f}
    small_names = list(small)
    g_conv_w = jnp.concatenate([g_cw_g, g_cw_u], axis=1)
    shard_small = {"rwkv_w_up": g_w_up_p[:LW], "rwkv_a_up": g_a_up_p[LW:], "rwkv_g_up": g_g_up, "ffn_conv_w": g_conv_w}
    parts = [small[n] for n in small_names] + [_to_shards(shard_small[n], ax) for n, _, ax in BIG_SMALL]
    spack = _pack_rows(parts, F32, SUBLANES)
    sm_moves = [(0, 0, lambda ref, me, peer: ref, lambda ref, me, k: ref.at[2 * _chip_of(me) + me[2]])]
    sm = _send_start("gsmall_start", ALL_FLIPS, [spack], [sds((8,) + spack.shape, F32)], sm_moves)
    mu_shift = mu_shift + sm["token"]

    def shift_bwd(dps, nx, mu):
        return dps * (1.0 - mu) + _shift_up(dps * mu, nx * mu, 1)

    (d_p,) = rw("shift_bwd", shift_bwd, R=256, tiled=[(d_ps, SHIFT, 0)], nxt=[(d_ps, SHIFT, 0)], full=[mu_shift],
                out_tiled=[(SHIFT, MXU_DTYPE)])
    g_w_in = jnp.concatenate([_mm_tn(h1, d_p, "g_w_p"), _mm_tn(h1, d_u, "g_w_u"), _mm_tn(h1, d_gates, "g_w_g")], axis=1)
    rsn, rsn_moves = reduce_start("in", BIG[:1], (g_w_in,))
    norm1_g = norm1_g + rsn["token"]
    d_h1 = _mm([d_p, d_u, d_gates], [w_p, w_u, w_g], F32, "d_h1", bt=True)

    def norm1_bwd(xv, dh1, dx1v, sc, sh, g):
        _, vjp = jax.vjp(_norm_mod, xv, g, sc, sh)
        dxn, dg, dsc, dsh = vjp(dh1)
        return dx1v + dxn, dsc, dsh, dg

    grad_x, d_sc1, d_sh1, g_norm1 = rw(
        "norm1_bwd", norm1_bwd, R=512, tiled=[(x2d, D, 0), (d_h1, D, 0), (dx1, D, 0)],
        batch=[(mod, D, SC1), (mod, D, SH1)], full=[norm1_g], out_tiled=[(D, F32)], out_batch=[D, D], out_acc=[(1, D)])

    dmod = jnp.concatenate([d_sh1, d_sc1, d_gt1, d_sh2, d_sc2, d_gt2], axis=2).reshape(Bl, 6 * D)
    last_all = _gather_two_level([], [dmod, g_norm1], "gather_dmod")[1]
    dmod_all = last_all[0].reshape(8 * Bl, 6 * D)
    shn, shn_moves = share_start("in", reduced_halves("in", rsn, rsn_moves, dmod_all))
    dmod_cols = lax.dynamic_slice_in_dim(dmod_all, chip * ncol, ncol, 1)
    g_w_ada, g_b_ada = _ada_bwd(c_all, dmod_cols, dmod_all)

    grads = {"norm1_g": _sum_slots(last_all[1].reshape(8, 1, D), F32, "sum_norm1")}
    sm_own, sm_got = _send_wait("gsmall_wait", ALL_FLIPS, sm, sm_moves, g_b_ada)
    s_all = lax.dynamic_update_slice(sm_got[0], sm_own[0][None], (dev, 0, 0))
    s_sum = _sum_slots(s_all, F32, "sum_gsmall").reshape(-1)
    off = 0
    for n in small_names:
        grads[n] = s_sum[off:off + W[n].size].reshape(W[n].shape)
        off += W[n].size
    for n, shape, axis in BIG_SMALL:
        ss = _shard_shape(shape, axis)
        k4 = 4 * math.prod(ss)
        sh4 = s_sum[off:off + k4].reshape(4, math.prod(ss))
        grads[n] = lax.dynamic_index_in_dim(sh4, chip, 0, keepdims=False).reshape((1,) + ss)
        off += k4

    share_finish("ffn", shl, shl_moves, s_sum, BIG_LATE, grads)
    share_finish("mid", shm, shm_moves, grads[BIG_LATE[0][0]], BIG_MID, grads)
    grads["w_ada"] = g_w_ada[None]
    grads["b_ada"] = g_b_ada

    delta, new_m, new_v = {}, {}, {}
    to2 = lambda z: z.reshape(-1, z.shape[-1])

    def adamw(n):
        d_, m_, v2_ = _adamw(to2(W[n]), to2(grads[n]), to2(M[n]), to2(V[n]), "adamw_" + n)
        delta[n], new_m[n], new_v[n] = (z.reshape(W[n].shape) for z in (d_, m_, v2_))

    for n in ["w_ada"] + [b[0] for b in BIG[1:]]:
        adamw(n)
    rest = [n for n in names if n not in delta and n != "w_in"]
    packs = [_pack_rows([src[n] for n in rest], F32, SUBLANES) for src in (W, grads, M, V)]
    d_, m_, v2_ = _adamw(*packs, "adamw_small")
    shapes = [W[n].shape for n in rest]
    for dst, z in ((delta, d_), (new_m, m_), (new_v, v2_)):
        for n, val in zip(rest, _unpack(z.reshape(-1), shapes)):
            dst[n] = val
    share_finish("in", shn, shn_moves, d_, BIG[:1], grads)
    adamw("w_in")

    return (loss, grad_x.reshape(Bl, S, D), *[grads[n] for n in names], *[delta[n] for n in names],
            *[new_m[n] for n in names], *[new_v[n] for n in names])
```

```python
import functools
import math

import jax
import jax.numpy as jnp
from jax import lax
from jax.experimental import pallas as pl
from jax.experimental.pallas import tpu as pltpu

F32 = jnp.float32
BF16 = jnp.bfloat16
MXU_DTYPE = jnp.bfloat16
MESH_IDS = pl.DeviceIdType.MESH
HIGHEST = lax.Precision.HIGHEST

D = 1024
RW, NH, HD = 512, 8, 64
LW, LA, LG = 64, 64, 128
SW, SGC, NG, SP = 512, 16, 32, 64
NSG = 4
SHIFT = 3 * RW + LW + LA + LG
DFF = 2816
RMS_EPS, GN_EPS, L2_EPS = 1e-6, 64e-5, 1e-12
LR, B1, B2, ADAM_EPS, WD, STEP = 0.001, 0.9, 0.999, 1e-8, 0.01, 10
DECAY_SCALE = math.exp(-0.5)
GELU_C = math.sqrt(2.0 / math.pi)

VMEM_LIMIT = 52 * 1024 * 1024
SUBLANES, LANES = 8, 128
HALO = 16


def _pick(n, cap):
    if n <= cap:
        return n
    best = None
    for t in range(LANES, cap + 1, LANES):
        if n % t == 0:
            best = t
    assert best is not None, (n, cap)
    return best


def _params(sem=None, vmem=VMEM_LIMIT):
    return pltpu.CompilerParams(dimension_semantics=sem, vmem_limit_bytes=vmem)


def _chip_of(p):
    return 2 * p[0] + p[1]


def _me():
    return (lax.axis_index("x"), lax.axis_index("y"), lax.axis_index("c"))


def _half(rows, core):
    h = rows // 2
    return pl.ds(pl.multiple_of(core * h, 16 if h % 16 == 0 else SUBLANES), h)


_HBM =pl.BlockSpec(memory_space=pltpu.HBM)
_SEM = pl.BlockSpec(memory_space=pltpu.SEMAPHORE)
_DATAFLOW = pltpu.SideEffectType.DATAFLOW_SIDE_EFFECTING


def _split_copies(flips, moves, src_refs, land_refs, send_sems, recv_sems):
    me = _me()
    nf = len(flips)
    out = []
    for m, (si, li, src_sel, dst_sel) in enumerate(moves):
        for k, f in enumerate(flips):
            peer = tuple(1 - v if b else v for v, b in zip(me, f))
            out.append(pltpu.make_async_remote_copy(
                src_ref=src_sel(src_refs[si], me, peer), dst_ref=dst_sel(land_refs[li], me, k),
                send_sem=send_sems.at[m * nf + k], recv_sem=recv_sems.at[m * nf + k],
                device_id=peer, device_id_type=MESH_IDS))
    return out


def _send_start(name, flips, srcs, land_shapes, moves):
    ns, nl = len(srcs), len(land_shapes)
    n = len(moves) * len(flips)

    def body(*refs):
        for cp in _split_copies(flips, moves, refs[:ns], refs[ns:ns + nl], refs[ns + nl], refs[ns + nl + 1]):
            cp.start()
        refs[-1][...] = jnp.zeros(refs[-1].shape, F32)

    hbm = lambda z: pltpu.with_memory_space_constraint(z, pltpu.HBM)
    lands = [lax.empty(s.shape, s.dtype) for s in land_shapes]
    res = pl.pallas_call(
        body, name=name,
        out_shape=(pltpu.SemaphoreType.DMA((n,)), pltpu.SemaphoreType.DMA((n,)),
                   *[pltpu.HBM(z.shape, z.dtype) for z in srcs], *[pltpu.HBM(s.shape, s.dtype) for s in land_shapes],
                   jax.ShapeDtypeStruct((SUBLANES, LANES), F32)),
        in_specs=[_HBM] * (ns + nl),
        out_specs=(_SEM, _SEM, *[_HBM] * (ns + nl), pl.BlockSpec(memory_space=pltpu.VMEM)),
        input_output_aliases={i: 2 + i for i in range(ns + nl)},
        compiler_params=pltpu.CompilerParams(has_side_effects=_DATAFLOW),
    )(*[hbm(z) for z in srcs], *[hbm(z) for z in lands])
    return {"sems": res[:2], "srcs": list(res[2:2 + ns]), "lands": list(res[2 + ns:2 + ns + nl]), "token": res[-1][0, 0]}


def _send_wait(name, flips, started, moves, after):
    srcs, lands = started["srcs"], started["lands"]
    ns, nl = len(srcs), len(lands)

    def body(*refs):
        for cp in _split_copies(flips, moves, refs[:ns], refs[ns:ns + nl], refs[ns + nl], refs[ns + nl + 1]):
            cp.wait_send()
            cp.wait_recv()

    res = pl.pallas_call(
        body, name=name, out_shape=[pltpu.HBM(z.shape, z.dtype) for z in srcs + lands],
        in_specs=[_HBM] * (ns + nl) + [_SEM, _SEM, pl.BlockSpec(memory_space=pl.ANY)],
        out_specs=[_HBM] * (ns + nl), input_output_aliases={i: i for i in range(ns + nl)},
        compiler_params=pltpu.CompilerParams(has_side_effects=_DATAFLOW),
    )(*srcs, *lands, *started["sems"], after)
    return list(res[:ns]), list(res[ns:])


CHIP_FLIPS = ((1, 0, 0), (0, 1, 0), (1, 1, 0))
PAIR_FLIPS = ((0, 0, 1),)
ALL_FLIPS = CHIP_FLIPS + ((1, 0, 1), (0, 1, 1), (1, 1, 1)) + PAIR_FLIPS


def _gather_two_level(chip_arrs, dev_arrs, name):
    arrs = list(chip_arrs) + list(dev_arrs)
    n, nchip = len(arrs), len(chip_arrs)
    NS = 7

    def body(*refs):
        srcs, outs = refs[:n], refs[n:2 * n]
        send_sems, recv_sems, loc_sems = refs[2 * n:]
        x, y, c = _me()
        sib = (x, y, 1 - c)
        chips = [(1 - x, y), (x, 1 - y), (1 - x, 1 - y)]
        mine = 2 * x + y
        ids = [2 * cx + cy for cx, cy in chips]

        def part(i, slot, core):
            if i < nchip:
                return outs[i].at[slot, _half(arrs[i].shape[0], core)]
            return outs[i].at[slot, core]

        def rcopy(i, k, src, dst, to):
            return pltpu.make_async_remote_copy(src_ref=src, dst_ref=dst, send_sem=send_sems.at[i * NS + k],
                                                recv_sem=recv_sems.at[i * NS + k], device_id=to, device_id_type=MESH_IDS)

        started, locs = [], []
        for i in range(n):
            own = srcs[i].at[_half(arrs[i].shape[0], c)] if i < nchip else srcs[i]
            loc = pltpu.make_async_copy(srcs[i], outs[i].at[mine] if i < nchip else outs[i].at[mine, c], loc_sems.at[i])
            loc.start()
            locs.append(loc)
            for f, chip in enumerate(chips):
                cp = rcopy(i, f, own, part(i, mine, c), (*chip, c))
                cp.start()
                started.append(cp)
            if i >= nchip:
                cp = rcopy(i, 6, own, part(i, mine, c), sib)
                cp.start()
                started.append(cp)
        for i in range(n):
            for f in range(3):
                land = part(i, ids[f], c)
                rcopy(i, f, land, land, sib).wait_recv()
                fw = rcopy(i, 3 + f, land, land, sib)
                fw.start()
                started.append(fw)
        for i in range(n):
            for f in range(3):
                land = part(i, ids[f], 1 - c)
                rcopy(i, 3 + f, land, land, sib).wait_recv()
            if i >= nchip:
                land = part(i, mine, 1 - c)
                rcopy(i, 6, land, land, sib).wait_recv()
        for cp in started:
            cp.wait_send()
        for loc in locs:
            loc.wait()

    outs = [jax.ShapeDtypeStruct((4,) + a.shape, a.dtype) for a in chip_arrs]
    outs += [jax.ShapeDtypeStruct((4, 2) + a.shape, a.dtype) for a in dev_arrs]
    res = pl.pallas_call(
        body, name=name, out_shape=outs,
        in_specs=[pl.BlockSpec(memory_space=pl.ANY)] * n, out_specs=[pl.BlockSpec(memory_space=pl.ANY)] * n,
        scratch_shapes=[pltpu.SemaphoreType.DMA((n * NS,)), pltpu.SemaphoreType.DMA((n * NS,)),
                        pltpu.SemaphoreType.DMA((n,))],
    )(*arrs)
    return res[:nchip], res[nchip:]


def _mm(As, Bs, out_dtype, name, tm=512, cap=1408, bt=False):
    n = len(As)
    M, N = As[0].shape[0], Bs[0].shape[0 if bt else 1]
    if sum(a.shape[1] for a in As) <= 1024:
        tm = 2 * tm
    tm = min(tm, M)
    tn = _pick(N, cap)
    dims = (((1,), (1,)), ((), ())) if bt else (((1,), (0,)), ((), ()))

    def body(*refs):
        o = refs[2 * n]
        acc = None
        for a, b in zip(refs[:n], refs[n:2 * n]):
            d = lax.dot_general(a[...].astype(MXU_DTYPE), b[...].astype(MXU_DTYPE), dims, preferred_element_type=F32)
            acc = d if acc is None else acc + d
        o[...] = acc.astype(o.dtype)

    in_specs = [pl.BlockSpec((tm, a.shape[1]), lambda i, j: (i, 0)) for a in As]
    if bt:
        in_specs += [pl.BlockSpec((tn, b.shape[1]), lambda i, j: (j, 0)) for b in Bs]
    else:
        in_specs += [pl.BlockSpec((b.shape[0], tn), lambda i, j: (0, j)) for b in Bs]
    return pl.pallas_call(
        body, name=name, grid=(M // tm, N // tn), in_specs=in_specs,
        out_specs=pl.BlockSpec((tm, tn), lambda i, j: (i, j)),
        out_shape=jax.ShapeDtypeStruct((M, N), out_dtype),
        compiler_params=_params(("parallel", "parallel")),
    )(*As, *Bs)


def _mm_tn(A, G, name, tt=2048, cap=1408):
    T, Ka = A.shape
    N = G.shape[1]
    tt = min(tt, T)
    tk = _pick(Ka, cap)
    tn = _pick(N, cap)

    def body(a, g, o):
        @pl.when(pl.program_id(2) == 0)
        def _():
            o[...] = jnp.zeros(o.shape, F32)
        o[...] += lax.dot_general(a[...].astype(MXU_DTYPE), g[...].astype(MXU_DTYPE),
                                  (((0,), (0,)), ((), ())), preferred_element_type=F32)

    return pl.pallas_call(
        body, name=name, grid=(Ka // tk, N // tn, T // tt),
        in_specs=[pl.BlockSpec((tt, tk), lambda i, j, t: (t, i)), pl.BlockSpec((tt, tn), lambda i, j, t: (t, j))],
        out_specs=pl.BlockSpec((tk, tn), lambda i, j, t: (i, j)),
        out_shape=jax.ShapeDtypeStruct((Ka, N), F32),
        compiler_params=_params(("parallel", "parallel", "arbitrary")),
    )(A, G)


def _rowwise(name, fn, *, Bl, S, R, tiled=(), prev=(), nxt=(), batch=(), full=(),
             out_tiled=(), out_batch=(), out_acc=()):
    R = min(R, S)
    nS = S // R
    T = Bl * S
    hb = R // HALO
    n_in = len(tiled) + len(prev) + len(nxt) + len(batch) + len(full)

    in_specs, args = [], []
    for a, wd, cb in tiled:
        in_specs.append(pl.BlockSpec((R, wd), lambda b, i, cb=cb: (b * nS + i, cb)))
        args.append(a)
    for a, wd, cb in prev:
        in_specs.append(pl.BlockSpec((HALO, wd), lambda b, i, cb=cb: (jnp.maximum((b * nS + i) * hb - 1, 0), cb)))
        args.append(a)
    for a, wd, cb in nxt:
        in_specs.append(pl.BlockSpec((HALO, wd), lambda b, i, cb=cb: (jnp.minimum((b * nS + i + 1) * hb, T // HALO - 1), cb)))
        args.append(a)
    for a, wd, cb in batch:
        in_specs.append(pl.BlockSpec((1, 1, wd), lambda b, i, cb=cb: (b, 0, cb)))
        args.append(a)
    for a in full:
        in_specs.append(pl.BlockSpec(a.shape, lambda b, i, nd=a.ndim: (0,) * nd))
        args.append(a)

    out_specs, out_shape = [], []
    for C, dt in out_tiled:
        out_specs.append(pl.BlockSpec((R, C), lambda b, i: (b * nS + i, 0)))
        out_shape.append(jax.ShapeDtypeStruct((T, C), dt))
    for C in out_batch:
        out_specs.append(pl.BlockSpec((1, 1, C), lambda b, i: (b, 0, 0)))
        out_shape.append(jax.ShapeDtypeStruct((Bl, 1, C), F32))
    for shp in out_acc:
        out_specs.append(pl.BlockSpec(shp, lambda b, i, nd=len(shp): (0,) * nd))
        out_shape.append(jax.ShapeDtypeStruct(shp, F32))

    nt, npv, nnx, nbt = len(tiled), len(prev), len(nxt), len(batch)

    def body(*refs):
        b, i = pl.program_id(0), pl.program_id(1)
        ins, outs = refs[:n_in], refs[n_in:]
        vals = [r[...] for r in ins[:nt]]
        vals += [jnp.where(i > 0, r[...], jnp.zeros(r.shape, r.dtype)) for r in ins[nt:nt + npv]]
        vals += [jnp.where(i < nS - 1, r[...], jnp.zeros(r.shape, r.dtype)) for r in ins[nt + npv:nt + npv + nnx]]
        vals += [r[0] for r in ins[nt + npv + nnx:nt + npv + nnx + nbt]]
        vals += [r[...] for r in ins[nt + npv + nnx + nbt:]]
        res = fn(*vals)
        if not isinstance(res, (tuple, list)):
            res = (res,)
        k = 0
        for _ in out_tiled:
            outs[k][...] = res[k].astype(outs[k].dtype)
            k += 1
        for _ in out_batch:
            o = outs[k]

            @pl.when(i == 0)
            def _(o=o):
                o[...] = jnp.zeros(o.shape, F32)
            o[0] += res[k]
            k += 1
        for _ in out_acc:
            o = outs[k]

            @pl.when((i == 0) & (b == 0))
            def _(o=o):
                o[...] = jnp.zeros(o.shape, F32)
            o[...] += res[k]
            k += 1

    out = pl.pallas_call(
        body, name=name, grid=(Bl, nS), in_specs=in_specs, out_specs=out_specs, out_shape=out_shape,
        compiler_params=_params(("arbitrary", "arbitrary")),
    )(*args)
    return out


def _colwise(name, fn, *, Bl, S, R, W, strip, tiled=(), prev=(), nxt=(), full=(), out_tiled=(), n_acc=0):
    R = min(R, S)
    nS = S // R
    T = Bl * S
    hb = R // HALO
    nt, npv, nnx, nfl = len(tiled), len(prev), len(nxt), len(full)
    n_in = nt + npv + nnx + nfl
    in_specs = [pl.BlockSpec((R, W), lambda b, i, cb=cb: (b * nS + i, cb)) for _, cb in tiled]
    in_specs += [pl.BlockSpec((HALO, W), lambda b, i, cb=cb: (jnp.maximum((b * nS + i) * hb - 1, 0), cb)) for _, cb in prev]
    in_specs += [pl.BlockSpec((HALO, W), lambda b, i, cb=cb: (jnp.minimum((b * nS + i + 1) * hb, T // HALO - 1), cb))
                 for _, cb in nxt]
    in_specs += [pl.BlockSpec(a.shape, lambda b, i: (0, 0)) for a in full]
    out_specs = [pl.BlockSpec((R, m * W), lambda b, i: (b * nS + i, 0)) for m, _ in out_tiled]
    out_specs += [pl.BlockSpec((1, W), lambda b, i: (0, 0))] * n_acc
    out_shape = [jax.ShapeDtypeStruct((T, m * W), dt) for m, dt in out_tiled] + [jax.ShapeDtypeStruct((1, W), F32)] * n_acc

    def body(*refs):
        b, i = pl.program_id(0), pl.program_id(1)
        ins, outs = refs[:n_in], refs[n_in:]

        @pl.when((i == 0) & (b == 0))
        def _():
            for o in outs[len(out_tiled):]:
                o[...] = jnp.zeros(o.shape, F32)

        def col(j, carry):
            cs = pl.ds(pl.multiple_of(j * strip, strip), strip)
            vals = [r[:, cs] for r in ins[:nt]]
            vals += [jnp.where(i > 0, r[:, cs], jnp.zeros((HALO, strip), r.dtype)) for r in ins[nt:nt + npv]]
            vals += [jnp.where(i < nS - 1, r[:, cs], jnp.zeros((HALO, strip), r.dtype)) for r in ins[nt + npv:nt + npv + nnx]]
            vals += [r[:, cs] for r in ins[nt + npv + nnx:]]
            res = fn(*vals)
            for k, (m, _) in enumerate(out_tiled):
                for q in range(m):
                    outs[k][:, pl.ds(pl.multiple_of(q * W + j * strip, strip), strip)] = res[k][q].astype(outs[k].dtype)
            for k in range(len(out_tiled), len(outs)):
                outs[k][:, cs] += res[k]
            return carry

        lax.fori_loop(0, W // strip, col, 0)

    return pl.pallas_call(
        body, name=name, grid=(Bl, nS), in_specs=in_specs, out_specs=out_specs, out_shape=out_shape,
        compiler_params=_params(("arbitrary", "arbitrary")),
    )(*[a for a, _ in tiled], *[a for a, _ in prev], *[a for a, _ in nxt], *full)


def _shift_down(x, halo, k):
    rolled = pltpu.roll(x, k, 0)
    row = lax.broadcasted_iota(jnp.int32, (SUBLANES, x.shape[1]), 0)
    head = rolled[0:SUBLANES]
    for j in range(k):
        head = jnp.where(row == j, halo[HALO - k + j:HALO - k + j + 1, :], head)
    return jnp.concatenate([head, rolled[SUBLANES:]], axis=0)


def _shift_up(x, halo, k):
    n = x.shape[0]
    rolled = pltpu.roll(x, n - k, 0)
    row = lax.broadcasted_iota(jnp.int32, (SUBLANES, x.shape[1]), 0)
    tail = rolled[n - SUBLANES:]
    for j in range(k):
        tail = jnp.where(row == SUBLANES - k + j, halo[j:j + 1, :], tail)
    return jnp.concatenate([rolled[:n - SUBLANES], tail], axis=0)


def _dotm(a, b):
    return jnp.dot(a.astype(MXU_DTYPE), b.astype(MXU_DTYPE), preferred_element_type=F32)


def _split_bf16(x):
    hi = x.astype(BF16)
    return hi, (x - hi.astype(F32)).astype(BF16)


def _headsum_2pass(x, hm):
    hi, lo = _split_bf16(x)
    hb = hm.astype(BF16)
    return jnp.dot(hi, hb, preferred_element_type=F32) + jnp.dot(lo, hb, preferred_element_type=F32)


@jax.custom_vjp
def _headsum(x, hm):
    return _headsum_2pass(x, hm)


_headsum.defvjp(lambda x, hm: (_headsum_2pass(x, hm), hm),
                lambda hm, g: (_headsum_2pass(g, hm), jnp.zeros_like(hm)))


def _sigmoid(x):
    return 0.5 * jnp.tanh(0.5 * x) + 0.5


def _rms(x, g):
    return x * lax.rsqrt(jnp.mean(x * x, axis=-1, keepdims=True) + RMS_EPS) * g


def _norm_mod(x, g, sc, sh):
    return _rms(x, g) * (1.0 + sc) + sh


def _split_ps(ps):
    return (ps[:, 0:RW], ps[:, RW:2 * RW], ps[:, 2 * RW:3 * RW], ps[:, 3 * RW:3 * RW + LW + LA],
            ps[:, 3 * RW + LW + LA:SHIFT])


def _rwkv_prep(r, k, v, wa, gd, w0, w_up_p, a0, a_up_p, g_up, k_k, k_a, hm):
    w_raw = w0 + _dotm(jnp.tanh(wa), w_up_p)
    decay = jnp.exp(-DECAY_SCALE * _sigmoid(w_raw))
    a = _sigmoid(a0 + _dotm(wa, a_up_p))
    g = _dotm(_sigmoid(gd), g_up)
    kk = k * k_k
    kk = kk * lax.rsqrt(_headsum(kk * kk, hm) + L2_EPS)
    k2 = k * (1.0 + (a - 1.0) * k_a)
    return r, decay, k2, v, -kk, kk * a, g


def _rwkv_post(y, r, k2, v, g, ln_g, ln_b, r_k, hm):
    mean = _headsum(y, hm) * (1.0 / HD)
    yc = y - mean
    var = _headsum(yc * yc, hm) * (1.0 / HD)
    yn = yc * lax.rsqrt(var + GN_EPS) * ln_g + ln_b
    bonus = _headsum(r * k2 * r_k, hm) * v
    return (yn + bonus) * g


def _gelu(x):
    return 0.5 * x * (1.0 + jnp.tanh(GELU_C * (x + 0.044715 * (x * x * x))))


def _s5_post(yssm, u, d):
    return _gelu(yssm + d * u)


def _mix(ga, gb, ya, za, zb):
    return _sigmoid(ga) * ya + _sigmoid(gb) * (za * _sigmoid(zb))


def _conv_act(up_g, up_u, hg, hu, w_g, w_u, b_g, b_u):
    gate, upv = _conv3(up_g, hg, w_g, b_g)[0], _conv3(up_u, hu, w_u, b_u)[0]
    return gate, upv


def _conv3(x, h, w, b):
    x, h = x.astype(F32), h.astype(F32)
    s2, s1 = _shift_down(x, h, 2), _shift_down(x, h, 1)
    return b + w[0:1] * s2 + w[1:2] * s1 + w[2:3] * x, (s2, s1, x)


def _silu_gate(gate, upv):
    return gate * _sigmoid(gate) * upv


WKV_L = 64
WKV_KEPT = 5
_NT, _NN, _TN = ((1,), (1,)), ((1,), (0,)), ((0,), (0,))


def _dotw(x, y, dims):
    return lax.dot_general(x.astype(MXU_DTYPE), y.astype(MXU_DTYPE), (dims, ((), ())), preferred_element_type=F32)


def _dot3(x, y, dims):
    (xh, xl), (yh, yl) = _split_bf16(x), _split_bf16(y)
    d = lambda p, q: lax.dot_general(p, q, (dims, ((), ())), preferred_element_type=F32)
    return d(xh, yh) + d(xh, yl) + d(xl, yh)


@jax.custom_vjp
def _gram3(x, y):
    return _dot3(x, y, _NT)


_gram3.defvjp(lambda x, y: (_dot3(x, y, _NT), (x, y)),
              lambda res, g: (_dot3(g, res[1], _NN), _dot3(g, res[0], _TN)))


@jax.custom_vjp
def _gram_known(x, y, value):
    return value


_gram_known.defvjp(lambda x, y, value: (value, (x, y, value)),
                   lambda res, g: (_dot3(g, res[1], _NN), _dot3(g, res[0], _TN), jnp.zeros_like(res[2])))


def _tri_solve_fwd(ns, xs):
    each = lambda f, *ls: tuple(f(*zs) for zs in zip(*ls))
    size = ns[0].shape[0]
    eye = (lax.broadcasted_iota(jnp.int32, (size, size), 0) == lax.broadcasted_iota(jnp.int32, (size, size), 1)).astype(F32)
    ts = each(lambda n: n + eye, ns)
    qs = ns
    for _ in range(WKV_L.bit_length() - 2):
        qs = each(lambda q: _dotw(q, q, _NN), qs)
        ts = each(lambda t, q: t + _dotw(t, q, _NN), ts, qs)
    us = each(lambda t, x: _dotw(t, x, _NN), ts, xs)
    return us, (ts, us)


def _tri_solve_bwd(res, dus):
    ts, us = res
    each = lambda f, *ls: tuple(f(*zs) for zs in zip(*ls))
    dxs = each(lambda t, du: _dotw(t, du, _TN), ts, dus)
    return each(lambda dx, u: _dotw(dx, u, _NT), dxs, us), dxs


@jax.custom_vjp
def _tri_solve(ns, xs):
    return _tri_solve_fwd(ns, xs)[0]


_tri_solve.defvjp(_tri_solve_fwd, _tri_solve_bwd)


@jax.custom_vjp
def _tri_known(ns, xs, ts, us):
    return us


_tri_known.defvjp(lambda ns, xs, ts, us: (us, (ts, us)),
                  lambda res, dus: _tri_solve_bwd(res, dus) + tuple(tuple(jnp.zeros_like(z) for z in r) for r in res))


def _wkv_chunk(s0, r, w, k, v, a, b):
    y, s1 = _wkv_chunks((s0,), (r,), (w,), (k,), (v,), (a,), (b,))
    return y[0], s1[0]


def _wkv_chunks(s0, r, w, k, v, a, b, tinv=None, want_tinv=False):
    each = lambda f, *ls: tuple(f(*xs) for xs in zip(*ls))
    L = r[0].shape[0]
    n2 = 2 * L
    lane_head = lax.broadcasted_iota(jnp.int32, (2, 1, 2 * HD), 2) // HD
    head_mask = (lane_head == lax.broadcasted_iota(jnp.int32, (2, 1, 2 * HD), 0)).astype(F32)
    ri = lax.broadcasted_iota(jnp.int32, (n2, n2), 0)
    ci = lax.broadcasted_iota(jnp.int32, (n2, n2), 1)
    same = (ri // L) == (ci // L)
    strict = same & ((ci % L) < (ri % L))
    incl = same & ((ci % L) <= (ri % L))
    si = lax.broadcasted_iota(jnp.int32, (2 * HD, 2 * HD), 0) // HD
    sj = lax.broadcasted_iota(jnp.int32, (2 * HD, 2 * HD), 1) // HD
    tri = (lax.broadcasted_iota(jnp.int32, (L, L), 0) >= lax.broadcasted_iota(jnp.int32, (L, L), 1)).astype(F32)

    stack = lambda z: (z[None] * head_mask).reshape(n2, 2 * HD)
    dup = lambda z: jnp.broadcast_to(z[None], (2, L, 2 * HD)).reshape(n2, 2 * HD)
    gram = _gram3
    nt, nn, tn = (lambda x, y, d=d: _dotw(x, y, d) for d in (_NT, _NN, _TN))
    add = lambda x, y: x + y

    lw = each(jnp.log, w)
    cum = each(lambda z: jnp.dot(tri, z, preferred_element_type=F32, precision=HIGHEST), lw)
    tot = each(lambda z: jnp.sum(z, axis=0, keepdims=True), lw)
    a2 = each(lambda av, cv, lv: stack(av * jnp.exp(cv - lv)), a, cum, lw)
    r2 = each(lambda rv, cv: stack(rv * jnp.exp(cv)), r, cum)
    v2 = each(stack, v)
    b2 = each(lambda bv, cv: dup(bv * jnp.exp(-cv)), b, cum)
    k2 = each(lambda kv, cv: dup(kv * jnp.exp(-cv)), k, cum)
    n_ab = each(lambda x, y: jnp.where(strict, gram(x, y), 0.0), a2, b2)
    if tinv is None:
        n_ak = each(lambda x, y: jnp.where(strict, gram(x, y), 0.0), a2, k2)
        m_rb = each(lambda x, y: jnp.where(incl, gram(x, y), 0.0), r2, b2)
        m_rk = each(lambda x, y: jnp.where(incl, gram(x, y), 0.0), r2, k2)
    else:
        tinv, k_u, k_ak, k_rb, k_rk = tinv
        n_ak = each(lambda x, y, g: jnp.where(strict, _gram_known(x, y, g), 0.0), a2, k2, k_ak)
        m_rb = each(lambda x, y, g: jnp.where(incl, _gram_known(x, y, g), 0.0), r2, b2, k_rb)
        m_rk = each(lambda x, y, g: jnp.where(incl, _gram_known(x, y, g), 0.0), r2, k2, k_rk)
    x = each(add, each(nt, a2, s0), each(nn, n_ak, v2))
    if want_tinv:
        u, (tinv, _) = _tri_solve_fwd(n_ab, x)
        tinv = (tinv, u, n_ak, m_rb, m_rk)
    else:
        u = _tri_solve(n_ab, x) if tinv is None else _tri_known(n_ab, x, tinv, k_u)
    y2 = each(lambda x, y, z: x + y + z, each(nt, r2, s0), each(nn, m_rb, u), each(nn, m_rk, v2))
    y = each(lambda z: jnp.sum(z.reshape(2, L, 2 * HD), axis=0), y2)
    b3 = each(lambda bv, tv, cv: dup(bv * jnp.exp(tv - cv)), b, tot, cum)
    k3 = each(lambda kv, tv, cv: dup(kv * jnp.exp(tv - cv)), k, tot, cum)
    upd = each(add, each(tn, u, b3), each(tn, v2, k3))
    s1 = each(lambda sv, tv, uv: sv * jnp.exp(tv) + jnp.where(si == sj, uv, 0.0), s0, tot, upd)
    return (y, s1, tinv) if want_tinv else (y, s1)


NPAIR = NH // 2


def _wkv_nb(Bl):
    return 4 if Bl % 4 == 0 else 2 if Bl % 2 == 0 else 1


def _wkv_fwd(r, w, k, v, a, b, Bl, S):
    L = WKV_L
    nC = S // L
    nb = _wkv_nb(Bl)
    chains = [(bi, p, slice(p * 2 * HD, (p + 1) * 2 * HD)) for bi in range(nb) for p in range(NPAIR)]

    def body(r_ref, w_ref, k_ref, v_ref, a_ref, b_ref, y_ref, ck_ref, ti_ref, s_ref):
        @pl.when(pl.program_id(1) == 0)
        def _():
            s_ref[...] = jnp.zeros(s_ref.shape, F32)
        s0 = tuple(s_ref[bi, p] for bi, p, _ in chains)
        ops = [tuple(z[bi, :, cs] for bi, _, cs in chains) for z in (r_ref, w_ref, k_ref, v_ref, a_ref, b_ref)]
        y, s1, kept = _wkv_chunks(s0, *ops, want_tinv=True)
        for i, (bi, p, cs) in enumerate(chains):
            ck_ref[bi, 0, p] = s0[i]
            for q in range(WKV_KEPT):
                ti_ref[bi, 0, p, q] = kept[q][i]
            y_ref[bi, :, cs] = y[i]
            s_ref[bi, p] = s1[i]

    to3 = lambda z: z.reshape(Bl, S, RW)
    row_spec = pl.BlockSpec((nb, L, RW), lambda g, c: (g, c, 0))
    mats = jax.ShapeDtypeStruct((Bl, nC, NPAIR, 2 * HD, 2 * HD), F32)
    mat_spec = pl.BlockSpec((nb, 1, NPAIR, 2 * HD, 2 * HD), lambda g, c: (g, c, 0, 0, 0))
    y, ck, ti = pl.pallas_call(
        body, name="wkv_fwd", grid=(Bl // nb, nC), in_specs=[row_spec] * 6,
        out_specs=[row_spec, mat_spec,
                   pl.BlockSpec((nb, 1, NPAIR, WKV_KEPT, 2 * HD, 2 * HD), lambda g, c: (g, c, 0, 0, 0, 0))],
        out_shape=[jax.ShapeDtypeStruct((Bl, S, RW), F32), mats,
                   jax.ShapeDtypeStruct((Bl, nC, NPAIR, WKV_KEPT, 2 * HD, 2 * HD), F32)],
        scratch_shapes=[pltpu.VMEM((nb, NPAIR, 2 * HD, 2 * HD), F32)],
        compiler_params=_params(("arbitrary", "arbitrary")),
    )(*(to3(z) for z in (r, w, k, v, a, b)))
    return y.reshape(Bl * S, RW), ck, ti


def _wkv_bwd(r, w, k, v, a, b, dy, ck, ti, Bl, S):
    L = WKV_L
    nC = S // L
    nb = _wkv_nb(Bl)
    chains = [(bi, p, slice(p * 2 * HD, (p + 1) * 2 * HD)) for bi in range(nb) for p in range(NPAIR)]

    def body(r_ref, w_ref, k_ref, v_ref, a_ref, b_ref, dy_ref, ck_ref, ti_ref,
             dr_ref, dw_ref, dk_ref, dv_ref, da_ref, db_ref, ds_ref):
        @pl.when(pl.program_id(1) == 0)
        def _():
            ds_ref[...] = jnp.zeros(ds_ref.shape, F32)
        s0 = tuple(ck_ref[bi, 0, p] for bi, p, _ in chains)
        tinv = tuple(tuple(ti_ref[bi, 0, p, q] for bi, p, _ in chains) for q in range(WKV_KEPT))
        ops = [tuple(z[bi, :, cs] for bi, _, cs in chains) for z in (r_ref, w_ref, k_ref, v_ref, a_ref, b_ref)]
        cts = (tuple(dy_ref[bi, :, cs] for bi, _, cs in chains), tuple(ds_ref[bi, p] for bi, p, _ in chains))
        ds0, *grads = jax.vjp(lambda *z: _wkv_chunks(*z, tinv=tinv), s0, *ops)[1](cts)
        for i, (bi, p, cs) in enumerate(chains):
            ds_ref[bi, p] = ds0[i]
            for o, g in zip((dr_ref, dw_ref, dk_ref, dv_ref, da_ref, db_ref), grads):
                o[bi, :, cs] = g[i]

    to3 = lambda z: z.reshape(Bl, S, RW)
    row_spec = pl.BlockSpec((nb, L, RW), lambda g, c: (g, nC - 1 - c, 0))
    rows = jax.ShapeDtypeStruct((Bl, S, RW), F32)
    mat_spec = pl.BlockSpec((nb, 1, NPAIR, 2 * HD, 2 * HD), lambda g, c: (g, nC - 1 - c, 0, 0, 0))
    outs = pl.pallas_call(
        body, name="wkv_bwd", grid=(Bl // nb, nC),
        in_specs=[row_spec] * 7 + [mat_spec, pl.BlockSpec((nb, 1, NPAIR, WKV_KEPT, 2 * HD, 2 * HD),
                                                          lambda g, c: (g, nC - 1 - c, 0, 0, 0, 0))],
        out_specs=[row_spec] * 6, out_shape=[rows] * 6,
        scratch_shapes=[pltpu.VMEM((nb, NPAIR, 2 * HD, 2 * HD), F32)],
        compiler_params=_params(("arbitrary", "arbitrary")),
    )(*(to3(z) for z in (r, w, k, v, a, b, dy)), ck, ti)
    return [o.reshape(Bl * S, RW) for o in outs]


NST = NG * SP


def _cmul(ar, ai, br, bi):
    return ar * br - ai * bi, ar * bi + ai * br


def _s5_tiles(are, aim, reverse):
    if reverse:
        aim = -aim
    row = lax.broadcasted_iota(jnp.int32, (SUBLANES, NST), 0)
    pw = [(are, aim)]
    for _ in range(SUBLANES - 1):
        pw.append(_cmul(pw[-1][0], pw[-1][1], are, aim))
    bc = lambda z: jnp.broadcast_to(z, (SUBLANES, NST))
    ms = []
    for kk in (1, 2, 4):
        cond = (row < SUBLANES - kk) if reverse else (row >= kk)
        ms.append((jnp.where(cond, bc(pw[kk - 1][0]), 0.0), jnp.where(cond, bc(pw[kk - 1][1]), 0.0)))
    pr = jnp.zeros((SUBLANES, NST), F32)
    pi = jnp.zeros((SUBLANES, NST), F32)
    for i in range(SUBLANES):
        n = SUBLANES - i if reverse else i + 1
        pr = jnp.where(row == i, bc(pw[n - 1][0]), pr)
        pi = jnp.where(row == i, bc(pw[n - 1][1]), pi)
    return ms, (pr, pi)


def _s5_block(re, im, ms, pc, cre, cim, sg, reverse):
    ln = slice(sg * 512, (sg + 1) * 512)
    for (mr, mi), kk in zip(ms, (1, 2, 4)):
        sh = SUBLANES - kk if reverse else kk
        sre, sim = pltpu.roll(re, sh, 0), pltpu.roll(im, sh, 0)
        tr, ti = _cmul(mr[:, ln], mi[:, ln], sre, sim)
        re, im = re + tr, im + ti
    tr, ti = _cmul(pc[0][:, ln], pc[1][:, ln], cre[:, ln], cim[:, ln])
    return re + tr, im + ti


def _s5_scan(X_ref, n_rows, ms, pc, cre, cim, reverse, visit=None, acc0=None):
    nblk = n_rows // SUBLANES

    def it(i, carry):
        cre, cim, acc = carry
        j = nblk - 1 - i if reverse else i
        rows = pl.ds(pl.multiple_of(j * SUBLANES, SUBLANES), SUBLANES)
        edge = 0 if reverse else SUBLANES - 1
        blocks, ncre, ncim = [], [], []
        for sg in range(NSG):
            lr = slice(sg * 1024, sg * 1024 + 512)
            li = slice(sg * 1024 + 512, (sg + 1) * 1024)
            re, im = _s5_block(X_ref[rows, lr], X_ref[rows, li], ms, pc, cre, cim, sg, reverse)
            X_ref[rows, lr] = re
            X_ref[rows, li] = im
            blocks.append((re, im))
            ncre.append(re[edge:edge + 1])
            ncim.append(im[edge:edge + 1])
        if visit is not None:
            acc = visit(j, blocks, acc)
        return jnp.concatenate(ncre, axis=1), jnp.concatenate(ncim, axis=1), acc

    return lax.fori_loop(0, nblk, it, (cre, cim, acc0 if acc0 is not None else 0))


def _s5_fwd(u, wb, wc, ab, d, Bl, S, R=256):
    R = min(R, S)
    nC = S // R

    def body(u_ref, wb_ref, wc_ref, ab_ref, d_ref, y_ref, st_ref, X_ref, o_ref, car_ref):
        @pl.when(pl.program_id(1) == 0)
        def _():
            car_ref[...] = jnp.zeros(car_ref.shape, F32)
        st_ref[0, 0] = car_ref[...]
        ms, pc = _s5_tiles(ab_ref[0:1], ab_ref[1:2], False)
        for sg in range(NSG):
            X_ref[:, sg * 1024:(sg + 1) * 1024] = _dotm(u_ref[:, sg * 128:(sg + 1) * 128], wb_ref[sg])
        cre, cim, _ = _s5_scan(X_ref, R, ms, pc, car_ref[0:1], car_ref[1:2], False)
        car_ref[0:1] = cre
        car_ref[1:2] = cim
        for sg in range(NSG):
            y_ref[:, sg * 128:(sg + 1) * 128] = _dotm(X_ref[:, sg * 1024:(sg + 1) * 1024], wc_ref[sg])
        o_ref[...] = _s5_post(y_ref[...], u_ref[...], d_ref[...]).astype(o_ref.dtype)

    rows = pl.BlockSpec((R, SW), lambda b, c: (b * nC + c, 0))
    return pl.pallas_call(
        body, name="s5_fwd", grid=(Bl, nC),
        in_specs=[rows, pl.BlockSpec(wb.shape, lambda b, c: (0, 0, 0)), pl.BlockSpec(wc.shape, lambda b, c: (0, 0, 0)),
                  pl.BlockSpec(ab.shape, lambda b, c: (0, 0)), pl.BlockSpec(d.shape, lambda b, c: (0, 0))],
        out_specs=[rows, pl.BlockSpec((1, 1, 2, NST), lambda b, c: (b, c, 0, 0)),
                   pl.BlockSpec((R, 2 * NST), lambda b, c: (b * nC + c, 0)), rows],
        out_shape=[jax.ShapeDtypeStruct((Bl * S, SW), F32), jax.ShapeDtypeStruct((Bl, nC, 2, NST), F32),
                   jax.ShapeDtypeStruct((Bl * S, 2 * NST), F32), jax.ShapeDtypeStruct((Bl * S, SW), MXU_DTYPE)],
        scratch_shapes=[pltpu.VMEM((2, NST), F32)],
        compiler_params=_params(("arbitrary", "arbitrary")),
    )(u, wb, wc, ab, d)


def _s5_bwd(u, y, do, d, wb, wc, ab, st, xs, Bl, S, R=256):
    R = min(R, S)
    nC = S // R

    def body(u_ref, y_ref, do_ref, d_ref, wb_ref, wc_ref, ab_ref, st_ref, X_ref,
             du_ref, dwb_ref, dwc_ref, dab_ref, dd_ref, G_ref, car_ref):
        first = (pl.program_id(0) == 0) & (pl.program_id(1) == 0)

        @pl.when(first)
        def _():
            for o in (dwb_ref, dwc_ref, dab_ref, dd_ref):
                o[...] = jnp.zeros(o.shape, F32)

        @pl.when(pl.program_id(1) == 0)
        def _():
            car_ref[...] = jnp.zeros(car_ref.shape, F32)

        are, aim = ab_ref[0:1], ab_ref[1:2]
        dy, du_direct, dd = jax.vjp(_s5_post, y_ref[...], u_ref[...], d_ref[...])[1](do_ref[...])
        dd_ref[...] += dd
        dyv = dy.astype(MXU_DTYPE)
        for sg in range(NSG):
            G_ref[:, sg * 1024:(sg + 1) * 1024] = lax.dot_general(
                dyv[:, sg * 128:(sg + 1) * 128], wc_ref[sg].astype(MXU_DTYPE), (((1,), (1,)), ((), ())),
                preferred_element_type=F32)
        rms_, rpc = _s5_tiles(are, aim, True)
        row = lax.broadcasted_iota(jnp.int32, (SUBLANES, 512), 0)

        def visit(j, blocks, acc):
            before = pl.multiple_of(jnp.maximum(j - 1, 0) * SUBLANES, SUBLANES)
            prow = X_ref[pl.ds(before, SUBLANES), :][SUBLANES - 1:SUBLANES]
            rows = pl.ds(pl.multiple_of(j * SUBLANES, SUBLANES), SUBLANES)
            are_acc, aim_acc = [], []
            for sg in range(NSG):
                lr = slice(sg * 1024, sg * 1024 + 512)
                li = slice(sg * 1024 + 512, (sg + 1) * 1024)
                ln = slice(sg * 512, (sg + 1) * 512)
                pre = jnp.where(j > 0, prow[:, lr], st_ref[0, 0, 0:1, ln])
                pim = jnp.where(j > 0, prow[:, li], st_ref[0, 0, 1:2, ln])
                xre = jnp.where(row == 0, pre, pltpu.roll(X_ref[rows, lr], 1, 0))
                xim = jnp.where(row == 0, pim, pltpu.roll(X_ref[rows, li], 1, 0))
                dre, dim = blocks[sg]
                are_acc.append(dre * xre + dim * xim)
                aim_acc.append(dim * xre - dre * xim)
            return acc[0] + jnp.concatenate(are_acc, axis=1), acc[1] + jnp.concatenate(aim_acc, axis=1)

        zero = jnp.zeros((SUBLANES, NST), F32)
        cre, cim, acc = _s5_scan(G_ref, R, rms_, rpc, car_ref[0:1], car_ref[1:2], True, visit, (zero, zero))
        car_ref[0:1] = cre
        car_ref[1:2] = cim
        dab_ref[0:1] += jnp.sum(acc[0], axis=0, keepdims=True)
        dab_ref[1:2] += jnp.sum(acc[1], axis=0, keepdims=True)
        uv = u_ref[...].astype(MXU_DTYPE)
        for sg in range(NSG):
            cs = slice(sg * 1024, (sg + 1) * 1024)
            us = slice(sg * 128, (sg + 1) * 128)
            gx = G_ref[:, cs].astype(MXU_DTYPE)
            dwb_ref[sg] += lax.dot_general(uv[:, us], gx, (((0,), (0,)), ((), ())), preferred_element_type=F32)
            dwc_ref[sg] += lax.dot_general(X_ref[:, cs].astype(MXU_DTYPE), dyv[:, us], (((0,), (0,)), ((), ())),
                                           preferred_element_type=F32)
            du_ssm = lax.dot_general(gx, wb_ref[sg].astype(MXU_DTYPE), (((1,), (1,)), ((), ())),
                                     preferred_element_type=F32)
            du_ref[:, us] = (du_ssm + du_direct[:, us]).astype(du_ref.dtype)

    rmap = lambda b, c: (b * nC + nC - 1 - c, 0)
    rows = pl.BlockSpec((R, SW), rmap)
    return pl.pallas_call(
        body, name="s5_bwd", grid=(Bl, nC),
        in_specs=[rows, rows, rows, pl.BlockSpec(d.shape, lambda b, c: (0, 0)),
                  pl.BlockSpec(wb.shape, lambda b, c: (0, 0, 0)), pl.BlockSpec(wc.shape, lambda b, c: (0, 0, 0)),
                  pl.BlockSpec(ab.shape, lambda b, c: (0, 0)),
                  pl.BlockSpec((1, 1, 2, NST), lambda b, c: (b, nC - 1 - c, 0, 0)),
                  pl.BlockSpec((R, 2 * NST), rmap)],
        out_specs=[rows, pl.BlockSpec(wb.shape, lambda b, c: (0, 0, 0)),
                   pl.BlockSpec(wc.shape, lambda b, c: (0, 0, 0)), pl.BlockSpec((2, NST), lambda b, c: (0, 0)),
                   pl.BlockSpec(d.shape, lambda b, c: (0, 0))],
        out_shape=[jax.ShapeDtypeStruct((Bl * S, SW), MXU_DTYPE), jax.ShapeDtypeStruct(wb.shape, F32),
                   jax.ShapeDtypeStruct(wc.shape, F32), jax.ShapeDtypeStruct((2, NST), F32),
                   jax.ShapeDtypeStruct(d.shape, F32)],
        scratch_shapes=[pltpu.VMEM((R, 2 * NST), F32), pltpu.VMEM((2, NST), F32)],
        compiler_params=_params(("arbitrary", "arbitrary")),
    )(u, y, do, d, wb, wc, ab, st, xs)


def _s5_disc_math(a_re, a_im, log_dt, b_re, b_im, expand):
    dt = jnp.exp(log_dt)
    z_re, z_im = a_re * dt, a_im * dt
    mag = jnp.exp(z_re)
    ab_re, ab_im = mag * jnp.cos(z_im), mag * jnp.sin(z_im)
    den = a_re * a_re + a_im * a_im
    q_re = ((ab_re - 1.0) * a_re + ab_im * a_im) / den
    q_im = (ab_im * a_re - (ab_re - 1.0) * a_im) / den
    qe_re = jnp.dot(q_re, expand, preferred_element_type=F32, precision=HIGHEST)
    qe_im = jnp.dot(q_im, expand, preferred_element_type=F32, precision=HIGHEST)
    return ab_re, ab_im, qe_re * b_re - qe_im * b_im, qe_re * b_im + qe_im * b_re


def _whole(shape):
    return pl.BlockSpec(shape, lambda nd=len(shape): (0,) * nd)


def _s5_disc(a_re, a_im, log_dt, b_re, b_im, expand):
    def body(a, b, c, d, e, f, o0, o1, o2, o3):
        res = _s5_disc_math(a[...], b[...], c[...], d[...], e[...], f[...])
        for o, v in zip((o0, o1, o2, o3), res):
            o[...] = v
    ins = (a_re, a_im, log_dt, b_re, b_im, expand)
    outs = [jax.ShapeDtypeStruct(a_re.shape, F32)] * 2 + [jax.ShapeDtypeStruct(b_re.shape, F32)] * 2
    return pl.pallas_call(body, name="s5_disc", in_specs=[_whole(x.shape) for x in ins],
                          out_specs=[_whole(o.shape) for o in outs], out_shape=outs)(*ins)


def _s5_disc_bwd(a_re, a_im, log_dt, b_re, b_im, expand, cts):
    def body(a, b, c, d, e, f, g0, g1, g2, g3, o0, o1, o2, o3, o4):
        fn = lambda *p: _s5_disc_math(*p, f[...])
        _, vjp = jax.vjp(fn, a[...], b[...], c[...], d[...], e[...])
        for o, v in zip((o0, o1, o2, o3, o4), vjp((g0[...], g1[...], g2[...], g3[...]))):
            o[...] = v
    ins = (a_re, a_im, log_dt, b_re, b_im, expand) + tuple(cts)
    outs = [jax.ShapeDtypeStruct(x.shape, F32) for x in (a_re, a_im, log_dt, b_re, b_im)]
    return pl.pallas_call(body, name="s5_disc_bwd", in_specs=[_whole(x.shape) for x in ins],
                          out_specs=[_whole(o.shape) for o in outs], out_shape=outs)(*ins)


def _ada_fwd(c_all, w_shard, b_shard):
    def body(c_ref, w_ref, b_ref, o_ref):
        cv = c_ref[...]
        o_ref[...] = _dotm(cv * _sigmoid(cv), w_ref[...]) + b_ref[...]
    n = w_shard.shape[1]
    return pl.pallas_call(
        body, name="ada_fwd", in_specs=[_whole(c_all.shape), _whole(w_shard.shape), _whole(b_shard.shape)],
        out_specs=_whole((c_all.shape[0], n)), out_shape=jax.ShapeDtypeStruct((c_all.shape[0], n), F32),
        compiler_params=_params(),
    )(c_all, w_shard, b_shard)


def _ada_bwd(c_all, dmod_cols, dmod_all):
    def body(c_ref, dc_ref, da_ref, gw_ref, gb_ref):
        cv = c_ref[...]
        gw_ref[...] = lax.dot_general((cv * _sigmoid(cv)).astype(MXU_DTYPE), dc_ref[...].astype(MXU_DTYPE),
                                      (((0,), (0,)), ((), ())), preferred_element_type=F32)
        gb_ref[...] = jnp.sum(da_ref[...], axis=0, keepdims=True)
    n = dmod_cols.shape[1]
    return pl.pallas_call(
        body, name="ada_bwd", in_specs=[_whole(c_all.shape), _whole(dmod_cols.shape), _whole(dmod_all.shape)],
        out_specs=[_whole((D, n)), _whole((1, dmod_all.shape[1]))],
        out_shape=[jax.ShapeDtypeStruct((D, n), F32), jax.ShapeDtypeStruct((1, dmod_all.shape[1]), F32)],
        compiler_params=_params(),
    )(c_all, dmod_cols, dmod_all)


def _rows_block(n_rows, cap=512):
    if n_rows <= cap:
        return n_rows
    for t in range(cap - cap % SUBLANES, 0, -SUBLANES):
        if n_rows % t == 0:
            return t
    return n_rows


def _adamw(w, g, m, v, name):
    rows, cols = w.shape
    tr = _rows_block(rows, max(SUBLANES, (1 << 19) // max(cols, 1) // SUBLANES * SUBLANES))

    def body(w_ref, g_ref, m_ref, v_ref, d_ref, nm_ref, nv_ref):
        gv = g_ref[...]
        nm = B1 * m_ref[...] + (1.0 - B1) * gv
        nv = B2 * v_ref[...] + (1.0 - B2) * (gv * gv)
        m_hat = nm / (1.0 - B1 ** STEP)
        v_hat = nv / (1.0 - B2 ** STEP)
        d_ref[...] = -LR * (m_hat / (jnp.sqrt(v_hat) + ADAM_EPS) + WD * w_ref[...])
        nm_ref[...] = nm
        nv_ref[...] = nv

    spec = pl.BlockSpec((tr, cols), lambda i: (i, 0))
    sd = jax.ShapeDtypeStruct((rows, cols), F32)
    return pl.pallas_call(body, name=name, grid=(rows // tr,), in_specs=[spec] * 4, out_specs=[spec] * 3,
                          out_shape=[sd] * 3, compiler_params=_params(("parallel",)))(w, g, m, v)


def _sum_slots(x, out_dtype, name):
    xs = x if isinstance(x, (list, tuple)) else [x]
    _, rows, cols = xs[0].shape
    tr = _rows_block(rows)

    def body(*refs):
        acc = None
        for x_ref in refs[:-1]:
            for j in range(x_ref.shape[0]):
                term = x_ref[j].astype(F32)
                acc = term if acc is None else acc + term
        refs[-1][...] = acc.astype(refs[-1].dtype)

    return pl.pallas_call(
        body, name=name, grid=(rows // tr,),
        in_specs=[pl.BlockSpec((z.shape[0], tr, cols), lambda i: (0, i, 0)) for z in xs],
        out_specs=pl.BlockSpec((tr, cols), lambda i: (i, 0)), out_shape=jax.ShapeDtypeStruct((rows, cols), out_dtype),
        compiler_params=_params(("parallel",)))(*xs)


PACK_COLS = 1024


def _pack_rows(parts, dtype, row_mult):
    flat = jnp.concatenate([p.reshape(-1).astype(dtype) for p in parts])
    per = PACK_COLS * row_mult
    n = -(-flat.shape[0] // per) * per
    flat = jnp.pad(flat, (0, n - flat.shape[0]))
    return flat.reshape(n // PACK_COLS, PACK_COLS)


def _unpack(flat, shapes):
    out, off = [], 0
    for s in shapes:
        n = math.prod(s)
        out.append(flat[off:off + n].reshape(s))
        off += n
    return out


BIG = (("w_in", (D, SHIFT + SW + 2 * D), 1), ("w_out_rwkv", (RW, D), 1), ("w_glu", (SW, 2 * D), 1),
       ("w_out", (D, D), 0), ("w_ffn_up", (D, 2 * DFF), 1), ("w_ffn_down", (DFF, D), 0))
BIG_SMALL = (("rwkv_w_up", (LW, RW), 1), ("rwkv_a_up", (LA, RW), 1), ("rwkv_g_up", (LG, RW), 1),
             ("ffn_conv_w", (3, 2 * DFF), 1))
BIG_LATE = BIG[4:]
BIG_MID = BIG[1:4]


def _shard_shape(shape, axis):
    return (shape[0] // 4, shape[1]) if axis == 0 else (shape[0], shape[1] // 4)


def _to_shards(g, axis):
    r, C = g.shape
    return g.reshape(4, r // 4, C) if axis == 0 else g.reshape(r, 4, C // 4).transpose(1, 0, 2)


def _from_shards(x, axis):
    _, r, C = x.shape
    return x.reshape(4 * r, C) if axis == 0 else x.transpose(1, 0, 2).reshape(r, 4 * C)


def kernel(x, c, w_ada, b_ada, norm1_g, w_in, mu_shift, rwkv_w0, rwkv_w_up, rwkv_a0, rwkv_a_up, rwkv_g_up, rwkv_k_k, rwkv_k_a, rwkv_r_k, rwkv_ln_g, rwkv_ln_b, w_out_rwkv, s5_a_re, s5_a_im, s5_log_dt, s5_b_re, s5_b_im, s5_c_re, s5_c_im, s5_d, w_glu, w_out, norm2_g, w_ffn_up, ffn_conv_w, ffn_conv_b, w_ffn_down, norm_f_g, loss_target, m_w_ada, m_b_ada, m_norm1_g, m_w_in, m_mu_shift, m_rwkv_w0, m_rwkv_w_up, m_rwkv_a0, m_rwkv_a_up, m_rwkv_g_up, m_rwkv_k_k, m_rwkv_k_a, m_rwkv_r_k, m_rwkv_ln_g, m_rwkv_ln_b, m_w_out_rwkv, m_s5_a_re, m_s5_a_im, m_s5_log_dt, m_s5_b_re, m_s5_b_im, m_s5_c_re, m_s5_c_im, m_s5_d, m_w_glu, m_w_out, m_norm2_g, m_w_ffn_up, m_ffn_conv_w, m_ffn_conv_b, m_w_ffn_down, m_norm_f_g, v_w_ada, v_b_ada, v_norm1_g, v_w_in, v_mu_shift, v_rwkv_w0, v_rwkv_w_up, v_rwkv_a0, v_rwkv_a_up, v_rwkv_g_up, v_rwkv_k_k, v_rwkv_k_a, v_rwkv_r_k, v_rwkv_ln_g, v_rwkv_ln_b, v_w_out_rwkv, v_s5_a_re, v_s5_a_im, v_s5_log_dt, v_s5_b_re, v_s5_b_im, v_s5_c_re, v_s5_c_im, v_s5_d, v_w_glu, v_w_out, v_norm2_g, v_w_ffn_up, v_ffn_conv_w, v_ffn_conv_b, v_w_ffn_down, v_norm_f_g):
    names = ["w_ada", "b_ada", "norm1_g", "w_in", "mu_shift", "rwkv_w0", "rwkv_w_up", "rwkv_a0", "rwkv_a_up",
             "rwkv_g_up", "rwkv_k_k", "rwkv_k_a", "rwkv_r_k", "rwkv_ln_g", "rwkv_ln_b", "w_out_rwkv", "s5_a_re",
             "s5_a_im", "s5_log_dt", "s5_b_re", "s5_b_im", "s5_c_re", "s5_c_im", "s5_d", "w_glu", "w_out", "norm2_g",
             "w_ffn_up", "ffn_conv_w", "ffn_conv_b", "w_ffn_down", "norm_f_g"]
    env = dict(locals())
    W = {n: env[n] for n in names}
    M = {n: env["m_" + n] for n in names}
    V = {n: env["v_" + n] for n in names}

    Bl, S, _ = x.shape
    T = Bl * S
    ix, iy, ic = lax.axis_index("x"), lax.axis_index("y"), lax.axis_index("c")
    chip = 2 * ix + iy
    dev = 2 * chip + ic
    rw = functools.partial(_rowwise, Bl=Bl, S=S)

    got_chip, got_dev = _gather_two_level([W[n][0] for n, _, _ in BIG_SMALL[:3]], [W["ffn_conv_w"][0], c], "gather_w")
    full = {n: _from_shards(g, axis) for (n, _, axis), g in zip(BIG_SMALL[:3], got_chip)}
    full["ffn_conv_w"] = _from_shards(got_dev[0][:, 0], 1)
    c_all = got_dev[1].reshape(8 * Bl, D)
    zeros_l = jnp.zeros((LW, RW), F32)
    w_up_p = jnp.concatenate([full["rwkv_w_up"], zeros_l], axis=0)
    a_up_p = jnp.concatenate([zeros_l, full["rwkv_a_up"]], axis=0)
    g_up = full["rwkv_g_up"]
    conv_w = full["ffn_conv_w"]
    conv_wg, conv_wu = conv_w[:, :DFF], conv_w[:, DFF:]
    conv_bg, conv_bu = ffn_conv_b[:, :DFF], ffn_conv_b[:, DFF:]
    hm = jnp.kron(jnp.eye(NH, dtype=F32), jnp.ones((HD, HD), F32))

    ncol = 6 * D // 4
    b_ada_cols = lax.dynamic_slice_in_dim(b_ada, chip * ncol, ncol, 1)
    mod_part = _ada_fwd(c_all, w_ada[0], b_ada_cols)
    mod4 = _gather_two_level([], [mod_part], "gather_mod")[1][0][:, 0]
    mod4, shards = lax.optimization_barrier((mod4, [W[n][0].astype(MXU_DTYPE) for n, _, _ in BIG]))

    def push_shards(tag, arrs):
        moves = [(i, i, lambda ref, me, peer: ref, lambda ref, me, k: ref.at[_chip_of(me)]) for i in range(len(arrs))]
        lands = [jax.ShapeDtypeStruct((4,) + z.shape, z.dtype) for z in arrs]
        return _send_start("gather_%s_start" % tag, CHIP_FLIPS, arrs, lands, moves), moves

    def pushed_shards(tag, started, moves, after, group):
        owns, gots = _send_wait("gather_%s_wait" % tag, CHIP_FLIPS, started, moves, after)
        for (n, _, axis), own, got in zip(group, owns, gots):
            full[n] = _from_shards(lax.dynamic_update_slice(got, own[None], (chip, 0, 0)), axis)

    first_start, first_moves = push_shards("in", shards[:1])
    norm1_g = norm1_g + first_start["token"]
    mod =lax.dynamic_slice_in_dim(mod4, dev * Bl, Bl, 1).transpose(1, 0, 2).reshape(Bl, 1, 6 * D)
    SH1, SC1, GT1, SH2, SC2, GT2 = range(6)

    x2d = x.reshape(T, D)
    tgt = loss_target.reshape(T, D)

    (h1,) = rw("norm1", lambda xv, sc, sh, g: _norm_mod(xv, g, sc, sh), R=512, tiled=[(x2d, D, 0)],
               batch=[(mod, D, SC1), (mod, D, SH1)], full=[norm1_g], out_tiled=[(D, MXU_DTYPE)])
    pushed_shards("in", first_start, first_moves, h1, BIG[:1])
    full["w_in"], rest = lax.optimization_barrier((full["w_in"], shards[1:]))
    late_start, late_moves = push_shards("rest", rest)
    mu_shift = mu_shift + late_start["token"]
    w_p, w_u, w_g = full["w_in"][:, :SHIFT], full["w_in"][:, SHIFT:SHIFT + SW], full["w_in"][:, SHIFT + SW:]
    p = _mm([h1], [w_p], F32, "proj_p")
    u = _mm([h1], [w_u], F32, "proj_u")
    gates = _mm([h1], [w_g], MXU_DTYPE, "proj_g")

    prep_params = [rwkv_w0, w_up_p, rwkv_a0, a_up_p, g_up, rwkv_k_k, rwkv_k_a, hm]

    def prep_fwd(pv, ph, mu, *pp):
        ps = pv + (_shift_down(pv, ph, 1) - pv) * mu
        return _rwkv_prep(*_split_ps(ps), *pp)

    r_, w_, k_, v_, a_, b_, g_ = rw("rwkv_prep", prep_fwd, R=256, tiled=[(p, SHIFT, 0)], prev=[(p, SHIFT, 0)],
                                    full=[mu_shift] + prep_params, out_tiled=[(RW, F32)] * 7)
    y_wkv, ck, tinv = _wkv_fwd(r_, w_, k_, v_, a_, b_, Bl, S)
    r_k_row = rwkv_r_k.reshape(1, RW)
    post_params = [rwkv_ln_g, rwkv_ln_b, r_k_row, hm]
    (o_rwkv,) = rw("rwkv_post", _rwkv_post, R=256,
                   tiled=[(y_wkv, RW, 0), (r_, RW, 0), (k_, RW, 0), (v_, RW, 0), (g_, RW, 0)],
                   full=post_params, out_tiled=[(RW, MXU_DTYPE)])
    pushed_shards("rest", late_start, late_moves, o_rwkv, BIG[1:])
    y_a = _mm([o_rwkv], [full["w_out_rwkv"]], MXU_DTYPE, "out_rwkv")

    expand = jnp.kron(jnp.eye(SP, dtype=F32), jnp.ones((1, SGC), F32))
    s5_in = (s5_a_re[0], s5_a_im[0], s5_log_dt[0].reshape(NG, 1), s5_b_re[0].reshape(NG, SP * SGC),
             s5_b_im[0].reshape(NG, SP * SGC), expand)
    ab_re, ab_im, bb_re, bb_im = _s5_disc(*s5_in)
    eye8 = jnp.eye(8, dtype=F32)

    def blockdiag_in(bb):
        t = bb.reshape(NSG, 8, SP, SGC)
        return jnp.einsum("ab,sapc->sacbp", eye8, t).reshape(NSG, 128, 512)

    def blockdiag_out(cc):
        t = cc.reshape(NSG, 8, SGC, SP)
        return jnp.einsum("ab,sacp->sapbc", eye8, t).reshape(NSG, 512, 128)

    wb = jnp.concatenate([blockdiag_in(bb_re), blockdiag_in(bb_im)], axis=2).astype(MXU_DTYPE)
    wc = jnp.concatenate([blockdiag_out(s5_c_re[0]), -blockdiag_out(s5_c_im[0])], axis=1).astype(MXU_DTYPE)
    ab = jnp.stack([ab_re.reshape(NST), ab_im.reshape(NST)])
    y_ssm, s5_st, s5_x, s5o = _s5_fwd(u, wb, wc, ab, s5_d, Bl, S)
    z = _mm([s5o], [full["w_glu"]], MXU_DTYPE, "glu")
    mix_tiled = [(gates, D, 0), (gates, D, 1), (y_a, D, 0), (z, D, 0), (z, D, 1)]
    (mixed_in,) = rw("mix", lambda *a: _mix(*(v.astype(F32) for v in a)), R=256, tiled=mix_tiled,
                     out_tiled=[(D, MXU_DTYPE)])
    mixed = _mm([mixed_in], [full["w_out"]], F32, "out_proj")

    def norm2_fwd(xv, mx, gt, sc, sh, g):
        x1 = xv + gt * mx
        return x1, _norm_mod(x1, g, sc, sh)

    x1, h2 = rw("norm2", norm2_fwd, R=512, tiled=[(x2d, D, 0), (mixed, D, 0)],
                batch=[(mod, D, GT1), (mod, D, SC2), (mod, D, SH2)], full=[norm2_g],
                out_tiled=[(D, F32), (D, MXU_DTYPE)])
    up =_mm([h2], [full["w_ffn_up"]], MXU_DTYPE, "ffn_up")
    conv_tiled = [(up, 0), (up, 1)]
    conv_full = [conv_wg, conv_wu, conv_bg, conv_bu]
    cw = functools.partial(_colwise, Bl=Bl, S=S, R=128, W=DFF, strip=LANES)

    def act_fwd(*a):
        return ((_silu_gate(*_conv_act(*a)),),)

    (act,) = cw("ffn_act", act_fwd, tiled=conv_tiled, prev=conv_tiled, full=conv_full, out_tiled=[(1, MXU_DTYPE)])
    ffn = _mm([act], [full["w_ffn_down"]], F32, "ffn_down")

    def head(x1v, fv, tv, gt, g):
        x2 = x1v + gt * fv
        y, vjp = jax.vjp(_rms, x2, g)
        e = y - tv
        dx2, dg = vjp(e * (1.0 / D))
        loss = jnp.sum(e * e, keepdims=True) * jnp.ones((1, LANES), F32)
        return dx2, dx2 * gt, jnp.sum(dx2 * fv, axis=0, keepdims=True), dg.reshape(1, D), loss

    dx2, d_ffn, d_gt2, g_norm_f, loss_acc = rw(
        "head", head, R=512, tiled=[(x1, D, 0), (ffn, D, 0), (tgt, D, 0)], batch=[(mod, D, GT2)],
        full=[norm_f_g.reshape(1, D)], out_tiled=[(D, F32), (D, MXU_DTYPE)], out_batch=[D],
        out_acc=[(1, D), (1, LANES)])
    loss = lax.psum(0.5 / D * loss_acc[0, 0], ("x", "y", "c"))

    d_act = _mm([d_ffn], [full["w_ffn_down"]], F32, "d_act", bt=True)
    g_w_ffn_down = _mm_tn(act, d_ffn, "g_ffn_down")

    def act_bwd(ug, uu, dact, hg, hu, wg, wu, bg, bu):
        (gate, taps_g), (upv, taps_u) = _conv3(ug, hg, wg, bg), _conv3(uu, hu, wu, bu)
        _, vjp_s = jax.vjp(_silu_gate, gate, upv)
        d_gate, d_upv = vjp_s(dact)
        def taps(dh, shifted):
            return [jnp.sum(dh * s, axis=0, keepdims=True) for s in shifted] + [jnp.sum(dh, axis=0, keepdims=True)]
        return ((d_gate,), (d_upv,), *taps(d_gate, taps_g), *taps(d_upv, taps_u))

    dh_g, dh_u, *tapg = cw("ffn_act_bwd", act_bwd, tiled=conv_tiled + [(d_act, 0)], prev=conv_tiled, full=conv_full,
                           out_tiled=[(1, MXU_DTYPE), (1, MXU_DTYPE)], n_acc=8)
    g_cw_g, g_cb_g = jnp.concatenate(tapg[0:3], axis=0), tapg[3]
    g_cw_u, g_cb_u = jnp.concatenate(tapg[4:7], axis=0), tapg[7]

    def conv_t(dg, du_, ng, nu, wg, wu):
        dg, du_, ng, nu = (z.astype(F32) for z in (dg, du_, ng, nu))

        def ct(d, n, w):
            return w[2:3] * d + w[1:2] * _shift_up(d, n, 1) + w[0:1] * _shift_up(d, n, 2)
        return ((ct(dg, ng, wg), ct(du_, nu, wu)),)

    (d_up,) = cw("conv_bwd", conv_t, tiled=[(dh_g, 0), (dh_u, 0)], nxt=[(dh_g, 0), (dh_u, 0)],
                 full=[conv_wg, conv_wu], out_tiled=[(2, MXU_DTYPE)])
    d_h2 = _mm([d_up], [full["w_ffn_up"]], F32, "d_h2", bt=True)
    g_w_ffn_up = _mm_tn(h2, d_up, "g_ffn_up")

    sds = jax.ShapeDtypeStruct
    reduce_src = lambda r: (lambda ref, me, peer: ref.at[_chip_of(peer), _half(r, peer[2])])

    def reduced_halves(tag, started, moves, after):
        gsh_own, got = _send_wait("rs_%s_wait" % tag, ALL_FLIPS, started, moves, after)
        halves = []
        for i, (g, gt) in enumerate(zip(gsh_own, got)):
            h = g.shape[1] // 2
            own = lax.dynamic_slice(g, (chip, ic * h, 0), (1, h, g.shape[2]))
            halves.append(_sum_slots([own, gt], F32, "rs_%s_sum%d" % (tag, i)))
        return halves

    def share_start(tag, halves):
        moves = [(i, i, lambda ref, me, peer: ref, lambda ref, me, k, r=2 * g.shape[0]: ref.at[_half(r, me[2])])
                 for i, g in enumerate(halves)]
        lands = [sds((2 * g.shape[0], g.shape[1]), F32) for g in halves]
        return _send_start("share_%s_start" % tag, PAIR_FLIPS, halves, lands, moves), moves

    def share_finish(tag, started, moves, after, group, grads):
        mine_h, got = _send_wait("share_%s_wait" % tag, PAIR_FLIPS, started, moves, after)
        for (n, _, _), mh, whole in zip(group, mine_h, got):
            grads[n] = lax.dynamic_update_slice(whole, mh, (ic * mh.shape[0], 0))[None]

    def reduce_start(tag, group, mats):
        gsh = [_to_shards(g, ax).astype(MXU_DTYPE) for g, (_, _, ax) in zip(mats, group)]
        moves = [(i, i, reduce_src(g.shape[1]), lambda ref, me, k: ref.at[k]) for i, g in enumerate(gsh)]
        lands = [sds((len(ALL_FLIPS), g.shape[1] // 2, g.shape[2]), MXU_DTYPE) for g in gsh]
        return _send_start("rs_%s_start" % tag, ALL_FLIPS, gsh, lands, moves), moves

    rsl, rsl_moves = reduce_start("ffn", BIG_LATE, (g_w_ffn_up, g_w_ffn_down))
    norm2_g = norm2_g + rsl["token"]

    def norm2_bwd(x1v, dh2, dx2v, mx, gt, sc, sh, g):
        _, vjp = jax.vjp(_norm_mod, x1v, g, sc, sh)
        dxn, dg, dsc, dsh = vjp(dh2)
        dx1 = dx2v + dxn
        return dx1, dx1 * gt, jnp.sum(dx1 * mx, axis=0, keepdims=True), dsc, dsh, dg

    dx1, d_mixed, d_gt1, d_sc2, d_sh2, g_norm2 = rw(
        "norm2_bwd", norm2_bwd, R=512, tiled=[(x1, D, 0), (d_h2, D, 0), (dx2, D, 0), (mixed, D, 0)],
        batch=[(mod, D, GT1), (mod, D, SC2), (mod, D, SH2)], full=[norm2_g],
        out_tiled=[(D, F32), (D, MXU_DTYPE)], out_batch=[D, D, D], out_acc=[(1, D)])

    d_mixed_in = _mm([d_mixed], [full["w_out"]], MXU_DTYPE, "d_mixed_in", bt=True)
    g_w_out = _mm_tn(mixed_in, d_mixed, "g_w_out")

    def mix_bwd(*a):
        ga, gb, ya, za, zb, dm = (v.astype(F32) for v in a)
        _, vjp = jax.vjp(_mix, ga, gb, ya, za, zb)
        dga, dgb, dya, dza, dzb = vjp(dm)
        return jnp.concatenate([dga, dgb], axis=1), dya, jnp.concatenate([dza, dzb], axis=1)

    d_gates, d_ya, d_z = rw("mix_bwd", mix_bwd, R=256, tiled=mix_tiled + [(d_mixed_in, D, 0)],
                            out_tiled=[(2 * D, MXU_DTYPE), (D, MXU_DTYPE), (2 * D, MXU_DTYPE)])
    d_o_rwkv = _mm([d_ya], [full["w_out_rwkv"]], F32, "d_o_rwkv", bt=True)
    g_w_out_rwkv = _mm_tn(o_rwkv, d_ya, "g_out_rwkv")
    d_s5o = _mm([d_z], [full["w_glu"]], F32, "d_s5o", bt=True)
    g_w_glu = _mm_tn(s5o, d_z, "g_glu")
    rsm, rsm_moves = reduce_start("mid", BIG_MID, (g_w_out_rwkv, g_w_glu, g_w_out))
    s5_d = s5_d + rsm["token"]

    d_u, d_wb, d_wc, d_ab, g_s5_d = _s5_bwd(u, y_ssm, d_s5o, s5_d, wb, wc, ab, s5_st, s5_x, Bl, S)

    def diag_in(dw):
        t = dw.reshape(NSG, 8, SGC, 8, SP)
        return jnp.einsum("ab,sacbp->sapc", eye8, t).reshape(NG, SP * SGC)

    def diag_out(dw):
        t = dw.reshape(NSG, 8, SP, 8, SGC)
        return jnp.einsum("ab,sapbc->sacp", eye8, t).reshape(NG, SGC, SP)

    g_s5_c_re = diag_out(d_wc[:, :512])
    g_s5_c_im = -diag_out(d_wc[:, 512:])
    disc_cts = (d_ab[0].reshape(NG, SP), d_ab[1].reshape(NG, SP), diag_in(d_wb[:, :, :512]), diag_in(d_wb[:, :, 512:]))
    g_a_re, g_a_im, g_log_dt, g_b_re, g_b_im = _s5_disc_bwd(*s5_in, disc_cts)

    def post_bwd(yv, rv, kv, vv, gv, do, *pp):
        _, vjp = jax.vjp(lambda *a: _rwkv_post(*a, pp[3]), yv, rv, kv, vv, gv, *pp[:3])
        return vjp(do)

    dy_wkv, dr_b, dk_b, dv_b, dg_, g_ln_g, g_ln_b, g_r_k = rw(
        "rwkv_post_bwd", post_bwd, R=256,
        tiled=[(y_wkv, RW, 0), (r_, RW, 0), (k_, RW, 0), (v_, RW, 0), (g_, RW, 0), (d_o_rwkv, RW, 0)],
        full=post_params, out_tiled=[(RW, F32)] * 5, out_acc=[(1, RW)] * 3)
    dr3, dw3, dk3, dv3, da3, db3 = _wkv_bwd(r_, w_, k_, v_, a_, b_, dy_wkv, ck, tinv, Bl, S)

    shl, shl_moves = share_start("ffn", reduced_halves("ffn", rsl, rsl_moves, dr3))
    shm, shm_moves = share_start("mid", reduced_halves("mid", rsm, rsm_moves, dr3))
    mu_shift = mu_shift + (shl["token"] + shm["token"])

    def prep_bwd(pv, dr1, dr2, dwv, dk1, dk2, dv1, dv2, dav, dbv, dgv, ph, mu, *pp):
        prev = _shift_down(pv, ph, 1)
        ps = pv + (prev - pv) * mu
        _, vjp = jax.vjp(lambda *q: _rwkv_prep(*q, pp[7]), *_split_ps(ps), *pp[:7])
        grads = vjp((dr1 + dr2, dwv, dk1 + dk2, dv1 + dv2, dav, dbv, dgv))
        dps = jnp.concatenate(grads[:5], axis=1)
        return (dps,) + tuple(grads[5:]) + (jnp.sum(dps * (prev - pv), axis=0, keepdims=True),)

    prep_outs = rw(
        "rwkv_prep_bwd", prep_bwd, R=256,
        tiled=[(p, SHIFT, 0), (dr3, RW, 0), (dr_b, RW, 0), (dw3, RW, 0), (dk3, RW, 0), (dk_b, RW, 0),
               (dv3, RW, 0), (dv_b, RW, 0), (da3, RW, 0), (db3, RW, 0), (dg_, RW, 0)],
        prev=[(p, SHIFT, 0)], full=[mu_shift] + prep_params,
        out_tiled=[(SHIFT, F32)],
        out_acc=[(1, RW), (LW + LA, RW), (1, RW), (LW + LA, RW), (LG, RW), (1, RW), (1, RW), (1, SHIFT)])
    d_ps, g_w0, g_w_up_p, g_a0, g_a_up_p, g_g_up, g_k_k, g_k_a, g_mu = prep_outs

    small = {"mu_shift": g_mu, "rwkv_w0": g_w0, "rwkv_a0": g_a0, "rwkv_k_k": g_k_k,
             "rwkv_k_a": g_k_a, "rwkv_r_k": g_r_k, "rwkv_ln_g": g_ln_g, "rwkv_ln_b": g_ln_b, "s5_a_re": g_a_re,
             "s5_a_im": g_a_im, "s5_log_dt": g_log_dt, "s5_b_re": g_b_re, "s5_b_im": g_b_im, "s5_c_re": g_s5_c_re,
             "s5_c_im": g_s5_c_im, "s5_d": g_s5_d, "norm2_g": g_norm2,
             "ffn_conv_b": jnp.concatenate([g_cb_g, g_cb_u], axis=1), "norm_f_g": g_norm_f}
    small_names = list(small)
    g_conv_w = jnp.concatenate([g_cw_g, g_cw_u], axis=1)
    shard_small = {"rwkv_w_up": g_w_up_p[:LW], "rwkv_a_up": g_a_up_p[LW:], "rwkv_g_up": g_g_up, "ffn_conv_w": g_conv_w}
    parts = [small[n] for n in small_names] + [_to_shards(shard_small[n], ax) for n, _, ax in BIG_SMALL]
    spack = _pack_rows(parts, F32, SUBLANES)
    sm_moves = [(0, 0, lambda ref, me, peer: ref, lambda ref, me, k: ref.at[2 * _chip_of(me) + me[2]])]
    sm = _send_start("gsmall_start", ALL_FLIPS, [spack], [sds((8,) + spack.shape, F32)], sm_moves)
    mu_shift = mu_shift + sm["token"]

    def shift_bwd(dps, nx, mu):
        return dps * (1.0 - mu) + _shift_up(dps * mu, nx * mu, 1)

    (d_p,) = rw("shift_bwd", shift_bwd, R=256, tiled=[(d_ps, SHIFT, 0)], nxt=[(d_ps, SHIFT, 0)], full=[mu_shift],
                out_tiled=[(SHIFT, MXU_DTYPE)])
    g_w_in = jnp.concatenate([_mm_tn(h1, d_p, "g_w_p"), _mm_tn(h1, d_u, "g_w_u"), _mm_tn(h1, d_gates, "g_w_g")], axis=1)
    rsn, rsn_moves = reduce_start("in", BIG[:1], (g_w_in,))
    norm1_g = norm1_g + rsn["token"]
    d_h1 = _mm([d_p, d_u, d_gates], [w_p, w_u, w_g], F32, "d_h1", bt=True)

    def norm1_bwd(xv, dh1, dx1v, sc, sh, g):
        _, vjp = jax.vjp(_norm_mod, xv, g, sc, sh)
        dxn, dg, dsc, dsh = vjp(dh1)
        return dx1v + dxn, dsc, dsh, dg

    grad_x, d_sc1, d_sh1, g_norm1 = rw(
        "norm1_bwd", norm1_bwd, R=512, tiled=[(x2d, D, 0), (d_h1, D, 0), (dx1, D, 0)],
        batch=[(mod, D, SC1), (mod, D, SH1)], full=[norm1_g], out_tiled=[(D, F32)], out_batch=[D, D], out_acc=[(1, D)])

    dmod = jnp.concatenate([d_sh1, d_sc1, d_gt1, d_sh2, d_sc2, d_gt2], axis=2).reshape(Bl, 6 * D)
    last_all = _gather_two_level([], [dmod, g_norm1], "gather_dmod")[1]
    dmod_all = last_all[0].reshape(8 * Bl, 6 * D)
    shn, shn_moves = share_start("in", reduced_halves("in", rsn, rsn_moves, dmod_all))
    dmod_cols = lax.dynamic_slice_in_dim(dmod_all, chip * ncol, ncol, 1)
    g_w_ada, g_b_ada = _ada_bwd(c_all, dmod_cols, dmod_all)

    grads = {"norm1_g": _sum_slots(last_all[1].reshape(8, 1, D), F32, "sum_norm1")}
    sm_own, sm_got = _send_wait("gsmall_wait", ALL_FLIPS, sm, sm_moves, g_b_ada)
    s_all = lax.dynamic_update_slice(sm_got[0], sm_own[0][None], (dev, 0, 0))
    s_sum = _sum_slots(s_all, F32, "sum_gsmall").reshape(-1)
    off = 0
    for n in small_names:
        grads[n] = s_sum[off:off + W[n].size].reshape(W[n].shape)
        off += W[n].size
    for n, shape, axis in BIG_SMALL:
        ss = _shard_shape(shape, axis)
        k4 = 4 * math.prod(ss)
        sh4 = s_sum[off:off + k4].reshape(4, math.prod(ss))
        grads[n] = lax.dynamic_index_in_dim(sh4, chip, 0, keepdims=False).reshape((1,) + ss)
        off += k4

    share_finish("ffn", shl, shl_moves, s_sum, BIG_LATE, grads)
    share_finish("mid", shm, shm_moves, grads[BIG_LATE[0][0]], BIG_MID, grads)
    grads["w_ada"] = g_w_ada[None]
    grads["b_ada"] = g_b_ada

    delta, new_m, new_v = {}, {}, {}
    to2 = lambda z: z.reshape(-1, z.shape[-1])

    def adamw(n):
        d_, m_, v2_ = _adamw(to2(W[n]), to2(grads[n]), to2(M[n]), to2(V[n]), "adamw_" + n)
        delta[n], new_m[n], new_v[n] = (z.reshape(W[n].shape) for z in (d_, m_, v2_))

    for n in ["w_ada"] + [b[0] for b in BIG[1:]]:
        adamw(n)
    rest = [n for n in names if n not in delta and n != "w_in"]
    packs = [_pack_rows([src[n] for n in rest], F32, SUBLANES) for src in (W, grads, M, V)]
    d_, m_, v2_ = _adamw(*packs, "adamw_small")
    shapes = [W[n].shape for n in rest]
    for dst, z in ((delta, d_), (new_m, m_), (new_v, v2_)):
        for n, val in zip(rest, _unpack(z.reshape(-1), shapes)):
            dst[n] = val
    share_finish("in", shn, shn_moves, d_, BIG[:1], grads)
    adamw("w_in")

    return (loss, grad_x.reshape(Bl, S, D), *[grads[n] for n in names], *[delta[n] for n in names],
            *[new_m[n] for n in names], *[new_v[n] for n in names])
```

```python
import functools
import math

import jax
import jax.numpy as jnp
from jax import lax
from jax.experimental import pallas as pl
from jax.experimental.pallas import tpu as pltpu

F32 = jnp.float32
BF16 = jnp.bfloat16
MXU_DTYPE = jnp.bfloat16
MESH_IDS = pl.DeviceIdType.MESH
HIGHEST = lax.Precision.HIGHEST

D = 1024
RW, NH, HD = 512, 8, 64
LW, LA, LG = 64, 64, 128
SW, SGC, NG, SP = 512, 16, 32, 64
NSG = 4
SHIFT = 3 * RW + LW + LA + LG
DFF = 2816
RMS_EPS, GN_EPS, L2_EPS = 1e-6, 64e-5, 1e-12
LR, B1, B2, ADAM_EPS, WD, STEP = 0.001, 0.9, 0.999, 1e-8, 0.01, 10
DECAY_SCALE = math.exp(-0.5)
GELU_C = math.sqrt(2.0 / math.pi)

VMEM_LIMIT = 52 * 1024 * 1024
SUBLANES, LANES = 8, 128
HALO = 16


def _pick(n, cap):
    if n <= cap:
        return n
    best = None
    for t in range(LANES, cap + 1, LANES):
        if n % t == 0:
            best = t
    assert best is not None, (n, cap)
    return best


def _params(sem=None, vmem=VMEM_LIMIT):
    return pltpu.CompilerParams(dimension_semantics=sem, vmem_limit_bytes=vmem)


def _chip_of(p):
    return 2 * p[0] + p[1]


def _me():
    return (lax.axis_index("x"), lax.axis_index("y"), lax.axis_index("c"))


def _half(rows, core):
    h = rows // 2
    return pl.ds(pl.multiple_of(core * h, 16 if h % 16 == 0 else SUBLANES), h)


_HBM =pl.BlockSpec(memory_space=pltpu.HBM)
_SEM = pl.BlockSpec(memory_space=pltpu.SEMAPHORE)
_DATAFLOW = pltpu.SideEffectType.DATAFLOW_SIDE_EFFECTING


def _split_copies(flips, moves, src_refs, land_refs, send_sems, recv_sems):
    me = _me()
    nf = len(flips)
    out = []
    for m, (si, li, src_sel, dst_sel) in enumerate(moves):
        for k, f in enumerate(flips):
            peer = tuple(1 - v if b else v for v, b in zip(me, f))
            out.append(pltpu.make_async_remote_copy(
                src_ref=src_sel(src_refs[si], me, peer), dst_ref=dst_sel(land_refs[li], me, k),
                send_sem=send_sems.at[m * nf + k], recv_sem=recv_sems.at[m * nf + k],
                device_id=peer, device_id_type=MESH_IDS))
    return out


def _send_start(name, flips, srcs, land_shapes, moves):
    ns, nl = len(srcs), len(land_shapes)
    n = len(moves) * len(flips)

    def body(*refs):
        for cp in _split_copies(flips, moves, refs[:ns], refs[ns:ns + nl], refs[ns + nl], refs[ns + nl + 1]):
            cp.start()
        refs[-1][...] = jnp.zeros(refs[-1].shape, F32)

    hbm = lambda z: pltpu.with_memory_space_constraint(z, pltpu.HBM)
    lands = [lax.empty(s.shape, s.dtype) for s in land_shapes]
    res = pl.pallas_call(
        body, name=name,
        out_shape=(pltpu.SemaphoreType.DMA((n,)), pltpu.SemaphoreType.DMA((n,)),
                   *[pltpu.HBM(z.shape, z.dtype) for z in srcs], *[pltpu.HBM(s.shape, s.dtype) for s in land_shapes],
                   jax.ShapeDtypeStruct((SUBLANES, LANES), F32)),
        in_specs=[_HBM] * (ns + nl),
        out_specs=(_SEM, _SEM, *[_HBM] * (ns + nl), pl.BlockSpec(memory_space=pltpu.VMEM)),
        input_output_aliases={i: 2 + i for i in range(ns + nl)},
        compiler_params=pltpu.CompilerParams(has_side_effects=_DATAFLOW),
    )(*[hbm(z) for z in srcs], *[hbm(z) for z in lands])
    return {"sems": res[:2], "srcs": list(res[2:2 + ns]), "lands": list(res[2 + ns:2 + ns + nl]), "token": res[-1][0, 0]}


def _send_wait(name, flips, started, moves, after):
    srcs, lands = started["srcs"], started["lands"]
    ns, nl = len(srcs), len(lands)

    def body(*refs):
        for cp in _split_copies(flips, moves, refs[:ns], refs[ns:ns + nl], refs[ns + nl], refs[ns + nl + 1]):
            cp.wait_send()
            cp.wait_recv()

    res = pl.pallas_call(
        body, name=name, out_shape=[pltpu.HBM(z.shape, z.dtype) for z in srcs + lands],
        in_specs=[_HBM] * (ns + nl) + [_SEM, _SEM, pl.BlockSpec(memory_space=pl.ANY)],
        out_specs=[_HBM] * (ns + nl), input_output_aliases={i: i for i in range(ns + nl)},
        compiler_params=pltpu.CompilerParams(has_side_effects=_DATAFLOW),
    )(*srcs, *lands, *started["sems"], after)
    return list(res[:ns]), list(res[ns:])


CHIP_FLIPS = ((1, 0, 0), (0, 1, 0), (1, 1, 0))
PAIR_FLIPS = ((0, 0, 1),)
ALL_FLIPS = CHIP_FLIPS + ((1, 0, 1), (0, 1, 1), (1, 1, 1)) + PAIR_FLIPS


def _gather_two_level(chip_arrs, dev_arrs, name):
    arrs = list(chip_arrs) + list(dev_arrs)
    n, nchip = len(arrs), len(chip_arrs)
    NS = 7

    def body(*refs):
        srcs, outs = refs[:n], refs[n:2 * n]
        send_sems, recv_sems, loc_sems = refs[2 * n:]
        x, y, c = _me()
        sib = (x, y, 1 - c)
        chips = [(1 - x, y), (x, 1 - y), (1 - x, 1 - y)]
        mine = 2 * x + y
        ids = [2 * cx + cy for cx, cy in chips]

        def part(i, slot, core):
            if i < nchip:
                return outs[i].at[slot, _half(arrs[i].shape[0], core)]
            return outs[i].at[slot, core]

        def rcopy(i, k, src, dst, to):
            return pltpu.make_async_remote_copy(src_ref=src, dst_ref=dst, send_sem=send_sems.at[i * NS + k],
                                                recv_sem=recv_sems.at[i * NS + k], device_id=to, device_id_type=MESH_IDS)

        started, locs = [], []
        for i in range(n):
            own = srcs[i].at[_half(arrs[i].shape[0], c)] if i < nchip else srcs[i]
            loc = pltpu.make_async_copy(srcs[i], outs[i].at[mine] if i < nchip else outs[i].at[mine, c], loc_sems.at[i])
            loc.start()
            locs.append(loc)
            for f, chip in enumerate(chips):
                cp = rcopy(i, f, own, part(i, mine, c), (*chip, c))
                cp.start()
                started.append(cp)
            if i >= nchip:
                cp = rcopy(i, 6, own, part(i, mine, c), sib)
                cp.start()
                started.append(cp)
        for i in range(n):
            for f in range(3):
                land = part(i, ids[f], c)
                rcopy(i, f, land, land, sib).wait_recv()
                fw = rcopy(i, 3 + f, land, land, sib)
                fw.start()
                started.append(fw)
        for i in range(n):
            for f in range(3):
                land = part(i, ids[f], 1 - c)
                rcopy(i, 3 + f, land, land, sib).wait_recv()
            if i >= nchip:
                land = part(i, mine, 1 - c)
                rcopy(i, 6, land, land, sib).wait_recv()
        for cp in started:
            cp.wait_send()
        for loc in locs:
            loc.wait()

    outs = [jax.ShapeDtypeStruct((4,) + a.shape, a.dtype) for a in chip_arrs]
    outs += [jax.ShapeDtypeStruct((4, 2) + a.shape, a.dtype) for a in dev_arrs]
    res = pl.pallas_call(
        body, name=name, out_shape=outs,
        in_specs=[pl.BlockSpec(memory_space=pl.ANY)] * n, out_specs=[pl.BlockSpec(memory_space=pl.ANY)] * n,
        scratch_shapes=[pltpu.SemaphoreType.DMA((n * NS,)), pltpu.SemaphoreType.DMA((n * NS,)),
                        pltpu.SemaphoreType.DMA((n,))],
    )(*arrs)
    return res[:nchip], res[nchip:]


def _mm(As, Bs, out_dtype, name, tm=512, cap=1408, bt=False):
    n = len(As)
    M, N = As[0].shape[0], Bs[0].shape[0 if bt else 1]
    if sum(a.shape[1] for a in As) <= 1024:
        tm = 2 * tm
    tm = min(tm, M)
    tn = _pick(N, cap)
    dims = (((1,), (1,)), ((), ())) if bt else (((1,), (0,)), ((), ()))

    def body(*refs):
        o = refs[2 * n]
        acc = None
        for a, b in zip(refs[:n], refs[n:2 * n]):
            d = lax.dot_general(a[...].astype(MXU_DTYPE), b[...].astype(MXU_DTYPE), dims, preferred_element_type=F32)
            acc = d if acc is None else acc + d
        o[...] = acc.astype(o.dtype)

    in_specs = [pl.BlockSpec((tm, a.shape[1]), lambda i, j: (i, 0)) for a in As]
    if bt:
        in_specs += [pl.BlockSpec((tn, b.shape[1]), lambda i, j: (j, 0)) for b in Bs]
    else:
        in_specs += [pl.BlockSpec((b.shape[0], tn), lambda i, j: (0, j)) for b in Bs]
    return pl.pallas_call(
        body, name=name, grid=(M // tm, N // tn), in_specs=in_specs,
        out_specs=pl.BlockSpec((tm, tn), lambda i, j: (i, j)),
        out_shape=jax.ShapeDtypeStruct((M, N), out_dtype),
        compiler_params=_params(("parallel", "parallel")),
    )(*As, *Bs)


def _mm_split(a, b, splits, name, tm=512):
    M, K = a.shape
    tm = min(tm, M)

    def body(a_ref, b_ref, *outs):
        av = a_ref[...].astype(MXU_DTYPE)
        for o, (c0, wd, _) in zip(outs, splits):
            o[...] = jnp.dot(av, b_ref[:, c0:c0 + wd].astype(MXU_DTYPE), preferred_element_type=F32).astype(o.dtype)

    return pl.pallas_call(
        body, name=name, grid=(M // tm,),
        in_specs=[pl.BlockSpec((tm, K), lambda i: (i, 0)), pl.BlockSpec(b.shape, lambda i: (0, 0))],
        out_specs=[pl.BlockSpec((tm, wd), lambda i: (i, 0)) for _, wd, _ in splits],
        out_shape=[jax.ShapeDtypeStruct((M, wd), dt) for _, wd, dt in splits],
        compiler_params=_params(("parallel",)),
    )(a, b)


def _mm_tn(A, G, name, tt=2048, cap=1408):
    T, Ka = A.shape
    N = G.shape[1]
    tt = min(tt, T)
    tk = _pick(Ka, cap)
    tn = _pick(N, cap)

    def body(a, g, o):
        @pl.when(pl.program_id(2) == 0)
        def _():
            o[...] = jnp.zeros(o.shape, F32)
        o[...] += lax.dot_general(a[...].astype(MXU_DTYPE), g[...].astype(MXU_DTYPE),
                                  (((0,), (0,)), ((), ())), preferred_element_type=F32)

    return pl.pallas_call(
        body, name=name, grid=(Ka // tk, N // tn, T // tt),
        in_specs=[pl.BlockSpec((tt, tk), lambda i, j, t: (t, i)), pl.BlockSpec((tt, tn), lambda i, j, t: (t, j))],
        out_specs=pl.BlockSpec((tk, tn), lambda i, j, t: (i, j)),
        out_shape=jax.ShapeDtypeStruct((Ka, N), F32),
        compiler_params=_params(("parallel", "parallel", "arbitrary")),
    )(A, G)


def _rowwise(name, fn, *, Bl, S, R, tiled=(), prev=(), nxt=(), batch=(), full=(),
             out_tiled=(), out_batch=(), out_acc=()):
    R = min(R, S)
    nS = S // R
    T = Bl * S
    hb = R // HALO
    n_in = len(tiled) + len(prev) + len(nxt) + len(batch) + len(full)

    in_specs, args = [], []
    for a, wd, cb in tiled:
        in_specs.append(pl.BlockSpec((R, wd), lambda b, i, cb=cb: (b * nS + i, cb)))
        args.append(a)
    for a, wd, cb in prev:
        in_specs.append(pl.BlockSpec((HALO, wd), lambda b, i, cb=cb: (jnp.maximum((b * nS + i) * hb - 1, 0), cb)))
        args.append(a)
    for a, wd, cb in nxt:
        in_specs.append(pl.BlockSpec((HALO, wd), lambda b, i, cb=cb: (jnp.minimum((b * nS + i + 1) * hb, T // HALO - 1), cb)))
        args.append(a)
    for a, wd, cb in batch:
        in_specs.append(pl.BlockSpec((1, 1, wd), lambda b, i, cb=cb: (b, 0, cb)))
        args.append(a)
    for a in full:
        in_specs.append(pl.BlockSpec(a.shape, lambda b, i, nd=a.ndim: (0,) * nd))
        args.append(a)

    out_specs, out_shape = [], []
    for C, dt in out_tiled:
        out_specs.append(pl.BlockSpec((R, C), lambda b, i: (b * nS + i, 0)))
        out_shape.append(jax.ShapeDtypeStruct((T, C), dt))
    for C in out_batch:
        out_specs.append(pl.BlockSpec((1, 1, C), lambda b, i: (b, 0, 0)))
        out_shape.append(jax.ShapeDtypeStruct((Bl, 1, C), F32))
    for shp in out_acc:
        out_specs.append(pl.BlockSpec(shp, lambda b, i, nd=len(shp): (0,) * nd))
        out_shape.append(jax.ShapeDtypeStruct(shp, F32))

    nt, npv, nnx, nbt = len(tiled), len(prev), len(nxt), len(batch)

    def body(*refs):
        b, i = pl.program_id(0), pl.program_id(1)
        ins, outs = refs[:n_in], refs[n_in:]
        vals = [r[...] for r in ins[:nt]]
        vals += [jnp.where(i > 0, r[...], jnp.zeros(r.shape, r.dtype)) for r in ins[nt:nt + npv]]
        vals += [jnp.where(i < nS - 1, r[...], jnp.zeros(r.shape, r.dtype)) for r in ins[nt + npv:nt + npv + nnx]]
        vals += [r[0] for r in ins[nt + npv + nnx:nt + npv + nnx + nbt]]
        vals += [r[...] for r in ins[nt + npv + nnx + nbt:]]
        res = fn(*vals)
        if not isinstance(res, (tuple, list)):
            res = (res,)
        k = 0
        for _ in out_tiled:
            outs[k][...] = res[k].astype(outs[k].dtype)
            k += 1
        for _ in out_batch:
            o = outs[k]

            @pl.when(i == 0)
            def _(o=o):
                o[...] = jnp.zeros(o.shape, F32)
            o[0] += res[k]
            k += 1
        for _ in out_acc:
            o = outs[k]

            @pl.when((i == 0) & (b == 0))
            def _(o=o):
                o[...] = jnp.zeros(o.shape, F32)
            o[...] += res[k]
            k += 1

    out = pl.pallas_call(
        body, name=name, grid=(Bl, nS), in_specs=in_specs, out_specs=out_specs, out_shape=out_shape,
        compiler_params=_params(("arbitrary", "arbitrary")),
    )(*args)
    return out


def _colwise(name, fn, *, Bl, S, R, W, strip, tiled=(), prev=(), nxt=(), full=(), out_tiled=(), n_acc=0):
    R = min(R, S)
    nS = S // R
    T = Bl * S
    hb = R // HALO
    nt, npv, nnx, nfl = len(tiled), len(prev), len(nxt), len(full)
    n_in = nt + npv + nnx + nfl
    in_specs = [pl.BlockSpec((R, W), lambda b, i, cb=cb: (b * nS + i, cb)) for _, cb in tiled]
    in_specs += [pl.BlockSpec((HALO, W), lambda b, i, cb=cb: (jnp.maximum((b * nS + i) * hb - 1, 0), cb)) for _, cb in prev]
    in_specs += [pl.BlockSpec((HALO, W), lambda b, i, cb=cb: (jnp.minimum((b * nS + i + 1) * hb, T // HALO - 1), cb))
                 for _, cb in nxt]
    in_specs += [pl.BlockSpec(a.shape, lambda b, i: (0, 0)) for a in full]
    out_specs = [pl.BlockSpec((R, m * W), lambda b, i: (b * nS + i, 0)) for m, _ in out_tiled]
    out_specs += [pl.BlockSpec((1, W), lambda b, i: (0, 0))] * n_acc
    out_shape = [jax.ShapeDtypeStruct((T, m * W), dt) for m, dt in out_tiled] + [jax.ShapeDtypeStruct((1, W), F32)] * n_acc

    def body(*refs):
        b, i = pl.program_id(0), pl.program_id(1)
        ins, outs = refs[:n_in], refs[n_in:]

        @pl.when((i == 0) & (b == 0))
        def _():
            for o in outs[len(out_tiled):]:
                o[...] = jnp.zeros(o.shape, F32)

        def col(j, carry):
            cs = pl.ds(pl.multiple_of(j * strip, strip), strip)
            vals = [r[:, cs] for r in ins[:nt]]
            vals += [jnp.where(i > 0, r[:, cs], jnp.zeros((HALO, strip), r.dtype)) for r in ins[nt:nt + npv]]
            vals += [jnp.where(i < nS - 1, r[:, cs], jnp.zeros((HALO, strip), r.dtype)) for r in ins[nt + npv:nt + npv + nnx]]
            vals += [r[:, cs] for r in ins[nt + npv + nnx:]]
            res = fn(*vals)
            for k, (m, _) in enumerate(out_tiled):
                for q in range(m):
                    outs[k][:, pl.ds(pl.multiple_of(q * W + j * strip, strip), strip)] = res[k][q].astype(outs[k].dtype)
            for k in range(len(out_tiled), len(outs)):
                outs[k][:, cs] += res[k]
            return carry

        lax.fori_loop(0, W // strip, col, 0)

    return pl.pallas_call(
        body, name=name, grid=(Bl, nS), in_specs=in_specs, out_specs=out_specs, out_shape=out_shape,
        compiler_params=_params(("arbitrary", "arbitrary")),
    )(*[a for a, _ in tiled], *[a for a, _ in prev], *[a for a, _ in nxt], *full)


def _shift_down(x, halo, k):
    rolled = pltpu.roll(x, k, 0)
    row = lax.broadcasted_iota(jnp.int32, (SUBLANES, x.shape[1]), 0)
    head = rolled[0:SUBLANES]
    for j in range(k):
        head = jnp.where(row == j, halo[HALO - k + j:HALO - k + j + 1, :], head)
    return jnp.concatenate([head, rolled[SUBLANES:]], axis=0)


def _shift_up(x, halo, k):
    n = x.shape[0]
    rolled = pltpu.roll(x, n - k, 0)
    row = lax.broadcasted_iota(jnp.int32, (SUBLANES, x.shape[1]), 0)
    tail = rolled[n - SUBLANES:]
    for j in range(k):
        tail = jnp.where(row == SUBLANES - k + j, halo[j:j + 1, :], tail)
    return jnp.concatenate([rolled[:n - SUBLANES], tail], axis=0)


def _dotm(a, b):
    return jnp.dot(a.astype(MXU_DTYPE), b.astype(MXU_DTYPE), preferred_element_type=F32)


def _split_bf16(x):
    hi = x.astype(BF16)
    return hi, (x - hi.astype(F32)).astype(BF16)


def _headsum_2pass(x, hm):
    hi, lo = _split_bf16(x)
    hb = hm.astype(BF16)
    return jnp.dot(hi, hb, preferred_element_type=F32) + jnp.dot(lo, hb, preferred_element_type=F32)


@jax.custom_vjp
def _headsum(x, hm):
    return _headsum_2pass(x, hm)


_headsum.defvjp(lambda x, hm: (_headsum_2pass(x, hm), hm),
                lambda hm, g: (_headsum_2pass(g, hm), jnp.zeros_like(hm)))


def _sigmoid(x):
    return 0.5 * jnp.tanh(0.5 * x) + 0.5


def _rms(x, g):
    return x * lax.rsqrt(jnp.mean(x * x, axis=-1, keepdims=True) + RMS_EPS) * g


def _norm_mod(x, g, sc, sh):
    return _rms(x, g) * (1.0 + sc) + sh


def _split_ps(ps):
    return (ps[:, 0:RW], ps[:, RW:2 * RW], ps[:, 2 * RW:3 * RW], ps[:, 3 * RW:3 * RW + LW + LA],
            ps[:, 3 * RW + LW + LA:SHIFT])


def _rwkv_prep(r, k, v, wa, gd, w0, w_up_p, a0, a_up_p, g_up, k_k, k_a, hm):
    w_raw = w0 + _dotm(jnp.tanh(wa), w_up_p)
    decay = jnp.exp(-DECAY_SCALE * _sigmoid(w_raw))
    a = _sigmoid(a0 + _dotm(wa, a_up_p))
    g = _dotm(_sigmoid(gd), g_up)
    kk = k * k_k
    kk = kk * lax.rsqrt(_headsum(kk * kk, hm) + L2_EPS)
    k2 = k * (1.0 + (a - 1.0) * k_a)
    return r, decay, k2, v, -kk, kk * a, g


def _rwkv_post(y, r, k2, v, g, ln_g, ln_b, r_k, hm):
    mean = _headsum(y, hm) * (1.0 / HD)
    yc = y - mean
    var = _headsum(yc * yc, hm) * (1.0 / HD)
    yn = yc * lax.rsqrt(var + GN_EPS) * ln_g + ln_b
    bonus = _headsum(r * k2 * r_k, hm) * v
    return (yn + bonus) * g


def _gelu(x):
    return 0.5 * x * (1.0 + jnp.tanh(GELU_C * (x + 0.044715 * (x * x * x))))


def _s5_post(yssm, u, d):
    return _gelu(yssm + d * u)


def _mix(ga, gb, ya, za, zb):
    return _sigmoid(ga) * ya + _sigmoid(gb) * (za * _sigmoid(zb))


def _conv_act(up_g, up_u, hg, hu, w_g, w_u, b_g, b_u):
    gate, upv = _conv3(up_g, hg, w_g, b_g)[0], _conv3(up_u, hu, w_u, b_u)[0]
    return gate, upv


def _conv3(x, h, w, b):
    x, h = x.astype(F32), h.astype(F32)
    s2, s1 = _shift_down(x, h, 2), _shift_down(x, h, 1)
    return b + w[0:1] * s2 + w[1:2] * s1 + w[2:3] * x, (s2, s1, x)


def _silu_gate(gate, upv):
    return gate * _sigmoid(gate) * upv


WKV_L = 64
WKV_KEPT = 5
_NT, _NN, _TN = ((1,), (1,)), ((1,), (0,)), ((0,), (0,))


def _dotw(x, y, dims):
    return lax.dot_general(x.astype(MXU_DTYPE), y.astype(MXU_DTYPE), (dims, ((), ())), preferred_element_type=F32)


def _dot3(x, y, dims):
    (xh, xl), (yh, yl) = _split_bf16(x), _split_bf16(y)
    d = lambda p, q: lax.dot_general(p, q, (dims, ((), ())), preferred_element_type=F32)
    return d(xh, yh) + d(xh, yl) + d(xl, yh)


@jax.custom_vjp
def _gram3(x, y):
    return _dot3(x, y, _NT)


_gram3.defvjp(lambda x, y: (_dot3(x, y, _NT), (x, y)),
              lambda res, g: (_dot3(g, res[1], _NN), _dot3(g, res[0], _TN)))


@jax.custom_vjp
def _gram_known(x, y, value):
    return value


_gram_known.defvjp(lambda x, y, value: (value, (x, y, value)),
                   lambda res, g: (_dot3(g, res[1], _NN), _dot3(g, res[0], _TN), jnp.zeros_like(res[2])))


def _tri_solve_fwd(ns, xs):
    each = lambda f, *ls: tuple(f(*zs) for zs in zip(*ls))
    size = ns[0].shape[0]
    eye = (lax.broadcasted_iota(jnp.int32, (size, size), 0) == lax.broadcasted_iota(jnp.int32, (size, size), 1)).astype(F32)
    ts = each(lambda n: n + eye, ns)
    qs = ns
    for _ in range(WKV_L.bit_length() - 2):
        qs = each(lambda q: _dotw(q, q, _NN), qs)
        ts = each(lambda t, q: t + _dotw(t, q, _NN), ts, qs)
    us = each(lambda t, x: _dotw(t, x, _NN), ts, xs)
    return us, (ts, us)


def _tri_solve_bwd(res, dus):
    ts, us = res
    each = lambda f, *ls: tuple(f(*zs) for zs in zip(*ls))
    dxs = each(lambda t, du: _dotw(t, du, _TN), ts, dus)
    return each(lambda dx, u: _dotw(dx, u, _NT), dxs, us), dxs


@jax.custom_vjp
def _tri_solve(ns, xs):
    return _tri_solve_fwd(ns, xs)[0]


_tri_solve.defvjp(_tri_solve_fwd, _tri_solve_bwd)


@jax.custom_vjp
def _tri_known(ns, xs, ts, us):
    return us


_tri_known.defvjp(lambda ns, xs, ts, us: (us, (ts, us)),
                  lambda res, dus: _tri_solve_bwd(res, dus) + tuple(tuple(jnp.zeros_like(z) for z in r) for r in res))


def _wkv_chunk(s0, r, w, k, v, a, b):
    y, s1 = _wkv_chunks((s0,), (r,), (w,), (k,), (v,), (a,), (b,))
    return y[0], s1[0]


def _wkv_chunks(s0, r, w, k, v, a, b, tinv=None, want_tinv=False):
    each = lambda f, *ls: tuple(f(*xs) for xs in zip(*ls))
    L = r[0].shape[0]
    n2 = 2 * L
    lane_head = lax.broadcasted_iota(jnp.int32, (2, 1, 2 * HD), 2) // HD
    head_mask = (lane_head == lax.broadcasted_iota(jnp.int32, (2, 1, 2 * HD), 0)).astype(F32)
    ri = lax.broadcasted_iota(jnp.int32, (n2, n2), 0)
    ci = lax.broadcasted_iota(jnp.int32, (n2, n2), 1)
    same = (ri // L) == (ci // L)
    strict = same & ((ci % L) < (ri % L))
    incl = same & ((ci % L) <= (ri % L))
    si = lax.broadcasted_iota(jnp.int32, (2 * HD, 2 * HD), 0) // HD
    sj = lax.broadcasted_iota(jnp.int32, (2 * HD, 2 * HD), 1) // HD
    tri = (lax.broadcasted_iota(jnp.int32, (L, L), 0) >= lax.broadcasted_iota(jnp.int32, (L, L), 1)).astype(F32)

    stack = lambda z: (z[None] * head_mask).reshape(n2, 2 * HD)
    dup = lambda z: jnp.broadcast_to(z[None], (2, L, 2 * HD)).reshape(n2, 2 * HD)
    gram = _gram3
    nt, nn, tn = (lambda x, y, d=d: _dotw(x, y, d) for d in (_NT, _NN, _TN))
    add = lambda x, y: x + y

    lw = each(jnp.log, w)
    cum = each(lambda z: jnp.dot(tri, z, preferred_element_type=F32, precision=HIGHEST), lw)
    tot = each(lambda z: jnp.sum(z, axis=0, keepdims=True), lw)
    a2 = each(lambda av, cv, lv: stack(av * jnp.exp(cv - lv)), a, cum, lw)
    r2 = each(lambda rv, cv: stack(rv * jnp.exp(cv)), r, cum)
    v2 = each(stack, v)
    b2 = each(lambda bv, cv: dup(bv * jnp.exp(-cv)), b, cum)
    k2 = each(lambda kv, cv: dup(kv * jnp.exp(-cv)), k, cum)
    n_ab = each(lambda x, y: jnp.where(strict, gram(x, y), 0.0), a2, b2)
    if tinv is None:
        n_ak = each(lambda x, y: jnp.where(strict, gram(x, y), 0.0), a2, k2)
        m_rb = each(lambda x, y: jnp.where(incl, gram(x, y), 0.0), r2, b2)
        m_rk = each(lambda x, y: jnp.where(incl, gram(x, y), 0.0), r2, k2)
    else:
        tinv, k_u, k_ak, k_rb, k_rk = tinv
        n_ak = each(lambda x, y, g: jnp.where(strict, _gram_known(x, y, g), 0.0), a2, k2, k_ak)
        m_rb = each(lambda x, y, g: jnp.where(incl, _gram_known(x, y, g), 0.0), r2, b2, k_rb)
        m_rk = each(lambda x, y, g: jnp.where(incl, _gram_known(x, y, g), 0.0), r2, k2, k_rk)
    x = each(add, each(nt, a2, s0), each(nn, n_ak, v2))
    if want_tinv:
        u, (tinv, _) = _tri_solve_fwd(n_ab, x)
        tinv = (tinv, u, n_ak, m_rb, m_rk)
    else:
        u = _tri_solve(n_ab, x) if tinv is None else _tri_known(n_ab, x, tinv, k_u)
    y2 = each(lambda x, y, z: x + y + z, each(nt, r2, s0), each(nn, m_rb, u), each(nn, m_rk, v2))
    y = each(lambda z: jnp.sum(z.reshape(2, L, 2 * HD), axis=0), y2)
    b3 = each(lambda bv, tv, cv: dup(bv * jnp.exp(tv - cv)), b, tot, cum)
    k3 = each(lambda kv, tv, cv: dup(kv * jnp.exp(tv - cv)), k, tot, cum)
    upd = each(add, each(tn, u, b3), each(tn, v2, k3))
    s1 = each(lambda sv, tv, uv: sv * jnp.exp(tv) + jnp.where(si == sj, uv, 0.0), s0, tot, upd)
    return (y, s1, tinv) if want_tinv else (y, s1)


NPAIR = NH // 2


def _wkv_nb(Bl):
    return 4 if Bl % 4 == 0 else 2 if Bl % 2 == 0 else 1


def _wkv_fwd(r, w, k, v, a, b, Bl, S):
    L = WKV_L
    nC = S // L
    nb = _wkv_nb(Bl)
    chains = [(bi, p, slice(p * 2 * HD, (p + 1) * 2 * HD)) for bi in range(nb) for p in range(NPAIR)]

    def body(r_ref, w_ref, k_ref, v_ref, a_ref, b_ref, y_ref, ck_ref, ti_ref, s_ref):
        @pl.when(pl.program_id(1) == 0)
        def _():
            s_ref[...] = jnp.zeros(s_ref.shape, F32)
        s0 = tuple(s_ref[bi, p] for bi, p, _ in chains)
        ops = [tuple(z[bi, :, cs] for bi, _, cs in chains) for z in (r_ref, w_ref, k_ref, v_ref, a_ref, b_ref)]
        y, s1, kept = _wkv_chunks(s0, *ops, want_tinv=True)
        for i, (bi, p, cs) in enumerate(chains):
            ck_ref[bi, 0, p] = s0[i]
            for q in range(WKV_KEPT):
                ti_ref[bi, 0, p, q] = kept[q][i]
            y_ref[bi, :, cs] = y[i]
            s_ref[bi, p] = s1[i]

    to3 = lambda z: z.reshape(Bl, S, RW)
    row_spec = pl.BlockSpec((nb, L, RW), lambda g, c: (g, c, 0))
    mats = jax.ShapeDtypeStruct((Bl, nC, NPAIR, 2 * HD, 2 * HD), F32)
    mat_spec = pl.BlockSpec((nb, 1, NPAIR, 2 * HD, 2 * HD), lambda g, c: (g, c, 0, 0, 0))
    y, ck, ti = pl.pallas_call(
        body, name="wkv_fwd", grid=(Bl // nb, nC), in_specs=[row_spec] * 6,
        out_specs=[row_spec, mat_spec,
                   pl.BlockSpec((nb, 1, NPAIR, WKV_KEPT, 2 * HD, 2 * HD), lambda g, c: (g, c, 0, 0, 0, 0))],
        out_shape=[jax.ShapeDtypeStruct((Bl, S, RW), F32), mats,
                   jax.ShapeDtypeStruct((Bl, nC, NPAIR, WKV_KEPT, 2 * HD, 2 * HD), F32)],
        scratch_shapes=[pltpu.VMEM((nb, NPAIR, 2 * HD, 2 * HD), F32)],
        compiler_params=_params(("arbitrary", "arbitrary")),
    )(*(to3(z) for z in (r, w, k, v, a, b)))
    return y.reshape(Bl * S, RW), ck, ti


def _wkv_bwd(r, w, k, v, a, b, dy, ck, ti, Bl, S):
    L = WKV_L
    nC = S // L
    nb = _wkv_nb(Bl)
    chains = [(bi, p, slice(p * 2 * HD, (p + 1) * 2 * HD)) for bi in range(nb) for p in range(NPAIR)]

    def body(r_ref, w_ref, k_ref, v_ref, a_ref, b_ref, dy_ref, ck_ref, ti_ref,
             dr_ref, dw_ref, dk_ref, dv_ref, da_ref, db_ref, ds_ref):
        @pl.when(pl.program_id(1) == 0)
        def _():
            ds_ref[...] = jnp.zeros(ds_ref.shape, F32)
        s0 = tuple(ck_ref[bi, 0, p] for bi, p, _ in chains)
        tinv = tuple(tuple(ti_ref[bi, 0, p, q] for bi, p, _ in chains) for q in range(WKV_KEPT))
        ops = [tuple(z[bi, :, cs] for bi, _, cs in chains) for z in (r_ref, w_ref, k_ref, v_ref, a_ref, b_ref)]
        cts = (tuple(dy_ref[bi, :, cs] for bi, _, cs in chains), tuple(ds_ref[bi, p] for bi, p, _ in chains))
        ds0, *grads = jax.vjp(lambda *z: _wkv_chunks(*z, tinv=tinv), s0, *ops)[1](cts)
        for i, (bi, p, cs) in enumerate(chains):
            ds_ref[bi, p] = ds0[i]
            for o, g in zip((dr_ref, dw_ref, dk_ref, dv_ref, da_ref, db_ref), grads):
                o[bi, :, cs] = g[i]

    to3 = lambda z: z.reshape(Bl, S, RW)
    row_spec = pl.BlockSpec((nb, L, RW), lambda g, c: (g, nC - 1 - c, 0))
    rows = jax.ShapeDtypeStruct((Bl, S, RW), F32)
    mat_spec = pl.BlockSpec((nb, 1, NPAIR, 2 * HD, 2 * HD), lambda g, c: (g, nC - 1 - c, 0, 0, 0))
    outs = pl.pallas_call(
        body, name="wkv_bwd", grid=(Bl // nb, nC),
        in_specs=[row_spec] * 7 + [mat_spec, pl.BlockSpec((nb, 1, NPAIR, WKV_KEPT, 2 * HD, 2 * HD),
                                                          lambda g, c: (g, nC - 1 - c, 0, 0, 0, 0))],
        out_specs=[row_spec] * 6, out_shape=[rows] * 6,
        scratch_shapes=[pltpu.VMEM((nb, NPAIR, 2 * HD, 2 * HD), F32)],
        compiler_params=_params(("arbitrary", "arbitrary")),
    )(*(to3(z) for z in (r, w, k, v, a, b, dy)), ck, ti)
    return [o.reshape(Bl * S, RW) for o in outs]


NST = NG * SP


def _cmul(ar, ai, br, bi):
    return ar * br - ai * bi, ar * bi + ai * br


def _s5_tiles(are, aim, reverse):
    if reverse:
        aim = -aim
    row = lax.broadcasted_iota(jnp.int32, (SUBLANES, NST), 0)
    pw = [(are, aim)]
    for _ in range(SUBLANES - 1):
        pw.append(_cmul(pw[-1][0], pw[-1][1], are, aim))
    bc = lambda z: jnp.broadcast_to(z, (SUBLANES, NST))
    ms = []
    for kk in (1, 2, 4):
        cond = (row < SUBLANES - kk) if reverse else (row >= kk)
        ms.append((jnp.where(cond, bc(pw[kk - 1][0]), 0.0), jnp.where(cond, bc(pw[kk - 1][1]), 0.0)))
    pr = jnp.zeros((SUBLANES, NST), F32)
    pi = jnp.zeros((SUBLANES, NST), F32)
    for i in range(SUBLANES):
        n = SUBLANES - i if reverse else i + 1
        pr = jnp.where(row == i, bc(pw[n - 1][0]), pr)
        pi = jnp.where(row == i, bc(pw[n - 1][1]), pi)
    return ms, (pr, pi)


def _s5_block(re, im, ms, pc, cre, cim, sg, reverse):
    ln = slice(sg * 512, (sg + 1) * 512)
    for (mr, mi), kk in zip(ms, (1, 2, 4)):
        sh = SUBLANES - kk if reverse else kk
        sre, sim = pltpu.roll(re, sh, 0), pltpu.roll(im, sh, 0)
        tr, ti = _cmul(mr[:, ln], mi[:, ln], sre, sim)
        re, im = re + tr, im + ti
    tr, ti = _cmul(pc[0][:, ln], pc[1][:, ln], cre[:, ln], cim[:, ln])
    return re + tr, im + ti


def _s5_scan(X_ref, n_rows, ms, pc, cre, cim, reverse, visit=None, acc0=None):
    nblk = n_rows // SUBLANES

    def it(i, carry):
        cre, cim, acc = carry
        j = nblk - 1 - i if reverse else i
        rows = pl.ds(pl.multiple_of(j * SUBLANES, SUBLANES), SUBLANES)
        edge = 0 if reverse else SUBLANES - 1
        blocks, ncre, ncim = [], [], []
        for sg in range(NSG):
            lr = slice(sg * 1024, sg * 1024 + 512)
            li = slice(sg * 1024 + 512, (sg + 1) * 1024)
            re, im = _s5_block(X_ref[rows, lr], X_ref[rows, li], ms, pc, cre, cim, sg, reverse)
            X_ref[rows, lr] = re
            X_ref[rows, li] = im
            blocks.append((re, im))
            ncre.append(re[edge:edge + 1])
            ncim.append(im[edge:edge + 1])
        if visit is not None:
            acc = visit(j, blocks, acc)
        return jnp.concatenate(ncre, axis=1), jnp.concatenate(ncim, axis=1), acc

    return lax.fori_loop(0, nblk, it, (cre, cim, acc0 if acc0 is not None else 0))


def _s5_fwd(u, wb, wc, ab, d, Bl, S, R=256):
    R = min(R, S)
    nC = S // R

    def body(u_ref, wb_ref, wc_ref, ab_ref, d_ref, y_ref, st_ref, X_ref, o_ref, car_ref):
        @pl.when(pl.program_id(1) == 0)
        def _():
            car_ref[...] = jnp.zeros(car_ref.shape, F32)
        st_ref[0, 0] = car_ref[...]
        ms, pc = _s5_tiles(ab_ref[0:1], ab_ref[1:2], False)
        for sg in range(NSG):
            X_ref[:, sg * 1024:(sg + 1) * 1024] = _dotm(u_ref[:, sg * 128:(sg + 1) * 128], wb_ref[sg])
        cre, cim, _ = _s5_scan(X_ref, R, ms, pc, car_ref[0:1], car_ref[1:2], False)
        car_ref[0:1] = cre
        car_ref[1:2] = cim
        for sg in range(NSG):
            y_ref[:, sg * 128:(sg + 1) * 128] = _dotm(X_ref[:, sg * 1024:(sg + 1) * 1024], wc_ref[sg])
        o_ref[...] = _s5_post(y_ref[...], u_ref[...], d_ref[...]).astype(o_ref.dtype)

    rows = pl.BlockSpec((R, SW), lambda b, c: (b * nC + c, 0))
    return pl.pallas_call(
        body, name="s5_fwd", grid=(Bl, nC),
        in_specs=[rows, pl.BlockSpec(wb.shape, lambda b, c: (0, 0, 0)), pl.BlockSpec(wc.shape, lambda b, c: (0, 0, 0)),
                  pl.BlockSpec(ab.shape, lambda b, c: (0, 0)), pl.BlockSpec(d.shape, lambda b, c: (0, 0))],
        out_specs=[rows, pl.BlockSpec((1, 1, 2, NST), lambda b, c: (b, c, 0, 0)),
                   pl.BlockSpec((R, 2 * NST), lambda b, c: (b * nC + c, 0)), rows],
        out_shape=[jax.ShapeDtypeStruct((Bl * S, SW), F32), jax.ShapeDtypeStruct((Bl, nC, 2, NST), F32),
                   jax.ShapeDtypeStruct((Bl * S, 2 * NST), F32), jax.ShapeDtypeStruct((Bl * S, SW), MXU_DTYPE)],
        scratch_shapes=[pltpu.VMEM((2, NST), F32)],
        compiler_params=_params(("arbitrary", "arbitrary")),
    )(u, wb, wc, ab, d)


def _s5_bwd(u, y, do, d, wb, wc, ab, st, xs, Bl, S, R=256):
    R = min(R, S)
    nC = S // R

    def body(u_ref, y_ref, do_ref, d_ref, wb_ref, wc_ref, ab_ref, st_ref, X_ref,
             du_ref, dwb_ref, dwc_ref, dab_ref, dd_ref, G_ref, car_ref):
        first = (pl.program_id(0) == 0) & (pl.program_id(1) == 0)

        @pl.when(first)
        def _():
            for o in (dwb_ref, dwc_ref, dab_ref, dd_ref):
                o[...] = jnp.zeros(o.shape, F32)

        @pl.when(pl.program_id(1) == 0)
        def _():
            car_ref[...] = jnp.zeros(car_ref.shape, F32)

        are, aim = ab_ref[0:1], ab_ref[1:2]
        dy, du_direct, dd = jax.vjp(_s5_post, y_ref[...], u_ref[...], d_ref[...])[1](do_ref[...])
        dd_ref[...] += dd
        dyv = dy.astype(MXU_DTYPE)
        for sg in range(NSG):
            G_ref[:, sg * 1024:(sg + 1) * 1024] = lax.dot_general(
                dyv[:, sg * 128:(sg + 1) * 128], wc_ref[sg].astype(MXU_DTYPE), (((1,), (1,)), ((), ())),
                preferred_element_type=F32)
        rms_, rpc = _s5_tiles(are, aim, True)
        row = lax.broadcasted_iota(jnp.int32, (SUBLANES, 512), 0)

        def visit(j, blocks, acc):
            before = pl.multiple_of(jnp.maximum(j - 1, 0) * SUBLANES, SUBLANES)
            prow = X_ref[pl.ds(before, SUBLANES), :][SUBLANES - 1:SUBLANES]
            rows = pl.ds(pl.multiple_of(j * SUBLANES, SUBLANES), SUBLANES)
            are_acc, aim_acc = [], []
            for sg in range(NSG):
                lr = slice(sg * 1024, sg * 1024 + 512)
                li = slice(sg * 1024 + 512, (sg + 1) * 1024)
                ln = slice(sg * 512, (sg + 1) * 512)
                pre = jnp.where(j > 0, prow[:, lr], st_ref[0, 0, 0:1, ln])
                pim = jnp.where(j > 0, prow[:, li], st_ref[0, 0, 1:2, ln])
                xre = jnp.where(row == 0, pre, pltpu.roll(X_ref[rows, lr], 1, 0))
                xim = jnp.where(row == 0, pim, pltpu.roll(X_ref[rows, li], 1, 0))
                dre, dim = blocks[sg]
                are_acc.append(dre * xre + dim * xim)
                aim_acc.append(dim * xre - dre * xim)
            return acc[0] + jnp.concatenate(are_acc, axis=1), acc[1] + jnp.concatenate(aim_acc, axis=1)

        zero = jnp.zeros((SUBLANES, NST), F32)
        cre, cim, acc = _s5_scan(G_ref, R, rms_, rpc, car_ref[0:1], car_ref[1:2], True, visit, (zero, zero))
        car_ref[0:1] = cre
        car_ref[1:2] = cim
        dab_ref[0:1] += jnp.sum(acc[0], axis=0, keepdims=True)
        dab_ref[1:2] += jnp.sum(acc[1], axis=0, keepdims=True)
        uv = u_ref[...].astype(MXU_DTYPE)
        for sg in range(NSG):
            cs = slice(sg * 1024, (sg + 1) * 1024)
            us = slice(sg * 128, (sg + 1) * 128)
            gx = G_ref[:, cs].astype(MXU_DTYPE)
            dwb_ref[sg] += lax.dot_general(uv[:, us], gx, (((0,), (0,)), ((), ())), preferred_element_type=F32)
            dwc_ref[sg] += lax.dot_general(X_ref[:, cs].astype(MXU_DTYPE), dyv[:, us], (((0,), (0,)), ((), ())),
                                           preferred_element_type=F32)
            du_ssm = lax.dot_general(gx, wb_ref[sg].astype(MXU_DTYPE), (((1,), (1,)), ((), ())),
                                     preferred_element_type=F32)
            du_ref[:, us] = (du_ssm + du_direct[:, us]).astype(du_ref.dtype)

    rmap = lambda b, c: (b * nC + nC - 1 - c, 0)
    rows = pl.BlockSpec((R, SW), rmap)
    return pl.pallas_call(
        body, name="s5_bwd", grid=(Bl, nC),
        in_specs=[rows, rows, rows, pl.BlockSpec(d.shape, lambda b, c: (0, 0)),
                  pl.BlockSpec(wb.shape, lambda b, c: (0, 0, 0)), pl.BlockSpec(wc.shape, lambda b, c: (0, 0, 0)),
                  pl.BlockSpec(ab.shape, lambda b, c: (0, 0)),
                  pl.BlockSpec((1, 1, 2, NST), lambda b, c: (b, nC - 1 - c, 0, 0)),
                  pl.BlockSpec((R, 2 * NST), rmap)],
        out_specs=[rows, pl.BlockSpec(wb.shape, lambda b, c: (0, 0, 0)),
                   pl.BlockSpec(wc.shape, lambda b, c: (0, 0, 0)), pl.BlockSpec((2, NST), lambda b, c: (0, 0)),
                   pl.BlockSpec(d.shape, lambda b, c: (0, 0))],
        out_shape=[jax.ShapeDtypeStruct((Bl * S, SW), MXU_DTYPE), jax.ShapeDtypeStruct(wb.shape, F32),
                   jax.ShapeDtypeStruct(wc.shape, F32), jax.ShapeDtypeStruct((2, NST), F32),
                   jax.ShapeDtypeStruct(d.shape, F32)],
        scratch_shapes=[pltpu.VMEM((R, 2 * NST), F32), pltpu.VMEM((2, NST), F32)],
        compiler_params=_params(("arbitrary", "arbitrary")),
    )(u, y, do, d, wb, wc, ab, st, xs)


def _s5_disc_math(a_re, a_im, log_dt, b_re, b_im, expand):
    dt = jnp.exp(log_dt)
    z_re, z_im = a_re * dt, a_im * dt
    mag = jnp.exp(z_re)
    ab_re, ab_im = mag * jnp.cos(z_im), mag * jnp.sin(z_im)
    den = a_re * a_re + a_im * a_im
    q_re = ((ab_re - 1.0) * a_re + ab_im * a_im) / den
    q_im = (ab_im * a_re - (ab_re - 1.0) * a_im) / den
    qe_re = jnp.dot(q_re, expand, preferred_element_type=F32, precision=HIGHEST)
    qe_im = jnp.dot(q_im, expand, preferred_element_type=F32, precision=HIGHEST)
    return ab_re, ab_im, qe_re * b_re - qe_im * b_im, qe_re * b_im + qe_im * b_re


def _whole(shape):
    return pl.BlockSpec(shape, lambda nd=len(shape): (0,) * nd)


def _s5_disc(a_re, a_im, log_dt, b_re, b_im, expand):
    def body(a, b, c, d, e, f, o0, o1, o2, o3):
        res = _s5_disc_math(a[...], b[...], c[...], d[...], e[...], f[...])
        for o, v in zip((o0, o1, o2, o3), res):
            o[...] = v
    ins = (a_re, a_im, log_dt, b_re, b_im, expand)
    outs = [jax.ShapeDtypeStruct(a_re.shape, F32)] * 2 + [jax.ShapeDtypeStruct(b_re.shape, F32)] * 2
    return pl.pallas_call(body, name="s5_disc", in_specs=[_whole(x.shape) for x in ins],
                          out_specs=[_whole(o.shape) for o in outs], out_shape=outs)(*ins)


def _s5_disc_bwd(a_re, a_im, log_dt, b_re, b_im, expand, cts):
    def body(a, b, c, d, e, f, g0, g1, g2, g3, o0, o1, o2, o3, o4):
        fn = lambda *p: _s5_disc_math(*p, f[...])
        _, vjp = jax.vjp(fn, a[...], b[...], c[...], d[...], e[...])
        for o, v in zip((o0, o1, o2, o3, o4), vjp((g0[...], g1[...], g2[...], g3[...]))):
            o[...] = v
    ins = (a_re, a_im, log_dt, b_re, b_im, expand) + tuple(cts)
    outs = [jax.ShapeDtypeStruct(x.shape, F32) for x in (a_re, a_im, log_dt, b_re, b_im)]
    return pl.pallas_call(body, name="s5_disc_bwd", in_specs=[_whole(x.shape) for x in ins],
                          out_specs=[_whole(o.shape) for o in outs], out_shape=outs)(*ins)


def _ada_fwd(c_all, w_shard, b_shard):
    def body(c_ref, w_ref, b_ref, o_ref):
        cv = c_ref[...]
        o_ref[...] = _dotm(cv * _sigmoid(cv), w_ref[...]) + b_ref[...]
    n = w_shard.shape[1]
    return pl.pallas_call(
        body, name="ada_fwd", in_specs=[_whole(c_all.shape), _whole(w_shard.shape), _whole(b_shard.shape)],
        out_specs=_whole((c_all.shape[0], n)), out_shape=jax.ShapeDtypeStruct((c_all.shape[0], n), F32),
        compiler_params=_params(),
    )(c_all, w_shard, b_shard)


def _ada_bwd(c_all, dmod_cols, dmod_all):
    def body(c_ref, dc_ref, da_ref, gw_ref, gb_ref):
        cv = c_ref[...]
        gw_ref[...] = lax.dot_general((cv * _sigmoid(cv)).astype(MXU_DTYPE), dc_ref[...].astype(MXU_DTYPE),
                                      (((0,), (0,)), ((), ())), preferred_element_type=F32)
        gb_ref[...] = jnp.sum(da_ref[...], axis=0, keepdims=True)
    n = dmod_cols.shape[1]
    return pl.pallas_call(
        body, name="ada_bwd", in_specs=[_whole(c_all.shape), _whole(dmod_cols.shape), _whole(dmod_all.shape)],
        out_specs=[_whole((D, n)), _whole((1, dmod_all.shape[1]))],
        out_shape=[jax.ShapeDtypeStruct((D, n), F32), jax.ShapeDtypeStruct((1, dmod_all.shape[1]), F32)],
        compiler_params=_params(),
    )(c_all, dmod_cols, dmod_all)


def _rows_block(n_rows, cap=512):
    if n_rows <= cap:
        return n_rows
    for t in range(cap - cap % SUBLANES, 0, -SUBLANES):
        if n_rows % t == 0:
            return t
    return n_rows


def _adamw(w, g, m, v, name):
    rows, cols = w.shape
    tr = _rows_block(rows, max(SUBLANES, (1 << 19) // max(cols, 1) // SUBLANES * SUBLANES))

    def body(w_ref, g_ref, m_ref, v_ref, d_ref, nm_ref, nv_ref):
        gv = g_ref[...]
        nm = B1 * m_ref[...] + (1.0 - B1) * gv
        nv = B2 * v_ref[...] + (1.0 - B2) * (gv * gv)
        m_hat = nm / (1.0 - B1 ** STEP)
        v_hat = nv / (1.0 - B2 ** STEP)
        d_ref[...] = -LR * (m_hat / (jnp.sqrt(v_hat) + ADAM_EPS) + WD * w_ref[...])
        nm_ref[...] = nm
        nv_ref[...] = nv

    spec = pl.BlockSpec((tr, cols), lambda i: (i, 0))
    sd = jax.ShapeDtypeStruct((rows, cols), F32)
    return pl.pallas_call(body, name=name, grid=(rows // tr,), in_specs=[spec] * 4, out_specs=[spec] * 3,
                          out_shape=[sd] * 3, compiler_params=_params(("parallel",)))(w, g, m, v)


def _sum_slots(x, out_dtype, name):
    xs = x if isinstance(x, (list, tuple)) else [x]
    _, rows, cols = xs[0].shape
    tr = _rows_block(rows)

    def body(*refs):
        acc = None
        for x_ref in refs[:-1]:
            for j in range(x_ref.shape[0]):
                term = x_ref[j].astype(F32)
                acc = term if acc is None else acc + term
        refs[-1][...] = acc.astype(refs[-1].dtype)

    return pl.pallas_call(
        body, name=name, grid=(rows // tr,),
        in_specs=[pl.BlockSpec((z.shape[0], tr, cols), lambda i: (0, i, 0)) for z in xs],
        out_specs=pl.BlockSpec((tr, cols), lambda i: (i, 0)), out_shape=jax.ShapeDtypeStruct((rows, cols), out_dtype),
        compiler_params=_params(("parallel",)))(*xs)


PACK_COLS = 1024


def _pack_rows(parts, dtype, row_mult):
    flat = jnp.concatenate([p.reshape(-1).astype(dtype) for p in parts])
    per = PACK_COLS * row_mult
    n = -(-flat.shape[0] // per) * per
    flat = jnp.pad(flat, (0, n - flat.shape[0]))
    return flat.reshape(n // PACK_COLS, PACK_COLS)


def _unpack(flat, shapes):
    out, off = [], 0
    for s in shapes:
        n = math.prod(s)
        out.append(flat[off:off + n].reshape(s))
        off += n
    return out


BIG = (("w_in", (D, SHIFT + SW + 2 * D), 1), ("w_out_rwkv", (RW, D), 1), ("w_glu", (SW, 2 * D), 1),
       ("w_out", (D, D), 0), ("w_ffn_up", (D, 2 * DFF), 1), ("w_ffn_down", (DFF, D), 0))
BIG_SMALL = (("rwkv_w_up", (LW, RW), 1), ("rwkv_a_up", (LA, RW), 1), ("rwkv_g_up", (LG, RW), 1),
             ("ffn_conv_w", (3, 2 * DFF), 1))
BIG_LATE = BIG[4:]
BIG_MID = BIG[1:4]


def _shard_shape(shape, axis):
    return (shape[0] // 4, shape[1]) if axis == 0 else (shape[0], shape[1] // 4)


def _to_shards(g, axis):
    r, C = g.shape
    return g.reshape(4, r // 4, C) if axis == 0 else g.reshape(r, 4, C // 4).transpose(1, 0, 2)


def _from_shards(x, axis):
    _, r, C = x.shape
    return x.reshape(4 * r, C) if axis == 0 else x.transpose(1, 0, 2).reshape(r, 4 * C)


def kernel(x, c, w_ada, b_ada, norm1_g, w_in, mu_shift, rwkv_w0, rwkv_w_up, rwkv_a0, rwkv_a_up, rwkv_g_up, rwkv_k_k, rwkv_k_a, rwkv_r_k, rwkv_ln_g, rwkv_ln_b, w_out_rwkv, s5_a_re, s5_a_im, s5_log_dt, s5_b_re, s5_b_im, s5_c_re, s5_c_im, s5_d, w_glu, w_out, norm2_g, w_ffn_up, ffn_conv_w, ffn_conv_b, w_ffn_down, norm_f_g, loss_target, m_w_ada, m_b_ada, m_norm1_g, m_w_in, m_mu_shift, m_rwkv_w0, m_rwkv_w_up, m_rwkv_a0, m_rwkv_a_up, m_rwkv_g_up, m_rwkv_k_k, m_rwkv_k_a, m_rwkv_r_k, m_rwkv_ln_g, m_rwkv_ln_b, m_w_out_rwkv, m_s5_a_re, m_s5_a_im, m_s5_log_dt, m_s5_b_re, m_s5_b_im, m_s5_c_re, m_s5_c_im, m_s5_d, m_w_glu, m_w_out, m_norm2_g, m_w_ffn_up, m_ffn_conv_w, m_ffn_conv_b, m_w_ffn_down, m_norm_f_g, v_w_ada, v_b_ada, v_norm1_g, v_w_in, v_mu_shift, v_rwkv_w0, v_rwkv_w_up, v_rwkv_a0, v_rwkv_a_up, v_rwkv_g_up, v_rwkv_k_k, v_rwkv_k_a, v_rwkv_r_k, v_rwkv_ln_g, v_rwkv_ln_b, v_w_out_rwkv, v_s5_a_re, v_s5_a_im, v_s5_log_dt, v_s5_b_re, v_s5_b_im, v_s5_c_re, v_s5_c_im, v_s5_d, v_w_glu, v_w_out, v_norm2_g, v_w_ffn_up, v_ffn_conv_w, v_ffn_conv_b, v_w_ffn_down, v_norm_f_g):
    names = ["w_ada", "b_ada", "norm1_g", "w_in", "mu_shift", "rwkv_w0", "rwkv_w_up", "rwkv_a0", "rwkv_a_up",
             "rwkv_g_up", "rwkv_k_k", "rwkv_k_a", "rwkv_r_k", "rwkv_ln_g", "rwkv_ln_b", "w_out_rwkv", "s5_a_re",
             "s5_a_im", "s5_log_dt", "s5_b_re", "s5_b_im", "s5_c_re", "s5_c_im", "s5_d", "w_glu", "w_out", "norm2_g",
             "w_ffn_up", "ffn_conv_w", "ffn_conv_b", "w_ffn_down", "norm_f_g"]
    env = dict(locals())
    W = {n: env[n] for n in names}
    M = {n: env["m_" + n] for n in names}
    V = {n: env["v_" + n] for n in names}

    Bl, S, _ = x.shape
    T = Bl * S
    ix, iy, ic = lax.axis_index("x"), lax.axis_index("y"), lax.axis_index("c")
    chip = 2 * ix + iy
    dev = 2 * chip + ic
    rw = functools.partial(_rowwise, Bl=Bl, S=S)

    got_chip, got_dev = _gather_two_level([W[n][0] for n, _, _ in BIG_SMALL[:3]], [W["ffn_conv_w"][0], c], "gather_w")
    full = {n: _from_shards(g, axis) for (n, _, axis), g in zip(BIG_SMALL[:3], got_chip)}
    full["ffn_conv_w"] = _from_shards(got_dev[0][:, 0], 1)
    c_all = got_dev[1].reshape(8 * Bl, D)
    zeros_l = jnp.zeros((LW, RW), F32)
    w_up_p = jnp.concatenate([full["rwkv_w_up"], zeros_l], axis=0)
    a_up_p = jnp.concatenate([zeros_l, full["rwkv_a_up"]], axis=0)
    g_up = full["rwkv_g_up"]
    conv_w = full["ffn_conv_w"]
    conv_wg, conv_wu = conv_w[:, :DFF], conv_w[:, DFF:]
    conv_bg, conv_bu = ffn_conv_b[:, :DFF], ffn_conv_b[:, DFF:]
    hm = jnp.kron(jnp.eye(NH, dtype=F32), jnp.ones((HD, HD), F32))

    ncol = 6 * D // 4
    b_ada_cols = lax.dynamic_slice_in_dim(b_ada, chip * ncol, ncol, 1)
    mod_part = _ada_fwd(c_all, w_ada[0], b_ada_cols)
    mod4 = _gather_two_level([], [mod_part], "gather_mod")[1][0][:, 0]
    mod4, shards = lax.optimization_barrier((mod4, [W[n][0].astype(MXU_DTYPE) for n, _, _ in BIG]))

    def push_shards(tag, arrs):
        moves = [(i, i, lambda ref, me, peer: ref, lambda ref, me, k: ref.at[_chip_of(me)]) for i in range(len(arrs))]
        lands = [jax.ShapeDtypeStruct((4,) + z.shape, z.dtype) for z in arrs]
        return _send_start("gather_%s_start" % tag, CHIP_FLIPS, arrs, lands, moves), moves

    def pushed_shards(tag, started, moves, after, group):
        owns, gots = _send_wait("gather_%s_wait" % tag, CHIP_FLIPS, started, moves, after)
        for (n, _, axis), own, got in zip(group, owns, gots):
            full[n] = _from_shards(lax.dynamic_update_slice(got, own[None], (chip, 0, 0)), axis)

    first_start, first_moves = push_shards("in", shards[:1])
    norm1_g = norm1_g + first_start["token"]
    mod =lax.dynamic_slice_in_dim(mod4, dev * Bl, Bl, 1).transpose(1, 0, 2).reshape(Bl, 1, 6 * D)
    SH1, SC1, GT1, SH2, SC2, GT2 = range(6)

    x2d = x.reshape(T, D)
    tgt = loss_target.reshape(T, D)

    (h1,) = rw("norm1", lambda xv, sc, sh, g: _norm_mod(xv, g, sc, sh), R=512, tiled=[(x2d, D, 0)],
               batch=[(mod, D, SC1), (mod, D, SH1)], full=[norm1_g], out_tiled=[(D, MXU_DTYPE)])
    pushed_shards("in", first_start, first_moves, h1, BIG[:1])
    full["w_in"], rest = lax.optimization_barrier((full["w_in"], shards[1:]))
    late_start, late_moves = push_shards("rest", rest)
    mu_shift = mu_shift + late_start["token"]
    w_p, w_u, w_g = full["w_in"][:, :SHIFT], full["w_in"][:, SHIFT:SHIFT + SW], full["w_in"][:, SHIFT + SW:]
    p, u, gates = _mm_split(h1, full["w_in"], ((0, SHIFT, F32), (SHIFT, SW, F32), (SHIFT + SW, 2 * D, MXU_DTYPE)), "proj")

    prep_params = [rwkv_w0, w_up_p, rwkv_a0, a_up_p, g_up, rwkv_k_k, rwkv_k_a, hm]

    def prep_fwd(pv, ph, mu, *pp):
        ps = pv + (_shift_down(pv, ph, 1) - pv) * mu
        return _rwkv_prep(*_split_ps(ps), *pp)

    r_, w_, k_, v_, a_, b_, g_ = rw("rwkv_prep", prep_fwd, R=256, tiled=[(p, SHIFT, 0)], prev=[(p, SHIFT, 0)],
                                    full=[mu_shift] + prep_params, out_tiled=[(RW, F32)] * 7)
    y_wkv, ck, tinv = _wkv_fwd(r_, w_, k_, v_, a_, b_, Bl, S)
    r_k_row = rwkv_r_k.reshape(1, RW)
    post_params = [rwkv_ln_g, rwkv_ln_b, r_k_row, hm]
    (o_rwkv,) = rw("rwkv_post", _rwkv_post, R=256,
                   tiled=[(y_wkv, RW, 0), (r_, RW, 0), (k_, RW, 0), (v_, RW, 0), (g_, RW, 0)],
                   full=post_params, out_tiled=[(RW, MXU_DTYPE)])
    pushed_shards("rest", late_start, late_moves, o_rwkv, BIG[1:])
    y_a = _mm([o_rwkv], [full["w_out_rwkv"]], MXU_DTYPE, "out_rwkv")

    expand = jnp.kron(jnp.eye(SP, dtype=F32), jnp.ones((1, SGC), F32))
    s5_in = (s5_a_re[0], s5_a_im[0], s5_log_dt[0].reshape(NG, 1), s5_b_re[0].reshape(NG, SP * SGC),
             s5_b_im[0].reshape(NG, SP * SGC), expand)
    ab_re, ab_im, bb_re, bb_im = _s5_disc(*s5_in)
    eye8 = jnp.eye(8, dtype=F32)

    def blockdiag_in(bb):
        t = bb.reshape(NSG, 8, SP, SGC)
        return jnp.einsum("ab,sapc->sacbp", eye8, t).reshape(NSG, 128, 512)

    def blockdiag_out(cc):
        t = cc.reshape(NSG, 8, SGC, SP)
        return jnp.einsum("ab,sacp->sapbc", eye8, t).reshape(NSG, 512, 128)

    wb = jnp.concatenate([blockdiag_in(bb_re), blockdiag_in(bb_im)], axis=2).astype(MXU_DTYPE)
    wc = jnp.concatenate([blockdiag_out(s5_c_re[0]), -blockdiag_out(s5_c_im[0])], axis=1).astype(MXU_DTYPE)
    ab = jnp.stack([ab_re.reshape(NST), ab_im.reshape(NST)])
    y_ssm, s5_st, s5_x, s5o = _s5_fwd(u, wb, wc, ab, s5_d, Bl, S)
    z = _mm([s5o], [full["w_glu"]], MXU_DTYPE, "glu")
    mix_tiled = [(gates, D, 0), (gates, D, 1), (y_a, D, 0), (z, D, 0), (z, D, 1)]
    (mixed_in,) = rw("mix", lambda *a: _mix(*(v.astype(F32) for v in a)), R=256, tiled=mix_tiled,
                     out_tiled=[(D, MXU_DTYPE)])
    mixed = _mm([mixed_in], [full["w_out"]], F32, "out_proj")

    def norm2_fwd(xv, mx, gt, sc, sh, g):
        x1 = xv + gt * mx
        return x1, _norm_mod(x1, g, sc, sh)

    x1, h2 = rw("norm2", norm2_fwd, R=512, tiled=[(x2d, D, 0), (mixed, D, 0)],
                batch=[(mod, D, GT1), (mod, D, SC2), (mod, D, SH2)], full=[norm2_g],
                out_tiled=[(D, F32), (D, MXU_DTYPE)])
    up =_mm([h2], [full["w_ffn_up"]], MXU_DTYPE, "ffn_up")
    conv_tiled = [(up, 0), (up, 1)]
    conv_full = [conv_wg, conv_wu, conv_bg, conv_bu]
    cw = functools.partial(_colwise, Bl=Bl, S=S, R=128, W=DFF, strip=LANES)

    def act_fwd(*a):
        return ((_silu_gate(*_conv_act(*a)),),)

    (act,) = cw("ffn_act", act_fwd, tiled=conv_tiled, prev=conv_tiled, full=conv_full, out_tiled=[(1, MXU_DTYPE)])
    ffn = _mm([act], [full["w_ffn_down"]], F32, "ffn_down")

    def head(x1v, fv, tv, gt, g):
        x2 = x1v + gt * fv
        y, vjp = jax.vjp(_rms, x2, g)
        e = y - tv
        dx2, dg = vjp(e * (1.0 / D))
        loss = jnp.sum(e * e, keepdims=True) * jnp.ones((1, LANES), F32)
        return dx2, dx2 * gt, jnp.sum(dx2 * fv, axis=0, keepdims=True), dg.reshape(1, D), loss

    dx2, d_ffn, d_gt2, g_norm_f, loss_acc = rw(
        "head", head, R=512, tiled=[(x1, D, 0), (ffn, D, 0), (tgt, D, 0)], batch=[(mod, D, GT2)],
        full=[norm_f_g.reshape(1, D)], out_tiled=[(D, F32), (D, MXU_DTYPE)], out_batch=[D],
        out_acc=[(1, D), (1, LANES)])
    loss = lax.psum(0.5 / D * loss_acc[0, 0], ("x", "y", "c"))

    d_act = _mm([d_ffn], [full["w_ffn_down"]], F32, "d_act", bt=True)
    g_w_ffn_down = _mm_tn(act, d_ffn, "g_ffn_down")

    def act_bwd(ug, uu, dact, hg, hu, wg, wu, bg, bu):
        (gate, taps_g), (upv, taps_u) = _conv3(ug, hg, wg, bg), _conv3(uu, hu, wu, bu)
        _, vjp_s = jax.vjp(_silu_gate, gate, upv)
        d_gate, d_upv = vjp_s(dact)
        def taps(dh, shifted):
            return [jnp.sum(dh * s, axis=0, keepdims=True) for s in shifted] + [jnp.sum(dh, axis=0, keepdims=True)]
        return ((d_gate,), (d_upv,), *taps(d_gate, taps_g), *taps(d_upv, taps_u))

    dh_g, dh_u, *tapg = cw("ffn_act_bwd", act_bwd, tiled=conv_tiled + [(d_act, 0)], prev=conv_tiled, full=conv_full,
                           out_tiled=[(1, MXU_DTYPE), (1, MXU_DTYPE)], n_acc=8)
    g_cw_g, g_cb_g = jnp.concatenate(tapg[0:3], axis=0), tapg[3]
    g_cw_u, g_cb_u = jnp.concatenate(tapg[4:7], axis=0), tapg[7]

    def conv_t(dg, du_, ng, nu, wg, wu):
        dg, du_, ng, nu = (z.astype(F32) for z in (dg, du_, ng, nu))

        def ct(d, n, w):
            return w[2:3] * d + w[1:2] * _shift_up(d, n, 1) + w[0:1] * _shift_up(d, n, 2)
        return ((ct(dg, ng, wg), ct(du_, nu, wu)),)

    (d_up,) = cw("conv_bwd", conv_t, tiled=[(dh_g, 0), (dh_u, 0)], nxt=[(dh_g, 0), (dh_u, 0)],
                 full=[conv_wg, conv_wu], out_tiled=[(2, MXU_DTYPE)])
    d_h2 = _mm([d_up], [full["w_ffn_up"]], F32, "d_h2", bt=True)
    g_w_ffn_up = _mm_tn(h2, d_up, "g_ffn_up")

    sds = jax.ShapeDtypeStruct
    reduce_src = lambda r: (lambda ref, me, peer: ref.at[_chip_of(peer), _half(r, peer[2])])

    def reduced_halves(tag, started, moves, after):
        gsh_own, got = _send_wait("rs_%s_wait" % tag, ALL_FLIPS, started, moves, after)
        halves = []
        for i, (g, gt) in enumerate(zip(gsh_own, got)):
            h = g.shape[1] // 2
            own = lax.dynamic_slice(g, (chip, ic * h, 0), (1, h, g.shape[2]))
            halves.append(_sum_slots([own, gt], F32, "rs_%s_sum%d" % (tag, i)))
        return halves

    def share_start(tag, halves):
        moves = [(i, i, lambda ref, me, peer: ref, lambda ref, me, k, r=2 * g.shape[0]: ref.at[_half(r, me[2])])
                 for i, g in enumerate(halves)]
        lands = [sds((2 * g.shape[0], g.shape[1]), F32) for g in halves]
        return _send_start("share_%s_start" % tag, PAIR_FLIPS, halves, lands, moves), moves

    def share_finish(tag, started, moves, after, group, grads):
        mine_h, got = _send_wait("share_%s_wait" % tag, PAIR_FLIPS, started, moves, after)
        for (n, _, _), mh, whole in zip(group, mine_h, got):
            grads[n] = lax.dynamic_update_slice(whole, mh, (ic * mh.shape[0], 0))[None]

    def reduce_start(tag, group, mats):
        gsh = [_to_shards(g, ax).astype(MXU_DTYPE) for g, (_, _, ax) in zip(mats, group)]
        moves = [(i, i, reduce_src(g.shape[1]), lambda ref, me, k: ref.at[k]) for i, g in enumerate(gsh)]
        lands = [sds((len(ALL_FLIPS), g.shape[1] // 2, g.shape[2]), MXU_DTYPE) for g in gsh]
        return _send_start("rs_%s_start" % tag, ALL_FLIPS, gsh, lands, moves), moves

    rsl, rsl_moves = reduce_start("ffn", BIG_LATE, (g_w_ffn_up, g_w_ffn_down))
    norm2_g = norm2_g + rsl["token"]

    def norm2_bwd(x1v, dh2, dx2v, mx, gt, sc, sh, g):
        _, vjp = jax.vjp(_norm_mod, x1v, g, sc, sh)
        dxn, dg, dsc, dsh = vjp(dh2)
        dx1 = dx2v + dxn
        return dx1, dx1 * gt, jnp.sum(dx1 * mx, axis=0, keepdims=True), dsc, dsh, dg

    dx1, d_mixed, d_gt1, d_sc2, d_sh2, g_norm2 = rw(
        "norm2_bwd", norm2_bwd, R=512, tiled=[(x1, D, 0), (d_h2, D, 0), (dx2, D, 0), (mixed, D, 0)],
        batch=[(mod, D, GT1), (mod, D, SC2), (mod, D, SH2)], full=[norm2_g],
        out_tiled=[(D, F32), (D, MXU_DTYPE)], out_batch=[D, D, D], out_acc=[(1, D)])

    d_mixed_in = _mm([d_mixed], [full["w_out"]], MXU_DTYPE, "d_mixed_in", bt=True)
    g_w_out = _mm_tn(mixed_in, d_mixed, "g_w_out")

    def mix_bwd(*a):
        ga, gb, ya, za, zb, dm = (v.astype(F32) for v in a)
        _, vjp = jax.vjp(_mix, ga, gb, ya, za, zb)
        dga, dgb, dya, dza, dzb = vjp(dm)
        return jnp.concatenate([dga, dgb], axis=1), dya, jnp.concatenate([dza, dzb], axis=1)

    d_gates, d_ya, d_z = rw("mix_bwd", mix_bwd, R=256, tiled=mix_tiled + [(d_mixed_in, D, 0)],
                            out_tiled=[(2 * D, MXU_DTYPE), (D, MXU_DTYPE), (2 * D, MXU_DTYPE)])
    d_o_rwkv = _mm([d_ya], [full["w_out_rwkv"]], F32, "d_o_rwkv", bt=True)
    g_w_out_rwkv = _mm_tn(o_rwkv, d_ya, "g_out_rwkv")
    d_s5o = _mm([d_z], [full["w_glu"]], F32, "d_s5o", bt=True)
    g_w_glu = _mm_tn(s5o, d_z, "g_glu")
    rsm, rsm_moves = reduce_start("mid", BIG_MID, (g_w_out_rwkv, g_w_glu, g_w_out))
    s5_d = s5_d + rsm["token"]

    d_u, d_wb, d_wc, d_ab, g_s5_d = _s5_bwd(u, y_ssm, d_s5o, s5_d, wb, wc, ab, s5_st, s5_x, Bl, S)

    def diag_in(dw):
        t = dw.reshape(NSG, 8, SGC, 8, SP)
        return jnp.einsum("ab,sacbp->sapc", eye8, t).reshape(NG, SP * SGC)

    def diag_out(dw):
        t = dw.reshape(NSG, 8, SP, 8, SGC)
        return jnp.einsum("ab,sapbc->sacp", eye8, t).reshape(NG, SGC, SP)

    g_s5_c_re = diag_out(d_wc[:, :512])
    g_s5_c_im = -diag_out(d_wc[:, 512:])
    disc_cts = (d_ab[0].reshape(NG, SP), d_ab[1].reshape(NG, SP), diag_in(d_wb[:, :, :512]), diag_in(d_wb[:, :, 512:]))
    g_a_re, g_a_im, g_log_dt, g_b_re, g_b_im = _s5_disc_bwd(*s5_in, disc_cts)

    def post_bwd(yv, rv, kv, vv, gv, do, *pp):
        _, vjp = jax.vjp(lambda *a: _rwkv_post(*a, pp[3]), yv, rv, kv, vv, gv, *pp[:3])
        return vjp(do)

    dy_wkv, dr_b, dk_b, dv_b, dg_, g_ln_g, g_ln_b, g_r_k = rw(
        "rwkv_post_bwd", post_bwd, R=256,
        tiled=[(y_wkv, RW, 0), (r_, RW, 0), (k_, RW, 0), (v_, RW, 0), (g_, RW, 0), (d_o_rwkv, RW, 0)],
        full=post_params, out_tiled=[(RW, F32)] * 5, out_acc=[(1, RW)] * 3)
    dr3, dw3, dk3, dv3, da3, db3 = _wkv_bwd(r_, w_, k_, v_, a_, b_, dy_wkv, ck, tinv, Bl, S)

    shl, shl_moves = share_start("ffn", reduced_halves("ffn", rsl, rsl_moves, dr3))
    shm, shm_moves = share_start("mid", reduced_halves("mid", rsm, rsm_moves, dr3))
    mu_shift = mu_shift + (shl["token"] + shm["token"])

    def prep_bwd(pv, dr1, dr2, dwv, dk1, dk2, dv1, dv2, dav, dbv, dgv, ph, mu, *pp):
        prev = _shift_down(pv, ph, 1)
        ps = pv + (prev - pv) * mu
        _, vjp = jax.vjp(lambda *q: _rwkv_prep(*q, pp[7]), *_split_ps(ps), *pp[:7])
        grads = vjp((dr1 + dr2, dwv, dk1 + dk2, dv1 + dv2, dav, dbv, dgv))
        dps = jnp.concatenate(grads[:5], axis=1)
        return (dps,) + tuple(grads[5:]) + (jnp.sum(dps * (prev - pv), axis=0, keepdims=True),)

    prep_outs = rw(
        "rwkv_prep_bwd", prep_bwd, R=256,
        tiled=[(p, SHIFT, 0), (dr3, RW, 0), (dr_b, RW, 0), (dw3, RW, 0), (dk3, RW, 0), (dk_b, RW, 0),
               (dv3, RW, 0), (dv_b, RW, 0), (da3, RW, 0), (db3, RW, 0), (dg_, RW, 0)],
        prev=[(p, SHIFT, 0)], full=[mu_shift] + prep_params,
        out_tiled=[(SHIFT, F32)],
        out_acc=[(1, RW), (LW + LA, RW), (1, RW), (LW + LA, RW), (LG, RW), (1, RW), (1, RW), (1, SHIFT)])
    d_ps, g_w0, g_w_up_p, g_a0, g_a_up_p, g_g_up, g_k_k, g_k_a, g_mu = prep_outs

    small = {"mu_shift": g_mu, "rwkv_w0": g_w0, "rwkv_a0": g_a0, "rwkv_k_k": g_k_k,
             "rwkv_k_a": g_k_a, "rwkv_r_k": g_r_k, "rwkv_ln_g": g_ln_g, "rwkv_ln_b": g_ln_b, "s5_a_re": g_a_re,
             "s5_a_im": g_a_im, "s5_log_dt": g_log_dt, "s5_b_re": g_b_re, "s5_b_im": g_b_im, "s5_c_re": g_s5_c_re,
             "s5_c_im": g_s5_c_im, "s5_d": g_s5_d, "norm2_g": g_norm2,
             "ffn_conv_b": jnp.concatenate([g_cb_g, g_cb_u], axis=1), "norm_f_g": g_norm_f}
    small_names = list(small)
    g_conv_w = jnp.concatenate([g_cw_g, g_cw_u], axis=1)
    shard_small = {"rwkv_w_up": g_w_up_p[:LW], "rwkv_a_up": g_a_up_p[LW:], "rwkv_g_up": g_g_up, "ffn_conv_w": g_conv_w}
    parts = [small[n] for n in small_names] + [_to_shards(shard_small[n], ax) for n, _, ax in BIG_SMALL]
    spack = _pack_rows(parts, F32, SUBLANES)
    sm_moves = [(0, 0, lambda ref, me, peer: ref, lambda ref, me, k: ref.at[2 * _chip_of(me) + me[2]])]
    sm = _send_start("gsmall_start", ALL_FLIPS, [spack], [sds((8,) + spack.shape, F32)], sm_moves)
    mu_shift = mu_shift + sm["token"]

    def shift_bwd(dps, nx, mu):
        return dps * (1.0 - mu) + _shift_up(dps * mu, nx * mu, 1)

    (d_p,) = rw("shift_bwd", shift_bwd, R=256, tiled=[(d_ps, SHIFT, 0)], nxt=[(d_ps, SHIFT, 0)], full=[mu_shift],
                out_tiled=[(SHIFT, MXU_DTYPE)])
    g_w_in = jnp.concatenate([_mm_tn(h1, d_p, "g_w_p"), _mm_tn(h1, d_u, "g_w_u"), _mm_tn(h1, d_gates, "g_w_g")], axis=1)
    rsn, rsn_moves = reduce_start("in", BIG[:1], (g_w_in,))
    norm1_g = norm1_g + rsn["token"]
    d_h1 = _mm([d_p, d_u, d_gates], [w_p, w_u, w_g], F32, "d_h1", bt=True)

    def norm1_bwd(xv, dh1, dx1v, sc, sh, g):
        _, vjp = jax.vjp(_norm_mod, xv, g, sc, sh)
        dxn, dg, dsc, dsh = vjp(dh1)
        return dx1v + dxn, dsc, dsh, dg

    grad_x, d_sc1, d_sh1, g_norm1 = rw(
        "norm1_bwd", norm1_bwd, R=512, tiled=[(x2d, D, 0), (d_h1, D, 0), (dx1, D, 0)],
        batch=[(mod, D, SC1), (mod, D, SH1)], full=[norm1_g], out_tiled=[(D, F32)], out_batch=[D, D], out_acc=[(1, D)])

    dmod = jnp.concatenate([d_sh1, d_sc1, d_gt1, d_sh2, d_sc2, d_gt2], axis=2).reshape(Bl, 6 * D)
    last_all = _gather_two_level([], [dmod, g_norm1], "gather_dmod")[1]
    dmod_all = last_all[0].reshape(8 * Bl, 6 * D)
    shn, shn_moves = share_start("in", reduced_halves("in", rsn, rsn_moves, dmod_all))
    dmod_cols = lax.dynamic_slice_in_dim(dmod_all, chip * ncol, ncol, 1)
    g_w_ada, g_b_ada = _ada_bwd(c_all, dmod_cols, dmod_all)

    grads = {"norm1_g": _sum_slots(last_all[1].reshape(8, 1, D), F32, "sum_norm1")}
    sm_own, sm_got = _send_wait("gsmall_wait", ALL_FLIPS, sm, sm_moves, g_b_ada)
    s_all = lax.dynamic_update_slice(sm_got[0], sm_own[0][None], (dev, 0, 0))
    s_sum = _sum_slots(s_all, F32, "sum_gsmall").reshape(-1)
    off = 0
    for n in small_names:
        grads[n] = s_sum[off:off + W[n].size].reshape(W[n].shape)
        off += W[n].size
    for n, shape, axis in BIG_SMALL:
        ss = _shard_shape(shape, axis)
        k4 = 4 * math.prod(ss)
        sh4 = s_sum[off:off + k4].reshape(4, math.prod(ss))
        grads[n] = lax.dynamic_index_in_dim(sh4, chip, 0, keepdims=False).reshape((1,) + ss)
        off += k4

    share_finish("ffn", shl, shl_moves, s_sum, BIG_LATE, grads)
    share_finish("mid", shm, shm_moves, grads[BIG_LATE[0][0]], BIG_MID, grads)
    grads["w_ada"] = g_w_ada[None]
    grads["b_ada"] = g_b_ada

    delta, new_m, new_v = {}, {}, {}
    to2 = lambda z: z.reshape(-1, z.shape[-1])

    def adamw(n):
        d_, m_, v2_ = _adamw(to2(W[n]), to2(grads[n]), to2(M[n]), to2(V[n]), "adamw_" + n)
        delta[n], new_m[n], new_v[n] = (z.reshape(W[n].shape) for z in (d_, m_, v2_))

    for n in ["w_ada"] + [b[0] for b in BIG[1:]]:
        adamw(n)
    rest = [n for n in names if n not in delta and n != "w_in"]
    packs = [_pack_rows([src[n] for n in rest], F32, SUBLANES) for src in (W, grads, M, V)]
    d_, m_, v2_ = _adamw(*packs, "adamw_small")
    shapes = [W[n].shape for n in rest]
    for dst, z in ((delta, d_), (new_m, m_), (new_v, v2_)):
        for n, val in zip(rest, _unpack(z.reshape(-1), shapes)):
            dst[n] = val
    share_finish("in", shn, shn_moves, d_, BIG[:1], grads)
    adamw("w_in")

    return (loss, grad_x.reshape(Bl, S, D), *[grads[n] for n in names], *[delta[n] for n in names],
            *[new_m[n] for n in names], *[new_v[n] for n in names])
```

```python
import functools
import math

import jax
import jax.numpy as jnp
from jax import lax
from jax.experimental import pallas as pl
from jax.experimental.pallas import tpu as pltpu

F32 = jnp.float32
BF16 = jnp.bfloat16
MXU_DTYPE = jnp.bfloat16
MESH_IDS = pl.DeviceIdType.MESH
HIGHEST = lax.Precision.HIGHEST

D = 1024
RW, NH, HD = 512, 8, 64
LW, LA, LG = 64, 64, 128
SW, SGC, NG, SP = 512, 16, 32, 64
NSG = 4
SHIFT = 3 * RW + LW + LA + LG
DFF = 2816
RMS_EPS, GN_EPS, L2_EPS = 1e-6, 64e-5, 1e-12
LR, B1, B2, ADAM_EPS, WD, STEP = 0.001, 0.9, 0.999, 1e-8, 0.01, 10
DECAY_SCALE = math.exp(-0.5)
GELU_C = math.sqrt(2.0 / math.pi)

VMEM_LIMIT = 52 * 1024 * 1024
SUBLANES, LANES = 8, 128
HALO = 16


def _pick(n, cap):
    if n <= cap:
        return n
    best = None
    for t in range(LANES, cap + 1, LANES):
        if n % t == 0:
            best = t
    assert best is not None, (n, cap)
    return best


def _params(sem=None, vmem=VMEM_LIMIT):
    return pltpu.CompilerParams(dimension_semantics=sem, vmem_limit_bytes=vmem)


def _chip_of(p):
    return 2 * p[0] + p[1]


def _me():
    return (lax.axis_index("x"), lax.axis_index("y"), lax.axis_index("c"))


def _half(rows, core):
    h = rows // 2
    return pl.ds(pl.multiple_of(core * h, 16 if h % 16 == 0 else SUBLANES), h)


_HBM =pl.BlockSpec(memory_space=pltpu.HBM)
_SEM = pl.BlockSpec(memory_space=pltpu.SEMAPHORE)
_DATAFLOW = pltpu.SideEffectType.DATAFLOW_SIDE_EFFECTING


def _split_copies(flips, moves, src_refs, land_refs, send_sems, recv_sems):
    me = _me()
    nf = len(flips)
    out = []
    for m, (si, li, src_sel, dst_sel) in enumerate(moves):
        for k, f in enumerate(flips):
            peer = tuple(1 - v if b else v for v, b in zip(me, f))
            out.append(pltpu.make_async_remote_copy(
                src_ref=src_sel(src_refs[si], me, peer), dst_ref=dst_sel(land_refs[li], me, k),
                send_sem=send_sems.at[m * nf + k], recv_sem=recv_sems.at[m * nf + k],
                device_id=peer, device_id_type=MESH_IDS))
    return out


def _send_start(name, flips, srcs, land_shapes, moves):
    ns, nl = len(srcs), len(land_shapes)
    n = len(moves) * len(flips)

    def body(*refs):
        for cp in _split_copies(flips, moves, refs[:ns], refs[ns:ns + nl], refs[ns + nl], refs[ns + nl + 1]):
            cp.start()
        refs[-1][...] = jnp.zeros(refs[-1].shape, F32)

    hbm = lambda z: pltpu.with_memory_space_constraint(z, pltpu.HBM)
    lands = [lax.empty(s.shape, s.dtype) for s in land_shapes]
    res = pl.pallas_call(
        body, name=name,
        out_shape=(pltpu.SemaphoreType.DMA((n,)), pltpu.SemaphoreType.DMA((n,)),
                   *[pltpu.HBM(z.shape, z.dtype) for z in srcs], *[pltpu.HBM(s.shape, s.dtype) for s in land_shapes],
                   jax.ShapeDtypeStruct((SUBLANES, LANES), F32)),
        in_specs=[_HBM] * (ns + nl),
        out_specs=(_SEM, _SEM, *[_HBM] * (ns + nl), pl.BlockSpec(memory_space=pltpu.VMEM)),
        input_output_aliases={i: 2 + i for i in range(ns + nl)},
        compiler_params=pltpu.CompilerParams(has_side_effects=_DATAFLOW),
    )(*[hbm(z) for z in srcs], *[hbm(z) for z in lands])
    return {"sems": res[:2], "srcs": list(res[2:2 + ns]), "lands": list(res[2 + ns:2 + ns + nl]), "token": res[-1][0, 0]}


def _send_wait(name, flips, started, moves, after):
    srcs, lands = started["srcs"], started["lands"]
    ns, nl = len(srcs), len(lands)

    def body(*refs):
        for cp in _split_copies(flips, moves, refs[:ns], refs[ns:ns + nl], refs[ns + nl], refs[ns + nl + 1]):
            cp.wait_send()
            cp.wait_recv()

    res = pl.pallas_call(
        body, name=name, out_shape=[pltpu.HBM(z.shape, z.dtype) for z in srcs + lands],
        in_specs=[_HBM] * (ns + nl) + [_SEM, _SEM, pl.BlockSpec(memory_space=pl.ANY)],
        out_specs=[_HBM] * (ns + nl), input_output_aliases={i: i for i in range(ns + nl)},
        compiler_params=pltpu.CompilerParams(has_side_effects=_DATAFLOW),
    )(*srcs, *lands, *started["sems"], after)
    return list(res[:ns]), list(res[ns:])


CHIP_FLIPS = ((1, 0, 0), (0, 1, 0), (1, 1, 0))
PAIR_FLIPS = ((0, 0, 1),)
ALL_FLIPS = CHIP_FLIPS + ((1, 0, 1), (0, 1, 1), (1, 1, 1)) + PAIR_FLIPS


def _gather_two_level(chip_arrs, dev_arrs, name):
    arrs = list(chip_arrs) + list(dev_arrs)
    n, nchip = len(arrs), len(chip_arrs)
    NS = 7

    def body(*refs):
        srcs, outs = refs[:n], refs[n:2 * n]
        send_sems, recv_sems, loc_sems = refs[2 * n:]
        x, y, c = _me()
        sib = (x, y, 1 - c)
        chips = [(1 - x, y), (x, 1 - y), (1 - x, 1 - y)]
        mine = 2 * x + y
        ids = [2 * cx + cy for cx, cy in chips]

        def part(i, slot, core):
            if i < nchip:
                return outs[i].at[slot, _half(arrs[i].shape[0], core)]
            return outs[i].at[slot, core]

        def rcopy(i, k, src, dst, to):
            return pltpu.make_async_remote_copy(src_ref=src, dst_ref=dst, send_sem=send_sems.at[i * NS + k],
                                                recv_sem=recv_sems.at[i * NS + k], device_id=to, device_id_type=MESH_IDS)

        started, locs = [], []
        for i in range(n):
            own = srcs[i].at[_half(arrs[i].shape[0], c)] if i < nchip else srcs[i]
            loc = pltpu.make_async_copy(srcs[i], outs[i].at[mine] if i < nchip else outs[i].at[mine, c], loc_sems.at[i])
            loc.start()
            locs.append(loc)
            for f, chip in enumerate(chips):
                cp = rcopy(i, f, own, part(i, mine, c), (*chip, c))
                cp.start()
                started.append(cp)
            if i >= nchip:
                cp = rcopy(i, 6, own, part(i, mine, c), sib)
                cp.start()
                started.append(cp)
        for i in range(n):
            for f in range(3):
                land = part(i, ids[f], c)
                rcopy(i, f, land, land, sib).wait_recv()
                fw = rcopy(i, 3 + f, land, land, sib)
                fw.start()
                started.append(fw)
        for i in range(n):
            for f in range(3):
                land = part(i, ids[f], 1 - c)
                rcopy(i, 3 + f, land, land, sib).wait_recv()
            if i >= nchip:
                land = part(i, mine, 1 - c)
                rcopy(i, 6, land, land, sib).wait_recv()
        for cp in started:
            cp.wait_send()
        for loc in locs:
            loc.wait()

    outs = [jax.ShapeDtypeStruct((4,) + a.shape, a.dtype) for a in chip_arrs]
    outs += [jax.ShapeDtypeStruct((4, 2) + a.shape, a.dtype) for a in dev_arrs]
    res = pl.pallas_call(
        body, name=name, out_shape=outs,
        in_specs=[pl.BlockSpec(memory_space=pl.ANY)] * n, out_specs=[pl.BlockSpec(memory_space=pl.ANY)] * n,
        scratch_shapes=[pltpu.SemaphoreType.DMA((n * NS,)), pltpu.SemaphoreType.DMA((n * NS,)),
                        pltpu.SemaphoreType.DMA((n,))],
    )(*arrs)
    return res[:nchip], res[nchip:]


def _mm(As, Bs, out_dtype, name, tm=512, cap=1408, bt=False):
    n = len(As)
    M, N = As[0].shape[0], Bs[0].shape[0 if bt else 1]
    if sum(a.shape[1] for a in As) <= 1024:
        tm = 2 * tm
    tm = min(tm, M)
    tn = _pick(N, cap)
    dims = (((1,), (1,)), ((), ())) if bt else (((1,), (0,)), ((), ()))

    def body(*refs):
        o = refs[2 * n]
        acc = None
        for a, b in zip(refs[:n], refs[n:2 * n]):
            d = lax.dot_general(a[...].astype(MXU_DTYPE), b[...].astype(MXU_DTYPE), dims, preferred_element_type=F32)
            acc = d if acc is None else acc + d
        o[...] = acc.astype(o.dtype)

    in_specs = [pl.BlockSpec((tm, a.shape[1]), lambda i, j: (i, 0)) for a in As]
    if bt:
        in_specs += [pl.BlockSpec((tn, b.shape[1]), lambda i, j: (j, 0)) for b in Bs]
    else:
        in_specs += [pl.BlockSpec((b.shape[0], tn), lambda i, j: (0, j)) for b in Bs]
    return pl.pallas_call(
        body, name=name, grid=(M // tm, N // tn), in_specs=in_specs,
        out_specs=pl.BlockSpec((tm, tn), lambda i, j: (i, j)),
        out_shape=jax.ShapeDtypeStruct((M, N), out_dtype),
        compiler_params=_params(("parallel", "parallel")),
    )(*As, *Bs)


def _mm_split(a, b, splits, name, tm=512):
    M, K = a.shape
    tm = min(tm, M)

    def body(a_ref, b_ref, *outs):
        av = a_ref[...].astype(MXU_DTYPE)
        for o, (c0, wd, _) in zip(outs, splits):
            o[...] = jnp.dot(av, b_ref[:, c0:c0 + wd].astype(MXU_DTYPE), preferred_element_type=F32).astype(o.dtype)

    return pl.pallas_call(
        body, name=name, grid=(M // tm,),
        in_specs=[pl.BlockSpec((tm, K), lambda i: (i, 0)), pl.BlockSpec(b.shape, lambda i: (0, 0))],
        out_specs=[pl.BlockSpec((tm, wd), lambda i: (i, 0)) for _, wd, _ in splits],
        out_shape=[jax.ShapeDtypeStruct((M, wd), dt) for _, wd, dt in splits],
        compiler_params=_params(("parallel",)),
    )(a, b)


def _mm_tn_cat(A, Gs, name, tt=512):
    T, Ka = A.shape
    tt = min(tt, T)
    N = sum(g.shape[1] for g in Gs)

    def body(a_ref, *refs):
        o = refs[-1]

        @pl.when(pl.program_id(0) == 0)
        def _():
            o[...] = jnp.zeros(o.shape, F32)
        av = a_ref[...].astype(MXU_DTYPE)
        c0 = 0
        for g in refs[:-1]:
            wd = g.shape[1]
            o[:, c0:c0 + wd] += lax.dot_general(av, g[...].astype(MXU_DTYPE), (((0,), (0,)), ((), ())),
                                                preferred_element_type=F32)
            c0 += wd

    return pl.pallas_call(
        body, name=name, grid=(T // tt,),
        in_specs=[pl.BlockSpec((tt, Ka), lambda t: (t, 0))] + [pl.BlockSpec((tt, g.shape[1]), lambda t: (t, 0)) for g in Gs],
        out_specs=pl.BlockSpec((Ka, N), lambda t: (0, 0)), out_shape=jax.ShapeDtypeStruct((Ka, N), F32),
        compiler_params=_params(("arbitrary",)),
    )(A, *Gs)


def _mm_tn(A, G, name, tt=2048, cap=1408):
    T, Ka = A.shape
    N = G.shape[1]
    tt = min(tt, T)
    tk = _pick(Ka, cap)
    tn = _pick(N, cap)

    def body(a, g, o):
        @pl.when(pl.program_id(2) == 0)
        def _():
            o[...] = jnp.zeros(o.shape, F32)
        o[...] += lax.dot_general(a[...].astype(MXU_DTYPE), g[...].astype(MXU_DTYPE),
                                  (((0,), (0,)), ((), ())), preferred_element_type=F32)

    return pl.pallas_call(
        body, name=name, grid=(Ka // tk, N // tn, T // tt),
        in_specs=[pl.BlockSpec((tt, tk), lambda i, j, t: (t, i)), pl.BlockSpec((tt, tn), lambda i, j, t: (t, j))],
        out_specs=pl.BlockSpec((tk, tn), lambda i, j, t: (i, j)),
        out_shape=jax.ShapeDtypeStruct((Ka, N), F32),
        compiler_params=_params(("parallel", "parallel", "arbitrary")),
    )(A, G)


def _rowwise(name, fn, *, Bl, S, R, tiled=(), prev=(), nxt=(), batch=(), full=(),
             out_tiled=(), out_batch=(), out_acc=()):
    R = min(R, S)
    nS = S // R
    T = Bl * S
    hb = R // HALO
    n_in = len(tiled) + len(prev) + len(nxt) + len(batch) + len(full)

    in_specs, args = [], []
    for a, wd, cb in tiled:
        in_specs.append(pl.BlockSpec((R, wd), lambda b, i, cb=cb: (b * nS + i, cb)))
        args.append(a)
    for a, wd, cb in prev:
        in_specs.append(pl.BlockSpec((HALO, wd), lambda b, i, cb=cb: (jnp.maximum((b * nS + i) * hb - 1, 0), cb)))
        args.append(a)
    for a, wd, cb in nxt:
        in_specs.append(pl.BlockSpec((HALO, wd), lambda b, i, cb=cb: (jnp.minimum((b * nS + i + 1) * hb, T // HALO - 1), cb)))
        args.append(a)
    for a, wd, cb in batch:
        in_specs.append(pl.BlockSpec((1, 1, wd), lambda b, i, cb=cb: (b, 0, cb)))
        args.append(a)
    for a in full:
        in_specs.append(pl.BlockSpec(a.shape, lambda b, i, nd=a.ndim: (0,) * nd))
        args.append(a)

    out_specs, out_shape = [], []
    for C, dt in out_tiled:
        out_specs.append(pl.BlockSpec((R, C), lambda b, i: (b * nS + i, 0)))
        out_shape.append(jax.ShapeDtypeStruct((T, C), dt))
    for C in out_batch:
        out_specs.append(pl.BlockSpec((1, 1, C), lambda b, i: (b, 0, 0)))
        out_shape.append(jax.ShapeDtypeStruct((Bl, 1, C), F32))
    for shp in out_acc:
        out_specs.append(pl.BlockSpec(shp, lambda b, i, nd=len(shp): (0,) * nd))
        out_shape.append(jax.ShapeDtypeStruct(shp, F32))

    nt, npv, nnx, nbt = len(tiled), len(prev), len(nxt), len(batch)

    def body(*refs):
        b, i = pl.program_id(0), pl.program_id(1)
        ins, outs = refs[:n_in], refs[n_in:]
        vals = [r[...] for r in ins[:nt]]
        vals += [jnp.where(i > 0, r[...], jnp.zeros(r.shape, r.dtype)) for r in ins[nt:nt + npv]]
        vals += [jnp.where(i < nS - 1, r[...], jnp.zeros(r.shape, r.dtype)) for r in ins[nt + npv:nt + npv + nnx]]
        vals += [r[0] for r in ins[nt + npv + nnx:nt + npv + nnx + nbt]]
        vals += [r[...] for r in ins[nt + npv + nnx + nbt:]]
        res = fn(*vals)
        if not isinstance(res, (tuple, list)):
            res = (res,)
        k = 0
        for _ in out_tiled:
            outs[k][...] = res[k].astype(outs[k].dtype)
            k += 1
        for _ in out_batch:
            o = outs[k]

            @pl.when(i == 0)
            def _(o=o):
                o[...] = jnp.zeros(o.shape, F32)
            o[0] += res[k]
            k += 1
        for _ in out_acc:
            o = outs[k]

            @pl.when((i == 0) & (b == 0))
            def _(o=o):
                o[...] = jnp.zeros(o.shape, F32)
            o[...] += res[k]
            k += 1

    out = pl.pallas_call(
        body, name=name, grid=(Bl, nS), in_specs=in_specs, out_specs=out_specs, out_shape=out_shape,
        compiler_params=_params(("arbitrary", "arbitrary")),
    )(*args)
    return out


def _colwise(name, fn, *, Bl, S, R, W, strip, tiled=(), prev=(), nxt=(), full=(), out_tiled=(), n_acc=0):
    R = min(R, S)
    nS = S // R
    T = Bl * S
    hb = R // HALO
    nt, npv, nnx, nfl = len(tiled), len(prev), len(nxt), len(full)
    n_in = nt + npv + nnx + nfl
    in_specs = [pl.BlockSpec((R, W), lambda b, i, cb=cb: (b * nS + i, cb)) for _, cb in tiled]
    in_specs += [pl.BlockSpec((HALO, W), lambda b, i, cb=cb: (jnp.maximum((b * nS + i) * hb - 1, 0), cb)) for _, cb in prev]
    in_specs += [pl.BlockSpec((HALO, W), lambda b, i, cb=cb: (jnp.minimum((b * nS + i + 1) * hb, T // HALO - 1), cb))
                 for _, cb in nxt]
    in_specs += [pl.BlockSpec(a.shape, lambda b, i: (0, 0)) for a in full]
    out_specs = [pl.BlockSpec((R, m * W), lambda b, i: (b * nS + i, 0)) for m, _ in out_tiled]
    out_specs += [pl.BlockSpec((1, W), lambda b, i: (0, 0))] * n_acc
    out_shape = [jax.ShapeDtypeStruct((T, m * W), dt) for m, dt in out_tiled] + [jax.ShapeDtypeStruct((1, W), F32)] * n_acc

    def body(*refs):
        b, i = pl.program_id(0), pl.program_id(1)
        ins, outs = refs[:n_in], refs[n_in:]

        @pl.when((i == 0) & (b == 0))
        def _():
            for o in outs[len(out_tiled):]:
                o[...] = jnp.zeros(o.shape, F32)

        def col(j, carry):
            cs = pl.ds(pl.multiple_of(j * strip, strip), strip)
            vals = [r[:, cs] for r in ins[:nt]]
            vals += [jnp.where(i > 0, r[:, cs], jnp.zeros((HALO, strip), r.dtype)) for r in ins[nt:nt + npv]]
            vals += [jnp.where(i < nS - 1, r[:, cs], jnp.zeros((HALO, strip), r.dtype)) for r in ins[nt + npv:nt + npv + nnx]]
            vals += [r[:, cs] for r in ins[nt + npv + nnx:]]
            res = fn(*vals)
            for k, (m, _) in enumerate(out_tiled):
                for q in range(m):
                    outs[k][:, pl.ds(pl.multiple_of(q * W + j * strip, strip), strip)] = res[k][q].astype(outs[k].dtype)
            for k in range(len(out_tiled), len(outs)):
                outs[k][:, cs] += res[k]
            return carry

        lax.fori_loop(0, W // strip, col, 0)

    return pl.pallas_call(
        body, name=name, grid=(Bl, nS), in_specs=in_specs, out_specs=out_specs, out_shape=out_shape,
        compiler_params=_params(("arbitrary", "arbitrary")),
    )(*[a for a, _ in tiled], *[a for a, _ in prev], *[a for a, _ in nxt], *full)


def _shift_down(x, halo, k):
    rolled = pltpu.roll(x, k, 0)
    row = lax.broadcasted_iota(jnp.int32, (SUBLANES, x.shape[1]), 0)
    head = rolled[0:SUBLANES]
    for j in range(k):
        head = jnp.where(row == j, halo[HALO - k + j:HALO - k + j + 1, :], head)
    return jnp.concatenate([head, rolled[SUBLANES:]], axis=0)


def _shift_up(x, halo, k):
    n = x.shape[0]
    rolled = pltpu.roll(x, n - k, 0)
    row = lax.broadcasted_iota(jnp.int32, (SUBLANES, x.shape[1]), 0)
    tail = rolled[n - SUBLANES:]
    for j in range(k):
        tail = jnp.where(row == SUBLANES - k + j, halo[j:j + 1, :], tail)
    return jnp.concatenate([rolled[:n - SUBLANES], tail], axis=0)


def _dotm(a, b):
    return jnp.dot(a.astype(MXU_DTYPE), b.astype(MXU_DTYPE), preferred_element_type=F32)


def _split_bf16(x):
    hi = x.astype(BF16)
    return hi, (x - hi.astype(F32)).astype(BF16)


def _headsum_2pass(x, hm):
    hi, lo = _split_bf16(x)
    hb = hm.astype(BF16)
    return jnp.dot(hi, hb, preferred_element_type=F32) + jnp.dot(lo, hb, preferred_element_type=F32)


@jax.custom_vjp
def _headsum(x, hm):
    return _headsum_2pass(x, hm)


_headsum.defvjp(lambda x, hm: (_headsum_2pass(x, hm), hm),
                lambda hm, g: (_headsum_2pass(g, hm), jnp.zeros_like(hm)))


def _sigmoid(x):
    return 0.5 * jnp.tanh(0.5 * x) + 0.5


def _rms(x, g):
    return x * lax.rsqrt(jnp.mean(x * x, axis=-1, keepdims=True) + RMS_EPS) * g


def _norm_mod(x, g, sc, sh):
    return _rms(x, g) * (1.0 + sc) + sh


def _split_ps(ps):
    return (ps[:, 0:RW], ps[:, RW:2 * RW], ps[:, 2 * RW:3 * RW], ps[:, 3 * RW:3 * RW + LW + LA],
            ps[:, 3 * RW + LW + LA:SHIFT])


def _rwkv_prep(r, k, v, wa, gd, w0, w_up_p, a0, a_up_p, g_up, k_k, k_a, hm):
    w_raw = w0 + _dotm(jnp.tanh(wa), w_up_p)
    decay = jnp.exp(-DECAY_SCALE * _sigmoid(w_raw))
    a = _sigmoid(a0 + _dotm(wa, a_up_p))
    g = _dotm(_sigmoid(gd), g_up)
    kk = k * k_k
    kk = kk * lax.rsqrt(_headsum(kk * kk, hm) + L2_EPS)
    k2 = k * (1.0 + (a - 1.0) * k_a)
    return r, decay, k2, v, -kk, kk * a, g


def _rwkv_post(y, r, k2, v, g, ln_g, ln_b, r_k, hm):
    mean = _headsum(y, hm) * (1.0 / HD)
    yc = y - mean
    var = _headsum(yc * yc, hm) * (1.0 / HD)
    yn = yc * lax.rsqrt(var + GN_EPS) * ln_g + ln_b
    bonus = _headsum(r * k2 * r_k, hm) * v
    return (yn + bonus) * g


def _gelu(x):
    return 0.5 * x * (1.0 + jnp.tanh(GELU_C * (x + 0.044715 * (x * x * x))))


def _s5_post(yssm, u, d):
    return _gelu(yssm + d * u)


def _mix(ga, gb, ya, za, zb):
    return _sigmoid(ga) * ya + _sigmoid(gb) * (za * _sigmoid(zb))


def _conv_act(up_g, up_u, hg, hu, w_g, w_u, b_g, b_u):
    gate, upv = _conv3(up_g, hg, w_g, b_g)[0], _conv3(up_u, hu, w_u, b_u)[0]
    return gate, upv


def _conv3(x, h, w, b):
    x, h = x.astype(F32), h.astype(F32)
    s2, s1 = _shift_down(x, h, 2), _shift_down(x, h, 1)
    return b + w[0:1] * s2 + w[1:2] * s1 + w[2:3] * x, (s2, s1, x)


def _silu_gate(gate, upv):
    return gate * _sigmoid(gate) * upv


WKV_L = 64
WKV_KEPT = 5
_NT, _NN, _TN = ((1,), (1,)), ((1,), (0,)), ((0,), (0,))


def _dotw(x, y, dims):
    return lax.dot_general(x.astype(MXU_DTYPE), y.astype(MXU_DTYPE), (dims, ((), ())), preferred_element_type=F32)


def _dot3(x, y, dims):
    (xh, xl), (yh, yl) = _split_bf16(x), _split_bf16(y)
    d = lambda p, q: lax.dot_general(p, q, (dims, ((), ())), preferred_element_type=F32)
    return d(xh, yh) + d(xh, yl) + d(xl, yh)


@jax.custom_vjp
def _gram3(x, y):
    return _dot3(x, y, _NT)


_gram3.defvjp(lambda x, y: (_dot3(x, y, _NT), (x, y)),
              lambda res, g: (_dot3(g, res[1], _NN), _dot3(g, res[0], _TN)))


@jax.custom_vjp
def _gram_known(x, y, value):
    return value


_gram_known.defvjp(lambda x, y, value: (value, (x, y, value)),
                   lambda res, g: (_dot3(g, res[1], _NN), _dot3(g, res[0], _TN), jnp.zeros_like(res[2])))


def _tri_solve_fwd(ns, xs):
    each = lambda f, *ls: tuple(f(*zs) for zs in zip(*ls))
    size = ns[0].shape[0]
    eye = (lax.broadcasted_iota(jnp.int32, (size, size), 0) == lax.broadcasted_iota(jnp.int32, (size, size), 1)).astype(F32)
    ts = each(lambda n: n + eye, ns)
    qs = ns
    for _ in range(WKV_L.bit_length() - 2):
        qs = each(lambda q: _dotw(q, q, _NN), qs)
        ts = each(lambda t, q: t + _dotw(t, q, _NN), ts, qs)
    us = each(lambda t, x: _dotw(t, x, _NN), ts, xs)
    return us, (ts, us)


def _tri_solve_bwd(res, dus):
    ts, us = res
    each = lambda f, *ls: tuple(f(*zs) for zs in zip(*ls))
    dxs = each(lambda t, du: _dotw(t, du, _TN), ts, dus)
    return each(lambda dx, u: _dotw(dx, u, _NT), dxs, us), dxs


@jax.custom_vjp
def _tri_solve(ns, xs):
    return _tri_solve_fwd(ns, xs)[0]


_tri_solve.defvjp(_tri_solve_fwd, _tri_solve_bwd)


@jax.custom_vjp
def _tri_known(ns, xs, ts, us):
    return us


_tri_known.defvjp(lambda ns, xs, ts, us: (us, (ts, us)),
                  lambda res, dus: _tri_solve_bwd(res, dus) + tuple(tuple(jnp.zeros_like(z) for z in r) for r in res))


def _wkv_chunk(s0, r, w, k, v, a, b):
    y, s1 = _wkv_chunks((s0,), (r,), (w,), (k,), (v,), (a,), (b,))
    return y[0], s1[0]


def _wkv_chunks(s0, r, w, k, v, a, b, tinv=None, want_tinv=False):
    each = lambda f, *ls: tuple(f(*xs) for xs in zip(*ls))
    L = r[0].shape[0]
    n2 = 2 * L
    lane_head = lax.broadcasted_iota(jnp.int32, (2, 1, 2 * HD), 2) // HD
    head_mask = (lane_head == lax.broadcasted_iota(jnp.int32, (2, 1, 2 * HD), 0)).astype(F32)
    ri = lax.broadcasted_iota(jnp.int32, (n2, n2), 0)
    ci = lax.broadcasted_iota(jnp.int32, (n2, n2), 1)
    same = (ri // L) == (ci // L)
    strict = same & ((ci % L) < (ri % L))
    incl = same & ((ci % L) <= (ri % L))
    si = lax.broadcasted_iota(jnp.int32, (2 * HD, 2 * HD), 0) // HD
    sj = lax.broadcasted_iota(jnp.int32, (2 * HD, 2 * HD), 1) // HD
    tri = (lax.broadcasted_iota(jnp.int32, (L, L), 0) >= lax.broadcasted_iota(jnp.int32, (L, L), 1)).astype(F32)

    stack = lambda z: (z[None] * head_mask).reshape(n2, 2 * HD)
    dup = lambda z: jnp.broadcast_to(z[None], (2, L, 2 * HD)).reshape(n2, 2 * HD)
    gram = _gram3
    nt, nn, tn = (lambda x, y, d=d: _dotw(x, y, d) for d in (_NT, _NN, _TN))
    add = lambda x, y: x + y

    lw = each(jnp.log, w)
    cum = each(lambda z: jnp.dot(tri, z, preferred_element_type=F32, precision=HIGHEST), lw)
    tot = each(lambda z: jnp.sum(z, axis=0, keepdims=True), lw)
    a2 = each(lambda av, cv, lv: stack(av * jnp.exp(cv - lv)), a, cum, lw)
    r2 = each(lambda rv, cv: stack(rv * jnp.exp(cv)), r, cum)
    v2 = each(stack, v)
    b2 = each(lambda bv, cv: dup(bv * jnp.exp(-cv)), b, cum)
    k2 = each(lambda kv, cv: dup(kv * jnp.exp(-cv)), k, cum)
    n_ab = each(lambda x, y: jnp.where(strict, gram(x, y), 0.0), a2, b2)
    if tinv is None:
        n_ak = each(lambda x, y: jnp.where(strict, gram(x, y), 0.0), a2, k2)
        m_rb = each(lambda x, y: jnp.where(incl, gram(x, y), 0.0), r2, b2)
        m_rk = each(lambda x, y: jnp.where(incl, gram(x, y), 0.0), r2, k2)
    else:
        tinv, k_u, k_ak, k_rb, k_rk = tinv
        n_ak = each(lambda x, y, g: jnp.where(strict, _gram_known(x, y, g), 0.0), a2, k2, k_ak)
        m_rb = each(lambda x, y, g: jnp.where(incl, _gram_known(x, y, g), 0.0), r2, b2, k_rb)
        m_rk = each(lambda x, y, g: jnp.where(incl, _gram_known(x, y, g), 0.0), r2, k2, k_rk)
    x = each(add, each(nt, a2, s0), each(nn, n_ak, v2))
    if want_tinv:
        u, (tinv, _) = _tri_solve_fwd(n_ab, x)
        tinv = (tinv, u, n_ak, m_rb, m_rk)
    else:
        u = _tri_solve(n_ab, x) if tinv is None else _tri_known(n_ab, x, tinv, k_u)
    y2 = each(lambda x, y, z: x + y + z, each(nt, r2, s0), each(nn, m_rb, u), each(nn, m_rk, v2))
    y = each(lambda z: jnp.sum(z.reshape(2, L, 2 * HD), axis=0), y2)
    b3 = each(lambda bv, tv, cv: dup(bv * jnp.exp(tv - cv)), b, tot, cum)
    k3 = each(lambda kv, tv, cv: dup(kv * jnp.exp(tv - cv)), k, tot, cum)
    upd = each(add, each(tn, u, b3), each(tn, v2, k3))
    s1 = each(lambda sv, tv, uv: sv * jnp.exp(tv) + jnp.where(si == sj, uv, 0.0), s0, tot, upd)
    return (y, s1, tinv) if want_tinv else (y, s1)


NPAIR = NH // 2


def _wkv_nb(Bl):
    return 4 if Bl % 4 == 0 else 2 if Bl % 2 == 0 else 1


def _wkv_fwd(r, w, k, v, a, b, Bl, S):
    L = WKV_L
    nC = S // L
    nb = _wkv_nb(Bl)
    chains = [(bi, p, slice(p * 2 * HD, (p + 1) * 2 * HD)) for bi in range(nb) for p in range(NPAIR)]

    def body(r_ref, w_ref, k_ref, v_ref, a_ref, b_ref, y_ref, ck_ref, ti_ref, s_ref):
        @pl.when(pl.program_id(1) == 0)
        def _():
            s_ref[...] = jnp.zeros(s_ref.shape, F32)
        s0 = tuple(s_ref[bi, p] for bi, p, _ in chains)
        ops = [tuple(z[bi, :, cs] for bi, _, cs in chains) for z in (r_ref, w_ref, k_ref, v_ref, a_ref, b_ref)]
        y, s1, kept = _wkv_chunks(s0, *ops, want_tinv=True)
        for i, (bi, p, cs) in enumerate(chains):
            ck_ref[bi, 0, p] = s0[i]
            for q in range(WKV_KEPT):
                ti_ref[bi, 0, p, q] = kept[q][i]
            y_ref[bi, :, cs] = y[i]
            s_ref[bi, p] = s1[i]

    to3 = lambda z: z.reshape(Bl, S, RW)
    row_spec = pl.BlockSpec((nb, L, RW), lambda g, c: (g, c, 0))
    mats = jax.ShapeDtypeStruct((Bl, nC, NPAIR, 2 * HD, 2 * HD), F32)
    mat_spec = pl.BlockSpec((nb, 1, NPAIR, 2 * HD, 2 * HD), lambda g, c: (g, c, 0, 0, 0))
    y, ck, ti = pl.pallas_call(
        body, name="wkv_fwd", grid=(Bl // nb, nC), in_specs=[row_spec] * 6,
        out_specs=[row_spec, mat_spec,
                   pl.BlockSpec((nb, 1, NPAIR, WKV_KEPT, 2 * HD, 2 * HD), lambda g, c: (g, c, 0, 0, 0, 0))],
        out_shape=[jax.ShapeDtypeStruct((Bl, S, RW), F32), mats,
                   jax.ShapeDtypeStruct((Bl, nC, NPAIR, WKV_KEPT, 2 * HD, 2 * HD), F32)],
        scratch_shapes=[pltpu.VMEM((nb, NPAIR, 2 * HD, 2 * HD), F32)],
        compiler_params=_params(("arbitrary", "arbitrary")),
    )(*(to3(z) for z in (r, w, k, v, a, b)))
    return y.reshape(Bl * S, RW), ck, ti


def _wkv_bwd(r, w, k, v, a, b, dy, ck, ti, Bl, S):
    L = WKV_L
    nC = S // L
    nb = _wkv_nb(Bl)
    chains = [(bi, p, slice(p * 2 * HD, (p + 1) * 2 * HD)) for bi in range(nb) for p in range(NPAIR)]

    def body(r_ref, w_ref, k_ref, v_ref, a_ref, b_ref, dy_ref, ck_ref, ti_ref,
             dr_ref, dw_ref, dk_ref, dv_ref, da_ref, db_ref, ds_ref):
        @pl.when(pl.program_id(1) == 0)
        def _():
            ds_ref[...] = jnp.zeros(ds_ref.shape, F32)
        s0 = tuple(ck_ref[bi, 0, p] for bi, p, _ in chains)
        tinv = tuple(tuple(ti_ref[bi, 0, p, q] for bi, p, _ in chains) for q in range(WKV_KEPT))
        ops = [tuple(z[bi, :, cs] for bi, _, cs in chains) for z in (r_ref, w_ref, k_ref, v_ref, a_ref, b_ref)]
        cts = (tuple(dy_ref[bi, :, cs] for bi, _, cs in chains), tuple(ds_ref[bi, p] for bi, p, _ in chains))
        ds0, *grads = jax.vjp(lambda *z: _wkv_chunks(*z, tinv=tinv), s0, *ops)[1](cts)
        for i, (bi, p, cs) in enumerate(chains):
            ds_ref[bi, p] = ds0[i]
            for o, g in zip((dr_ref, dw_ref, dk_ref, dv_ref, da_ref, db_ref), grads):
                o[bi, :, cs] = g[i]

    to3 = lambda z: z.reshape(Bl, S, RW)
    row_spec = pl.BlockSpec((nb, L, RW), lambda g, c: (g, nC - 1 - c, 0))
    rows = jax.ShapeDtypeStruct((Bl, S, RW), F32)
    mat_spec = pl.BlockSpec((nb, 1, NPAIR, 2 * HD, 2 * HD), lambda g, c: (g, nC - 1 - c, 0, 0, 0))
    outs = pl.pallas_call(
        body, name="wkv_bwd", grid=(Bl // nb, nC),
        in_specs=[row_spec] * 7 + [mat_spec, pl.BlockSpec((nb, 1, NPAIR, WKV_KEPT, 2 * HD, 2 * HD),
                                                          lambda g, c: (g, nC - 1 - c, 0, 0, 0, 0))],
        out_specs=[row_spec] * 6, out_shape=[rows] * 6,
        scratch_shapes=[pltpu.VMEM((nb, NPAIR, 2 * HD, 2 * HD), F32)],
        compiler_params=_params(("arbitrary", "arbitrary")),
    )(*(to3(z) for z in (r, w, k, v, a, b, dy)), ck, ti)
    return [o.reshape(Bl * S, RW) for o in outs]


NST = NG * SP


def _cmul(ar, ai, br, bi):
    return ar * br - ai * bi, ar * bi + ai * br


def _s5_tiles(are, aim, reverse):
    if reverse:
        aim = -aim
    row = lax.broadcasted_iota(jnp.int32, (SUBLANES, NST), 0)
    pw = [(are, aim)]
    for _ in range(SUBLANES - 1):
        pw.append(_cmul(pw[-1][0], pw[-1][1], are, aim))
    bc = lambda z: jnp.broadcast_to(z, (SUBLANES, NST))
    ms = []
    for kk in (1, 2, 4):
        cond = (row < SUBLANES - kk) if reverse else (row >= kk)
        ms.append((jnp.where(cond, bc(pw[kk - 1][0]), 0.0), jnp.where(cond, bc(pw[kk - 1][1]), 0.0)))
    pr = jnp.zeros((SUBLANES, NST), F32)
    pi = jnp.zeros((SUBLANES, NST), F32)
    for i in range(SUBLANES):
        n = SUBLANES - i if reverse else i + 1
        pr = jnp.where(row == i, bc(pw[n - 1][0]), pr)
        pi = jnp.where(row == i, bc(pw[n - 1][1]), pi)
    return ms, (pr, pi)


def _s5_block(re, im, ms, pc, cre, cim, sg, reverse):
    ln = slice(sg * 512, (sg + 1) * 512)
    for (mr, mi), kk in zip(ms, (1, 2, 4)):
        sh = SUBLANES - kk if reverse else kk
        sre, sim = pltpu.roll(re, sh, 0), pltpu.roll(im, sh, 0)
        tr, ti = _cmul(mr[:, ln], mi[:, ln], sre, sim)
        re, im = re + tr, im + ti
    tr, ti = _cmul(pc[0][:, ln], pc[1][:, ln], cre[:, ln], cim[:, ln])
    return re + tr, im + ti


def _s5_scan(X_ref, n_rows, ms, pc, cre, cim, reverse, visit=None, acc0=None):
    nblk = n_rows // SUBLANES

    def it(i, carry):
        cre, cim, acc = carry
        j = nblk - 1 - i if reverse else i
        rows = pl.ds(pl.multiple_of(j * SUBLANES, SUBLANES), SUBLANES)
        edge = 0 if reverse else SUBLANES - 1
        blocks, ncre, ncim = [], [], []
        for sg in range(NSG):
            lr = slice(sg * 1024, sg * 1024 + 512)
            li = slice(sg * 1024 + 512, (sg + 1) * 1024)
            re, im = _s5_block(X_ref[rows, lr], X_ref[rows, li], ms, pc, cre, cim, sg, reverse)
            X_ref[rows, lr] = re
            X_ref[rows, li] = im
            blocks.append((re, im))
            ncre.append(re[edge:edge + 1])
            ncim.append(im[edge:edge + 1])
        if visit is not None:
            acc = visit(j, blocks, acc)
        return jnp.concatenate(ncre, axis=1), jnp.concatenate(ncim, axis=1), acc

    return lax.fori_loop(0, nblk, it, (cre, cim, acc0 if acc0 is not None else 0))


def _s5_fwd(u, wb, wc, ab, d, Bl, S, R=256):
    R = min(R, S)
    nC = S // R

    def body(u_ref, wb_ref, wc_ref, ab_ref, d_ref, y_ref, st_ref, X_ref, o_ref, car_ref):
        @pl.when(pl.program_id(1) == 0)
        def _():
            car_ref[...] = jnp.zeros(car_ref.shape, F32)
        st_ref[0, 0] = car_ref[...]
        ms, pc = _s5_tiles(ab_ref[0:1], ab_ref[1:2], False)
        for sg in range(NSG):
            X_ref[:, sg * 1024:(sg + 1) * 1024] = _dotm(u_ref[:, sg * 128:(sg + 1) * 128], wb_ref[sg])
        cre, cim, _ = _s5_scan(X_ref, R, ms, pc, car_ref[0:1], car_ref[1:2], False)
        car_ref[0:1] = cre
        car_ref[1:2] = cim
        for sg in range(NSG):
            y_ref[:, sg * 128:(sg + 1) * 128] = _dotm(X_ref[:, sg * 1024:(sg + 1) * 1024], wc_ref[sg])
        o_ref[...] = _s5_post(y_ref[...], u_ref[...], d_ref[...]).astype(o_ref.dtype)

    rows = pl.BlockSpec((R, SW), lambda b, c: (b * nC + c, 0))
    return pl.pallas_call(
        body, name="s5_fwd", grid=(Bl, nC),
        in_specs=[rows, pl.BlockSpec(wb.shape, lambda b, c: (0, 0, 0)), pl.BlockSpec(wc.shape, lambda b, c: (0, 0, 0)),
                  pl.BlockSpec(ab.shape, lambda b, c: (0, 0)), pl.BlockSpec(d.shape, lambda b, c: (0, 0))],
        out_specs=[rows, pl.BlockSpec((1, 1, 2, NST), lambda b, c: (b, c, 0, 0)),
                   pl.BlockSpec((R, 2 * NST), lambda b, c: (b * nC + c, 0)), rows],
        out_shape=[jax.ShapeDtypeStruct((Bl * S, SW), F32), jax.ShapeDtypeStruct((Bl, nC, 2, NST), F32),
                   jax.ShapeDtypeStruct((Bl * S, 2 * NST), F32), jax.ShapeDtypeStruct((Bl * S, SW), MXU_DTYPE)],
        scratch_shapes=[pltpu.VMEM((2, NST), F32)],
        compiler_params=_params(("arbitrary", "arbitrary")),
    )(u, wb, wc, ab, d)


def _s5_bwd(u, y, do, d, wb, wc, ab, st, xs, Bl, S, R=256):
    R = min(R, S)
    nC = S // R

    def body(u_ref, y_ref, do_ref, d_ref, wb_ref, wc_ref, ab_ref, st_ref, X_ref,
             du_ref, dwb_ref, dwc_ref, dab_ref, dd_ref, G_ref, car_ref):
        first = (pl.program_id(0) == 0) & (pl.program_id(1) == 0)

        @pl.when(first)
        def _():
            for o in (dwb_ref, dwc_ref, dab_ref, dd_ref):
                o[...] = jnp.zeros(o.shape, F32)

        @pl.when(pl.program_id(1) == 0)
        def _():
            car_ref[...] = jnp.zeros(car_ref.shape, F32)

        are, aim = ab_ref[0:1], ab_ref[1:2]
        dy, du_direct, dd = jax.vjp(_s5_post, y_ref[...], u_ref[...], d_ref[...])[1](do_ref[...])
        dd_ref[...] += dd
        dyv = dy.astype(MXU_DTYPE)
        for sg in range(NSG):
            G_ref[:, sg * 1024:(sg + 1) * 1024] = lax.dot_general(
                dyv[:, sg * 128:(sg + 1) * 128], wc_ref[sg].astype(MXU_DTYPE), (((1,), (1,)), ((), ())),
                preferred_element_type=F32)
        rms_, rpc = _s5_tiles(are, aim, True)
        row = lax.broadcasted_iota(jnp.int32, (SUBLANES, 512), 0)

        def visit(j, blocks, acc):
            before = pl.multiple_of(jnp.maximum(j - 1, 0) * SUBLANES, SUBLANES)
            prow = X_ref[pl.ds(before, SUBLANES), :][SUBLANES - 1:SUBLANES]
            rows = pl.ds(pl.multiple_of(j * SUBLANES, SUBLANES), SUBLANES)
            are_acc, aim_acc = [], []
            for sg in range(NSG):
                lr = slice(sg * 1024, sg * 1024 + 512)
                li = slice(sg * 1024 + 512, (sg + 1) * 1024)
                ln = slice(sg * 512, (sg + 1) * 512)
                pre = jnp.where(j > 0, prow[:, lr], st_ref[0, 0, 0:1, ln])
                pim = jnp.where(j > 0, prow[:, li], st_ref[0, 0, 1:2, ln])
                xre = jnp.where(row == 0, pre, pltpu.roll(X_ref[rows, lr], 1, 0))
                xim = jnp.where(row == 0, pim, pltpu.roll(X_ref[rows, li], 1, 0))
                dre, dim = blocks[sg]
                are_acc.append(dre * xre + dim * xim)
                aim_acc.append(dim * xre - dre * xim)
            return acc[0] + jnp.concatenate(are_acc, axis=1), acc[1] + jnp.concatenate(aim_acc, axis=1)

        zero = jnp.zeros((SUBLANES, NST), F32)
        cre, cim, acc = _s5_scan(G_ref, R, rms_, rpc, car_ref[0:1], car_ref[1:2], True, visit, (zero, zero))
        car_ref[0:1] = cre
        car_ref[1:2] = cim
        dab_ref[0:1] += jnp.sum(acc[0], axis=0, keepdims=True)
        dab_ref[1:2] += jnp.sum(acc[1], axis=0, keepdims=True)
        uv = u_ref[...].astype(MXU_DTYPE)
        for sg in range(NSG):
            cs = slice(sg * 1024, (sg + 1) * 1024)
            us = slice(sg * 128, (sg + 1) * 128)
            gx = G_ref[:, cs].astype(MXU_DTYPE)
            dwb_ref[sg] += lax.dot_general(uv[:, us], gx, (((0,), (0,)), ((), ())), preferred_element_type=F32)
            dwc_ref[sg] += lax.dot_general(X_ref[:, cs].astype(MXU_DTYPE), dyv[:, us], (((0,), (0,)), ((), ())),
                                           preferred_element_type=F32)
            du_ssm = lax.dot_general(gx, wb_ref[sg].astype(MXU_DTYPE), (((1,), (1,)), ((), ())),
                                     preferred_element_type=F32)
            du_ref[:, us] = (du_ssm + du_direct[:, us]).astype(du_ref.dtype)

    rmap = lambda b, c: (b * nC + nC - 1 - c, 0)
    rows = pl.BlockSpec((R, SW), rmap)
    return pl.pallas_call(
        body, name="s5_bwd", grid=(Bl, nC),
        in_specs=[rows, rows, rows, pl.BlockSpec(d.shape, lambda b, c: (0, 0)),
                  pl.BlockSpec(wb.shape, lambda b, c: (0, 0, 0)), pl.BlockSpec(wc.shape, lambda b, c: (0, 0, 0)),
                  pl.BlockSpec(ab.shape, lambda b, c: (0, 0)),
                  pl.BlockSpec((1, 1, 2, NST), lambda b, c: (b, nC - 1 - c, 0, 0)),
                  pl.BlockSpec((R, 2 * NST), rmap)],
        out_specs=[rows, pl.BlockSpec(wb.shape, lambda b, c: (0, 0, 0)),
                   pl.BlockSpec(wc.shape, lambda b, c: (0, 0, 0)), pl.BlockSpec((2, NST), lambda b, c: (0, 0)),
                   pl.BlockSpec(d.shape, lambda b, c: (0, 0))],
        out_shape=[jax.ShapeDtypeStruct((Bl * S, SW), MXU_DTYPE), jax.ShapeDtypeStruct(wb.shape, F32),
                   jax.ShapeDtypeStruct(wc.shape, F32), jax.ShapeDtypeStruct((2, NST), F32),
                   jax.ShapeDtypeStruct(d.shape, F32)],
        scratch_shapes=[pltpu.VMEM((R, 2 * NST), F32), pltpu.VMEM((2, NST), F32)],
        compiler_params=_params(("arbitrary", "arbitrary")),
    )(u, y, do, d, wb, wc, ab, st, xs)


def _s5_disc_math(a_re, a_im, log_dt, b_re, b_im, expand):
    dt = jnp.exp(log_dt)
    z_re, z_im = a_re * dt, a_im * dt
    mag = jnp.exp(z_re)
    ab_re, ab_im = mag * jnp.cos(z_im), mag * jnp.sin(z_im)
    den = a_re * a_re + a_im * a_im
    q_re = ((ab_re - 1.0) * a_re + ab_im * a_im) / den
    q_im = (ab_im * a_re - (ab_re - 1.0) * a_im) / den
    qe_re = jnp.dot(q_re, expand, preferred_element_type=F32, precision=HIGHEST)
    qe_im = jnp.dot(q_im, expand, preferred_element_type=F32, precision=HIGHEST)
    return ab_re, ab_im, qe_re * b_re - qe_im * b_im, qe_re * b_im + qe_im * b_re


def _whole(shape):
    return pl.BlockSpec(shape, lambda nd=len(shape): (0,) * nd)


def _s5_disc(a_re, a_im, log_dt, b_re, b_im, expand):
    def body(a, b, c, d, e, f, o0, o1, o2, o3):
        res = _s5_disc_math(a[...], b[...], c[...], d[...], e[...], f[...])
        for o, v in zip((o0, o1, o2, o3), res):
            o[...] = v
    ins = (a_re, a_im, log_dt, b_re, b_im, expand)
    outs = [jax.ShapeDtypeStruct(a_re.shape, F32)] * 2 + [jax.ShapeDtypeStruct(b_re.shape, F32)] * 2
    return pl.pallas_call(body, name="s5_disc", in_specs=[_whole(x.shape) for x in ins],
                          out_specs=[_whole(o.shape) for o in outs], out_shape=outs)(*ins)


def _s5_disc_bwd(a_re, a_im, log_dt, b_re, b_im, expand, cts):
    def body(a, b, c, d, e, f, g0, g1, g2, g3, o0, o1, o2, o3, o4):
        fn = lambda *p: _s5_disc_math(*p, f[...])
        _, vjp = jax.vjp(fn, a[...], b[...], c[...], d[...], e[...])
        for o, v in zip((o0, o1, o2, o3, o4), vjp((g0[...], g1[...], g2[...], g3[...]))):
            o[...] = v
    ins = (a_re, a_im, log_dt, b_re, b_im, expand) + tuple(cts)
    outs = [jax.ShapeDtypeStruct(x.shape, F32) for x in (a_re, a_im, log_dt, b_re, b_im)]
    return pl.pallas_call(body, name="s5_disc_bwd", in_specs=[_whole(x.shape) for x in ins],
                          out_specs=[_whole(o.shape) for o in outs], out_shape=outs)(*ins)


def _ada_fwd(c_all, w_shard, b_shard):
    def body(c_ref, w_ref, b_ref, o_ref):
        cv = c_ref[...]
        o_ref[...] = _dotm(cv * _sigmoid(cv), w_ref[...]) + b_ref[...]
    n = w_shard.shape[1]
    return pl.pallas_call(
        body, name="ada_fwd", in_specs=[_whole(c_all.shape), _whole(w_shard.shape), _whole(b_shard.shape)],
        out_specs=_whole((c_all.shape[0], n)), out_shape=jax.ShapeDtypeStruct((c_all.shape[0], n), F32),
        compiler_params=_params(),
    )(c_all, w_shard, b_shard)


def _ada_bwd(c_all, dmod_cols, dmod_all):
    def body(c_ref, dc_ref, da_ref, gw_ref, gb_ref):
        cv = c_ref[...]
        gw_ref[...] = lax.dot_general((cv * _sigmoid(cv)).astype(MXU_DTYPE), dc_ref[...].astype(MXU_DTYPE),
                                      (((0,), (0,)), ((), ())), preferred_element_type=F32)
        gb_ref[...] = jnp.sum(da_ref[...], axis=0, keepdims=True)
    n = dmod_cols.shape[1]
    return pl.pallas_call(
        body, name="ada_bwd", in_specs=[_whole(c_all.shape), _whole(dmod_cols.shape), _whole(dmod_all.shape)],
        out_specs=[_whole((D, n)), _whole((1, dmod_all.shape[1]))],
        out_shape=[jax.ShapeDtypeStruct((D, n), F32), jax.ShapeDtypeStruct((1, dmod_all.shape[1]), F32)],
        compiler_params=_params(),
    )(c_all, dmod_cols, dmod_all)


def _rows_block(n_rows, cap=512):
    if n_rows <= cap:
        return n_rows
    for t in range(cap - cap % SUBLANES, 0, -SUBLANES):
        if n_rows % t == 0:
            return t
    return n_rows


def _adamw(w, g, m, v, name):
    rows, cols = w.shape
    tr = _rows_block(rows, max(SUBLANES, (1 << 19) // max(cols, 1) // SUBLANES * SUBLANES))

    def body(w_ref, g_ref, m_ref, v_ref, d_ref, nm_ref, nv_ref):
        gv = g_ref[...]
        nm = B1 * m_ref[...] + (1.0 - B1) * gv
        nv = B2 * v_ref[...] + (1.0 - B2) * (gv * gv)
        m_hat = nm / (1.0 - B1 ** STEP)
        v_hat = nv / (1.0 - B2 ** STEP)
        d_ref[...] = -LR * (m_hat / (jnp.sqrt(v_hat) + ADAM_EPS) + WD * w_ref[...])
        nm_ref[...] = nm
        nv_ref[...] = nv

    spec = pl.BlockSpec((tr, cols), lambda i: (i, 0))
    sd = jax.ShapeDtypeStruct((rows, cols), F32)
    return pl.pallas_call(body, name=name, grid=(rows // tr,), in_specs=[spec] * 4, out_specs=[spec] * 3,
                          out_shape=[sd] * 3, compiler_params=_params(("parallel",)))(w, g, m, v)


def _sum_slots(x, out_dtype, name):
    xs = x if isinstance(x, (list, tuple)) else [x]
    _, rows, cols = xs[0].shape
    tr = _rows_block(rows)

    def body(*refs):
        acc = None
        for x_ref in refs[:-1]:
            for j in range(x_ref.shape[0]):
                term = x_ref[j].astype(F32)
                acc = term if acc is None else acc + term
        refs[-1][...] = acc.astype(refs[-1].dtype)

    return pl.pallas_call(
        body, name=name, grid=(rows // tr,),
        in_specs=[pl.BlockSpec((z.shape[0], tr, cols), lambda i: (0, i, 0)) for z in xs],
        out_specs=pl.BlockSpec((tr, cols), lambda i: (i, 0)), out_shape=jax.ShapeDtypeStruct((rows, cols), out_dtype),
        compiler_params=_params(("parallel",)))(*xs)


PACK_COLS = 1024


def _pack_rows(parts, dtype, row_mult):
    flat = jnp.concatenate([p.reshape(-1).astype(dtype) for p in parts])
    per = PACK_COLS * row_mult
    n = -(-flat.shape[0] // per) * per
    flat = jnp.pad(flat, (0, n - flat.shape[0]))
    return flat.reshape(n // PACK_COLS, PACK_COLS)


def _unpack(flat, shapes):
    out, off = [], 0
    for s in shapes:
        n = math.prod(s)
        out.append(flat[off:off + n].reshape(s))
        off += n
    return out


BIG = (("w_in", (D, SHIFT + SW + 2 * D), 1), ("w_out_rwkv", (RW, D), 1), ("w_glu", (SW, 2 * D), 1),
       ("w_out", (D, D), 0), ("w_ffn_up", (D, 2 * DFF), 1), ("w_ffn_down", (DFF, D), 0))
BIG_SMALL = (("rwkv_w_up", (LW, RW), 1), ("rwkv_a_up", (LA, RW), 1), ("rwkv_g_up", (LG, RW), 1),
             ("ffn_conv_w", (3, 2 * DFF), 1))
BIG_LATE = BIG[4:]
BIG_MID = BIG[1:4]


def _shard_shape(shape, axis):
    return (shape[0] // 4, shape[1]) if axis == 0 else (shape[0], shape[1] // 4)


def _to_shards(g, axis):
    r, C = g.shape
    return g.reshape(4, r // 4, C) if axis == 0 else g.reshape(r, 4, C // 4).transpose(1, 0, 2)


def _from_shards(x, axis):
    _, r, C = x.shape
    return x.reshape(4 * r, C) if axis == 0 else x.transpose(1, 0, 2).reshape(r, 4 * C)


def kernel(x, c, w_ada, b_ada, norm1_g, w_in, mu_shift, rwkv_w0, rwkv_w_up, rwkv_a0, rwkv_a_up, rwkv_g_up, rwkv_k_k, rwkv_k_a, rwkv_r_k, rwkv_ln_g, rwkv_ln_b, w_out_rwkv, s5_a_re, s5_a_im, s5_log_dt, s5_b_re, s5_b_im, s5_c_re, s5_c_im, s5_d, w_glu, w_out, norm2_g, w_ffn_up, ffn_conv_w, ffn_conv_b, w_ffn_down, norm_f_g, loss_target, m_w_ada, m_b_ada, m_norm1_g, m_w_in, m_mu_shift, m_rwkv_w0, m_rwkv_w_up, m_rwkv_a0, m_rwkv_a_up, m_rwkv_g_up, m_rwkv_k_k, m_rwkv_k_a, m_rwkv_r_k, m_rwkv_ln_g, m_rwkv_ln_b, m_w_out_rwkv, m_s5_a_re, m_s5_a_im, m_s5_log_dt, m_s5_b_re, m_s5_b_im, m_s5_c_re, m_s5_c_im, m_s5_d, m_w_glu, m_w_out, m_norm2_g, m_w_ffn_up, m_ffn_conv_w, m_ffn_conv_b, m_w_ffn_down, m_norm_f_g, v_w_ada, v_b_ada, v_norm1_g, v_w_in, v_mu_shift, v_rwkv_w0, v_rwkv_w_up, v_rwkv_a0, v_rwkv_a_up, v_rwkv_g_up, v_rwkv_k_k, v_rwkv_k_a, v_rwkv_r_k, v_rwkv_ln_g, v_rwkv_ln_b, v_w_out_rwkv, v_s5_a_re, v_s5_a_im, v_s5_log_dt, v_s5_b_re, v_s5_b_im, v_s5_c_re, v_s5_c_im, v_s5_d, v_w_glu, v_w_out, v_norm2_g, v_w_ffn_up, v_ffn_conv_w, v_ffn_conv_b, v_w_ffn_down, v_norm_f_g):
    names = ["w_ada", "b_ada", "norm1_g", "w_in", "mu_shift", "rwkv_w0", "rwkv_w_up", "rwkv_a0", "rwkv_a_up",
             "rwkv_g_up", "rwkv_k_k", "rwkv_k_a", "rwkv_r_k", "rwkv_ln_g", "rwkv_ln_b", "w_out_rwkv", "s5_a_re",
             "s5_a_im", "s5_log_dt", "s5_b_re", "s5_b_im", "s5_c_re", "s5_c_im", "s5_d", "w_glu", "w_out", "norm2_g",
             "w_ffn_up", "ffn_conv_w", "ffn_conv_b", "w_ffn_down", "norm_f_g"]
    env = dict(locals())
    W = {n: env[n] for n in names}
    M = {n: env["m_" + n] for n in names}
    V = {n: env["v_" + n] for n in names}

    Bl, S, _ = x.shape
    T = Bl * S
    ix, iy, ic = lax.axis_index("x"), lax.axis_index("y"), lax.axis_index("c")
    chip = 2 * ix + iy
    dev = 2 * chip + ic
    rw = functools.partial(_rowwise, Bl=Bl, S=S)

    got_chip, got_dev = _gather_two_level([W[n][0] for n, _, _ in BIG_SMALL[:3]], [W["ffn_conv_w"][0], c], "gather_w")
    full = {n: _from_shards(g, axis) for (n, _, axis), g in zip(BIG_SMALL[:3], got_chip)}
    full["ffn_conv_w"] = _from_shards(got_dev[0][:, 0], 1)
    c_all = got_dev[1].reshape(8 * Bl, D)
    zeros_l = jnp.zeros((LW, RW), F32)
    w_up_p = jnp.concatenate([full["rwkv_w_up"], zeros_l], axis=0)
    a_up_p = jnp.concatenate([zeros_l, full["rwkv_a_up"]], axis=0)
    g_up = full["rwkv_g_up"]
    conv_w = full["ffn_conv_w"]
    conv_wg, conv_wu = conv_w[:, :DFF], conv_w[:, DFF:]
    conv_bg, conv_bu = ffn_conv_b[:, :DFF], ffn_conv_b[:, DFF:]
    hm = jnp.kron(jnp.eye(NH, dtype=F32), jnp.ones((HD, HD), F32))

    ncol = 6 * D // 4
    b_ada_cols = lax.dynamic_slice_in_dim(b_ada, chip * ncol, ncol, 1)
    mod_part = _ada_fwd(c_all, w_ada[0], b_ada_cols)
    mod4 = _gather_two_level([], [mod_part], "gather_mod")[1][0][:, 0]
    mod4, shards = lax.optimization_barrier((mod4, [W[n][0].astype(MXU_DTYPE) for n, _, _ in BIG]))

    def push_shards(tag, arrs):
        moves = [(i, i, lambda ref, me, peer: ref, lambda ref, me, k: ref.at[_chip_of(me)]) for i in range(len(arrs))]
        lands = [jax.ShapeDtypeStruct((4,) + z.shape, z.dtype) for z in arrs]
        return _send_start("gather_%s_start" % tag, CHIP_FLIPS, arrs, lands, moves), moves

    def pushed_shards(tag, started, moves, after, group):
        owns, gots = _send_wait("gather_%s_wait" % tag, CHIP_FLIPS, started, moves, after)
        for (n, _, axis), own, got in zip(group, owns, gots):
            full[n] = _from_shards(lax.dynamic_update_slice(got, own[None], (chip, 0, 0)), axis)

    first_start, first_moves = push_shards("in", shards[:1])
    norm1_g = norm1_g + first_start["token"]
    mod =lax.dynamic_slice_in_dim(mod4, dev * Bl, Bl, 1).transpose(1, 0, 2).reshape(Bl, 1, 6 * D)
    SH1, SC1, GT1, SH2, SC2, GT2 = range(6)

    x2d = x.reshape(T, D)
    tgt = loss_target.reshape(T, D)

    (h1,) = rw("norm1", lambda xv, sc, sh, g: _norm_mod(xv, g, sc, sh), R=512, tiled=[(x2d, D, 0)],
               batch=[(mod, D, SC1), (mod, D, SH1)], full=[norm1_g], out_tiled=[(D, MXU_DTYPE)])
    pushed_shards("in", first_start, first_moves, h1, BIG[:1])
    full["w_in"], rest = lax.optimization_barrier((full["w_in"], shards[1:]))
    late_start, late_moves = push_shards("rest", rest)
    mu_shift = mu_shift + late_start["token"]
    w_p, w_u, w_g = full["w_in"][:, :SHIFT], full["w_in"][:, SHIFT:SHIFT + SW], full["w_in"][:, SHIFT + SW:]
    p, u, gates = _mm_split(h1, full["w_in"], ((0, SHIFT, F32), (SHIFT, SW, F32), (SHIFT + SW, 2 * D, MXU_DTYPE)), "proj")

    prep_params = [rwkv_w0, w_up_p, rwkv_a0, a_up_p, g_up, rwkv_k_k, rwkv_k_a, hm]

    def prep_fwd(pv, ph, mu, *pp):
        ps = pv + (_shift_down(pv, ph, 1) - pv) * mu
        return _rwkv_prep(*_split_ps(ps), *pp)

    r_, w_, k_, v_, a_, b_, g_ = rw("rwkv_prep", prep_fwd, R=256, tiled=[(p, SHIFT, 0)], prev=[(p, SHIFT, 0)],
                                    full=[mu_shift] + prep_params, out_tiled=[(RW, F32)] * 7)
    y_wkv, ck, tinv = _wkv_fwd(r_, w_, k_, v_, a_, b_, Bl, S)
    r_k_row = rwkv_r_k.reshape(1, RW)
    post_params = [rwkv_ln_g, rwkv_ln_b, r_k_row, hm]
    (o_rwkv,) = rw("rwkv_post", _rwkv_post, R=256,
                   tiled=[(y_wkv, RW, 0), (r_, RW, 0), (k_, RW, 0), (v_, RW, 0), (g_, RW, 0)],
                   full=post_params, out_tiled=[(RW, MXU_DTYPE)])
    pushed_shards("rest", late_start, late_moves, o_rwkv, BIG[1:])
    y_a = _mm([o_rwkv], [full["w_out_rwkv"]], MXU_DTYPE, "out_rwkv")

    expand = jnp.kron(jnp.eye(SP, dtype=F32), jnp.ones((1, SGC), F32))
    s5_in = (s5_a_re[0], s5_a_im[0], s5_log_dt[0].reshape(NG, 1), s5_b_re[0].reshape(NG, SP * SGC),
             s5_b_im[0].reshape(NG, SP * SGC), expand)
    ab_re, ab_im, bb_re, bb_im = _s5_disc(*s5_in)
    eye8 = jnp.eye(8, dtype=F32)

    def blockdiag_in(bb):
        t = bb.reshape(NSG, 8, SP, SGC)
        return jnp.einsum("ab,sapc->sacbp", eye8, t).reshape(NSG, 128, 512)

    def blockdiag_out(cc):
        t = cc.reshape(NSG, 8, SGC, SP)
        return jnp.einsum("ab,sacp->sapbc", eye8, t).reshape(NSG, 512, 128)

    wb = jnp.concatenate([blockdiag_in(bb_re), blockdiag_in(bb_im)], axis=2).astype(MXU_DTYPE)
    wc = jnp.concatenate([blockdiag_out(s5_c_re[0]), -blockdiag_out(s5_c_im[0])], axis=1).astype(MXU_DTYPE)
    ab = jnp.stack([ab_re.reshape(NST), ab_im.reshape(NST)])
    y_ssm, s5_st, s5_x, s5o = _s5_fwd(u, wb, wc, ab, s5_d, Bl, S)
    z = _mm([s5o], [full["w_glu"]], MXU_DTYPE, "glu")
    mix_tiled = [(gates, D, 0), (gates, D, 1), (y_a, D, 0), (z, D, 0), (z, D, 1)]
    (mixed_in,) = rw("mix", lambda *a: _mix(*(v.astype(F32) for v in a)), R=256, tiled=mix_tiled,
                     out_tiled=[(D, MXU_DTYPE)])
    mixed = _mm([mixed_in], [full["w_out"]], F32, "out_proj")

    def norm2_fwd(xv, mx, gt, sc, sh, g):
        x1 = xv + gt * mx
        return x1, _norm_mod(x1, g, sc, sh)

    x1, h2 = rw("norm2", norm2_fwd, R=512, tiled=[(x2d, D, 0), (mixed, D, 0)],
                batch=[(mod, D, GT1), (mod, D, SC2), (mod, D, SH2)], full=[norm2_g],
                out_tiled=[(D, F32), (D, MXU_DTYPE)])
    up =_mm([h2], [full["w_ffn_up"]], MXU_DTYPE, "ffn_up")
    conv_tiled = [(up, 0), (up, 1)]
    conv_full = [conv_wg, conv_wu, conv_bg, conv_bu]
    cw = functools.partial(_colwise, Bl=Bl, S=S, R=128, W=DFF, strip=LANES)

    def act_fwd(*a):
        return ((_silu_gate(*_conv_act(*a)),),)

    (act,) = cw("ffn_act", act_fwd, tiled=conv_tiled, prev=conv_tiled, full=conv_full, out_tiled=[(1, MXU_DTYPE)])
    ffn = _mm([act], [full["w_ffn_down"]], F32, "ffn_down")

    def head(x1v, fv, tv, gt, g):
        x2 = x1v + gt * fv
        y, vjp = jax.vjp(_rms, x2, g)
        e = y - tv
        dx2, dg = vjp(e * (1.0 / D))
        loss = jnp.sum(e * e, keepdims=True) * jnp.ones((1, LANES), F32)
        return dx2, dx2 * gt, jnp.sum(dx2 * fv, axis=0, keepdims=True), dg.reshape(1, D), loss

    dx2, d_ffn, d_gt2, g_norm_f, loss_acc = rw(
        "head", head, R=512, tiled=[(x1, D, 0), (ffn, D, 0), (tgt, D, 0)], batch=[(mod, D, GT2)],
        full=[norm_f_g.reshape(1, D)], out_tiled=[(D, F32), (D, MXU_DTYPE)], out_batch=[D],
        out_acc=[(1, D), (1, LANES)])
    loss = lax.psum(0.5 / D * loss_acc[0, 0], ("x", "y", "c"))

    d_act = _mm([d_ffn], [full["w_ffn_down"]], F32, "d_act", bt=True)
    g_w_ffn_down = _mm_tn(act, d_ffn, "g_ffn_down")

    def act_bwd(ug, uu, dact, hg, hu, wg, wu, bg, bu):
        (gate, taps_g), (upv, taps_u) = _conv3(ug, hg, wg, bg), _conv3(uu, hu, wu, bu)
        _, vjp_s = jax.vjp(_silu_gate, gate, upv)
        d_gate, d_upv = vjp_s(dact)
        def taps(dh, shifted):
            return [jnp.sum(dh * s, axis=0, keepdims=True) for s in shifted] + [jnp.sum(dh, axis=0, keepdims=True)]
        return ((d_gate,), (d_upv,), *taps(d_gate, taps_g), *taps(d_upv, taps_u))

    dh_g, dh_u, *tapg = cw("ffn_act_bwd", act_bwd, tiled=conv_tiled + [(d_act, 0)], prev=conv_tiled, full=conv_full,
                           out_tiled=[(1, MXU_DTYPE), (1, MXU_DTYPE)], n_acc=8)
    g_cw_g, g_cb_g = jnp.concatenate(tapg[0:3], axis=0), tapg[3]
    g_cw_u, g_cb_u = jnp.concatenate(tapg[4:7], axis=0), tapg[7]

    def conv_t(dg, du_, ng, nu, wg, wu):
        dg, du_, ng, nu = (z.astype(F32) for z in (dg, du_, ng, nu))

        def ct(d, n, w):
            return w[2:3] * d + w[1:2] * _shift_up(d, n, 1) + w[0:1] * _shift_up(d, n, 2)
        return ((ct(dg, ng, wg), ct(du_, nu, wu)),)

    (d_up,) = cw("conv_bwd", conv_t, tiled=[(dh_g, 0), (dh_u, 0)], nxt=[(dh_g, 0), (dh_u, 0)],
                 full=[conv_wg, conv_wu], out_tiled=[(2, MXU_DTYPE)])
    d_h2 = _mm([d_up], [full["w_ffn_up"]], F32, "d_h2", bt=True)
    g_w_ffn_up = _mm_tn(h2, d_up, "g_ffn_up")

    sds = jax.ShapeDtypeStruct
    reduce_src = lambda r: (lambda ref, me, peer: ref.at[_chip_of(peer), _half(r, peer[2])])

    def reduced_halves(tag, started, moves, after):
        gsh_own, got = _send_wait("rs_%s_wait" % tag, ALL_FLIPS, started, moves, after)
        halves = []
        for i, (g, gt) in enumerate(zip(gsh_own, got)):
            h = g.shape[1] // 2
            own = lax.dynamic_slice(g, (chip, ic * h, 0), (1, h, g.shape[2]))
            halves.append(_sum_slots([own, gt], F32, "rs_%s_sum%d" % (tag, i)))
        return halves

    def share_start(tag, halves):
        moves = [(i, i, lambda ref, me, peer: ref, lambda ref, me, k, r=2 * g.shape[0]: ref.at[_half(r, me[2])])
                 for i, g in enumerate(halves)]
        lands = [sds((2 * g.shape[0], g.shape[1]), F32) for g in halves]
        return _send_start("share_%s_start" % tag, PAIR_FLIPS, halves, lands, moves), moves

    def share_finish(tag, started, moves, after, group, grads):
        mine_h, got = _send_wait("share_%s_wait" % tag, PAIR_FLIPS, started, moves, after)
        for (n, _, _), mh, whole in zip(group, mine_h, got):
            grads[n] = lax.dynamic_update_slice(whole, mh, (ic * mh.shape[0], 0))[None]

    def reduce_start(tag, group, mats):
        gsh = [_to_shards(g, ax).astype(MXU_DTYPE) for g, (_, _, ax) in zip(mats, group)]
        moves = [(i, i, reduce_src(g.shape[1]), lambda ref, me, k: ref.at[k]) for i, g in enumerate(gsh)]
        lands = [sds((len(ALL_FLIPS), g.shape[1] // 2, g.shape[2]), MXU_DTYPE) for g in gsh]
        return _send_start("rs_%s_start" % tag, ALL_FLIPS, gsh, lands, moves), moves

    rsl, rsl_moves = reduce_start("ffn", BIG_LATE, (g_w_ffn_up, g_w_ffn_down))
    norm2_g = norm2_g + rsl["token"]

    def norm2_bwd(x1v, dh2, dx2v, mx, gt, sc, sh, g):
        _, vjp = jax.vjp(_norm_mod, x1v, g, sc, sh)
        dxn, dg, dsc, dsh = vjp(dh2)
        dx1 = dx2v + dxn
        return dx1, dx1 * gt, jnp.sum(dx1 * mx, axis=0, keepdims=True), dsc, dsh, dg

    dx1, d_mixed, d_gt1, d_sc2, d_sh2, g_norm2 = rw(
        "norm2_bwd", norm2_bwd, R=512, tiled=[(x1, D, 0), (d_h2, D, 0), (dx2, D, 0), (mixed, D, 0)],
        batch=[(mod, D, GT1), (mod, D, SC2), (mod, D, SH2)], full=[norm2_g],
        out_tiled=[(D, F32), (D, MXU_DTYPE)], out_batch=[D, D, D], out_acc=[(1, D)])

    d_mixed_in = _mm([d_mixed], [full["w_out"]], MXU_DTYPE, "d_mixed_in", bt=True)
    g_w_out = _mm_tn(mixed_in, d_mixed, "g_w_out")

    def mix_bwd(*a):
        ga, gb, ya, za, zb, dm = (v.astype(F32) for v in a)
        _, vjp = jax.vjp(_mix, ga, gb, ya, za, zb)
        dga, dgb, dya, dza, dzb = vjp(dm)
        return jnp.concatenate([dga, dgb], axis=1), dya, jnp.concatenate([dza, dzb], axis=1)

    d_gates, d_ya, d_z = rw("mix_bwd", mix_bwd, R=256, tiled=mix_tiled + [(d_mixed_in, D, 0)],
                            out_tiled=[(2 * D, MXU_DTYPE), (D, MXU_DTYPE), (2 * D, MXU_DTYPE)])
    d_o_rwkv = _mm([d_ya], [full["w_out_rwkv"]], F32, "d_o_rwkv", bt=True)
    g_w_out_rwkv = _mm_tn(o_rwkv, d_ya, "g_out_rwkv")
    d_s5o = _mm([d_z], [full["w_glu"]], F32, "d_s5o", bt=True)
    g_w_glu = _mm_tn(s5o, d_z, "g_glu")
    rsm, rsm_moves = reduce_start("mid", BIG_MID, (g_w_out_rwkv, g_w_glu, g_w_out))
    s5_d = s5_d + rsm["token"]

    d_u, d_wb, d_wc, d_ab, g_s5_d = _s5_bwd(u, y_ssm, d_s5o, s5_d, wb, wc, ab, s5_st, s5_x, Bl, S)

    def diag_in(dw):
        t = dw.reshape(NSG, 8, SGC, 8, SP)
        return jnp.einsum("ab,sacbp->sapc", eye8, t).reshape(NG, SP * SGC)

    def diag_out(dw):
        t = dw.reshape(NSG, 8, SP, 8, SGC)
        return jnp.einsum("ab,sapbc->sacp", eye8, t).reshape(NG, SGC, SP)

    g_s5_c_re = diag_out(d_wc[:, :512])
    g_s5_c_im = -diag_out(d_wc[:, 512:])
    disc_cts = (d_ab[0].reshape(NG, SP), d_ab[1].reshape(NG, SP), diag_in(d_wb[:, :, :512]), diag_in(d_wb[:, :, 512:]))
    g_a_re, g_a_im, g_log_dt, g_b_re, g_b_im = _s5_disc_bwd(*s5_in, disc_cts)

    def post_bwd(yv, rv, kv, vv, gv, do, *pp):
        _, vjp = jax.vjp(lambda *a: _rwkv_post(*a, pp[3]), yv, rv, kv, vv, gv, *pp[:3])
        return vjp(do)

    dy_wkv, dr_b, dk_b, dv_b, dg_, g_ln_g, g_ln_b, g_r_k = rw(
        "rwkv_post_bwd", post_bwd, R=256,
        tiled=[(y_wkv, RW, 0), (r_, RW, 0), (k_, RW, 0), (v_, RW, 0), (g_, RW, 0), (d_o_rwkv, RW, 0)],
        full=post_params, out_tiled=[(RW, F32)] * 5, out_acc=[(1, RW)] * 3)
    dr3, dw3, dk3, dv3, da3, db3 = _wkv_bwd(r_, w_, k_, v_, a_, b_, dy_wkv, ck, tinv, Bl, S)

    shl, shl_moves = share_start("ffn", reduced_halves("ffn", rsl, rsl_moves, dr3))
    shm, shm_moves = share_start("mid", reduced_halves("mid", rsm, rsm_moves, dr3))
    mu_shift = mu_shift + (shl["token"] + shm["token"])

    def prep_bwd(pv, dr1, dr2, dwv, dk1, dk2, dv1, dv2, dav, dbv, dgv, ph, mu, *pp):
        prev = _shift_down(pv, ph, 1)
        ps = pv + (prev - pv) * mu
        _, vjp = jax.vjp(lambda *q: _rwkv_prep(*q, pp[7]), *_split_ps(ps), *pp[:7])
        grads = vjp((dr1 + dr2, dwv, dk1 + dk2, dv1 + dv2, dav, dbv, dgv))
        dps = jnp.concatenate(grads[:5], axis=1)
        return (dps,) + tuple(grads[5:]) + (jnp.sum(dps * (prev - pv), axis=0, keepdims=True),)

    prep_outs = rw(
        "rwkv_prep_bwd", prep_bwd, R=256,
        tiled=[(p, SHIFT, 0), (dr3, RW, 0), (dr_b, RW, 0), (dw3, RW, 0), (dk3, RW, 0), (dk_b, RW, 0),
               (dv3, RW, 0), (dv_b, RW, 0), (da3, RW, 0), (db3, RW, 0), (dg_, RW, 0)],
        prev=[(p, SHIFT, 0)], full=[mu_shift] + prep_params,
        out_tiled=[(SHIFT, F32)],
        out_acc=[(1, RW), (LW + LA, RW), (1, RW), (LW + LA, RW), (LG, RW), (1, RW), (1, RW), (1, SHIFT)])
    d_ps, g_w0, g_w_up_p, g_a0, g_a_up_p, g_g_up, g_k_k, g_k_a, g_mu = prep_outs

    small = {"mu_shift": g_mu, "rwkv_w0": g_w0, "rwkv_a0": g_a0, "rwkv_k_k": g_k_k,
             "rwkv_k_a": g_k_a, "rwkv_r_k": g_r_k, "rwkv_ln_g": g_ln_g, "rwkv_ln_b": g_ln_b, "s5_a_re": g_a_re,
             "s5_a_im": g_a_im, "s5_log_dt": g_log_dt, "s5_b_re": g_b_re, "s5_b_im": g_b_im, "s5_c_re": g_s5_c_re,
             "s5_c_im": g_s5_c_im, "s5_d": g_s5_d, "norm2_g": g_norm2,
             "ffn_conv_b": jnp.concatenate([g_cb_g, g_cb_u], axis=1), "norm_f_g": g_norm_f}
    small_names = list(small)
    g_conv_w = jnp.concatenate([g_cw_g, g_cw_u], axis=1)
    shard_small = {"rwkv_w_up": g_w_up_p[:LW], "rwkv_a_up": g_a_up_p[LW:], "rwkv_g_up": g_g_up, "ffn_conv_w": g_conv_w}
    parts = [small[n] for n in small_names] + [_to_shards(shard_small[n], ax) for n, _, ax in BIG_SMALL]
    spack = _pack_rows(parts, F32, SUBLANES)
    sm_moves = [(0, 0, lambda ref, me, peer: ref, lambda ref, me, k: ref.at[2 * _chip_of(me) + me[2]])]
    sm = _send_start("gsmall_start", ALL_FLIPS, [spack], [sds((8,) + spack.shape, F32)], sm_moves)
    mu_shift = mu_shift + sm["token"]

    def shift_bwd(dps, nx, mu):
        return dps * (1.0 - mu) + _shift_up(dps * mu, nx * mu, 1)

    (d_p,) = rw("shift_bwd", shift_bwd, R=256, tiled=[(d_ps, SHIFT, 0)], nxt=[(d_ps, SHIFT, 0)], full=[mu_shift],
                out_tiled=[(SHIFT, MXU_DTYPE)])
    g_w_in = _mm_tn_cat(h1, [d_p, d_u, d_gates], "g_w_in")
    rsn, rsn_moves = reduce_start("in", BIG[:1], (g_w_in,))
    norm1_g = norm1_g + rsn["token"]
    d_h1 = _mm([d_p, d_u, d_gates], [w_p, w_u, w_g], F32, "d_h1", bt=True)

    def norm1_bwd(xv, dh1, dx1v, sc, sh, g):
        _, vjp = jax.vjp(_norm_mod, xv, g, sc, sh)
        dxn, dg, dsc, dsh = vjp(dh1)
        return dx1v + dxn, dsc, dsh, dg

    grad_x, d_sc1, d_sh1, g_norm1 = rw(
        "norm1_bwd", norm1_bwd, R=512, tiled=[(x2d, D, 0), (d_h1, D, 0), (dx1, D, 0)],
        batch=[(mod, D, SC1), (mod, D, SH1)], full=[norm1_g], out_tiled=[(D, F32)], out_batch=[D, D], out_acc=[(1, D)])

    dmod = jnp.concatenate([d_sh1, d_sc1, d_gt1, d_sh2, d_sc2, d_gt2], axis=2).reshape(Bl, 6 * D)
    last_all = _gather_two_level([], [dmod, g_norm1], "gather_dmod")[1]
    dmod_all = last_all[0].reshape(8 * Bl, 6 * D)
    shn, shn_moves = share_start("in", reduced_halves("in", rsn, rsn_moves, dmod_all))
    dmod_cols = lax.dynamic_slice_in_dim(dmod_all, chip * ncol, ncol, 1)
    g_w_ada, g_b_ada = _ada_bwd(c_all, dmod_cols, dmod_all)

    grads = {"norm1_g": _sum_slots(last_all[1].reshape(8, 1, D), F32, "sum_norm1")}
    sm_own, sm_got = _send_wait("gsmall_wait", ALL_FLIPS, sm, sm_moves, g_b_ada)
    s_all = lax.dynamic_update_slice(sm_got[0], sm_own[0][None], (dev, 0, 0))
    s_sum = _sum_slots(s_all, F32, "sum_gsmall").reshape(-1)
    off = 0
    for n in small_names:
        grads[n] = s_sum[off:off + W[n].size].reshape(W[n].shape)
        off += W[n].size
    for n, shape, axis in BIG_SMALL:
        ss = _shard_shape(shape, axis)
        k4 = 4 * math.prod(ss)
        sh4 = s_sum[off:off + k4].reshape(4, math.prod(ss))
        grads[n] = lax.dynamic_index_in_dim(sh4, chip, 0, keepdims=False).reshape((1,) + ss)
        off += k4

    share_finish("ffn", shl, shl_moves, s_sum, BIG_LATE, grads)
    share_finish("mid", shm, shm_moves, grads[BIG_LATE[0][0]], BIG_MID, grads)
    grads["w_ada"] = g_w_ada[None]
    grads["b_ada"] = g_b_ada

    delta, new_m, new_v = {}, {}, {}
    to2 = lambda z: z.reshape(-1, z.shape[-1])

    def adamw(n):
        d_, m_, v2_ = _adamw(to2(W[n]), to2(grads[n]), to2(M[n]), to2(V[n]), "adamw_" + n)
        delta[n], new_m[n], new_v[n] = (z.reshape(W[n].shape) for z in (d_, m_, v2_))

    for n in ["w_ada"] + [b[0] for b in BIG[1:]]:
        adamw(n)
    rest = [n for n in names if n not in delta and n != "w_in"]
    packs = [_pack_rows([src[n] for n in rest], F32, SUBLANES) for src in (W, grads, M, V)]
    d_, m_, v2_ = _adamw(*packs, "adamw_small")
    shapes = [W[n].shape for n in rest]
    for dst, z in ((delta, d_), (new_m, m_), (new_v, v2_)):
        for n, val in zip(rest, _unpack(z.reshape(-1), shapes)):
            dst[n] = val
    share_finish("in", shn, shn_moves, d_, BIG[:1], grads)
    adamw("w_in")

    return (loss, grad_x.reshape(Bl, S, D), *[grads[n] for n in names], *[delta[n] for n in names],
            *[new_m[n] for n in names], *[new_v[n] for n in names])
```
